```python
import math
import jax, jax.numpy as jnp
from jax import lax
import numpy as np

D_MODEL = 1024
BATCH = 8
SEQ = 4096
DEPTH = 2

HEAD_DIM = 64
N_HEADS_A = D_MODEL // HEAD_DIM
N_HEADS_B = D_MODEL // HEAD_DIM
N_KV_B = 4
GROUP_B = N_HEADS_B // N_KV_B
D_FF = 2816
DILATED_PATTERNS = ((128, 1), (512, 4), (2048, 16))
WINDOW_B = 128
BLOCK = 128
N_A_LAYERS = DEPTH // 2
N_B_LAYERS = DEPTH - N_A_LAYERS
ALPHA = (2.0 * DEPTH) ** 0.25
BETA = (8.0 * DEPTH) ** -0.25
LN_EPS = 1e-5

kernel_name = "yoco_dilated_swa_sink_hybrid"


def alibi_slopes(n):
    return np.array([2.0 ** (-8.0 * (h + 1) / n) for h in range(n)], dtype=np.float32)


def layer_norm(x, g, b):
    xf = x.astype(jnp.float32)
    mu = xf.mean(-1, keepdims=True)
    var = jnp.mean(jnp.square(xf - mu), -1, keepdims=True)
    y = (xf - mu) * lax.rsqrt(var + LN_EPS) * g.astype(jnp.float32) + b.astype(jnp.float32)
    return y.astype(x.dtype)


def swiglu(x, w_in, w_out):
    gate, up = jnp.split(x @ w_in, 2, axis=-1)
    return (jax.nn.silu(gate) * up) @ w_out


def banded_attention(q, k, v, slopes, max_dist, dist_scale, sinks=None):
    b, L, hk, g, dh = q.shape
    P = BLOCK
    nb = -(-L // P)
    pad = nb * P - L
    if pad:
        q = jnp.pad(q, ((0, 0), (0, pad), (0, 0), (0, 0), (0, 0)))
        k = jnp.pad(k, ((0, 0), (0, pad), (0, 0), (0, 0)))
        v = jnp.pad(v, ((0, 0), (0, pad), (0, 0), (0, 0)))
    qb = q.reshape(b, nb, P, hk, g, dh)

    def with_prev(t):
        t = t.reshape(b, nb, P, hk, dh)
        prev = jnp.concatenate([jnp.zeros_like(t[:, :1]), t[:, :-1]], axis=1)
        return jnp.concatenate([prev, t], axis=2)

    kc, vc = with_prev(k), with_prev(v)
    s = jnp.einsum('bnqhgd,bnkhd->bnhgqk', qb.astype(jnp.float32), kc.astype(jnp.float32)) * (dh ** -0.5)
    qi = np.arange(P)[:, None]
    kj = np.arange(2 * P)[None, :]
    dist = P + qi - kj
    kpos = (np.arange(nb)[:, None, None] - 1) * P + kj[None]
    valid = (dist >= 0) & (dist <= max_dist) & (kpos >= 0)
    bias = -(slopes.astype(jnp.float32)[:, :, None, None]
             * jnp.asarray((dist * dist_scale).astype(np.float32)))
    s = jnp.where(jnp.asarray(valid)[None, :, None, None], s + bias, -jnp.inf)
    m = s.max(-1, keepdims=True)
    if sinks is not None:
        sink = sinks.astype(jnp.float32)[:, :, None, None]
        m = jnp.maximum(m, sink)
    p = jnp.exp(s - m)
    den = p.sum(-1, keepdims=True)
    if sinks is not None:
        den = den + jnp.exp(sink - m)
    o = jnp.einsum('bnhgqk,bnkhd->bnqhgd', p / den, vc.astype(jnp.float32))
    lse = jnp.moveaxis((m + jnp.log(den))[..., 0], -1, 2)
    o = o.reshape(b, nb * P, hk, g, dh)[:, :L].astype(q.dtype)
    lse = lse.reshape(b, nb * P, hk, g)[:, :L]
    return o, lse


def to_strided(t, d):
    b, S = t.shape[:2]
    t = t.reshape(b, S // d, d, *t.shape[2:])
    t = jnp.moveaxis(t, 2, 1)
    return t.reshape(b * d, S // d, *t.shape[3:])


def from_strided(t, b, d):
    t = t.reshape(b, d, *t.shape[1:])
    t = jnp.moveaxis(t, 1, 2)
    return t.reshape(b, t.shape[1] * d, *t.shape[3:])


def dilated_mixer(h, w_qkv, w_o):
    b, S, _ = h.shape
    q, k, v = jnp.split(h @ w_qkv, 3, axis=-1)
    q = q.reshape(b, S, N_HEADS_A, 1, HEAD_DIM)
    k = k.reshape(b, S, N_HEADS_A, HEAD_DIM)
    v = v.reshape(b, S, N_HEADS_A, HEAD_DIM)
    slopes = jnp.asarray(alibi_slopes(N_HEADS_A)).reshape(N_HEADS_A, 1)
    outs, lses = [], []
    for window, d in DILATED_PATTERNS:
        o, lse = banded_attention(to_strided(q, d), to_strided(k, d), to_strided(v, d),
                                  slopes, window // d, d)
        outs.append(from_strided(o, b, d))
        lses.append(from_strided(lse, b, d))
    wts = jax.nn.softmax(jnp.stack(lses, 0), axis=0)
    out = jnp.sum(wts[..., None] * jnp.stack(outs, 0).astype(jnp.float32), axis=0)
    return out.astype(h.dtype).reshape(b, S, D_MODEL) @ w_o


def swa_sink_mixer(h, k_sh, v_sh, w_q, sinks, w_o):
    b, S, _ = h.shape
    q = (h @ w_q).reshape(b, S, N_KV_B, GROUP_B, HEAD_DIM)
    slopes = jnp.asarray(alibi_slopes(N_HEADS_B)).reshape(N_KV_B, GROUP_B)
    o, _ = banded_attention(q, k_sh, v_sh, slopes, WINDOW_B - 1, 1,
                            sinks.reshape(N_KV_B, GROUP_B))
    return o.reshape(b, S, D_MODEL) @ w_o


def _fwd_setup_inputs(seed: int = 0) -> dict:
    key = jax.random.key(seed)
    ks = jax.random.split(key, 16)
    f32 = jnp.float32
    nrm = lambda k, shape, fan_in, scale=1.0: jax.random.normal(k, shape, f32) * (fan_in ** -0.5) * scale
    return {
        "x": jax.random.normal(ks[0], (BATCH, SEQ, D_MODEL), f32),
        "ffn1_w_in": nrm(ks[1], (DEPTH, D_MODEL, 2 * D_FF), D_MODEL),
        "ffn1_w_out": nrm(ks[2], (DEPTH, D_FF, D_MODEL), D_FF, BETA),
        "ffn2_w_in": nrm(ks[3], (DEPTH, D_MODEL, 2 * D_FF), D_MODEL),
        "ffn2_w_out": nrm(ks[4], (DEPTH, D_FF, D_MODEL), D_FF, BETA),
        "ln_g": 1.0 + 0.02 * jax.random.normal(ks[5], (DEPTH, 3, D_MODEL), f32),
        "ln_b": 0.02 * jax.random.normal(ks[6], (DEPTH, 3, D_MODEL), f32),
        "a_w_qkv": nrm(ks[7], (N_A_LAYERS, D_MODEL, 3 * N_HEADS_A * HEAD_DIM), D_MODEL),
        "a_w_o": nrm(ks[8], (N_A_LAYERS, N_HEADS_A * HEAD_DIM, D_MODEL), D_MODEL, BETA),
        "kv_w": nrm(ks[9], (D_MODEL, 2 * N_KV_B * HEAD_DIM), D_MODEL),
        "b_w_q": nrm(ks[10], (N_B_LAYERS, D_MODEL, N_HEADS_B * HEAD_DIM), D_MODEL),
        "b_sinks": 0.5 * jax.random.normal(ks[11], (N_B_LAYERS, N_HEADS_B), f32),
        "b_w_o": nrm(ks[12], (N_B_LAYERS, N_HEADS_B * HEAD_DIM, D_MODEL), D_MODEL, BETA),
    }


def _fwd_reference(x, ffn1_w_in, ffn1_w_out, ffn2_w_in, ffn2_w_out, ln_g, ln_b,
              a_w_qkv, a_w_o, kv_w, b_w_q, b_sinks, b_w_o):
    b, S, _ = x.shape
    k_sh = v_sh = None
    for i in range(DEPTH):
        x = layer_norm(ALPHA * x + 0.5 * swiglu(x, ffn1_w_in[i], ffn1_w_out[i]), ln_g[i, 0], ln_b[i, 0])
        if i < N_A_LAYERS:
            mix = dilated_mixer(x, a_w_qkv[i], a_w_o[i])
        else:
            j = i - N_A_LAYERS
            mix = swa_sink_mixer(x, k_sh, v_sh, b_w_q[j], b_sinks[j], b_w_o[j])
        x = layer_norm(ALPHA * x + mix, ln_g[i, 1], ln_b[i, 1])
        x = layer_norm(ALPHA * x + 0.5 * swiglu(x, ffn2_w_in[i], ffn2_w_out[i]), ln_g[i, 2], ln_b[i, 2])
        if i == N_A_LAYERS - 1:
            k_flat, v_flat = jnp.split(x @ kv_w, 2, axis=-1)
            k_sh = k_flat.reshape(b, S, N_KV_B, HEAD_DIM)
            v_sh = v_flat.reshape(b, S, N_KV_B, HEAD_DIM)
    return x


import jax as _jax
import jax.numpy as _jnp

TWIN_FORMAT = 'train_step'
FWD_PARAMS = ['x', 'ffn1_w_in', 'ffn1_w_out', 'ffn2_w_in', 'ffn2_w_out', 'ln_g', 'ln_b', 'a_w_qkv', 'a_w_o', 'kv_w', 'b_w_q', 'b_sinks', 'b_w_o']
TWIN_WEIGHTS = ['ffn1_w_in', 'ffn1_w_out', 'ffn2_w_in', 'ffn2_w_out', 'ln_g', 'ln_b', 'a_w_qkv', 'a_w_o', 'kv_w', 'b_w_q', 'b_sinks', 'b_w_o']
TWIN_DIFF_INPUT = 'x'
TWIN_INPUTS = ['x', 'ffn1_w_in', 'ffn1_w_out', 'ffn2_w_in', 'ffn2_w_out', 'ln_g', 'ln_b', 'a_w_qkv', 'a_w_o', 'kv_w', 'b_w_q', 'b_sinks', 'b_w_o', 'loss_target', 'm_ffn1_w_in', 'm_ffn1_w_out', 'm_ffn2_w_in', 'm_ffn2_w_out', 'm_ln_g', 'm_ln_b', 'm_a_w_qkv', 'm_a_w_o', 'm_kv_w', 'm_b_w_q', 'm_b_sinks', 'm_b_w_o', 'v_ffn1_w_in', 'v_ffn1_w_out', 'v_ffn2_w_in', 'v_ffn2_w_out', 'v_ln_g', 'v_ln_b', 'v_a_w_qkv', 'v_a_w_o', 'v_kv_w', 'v_b_w_q', 'v_b_sinks', 'v_b_w_o']
TWIN_OUTPUTS = ['loss', 'grad_x', 'grad_ffn1_w_in', 'grad_ffn1_w_out', 'grad_ffn2_w_in', 'grad_ffn2_w_out', 'grad_ln_g', 'grad_ln_b', 'grad_a_w_qkv', 'grad_a_w_o', 'grad_kv_w', 'grad_b_w_q', 'grad_b_sinks', 'grad_b_w_o', 'delta_ffn1_w_in', 'delta_ffn1_w_out', 'delta_ffn2_w_in', 'delta_ffn2_w_out', 'delta_ln_g', 'delta_ln_b', 'delta_a_w_qkv', 'delta_a_w_o', 'delta_kv_w', 'delta_b_w_q', 'delta_b_sinks', 'delta_b_w_o', 'new_m_ffn1_w_in', 'new_m_ffn1_w_out', 'new_m_ffn2_w_in', 'new_m_ffn2_w_out', 'new_m_ln_g', 'new_m_ln_b', 'new_m_a_w_qkv', 'new_m_a_w_o', 'new_m_kv_w', 'new_m_b_w_q', 'new_m_b_sinks', 'new_m_b_w_o', 'new_v_ffn1_w_in', 'new_v_ffn1_w_out', 'new_v_ffn2_w_in', 'new_v_ffn2_w_out', 'new_v_ln_g', 'new_v_ln_b', 'new_v_a_w_qkv', 'new_v_a_w_o', 'new_v_kv_w', 'new_v_b_w_q', 'new_v_b_sinks', 'new_v_b_w_o']
TWIN_LEAF_KINDS = {'loss': 'loss', 'grad_x': 'grad_x', 'grad_ffn1_w_in': 'grad_w', 'grad_ffn1_w_out': 'grad_w', 'grad_ffn2_w_in': 'grad_w', 'grad_ffn2_w_out': 'grad_w', 'grad_ln_g': 'grad_w', 'grad_ln_b': 'grad_w', 'grad_a_w_qkv': 'grad_w', 'grad_a_w_o': 'grad_w', 'grad_kv_w': 'grad_w', 'grad_b_w_q': 'grad_w', 'grad_b_sinks': 'grad_w', 'grad_b_w_o': 'grad_w', 'delta_ffn1_w_in': 'delta_w', 'delta_ffn1_w_out': 'delta_w', 'delta_ffn2_w_in': 'delta_w', 'delta_ffn2_w_out': 'delta_w', 'delta_ln_g': 'delta_w', 'delta_ln_b': 'delta_w', 'delta_a_w_qkv': 'delta_w', 'delta_a_w_o': 'delta_w', 'delta_kv_w': 'delta_w', 'delta_b_w_q': 'delta_w', 'delta_b_sinks': 'delta_w', 'delta_b_w_o': 'delta_w', 'new_m_ffn1_w_in': 'new_m', 'new_m_ffn1_w_out': 'new_m', 'new_m_ffn2_w_in': 'new_m', 'new_m_ffn2_w_out': 'new_m', 'new_m_ln_g': 'new_m', 'new_m_ln_b': 'new_m', 'new_m_a_w_qkv': 'new_m', 'new_m_a_w_o': 'new_m', 'new_m_kv_w': 'new_m', 'new_m_b_w_q': 'new_m', 'new_m_b_sinks': 'new_m', 'new_m_b_w_o': 'new_m', 'new_v_ffn1_w_in': 'new_v', 'new_v_ffn1_w_out': 'new_v', 'new_v_ffn2_w_in': 'new_v', 'new_v_ffn2_w_out': 'new_v', 'new_v_ln_g': 'new_v', 'new_v_ln_b': 'new_v', 'new_v_a_w_qkv': 'new_v', 'new_v_a_w_o': 'new_v', 'new_v_kv_w': 'new_v', 'new_v_b_w_q': 'new_v', 'new_v_b_sinks': 'new_v', 'new_v_b_w_o': 'new_v'}


def _forward(args):
    return _fwd_reference(*[args[k] for k in FWD_PARAMS])


def _output_shape():
    out = _jax.eval_shape(lambda: _forward(_fwd_setup_inputs(0)))
    return out.shape, out.dtype

N_MICROBATCH = 1
ADAM_LR = 0.001
ADAM_B1 = 0.9
ADAM_B2 = 0.999
ADAM_EPS = 1e-08
ADAM_WD = 0.01
ADAM_STEP = 10
PER_EXAMPLE_BATCH_AXIS = {'x': 0, 'loss_target': 0}
SHARED_INPUTS = []
_WEIGHT_DTYPES = {'ffn1_w_in': _jnp.float32, 'ffn1_w_out': _jnp.float32, 'ffn2_w_in': _jnp.float32, 'ffn2_w_out': _jnp.float32, 'ln_g': _jnp.float32, 'ln_b': _jnp.float32, 'a_w_qkv': _jnp.float32, 'a_w_o': _jnp.float32, 'kv_w': _jnp.float32, 'b_w_q': _jnp.float32, 'b_sinks': _jnp.float32, 'b_w_o': _jnp.float32}
MOMENT_SCALE = {'ffn1_w_in': 1.158275e-02, 'ffn1_w_out': 3.777515e-02, 'ffn2_w_in': 1.149211e-02, 'ffn2_w_out': 3.750519e-02, 'ln_g': 1.311169e+01, 'ln_b': 6.458285e-01, 'a_w_qkv': 2.028330e-02, 'a_w_o': 5.274634e-02, 'kv_w': 3.485397e-02, 'b_w_q': 1.469637e-02, 'b_sinks': 2.538309e-02, 'b_w_o': 3.955001e-02}


def _to_microbatches(a, axis):
    t = _jnp.moveaxis(a, axis, 0)
    t = t.reshape((N_MICROBATCH, t.shape[0] // N_MICROBATCH) + t.shape[1:])
    return _jnp.moveaxis(t, 1, axis + 1)


def setup_inputs(seed: int = 0) -> dict:
    inp = _fwd_setup_inputs(seed)
    key = _jax.random.fold_in(_jax.random.key(seed), 7919)
    shape, _ = _output_shape()
    out = dict(inp)
    out["loss_target"] = _jax.random.normal(_jax.random.fold_in(key, 0), shape, _jnp.float32)
    for i, name in enumerate(TWIN_WEIGHTS):
        w = inp[name].astype(_jnp.float32)
        if MOMENT_SCALE is None:
            s = _jnp.sqrt(_jnp.mean(_jnp.square(w)) + 1e-30)
        else:
            s = MOMENT_SCALE[name]
        km, kv = _jax.random.split(_jax.random.fold_in(key, i + 1))
        out[name] = w
        out["m_" + name] = s * _jax.random.normal(km, w.shape, _jnp.float32)
        out["v_" + name] = (s * s) * _jax.random.uniform(kv, w.shape, _jnp.float32, 0.5, 1.5)
    if N_MICROBATCH > 1:
        for name, axis in PER_EXAMPLE_BATCH_AXIS.items():
            out[name] = _to_microbatches(out[name], axis)
    return {'x': out['x'], 'ffn1_w_in': out['ffn1_w_in'], 'ffn1_w_out': out['ffn1_w_out'], 'ffn2_w_in': out['ffn2_w_in'], 'ffn2_w_out': out['ffn2_w_out'], 'ln_g': out['ln_g'], 'ln_b': out['ln_b'], 'a_w_qkv': out['a_w_qkv'], 'a_w_o': out['a_w_o'], 'kv_w': out['kv_w'], 'b_w_q': out['b_w_q'], 'b_sinks': out['b_sinks'], 'b_w_o': out['b_w_o'], 'loss_target': out['loss_target'], 'm_ffn1_w_in': out['m_ffn1_w_in'], 'm_ffn1_w_out': out['m_ffn1_w_out'], 'm_ffn2_w_in': out['m_ffn2_w_in'], 'm_ffn2_w_out': out['m_ffn2_w_out'], 'm_ln_g': out['m_ln_g'], 'm_ln_b': out['m_ln_b'], 'm_a_w_qkv': out['m_a_w_qkv'], 'm_a_w_o': out['m_a_w_o'], 'm_kv_w': out['m_kv_w'], 'm_b_w_q': out['m_b_w_q'], 'm_b_sinks': out['m_b_sinks'], 'm_b_w_o': out['m_b_w_o'], 'v_ffn1_w_in': out['v_ffn1_w_in'], 'v_ffn1_w_out': out['v_ffn1_w_out'], 'v_ffn2_w_in': out['v_ffn2_w_in'], 'v_ffn2_w_out': out['v_ffn2_w_out'], 'v_ln_g': out['v_ln_g'], 'v_ln_b': out['v_ln_b'], 'v_a_w_qkv': out['v_a_w_qkv'], 'v_a_w_o': out['v_a_w_o'], 'v_kv_w': out['v_kv_w'], 'v_b_w_q': out['v_b_w_q'], 'v_b_sinks': out['v_b_sinks'], 'v_b_w_o': out['v_b_w_o']}


def _loss(weights, diff, rest, loss_target):
    with _jax.named_scope("forward"):
        args = {**rest, TWIN_DIFF_INPUT: diff, **{k: w.astype(_WEIGHT_DTYPES[k]) for k, w in weights.items()}}
        y = _forward(args)
    with _jax.named_scope("loss_head"):
        err = _jnp.square(y.astype(_jnp.float32) - loss_target)
        return 0.5 * _jnp.sum(_jnp.mean(err, axis=-1)) if err.ndim else 0.5 * err


def _adamw(w, g, m, v):
    m = ADAM_B1 * m + (1.0 - ADAM_B1) * g
    v = ADAM_B2 * v + (1.0 - ADAM_B2) * _jnp.square(g)
    m_hat = m / (1.0 - ADAM_B1 ** ADAM_STEP)
    v_hat = v / (1.0 - ADAM_B2 ** ADAM_STEP)
    delta = -ADAM_LR * (m_hat / (_jnp.sqrt(v_hat) + ADAM_EPS) + ADAM_WD * w)
    return delta, m, v


def reference(x, ffn1_w_in, ffn1_w_out, ffn2_w_in, ffn2_w_out, ln_g, ln_b, a_w_qkv, a_w_o, kv_w, b_w_q, b_sinks, b_w_o, loss_target, m_ffn1_w_in, m_ffn1_w_out, m_ffn2_w_in, m_ffn2_w_out, m_ln_g, m_ln_b, m_a_w_qkv, m_a_w_o, m_kv_w, m_b_w_q, m_b_sinks, m_b_w_o, v_ffn1_w_in, v_ffn1_w_out, v_ffn2_w_in, v_ffn2_w_out, v_ln_g, v_ln_b, v_a_w_qkv, v_a_w_o, v_kv_w, v_b_w_q, v_b_sinks, v_b_w_o):
    given = dict(x=x, ffn1_w_in=ffn1_w_in, ffn1_w_out=ffn1_w_out, ffn2_w_in=ffn2_w_in, ffn2_w_out=ffn2_w_out, ln_g=ln_g, ln_b=ln_b, a_w_qkv=a_w_qkv, a_w_o=a_w_o, kv_w=kv_w, b_w_q=b_w_q, b_sinks=b_sinks, b_w_o=b_w_o, loss_target=loss_target, m_ffn1_w_in=m_ffn1_w_in, m_ffn1_w_out=m_ffn1_w_out, m_ffn2_w_in=m_ffn2_w_in, m_ffn2_w_out=m_ffn2_w_out, m_ln_g=m_ln_g, m_ln_b=m_ln_b, m_a_w_qkv=m_a_w_qkv, m_a_w_o=m_a_w_o, m_kv_w=m_kv_w, m_b_w_q=m_b_w_q, m_b_sinks=m_b_sinks, m_b_w_o=m_b_w_o, v_ffn1_w_in=v_ffn1_w_in, v_ffn1_w_out=v_ffn1_w_out, v_ffn2_w_in=v_ffn2_w_in, v_ffn2_w_out=v_ffn2_w_out, v_ln_g=v_ln_g, v_ln_b=v_ln_b, v_a_w_qkv=v_a_w_qkv, v_a_w_o=v_a_w_o, v_kv_w=v_kv_w, v_b_w_q=v_b_w_q, v_b_sinks=v_b_sinks, v_b_w_o=v_b_w_o)
    weights = {n: given[n] for n in TWIN_WEIGHTS}
    shared = {n: given[n] for n in SHARED_INPUTS}
    per_example = {n: given[n] for n in ['x']}
    grad_fn = _jax.value_and_grad(_loss, argnums=(0, 1))

    def one_microbatch(ex, loss_target):
        ex = dict(ex)
        diff = ex.pop(TWIN_DIFF_INPUT)
        return grad_fn(weights, diff, {**shared, **ex}, loss_target)

    if N_MICROBATCH == 1:
        loss, (grad_w, grad_x) = one_microbatch(per_example, given["loss_target"])
    else:
        def body(carry, xs):
            loss_sum, grad_sum = carry
            l_k, (gw_k, gx_k) = one_microbatch(xs[0], xs[1])
            with _jax.named_scope("update"):
                return (loss_sum + l_k, _jax.tree.map(_jnp.add, grad_sum, gw_k)), gx_k

        init = (_jnp.zeros((), _jnp.float32), _jax.tree.map(_jnp.zeros_like, weights))
        (loss, grad_w), grad_x = _jax.lax.scan(body, init, (per_example, given["loss_target"]))
    with _jax.named_scope("update"):
        delta_w, new_m, new_v = {}, {}, {}
        for n in TWIN_WEIGHTS:
            delta_w[n], new_m[n], new_v[n] = _adamw(weights[n], grad_w[n], given["m_" + n], given["v_" + n])
    return (loss, grad_x, *[grad_w[n] for n in TWIN_WEIGHTS], *[delta_w[n] for n in TWIN_WEIGHTS],
            *[new_m[n] for n in TWIN_WEIGHTS], *[new_v[n] for n in TWIN_WEIGHTS])
```

```python
import functools

import numpy as np
import jax
import jax.numpy as jnp
from jax import lax
from jax.experimental import pallas as pl
from jax.experimental.pallas import tpu as pltpu

F32 = jnp.float32
BF16 = jnp.bfloat16

D_MODEL = 1024
D_FF = 2816
HALF_FF = D_FF // 2
HEAD_DIM = 64
N_HEADS = 16
N_KV_B = 4
GROUP_B = N_HEADS // N_KV_B
DEPTH = 2
ALPHA = (2.0 * DEPTH) ** 0.25
LN_EPS = 1e-5
BLOCK = 128
SLAB = 128
N_SLABS = D_MODEL // SLAB
PATTERNS_A = ((1, 128, 1.0), (4, 128, 4.0), (16, 128, 16.0))
PATTERNS_B = ((1, 127, 1.0),)
NEG = -1e30

ADAM_LR = 0.001
ADAM_B1 = 0.9
ADAM_B2 = 0.999
ADAM_EPS = 1e-08
ADAM_WD = 0.01
ADAM_STEP = 10

N_CHIPS = 4
VMEM_LIMIT = 56 * 1024 * 1024
MESH = pl.DeviceIdType.MESH

SLOT_OF_CHIP = (0, 2, 1, 3)
CHIP_OF_SLOT = (0, 2, 1, 3)


def _alibi_slopes(n):
    return np.array([2.0 ** (-8.0 * (h + 1) / n) for h in range(n)], dtype=np.float32)


def _cparams(sem=None, vmem=VMEM_LIMIT):
    return pltpu.CompilerParams(dimension_semantics=sem, vmem_limit_bytes=vmem)


_DIMS = {"nn": ((1,), (0,)), "nt": ((1,), (1,)), "tn": ((0,), (0,))}


def _matmul(a, b, mode, out_dtype, tm, tn, tk, name, add=None, add_scale=1.0, split=False):
    out_spec = pl.BlockSpec((tm, tn), lambda i, j, k: (i, j))
    if mode == "nn":
        (M, K), (K2, N) = a.shape, b.shape
        a_spec = pl.BlockSpec((tm, tk), lambda i, j, k: (i, k))
        b_spec = pl.BlockSpec((tk, tn), lambda i, j, k: (k, j))
        out_struct = jax.ShapeDtypeStruct((M, N), out_dtype)
        if split:
            assert tn == D_MODEL
            out_spec = pl.BlockSpec((None, tm, tn), lambda i, j, k: (j, i, 0))
            out_struct = jax.ShapeDtypeStruct((N // tn, M, tn), out_dtype)
    elif mode == "nt":
        N, K2 = b.shape
        if split:
            assert tk == D_MODEL
            M, K = a.shape[1], a.shape[0] * a.shape[2]
            a_spec = pl.BlockSpec((None, tm, tk), lambda i, j, k: (k, i, 0))
        else:
            M, K = a.shape
            a_spec = pl.BlockSpec((tm, tk), lambda i, j, k: (i, k))
        b_spec = pl.BlockSpec((tn, tk), lambda i, j, k: (j, k))
        out_struct = jax.ShapeDtypeStruct((M, N), out_dtype)
    else:
        K, M = a.shape
        if split:
            assert tn == D_MODEL
            K2, N = b.shape[1], b.shape[0] * b.shape[2]
            b_spec = pl.BlockSpec((None, tk, tn), lambda i, j, k: (j, k, 0))
        else:
            K2, N = b.shape
            b_spec = pl.BlockSpec((tk, tn), lambda i, j, k: (k, j))
        a_spec = pl.BlockSpec((tk, tm), lambda i, j, k: (k, i))
        out_struct = jax.ShapeDtypeStruct((M, N), out_dtype)
    assert K == K2 and M % tm == 0 and N % tn == 0 and K % tk == 0, (a.shape, b.shape, mode, tm, tn, tk)
    nk = K // tk
    dims = (_DIMS[mode], ((), ()))
    has_add = add is not None

    def body(*refs):
        if has_add:
            a_ref, b_ref, add_ref, o_ref, acc_ref = refs
        else:
            a_ref, b_ref, o_ref, acc_ref = refs
        k = pl.program_id(2)
        part = lax.dot_general(a_ref[...].astype(BF16), b_ref[...].astype(BF16), dims, preferred_element_type=F32)

        @pl.when(k == 0)
        def _():
            acc_ref[...] = part

        @pl.when(k > 0)
        def _():
            acc_ref[...] += part

        @pl.when(k == nk - 1)
        def _():
            r = acc_ref[...]
            if has_add:
                r = r + add_scale * add_ref[...]
            o_ref[...] = r.astype(out_dtype)

    in_specs = [a_spec, b_spec]
    args = [a, b]
    if has_add:
        in_specs.append(pl.BlockSpec((tm, tn), lambda i, j, k: (i, j)))
        args.append(add)
    return pl.pallas_call(
        body, name=name, grid=(M // tm, N // tn, nk),
        in_specs=in_specs, out_specs=out_spec, out_shape=out_struct,
        scratch_shapes=[pltpu.VMEM((tm, tn), F32)],
        compiler_params=_cparams(("parallel", "parallel", "arbitrary")),
    )(*args)


def _pick(n, cands):
    for c in cands:
        if n % c == 0:
            return c
    raise ValueError((n, cands))


def _mm_nn(a, b, out_dtype, name, split=False):
    M, K = a.shape
    N = b.shape[1]
    return _matmul(a, b, "nn", out_dtype, _pick(M, (1024, 512, 256)), _pick(N, (1024, 512)), _pick(K, (1024, 512)), name,
                   split=split)


def _mm_nt(a, b, name, add=None, add_scale=1.0, split=False):
    M, K = (a.shape[1], D_MODEL) if split else a.shape
    N = b.shape[0]
    return _matmul(a, b, "nt", F32, _pick(M, (1024, 512, 256)), _pick(N, (1024, 512)),
                   _pick(K, (1408, 1024, 512)), name, add=add, add_scale=add_scale, split=split)


def _mm_tn(a, b, name, split=False):
    K, M = a.shape
    N = D_MODEL if split else b.shape[1]
    return _matmul(a, b, "tn", F32, _pick(M, (1024, 1408, 512)), _pick(N, (1408, 1024, 512)),
                   _pick(K, (1024, 512, 256)), name, split=split)


def _ffn_in(x, w, name):
    S = x.shape[0]
    tm = _pick(S, (512, 256))

    def body(x_ref, w_ref, u_ref, h_ref):
        acc = jnp.dot(x_ref[...].astype(BF16), w_ref[...], preferred_element_type=F32)
        g = acc[:, :HALF_FF]
        up = acc[:, HALF_FF:]
        u_ref[...] = acc.astype(BF16)
        h_ref[...] = (g * jax.nn.sigmoid(g) * up).astype(BF16)

    return pl.pallas_call(
        body, name=name, grid=(2, S // tm),
        in_specs=[pl.BlockSpec((tm, D_MODEL), lambda j, i: (i, 0)),
                  pl.BlockSpec((D_MODEL, D_FF), lambda j, i: (0, j))],
        out_specs=[pl.BlockSpec((tm, D_FF), lambda j, i: (i, j)),
                   pl.BlockSpec((tm, HALF_FF), lambda j, i: (i, j))],
        out_shape=[jax.ShapeDtypeStruct((S, 2 * D_FF), BF16), jax.ShapeDtypeStruct((S, D_FF), BF16)],
        compiler_params=_cparams(("parallel", "parallel")),
    )(x, w)


def _ffn_bwd_h(dzc, w_out, u, name):
    S = dzc.shape[0]
    tm = _pick(S, (512, 256))

    def body(dz_ref, w_ref, u_ref, du_ref):
        dh = lax.dot_general(dz_ref[...], w_ref[...], (((1,), (1,)), ((), ())), preferred_element_type=F32)
        g = u_ref[:, :HALF_FF].astype(F32)
        up = u_ref[:, HALF_FF:].astype(F32)
        sg = jax.nn.sigmoid(g)
        du_ref[:, :HALF_FF] = (dh * up * (sg * (1.0 + g * (1.0 - sg)))).astype(BF16)
        du_ref[:, HALF_FF:] = (dh * (g * sg)).astype(BF16)

    return pl.pallas_call(
        body, name=name, grid=(2, S // tm),
        in_specs=[pl.BlockSpec((tm, D_MODEL), lambda j, i: (i, 0)),
                  pl.BlockSpec((HALF_FF, D_MODEL), lambda j, i: (j, 0)),
                  pl.BlockSpec((tm, D_FF), lambda j, i: (i, j))],
        out_specs=pl.BlockSpec((tm, D_FF), lambda j, i: (i, j)),
        out_shape=jax.ShapeDtypeStruct((S, 2 * D_FF), BF16),
        compiler_params=_cparams(("parallel", "parallel")),
    )(dzc, w_out, u)


def _mm_ln(a, w, resid, gain, bias, c, name):
    S, K = a.shape
    tm = _pick(S, (512, 256))
    tk = _pick(K, (1408, 1024))
    nk = K // tk

    def body(a_ref, w_ref, r_ref, g_ref, b_ref, y_ref, yb_ref, z_ref, acc_ref):
        k = pl.program_id(1)
        part = jnp.dot(a_ref[...], w_ref[...], preferred_element_type=F32)

        @pl.when(k == 0)
        def _():
            acc_ref[...] = part

        @pl.when(k > 0)
        def _():
            acc_ref[...] += part

        @pl.when(k == nk - 1)
        def _():
            z = ALPHA * r_ref[...] + c * acc_ref[...]
            mu = jnp.mean(z, axis=-1, keepdims=True)
            zc = z - mu
            var = jnp.mean(zc * zc, axis=-1, keepdims=True)
            y = zc * lax.rsqrt(var + LN_EPS) * g_ref[...] + b_ref[...]
            z_ref[...] = z
            y_ref[...] = y
            yb_ref[...] = y.astype(BF16)

    row = pl.BlockSpec((tm, D_MODEL), lambda i, k: (i, 0))
    vec = pl.BlockSpec((1, D_MODEL), lambda i, k: (0, 0))
    return pl.pallas_call(
        body, name=name, grid=(S // tm, nk),
        in_specs=[pl.BlockSpec((tm, tk), lambda i, k: (i, k)), pl.BlockSpec((tk, D_MODEL), lambda i, k: (k, 0)),
                  row, vec, vec],
        out_specs=[row, row, row],
        out_shape=[jax.ShapeDtypeStruct((S, D_MODEL), F32), jax.ShapeDtypeStruct((S, D_MODEL), BF16),
                   jax.ShapeDtypeStruct((S, D_MODEL), F32)],
        scratch_shapes=[pltpu.VMEM((tm, D_MODEL), F32)],
        compiler_params=_cparams(("parallel", "arbitrary")),
    )(a, w, resid, gain, bias)


def _ln_bwd(z, dy, gain, c, name):
    S = z.shape[0]
    tm = _pick(S, (512, 256))

    def body(z_ref, dy_ref, g_ref, dz_ref, dzc_ref, gg_ref, gb_ref):
        i = pl.program_id(0)
        zv = z_ref[...]
        dyv = dy_ref[...]
        mu = jnp.mean(zv, axis=-1, keepdims=True)
        zc = zv - mu
        var = jnp.mean(zc * zc, axis=-1, keepdims=True)
        rstd = lax.rsqrt(var + LN_EPS)
        xhat = zc * rstd
        dyg = dyv * g_ref[...]
        m1 = jnp.mean(dyg, axis=-1, keepdims=True)
        m2 = jnp.mean(dyg * xhat, axis=-1, keepdims=True)
        dz = rstd * (dyg - m1 - xhat * m2)
        dz_ref[...] = dz
        dzc_ref[...] = (c * dz).astype(BF16)
        pg = jnp.sum((dyv * xhat).reshape(tm // 8, 8, D_MODEL), axis=0)
        pb = jnp.sum(dyv.reshape(tm // 8, 8, D_MODEL), axis=0)

        @pl.when(i == 0)
        def _():
            gg_ref[...] = pg
            gb_ref[...] = pb

        @pl.when(i > 0)
        def _():
            gg_ref[...] += pg
            gb_ref[...] += pb

    row = pl.BlockSpec((tm, D_MODEL), lambda i: (i, 0))
    part = pl.BlockSpec((8, D_MODEL), lambda i: (0, 0))
    return pl.pallas_call(
        body, name=name, grid=(S // tm,),
        in_specs=[row, row, pl.BlockSpec((1, D_MODEL), lambda i: (0, 0))],
        out_specs=[row, row, part, part],
        out_shape=[jax.ShapeDtypeStruct((S, D_MODEL), F32), jax.ShapeDtypeStruct((S, D_MODEL), BF16),
                   jax.ShapeDtypeStruct((8, D_MODEL), F32), jax.ShapeDtypeStruct((8, D_MODEL), F32)],
        compiler_params=_cparams(("arbitrary",)),
    )(z, dy, gain)


def _loss_grad(y, t, name):
    S = y.shape[0]
    tm = _pick(S, (512, 256))

    def body(y_ref, t_ref, dy_ref, sq_ref):
        i = pl.program_id(0)
        e = y_ref[...] - t_ref[...]
        dy_ref[...] = e * (1.0 / D_MODEL)
        ps = jnp.sum((e * e).reshape(tm // 8, 8, D_MODEL), axis=0)

        @pl.when(i == 0)
        def _():
            sq_ref[...] = ps

        @pl.when(i > 0)
        def _():
            sq_ref[...] += ps

    row = pl.BlockSpec((tm, D_MODEL), lambda i: (i, 0))
    return pl.pallas_call(
        body, name=name, grid=(S // tm,),
        in_specs=[row, row], out_specs=[row, pl.BlockSpec((8, D_MODEL), lambda i: (0, 0))],
        out_shape=[jax.ShapeDtypeStruct((S, D_MODEL), F32), jax.ShapeDtypeStruct((8, D_MODEL), F32)],
        compiler_params=_cparams(("arbitrary",)),
    )(y, t)


def _rows(start, d):
    if d == 1:
        return pl.ds(pl.multiple_of(start, BLOCK), BLOCK)
    return pl.ds(start, BLOCK, stride=d)


def _ld(ref, start, d):
    return ref[_rows(start, d), :]


def _ld3(ref, lead, start, d):
    return ref[lead, _rows(start, d), :]


def _st3(ref, lead, start, d, val):
    ref[lead, _rows(start, d), :] = val


def _acc3(ref, lead, start, d, val):
    ref[lead, _rows(start, d), :] = ref[lead, _rows(start, d), :] + val


def _band_consts():
    qi = lax.broadcasted_iota(jnp.int32, (BLOCK, 2 * BLOCK), 0)
    kj = lax.broadcasted_iota(jnp.int32, (BLOCK, 2 * BLOCK), 1)
    return BLOCK + qi - kj, kj


def _scores(q, k2, hm, slope, dsc, valid):
    qm = jnp.where(hm, q, 0.0).astype(BF16)
    s = lax.dot_general(qm, k2, (((1,), (1,)), ((), ())), preferred_element_type=F32) * (HEAD_DIM ** -0.5)
    return qm, jnp.where(valid, s - slope * dsc, NEG)


def _softmax_weights(ls):
    mx = ls[0]
    for l in ls[1:]:
        mx = jnp.maximum(mx, l)
    es = [jnp.exp(l - mx) for l in ls]
    tot = es[0]
    for e in es[1:]:
        tot = tot + e
    inv = 1.0 / tot
    return [e * inv for e in es]


def _attn_fwd(qkv, slopes, sinks, patterns, name):
    S = qkv.shape[1]
    npat = len(patterns)
    has_sink = sinks is not None
    if not has_sink:
        sinks = jnp.zeros((N_HEADS,), F32)
    rows_c = 256

    def body(slopes_ref, sinks_ref, x_ref, mix_ref, o_ref, lse_ref):
        p = pl.program_id(0)
        lo = lax.broadcasted_iota(jnp.int32, (BLOCK, SLAB), 1) < HEAD_DIM
        dist, kj = _band_consts()
        distf = dist.astype(F32)
        for pi, (d, maxd, scale) in enumerate(patterns):
            nb = S // d // BLOCK
            band = (dist >= 0) & (dist <= maxd)
            dsc = distf * scale

            def blk(t, carry, pi=pi, d=d, nb=nb, band=band, dsc=dsc):
                r = t // nb
                n = t - r * nb
                start = r + (d * BLOCK) * n
                prev = jnp.where(n > 0, start - d * BLOCK, start)
                valid = band & (kj + jnp.where(n > 0, BLOCK, 0) >= BLOCK)
                q = _ld3(x_ref, 0, start, d)
                k2 = jnp.concatenate([_ld3(x_ref, 1, prev, d), _ld3(x_ref, 1, start, d)], axis=0).astype(BF16)
                v2 = jnp.concatenate([_ld3(x_ref, 2, prev, d), _ld3(x_ref, 2, start, d)], axis=0).astype(BF16)
                outs, lses = [], []
                for h in (0, 1):
                    hm = lo if h == 0 else jnp.logical_not(lo)
                    _, s = _scores(q, k2, hm, slopes_ref[2 * p + h], dsc, valid)
                    m = jnp.max(s, axis=-1, keepdims=True)
                    if has_sink:
                        sk = sinks_ref[2 * p + h]
                        m = jnp.maximum(m, sk)
                    e = jnp.exp(s - m)
                    den = jnp.sum(e, axis=-1, keepdims=True)
                    if has_sink:
                        den = den + jnp.exp(sk - m)
                    outs.append(jnp.dot((e / den).astype(BF16), v2, preferred_element_type=F32))
                    lses.append(m + jnp.log(den))
                _st3(o_ref, pi, start, d, jnp.where(lo, outs[0], outs[1]))
                _st3(lse_ref, pi, start, d, jnp.where(lo, lses[0], lses[1]))
                return carry

            lax.fori_loop(0, d * nb, blk, 0)

        def comb(ci, carry):
            rows = pl.ds(pl.multiple_of(ci * rows_c, rows_c), rows_c)
            if npat == 1:
                mix_ref[rows, :] = o_ref[0, rows, :].astype(BF16)
            else:
                ws = _softmax_weights([lse_ref[i, rows, :] for i in range(npat)])
                acc = ws[0] * o_ref[0, rows, :]
                for i in range(1, npat):
                    acc = acc + ws[i] * o_ref[i, rows, :]
                mix_ref[rows, :] = acc.astype(BF16)
            return carry

        lax.fori_loop(0, S // rows_c, comb, 0)

    smem = pl.BlockSpec(memory_space=pltpu.SMEM)
    slab3 = pl.BlockSpec((npat, S, SLAB), lambda p: (0, 0, p))
    return pl.pallas_call(
        body, name=name, grid=(N_SLABS,),
        in_specs=[smem, smem, pl.BlockSpec((3, S, SLAB), lambda p: (0, 0, p))],
        out_specs=[pl.BlockSpec((S, SLAB), lambda p: (0, p)), slab3, slab3],
        out_shape=[jax.ShapeDtypeStruct((S, D_MODEL), BF16), jax.ShapeDtypeStruct((npat, S, D_MODEL), F32),
                   jax.ShapeDtypeStruct((npat, S, D_MODEL), F32)],
        compiler_params=_cparams(("arbitrary",)),
    )(slopes, sinks, qkv)


def _attn_bwd(qkv, dout, o, lse, slopes, sinks, patterns, name):
    S = qkv.shape[1]
    npat = len(patterns)
    has_sink = sinks is not None
    if not has_sink:
        sinks = jnp.zeros((N_HEADS,), F32)
    rows_c = 256

    def headsum(x, lo):
        s0 = jnp.sum(jnp.where(lo, x, 0.0), axis=-1, keepdims=True)
        s1 = jnp.sum(jnp.where(lo, 0.0, x), axis=-1, keepdims=True)
        return jnp.where(lo, s0, s1)

    def body(slopes_ref, sinks_ref, x_ref, do_ref, o_ref, lse_ref, dx_ref, dsink_ref, dbar_ref, sacc_ref):
        p = pl.program_id(0)
        lo = lax.broadcasted_iota(jnp.int32, (BLOCK, SLAB), 1) < HEAD_DIM
        lo_c = lax.broadcasted_iota(jnp.int32, (rows_c, SLAB), 1) < HEAD_DIM
        dist, kj = _band_consts()
        distf = dist.astype(F32)

        def prep(ci, carry):
            rows = pl.ds(pl.multiple_of(ci * rows_c, rows_c), rows_c)
            dov = do_ref[rows, :]
            dx_ref[:, rows, :] = jnp.zeros((3, rows_c, SLAB), F32)
            if npat == 1:
                dbar_ref[rows, :] = headsum(dov * o_ref[0, rows, :], lo_c)
            else:
                ws = _softmax_weights([lse_ref[i, rows, :] for i in range(npat)])
                acc = ws[0] * headsum(dov * o_ref[0, rows, :], lo_c)
                for i in range(1, npat):
                    acc = acc + ws[i] * headsum(dov * o_ref[i, rows, :], lo_c)
                dbar_ref[rows, :] = acc
            return carry

        lax.fori_loop(0, S // rows_c, prep, 0)
        sacc_ref[...] = jnp.zeros((BLOCK, SLAB), F32)

        for pi, (d, maxd, scale) in enumerate(patterns):
            nb = S // d // BLOCK
            band = (dist >= 0) & (dist <= maxd)
            dsc = distf * scale

            def blk(t, carry, pi=pi, d=d, nb=nb, band=band, dsc=dsc):
                r = t // nb
                n = t - r * nb
                start = r + (d * BLOCK) * n
                prev = jnp.where(n > 0, start - d * BLOCK, start)
                valid = band & (kj + jnp.where(n > 0, BLOCK, 0) >= BLOCK)
                q = _ld3(x_ref, 0, start, d)
                k2 = jnp.concatenate([_ld3(x_ref, 1, prev, d), _ld3(x_ref, 1, start, d)], axis=0).astype(BF16)
                v2 = jnp.concatenate([_ld3(x_ref, 2, prev, d), _ld3(x_ref, 2, start, d)], axis=0).astype(BF16)
                ls = [_ld3(lse_ref, i, start, d) for i in range(npat)]
                w = _softmax_weights(ls)[pi] if npat > 1 else 1.0
                d_o = w * _ld(do_ref, start, d)
                dl = w * _ld(dbar_ref, start, d)
                dk2 = jnp.zeros((2 * BLOCK, SLAB), F32)
                dv2 = jnp.zeros((2 * BLOCK, SLAB), F32)
                dqs = []
                sk_terms = []
                for h in (0, 1):
                    hm = lo if h == 0 else jnp.logical_not(lo)
                    c0 = h * HEAD_DIM
                    qm, s = _scores(q, k2, hm, slopes_ref[2 * p + h], dsc, valid)
                    lse_h = ls[pi][:, c0:c0 + 1]
                    dl_h = dl[:, c0:c0 + 1]
                    pr = jnp.exp(s - lse_h)
                    dom = jnp.where(hm, d_o, 0.0).astype(BF16)
                    dp = lax.dot_general(dom, v2, (((1,), (1,)), ((), ())), preferred_element_type=F32)
                    ds = (pr * (dp - dl_h) * (HEAD_DIM ** -0.5)).astype(BF16)
                    dqs.append(jnp.dot(ds, k2, preferred_element_type=F32))
                    dk2 = dk2 + lax.dot_general(ds, qm, (((0,), (0,)), ((), ())), preferred_element_type=F32)
                    dv2 = dv2 + lax.dot_general(pr.astype(BF16), dom, (((0,), (0,)), ((), ())), preferred_element_type=F32)
                    if has_sink:
                        sk_terms.append(-jnp.exp(sinks_ref[2 * p + h] - lse_h) * dl_h)
                _acc3(dx_ref, 0, start, d, jnp.where(lo, dqs[0], dqs[1]))
                _acc3(dx_ref, 1, prev, d, dk2[:BLOCK])
                _acc3(dx_ref, 1, start, d, dk2[BLOCK:])
                _acc3(dx_ref, 2, prev, d, dv2[:BLOCK])
                _acc3(dx_ref, 2, start, d, dv2[BLOCK:])
                if has_sink:
                    sacc_ref[...] += jnp.where(lo, sk_terms[0], sk_terms[1])
                return carry

            lax.fori_loop(0, d * nb, blk, 0)

        dsink_ref[...] = jnp.broadcast_to(jnp.sum(sacc_ref[...], axis=0, keepdims=True), (8, SLAB))

    smem = pl.BlockSpec(memory_space=pltpu.SMEM)
    one = pl.Buffered(1)
    slab3 = pl.BlockSpec((npat, S, SLAB), lambda p: (0, 0, p), pipeline_mode=one)
    return pl.pallas_call(
        body, name=name, grid=(N_SLABS,),
        in_specs=[smem, smem, pl.BlockSpec((3, S, SLAB), lambda p: (0, 0, p), pipeline_mode=one),
                  pl.BlockSpec((S, SLAB), lambda p: (0, p), pipeline_mode=one), slab3, slab3],
        out_specs=[pl.BlockSpec((3, S, SLAB), lambda p: (0, 0, p)), pl.BlockSpec((None, 8, SLAB), lambda p: (p, 0, 0))],
        out_shape=[jax.ShapeDtypeStruct((3, S, D_MODEL), F32), jax.ShapeDtypeStruct((N_SLABS, 8, SLAB), F32)],
        scratch_shapes=[pltpu.VMEM((S, SLAB), F32), pltpu.VMEM((BLOCK, SLAB), F32)],
        compiler_params=_cparams(("arbitrary",)),
    )(slopes, sinks, qkv, dout, o, lse)


def _place():
    x, y, c = lax.axis_index("x"), lax.axis_index("y"), lax.axis_index("c")
    return x, y, c, 2 * x + y


def _other_chips(x, y):
    return [(1 - x, y), (x, 1 - y), (1 - x, 1 - y)]


HBM_SPEC = pl.BlockSpec(memory_space=pl.ANY)


def _all_gather(blob, small):
    _, R, W = blob.shape
    r = small.shape[0]

    def body(blob_ref, small_ref, g_ref, s_ref, send_sems, recv_sems, local_sems):
        x, y, c, myq = _place()
        sibling = (x, y, 1 - c)
        chips = _other_chips(x, y)

        def big(k, src, dst_q, dst_h, to):
            return pltpu.make_async_remote_copy(src_ref=src, dst_ref=g_ref.at[dst_q, dst_h], send_sem=send_sems.at[k],
                                                recv_sem=recv_sems.at[k], device_id=to, device_id_type=MESH)

        def tiny(k, dst_q, to):
            return pltpu.make_async_remote_copy(src_ref=small_ref, dst_ref=s_ref.at[dst_q], send_sem=send_sems.at[k],
                                                recv_sem=recv_sems.at[k], device_id=to, device_id_type=MESH)

        own = pltpu.make_async_copy(blob_ref, g_ref.at[myq], local_sems.at[0])
        own_small = pltpu.make_async_copy(small_ref, s_ref.at[myq], local_sems.at[1])
        own.start()
        own_small.start()
        first = [big(j, blob_ref.at[c], myq, c, (*chip, c)) for j, chip in enumerate(chips)]
        first += [tiny(6 + j, myq, (*chip, c)) for j, chip in enumerate(chips)]
        for cp in first:
            cp.start()
        passed = []
        for j, (cx, cy) in enumerate(chips):
            q = 2 * cx + cy
            big(j, blob_ref.at[c], q, c, sibling).wait_recv()
            fwd = big(3 + j, g_ref.at[q, c], q, c, sibling)
            fwd.start()
            passed.append(fwd)
        for j, (cx, cy) in enumerate(chips):
            q = 2 * cx + cy
            big(3 + j, blob_ref.at[c], q, 1 - c, sibling).wait_recv()
            tiny(6 + j, q, sibling).wait_recv()
        for cp in first + passed:
            cp.wait_send()
        own.wait()
        own_small.wait()

    return pl.pallas_call(
        body, name="all_gather_weights",
        in_specs=[HBM_SPEC, HBM_SPEC], out_specs=[HBM_SPEC, HBM_SPEC],
        out_shape=[jax.ShapeDtypeStruct((N_CHIPS, 2, R, W), blob.dtype), jax.ShapeDtypeStruct((N_CHIPS, r, 128), F32)],
        scratch_shapes=[pltpu.SemaphoreType.DMA((9,)), pltpu.SemaphoreType.DMA((9,)), pltpu.SemaphoreType.DMA((2,))],
    )(blob, small)


def _small_all_reduce(v):
    r = v.shape[0]

    def body(v_ref, o_ref, buf_ref, send_sems, recv_sems):
        x, y, c, _ = _place()
        me = 4 * x + 2 * y + c
        buf_ref[me] = v_ref[...]
        copies = []
        for k in range(1, 8):
            fx, fy, fc = (k >> 2) & 1, (k >> 1) & 1, k & 1
            to = (x ^ fx, y ^ fy, c ^ fc)
            cp = pltpu.make_async_remote_copy(src_ref=v_ref, dst_ref=buf_ref.at[me], send_sem=send_sems.at[k - 1],
                                              recv_sem=recv_sems.at[k - 1], device_id=to, device_id_type=MESH)
            cp.start()
            copies.append(cp)
        for k in range(1, 8):
            fx, fy, fc = (k >> 2) & 1, (k >> 1) & 1, k & 1
            src_dev = 4 * (x ^ fx) + 2 * (y ^ fy) + (c ^ fc)
            pltpu.make_async_remote_copy(src_ref=v_ref, dst_ref=buf_ref.at[src_dev], send_sem=send_sems.at[k - 1],
                                         recv_sem=recv_sems.at[k - 1], device_id=(x, y, c), device_id_type=MESH).wait_recv()
        for cp in copies:
            cp.wait_send()
        tot = buf_ref[0]
        for i in range(1, 8):
            tot = tot + buf_ref[i]
        o_ref[...] = tot

    vm = pl.BlockSpec(memory_space=pltpu.VMEM)
    return pl.pallas_call(
        body, name="small_all_reduce", in_specs=[vm], out_specs=vm,
        out_shape=jax.ShapeDtypeStruct((r, 128), F32),
        scratch_shapes=[pltpu.VMEM((8, r, 128), F32), pltpu.SemaphoreType.DMA((7,)), pltpu.SemaphoreType.DMA((7,))],
    )(v)


def _sibling_exchange(src, which, name):
    shape = (src.shape[0],) + src.shape[2:] if which == "other" else src.shape

    def body(src_ref, out_ref, send_sem, recv_sem):
        x, y, c, _ = _place()
        part = src_ref.at[:, 1 - c] if which == "other" else src_ref
        cp = pltpu.make_async_remote_copy(src_ref=part, dst_ref=out_ref, send_sem=send_sem, recv_sem=recv_sem,
                                          device_id=(x, y, 1 - c), device_id_type=MESH)
        cp.start()
        cp.wait()

    return pl.pallas_call(
        body, name=name, in_specs=[HBM_SPEC], out_specs=HBM_SPEC,
        out_shape=jax.ShapeDtypeStruct(shape, src.dtype),
        scratch_shapes=[pltpu.SemaphoreType.DMA, pltpu.SemaphoreType.DMA],
    )(src)


def _chip_exchange(s):
    _, R, W = s.shape

    def body(s_ref, out_ref, send_sems, recv_sems):
        x, y, c, _ = _place()
        copies = []
        for j, (cx, cy) in enumerate(_other_chips(x, y)):
            cp = pltpu.make_async_remote_copy(src_ref=s_ref.at[2 * cx + cy], dst_ref=out_ref.at[j], send_sem=send_sems.at[j],
                                              recv_sem=recv_sems.at[j], device_id=(cx, cy, c), device_id_type=MESH)
            cp.start()
            copies.append(cp)
        for cp in copies:
            cp.wait()

    return pl.pallas_call(
        body, name="reduce_scatter_chips", in_specs=[HBM_SPEC], out_specs=HBM_SPEC,
        out_shape=jax.ShapeDtypeStruct((3, R, W), s.dtype),
        scratch_shapes=[pltpu.SemaphoreType.DMA((3,)), pltpu.SemaphoreType.DMA((3,))],
    )(s)


def _pair_sum(p, recv, c):
    _, _, R, W = p.shape
    tr = _pick(R, (1264, 64, 16))

    def body(c_ref, p_ref, r_ref, s_ref, sb_ref):
        s = p_ref[...] + r_ref[...]
        s_ref[...] = s
        sb_ref[...] = s.astype(BF16)

    blk = pl.BlockSpec((None, tr, W), lambda q, i, c_ref: (q, i, 0))
    return pl.pallas_call(
        body, name="pair_sum",
        grid_spec=pltpu.PrefetchScalarGridSpec(
            num_scalar_prefetch=1, grid=(N_CHIPS, R // tr),
            in_specs=[pl.BlockSpec((None, None, tr, W), lambda q, i, c_ref: (q, c_ref[0], i, 0)), blk],
            out_specs=[blk, blk]),
        out_shape=[jax.ShapeDtypeStruct((N_CHIPS, R, W), F32), jax.ShapeDtypeStruct((N_CHIPS, R, W), BF16)],
        compiler_params=_cparams(("parallel", "parallel")),
    )(c.reshape(1).astype(jnp.int32), p, recv)


def _chip_sum(s, recv, myq):
    _, R, W = s.shape
    tr = _pick(R, (1264, 64, 16))

    def body(q_ref, s_ref, r_ref, o_ref):
        o_ref[...] = ((s_ref[...] + r_ref[0].astype(F32)) + r_ref[1].astype(F32)) + r_ref[2].astype(F32)

    return pl.pallas_call(
        body, name="chip_sum",
        grid_spec=pltpu.PrefetchScalarGridSpec(
            num_scalar_prefetch=1, grid=(R // tr,),
            in_specs=[pl.BlockSpec((None, tr, W), lambda i, q_ref: (q_ref[0], i, 0)),
                      pl.BlockSpec((3, tr, W), lambda i, q_ref: (0, i, 0))],
            out_specs=pl.BlockSpec((tr, W), lambda i, q_ref: (i, 0))),
        out_shape=jax.ShapeDtypeStruct((R, W), F32),
        compiler_params=_cparams(("parallel",)),
    )(myq.reshape(1).astype(jnp.int32), s, recv)


def _adamw(w, g, m, v, name):
    R, W = w.shape
    tr = _pick(R, (632, 316, 32, 8))

    def body(w_ref, g_ref, m_ref, v_ref, d_ref, nm_ref, nv_ref):
        gv = g_ref[...]
        nm = ADAM_B1 * m_ref[...] + (1.0 - ADAM_B1) * gv
        nv = ADAM_B2 * v_ref[...] + (1.0 - ADAM_B2) * (gv * gv)
        m_hat = nm / (1.0 - ADAM_B1 ** ADAM_STEP)
        v_hat = nv / (1.0 - ADAM_B2 ** ADAM_STEP)
        d_ref[...] = -ADAM_LR * (m_hat / (jnp.sqrt(v_hat) + ADAM_EPS) + ADAM_WD * w_ref[...])
        nm_ref[...] = nm
        nv_ref[...] = nv

    blk = pl.BlockSpec((tr, W), lambda i: (i, 0))
    shp = jax.ShapeDtypeStruct((R, W), F32)
    return pl.pallas_call(
        body, name=name, grid=(R // tr,), in_specs=[blk] * 4, out_specs=[blk] * 3, out_shape=[shp] * 3,
        compiler_params=_cparams(("parallel",)),
    )(w, g, m, v)


BIG = ("ffn1_w_in", "ffn1_w_out", "ffn2_w_in", "ffn2_w_out", "a_w_qkv", "a_w_o", "kv_w", "b_w_q", "b_w_o")


def _local_pieces(name, w):
    if name.endswith("w_in") or name.endswith("w_out"):
        return [w[0], w[1]]
    if name == "kv_w":
        return [w[:128], w[128:]]
    w = w[0]
    h = w.shape[0] // 2
    return [w[:h], w[h:]]


def _pack_local(ws, dtype):
    halves = []
    for c in (0, 1):
        halves.append(jnp.concatenate([_local_pieces(n, ws[n])[c].astype(dtype).reshape(-1, D_MODEL) for n in BIG], axis=0))
    return jnp.stack(halves)


def _unpack_local(blob, like):
    out = {}
    r0 = 0
    for n in BIG:
        pcs = _local_pieces(n, like[n])
        rows = pcs[0].size // D_MODEL
        parts = [blob[c, r0:r0 + rows].reshape(pcs[c].shape) for c in (0, 1)]
        r0 += rows
        if n.endswith("w_in") or n.endswith("w_out"):
            out[n] = jnp.stack(parts)
        elif n == "kv_w":
            out[n] = jnp.concatenate(parts, axis=0)
        else:
            out[n] = jnp.concatenate(parts, axis=0)[None]
    return out


def _piece_shapes(local_shapes):
    return {n: _local_pieces(n, np.empty(local_shapes[n], np.int8))[0].shape for n in BIG}


def _unpack_full(g, local_shapes):
    out = {}
    r0 = 0
    ps = _piece_shapes(local_shapes)
    for n in BIG:
        pr, pc = ps[n]
        rows = pr * pc // D_MODEL
        pcs = g[:, :, r0:r0 + rows].reshape(N_CHIPS, 2, pr, pc)
        r0 += rows
        if n.endswith("w_in"):
            out[n] = [jnp.concatenate([pcs[CHIP_OF_SLOT[s], c] for s in range(4)], axis=1) for c in (0, 1)]
        elif n.endswith("w_out"):
            out[n] = [pcs[:, c].reshape(N_CHIPS * pr, pc) for c in (0, 1)]
        elif n == "a_w_qkv":
            out[n] = jnp.concatenate([jnp.concatenate([pcs[q, c] for q in range(4)], axis=1) for c in (0, 1)], axis=0)
        else:
            out[n] = pcs.reshape(N_CHIPS * 2 * pr, pc)
    return out


def _pack_full(gr, local_shapes):
    ps = _piece_shapes(local_shapes)
    cols = []
    for n in BIG:
        pr, pc = ps[n]
        t = gr[n]
        if n.endswith("w_in"):
            pcs = jnp.stack([jnp.stack([t[c][:, SLOT_OF_CHIP[q] * pc:(SLOT_OF_CHIP[q] + 1) * pc] for c in (0, 1)]) for q in range(4)])
        elif n.endswith("w_out"):
            pcs = jnp.stack([jnp.stack([t[c][q * pr:(q + 1) * pr] for c in (0, 1)]) for q in range(4)])
        elif n == "a_w_qkv":
            pcs = jnp.stack([jnp.stack([t[c * pr:(c + 1) * pr, q * pc:(q + 1) * pc] for c in (0, 1)]) for q in range(4)])
        else:
            pcs = t.reshape(N_CHIPS, 2, pr, pc)
        cols.append(pcs.reshape(N_CHIPS, 2, -1, D_MODEL))
    return jnp.concatenate(cols, axis=2)


SMALL_ROWS = 32


def _pack_small(ln_g, ln_b, sinks):
    rows = jnp.concatenate([ln_g.reshape(-1, 128), ln_b.reshape(-1, 128),
                            jnp.pad(sinks.reshape(1, -1), ((0, 0), (0, 128 - sinks.size)))], axis=0)
    return jnp.pad(rows, ((0, SMALL_ROWS - rows.shape[0]), (0, 0)))


def _unpack_small(s, ln_shape, sink_shape):
    n = ln_shape[0] * ln_shape[1] * ln_shape[2] // 128
    return s[:n].reshape(ln_shape), s[n:2 * n].reshape(ln_shape), s[2 * n, :sink_shape[1]].reshape(sink_shape)


def _ffn_fwd(xin, w_in, w_out, gain, bias, tag):
    u, h = _ffn_in(xin, w_in, "ffn_in_" + tag)
    y, yb, z = _mm_ln(h, w_out, xin, gain, bias, 0.5, "ffn_out_ln_" + tag)
    return y, yb, dict(u=u, h=h, z=z, xin=xin)


def _ffn_bwd(dy, saved, w_in, w_out, gain, xin_b, tag):
    dz, dzc, gg, gb = _ln_bwd(saved["z"], dy, gain, 0.5, "ln_bwd_" + tag)
    du = _ffn_bwd_h(dzc, w_out, saved["u"], "ffn_bwd_h_" + tag)
    d_w_out = _mm_tn(saved["h"], dzc, "ffn_dwout_" + tag)
    d_w_in = _mm_tn(xin_b, du, "ffn_dwin_" + tag)
    dx = _mm_nt(du, w_in, "ffn_dx_" + tag, add=dz, add_scale=ALPHA)
    return dx, d_w_in, d_w_out, gg, gb


def kernel(x, ffn1_w_in, ffn1_w_out, ffn2_w_in, ffn2_w_out, ln_g, ln_b, a_w_qkv, a_w_o, kv_w, b_w_q, b_sinks, b_w_o, loss_target, m_ffn1_w_in, m_ffn1_w_out, m_ffn2_w_in, m_ffn2_w_out, m_ln_g, m_ln_b, m_a_w_qkv, m_a_w_o, m_kv_w, m_b_w_q, m_b_sinks, m_b_w_o, v_ffn1_w_in, v_ffn1_w_out, v_ffn2_w_in, v_ffn2_w_out, v_ln_g, v_ln_b, v_a_w_qkv, v_a_w_o, v_kv_w, v_b_w_q, v_b_sinks, v_b_w_o):
    ws = dict(ffn1_w_in=ffn1_w_in, ffn1_w_out=ffn1_w_out, ffn2_w_in=ffn2_w_in, ffn2_w_out=ffn2_w_out, a_w_qkv=a_w_qkv,
              a_w_o=a_w_o, kv_w=kv_w, b_w_q=b_w_q, b_w_o=b_w_o)
    ms = dict(ffn1_w_in=m_ffn1_w_in, ffn1_w_out=m_ffn1_w_out, ffn2_w_in=m_ffn2_w_in, ffn2_w_out=m_ffn2_w_out,
              a_w_qkv=m_a_w_qkv, a_w_o=m_a_w_o, kv_w=m_kv_w, b_w_q=m_b_w_q, b_w_o=m_b_w_o)
    vs = dict(ffn1_w_in=v_ffn1_w_in, ffn1_w_out=v_ffn1_w_out, ffn2_w_in=v_ffn2_w_in, ffn2_w_out=v_ffn2_w_out,
              a_w_qkv=v_a_w_qkv, a_w_o=v_a_w_o, kv_w=v_kv_w, b_w_q=v_b_w_q, b_w_o=v_b_w_o)
    local_shapes = {n: ws[n].shape for n in BIG}
    _, _, c_idx, myq = _place()
    xs = x[0]
    target = loss_target[0]
    S = xs.shape[0]

    gathered, small = _all_gather(_pack_local(ws, BF16), _pack_small(ln_g, ln_b, b_sinks))
    W = _unpack_full(gathered, local_shapes)
    n_ln = ln_g.size // 128
    lg = jnp.concatenate([small[q, :n_ln].reshape(DEPTH, 3, 1, -1) for q in range(N_CHIPS)], axis=-1)
    lb = jnp.concatenate([small[q, n_ln:2 * n_ln].reshape(DEPTH, 3, 1, -1) for q in range(N_CHIPS)], axis=-1)
    sq, grad_x, gr, gg, gb, dsink_part = _local_step(xs, target, W, lg, lb, b_sinks.reshape(N_HEADS))

    loss_row = jnp.pad(jnp.sum(sq).reshape(1, 1), ((0, 0), (0, 127)))
    dsinks = jnp.pad(dsink_part[:, 0, :].reshape(N_SLABS, 2, HEAD_DIM)[:, :, 0].reshape(1, N_HEADS), ((0, 0), (0, 128 - N_HEADS)))
    gg_full = jnp.stack([jnp.stack([jnp.sum(gg[i][j], axis=0) for j in range(3)]) for i in range(DEPTH)])
    gb_full = jnp.stack([jnp.stack([jnp.sum(gb[i][j], axis=0) for j in range(3)]) for i in range(DEPTH)])
    small_in = jnp.concatenate([loss_row, dsinks, gg_full.reshape(-1, 128), gb_full.reshape(-1, 128)], axis=0)
    small_in = jnp.pad(small_in, ((0, (-small_in.shape[0]) % 8), (0, 0)))
    small_sum = _small_all_reduce(small_in)
    loss = small_sum[0, 0] * (0.5 / D_MODEL)
    grad_sinks = small_sum[1, :N_HEADS].reshape(b_sinks.shape)
    n_full = DEPTH * 3 * D_MODEL // 128
    cols = D_MODEL // N_CHIPS
    grad_ln_g = lax.dynamic_slice_in_dim(small_sum[2:2 + n_full].reshape(DEPTH, 3, D_MODEL), myq * cols, cols, axis=2)
    grad_ln_b = lax.dynamic_slice_in_dim(small_sum[2 + n_full:2 + 2 * n_full].reshape(DEPTH, 3, D_MODEL), myq * cols, cols, axis=2)
    return _reduce_and_update(gr, grad_x, loss, grad_ln_g, grad_ln_b, grad_sinks, ws, ms, vs, local_shapes, c_idx, myq,
                              (ln_g, ln_b, b_sinks), (m_ln_g, m_ln_b, m_b_sinks), (v_ln_g, v_ln_b, v_b_sinks))


def _local_step(xs, target, W, lg, lb, sinks):
    S = xs.shape[0]
    slopes = jnp.asarray(_alibi_slopes(N_HEADS))

    y1, y1b, s1 = _ffn_fwd(xs, W["ffn1_w_in"][0], W["ffn1_w_out"][0], lg[0, 0], lb[0, 0], "a1")
    qkv_a = _mm_nn(y1b, W["a_w_qkv"], F32, "qkv_a", split=True)
    mix_a, o_a, lse_a = _attn_fwd(qkv_a, slopes, None, PATTERNS_A, "attn_a_fwd")
    y2, y2b, z2 = _mm_ln(mix_a, W["a_w_o"], y1, lg[0, 1], lb[0, 1], 1.0, "attn_a_out_ln")
    y3, y3b, s3 = _ffn_fwd(y2, W["ffn2_w_in"][0], W["ffn2_w_out"][0], lg[0, 2], lb[0, 2], "a2")
    kv = _mm_nn(y3b, W["kv_w"], F32, "kv_proj")
    y4, y4b, s4 = _ffn_fwd(y3, W["ffn1_w_in"][1], W["ffn1_w_out"][1], lg[1, 0], lb[1, 0], "b1")
    q_b = _mm_nn(y4b, W["b_w_q"], F32, "q_b")
    k_sh = kv[:, :N_KV_B * HEAD_DIM].reshape(S, N_KV_B, 1, HEAD_DIM)
    v_sh = kv[:, N_KV_B * HEAD_DIM:].reshape(S, N_KV_B, 1, HEAD_DIM)
    k_exp = jnp.broadcast_to(k_sh, (S, N_KV_B, GROUP_B, HEAD_DIM)).reshape(S, D_MODEL)
    v_exp = jnp.broadcast_to(v_sh, (S, N_KV_B, GROUP_B, HEAD_DIM)).reshape(S, D_MODEL)
    qkv_b = jnp.stack([q_b, k_exp, v_exp])
    mix_b, o_b, lse_b = _attn_fwd(qkv_b, slopes, sinks, PATTERNS_B, "attn_b_fwd")
    y5, y5b, z5 = _mm_ln(mix_b, W["b_w_o"], y4, lg[1, 1], lb[1, 1], 1.0, "attn_b_out_ln")
    y6, _, s6 = _ffn_fwd(y5, W["ffn2_w_in"][1], W["ffn2_w_out"][1], lg[1, 2], lb[1, 2], "b2")

    dy6, sq = _loss_grad(y6, target, "loss_grad")
    gr = {n: None for n in BIG}
    gg = [[None] * 3 for _ in range(DEPTH)]
    gb = [[None] * 3 for _ in range(DEPTH)]

    dy5, d_in2_b, d_out2_b, gg[1][2], gb[1][2] = _ffn_bwd(dy6, s6, W["ffn2_w_in"][1], W["ffn2_w_out"][1], lg[1, 2], y5b, "b2")
    dz5, dz5b, gg[1][1], gb[1][1] = _ln_bwd(z5, dy5, lg[1, 1], 1.0, "ln_bwd_attn_b")
    gr["b_w_o"] = _mm_tn(mix_b, dz5b, "d_b_w_o")
    dmix_b = _mm_nt(dz5b, W["b_w_o"], "d_mix_b")
    dqkv_b, dsink_part = _attn_bwd(qkv_b, dmix_b, o_b, lse_b, slopes, sinks, PATTERNS_B, "attn_b_bwd")
    dq_b, dk_exp, dv_exp = dqkv_b[0], dqkv_b[1], dqkv_b[2]
    dkv = jnp.concatenate([dk_exp.reshape(S, N_KV_B, GROUP_B, HEAD_DIM).sum(axis=2).reshape(S, -1),
                           dv_exp.reshape(S, N_KV_B, GROUP_B, HEAD_DIM).sum(axis=2).reshape(S, -1)], axis=1)
    gr["b_w_q"] = _mm_tn(y4b, dq_b, "d_b_w_q")
    dy4 = _mm_nt(dq_b, W["b_w_q"], "d_y4", add=dz5, add_scale=ALPHA)
    dy3, d_in1_b, d_out1_b, gg[1][0], gb[1][0] = _ffn_bwd(dy4, s4, W["ffn1_w_in"][1], W["ffn1_w_out"][1], lg[1, 0], y3b, "b1")
    gr["kv_w"] = _mm_tn(y3b, dkv, "d_kv_w")
    dy3 = _mm_nt(dkv, W["kv_w"], "d_y3_kv", add=dy3, add_scale=1.0)

    dy2, d_in2_a, d_out2_a, gg[0][2], gb[0][2] = _ffn_bwd(dy3, s3, W["ffn2_w_in"][0], W["ffn2_w_out"][0], lg[0, 2], y2b, "a2")
    dz2, dz2b, gg[0][1], gb[0][1] = _ln_bwd(z2, dy2, lg[0, 1], 1.0, "ln_bwd_attn_a")
    gr["a_w_o"] = _mm_tn(mix_a, dz2b, "d_a_w_o")
    dmix_a = _mm_nt(dz2b, W["a_w_o"], "d_mix_a")
    dqkv_a, _ = _attn_bwd(qkv_a, dmix_a, o_a, lse_a, slopes, None, PATTERNS_A, "attn_a_bwd")
    gr["a_w_qkv"] = _mm_tn(y1b, dqkv_a, "d_a_w_qkv", split=True)
    dy1 = _mm_nt(dqkv_a, W["a_w_qkv"], "d_y1", add=dz2, add_scale=ALPHA, split=True)
    grad_x, d_in1_a, d_out1_a, gg[0][0], gb[0][0] = _ffn_bwd(dy1, s1, W["ffn1_w_in"][0], W["ffn1_w_out"][0], lg[0, 0], xs, "a1")
    gr["ffn1_w_in"] = [d_in1_a, d_in1_b]
    gr["ffn1_w_out"] = [d_out1_a, d_out1_b]
    gr["ffn2_w_in"] = [d_in2_a, d_in2_b]
    gr["ffn2_w_out"] = [d_out2_a, d_out2_b]
    return sq, grad_x, gr, gg, gb, dsink_part


def _reduce_and_update(gr, grad_x, loss, grad_ln_g, grad_ln_b, grad_sinks, ws, ms, vs, local_shapes, c_idx, myq,
                       small_w, small_m, small_v):
    ln_g, ln_b, b_sinks = small_w
    m_ln_g, m_ln_b, m_b_sinks = small_m
    v_ln_g, v_ln_b, v_b_sinks = small_v

    packed = _pack_full(gr, local_shapes)
    from_sibling = _sibling_exchange(packed, "other", "reduce_scatter_pair")
    pair_f32, pair_bf16 = _pair_sum(packed, from_sibling, c_idx)
    from_chips = _chip_exchange(pair_bf16)
    mine = _chip_sum(pair_f32, from_chips, myq)
    theirs = _sibling_exchange(mine, "all", "share_halves")
    is0 = c_idx == 0
    g_blob = jnp.stack([jnp.where(is0, mine, theirs), jnp.where(is0, theirs, mine)])

    R = g_blob.shape[1]
    delta_b, nm_b, nv_b = _adamw(_pack_local(ws, F32).reshape(2 * R, D_MODEL), g_blob.reshape(2 * R, D_MODEL),
                                 _pack_local(ms, F32).reshape(2 * R, D_MODEL), _pack_local(vs, F32).reshape(2 * R, D_MODEL), "adamw_big")
    grads = _unpack_local(g_blob, ws)
    deltas = _unpack_local(delta_b.reshape(2, R, D_MODEL), ws)
    new_m = _unpack_local(nm_b.reshape(2, R, D_MODEL), ws)
    new_v = _unpack_local(nv_b.reshape(2, R, D_MODEL), ws)
    delta_s, nm_s, nv_s = _adamw(_pack_small(ln_g, ln_b, b_sinks), _pack_small(grad_ln_g, grad_ln_b, grad_sinks),
                                 _pack_small(m_ln_g, m_ln_b, m_b_sinks), _pack_small(v_ln_g, v_ln_b, v_b_sinks), "adamw_small")
    for d, blob in ((grads, None), (deltas, delta_s), (new_m, nm_s), (new_v, nv_s)):
        if blob is None:
            d["ln_g"], d["ln_b"], d["b_sinks"] = grad_ln_g, grad_ln_b, grad_sinks
        else:
            d["ln_g"], d["ln_b"], d["b_sinks"] = _unpack_small(blob, ln_g.shape, b_sinks.shape)

    order = ("ffn1_w_in", "ffn1_w_out", "ffn2_w_in", "ffn2_w_out", "ln_g", "ln_b", "a_w_qkv", "a_w_o", "kv_w", "b_w_q",
             "b_sinks", "b_w_o")
    outs = [loss, grad_x[None]]
    for d in (grads, deltas, new_m, new_v):
        outs += [d[n] for n in order]
    return tuple(outs)
```

```python
import numpy as np
import jax
import jax.numpy as jnp
from jax import lax
from jax.experimental import pallas as pl
from jax.experimental.pallas import tpu as pltpu

F32 = jnp.float32
BF16 = jnp.bfloat16

D_MODEL = 1024
D_FF = 2816
HALF_FF = D_FF // 2
HEAD_DIM = 64
N_HEADS = 16
N_KV_B = 4
GROUP_B = N_HEADS // N_KV_B
DEPTH = 2
ALPHA = (2.0 * DEPTH) ** 0.25
LN_EPS = 1e-5
BLOCK = 128
SLAB = 128
N_SLABS = D_MODEL // SLAB
PATTERNS_A = ((1, 128, 1.0), (4, 128, 4.0), (16, 128, 16.0))
PATTERNS_B = ((1, 127, 1.0),)
NEG = -1e30

ADAM_LR = 0.001
ADAM_B1 = 0.9
ADAM_B2 = 0.999
ADAM_EPS = 1e-08
ADAM_WD = 0.01
ADAM_STEP = 10

N_CHIPS = 4
VMEM_LIMIT = 56 * 1024 * 1024
MESH = pl.DeviceIdType.MESH


def _alibi_slopes(n):
    return np.array([2.0 ** (-8.0 * (h + 1) / n) for h in range(n)], dtype=np.float32)


def _cparams(sem=None, vmem=VMEM_LIMIT):
    return pltpu.CompilerParams(dimension_semantics=sem, vmem_limit_bytes=vmem)


_DIMS = {"nn": ((1,), (0,)), "nt": ((1,), (1,)), "tn": ((0,), (0,))}


def _unlead(x):
    if isinstance(x, tuple):
        return x[0], x[1], x[0].shape[1:]
    return x, None, x.shape


def _bspec(block, imap, lead=None):
    if lead is None:
        return pl.BlockSpec(block, imap)
    return pl.BlockSpec((None,) + tuple(block), lambda *g: (lead,) + tuple(imap(*g)))


def _matmul(a, b, mode, out_dtype, tm, tn, tk, name, add=None, add_scale=1.0, split=False):
    out_spec = pl.BlockSpec((tm, tn), lambda i, j, k: (i, j))
    if mode == "nn":
        a, al, (M, K) = _unlead(a)
        b, bl, (K2, N) = _unlead(b)
        a_spec = _bspec((tm, tk), lambda i, j, k: (i, k), al)
        b_spec = _bspec((tk, tn), lambda i, j, k: (k, j), bl)
        out_struct = jax.ShapeDtypeStruct((M, N), out_dtype)
        if split:
            assert tn == D_MODEL
            out_spec = pl.BlockSpec((None, tm, tn), lambda i, j, k: (j, i, 0))
            out_struct = jax.ShapeDtypeStruct((N // tn, M, tn), out_dtype)
    elif mode == "nt":
        b, bl, (N, K2) = _unlead(b)
        if split:
            assert tk == D_MODEL
            M, K = a.shape[1], a.shape[0] * a.shape[2]
            a_spec = pl.BlockSpec((None, tm, tk), lambda i, j, k: (k, i, 0))
        else:
            a, al, (M, K) = _unlead(a)
            a_spec = _bspec((tm, tk), lambda i, j, k: (i, k), al)
        b_spec = _bspec((tn, tk), lambda i, j, k: (j, k), bl)
        out_struct = jax.ShapeDtypeStruct((M, N), out_dtype)
    else:
        a, al, (K, M) = _unlead(a)
        if split:
            assert tn == D_MODEL
            K2, N = b.shape[1], b.shape[0] * b.shape[2]
            b_spec = pl.BlockSpec((None, tk, tn), lambda i, j, k: (j, k, 0))
        else:
            b, bl, (K2, N) = _unlead(b)
            b_spec = _bspec((tk, tn), lambda i, j, k: (k, j), bl)
        a_spec = _bspec((tk, tm), lambda i, j, k: (k, i), al)
        out_struct = jax.ShapeDtypeStruct((M, N), out_dtype)
    assert K == K2 and M % tm == 0 and N % tn == 0 and K % tk == 0, (a.shape, b.shape, mode, tm, tn, tk)
    nk = K // tk
    dims = (_DIMS[mode], ((), ()))
    has_add = add is not None

    def body(*refs):
        if has_add:
            a_ref, b_ref, add_ref, o_ref, acc_ref = refs
        else:
            a_ref, b_ref, o_ref, acc_ref = refs
        k = pl.program_id(2)
        part = lax.dot_general(a_ref[...].astype(BF16), b_ref[...].astype(BF16), dims, preferred_element_type=F32)

        @pl.when(k == 0)
        def _():
            acc_ref[...] = part

        @pl.when(k > 0)
        def _():
            acc_ref[...] += part

        @pl.when(k == nk - 1)
        def _():
            r = acc_ref[...]
            if has_add:
                r = r + add_scale * add_ref[...]
            o_ref[...] = r.astype(out_dtype)

    in_specs = [a_spec, b_spec]
    args = [a, b]
    if has_add:
        in_specs.append(pl.BlockSpec((tm, tn), lambda i, j, k: (i, j)))
        args.append(add)
    return pl.pallas_call(
        body, name=name, grid=(M // tm, N // tn, nk),
        in_specs=in_specs, out_specs=out_spec, out_shape=out_struct,
        scratch_shapes=[pltpu.VMEM((tm, tn), F32)],
        compiler_params=_cparams(("parallel", "parallel", "arbitrary")),
    )(*args)


def _pick(n, cands):
    for c in cands:
        if n % c == 0:
            return c
    raise ValueError((n, cands))


def _mm_nn(a, b, out_dtype, name, split=False):
    M, K = _unlead(a)[2]
    N = _unlead(b)[2][1]
    return _matmul(a, b, "nn", out_dtype, _pick(M, (1024, 512, 256)), _pick(N, (1024, 512)), _pick(K, (1024, 512)), name,
                   split=split)


def _mm_nt(a, b, name, add=None, add_scale=1.0, split=False):
    M, K = (a.shape[1], D_MODEL) if split else _unlead(a)[2]
    N = _unlead(b)[2][0]
    return _matmul(a, b, "nt", F32, _pick(M, (1024, 512, 256)), _pick(N, (1024, 512)),
                   _pick(K, (1408, 1024, 512)), name, add=add, add_scale=add_scale, split=split)


def _mm_tn(a, b, name, split=False):
    K, M = _unlead(a)[2]
    N = D_MODEL if split else _unlead(b)[2][1]
    return _matmul(a, b, "tn", F32, _pick(M, (1024, 1408, 512)), _pick(N, (1408, 1024, 512)),
                   _pick(K, (1024, 512, 256)), name, split=split)


def _ffn_in(x, w, name):
    S = x.shape[0]
    tm = _pick(S, (512, 256))
    w, wl, _ = _unlead(w)

    def body(x_ref, w_ref, u_ref, h_ref):
        acc = jnp.dot(x_ref[...].astype(BF16), w_ref[...], preferred_element_type=F32)
        g = acc[:, :HALF_FF]
        up = acc[:, HALF_FF:]
        u_ref[...] = acc.astype(BF16)
        h_ref[...] = (g * jax.nn.sigmoid(g) * up).astype(BF16)

    return pl.pallas_call(
        body, name=name, grid=(2, S // tm),
        in_specs=[pl.BlockSpec((tm, D_MODEL), lambda j, i: (i, 0)),
                  _bspec((D_MODEL, D_FF), lambda j, i: (0, j), wl)],
        out_specs=[pl.BlockSpec((tm, D_FF), lambda j, i: (i, j)),
                   pl.BlockSpec((tm, HALF_FF), lambda j, i: (i, j))],
        out_shape=[jax.ShapeDtypeStruct((S, 2 * D_FF), BF16), jax.ShapeDtypeStruct((S, D_FF), BF16)],
        compiler_params=_cparams(("parallel", "parallel")),
    )(x, w)


def _ffn_bwd_h(dzc, w_out, u, name):
    S = dzc.shape[0]
    tm = _pick(S, (512, 256))
    w_out, wl, _ = _unlead(w_out)

    def body(dz_ref, w_ref, u_ref, du_ref):
        dh = lax.dot_general(dz_ref[...], w_ref[...], (((1,), (1,)), ((), ())), preferred_element_type=F32)
        g = u_ref[:, :HALF_FF].astype(F32)
        up = u_ref[:, HALF_FF:].astype(F32)
        sg = jax.nn.sigmoid(g)
        du_ref[:, :HALF_FF] = (dh * up * (sg * (1.0 + g * (1.0 - sg)))).astype(BF16)
        du_ref[:, HALF_FF:] = (dh * (g * sg)).astype(BF16)

    return pl.pallas_call(
        body, name=name, grid=(2, S // tm),
        in_specs=[pl.BlockSpec((tm, D_MODEL), lambda j, i: (i, 0)),
                  _bspec((HALF_FF, D_MODEL), lambda j, i: (j, 0), wl),
                  pl.BlockSpec((tm, D_FF), lambda j, i: (i, j))],
        out_specs=pl.BlockSpec((tm, D_FF), lambda j, i: (i, j)),
        out_shape=jax.ShapeDtypeStruct((S, 2 * D_FF), BF16),
        compiler_params=_cparams(("parallel", "parallel")),
    )(dzc, w_out, u)


def _mm_ln(a, w, resid, gain, bias, c, name):
    S, K = a.shape
    tm = _pick(S, (512, 256))
    tk = _pick(K, (1408, 1024))
    nk = K // tk
    w, wl, _ = _unlead(w)

    def body(a_ref, w_ref, r_ref, g_ref, b_ref, y_ref, yb_ref, z_ref, acc_ref):
        k = pl.program_id(1)
        part = jnp.dot(a_ref[...], w_ref[...], preferred_element_type=F32)

        @pl.when(k == 0)
        def _():
            acc_ref[...] = part

        @pl.when(k > 0)
        def _():
            acc_ref[...] += part

        @pl.when(k == nk - 1)
        def _():
            z = ALPHA * r_ref[...] + c * acc_ref[...]
            mu = jnp.mean(z, axis=-1, keepdims=True)
            zc = z - mu
            var = jnp.mean(zc * zc, axis=-1, keepdims=True)
            y = zc * lax.rsqrt(var + LN_EPS) * g_ref[...] + b_ref[...]
            z_ref[...] = z
            y_ref[...] = y
            yb_ref[...] = y.astype(BF16)

    row = pl.BlockSpec((tm, D_MODEL), lambda i, k: (i, 0))
    vec = pl.BlockSpec((1, D_MODEL), lambda i, k: (0, 0))
    return pl.pallas_call(
        body, name=name, grid=(S // tm, nk),
        in_specs=[pl.BlockSpec((tm, tk), lambda i, k: (i, k)), _bspec((tk, D_MODEL), lambda i, k: (k, 0), wl),
                  row, vec, vec],
        out_specs=[row, row, row],
        out_shape=[jax.ShapeDtypeStruct((S, D_MODEL), F32), jax.ShapeDtypeStruct((S, D_MODEL), BF16),
                   jax.ShapeDtypeStruct((S, D_MODEL), F32)],
        scratch_shapes=[pltpu.VMEM((tm, D_MODEL), F32)],
        compiler_params=_cparams(("parallel", "arbitrary")),
    )(a, w, resid, gain, bias)


def _ln_bwd(z, dy, gain, c, name):
    S = z.shape[0]
    tm = _pick(S, (512, 256))

    def body(z_ref, dy_ref, g_ref, dz_ref, dzc_ref, gg_ref, gb_ref):
        i = pl.program_id(0)
        zv = z_ref[...]
        dyv = dy_ref[...]
        mu = jnp.mean(zv, axis=-1, keepdims=True)
        zc = zv - mu
        var = jnp.mean(zc * zc, axis=-1, keepdims=True)
        rstd = lax.rsqrt(var + LN_EPS)
        xhat = zc * rstd
        dyg = dyv * g_ref[...]
        m1 = jnp.mean(dyg, axis=-1, keepdims=True)
        m2 = jnp.mean(dyg * xhat, axis=-1, keepdims=True)
        dz = rstd * (dyg - m1 - xhat * m2)
        dz_ref[...] = dz
        dzc_ref[...] = (c * dz).astype(BF16)
        pg = jnp.sum((dyv * xhat).reshape(tm // 8, 8, D_MODEL), axis=0)
        pb = jnp.sum(dyv.reshape(tm // 8, 8, D_MODEL), axis=0)

        @pl.when(i == 0)
        def _():
            gg_ref[...] = pg
            gb_ref[...] = pb

        @pl.when(i > 0)
        def _():
            gg_ref[...] += pg
            gb_ref[...] += pb

    row = pl.BlockSpec((tm, D_MODEL), lambda i: (i, 0))
    part = pl.BlockSpec((8, D_MODEL), lambda i: (0, 0))
    return pl.pallas_call(
        body, name=name, grid=(S // tm,),
        in_specs=[row, row, pl.BlockSpec((1, D_MODEL), lambda i: (0, 0))],
        out_specs=[row, row, part, part],
        out_shape=[jax.ShapeDtypeStruct((S, D_MODEL), F32), jax.ShapeDtypeStruct((S, D_MODEL), BF16),
                   jax.ShapeDtypeStruct((8, D_MODEL), F32), jax.ShapeDtypeStruct((8, D_MODEL), F32)],
        compiler_params=_cparams(("arbitrary",)),
    )(z, dy, gain)


def _loss_grad(y, t, name):
    S = y.shape[0]
    tm = _pick(S, (512, 256))

    def body(y_ref, t_ref, dy_ref, sq_ref):
        i = pl.program_id(0)
        e = y_ref[...] - t_ref[...]
        dy_ref[...] = e * (1.0 / D_MODEL)
        ps = jnp.sum((e * e).reshape(tm // 8, 8, D_MODEL), axis=0)

        @pl.when(i == 0)
        def _():
            sq_ref[...] = ps

        @pl.when(i > 0)
        def _():
            sq_ref[...] += ps

    row = pl.BlockSpec((tm, D_MODEL), lambda i: (i, 0))
    return pl.pallas_call(
        body, name=name, grid=(S // tm,),
        in_specs=[row, row], out_specs=[row, pl.BlockSpec((8, D_MODEL), lambda i: (0, 0))],
        out_shape=[jax.ShapeDtypeStruct((S, D_MODEL), F32), jax.ShapeDtypeStruct((8, D_MODEL), F32)],
        compiler_params=_cparams(("arbitrary",)),
    )(y, t)


def _rows(start, d):
    if d == 1:
        return pl.ds(pl.multiple_of(start, BLOCK), BLOCK)
    return pl.ds(start, BLOCK, stride=d)


def _ld(ref, start, d):
    return ref[_rows(start, d), :]


def _ld3(ref, lead, start, d):
    return ref[lead, _rows(start, d), :]


def _st3(ref, lead, start, d, val):
    ref[lead, _rows(start, d), :] = val


def _acc3(ref, lead, start, d, val):
    ref[lead, _rows(start, d), :] = ref[lead, _rows(start, d), :] + val


def _band_consts():
    qi = lax.broadcasted_iota(jnp.int32, (BLOCK, 2 * BLOCK), 0)
    kj = lax.broadcasted_iota(jnp.int32, (BLOCK, 2 * BLOCK), 1)
    return BLOCK + qi - kj, kj


def _scores(q, k2, hm, slope, dsc, valid):
    qm = jnp.where(hm, q, 0.0).astype(BF16)
    s = lax.dot_general(qm, k2, (((1,), (1,)), ((), ())), preferred_element_type=F32) * (HEAD_DIM ** -0.5)
    return qm, jnp.where(valid, s - slope * dsc, NEG)


def _softmax_weights(ls):
    mx = ls[0]
    for l in ls[1:]:
        mx = jnp.maximum(mx, l)
    es = [jnp.exp(l - mx) for l in ls]
    tot = es[0]
    for e in es[1:]:
        tot = tot + e
    inv = 1.0 / tot
    return [e * inv for e in es]


def _attn_fwd(qkv, slopes, sinks, patterns, name):
    S = qkv.shape[1]
    npat = len(patterns)
    has_sink = sinks is not None
    if not has_sink:
        sinks = jnp.zeros((N_HEADS,), F32)
    rows_c = 256

    def body(slopes_ref, sinks_ref, x_ref, mix_ref, o_ref, lse_ref):
        p = pl.program_id(0)
        lo = lax.broadcasted_iota(jnp.int32, (BLOCK, SLAB), 1) < HEAD_DIM
        dist, kj = _band_consts()
        distf = dist.astype(F32)
        for pi, (d, maxd, scale) in enumerate(patterns):
            nb = S // d // BLOCK
            band = (dist >= 0) & (dist <= maxd)
            dsc = distf * scale

            def blk(t, carry, pi=pi, d=d, nb=nb, band=band, dsc=dsc):
                r = t // nb
                n = t - r * nb
                start = r + (d * BLOCK) * n
                prev = jnp.where(n > 0, start - d * BLOCK, start)
                valid = band & (kj + jnp.where(n > 0, BLOCK, 0) >= BLOCK)
                q = _ld3(x_ref, 0, start, d)
                k2 = jnp.concatenate([_ld3(x_ref, 1, prev, d), _ld3(x_ref, 1, start, d)], axis=0).astype(BF16)
                v2 = jnp.concatenate([_ld3(x_ref, 2, prev, d), _ld3(x_ref, 2, start, d)], axis=0).astype(BF16)
                outs, lses = [], []
                for h in (0, 1):
                    hm = lo if h == 0 else jnp.logical_not(lo)
                    _, s = _scores(q, k2, hm, slopes_ref[2 * p + h], dsc, valid)
                    m = jnp.max(s, axis=-1, keepdims=True)
                    if has_sink:
                        sk = sinks_ref[2 * p + h]
                        m = jnp.maximum(m, sk)
                    e = jnp.exp(s - m)
                    den = jnp.sum(e, axis=-1, keepdims=True)
                    if has_sink:
                        den = den + jnp.exp(sk - m)
                    outs.append(jnp.dot((e / den).astype(BF16), v2, preferred_element_type=F32))
                    lses.append(m + jnp.log(den))
                _st3(o_ref, pi, start, d, jnp.where(lo, outs[0], outs[1]))
                _st3(lse_ref, pi, start, d, jnp.where(lo, lses[0], lses[1]))
                return carry

            lax.fori_loop(0, d * nb, blk, 0)

        def comb(ci, carry):
            rows = pl.ds(pl.multiple_of(ci * rows_c, rows_c), rows_c)
            if npat == 1:
                mix_ref[rows, :] = o_ref[0, rows, :].astype(BF16)
            else:
                ws = _softmax_weights([lse_ref[i, rows, :] for i in range(npat)])
                acc = ws[0] * o_ref[0, rows, :]
                for i in range(1, npat):
                    acc = acc + ws[i] * o_ref[i, rows, :]
                mix_ref[rows, :] = acc.astype(BF16)
            return carry

        lax.fori_loop(0, S // rows_c, comb, 0)

    smem = pl.BlockSpec(memory_space=pltpu.SMEM)
    slab3 = pl.BlockSpec((npat, S, SLAB), lambda p: (0, 0, p))
    return pl.pallas_call(
        body, name=name, grid=(N_SLABS,),
        in_specs=[smem, smem, pl.BlockSpec((3, S, SLAB), lambda p: (0, 0, p))],
        out_specs=[pl.BlockSpec((S, SLAB), lambda p: (0, p)), slab3, slab3],
        out_shape=[jax.ShapeDtypeStruct((S, D_MODEL), BF16), jax.ShapeDtypeStruct((npat, S, D_MODEL), F32),
                   jax.ShapeDtypeStruct((npat, S, D_MODEL), F32)],
        compiler_params=_cparams(("arbitrary",)),
    )(slopes, sinks, qkv)


def _attn_bwd(qkv, dout, o, lse, slopes, sinks, patterns, name):
    S = qkv.shape[1]
    npat = len(patterns)
    has_sink = sinks is not None
    if not has_sink:
        sinks = jnp.zeros((N_HEADS,), F32)
    rows_c = 256

    def headsum(x, lo):
        s0 = jnp.sum(jnp.where(lo, x, 0.0), axis=-1, keepdims=True)
        s1 = jnp.sum(jnp.where(lo, 0.0, x), axis=-1, keepdims=True)
        return jnp.where(lo, s0, s1)

    def body(slopes_ref, sinks_ref, x_ref, do_ref, o_ref, lse_ref, dx_ref, dsink_ref, dbar_ref, sacc_ref):
        p = pl.program_id(0)
        lo = lax.broadcasted_iota(jnp.int32, (BLOCK, SLAB), 1) < HEAD_DIM
        lo_c = lax.broadcasted_iota(jnp.int32, (rows_c, SLAB), 1) < HEAD_DIM
        dist, kj = _band_consts()
        distf = dist.astype(F32)

        def prep(ci, carry):
            rows = pl.ds(pl.multiple_of(ci * rows_c, rows_c), rows_c)
            dov = do_ref[rows, :]
            dx_ref[:, rows, :] = jnp.zeros((3, rows_c, SLAB), F32)
            if npat == 1:
                dbar_ref[rows, :] = headsum(dov * o_ref[0, rows, :], lo_c)
            else:
                ws = _softmax_weights([lse_ref[i, rows, :] for i in range(npat)])
                acc = ws[0] * headsum(dov * o_ref[0, rows, :], lo_c)
                for i in range(1, npat):
                    acc = acc + ws[i] * headsum(dov * o_ref[i, rows, :], lo_c)
                dbar_ref[rows, :] = acc
            return carry

        lax.fori_loop(0, S // rows_c, prep, 0)
        sacc_ref[...] = jnp.zeros((BLOCK, SLAB), F32)

        for pi, (d, maxd, scale) in enumerate(patterns):
            nb = S // d // BLOCK
            band = (dist >= 0) & (dist <= maxd)
            dsc = distf * scale

            def blk(t, carry, pi=pi, d=d, nb=nb, band=band, dsc=dsc):
                r = t // nb
                n = t - r * nb
                start = r + (d * BLOCK) * n
                prev = jnp.where(n > 0, start - d * BLOCK, start)
                valid = band & (kj + jnp.where(n > 0, BLOCK, 0) >= BLOCK)
                q = _ld3(x_ref, 0, start, d)
                k2 = jnp.concatenate([_ld3(x_ref, 1, prev, d), _ld3(x_ref, 1, start, d)], axis=0).astype(BF16)
                v2 = jnp.concatenate([_ld3(x_ref, 2, prev, d), _ld3(x_ref, 2, start, d)], axis=0).astype(BF16)
                ls = [_ld3(lse_ref, i, start, d) for i in range(npat)]
                w = _softmax_weights(ls)[pi] if npat > 1 else 1.0
                d_o = w * _ld(do_ref, start, d)
                dl = w * _ld(dbar_ref, start, d)
                dk2 = jnp.zeros((2 * BLOCK, SLAB), F32)
                dv2 = jnp.zeros((2 * BLOCK, SLAB), F32)
                dqs = []
                sk_terms = []
                for h in (0, 1):
                    hm = lo if h == 0 else jnp.logical_not(lo)
                    c0 = h * HEAD_DIM
                    qm, s = _scores(q, k2, hm, slopes_ref[2 * p + h], dsc, valid)
                    lse_h = ls[pi][:, c0:c0 + 1]
                    dl_h = dl[:, c0:c0 + 1]
                    pr = jnp.exp(s - lse_h)
                    dom = jnp.where(hm, d_o, 0.0).astype(BF16)
                    dp = lax.dot_general(dom, v2, (((1,), (1,)), ((), ())), preferred_element_type=F32)
                    ds = (pr * (dp - dl_h) * (HEAD_DIM ** -0.5)).astype(BF16)
                    dqs.append(jnp.dot(ds, k2, preferred_element_type=F32))
                    dk2 = dk2 + lax.dot_general(ds, qm, (((0,), (0,)), ((), ())), preferred_element_type=F32)
                    dv2 = dv2 + lax.dot_general(pr.astype(BF16), dom, (((0,), (0,)), ((), ())), preferred_element_type=F32)
                    if has_sink:
                        sk_terms.append(-jnp.exp(sinks_ref[2 * p + h] - lse_h) * dl_h)
                _acc3(dx_ref, 0, start, d, jnp.where(lo, dqs[0], dqs[1]))
                _acc3(dx_ref, 1, prev, d, dk2[:BLOCK])
                _acc3(dx_ref, 1, start, d, dk2[BLOCK:])
                _acc3(dx_ref, 2, prev, d, dv2[:BLOCK])
                _acc3(dx_ref, 2, start, d, dv2[BLOCK:])
                if has_sink:
                    sacc_ref[...] += jnp.where(lo, sk_terms[0], sk_terms[1])
                return carry

            lax.fori_loop(0, d * nb, blk, 0)

        dsink_ref[...] = jnp.broadcast_to(jnp.sum(sacc_ref[...], axis=0, keepdims=True), (8, SLAB))

    smem = pl.BlockSpec(memory_space=pltpu.SMEM)
    one = pl.Buffered(1)
    slab3 = pl.BlockSpec((npat, S, SLAB), lambda p: (0, 0, p), pipeline_mode=one)
    return pl.pallas_call(
        body, name=name, grid=(N_SLABS,),
        in_specs=[smem, smem, pl.BlockSpec((3, S, SLAB), lambda p: (0, 0, p), pipeline_mode=one),
                  pl.BlockSpec((S, SLAB), lambda p: (0, p), pipeline_mode=one), slab3, slab3],
        out_specs=[pl.BlockSpec((3, S, SLAB), lambda p: (0, 0, p)), pl.BlockSpec((None, 8, SLAB), lambda p: (p, 0, 0))],
        out_shape=[jax.ShapeDtypeStruct((3, S, D_MODEL), F32), jax.ShapeDtypeStruct((N_SLABS, 8, SLAB), F32)],
        scratch_shapes=[pltpu.VMEM((S, SLAB), F32), pltpu.VMEM((BLOCK, SLAB), F32)],
        compiler_params=_cparams(("arbitrary",)),
    )(slopes, sinks, qkv, dout, o, lse)


def _place():
    x, y, c = lax.axis_index("x"), lax.axis_index("y"), lax.axis_index("c")
    return x, y, c, 2 * x + y


def _other_chips(x, y):
    return [(1 - x, y), (x, 1 - y), (1 - x, 1 - y)]


HBM_SPEC = pl.BlockSpec(memory_space=pl.ANY)


def _slot(q):
    return 2 * (q % 2) + q // 2


BIG = ("ffn1_w_in", "ffn1_w_out", "ffn2_w_in", "ffn2_w_out", "a_w_qkv", "a_w_o", "kv_w", "b_w_q", "b_w_o")
QKV_SHARD = 3 * D_MODEL // N_CHIPS
ROW_SHARD = D_MODEL // N_CHIPS


def _full_shape(name):
    if name.endswith("w_in"):
        return (DEPTH, D_MODEL, 2 * D_FF)
    if name.endswith("w_out"):
        return (DEPTH, D_FF, D_MODEL)
    if name == "a_w_qkv":
        return (D_MODEL, 3 * D_MODEL)
    if name == "kv_w":
        return (N_CHIPS, 2, ROW_SHARD // 2, 2 * N_KV_B * HEAD_DIM)
    return (N_CHIPS, 2, ROW_SHARD // 2, D_MODEL)


def _gather_src(name, ref, c):
    if name.endswith("w_in") or name.endswith("w_out"):
        return ref.at[c]
    if name == "a_w_qkv":
        return ref.at[0, pl.ds(c * (D_MODEL // 2), D_MODEL // 2)]
    if name == "kv_w":
        return ref.at[pl.ds(c * (ROW_SHARD // 2), ROW_SHARD // 2)]
    return ref.at[0, pl.ds(c * (ROW_SHARD // 2), ROW_SHARD // 2)]


def _gather_dst(name, ref, q, c):
    if name.endswith("w_in"):
        return ref.at[c, :, pl.ds(_slot(q) * HALF_FF, HALF_FF)]
    if name.endswith("w_out"):
        return ref.at[c, pl.ds(q * (D_FF // N_CHIPS), D_FF // N_CHIPS)]
    if name == "a_w_qkv":
        return ref.at[pl.ds(c * (D_MODEL // 2), D_MODEL // 2), pl.ds(q * QKV_SHARD, QKV_SHARD)]
    return ref.at[q, c]


def _all_gather(shards, small):
    n = len(BIG)
    r = small.shape[0]

    def body(*refs):
        srcs, small_ref = refs[:n], refs[n]
        dsts, s_ref = refs[n + 1:2 * n + 1], refs[2 * n + 1]
        send_sems, recv_sems, local_sems = refs[2 * n + 2:]
        x, y, c, myq = _place()
        sibling = (x, y, 1 - c)
        chips = _other_chips(x, y)

        def big(t, k, src, q, h, to):
            return pltpu.make_async_remote_copy(src_ref=src, dst_ref=_gather_dst(BIG[t], dsts[t], q, h),
                                                send_sem=send_sems.at[6 * t + k], recv_sem=recv_sems.at[6 * t + k],
                                                device_id=to, device_id_type=MESH)

        def tiny(k, q, to):
            return pltpu.make_async_remote_copy(src_ref=small_ref, dst_ref=s_ref.at[q], send_sem=send_sems.at[6 * n + k],
                                                recv_sem=recv_sems.at[6 * n + k], device_id=to, device_id_type=MESH)

        local = [pltpu.make_async_copy(small_ref, s_ref.at[myq], local_sems.at[2 * n])]
        for t in range(n):
            for h in (0, 1):
                local.append(pltpu.make_async_copy(_gather_src(BIG[t], srcs[t], h), _gather_dst(BIG[t], dsts[t], myq, h),
                                                   local_sems.at[2 * t + h]))
        for cp in local:
            cp.start()
        first = []
        for j, chip in enumerate(chips):
            first += [big(t, j, _gather_src(BIG[t], srcs[t], c), myq, c, (*chip, c)) for t in range(n)]
            first.append(tiny(j, myq, (*chip, c)))
        for cp in first:
            cp.start()
        passed = []
        for j, (cx, cy) in enumerate(chips):
            q = 2 * cx + cy
            for t in range(n):
                src = _gather_src(BIG[t], srcs[t], c)
                big(t, j, src, q, c, sibling).wait_recv()
                fwd = big(t, 3 + j, _gather_dst(BIG[t], dsts[t], q, c), q, c, sibling)
                fwd.start()
                passed.append(fwd)
        for j, (cx, cy) in enumerate(chips):
            q = 2 * cx + cy
            for t in range(n):
                big(t, 3 + j, _gather_src(BIG[t], srcs[t], c), q, 1 - c, sibling).wait_recv()
            tiny(j, q, sibling).wait_recv()
        for cp in first + passed:
            cp.wait_send()
        for cp in local:
            cp.wait()

    outs = pl.pallas_call(
        body, name="all_gather_weights",
        in_specs=[HBM_SPEC] * (n + 1), out_specs=[HBM_SPEC] * (n + 1),
        out_shape=[jax.ShapeDtypeStruct(_full_shape(name), BF16) for name in BIG]
        + [jax.ShapeDtypeStruct((N_CHIPS, r, 128), F32)],
        scratch_shapes=[pltpu.SemaphoreType.DMA((6 * n + 3,)), pltpu.SemaphoreType.DMA((6 * n + 3,)),
                        pltpu.SemaphoreType.DMA((2 * n + 1,))],
    )(*[shards[name] for name in BIG], small)
    return dict(zip(BIG, outs[:n])), outs[n]


def _small_all_reduce(v):
    r = v.shape[0]

    def body(v_ref, o_ref, buf_ref, send_sems, recv_sems):
        x, y, c, _ = _place()
        me = 4 * x + 2 * y + c
        buf_ref[me] = v_ref[...]
        copies = []
        for k in range(1, 8):
            fx, fy, fc = (k >> 2) & 1, (k >> 1) & 1, k & 1
            to = (x ^ fx, y ^ fy, c ^ fc)
            cp = pltpu.make_async_remote_copy(src_ref=v_ref, dst_ref=buf_ref.at[me], send_sem=send_sems.at[k - 1],
                                              recv_sem=recv_sems.at[k - 1], device_id=to, device_id_type=MESH)
            cp.start()
            copies.append(cp)
        for k in range(1, 8):
            fx, fy, fc = (k >> 2) & 1, (k >> 1) & 1, k & 1
            src_dev = 4 * (x ^ fx) + 2 * (y ^ fy) + (c ^ fc)
            pltpu.make_async_remote_copy(src_ref=v_ref, dst_ref=buf_ref.at[src_dev], send_sem=send_sems.at[k - 1],
                                         recv_sem=recv_sems.at[k - 1], device_id=(x, y, c), device_id_type=MESH).wait_recv()
        for cp in copies:
            cp.wait_send()
        tot = buf_ref[0]
        for i in range(1, 8):
            tot = tot + buf_ref[i]
        o_ref[...] = tot

    vm = pl.BlockSpec(memory_space=pltpu.VMEM)
    return pl.pallas_call(
        body, name="small_all_reduce", in_specs=[vm], out_specs=vm,
        out_shape=jax.ShapeDtypeStruct((r, 128), F32),
        scratch_shapes=[pltpu.VMEM((8, r, 128), F32), pltpu.SemaphoreType.DMA((7,)), pltpu.SemaphoreType.DMA((7,))],
    )(v)


def _grad_view(kind, g):
    if kind == "col":
        return g.reshape(2, g.shape[0] // 2, g.shape[1])
    return g.reshape(N_CHIPS, 2, g.shape[0] // (2 * N_CHIPS), g.shape[1])


def _half_of(kind, ref, h):
    return ref.at[h] if kind == "col" else ref.at[:, h]


def _half_shape(kind, view_shape):
    return view_shape[1:] if kind == "col" else (view_shape[0],) + view_shape[2:]


def _piece_of(kind, width, colblock, ref, q):
    if kind == "col":
        return ref.at[:, pl.ds(colblock(q) * width, width)]
    return ref.at[q]


def _piece_shape(kind, width, half_shape):
    return (half_shape[0], width) if kind == "col" else half_shape[1:]


def _pair_exchange(views, kinds):
    n = len(views)

    def body(*refs):
        ins, outs = refs[:n], refs[n:2 * n]
        send_sems, recv_sems = refs[2 * n:]
        x, y, c, _ = _place()
        cps = []
        for t in range(n):
            cp = pltpu.make_async_remote_copy(src_ref=_half_of(kinds[t], ins[t], 1 - c), dst_ref=outs[t],
                                              send_sem=send_sems.at[t], recv_sem=recv_sems.at[t],
                                              device_id=(x, y, 1 - c), device_id_type=MESH)
            cp.start()
            cps.append(cp)
        for cp in cps:
            cp.wait()

    return pl.pallas_call(
        body, name="grad_pair_exchange", in_specs=[HBM_SPEC] * n, out_specs=[HBM_SPEC] * n,
        out_shape=[jax.ShapeDtypeStruct(_half_shape(k, v.shape), v.dtype) for k, v in zip(kinds, views)],
        scratch_shapes=[pltpu.SemaphoreType.DMA((n,)), pltpu.SemaphoreType.DMA((n,))],
    )(*views)


def _pair_sum(kind, view, recv, c, name):
    hs = recv.shape
    N = hs[-1]
    rows = hs[-2]
    tr = _pick(rows, (512, 352, 128))
    tn = _pick(N, (1408, 1024, 512))

    def body(c_ref, p_ref, r_ref, s_ref):
        s_ref[...] = (p_ref[...] + r_ref[...]).astype(BF16)

    if kind == "col":
        grid = (rows // tr, N // tn)
        mine = pl.BlockSpec((None, tr, tn), lambda i, j, c_ref: (c_ref[0], i, j))
        blk = pl.BlockSpec((tr, tn), lambda i, j, c_ref: (i, j))
        sem = ("parallel", "parallel")
    else:
        grid = (N_CHIPS, rows // tr, N // tn)
        mine = pl.BlockSpec((None, None, tr, tn), lambda q, i, j, c_ref: (q, c_ref[0], i, j))
        blk = pl.BlockSpec((None, tr, tn), lambda q, i, j, c_ref: (q, i, j))
        sem = ("parallel", "parallel", "parallel")
    return pl.pallas_call(
        body, name=name,
        grid_spec=pltpu.PrefetchScalarGridSpec(num_scalar_prefetch=1, grid=grid, in_specs=[mine, blk], out_specs=blk),
        out_shape=jax.ShapeDtypeStruct(hs, BF16),
        compiler_params=_cparams(sem),
    )(c.reshape(1).astype(jnp.int32), view, recv)


def _chip_exchange(sums, kinds, widths, colblocks):
    n = len(sums)

    def body(*refs):
        ins, outs = refs[:n], refs[n:2 * n]
        send_sems, recv_sems = refs[2 * n:]
        x, y, c, _ = _place()
        cps = []
        for j, (cx, cy) in enumerate(_other_chips(x, y)):
            for t in range(n):
                cp = pltpu.make_async_remote_copy(
                    src_ref=_piece_of(kinds[t], widths[t], colblocks[t], ins[t], 2 * cx + cy), dst_ref=outs[t].at[j],
                    send_sem=send_sems.at[3 * t + j], recv_sem=recv_sems.at[3 * t + j],
                    device_id=(cx, cy, c), device_id_type=MESH)
                cp.start()
                cps.append(cp)
        for cp in cps:
            cp.wait()

    return pl.pallas_call(
        body, name="grad_chip_exchange", in_specs=[HBM_SPEC] * n, out_specs=[HBM_SPEC] * n,
        out_shape=[jax.ShapeDtypeStruct((3,) + _piece_shape(k, w, s.shape), BF16) for k, w, s in zip(kinds, widths, sums)],
        scratch_shapes=[pltpu.SemaphoreType.DMA((3 * n,)), pltpu.SemaphoreType.DMA((3 * n,))],
    )(*sums)


def _chip_sum(kind, width, s, recv, block_idx, name):
    rows, N = recv.shape[1:]
    tr = _pick(rows, (512, 352, 128))
    tn = _pick(N, (1408, 1024, 768, 512))

    def body(q_ref, s_ref, r_ref, o_ref):
        o_ref[...] = ((s_ref[...].astype(F32) + r_ref[0].astype(F32)) + r_ref[1].astype(F32)) + r_ref[2].astype(F32)

    nj = N // tn
    if kind == "col":
        own = pl.BlockSpec((tr, tn), lambda i, j, q_ref: (i, q_ref[0] * nj + j))
    else:
        own = pl.BlockSpec((None, tr, tn), lambda i, j, q_ref: (q_ref[0], i, j))
    return pl.pallas_call(
        body, name=name,
        grid_spec=pltpu.PrefetchScalarGridSpec(
            num_scalar_prefetch=1, grid=(rows // tr, nj),
            in_specs=[own, pl.BlockSpec((3, tr, tn), lambda i, j, q_ref: (0, i, j))],
            out_specs=pl.BlockSpec((tr, tn), lambda i, j, q_ref: (i, j))),
        out_shape=jax.ShapeDtypeStruct((rows, N), F32),
        compiler_params=_cparams(("parallel", "parallel")),
    )(block_idx, s, recv)


def _share_dst(name, layer, ref, c):
    rows = ref.shape[-2] // 2
    if layer is not None:
        return ref.at[layer, pl.ds(c * rows, rows)]
    if ref.ndim == 3:
        return ref.at[0, pl.ds(c * rows, rows)]
    return ref.at[pl.ds(c * rows, rows)]


def _share_halves(pieces, targets, out_shapes):
    n = len(pieces)
    m = len(out_shapes)

    def body(*refs):
        ins, outs = refs[:n], refs[n:n + m]
        send_sems, recv_sems, local_sems = refs[n + m:]
        x, y, c, _ = _place()
        cps, loc = [], []
        for t, (oi, name, layer) in enumerate(targets):
            cp = pltpu.make_async_remote_copy(src_ref=ins[t], dst_ref=_share_dst(name, layer, outs[oi], c),
                                              send_sem=send_sems.at[t], recv_sem=recv_sems.at[t],
                                              device_id=(x, y, 1 - c), device_id_type=MESH)
            cp.start()
            cps.append(cp)
            lc = pltpu.make_async_copy(ins[t], _share_dst(name, layer, outs[oi], c), local_sems.at[t])
            lc.start()
            loc.append(lc)
        for t, (oi, name, layer) in enumerate(targets):
            cps[t].wait_send()
            pltpu.make_async_remote_copy(src_ref=ins[t], dst_ref=_share_dst(name, layer, outs[oi], 1 - c),
                                         send_sem=send_sems.at[t], recv_sem=recv_sems.at[t],
                                         device_id=(x, y, 1 - c), device_id_type=MESH).wait_recv()
            loc[t].wait()

    return pl.pallas_call(
        body, name="grad_share_halves", in_specs=[HBM_SPEC] * n, out_specs=[HBM_SPEC] * m,
        out_shape=[jax.ShapeDtypeStruct(s, F32) for s in out_shapes],
        scratch_shapes=[pltpu.SemaphoreType.DMA((n,)), pltpu.SemaphoreType.DMA((n,)), pltpu.SemaphoreType.DMA((n,))],
    )(*pieces)


def _adamw(w, g, m, v, name):
    R, W = w.shape
    tr = _pick(R, (512, 352, 256, 32))

    def body(w_ref, g_ref, m_ref, v_ref, d_ref, nm_ref, nv_ref):
        gv = g_ref[...]
        nm = ADAM_B1 * m_ref[...] + (1.0 - ADAM_B1) * gv
        nv = ADAM_B2 * v_ref[...] + (1.0 - ADAM_B2) * (gv * gv)
        m_hat = nm / (1.0 - ADAM_B1 ** ADAM_STEP)
        v_hat = nv / (1.0 - ADAM_B2 ** ADAM_STEP)
        d_ref[...] = -ADAM_LR * (m_hat / (jnp.sqrt(v_hat) + ADAM_EPS) + ADAM_WD * w_ref[...])
        nm_ref[...] = nm
        nv_ref[...] = nv

    blk = pl.BlockSpec((tr, W), lambda i: (i, 0))
    shp = jax.ShapeDtypeStruct((R, W), F32)
    return pl.pallas_call(
        body, name=name, grid=(R // tr,), in_specs=[blk] * 4, out_specs=[blk] * 3, out_shape=[shp] * 3,
        compiler_params=_cparams(("parallel",)),
    )(w, g, m, v)


SMALL_ROWS = 32


def _pack_small(ln_g, ln_b, sinks):
    rows = jnp.concatenate([ln_g.reshape(-1, 128), ln_b.reshape(-1, 128),
                            jnp.pad(sinks.reshape(1, -1), ((0, 0), (0, 128 - sinks.size)))], axis=0)
    return jnp.pad(rows, ((0, SMALL_ROWS - rows.shape[0]), (0, 0)))


def _unpack_small(s, ln_shape, sink_shape):
    n = ln_shape[0] * ln_shape[1] * ln_shape[2] // 128
    return s[:n].reshape(ln_shape), s[n:2 * n].reshape(ln_shape), s[2 * n, :sink_shape[1]].reshape(sink_shape)


def _ffn_fwd(xin, w_in, w_out, gain, bias, tag):
    u, h = _ffn_in(xin, w_in, "ffn_in_" + tag)
    y, yb, z = _mm_ln(h, w_out, xin, gain, bias, 0.5, "ffn_out_ln_" + tag)
    return y, yb, dict(u=u, h=h, z=z, xin=xin)


def _ffn_bwd(dy, saved, w_in, w_out, gain, xin_b, tag):
    dz, dzc, gg, gb = _ln_bwd(saved["z"], dy, gain, 0.5, "ln_bwd_" + tag)
    du = _ffn_bwd_h(dzc, w_out, saved["u"], "ffn_bwd_h_" + tag)
    d_w_out = _mm_tn(saved["h"], dzc, "ffn_dwout_" + tag)
    d_w_in = _mm_tn(xin_b, du, "ffn_dwin_" + tag)
    dx = _mm_nt(du, w_in, "ffn_dx_" + tag, add=dz, add_scale=ALPHA)
    return dx, d_w_in, d_w_out, gg, gb


def kernel(x, ffn1_w_in, ffn1_w_out, ffn2_w_in, ffn2_w_out, ln_g, ln_b, a_w_qkv, a_w_o, kv_w, b_w_q, b_sinks, b_w_o, loss_target, m_ffn1_w_in, m_ffn1_w_out, m_ffn2_w_in, m_ffn2_w_out, m_ln_g, m_ln_b, m_a_w_qkv, m_a_w_o, m_kv_w, m_b_w_q, m_b_sinks, m_b_w_o, v_ffn1_w_in, v_ffn1_w_out, v_ffn2_w_in, v_ffn2_w_out, v_ln_g, v_ln_b, v_a_w_qkv, v_a_w_o, v_kv_w, v_b_w_q, v_b_sinks, v_b_w_o):
    ws = dict(ffn1_w_in=ffn1_w_in, ffn1_w_out=ffn1_w_out, ffn2_w_in=ffn2_w_in, ffn2_w_out=ffn2_w_out, a_w_qkv=a_w_qkv,
              a_w_o=a_w_o, kv_w=kv_w, b_w_q=b_w_q, b_w_o=b_w_o)
    ms = dict(ffn1_w_in=m_ffn1_w_in, ffn1_w_out=m_ffn1_w_out, ffn2_w_in=m_ffn2_w_in, ffn2_w_out=m_ffn2_w_out,
              a_w_qkv=m_a_w_qkv, a_w_o=m_a_w_o, kv_w=m_kv_w, b_w_q=m_b_w_q, b_w_o=m_b_w_o)
    vs = dict(ffn1_w_in=v_ffn1_w_in, ffn1_w_out=v_ffn1_w_out, ffn2_w_in=v_ffn2_w_in, ffn2_w_out=v_ffn2_w_out,
              a_w_qkv=v_a_w_qkv, a_w_o=v_a_w_o, kv_w=v_kv_w, b_w_q=v_b_w_q, b_w_o=v_b_w_o)
    _, _, c_idx, myq = _place()
    xs = x[0]
    target = loss_target[0]

    W, small = _all_gather({n: ws[n].astype(BF16) for n in BIG}, _pack_small(ln_g, ln_b, b_sinks))
    for n in ("a_w_o", "kv_w", "b_w_q", "b_w_o"):
        W[n] = W[n].reshape(D_MODEL, W[n].shape[-1])
    n_ln = ln_g.size // 128
    lg = jnp.concatenate([small[q, :n_ln].reshape(DEPTH, 3, 1, -1) for q in range(N_CHIPS)], axis=-1)
    lb = jnp.concatenate([small[q, n_ln:2 * n_ln].reshape(DEPTH, 3, 1, -1) for q in range(N_CHIPS)], axis=-1)
    sq, grad_x, gr, gg, gb, dsink_part = _local_step(xs, target, W, lg, lb, b_sinks.reshape(N_HEADS))

    loss_row = jnp.pad(jnp.sum(sq).reshape(1, 1), ((0, 0), (0, 127)))
    dsinks = jnp.pad(dsink_part[:, 0, :].reshape(N_SLABS, 2, HEAD_DIM)[:, :, 0].reshape(1, N_HEADS), ((0, 0), (0, 128 - N_HEADS)))
    gg_full = jnp.stack([jnp.stack([jnp.sum(gg[i][j], axis=0) for j in range(3)]) for i in range(DEPTH)])
    gb_full = jnp.stack([jnp.stack([jnp.sum(gb[i][j], axis=0) for j in range(3)]) for i in range(DEPTH)])
    small_in = jnp.concatenate([loss_row, dsinks, gg_full.reshape(-1, 128), gb_full.reshape(-1, 128)], axis=0)
    small_in = jnp.pad(small_in, ((0, (-small_in.shape[0]) % 8), (0, 0)))
    small_sum = _small_all_reduce(small_in)
    loss = small_sum[0, 0] * (0.5 / D_MODEL)
    grad_sinks = small_sum[1, :N_HEADS].reshape(b_sinks.shape)
    n_full = DEPTH * 3 * D_MODEL // 128
    cols = D_MODEL // N_CHIPS
    grad_ln_g = lax.dynamic_slice_in_dim(small_sum[2:2 + n_full].reshape(DEPTH, 3, D_MODEL), myq * cols, cols, axis=2)
    grad_ln_b = lax.dynamic_slice_in_dim(small_sum[2 + n_full:2 + 2 * n_full].reshape(DEPTH, 3, D_MODEL), myq * cols, cols, axis=2)
    return _reduce_and_update(gr, grad_x, loss, grad_ln_g, grad_ln_b, grad_sinks, ws, ms, vs, c_idx, myq,
                              (ln_g, ln_b, b_sinks), (m_ln_g, m_ln_b, m_b_sinks), (v_ln_g, v_ln_b, v_b_sinks))


def _local_step(xs, target, W, lg, lb, sinks):
    S = xs.shape[0]
    slopes = jnp.asarray(_alibi_slopes(N_HEADS))
    in1 = [(W["ffn1_w_in"], i) for i in range(DEPTH)]
    out1 = [(W["ffn1_w_out"], i) for i in range(DEPTH)]
    in2 = [(W["ffn2_w_in"], i) for i in range(DEPTH)]
    out2 = [(W["ffn2_w_out"], i) for i in range(DEPTH)]

    y1, y1b, s1 = _ffn_fwd(xs, in1[0], out1[0], lg[0, 0], lb[0, 0], "a1")
    qkv_a = _mm_nn(y1b, W["a_w_qkv"], F32, "qkv_a", split=True)
    mix_a, o_a, lse_a = _attn_fwd(qkv_a, slopes, None, PATTERNS_A, "attn_a_fwd")
    y2, y2b, z2 = _mm_ln(mix_a, W["a_w_o"], y1, lg[0, 1], lb[0, 1], 1.0, "attn_a_out_ln")
    y3, y3b, s3 = _ffn_fwd(y2, in2[0], out2[0], lg[0, 2], lb[0, 2], "a2")
    kv = _mm_nn(y3b, W["kv_w"], F32, "kv_proj")
    y4, y4b, s4 = _ffn_fwd(y3, in1[1], out1[1], lg[1, 0], lb[1, 0], "b1")
    q_b = _mm_nn(y4b, W["b_w_q"], F32, "q_b")
    k_sh = kv[:, :N_KV_B * HEAD_DIM].reshape(S, N_KV_B, 1, HEAD_DIM)
    v_sh = kv[:, N_KV_B * HEAD_DIM:].reshape(S, N_KV_B, 1, HEAD_DIM)
    k_exp = jnp.broadcast_to(k_sh, (S, N_KV_B, GROUP_B, HEAD_DIM)).reshape(S, D_MODEL)
    v_exp = jnp.broadcast_to(v_sh, (S, N_KV_B, GROUP_B, HEAD_DIM)).reshape(S, D_MODEL)
    qkv_b = jnp.stack([q_b, k_exp, v_exp])
    mix_b, o_b, lse_b = _attn_fwd(qkv_b, slopes, sinks, PATTERNS_B, "attn_b_fwd")
    y5, y5b, z5 = _mm_ln(mix_b, W["b_w_o"], y4, lg[1, 1], lb[1, 1], 1.0, "attn_b_out_ln")
    y6, _, s6 = _ffn_fwd(y5, in2[1], out2[1], lg[1, 2], lb[1, 2], "b2")

    dy6, sq = _loss_grad(y6, target, "loss_grad")
    gr = {n: None for n in BIG}
    gg = [[None] * 3 for _ in range(DEPTH)]
    gb = [[None] * 3 for _ in range(DEPTH)]

    dy5, d_in2_b, d_out2_b, gg[1][2], gb[1][2] = _ffn_bwd(dy6, s6, in2[1], out2[1], lg[1, 2], y5b, "b2")
    dz5, dz5b, gg[1][1], gb[1][1] = _ln_bwd(z5, dy5, lg[1, 1], 1.0, "ln_bwd_attn_b")
    gr["b_w_o"] = _mm_tn(mix_b, dz5b, "d_b_w_o")
    dmix_b = _mm_nt(dz5b, W["b_w_o"], "d_mix_b")
    dqkv_b, dsink_part = _attn_bwd(qkv_b, dmix_b, o_b, lse_b, slopes, sinks, PATTERNS_B, "attn_b_bwd")
    dq_b, dk_exp, dv_exp = (dqkv_b, 0), dqkv_b[1], dqkv_b[2]
    dkv = jnp.concatenate([dk_exp.reshape(S, N_KV_B, GROUP_B, HEAD_DIM).sum(axis=2).reshape(S, -1),
                           dv_exp.reshape(S, N_KV_B, GROUP_B, HEAD_DIM).sum(axis=2).reshape(S, -1)], axis=1)
    gr["b_w_q"] = _mm_tn(y4b, dq_b, "d_b_w_q")
    dy4 = _mm_nt(dq_b, W["b_w_q"], "d_y4", add=dz5, add_scale=ALPHA)
    dy3, d_in1_b, d_out1_b, gg[1][0], gb[1][0] = _ffn_bwd(dy4, s4, in1[1], out1[1], lg[1, 0], y3b, "b1")
    gr["kv_w"] = _mm_tn(y3b, dkv, "d_kv_w")
    dy3 = _mm_nt(dkv, W["kv_w"], "d_y3_kv", add=dy3, add_scale=1.0)

    dy2, d_in2_a, d_out2_a, gg[0][2], gb[0][2] = _ffn_bwd(dy3, s3, in2[0], out2[0], lg[0, 2], y2b, "a2")
    dz2, dz2b, gg[0][1], gb[0][1] = _ln_bwd(z2, dy2, lg[0, 1], 1.0, "ln_bwd_attn_a")
    gr["a_w_o"] = _mm_tn(mix_a, dz2b, "d_a_w_o")
    dmix_a = _mm_nt(dz2b, W["a_w_o"], "d_mix_a")
    dqkv_a, _ = _attn_bwd(qkv_a, dmix_a, o_a, lse_a, slopes, None, PATTERNS_A, "attn_a_bwd")
    gr["a_w_qkv"] = _mm_tn(y1b, dqkv_a, "d_a_w_qkv", split=True)
    dy1 = _mm_nt(dqkv_a, W["a_w_qkv"], "d_y1", add=dz2, add_scale=ALPHA, split=True)
    grad_x, d_in1_a, d_out1_a, gg[0][0], gb[0][0] = _ffn_bwd(dy1, s1, in1[0], out1[0], lg[0, 0], xs, "a1")
    gr["ffn1_w_in"] = [d_in1_a, d_in1_b]
    gr["ffn1_w_out"] = [d_out1_a, d_out1_b]
    gr["ffn2_w_in"] = [d_in2_a, d_in2_b]
    gr["ffn2_w_out"] = [d_out2_a, d_out2_b]
    return sq, grad_x, gr, gg, gb, dsink_part


def _reduce_and_update(gr, grad_x, loss, grad_ln_g, grad_ln_b, grad_sinks, ws, ms, vs, c_idx, myq,
                       small_w, small_m, small_v):
    ln_g, ln_b, b_sinks = small_w
    m_ln_g, m_ln_b, m_b_sinks = small_m
    v_ln_g, v_ln_b, v_b_sinks = small_v

    items = []
    for oi, name in enumerate(BIG):
        if name.endswith("w_in"):
            items += [(gr[name][l], "col", HALF_FF, _slot, (oi, name, l)) for l in range(DEPTH)]
        elif name.endswith("w_out"):
            items += [(gr[name][l], "row", D_MODEL, None, (oi, name, l)) for l in range(DEPTH)]
        elif name == "a_w_qkv":
            items.append((gr[name], "col", QKV_SHARD, lambda q: q, (oi, name, None)))
        else:
            items.append((gr[name], "row", gr[name].shape[1], None, (oi, name, None)))
    kinds = [it[1] for it in items]
    widths = [it[2] for it in items]
    colblocks = [it[3] for it in items]
    views = [_grad_view(k, it[0]) for k, it in zip(kinds, items)]
    from_sibling = _pair_exchange(views, kinds)
    sums = [_pair_sum(k, v, r, c_idx, "pair_sum_%d" % t) for t, (k, v, r) in enumerate(zip(kinds, views, from_sibling))]
    from_chips = _chip_exchange(sums, kinds, widths, colblocks)
    pieces = []
    for t, (k, w, cb, s, r) in enumerate(zip(kinds, widths, colblocks, sums, from_chips)):
        own = (cb(myq) if k == "col" else myq).reshape(1).astype(jnp.int32)
        pieces.append(_chip_sum(k, w, s, r, own, "chip_sum_%d" % t))
    grad_list = _share_halves(pieces, [it[4] for it in items], [ws[name].shape for name in BIG])
    grads = dict(zip(BIG, grad_list))

    deltas, new_m, new_v = {}, {}, {}
    for name in BIG:
        shp = ws[name].shape
        flat = lambda a: a.reshape(-1, shp[-1])
        d, nm, nv = _adamw(flat(ws[name]), flat(grads[name]), flat(ms[name]), flat(vs[name]), "adamw_" + name)
        deltas[name], new_m[name], new_v[name] = d.reshape(shp), nm.reshape(shp), nv.reshape(shp)
    delta_s, nm_s, nv_s = _adamw(_pack_small(ln_g, ln_b, b_sinks), _pack_small(grad_ln_g, grad_ln_b, grad_sinks),
                                 _pack_small(m_ln_g, m_ln_b, m_b_sinks), _pack_small(v_ln_g, v_ln_b, v_b_sinks), "adamw_small")
    for d, blob in ((grads, None), (deltas, delta_s), (new_m, nm_s), (new_v, nv_s)):
        if blob is None:
            d["ln_g"], d["ln_b"], d["b_sinks"] = grad_ln_g, grad_ln_b, grad_sinks
        else:
            d["ln_g"], d["ln_b"], d["b_sinks"] = _unpack_small(blob, ln_g.shape, b_sinks.shape)

    order = ("ffn1_w_in", "ffn1_w_out", "ffn2_w_in", "ffn2_w_out", "ln_g", "ln_b", "a_w_qkv", "a_w_o", "kv_w", "b_w_q",
             "b_sinks", "b_w_o")
    outs = [loss, grad_x[None]]
    for d in (grads, deltas, new_m, new_v):
        outs += [d[n] for n in order]
    return tuple(outs)
```

```python
import numpy as np
import jax
import jax.numpy as jnp
from jax import lax
from jax.experimental import pallas as pl
from jax.experimental.pallas import tpu as pltpu

F32 = jnp.float32
BF16 = jnp.bfloat16

D_MODEL = 1024
D_FF = 2816
HALF_FF = D_FF // 2
HEAD_DIM = 64
N_HEADS = 16
N_KV_B = 4
GROUP_B = N_HEADS // N_KV_B
DEPTH = 2
ALPHA = (2.0 * DEPTH) ** 0.25
LN_EPS = 1e-5
BLOCK = 128
SLAB = 128
N_SLABS = D_MODEL // SLAB
PATTERNS_A = ((1, 128, 1.0), (4, 128, 4.0), (16, 128, 16.0))
PATTERNS_B = ((1, 127, 1.0),)
NEG = -1e30

ADAM_LR = 0.001
ADAM_B1 = 0.9
ADAM_B2 = 0.999
ADAM_EPS = 1e-08
ADAM_WD = 0.01
ADAM_STEP = 10

N_CHIPS = 4
VMEM_LIMIT = 56 * 1024 * 1024
MESH = pl.DeviceIdType.MESH


def _alibi_slopes(n):
    return np.array([2.0 ** (-8.0 * (h + 1) / n) for h in range(n)], dtype=np.float32)


def _cparams(sem=None, vmem=VMEM_LIMIT):
    return pltpu.CompilerParams(dimension_semantics=sem, vmem_limit_bytes=vmem)


_DIMS = {"nn": ((1,), (0,)), "nt": ((1,), (1,)), "tn": ((0,), (0,))}


def _unlead(x):
    if isinstance(x, tuple):
        return x[0], x[1], x[0].shape[1:]
    return x, None, x.shape


def _bspec(block, imap, lead=None):
    if lead is None:
        return pl.BlockSpec(block, imap)
    return pl.BlockSpec((None,) + tuple(block), lambda *g: (lead,) + tuple(imap(*g)))


def _matmul(a, b, mode, out_dtype, tm, tn, tk, name, add=None, add_scale=1.0, split=False):
    out_spec = pl.BlockSpec((tm, tn), lambda i, j, k: (i, j))
    if mode == "nn":
        a, al, (M, K) = _unlead(a)
        b, bl, (K2, N) = _unlead(b)
        a_spec = _bspec((tm, tk), lambda i, j, k: (i, k), al)
        b_spec = _bspec((tk, tn), lambda i, j, k: (k, j), bl)
        out_struct = jax.ShapeDtypeStruct((M, N), out_dtype)
        if split:
            assert tn == D_MODEL
            out_spec = pl.BlockSpec((None, tm, tn), lambda i, j, k: (j, i, 0))
            out_struct = jax.ShapeDtypeStruct((N // tn, M, tn), out_dtype)
    elif mode == "nt":
        b, bl, (N, K2) = _unlead(b)
        if split:
            assert tk == D_MODEL
            M, K = a.shape[1], a.shape[0] * a.shape[2]
            a_spec = pl.BlockSpec((None, tm, tk), lambda i, j, k: (k, i, 0))
        else:
            a, al, (M, K) = _unlead(a)
            a_spec = _bspec((tm, tk), lambda i, j, k: (i, k), al)
        b_spec = _bspec((tn, tk), lambda i, j, k: (j, k), bl)
        out_struct = jax.ShapeDtypeStruct((M, N), out_dtype)
    else:
        a, al, (K, M) = _unlead(a)
        if split:
            assert tn == D_MODEL
            K2, N = b.shape[1], b.shape[0] * b.shape[2]
            b_spec = pl.BlockSpec((None, tk, tn), lambda i, j, k: (j, k, 0))
        else:
            b, bl, (K2, N) = _unlead(b)
            b_spec = _bspec((tk, tn), lambda i, j, k: (k, j), bl)
        a_spec = _bspec((tk, tm), lambda i, j, k: (k, i), al)
        out_struct = jax.ShapeDtypeStruct((M, N), out_dtype)
    assert K == K2 and M % tm == 0 and N % tn == 0 and K % tk == 0, (a.shape, b.shape, mode, tm, tn, tk)
    nk = K // tk
    dims = (_DIMS[mode], ((), ()))
    has_add = add is not None

    def body(*refs):
        if has_add:
            a_ref, b_ref, add_ref, o_ref, acc_ref = refs
        else:
            a_ref, b_ref, o_ref, acc_ref = refs
        k = pl.program_id(2)
        part = lax.dot_general(a_ref[...].astype(BF16), b_ref[...].astype(BF16), dims, preferred_element_type=F32)

        @pl.when(k == 0)
        def _():
            acc_ref[...] = part

        @pl.when(k > 0)
        def _():
            acc_ref[...] += part

        @pl.when(k == nk - 1)
        def _():
            r = acc_ref[...]
            if has_add:
                r = r + add_scale * add_ref[...]
            o_ref[...] = r.astype(out_dtype)

    in_specs = [a_spec, b_spec]
    args = [a, b]
    if has_add:
        in_specs.append(pl.BlockSpec((tm, tn), lambda i, j, k: (i, j)))
        args.append(add)
    return pl.pallas_call(
        body, name=name, grid=(M // tm, N // tn, nk),
        in_specs=in_specs, out_specs=out_spec, out_shape=out_struct,
        scratch_shapes=[pltpu.VMEM((tm, tn), F32)],
        compiler_params=_cparams(("parallel", "parallel", "arbitrary")),
    )(*args)


def _pick(n, cands):
    for c in cands:
        if n % c == 0:
            return c
    raise ValueError((n, cands))


def _mm_nn(a, b, out_dtype, name, split=False):
    M, K = _unlead(a)[2]
    N = _unlead(b)[2][1]
    return _matmul(a, b, "nn", out_dtype, _pick(M, (1024, 512, 256)), _pick(N, (1024, 512)), _pick(K, (1024, 512)), name,
                   split=split)


def _mm_nt(a, b, name, add=None, add_scale=1.0, split=False):
    M, K = (a.shape[1], D_MODEL) if split else _unlead(a)[2]
    N = _unlead(b)[2][0]
    return _matmul(a, b, "nt", F32, _pick(M, (1024, 512, 256)), _pick(N, (1024, 512)),
                   _pick(K, (1408, 1024, 512)), name, add=add, add_scale=add_scale, split=split)


def _mm_tn(a, b, name, split=False):
    K, M = _unlead(a)[2]
    N = D_MODEL if split else _unlead(b)[2][1]
    return _matmul(a, b, "tn", F32, _pick(M, (1024, 1408, 512)), _pick(N, (1408, 1024, 512)),
                   _pick(K, (1024, 512, 256)), name, split=split)


def _ffn_in(x, w, name):
    S = x.shape[0]
    tm = _pick(S, (512, 256))
    w, wl, _ = _unlead(w)

    def body(x_ref, w_ref, u_ref, h_ref):
        acc = jnp.dot(x_ref[...].astype(BF16), w_ref[...], preferred_element_type=F32)
        g = acc[:, :HALF_FF]
        up = acc[:, HALF_FF:]
        u_ref[...] = acc.astype(BF16)
        h_ref[...] = (g * jax.nn.sigmoid(g) * up).astype(BF16)

    return pl.pallas_call(
        body, name=name, grid=(2, S // tm),
        in_specs=[pl.BlockSpec((tm, D_MODEL), lambda j, i: (i, 0)),
                  _bspec((D_MODEL, D_FF), lambda j, i: (0, j), wl)],
        out_specs=[pl.BlockSpec((tm, D_FF), lambda j, i: (i, j)),
                   pl.BlockSpec((tm, HALF_FF), lambda j, i: (i, j))],
        out_shape=[jax.ShapeDtypeStruct((S, 2 * D_FF), BF16), jax.ShapeDtypeStruct((S, D_FF), BF16)],
        compiler_params=_cparams(("parallel", "parallel")),
    )(x, w)


def _ffn_bwd_h(dzc, w_out, u, name):
    S = dzc.shape[0]
    tm = _pick(S, (512, 256))
    w_out, wl, _ = _unlead(w_out)

    def body(dz_ref, w_ref, u_ref, du_ref):
        dh = lax.dot_general(dz_ref[...], w_ref[...], (((1,), (1,)), ((), ())), preferred_element_type=F32)
        g = u_ref[:, :HALF_FF].astype(F32)
        up = u_ref[:, HALF_FF:].astype(F32)
        sg = jax.nn.sigmoid(g)
        du_ref[:, :HALF_FF] = (dh * up * (sg * (1.0 + g * (1.0 - sg)))).astype(BF16)
        du_ref[:, HALF_FF:] = (dh * (g * sg)).astype(BF16)

    return pl.pallas_call(
        body, name=name, grid=(2, S // tm),
        in_specs=[pl.BlockSpec((tm, D_MODEL), lambda j, i: (i, 0)),
                  _bspec((HALF_FF, D_MODEL), lambda j, i: (j, 0), wl),
                  pl.BlockSpec((tm, D_FF), lambda j, i: (i, j))],
        out_specs=pl.BlockSpec((tm, D_FF), lambda j, i: (i, j)),
        out_shape=jax.ShapeDtypeStruct((S, 2 * D_FF), BF16),
        compiler_params=_cparams(("parallel", "parallel")),
    )(dzc, w_out, u)


def _mm_ln(a, w, resid, gain, bias, c, name):
    S, K = a.shape
    tm = _pick(S, (512, 256))
    tk = _pick(K, (1408, 1024))
    nk = K // tk
    w, wl, _ = _unlead(w)

    def body(a_ref, w_ref, r_ref, g_ref, b_ref, y_ref, yb_ref, z_ref, acc_ref):
        k = pl.program_id(1)
        part = jnp.dot(a_ref[...], w_ref[...], preferred_element_type=F32)

        @pl.when(k == 0)
        def _():
            acc_ref[...] = part

        @pl.when(k > 0)
        def _():
            acc_ref[...] += part

        @pl.when(k == nk - 1)
        def _():
            z = ALPHA * r_ref[...] + c * acc_ref[...]
            mu = jnp.mean(z, axis=-1, keepdims=True)
            zc = z - mu
            var = jnp.mean(zc * zc, axis=-1, keepdims=True)
            y = zc * lax.rsqrt(var + LN_EPS) * g_ref[...] + b_ref[...]
            z_ref[...] = z
            y_ref[...] = y
            yb_ref[...] = y.astype(BF16)

    row = pl.BlockSpec((tm, D_MODEL), lambda i, k: (i, 0))
    vec = pl.BlockSpec((1, D_MODEL), lambda i, k: (0, 0))
    return pl.pallas_call(
        body, name=name, grid=(S // tm, nk),
        in_specs=[pl.BlockSpec((tm, tk), lambda i, k: (i, k)), _bspec((tk, D_MODEL), lambda i, k: (k, 0), wl),
                  row, vec, vec],
        out_specs=[row, row, row],
        out_shape=[jax.ShapeDtypeStruct((S, D_MODEL), F32), jax.ShapeDtypeStruct((S, D_MODEL), BF16),
                   jax.ShapeDtypeStruct((S, D_MODEL), F32)],
        scratch_shapes=[pltpu.VMEM((tm, D_MODEL), F32)],
        compiler_params=_cparams(("parallel", "arbitrary")),
    )(a, w, resid, gain, bias)


def _ln_bwd(z, dy, gain, c, name):
    S = z.shape[0]
    tm = _pick(S, (512, 256))

    def body(z_ref, dy_ref, g_ref, dz_ref, dzc_ref, gg_ref, gb_ref):
        i = pl.program_id(0)
        zv = z_ref[...]
        dyv = dy_ref[...]
        mu = jnp.mean(zv, axis=-1, keepdims=True)
        zc = zv - mu
        var = jnp.mean(zc * zc, axis=-1, keepdims=True)
        rstd = lax.rsqrt(var + LN_EPS)
        xhat = zc * rstd
        dyg = dyv * g_ref[...]
        m1 = jnp.mean(dyg, axis=-1, keepdims=True)
        m2 = jnp.mean(dyg * xhat, axis=-1, keepdims=True)
        dz = rstd * (dyg - m1 - xhat * m2)
        dz_ref[...] = dz
        dzc_ref[...] = (c * dz).astype(BF16)
        pg = jnp.sum((dyv * xhat).reshape(tm // 8, 8, D_MODEL), axis=0)
        pb = jnp.sum(dyv.reshape(tm // 8, 8, D_MODEL), axis=0)

        @pl.when(i == 0)
        def _():
            gg_ref[...] = pg
            gb_ref[...] = pb

        @pl.when(i > 0)
        def _():
            gg_ref[...] += pg
            gb_ref[...] += pb

    row = pl.BlockSpec((tm, D_MODEL), lambda i: (i, 0))
    part = pl.BlockSpec((8, D_MODEL), lambda i: (0, 0))
    return pl.pallas_call(
        body, name=name, grid=(S // tm,),
        in_specs=[row, row, pl.BlockSpec((1, D_MODEL), lambda i: (0, 0))],
        out_specs=[row, row, part, part],
        out_shape=[jax.ShapeDtypeStruct((S, D_MODEL), F32), jax.ShapeDtypeStruct((S, D_MODEL), BF16),
                   jax.ShapeDtypeStruct((8, D_MODEL), F32), jax.ShapeDtypeStruct((8, D_MODEL), F32)],
        compiler_params=_cparams(("arbitrary",)),
    )(z, dy, gain)


def _loss_grad(y, t, name):
    S = y.shape[0]
    tm = _pick(S, (512, 256))

    def body(y_ref, t_ref, dy_ref, sq_ref):
        i = pl.program_id(0)
        e = y_ref[...] - t_ref[...]
        dy_ref[...] = e * (1.0 / D_MODEL)
        ps = jnp.sum((e * e).reshape(tm // 8, 8, D_MODEL), axis=0)

        @pl.when(i == 0)
        def _():
            sq_ref[...] = ps

        @pl.when(i > 0)
        def _():
            sq_ref[...] += ps

    row = pl.BlockSpec((tm, D_MODEL), lambda i: (i, 0))
    return pl.pallas_call(
        body, name=name, grid=(S // tm,),
        in_specs=[row, row], out_specs=[row, pl.BlockSpec((8, D_MODEL), lambda i: (0, 0))],
        out_shape=[jax.ShapeDtypeStruct((S, D_MODEL), F32), jax.ShapeDtypeStruct((8, D_MODEL), F32)],
        compiler_params=_cparams(("arbitrary",)),
    )(y, t)


def _rows(start, d):
    if d == 1:
        return pl.ds(pl.multiple_of(start, BLOCK), BLOCK)
    return pl.ds(start, BLOCK, stride=d)


def _ld(ref, start, d):
    return ref[_rows(start, d), :]


def _ld3(ref, lead, start, d):
    return ref[lead, _rows(start, d), :]


def _st3(ref, lead, start, d, val):
    ref[lead, _rows(start, d), :] = val


def _acc3(ref, lead, start, d, val):
    ref[lead, _rows(start, d), :] = ref[lead, _rows(start, d), :] + val


def _band_consts():
    qi = lax.broadcasted_iota(jnp.int32, (BLOCK, 2 * BLOCK), 0)
    kj = lax.broadcasted_iota(jnp.int32, (BLOCK, 2 * BLOCK), 1)
    return BLOCK + qi - kj, kj


def _scores(q, k2, hm, slope, dsc, valid):
    qm = jnp.where(hm, q, 0.0).astype(BF16)
    s = lax.dot_general(qm, k2, (((1,), (1,)), ((), ())), preferred_element_type=F32) * (HEAD_DIM ** -0.5)
    return qm, jnp.where(valid, s - slope * dsc, NEG)


def _softmax_weights(ls):
    mx = ls[0]
    for l in ls[1:]:
        mx = jnp.maximum(mx, l)
    es = [jnp.exp(l - mx) for l in ls]
    tot = es[0]
    for e in es[1:]:
        tot = tot + e
    inv = 1.0 / tot
    return [e * inv for e in es]


def _attn_fwd(qkv, slopes, sinks, patterns, name):
    S = qkv.shape[1]
    npat = len(patterns)
    has_sink = sinks is not None
    if not has_sink:
        sinks = jnp.zeros((N_HEADS,), F32)
    rows_c = 256

    def body(slopes_ref, sinks_ref, x_ref, mix_ref, o_ref, lse_ref):
        p = pl.program_id(0)
        lo = lax.broadcasted_iota(jnp.int32, (BLOCK, SLAB), 1) < HEAD_DIM
        dist, kj = _band_consts()
        distf = dist.astype(F32)
        for pi, (d, maxd, scale) in enumerate(patterns):
            nb = S // d // BLOCK
            band = (dist >= 0) & (dist <= maxd)
            dsc = distf * scale

            def blk(t, carry, pi=pi, d=d, nb=nb, band=band, dsc=dsc):
                r = t // nb
                n = t - r * nb
                start = r + (d * BLOCK) * n
                prev = jnp.where(n > 0, start - d * BLOCK, start)
                valid = band & (kj + jnp.where(n > 0, BLOCK, 0) >= BLOCK)
                q = _ld3(x_ref, 0, start, d)
                k2 = jnp.concatenate([_ld3(x_ref, 1, prev, d), _ld3(x_ref, 1, start, d)], axis=0).astype(BF16)
                v2 = jnp.concatenate([_ld3(x_ref, 2, prev, d), _ld3(x_ref, 2, start, d)], axis=0).astype(BF16)
                outs, lses = [], []
                for h in (0, 1):
                    hm = lo if h == 0 else jnp.logical_not(lo)
                    _, s = _scores(q, k2, hm, slopes_ref[2 * p + h], dsc, valid)
                    m = jnp.max(s, axis=-1, keepdims=True)
                    if has_sink:
                        sk = sinks_ref[2 * p + h]
                        m = jnp.maximum(m, sk)
                    e = jnp.exp(s - m)
                    den = jnp.sum(e, axis=-1, keepdims=True)
                    if has_sink:
                        den = den + jnp.exp(sk - m)
                    outs.append(jnp.dot((e / den).astype(BF16), v2, preferred_element_type=F32))
                    lses.append(m + jnp.log(den))
                _st3(o_ref, pi, start, d, jnp.where(lo, outs[0], outs[1]))
                _st3(lse_ref, pi, start, d, jnp.where(lo, lses[0], lses[1]))
                return carry

            lax.fori_loop(0, d * nb, blk, 0)

        def comb(ci, carry):
            rows = pl.ds(pl.multiple_of(ci * rows_c, rows_c), rows_c)
            if npat == 1:
                mix_ref[rows, :] = o_ref[0, rows, :].astype(BF16)
            else:
                ws = _softmax_weights([lse_ref[i, rows, :] for i in range(npat)])
                acc = ws[0] * o_ref[0, rows, :]
                for i in range(1, npat):
                    acc = acc + ws[i] * o_ref[i, rows, :]
                mix_ref[rows, :] = acc.astype(BF16)
            return carry

        lax.fori_loop(0, S // rows_c, comb, 0)

    smem = pl.BlockSpec(memory_space=pltpu.SMEM)
    slab3 = pl.BlockSpec((npat, S, SLAB), lambda p: (0, 0, p))
    return pl.pallas_call(
        body, name=name, grid=(N_SLABS,),
        in_specs=[smem, smem, pl.BlockSpec((3, S, SLAB), lambda p: (0, 0, p))],
        out_specs=[pl.BlockSpec((S, SLAB), lambda p: (0, p)), slab3, slab3],
        out_shape=[jax.ShapeDtypeStruct((S, D_MODEL), BF16), jax.ShapeDtypeStruct((npat, S, D_MODEL), F32),
                   jax.ShapeDtypeStruct((npat, S, D_MODEL), F32)],
        compiler_params=_cparams(("arbitrary",)),
    )(slopes, sinks, qkv)


def _attn_bwd(qkv, dout, o, lse, slopes, sinks, patterns, name):
    S = qkv.shape[1]
    npat = len(patterns)
    has_sink = sinks is not None
    if not has_sink:
        sinks = jnp.zeros((N_HEADS,), F32)
    rows_c = 256

    def headsum(x, lo):
        s0 = jnp.sum(jnp.where(lo, x, 0.0), axis=-1, keepdims=True)
        s1 = jnp.sum(jnp.where(lo, 0.0, x), axis=-1, keepdims=True)
        return jnp.where(lo, s0, s1)

    def body(slopes_ref, sinks_ref, x_ref, do_ref, o_ref, lse_ref, dx_ref, dsink_ref, dbar_ref, sacc_ref):
        p = pl.program_id(0)
        lo = lax.broadcasted_iota(jnp.int32, (BLOCK, SLAB), 1) < HEAD_DIM
        lo_c = lax.broadcasted_iota(jnp.int32, (rows_c, SLAB), 1) < HEAD_DIM
        dist, kj = _band_consts()
        distf = dist.astype(F32)

        def prep(ci, carry):
            rows = pl.ds(pl.multiple_of(ci * rows_c, rows_c), rows_c)
            dov = do_ref[rows, :]
            dx_ref[:, rows, :] = jnp.zeros((3, rows_c, SLAB), F32)
            if npat == 1:
                dbar_ref[rows, :] = headsum(dov * o_ref[0, rows, :], lo_c)
            else:
                ws = _softmax_weights([lse_ref[i, rows, :] for i in range(npat)])
                acc = ws[0] * headsum(dov * o_ref[0, rows, :], lo_c)
                for i in range(1, npat):
                    acc = acc + ws[i] * headsum(dov * o_ref[i, rows, :], lo_c)
                dbar_ref[rows, :] = acc
            return carry

        lax.fori_loop(0, S // rows_c, prep, 0)
        sacc_ref[...] = jnp.zeros((BLOCK, SLAB), F32)

        for pi, (d, maxd, scale) in enumerate(patterns):
            nb = S // d // BLOCK
            band = (dist >= 0) & (dist <= maxd)
            dsc = distf * scale

            def blk(t, carry, pi=pi, d=d, nb=nb, band=band, dsc=dsc):
                r = t // nb
                n = t - r * nb
                start = r + (d * BLOCK) * n
                prev = jnp.where(n > 0, start - d * BLOCK, start)
                valid = band & (kj + jnp.where(n > 0, BLOCK, 0) >= BLOCK)
                q = _ld3(x_ref, 0, start, d)
                k2 = jnp.concatenate([_ld3(x_ref, 1, prev, d), _ld3(x_ref, 1, start, d)], axis=0).astype(BF16)
                v2 = jnp.concatenate([_ld3(x_ref, 2, prev, d), _ld3(x_ref, 2, start, d)], axis=0).astype(BF16)
                ls = [_ld3(lse_ref, i, start, d) for i in range(npat)]
                w = _softmax_weights(ls)[pi] if npat > 1 else 1.0
                d_o = w * _ld(do_ref, start, d)
                dl = w * _ld(dbar_ref, start, d)
                dk2 = jnp.zeros((2 * BLOCK, SLAB), F32)
                dv2 = jnp.zeros((2 * BLOCK, SLAB), F32)
                dqs = []
                sk_terms = []
                for h in (0, 1):
                    hm = lo if h == 0 else jnp.logical_not(lo)
                    c0 = h * HEAD_DIM
                    qm, s = _scores(q, k2, hm, slopes_ref[2 * p + h], dsc, valid)
                    lse_h = ls[pi][:, c0:c0 + 1]
                    dl_h = dl[:, c0:c0 + 1]
                    pr = jnp.exp(s - lse_h)
                    dom = jnp.where(hm, d_o, 0.0).astype(BF16)
                    dp = lax.dot_general(dom, v2, (((1,), (1,)), ((), ())), preferred_element_type=F32)
                    ds = (pr * (dp - dl_h) * (HEAD_DIM ** -0.5)).astype(BF16)
                    dqs.append(jnp.dot(ds, k2, preferred_element_type=F32))
                    dk2 = dk2 + lax.dot_general(ds, qm, (((0,), (0,)), ((), ())), preferred_element_type=F32)
                    dv2 = dv2 + lax.dot_general(pr.astype(BF16), dom, (((0,), (0,)), ((), ())), preferred_element_type=F32)
                    if has_sink:
                        sk_terms.append(-jnp.exp(sinks_ref[2 * p + h] - lse_h) * dl_h)
                _acc3(dx_ref, 0, start, d, jnp.where(lo, dqs[0], dqs[1]))
                _acc3(dx_ref, 1, prev, d, dk2[:BLOCK])
                _acc3(dx_ref, 1, start, d, dk2[BLOCK:])
                _acc3(dx_ref, 2, prev, d, dv2[:BLOCK])
                _acc3(dx_ref, 2, start, d, dv2[BLOCK:])
                if has_sink:
                    sacc_ref[...] += jnp.where(lo, sk_terms[0], sk_terms[1])
                return carry

            lax.fori_loop(0, d * nb, blk, 0)

        dsink_ref[...] = jnp.broadcast_to(jnp.sum(sacc_ref[...], axis=0, keepdims=True), (8, SLAB))

    smem = pl.BlockSpec(memory_space=pltpu.SMEM)
    one = pl.Buffered(1)
    slab3 = pl.BlockSpec((npat, S, SLAB), lambda p: (0, 0, p), pipeline_mode=one)
    return pl.pallas_call(
        body, name=name, grid=(N_SLABS,),
        in_specs=[smem, smem, pl.BlockSpec((3, S, SLAB), lambda p: (0, 0, p), pipeline_mode=one),
                  pl.BlockSpec((S, SLAB), lambda p: (0, p), pipeline_mode=one), slab3, slab3],
        out_specs=[pl.BlockSpec((3, S, SLAB), lambda p: (0, 0, p)), pl.BlockSpec((None, 8, SLAB), lambda p: (p, 0, 0))],
        out_shape=[jax.ShapeDtypeStruct((3, S, D_MODEL), F32), jax.ShapeDtypeStruct((N_SLABS, 8, SLAB), F32)],
        scratch_shapes=[pltpu.VMEM((S, SLAB), F32), pltpu.VMEM((BLOCK, SLAB), F32)],
        compiler_params=_cparams(("arbitrary",)),
    )(slopes, sinks, qkv, dout, o, lse)


def _place():
    x, y, c = lax.axis_index("x"), lax.axis_index("y"), lax.axis_index("c")
    return x, y, c, 2 * x + y


def _other_chips(x, y):
    return [(1 - x, y), (x, 1 - y), (1 - x, 1 - y)]


HBM_SPEC = pl.BlockSpec(memory_space=pl.ANY)


def _slot(q):
    return 2 * (q % 2) + q // 2


BIG = ("ffn1_w_in", "ffn1_w_out", "ffn2_w_in", "ffn2_w_out", "a_w_qkv", "a_w_o", "kv_w", "b_w_q", "b_w_o")
QKV_SHARD = 3 * D_MODEL // N_CHIPS
ROW_SHARD = D_MODEL // N_CHIPS


def _full_shape(name):
    if name.endswith("w_in"):
        return (DEPTH, D_MODEL, 2 * D_FF)
    if name.endswith("w_out"):
        return (DEPTH, D_FF, D_MODEL)
    if name == "a_w_qkv":
        return (D_MODEL, 3 * D_MODEL)
    if name == "kv_w":
        return (N_CHIPS, 2, ROW_SHARD // 2, 2 * N_KV_B * HEAD_DIM)
    return (N_CHIPS, 2, ROW_SHARD // 2, D_MODEL)


def _gather_src(name, ref, c):
    if name.endswith("w_in") or name.endswith("w_out"):
        return ref.at[c]
    if name == "a_w_qkv":
        return ref.at[0, pl.ds(c * (D_MODEL // 2), D_MODEL // 2)]
    if name == "kv_w":
        return ref.at[pl.ds(c * (ROW_SHARD // 2), ROW_SHARD // 2)]
    return ref.at[0, pl.ds(c * (ROW_SHARD // 2), ROW_SHARD // 2)]


def _gather_dst(name, ref, q, c):
    if name.endswith("w_in"):
        return ref.at[c, :, pl.ds(_slot(q) * HALF_FF, HALF_FF)]
    if name.endswith("w_out"):
        return ref.at[c, pl.ds(q * (D_FF // N_CHIPS), D_FF // N_CHIPS)]
    if name == "a_w_qkv":
        return ref.at[pl.ds(c * (D_MODEL // 2), D_MODEL // 2), pl.ds(q * QKV_SHARD, QKV_SHARD)]
    return ref.at[q, c]


def _all_gather(shards, small):
    n = len(BIG)
    r = small.shape[0]
    per = 8

    def body(*refs):
        srcs, small_ref = refs[:n], refs[n]
        dsts, s_ref = refs[n + 1:2 * n + 1], refs[2 * n + 1]
        send_sems, recv_sems = refs[2 * n + 2:]
        x, y, c, myq = _place()
        sibling = (x, y, 1 - c)
        chips = _other_chips(x, y)

        def big(t, k, src, q, h, to):
            return pltpu.make_async_remote_copy(src_ref=src, dst_ref=_gather_dst(BIG[t], dsts[t], q, h),
                                                send_sem=send_sems.at[per * t + k], recv_sem=recv_sems.at[per * t + k],
                                                device_id=to, device_id_type=MESH)

        def tiny(k, q, to):
            return pltpu.make_async_remote_copy(src_ref=small_ref, dst_ref=s_ref.at[q], send_sem=send_sems.at[per * n + k],
                                                recv_sem=recv_sems.at[per * n + k], device_id=to, device_id_type=MESH)

        first = []
        for j, chip in enumerate(chips):
            first += [big(t, j, _gather_src(BIG[t], srcs[t], c), myq, c, (*chip, c)) for t in range(n)]
            first.append(tiny(j, myq, (*chip, c)))
        own = [big(t, 6 + h, _gather_src(BIG[t], srcs[t], h), myq, h, sibling) for t in range(n) for h in (0, 1)]
        own.append(tiny(3, myq, sibling))
        for cp in first + own:
            cp.start()
        passed = []
        for j, (cx, cy) in enumerate(chips):
            q = 2 * cx + cy
            for t in range(n):
                src = _gather_src(BIG[t], srcs[t], c)
                big(t, j, src, q, c, sibling).wait_recv()
                fwd = big(t, 3 + j, _gather_dst(BIG[t], dsts[t], q, c), q, c, sibling)
                fwd.start()
                passed.append(fwd)
        for j, (cx, cy) in enumerate(chips):
            q = 2 * cx + cy
            for t in range(n):
                big(t, 3 + j, _gather_src(BIG[t], srcs[t], c), q, 1 - c, sibling).wait_recv()
            tiny(j, q, sibling).wait_recv()
        for cp in own:
            cp.wait_recv()
        for cp in first + passed + own:
            cp.wait_send()

    outs = pl.pallas_call(
        body, name="all_gather_weights",
        in_specs=[HBM_SPEC] * (n + 1), out_specs=[HBM_SPEC] * (n + 1),
        out_shape=[jax.ShapeDtypeStruct(_full_shape(name), BF16) for name in BIG]
        + [jax.ShapeDtypeStruct((N_CHIPS, r, 128), F32)],
        scratch_shapes=[pltpu.SemaphoreType.DMA((per * n + 4,)), pltpu.SemaphoreType.DMA((per * n + 4,))],
    )(*[shards[name] for name in BIG], small)
    return dict(zip(BIG, outs[:n])), outs[n]


def _small_all_reduce(v):
    r = v.shape[0]

    def body(v_ref, o_ref, buf_ref, send_sems, recv_sems):
        x, y, c, _ = _place()
        me = 4 * x + 2 * y + c
        buf_ref[me] = v_ref[...]
        copies = []
        for k in range(1, 8):
            fx, fy, fc = (k >> 2) & 1, (k >> 1) & 1, k & 1
            to = (x ^ fx, y ^ fy, c ^ fc)
            cp = pltpu.make_async_remote_copy(src_ref=v_ref, dst_ref=buf_ref.at[me], send_sem=send_sems.at[k - 1],
                                              recv_sem=recv_sems.at[k - 1], device_id=to, device_id_type=MESH)
            cp.start()
            copies.append(cp)
        for k in range(1, 8):
            fx, fy, fc = (k >> 2) & 1, (k >> 1) & 1, k & 1
            src_dev = 4 * (x ^ fx) + 2 * (y ^ fy) + (c ^ fc)
            pltpu.make_async_remote_copy(src_ref=v_ref, dst_ref=buf_ref.at[src_dev], send_sem=send_sems.at[k - 1],
                                         recv_sem=recv_sems.at[k - 1], device_id=(x, y, c), device_id_type=MESH).wait_recv()
        for cp in copies:
            cp.wait_send()
        tot = buf_ref[0]
        for i in range(1, 8):
            tot = tot + buf_ref[i]
        o_ref[...] = tot

    vm = pl.BlockSpec(memory_space=pltpu.VMEM)
    return pl.pallas_call(
        body, name="small_all_reduce", in_specs=[vm], out_specs=vm,
        out_shape=jax.ShapeDtypeStruct((r, 128), F32),
        scratch_shapes=[pltpu.VMEM((8, r, 128), F32), pltpu.SemaphoreType.DMA((7,)), pltpu.SemaphoreType.DMA((7,))],
    )(v)


def _grad_view(kind, g):
    if kind == "col":
        return g.reshape(2, g.shape[0] // 2, g.shape[1])
    return g.reshape(N_CHIPS, 2, g.shape[0] // (2 * N_CHIPS), g.shape[1])


def _half_of(kind, ref, h):
    return ref.at[h] if kind == "col" else ref.at[:, h]


def _half_shape(kind, view_shape):
    return view_shape[1:] if kind == "col" else (view_shape[0],) + view_shape[2:]


def _piece_of(kind, width, colblock, ref, q):
    if kind == "col":
        return ref.at[:, pl.ds(colblock(q) * width, width)]
    return ref.at[q]


def _piece_shape(kind, width, half_shape):
    return (half_shape[0], width) if kind == "col" else half_shape[1:]


def _pair_exchange(views, kinds):
    n = len(views)

    def body(*refs):
        ins, outs = refs[:n], refs[n:2 * n]
        send_sems, recv_sems = refs[2 * n:]
        x, y, c, _ = _place()
        cps = []
        for t in range(n):
            cp = pltpu.make_async_remote_copy(src_ref=_half_of(kinds[t], ins[t], 1 - c), dst_ref=outs[t],
                                              send_sem=send_sems.at[t], recv_sem=recv_sems.at[t],
                                              device_id=(x, y, 1 - c), device_id_type=MESH)
            cp.start()
            cps.append(cp)
        for cp in cps:
            cp.wait()

    return pl.pallas_call(
        body, name="grad_pair_exchange", in_specs=[HBM_SPEC] * n, out_specs=[HBM_SPEC] * n,
        out_shape=[jax.ShapeDtypeStruct(_half_shape(k, v.shape), v.dtype) for k, v in zip(kinds, views)],
        scratch_shapes=[pltpu.SemaphoreType.DMA((n,)), pltpu.SemaphoreType.DMA((n,))],
    )(*views)


def _pair_sum(kind, view, recv, c, name):
    hs = recv.shape
    N = hs[-1]
    rows = hs[-2]
    tr = _pick(rows, (512, 352, 128))
    tn = _pick(N, (1408, 1024, 512))

    def body(c_ref, p_ref, r_ref, s_ref):
        s_ref[...] = (p_ref[...] + r_ref[...]).astype(BF16)

    if kind == "col":
        grid = (rows // tr, N // tn)
        mine = pl.BlockSpec((None, tr, tn), lambda i, j, c_ref: (c_ref[0], i, j))
        blk = pl.BlockSpec((tr, tn), lambda i, j, c_ref: (i, j))
        sem = ("parallel", "parallel")
    else:
        grid = (N_CHIPS, rows // tr, N // tn)
        mine = pl.BlockSpec((None, None, tr, tn), lambda q, i, j, c_ref: (q, c_ref[0], i, j))
        blk = pl.BlockSpec((None, tr, tn), lambda q, i, j, c_ref: (q, i, j))
        sem = ("parallel", "parallel", "parallel")
    return pl.pallas_call(
        body, name=name,
        grid_spec=pltpu.PrefetchScalarGridSpec(num_scalar_prefetch=1, grid=grid, in_specs=[mine, blk], out_specs=blk),
        out_shape=jax.ShapeDtypeStruct(hs, BF16),
        compiler_params=_cparams(sem),
    )(c.reshape(1).astype(jnp.int32), view, recv)


def _chip_exchange(sums, kinds, widths, colblocks):
    n = len(sums)

    def body(*refs):
        ins, outs = refs[:n], refs[n:2 * n]
        send_sems, recv_sems = refs[2 * n:]
        x, y, c, _ = _place()
        cps = []
        for j, (cx, cy) in enumerate(_other_chips(x, y)):
            for t in range(n):
                cp = pltpu.make_async_remote_copy(
                    src_ref=_piece_of(kinds[t], widths[t], colblocks[t], ins[t], 2 * cx + cy), dst_ref=outs[t].at[j],
                    send_sem=send_sems.at[3 * t + j], recv_sem=recv_sems.at[3 * t + j],
                    device_id=(cx, cy, c), device_id_type=MESH)
                cp.start()
                cps.append(cp)
        for cp in cps:
            cp.wait()

    return pl.pallas_call(
        body, name="grad_chip_exchange", in_specs=[HBM_SPEC] * n, out_specs=[HBM_SPEC] * n,
        out_shape=[jax.ShapeDtypeStruct((3,) + _piece_shape(k, w, s.shape), BF16) for k, w, s in zip(kinds, widths, sums)],
        scratch_shapes=[pltpu.SemaphoreType.DMA((3 * n,)), pltpu.SemaphoreType.DMA((3 * n,))],
    )(*sums)


def _chip_sum(kind, s, recv, block_idx, c, shard_shape, layer, into, name):
    rows, N = recv.shape[1:]
    tr = _pick(rows, (512, 352, 128))
    tn = _pick(N, (1408, 1024, 768, 512))
    ni, nj = rows // tr, N // tn

    def body(q_ref, s_ref, r_ref, *rest):
        o_ref = rest[-1]
        o_ref[...] = ((s_ref[...].astype(F32) + r_ref[0].astype(F32)) + r_ref[1].astype(F32)) + r_ref[2].astype(F32)

    if kind == "col":
        own = pl.BlockSpec((tr, tn), lambda i, j, q_ref: (i, q_ref[0] * nj + j))
    else:
        own = pl.BlockSpec((None, tr, tn), lambda i, j, q_ref: (q_ref[0], i, j))
    if len(shard_shape) == 3:
        lead = 0 if layer is None else layer
        out_spec = pl.BlockSpec((None, tr, tn), lambda i, j, q_ref: (lead, q_ref[1] * ni + i, j))
    else:
        out_spec = pl.BlockSpec((tr, tn), lambda i, j, q_ref: (q_ref[1] * ni + i, j))
    in_specs = [own, pl.BlockSpec((3, tr, tn), lambda i, j, q_ref: (0, i, j))]
    args = [jnp.stack([block_idx, c]).astype(jnp.int32), s, recv]
    aliases = {}
    if into is not None:
        in_specs.append(HBM_SPEC)
        args.append(into)
        aliases = {3: 0}
    return pl.pallas_call(
        body, name=name,
        grid_spec=pltpu.PrefetchScalarGridSpec(num_scalar_prefetch=1, grid=(ni, nj), in_specs=in_specs, out_specs=out_spec),
        out_shape=jax.ShapeDtypeStruct(shard_shape, F32), input_output_aliases=aliases,
        compiler_params=_cparams(("parallel", "parallel")),
    )(*args)


def _half_window(ref, h):
    rows = ref.shape[-2] // 2
    if ref.ndim == 3:
        return ref.at[:, pl.ds(h * rows, rows)]
    return ref.at[pl.ds(h * rows, rows)]


def _share_halves(grads):
    n = len(grads)

    def body(*refs):
        outs = refs[n:2 * n]
        send_sems, recv_sems = refs[2 * n:]
        x, y, c, _ = _place()
        cps = []
        for t in range(n):
            cp = pltpu.make_async_remote_copy(src_ref=_half_window(outs[t], c), dst_ref=_half_window(outs[t], c),
                                              send_sem=send_sems.at[t], recv_sem=recv_sems.at[t],
                                              device_id=(x, y, 1 - c), device_id_type=MESH)
            cp.start()
            cps.append(cp)
        for t in range(n):
            cps[t].wait_send()
            pltpu.make_async_remote_copy(src_ref=_half_window(outs[t], c), dst_ref=_half_window(outs[t], 1 - c),
                                         send_sem=send_sems.at[t], recv_sem=recv_sems.at[t],
                                         device_id=(x, y, 1 - c), device_id_type=MESH).wait_recv()

    return pl.pallas_call(
        body, name="grad_share_halves", in_specs=[HBM_SPEC] * n, out_specs=[HBM_SPEC] * n,
        out_shape=[jax.ShapeDtypeStruct(g.shape, F32) for g in grads],
        input_output_aliases={t: t for t in range(n)},
        scratch_shapes=[pltpu.SemaphoreType.DMA((n,)), pltpu.SemaphoreType.DMA((n,))],
    )(*grads)


def _adamw(w, g, m, v, name):
    R, W = w.shape
    tr = _pick(R, (512, 352, 256, 32))

    def body(w_ref, g_ref, m_ref, v_ref, d_ref, nm_ref, nv_ref):
        gv = g_ref[...]
        nm = ADAM_B1 * m_ref[...] + (1.0 - ADAM_B1) * gv
        nv = ADAM_B2 * v_ref[...] + (1.0 - ADAM_B2) * (gv * gv)
        m_hat = nm / (1.0 - ADAM_B1 ** ADAM_STEP)
        v_hat = nv / (1.0 - ADAM_B2 ** ADAM_STEP)
        d_ref[...] = -ADAM_LR * (m_hat / (jnp.sqrt(v_hat) + ADAM_EPS) + ADAM_WD * w_ref[...])
        nm_ref[...] = nm
        nv_ref[...] = nv

    blk = pl.BlockSpec((tr, W), lambda i: (i, 0))
    shp = jax.ShapeDtypeStruct((R, W), F32)
    return pl.pallas_call(
        body, name=name, grid=(R // tr,), in_specs=[blk] * 4, out_specs=[blk] * 3, out_shape=[shp] * 3,
        compiler_params=_cparams(("parallel",)),
    )(w, g, m, v)


SMALL_ROWS = 32


def _pack_small(ln_g, ln_b, sinks):
    rows = jnp.concatenate([ln_g.reshape(-1, 128), ln_b.reshape(-1, 128),
                            jnp.pad(sinks.reshape(1, -1), ((0, 0), (0, 128 - sinks.size)))], axis=0)
    return jnp.pad(rows, ((0, SMALL_ROWS - rows.shape[0]), (0, 0)))


def _unpack_small(s, ln_shape, sink_shape):
    n = ln_shape[0] * ln_shape[1] * ln_shape[2] // 128
    return s[:n].reshape(ln_shape), s[n:2 * n].reshape(ln_shape), s[2 * n, :sink_shape[1]].reshape(sink_shape)


def _ffn_fwd(xin, w_in, w_out, gain, bias, tag):
    u, h = _ffn_in(xin, w_in, "ffn_in_" + tag)
    y, yb, z = _mm_ln(h, w_out, xin, gain, bias, 0.5, "ffn_out_ln_" + tag)
    return y, yb, dict(u=u, h=h, z=z, xin=xin)


def _ffn_bwd(dy, saved, w_in, w_out, gain, xin_b, tag):
    dz, dzc, gg, gb = _ln_bwd(saved["z"], dy, gain, 0.5, "ln_bwd_" + tag)
    du = _ffn_bwd_h(dzc, w_out, saved["u"], "ffn_bwd_h_" + tag)
    d_w_out = _mm_tn(saved["h"], dzc, "ffn_dwout_" + tag)
    d_w_in = _mm_tn(xin_b, du, "ffn_dwin_" + tag)
    dx = _mm_nt(du, w_in, "ffn_dx_" + tag, add=dz, add_scale=ALPHA)
    return dx, d_w_in, d_w_out, gg, gb


def kernel(x, ffn1_w_in, ffn1_w_out, ffn2_w_in, ffn2_w_out, ln_g, ln_b, a_w_qkv, a_w_o, kv_w, b_w_q, b_sinks, b_w_o, loss_target, m_ffn1_w_in, m_ffn1_w_out, m_ffn2_w_in, m_ffn2_w_out, m_ln_g, m_ln_b, m_a_w_qkv, m_a_w_o, m_kv_w, m_b_w_q, m_b_sinks, m_b_w_o, v_ffn1_w_in, v_ffn1_w_out, v_ffn2_w_in, v_ffn2_w_out, v_ln_g, v_ln_b, v_a_w_qkv, v_a_w_o, v_kv_w, v_b_w_q, v_b_sinks, v_b_w_o):
    ws = dict(ffn1_w_in=ffn1_w_in, ffn1_w_out=ffn1_w_out, ffn2_w_in=ffn2_w_in, ffn2_w_out=ffn2_w_out, a_w_qkv=a_w_qkv,
              a_w_o=a_w_o, kv_w=kv_w, b_w_q=b_w_q, b_w_o=b_w_o)
    ms = dict(ffn1_w_in=m_ffn1_w_in, ffn1_w_out=m_ffn1_w_out, ffn2_w_in=m_ffn2_w_in, ffn2_w_out=m_ffn2_w_out,
              a_w_qkv=m_a_w_qkv, a_w_o=m_a_w_o, kv_w=m_kv_w, b_w_q=m_b_w_q, b_w_o=m_b_w_o)
    vs = dict(ffn1_w_in=v_ffn1_w_in, ffn1_w_out=v_ffn1_w_out, ffn2_w_in=v_ffn2_w_in, ffn2_w_out=v_ffn2_w_out,
              a_w_qkv=v_a_w_qkv, a_w_o=v_a_w_o, kv_w=v_kv_w, b_w_q=v_b_w_q, b_w_o=v_b_w_o)
    _, _, c_idx, myq = _place()
    xs = x[0]
    target = loss_target[0]

    W, small = _all_gather({n: ws[n].astype(BF16) for n in BIG}, _pack_small(ln_g, ln_b, b_sinks))
    for n in ("a_w_o", "kv_w", "b_w_q", "b_w_o"):
        W[n] = W[n].reshape(D_MODEL, W[n].shape[-1])
    n_ln = ln_g.size // 128
    lg = jnp.concatenate([small[q, :n_ln].reshape(DEPTH, 3, 1, -1) for q in range(N_CHIPS)], axis=-1)
    lb = jnp.concatenate([small[q, n_ln:2 * n_ln].reshape(DEPTH, 3, 1, -1) for q in range(N_CHIPS)], axis=-1)
    sq, grad_x, gr, gg, gb, dsink_part = _local_step(xs, target, W, lg, lb, b_sinks.reshape(N_HEADS))

    loss_row = jnp.pad(jnp.sum(sq).reshape(1, 1), ((0, 0), (0, 127)))
    dsinks = jnp.pad(dsink_part[:, 0, :].reshape(N_SLABS, 2, HEAD_DIM)[:, :, 0].reshape(1, N_HEADS), ((0, 0), (0, 128 - N_HEADS)))
    gg_full = jnp.stack([jnp.stack([jnp.sum(gg[i][j], axis=0) for j in range(3)]) for i in range(DEPTH)])
    gb_full = jnp.stack([jnp.stack([jnp.sum(gb[i][j], axis=0) for j in range(3)]) for i in range(DEPTH)])
    small_in = jnp.concatenate([loss_row, dsinks, gg_full.reshape(-1, 128), gb_full.reshape(-1, 128)], axis=0)
    small_in = jnp.pad(small_in, ((0, (-small_in.shape[0]) % 8), (0, 0)))
    small_sum = _small_all_reduce(small_in)
    loss = small_sum[0, 0] * (0.5 / D_MODEL)
    grad_sinks = small_sum[1, :N_HEADS].reshape(b_sinks.shape)
    n_full = DEPTH * 3 * D_MODEL // 128
    cols = D_MODEL // N_CHIPS
    grad_ln_g = lax.dynamic_slice_in_dim(small_sum[2:2 + n_full].reshape(DEPTH, 3, D_MODEL), myq * cols, cols, axis=2)
    grad_ln_b = lax.dynamic_slice_in_dim(small_sum[2 + n_full:2 + 2 * n_full].reshape(DEPTH, 3, D_MODEL), myq * cols, cols, axis=2)
    return _reduce_and_update(gr, grad_x, loss, grad_ln_g, grad_ln_b, grad_sinks, ws, ms, vs, c_idx, myq,
                              (ln_g, ln_b, b_sinks), (m_ln_g, m_ln_b, m_b_sinks), (v_ln_g, v_ln_b, v_b_sinks))


def _local_step(xs, target, W, lg, lb, sinks):
    S = xs.shape[0]
    slopes = jnp.asarray(_alibi_slopes(N_HEADS))
    in1 = [(W["ffn1_w_in"], i) for i in range(DEPTH)]
    out1 = [(W["ffn1_w_out"], i) for i in range(DEPTH)]
    in2 = [(W["ffn2_w_in"], i) for i in range(DEPTH)]
    out2 = [(W["ffn2_w_out"], i) for i in range(DEPTH)]

    y1, y1b, s1 = _ffn_fwd(xs, in1[0], out1[0], lg[0, 0], lb[0, 0], "a1")
    qkv_a = _mm_nn(y1b, W["a_w_qkv"], F32, "qkv_a", split=True)
    mix_a, o_a, lse_a = _attn_fwd(qkv_a, slopes, None, PATTERNS_A, "attn_a_fwd")
    y2, y2b, z2 = _mm_ln(mix_a, W["a_w_o"], y1, lg[0, 1], lb[0, 1], 1.0, "attn_a_out_ln")
    y3, y3b, s3 = _ffn_fwd(y2, in2[0], out2[0], lg[0, 2], lb[0, 2], "a2")
    kv = _mm_nn(y3b, W["kv_w"], F32, "kv_proj")
    y4, y4b, s4 = _ffn_fwd(y3, in1[1], out1[1], lg[1, 0], lb[1, 0], "b1")
    q_b = _mm_nn(y4b, W["b_w_q"], F32, "q_b")
    k_sh = kv[:, :N_KV_B * HEAD_DIM].reshape(S, N_KV_B, 1, HEAD_DIM)
    v_sh = kv[:, N_KV_B * HEAD_DIM:].reshape(S, N_KV_B, 1, HEAD_DIM)
    k_exp = jnp.broadcast_to(k_sh, (S, N_KV_B, GROUP_B, HEAD_DIM)).reshape(S, D_MODEL)
    v_exp = jnp.broadcast_to(v_sh, (S, N_KV_B, GROUP_B, HEAD_DIM)).reshape(S, D_MODEL)
    qkv_b = jnp.stack([q_b, k_exp, v_exp])
    mix_b, o_b, lse_b = _attn_fwd(qkv_b, slopes, sinks, PATTERNS_B, "attn_b_fwd")
    y5, y5b, z5 = _mm_ln(mix_b, W["b_w_o"], y4, lg[1, 1], lb[1, 1], 1.0, "attn_b_out_ln")
    y6, _, s6 = _ffn_fwd(y5, in2[1], out2[1], lg[1, 2], lb[1, 2], "b2")

    dy6, sq = _loss_grad(y6, target, "loss_grad")
    gr = {n: None for n in BIG}
    gg = [[None] * 3 for _ in range(DEPTH)]
    gb = [[None] * 3 for _ in range(DEPTH)]

    dy5, d_in2_b, d_out2_b, gg[1][2], gb[1][2] = _ffn_bwd(dy6, s6, in2[1], out2[1], lg[1, 2], y5b, "b2")
    dz5, dz5b, gg[1][1], gb[1][1] = _ln_bwd(z5, dy5, lg[1, 1], 1.0, "ln_bwd_attn_b")
    gr["b_w_o"] = _mm_tn(mix_b, dz5b, "d_b_w_o")
    dmix_b = _mm_nt(dz5b, W["b_w_o"], "d_mix_b")
    dqkv_b, dsink_part = _attn_bwd(qkv_b, dmix_b, o_b, lse_b, slopes, sinks, PATTERNS_B, "attn_b_bwd")
    dq_b, dk_exp, dv_exp = (dqkv_b, 0), dqkv_b[1], dqkv_b[2]
    dkv = jnp.concatenate([dk_exp.reshape(S, N_KV_B, GROUP_B, HEAD_DIM).sum(axis=2).reshape(S, -1),
                           dv_exp.reshape(S, N_KV_B, GROUP_B, HEAD_DIM).sum(axis=2).reshape(S, -1)], axis=1)
    gr["b_w_q"] = _mm_tn(y4b, dq_b, "d_b_w_q")
    dy4 = _mm_nt(dq_b, W["b_w_q"], "d_y4", add=dz5, add_scale=ALPHA)
    dy3, d_in1_b, d_out1_b, gg[1][0], gb[1][0] = _ffn_bwd(dy4, s4, in1[1], out1[1], lg[1, 0], y3b, "b1")
    gr["kv_w"] = _mm_tn(y3b, dkv, "d_kv_w")
    dy3 = _mm_nt(dkv, W["kv_w"], "d_y3_kv", add=dy3, add_scale=1.0)

    dy2, d_in2_a, d_out2_a, gg[0][2], gb[0][2] = _ffn_bwd(dy3, s3, in2[0], out2[0], lg[0, 2], y2b, "a2")
    dz2, dz2b, gg[0][1], gb[0][1] = _ln_bwd(z2, dy2, lg[0, 1], 1.0, "ln_bwd_attn_a")
    gr["a_w_o"] = _mm_tn(mix_a, dz2b, "d_a_w_o")
    dmix_a = _mm_nt(dz2b, W["a_w_o"], "d_mix_a")
    dqkv_a, _ = _attn_bwd(qkv_a, dmix_a, o_a, lse_a, slopes, None, PATTERNS_A, "attn_a_bwd")
    gr["a_w_qkv"] = _mm_tn(y1b, dqkv_a, "d_a_w_qkv", split=True)
    dy1 = _mm_nt(dqkv_a, W["a_w_qkv"], "d_y1", add=dz2, add_scale=ALPHA, split=True)
    grad_x, d_in1_a, d_out1_a, gg[0][0], gb[0][0] = _ffn_bwd(dy1, s1, in1[0], out1[0], lg[0, 0], xs, "a1")
    gr["ffn1_w_in"] = [d_in1_a, d_in1_b]
    gr["ffn1_w_out"] = [d_out1_a, d_out1_b]
    gr["ffn2_w_in"] = [d_in2_a, d_in2_b]
    gr["ffn2_w_out"] = [d_out2_a, d_out2_b]
    return sq, grad_x, gr, gg, gb, dsink_part


def _reduce_and_update(gr, grad_x, loss, grad_ln_g, grad_ln_b, grad_sinks, ws, ms, vs, c_idx, myq,
                       small_w, small_m, small_v):
    ln_g, ln_b, b_sinks = small_w
    m_ln_g, m_ln_b, m_b_sinks = small_m
    v_ln_g, v_ln_b, v_b_sinks = small_v

    items = []
    for oi, name in enumerate(BIG):
        if name.endswith("w_in"):
            items += [(gr[name][l], "col", HALF_FF, _slot, (oi, name, l)) for l in range(DEPTH)]
        elif name.endswith("w_out"):
            items += [(gr[name][l], "row", D_MODEL, None, (oi, name, l)) for l in range(DEPTH)]
        elif name == "a_w_qkv":
            items.append((gr[name], "col", QKV_SHARD, lambda q: q, (oi, name, None)))
        else:
            items.append((gr[name], "row", gr[name].shape[1], None, (oi, name, None)))
    kinds = [it[1] for it in items]
    widths = [it[2] for it in items]
    colblocks = [it[3] for it in items]
    views = [_grad_view(k, it[0]) for k, it in zip(kinds, items)]
    from_sibling = _pair_exchange(views, kinds)
    sums = [_pair_sum(k, v, r, c_idx, "pair_sum_%d" % t) for t, (k, v, r) in enumerate(zip(kinds, views, from_sibling))]
    from_chips = _chip_exchange(sums, kinds, widths, colblocks)
    half_done = {name: None for name in BIG}
    for t, (k, cb, s, r, it) in enumerate(zip(kinds, colblocks, sums, from_chips, items)):
        _, name, layer = it[4]
        own = cb(myq) if k == "col" else myq
        half_done[name] = _chip_sum(k, s, r, own, c_idx, ws[name].shape, layer, half_done[name], "chip_sum_%d" % t)
    grads = dict(zip(BIG, _share_halves([half_done[name] for name in BIG])))

    deltas, new_m, new_v = {}, {}, {}
    for name in BIG:
        shp = ws[name].shape
        flat = lambda a: a.reshape(-1, shp[-1])
        d, nm, nv = _adamw(flat(ws[name]), flat(grads[name]), flat(ms[name]), flat(vs[name]), "adamw_" + name)
        deltas[name], new_m[name], new_v[name] = d.reshape(shp), nm.reshape(shp), nv.reshape(shp)
    delta_s, nm_s, nv_s = _adamw(_pack_small(ln_g, ln_b, b_sinks), _pack_small(grad_ln_g, grad_ln_b, grad_sinks),
                                 _pack_small(m_ln_g, m_ln_b, m_b_sinks), _pack_small(v_ln_g, v_ln_b, v_b_sinks), "adamw_small")
    for d, blob in ((grads, None), (deltas, delta_s), (new_m, nm_s), (new_v, nv_s)):
        if blob is None:
            d["ln_g"], d["ln_b"], d["b_sinks"] = grad_ln_g, grad_ln_b, grad_sinks
        else:
            d["ln_g"], d["ln_b"], d["b_sinks"] = _unpack_small(blob, ln_g.shape, b_sinks.shape)

    order = ("ffn1_w_in", "ffn1_w_out", "ffn2_w_in", "ffn2_w_out", "ln_g", "ln_b", "a_w_qkv", "a_w_o", "kv_w", "b_w_q",
             "b_sinks", "b_w_o")
    outs = [loss, grad_x[None]]
    for d in (grads, deltas, new_m, new_v):
        outs += [d[n] for n in order]
    return tuple(outs)
```

```python
import numpy as np
import jax
import jax.numpy as jnp
from jax import lax
from jax.experimental import pallas as pl
from jax.experimental.pallas import tpu as pltpu

F32 = jnp.float32
BF16 = jnp.bfloat16

D_MODEL = 1024
D_FF = 2816
HALF_FF = D_FF // 2
HEAD_DIM = 64
N_HEADS = 16
N_KV_B = 4
GROUP_B = N_HEADS // N_KV_B
DEPTH = 2
ALPHA = (2.0 * DEPTH) ** 0.25
LN_EPS = 1e-5
BLOCK = 128
SLAB = 128
N_SLABS = D_MODEL // SLAB
PATTERNS_A = ((1, 128, 1.0), (4, 128, 4.0), (16, 128, 16.0))
PATTERNS_B = ((1, 127, 1.0),)
NEG = -1e30

ADAM_LR = 0.001
ADAM_B1 = 0.9
ADAM_B2 = 0.999
ADAM_EPS = 1e-08
ADAM_WD = 0.01
ADAM_STEP = 10

N_CHIPS = 4
VMEM_LIMIT = 56 * 1024 * 1024
MESH = pl.DeviceIdType.MESH


def _alibi_slopes(n):
    return np.array([2.0 ** (-8.0 * (h + 1) / n) for h in range(n)], dtype=np.float32)


def _cparams(sem=None, vmem=VMEM_LIMIT):
    return pltpu.CompilerParams(dimension_semantics=sem, vmem_limit_bytes=vmem)


_DIMS = {"nn": ((1,), (0,)), "nt": ((1,), (1,)), "tn": ((0,), (0,))}


def _unlead(x):
    if isinstance(x, tuple):
        return x[0], x[1], x[0].shape[1:]
    return x, None, x.shape


def _bspec(block, imap, lead=None):
    if lead is None:
        return pl.BlockSpec(block, imap)
    return pl.BlockSpec((None,) + tuple(block), lambda *g: (lead,) + tuple(imap(*g)))


def _matmul(a, b, mode, out_dtype, tm, tn, tk, name, add=None, add_scale=1.0, split=False):
    out_spec = pl.BlockSpec((tm, tn), lambda i, j, k: (i, j))
    if mode == "nn":
        a, al, (M, K) = _unlead(a)
        b, bl, (K2, N) = _unlead(b)
        a_spec = _bspec((tm, tk), lambda i, j, k: (i, k), al)
        b_spec = _bspec((tk, tn), lambda i, j, k: (k, j), bl)
        out_struct = jax.ShapeDtypeStruct((M, N), out_dtype)
        if split:
            assert tn == D_MODEL
            out_spec = pl.BlockSpec((None, tm, tn), lambda i, j, k: (j, i, 0))
            out_struct = jax.ShapeDtypeStruct((N // tn, M, tn), out_dtype)
    elif mode == "nt":
        b, bl, (N, K2) = _unlead(b)
        if split:
            assert tk == D_MODEL
            M, K = a.shape[1], a.shape[0] * a.shape[2]
            a_spec = pl.BlockSpec((None, tm, tk), lambda i, j, k: (k, i, 0))
        else:
            a, al, (M, K) = _unlead(a)
            a_spec = _bspec((tm, tk), lambda i, j, k: (i, k), al)
        b_spec = _bspec((tn, tk), lambda i, j, k: (j, k), bl)
        out_struct = jax.ShapeDtypeStruct((M, N), out_dtype)
    else:
        a, al, (K, M) = _unlead(a)
        if split:
            assert tn == D_MODEL
            K2, N = b.shape[1], b.shape[0] * b.shape[2]
            b_spec = pl.BlockSpec((None, tk, tn), lambda i, j, k: (j, k, 0))
        else:
            b, bl, (K2, N) = _unlead(b)
            b_spec = _bspec((tk, tn), lambda i, j, k: (k, j), bl)
        a_spec = _bspec((tk, tm), lambda i, j, k: (k, i), al)
        out_struct = jax.ShapeDtypeStruct((M, N), out_dtype)
    assert K == K2 and M % tm == 0 and N % tn == 0 and K % tk == 0, (a.shape, b.shape, mode, tm, tn, tk)
    nk = K // tk
    dims = (_DIMS[mode], ((), ()))
    has_add = add is not None

    def body(*refs):
        if has_add:
            a_ref, b_ref, add_ref, o_ref, acc_ref = refs
        else:
            a_ref, b_ref, o_ref, acc_ref = refs
        k = pl.program_id(2)
        part = lax.dot_general(a_ref[...].astype(BF16), b_ref[...].astype(BF16), dims, preferred_element_type=F32)

        @pl.when(k == 0)
        def _():
            acc_ref[...] = part

        @pl.when(k > 0)
        def _():
            acc_ref[...] += part

        @pl.when(k == nk - 1)
        def _():
            r = acc_ref[...]
            if has_add:
                r = r + add_scale * add_ref[...]
            o_ref[...] = r.astype(out_dtype)

    in_specs = [a_spec, b_spec]
    args = [a, b]
    if has_add:
        in_specs.append(pl.BlockSpec((tm, tn), lambda i, j, k: (i, j)))
        args.append(add)
    return pl.pallas_call(
        body, name=name, grid=(M // tm, N // tn, nk),
        in_specs=in_specs, out_specs=out_spec, out_shape=out_struct,
        scratch_shapes=[pltpu.VMEM((tm, tn), F32)],
        compiler_params=_cparams(("parallel", "parallel", "arbitrary")),
    )(*args)


def _pick(n, cands):
    for c in cands:
        if n % c == 0:
            return c
    raise ValueError((n, cands))


def _mm_nn(a, b, out_dtype, name, split=False):
    M, K = _unlead(a)[2]
    N = _unlead(b)[2][1]
    return _matmul(a, b, "nn", out_dtype, _pick(M, (1024, 512, 256)), _pick(N, (1024, 512)), _pick(K, (1024, 512)), name,
                   split=split)


def _mm_nt(a, b, name, add=None, add_scale=1.0, split=False):
    M, K = (a.shape[1], D_MODEL) if split else _unlead(a)[2]
    N = _unlead(b)[2][0]
    return _matmul(a, b, "nt", F32, _pick(M, (1024, 512, 256)), _pick(N, (1024, 512)),
                   _pick(K, (1408, 1024, 512)), name, add=add, add_scale=add_scale, split=split)


def _mm_tn(a, b, name, split=False):
    K, M = _unlead(a)[2]
    N = D_MODEL if split else _unlead(b)[2][1]
    return _matmul(a, b, "tn", F32, _pick(M, (1024, 1408, 512)), _pick(N, (1408, 1024, 512)),
                   _pick(K, (1024, 512, 256)), name, split=split)


def _ffn_in(x, w, name):
    S = x.shape[0]
    tm = _pick(S, (512, 256))
    w, wl, _ = _unlead(w)

    def body(x_ref, w_ref, u_ref, h_ref):
        acc = jnp.dot(x_ref[...].astype(BF16), w_ref[...], preferred_element_type=F32)
        g = acc[:, :HALF_FF]
        up = acc[:, HALF_FF:]
        u_ref[...] = acc.astype(BF16)
        h_ref[...] = (g * jax.nn.sigmoid(g) * up).astype(BF16)

    return pl.pallas_call(
        body, name=name, grid=(2, S // tm),
        in_specs=[pl.BlockSpec((tm, D_MODEL), lambda j, i: (i, 0)),
                  _bspec((D_MODEL, D_FF), lambda j, i: (0, j), wl)],
        out_specs=[pl.BlockSpec((tm, D_FF), lambda j, i: (i, j)),
                   pl.BlockSpec((tm, HALF_FF), lambda j, i: (i, j))],
        out_shape=[jax.ShapeDtypeStruct((S, 2 * D_FF), BF16), jax.ShapeDtypeStruct((S, D_FF), BF16)],
        compiler_params=_cparams(("parallel", "parallel")),
    )(x, w)


def _ffn_bwd_h(dzc, w_out, u, name):
    S = dzc.shape[0]
    tm = _pick(S, (512, 256))
    w_out, wl, _ = _unlead(w_out)

    def body(dz_ref, w_ref, u_ref, du_ref):
        dh = lax.dot_general(dz_ref[...], w_ref[...], (((1,), (1,)), ((), ())), preferred_element_type=F32)
        g = u_ref[:, :HALF_FF].astype(F32)
        up = u_ref[:, HALF_FF:].astype(F32)
        sg = jax.nn.sigmoid(g)
        du_ref[:, :HALF_FF] = (dh * up * (sg * (1.0 + g * (1.0 - sg)))).astype(BF16)
        du_ref[:, HALF_FF:] = (dh * (g * sg)).astype(BF16)

    return pl.pallas_call(
        body, name=name, grid=(2, S // tm),
        in_specs=[pl.BlockSpec((tm, D_MODEL), lambda j, i: (i, 0)),
                  _bspec((HALF_FF, D_MODEL), lambda j, i: (j, 0), wl),
                  pl.BlockSpec((tm, D_FF), lambda j, i: (i, j))],
        out_specs=pl.BlockSpec((tm, D_FF), lambda j, i: (i, j)),
        out_shape=jax.ShapeDtypeStruct((S, 2 * D_FF), BF16),
        compiler_params=_cparams(("parallel", "parallel")),
    )(dzc, w_out, u)


def _mm_ln(a, w, resid, gain, bias, c, name):
    S, K = a.shape
    tm = _pick(S, (512, 256))
    tk = _pick(K, (1408, 1024))
    nk = K // tk
    w, wl, _ = _unlead(w)

    def body(a_ref, w_ref, r_ref, g_ref, b_ref, y_ref, yb_ref, z_ref, acc_ref):
        k = pl.program_id(1)
        part = jnp.dot(a_ref[...], w_ref[...], preferred_element_type=F32)

        @pl.when(k == 0)
        def _():
            acc_ref[...] = part

        @pl.when(k > 0)
        def _():
            acc_ref[...] += part

        @pl.when(k == nk - 1)
        def _():
            z = ALPHA * r_ref[...] + c * acc_ref[...]
            mu = jnp.mean(z, axis=-1, keepdims=True)
            zc = z - mu
            var = jnp.mean(zc * zc, axis=-1, keepdims=True)
            y = zc * lax.rsqrt(var + LN_EPS) * g_ref[...] + b_ref[...]
            z_ref[...] = z
            y_ref[...] = y
            yb_ref[...] = y.astype(BF16)

    row = pl.BlockSpec((tm, D_MODEL), lambda i, k: (i, 0))
    vec = pl.BlockSpec((1, D_MODEL), lambda i, k: (0, 0))
    return pl.pallas_call(
        body, name=name, grid=(S // tm, nk),
        in_specs=[pl.BlockSpec((tm, tk), lambda i, k: (i, k)), _bspec((tk, D_MODEL), lambda i, k: (k, 0), wl),
                  row, vec, vec],
        out_specs=[row, row, row],
        out_shape=[jax.ShapeDtypeStruct((S, D_MODEL), F32), jax.ShapeDtypeStruct((S, D_MODEL), BF16),
                   jax.ShapeDtypeStruct((S, D_MODEL), F32)],
        scratch_shapes=[pltpu.VMEM((tm, D_MODEL), F32)],
        compiler_params=_cparams(("parallel", "arbitrary")),
    )(a, w, resid, gain, bias)


def _ln_bwd(z, dy, gain, c, name):
    S = z.shape[0]
    tm = _pick(S, (512, 256))

    def body(z_ref, dy_ref, g_ref, dz_ref, dzc_ref, gg_ref, gb_ref):
        i = pl.program_id(0)
        zv = z_ref[...]
        dyv = dy_ref[...]
        mu = jnp.mean(zv, axis=-1, keepdims=True)
        zc = zv - mu
        var = jnp.mean(zc * zc, axis=-1, keepdims=True)
        rstd = lax.rsqrt(var + LN_EPS)
        xhat = zc * rstd
        dyg = dyv * g_ref[...]
        m1 = jnp.mean(dyg, axis=-1, keepdims=True)
        m2 = jnp.mean(dyg * xhat, axis=-1, keepdims=True)
        dz = rstd * (dyg - m1 - xhat * m2)
        dz_ref[...] = dz
        dzc_ref[...] = (c * dz).astype(BF16)
        pg = jnp.sum((dyv * xhat).reshape(tm // 8, 8, D_MODEL), axis=0)
        pb = jnp.sum(dyv.reshape(tm // 8, 8, D_MODEL), axis=0)

        @pl.when(i == 0)
        def _():
            gg_ref[...] = pg
            gb_ref[...] = pb

        @pl.when(i > 0)
        def _():
            gg_ref[...] += pg
            gb_ref[...] += pb

    row = pl.BlockSpec((tm, D_MODEL), lambda i: (i, 0))
    part = pl.BlockSpec((8, D_MODEL), lambda i: (0, 0))
    return pl.pallas_call(
        body, name=name, grid=(S // tm,),
        in_specs=[row, row, pl.BlockSpec((1, D_MODEL), lambda i: (0, 0))],
        out_specs=[row, row, part, part],
        out_shape=[jax.ShapeDtypeStruct((S, D_MODEL), F32), jax.ShapeDtypeStruct((S, D_MODEL), BF16),
                   jax.ShapeDtypeStruct((8, D_MODEL), F32), jax.ShapeDtypeStruct((8, D_MODEL), F32)],
        compiler_params=_cparams(("arbitrary",)),
    )(z, dy, gain)


def _loss_grad(y, t, name):
    S = y.shape[0]
    tm = _pick(S, (512, 256))

    def body(y_ref, t_ref, dy_ref, sq_ref):
        i = pl.program_id(0)
        e = y_ref[...] - t_ref[...]
        dy_ref[...] = e * (1.0 / D_MODEL)
        ps = jnp.sum((e * e).reshape(tm // 8, 8, D_MODEL), axis=0)

        @pl.when(i == 0)
        def _():
            sq_ref[...] = ps

        @pl.when(i > 0)
        def _():
            sq_ref[...] += ps

    row = pl.BlockSpec((tm, D_MODEL), lambda i: (i, 0))
    return pl.pallas_call(
        body, name=name, grid=(S // tm,),
        in_specs=[row, row], out_specs=[row, pl.BlockSpec((8, D_MODEL), lambda i: (0, 0))],
        out_shape=[jax.ShapeDtypeStruct((S, D_MODEL), F32), jax.ShapeDtypeStruct((8, D_MODEL), F32)],
        compiler_params=_cparams(("arbitrary",)),
    )(y, t)


def _rows(start, d):
    if d == 1:
        return pl.ds(pl.multiple_of(start, BLOCK), BLOCK)
    return pl.ds(start, BLOCK, stride=d)


def _ld(ref, start, d):
    return ref[_rows(start, d), :]


def _ld3(ref, lead, start, d):
    return ref[lead, _rows(start, d), :]


def _st3(ref, lead, start, d, val):
    ref[lead, _rows(start, d), :] = val


def _acc3(ref, lead, start, d, val):
    ref[lead, _rows(start, d), :] = ref[lead, _rows(start, d), :] + val


def _band_consts():
    qi = lax.broadcasted_iota(jnp.int32, (BLOCK, 2 * BLOCK), 0)
    kj = lax.broadcasted_iota(jnp.int32, (BLOCK, 2 * BLOCK), 1)
    return BLOCK + qi - kj, kj


def _scores(q, k2, hm, slope, dsc, valid):
    qm = jnp.where(hm, q, 0.0).astype(BF16)
    s = lax.dot_general(qm, k2, (((1,), (1,)), ((), ())), preferred_element_type=F32) * (HEAD_DIM ** -0.5)
    return qm, jnp.where(valid, s - slope * dsc, NEG)


def _softmax_weights(ls):
    mx = ls[0]
    for l in ls[1:]:
        mx = jnp.maximum(mx, l)
    es = [jnp.exp(l - mx) for l in ls]
    tot = es[0]
    for e in es[1:]:
        tot = tot + e
    inv = 1.0 / tot
    return [e * inv for e in es]


def _attn_fwd(qkv, slopes, sinks, patterns, name):
    S = qkv.shape[1]
    npat = len(patterns)
    has_sink = sinks is not None
    if not has_sink:
        sinks = jnp.zeros((N_HEADS,), F32)
    rows_c = 256

    def body(slopes_ref, sinks_ref, x_ref, mix_ref, o_ref, lse_ref):
        p = pl.program_id(0)
        lo = lax.broadcasted_iota(jnp.int32, (BLOCK, SLAB), 1) < HEAD_DIM
        dist, kj = _band_consts()
        distf = dist.astype(F32)
        for pi, (d, maxd, scale) in enumerate(patterns):
            nb = S // d // BLOCK
            band = (dist >= 0) & (dist <= maxd)
            dsc = distf * scale

            def blk(t, carry, pi=pi, d=d, nb=nb, band=band, dsc=dsc):
                r = t // nb
                n = t - r * nb
                start = r + (d * BLOCK) * n
                prev = jnp.where(n > 0, start - d * BLOCK, start)
                valid = band & (kj + jnp.where(n > 0, BLOCK, 0) >= BLOCK)
                q = _ld3(x_ref, 0, start, d)
                k2 = jnp.concatenate([_ld3(x_ref, 1, prev, d), _ld3(x_ref, 1, start, d)], axis=0).astype(BF16)
                v2 = jnp.concatenate([_ld3(x_ref, 2, prev, d), _ld3(x_ref, 2, start, d)], axis=0).astype(BF16)
                outs, lses = [], []
                for h in (0, 1):
                    hm = lo if h == 0 else jnp.logical_not(lo)
                    _, s = _scores(q, k2, hm, slopes_ref[2 * p + h], dsc, valid)
                    m = jnp.max(s, axis=-1, keepdims=True)
                    if has_sink:
                        sk = sinks_ref[2 * p + h]
                        m = jnp.maximum(m, sk)
                    e = jnp.exp(s - m)
                    den = jnp.sum(e, axis=-1, keepdims=True)
                    if has_sink:
                        den = den + jnp.exp(sk - m)
                    outs.append(jnp.dot((e / den).astype(BF16), v2, preferred_element_type=F32))
                    lses.append(m + jnp.log(den))
                _st3(o_ref, pi, start, d, jnp.where(lo, outs[0], outs[1]))
                _st3(lse_ref, pi, start, d, jnp.where(lo, lses[0], lses[1]))
                return carry

            lax.fori_loop(0, d * nb, blk, 0, unroll=2)

        def comb(ci, carry):
            rows = pl.ds(pl.multiple_of(ci * rows_c, rows_c), rows_c)
            if npat == 1:
                mix_ref[rows, :] = o_ref[0, rows, :].astype(BF16)
            else:
                ws = _softmax_weights([lse_ref[i, rows, :] for i in range(npat)])
                acc = ws[0] * o_ref[0, rows, :]
                for i in range(1, npat):
                    acc = acc + ws[i] * o_ref[i, rows, :]
                mix_ref[rows, :] = acc.astype(BF16)
            return carry

        lax.fori_loop(0, S // rows_c, comb, 0)

    smem = pl.BlockSpec(memory_space=pltpu.SMEM)
    slab3 = pl.BlockSpec((npat, S, SLAB), lambda p: (0, 0, p))
    return pl.pallas_call(
        body, name=name, grid=(N_SLABS,),
        in_specs=[smem, smem, pl.BlockSpec((3, S, SLAB), lambda p: (0, 0, p))],
        out_specs=[pl.BlockSpec((S, SLAB), lambda p: (0, p)), slab3, slab3],
        out_shape=[jax.ShapeDtypeStruct((S, D_MODEL), BF16), jax.ShapeDtypeStruct((npat, S, D_MODEL), F32),
                   jax.ShapeDtypeStruct((npat, S, D_MODEL), F32)],
        compiler_params=_cparams(("arbitrary",)),
    )(slopes, sinks, qkv)


def _attn_bwd(qkv, dout, o, lse, slopes, sinks, patterns, name):
    S = qkv.shape[1]
    npat = len(patterns)
    has_sink = sinks is not None
    if not has_sink:
        sinks = jnp.zeros((N_HEADS,), F32)
    rows_c = 256

    def headsum(x, lo):
        s0 = jnp.sum(jnp.where(lo, x, 0.0), axis=-1, keepdims=True)
        s1 = jnp.sum(jnp.where(lo, 0.0, x), axis=-1, keepdims=True)
        return jnp.where(lo, s0, s1)

    def body(slopes_ref, sinks_ref, x_ref, do_ref, o_ref, lse_ref, dx_ref, dsink_ref, dbar_ref, sacc_ref):
        p = pl.program_id(0)
        lo = lax.broadcasted_iota(jnp.int32, (BLOCK, SLAB), 1) < HEAD_DIM
        lo_c = lax.broadcasted_iota(jnp.int32, (rows_c, SLAB), 1) < HEAD_DIM
        dist, kj = _band_consts()
        distf = dist.astype(F32)

        def prep(ci, carry):
            rows = pl.ds(pl.multiple_of(ci * rows_c, rows_c), rows_c)
            dov = do_ref[rows, :]
            dx_ref[:, rows, :] = jnp.zeros((3, rows_c, SLAB), F32)
            if npat == 1:
                dbar_ref[rows, :] = headsum(dov * o_ref[0, rows, :], lo_c)
            else:
                ws = _softmax_weights([lse_ref[i, rows, :] for i in range(npat)])
                acc = ws[0] * headsum(dov * o_ref[0, rows, :], lo_c)
                for i in range(1, npat):
                    acc = acc + ws[i] * headsum(dov * o_ref[i, rows, :], lo_c)
                dbar_ref[rows, :] = acc
            return carry

        lax.fori_loop(0, S // rows_c, prep, 0)
        sacc_ref[...] = jnp.zeros((BLOCK, SLAB), F32)

        for pi, (d, maxd, scale) in enumerate(patterns):
            nb = S // d // BLOCK
            band = (dist >= 0) & (dist <= maxd)
            dsc = distf * scale

            def blk(t, carry, pi=pi, d=d, nb=nb, band=band, dsc=dsc):
                r = t // nb
                n = t - r * nb
                start = r + (d * BLOCK) * n
                prev = jnp.where(n > 0, start - d * BLOCK, start)
                valid = band & (kj + jnp.where(n > 0, BLOCK, 0) >= BLOCK)
                q = _ld3(x_ref, 0, start, d)
                k2 = jnp.concatenate([_ld3(x_ref, 1, prev, d), _ld3(x_ref, 1, start, d)], axis=0).astype(BF16)
                v2 = jnp.concatenate([_ld3(x_ref, 2, prev, d), _ld3(x_ref, 2, start, d)], axis=0).astype(BF16)
                ls = [_ld3(lse_ref, i, start, d) for i in range(npat)]
                w = _softmax_weights(ls)[pi] if npat > 1 else 1.0
                d_o = w * _ld(do_ref, start, d)
                dl = w * _ld(dbar_ref, start, d)
                dk2 = jnp.zeros((2 * BLOCK, SLAB), F32)
                dv2 = jnp.zeros((2 * BLOCK, SLAB), F32)
                dqs = []
                sk_terms = []
                for h in (0, 1):
                    hm = lo if h == 0 else jnp.logical_not(lo)
                    c0 = h * HEAD_DIM
                    qm, s = _scores(q, k2, hm, slopes_ref[2 * p + h], dsc, valid)
                    lse_h = ls[pi][:, c0:c0 + 1]
                    dl_h = dl[:, c0:c0 + 1]
                    pr = jnp.exp(s - lse_h)
                    dom = jnp.where(hm, d_o, 0.0).astype(BF16)
                    dp = lax.dot_general(dom, v2, (((1,), (1,)), ((), ())), preferred_element_type=F32)
                    ds = (pr * (dp - dl_h) * (HEAD_DIM ** -0.5)).astype(BF16)
                    dqs.append(jnp.dot(ds, k2, preferred_element_type=F32))
                    dk2 = dk2 + lax.dot_general(ds, qm, (((0,), (0,)), ((), ())), preferred_element_type=F32)
                    dv2 = dv2 + lax.dot_general(pr.astype(BF16), dom, (((0,), (0,)), ((), ())), preferred_element_type=F32)
                    if has_sink:
                        sk_terms.append(-jnp.exp(sinks_ref[2 * p + h] - lse_h) * dl_h)
                _acc3(dx_ref, 0, start, d, jnp.where(lo, dqs[0], dqs[1]))
                _acc3(dx_ref, 1, prev, d, dk2[:BLOCK])
                _acc3(dx_ref, 1, start, d, dk2[BLOCK:])
                _acc3(dx_ref, 2, prev, d, dv2[:BLOCK])
                _acc3(dx_ref, 2, start, d, dv2[BLOCK:])
                if has_sink:
                    sacc_ref[...] += jnp.where(lo, sk_terms[0], sk_terms[1])
                return carry

            lax.fori_loop(0, d * nb, blk, 0, unroll=2)

        dsink_ref[...] = jnp.broadcast_to(jnp.sum(sacc_ref[...], axis=0, keepdims=True), (8, SLAB))

    smem = pl.BlockSpec(memory_space=pltpu.SMEM)
    one = pl.Buffered(1)
    slab3 = pl.BlockSpec((npat, S, SLAB), lambda p: (0, 0, p), pipeline_mode=one)
    return pl.pallas_call(
        body, name=name, grid=(N_SLABS,),
        in_specs=[smem, smem, pl.BlockSpec((3, S, SLAB), lambda p: (0, 0, p), pipeline_mode=one),
                  pl.BlockSpec((S, SLAB), lambda p: (0, p), pipeline_mode=one), slab3, slab3],
        out_specs=[pl.BlockSpec((3, S, SLAB), lambda p: (0, 0, p)), pl.BlockSpec((None, 8, SLAB), lambda p: (p, 0, 0))],
        out_shape=[jax.ShapeDtypeStruct((3, S, D_MODEL), F32), jax.ShapeDtypeStruct((N_SLABS, 8, SLAB), F32)],
        scratch_shapes=[pltpu.VMEM((S, SLAB), F32), pltpu.VMEM((BLOCK, SLAB), F32)],
        compiler_params=_cparams(("arbitrary",)),
    )(slopes, sinks, qkv, dout, o, lse)


def _place():
    x, y, c = lax.axis_index("x"), lax.axis_index("y"), lax.axis_index("c")
    return x, y, c, 2 * x + y


def _other_chips(x, y):
    return [(1 - x, y), (x, 1 - y), (1 - x, 1 - y)]


HBM_SPEC = pl.BlockSpec(memory_space=pl.ANY)


def _slot(q):
    return 2 * (q % 2) + q // 2


BIG = ("ffn1_w_in", "ffn1_w_out", "ffn2_w_in", "ffn2_w_out", "a_w_qkv", "a_w_o", "kv_w", "b_w_q", "b_w_o")
QKV_SHARD = 3 * D_MODEL // N_CHIPS
ROW_SHARD = D_MODEL // N_CHIPS


def _full_shape(name):
    if name.endswith("w_in"):
        return (DEPTH, D_MODEL, 2 * D_FF)
    if name.endswith("w_out"):
        return (DEPTH, D_FF, D_MODEL)
    if name == "a_w_qkv":
        return (D_MODEL, 3 * D_MODEL)
    if name == "kv_w":
        return (N_CHIPS, 2, ROW_SHARD // 2, 2 * N_KV_B * HEAD_DIM)
    return (N_CHIPS, 2, ROW_SHARD // 2, D_MODEL)


def _gather_src(name, ref, c):
    if name.endswith("w_in") or name.endswith("w_out"):
        return ref.at[c]
    if name == "a_w_qkv":
        return ref.at[0, pl.ds(c * (D_MODEL // 2), D_MODEL // 2)]
    if name == "kv_w":
        return ref.at[pl.ds(c * (ROW_SHARD // 2), ROW_SHARD // 2)]
    return ref.at[0, pl.ds(c * (ROW_SHARD // 2), ROW_SHARD // 2)]


def _gather_dst(name, ref, q, c):
    if name.endswith("w_in"):
        return ref.at[c, :, pl.ds(_slot(q) * HALF_FF, HALF_FF)]
    if name.endswith("w_out"):
        return ref.at[c, pl.ds(q * (D_FF // N_CHIPS), D_FF // N_CHIPS)]
    if name == "a_w_qkv":
        return ref.at[pl.ds(c * (D_MODEL // 2), D_MODEL // 2), pl.ds(q * QKV_SHARD, QKV_SHARD)]
    return ref.at[q, c]


def _all_gather(shards, small):
    n = len(BIG)
    r = small.shape[0]
    per = 8

    def body(*refs):
        srcs, small_ref = refs[:n], refs[n]
        dsts, s_ref = refs[n + 1:2 * n + 1], refs[2 * n + 1]
        send_sems, recv_sems = refs[2 * n + 2:]
        x, y, c, myq = _place()
        sibling = (x, y, 1 - c)
        chips = _other_chips(x, y)

        def big(t, k, src, q, h, to):
            return pltpu.make_async_remote_copy(src_ref=src, dst_ref=_gather_dst(BIG[t], dsts[t], q, h),
                                                send_sem=send_sems.at[per * t + k], recv_sem=recv_sems.at[per * t + k],
                                                device_id=to, device_id_type=MESH)

        def tiny(k, q, to):
            return pltpu.make_async_remote_copy(src_ref=small_ref, dst_ref=s_ref.at[q], send_sem=send_sems.at[per * n + k],
                                                recv_sem=recv_sems.at[per * n + k], device_id=to, device_id_type=MESH)

        first = []
        for j, chip in enumerate(chips):
            first += [big(t, j, _gather_src(BIG[t], srcs[t], c), myq, c, (*chip, c)) for t in range(n)]
            first.append(tiny(j, myq, (*chip, c)))
        own = [big(t, 6 + h, _gather_src(BIG[t], srcs[t], h), myq, h, sibling) for t in range(n) for h in (0, 1)]
        own.append(tiny(3, myq, sibling))
        for cp in first + own:
            cp.start()
        passed = []
        for j, (cx, cy) in enumerate(chips):
            q = 2 * cx + cy
            for t in range(n):
                src = _gather_src(BIG[t], srcs[t], c)
                big(t, j, src, q, c, sibling).wait_recv()
                fwd = big(t, 3 + j, _gather_dst(BIG[t], dsts[t], q, c), q, c, sibling)
                fwd.start()
                passed.append(fwd)
        for j, (cx, cy) in enumerate(chips):
            q = 2 * cx + cy
            for t in range(n):
                big(t, 3 + j, _gather_src(BIG[t], srcs[t], c), q, 1 - c, sibling).wait_recv()
            tiny(j, q, sibling).wait_recv()
        for cp in own:
            cp.wait_recv()
        for cp in first + passed + own:
            cp.wait_send()

    outs = pl.pallas_call(
        body, name="all_gather_weights",
        in_specs=[HBM_SPEC] * (n + 1), out_specs=[HBM_SPEC] * (n + 1),
        out_shape=[jax.ShapeDtypeStruct(_full_shape(name), BF16) for name in BIG]
        + [jax.ShapeDtypeStruct((N_CHIPS, r, 128), F32)],
        scratch_shapes=[pltpu.SemaphoreType.DMA((per * n + 4,)), pltpu.SemaphoreType.DMA((per * n + 4,))],
    )(*[shards[name] for name in BIG], small)
    return dict(zip(BIG, outs[:n])), outs[n]


def _small_all_reduce(v):
    r = v.shape[0]

    def body(v_ref, o_ref, buf_ref, send_sems, recv_sems):
        x, y, c, _ = _place()
        me = 4 * x + 2 * y + c
        buf_ref[me] = v_ref[...]
        copies = []
        for k in range(1, 8):
            fx, fy, fc = (k >> 2) & 1, (k >> 1) & 1, k & 1
            to = (x ^ fx, y ^ fy, c ^ fc)
            cp = pltpu.make_async_remote_copy(src_ref=v_ref, dst_ref=buf_ref.at[me], send_sem=send_sems.at[k - 1],
                                              recv_sem=recv_sems.at[k - 1], device_id=to, device_id_type=MESH)
            cp.start()
            copies.append(cp)
        for k in range(1, 8):
            fx, fy, fc = (k >> 2) & 1, (k >> 1) & 1, k & 1
            src_dev = 4 * (x ^ fx) + 2 * (y ^ fy) + (c ^ fc)
            pltpu.make_async_remote_copy(src_ref=v_ref, dst_ref=buf_ref.at[src_dev], send_sem=send_sems.at[k - 1],
                                         recv_sem=recv_sems.at[k - 1], device_id=(x, y, c), device_id_type=MESH).wait_recv()
        for cp in copies:
            cp.wait_send()
        tot = buf_ref[0]
        for i in range(1, 8):
            tot = tot + buf_ref[i]
        o_ref[...] = tot

    vm = pl.BlockSpec(memory_space=pltpu.VMEM)
    return pl.pallas_call(
        body, name="small_all_reduce", in_specs=[vm], out_specs=vm,
        out_shape=jax.ShapeDtypeStruct((r, 128), F32),
        scratch_shapes=[pltpu.VMEM((8, r, 128), F32), pltpu.SemaphoreType.DMA((7,)), pltpu.SemaphoreType.DMA((7,))],
    )(v)


def _grad_view(kind, g):
    if kind == "col":
        return g.reshape(2, g.shape[0] // 2, g.shape[1])
    return g.reshape(N_CHIPS, 2, g.shape[0] // (2 * N_CHIPS), g.shape[1])


def _half_of(kind, ref, h):
    return ref.at[h] if kind == "col" else ref.at[:, h]


def _half_shape(kind, view_shape):
    return view_shape[1:] if kind == "col" else (view_shape[0],) + view_shape[2:]


def _piece_of(kind, width, colblock, ref, q):
    if kind == "col":
        return ref.at[:, pl.ds(colblock(q) * width, width)]
    return ref.at[q]


def _piece_shape(kind, width, half_shape):
    return (half_shape[0], width) if kind == "col" else half_shape[1:]


def _pair_exchange(views, kinds):
    n = len(views)

    def body(*refs):
        ins, outs = refs[:n], refs[n:2 * n]
        send_sems, recv_sems = refs[2 * n:]
        x, y, c, _ = _place()
        cps = []
        for t in range(n):
            cp = pltpu.make_async_remote_copy(src_ref=_half_of(kinds[t], ins[t], 1 - c), dst_ref=outs[t],
                                              send_sem=send_sems.at[t], recv_sem=recv_sems.at[t],
                                              device_id=(x, y, 1 - c), device_id_type=MESH)
            cp.start()
            cps.append(cp)
        for cp in cps:
            cp.wait()

    return pl.pallas_call(
        body, name="grad_pair_exchange", in_specs=[HBM_SPEC] * n, out_specs=[HBM_SPEC] * n,
        out_shape=[jax.ShapeDtypeStruct(_half_shape(k, v.shape), v.dtype) for k, v in zip(kinds, views)],
        scratch_shapes=[pltpu.SemaphoreType.DMA((n,)), pltpu.SemaphoreType.DMA((n,))],
    )(*views)


def _pair_sum(kind, view, recv, c, name):
    hs = recv.shape
    N = hs[-1]
    rows = hs[-2]
    tr = _pick(rows, (512, 352, 128))
    tn = _pick(N, (1408, 1024, 512))

    def body(c_ref, p_ref, r_ref, s_ref):
        s_ref[...] = (p_ref[...] + r_ref[...]).astype(BF16)

    if kind == "col":
        grid = (rows // tr, N // tn)
        mine = pl.BlockSpec((None, tr, tn), lambda i, j, c_ref: (c_ref[0], i, j))
        blk = pl.BlockSpec((tr, tn), lambda i, j, c_ref: (i, j))
        sem = ("parallel", "parallel")
    else:
        grid = (N_CHIPS, rows // tr, N // tn)
        mine = pl.BlockSpec((None, None, tr, tn), lambda q, i, j, c_ref: (q, c_ref[0], i, j))
        blk = pl.BlockSpec((None, tr, tn), lambda q, i, j, c_ref: (q, i, j))
        sem = ("parallel", "parallel", "parallel")
    return pl.pallas_call(
        body, name=name,
        grid_spec=pltpu.PrefetchScalarGridSpec(num_scalar_prefetch=1, grid=grid, in_specs=[mine, blk], out_specs=blk),
        out_shape=jax.ShapeDtypeStruct(hs, BF16),
        compiler_params=_cparams(sem),
    )(c.reshape(1).astype(jnp.int32), view, recv)


def _chip_exchange(sums, kinds, widths, colblocks):
    n = len(sums)

    def body(*refs):
        ins, outs = refs[:n], refs[n:2 * n]
        send_sems, recv_sems = refs[2 * n:]
        x, y, c, _ = _place()
        cps = []
        for j, (cx, cy) in enumerate(_other_chips(x, y)):
            for t in range(n):
                cp = pltpu.make_async_remote_copy(
                    src_ref=_piece_of(kinds[t], widths[t], colblocks[t], ins[t], 2 * cx + cy), dst_ref=outs[t].at[j],
                    send_sem=send_sems.at[3 * t + j], recv_sem=recv_sems.at[3 * t + j],
                    device_id=(cx, cy, c), device_id_type=MESH)
                cp.start()
                cps.append(cp)
        for cp in cps:
            cp.wait()

    return pl.pallas_call(
        body, name="grad_chip_exchange", in_specs=[HBM_SPEC] * n, out_specs=[HBM_SPEC] * n,
        out_shape=[jax.ShapeDtypeStruct((3,) + _piece_shape(k, w, s.shape), BF16) for k, w, s in zip(kinds, widths, sums)],
        scratch_shapes=[pltpu.SemaphoreType.DMA((3 * n,)), pltpu.SemaphoreType.DMA((3 * n,))],
    )(*sums)


def _chip_sum(kind, s, recv, block_idx, c, shard_shape, layer, into, name):
    rows, N = recv.shape[1:]
    tr = _pick(rows, (512, 352, 128))
    tn = _pick(N, (1408, 1024, 768, 512))
    ni, nj = rows // tr, N // tn

    def body(q_ref, s_ref, r_ref, *rest):
        o_ref = rest[-1]
        o_ref[...] = ((s_ref[...].astype(F32) + r_ref[0].astype(F32)) + r_ref[1].astype(F32)) + r_ref[2].astype(F32)

    if kind == "col":
        own = pl.BlockSpec((tr, tn), lambda i, j, q_ref: (i, q_ref[0] * nj + j))
    else:
        own = pl.BlockSpec((None, tr, tn), lambda i, j, q_ref: (q_ref[0], i, j))
    if len(shard_shape) == 3:
        lead = 0 if layer is None else layer
        out_spec = pl.BlockSpec((None, tr, tn), lambda i, j, q_ref: (lead, q_ref[1] * ni + i, j))
    else:
        out_spec = pl.BlockSpec((tr, tn), lambda i, j, q_ref: (q_ref[1] * ni + i, j))
    in_specs = [own, pl.BlockSpec((3, tr, tn), lambda i, j, q_ref: (0, i, j))]
    args = [jnp.stack([block_idx, c]).astype(jnp.int32), s, recv]
    aliases = {}
    if into is not None:
        in_specs.append(HBM_SPEC)
        args.append(into)
        aliases = {3: 0}
    return pl.pallas_call(
        body, name=name,
        grid_spec=pltpu.PrefetchScalarGridSpec(num_scalar_prefetch=1, grid=(ni, nj), in_specs=in_specs, out_specs=out_spec),
        out_shape=jax.ShapeDtypeStruct(shard_shape, F32), input_output_aliases=aliases,
        compiler_params=_cparams(("parallel", "parallel")),
    )(*args)


def _half_window(ref, h):
    rows = ref.shape[-2] // 2
    if ref.ndim == 3:
        return ref.at[:, pl.ds(h * rows, rows)]
    return ref.at[pl.ds(h * rows, rows)]


def _share_halves(grads):
    n = len(grads)

    def body(*refs):
        outs = refs[n:2 * n]
        send_sems, recv_sems = refs[2 * n:]
        x, y, c, _ = _place()
        cps = []
        for t in range(n):
            cp = pltpu.make_async_remote_copy(src_ref=_half_window(outs[t], c), dst_ref=_half_window(outs[t], c),
                                              send_sem=send_sems.at[t], recv_sem=recv_sems.at[t],
                                              device_id=(x, y, 1 - c), device_id_type=MESH)
            cp.start()
            cps.append(cp)
        for t in range(n):
            cps[t].wait_send()
            pltpu.make_async_remote_copy(src_ref=_half_window(outs[t], c), dst_ref=_half_window(outs[t], 1 - c),
                                         send_sem=send_sems.at[t], recv_sem=recv_sems.at[t],
                                         device_id=(x, y, 1 - c), device_id_type=MESH).wait_recv()

    return pl.pallas_call(
        body, name="grad_share_halves", in_specs=[HBM_SPEC] * n, out_specs=[HBM_SPEC] * n,
        out_shape=[jax.ShapeDtypeStruct(g.shape, F32) for g in grads],
        input_output_aliases={t: t for t in range(n)},
        scratch_shapes=[pltpu.SemaphoreType.DMA((n,)), pltpu.SemaphoreType.DMA((n,))],
    )(*grads)


def _adamw(w, g, m, v, name):
    R, W = w.shape
    tr = _pick(R, (512, 352, 256, 32))

    def body(w_ref, g_ref, m_ref, v_ref, d_ref, nm_ref, nv_ref):
        gv = g_ref[...]
        nm = ADAM_B1 * m_ref[...] + (1.0 - ADAM_B1) * gv
        nv = ADAM_B2 * v_ref[...] + (1.0 - ADAM_B2) * (gv * gv)
        m_hat = nm / (1.0 - ADAM_B1 ** ADAM_STEP)
        v_hat = nv / (1.0 - ADAM_B2 ** ADAM_STEP)
        d_ref[...] = -ADAM_LR * (m_hat / (jnp.sqrt(v_hat) + ADAM_EPS) + ADAM_WD * w_ref[...])
        nm_ref[...] = nm
        nv_ref[...] = nv

    blk = pl.BlockSpec((tr, W), lambda i: (i, 0))
    shp = jax.ShapeDtypeStruct((R, W), F32)
    return pl.pallas_call(
        body, name=name, grid=(R // tr,), in_specs=[blk] * 4, out_specs=[blk] * 3, out_shape=[shp] * 3,
        compiler_params=_cparams(("parallel",)),
    )(w, g, m, v)


SMALL_ROWS = 32


def _pack_small(ln_g, ln_b, sinks):
    rows = jnp.concatenate([ln_g.reshape(-1, 128), ln_b.reshape(-1, 128),
                            jnp.pad(sinks.reshape(1, -1), ((0, 0), (0, 128 - sinks.size)))], axis=0)
    return jnp.pad(rows, ((0, SMALL_ROWS - rows.shape[0]), (0, 0)))


def _unpack_small(s, ln_shape, sink_shape):
    n = ln_shape[0] * ln_shape[1] * ln_shape[2] // 128
    return s[:n].reshape(ln_shape), s[n:2 * n].reshape(ln_shape), s[2 * n, :sink_shape[1]].reshape(sink_shape)


def _ffn_fwd(xin, w_in, w_out, gain, bias, tag):
    u, h = _ffn_in(xin, w_in, "ffn_in_" + tag)
    y, yb, z = _mm_ln(h, w_out, xin, gain, bias, 0.5, "ffn_out_ln_" + tag)
    return y, yb, dict(u=u, h=h, z=z, xin=xin)


def _ffn_bwd(dy, saved, w_in, w_out, gain, xin_b, tag):
    dz, dzc, gg, gb = _ln_bwd(saved["z"], dy, gain, 0.5, "ln_bwd_" + tag)
    du = _ffn_bwd_h(dzc, w_out, saved["u"], "ffn_bwd_h_" + tag)
    d_w_out = _mm_tn(saved["h"], dzc, "ffn_dwout_" + tag)
    d_w_in = _mm_tn(xin_b, du, "ffn_dwin_" + tag)
    dx = _mm_nt(du, w_in, "ffn_dx_" + tag, add=dz, add_scale=ALPHA)
    return dx, d_w_in, d_w_out, gg, gb


def kernel(x, ffn1_w_in, ffn1_w_out, ffn2_w_in, ffn2_w_out, ln_g, ln_b, a_w_qkv, a_w_o, kv_w, b_w_q, b_sinks, b_w_o, loss_target, m_ffn1_w_in, m_ffn1_w_out, m_ffn2_w_in, m_ffn2_w_out, m_ln_g, m_ln_b, m_a_w_qkv, m_a_w_o, m_kv_w, m_b_w_q, m_b_sinks, m_b_w_o, v_ffn1_w_in, v_ffn1_w_out, v_ffn2_w_in, v_ffn2_w_out, v_ln_g, v_ln_b, v_a_w_qkv, v_a_w_o, v_kv_w, v_b_w_q, v_b_sinks, v_b_w_o):
    ws = dict(ffn1_w_in=ffn1_w_in, ffn1_w_out=ffn1_w_out, ffn2_w_in=ffn2_w_in, ffn2_w_out=ffn2_w_out, a_w_qkv=a_w_qkv,
              a_w_o=a_w_o, kv_w=kv_w, b_w_q=b_w_q, b_w_o=b_w_o)
    ms = dict(ffn1_w_in=m_ffn1_w_in, ffn1_w_out=m_ffn1_w_out, ffn2_w_in=m_ffn2_w_in, ffn2_w_out=m_ffn2_w_out,
              a_w_qkv=m_a_w_qkv, a_w_o=m_a_w_o, kv_w=m_kv_w, b_w_q=m_b_w_q, b_w_o=m_b_w_o)
    vs = dict(ffn1_w_in=v_ffn1_w_in, ffn1_w_out=v_ffn1_w_out, ffn2_w_in=v_ffn2_w_in, ffn2_w_out=v_ffn2_w_out,
              a_w_qkv=v_a_w_qkv, a_w_o=v_a_w_o, kv_w=v_kv_w, b_w_q=v_b_w_q, b_w_o=v_b_w_o)
    _, _, c_idx, myq = _place()
    xs = x[0]
    target = loss_target[0]

    W, small = _all_gather({n: ws[n].astype(BF16) for n in BIG}, _pack_small(ln_g, ln_b, b_sinks))
    for n in ("a_w_o", "kv_w", "b_w_q", "b_w_o"):
        W[n] = W[n].reshape(D_MODEL, W[n].shape[-1])
    n_ln = ln_g.size // 128
    lg = jnp.concatenate([small[q, :n_ln].reshape(DEPTH, 3, 1, -1) for q in range(N_CHIPS)], axis=-1)
    lb = jnp.concatenate([small[q, n_ln:2 * n_ln].reshape(DEPTH, 3, 1, -1) for q in range(N_CHIPS)], axis=-1)
    sq, grad_x, gr, gg, gb, dsink_part = _local_step(xs, target, W, lg, lb, b_sinks.reshape(N_HEADS))

    loss_row = jnp.pad(jnp.sum(sq).reshape(1, 1), ((0, 0), (0, 127)))
    dsinks = jnp.pad(dsink_part[:, 0, :].reshape(N_SLABS, 2, HEAD_DIM)[:, :, 0].reshape(1, N_HEADS), ((0, 0), (0, 128 - N_HEADS)))
    gg_full = jnp.stack([jnp.stack([jnp.sum(gg[i][j], axis=0) for j in range(3)]) for i in range(DEPTH)])
    gb_full = jnp.stack([jnp.stack([jnp.sum(gb[i][j], axis=0) for j in range(3)]) for i in range(DEPTH)])
    small_in = jnp.concatenate([loss_row, dsinks, gg_full.reshape(-1, 128), gb_full.reshape(-1, 128)], axis=0)
    small_in = jnp.pad(small_in, ((0, (-small_in.shape[0]) % 8), (0, 0)))
    small_sum = _small_all_reduce(small_in)
    loss = small_sum[0, 0] * (0.5 / D_MODEL)
    grad_sinks = small_sum[1, :N_HEADS].reshape(b_sinks.shape)
    n_full = DEPTH * 3 * D_MODEL // 128
    cols = D_MODEL // N_CHIPS
    grad_ln_g = lax.dynamic_slice_in_dim(small_sum[2:2 + n_full].reshape(DEPTH, 3, D_MODEL), myq * cols, cols, axis=2)
    grad_ln_b = lax.dynamic_slice_in_dim(small_sum[2 + n_full:2 + 2 * n_full].reshape(DEPTH, 3, D_MODEL), myq * cols, cols, axis=2)
    return _reduce_and_update(gr, grad_x, loss, grad_ln_g, grad_ln_b, grad_sinks, ws, ms, vs, c_idx, myq,
                              (ln_g, ln_b, b_sinks), (m_ln_g, m_ln_b, m_b_sinks), (v_ln_g, v_ln_b, v_b_sinks))


def _local_step(xs, target, W, lg, lb, sinks):
    S = xs.shape[0]
    slopes = jnp.asarray(_alibi_slopes(N_HEADS))
    in1 = [(W["ffn1_w_in"], i) for i in range(DEPTH)]
    out1 = [(W["ffn1_w_out"], i) for i in range(DEPTH)]
    in2 = [(W["ffn2_w_in"], i) for i in range(DEPTH)]
    out2 = [(W["ffn2_w_out"], i) for i in range(DEPTH)]

    y1, y1b, s1 = _ffn_fwd(xs, in1[0], out1[0], lg[0, 0], lb[0, 0], "a1")
    qkv_a = _mm_nn(y1b, W["a_w_qkv"], F32, "qkv_a", split=True)
    mix_a, o_a, lse_a = _attn_fwd(qkv_a, slopes, None, PATTERNS_A, "attn_a_fwd")
    y2, y2b, z2 = _mm_ln(mix_a, W["a_w_o"], y1, lg[0, 1], lb[0, 1], 1.0, "attn_a_out_ln")
    y3, y3b, s3 = _ffn_fwd(y2, in2[0], out2[0], lg[0, 2], lb[0, 2], "a2")
    kv = _mm_nn(y3b, W["kv_w"], F32, "kv_proj")
    y4, y4b, s4 = _ffn_fwd(y3, in1[1], out1[1], lg[1, 0], lb[1, 0], "b1")
    q_b = _mm_nn(y4b, W["b_w_q"], F32, "q_b")
    k_sh = kv[:, :N_KV_B * HEAD_DIM].reshape(S, N_KV_B, 1, HEAD_DIM)
    v_sh = kv[:, N_KV_B * HEAD_DIM:].reshape(S, N_KV_B, 1, HEAD_DIM)
    k_exp = jnp.broadcast_to(k_sh, (S, N_KV_B, GROUP_B, HEAD_DIM)).reshape(S, D_MODEL)
    v_exp = jnp.broadcast_to(v_sh, (S, N_KV_B, GROUP_B, HEAD_DIM)).reshape(S, D_MODEL)
    qkv_b = jnp.stack([q_b, k_exp, v_exp])
    mix_b, o_b, lse_b = _attn_fwd(qkv_b, slopes, sinks, PATTERNS_B, "attn_b_fwd")
    y5, y5b, z5 = _mm_ln(mix_b, W["b_w_o"], y4, lg[1, 1], lb[1, 1], 1.0, "attn_b_out_ln")
    y6, _, s6 = _ffn_fwd(y5, in2[1], out2[1], lg[1, 2], lb[1, 2], "b2")

    dy6, sq = _loss_grad(y6, target, "loss_grad")
    gr = {n: None for n in BIG}
    gg = [[None] * 3 for _ in range(DEPTH)]
    gb = [[None] * 3 for _ in range(DEPTH)]

    dy5, d_in2_b, d_out2_b, gg[1][2], gb[1][2] = _ffn_bwd(dy6, s6, in2[1], out2[1], lg[1, 2], y5b, "b2")
    dz5, dz5b, gg[1][1], gb[1][1] = _ln_bwd(z5, dy5, lg[1, 1], 1.0, "ln_bwd_attn_b")
    gr["b_w_o"] = _mm_tn(mix_b, dz5b, "d_b_w_o")
    dmix_b = _mm_nt(dz5b, W["b_w_o"], "d_mix_b")
    dqkv_b, dsink_part = _attn_bwd(qkv_b, dmix_b, o_b, lse_b, slopes, sinks, PATTERNS_B, "attn_b_bwd")
    dq_b, dk_exp, dv_exp = (dqkv_b, 0), dqkv_b[1], dqkv_b[2]
    dkv = jnp.concatenate([dk_exp.reshape(S, N_KV_B, GROUP_B, HEAD_DIM).sum(axis=2).reshape(S, -1),
                           dv_exp.reshape(S, N_KV_B, GROUP_B, HEAD_DIM).sum(axis=2).reshape(S, -1)], axis=1)
    gr["b_w_q"] = _mm_tn(y4b, dq_b, "d_b_w_q")
    dy4 = _mm_nt(dq_b, W["b_w_q"], "d_y4", add=dz5, add_scale=ALPHA)
    dy3, d_in1_b, d_out1_b, gg[1][0], gb[1][0] = _ffn_bwd(dy4, s4, in1[1], out1[1], lg[1, 0], y3b, "b1")
    gr["kv_w"] = _mm_tn(y3b, dkv, "d_kv_w")
    dy3 = _mm_nt(dkv, W["kv_w"], "d_y3_kv", add=dy3, add_scale=1.0)

    dy2, d_in2_a, d_out2_a, gg[0][2], gb[0][2] = _ffn_bwd(dy3, s3, in2[0], out2[0], lg[0, 2], y2b, "a2")
    dz2, dz2b, gg[0][1], gb[0][1] = _ln_bwd(z2, dy2, lg[0, 1], 1.0, "ln_bwd_attn_a")
    gr["a_w_o"] = _mm_tn(mix_a, dz2b, "d_a_w_o")
    dmix_a = _mm_nt(dz2b, W["a_w_o"], "d_mix_a")
    dqkv_a, _ = _attn_bwd(qkv_a, dmix_a, o_a, lse_a, slopes, None, PATTERNS_A, "attn_a_bwd")
    gr["a_w_qkv"] = _mm_tn(y1b, dqkv_a, "d_a_w_qkv", split=True)
    dy1 = _mm_nt(dqkv_a, W["a_w_qkv"], "d_y1", add=dz2, add_scale=ALPHA, split=True)
    grad_x, d_in1_a, d_out1_a, gg[0][0], gb[0][0] = _ffn_bwd(dy1, s1, in1[0], out1[0], lg[0, 0], xs, "a1")
    gr["ffn1_w_in"] = [d_in1_a, d_in1_b]
    gr["ffn1_w_out"] = [d_out1_a, d_out1_b]
    gr["ffn2_w_in"] = [d_in2_a, d_in2_b]
    gr["ffn2_w_out"] = [d_out2_a, d_out2_b]
    return sq, grad_x, gr, gg, gb, dsink_part


def _reduce_and_update(gr, grad_x, loss, grad_ln_g, grad_ln_b, grad_sinks, ws, ms, vs, c_idx, myq,
                       small_w, small_m, small_v):
    ln_g, ln_b, b_sinks = small_w
    m_ln_g, m_ln_b, m_b_sinks = small_m
    v_ln_g, v_ln_b, v_b_sinks = small_v

    items = []
    for oi, name in enumerate(BIG):
        if name.endswith("w_in"):
            items += [(gr[name][l], "col", HALF_FF, _slot, (oi, name, l)) for l in range(DEPTH)]
        elif name.endswith("w_out"):
            items += [(gr[name][l], "row", D_MODEL, None, (oi, name, l)) for l in range(DEPTH)]
        elif name == "a_w_qkv":
            items.append((gr[name], "col", QKV_SHARD, lambda q: q, (oi, name, None)))
        else:
            items.append((gr[name], "row", gr[name].shape[1], None, (oi, name, None)))
    kinds = [it[1] for it in items]
    widths = [it[2] for it in items]
    colblocks = [it[3] for it in items]
    views = [_grad_view(k, it[0]) for k, it in zip(kinds, items)]
    from_sibling = _pair_exchange(views, kinds)
    sums = [_pair_sum(k, v, r, c_idx, "pair_sum_%d" % t) for t, (k, v, r) in enumerate(zip(kinds, views, from_sibling))]
    from_chips = _chip_exchange(sums, kinds, widths, colblocks)
    half_done = {name: None for name in BIG}
    for t, (k, cb, s, r, it) in enumerate(zip(kinds, colblocks, sums, from_chips, items)):
        _, name, layer = it[4]
        own = cb(myq) if k == "col" else myq
        half_done[name] = _chip_sum(k, s, r, own, c_idx, ws[name].shape, layer, half_done[name], "chip_sum_%d" % t)
    grads = dict(zip(BIG, _share_halves([half_done[name] for name in BIG])))

    deltas, new_m, new_v = {}, {}, {}
    for name in BIG:
        shp = ws[name].shape
        flat = lambda a: a.reshape(-1, shp[-1])
        d, nm, nv = _adamw(flat(ws[name]), flat(grads[name]), flat(ms[name]), flat(vs[name]), "adamw_" + name)
        deltas[name], new_m[name], new_v[name] = d.reshape(shp), nm.reshape(shp), nv.reshape(shp)
    delta_s, nm_s, nv_s = _adamw(_pack_small(ln_g, ln_b, b_sinks), _pack_small(grad_ln_g, grad_ln_b, grad_sinks),
                                 _pack_small(m_ln_g, m_ln_b, m_b_sinks), _pack_small(v_ln_g, v_ln_b, v_b_sinks), "adamw_small")
    for d, blob in ((grads, None), (deltas, delta_s), (new_m, nm_s), (new_v, nv_s)):
        if blob is None:
            d["ln_g"], d["ln_b"], d["b_sinks"] = grad_ln_g, grad_ln_b, grad_sinks
        else:
            d["ln_g"], d["ln_b"], d["b_sinks"] = _unpack_small(blob, ln_g.shape, b_sinks.shape)

    order = ("ffn1_w_in", "ffn1_w_out", "ffn2_w_in", "ffn2_w_out", "ln_g", "ln_b", "a_w_qkv", "a_w_o", "kv_w", "b_w_q",
             "b_sinks", "b_w_o")
    outs = [loss, grad_x[None]]
    for d in (grads, deltas, new_m, new_v):
        outs += [d[n] for n in order]
    return tuple(outs)
```

```python
import numpy as np
import jax
import jax.numpy as jnp
from jax import lax
from jax.experimental import pallas as pl
from jax.experimental.pallas import tpu as pltpu

F32 = jnp.float32
BF16 = jnp.bfloat16

D_MODEL = 1024
D_FF = 2816
HALF_FF = D_FF // 2
HEAD_DIM = 64
N_HEADS = 16
N_KV_B = 4
GROUP_B = N_HEADS // N_KV_B
DEPTH = 2
ALPHA = (2.0 * DEPTH) ** 0.25
LN_EPS = 1e-5
BLOCK = 128
SLAB = 128
N_SLABS = D_MODEL // SLAB
PATTERNS_A = ((1, 128, 1.0), (4, 128, 4.0), (16, 128, 16.0))
PATTERNS_B = ((1, 127, 1.0),)
NEG = -1e30

ADAM_LR = 0.001
ADAM_B1 = 0.9
ADAM_B2 = 0.999
ADAM_EPS = 1e-08
ADAM_WD = 0.01
ADAM_STEP = 10

N_CHIPS = 4
VMEM_LIMIT = 56 * 1024 * 1024
MESH = pl.DeviceIdType.MESH


def _alibi_slopes(n):
    return np.array([2.0 ** (-8.0 * (h + 1) / n) for h in range(n)], dtype=np.float32)


def _cparams(sem=None, vmem=VMEM_LIMIT):
    return pltpu.CompilerParams(dimension_semantics=sem, vmem_limit_bytes=vmem)


_DIMS = {"nn": ((1,), (0,)), "nt": ((1,), (1,)), "tn": ((0,), (0,))}


def _unlead(x):
    if isinstance(x, tuple):
        return x[0], x[1], x[0].shape[1:]
    return x, None, x.shape


def _bspec(block, imap, lead=None):
    if lead is None:
        return pl.BlockSpec(block, imap)
    return pl.BlockSpec((None,) + tuple(block), lambda *g: (lead,) + tuple(imap(*g)))


def _matmul(a, b, mode, out_dtype, tm, tn, tk, name, add=None, add_scale=1.0, split=False):
    out_spec = pl.BlockSpec((tm, tn), lambda i, j, k: (i, j))
    if mode == "nn":
        a, al, (M, K) = _unlead(a)
        b, bl, (K2, N) = _unlead(b)
        a_spec = _bspec((tm, tk), lambda i, j, k: (i, k), al)
        b_spec = _bspec((tk, tn), lambda i, j, k: (k, j), bl)
        out_struct = jax.ShapeDtypeStruct((M, N), out_dtype)
        if split:
            assert tn == D_MODEL
            out_spec = pl.BlockSpec((None, tm, tn), lambda i, j, k: (j, i, 0))
            out_struct = jax.ShapeDtypeStruct((N // tn, M, tn), out_dtype)
    elif mode == "nt":
        b, bl, (N, K2) = _unlead(b)
        if split:
            assert tk == D_MODEL
            M, K = a.shape[1], a.shape[0] * a.shape[2]
            a_spec = pl.BlockSpec((None, tm, tk), lambda i, j, k: (k, i, 0))
        else:
            a, al, (M, K) = _unlead(a)
            a_spec = _bspec((tm, tk), lambda i, j, k: (i, k), al)
        b_spec = _bspec((tn, tk), lambda i, j, k: (j, k), bl)
        out_struct = jax.ShapeDtypeStruct((M, N), out_dtype)
    else:
        a, al, (K, M) = _unlead(a)
        if split:
            assert tn == D_MODEL
            K2, N = b.shape[1], b.shape[0] * b.shape[2]
            b_spec = pl.BlockSpec((None, tk, tn), lambda i, j, k: (j, k, 0))
        else:
            b, bl, (K2, N) = _unlead(b)
            b_spec = _bspec((tk, tn), lambda i, j, k: (k, j), bl)
        a_spec = _bspec((tk, tm), lambda i, j, k: (k, i), al)
        out_struct = jax.ShapeDtypeStruct((M, N), out_dtype)
    assert K == K2 and M % tm == 0 and N % tn == 0 and K % tk == 0, (a.shape, b.shape, mode, tm, tn, tk)
    nk = K // tk
    dims = (_DIMS[mode], ((), ()))
    has_add = add is not None

    def body(*refs):
        if has_add:
            a_ref, b_ref, add_ref, o_ref, acc_ref = refs
        else:
            a_ref, b_ref, o_ref, acc_ref = refs
        k = pl.program_id(2)
        part = lax.dot_general(a_ref[...].astype(BF16), b_ref[...].astype(BF16), dims, preferred_element_type=F32)

        @pl.when(k == 0)
        def _():
            acc_ref[...] = part

        @pl.when(k > 0)
        def _():
            acc_ref[...] += part

        @pl.when(k == nk - 1)
        def _():
            r = acc_ref[...]
            if has_add:
                r = r + add_scale * add_ref[...]
            o_ref[...] = r.astype(out_dtype)

    in_specs = [a_spec, b_spec]
    args = [a, b]
    if has_add:
        in_specs.append(pl.BlockSpec((tm, tn), lambda i, j, k: (i, j)))
        args.append(add)
    return pl.pallas_call(
        body, name=name, grid=(M // tm, N // tn, nk),
        in_specs=in_specs, out_specs=out_spec, out_shape=out_struct,
        scratch_shapes=[pltpu.VMEM((tm, tn), F32)],
        compiler_params=_cparams(("parallel", "parallel", "arbitrary")),
    )(*args)


def _pick(n, cands):
    for c in cands:
        if n % c == 0:
            return c
    raise ValueError((n, cands))


def _mm_nn(a, b, out_dtype, name, split=False):
    M, K = _unlead(a)[2]
    N = _unlead(b)[2][1]
    return _matmul(a, b, "nn", out_dtype, _pick(M, (1024, 512, 256)), _pick(N, (1024, 512)), _pick(K, (1024, 512)), name,
                   split=split)


def _mm_nt(a, b, name, add=None, add_scale=1.0, split=False):
    M, K = (a.shape[1], D_MODEL) if split else _unlead(a)[2]
    N = _unlead(b)[2][0]
    return _matmul(a, b, "nt", F32, _pick(M, (1024, 512, 256)), _pick(N, (1024, 512)),
                   _pick(K, (1408, 1024, 512)), name, add=add, add_scale=add_scale, split=split)


def _mm_tn(a, b, name, split=False):
    K, M = _unlead(a)[2]
    N = D_MODEL if split else _unlead(b)[2][1]
    return _matmul(a, b, "tn", F32, _pick(M, (1024, 1408, 512)), _pick(N, (1408, 1024, 512)),
                   _pick(K, (1024, 512, 256)), name, split=split)


def _ffn_in(x, w, name):
    S = x.shape[0]
    tm = _pick(S, (512, 256))
    w, wl, _ = _unlead(w)

    def body(x_ref, w_ref, u_ref, h_ref):
        acc = jnp.dot(x_ref[...].astype(BF16), w_ref[...], preferred_element_type=F32)
        g = acc[:, :HALF_FF]
        up = acc[:, HALF_FF:]
        u_ref[...] = acc.astype(BF16)
        h_ref[...] = (g * jax.nn.sigmoid(g) * up).astype(BF16)

    return pl.pallas_call(
        body, name=name, grid=(2, S // tm),
        in_specs=[pl.BlockSpec((tm, D_MODEL), lambda j, i: (i, 0)),
                  _bspec((D_MODEL, D_FF), lambda j, i: (0, j), wl)],
        out_specs=[pl.BlockSpec((tm, D_FF), lambda j, i: (i, j)),
                   pl.BlockSpec((tm, HALF_FF), lambda j, i: (i, j))],
        out_shape=[jax.ShapeDtypeStruct((S, 2 * D_FF), BF16), jax.ShapeDtypeStruct((S, D_FF), BF16)],
        compiler_params=_cparams(("parallel", "parallel")),
    )(x, w)


def _ffn_bwd_h(dzc, w_out, u, name):
    S = dzc.shape[0]
    tm = _pick(S, (512, 256))
    w_out, wl, _ = _unlead(w_out)

    def body(dz_ref, w_ref, u_ref, du_ref):
        dh = lax.dot_general(dz_ref[...], w_ref[...], (((1,), (1,)), ((), ())), preferred_element_type=F32)
        g = u_ref[:, :HALF_FF].astype(F32)
        up = u_ref[:, HALF_FF:].astype(F32)
        sg = jax.nn.sigmoid(g)
        du_ref[:, :HALF_FF] = (dh * up * (sg * (1.0 + g * (1.0 - sg)))).astype(BF16)
        du_ref[:, HALF_FF:] = (dh * (g * sg)).astype(BF16)

    return pl.pallas_call(
        body, name=name, grid=(2, S // tm),
        in_specs=[pl.BlockSpec((tm, D_MODEL), lambda j, i: (i, 0)),
                  _bspec((HALF_FF, D_MODEL), lambda j, i: (j, 0), wl),
                  pl.BlockSpec((tm, D_FF), lambda j, i: (i, j))],
        out_specs=pl.BlockSpec((tm, D_FF), lambda j, i: (i, j)),
        out_shape=jax.ShapeDtypeStruct((S, 2 * D_FF), BF16),
        compiler_params=_cparams(("parallel", "parallel")),
    )(dzc, w_out, u)


def _mm_ln(a, w, resid, gain, bias, c, name):
    S, K = a.shape
    tm = _pick(S, (512, 256))
    tk = _pick(K, (1408, 1024))
    nk = K // tk
    w, wl, _ = _unlead(w)

    def body(a_ref, w_ref, r_ref, g_ref, b_ref, y_ref, yb_ref, z_ref, acc_ref):
        k = pl.program_id(1)
        part = jnp.dot(a_ref[...], w_ref[...], preferred_element_type=F32)

        @pl.when(k == 0)
        def _():
            acc_ref[...] = part

        @pl.when(k > 0)
        def _():
            acc_ref[...] += part

        @pl.when(k == nk - 1)
        def _():
            z = ALPHA * r_ref[...] + c * acc_ref[...]
            mu = jnp.mean(z, axis=-1, keepdims=True)
            zc = z - mu
            var = jnp.mean(zc * zc, axis=-1, keepdims=True)
            y = zc * lax.rsqrt(var + LN_EPS) * g_ref[...] + b_ref[...]
            z_ref[...] = z
            y_ref[...] = y
            yb_ref[...] = y.astype(BF16)

    row = pl.BlockSpec((tm, D_MODEL), lambda i, k: (i, 0))
    vec = pl.BlockSpec((1, D_MODEL), lambda i, k: (0, 0))
    return pl.pallas_call(
        body, name=name, grid=(S // tm, nk),
        in_specs=[pl.BlockSpec((tm, tk), lambda i, k: (i, k)), _bspec((tk, D_MODEL), lambda i, k: (k, 0), wl),
                  row, vec, vec],
        out_specs=[row, row, row],
        out_shape=[jax.ShapeDtypeStruct((S, D_MODEL), F32), jax.ShapeDtypeStruct((S, D_MODEL), BF16),
                   jax.ShapeDtypeStruct((S, D_MODEL), F32)],
        scratch_shapes=[pltpu.VMEM((tm, D_MODEL), F32)],
        compiler_params=_cparams(("parallel", "arbitrary")),
    )(a, w, resid, gain, bias)


def _ln_bwd(z, dy, gain, c, name):
    S = z.shape[0]
    tm = _pick(S, (512, 256))

    def body(z_ref, dy_ref, g_ref, dz_ref, dzc_ref, gg_ref, gb_ref):
        i = pl.program_id(0)
        zv = z_ref[...]
        dyv = dy_ref[...]
        mu = jnp.mean(zv, axis=-1, keepdims=True)
        zc = zv - mu
        var = jnp.mean(zc * zc, axis=-1, keepdims=True)
        rstd = lax.rsqrt(var + LN_EPS)
        xhat = zc * rstd
        dyg = dyv * g_ref[...]
        m1 = jnp.mean(dyg, axis=-1, keepdims=True)
        m2 = jnp.mean(dyg * xhat, axis=-1, keepdims=True)
        dz = rstd * (dyg - m1 - xhat * m2)
        dz_ref[...] = dz
        dzc_ref[...] = (c * dz).astype(BF16)
        pg = jnp.sum((dyv * xhat).reshape(tm // 8, 8, D_MODEL), axis=0)
        pb = jnp.sum(dyv.reshape(tm // 8, 8, D_MODEL), axis=0)

        @pl.when(i == 0)
        def _():
            gg_ref[...] = pg
            gb_ref[...] = pb

        @pl.when(i > 0)
        def _():
            gg_ref[...] += pg
            gb_ref[...] += pb

    row = pl.BlockSpec((tm, D_MODEL), lambda i: (i, 0))
    part = pl.BlockSpec((8, D_MODEL), lambda i: (0, 0))
    return pl.pallas_call(
        body, name=name, grid=(S // tm,),
        in_specs=[row, row, pl.BlockSpec((1, D_MODEL), lambda i: (0, 0))],
        out_specs=[row, row, part, part],
        out_shape=[jax.ShapeDtypeStruct((S, D_MODEL), F32), jax.ShapeDtypeStruct((S, D_MODEL), BF16),
                   jax.ShapeDtypeStruct((8, D_MODEL), F32), jax.ShapeDtypeStruct((8, D_MODEL), F32)],
        compiler_params=_cparams(("arbitrary",)),
    )(z, dy, gain)


def _loss_grad(y, t, name):
    S = y.shape[0]
    tm = _pick(S, (512, 256))

    def body(y_ref, t_ref, dy_ref, sq_ref):
        i = pl.program_id(0)
        e = y_ref[...] - t_ref[...]
        dy_ref[...] = e * (1.0 / D_MODEL)
        ps = jnp.sum((e * e).reshape(tm // 8, 8, D_MODEL), axis=0)

        @pl.when(i == 0)
        def _():
            sq_ref[...] = ps

        @pl.when(i > 0)
        def _():
            sq_ref[...] += ps

    row = pl.BlockSpec((tm, D_MODEL), lambda i: (i, 0))
    return pl.pallas_call(
        body, name=name, grid=(S // tm,),
        in_specs=[row, row], out_specs=[row, pl.BlockSpec((8, D_MODEL), lambda i: (0, 0))],
        out_shape=[jax.ShapeDtypeStruct((S, D_MODEL), F32), jax.ShapeDtypeStruct((8, D_MODEL), F32)],
        compiler_params=_cparams(("arbitrary",)),
    )(y, t)


def _rows(start, d):
    if d == 1:
        return pl.ds(pl.multiple_of(start, BLOCK), BLOCK)
    return pl.ds(start, BLOCK, stride=d)


def _ld(ref, start, d):
    return ref[_rows(start, d), :]


def _ld3(ref, lead, start, d):
    return ref[lead, _rows(start, d), :]


def _st3(ref, lead, start, d, val):
    ref[lead, _rows(start, d), :] = val


def _acc3(ref, lead, start, d, val):
    ref[lead, _rows(start, d), :] = ref[lead, _rows(start, d), :] + val


def _band_consts(slope0, slope1, maxd, scale):
    row = lax.broadcasted_iota(jnp.int32, (2 * BLOCK, 2 * BLOCK), 0)
    kj = lax.broadcasted_iota(jnp.int32, (2 * BLOCK, 2 * BLOCK), 1)
    top = row < BLOCK
    dist = BLOCK + jnp.where(top, row, row - BLOCK) - kj
    slope = jnp.where(top, slope0, slope1)
    base = jnp.where((dist >= 0) & (dist <= maxd), -(slope * (dist.astype(F32) * scale)), NEG)
    return base, kj < BLOCK


def _stack_heads(x, lo):
    return jnp.concatenate([jnp.where(lo, x, 0.0), jnp.where(lo, 0.0, x)], axis=0)


def _unstack_heads(x2, lo):
    return jnp.where(lo, x2[:BLOCK], x2[BLOCK:])


def _scores(q2, k2, base, prev_keys, first):
    s = lax.dot_general(q2, k2, (((1,), (1,)), ((), ())), preferred_element_type=F32) * (HEAD_DIM ** -0.5) + base
    return jnp.where(jnp.logical_and(prev_keys, first), NEG, s)


def _softmax_weights(ls):
    mx = ls[0]
    for l in ls[1:]:
        mx = jnp.maximum(mx, l)
    es = [jnp.exp(l - mx) for l in ls]
    tot = es[0]
    for e in es[1:]:
        tot = tot + e
    inv = 1.0 / tot
    return [e * inv for e in es]


def _attn_fwd(qkv, slopes, sinks, patterns, name):
    S = qkv.shape[1]
    npat = len(patterns)
    has_sink = sinks is not None
    if not has_sink:
        sinks = jnp.zeros((N_HEADS,), F32)
    rows_c = 256

    def body(slopes_ref, sinks_ref, x_ref, mix_ref, o_ref, lse_ref):
        p = pl.program_id(0)
        lo = lax.broadcasted_iota(jnp.int32, (BLOCK, SLAB), 1) < HEAD_DIM
        top1 = lax.broadcasted_iota(jnp.int32, (2 * BLOCK, 1), 0) < BLOCK
        sk2 = jnp.where(top1, sinks_ref[2 * p], sinks_ref[2 * p + 1])
        for pi, (d, maxd, scale) in enumerate(patterns):
            nb = S // d // BLOCK
            base, prev_keys = _band_consts(slopes_ref[2 * p], slopes_ref[2 * p + 1], maxd, scale)

            def blk(t, carry, pi=pi, d=d, nb=nb, base=base, prev_keys=prev_keys):
                r = t // nb
                n = t - r * nb
                start = r + (d * BLOCK) * n
                prev = jnp.where(n > 0, start - d * BLOCK, start)
                q2 = _stack_heads(_ld3(x_ref, 0, start, d), lo).astype(BF16)
                k2 = jnp.concatenate([_ld3(x_ref, 1, prev, d), _ld3(x_ref, 1, start, d)], axis=0).astype(BF16)
                v2 = jnp.concatenate([_ld3(x_ref, 2, prev, d), _ld3(x_ref, 2, start, d)], axis=0).astype(BF16)
                s = _scores(q2, k2, base, prev_keys, n == 0)
                m = jnp.max(s, axis=-1, keepdims=True)
                if has_sink:
                    m = jnp.maximum(m, sk2)
                e = jnp.exp(s - m)
                den = jnp.sum(e, axis=-1, keepdims=True)
                if has_sink:
                    den = den + jnp.exp(sk2 - m)
                o2 = jnp.dot((e / den).astype(BF16), v2, preferred_element_type=F32)
                _st3(o_ref, pi, start, d, _unstack_heads(o2, lo))
                _st3(lse_ref, pi, start, d, _unstack_heads(m + jnp.log(den), lo))
                return carry

            lax.fori_loop(0, d * nb, blk, 0, unroll=8)

        def comb(ci, carry):
            rows = pl.ds(pl.multiple_of(ci * rows_c, rows_c), rows_c)
            if npat == 1:
                mix_ref[rows, :] = o_ref[0, rows, :].astype(BF16)
            else:
                ws = _softmax_weights([lse_ref[i, rows, :] for i in range(npat)])
                acc = ws[0] * o_ref[0, rows, :]
                for i in range(1, npat):
                    acc = acc + ws[i] * o_ref[i, rows, :]
                mix_ref[rows, :] = acc.astype(BF16)
            return carry

        lax.fori_loop(0, S // rows_c, comb, 0)

    smem = pl.BlockSpec(memory_space=pltpu.SMEM)
    slab3 = pl.BlockSpec((npat, S, SLAB), lambda p: (0, 0, p))
    return pl.pallas_call(
        body, name=name, grid=(N_SLABS,),
        in_specs=[smem, smem, pl.BlockSpec((3, S, SLAB), lambda p: (0, 0, p))],
        out_specs=[pl.BlockSpec((S, SLAB), lambda p: (0, p)), slab3, slab3],
        out_shape=[jax.ShapeDtypeStruct((S, D_MODEL), BF16), jax.ShapeDtypeStruct((npat, S, D_MODEL), F32),
                   jax.ShapeDtypeStruct((npat, S, D_MODEL), F32)],
        compiler_params=_cparams(("arbitrary",)),
    )(slopes, sinks, qkv)


def _attn_bwd(qkv, dout, o, lse, slopes, sinks, patterns, name):
    S = qkv.shape[1]
    npat = len(patterns)
    has_sink = sinks is not None
    if not has_sink:
        sinks = jnp.zeros((N_HEADS,), F32)
    rows_c = 256

    def headsum(x, lo):
        s0 = jnp.sum(jnp.where(lo, x, 0.0), axis=-1, keepdims=True)
        s1 = jnp.sum(jnp.where(lo, 0.0, x), axis=-1, keepdims=True)
        return jnp.where(lo, s0, s1)

    def body(slopes_ref, sinks_ref, x_ref, do_ref, o_ref, lse_ref, dx_ref, dsink_ref, dbar_ref, sacc_ref):
        p = pl.program_id(0)
        lo = lax.broadcasted_iota(jnp.int32, (BLOCK, SLAB), 1) < HEAD_DIM
        lo_c = lax.broadcasted_iota(jnp.int32, (rows_c, SLAB), 1) < HEAD_DIM
        top1 = lax.broadcasted_iota(jnp.int32, (2 * BLOCK, 1), 0) < BLOCK
        sk2 = jnp.where(top1, sinks_ref[2 * p], sinks_ref[2 * p + 1])

        def prep(ci, carry):
            rows = pl.ds(pl.multiple_of(ci * rows_c, rows_c), rows_c)
            dov = do_ref[rows, :]
            dx_ref[:, rows, :] = jnp.zeros((3, rows_c, SLAB), F32)
            if npat == 1:
                dbar_ref[rows, :] = headsum(dov * o_ref[0, rows, :], lo_c)
            else:
                ws = _softmax_weights([lse_ref[i, rows, :] for i in range(npat)])
                acc = ws[0] * headsum(dov * o_ref[0, rows, :], lo_c)
                for i in range(1, npat):
                    acc = acc + ws[i] * headsum(dov * o_ref[i, rows, :], lo_c)
                dbar_ref[rows, :] = acc
            return carry

        lax.fori_loop(0, S // rows_c, prep, 0)
        sacc_ref[...] = jnp.zeros((BLOCK, SLAB), F32)

        for pi, (d, maxd, scale) in enumerate(patterns):
            nb = S // d // BLOCK
            base, prev_keys = _band_consts(slopes_ref[2 * p], slopes_ref[2 * p + 1], maxd, scale)

            def blk(t, carry, pi=pi, d=d, nb=nb, base=base, prev_keys=prev_keys):
                r = t // nb
                n = t - r * nb
                start = r + (d * BLOCK) * n
                prev = jnp.where(n > 0, start - d * BLOCK, start)
                q2 = _stack_heads(_ld3(x_ref, 0, start, d), lo).astype(BF16)
                k2 = jnp.concatenate([_ld3(x_ref, 1, prev, d), _ld3(x_ref, 1, start, d)], axis=0).astype(BF16)
                v2 = jnp.concatenate([_ld3(x_ref, 2, prev, d), _ld3(x_ref, 2, start, d)], axis=0).astype(BF16)
                ls = [_ld3(lse_ref, i, start, d) for i in range(npat)]
                w = _softmax_weights(ls)[pi] if npat > 1 else 1.0
                do2 = _stack_heads(w * _ld(do_ref, start, d), lo).astype(BF16)
                dl = w * _ld(dbar_ref, start, d)
                lse2 = jnp.concatenate([ls[pi][:, :1], ls[pi][:, HEAD_DIM:HEAD_DIM + 1]], axis=0)
                dl2 = jnp.concatenate([dl[:, :1], dl[:, HEAD_DIM:HEAD_DIM + 1]], axis=0)
                s = _scores(q2, k2, base, prev_keys, n == 0)
                pr = jnp.exp(s - lse2)
                dp = lax.dot_general(do2, v2, (((1,), (1,)), ((), ())), preferred_element_type=F32)
                ds = (pr * (dp - dl2) * (HEAD_DIM ** -0.5)).astype(BF16)
                dq2 = jnp.dot(ds, k2, preferred_element_type=F32)
                dk2 = lax.dot_general(ds, q2, (((0,), (0,)), ((), ())), preferred_element_type=F32)
                dv2 = lax.dot_general(pr.astype(BF16), do2, (((0,), (0,)), ((), ())), preferred_element_type=F32)
                _acc3(dx_ref, 0, start, d, _unstack_heads(dq2, lo))
                _acc3(dx_ref, 1, prev, d, dk2[:BLOCK])
                _acc3(dx_ref, 1, start, d, dk2[BLOCK:])
                _acc3(dx_ref, 2, prev, d, dv2[:BLOCK])
                _acc3(dx_ref, 2, start, d, dv2[BLOCK:])
                if has_sink:
                    sacc_ref[...] += _unstack_heads(-jnp.exp(sk2 - lse2) * dl2, lo)
                return carry

            lax.fori_loop(0, d * nb, blk, 0, unroll=4)

        dsink_ref[...] = jnp.broadcast_to(jnp.sum(sacc_ref[...], axis=0, keepdims=True), (8, SLAB))

    smem = pl.BlockSpec(memory_space=pltpu.SMEM)
    one = pl.Buffered(1)
    slab3 = pl.BlockSpec((npat, S, SLAB), lambda p: (0, 0, p), pipeline_mode=one)
    return pl.pallas_call(
        body, name=name, grid=(N_SLABS,),
        in_specs=[smem, smem, pl.BlockSpec((3, S, SLAB), lambda p: (0, 0, p), pipeline_mode=one),
                  pl.BlockSpec((S, SLAB), lambda p: (0, p), pipeline_mode=one), slab3, slab3],
        out_specs=[pl.BlockSpec((3, S, SLAB), lambda p: (0, 0, p)), pl.BlockSpec((None, 8, SLAB), lambda p: (p, 0, 0))],
        out_shape=[jax.ShapeDtypeStruct((3, S, D_MODEL), F32), jax.ShapeDtypeStruct((N_SLABS, 8, SLAB), F32)],
        scratch_shapes=[pltpu.VMEM((S, SLAB), F32), pltpu.VMEM((BLOCK, SLAB), F32)],
        compiler_params=_cparams(("arbitrary",)),
    )(slopes, sinks, qkv, dout, o, lse)


def _place():
    x, y, c = lax.axis_index("x"), lax.axis_index("y"), lax.axis_index("c")
    return x, y, c, 2 * x + y


def _other_chips(x, y):
    return [(1 - x, y), (x, 1 - y), (1 - x, 1 - y)]


HBM_SPEC = pl.BlockSpec(memory_space=pl.ANY)


def _slot(q):
    return 2 * (q % 2) + q // 2


BIG = ("ffn1_w_in", "ffn1_w_out", "ffn2_w_in", "ffn2_w_out", "a_w_qkv", "a_w_o", "kv_w", "b_w_q", "b_w_o")
QKV_SHARD = 3 * D_MODEL // N_CHIPS
ROW_SHARD = D_MODEL // N_CHIPS


def _full_shape(name):
    if name.endswith("w_in"):
        return (DEPTH, D_MODEL, 2 * D_FF)
    if name.endswith("w_out"):
        return (DEPTH, D_FF, D_MODEL)
    if name == "a_w_qkv":
        return (D_MODEL, 3 * D_MODEL)
    if name == "kv_w":
        return (N_CHIPS, 2, ROW_SHARD // 2, 2 * N_KV_B * HEAD_DIM)
    return (N_CHIPS, 2, ROW_SHARD // 2, D_MODEL)


def _gather_src(name, ref, c):
    if name.endswith("w_in") or name.endswith("w_out"):
        return ref.at[c]
    if name == "a_w_qkv":
        return ref.at[0, pl.ds(c * (D_MODEL // 2), D_MODEL // 2)]
    if name == "kv_w":
        return ref.at[pl.ds(c * (ROW_SHARD // 2), ROW_SHARD // 2)]
    return ref.at[0, pl.ds(c * (ROW_SHARD // 2), ROW_SHARD // 2)]


def _gather_dst(name, ref, q, c):
    if name.endswith("w_in"):
        return ref.at[c, :, pl.ds(_slot(q) * HALF_FF, HALF_FF)]
    if name.endswith("w_out"):
        return ref.at[c, pl.ds(q * (D_FF // N_CHIPS), D_FF // N_CHIPS)]
    if name == "a_w_qkv":
        return ref.at[pl.ds(c * (D_MODEL // 2), D_MODEL // 2), pl.ds(q * QKV_SHARD, QKV_SHARD)]
    return ref.at[q, c]


def _all_gather(shards, small):
    n = len(BIG)
    r = small.shape[0]
    per = 8

    def body(*refs):
        srcs, small_ref = refs[:n], refs[n]
        dsts, s_ref = refs[n + 1:2 * n + 1], refs[2 * n + 1]
        send_sems, recv_sems = refs[2 * n + 2:]
        x, y, c, myq = _place()
        sibling = (x, y, 1 - c)
        chips = _other_chips(x, y)

        def big(t, k, src, q, h, to):
            return pltpu.make_async_remote_copy(src_ref=src, dst_ref=_gather_dst(BIG[t], dsts[t], q, h),
                                                send_sem=send_sems.at[per * t + k], recv_sem=recv_sems.at[per * t + k],
                                                device_id=to, device_id_type=MESH)

        def tiny(k, q, to):
            return pltpu.make_async_remote_copy(src_ref=small_ref, dst_ref=s_ref.at[q], send_sem=send_sems.at[per * n + k],
                                                recv_sem=recv_sems.at[per * n + k], device_id=to, device_id_type=MESH)

        first = []
        for j, chip in enumerate(chips):
            first += [big(t, j, _gather_src(BIG[t], srcs[t], c), myq, c, (*chip, c)) for t in range(n)]
            first.append(tiny(j, myq, (*chip, c)))
        own = [big(t, 6 + h, _gather_src(BIG[t], srcs[t], h), myq, h, sibling) for t in range(n) for h in (0, 1)]
        own.append(tiny(3, myq, sibling))
        for cp in first + own:
            cp.start()
        passed = []
        for j, (cx, cy) in enumerate(chips):
            q = 2 * cx + cy
            for t in range(n):
                src = _gather_src(BIG[t], srcs[t], c)
                big(t, j, src, q, c, sibling).wait_recv()
                fwd = big(t, 3 + j, _gather_dst(BIG[t], dsts[t], q, c), q, c, sibling)
                fwd.start()
                passed.append(fwd)
        for j, (cx, cy) in enumerate(chips):
            q = 2 * cx + cy
            for t in range(n):
                big(t, 3 + j, _gather_src(BIG[t], srcs[t], c), q, 1 - c, sibling).wait_recv()
            tiny(j, q, sibling).wait_recv()
        for cp in own:
            cp.wait_recv()
        for cp in first + passed + own:
            cp.wait_send()

    outs = pl.pallas_call(
        body, name="all_gather_weights",
        in_specs=[HBM_SPEC] * (n + 1), out_specs=[HBM_SPEC] * (n + 1),
        out_shape=[jax.ShapeDtypeStruct(_full_shape(name), BF16) for name in BIG]
        + [jax.ShapeDtypeStruct((N_CHIPS, r, 128), F32)],
        scratch_shapes=[pltpu.SemaphoreType.DMA((per * n + 4,)), pltpu.SemaphoreType.DMA((per * n + 4,))],
    )(*[shards[name] for name in BIG], small)
    return dict(zip(BIG, outs[:n])), outs[n]


def _small_all_reduce(v):
    r = v.shape[0]

    def body(v_ref, o_ref, buf_ref, send_sems, recv_sems):
        x, y, c, _ = _place()
        me = 4 * x + 2 * y + c
        buf_ref[me] = v_ref[...]
        copies = []
        for k in range(1, 8):
            fx, fy, fc = (k >> 2) & 1, (k >> 1) & 1, k & 1
            to = (x ^ fx, y ^ fy, c ^ fc)
            cp = pltpu.make_async_remote_copy(src_ref=v_ref, dst_ref=buf_ref.at[me], send_sem=send_sems.at[k - 1],
                                              recv_sem=recv_sems.at[k - 1], device_id=to, device_id_type=MESH)
            cp.start()
            copies.append(cp)
        for k in range(1, 8):
            fx, fy, fc = (k >> 2) & 1, (k >> 1) & 1, k & 1
            src_dev = 4 * (x ^ fx) + 2 * (y ^ fy) + (c ^ fc)
            pltpu.make_async_remote_copy(src_ref=v_ref, dst_ref=buf_ref.at[src_dev], send_sem=send_sems.at[k - 1],
                                         recv_sem=recv_sems.at[k - 1], device_id=(x, y, c), device_id_type=MESH).wait_recv()
        for cp in copies:
            cp.wait_send()
        tot = buf_ref[0]
        for i in range(1, 8):
            tot = tot + buf_ref[i]
        o_ref[...] = tot

    vm = pl.BlockSpec(memory_space=pltpu.VMEM)
    return pl.pallas_call(
        body, name="small_all_reduce", in_specs=[vm], out_specs=vm,
        out_shape=jax.ShapeDtypeStruct((r, 128), F32),
        scratch_shapes=[pltpu.VMEM((8, r, 128), F32), pltpu.SemaphoreType.DMA((7,)), pltpu.SemaphoreType.DMA((7,))],
    )(v)


def _grad_view(kind, g):
    if kind == "col":
        return g.reshape(2, g.shape[0] // 2, g.shape[1])
    return g.reshape(N_CHIPS, 2, g.shape[0] // (2 * N_CHIPS), g.shape[1])


def _half_of(kind, ref, h):
    return ref.at[h] if kind == "col" else ref.at[:, h]


def _half_shape(kind, view_shape):
    return view_shape[1:] if kind == "col" else (view_shape[0],) + view_shape[2:]


def _piece_of(kind, width, colblock, ref, q):
    if kind == "col":
        return ref.at[:, pl.ds(colblock(q) * width, width)]
    return ref.at[q]


def _piece_shape(kind, width, half_shape):
    return (half_shape[0], width) if kind == "col" else half_shape[1:]


def _pair_exchange(views, kinds):
    n = len(views)

    def body(*refs):
        ins, outs = refs[:n], refs[n:2 * n]
        send_sems, recv_sems = refs[2 * n:]
        x, y, c, _ = _place()
        cps = []
        for t in range(n):
            cp = pltpu.make_async_remote_copy(src_ref=_half_of(kinds[t], ins[t], 1 - c), dst_ref=outs[t],
                                              send_sem=send_sems.at[t], recv_sem=recv_sems.at[t],
                                              device_id=(x, y, 1 - c), device_id_type=MESH)
            cp.start()
            cps.append(cp)
        for cp in cps:
            cp.wait()

    return pl.pallas_call(
        body, name="grad_pair_exchange", in_specs=[HBM_SPEC] * n, out_specs=[HBM_SPEC] * n,
        out_shape=[jax.ShapeDtypeStruct(_half_shape(k, v.shape), v.dtype) for k, v in zip(kinds, views)],
        scratch_shapes=[pltpu.SemaphoreType.DMA((n,)), pltpu.SemaphoreType.DMA((n,))],
    )(*views)


def _pair_sum(kind, view, recv, c, name):
    hs = recv.shape
    N = hs[-1]
    rows = hs[-2]
    tr = _pick(rows, (512, 352, 128))
    tn = _pick(N, (1408, 1024, 512))

    def body(c_ref, p_ref, r_ref, s_ref):
        s_ref[...] = (p_ref[...] + r_ref[...]).astype(BF16)

    if kind == "col":
        grid = (rows // tr, N // tn)
        mine = pl.BlockSpec((None, tr, tn), lambda i, j, c_ref: (c_ref[0], i, j))
        blk = pl.BlockSpec((tr, tn), lambda i, j, c_ref: (i, j))
        sem = ("parallel", "parallel")
    else:
        grid = (N_CHIPS, rows // tr, N // tn)
        mine = pl.BlockSpec((None, None, tr, tn), lambda q, i, j, c_ref: (q, c_ref[0], i, j))
        blk = pl.BlockSpec((None, tr, tn), lambda q, i, j, c_ref: (q, i, j))
        sem = ("parallel", "parallel", "parallel")
    return pl.pallas_call(
        body, name=name,
        grid_spec=pltpu.PrefetchScalarGridSpec(num_scalar_prefetch=1, grid=grid, in_specs=[mine, blk], out_specs=blk),
        out_shape=jax.ShapeDtypeStruct(hs, BF16),
        compiler_params=_cparams(sem),
    )(c.reshape(1).astype(jnp.int32), view, recv)


def _chip_exchange(sums, kinds, widths, colblocks):
    n = len(sums)

    def body(*refs):
        ins, outs = refs[:n], refs[n:2 * n]
        send_sems, recv_sems = refs[2 * n:]
        x, y, c, _ = _place()
        cps = []
        for j, (cx, cy) in enumerate(_other_chips(x, y)):
            for t in range(n):
                cp = pltpu.make_async_remote_copy(
                    src_ref=_piece_of(kinds[t], widths[t], colblocks[t], ins[t], 2 * cx + cy), dst_ref=outs[t].at[j],
                    send_sem=send_sems.at[3 * t + j], recv_sem=recv_sems.at[3 * t + j],
                    device_id=(cx, cy, c), device_id_type=MESH)
                cp.start()
                cps.append(cp)
        for cp in cps:
            cp.wait()

    return pl.pallas_call(
        body, name="grad_chip_exchange", in_specs=[HBM_SPEC] * n, out_specs=[HBM_SPEC] * n,
        out_shape=[jax.ShapeDtypeStruct((3,) + _piece_shape(k, w, s.shape), BF16) for k, w, s in zip(kinds, widths, sums)],
        scratch_shapes=[pltpu.SemaphoreType.DMA((3 * n,)), pltpu.SemaphoreType.DMA((3 * n,))],
    )(*sums)


def _chip_sum(kind, s, recv, block_idx, c, shard_shape, layer, into, name):
    rows, N = recv.shape[1:]
    tr = _pick(rows, (512, 352, 128))
    tn = _pick(N, (1408, 1024, 768, 512))
    ni, nj = rows // tr, N // tn

    def body(q_ref, s_ref, r_ref, *rest):
        o_ref = rest[-1]
        o_ref[...] = ((s_ref[...].astype(F32) + r_ref[0].astype(F32)) + r_ref[1].astype(F32)) + r_ref[2].astype(F32)

    if kind == "col":
        own = pl.BlockSpec((tr, tn), lambda i, j, q_ref: (i, q_ref[0] * nj + j))
    else:
        own = pl.BlockSpec((None, tr, tn), lambda i, j, q_ref: (q_ref[0], i, j))
    if len(shard_shape) == 3:
        lead = 0 if layer is None else layer
        out_spec = pl.BlockSpec((None, tr, tn), lambda i, j, q_ref: (lead, q_ref[1] * ni + i, j))
    else:
        out_spec = pl.BlockSpec((tr, tn), lambda i, j, q_ref: (q_ref[1] * ni + i, j))
    in_specs = [own, pl.BlockSpec((3, tr, tn), lambda i, j, q_ref: (0, i, j))]
    args = [jnp.stack([block_idx, c]).astype(jnp.int32), s, recv]
    aliases = {}
    if into is not None:
        in_specs.append(HBM_SPEC)
        args.append(into)
        aliases = {3: 0}
    return pl.pallas_call(
        body, name=name,
        grid_spec=pltpu.PrefetchScalarGridSpec(num_scalar_prefetch=1, grid=(ni, nj), in_specs=in_specs, out_specs=out_spec),
        out_shape=jax.ShapeDtypeStruct(shard_shape, F32), input_output_aliases=aliases,
        compiler_params=_cparams(("parallel", "parallel")),
    )(*args)


def _half_window(ref, h):
    rows = ref.shape[-2] // 2
    if ref.ndim == 3:
        return ref.at[:, pl.ds(h * rows, rows)]
    return ref.at[pl.ds(h * rows, rows)]


def _share_halves(grads):
    n = len(grads)

    def body(*refs):
        outs = refs[n:2 * n]
        send_sems, recv_sems = refs[2 * n:]
        x, y, c, _ = _place()
        cps = []
        for t in range(n):
            cp = pltpu.make_async_remote_copy(src_ref=_half_window(outs[t], c), dst_ref=_half_window(outs[t], c),
                                              send_sem=send_sems.at[t], recv_sem=recv_sems.at[t],
                                              device_id=(x, y, 1 - c), device_id_type=MESH)
            cp.start()
            cps.append(cp)
        for t in range(n):
            cps[t].wait_send()
            pltpu.make_async_remote_copy(src_ref=_half_window(outs[t], c), dst_ref=_half_window(outs[t], 1 - c),
                                         send_sem=send_sems.at[t], recv_sem=recv_sems.at[t],
                                         device_id=(x, y, 1 - c), device_id_type=MESH).wait_recv()

    return pl.pallas_call(
        body, name="grad_share_halves", in_specs=[HBM_SPEC] * n, out_specs=[HBM_SPEC] * n,
        out_shape=[jax.ShapeDtypeStruct(g.shape, F32) for g in grads],
        input_output_aliases={t: t for t in range(n)},
        scratch_shapes=[pltpu.SemaphoreType.DMA((n,)), pltpu.SemaphoreType.DMA((n,))],
    )(*grads)


def _adamw(w, g, m, v, name):
    R, W = w.shape
    tr = _pick(R, (512, 352, 256, 32))

    def body(w_ref, g_ref, m_ref, v_ref, d_ref, nm_ref, nv_ref):
        gv = g_ref[...]
        nm = ADAM_B1 * m_ref[...] + (1.0 - ADAM_B1) * gv
        nv = ADAM_B2 * v_ref[...] + (1.0 - ADAM_B2) * (gv * gv)
        m_hat = nm / (1.0 - ADAM_B1 ** ADAM_STEP)
        v_hat = nv / (1.0 - ADAM_B2 ** ADAM_STEP)
        d_ref[...] = -ADAM_LR * (m_hat / (jnp.sqrt(v_hat) + ADAM_EPS) + ADAM_WD * w_ref[...])
        nm_ref[...] = nm
        nv_ref[...] = nv

    blk = pl.BlockSpec((tr, W), lambda i: (i, 0))
    shp = jax.ShapeDtypeStruct((R, W), F32)
    return pl.pallas_call(
        body, name=name, grid=(R // tr,), in_specs=[blk] * 4, out_specs=[blk] * 3, out_shape=[shp] * 3,
        compiler_params=_cparams(("parallel",)),
    )(w, g, m, v)


SMALL_ROWS = 32


def _pack_small(ln_g, ln_b, sinks):
    rows = jnp.concatenate([ln_g.reshape(-1, 128), ln_b.reshape(-1, 128),
                            jnp.pad(sinks.reshape(1, -1), ((0, 0), (0, 128 - sinks.size)))], axis=0)
    return jnp.pad(rows, ((0, SMALL_ROWS - rows.shape[0]), (0, 0)))


def _unpack_small(s, ln_shape, sink_shape):
    n = ln_shape[0] * ln_shape[1] * ln_shape[2] // 128
    return s[:n].reshape(ln_shape), s[n:2 * n].reshape(ln_shape), s[2 * n, :sink_shape[1]].reshape(sink_shape)


def _ffn_fwd(xin, w_in, w_out, gain, bias, tag):
    u, h = _ffn_in(xin, w_in, "ffn_in_" + tag)
    y, yb, z = _mm_ln(h, w_out, xin, gain, bias, 0.5, "ffn_out_ln_" + tag)
    return y, yb, dict(u=u, h=h, z=z, xin=xin)


def _ffn_bwd(dy, saved, w_in, w_out, gain, xin_b, tag):
    dz, dzc, gg, gb = _ln_bwd(saved["z"], dy, gain, 0.5, "ln_bwd_" + tag)
    du = _ffn_bwd_h(dzc, w_out, saved["u"], "ffn_bwd_h_" + tag)
    d_w_out = _mm_tn(saved["h"], dzc, "ffn_dwout_" + tag)
    d_w_in = _mm_tn(xin_b, du, "ffn_dwin_" + tag)
    dx = _mm_nt(du, w_in, "ffn_dx_" + tag, add=dz, add_scale=ALPHA)
    return dx, d_w_in, d_w_out, gg, gb


def kernel(x, ffn1_w_in, ffn1_w_out, ffn2_w_in, ffn2_w_out, ln_g, ln_b, a_w_qkv, a_w_o, kv_w, b_w_q, b_sinks, b_w_o, loss_target, m_ffn1_w_in, m_ffn1_w_out, m_ffn2_w_in, m_ffn2_w_out, m_ln_g, m_ln_b, m_a_w_qkv, m_a_w_o, m_kv_w, m_b_w_q, m_b_sinks, m_b_w_o, v_ffn1_w_in, v_ffn1_w_out, v_ffn2_w_in, v_ffn2_w_out, v_ln_g, v_ln_b, v_a_w_qkv, v_a_w_o, v_kv_w, v_b_w_q, v_b_sinks, v_b_w_o):
    ws = dict(ffn1_w_in=ffn1_w_in, ffn1_w_out=ffn1_w_out, ffn2_w_in=ffn2_w_in, ffn2_w_out=ffn2_w_out, a_w_qkv=a_w_qkv,
              a_w_o=a_w_o, kv_w=kv_w, b_w_q=b_w_q, b_w_o=b_w_o)
    ms = dict(ffn1_w_in=m_ffn1_w_in, ffn1_w_out=m_ffn1_w_out, ffn2_w_in=m_ffn2_w_in, ffn2_w_out=m_ffn2_w_out,
              a_w_qkv=m_a_w_qkv, a_w_o=m_a_w_o, kv_w=m_kv_w, b_w_q=m_b_w_q, b_w_o=m_b_w_o)
    vs = dict(ffn1_w_in=v_ffn1_w_in, ffn1_w_out=v_ffn1_w_out, ffn2_w_in=v_ffn2_w_in, ffn2_w_out=v_ffn2_w_out,
              a_w_qkv=v_a_w_qkv, a_w_o=v_a_w_o, kv_w=v_kv_w, b_w_q=v_b_w_q, b_w_o=v_b_w_o)
    _, _, c_idx, myq = _place()
    xs = x[0]
    target = loss_target[0]

    W, small = _all_gather({n: ws[n].astype(BF16) for n in BIG}, _pack_small(ln_g, ln_b, b_sinks))
    for n in ("a_w_o", "kv_w", "b_w_q", "b_w_o"):
        W[n] = W[n].reshape(D_MODEL, W[n].shape[-1])
    n_ln = ln_g.size // 128
    lg = jnp.concatenate([small[q, :n_ln].reshape(DEPTH, 3, 1, -1) for q in range(N_CHIPS)], axis=-1)
    lb = jnp.concatenate([small[q, n_ln:2 * n_ln].reshape(DEPTH, 3, 1, -1) for q in range(N_CHIPS)], axis=-1)
    sq, grad_x, gr, gg, gb, dsink_part = _local_step(xs, target, W, lg, lb, b_sinks.reshape(N_HEADS))

    loss_row = jnp.pad(jnp.sum(sq).reshape(1, 1), ((0, 0), (0, 127)))
    dsinks = jnp.pad(dsink_part[:, 0, :].reshape(N_SLABS, 2, HEAD_DIM)[:, :, 0].reshape(1, N_HEADS), ((0, 0), (0, 128 - N_HEADS)))
    gg_full = jnp.stack([jnp.stack([jnp.sum(gg[i][j], axis=0) for j in range(3)]) for i in range(DEPTH)])
    gb_full = jnp.stack([jnp.stack([jnp.sum(gb[i][j], axis=0) for j in range(3)]) for i in range(DEPTH)])
    small_in = jnp.concatenate([loss_row, dsinks, gg_full.reshape(-1, 128), gb_full.reshape(-1, 128)], axis=0)
    small_in = jnp.pad(small_in, ((0, (-small_in.shape[0]) % 8), (0, 0)))
    small_sum = _small_all_reduce(small_in)
    loss = small_sum[0, 0] * (0.5 / D_MODEL)
    grad_sinks = small_sum[1, :N_HEADS].reshape(b_sinks.shape)
    n_full = DEPTH * 3 * D_MODEL // 128
    cols = D_MODEL // N_CHIPS
    grad_ln_g = lax.dynamic_slice_in_dim(small_sum[2:2 + n_full].reshape(DEPTH, 3, D_MODEL), myq * cols, cols, axis=2)
    grad_ln_b = lax.dynamic_slice_in_dim(small_sum[2 + n_full:2 + 2 * n_full].reshape(DEPTH, 3, D_MODEL), myq * cols, cols, axis=2)
    return _reduce_and_update(gr, grad_x, loss, grad_ln_g, grad_ln_b, grad_sinks, ws, ms, vs, c_idx, myq,
                              (ln_g, ln_b, b_sinks), (m_ln_g, m_ln_b, m_b_sinks), (v_ln_g, v_ln_b, v_b_sinks))


def _local_step(xs, target, W, lg, lb, sinks):
    S = xs.shape[0]
    slopes = jnp.asarray(_alibi_slopes(N_HEADS))
    in1 = [(W["ffn1_w_in"], i) for i in range(DEPTH)]
    out1 = [(W["ffn1_w_out"], i) for i in range(DEPTH)]
    in2 = [(W["ffn2_w_in"], i) for i in range(DEPTH)]
    out2 = [(W["ffn2_w_out"], i) for i in range(DEPTH)]

    y1, y1b, s1 = _ffn_fwd(xs, in1[0], out1[0], lg[0, 0], lb[0, 0], "a1")
    qkv_a = _mm_nn(y1b, W["a_w_qkv"], F32, "qkv_a", split=True)
    mix_a, o_a, lse_a = _attn_fwd(qkv_a, slopes, None, PATTERNS_A, "attn_a_fwd")
    y2, y2b, z2 = _mm_ln(mix_a, W["a_w_o"], y1, lg[0, 1], lb[0, 1], 1.0, "attn_a_out_ln")
    y3, y3b, s3 = _ffn_fwd(y2, in2[0], out2[0], lg[0, 2], lb[0, 2], "a2")
    kv = _mm_nn(y3b, W["kv_w"], F32, "kv_proj")
    y4, y4b, s4 = _ffn_fwd(y3, in1[1], out1[1], lg[1, 0], lb[1, 0], "b1")
    q_b = _mm_nn(y4b, W["b_w_q"], F32, "q_b")
    k_sh = kv[:, :N_KV_B * HEAD_DIM].reshape(S, N_KV_B, 1, HEAD_DIM)
    v_sh = kv[:, N_KV_B * HEAD_DIM:].reshape(S, N_KV_B, 1, HEAD_DIM)
    k_exp = jnp.broadcast_to(k_sh, (S, N_KV_B, GROUP_B, HEAD_DIM)).reshape(S, D_MODEL)
    v_exp = jnp.broadcast_to(v_sh, (S, N_KV_B, GROUP_B, HEAD_DIM)).reshape(S, D_MODEL)
    qkv_b = jnp.stack([q_b, k_exp, v_exp])
    mix_b, o_b, lse_b = _attn_fwd(qkv_b, slopes, sinks, PATTERNS_B, "attn_b_fwd")
    y5, y5b, z5 = _mm_ln(mix_b, W["b_w_o"], y4, lg[1, 1], lb[1, 1], 1.0, "attn_b_out_ln")
    y6, _, s6 = _ffn_fwd(y5, in2[1], out2[1], lg[1, 2], lb[1, 2], "b2")

    dy6, sq = _loss_grad(y6, target, "loss_grad")
    gr = {n: None for n in BIG}
    gg = [[None] * 3 for _ in range(DEPTH)]
    gb = [[None] * 3 for _ in range(DEPTH)]

    dy5, d_in2_b, d_out2_b, gg[1][2], gb[1][2] = _ffn_bwd(dy6, s6, in2[1], out2[1], lg[1, 2], y5b, "b2")
    dz5, dz5b, gg[1][1], gb[1][1] = _ln_bwd(z5, dy5, lg[1, 1], 1.0, "ln_bwd_attn_b")
    gr["b_w_o"] = _mm_tn(mix_b, dz5b, "d_b_w_o")
    dmix_b = _mm_nt(dz5b, W["b_w_o"], "d_mix_b")
    dqkv_b, dsink_part = _attn_bwd(qkv_b, dmix_b, o_b, lse_b, slopes, sinks, PATTERNS_B, "attn_b_bwd")
    dq_b, dk_exp, dv_exp = (dqkv_b, 0), dqkv_b[1], dqkv_b[2]
    dkv = jnp.concatenate([dk_exp.reshape(S, N_KV_B, GROUP_B, HEAD_DIM).sum(axis=2).reshape(S, -1),
                           dv_exp.reshape(S, N_KV_B, GROUP_B, HEAD_DIM).sum(axis=2).reshape(S, -1)], axis=1)
    gr["b_w_q"] = _mm_tn(y4b, dq_b, "d_b_w_q")
    dy4 = _mm_nt(dq_b, W["b_w_q"], "d_y4", add=dz5, add_scale=ALPHA)
    dy3, d_in1_b, d_out1_b, gg[1][0], gb[1][0] = _ffn_bwd(dy4, s4, in1[1], out1[1], lg[1, 0], y3b, "b1")
    gr["kv_w"] = _mm_tn(y3b, dkv, "d_kv_w")
    dy3 = _mm_nt(dkv, W["kv_w"], "d_y3_kv", add=dy3, add_scale=1.0)

    dy2, d_in2_a, d_out2_a, gg[0][2], gb[0][2] = _ffn_bwd(dy3, s3, in2[0], out2[0], lg[0, 2], y2b, "a2")
    dz2, dz2b, gg[0][1], gb[0][1] = _ln_bwd(z2, dy2, lg[0, 1], 1.0, "ln_bwd_attn_a")
    gr["a_w_o"] = _mm_tn(mix_a, dz2b, "d_a_w_o")
    dmix_a = _mm_nt(dz2b, W["a_w_o"], "d_mix_a")
    dqkv_a, _ = _attn_bwd(qkv_a, dmix_a, o_a, lse_a, slopes, None, PATTERNS_A, "attn_a_bwd")
    gr["a_w_qkv"] = _mm_tn(y1b, dqkv_a, "d_a_w_qkv", split=True)
    dy1 = _mm_nt(dqkv_a, W["a_w_qkv"], "d_y1", add=dz2, add_scale=ALPHA, split=True)
    grad_x, d_in1_a, d_out1_a, gg[0][0], gb[0][0] = _ffn_bwd(dy1, s1, in1[0], out1[0], lg[0, 0], xs, "a1")
    gr["ffn1_w_in"] = [d_in1_a, d_in1_b]
    gr["ffn1_w_out"] = [d_out1_a, d_out1_b]
    gr["ffn2_w_in"] = [d_in2_a, d_in2_b]
    gr["ffn2_w_out"] = [d_out2_a, d_out2_b]
    return sq, grad_x, gr, gg, gb, dsink_part


def _reduce_and_update(gr, grad_x, loss, grad_ln_g, grad_ln_b, grad_sinks, ws, ms, vs, c_idx, myq,
                       small_w, small_m, small_v):
    ln_g, ln_b, b_sinks = small_w
    m_ln_g, m_ln_b, m_b_sinks = small_m
    v_ln_g, v_ln_b, v_b_sinks = small_v

    items = []
    for oi, name in enumerate(BIG):
        if name.endswith("w_in"):
            items += [(gr[name][l], "col", HALF_FF, _slot, (oi, name, l)) for l in range(DEPTH)]
        elif name.endswith("w_out"):
            items += [(gr[name][l], "row", D_MODEL, None, (oi, name, l)) for l in range(DEPTH)]
        elif name == "a_w_qkv":
            items.append((gr[name], "col", QKV_SHARD, lambda q: q, (oi, name, None)))
        else:
            items.append((gr[name], "row", gr[name].shape[1], None, (oi, name, None)))
    kinds = [it[1] for it in items]
    widths = [it[2] for it in items]
    colblocks = [it[3] for it in items]
    views = [_grad_view(k, it[0]) for k, it in zip(kinds, items)]
    from_sibling = _pair_exchange(views, kinds)
    sums = [_pair_sum(k, v, r, c_idx, "pair_sum_%d" % t) for t, (k, v, r) in enumerate(zip(kinds, views, from_sibling))]
    from_chips = _chip_exchange(sums, kinds, widths, colblocks)
    half_done = {name: None for name in BIG}
    for t, (k, cb, s, r, it) in enumerate(zip(kinds, colblocks, sums, from_chips, items)):
        _, name, layer = it[4]
        own = cb(myq) if k == "col" else myq
        half_done[name] = _chip_sum(k, s, r, own, c_idx, ws[name].shape, layer, half_done[name], "chip_sum_%d" % t)
    grads = dict(zip(BIG, _share_halves([half_done[name] for name in BIG])))

    deltas, new_m, new_v = {}, {}, {}
    for name in BIG:
        shp = ws[name].shape
        flat = lambda a: a.reshape(-1, shp[-1])
        d, nm, nv = _adamw(flat(ws[name]), flat(grads[name]), flat(ms[name]), flat(vs[name]), "adamw_" + name)
        deltas[name], new_m[name], new_v[name] = d.reshape(shp), nm.reshape(shp), nv.reshape(shp)
    delta_s, nm_s, nv_s = _adamw(_pack_small(ln_g, ln_b, b_sinks), _pack_small(grad_ln_g, grad_ln_b, grad_sinks),
                                 _pack_small(m_ln_g, m_ln_b, m_b_sinks), _pack_small(v_ln_g, v_ln_b, v_b_sinks), "adamw_small")
    for d, blob in ((grads, None), (deltas, delta_s), (new_m, nm_s), (new_v, nv_s)):
        if blob is None:
            d["ln_g"], d["ln_b"], d["b_sinks"] = grad_ln_g, grad_ln_b, grad_sinks
        else:
            d["ln_g"], d["ln_b"], d["b_sinks"] = _unpack_small(blob, ln_g.shape, b_sinks.shape)

    order = ("ffn1_w_in", "ffn1_w_out", "ffn2_w_in", "ffn2_w_out", "ln_g", "ln_b", "a_w_qkv", "a_w_o", "kv_w", "b_w_q",
             "b_sinks", "b_w_o")
    outs = [loss, grad_x[None]]
    for d in (grads, deltas, new_m, new_v):
        outs += [d[n] for n in order]
    return tuple(outs)
```

```python
import numpy as np
import jax
import jax.numpy as jnp
from jax import lax
from jax.experimental import pallas as pl
from jax.experimental.pallas import tpu as pltpu

F32 = jnp.float32
BF16 = jnp.bfloat16

D_MODEL = 1024
D_FF = 2816
HALF_FF = D_FF // 2
HEAD_DIM = 64
N_HEADS = 16
N_KV_B = 4
GROUP_B = N_HEADS // N_KV_B
DEPTH = 2
ALPHA = (2.0 * DEPTH) ** 0.25
LN_EPS = 1e-5
BLOCK = 128
SLAB = 128
N_SLABS = D_MODEL // SLAB
PATTERNS_A = ((1, 128, 1.0), (4, 128, 4.0), (16, 128, 16.0))
PATTERNS_B = ((1, 127, 1.0),)
NEG = -1e30

ADAM_LR = 0.001
ADAM_B1 = 0.9
ADAM_B2 = 0.999
ADAM_EPS = 1e-08
ADAM_WD = 0.01
ADAM_STEP = 10

N_CHIPS = 4
VMEM_LIMIT = 56 * 1024 * 1024
MESH = pl.DeviceIdType.MESH


def _alibi_slopes(n):
    return np.array([2.0 ** (-8.0 * (h + 1) / n) for h in range(n)], dtype=np.float32)


def _cparams(sem=None, vmem=VMEM_LIMIT):
    return pltpu.CompilerParams(dimension_semantics=sem, vmem_limit_bytes=vmem)


_DIMS = {"nn": ((1,), (0,)), "nt": ((1,), (1,)), "tn": ((0,), (0,))}


def _unlead(x):
    if isinstance(x, tuple):
        return x[0], x[1], x[0].shape[1:]
    return x, None, x.shape


def _bspec(block, imap, lead=None):
    if lead is None:
        return pl.BlockSpec(block, imap)
    return pl.BlockSpec((None,) + tuple(block), lambda *g: (lead,) + tuple(imap(*g)))


def _matmul(a, b, mode, out_dtype, tm, tn, tk, name, add=None, add_scale=1.0, split=False):
    out_spec = pl.BlockSpec((tm, tn), lambda i, j, k: (i, j))
    if mode == "nn":
        a, al, (M, K) = _unlead(a)
        b, bl, (K2, N) = _unlead(b)
        a_spec = _bspec((tm, tk), lambda i, j, k: (i, k), al)
        b_spec = _bspec((tk, tn), lambda i, j, k: (k, j), bl)
        out_struct = jax.ShapeDtypeStruct((M, N), out_dtype)
        if split:
            assert tn == D_MODEL
            out_spec = pl.BlockSpec((None, tm, tn), lambda i, j, k: (j, i, 0))
            out_struct = jax.ShapeDtypeStruct((N // tn, M, tn), out_dtype)
    elif mode == "nt":
        b, bl, (N, K2) = _unlead(b)
        if split:
            assert tk == D_MODEL
            M, K = a.shape[1], a.shape[0] * a.shape[2]
            a_spec = pl.BlockSpec((None, tm, tk), lambda i, j, k: (k, i, 0))
        else:
            a, al, (M, K) = _unlead(a)
            a_spec = _bspec((tm, tk), lambda i, j, k: (i, k), al)
        b_spec = _bspec((tn, tk), lambda i, j, k: (j, k), bl)
        out_struct = jax.ShapeDtypeStruct((M, N), out_dtype)
    else:
        a, al, (K, M) = _unlead(a)
        if split:
            assert tn == D_MODEL
            K2, N = b.shape[1], b.shape[0] * b.shape[2]
            b_spec = pl.BlockSpec((None, tk, tn), lambda i, j, k: (j, k, 0))
        else:
            b, bl, (K2, N) = _unlead(b)
            b_spec = _bspec((tk, tn), lambda i, j, k: (k, j), bl)
        a_spec = _bspec((tk, tm), lambda i, j, k: (k, i), al)
        out_struct = jax.ShapeDtypeStruct((M, N), out_dtype)
    assert K == K2 and M % tm == 0 and N % tn == 0 and K % tk == 0, (a.shape, b.shape, mode, tm, tn, tk)
    nk = K // tk
    dims = (_DIMS[mode], ((), ()))
    has_add = add is not None

    def body(*refs):
        if has_add:
            a_ref, b_ref, add_ref, o_ref, acc_ref = refs
        else:
            a_ref, b_ref, o_ref, acc_ref = refs
        k = pl.program_id(2)
        part = lax.dot_general(a_ref[...].astype(BF16), b_ref[...].astype(BF16), dims, preferred_element_type=F32)

        @pl.when(k == 0)
        def _():
            acc_ref[...] = part

        @pl.when(k > 0)
        def _():
            acc_ref[...] += part

        @pl.when(k == nk - 1)
        def _():
            r = acc_ref[...]
            if has_add:
                r = r + add_scale * add_ref[...]
            o_ref[...] = r.astype(out_dtype)

    in_specs = [a_spec, b_spec]
    args = [a, b]
    if has_add:
        in_specs.append(pl.BlockSpec((tm, tn), lambda i, j, k: (i, j)))
        args.append(add)
    return pl.pallas_call(
        body, name=name, grid=(M // tm, N // tn, nk),
        in_specs=in_specs, out_specs=out_spec, out_shape=out_struct,
        scratch_shapes=[pltpu.VMEM((tm, tn), F32)],
        compiler_params=_cparams(("parallel", "parallel", "arbitrary")),
    )(*args)


def _pick(n, cands):
    for c in cands:
        if n % c == 0:
            return c
    raise ValueError((n, cands))


def _mm_nn(a, b, out_dtype, name, split=False):
    M, K = _unlead(a)[2]
    N = _unlead(b)[2][1]
    return _matmul(a, b, "nn", out_dtype, _pick(M, (1024, 512, 256)), _pick(N, (1024, 512)), _pick(K, (1024, 512)), name,
                   split=split)


def _mm_nt(a, b, name, add=None, add_scale=1.0, split=False):
    M, K = (a.shape[1], D_MODEL) if split else _unlead(a)[2]
    N = _unlead(b)[2][0]
    return _matmul(a, b, "nt", F32, _pick(M, (1024, 512, 256)), _pick(N, (1024, 512)),
                   _pick(K, (1408, 1024, 512)), name, add=add, add_scale=add_scale, split=split)


def _mm_tn(a, b, name, split=False):
    K, M = _unlead(a)[2]
    N = D_MODEL if split else _unlead(b)[2][1]
    return _matmul(a, b, "tn", F32, _pick(M, (1024, 1408, 512)), _pick(N, (1408, 1024, 512)),
                   _pick(K, (1024, 512, 256)), name, split=split)


def _ffn_in(x, w, name):
    S = x.shape[0]
    tm = _pick(S, (512, 256))
    w, wl, _ = _unlead(w)

    def body(x_ref, w_ref, u_ref, h_ref):
        acc = jnp.dot(x_ref[...].astype(BF16), w_ref[...], preferred_element_type=F32)
        g = acc[:, :HALF_FF]
        up = acc[:, HALF_FF:]
        u_ref[...] = acc.astype(BF16)
        h_ref[...] = (g * jax.nn.sigmoid(g) * up).astype(BF16)

    return pl.pallas_call(
        body, name=name, grid=(2, S // tm),
        in_specs=[pl.BlockSpec((tm, D_MODEL), lambda j, i: (i, 0)),
                  _bspec((D_MODEL, D_FF), lambda j, i: (0, j), wl)],
        out_specs=[pl.BlockSpec((tm, D_FF), lambda j, i: (i, j)),
                   pl.BlockSpec((tm, HALF_FF), lambda j, i: (i, j))],
        out_shape=[jax.ShapeDtypeStruct((S, 2 * D_FF), BF16), jax.ShapeDtypeStruct((S, D_FF), BF16)],
        compiler_params=_cparams(("parallel", "parallel")),
    )(x, w)


def _ffn_bwd_h(dzc, w_out, u, name):
    S = dzc.shape[0]
    tm = _pick(S, (512, 256))
    w_out, wl, _ = _unlead(w_out)

    def body(dz_ref, w_ref, u_ref, du_ref):
        dh = lax.dot_general(dz_ref[...], w_ref[...], (((1,), (1,)), ((), ())), preferred_element_type=F32)
        g = u_ref[:, :HALF_FF].astype(F32)
        up = u_ref[:, HALF_FF:].astype(F32)
        sg = jax.nn.sigmoid(g)
        du_ref[:, :HALF_FF] = (dh * up * (sg * (1.0 + g * (1.0 - sg)))).astype(BF16)
        du_ref[:, HALF_FF:] = (dh * (g * sg)).astype(BF16)

    return pl.pallas_call(
        body, name=name, grid=(2, S // tm),
        in_specs=[pl.BlockSpec((tm, D_MODEL), lambda j, i: (i, 0)),
                  _bspec((HALF_FF, D_MODEL), lambda j, i: (j, 0), wl),
                  pl.BlockSpec((tm, D_FF), lambda j, i: (i, j))],
        out_specs=pl.BlockSpec((tm, D_FF), lambda j, i: (i, j)),
        out_shape=jax.ShapeDtypeStruct((S, 2 * D_FF), BF16),
        compiler_params=_cparams(("parallel", "parallel")),
    )(dzc, w_out, u)


def _mm_ln(a, w, resid, gain, bias, c, name):
    S, K = a.shape
    tm = _pick(S, (512, 256))
    tk = _pick(K, (1408, 1024))
    nk = K // tk
    w, wl, _ = _unlead(w)

    def body(a_ref, w_ref, r_ref, g_ref, b_ref, y_ref, yb_ref, z_ref, acc_ref):
        k = pl.program_id(1)
        part = jnp.dot(a_ref[...], w_ref[...], preferred_element_type=F32)

        @pl.when(k == 0)
        def _():
            acc_ref[...] = part

        @pl.when(k > 0)
        def _():
            acc_ref[...] += part

        @pl.when(k == nk - 1)
        def _():
            z = ALPHA * r_ref[...] + c * acc_ref[...]
            mu = jnp.mean(z, axis=-1, keepdims=True)
            zc = z - mu
            var = jnp.mean(zc * zc, axis=-1, keepdims=True)
            y = zc * lax.rsqrt(var + LN_EPS) * g_ref[...] + b_ref[...]
            z_ref[...] = z
            y_ref[...] = y
            yb_ref[...] = y.astype(BF16)

    row = pl.BlockSpec((tm, D_MODEL), lambda i, k: (i, 0))
    vec = pl.BlockSpec((1, D_MODEL), lambda i, k: (0, 0))
    return pl.pallas_call(
        body, name=name, grid=(S // tm, nk),
        in_specs=[pl.BlockSpec((tm, tk), lambda i, k: (i, k)), _bspec((tk, D_MODEL), lambda i, k: (k, 0), wl),
                  row, vec, vec],
        out_specs=[row, row, row],
        out_shape=[jax.ShapeDtypeStruct((S, D_MODEL), F32), jax.ShapeDtypeStruct((S, D_MODEL), BF16),
                   jax.ShapeDtypeStruct((S, D_MODEL), F32)],
        scratch_shapes=[pltpu.VMEM((tm, D_MODEL), F32)],
        compiler_params=_cparams(("parallel", "arbitrary")),
    )(a, w, resid, gain, bias)


def _ln_bwd(z, dy, gain, c, name):
    S = z.shape[0]
    tm = _pick(S, (512, 256))

    def body(z_ref, dy_ref, g_ref, dz_ref, dzc_ref, gg_ref, gb_ref):
        i = pl.program_id(0)
        zv = z_ref[...]
        dyv = dy_ref[...]
        mu = jnp.mean(zv, axis=-1, keepdims=True)
        zc = zv - mu
        var = jnp.mean(zc * zc, axis=-1, keepdims=True)
        rstd = lax.rsqrt(var + LN_EPS)
        xhat = zc * rstd
        dyg = dyv * g_ref[...]
        m1 = jnp.mean(dyg, axis=-1, keepdims=True)
        m2 = jnp.mean(dyg * xhat, axis=-1, keepdims=True)
        dz = rstd * (dyg - m1 - xhat * m2)
        dz_ref[...] = dz
        dzc_ref[...] = (c * dz).astype(BF16)
        pg = jnp.sum((dyv * xhat).reshape(tm // 8, 8, D_MODEL), axis=0)
        pb = jnp.sum(dyv.reshape(tm // 8, 8, D_MODEL), axis=0)

        @pl.when(i == 0)
        def _():
            gg_ref[...] = pg
            gb_ref[...] = pb

        @pl.when(i > 0)
        def _():
            gg_ref[...] += pg
            gb_ref[...] += pb

    row = pl.BlockSpec((tm, D_MODEL), lambda i: (i, 0))
    part = pl.BlockSpec((8, D_MODEL), lambda i: (0, 0))
    return pl.pallas_call(
        body, name=name, grid=(S // tm,),
        in_specs=[row, row, pl.BlockSpec((1, D_MODEL), lambda i: (0, 0))],
        out_specs=[row, row, part, part],
        out_shape=[jax.ShapeDtypeStruct((S, D_MODEL), F32), jax.ShapeDtypeStruct((S, D_MODEL), BF16),
                   jax.ShapeDtypeStruct((8, D_MODEL), F32), jax.ShapeDtypeStruct((8, D_MODEL), F32)],
        compiler_params=_cparams(("arbitrary",)),
    )(z, dy, gain)


def _loss_grad(y, t, name):
    S = y.shape[0]
    tm = _pick(S, (512, 256))

    def body(y_ref, t_ref, dy_ref, sq_ref):
        i = pl.program_id(0)
        e = y_ref[...] - t_ref[...]
        dy_ref[...] = e * (1.0 / D_MODEL)
        ps = jnp.sum((e * e).reshape(tm // 8, 8, D_MODEL), axis=0)

        @pl.when(i == 0)
        def _():
            sq_ref[...] = ps

        @pl.when(i > 0)
        def _():
            sq_ref[...] += ps

    row = pl.BlockSpec((tm, D_MODEL), lambda i: (i, 0))
    return pl.pallas_call(
        body, name=name, grid=(S // tm,),
        in_specs=[row, row], out_specs=[row, pl.BlockSpec((8, D_MODEL), lambda i: (0, 0))],
        out_shape=[jax.ShapeDtypeStruct((S, D_MODEL), F32), jax.ShapeDtypeStruct((8, D_MODEL), F32)],
        compiler_params=_cparams(("arbitrary",)),
    )(y, t)


def _rows(start, d):
    if d == 1:
        return pl.ds(pl.multiple_of(start, BLOCK), BLOCK)
    return pl.ds(start, BLOCK, stride=d)


def _ld(ref, start, d):
    return ref[_rows(start, d), :]


def _ld3(ref, lead, start, d):
    return ref[lead, _rows(start, d), :]


def _st3(ref, lead, start, d, val):
    ref[lead, _rows(start, d), :] = val


def _acc3(ref, lead, start, d, val):
    ref[lead, _rows(start, d), :] = ref[lead, _rows(start, d), :] + val


def _band_consts(slope0, slope1, maxd, scale):
    row = lax.broadcasted_iota(jnp.int32, (2 * BLOCK, 2 * BLOCK), 0)
    kj = lax.broadcasted_iota(jnp.int32, (2 * BLOCK, 2 * BLOCK), 1)
    top = row < BLOCK
    dist = BLOCK + jnp.where(top, row, row - BLOCK) - kj
    slope = jnp.where(top, slope0, slope1)
    base = jnp.where((dist >= 0) & (dist <= maxd), -(slope * (dist.astype(F32) * scale)), NEG)
    return base, kj < BLOCK


def _stack_heads(x, lo):
    return jnp.concatenate([jnp.where(lo, x, 0.0), jnp.where(lo, 0.0, x)], axis=0)


def _unstack_heads(x2, lo):
    return jnp.where(lo, x2[:BLOCK], x2[BLOCK:])


def _scores(q2, k2, base, prev_keys, first):
    s = lax.dot_general(q2, k2, (((1,), (1,)), ((), ())), preferred_element_type=F32) * (HEAD_DIM ** -0.5) + base
    return jnp.where(jnp.logical_and(prev_keys, first), NEG, s)


def _softmax_weights(ls):
    mx = ls[0]
    for l in ls[1:]:
        mx = jnp.maximum(mx, l)
    es = [jnp.exp(l - mx) for l in ls]
    tot = es[0]
    for e in es[1:]:
        tot = tot + e
    inv = 1.0 / tot
    return [e * inv for e in es]


def _attn_fwd(qkv, slopes, sinks, patterns, name):
    S = qkv.shape[1]
    npat = len(patterns)
    has_sink = sinks is not None
    if not has_sink:
        sinks = jnp.zeros((N_HEADS,), F32)
    rows_c = 256

    def body(slopes_ref, sinks_ref, x_ref, mix_ref, o_ref, lse_ref):
        p = pl.program_id(0)
        lo = lax.broadcasted_iota(jnp.int32, (BLOCK, SLAB), 1) < HEAD_DIM
        top1 = lax.broadcasted_iota(jnp.int32, (2 * BLOCK, 1), 0) < BLOCK
        sk2 = jnp.where(top1, sinks_ref[2 * p], sinks_ref[2 * p + 1])
        for pi, (d, maxd, scale) in enumerate(patterns):
            nb = S // d // BLOCK
            base, prev_keys = _band_consts(slopes_ref[2 * p], slopes_ref[2 * p + 1], maxd, scale)

            def blk(t, carry, pi=pi, d=d, nb=nb, base=base, prev_keys=prev_keys):
                r = t // nb
                n = t - r * nb
                start = r + (d * BLOCK) * n
                prev = jnp.where(n > 0, start - d * BLOCK, start)
                q2 = _stack_heads(_ld3(x_ref, 0, start, d), lo).astype(BF16)
                k2 = jnp.concatenate([_ld3(x_ref, 1, prev, d), _ld3(x_ref, 1, start, d)], axis=0).astype(BF16)
                v2 = jnp.concatenate([_ld3(x_ref, 2, prev, d), _ld3(x_ref, 2, start, d)], axis=0).astype(BF16)
                s = _scores(q2, k2, base, prev_keys, n == 0)
                m = jnp.max(s, axis=-1, keepdims=True)
                if has_sink:
                    m = jnp.maximum(m, sk2)
                e = jnp.exp(s - m)
                den = jnp.sum(e, axis=-1, keepdims=True)
                if has_sink:
                    den = den + jnp.exp(sk2 - m)
                o2 = jnp.dot((e / den).astype(BF16), v2, preferred_element_type=F32)
                _st3(o_ref, pi, start, d, _unstack_heads(o2, lo))
                _st3(lse_ref, pi, start, d, _unstack_heads(m + jnp.log(den), lo))
                return carry

            lax.fori_loop(0, d * nb, blk, 0, unroll=8)

        def comb(ci, carry):
            rows = pl.ds(pl.multiple_of(ci * rows_c, rows_c), rows_c)
            if npat == 1:
                mix_ref[rows, :] = o_ref[0, rows, :].astype(BF16)
            else:
                ws = _softmax_weights([lse_ref[i, rows, :] for i in range(npat)])
                acc = ws[0] * o_ref[0, rows, :]
                for i in range(1, npat):
                    acc = acc + ws[i] * o_ref[i, rows, :]
                mix_ref[rows, :] = acc.astype(BF16)
            return carry

        lax.fori_loop(0, S // rows_c, comb, 0)

    smem = pl.BlockSpec(memory_space=pltpu.SMEM)
    slab3 = pl.BlockSpec((npat, S, SLAB), lambda p: (0, 0, p))
    return pl.pallas_call(
        body, name=name, grid=(N_SLABS,),
        in_specs=[smem, smem, pl.BlockSpec((3, S, SLAB), lambda p: (0, 0, p))],
        out_specs=[pl.BlockSpec((S, SLAB), lambda p: (0, p)), slab3, slab3],
        out_shape=[jax.ShapeDtypeStruct((S, D_MODEL), BF16), jax.ShapeDtypeStruct((npat, S, D_MODEL), F32),
                   jax.ShapeDtypeStruct((npat, S, D_MODEL), F32)],
        compiler_params=_cparams(("arbitrary",)),
    )(slopes, sinks, qkv)


def _attn_bwd(qkv, dout, o, lse, slopes, sinks, patterns, name):
    S = qkv.shape[1]
    npat = len(patterns)
    has_sink = sinks is not None
    if not has_sink:
        sinks = jnp.zeros((N_HEADS,), F32)
    rows_c = 256

    def headsum(x, lo):
        s0 = jnp.sum(jnp.where(lo, x, 0.0), axis=-1, keepdims=True)
        s1 = jnp.sum(jnp.where(lo, 0.0, x), axis=-1, keepdims=True)
        return jnp.where(lo, s0, s1)

    def body(slopes_ref, sinks_ref, x_ref, do_ref, o_ref, lse_ref, dx_ref, dsink_ref, dbar_ref, sacc_ref):
        p = pl.program_id(0)
        lo = lax.broadcasted_iota(jnp.int32, (BLOCK, SLAB), 1) < HEAD_DIM
        lo_c = lax.broadcasted_iota(jnp.int32, (rows_c, SLAB), 1) < HEAD_DIM
        top1 = lax.broadcasted_iota(jnp.int32, (2 * BLOCK, 1), 0) < BLOCK
        sk2 = jnp.where(top1, sinks_ref[2 * p], sinks_ref[2 * p + 1])

        def prep(ci, carry):
            rows = pl.ds(pl.multiple_of(ci * rows_c, rows_c), rows_c)
            dov = do_ref[rows, :]
            dx_ref[:, rows, :] = jnp.zeros((3, rows_c, SLAB), F32)
            if npat == 1:
                dbar_ref[rows, :] = headsum(dov * o_ref[0, rows, :], lo_c)
            else:
                ws = _softmax_weights([lse_ref[i, rows, :] for i in range(npat)])
                acc = ws[0] * headsum(dov * o_ref[0, rows, :], lo_c)
                for i in range(1, npat):
                    acc = acc + ws[i] * headsum(dov * o_ref[i, rows, :], lo_c)
                dbar_ref[rows, :] = acc
            return carry

        lax.fori_loop(0, S // rows_c, prep, 0)
        sacc_ref[...] = jnp.zeros((BLOCK, SLAB), F32)

        for pi, (d, maxd, scale) in enumerate(patterns):
            nb = S // d // BLOCK
            base, prev_keys = _band_consts(slopes_ref[2 * p], slopes_ref[2 * p + 1], maxd, scale)

            def blk(t, carry, pi=pi, d=d, nb=nb, base=base, prev_keys=prev_keys):
                r = t // nb
                n = t - r * nb
                start = r + (d * BLOCK) * n
                prev = jnp.where(n > 0, start - d * BLOCK, start)
                q2 = _stack_heads(_ld3(x_ref, 0, start, d), lo).astype(BF16)
                k2 = jnp.concatenate([_ld3(x_ref, 1, prev, d), _ld3(x_ref, 1, start, d)], axis=0).astype(BF16)
                v2 = jnp.concatenate([_ld3(x_ref, 2, prev, d), _ld3(x_ref, 2, start, d)], axis=0).astype(BF16)
                ls = [_ld3(lse_ref, i, start, d) for i in range(npat)]
                w = _softmax_weights(ls)[pi] if npat > 1 else 1.0
                do2 = _stack_heads(w * _ld(do_ref, start, d), lo).astype(BF16)
                dl = w * _ld(dbar_ref, start, d)
                lse2 = jnp.concatenate([ls[pi][:, :1], ls[pi][:, HEAD_DIM:HEAD_DIM + 1]], axis=0)
                dl2 = jnp.concatenate([dl[:, :1], dl[:, HEAD_DIM:HEAD_DIM + 1]], axis=0)
                s = _scores(q2, k2, base, prev_keys, n == 0)
                pr = jnp.exp(s - lse2)
                dp = lax.dot_general(do2, v2, (((1,), (1,)), ((), ())), preferred_element_type=F32)
                ds = (pr * (dp - dl2) * (HEAD_DIM ** -0.5)).astype(BF16)
                dq2 = jnp.dot(ds, k2, preferred_element_type=F32)
                dk2 = lax.dot_general(ds, q2, (((0,), (0,)), ((), ())), preferred_element_type=F32)
                dv2 = lax.dot_general(pr.astype(BF16), do2, (((0,), (0,)), ((), ())), preferred_element_type=F32)
                _acc3(dx_ref, 0, start, d, _unstack_heads(dq2, lo))
                _acc3(dx_ref, 1, prev, d, dk2[:BLOCK])
                _acc3(dx_ref, 1, start, d, dk2[BLOCK:])
                _acc3(dx_ref, 2, prev, d, dv2[:BLOCK])
                _acc3(dx_ref, 2, start, d, dv2[BLOCK:])
                if has_sink:
                    sacc_ref[...] += _unstack_heads(-jnp.exp(sk2 - lse2) * dl2, lo)
                return carry

            lax.fori_loop(0, d * nb, blk, 0, unroll=4)

        dsink_ref[...] = jnp.broadcast_to(jnp.sum(sacc_ref[...], axis=0, keepdims=True), (8, SLAB))

    smem = pl.BlockSpec(memory_space=pltpu.SMEM)
    one = pl.Buffered(1)
    slab3 = pl.BlockSpec((npat, S, SLAB), lambda p: (0, 0, p), pipeline_mode=one)
    return pl.pallas_call(
        body, name=name, grid=(N_SLABS,),
        in_specs=[smem, smem, pl.BlockSpec((3, S, SLAB), lambda p: (0, 0, p), pipeline_mode=one),
                  pl.BlockSpec((S, SLAB), lambda p: (0, p), pipeline_mode=one), slab3, slab3],
        out_specs=[pl.BlockSpec((3, S, SLAB), lambda p: (0, 0, p)), pl.BlockSpec((None, 8, SLAB), lambda p: (p, 0, 0))],
        out_shape=[jax.ShapeDtypeStruct((3, S, D_MODEL), F32), jax.ShapeDtypeStruct((N_SLABS, 8, SLAB), F32)],
        scratch_shapes=[pltpu.VMEM((S, SLAB), F32), pltpu.VMEM((BLOCK, SLAB), F32)],
        compiler_params=_cparams(("arbitrary",)),
    )(slopes, sinks, qkv, dout, o, lse)


def _place():
    x, y, c = lax.axis_index("x"), lax.axis_index("y"), lax.axis_index("c")
    return x, y, c, 2 * x + y


def _other_chips(x, y):
    return [(1 - x, y), (x, 1 - y), (1 - x, 1 - y)]


HBM_SPEC = pl.BlockSpec(memory_space=pl.ANY)


def _slot(q):
    return 2 * (q % 2) + q // 2


BIG = ("ffn1_w_in", "ffn1_w_out", "ffn2_w_in", "ffn2_w_out", "a_w_qkv", "a_w_o", "kv_w", "b_w_q", "b_w_o")
QKV_SHARD = 3 * D_MODEL // N_CHIPS
ROW_SHARD = D_MODEL // N_CHIPS


LAYER0_ITEMS = (("ffn1_w_in", 0), ("ffn1_w_out", 0), ("a_w_qkv", None), ("a_w_o", None), ("ffn2_w_in", 0),
                ("ffn2_w_out", 0), ("kv_w", None))
LAYER1_ITEMS = (("ffn1_w_in", 1), ("ffn1_w_out", 1), ("b_w_q", None), ("b_w_o", None), ("ffn2_w_in", 1),
                ("ffn2_w_out", 1))
OUT_SHARD = D_FF // N_CHIPS


def _full_shape(name):
    if name.endswith("w_in"):
        return (D_MODEL, 2 * D_FF)
    if name.endswith("w_out"):
        return (D_FF, D_MODEL)
    if name == "a_w_qkv":
        return (D_MODEL, 3 * D_MODEL)
    if name == "kv_w":
        return (N_CHIPS, 2, ROW_SHARD // 2, 2 * N_KV_B * HEAD_DIM)
    return (N_CHIPS, 2, ROW_SHARD // 2, D_MODEL)


def _gather_src(item, ref, c):
    name, _ = item
    if name.endswith("w_in"):
        return ref.at[pl.ds(c * (D_MODEL // 2), D_MODEL // 2)]
    if name.endswith("w_out"):
        return ref.at[pl.ds(c * (OUT_SHARD // 2), OUT_SHARD // 2)]
    if name == "a_w_qkv":
        return ref.at[0, pl.ds(c * (D_MODEL // 2), D_MODEL // 2)]
    if name == "kv_w":
        return ref.at[pl.ds(c * (ROW_SHARD // 2), ROW_SHARD // 2)]
    return ref.at[0, pl.ds(c * (ROW_SHARD // 2), ROW_SHARD // 2)]


def _gather_dst(item, ref, q, c):
    name, _ = item
    if name.endswith("w_in"):
        return ref.at[pl.ds(c * (D_MODEL // 2), D_MODEL // 2), pl.ds(_slot(q) * HALF_FF, HALF_FF)]
    if name.endswith("w_out"):
        return ref.at[pl.ds(q * OUT_SHARD + c * (OUT_SHARD // 2), OUT_SHARD // 2)]
    if name == "a_w_qkv":
        return ref.at[pl.ds(c * (D_MODEL // 2), D_MODEL // 2), pl.ds(q * QKV_SHARD, QKV_SHARD)]
    return ref.at[q, c]


def _all_gather(items, shards, small):
    n = len(items)
    r = small.shape[0]
    per = 8

    def body(*refs):
        srcs, small_ref = refs[:n], refs[n]
        dsts, s_ref = refs[n + 1:2 * n + 1], refs[2 * n + 1]
        send_sems, recv_sems = refs[2 * n + 2:]
        x, y, c, myq = _place()
        sibling = (x, y, 1 - c)
        chips = _other_chips(x, y)

        def big(t, k, src, q, h, to):
            return pltpu.make_async_remote_copy(src_ref=src, dst_ref=_gather_dst(items[t], dsts[t], q, h),
                                                send_sem=send_sems.at[per * t + k], recv_sem=recv_sems.at[per * t + k],
                                                device_id=to, device_id_type=MESH)

        def tiny(k, q, to):
            return pltpu.make_async_remote_copy(src_ref=small_ref, dst_ref=s_ref.at[q], send_sem=send_sems.at[per * n + k],
                                                recv_sem=recv_sems.at[per * n + k], device_id=to, device_id_type=MESH)

        first = []
        for j, chip in enumerate(chips):
            first += [big(t, j, _gather_src(items[t], srcs[t], c), myq, c, (*chip, c)) for t in range(n)]
            first.append(tiny(j, myq, (*chip, c)))
        own = [big(t, 6 + h, _gather_src(items[t], srcs[t], h), myq, h, sibling) for t in range(n) for h in (0, 1)]
        own.append(tiny(3, myq, sibling))
        for cp in first + own:
            cp.start()
        passed = []
        for j, (cx, cy) in enumerate(chips):
            q = 2 * cx + cy
            for t in range(n):
                src = _gather_src(items[t], srcs[t], c)
                big(t, j, src, q, c, sibling).wait_recv()
                fwd = big(t, 3 + j, _gather_dst(items[t], dsts[t], q, c), q, c, sibling)
                fwd.start()
                passed.append(fwd)
        for j, (cx, cy) in enumerate(chips):
            q = 2 * cx + cy
            for t in range(n):
                big(t, 3 + j, _gather_src(items[t], srcs[t], c), q, 1 - c, sibling).wait_recv()
            tiny(j, q, sibling).wait_recv()
        for cp in own:
            cp.wait_recv()
        for cp in first + passed + own:
            cp.wait_send()

    outs = pl.pallas_call(
        body, name="all_gather_layer0",
        in_specs=[HBM_SPEC] * (n + 1), out_specs=[HBM_SPEC] * (n + 1),
        out_shape=[jax.ShapeDtypeStruct(_full_shape(name), BF16) for name, _ in items]
        + [jax.ShapeDtypeStruct((N_CHIPS, r, 128), F32)],
        scratch_shapes=[pltpu.SemaphoreType.DMA((per * n + 4,)), pltpu.SemaphoreType.DMA((per * n + 4,))],
    )(*[shards[item] for item in items], small)
    return list(outs[:n]), outs[n]


SEM_SPEC = pl.BlockSpec(memory_space=pltpu.SEMAPHORE)
DATAFLOW = pltpu.SideEffectType.DATAFLOW_SIDE_EFFECTING
PER_ITEM = 8


def _split_copies(items, srcs, lands):
    x, y, c, myq = _place()
    out = []
    for t, item in enumerate(items):
        for h in (0, 1):
            src = _gather_src(item, srcs[t], h)
            for j, (cx, cy) in enumerate(_other_chips(x, y)):
                out.append((src, _gather_dst(item, lands[t], myq, h), _gather_dst(item, lands[t], 2 * cx + cy, h),
                            PER_ITEM * t + 2 * j + h, (cx, cy, c)))
            out.append((src, _gather_dst(item, lands[t], myq, h), _gather_dst(item, lands[t], myq, h),
                        PER_ITEM * t + 6 + h, (x, y, 1 - c)))
    return out


def _gather_start(items, shards, after):
    n = len(items)
    k = PER_ITEM * n

    def body(*refs):
        srcs, lands = refs[:n], refs[n:2 * n]
        send_sems, recv_sems = refs[2 * n + 1], refs[2 * n + 2]
        token = refs[-1]
        for src, dst_there, _, s, peer in _split_copies(items, srcs, lands):
            pltpu.make_async_remote_copy(src_ref=src, dst_ref=dst_there, send_sem=send_sems.at[s], recv_sem=recv_sems.at[s],
                                         device_id=peer, device_id_type=MESH).start()
        token[...] = jnp.zeros_like(token)

    src_arrays = [pltpu.with_memory_space_constraint(shards[item], pltpu.HBM) for item in items]
    land_arrays = [pltpu.with_memory_space_constraint(lax.empty(_full_shape(name), BF16), pltpu.HBM) for name, _ in items]
    outs = pl.pallas_call(
        body, name="gather_layer1_start",
        in_specs=[pl.BlockSpec(memory_space=pltpu.HBM)] * (2 * n) + [HBM_SPEC],
        out_specs=[SEM_SPEC, SEM_SPEC] + [pl.BlockSpec(memory_space=pltpu.HBM)] * (2 * n) + [pl.BlockSpec(memory_space=pltpu.VMEM)],
        out_shape=[pltpu.SemaphoreType.DMA((k,)), pltpu.SemaphoreType.DMA((k,))]
        + [pltpu.HBM(a.shape, a.dtype) for a in src_arrays] + [pltpu.HBM(a.shape, a.dtype) for a in land_arrays]
        + [jax.ShapeDtypeStruct((8, 128), F32)],
        input_output_aliases={i: 2 + i for i in range(2 * n)},
        compiler_params=pltpu.CompilerParams(has_side_effects=DATAFLOW),
    )(*src_arrays, *land_arrays, after)
    return outs[0], outs[1], list(outs[2:2 + n]), list(outs[2 + n:2 + 2 * n]), outs[-1]


def _gather_wait(items, send_sems, recv_sems, srcs_thru, lands_thru, after):
    n = len(items)

    def body(*refs):
        srcs, lands = refs[:n], refs[n:2 * n]
        send_sems, recv_sems = refs[2 * n], refs[2 * n + 1]
        for src, _, dst_here, s, peer in _split_copies(items, srcs, lands):
            cp = pltpu.make_async_remote_copy(src_ref=src, dst_ref=dst_here, send_sem=send_sems.at[s], recv_sem=recv_sems.at[s],
                                              device_id=peer, device_id_type=MESH)
            cp.wait_send()
            cp.wait_recv()

    outs = pl.pallas_call(
        body, name="gather_layer1_wait",
        in_specs=[pl.BlockSpec(memory_space=pltpu.HBM)] * (2 * n) + [SEM_SPEC, SEM_SPEC, HBM_SPEC],
        out_specs=[pl.BlockSpec(memory_space=pltpu.HBM)] * (2 * n),
        out_shape=[pltpu.HBM(a.shape, a.dtype) for a in srcs_thru] + [pltpu.HBM(a.shape, a.dtype) for a in lands_thru],
        input_output_aliases={i: i for i in range(2 * n)},
        compiler_params=pltpu.CompilerParams(has_side_effects=DATAFLOW),
    )(*srcs_thru, *lands_thru, send_sems, recv_sems, after)
    return list(outs[n:])


def _small_all_reduce(v):
    r = v.shape[0]

    def body(v_ref, o_ref, buf_ref, send_sems, recv_sems):
        x, y, c, _ = _place()
        me = 4 * x + 2 * y + c
        buf_ref[me] = v_ref[...]
        copies = []
        for k in range(1, 8):
            fx, fy, fc = (k >> 2) & 1, (k >> 1) & 1, k & 1
            to = (x ^ fx, y ^ fy, c ^ fc)
            cp = pltpu.make_async_remote_copy(src_ref=v_ref, dst_ref=buf_ref.at[me], send_sem=send_sems.at[k - 1],
                                              recv_sem=recv_sems.at[k - 1], device_id=to, device_id_type=MESH)
            cp.start()
            copies.append(cp)
        for k in range(1, 8):
            fx, fy, fc = (k >> 2) & 1, (k >> 1) & 1, k & 1
            src_dev = 4 * (x ^ fx) + 2 * (y ^ fy) + (c ^ fc)
            pltpu.make_async_remote_copy(src_ref=v_ref, dst_ref=buf_ref.at[src_dev], send_sem=send_sems.at[k - 1],
                                         recv_sem=recv_sems.at[k - 1], device_id=(x, y, c), device_id_type=MESH).wait_recv()
        for cp in copies:
            cp.wait_send()
        tot = buf_ref[0]
        for i in range(1, 8):
            tot = tot + buf_ref[i]
        o_ref[...] = tot

    vm = pl.BlockSpec(memory_space=pltpu.VMEM)
    return pl.pallas_call(
        body, name="small_all_reduce", in_specs=[vm], out_specs=vm,
        out_shape=jax.ShapeDtypeStruct((r, 128), F32),
        scratch_shapes=[pltpu.VMEM((8, r, 128), F32), pltpu.SemaphoreType.DMA((7,)), pltpu.SemaphoreType.DMA((7,))],
    )(v)


def _grad_view(kind, g):
    if kind == "col":
        return g.reshape(2, g.shape[0] // 2, g.shape[1])
    return g.reshape(N_CHIPS, 2, g.shape[0] // (2 * N_CHIPS), g.shape[1])


def _half_of(kind, ref, h):
    return ref.at[h] if kind == "col" else ref.at[:, h]


def _half_shape(kind, view_shape):
    return view_shape[1:] if kind == "col" else (view_shape[0],) + view_shape[2:]


def _piece_of(kind, width, colblock, ref, q):
    if kind == "col":
        return ref.at[:, pl.ds(colblock(q) * width, width)]
    return ref.at[q]


def _piece_shape(kind, width, half_shape):
    return (half_shape[0], width) if kind == "col" else half_shape[1:]


def _pair_exchange(views, kinds):
    n = len(views)

    def body(*refs):
        ins, outs = refs[:n], refs[n:2 * n]
        send_sems, recv_sems = refs[2 * n:]
        x, y, c, _ = _place()
        cps = []
        for t in range(n):
            cp = pltpu.make_async_remote_copy(src_ref=_half_of(kinds[t], ins[t], 1 - c), dst_ref=outs[t],
                                              send_sem=send_sems.at[t], recv_sem=recv_sems.at[t],
                                              device_id=(x, y, 1 - c), device_id_type=MESH)
            cp.start()
            cps.append(cp)
        for cp in cps:
            cp.wait()

    return pl.pallas_call(
        body, name="grad_pair_exchange", in_specs=[HBM_SPEC] * n, out_specs=[HBM_SPEC] * n,
        out_shape=[jax.ShapeDtypeStruct(_half_shape(k, v.shape), v.dtype) for k, v in zip(kinds, views)],
        scratch_shapes=[pltpu.SemaphoreType.DMA((n,)), pltpu.SemaphoreType.DMA((n,))],
    )(*views)


def _pair_sum(kind, view, recv, c, name):
    hs = recv.shape
    N = hs[-1]
    rows = hs[-2]
    tr = _pick(rows, (512, 352, 128))
    tn = _pick(N, (1408, 1024, 512))

    def body(c_ref, p_ref, r_ref, s_ref):
        s_ref[...] = (p_ref[...] + r_ref[...]).astype(BF16)

    if kind == "col":
        grid = (rows // tr, N // tn)
        mine = pl.BlockSpec((None, tr, tn), lambda i, j, c_ref: (c_ref[0], i, j))
        blk = pl.BlockSpec((tr, tn), lambda i, j, c_ref: (i, j))
        sem = ("parallel", "parallel")
    else:
        grid = (N_CHIPS, rows // tr, N // tn)
        mine = pl.BlockSpec((None, None, tr, tn), lambda q, i, j, c_ref: (q, c_ref[0], i, j))
        blk = pl.BlockSpec((None, tr, tn), lambda q, i, j, c_ref: (q, i, j))
        sem = ("parallel", "parallel", "parallel")
    return pl.pallas_call(
        body, name=name,
        grid_spec=pltpu.PrefetchScalarGridSpec(num_scalar_prefetch=1, grid=grid, in_specs=[mine, blk], out_specs=blk),
        out_shape=jax.ShapeDtypeStruct(hs, BF16),
        compiler_params=_cparams(sem),
    )(c.reshape(1).astype(jnp.int32), view, recv)


def _chip_exchange(sums, kinds, widths, colblocks):
    n = len(sums)

    def body(*refs):
        ins, outs = refs[:n], refs[n:2 * n]
        send_sems, recv_sems = refs[2 * n:]
        x, y, c, _ = _place()
        cps = []
        for j, (cx, cy) in enumerate(_other_chips(x, y)):
            for t in range(n):
                cp = pltpu.make_async_remote_copy(
                    src_ref=_piece_of(kinds[t], widths[t], colblocks[t], ins[t], 2 * cx + cy), dst_ref=outs[t].at[j],
                    send_sem=send_sems.at[3 * t + j], recv_sem=recv_sems.at[3 * t + j],
                    device_id=(cx, cy, c), device_id_type=MESH)
                cp.start()
                cps.append(cp)
        for cp in cps:
            cp.wait()

    return pl.pallas_call(
        body, name="grad_chip_exchange", in_specs=[HBM_SPEC] * n, out_specs=[HBM_SPEC] * n,
        out_shape=[jax.ShapeDtypeStruct((3,) + _piece_shape(k, w, s.shape), BF16) for k, w, s in zip(kinds, widths, sums)],
        scratch_shapes=[pltpu.SemaphoreType.DMA((3 * n,)), pltpu.SemaphoreType.DMA((3 * n,))],
    )(*sums)


def _chip_sum(kind, s, recv, block_idx, c, shard_shape, layer, into, name):
    rows, N = recv.shape[1:]
    tr = _pick(rows, (512, 352, 128))
    tn = _pick(N, (1408, 1024, 768, 512))
    ni, nj = rows // tr, N // tn

    def body(q_ref, s_ref, r_ref, *rest):
        o_ref = rest[-1]
        o_ref[...] = ((s_ref[...].astype(F32) + r_ref[0].astype(F32)) + r_ref[1].astype(F32)) + r_ref[2].astype(F32)

    if kind == "col":
        own = pl.BlockSpec((tr, tn), lambda i, j, q_ref: (i, q_ref[0] * nj + j))
    else:
        own = pl.BlockSpec((None, tr, tn), lambda i, j, q_ref: (q_ref[0], i, j))
    if len(shard_shape) == 3:
        lead = 0 if layer is None else layer
        out_spec = pl.BlockSpec((None, tr, tn), lambda i, j, q_ref: (lead, q_ref[1] * ni + i, j))
    else:
        out_spec = pl.BlockSpec((tr, tn), lambda i, j, q_ref: (q_ref[1] * ni + i, j))
    in_specs = [own, pl.BlockSpec((3, tr, tn), lambda i, j, q_ref: (0, i, j))]
    args = [jnp.stack([block_idx, c]).astype(jnp.int32), s, recv]
    aliases = {}
    if into is not None:
        in_specs.append(HBM_SPEC)
        args.append(into)
        aliases = {3: 0}
    return pl.pallas_call(
        body, name=name,
        grid_spec=pltpu.PrefetchScalarGridSpec(num_scalar_prefetch=1, grid=(ni, nj), in_specs=in_specs, out_specs=out_spec),
        out_shape=jax.ShapeDtypeStruct(shard_shape, F32), input_output_aliases=aliases,
        compiler_params=_cparams(("parallel", "parallel")),
    )(*args)


def _half_window(ref, h):
    rows = ref.shape[-2] // 2
    if ref.ndim == 3:
        return ref.at[:, pl.ds(h * rows, rows)]
    return ref.at[pl.ds(h * rows, rows)]


def _share_halves(grads):
    n = len(grads)

    def body(*refs):
        outs = refs[n:2 * n]
        send_sems, recv_sems = refs[2 * n:]
        x, y, c, _ = _place()
        cps = []
        for t in range(n):
            cp = pltpu.make_async_remote_copy(src_ref=_half_window(outs[t], c), dst_ref=_half_window(outs[t], c),
                                              send_sem=send_sems.at[t], recv_sem=recv_sems.at[t],
                                              device_id=(x, y, 1 - c), device_id_type=MESH)
            cp.start()
            cps.append(cp)
        for t in range(n):
            cps[t].wait_send()
            pltpu.make_async_remote_copy(src_ref=_half_window(outs[t], c), dst_ref=_half_window(outs[t], 1 - c),
                                         send_sem=send_sems.at[t], recv_sem=recv_sems.at[t],
                                         device_id=(x, y, 1 - c), device_id_type=MESH).wait_recv()

    return pl.pallas_call(
        body, name="grad_share_halves", in_specs=[HBM_SPEC] * n, out_specs=[HBM_SPEC] * n,
        out_shape=[jax.ShapeDtypeStruct(g.shape, F32) for g in grads],
        input_output_aliases={t: t for t in range(n)},
        scratch_shapes=[pltpu.SemaphoreType.DMA((n,)), pltpu.SemaphoreType.DMA((n,))],
    )(*grads)


def _adamw(w, g, m, v, name):
    R, W = w.shape
    tr = _pick(R, (512, 352, 256, 32))

    def body(w_ref, g_ref, m_ref, v_ref, d_ref, nm_ref, nv_ref):
        gv = g_ref[...]
        nm = ADAM_B1 * m_ref[...] + (1.0 - ADAM_B1) * gv
        nv = ADAM_B2 * v_ref[...] + (1.0 - ADAM_B2) * (gv * gv)
        m_hat = nm / (1.0 - ADAM_B1 ** ADAM_STEP)
        v_hat = nv / (1.0 - ADAM_B2 ** ADAM_STEP)
        d_ref[...] = -ADAM_LR * (m_hat / (jnp.sqrt(v_hat) + ADAM_EPS) + ADAM_WD * w_ref[...])
        nm_ref[...] = nm
        nv_ref[...] = nv

    blk = pl.BlockSpec((tr, W), lambda i: (i, 0))
    shp = jax.ShapeDtypeStruct((R, W), F32)
    return pl.pallas_call(
        body, name=name, grid=(R // tr,), in_specs=[blk] * 4, out_specs=[blk] * 3, out_shape=[shp] * 3,
        compiler_params=_cparams(("parallel",)),
    )(w, g, m, v)


SMALL_ROWS = 32


def _pack_small(ln_g, ln_b, sinks):
    rows = jnp.concatenate([ln_g.reshape(-1, 128), ln_b.reshape(-1, 128),
                            jnp.pad(sinks.reshape(1, -1), ((0, 0), (0, 128 - sinks.size)))], axis=0)
    return jnp.pad(rows, ((0, SMALL_ROWS - rows.shape[0]), (0, 0)))


def _unpack_small(s, ln_shape, sink_shape):
    n = ln_shape[0] * ln_shape[1] * ln_shape[2] // 128
    return s[:n].reshape(ln_shape), s[n:2 * n].reshape(ln_shape), s[2 * n, :sink_shape[1]].reshape(sink_shape)


def _ffn_fwd(xin, w_in, w_out, gain, bias, tag):
    u, h = _ffn_in(xin, w_in, "ffn_in_" + tag)
    y, yb, z = _mm_ln(h, w_out, xin, gain, bias, 0.5, "ffn_out_ln_" + tag)
    return y, yb, dict(u=u, h=h, z=z, xin=xin)


def _ffn_bwd(dy, saved, w_in, w_out, gain, xin_b, tag):
    dz, dzc, gg, gb = _ln_bwd(saved["z"], dy, gain, 0.5, "ln_bwd_" + tag)
    du = _ffn_bwd_h(dzc, w_out, saved["u"], "ffn_bwd_h_" + tag)
    d_w_out = _mm_tn(saved["h"], dzc, "ffn_dwout_" + tag)
    d_w_in = _mm_tn(xin_b, du, "ffn_dwin_" + tag)
    dx = _mm_nt(du, w_in, "ffn_dx_" + tag, add=dz, add_scale=ALPHA)
    return dx, d_w_in, d_w_out, gg, gb


def kernel(x, ffn1_w_in, ffn1_w_out, ffn2_w_in, ffn2_w_out, ln_g, ln_b, a_w_qkv, a_w_o, kv_w, b_w_q, b_sinks, b_w_o, loss_target, m_ffn1_w_in, m_ffn1_w_out, m_ffn2_w_in, m_ffn2_w_out, m_ln_g, m_ln_b, m_a_w_qkv, m_a_w_o, m_kv_w, m_b_w_q, m_b_sinks, m_b_w_o, v_ffn1_w_in, v_ffn1_w_out, v_ffn2_w_in, v_ffn2_w_out, v_ln_g, v_ln_b, v_a_w_qkv, v_a_w_o, v_kv_w, v_b_w_q, v_b_sinks, v_b_w_o):
    ws = dict(ffn1_w_in=ffn1_w_in, ffn1_w_out=ffn1_w_out, ffn2_w_in=ffn2_w_in, ffn2_w_out=ffn2_w_out, a_w_qkv=a_w_qkv,
              a_w_o=a_w_o, kv_w=kv_w, b_w_q=b_w_q, b_w_o=b_w_o)
    ms = dict(ffn1_w_in=m_ffn1_w_in, ffn1_w_out=m_ffn1_w_out, ffn2_w_in=m_ffn2_w_in, ffn2_w_out=m_ffn2_w_out,
              a_w_qkv=m_a_w_qkv, a_w_o=m_a_w_o, kv_w=m_kv_w, b_w_q=m_b_w_q, b_w_o=m_b_w_o)
    vs = dict(ffn1_w_in=v_ffn1_w_in, ffn1_w_out=v_ffn1_w_out, ffn2_w_in=v_ffn2_w_in, ffn2_w_out=v_ffn2_w_out,
              a_w_qkv=v_a_w_qkv, a_w_o=v_a_w_o, kv_w=v_kv_w, b_w_q=v_b_w_q, b_w_o=v_b_w_o)
    _, _, c_idx, myq = _place()
    xs = x[0]
    target = loss_target[0]

    shards = {(n, l): (ws[n] if l is None else ws[n][l]).astype(BF16) for n, l in LAYER0_ITEMS + LAYER1_ITEMS}

    def as_weights(items, arrays):
        return {n: (a.reshape(D_MODEL, a.shape[-1]) if a.ndim == 4 else a) for (n, _), a in zip(items, arrays)}

    full0, small = _all_gather(LAYER0_ITEMS, shards, _pack_small(ln_g, ln_b, b_sinks))
    send_sems, recv_sems, srcs_thru, lands_thru, token = _gather_start(LAYER1_ITEMS, shards, small)

    def layer1_weights(after):
        return as_weights(LAYER1_ITEMS, _gather_wait(LAYER1_ITEMS, send_sems, recv_sems, srcs_thru, lands_thru, after))

    n_ln = ln_g.size // 128
    lg = jnp.concatenate([small[q, :n_ln].reshape(DEPTH, 3, 1, -1) for q in range(N_CHIPS)], axis=-1)
    lb = jnp.concatenate([small[q, n_ln:2 * n_ln].reshape(DEPTH, 3, 1, -1) for q in range(N_CHIPS)], axis=-1)
    lg = lg + token[0, 0]
    sq, grad_x, gr, gg, gb, dsink_part = _local_step(xs, target, as_weights(LAYER0_ITEMS, full0), layer1_weights,
                                                     lg, lb, b_sinks.reshape(N_HEADS))

    loss_row = jnp.pad(jnp.sum(sq).reshape(1, 1), ((0, 0), (0, 127)))
    dsinks = jnp.pad(dsink_part[:, 0, :].reshape(N_SLABS, 2, HEAD_DIM)[:, :, 0].reshape(1, N_HEADS), ((0, 0), (0, 128 - N_HEADS)))
    gg_full = jnp.stack([jnp.stack([jnp.sum(gg[i][j], axis=0) for j in range(3)]) for i in range(DEPTH)])
    gb_full = jnp.stack([jnp.stack([jnp.sum(gb[i][j], axis=0) for j in range(3)]) for i in range(DEPTH)])
    small_in = jnp.concatenate([loss_row, dsinks, gg_full.reshape(-1, 128), gb_full.reshape(-1, 128)], axis=0)
    small_in = jnp.pad(small_in, ((0, (-small_in.shape[0]) % 8), (0, 0)))
    small_sum = _small_all_reduce(small_in)
    loss = small_sum[0, 0] * (0.5 / D_MODEL)
    grad_sinks = small_sum[1, :N_HEADS].reshape(b_sinks.shape)
    n_full = DEPTH * 3 * D_MODEL // 128
    cols = D_MODEL // N_CHIPS
    grad_ln_g = lax.dynamic_slice_in_dim(small_sum[2:2 + n_full].reshape(DEPTH, 3, D_MODEL), myq * cols, cols, axis=2)
    grad_ln_b = lax.dynamic_slice_in_dim(small_sum[2 + n_full:2 + 2 * n_full].reshape(DEPTH, 3, D_MODEL), myq * cols, cols, axis=2)
    return _reduce_and_update(gr, grad_x, loss, grad_ln_g, grad_ln_b, grad_sinks, ws, ms, vs, c_idx, myq,
                              (ln_g, ln_b, b_sinks), (m_ln_g, m_ln_b, m_b_sinks), (v_ln_g, v_ln_b, v_b_sinks))


def _local_step(xs, target, W, layer1_weights, lg, lb, sinks):
    S = xs.shape[0]
    slopes = jnp.asarray(_alibi_slopes(N_HEADS))
    in1, out1, in2, out2 = [W["ffn1_w_in"]], [W["ffn1_w_out"]], [W["ffn2_w_in"]], [W["ffn2_w_out"]]

    y1, y1b, s1 = _ffn_fwd(xs, in1[0], out1[0], lg[0, 0], lb[0, 0], "a1")
    qkv_a = _mm_nn(y1b, W["a_w_qkv"], F32, "qkv_a", split=True)
    mix_a, o_a, lse_a = _attn_fwd(qkv_a, slopes, None, PATTERNS_A, "attn_a_fwd")
    y2, y2b, z2 = _mm_ln(mix_a, W["a_w_o"], y1, lg[0, 1], lb[0, 1], 1.0, "attn_a_out_ln")
    y3, y3b, s3 = _ffn_fwd(y2, in2[0], out2[0], lg[0, 2], lb[0, 2], "a2")
    kv = _mm_nn(y3b, W["kv_w"], F32, "kv_proj")
    W = dict(W, **layer1_weights(kv))
    in1, out1, in2, out2 = (in1 + [W["ffn1_w_in"]], out1 + [W["ffn1_w_out"]], in2 + [W["ffn2_w_in"]],
                            out2 + [W["ffn2_w_out"]])
    y4, y4b, s4 = _ffn_fwd(y3, in1[1], out1[1], lg[1, 0], lb[1, 0], "b1")
    q_b = _mm_nn(y4b, W["b_w_q"], F32, "q_b")
    k_sh = kv[:, :N_KV_B * HEAD_DIM].reshape(S, N_KV_B, 1, HEAD_DIM)
    v_sh = kv[:, N_KV_B * HEAD_DIM:].reshape(S, N_KV_B, 1, HEAD_DIM)
    k_exp = jnp.broadcast_to(k_sh, (S, N_KV_B, GROUP_B, HEAD_DIM)).reshape(S, D_MODEL)
    v_exp = jnp.broadcast_to(v_sh, (S, N_KV_B, GROUP_B, HEAD_DIM)).reshape(S, D_MODEL)
    qkv_b = jnp.stack([q_b, k_exp, v_exp])
    mix_b, o_b, lse_b = _attn_fwd(qkv_b, slopes, sinks, PATTERNS_B, "attn_b_fwd")
    y5, y5b, z5 = _mm_ln(mix_b, W["b_w_o"], y4, lg[1, 1], lb[1, 1], 1.0, "attn_b_out_ln")
    y6, _, s6 = _ffn_fwd(y5, in2[1], out2[1], lg[1, 2], lb[1, 2], "b2")

    dy6, sq = _loss_grad(y6, target, "loss_grad")
    gr = {n: None for n in BIG}
    gg = [[None] * 3 for _ in range(DEPTH)]
    gb = [[None] * 3 for _ in range(DEPTH)]

    dy5, d_in2_b, d_out2_b, gg[1][2], gb[1][2] = _ffn_bwd(dy6, s6, in2[1], out2[1], lg[1, 2], y5b, "b2")
    dz5, dz5b, gg[1][1], gb[1][1] = _ln_bwd(z5, dy5, lg[1, 1], 1.0, "ln_bwd_attn_b")
    gr["b_w_o"] = _mm_tn(mix_b, dz5b, "d_b_w_o")
    dmix_b = _mm_nt(dz5b, W["b_w_o"], "d_mix_b")
    dqkv_b, dsink_part = _attn_bwd(qkv_b, dmix_b, o_b, lse_b, slopes, sinks, PATTERNS_B, "attn_b_bwd")
    dq_b, dk_exp, dv_exp = (dqkv_b, 0), dqkv_b[1], dqkv_b[2]
    dkv = jnp.concatenate([dk_exp.reshape(S, N_KV_B, GROUP_B, HEAD_DIM).sum(axis=2).reshape(S, -1),
                           dv_exp.reshape(S, N_KV_B, GROUP_B, HEAD_DIM).sum(axis=2).reshape(S, -1)], axis=1)
    gr["b_w_q"] = _mm_tn(y4b, dq_b, "d_b_w_q")
    dy4 = _mm_nt(dq_b, W["b_w_q"], "d_y4", add=dz5, add_scale=ALPHA)
    dy3, d_in1_b, d_out1_b, gg[1][0], gb[1][0] = _ffn_bwd(dy4, s4, in1[1], out1[1], lg[1, 0], y3b, "b1")
    gr["kv_w"] = _mm_tn(y3b, dkv, "d_kv_w")
    dy3 = _mm_nt(dkv, W["kv_w"], "d_y3_kv", add=dy3, add_scale=1.0)

    dy2, d_in2_a, d_out2_a, gg[0][2], gb[0][2] = _ffn_bwd(dy3, s3, in2[0], out2[0], lg[0, 2], y2b, "a2")
    dz2, dz2b, gg[0][1], gb[0][1] = _ln_bwd(z2, dy2, lg[0, 1], 1.0, "ln_bwd_attn_a")
    gr["a_w_o"] = _mm_tn(mix_a, dz2b, "d_a_w_o")
    dmix_a = _mm_nt(dz2b, W["a_w_o"], "d_mix_a")
    dqkv_a, _ = _attn_bwd(qkv_a, dmix_a, o_a, lse_a, slopes, None, PATTERNS_A, "attn_a_bwd")
    gr["a_w_qkv"] = _mm_tn(y1b, dqkv_a, "d_a_w_qkv", split=True)
    dy1 = _mm_nt(dqkv_a, W["a_w_qkv"], "d_y1", add=dz2, add_scale=ALPHA, split=True)
    grad_x, d_in1_a, d_out1_a, gg[0][0], gb[0][0] = _ffn_bwd(dy1, s1, in1[0], out1[0], lg[0, 0], xs, "a1")
    gr["ffn1_w_in"] = [d_in1_a, d_in1_b]
    gr["ffn1_w_out"] = [d_out1_a, d_out1_b]
    gr["ffn2_w_in"] = [d_in2_a, d_in2_b]
    gr["ffn2_w_out"] = [d_out2_a, d_out2_b]
    return sq, grad_x, gr, gg, gb, dsink_part


def _reduce_and_update(gr, grad_x, loss, grad_ln_g, grad_ln_b, grad_sinks, ws, ms, vs, c_idx, myq,
                       small_w, small_m, small_v):
    ln_g, ln_b, b_sinks = small_w
    m_ln_g, m_ln_b, m_b_sinks = small_m
    v_ln_g, v_ln_b, v_b_sinks = small_v

    items = []
    for oi, name in enumerate(BIG):
        if name.endswith("w_in"):
            items += [(gr[name][l], "col", HALF_FF, _slot, (oi, name, l)) for l in range(DEPTH)]
        elif name.endswith("w_out"):
            items += [(gr[name][l], "row", D_MODEL, None, (oi, name, l)) for l in range(DEPTH)]
        elif name == "a_w_qkv":
            items.append((gr[name], "col", QKV_SHARD, lambda q: q, (oi, name, None)))
        else:
            items.append((gr[name], "row", gr[name].shape[1], None, (oi, name, None)))
    kinds = [it[1] for it in items]
    widths = [it[2] for it in items]
    colblocks = [it[3] for it in items]
    views = [_grad_view(k, it[0]) for k, it in zip(kinds, items)]
    from_sibling = _pair_exchange(views, kinds)
    sums = [_pair_sum(k, v, r, c_idx, "pair_sum_%d" % t) for t, (k, v, r) in enumerate(zip(kinds, views, from_sibling))]
    from_chips = _chip_exchange(sums, kinds, widths, colblocks)
    half_done = {name: None for name in BIG}
    for t, (k, cb, s, r, it) in enumerate(zip(kinds, colblocks, sums, from_chips, items)):
        _, name, layer = it[4]
        own = cb(myq) if k == "col" else myq
        half_done[name] = _chip_sum(k, s, r, own, c_idx, ws[name].shape, layer, half_done[name], "chip_sum_%d" % t)
    grads = dict(zip(BIG, _share_halves([half_done[name] for name in BIG])))

    deltas, new_m, new_v = {}, {}, {}
    for name in BIG:
        shp = ws[name].shape
        flat = lambda a: a.reshape(-1, shp[-1])
        d, nm, nv = _adamw(flat(ws[name]), flat(grads[name]), flat(ms[name]), flat(vs[name]), "adamw_" + name)
        deltas[name], new_m[name], new_v[name] = d.reshape(shp), nm.reshape(shp), nv.reshape(shp)
    delta_s, nm_s, nv_s = _adamw(_pack_small(ln_g, ln_b, b_sinks), _pack_small(grad_ln_g, grad_ln_b, grad_sinks),
                                 _pack_small(m_ln_g, m_ln_b, m_b_sinks), _pack_small(v_ln_g, v_ln_b, v_b_sinks), "adamw_small")
    for d, blob in ((grads, None), (deltas, delta_s), (new_m, nm_s), (new_v, nv_s)):
        if blob is None:
            d["ln_g"], d["ln_b"], d["b_sinks"] = grad_ln_g, grad_ln_b, grad_sinks
        else:
            d["ln_g"], d["ln_b"], d["b_sinks"] = _unpack_small(blob, ln_g.shape, b_sinks.shape)

    order = ("ffn1_w_in", "ffn1_w_out", "ffn2_w_in", "ffn2_w_out", "ln_g", "ln_b", "a_w_qkv", "a_w_o", "kv_w", "b_w_q",
             "b_sinks", "b_w_o")
    outs = [loss, grad_x[None]]
    for d in (grads, deltas, new_m, new_v):
        outs += [d[n] for n in order]
    return tuple(outs)
```

```python
import numpy as np
import jax
import jax.numpy as jnp
from jax import lax
from jax.experimental import pallas as pl
from jax.experimental.pallas import tpu as pltpu

F32 = jnp.float32
BF16 = jnp.bfloat16

D_MODEL = 1024
D_FF = 2816
HALF_FF = D_FF // 2
HEAD_DIM = 64
N_HEADS = 16
N_KV_B = 4
GROUP_B = N_HEADS // N_KV_B
DEPTH = 2
ALPHA = (2.0 * DEPTH) ** 0.25
LN_EPS = 1e-5
BLOCK = 128
SLAB = 128
N_SLABS = D_MODEL // SLAB
PATTERNS_A = ((1, 128, 1.0), (4, 128, 4.0), (16, 128, 16.0))
PATTERNS_B = ((1, 127, 1.0),)
NEG = -1e30

ADAM_LR = 0.001
ADAM_B1 = 0.9
ADAM_B2 = 0.999
ADAM_EPS = 1e-08
ADAM_WD = 0.01
ADAM_STEP = 10

N_CHIPS = 4
VMEM_LIMIT = 56 * 1024 * 1024
MESH = pl.DeviceIdType.MESH


def _alibi_slopes(n):
    return np.array([2.0 ** (-8.0 * (h + 1) / n) for h in range(n)], dtype=np.float32)


def _cparams(sem=None, vmem=VMEM_LIMIT):
    return pltpu.CompilerParams(dimension_semantics=sem, vmem_limit_bytes=vmem)


_DIMS = {"nn": ((1,), (0,)), "nt": ((1,), (1,)), "tn": ((0,), (0,))}


def _unlead(x):
    if isinstance(x, tuple):
        return x[0], x[1], x[0].shape[1:]
    return x, None, x.shape


def _bspec(block, imap, lead=None):
    if lead is None:
        return pl.BlockSpec(block, imap)
    return pl.BlockSpec((None,) + tuple(block), lambda *g: (lead,) + tuple(imap(*g)))


def _matmul(a, b, mode, out_dtype, tm, tn, tk, name, add=None, add_scale=1.0, split=False):
    out_spec = pl.BlockSpec((tm, tn), lambda i, j, k: (i, j))
    if mode == "nn":
        a, al, (M, K) = _unlead(a)
        b, bl, (K2, N) = _unlead(b)
        a_spec = _bspec((tm, tk), lambda i, j, k: (i, k), al)
        b_spec = _bspec((tk, tn), lambda i, j, k: (k, j), bl)
        out_struct = jax.ShapeDtypeStruct((M, N), out_dtype)
        if split:
            assert tn == D_MODEL
            out_spec = pl.BlockSpec((None, tm, tn), lambda i, j, k: (j, i, 0))
            out_struct = jax.ShapeDtypeStruct((N // tn, M, tn), out_dtype)
    elif mode == "nt":
        b, bl, (N, K2) = _unlead(b)
        if split:
            assert tk == D_MODEL
            M, K = a.shape[1], a.shape[0] * a.shape[2]
            a_spec = pl.BlockSpec((None, tm, tk), lambda i, j, k: (k, i, 0))
        else:
            a, al, (M, K) = _unlead(a)
            a_spec = _bspec((tm, tk), lambda i, j, k: (i, k), al)
        b_spec = _bspec((tn, tk), lambda i, j, k: (j, k), bl)
        out_struct = jax.ShapeDtypeStruct((M, N), out_dtype)
    else:
        a, al, (K, M) = _unlead(a)
        if split:
            assert tn == D_MODEL
            K2, N = b.shape[1], b.shape[0] * b.shape[2]
            b_spec = pl.BlockSpec((None, tk, tn), lambda i, j, k: (j, k, 0))
        else:
            b, bl, (K2, N) = _unlead(b)
            b_spec = _bspec((tk, tn), lambda i, j, k: (k, j), bl)
        a_spec = _bspec((tk, tm), lambda i, j, k: (k, i), al)
        out_struct = jax.ShapeDtypeStruct((M, N), out_dtype)
    assert K == K2 and M % tm == 0 and N % tn == 0 and K % tk == 0, (a.shape, b.shape, mode, tm, tn, tk)
    nk = K // tk
    dims = (_DIMS[mode], ((), ()))
    has_add = add is not None

    def body(*refs):
        if has_add:
            a_ref, b_ref, add_ref, o_ref, acc_ref = refs
        else:
            a_ref, b_ref, o_ref, acc_ref = refs
        k = pl.program_id(2)
        part = lax.dot_general(a_ref[...].astype(BF16), b_ref[...].astype(BF16), dims, preferred_element_type=F32)

        @pl.when(k == 0)
        def _():
            acc_ref[...] = part

        @pl.when(k > 0)
        def _():
            acc_ref[...] += part

        @pl.when(k == nk - 1)
        def _():
            r = acc_ref[...]
            if has_add:
                r = r + add_scale * add_ref[...]
            o_ref[...] = r.astype(out_dtype)

    in_specs = [a_spec, b_spec]
    args = [a, b]
    if has_add:
        in_specs.append(pl.BlockSpec((tm, tn), lambda i, j, k: (i, j)))
        args.append(add)
    return pl.pallas_call(
        body, name=name, grid=(M // tm, N // tn, nk),
        in_specs=in_specs, out_specs=out_spec, out_shape=out_struct,
        scratch_shapes=[pltpu.VMEM((tm, tn), F32)],
        compiler_params=_cparams(("parallel", "parallel", "arbitrary")),
    )(*args)


def _pick(n, cands):
    for c in cands:
        if n % c == 0:
            return c
    raise ValueError((n, cands))


def _mm_nn(a, b, out_dtype, name, split=False):
    M, K = _unlead(a)[2]
    N = _unlead(b)[2][1]
    return _matmul(a, b, "nn", out_dtype, _pick(M, (1024, 512, 256)), _pick(N, (1024, 512)), _pick(K, (1024, 512)), name,
                   split=split)


def _mm_nt(a, b, name, add=None, add_scale=1.0, split=False):
    M, K = (a.shape[1], D_MODEL) if split else _unlead(a)[2]
    N = _unlead(b)[2][0]
    return _matmul(a, b, "nt", F32, _pick(M, (1024, 512, 256)), _pick(N, (1024, 512)),
                   _pick(K, (1408, 1024, 512)), name, add=add, add_scale=add_scale, split=split)


def _mm_tn(a, b, name, split=False):
    K, M = _unlead(a)[2]
    N = D_MODEL if split else _unlead(b)[2][1]
    return _matmul(a, b, "tn", F32, _pick(M, (1024, 1408, 512)), _pick(N, (1408, 1024, 512)),
                   _pick(K, (1024, 512, 256)), name, split=split)


def _ffn_in(x, w, name):
    S = x.shape[0]
    tm = _pick(S, (512, 256))
    w, wl, _ = _unlead(w)

    def body(x_ref, w_ref, u_ref, h_ref):
        acc = jnp.dot(x_ref[...].astype(BF16), w_ref[...], preferred_element_type=F32)
        g = acc[:, :HALF_FF]
        up = acc[:, HALF_FF:]
        u_ref[...] = acc.astype(BF16)
        h_ref[...] = (g * jax.nn.sigmoid(g) * up).astype(BF16)

    return pl.pallas_call(
        body, name=name, grid=(2, S // tm),
        in_specs=[pl.BlockSpec((tm, D_MODEL), lambda j, i: (i, 0)),
                  _bspec((D_MODEL, D_FF), lambda j, i: (0, j), wl)],
        out_specs=[pl.BlockSpec((tm, D_FF), lambda j, i: (i, j)),
                   pl.BlockSpec((tm, HALF_FF), lambda j, i: (i, j))],
        out_shape=[jax.ShapeDtypeStruct((S, 2 * D_FF), BF16), jax.ShapeDtypeStruct((S, D_FF), BF16)],
        compiler_params=_cparams(("parallel", "parallel")),
    )(x, w)


def _ffn_bwd_h(dzc, w_out, u, name):
    S = dzc.shape[0]
    tm = _pick(S, (512, 256))
    w_out, wl, _ = _unlead(w_out)

    def body(dz_ref, w_ref, u_ref, du_ref):
        dh = lax.dot_general(dz_ref[...], w_ref[...], (((1,), (1,)), ((), ())), preferred_element_type=F32)
        g = u_ref[:, :HALF_FF].astype(F32)
        up = u_ref[:, HALF_FF:].astype(F32)
        sg = jax.nn.sigmoid(g)
        du_ref[:, :HALF_FF] = (dh * up * (sg * (1.0 + g * (1.0 - sg)))).astype(BF16)
        du_ref[:, HALF_FF:] = (dh * (g * sg)).astype(BF16)

    return pl.pallas_call(
        body, name=name, grid=(2, S // tm),
        in_specs=[pl.BlockSpec((tm, D_MODEL), lambda j, i: (i, 0)),
                  _bspec((HALF_FF, D_MODEL), lambda j, i: (j, 0), wl),
                  pl.BlockSpec((tm, D_FF), lambda j, i: (i, j))],
        out_specs=pl.BlockSpec((tm, D_FF), lambda j, i: (i, j)),
        out_shape=jax.ShapeDtypeStruct((S, 2 * D_FF), BF16),
        compiler_params=_cparams(("parallel", "parallel")),
    )(dzc, w_out, u)


def _mm_ln(a, w, resid, gain, bias, c, name):
    S, K = a.shape
    tm = _pick(S, (512, 256))
    tk = _pick(K, (1408, 1024))
    nk = K // tk
    w, wl, _ = _unlead(w)

    def body(a_ref, w_ref, r_ref, g_ref, b_ref, y_ref, yb_ref, z_ref, acc_ref):
        k = pl.program_id(1)
        part = jnp.dot(a_ref[...], w_ref[...], preferred_element_type=F32)

        @pl.when(k == 0)
        def _():
            acc_ref[...] = part

        @pl.when(k > 0)
        def _():
            acc_ref[...] += part

        @pl.when(k == nk - 1)
        def _():
            z = ALPHA * r_ref[...] + c * acc_ref[...]
            mu = jnp.mean(z, axis=-1, keepdims=True)
            zc = z - mu
            var = jnp.mean(zc * zc, axis=-1, keepdims=True)
            y = zc * lax.rsqrt(var + LN_EPS) * g_ref[...] + b_ref[...]
            z_ref[...] = z
            y_ref[...] = y
            yb_ref[...] = y.astype(BF16)

    row = pl.BlockSpec((tm, D_MODEL), lambda i, k: (i, 0))
    vec = pl.BlockSpec((1, D_MODEL), lambda i, k: (0, 0))
    return pl.pallas_call(
        body, name=name, grid=(S // tm, nk),
        in_specs=[pl.BlockSpec((tm, tk), lambda i, k: (i, k)), _bspec((tk, D_MODEL), lambda i, k: (k, 0), wl),
                  row, vec, vec],
        out_specs=[row, row, row],
        out_shape=[jax.ShapeDtypeStruct((S, D_MODEL), F32), jax.ShapeDtypeStruct((S, D_MODEL), BF16),
                   jax.ShapeDtypeStruct((S, D_MODEL), F32)],
        scratch_shapes=[pltpu.VMEM((tm, D_MODEL), F32)],
        compiler_params=_cparams(("parallel", "arbitrary")),
    )(a, w, resid, gain, bias)


def _ln_bwd(z, dy, gain, c, name):
    S = z.shape[0]
    tm = _pick(S, (512, 256))

    def body(z_ref, dy_ref, g_ref, dz_ref, dzc_ref, gg_ref, gb_ref):
        i = pl.program_id(0)
        zv = z_ref[...]
        dyv = dy_ref[...]
        mu = jnp.mean(zv, axis=-1, keepdims=True)
        zc = zv - mu
        var = jnp.mean(zc * zc, axis=-1, keepdims=True)
        rstd = lax.rsqrt(var + LN_EPS)
        xhat = zc * rstd
        dyg = dyv * g_ref[...]
        m1 = jnp.mean(dyg, axis=-1, keepdims=True)
        m2 = jnp.mean(dyg * xhat, axis=-1, keepdims=True)
        dz = rstd * (dyg - m1 - xhat * m2)
        dz_ref[...] = dz
        dzc_ref[...] = (c * dz).astype(BF16)
        pg = jnp.sum((dyv * xhat).reshape(tm // 8, 8, D_MODEL), axis=0)
        pb = jnp.sum(dyv.reshape(tm // 8, 8, D_MODEL), axis=0)

        @pl.when(i == 0)
        def _():
            gg_ref[...] = pg
            gb_ref[...] = pb

        @pl.when(i > 0)
        def _():
            gg_ref[...] += pg
            gb_ref[...] += pb

    row = pl.BlockSpec((tm, D_MODEL), lambda i: (i, 0))
    part = pl.BlockSpec((8, D_MODEL), lambda i: (0, 0))
    return pl.pallas_call(
        body, name=name, grid=(S // tm,),
        in_specs=[row, row, pl.BlockSpec((1, D_MODEL), lambda i: (0, 0))],
        out_specs=[row, row, part, part],
        out_shape=[jax.ShapeDtypeStruct((S, D_MODEL), F32), jax.ShapeDtypeStruct((S, D_MODEL), BF16),
                   jax.ShapeDtypeStruct((8, D_MODEL), F32), jax.ShapeDtypeStruct((8, D_MODEL), F32)],
        compiler_params=_cparams(("arbitrary",)),
    )(z, dy, gain)


def _loss_grad(y, t, name):
    S = y.shape[0]
    tm = _pick(S, (512, 256))

    def body(y_ref, t_ref, dy_ref, sq_ref):
        i = pl.program_id(0)
        e = y_ref[...] - t_ref[...]
        dy_ref[...] = e * (1.0 / D_MODEL)
        ps = jnp.sum((e * e).reshape(tm // 8, 8, D_MODEL), axis=0)

        @pl.when(i == 0)
        def _():
            sq_ref[...] = ps

        @pl.when(i > 0)
        def _():
            sq_ref[...] += ps

    row = pl.BlockSpec((tm, D_MODEL), lambda i: (i, 0))
    return pl.pallas_call(
        body, name=name, grid=(S // tm,),
        in_specs=[row, row], out_specs=[row, pl.BlockSpec((8, D_MODEL), lambda i: (0, 0))],
        out_shape=[jax.ShapeDtypeStruct((S, D_MODEL), F32), jax.ShapeDtypeStruct((8, D_MODEL), F32)],
        compiler_params=_cparams(("arbitrary",)),
    )(y, t)


def _rows(start, d):
    if d == 1:
        return pl.ds(pl.multiple_of(start, BLOCK), BLOCK)
    return pl.ds(start, BLOCK, stride=d)


def _ld(ref, start, d):
    return ref[_rows(start, d), :]


def _ld3(ref, lead, start, d):
    return ref[lead, _rows(start, d), :]


def _st3(ref, lead, start, d, val):
    ref[lead, _rows(start, d), :] = val


def _acc3(ref, lead, start, d, val):
    ref[lead, _rows(start, d), :] = ref[lead, _rows(start, d), :] + val


def _band_consts(slope0, slope1, maxd, scale):
    row = lax.broadcasted_iota(jnp.int32, (2 * BLOCK, 2 * BLOCK), 0)
    kj = lax.broadcasted_iota(jnp.int32, (2 * BLOCK, 2 * BLOCK), 1)
    top = row < BLOCK
    dist = BLOCK + jnp.where(top, row, row - BLOCK) - kj
    slope = jnp.where(top, slope0, slope1)
    base = jnp.where((dist >= 0) & (dist <= maxd), -(slope * (dist.astype(F32) * scale)), NEG)
    return base, kj < BLOCK


def _stack_heads(x, lo):
    return jnp.concatenate([jnp.where(lo, x, 0.0), jnp.where(lo, 0.0, x)], axis=0)


def _unstack_heads(x2, lo):
    return jnp.where(lo, x2[:BLOCK], x2[BLOCK:])


def _scores(q2, k2, base, prev_keys, first):
    s = lax.dot_general(q2, k2, (((1,), (1,)), ((), ())), preferred_element_type=F32) * (HEAD_DIM ** -0.5) + base
    return jnp.where(jnp.logical_and(prev_keys, first), NEG, s)


def _softmax_weights(ls):
    mx = ls[0]
    for l in ls[1:]:
        mx = jnp.maximum(mx, l)
    es = [jnp.exp(l - mx) for l in ls]
    tot = es[0]
    for e in es[1:]:
        tot = tot + e
    inv = 1.0 / tot
    return [e * inv for e in es]


def _attn_fwd(qkv, slopes, sinks, patterns, name):
    S = qkv.shape[1]
    npat = len(patterns)
    has_sink = sinks is not None
    if not has_sink:
        sinks = jnp.zeros((N_HEADS,), F32)
    rows_c = 256

    def body(slopes_ref, sinks_ref, x_ref, mix_ref, o_ref, lse_ref):
        p = pl.program_id(0)
        lo = lax.broadcasted_iota(jnp.int32, (BLOCK, SLAB), 1) < HEAD_DIM
        top1 = lax.broadcasted_iota(jnp.int32, (2 * BLOCK, 1), 0) < BLOCK
        sk2 = jnp.where(top1, sinks_ref[2 * p], sinks_ref[2 * p + 1])
        for pi, (d, maxd, scale) in enumerate(patterns):
            nb = S // d // BLOCK
            base, prev_keys = _band_consts(slopes_ref[2 * p], slopes_ref[2 * p + 1], maxd, scale)

            def blk(t, carry, pi=pi, d=d, nb=nb, base=base, prev_keys=prev_keys):
                r = t // nb
                n = t - r * nb
                start = r + (d * BLOCK) * n
                prev = jnp.where(n > 0, start - d * BLOCK, start)
                q2 = _stack_heads(_ld3(x_ref, 0, start, d), lo).astype(BF16)
                k2 = jnp.concatenate([_ld3(x_ref, 1, prev, d), _ld3(x_ref, 1, start, d)], axis=0).astype(BF16)
                v2 = jnp.concatenate([_ld3(x_ref, 2, prev, d), _ld3(x_ref, 2, start, d)], axis=0).astype(BF16)
                s = _scores(q2, k2, base, prev_keys, n == 0)
                m = jnp.max(s, axis=-1, keepdims=True)
                if has_sink:
                    m = jnp.maximum(m, sk2)
                e = jnp.exp(s - m)
                den = jnp.sum(e, axis=-1, keepdims=True)
                if has_sink:
                    den = den + jnp.exp(sk2 - m)
                o2 = jnp.dot((e / den).astype(BF16), v2, preferred_element_type=F32)
                _st3(o_ref, pi, start, d, _unstack_heads(o2, lo))
                _st3(lse_ref, pi, start, d, _unstack_heads(m + jnp.log(den), lo))
                return carry

            lax.fori_loop(0, d * nb, blk, 0, unroll=8)

        def comb(ci, carry):
            rows = pl.ds(pl.multiple_of(ci * rows_c, rows_c), rows_c)
            if npat == 1:
                mix_ref[rows, :] = o_ref[0, rows, :].astype(BF16)
            else:
                ws = _softmax_weights([lse_ref[i, rows, :] for i in range(npat)])
                acc = ws[0] * o_ref[0, rows, :]
                for i in range(1, npat):
                    acc = acc + ws[i] * o_ref[i, rows, :]
                mix_ref[rows, :] = acc.astype(BF16)
            return carry

        lax.fori_loop(0, S // rows_c, comb, 0)

    smem = pl.BlockSpec(memory_space=pltpu.SMEM)
    slab3 = pl.BlockSpec((npat, S, SLAB), lambda p: (0, 0, p))
    return pl.pallas_call(
        body, name=name, grid=(N_SLABS,),
        in_specs=[smem, smem, pl.BlockSpec((3, S, SLAB), lambda p: (0, 0, p))],
        out_specs=[pl.BlockSpec((S, SLAB), lambda p: (0, p)), slab3, slab3],
        out_shape=[jax.ShapeDtypeStruct((S, D_MODEL), BF16), jax.ShapeDtypeStruct((npat, S, D_MODEL), F32),
                   jax.ShapeDtypeStruct((npat, S, D_MODEL), F32)],
        compiler_params=_cparams(("arbitrary",)),
    )(slopes, sinks, qkv)


def _attn_bwd(qkv, dout, o, lse, slopes, sinks, patterns, name):
    S = qkv.shape[1]
    npat = len(patterns)
    has_sink = sinks is not None
    if not has_sink:
        sinks = jnp.zeros((N_HEADS,), F32)
    rows_c = 256

    def headsum(x, lo):
        s0 = jnp.sum(jnp.where(lo, x, 0.0), axis=-1, keepdims=True)
        s1 = jnp.sum(jnp.where(lo, 0.0, x), axis=-1, keepdims=True)
        return jnp.where(lo, s0, s1)

    def body(slopes_ref, sinks_ref, x_ref, do_ref, o_ref, lse_ref, dx_ref, dsink_ref, dbar_ref, sacc_ref):
        p = pl.program_id(0)
        lo = lax.broadcasted_iota(jnp.int32, (BLOCK, SLAB), 1) < HEAD_DIM
        lo_c = lax.broadcasted_iota(jnp.int32, (rows_c, SLAB), 1) < HEAD_DIM
        top1 = lax.broadcasted_iota(jnp.int32, (2 * BLOCK, 1), 0) < BLOCK
        sk2 = jnp.where(top1, sinks_ref[2 * p], sinks_ref[2 * p + 1])

        def prep(ci, carry):
            rows = pl.ds(pl.multiple_of(ci * rows_c, rows_c), rows_c)
            dov = do_ref[rows, :]
            dx_ref[:, rows, :] = jnp.zeros((3, rows_c, SLAB), F32)
            if npat == 1:
                dbar_ref[rows, :] = headsum(dov * o_ref[0, rows, :], lo_c)
            else:
                ws = _softmax_weights([lse_ref[i, rows, :] for i in range(npat)])
                acc = ws[0] * headsum(dov * o_ref[0, rows, :], lo_c)
                for i in range(1, npat):
                    acc = acc + ws[i] * headsum(dov * o_ref[i, rows, :], lo_c)
                dbar_ref[rows, :] = acc
            return carry

        lax.fori_loop(0, S // rows_c, prep, 0)
        sacc_ref[...] = jnp.zeros((BLOCK, SLAB), F32)

        for pi, (d, maxd, scale) in enumerate(patterns):
            nb = S // d // BLOCK
            base, prev_keys = _band_consts(slopes_ref[2 * p], slopes_ref[2 * p + 1], maxd, scale)

            def blk(t, carry, pi=pi, d=d, nb=nb, base=base, prev_keys=prev_keys):
                r = t // nb
                n = t - r * nb
                start = r + (d * BLOCK) * n
                prev = jnp.where(n > 0, start - d * BLOCK, start)
                q2 = _stack_heads(_ld3(x_ref, 0, start, d), lo).astype(BF16)
                k2 = jnp.concatenate([_ld3(x_ref, 1, prev, d), _ld3(x_ref, 1, start, d)], axis=0).astype(BF16)
                v2 = jnp.concatenate([_ld3(x_ref, 2, prev, d), _ld3(x_ref, 2, start, d)], axis=0).astype(BF16)
                ls = [_ld3(lse_ref, i, start, d) for i in range(npat)]
                w = _softmax_weights(ls)[pi] if npat > 1 else 1.0
                do2 = _stack_heads(w * _ld(do_ref, start, d), lo).astype(BF16)
                dl = w * _ld(dbar_ref, start, d)
                lse2 = jnp.concatenate([ls[pi][:, :1], ls[pi][:, HEAD_DIM:HEAD_DIM + 1]], axis=0)
                dl2 = jnp.concatenate([dl[:, :1], dl[:, HEAD_DIM:HEAD_DIM + 1]], axis=0)
                s = _scores(q2, k2, base, prev_keys, n == 0)
                pr = jnp.exp(s - lse2)
                dp = lax.dot_general(do2, v2, (((1,), (1,)), ((), ())), preferred_element_type=F32)
                ds = (pr * (dp - dl2) * (HEAD_DIM ** -0.5)).astype(BF16)
                dq2 = jnp.dot(ds, k2, preferred_element_type=F32)
                dk2 = lax.dot_general(ds, q2, (((0,), (0,)), ((), ())), preferred_element_type=F32)
                dv2 = lax.dot_general(pr.astype(BF16), do2, (((0,), (0,)), ((), ())), preferred_element_type=F32)
                _acc3(dx_ref, 0, start, d, _unstack_heads(dq2, lo))
                _acc3(dx_ref, 1, prev, d, dk2[:BLOCK])
                _acc3(dx_ref, 1, start, d, dk2[BLOCK:])
                _acc3(dx_ref, 2, prev, d, dv2[:BLOCK])
                _acc3(dx_ref, 2, start, d, dv2[BLOCK:])
                if has_sink:
                    sacc_ref[...] += _unstack_heads(-jnp.exp(sk2 - lse2) * dl2, lo)
                return carry

            lax.fori_loop(0, d * nb, blk, 0, unroll=4)

        dsink_ref[...] = jnp.broadcast_to(jnp.sum(sacc_ref[...], axis=0, keepdims=True), (8, SLAB))

    smem = pl.BlockSpec(memory_space=pltpu.SMEM)
    one = pl.Buffered(1)
    slab3 = pl.BlockSpec((npat, S, SLAB), lambda p: (0, 0, p), pipeline_mode=one)
    return pl.pallas_call(
        body, name=name, grid=(N_SLABS,),
        in_specs=[smem, smem, pl.BlockSpec((3, S, SLAB), lambda p: (0, 0, p), pipeline_mode=one),
                  pl.BlockSpec((S, SLAB), lambda p: (0, p), pipeline_mode=one), slab3, slab3],
        out_specs=[pl.BlockSpec((3, S, SLAB), lambda p: (0, 0, p)), pl.BlockSpec((None, 8, SLAB), lambda p: (p, 0, 0))],
        out_shape=[jax.ShapeDtypeStruct((3, S, D_MODEL), F32), jax.ShapeDtypeStruct((N_SLABS, 8, SLAB), F32)],
        scratch_shapes=[pltpu.VMEM((S, SLAB), F32), pltpu.VMEM((BLOCK, SLAB), F32)],
        compiler_params=_cparams(("arbitrary",)),
    )(slopes, sinks, qkv, dout, o, lse)


def _place():
    x, y, c = lax.axis_index("x"), lax.axis_index("y"), lax.axis_index("c")
    return x, y, c, 2 * x + y


def _other_chips(x, y):
    return [(1 - x, y), (x, 1 - y), (1 - x, 1 - y)]


HBM_SPEC = pl.BlockSpec(memory_space=pl.ANY)


def _slot(q):
    return 2 * (q % 2) + q // 2


BIG = ("ffn1_w_in", "ffn1_w_out", "ffn2_w_in", "ffn2_w_out", "a_w_qkv", "a_w_o", "kv_w", "b_w_q", "b_w_o")
QKV_SHARD = 3 * D_MODEL // N_CHIPS
ROW_SHARD = D_MODEL // N_CHIPS


LAYER0_ITEMS = (("ffn1_w_in", 0), ("ffn1_w_out", 0), ("a_w_qkv", None), ("a_w_o", None), ("ffn2_w_in", 0),
                ("ffn2_w_out", 0), ("kv_w", None))
LAYER1_ITEMS = (("ffn1_w_in", 1), ("ffn1_w_out", 1), ("b_w_q", None), ("b_w_o", None), ("ffn2_w_in", 1),
                ("ffn2_w_out", 1))
OUT_SHARD = D_FF // N_CHIPS


def _full_shape(name):
    if name.endswith("w_in"):
        return (D_MODEL, 2 * D_FF)
    if name.endswith("w_out"):
        return (D_FF, D_MODEL)
    if name == "a_w_qkv":
        return (D_MODEL, 3 * D_MODEL)
    if name == "kv_w":
        return (N_CHIPS, 2, ROW_SHARD // 2, 2 * N_KV_B * HEAD_DIM)
    return (N_CHIPS, 2, ROW_SHARD // 2, D_MODEL)


def _gather_src(item, ref, c):
    name, _ = item
    if name.endswith("w_in"):
        return ref.at[pl.ds(c * (D_MODEL // 2), D_MODEL // 2)]
    if name.endswith("w_out"):
        return ref.at[pl.ds(c * (OUT_SHARD // 2), OUT_SHARD // 2)]
    if name == "a_w_qkv":
        return ref.at[0, pl.ds(c * (D_MODEL // 2), D_MODEL // 2)]
    if name == "kv_w":
        return ref.at[pl.ds(c * (ROW_SHARD // 2), ROW_SHARD // 2)]
    return ref.at[0, pl.ds(c * (ROW_SHARD // 2), ROW_SHARD // 2)]


def _gather_dst(item, ref, q, c):
    name, _ = item
    if name.endswith("w_in"):
        return ref.at[pl.ds(c * (D_MODEL // 2), D_MODEL // 2), pl.ds(_slot(q) * HALF_FF, HALF_FF)]
    if name.endswith("w_out"):
        return ref.at[pl.ds(q * OUT_SHARD + c * (OUT_SHARD // 2), OUT_SHARD // 2)]
    if name == "a_w_qkv":
        return ref.at[pl.ds(c * (D_MODEL // 2), D_MODEL // 2), pl.ds(q * QKV_SHARD, QKV_SHARD)]
    return ref.at[q, c]


def _all_gather(items, shards, small):
    n = len(items)
    r = small.shape[0]
    per = 8

    def body(*refs):
        srcs, small_ref = refs[:n], refs[n]
        dsts, s_ref = refs[n + 1:2 * n + 1], refs[2 * n + 1]
        send_sems, recv_sems = refs[2 * n + 2:]
        x, y, c, myq = _place()
        sibling = (x, y, 1 - c)
        chips = _other_chips(x, y)

        def big(t, k, src, q, h, to):
            return pltpu.make_async_remote_copy(src_ref=src, dst_ref=_gather_dst(items[t], dsts[t], q, h),
                                                send_sem=send_sems.at[per * t + k], recv_sem=recv_sems.at[per * t + k],
                                                device_id=to, device_id_type=MESH)

        def tiny(k, q, to):
            return pltpu.make_async_remote_copy(src_ref=small_ref, dst_ref=s_ref.at[q], send_sem=send_sems.at[per * n + k],
                                                recv_sem=recv_sems.at[per * n + k], device_id=to, device_id_type=MESH)

        first = []
        for j, chip in enumerate(chips):
            first += [big(t, j, _gather_src(items[t], srcs[t], c), myq, c, (*chip, c)) for t in range(n)]
            first.append(tiny(j, myq, (*chip, c)))
        own = [big(t, 6 + h, _gather_src(items[t], srcs[t], h), myq, h, sibling) for t in range(n) for h in (0, 1)]
        own.append(tiny(3, myq, sibling))
        for cp in first + own:
            cp.start()
        passed = []
        for j, (cx, cy) in enumerate(chips):
            q = 2 * cx + cy
            for t in range(n):
                src = _gather_src(items[t], srcs[t], c)
                big(t, j, src, q, c, sibling).wait_recv()
                fwd = big(t, 3 + j, _gather_dst(items[t], dsts[t], q, c), q, c, sibling)
                fwd.start()
                passed.append(fwd)
        for j, (cx, cy) in enumerate(chips):
            q = 2 * cx + cy
            for t in range(n):
                big(t, 3 + j, _gather_src(items[t], srcs[t], c), q, 1 - c, sibling).wait_recv()
            tiny(j, q, sibling).wait_recv()
        for cp in own:
            cp.wait_recv()
        for cp in first + passed + own:
            cp.wait_send()

    outs = pl.pallas_call(
        body, name="all_gather_layer0",
        in_specs=[HBM_SPEC] * (n + 1), out_specs=[HBM_SPEC] * (n + 1),
        out_shape=[jax.ShapeDtypeStruct(_full_shape(name), BF16) for name, _ in items]
        + [jax.ShapeDtypeStruct((N_CHIPS, r, 128), F32)],
        scratch_shapes=[pltpu.SemaphoreType.DMA((per * n + 4,)), pltpu.SemaphoreType.DMA((per * n + 4,))],
    )(*[shards[item] for item in items], small)
    return list(outs[:n]), outs[n]


SEM_SPEC = pl.BlockSpec(memory_space=pltpu.SEMAPHORE)
DATAFLOW = pltpu.SideEffectType.DATAFLOW_SIDE_EFFECTING
PER_ITEM = 8


def _split_start(name, copies, n_sems, sources, land_shapes, after):
    n, m = len(sources), len(land_shapes)

    def body(*refs):
        srcs, lands = refs[:n], refs[n:n + m]
        send_sems, recv_sems = refs[n + m + 1], refs[n + m + 2]
        token = refs[-1]
        for src, dst_there, _, s, peer in copies(srcs, lands):
            pltpu.make_async_remote_copy(src_ref=src, dst_ref=dst_there, send_sem=send_sems.at[s], recv_sem=recv_sems.at[s],
                                         device_id=peer, device_id_type=MESH).start()
        token[...] = jnp.zeros_like(token)

    src_arrays = [pltpu.with_memory_space_constraint(a, pltpu.HBM) for a in sources]
    land_arrays = [pltpu.with_memory_space_constraint(lax.empty(s.shape, s.dtype), pltpu.HBM) for s in land_shapes]
    hbm = pl.BlockSpec(memory_space=pltpu.HBM)
    outs = pl.pallas_call(
        body, name=name,
        in_specs=[hbm] * (n + m) + [HBM_SPEC],
        out_specs=[SEM_SPEC, SEM_SPEC] + [hbm] * (n + m) + [pl.BlockSpec(memory_space=pltpu.VMEM)],
        out_shape=[pltpu.SemaphoreType.DMA((n_sems,)), pltpu.SemaphoreType.DMA((n_sems,))]
        + [pltpu.HBM(a.shape, a.dtype) for a in src_arrays + land_arrays] + [jax.ShapeDtypeStruct((8, 128), F32)],
        input_output_aliases={i: 2 + i for i in range(n + m)},
        compiler_params=pltpu.CompilerParams(has_side_effects=DATAFLOW),
    )(*src_arrays, *land_arrays, after)
    return (outs[0], outs[1], list(outs[2:2 + n]), list(outs[2 + n:2 + n + m])), outs[-1]


def _split_wait(name, copies, state, after):
    send_sems, recv_sems, srcs_thru, lands_thru = state
    n, m = len(srcs_thru), len(lands_thru)

    def body(*refs):
        srcs, lands = refs[:n], refs[n:n + m]
        send_sems, recv_sems = refs[n + m], refs[n + m + 1]
        for src, _, dst_here, s, peer in copies(srcs, lands):
            cp = pltpu.make_async_remote_copy(src_ref=src, dst_ref=dst_here, send_sem=send_sems.at[s], recv_sem=recv_sems.at[s],
                                              device_id=peer, device_id_type=MESH)
            cp.wait_send()
            cp.wait_recv()

    hbm = pl.BlockSpec(memory_space=pltpu.HBM)
    outs = pl.pallas_call(
        body, name=name,
        in_specs=[hbm] * (n + m) + [SEM_SPEC, SEM_SPEC, HBM_SPEC],
        out_specs=[hbm] * (n + m),
        out_shape=[pltpu.HBM(a.shape, a.dtype) for a in srcs_thru + lands_thru],
        input_output_aliases={i: i for i in range(n + m)},
        compiler_params=pltpu.CompilerParams(has_side_effects=DATAFLOW),
    )(*srcs_thru, *lands_thru, send_sems, recv_sems, after)
    return list(outs[:n]), list(outs[n:])


def _gather_copies(items):
    def copies(srcs, lands):
        x, y, c, myq = _place()
        out = []
        for t, item in enumerate(items):
            for h in (0, 1):
                src = _gather_src(item, srcs[t], h)
                for j, (cx, cy) in enumerate(_other_chips(x, y)):
                    out.append((src, _gather_dst(item, lands[t], myq, h), _gather_dst(item, lands[t], 2 * cx + cy, h),
                                PER_ITEM * t + 2 * j + h, (cx, cy, c)))
                out.append((src, _gather_dst(item, lands[t], myq, h), _gather_dst(item, lands[t], myq, h),
                            PER_ITEM * t + 6 + h, (x, y, 1 - c)))
        return out
    return copies


def _gather_start(items, shards, after):
    lands = [jax.ShapeDtypeStruct(_full_shape(name), BF16) for name, _ in items]
    return _split_start("gather_layer1_start", _gather_copies(items), PER_ITEM * len(items),
                        [shards[item] for item in items], lands, after)


def _gather_wait(items, state, after):
    return _split_wait("gather_layer1_wait", _gather_copies(items), state, after)[1]


def _small_all_reduce(v):
    r = v.shape[0]

    def body(v_ref, o_ref, buf_ref, send_sems, recv_sems):
        x, y, c, _ = _place()
        me = 4 * x + 2 * y + c
        buf_ref[me] = v_ref[...]
        copies = []
        for k in range(1, 8):
            fx, fy, fc = (k >> 2) & 1, (k >> 1) & 1, k & 1
            to = (x ^ fx, y ^ fy, c ^ fc)
            cp = pltpu.make_async_remote_copy(src_ref=v_ref, dst_ref=buf_ref.at[me], send_sem=send_sems.at[k - 1],
                                              recv_sem=recv_sems.at[k - 1], device_id=to, device_id_type=MESH)
            cp.start()
            copies.append(cp)
        for k in range(1, 8):
            fx, fy, fc = (k >> 2) & 1, (k >> 1) & 1, k & 1
            src_dev = 4 * (x ^ fx) + 2 * (y ^ fy) + (c ^ fc)
            pltpu.make_async_remote_copy(src_ref=v_ref, dst_ref=buf_ref.at[src_dev], send_sem=send_sems.at[k - 1],
                                         recv_sem=recv_sems.at[k - 1], device_id=(x, y, c), device_id_type=MESH).wait_recv()
        for cp in copies:
            cp.wait_send()
        tot = buf_ref[0]
        for i in range(1, 8):
            tot = tot + buf_ref[i]
        o_ref[...] = tot

    vm = pl.BlockSpec(memory_space=pltpu.VMEM)
    return pl.pallas_call(
        body, name="small_all_reduce", in_specs=[vm], out_specs=vm,
        out_shape=jax.ShapeDtypeStruct((r, 128), F32),
        scratch_shapes=[pltpu.VMEM((8, r, 128), F32), pltpu.SemaphoreType.DMA((7,)), pltpu.SemaphoreType.DMA((7,))],
    )(v)


def _grad_view(kind, g):
    if kind == "col":
        return g.reshape(2, g.shape[0] // 2, g.shape[1])
    return g.reshape(N_CHIPS, 2, g.shape[0] // (2 * N_CHIPS), g.shape[1])


def _half_of(kind, ref, h):
    return ref.at[h] if kind == "col" else ref.at[:, h]


def _half_shape(kind, view_shape):
    return view_shape[1:] if kind == "col" else (view_shape[0],) + view_shape[2:]


def _piece_of(kind, width, colblock, ref, q):
    if kind == "col":
        return ref.at[:, pl.ds(colblock(q) * width, width)]
    return ref.at[q]


def _piece_shape(kind, width, half_shape):
    return (half_shape[0], width) if kind == "col" else half_shape[1:]


def _pair_exchange(views, kinds, name):
    n = len(views)

    def body(*refs):
        ins, outs = refs[:n], refs[n:2 * n]
        send_sems, recv_sems = refs[2 * n:]
        x, y, c, _ = _place()
        cps = []
        for t in range(n):
            cp = pltpu.make_async_remote_copy(src_ref=_half_of(kinds[t], ins[t], 1 - c), dst_ref=outs[t],
                                              send_sem=send_sems.at[t], recv_sem=recv_sems.at[t],
                                              device_id=(x, y, 1 - c), device_id_type=MESH)
            cp.start()
            cps.append(cp)
        for cp in cps:
            cp.wait()

    return pl.pallas_call(
        body, name=name, in_specs=[HBM_SPEC] * n, out_specs=[HBM_SPEC] * n,
        out_shape=[jax.ShapeDtypeStruct(_half_shape(k, v.shape), v.dtype) for k, v in zip(kinds, views)],
        scratch_shapes=[pltpu.SemaphoreType.DMA((n,)), pltpu.SemaphoreType.DMA((n,))],
    )(*views)


def _pair_sum(kind, view, recv, c, name):
    hs = recv.shape
    N = hs[-1]
    rows = hs[-2]
    tr = _pick(rows, (512, 352, 128))
    tn = _pick(N, (1408, 1024, 512))

    def body(c_ref, p_ref, r_ref, s_ref):
        s_ref[...] = (p_ref[...] + r_ref[...]).astype(BF16)

    if kind == "col":
        grid = (rows // tr, N // tn)
        mine = pl.BlockSpec((None, tr, tn), lambda i, j, c_ref: (c_ref[0], i, j))
        blk = pl.BlockSpec((tr, tn), lambda i, j, c_ref: (i, j))
        sem = ("parallel", "parallel")
    else:
        grid = (N_CHIPS, rows // tr, N // tn)
        mine = pl.BlockSpec((None, None, tr, tn), lambda q, i, j, c_ref: (q, c_ref[0], i, j))
        blk = pl.BlockSpec((None, tr, tn), lambda q, i, j, c_ref: (q, i, j))
        sem = ("parallel", "parallel", "parallel")
    return pl.pallas_call(
        body, name=name,
        grid_spec=pltpu.PrefetchScalarGridSpec(num_scalar_prefetch=1, grid=grid, in_specs=[mine, blk], out_specs=blk),
        out_shape=jax.ShapeDtypeStruct(hs, BF16),
        compiler_params=_cparams(sem),
    )(c.reshape(1).astype(jnp.int32), view, recv)


def _chip_copies(kinds, widths, colblocks):
    def copies(srcs, lands):
        x, y, c, _ = _place()
        out = []
        for j, (cx, cy) in enumerate(_other_chips(x, y)):
            for t in range(len(kinds)):
                out.append((_piece_of(kinds[t], widths[t], colblocks[t], srcs[t], 2 * cx + cy), lands[t].at[j],
                            lands[t].at[j], 3 * t + j, (cx, cy, c)))
        return out
    return copies


def _chip_land_shapes(sums, kinds, widths):
    return [jax.ShapeDtypeStruct((3,) + _piece_shape(k, w, s.shape), BF16) for k, w, s in zip(kinds, widths, sums)]


def _chip_exchange(sums, kinds, widths, colblocks, name):
    n = len(sums)
    copies = _chip_copies(kinds, widths, colblocks)

    def body(*refs):
        send_sems, recv_sems = refs[2 * n:]
        cps = [pltpu.make_async_remote_copy(src_ref=src, dst_ref=dst, send_sem=send_sems.at[s], recv_sem=recv_sems.at[s],
                                            device_id=peer, device_id_type=MESH)
               for src, dst, _, s, peer in copies(refs[:n], refs[n:2 * n])]
        for cp in cps:
            cp.start()
        for cp in cps:
            cp.wait()

    return pl.pallas_call(
        body, name=name, in_specs=[HBM_SPEC] * n, out_specs=[HBM_SPEC] * n,
        out_shape=_chip_land_shapes(sums, kinds, widths),
        scratch_shapes=[pltpu.SemaphoreType.DMA((3 * n,)), pltpu.SemaphoreType.DMA((3 * n,))],
    )(*sums)


def _chip_sum(kind, s, recv, block_idx, c, shard_shape, layer, into, name):
    rows, N = recv.shape[1:]
    tr = _pick(rows, (512, 352, 128))
    tn = _pick(N, (1408, 1024, 768, 512))
    ni, nj = rows // tr, N // tn

    def body(q_ref, s_ref, r_ref, *rest):
        o_ref = rest[-1]
        o_ref[...] = ((s_ref[...].astype(F32) + r_ref[0].astype(F32)) + r_ref[1].astype(F32)) + r_ref[2].astype(F32)

    if kind == "col":
        own = pl.BlockSpec((tr, tn), lambda i, j, q_ref: (i, q_ref[0] * nj + j))
    else:
        own = pl.BlockSpec((None, tr, tn), lambda i, j, q_ref: (q_ref[0], i, j))
    if len(shard_shape) == 3:
        lead = 0 if layer is None else layer
        out_spec = pl.BlockSpec((None, tr, tn), lambda i, j, q_ref: (lead, q_ref[1] * ni + i, j))
    else:
        out_spec = pl.BlockSpec((tr, tn), lambda i, j, q_ref: (q_ref[1] * ni + i, j))
    in_specs = [own, pl.BlockSpec((3, tr, tn), lambda i, j, q_ref: (0, i, j))]
    args = [jnp.stack([block_idx, c]).astype(jnp.int32), s, recv]
    aliases = {}
    if into is not None:
        in_specs.append(HBM_SPEC)
        args.append(into)
        aliases = {3: 0}
    return pl.pallas_call(
        body, name=name,
        grid_spec=pltpu.PrefetchScalarGridSpec(num_scalar_prefetch=1, grid=(ni, nj), in_specs=in_specs, out_specs=out_spec),
        out_shape=jax.ShapeDtypeStruct(shard_shape, F32), input_output_aliases=aliases,
        compiler_params=_cparams(("parallel", "parallel")),
    )(*args)


def _half_window(ref, h):
    rows = ref.shape[-2] // 2
    if ref.ndim == 3:
        return ref.at[:, pl.ds(h * rows, rows)]
    return ref.at[pl.ds(h * rows, rows)]


def _share_halves(grads):
    n = len(grads)

    def body(*refs):
        outs = refs[n:2 * n]
        send_sems, recv_sems = refs[2 * n:]
        x, y, c, _ = _place()
        cps = []
        for t in range(n):
            cp = pltpu.make_async_remote_copy(src_ref=_half_window(outs[t], c), dst_ref=_half_window(outs[t], c),
                                              send_sem=send_sems.at[t], recv_sem=recv_sems.at[t],
                                              device_id=(x, y, 1 - c), device_id_type=MESH)
            cp.start()
            cps.append(cp)
        for t in range(n):
            cps[t].wait_send()
            pltpu.make_async_remote_copy(src_ref=_half_window(outs[t], c), dst_ref=_half_window(outs[t], 1 - c),
                                         send_sem=send_sems.at[t], recv_sem=recv_sems.at[t],
                                         device_id=(x, y, 1 - c), device_id_type=MESH).wait_recv()

    return pl.pallas_call(
        body, name="grad_share_halves", in_specs=[HBM_SPEC] * n, out_specs=[HBM_SPEC] * n,
        out_shape=[jax.ShapeDtypeStruct(g.shape, F32) for g in grads],
        input_output_aliases={t: t for t in range(n)},
        scratch_shapes=[pltpu.SemaphoreType.DMA((n,)), pltpu.SemaphoreType.DMA((n,))],
    )(*grads)


def _adamw(w, g, m, v, name):
    R, W = w.shape
    tr = _pick(R, (512, 352, 256, 32))

    def body(w_ref, g_ref, m_ref, v_ref, d_ref, nm_ref, nv_ref):
        gv = g_ref[...]
        nm = ADAM_B1 * m_ref[...] + (1.0 - ADAM_B1) * gv
        nv = ADAM_B2 * v_ref[...] + (1.0 - ADAM_B2) * (gv * gv)
        m_hat = nm / (1.0 - ADAM_B1 ** ADAM_STEP)
        v_hat = nv / (1.0 - ADAM_B2 ** ADAM_STEP)
        d_ref[...] = -ADAM_LR * (m_hat / (jnp.sqrt(v_hat) + ADAM_EPS) + ADAM_WD * w_ref[...])
        nm_ref[...] = nm
        nv_ref[...] = nv

    blk = pl.BlockSpec((tr, W), lambda i: (i, 0))
    shp = jax.ShapeDtypeStruct((R, W), F32)
    return pl.pallas_call(
        body, name=name, grid=(R // tr,), in_specs=[blk] * 4, out_specs=[blk] * 3, out_shape=[shp] * 3,
        compiler_params=_cparams(("parallel",)),
    )(w, g, m, v)


SMALL_ROWS = 32


def _pack_small(ln_g, ln_b, sinks):
    rows = jnp.concatenate([ln_g.reshape(-1, 128), ln_b.reshape(-1, 128),
                            jnp.pad(sinks.reshape(1, -1), ((0, 0), (0, 128 - sinks.size)))], axis=0)
    return jnp.pad(rows, ((0, SMALL_ROWS - rows.shape[0]), (0, 0)))


def _unpack_small(s, ln_shape, sink_shape):
    n = ln_shape[0] * ln_shape[1] * ln_shape[2] // 128
    return s[:n].reshape(ln_shape), s[n:2 * n].reshape(ln_shape), s[2 * n, :sink_shape[1]].reshape(sink_shape)


def _ffn_fwd(xin, w_in, w_out, gain, bias, tag):
    u, h = _ffn_in(xin, w_in, "ffn_in_" + tag)
    y, yb, z = _mm_ln(h, w_out, xin, gain, bias, 0.5, "ffn_out_ln_" + tag)
    return y, yb, dict(u=u, h=h, z=z, xin=xin)


def _ffn_bwd(dy, saved, w_in, w_out, gain, xin_b, tag):
    dz, dzc, gg, gb = _ln_bwd(saved["z"], dy, gain, 0.5, "ln_bwd_" + tag)
    du = _ffn_bwd_h(dzc, w_out, saved["u"], "ffn_bwd_h_" + tag)
    d_w_out = _mm_tn(saved["h"], dzc, "ffn_dwout_" + tag)
    d_w_in = _mm_tn(xin_b, du, "ffn_dwin_" + tag)
    dx = _mm_nt(du, w_in, "ffn_dx_" + tag, add=dz, add_scale=ALPHA)
    return dx, d_w_in, d_w_out, gg, gb


def kernel(x, ffn1_w_in, ffn1_w_out, ffn2_w_in, ffn2_w_out, ln_g, ln_b, a_w_qkv, a_w_o, kv_w, b_w_q, b_sinks, b_w_o, loss_target, m_ffn1_w_in, m_ffn1_w_out, m_ffn2_w_in, m_ffn2_w_out, m_ln_g, m_ln_b, m_a_w_qkv, m_a_w_o, m_kv_w, m_b_w_q, m_b_sinks, m_b_w_o, v_ffn1_w_in, v_ffn1_w_out, v_ffn2_w_in, v_ffn2_w_out, v_ln_g, v_ln_b, v_a_w_qkv, v_a_w_o, v_kv_w, v_b_w_q, v_b_sinks, v_b_w_o):
    ws = dict(ffn1_w_in=ffn1_w_in, ffn1_w_out=ffn1_w_out, ffn2_w_in=ffn2_w_in, ffn2_w_out=ffn2_w_out, a_w_qkv=a_w_qkv,
              a_w_o=a_w_o, kv_w=kv_w, b_w_q=b_w_q, b_w_o=b_w_o)
    ms = dict(ffn1_w_in=m_ffn1_w_in, ffn1_w_out=m_ffn1_w_out, ffn2_w_in=m_ffn2_w_in, ffn2_w_out=m_ffn2_w_out,
              a_w_qkv=m_a_w_qkv, a_w_o=m_a_w_o, kv_w=m_kv_w, b_w_q=m_b_w_q, b_w_o=m_b_w_o)
    vs = dict(ffn1_w_in=v_ffn1_w_in, ffn1_w_out=v_ffn1_w_out, ffn2_w_in=v_ffn2_w_in, ffn2_w_out=v_ffn2_w_out,
              a_w_qkv=v_a_w_qkv, a_w_o=v_a_w_o, kv_w=v_kv_w, b_w_q=v_b_w_q, b_w_o=v_b_w_o)
    _, _, c_idx, myq = _place()
    xs = x[0]
    target = loss_target[0]

    shards = {(n, l): (ws[n] if l is None else ws[n][l]).astype(BF16) for n, l in LAYER0_ITEMS + LAYER1_ITEMS}

    def as_weights(items, arrays):
        return {n: (a.reshape(D_MODEL, a.shape[-1]) if a.ndim == 4 else a) for (n, _), a in zip(items, arrays)}

    full0, small = _all_gather(LAYER0_ITEMS, shards, _pack_small(ln_g, ln_b, b_sinks))
    gather_state, token = _gather_start(LAYER1_ITEMS, shards, small)

    def layer1_weights(after):
        return as_weights(LAYER1_ITEMS, _gather_wait(LAYER1_ITEMS, gather_state, after))

    n_ln = ln_g.size // 128
    lg = jnp.concatenate([small[q, :n_ln].reshape(DEPTH, 3, 1, -1) for q in range(N_CHIPS)], axis=-1)
    lb = jnp.concatenate([small[q, n_ln:2 * n_ln].reshape(DEPTH, 3, 1, -1) for q in range(N_CHIPS)], axis=-1)
    lg = lg + token[0, 0]
    reducer = _GradReducer(c_idx, myq, {n: ws[n].shape for n in BIG})
    sq, grad_x, _, gg, gb, dsink_part = _local_step(xs, target, as_weights(LAYER0_ITEMS, full0), layer1_weights,
                                                    lg, lb, b_sinks.reshape(N_HEADS), reducer.begin)

    loss_row = jnp.pad(jnp.sum(sq).reshape(1, 1), ((0, 0), (0, 127)))
    dsinks = jnp.pad(dsink_part[:, 0, :].reshape(N_SLABS, 2, HEAD_DIM)[:, :, 0].reshape(1, N_HEADS), ((0, 0), (0, 128 - N_HEADS)))
    gg_full = jnp.stack([jnp.stack([jnp.sum(gg[i][j], axis=0) for j in range(3)]) for i in range(DEPTH)])
    gb_full = jnp.stack([jnp.stack([jnp.sum(gb[i][j], axis=0) for j in range(3)]) for i in range(DEPTH)])
    small_in = jnp.concatenate([loss_row, dsinks, gg_full.reshape(-1, 128), gb_full.reshape(-1, 128)], axis=0)
    small_in = jnp.pad(small_in, ((0, (-small_in.shape[0]) % 8), (0, 0)))
    small_sum = _small_all_reduce(small_in)
    loss = small_sum[0, 0] * (0.5 / D_MODEL)
    grad_sinks = small_sum[1, :N_HEADS].reshape(b_sinks.shape)
    n_full = DEPTH * 3 * D_MODEL // 128
    cols = D_MODEL // N_CHIPS
    grad_ln_g = lax.dynamic_slice_in_dim(small_sum[2:2 + n_full].reshape(DEPTH, 3, D_MODEL), myq * cols, cols, axis=2)
    grad_ln_b = lax.dynamic_slice_in_dim(small_sum[2 + n_full:2 + 2 * n_full].reshape(DEPTH, 3, D_MODEL), myq * cols, cols, axis=2)
    return _update(reducer, grad_x, loss, grad_ln_g, grad_ln_b, grad_sinks, ws, ms, vs,
                   (ln_g, ln_b, b_sinks), (m_ln_g, m_ln_b, m_b_sinks), (v_ln_g, v_ln_b, v_b_sinks))


def _local_step(xs, target, W, layer1_weights, lg, lb, sinks, grads_ready=None):
    if grads_ready is None:
        grads_ready = lambda tag, grads, overlap: 0.0
    S = xs.shape[0]
    slopes = jnp.asarray(_alibi_slopes(N_HEADS))
    in1, out1, in2, out2 = [W["ffn1_w_in"]], [W["ffn1_w_out"]], [W["ffn2_w_in"]], [W["ffn2_w_out"]]

    y1, y1b, s1 = _ffn_fwd(xs, in1[0], out1[0], lg[0, 0], lb[0, 0], "a1")
    qkv_a = _mm_nn(y1b, W["a_w_qkv"], F32, "qkv_a", split=True)
    mix_a, o_a, lse_a = _attn_fwd(qkv_a, slopes, None, PATTERNS_A, "attn_a_fwd")
    y2, y2b, z2 = _mm_ln(mix_a, W["a_w_o"], y1, lg[0, 1], lb[0, 1], 1.0, "attn_a_out_ln")
    y3, y3b, s3 = _ffn_fwd(y2, in2[0], out2[0], lg[0, 2], lb[0, 2], "a2")
    kv = _mm_nn(y3b, W["kv_w"], F32, "kv_proj")
    W = dict(W, **layer1_weights(kv))
    in1, out1, in2, out2 = (in1 + [W["ffn1_w_in"]], out1 + [W["ffn1_w_out"]], in2 + [W["ffn2_w_in"]],
                            out2 + [W["ffn2_w_out"]])
    y4, y4b, s4 = _ffn_fwd(y3, in1[1], out1[1], lg[1, 0], lb[1, 0], "b1")
    q_b = _mm_nn(y4b, W["b_w_q"], F32, "q_b")
    k_sh = kv[:, :N_KV_B * HEAD_DIM].reshape(S, N_KV_B, 1, HEAD_DIM)
    v_sh = kv[:, N_KV_B * HEAD_DIM:].reshape(S, N_KV_B, 1, HEAD_DIM)
    k_exp = jnp.broadcast_to(k_sh, (S, N_KV_B, GROUP_B, HEAD_DIM)).reshape(S, D_MODEL)
    v_exp = jnp.broadcast_to(v_sh, (S, N_KV_B, GROUP_B, HEAD_DIM)).reshape(S, D_MODEL)
    qkv_b = jnp.stack([q_b, k_exp, v_exp])
    mix_b, o_b, lse_b = _attn_fwd(qkv_b, slopes, sinks, PATTERNS_B, "attn_b_fwd")
    y5, y5b, z5 = _mm_ln(mix_b, W["b_w_o"], y4, lg[1, 1], lb[1, 1], 1.0, "attn_b_out_ln")
    y6, _, s6 = _ffn_fwd(y5, in2[1], out2[1], lg[1, 2], lb[1, 2], "b2")

    dy6, sq = _loss_grad(y6, target, "loss_grad")
    gr = {n: None for n in BIG}
    gg = [[None] * 3 for _ in range(DEPTH)]
    gb = [[None] * 3 for _ in range(DEPTH)]

    dy5, d_in2_b, d_out2_b, gg[1][2], gb[1][2] = _ffn_bwd(dy6, s6, in2[1], out2[1], lg[1, 2], y5b, "b2")
    dz5, dz5b, gg[1][1], gb[1][1] = _ln_bwd(z5, dy5, lg[1, 1], 1.0, "ln_bwd_attn_b")
    gr["b_w_o"] = _mm_tn(mix_b, dz5b, "d_b_w_o")
    dmix_b = _mm_nt(dz5b, W["b_w_o"], "d_mix_b")
    dqkv_b, dsink_part = _attn_bwd(qkv_b, dmix_b, o_b, lse_b, slopes, sinks, PATTERNS_B, "attn_b_bwd")
    dq_b, dk_exp, dv_exp = (dqkv_b, 0), dqkv_b[1], dqkv_b[2]
    dkv = jnp.concatenate([dk_exp.reshape(S, N_KV_B, GROUP_B, HEAD_DIM).sum(axis=2).reshape(S, -1),
                           dv_exp.reshape(S, N_KV_B, GROUP_B, HEAD_DIM).sum(axis=2).reshape(S, -1)], axis=1)
    gr["b_w_q"] = _mm_tn(y4b, dq_b, "d_b_w_q")
    dy4 = _mm_nt(dq_b, W["b_w_q"], "d_y4", add=dz5, add_scale=ALPHA)
    dy3, d_in1_b, d_out1_b, gg[1][0], gb[1][0] = _ffn_bwd(dy4, s4, in1[1], out1[1], lg[1, 0], y3b, "b1")
    gr["kv_w"] = _mm_tn(y3b, dkv, "d_kv_w")
    dy3 = _mm_nt(dkv, W["kv_w"], "d_y3_kv", add=dy3, add_scale=1.0)
    tok = grads_ready("l1", {("ffn2_w_in", 1): d_in2_b, ("ffn2_w_out", 1): d_out2_b, ("b_w_o", None): gr["b_w_o"],
                             ("b_w_q", None): gr["b_w_q"], ("ffn1_w_in", 1): d_in1_b, ("ffn1_w_out", 1): d_out1_b,
                             ("kv_w", None): gr["kv_w"]}, True)
    lg0 = lg[0] + tok

    dy2, d_in2_a, d_out2_a, gg[0][2], gb[0][2] = _ffn_bwd(dy3, s3, in2[0], out2[0], lg0[2], y2b, "a2")
    tok = grads_ready("a2", {("ffn2_w_in", 0): d_in2_a, ("ffn2_w_out", 0): d_out2_a}, True)
    lg0 = lg0 + tok
    dz2, dz2b, gg[0][1], gb[0][1] = _ln_bwd(z2, dy2, lg0[1], 1.0, "ln_bwd_attn_a")
    gr["a_w_o"] = _mm_tn(mix_a, dz2b, "d_a_w_o")
    dmix_a = _mm_nt(dz2b, W["a_w_o"], "d_mix_a")
    dqkv_a, _ = _attn_bwd(qkv_a, dmix_a, o_a, lse_a, slopes, None, PATTERNS_A, "attn_a_bwd")
    gr["a_w_qkv"] = _mm_tn(y1b, dqkv_a, "d_a_w_qkv", split=True)
    dy1 = _mm_nt(dqkv_a, W["a_w_qkv"], "d_y1", add=dz2, add_scale=ALPHA, split=True)
    grad_x, d_in1_a, d_out1_a, gg[0][0], gb[0][0] = _ffn_bwd(dy1, s1, in1[0], out1[0], lg0[0], xs, "a1")
    grads_ready("a1", {("a_w_o", None): gr["a_w_o"], ("a_w_qkv", None): gr["a_w_qkv"], ("ffn1_w_in", 0): d_in1_a,
                       ("ffn1_w_out", 0): d_out1_a}, False)
    gr["ffn1_w_in"] = [d_in1_a, d_in1_b]
    gr["ffn1_w_out"] = [d_out1_a, d_out1_b]
    gr["ffn2_w_in"] = [d_in2_a, d_in2_b]
    gr["ffn2_w_out"] = [d_out2_a, d_out2_b]
    return sq, grad_x, gr, gg, gb, dsink_part


def _grad_item(name, layer, g):
    if name.endswith("w_in"):
        return (g, "col", HALF_FF, _slot, name, layer)
    if name.endswith("w_out"):
        return (g, "row", D_MODEL, None, name, layer)
    if name == "a_w_qkv":
        return (g, "col", QKV_SHARD, lambda q: q, name, None)
    return (g, "row", g.shape[1], None, name, None)


class _GradReducer:
    def __init__(self, c_idx, myq, shard_shapes):
        self.c_idx, self.myq, self.shard_shapes = c_idx, myq, shard_shapes
        self.groups = []

    def begin(self, tag, grads, overlap):
        items = [_grad_item(n, l, g) for (n, l), g in grads.items()]
        kinds, widths, colblocks = [it[1] for it in items], [it[2] for it in items], [it[3] for it in items]
        views = [_grad_view(k, it[0]) for k, it in zip(kinds, items)]
        from_sibling = _pair_exchange(views, kinds, "grad_pair_exchange_" + tag)
        sums = [_pair_sum(k, v, r, self.c_idx, "pair_sum_%s_%d" % (tag, t))
                for t, (k, v, r) in enumerate(zip(kinds, views, from_sibling))]
        if not overlap:
            self.groups.append((tag, items, sums, None))
            return 0.0
        state, token = _split_start("grad_chip_start_" + tag, _chip_copies(kinds, widths, colblocks), 3 * len(items), sums,
                                    _chip_land_shapes(sums, kinds, widths), sums[-1])
        self.groups.append((tag, items, None, state))
        return token[0, 0]

    def finish(self, after):
        half_done = {}
        for tag, items, sums, state in self.groups:
            kinds, widths, colblocks = [it[1] for it in items], [it[2] for it in items], [it[3] for it in items]
            if state is not None:
                sums, from_chips = _split_wait("grad_chip_wait_" + tag, _chip_copies(kinds, widths, colblocks), state, after)
            else:
                from_chips = _chip_exchange(sums, kinds, widths, colblocks, "grad_chip_exchange_" + tag)
            for t, (it, s, r) in enumerate(zip(items, sums, from_chips)):
                _, k, _, cb, name, layer = it
                own = cb(self.myq) if k == "col" else self.myq
                half_done[name] = _chip_sum(k, s, r, own, self.c_idx, self.shard_shapes[name], layer, half_done.get(name),
                                            "chip_sum_%s_%d" % (tag, t))
        return dict(zip(BIG, _share_halves([half_done[name] for name in BIG])))


def _update(reducer, grad_x, loss, grad_ln_g, grad_ln_b, grad_sinks, ws, ms, vs, small_w, small_m, small_v):
    ln_g, ln_b, b_sinks = small_w
    m_ln_g, m_ln_b, m_b_sinks = small_m
    v_ln_g, v_ln_b, v_b_sinks = small_v

    grads = reducer.finish(grad_x)

    deltas, new_m, new_v = {}, {}, {}
    for name in BIG:
        shp = ws[name].shape
        flat = lambda a: a.reshape(-1, shp[-1])
        d, nm, nv = _adamw(flat(ws[name]), flat(grads[name]), flat(ms[name]), flat(vs[name]), "adamw_" + name)
        deltas[name], new_m[name], new_v[name] = d.reshape(shp), nm.reshape(shp), nv.reshape(shp)
    delta_s, nm_s, nv_s = _adamw(_pack_small(ln_g, ln_b, b_sinks), _pack_small(grad_ln_g, grad_ln_b, grad_sinks),
                                 _pack_small(m_ln_g, m_ln_b, m_b_sinks), _pack_small(v_ln_g, v_ln_b, v_b_sinks), "adamw_small")
    for d, blob in ((grads, None), (deltas, delta_s), (new_m, nm_s), (new_v, nv_s)):
        if blob is None:
            d["ln_g"], d["ln_b"], d["b_sinks"] = grad_ln_g, grad_ln_b, grad_sinks
        else:
            d["ln_g"], d["ln_b"], d["b_sinks"] = _unpack_small(blob, ln_g.shape, b_sinks.shape)

    order = ("ffn1_w_in", "ffn1_w_out", "ffn2_w_in", "ffn2_w_out", "ln_g", "ln_b", "a_w_qkv", "a_w_o", "kv_w", "b_w_q",
             "b_sinks", "b_w_o")
    outs = [loss, grad_x[None]]
    for d in (grads, deltas, new_m, new_v):
        outs += [d[n] for n in order]
    return tuple(outs)
```

```python
import numpy as np
import jax
import jax.numpy as jnp
from jax import lax
from jax.experimental import pallas as pl
from jax.experimental.pallas import tpu as pltpu

F32 = jnp.float32
BF16 = jnp.bfloat16

D_MODEL = 1024
D_FF = 2816
HALF_FF = D_FF // 2
HEAD_DIM = 64
N_HEADS = 16
N_KV_B = 4
GROUP_B = N_HEADS // N_KV_B
DEPTH = 2
ALPHA = (2.0 * DEPTH) ** 0.25
LN_EPS = 1e-5
BLOCK = 128
SLAB = 128
N_SLABS = D_MODEL // SLAB
PATTERNS_A = ((1, 128, 1.0), (4, 128, 4.0), (16, 128, 16.0))
PATTERNS_B = ((1, 127, 1.0),)
NEG = -1e30

ADAM_LR = 0.001
ADAM_B1 = 0.9
ADAM_B2 = 0.999
ADAM_EPS = 1e-08
ADAM_WD = 0.01
ADAM_STEP = 10

N_CHIPS = 4
VMEM_LIMIT = 56 * 1024 * 1024
MESH = pl.DeviceIdType.MESH


def _alibi_slopes(n):
    return np.array([2.0 ** (-8.0 * (h + 1) / n) for h in range(n)], dtype=np.float32)


def _cparams(sem=None, vmem=VMEM_LIMIT):
    return pltpu.CompilerParams(dimension_semantics=sem, vmem_limit_bytes=vmem)


_DIMS = {"nn": ((1,), (0,)), "nt": ((1,), (1,)), "tn": ((0,), (0,))}


def _unlead(x):
    if isinstance(x, tuple):
        return x[0], x[1], x[0].shape[1:]
    return x, None, x.shape


def _bspec(block, imap, lead=None):
    if lead is None:
        return pl.BlockSpec(block, imap)
    return pl.BlockSpec((None,) + tuple(block), lambda *g: (lead,) + tuple(imap(*g)))


def _matmul(a, b, mode, out_dtype, tm, tn, tk, name, add=None, add_scale=1.0, split=False):
    out_spec = pl.BlockSpec((tm, tn), lambda i, j, k: (i, j))
    if mode == "nn":
        a, al, (M, K) = _unlead(a)
        b, bl, (K2, N) = _unlead(b)
        a_spec = _bspec((tm, tk), lambda i, j, k: (i, k), al)
        b_spec = _bspec((tk, tn), lambda i, j, k: (k, j), bl)
        out_struct = jax.ShapeDtypeStruct((M, N), out_dtype)
        if split:
            assert tn == D_MODEL
            out_spec = pl.BlockSpec((None, tm, tn), lambda i, j, k: (j, i, 0))
            out_struct = jax.ShapeDtypeStruct((N // tn, M, tn), out_dtype)
    elif mode == "nt":
        b, bl, (N, K2) = _unlead(b)
        if split:
            assert tk == D_MODEL
            M, K = a.shape[1], a.shape[0] * a.shape[2]
            a_spec = pl.BlockSpec((None, tm, tk), lambda i, j, k: (k, i, 0))
        else:
            a, al, (M, K) = _unlead(a)
            a_spec = _bspec((tm, tk), lambda i, j, k: (i, k), al)
        b_spec = _bspec((tn, tk), lambda i, j, k: (j, k), bl)
        out_struct = jax.ShapeDtypeStruct((M, N), out_dtype)
    else:
        a, al, (K, M) = _unlead(a)
        if split:
            assert tn == D_MODEL
            K2, N = b.shape[1], b.shape[0] * b.shape[2]
            b_spec = pl.BlockSpec((None, tk, tn), lambda i, j, k: (j, k, 0))
        else:
            b, bl, (K2, N) = _unlead(b)
            b_spec = _bspec((tk, tn), lambda i, j, k: (k, j), bl)
        a_spec = _bspec((tk, tm), lambda i, j, k: (k, i), al)
        out_struct = jax.ShapeDtypeStruct((M, N), out_dtype)
    assert K == K2 and M % tm == 0 and N % tn == 0 and K % tk == 0, (a.shape, b.shape, mode, tm, tn, tk)
    nk = K // tk
    dims = (_DIMS[mode], ((), ()))
    has_add = add is not None

    assert out_dtype == F32

    def body(*refs):
        if has_add:
            a_ref, b_ref, add_ref, o_ref = refs
        else:
            a_ref, b_ref, o_ref = refs
        k = pl.program_id(2)
        part = lax.dot_general(a_ref[...].astype(BF16), b_ref[...].astype(BF16), dims, preferred_element_type=F32)
        if has_add:
            @pl.when(k == 0)
            def _():
                o_ref[...] = part + add_scale * add_ref[...]
        else:
            @pl.when(k == 0)
            def _():
                o_ref[...] = part

        @pl.when(k > 0)
        def _():
            o_ref[...] += part

    in_specs = [a_spec, b_spec]
    args = [a, b]
    if has_add:
        in_specs.append(pl.BlockSpec((tm, tn), lambda i, j, k: (i, j)))
        args.append(add)
    return pl.pallas_call(
        body, name=name, grid=(M // tm, N // tn, nk),
        in_specs=in_specs, out_specs=out_spec, out_shape=out_struct,
        compiler_params=_cparams(("parallel", "parallel", "arbitrary")),
    )(*args)


def _pick(n, cands):
    for c in cands:
        if n % c == 0:
            return c
    raise ValueError((n, cands))


def _mm_nn(a, b, out_dtype, name, split=False):
    M, K = _unlead(a)[2]
    N = _unlead(b)[2][1]
    return _matmul(a, b, "nn", out_dtype, _pick(M, (1024, 512, 256)), _pick(N, (1024, 512)), _pick(K, (1024, 512)), name,
                   split=split)


def _mm_nt(a, b, name, add=None, add_scale=1.0, split=False):
    M, K = (a.shape[1], D_MODEL) if split else _unlead(a)[2]
    N = _unlead(b)[2][0]
    return _matmul(a, b, "nt", F32, _pick(M, (1024, 512, 256)), _pick(N, (1024, 512)),
                   _pick(K, (2816, 1024, 512)), name, add=add, add_scale=add_scale, split=split)


def _mm_tn(a, b, name, split=False):
    K, M = _unlead(a)[2]
    N = D_MODEL if split else _unlead(b)[2][1]
    return _matmul(a, b, "tn", F32, _pick(M, (1024, 1408, 512)), _pick(N, (1408, 1024, 512)),
                   _pick(K, (2048, 1024, 512, 256)), name, split=split)


def _ffn_in(x, w, name):
    S = x.shape[0]
    tm = _pick(S, (512, 256))
    w, wl, _ = _unlead(w)

    def body(x_ref, w_ref, t_ref, h_ref):
        acc = jnp.dot(x_ref[...].astype(BF16), w_ref[...], preferred_element_type=F32)
        g = acc[:, :HALF_FF]
        up = acc[:, HALF_FF:]
        sg = jax.nn.sigmoid(g)
        silu = g * sg
        t_ref[:, :HALF_FF] = (up * (sg * (1.0 + g * (1.0 - sg)))).astype(BF16)
        t_ref[:, HALF_FF:] = silu.astype(BF16)
        h_ref[...] = (silu * up).astype(BF16)

    return pl.pallas_call(
        body, name=name, grid=(2, S // tm),
        in_specs=[pl.BlockSpec((tm, D_MODEL), lambda j, i: (i, 0)),
                  _bspec((D_MODEL, D_FF), lambda j, i: (0, j), wl)],
        out_specs=[pl.BlockSpec((tm, D_FF), lambda j, i: (i, j)),
                   pl.BlockSpec((tm, HALF_FF), lambda j, i: (i, j))],
        out_shape=[jax.ShapeDtypeStruct((S, 2 * D_FF), BF16), jax.ShapeDtypeStruct((S, D_FF), BF16)],
        compiler_params=_cparams(("parallel", "parallel")),
    )(x, w)


def _ffn_bwd_h(dzc, w_out, u, name):
    S = dzc.shape[0]
    tm = _pick(S, (512, 256))
    w_out, wl, _ = _unlead(w_out)

    def body(dz_ref, w_ref, t_ref, du_ref):
        dh = lax.dot_general(dz_ref[...], w_ref[...], (((1,), (1,)), ((), ())), preferred_element_type=F32)
        du_ref[:, :HALF_FF] = (dh * t_ref[:, :HALF_FF].astype(F32)).astype(BF16)
        du_ref[:, HALF_FF:] = (dh * t_ref[:, HALF_FF:].astype(F32)).astype(BF16)

    return pl.pallas_call(
        body, name=name, grid=(2, S // tm),
        in_specs=[pl.BlockSpec((tm, D_MODEL), lambda j, i: (i, 0)),
                  _bspec((HALF_FF, D_MODEL), lambda j, i: (j, 0), wl),
                  pl.BlockSpec((tm, D_FF), lambda j, i: (i, j))],
        out_specs=pl.BlockSpec((tm, D_FF), lambda j, i: (i, j)),
        out_shape=jax.ShapeDtypeStruct((S, 2 * D_FF), BF16),
        compiler_params=_cparams(("parallel", "parallel")),
    )(dzc, w_out, u)


def _mm_ln(a, w, resid, gain, bias, c, name):
    S, K = a.shape
    tm = _pick(S, (512, 256))
    w, wl, _ = _unlead(w)

    def body(a_ref, w_ref, r_ref, g_ref, b_ref, y_ref, yb_ref, z_ref):
        z = ALPHA * r_ref[...] + c * jnp.dot(a_ref[...], w_ref[...], preferred_element_type=F32)
        mu = jnp.mean(z, axis=-1, keepdims=True)
        zc = z - mu
        var = jnp.mean(zc * zc, axis=-1, keepdims=True)
        y = zc * lax.rsqrt(var + LN_EPS) * g_ref[...] + b_ref[...]
        z_ref[...] = z
        y_ref[...] = y
        yb_ref[...] = y.astype(BF16)

    row = pl.BlockSpec((tm, D_MODEL), lambda i: (i, 0))
    vec = pl.BlockSpec((1, D_MODEL), lambda i: (0, 0))
    return pl.pallas_call(
        body, name=name, grid=(S // tm,),
        in_specs=[pl.BlockSpec((tm, K), lambda i: (i, 0)), _bspec((K, D_MODEL), lambda i: (0, 0), wl), row, vec, vec],
        out_specs=[row, row, row],
        out_shape=[jax.ShapeDtypeStruct((S, D_MODEL), F32), jax.ShapeDtypeStruct((S, D_MODEL), BF16),
                   jax.ShapeDtypeStruct((S, D_MODEL), F32)],
        compiler_params=_cparams(("parallel",)),
    )(a, w, resid, gain, bias)


def _ln_bwd(z, dy, gain, c, name):
    S = z.shape[0]
    tm = _pick(S, (512, 256))

    def body(z_ref, dy_ref, g_ref, dz_ref, dzc_ref, gg_ref, gb_ref):
        i = pl.program_id(0)
        zv = z_ref[...]
        dyv = dy_ref[...]
        mu = jnp.mean(zv, axis=-1, keepdims=True)
        zc = zv - mu
        var = jnp.mean(zc * zc, axis=-1, keepdims=True)
        rstd = lax.rsqrt(var + LN_EPS)
        xhat = zc * rstd
        dyg = dyv * g_ref[...]
        m1 = jnp.mean(dyg, axis=-1, keepdims=True)
        m2 = jnp.mean(dyg * xhat, axis=-1, keepdims=True)
        dz = rstd * (dyg - m1 - xhat * m2)
        dz_ref[...] = dz
        dzc_ref[...] = (c * dz).astype(BF16)
        pg = jnp.sum((dyv * xhat).reshape(tm // 8, 8, D_MODEL), axis=0)
        pb = jnp.sum(dyv.reshape(tm // 8, 8, D_MODEL), axis=0)

        @pl.when(i == 0)
        def _():
            gg_ref[...] = pg
            gb_ref[...] = pb

        @pl.when(i > 0)
        def _():
            gg_ref[...] += pg
            gb_ref[...] += pb

    row = pl.BlockSpec((tm, D_MODEL), lambda i: (i, 0))
    part = pl.BlockSpec((8, D_MODEL), lambda i: (0, 0))
    return pl.pallas_call(
        body, name=name, grid=(S // tm,),
        in_specs=[row, row, pl.BlockSpec((1, D_MODEL), lambda i: (0, 0))],
        out_specs=[row, row, part, part],
        out_shape=[jax.ShapeDtypeStruct((S, D_MODEL), F32), jax.ShapeDtypeStruct((S, D_MODEL), BF16),
                   jax.ShapeDtypeStruct((8, D_MODEL), F32), jax.ShapeDtypeStruct((8, D_MODEL), F32)],
        compiler_params=_cparams(("arbitrary",)),
    )(z, dy, gain)


def _loss_grad(y, t, name):
    S = y.shape[0]
    tm = _pick(S, (512, 256))

    def body(y_ref, t_ref, dy_ref, sq_ref):
        i = pl.program_id(0)
        e = y_ref[...] - t_ref[...]
        dy_ref[...] = e * (1.0 / D_MODEL)
        ps = jnp.sum((e * e).reshape(tm // 8, 8, D_MODEL), axis=0)

        @pl.when(i == 0)
        def _():
            sq_ref[...] = ps

        @pl.when(i > 0)
        def _():
            sq_ref[...] += ps

    row = pl.BlockSpec((tm, D_MODEL), lambda i: (i, 0))
    return pl.pallas_call(
        body, name=name, grid=(S // tm,),
        in_specs=[row, row], out_specs=[row, pl.BlockSpec((8, D_MODEL), lambda i: (0, 0))],
        out_shape=[jax.ShapeDtypeStruct((S, D_MODEL), F32), jax.ShapeDtypeStruct((8, D_MODEL), F32)],
        compiler_params=_cparams(("arbitrary",)),
    )(y, t)


def _rows(start, d):
    if d == 1:
        return pl.ds(pl.multiple_of(start, BLOCK), BLOCK)
    return pl.ds(start, BLOCK, stride=d)


def _ld(ref, start, d):
    return ref[_rows(start, d), :]


def _ld3(ref, lead, start, d):
    return ref[lead, _rows(start, d), :]


def _st3(ref, lead, start, d, val):
    ref[lead, _rows(start, d), :] = val


def _acc3(ref, lead, start, d, val):
    ref[lead, _rows(start, d), :] = ref[lead, _rows(start, d), :] + val


def _band_consts(slope0, slope1, maxd, scale):
    row = lax.broadcasted_iota(jnp.int32, (2 * BLOCK, 2 * BLOCK), 0)
    kj = lax.broadcasted_iota(jnp.int32, (2 * BLOCK, 2 * BLOCK), 1)
    top = row < BLOCK
    dist = BLOCK + jnp.where(top, row, row - BLOCK) - kj
    slope = jnp.where(top, slope0, slope1)
    base = jnp.where((dist >= 0) & (dist <= maxd), -(slope * (dist.astype(F32) * scale)), NEG)
    return base, kj < BLOCK


def _stack_heads(x, lo):
    return jnp.concatenate([jnp.where(lo, x, 0.0), jnp.where(lo, 0.0, x)], axis=0)


def _unstack_heads(x2, lo):
    return jnp.where(lo, x2[:BLOCK], x2[BLOCK:])


def _scores(q2, k2, base, prev_keys, first):
    s = lax.dot_general(q2, k2, (((1,), (1,)), ((), ())), preferred_element_type=F32) * (HEAD_DIM ** -0.5) + base
    return jnp.where(jnp.logical_and(prev_keys, first), NEG, s)


def _softmax_weights(ls):
    mx = ls[0]
    for l in ls[1:]:
        mx = jnp.maximum(mx, l)
    es = [jnp.exp(l - mx) for l in ls]
    tot = es[0]
    for e in es[1:]:
        tot = tot + e
    inv = 1.0 / tot
    return [e * inv for e in es]


def _attn_fwd(qkv, slopes, sinks, patterns, name):
    S = qkv.shape[1]
    npat = len(patterns)
    has_sink = sinks is not None
    if not has_sink:
        sinks = jnp.zeros((N_HEADS,), F32)
    rows_c = 256

    def body(slopes_ref, sinks_ref, x_ref, mix_ref, o_ref, lse_ref):
        p = pl.program_id(0)
        lo = lax.broadcasted_iota(jnp.int32, (BLOCK, SLAB), 1) < HEAD_DIM
        top1 = lax.broadcasted_iota(jnp.int32, (2 * BLOCK, 1), 0) < BLOCK
        sk2 = jnp.where(top1, sinks_ref[2 * p], sinks_ref[2 * p + 1])
        for pi, (d, maxd, scale) in enumerate(patterns):
            nb = S // d // BLOCK
            base, prev_keys = _band_consts(slopes_ref[2 * p], slopes_ref[2 * p + 1], maxd, scale)

            def blk(t, carry, pi=pi, d=d, nb=nb, base=base, prev_keys=prev_keys):
                r = t // nb
                n = t - r * nb
                start = r + (d * BLOCK) * n
                prev = jnp.where(n > 0, start - d * BLOCK, start)
                q2 = _stack_heads(_ld3(x_ref, 0, start, d), lo).astype(BF16)
                k2 = jnp.concatenate([_ld3(x_ref, 1, prev, d), _ld3(x_ref, 1, start, d)], axis=0).astype(BF16)
                v2 = jnp.concatenate([_ld3(x_ref, 2, prev, d), _ld3(x_ref, 2, start, d)], axis=0).astype(BF16)
                s = _scores(q2, k2, base, prev_keys, n == 0)
                m = jnp.max(s, axis=-1, keepdims=True)
                if has_sink:
                    m = jnp.maximum(m, sk2)
                e = jnp.exp(s - m)
                den = jnp.sum(e, axis=-1, keepdims=True)
                if has_sink:
                    den = den + jnp.exp(sk2 - m)
                o2 = jnp.dot((e / den).astype(BF16), v2, preferred_element_type=F32)
                _st3(o_ref, pi, start, d, _unstack_heads(o2, lo))
                _st3(lse_ref, pi, start, d, _unstack_heads(m + jnp.log(den), lo))
                return carry

            lax.fori_loop(0, d * nb, blk, 0, unroll=8)

        def comb(ci, carry):
            rows = pl.ds(pl.multiple_of(ci * rows_c, rows_c), rows_c)
            if npat == 1:
                mix_ref[rows, :] = o_ref[0, rows, :].astype(BF16)
            else:
                ws = _softmax_weights([lse_ref[i, rows, :] for i in range(npat)])
                acc = ws[0] * o_ref[0, rows, :]
                for i in range(1, npat):
                    acc = acc + ws[i] * o_ref[i, rows, :]
                mix_ref[rows, :] = acc.astype(BF16)
            return carry

        lax.fori_loop(0, S // rows_c, comb, 0)

    smem = pl.BlockSpec(memory_space=pltpu.SMEM)
    slab3 = pl.BlockSpec((npat, S, SLAB), lambda p: (0, 0, p))
    return pl.pallas_call(
        body, name=name, grid=(N_SLABS,),
        in_specs=[smem, smem, pl.BlockSpec((3, S, SLAB), lambda p: (0, 0, p))],
        out_specs=[pl.BlockSpec((S, SLAB), lambda p: (0, p)), slab3, slab3],
        out_shape=[jax.ShapeDtypeStruct((S, D_MODEL), BF16), jax.ShapeDtypeStruct((npat, S, D_MODEL), F32),
                   jax.ShapeDtypeStruct((npat, S, D_MODEL), F32)],
        compiler_params=_cparams(("arbitrary",)),
    )(slopes, sinks, qkv)


def _attn_bwd(qkv, dout, o, lse, slopes, sinks, patterns, name):
    S = qkv.shape[1]
    npat = len(patterns)
    has_sink = sinks is not None
    if not has_sink:
        sinks = jnp.zeros((N_HEADS,), F32)
    rows_c = 256

    def headsum(x, lo):
        s0 = jnp.sum(jnp.where(lo, x, 0.0), axis=-1, keepdims=True)
        s1 = jnp.sum(jnp.where(lo, 0.0, x), axis=-1, keepdims=True)
        return jnp.where(lo, s0, s1)

    def body(slopes_ref, sinks_ref, x_ref, do_ref, o_ref, lse_ref, dx_ref, dsink_ref, dbar_ref, sacc_ref):
        p = pl.program_id(0)
        lo = lax.broadcasted_iota(jnp.int32, (BLOCK, SLAB), 1) < HEAD_DIM
        lo_c = lax.broadcasted_iota(jnp.int32, (rows_c, SLAB), 1) < HEAD_DIM
        top1 = lax.broadcasted_iota(jnp.int32, (2 * BLOCK, 1), 0) < BLOCK
        sk2 = jnp.where(top1, sinks_ref[2 * p], sinks_ref[2 * p + 1])

        def prep(ci, carry):
            rows = pl.ds(pl.multiple_of(ci * rows_c, rows_c), rows_c)
            dov = do_ref[rows, :]
            dx_ref[:, rows, :] = jnp.zeros((3, rows_c, SLAB), F32)
            if npat == 1:
                dbar_ref[rows, :] = headsum(dov * o_ref[0, rows, :], lo_c)
            else:
                ws = _softmax_weights([lse_ref[i, rows, :] for i in range(npat)])
                acc = ws[0] * headsum(dov * o_ref[0, rows, :], lo_c)
                for i in range(1, npat):
                    acc = acc + ws[i] * headsum(dov * o_ref[i, rows, :], lo_c)
                dbar_ref[rows, :] = acc
            return carry

        lax.fori_loop(0, S // rows_c, prep, 0)
        sacc_ref[...] = jnp.zeros((BLOCK, SLAB), F32)

        for pi, (d, maxd, scale) in enumerate(patterns):
            nb = S // d // BLOCK
            base, prev_keys = _band_consts(slopes_ref[2 * p], slopes_ref[2 * p + 1], maxd, scale)

            def blk(t, carry, pi=pi, d=d, nb=nb, base=base, prev_keys=prev_keys):
                r = t // nb
                n = t - r * nb
                start = r + (d * BLOCK) * n
                prev = jnp.where(n > 0, start - d * BLOCK, start)
                q2 = _stack_heads(_ld3(x_ref, 0, start, d), lo).astype(BF16)
                k2 = jnp.concatenate([_ld3(x_ref, 1, prev, d), _ld3(x_ref, 1, start, d)], axis=0).astype(BF16)
                v2 = jnp.concatenate([_ld3(x_ref, 2, prev, d), _ld3(x_ref, 2, start, d)], axis=0).astype(BF16)
                ls = [_ld3(lse_ref, i, start, d) for i in range(npat)]
                w = _softmax_weights(ls)[pi] if npat > 1 else 1.0
                do2 = _stack_heads(w * _ld(do_ref, start, d), lo).astype(BF16)
                dl = w * _ld(dbar_ref, start, d)
                lse2 = jnp.concatenate([ls[pi][:, :1], ls[pi][:, HEAD_DIM:HEAD_DIM + 1]], axis=0)
                dl2 = jnp.concatenate([dl[:, :1], dl[:, HEAD_DIM:HEAD_DIM + 1]], axis=0)
                s = _scores(q2, k2, base, prev_keys, n == 0)
                pr = jnp.exp(s - lse2)
                dp = lax.dot_general(do2, v2, (((1,), (1,)), ((), ())), preferred_element_type=F32)
                ds = (pr * (dp - dl2) * (HEAD_DIM ** -0.5)).astype(BF16)
                dq2 = jnp.dot(ds, k2, preferred_element_type=F32)
                dk2 = lax.dot_general(ds, q2, (((0,), (0,)), ((), ())), preferred_element_type=F32)
                dv2 = lax.dot_general(pr.astype(BF16), do2, (((0,), (0,)), ((), ())), preferred_element_type=F32)
                _acc3(dx_ref, 0, start, d, _unstack_heads(dq2, lo))
                _acc3(dx_ref, 1, prev, d, dk2[:BLOCK])
                _acc3(dx_ref, 1, start, d, dk2[BLOCK:])
                _acc3(dx_ref, 2, prev, d, dv2[:BLOCK])
                _acc3(dx_ref, 2, start, d, dv2[BLOCK:])
                if has_sink:
                    sacc_ref[...] += _unstack_heads(-jnp.exp(sk2 - lse2) * dl2, lo)
                return carry

            lax.fori_loop(0, d * nb, blk, 0, unroll=4)

        dsink_ref[...] = jnp.broadcast_to(jnp.sum(sacc_ref[...], axis=0, keepdims=True), (8, SLAB))

    smem = pl.BlockSpec(memory_space=pltpu.SMEM)
    one = pl.Buffered(1)
    slab3 = pl.BlockSpec((npat, S, SLAB), lambda p: (0, 0, p), pipeline_mode=one)
    return pl.pallas_call(
        body, name=name, grid=(N_SLABS,),
        in_specs=[smem, smem, pl.BlockSpec((3, S, SLAB), lambda p: (0, 0, p), pipeline_mode=one),
                  pl.BlockSpec((S, SLAB), lambda p: (0, p), pipeline_mode=one), slab3, slab3],
        out_specs=[pl.BlockSpec((3, S, SLAB), lambda p: (0, 0, p)), pl.BlockSpec((None, 8, SLAB), lambda p: (p, 0, 0))],
        out_shape=[jax.ShapeDtypeStruct((3, S, D_MODEL), F32), jax.ShapeDtypeStruct((N_SLABS, 8, SLAB), F32)],
        scratch_shapes=[pltpu.VMEM((S, SLAB), F32), pltpu.VMEM((BLOCK, SLAB), F32)],
        compiler_params=_cparams(("arbitrary",)),
    )(slopes, sinks, qkv, dout, o, lse)


def _place():
    x, y, c = lax.axis_index("x"), lax.axis_index("y"), lax.axis_index("c")
    return x, y, c, 2 * x + y


def _other_chips(x, y):
    return [(1 - x, y), (x, 1 - y), (1 - x, 1 - y)]


HBM_SPEC = pl.BlockSpec(memory_space=pl.ANY)


def _slot(q):
    return 2 * (q % 2) + q // 2


BIG = ("ffn1_w_in", "ffn1_w_out", "ffn2_w_in", "ffn2_w_out", "a_w_qkv", "a_w_o", "kv_w", "b_w_q", "b_w_o")
QKV_SHARD = 3 * D_MODEL // N_CHIPS
ROW_SHARD = D_MODEL // N_CHIPS


LAYER0_ITEMS = (("ffn1_w_in", 0), ("ffn1_w_out", 0), ("a_w_qkv", None), ("a_w_o", None), ("ffn2_w_in", 0),
                ("ffn2_w_out", 0), ("kv_w", None))
LAYER1_ITEMS = (("ffn1_w_in", 1), ("ffn1_w_out", 1), ("b_w_q", None), ("b_w_o", None), ("ffn2_w_in", 1),
                ("ffn2_w_out", 1))
OUT_SHARD = D_FF // N_CHIPS


def _full_shape(name):
    if name.endswith("w_in"):
        return (D_MODEL, 2 * D_FF)
    if name.endswith("w_out"):
        return (D_FF, D_MODEL)
    if name == "a_w_qkv":
        return (D_MODEL, 3 * D_MODEL)
    if name == "kv_w":
        return (N_CHIPS, 2, ROW_SHARD // 2, 2 * N_KV_B * HEAD_DIM)
    return (N_CHIPS, 2, ROW_SHARD // 2, D_MODEL)


def _gather_src(item, ref, c):
    name, _ = item
    if name.endswith("w_in"):
        return ref.at[pl.ds(c * (D_MODEL // 2), D_MODEL // 2)]
    if name.endswith("w_out"):
        return ref.at[pl.ds(c * (OUT_SHARD // 2), OUT_SHARD // 2)]
    if name == "a_w_qkv":
        return ref.at[0, pl.ds(c * (D_MODEL // 2), D_MODEL // 2)]
    if name == "kv_w":
        return ref.at[pl.ds(c * (ROW_SHARD // 2), ROW_SHARD // 2)]
    return ref.at[0, pl.ds(c * (ROW_SHARD // 2), ROW_SHARD // 2)]


def _gather_dst(item, ref, q, c):
    name, _ = item
    if name.endswith("w_in"):
        return ref.at[pl.ds(c * (D_MODEL // 2), D_MODEL // 2), pl.ds(_slot(q) * HALF_FF, HALF_FF)]
    if name.endswith("w_out"):
        return ref.at[pl.ds(q * OUT_SHARD + c * (OUT_SHARD // 2), OUT_SHARD // 2)]
    if name == "a_w_qkv":
        return ref.at[pl.ds(c * (D_MODEL // 2), D_MODEL // 2), pl.ds(q * QKV_SHARD, QKV_SHARD)]
    return ref.at[q, c]


def _all_gather(items, shards, small):
    n = len(items)
    r = small.shape[0]
    per = 8

    def body(*refs):
        srcs, small_ref = refs[:n], refs[n]
        dsts, s_ref = refs[n + 1:2 * n + 1], refs[2 * n + 1]
        send_sems, recv_sems = refs[2 * n + 2:]
        x, y, c, myq = _place()
        sibling = (x, y, 1 - c)
        chips = _other_chips(x, y)

        def big(t, k, src, q, h, to):
            return pltpu.make_async_remote_copy(src_ref=src, dst_ref=_gather_dst(items[t], dsts[t], q, h),
                                                send_sem=send_sems.at[per * t + k], recv_sem=recv_sems.at[per * t + k],
                                                device_id=to, device_id_type=MESH)

        def tiny(k, q, to):
            return pltpu.make_async_remote_copy(src_ref=small_ref, dst_ref=s_ref.at[q], send_sem=send_sems.at[per * n + k],
                                                recv_sem=recv_sems.at[per * n + k], device_id=to, device_id_type=MESH)

        first = []
        for j, chip in enumerate(chips):
            first += [big(t, j, _gather_src(items[t], srcs[t], c), myq, c, (*chip, c)) for t in range(n)]
            first.append(tiny(j, myq, (*chip, c)))
        own = [big(t, 6 + h, _gather_src(items[t], srcs[t], h), myq, h, sibling) for t in range(n) for h in (0, 1)]
        own.append(tiny(3, myq, sibling))
        for cp in first + own:
            cp.start()
        passed = []
        for j, (cx, cy) in enumerate(chips):
            q = 2 * cx + cy
            for t in range(n):
                src = _gather_src(items[t], srcs[t], c)
                big(t, j, src, q, c, sibling).wait_recv()
                fwd = big(t, 3 + j, _gather_dst(items[t], dsts[t], q, c), q, c, sibling)
                fwd.start()
                passed.append(fwd)
        for j, (cx, cy) in enumerate(chips):
            q = 2 * cx + cy
            for t in range(n):
                big(t, 3 + j, _gather_src(items[t], srcs[t], c), q, 1 - c, sibling).wait_recv()
            tiny(j, q, sibling).wait_recv()
        for cp in own:
            cp.wait_recv()
        for cp in first + passed + own:
            cp.wait_send()

    outs = pl.pallas_call(
        body, name="all_gather_layer0",
        in_specs=[HBM_SPEC] * (n + 1), out_specs=[HBM_SPEC] * (n + 1),
        out_shape=[jax.ShapeDtypeStruct(_full_shape(name), BF16) for name, _ in items]
        + [jax.ShapeDtypeStruct((N_CHIPS, r, 128), F32)],
        scratch_shapes=[pltpu.SemaphoreType.DMA((per * n + 4,)), pltpu.SemaphoreType.DMA((per * n + 4,))],
    )(*[shards[item] for item in items], small)
    return list(outs[:n]), outs[n]


SEM_SPEC = pl.BlockSpec(memory_space=pltpu.SEMAPHORE)
DATAFLOW = pltpu.SideEffectType.DATAFLOW_SIDE_EFFECTING
PER_ITEM = 8


def _split_start(name, copies, n_sems, sources, land_shapes, after):
    n, m = len(sources), len(land_shapes)

    def body(*refs):
        srcs, lands = refs[:n], refs[n:n + m]
        send_sems, recv_sems = refs[n + m + 1], refs[n + m + 2]
        token = refs[-1]
        for src, dst_there, _, s, peer in copies(srcs, lands):
            pltpu.make_async_remote_copy(src_ref=src, dst_ref=dst_there, send_sem=send_sems.at[s], recv_sem=recv_sems.at[s],
                                         device_id=peer, device_id_type=MESH).start()
        token[...] = jnp.zeros_like(token)

    src_arrays = [pltpu.with_memory_space_constraint(a, pltpu.HBM) for a in sources]
    land_arrays = [pltpu.with_memory_space_constraint(lax.empty(s.shape, s.dtype), pltpu.HBM) for s in land_shapes]
    hbm = pl.BlockSpec(memory_space=pltpu.HBM)
    outs = pl.pallas_call(
        body, name=name,
        in_specs=[hbm] * (n + m) + [HBM_SPEC],
        out_specs=[SEM_SPEC, SEM_SPEC] + [hbm] * (n + m) + [pl.BlockSpec(memory_space=pltpu.VMEM)],
        out_shape=[pltpu.SemaphoreType.DMA((n_sems,)), pltpu.SemaphoreType.DMA((n_sems,))]
        + [pltpu.HBM(a.shape, a.dtype) for a in src_arrays + land_arrays] + [jax.ShapeDtypeStruct((8, 128), F32)],
        input_output_aliases={i: 2 + i for i in range(n + m)},
        compiler_params=pltpu.CompilerParams(has_side_effects=DATAFLOW),
    )(*src_arrays, *land_arrays, after)
    return (outs[0], outs[1], list(outs[2:2 + n]), list(outs[2 + n:2 + n + m])), outs[-1]


def _split_wait(name, copies, state, after):
    send_sems, recv_sems, srcs_thru, lands_thru = state
    n, m = len(srcs_thru), len(lands_thru)

    def body(*refs):
        srcs, lands = refs[:n], refs[n:n + m]
        send_sems, recv_sems = refs[n + m], refs[n + m + 1]
        for src, _, dst_here, s, peer in copies(srcs, lands):
            cp = pltpu.make_async_remote_copy(src_ref=src, dst_ref=dst_here, send_sem=send_sems.at[s], recv_sem=recv_sems.at[s],
                                              device_id=peer, device_id_type=MESH)
            cp.wait_send()
            cp.wait_recv()

    hbm = pl.BlockSpec(memory_space=pltpu.HBM)
    outs = pl.pallas_call(
        body, name=name,
        in_specs=[hbm] * (n + m) + [SEM_SPEC, SEM_SPEC, HBM_SPEC],
        out_specs=[hbm] * (n + m),
        out_shape=[pltpu.HBM(a.shape, a.dtype) for a in srcs_thru + lands_thru],
        input_output_aliases={i: i for i in range(n + m)},
        compiler_params=pltpu.CompilerParams(has_side_effects=DATAFLOW),
    )(*srcs_thru, *lands_thru, send_sems, recv_sems, after)
    return list(outs[:n]), list(outs[n:])


def _gather_copies(items):
    def copies(srcs, lands):
        x, y, c, myq = _place()
        out = []
        for t, item in enumerate(items):
            for h in (0, 1):
                src = _gather_src(item, srcs[t], h)
                for j, (cx, cy) in enumerate(_other_chips(x, y)):
                    out.append((src, _gather_dst(item, lands[t], myq, h), _gather_dst(item, lands[t], 2 * cx + cy, h),
                                PER_ITEM * t + 2 * j + h, (cx, cy, c)))
                out.append((src, _gather_dst(item, lands[t], myq, h), _gather_dst(item, lands[t], myq, h),
                            PER_ITEM * t + 6 + h, (x, y, 1 - c)))
        return out
    return copies


def _gather_start(items, shards, after):
    lands = [jax.ShapeDtypeStruct(_full_shape(name), BF16) for name, _ in items]
    return _split_start("gather_layer1_start", _gather_copies(items), PER_ITEM * len(items),
                        [shards[item] for item in items], lands, after)


def _gather_wait(items, state, after):
    return _split_wait("gather_layer1_wait", _gather_copies(items), state, after)[1]


def _small_all_reduce(v):
    r = v.shape[0]

    def body(v_ref, o_ref, buf_ref, send_sems, recv_sems):
        x, y, c, _ = _place()
        me = 4 * x + 2 * y + c
        buf_ref[me] = v_ref[...]
        copies = []
        for k in range(1, 8):
            fx, fy, fc = (k >> 2) & 1, (k >> 1) & 1, k & 1
            to = (x ^ fx, y ^ fy, c ^ fc)
            cp = pltpu.make_async_remote_copy(src_ref=v_ref, dst_ref=buf_ref.at[me], send_sem=send_sems.at[k - 1],
                                              recv_sem=recv_sems.at[k - 1], device_id=to, device_id_type=MESH)
            cp.start()
            copies.append(cp)
        for k in range(1, 8):
            fx, fy, fc = (k >> 2) & 1, (k >> 1) & 1, k & 1
            src_dev = 4 * (x ^ fx) + 2 * (y ^ fy) + (c ^ fc)
            pltpu.make_async_remote_copy(src_ref=v_ref, dst_ref=buf_ref.at[src_dev], send_sem=send_sems.at[k - 1],
                                         recv_sem=recv_sems.at[k - 1], device_id=(x, y, c), device_id_type=MESH).wait_recv()
        for cp in copies:
            cp.wait_send()
        tot = buf_ref[0]
        for i in range(1, 8):
            tot = tot + buf_ref[i]
        o_ref[...] = tot

    vm = pl.BlockSpec(memory_space=pltpu.VMEM)
    return pl.pallas_call(
        body, name="small_all_reduce", in_specs=[vm], out_specs=vm,
        out_shape=jax.ShapeDtypeStruct((r, 128), F32),
        scratch_shapes=[pltpu.VMEM((8, r, 128), F32), pltpu.SemaphoreType.DMA((7,)), pltpu.SemaphoreType.DMA((7,))],
    )(v)


def _grad_view(kind, g):
    if kind == "col":
        return g.reshape(2, g.shape[0] // 2, g.shape[1])
    return g.reshape(N_CHIPS, 2, g.shape[0] // (2 * N_CHIPS), g.shape[1])


def _half_of(kind, ref, h):
    return ref.at[h] if kind == "col" else ref.at[:, h]


def _half_shape(kind, view_shape):
    return view_shape[1:] if kind == "col" else (view_shape[0],) + view_shape[2:]


def _piece_of(kind, width, colblock, ref, q):
    if kind == "col":
        return ref.at[:, pl.ds(colblock(q) * width, width)]
    return ref.at[q]


def _piece_shape(kind, width, half_shape):
    return (half_shape[0], width) if kind == "col" else half_shape[1:]


def _pair_exchange(views, kinds, name):
    n = len(views)

    def body(*refs):
        ins, outs = refs[:n], refs[n:2 * n]
        send_sems, recv_sems = refs[2 * n:]
        x, y, c, _ = _place()
        cps = []
        for t in range(n):
            cp = pltpu.make_async_remote_copy(src_ref=_half_of(kinds[t], ins[t], 1 - c), dst_ref=outs[t],
                                              send_sem=send_sems.at[t], recv_sem=recv_sems.at[t],
                                              device_id=(x, y, 1 - c), device_id_type=MESH)
            cp.start()
            cps.append(cp)
        for cp in cps:
            cp.wait()

    return pl.pallas_call(
        body, name=name, in_specs=[HBM_SPEC] * n, out_specs=[HBM_SPEC] * n,
        out_shape=[jax.ShapeDtypeStruct(_half_shape(k, v.shape), v.dtype) for k, v in zip(kinds, views)],
        scratch_shapes=[pltpu.SemaphoreType.DMA((n,)), pltpu.SemaphoreType.DMA((n,))],
    )(*views)


def _pair_sum(kind, view, recv, c, name):
    hs = recv.shape
    N = hs[-1]
    rows = hs[-2]
    tr = _pick(rows, (512, 352, 128))
    tn = _pick(N, (1408, 1024, 512))

    def body(c_ref, p_ref, r_ref, s_ref):
        s_ref[...] = (p_ref[...] + r_ref[...]).astype(BF16)

    if kind == "col":
        grid = (rows // tr, N // tn)
        mine = pl.BlockSpec((None, tr, tn), lambda i, j, c_ref: (c_ref[0], i, j))
        blk = pl.BlockSpec((tr, tn), lambda i, j, c_ref: (i, j))
        sem = ("parallel", "parallel")
    else:
        grid = (N_CHIPS, rows // tr, N // tn)
        mine = pl.BlockSpec((None, None, tr, tn), lambda q, i, j, c_ref: (q, c_ref[0], i, j))
        blk = pl.BlockSpec((None, tr, tn), lambda q, i, j, c_ref: (q, i, j))
        sem = ("parallel", "parallel", "parallel")
    return pl.pallas_call(
        body, name=name,
        grid_spec=pltpu.PrefetchScalarGridSpec(num_scalar_prefetch=1, grid=grid, in_specs=[mine, blk], out_specs=blk),
        out_shape=jax.ShapeDtypeStruct(hs, BF16),
        compiler_params=_cparams(sem),
    )(c.reshape(1).astype(jnp.int32), view, recv)


def _chip_copies(kinds, widths, colblocks):
    def copies(srcs, lands):
        x, y, c, _ = _place()
        out = []
        for j, (cx, cy) in enumerate(_other_chips(x, y)):
            for t in range(len(kinds)):
                out.append((_piece_of(kinds[t], widths[t], colblocks[t], srcs[t], 2 * cx + cy), lands[t].at[j],
                            lands[t].at[j], 3 * t + j, (cx, cy, c)))
        return out
    return copies


def _chip_land_shapes(sums, kinds, widths):
    return [jax.ShapeDtypeStruct((3,) + _piece_shape(k, w, s.shape), BF16) for k, w, s in zip(kinds, widths, sums)]


def _chip_exchange(sums, kinds, widths, colblocks, name):
    n = len(sums)
    copies = _chip_copies(kinds, widths, colblocks)

    def body(*refs):
        send_sems, recv_sems = refs[2 * n:]
        cps = [pltpu.make_async_remote_copy(src_ref=src, dst_ref=dst, send_sem=send_sems.at[s], recv_sem=recv_sems.at[s],
                                            device_id=peer, device_id_type=MESH)
               for src, dst, _, s, peer in copies(refs[:n], refs[n:2 * n])]
        for cp in cps:
            cp.start()
        for cp in cps:
            cp.wait()

    return pl.pallas_call(
        body, name=name, in_specs=[HBM_SPEC] * n, out_specs=[HBM_SPEC] * n,
        out_shape=_chip_land_shapes(sums, kinds, widths),
        scratch_shapes=[pltpu.SemaphoreType.DMA((3 * n,)), pltpu.SemaphoreType.DMA((3 * n,))],
    )(*sums)


def _chip_sum(kind, s, recv, block_idx, c, shard_shape, layer, into, name):
    rows, N = recv.shape[1:]
    tr = _pick(rows, (512, 352, 128))
    tn = _pick(N, (1408, 1024, 768, 512))
    ni, nj = rows // tr, N // tn

    def body(q_ref, s_ref, r_ref, *rest):
        o_ref = rest[-1]
        o_ref[...] = ((s_ref[...].astype(F32) + r_ref[0].astype(F32)) + r_ref[1].astype(F32)) + r_ref[2].astype(F32)

    if kind == "col":
        own = pl.BlockSpec((tr, tn), lambda i, j, q_ref: (i, q_ref[0] * nj + j))
    else:
        own = pl.BlockSpec((None, tr, tn), lambda i, j, q_ref: (q_ref[0], i, j))
    if len(shard_shape) == 3:
        lead = 0 if layer is None else layer
        out_spec = pl.BlockSpec((None, tr, tn), lambda i, j, q_ref: (lead, q_ref[1] * ni + i, j))
    else:
        out_spec = pl.BlockSpec((tr, tn), lambda i, j, q_ref: (q_ref[1] * ni + i, j))
    in_specs = [own, pl.BlockSpec((3, tr, tn), lambda i, j, q_ref: (0, i, j))]
    args = [jnp.stack([block_idx, c]).astype(jnp.int32), s, recv]
    aliases = {}
    if into is not None:
        in_specs.append(HBM_SPEC)
        args.append(into)
        aliases = {3: 0}
    return pl.pallas_call(
        body, name=name,
        grid_spec=pltpu.PrefetchScalarGridSpec(num_scalar_prefetch=1, grid=(ni, nj), in_specs=in_specs, out_specs=out_spec),
        out_shape=jax.ShapeDtypeStruct(shard_shape, F32), input_output_aliases=aliases,
        compiler_params=_cparams(("parallel", "parallel")),
    )(*args)


def _half_window(ref, h):
    rows = ref.shape[-2] // 2
    if ref.ndim == 3:
        return ref.at[:, pl.ds(h * rows, rows)]
    return ref.at[pl.ds(h * rows, rows)]


def _share_halves(grads):
    n = len(grads)

    def body(*refs):
        outs = refs[n:2 * n]
        send_sems, recv_sems = refs[2 * n:]
        x, y, c, _ = _place()
        cps = []
        for t in range(n):
            cp = pltpu.make_async_remote_copy(src_ref=_half_window(outs[t], c), dst_ref=_half_window(outs[t], c),
                                              send_sem=send_sems.at[t], recv_sem=recv_sems.at[t],
                                              device_id=(x, y, 1 - c), device_id_type=MESH)
            cp.start()
            cps.append(cp)
        for t in range(n):
            cps[t].wait_send()
            pltpu.make_async_remote_copy(src_ref=_half_window(outs[t], c), dst_ref=_half_window(outs[t], 1 - c),
                                         send_sem=send_sems.at[t], recv_sem=recv_sems.at[t],
                                         device_id=(x, y, 1 - c), device_id_type=MESH).wait_recv()

    return pl.pallas_call(
        body, name="grad_share_halves", in_specs=[HBM_SPEC] * n, out_specs=[HBM_SPEC] * n,
        out_shape=[jax.ShapeDtypeStruct(g.shape, F32) for g in grads],
        input_output_aliases={t: t for t in range(n)},
        scratch_shapes=[pltpu.SemaphoreType.DMA((n,)), pltpu.SemaphoreType.DMA((n,))],
    )(*grads)


def _adamw(w, g, m, v, name):
    R, W = w.shape
    tr = _pick(R, (512, 352, 256, 32))

    def body(w_ref, g_ref, m_ref, v_ref, d_ref, nm_ref, nv_ref):
        gv = g_ref[...]
        nm = ADAM_B1 * m_ref[...] + (1.0 - ADAM_B1) * gv
        nv = ADAM_B2 * v_ref[...] + (1.0 - ADAM_B2) * (gv * gv)
        m_hat = nm / (1.0 - ADAM_B1 ** ADAM_STEP)
        v_hat = nv / (1.0 - ADAM_B2 ** ADAM_STEP)
        d_ref[...] = -ADAM_LR * (m_hat / (jnp.sqrt(v_hat) + ADAM_EPS) + ADAM_WD * w_ref[...])
        nm_ref[...] = nm
        nv_ref[...] = nv

    blk = pl.BlockSpec((tr, W), lambda i: (i, 0))
    shp = jax.ShapeDtypeStruct((R, W), F32)
    return pl.pallas_call(
        body, name=name, grid=(R // tr,), in_specs=[blk] * 4, out_specs=[blk] * 3, out_shape=[shp] * 3,
        compiler_params=_cparams(("parallel",)),
    )(w, g, m, v)


SMALL_ROWS = 32


def _pack_small(ln_g, ln_b, sinks):
    rows = jnp.concatenate([ln_g.reshape(-1, 128), ln_b.reshape(-1, 128),
                            jnp.pad(sinks.reshape(1, -1), ((0, 0), (0, 128 - sinks.size)))], axis=0)
    return jnp.pad(rows, ((0, SMALL_ROWS - rows.shape[0]), (0, 0)))


def _unpack_small(s, ln_shape, sink_shape):
    n = ln_shape[0] * ln_shape[1] * ln_shape[2] // 128
    return s[:n].reshape(ln_shape), s[n:2 * n].reshape(ln_shape), s[2 * n, :sink_shape[1]].reshape(sink_shape)


def _ffn_fwd(xin, w_in, w_out, gain, bias, tag):
    u, h = _ffn_in(xin, w_in, "ffn_in_" + tag)
    y, yb, z = _mm_ln(h, w_out, xin, gain, bias, 0.5, "ffn_out_ln_" + tag)
    return y, yb, dict(u=u, h=h, z=z, xin=xin)


def _ffn_bwd(dy, saved, w_in, w_out, gain, xin_b, tag):
    dz, dzc, gg, gb = _ln_bwd(saved["z"], dy, gain, 0.5, "ln_bwd_" + tag)
    du = _ffn_bwd_h(dzc, w_out, saved["u"], "ffn_bwd_h_" + tag)
    d_w_out = _mm_tn(saved["h"], dzc, "ffn_dwout_" + tag)
    d_w_in = _mm_tn(xin_b, du, "ffn_dwin_" + tag)
    dx = _mm_nt(du, w_in, "ffn_dx_" + tag, add=dz, add_scale=ALPHA)
    return dx, d_w_in, d_w_out, gg, gb


def kernel(x, ffn1_w_in, ffn1_w_out, ffn2_w_in, ffn2_w_out, ln_g, ln_b, a_w_qkv, a_w_o, kv_w, b_w_q, b_sinks, b_w_o, loss_target, m_ffn1_w_in, m_ffn1_w_out, m_ffn2_w_in, m_ffn2_w_out, m_ln_g, m_ln_b, m_a_w_qkv, m_a_w_o, m_kv_w, m_b_w_q, m_b_sinks, m_b_w_o, v_ffn1_w_in, v_ffn1_w_out, v_ffn2_w_in, v_ffn2_w_out, v_ln_g, v_ln_b, v_a_w_qkv, v_a_w_o, v_kv_w, v_b_w_q, v_b_sinks, v_b_w_o):
    ws = dict(ffn1_w_in=ffn1_w_in, ffn1_w_out=ffn1_w_out, ffn2_w_in=ffn2_w_in, ffn2_w_out=ffn2_w_out, a_w_qkv=a_w_qkv,
              a_w_o=a_w_o, kv_w=kv_w, b_w_q=b_w_q, b_w_o=b_w_o)
    ms = dict(ffn1_w_in=m_ffn1_w_in, ffn1_w_out=m_ffn1_w_out, ffn2_w_in=m_ffn2_w_in, ffn2_w_out=m_ffn2_w_out,
              a_w_qkv=m_a_w_qkv, a_w_o=m_a_w_o, kv_w=m_kv_w, b_w_q=m_b_w_q, b_w_o=m_b_w_o)
    vs = dict(ffn1_w_in=v_ffn1_w_in, ffn1_w_out=v_ffn1_w_out, ffn2_w_in=v_ffn2_w_in, ffn2_w_out=v_ffn2_w_out,
              a_w_qkv=v_a_w_qkv, a_w_o=v_a_w_o, kv_w=v_kv_w, b_w_q=v_b_w_q, b_w_o=v_b_w_o)
    _, _, c_idx, myq = _place()
    xs = x[0]
    target = loss_target[0]

    shards = {(n, l): (ws[n] if l is None else ws[n][l]).astype(BF16) for n, l in LAYER0_ITEMS + LAYER1_ITEMS}

    def as_weights(items, arrays):
        return {n: (a.reshape(D_MODEL, a.shape[-1]) if a.ndim == 4 else a) for (n, _), a in zip(items, arrays)}

    full0, small = _all_gather(LAYER0_ITEMS, shards, _pack_small(ln_g, ln_b, b_sinks))
    gather_state, token = _gather_start(LAYER1_ITEMS, shards, small)

    def layer1_weights(after):
        return as_weights(LAYER1_ITEMS, _gather_wait(LAYER1_ITEMS, gather_state, after))

    n_ln = ln_g.size // 128
    lg = jnp.concatenate([small[q, :n_ln].reshape(DEPTH, 3, 1, -1) for q in range(N_CHIPS)], axis=-1)
    lb = jnp.concatenate([small[q, n_ln:2 * n_ln].reshape(DEPTH, 3, 1, -1) for q in range(N_CHIPS)], axis=-1)
    lg = lg + token[0, 0]
    reducer = _GradReducer(c_idx, myq, {n: ws[n].shape for n in BIG})
    sq, grad_x, _, gg, gb, dsink_part = _local_step(xs, target, as_weights(LAYER0_ITEMS, full0), layer1_weights,
                                                    lg, lb, b_sinks.reshape(N_HEADS), reducer.begin)

    loss_row = jnp.pad(jnp.sum(sq).reshape(1, 1), ((0, 0), (0, 127)))
    dsinks = jnp.pad(dsink_part[:, 0, :].reshape(N_SLABS, 2, HEAD_DIM)[:, :, 0].reshape(1, N_HEADS), ((0, 0), (0, 128 - N_HEADS)))
    gg_full = jnp.stack([jnp.stack([jnp.sum(gg[i][j], axis=0) for j in range(3)]) for i in range(DEPTH)])
    gb_full = jnp.stack([jnp.stack([jnp.sum(gb[i][j], axis=0) for j in range(3)]) for i in range(DEPTH)])
    small_in = jnp.concatenate([loss_row, dsinks, gg_full.reshape(-1, 128), gb_full.reshape(-1, 128)], axis=0)
    small_in = jnp.pad(small_in, ((0, (-small_in.shape[0]) % 8), (0, 0)))
    small_sum = _small_all_reduce(small_in)
    loss = small_sum[0, 0] * (0.5 / D_MODEL)
    grad_sinks = small_sum[1, :N_HEADS].reshape(b_sinks.shape)
    n_full = DEPTH * 3 * D_MODEL // 128
    cols = D_MODEL // N_CHIPS
    grad_ln_g = lax.dynamic_slice_in_dim(small_sum[2:2 + n_full].reshape(DEPTH, 3, D_MODEL), myq * cols, cols, axis=2)
    grad_ln_b = lax.dynamic_slice_in_dim(small_sum[2 + n_full:2 + 2 * n_full].reshape(DEPTH, 3, D_MODEL), myq * cols, cols, axis=2)
    return _update(reducer, grad_x, loss, grad_ln_g, grad_ln_b, grad_sinks, ws, ms, vs,
                   (ln_g, ln_b, b_sinks), (m_ln_g, m_ln_b, m_b_sinks), (v_ln_g, v_ln_b, v_b_sinks))


def _local_step(xs, target, W, layer1_weights, lg, lb, sinks, grads_ready=None):
    if grads_ready is None:
        grads_ready = lambda tag, grads, overlap: 0.0
    S = xs.shape[0]
    slopes = jnp.asarray(_alibi_slopes(N_HEADS))
    in1, out1, in2, out2 = [W["ffn1_w_in"]], [W["ffn1_w_out"]], [W["ffn2_w_in"]], [W["ffn2_w_out"]]

    y1, y1b, s1 = _ffn_fwd(xs, in1[0], out1[0], lg[0, 0], lb[0, 0], "a1")
    qkv_a = _mm_nn(y1b, W["a_w_qkv"], F32, "qkv_a", split=True)
    mix_a, o_a, lse_a = _attn_fwd(qkv_a, slopes, None, PATTERNS_A, "attn_a_fwd")
    y2, y2b, z2 = _mm_ln(mix_a, W["a_w_o"], y1, lg[0, 1], lb[0, 1], 1.0, "attn_a_out_ln")
    y3, y3b, s3 = _ffn_fwd(y2, in2[0], out2[0], lg[0, 2], lb[0, 2], "a2")
    kv = _mm_nn(y3b, W["kv_w"], F32, "kv_proj")
    W = dict(W, **layer1_weights(kv))
    in1, out1, in2, out2 = (in1 + [W["ffn1_w_in"]], out1 + [W["ffn1_w_out"]], in2 + [W["ffn2_w_in"]],
                            out2 + [W["ffn2_w_out"]])
    y4, y4b, s4 = _ffn_fwd(y3, in1[1], out1[1], lg[1, 0], lb[1, 0], "b1")
    q_b = _mm_nn(y4b, W["b_w_q"], F32, "q_b")
    k_sh = kv[:, :N_KV_B * HEAD_DIM].reshape(S, N_KV_B, 1, HEAD_DIM)
    v_sh = kv[:, N_KV_B * HEAD_DIM:].reshape(S, N_KV_B, 1, HEAD_DIM)
    k_exp = jnp.broadcast_to(k_sh, (S, N_KV_B, GROUP_B, HEAD_DIM)).reshape(S, D_MODEL)
    v_exp = jnp.broadcast_to(v_sh, (S, N_KV_B, GROUP_B, HEAD_DIM)).reshape(S, D_MODEL)
    qkv_b = jnp.stack([q_b, k_exp, v_exp])
    mix_b, o_b, lse_b = _attn_fwd(qkv_b, slopes, sinks, PATTERNS_B, "attn_b_fwd")
    y5, y5b, z5 = _mm_ln(mix_b, W["b_w_o"], y4, lg[1, 1], lb[1, 1], 1.0, "attn_b_out_ln")
    y6, _, s6 = _ffn_fwd(y5, in2[1], out2[1], lg[1, 2], lb[1, 2], "b2")

    dy6, sq = _loss_grad(y6, target, "loss_grad")
    gr = {n: None for n in BIG}
    gg = [[None] * 3 for _ in range(DEPTH)]
    gb = [[None] * 3 for _ in range(DEPTH)]

    dy5, d_in2_b, d_out2_b, gg[1][2], gb[1][2] = _ffn_bwd(dy6, s6, in2[1], out2[1], lg[1, 2], y5b, "b2")
    dz5, dz5b, gg[1][1], gb[1][1] = _ln_bwd(z5, dy5, lg[1, 1], 1.0, "ln_bwd_attn_b")
    gr["b_w_o"] = _mm_tn(mix_b, dz5b, "d_b_w_o")
    dmix_b = _mm_nt(dz5b, W["b_w_o"], "d_mix_b")
    dqkv_b, dsink_part = _attn_bwd(qkv_b, dmix_b, o_b, lse_b, slopes, sinks, PATTERNS_B, "attn_b_bwd")
    dq_b, dk_exp, dv_exp = (dqkv_b, 0), dqkv_b[1], dqkv_b[2]
    dkv = jnp.concatenate([dk_exp.reshape(S, N_KV_B, GROUP_B, HEAD_DIM).sum(axis=2).reshape(S, -1),
                           dv_exp.reshape(S, N_KV_B, GROUP_B, HEAD_DIM).sum(axis=2).reshape(S, -1)], axis=1)
    gr["b_w_q"] = _mm_tn(y4b, dq_b, "d_b_w_q")
    dy4 = _mm_nt(dq_b, W["b_w_q"], "d_y4", add=dz5, add_scale=ALPHA)
    dy3, d_in1_b, d_out1_b, gg[1][0], gb[1][0] = _ffn_bwd(dy4, s4, in1[1], out1[1], lg[1, 0], y3b, "b1")
    gr["kv_w"] = _mm_tn(y3b, dkv, "d_kv_w")
    dy3 = _mm_nt(dkv, W["kv_w"], "d_y3_kv", add=dy3, add_scale=1.0)
    tok = grads_ready("l1", {("ffn2_w_in", 1): d_in2_b, ("ffn2_w_out", 1): d_out2_b, ("b_w_o", None): gr["b_w_o"],
                             ("b_w_q", None): gr["b_w_q"], ("ffn1_w_in", 1): d_in1_b, ("ffn1_w_out", 1): d_out1_b,
                             ("kv_w", None): gr["kv_w"]}, True)
    lg0 = lg[0] + tok

    dy2, d_in2_a, d_out2_a, gg[0][2], gb[0][2] = _ffn_bwd(dy3, s3, in2[0], out2[0], lg0[2], y2b, "a2")
    tok = grads_ready("a2", {("ffn2_w_in", 0): d_in2_a, ("ffn2_w_out", 0): d_out2_a}, True)
    lg0 = lg0 + tok
    dz2, dz2b, gg[0][1], gb[0][1] = _ln_bwd(z2, dy2, lg0[1], 1.0, "ln_bwd_attn_a")
    gr["a_w_o"] = _mm_tn(mix_a, dz2b, "d_a_w_o")
    dmix_a = _mm_nt(dz2b, W["a_w_o"], "d_mix_a")
    dqkv_a, _ = _attn_bwd(qkv_a, dmix_a, o_a, lse_a, slopes, None, PATTERNS_A, "attn_a_bwd")
    gr["a_w_qkv"] = _mm_tn(y1b, dqkv_a, "d_a_w_qkv", split=True)
    dy1 = _mm_nt(dqkv_a, W["a_w_qkv"], "d_y1", add=dz2, add_scale=ALPHA, split=True)
    grad_x, d_in1_a, d_out1_a, gg[0][0], gb[0][0] = _ffn_bwd(dy1, s1, in1[0], out1[0], lg0[0], xs, "a1")
    grads_ready("a1", {("a_w_o", None): gr["a_w_o"], ("a_w_qkv", None): gr["a_w_qkv"], ("ffn1_w_in", 0): d_in1_a,
                       ("ffn1_w_out", 0): d_out1_a}, False)
    gr["ffn1_w_in"] = [d_in1_a, d_in1_b]
    gr["ffn1_w_out"] = [d_out1_a, d_out1_b]
    gr["ffn2_w_in"] = [d_in2_a, d_in2_b]
    gr["ffn2_w_out"] = [d_out2_a, d_out2_b]
    return sq, grad_x, gr, gg, gb, dsink_part


def _grad_item(name, layer, g):
    if name.endswith("w_in"):
        return (g, "col", HALF_FF, _slot, name, layer)
    if name.endswith("w_out"):
        return (g, "row", D_MODEL, None, name, layer)
    if name == "a_w_qkv":
        return (g, "col", QKV_SHARD, lambda q: q, name, None)
    return (g, "row", g.shape[1], None, name, None)


class _GradReducer:
    def __init__(self, c_idx, myq, shard_shapes):
        self.c_idx, self.myq, self.shard_shapes = c_idx, myq, shard_shapes
        self.groups = []

    def begin(self, tag, grads, overlap):
        items = [_grad_item(n, l, g) for (n, l), g in grads.items()]
        kinds, widths, colblocks = [it[1] for it in items], [it[2] for it in items], [it[3] for it in items]
        views = [_grad_view(k, it[0]) for k, it in zip(kinds, items)]
        from_sibling = _pair_exchange(views, kinds, "grad_pair_exchange_" + tag)
        sums = [_pair_sum(k, v, r, self.c_idx, "pair_sum_%s_%d" % (tag, t))
                for t, (k, v, r) in enumerate(zip(kinds, views, from_sibling))]
        if not overlap:
            self.groups.append((tag, items, sums, None))
            return 0.0
        state, token = _split_start("grad_chip_start_" + tag, _chip_copies(kinds, widths, colblocks), 3 * len(items), sums,
                                    _chip_land_shapes(sums, kinds, widths), sums[-1])
        self.groups.append((tag, items, None, state))
        return token[0, 0]

    def finish(self, after):
        half_done = {}
        for tag, items, sums, state in self.groups:
            kinds, widths, colblocks = [it[1] for it in items], [it[2] for it in items], [it[3] for it in items]
            if state is not None:
                sums, from_chips = _split_wait("grad_chip_wait_" + tag, _chip_copies(kinds, widths, colblocks), state, after)
            else:
                from_chips = _chip_exchange(sums, kinds, widths, colblocks, "grad_chip_exchange_" + tag)
            for t, (it, s, r) in enumerate(zip(items, sums, from_chips)):
                _, k, _, cb, name, layer = it
                own = cb(self.myq) if k == "col" else self.myq
                half_done[name] = _chip_sum(k, s, r, own, self.c_idx, self.shard_shapes[name], layer, half_done.get(name),
                                            "chip_sum_%s_%d" % (tag, t))
        return dict(zip(BIG, _share_halves([half_done[name] for name in BIG])))


def _update(reducer, grad_x, loss, grad_ln_g, grad_ln_b, grad_sinks, ws, ms, vs, small_w, small_m, small_v):
    ln_g, ln_b, b_sinks = small_w
    m_ln_g, m_ln_b, m_b_sinks = small_m
    v_ln_g, v_ln_b, v_b_sinks = small_v

    grads = reducer.finish(grad_x)

    deltas, new_m, new_v = {}, {}, {}
    for name in BIG:
        shp = ws[name].shape
        flat = lambda a: a.reshape(-1, shp[-1])
        d, nm, nv = _adamw(flat(ws[name]), flat(grads[name]), flat(ms[name]), flat(vs[name]), "adamw_" + name)
        deltas[name], new_m[name], new_v[name] = d.reshape(shp), nm.reshape(shp), nv.reshape(shp)
    delta_s, nm_s, nv_s = _adamw(_pack_small(ln_g, ln_b, b_sinks), _pack_small(grad_ln_g, grad_ln_b, grad_sinks),
                                 _pack_small(m_ln_g, m_ln_b, m_b_sinks), _pack_small(v_ln_g, v_ln_b, v_b_sinks), "adamw_small")
    for d, blob in ((grads, None), (deltas, delta_s), (new_m, nm_s), (new_v, nv_s)):
        if blob is None:
            d["ln_g"], d["ln_b"], d["b_sinks"] = grad_ln_g, grad_ln_b, grad_sinks
        else:
            d["ln_g"], d["ln_b"], d["b_sinks"] = _unpack_small(blob, ln_g.shape, b_sinks.shape)

    order = ("ffn1_w_in", "ffn1_w_out", "ffn2_w_in", "ffn2_w_out", "ln_g", "ln_b", "a_w_qkv", "a_w_o", "kv_w", "b_w_q",
             "b_sinks", "b_w_o")
    outs = [loss, grad_x[None]]
    for d in (grads, deltas, new_m, new_v):
        outs += [d[n] for n in order]
    return tuple(outs)
```

```python
import numpy as np
import jax
import jax.numpy as jnp
from jax import lax
from jax.experimental import pallas as pl
from jax.experimental.pallas import tpu as pltpu

F32 = jnp.float32
BF16 = jnp.bfloat16

D_MODEL = 1024
D_FF = 2816
HALF_FF = D_FF // 2
HEAD_DIM = 64
N_HEADS = 16
N_KV_B = 4
GROUP_B = N_HEADS // N_KV_B
DEPTH = 2
ALPHA = (2.0 * DEPTH) ** 0.25
LN_EPS = 1e-5
BLOCK = 128
SLAB = 128
N_SLABS = D_MODEL // SLAB
PATTERNS_A = ((1, 128, 1.0), (4, 128, 4.0), (16, 128, 16.0))
PATTERNS_B = ((1, 127, 1.0),)
NEG = -1e30

ADAM_LR = 0.001
ADAM_B1 = 0.9
ADAM_B2 = 0.999
ADAM_EPS = 1e-08
ADAM_WD = 0.01
ADAM_STEP = 10

N_CHIPS = 4
VMEM_LIMIT = 56 * 1024 * 1024
MESH = pl.DeviceIdType.MESH


def _alibi_slopes(n):
    return np.array([2.0 ** (-8.0 * (h + 1) / n) for h in range(n)], dtype=np.float32)


def _cparams(sem=None, vmem=VMEM_LIMIT):
    return pltpu.CompilerParams(dimension_semantics=sem, vmem_limit_bytes=vmem)


_DIMS = {"nn": ((1,), (0,)), "nt": ((1,), (1,)), "tn": ((0,), (0,))}


def _unlead(x):
    if isinstance(x, tuple):
        return x[0], x[1], x[0].shape[1:]
    return x, None, x.shape


def _bspec(block, imap, lead=None):
    if lead is None:
        return pl.BlockSpec(block, imap)
    return pl.BlockSpec((None,) + tuple(block), lambda *g: (lead,) + tuple(imap(*g)))


def _matmul(a, b, mode, out_dtype, tm, tn, tk, name, add=None, add_scale=1.0, split=False):
    out_spec = pl.BlockSpec((tm, tn), lambda i, j, k: (i, j))
    if mode == "nn":
        a, al, (M, K) = _unlead(a)
        b, bl, (K2, N) = _unlead(b)
        a_spec = _bspec((tm, tk), lambda i, j, k: (i, k), al)
        b_spec = _bspec((tk, tn), lambda i, j, k: (k, j), bl)
        out_struct = jax.ShapeDtypeStruct((M, N), out_dtype)
        if split:
            assert tn == D_MODEL
            out_spec = pl.BlockSpec((None, tm, tn), lambda i, j, k: (j, i, 0))
            out_struct = jax.ShapeDtypeStruct((N // tn, M, tn), out_dtype)
    elif mode == "nt":
        b, bl, (N, K2) = _unlead(b)
        if split:
            assert tk == D_MODEL
            M, K = a.shape[1], a.shape[0] * a.shape[2]
            a_spec = pl.BlockSpec((None, tm, tk), lambda i, j, k: (k, i, 0))
        else:
            a, al, (M, K) = _unlead(a)
            a_spec = _bspec((tm, tk), lambda i, j, k: (i, k), al)
        b_spec = _bspec((tn, tk), lambda i, j, k: (j, k), bl)
        out_struct = jax.ShapeDtypeStruct((M, N), out_dtype)
    else:
        a, al, (K, M) = _unlead(a)
        if split:
            assert tn == D_MODEL
            K2, N = b.shape[1], b.shape[0] * b.shape[2]
            b_spec = pl.BlockSpec((None, tk, tn), lambda i, j, k: (j, k, 0))
        else:
            b, bl, (K2, N) = _unlead(b)
            b_spec = _bspec((tk, tn), lambda i, j, k: (k, j), bl)
        a_spec = _bspec((tk, tm), lambda i, j, k: (k, i), al)
        out_struct = jax.ShapeDtypeStruct((M, N), out_dtype)
    assert K == K2 and M % tm == 0 and N % tn == 0 and K % tk == 0, (a.shape, b.shape, mode, tm, tn, tk)
    nk = K // tk
    dims = (_DIMS[mode], ((), ()))
    has_add = add is not None

    narrow = out_dtype != F32
    assert not (narrow and has_add)

    def body(*refs):
        if has_add:
            a_ref, b_ref, add_ref, o_ref = refs
            acc_ref = o_ref
        elif narrow:
            a_ref, b_ref, o_ref, acc_ref = refs
        else:
            a_ref, b_ref, o_ref = refs
            acc_ref = o_ref
        k = pl.program_id(2)
        part = lax.dot_general(a_ref[...].astype(BF16), b_ref[...].astype(BF16), dims, preferred_element_type=F32)
        if has_add:
            @pl.when(k == 0)
            def _():
                acc_ref[...] = part + add_scale * add_ref[...]
        else:
            @pl.when(k == 0)
            def _():
                acc_ref[...] = part

        @pl.when(k > 0)
        def _():
            acc_ref[...] += part

        if narrow:
            @pl.when(k == nk - 1)
            def _():
                o_ref[...] = acc_ref[...].astype(out_dtype)

    in_specs = [a_spec, b_spec]
    args = [a, b]
    if has_add:
        in_specs.append(pl.BlockSpec((tm, tn), lambda i, j, k: (i, j)))
        args.append(add)
    return pl.pallas_call(
        body, name=name, grid=(M // tm, N // tn, nk),
        in_specs=in_specs, out_specs=out_spec, out_shape=out_struct,
        scratch_shapes=[pltpu.VMEM((tm, tn), F32)] if narrow else [],
        compiler_params=_cparams(("parallel", "parallel", "arbitrary")),
    )(*args)


def _pick(n, cands):
    for c in cands:
        if n % c == 0:
            return c
    raise ValueError((n, cands))


def _mm_nn(a, b, out_dtype, name, split=False):
    M, K = _unlead(a)[2]
    N = _unlead(b)[2][1]
    return _matmul(a, b, "nn", out_dtype, _pick(M, (1024, 512, 256)), _pick(N, (1024, 512)), _pick(K, (1024, 512)), name,
                   split=split)


def _mm_nt(a, b, name, add=None, add_scale=1.0, split=False):
    M, K = (a.shape[1], D_MODEL) if split else _unlead(a)[2]
    N = _unlead(b)[2][0]
    return _matmul(a, b, "nt", F32, _pick(M, (1024, 512, 256)), _pick(N, (1024, 512)),
                   _pick(K, (2816, 1024, 512)), name, add=add, add_scale=add_scale, split=split)


def _mm_tn(a, b, name, split=False, out_dtype=F32):
    K, M = _unlead(a)[2]
    N = D_MODEL if split else _unlead(b)[2][1]
    return _matmul(a, b, "tn", out_dtype, _pick(M, (1024, 1408, 512)), _pick(N, (1408, 1024, 512)),
                   _pick(K, (2048, 1024, 512, 256)), name, split=split)


def _ffn_in(x, w, name):
    S = x.shape[0]
    tm = _pick(S, (512, 256))
    w, wl, _ = _unlead(w)

    def body(x_ref, w_ref, t_ref, h_ref):
        acc = jnp.dot(x_ref[...].astype(BF16), w_ref[...], preferred_element_type=F32)
        g = acc[:, :HALF_FF]
        up = acc[:, HALF_FF:]
        sg = jax.nn.sigmoid(g)
        silu = g * sg
        t_ref[:, :HALF_FF] = (up * (sg * (1.0 + g * (1.0 - sg)))).astype(BF16)
        t_ref[:, HALF_FF:] = silu.astype(BF16)
        h_ref[...] = (silu * up).astype(BF16)

    return pl.pallas_call(
        body, name=name, grid=(2, S // tm),
        in_specs=[pl.BlockSpec((tm, D_MODEL), lambda j, i: (i, 0)),
                  _bspec((D_MODEL, D_FF), lambda j, i: (0, j), wl)],
        out_specs=[pl.BlockSpec((tm, D_FF), lambda j, i: (i, j)),
                   pl.BlockSpec((tm, HALF_FF), lambda j, i: (i, j))],
        out_shape=[jax.ShapeDtypeStruct((S, 2 * D_FF), BF16), jax.ShapeDtypeStruct((S, D_FF), BF16)],
        compiler_params=_cparams(("parallel", "parallel")),
    )(x, w)


def _ffn_bwd_h(dzc, w_out, u, name):
    S = dzc.shape[0]
    tm = _pick(S, (512, 256))
    w_out, wl, _ = _unlead(w_out)

    def body(dz_ref, w_ref, t_ref, du_ref):
        dh = lax.dot_general(dz_ref[...], w_ref[...], (((1,), (1,)), ((), ())), preferred_element_type=F32)
        du_ref[:, :HALF_FF] = (dh * t_ref[:, :HALF_FF].astype(F32)).astype(BF16)
        du_ref[:, HALF_FF:] = (dh * t_ref[:, HALF_FF:].astype(F32)).astype(BF16)

    return pl.pallas_call(
        body, name=name, grid=(2, S // tm),
        in_specs=[pl.BlockSpec((tm, D_MODEL), lambda j, i: (i, 0)),
                  _bspec((HALF_FF, D_MODEL), lambda j, i: (j, 0), wl),
                  pl.BlockSpec((tm, D_FF), lambda j, i: (i, j))],
        out_specs=pl.BlockSpec((tm, D_FF), lambda j, i: (i, j)),
        out_shape=jax.ShapeDtypeStruct((S, 2 * D_FF), BF16),
        compiler_params=_cparams(("parallel", "parallel")),
    )(dzc, w_out, u)


def _mm_ln(a, w, resid, gain, bias, c, name):
    S, K = a.shape
    tm = _pick(S, (512, 256))
    w, wl, _ = _unlead(w)

    def body(a_ref, w_ref, r_ref, g_ref, b_ref, y_ref, yb_ref, z_ref):
        z = ALPHA * r_ref[...] + c * jnp.dot(a_ref[...], w_ref[...], preferred_element_type=F32)
        mu = jnp.mean(z, axis=-1, keepdims=True)
        zc = z - mu
        var = jnp.mean(zc * zc, axis=-1, keepdims=True)
        y = zc * lax.rsqrt(var + LN_EPS) * g_ref[...] + b_ref[...]
        z_ref[...] = z
        y_ref[...] = y
        yb_ref[...] = y.astype(BF16)

    row = pl.BlockSpec((tm, D_MODEL), lambda i: (i, 0))
    vec = pl.BlockSpec((1, D_MODEL), lambda i: (0, 0))
    return pl.pallas_call(
        body, name=name, grid=(S // tm,),
        in_specs=[pl.BlockSpec((tm, K), lambda i: (i, 0)), _bspec((K, D_MODEL), lambda i: (0, 0), wl), row, vec, vec],
        out_specs=[row, row, row],
        out_shape=[jax.ShapeDtypeStruct((S, D_MODEL), F32), jax.ShapeDtypeStruct((S, D_MODEL), BF16),
                   jax.ShapeDtypeStruct((S, D_MODEL), F32)],
        compiler_params=_cparams(("parallel",)),
    )(a, w, resid, gain, bias)


def _ln_bwd(z, dy, gain, c, name):
    S = z.shape[0]
    tm = _pick(S, (512, 256))

    def body(z_ref, dy_ref, g_ref, dz_ref, dzc_ref, gg_ref, gb_ref):
        i = pl.program_id(0)
        zv = z_ref[...]
        dyv = dy_ref[...]
        mu = jnp.mean(zv, axis=-1, keepdims=True)
        zc = zv - mu
        var = jnp.mean(zc * zc, axis=-1, keepdims=True)
        rstd = lax.rsqrt(var + LN_EPS)
        xhat = zc * rstd
        dyg = dyv * g_ref[...]
        m1 = jnp.mean(dyg, axis=-1, keepdims=True)
        m2 = jnp.mean(dyg * xhat, axis=-1, keepdims=True)
        dz = rstd * (dyg - m1 - xhat * m2)
        dz_ref[...] = dz
        dzc_ref[...] = (c * dz).astype(BF16)
        pg = jnp.sum((dyv * xhat).reshape(tm // 8, 8, D_MODEL), axis=0)
        pb = jnp.sum(dyv.reshape(tm // 8, 8, D_MODEL), axis=0)

        @pl.when(i == 0)
        def _():
            gg_ref[...] = pg
            gb_ref[...] = pb

        @pl.when(i > 0)
        def _():
            gg_ref[...] += pg
            gb_ref[...] += pb

    row = pl.BlockSpec((tm, D_MODEL), lambda i: (i, 0))
    part = pl.BlockSpec((8, D_MODEL), lambda i: (0, 0))
    return pl.pallas_call(
        body, name=name, grid=(S // tm,),
        in_specs=[row, row, pl.BlockSpec((1, D_MODEL), lambda i: (0, 0))],
        out_specs=[row, row, part, part],
        out_shape=[jax.ShapeDtypeStruct((S, D_MODEL), F32), jax.ShapeDtypeStruct((S, D_MODEL), BF16),
                   jax.ShapeDtypeStruct((8, D_MODEL), F32), jax.ShapeDtypeStruct((8, D_MODEL), F32)],
        compiler_params=_cparams(("arbitrary",)),
    )(z, dy, gain)


def _loss_grad(y, t, name):
    S = y.shape[0]
    tm = _pick(S, (512, 256))

    def body(y_ref, t_ref, dy_ref, sq_ref):
        i = pl.program_id(0)
        e = y_ref[...] - t_ref[...]
        dy_ref[...] = e * (1.0 / D_MODEL)
        ps = jnp.sum((e * e).reshape(tm // 8, 8, D_MODEL), axis=0)

        @pl.when(i == 0)
        def _():
            sq_ref[...] = ps

        @pl.when(i > 0)
        def _():
            sq_ref[...] += ps

    row = pl.BlockSpec((tm, D_MODEL), lambda i: (i, 0))
    return pl.pallas_call(
        body, name=name, grid=(S // tm,),
        in_specs=[row, row], out_specs=[row, pl.BlockSpec((8, D_MODEL), lambda i: (0, 0))],
        out_shape=[jax.ShapeDtypeStruct((S, D_MODEL), F32), jax.ShapeDtypeStruct((8, D_MODEL), F32)],
        compiler_params=_cparams(("arbitrary",)),
    )(y, t)


def _rows(start, d):
    if d == 1:
        return pl.ds(pl.multiple_of(start, BLOCK), BLOCK)
    return pl.ds(start, BLOCK, stride=d)


def _ld(ref, start, d):
    return ref[_rows(start, d), :]


def _ld3(ref, lead, start, d):
    return ref[lead, _rows(start, d), :]


def _st3(ref, lead, start, d, val):
    ref[lead, _rows(start, d), :] = val


def _acc3(ref, lead, start, d, val):
    ref[lead, _rows(start, d), :] = ref[lead, _rows(start, d), :] + val


def _band_consts(slope0, slope1, maxd, scale):
    row = lax.broadcasted_iota(jnp.int32, (2 * BLOCK, 2 * BLOCK), 0)
    kj = lax.broadcasted_iota(jnp.int32, (2 * BLOCK, 2 * BLOCK), 1)
    top = row < BLOCK
    dist = BLOCK + jnp.where(top, row, row - BLOCK) - kj
    slope = jnp.where(top, slope0, slope1)
    base = jnp.where((dist >= 0) & (dist <= maxd), -(slope * (dist.astype(F32) * scale)), NEG)
    return base, kj < BLOCK


def _stack_heads(x, lo):
    return jnp.concatenate([jnp.where(lo, x, 0.0), jnp.where(lo, 0.0, x)], axis=0)


def _unstack_heads(x2, lo):
    return jnp.where(lo, x2[:BLOCK], x2[BLOCK:])


def _scores(q2, k2, base, prev_keys, first):
    s = lax.dot_general(q2, k2, (((1,), (1,)), ((), ())), preferred_element_type=F32) * (HEAD_DIM ** -0.5) + base
    return jnp.where(jnp.logical_and(prev_keys, first), NEG, s)


def _softmax_weights(ls):
    mx = ls[0]
    for l in ls[1:]:
        mx = jnp.maximum(mx, l)
    es = [jnp.exp(l - mx) for l in ls]
    tot = es[0]
    for e in es[1:]:
        tot = tot + e
    inv = 1.0 / tot
    return [e * inv for e in es]


def _attn_fwd(qkv, slopes, sinks, patterns, name):
    S = qkv.shape[1]
    npat = len(patterns)
    has_sink = sinks is not None
    if not has_sink:
        sinks = jnp.zeros((N_HEADS,), F32)
    rows_c = 256

    def body(slopes_ref, sinks_ref, x_ref, mix_ref, o_ref, lse_ref):
        p = pl.program_id(0)
        lo = lax.broadcasted_iota(jnp.int32, (BLOCK, SLAB), 1) < HEAD_DIM
        top1 = lax.broadcasted_iota(jnp.int32, (2 * BLOCK, 1), 0) < BLOCK
        sk2 = jnp.where(top1, sinks_ref[2 * p], sinks_ref[2 * p + 1])
        for pi, (d, maxd, scale) in enumerate(patterns):
            nb = S // d // BLOCK
            base, prev_keys = _band_consts(slopes_ref[2 * p], slopes_ref[2 * p + 1], maxd, scale)

            def blk(t, carry, pi=pi, d=d, nb=nb, base=base, prev_keys=prev_keys):
                r = t // nb
                n = t - r * nb
                start = r + (d * BLOCK) * n
                prev = jnp.where(n > 0, start - d * BLOCK, start)
                q2 = _stack_heads(_ld3(x_ref, 0, start, d), lo).astype(BF16)
                k2 = jnp.concatenate([_ld3(x_ref, 1, prev, d), _ld3(x_ref, 1, start, d)], axis=0).astype(BF16)
                v2 = jnp.concatenate([_ld3(x_ref, 2, prev, d), _ld3(x_ref, 2, start, d)], axis=0).astype(BF16)
                s = _scores(q2, k2, base, prev_keys, n == 0)
                m = jnp.max(s, axis=-1, keepdims=True)
                if has_sink:
                    m = jnp.maximum(m, sk2)
                e = jnp.exp(s - m)
                den = jnp.sum(e, axis=-1, keepdims=True)
                if has_sink:
                    den = den + jnp.exp(sk2 - m)
                o2 = jnp.dot((e / den).astype(BF16), v2, preferred_element_type=F32)
                _st3(o_ref, pi, start, d, _unstack_heads(o2, lo))
                _st3(lse_ref, pi, start, d, _unstack_heads(m + jnp.log(den), lo))
                return carry

            lax.fori_loop(0, d * nb, blk, 0, unroll=8)

        def comb(ci, carry):
            rows = pl.ds(pl.multiple_of(ci * rows_c, rows_c), rows_c)
            if npat == 1:
                mix_ref[rows, :] = o_ref[0, rows, :].astype(BF16)
            else:
                ws = _softmax_weights([lse_ref[i, rows, :] for i in range(npat)])
                acc = ws[0] * o_ref[0, rows, :]
                for i in range(1, npat):
                    acc = acc + ws[i] * o_ref[i, rows, :]
                mix_ref[rows, :] = acc.astype(BF16)
            return carry

        lax.fori_loop(0, S // rows_c, comb, 0)

    smem = pl.BlockSpec(memory_space=pltpu.SMEM)
    slab3 = pl.BlockSpec((npat, S, SLAB), lambda p: (0, 0, p))
    return pl.pallas_call(
        body, name=name, grid=(N_SLABS,),
        in_specs=[smem, smem, pl.BlockSpec((3, S, SLAB), lambda p: (0, 0, p))],
        out_specs=[pl.BlockSpec((S, SLAB), lambda p: (0, p)), slab3, slab3],
        out_shape=[jax.ShapeDtypeStruct((S, D_MODEL), BF16), jax.ShapeDtypeStruct((npat, S, D_MODEL), F32),
                   jax.ShapeDtypeStruct((npat, S, D_MODEL), F32)],
        compiler_params=_cparams(("arbitrary",)),
    )(slopes, sinks, qkv)


def _attn_bwd(qkv, dout, o, lse, slopes, sinks, patterns, name):
    S = qkv.shape[1]
    npat = len(patterns)
    has_sink = sinks is not None
    if not has_sink:
        sinks = jnp.zeros((N_HEADS,), F32)
    rows_c = 256

    def headsum(x, lo):
        s0 = jnp.sum(jnp.where(lo, x, 0.0), axis=-1, keepdims=True)
        s1 = jnp.sum(jnp.where(lo, 0.0, x), axis=-1, keepdims=True)
        return jnp.where(lo, s0, s1)

    def body(slopes_ref, sinks_ref, x_ref, do_ref, o_ref, lse_ref, dx_ref, dsink_ref, dbar_ref, sacc_ref):
        p = pl.program_id(0)
        lo = lax.broadcasted_iota(jnp.int32, (BLOCK, SLAB), 1) < HEAD_DIM
        lo_c = lax.broadcasted_iota(jnp.int32, (rows_c, SLAB), 1) < HEAD_DIM
        top1 = lax.broadcasted_iota(jnp.int32, (2 * BLOCK, 1), 0) < BLOCK
        sk2 = jnp.where(top1, sinks_ref[2 * p], sinks_ref[2 * p + 1])

        def prep(ci, carry):
            rows = pl.ds(pl.multiple_of(ci * rows_c, rows_c), rows_c)
            dov = do_ref[rows, :]
            dx_ref[:, rows, :] = jnp.zeros((3, rows_c, SLAB), F32)
            if npat == 1:
                dbar_ref[rows, :] = headsum(dov * o_ref[0, rows, :], lo_c)
            else:
                ws = _softmax_weights([lse_ref[i, rows, :] for i in range(npat)])
                acc = ws[0] * headsum(dov * o_ref[0, rows, :], lo_c)
                for i in range(1, npat):
                    acc = acc + ws[i] * headsum(dov * o_ref[i, rows, :], lo_c)
                dbar_ref[rows, :] = acc
            return carry

        lax.fori_loop(0, S // rows_c, prep, 0)
        sacc_ref[...] = jnp.zeros((BLOCK, SLAB), F32)

        for pi, (d, maxd, scale) in enumerate(patterns):
            nb = S // d // BLOCK
            base, prev_keys = _band_consts(slopes_ref[2 * p], slopes_ref[2 * p + 1], maxd, scale)

            def blk(t, carry, pi=pi, d=d, nb=nb, base=base, prev_keys=prev_keys):
                r = t // nb
                n = t - r * nb
                start = r + (d * BLOCK) * n
                prev = jnp.where(n > 0, start - d * BLOCK, start)
                q2 = _stack_heads(_ld3(x_ref, 0, start, d), lo).astype(BF16)
                k2 = jnp.concatenate([_ld3(x_ref, 1, prev, d), _ld3(x_ref, 1, start, d)], axis=0).astype(BF16)
                v2 = jnp.concatenate([_ld3(x_ref, 2, prev, d), _ld3(x_ref, 2, start, d)], axis=0).astype(BF16)
                ls = [_ld3(lse_ref, i, start, d) for i in range(npat)]
                w = _softmax_weights(ls)[pi] if npat > 1 else 1.0
                do2 = _stack_heads(w * _ld(do_ref, start, d), lo).astype(BF16)
                dl = w * _ld(dbar_ref, start, d)
                lse2 = jnp.concatenate([ls[pi][:, :1], ls[pi][:, HEAD_DIM:HEAD_DIM + 1]], axis=0)
                dl2 = jnp.concatenate([dl[:, :1], dl[:, HEAD_DIM:HEAD_DIM + 1]], axis=0)
                s = _scores(q2, k2, base, prev_keys, n == 0)
                pr = jnp.exp(s - lse2)
                dp = lax.dot_general(do2, v2, (((1,), (1,)), ((), ())), preferred_element_type=F32)
                ds = (pr * (dp - dl2) * (HEAD_DIM ** -0.5)).astype(BF16)
                dq2 = jnp.dot(ds, k2, preferred_element_type=F32)
                dk2 = lax.dot_general(ds, q2, (((0,), (0,)), ((), ())), preferred_element_type=F32)
                dv2 = lax.dot_general(pr.astype(BF16), do2, (((0,), (0,)), ((), ())), preferred_element_type=F32)
                _acc3(dx_ref, 0, start, d, _unstack_heads(dq2, lo))
                _acc3(dx_ref, 1, prev, d, dk2[:BLOCK])
                _acc3(dx_ref, 1, start, d, dk2[BLOCK:])
                _acc3(dx_ref, 2, prev, d, dv2[:BLOCK])
                _acc3(dx_ref, 2, start, d, dv2[BLOCK:])
                if has_sink:
                    sacc_ref[...] += _unstack_heads(-jnp.exp(sk2 - lse2) * dl2, lo)
                return carry

            lax.fori_loop(0, d * nb, blk, 0, unroll=4)

        dsink_ref[...] = jnp.broadcast_to(jnp.sum(sacc_ref[...], axis=0, keepdims=True), (8, SLAB))

    smem = pl.BlockSpec(memory_space=pltpu.SMEM)
    one = pl.Buffered(1)
    slab3 = pl.BlockSpec((npat, S, SLAB), lambda p: (0, 0, p), pipeline_mode=one)
    return pl.pallas_call(
        body, name=name, grid=(N_SLABS,),
        in_specs=[smem, smem, pl.BlockSpec((3, S, SLAB), lambda p: (0, 0, p), pipeline_mode=one),
                  pl.BlockSpec((S, SLAB), lambda p: (0, p), pipeline_mode=one), slab3, slab3],
        out_specs=[pl.BlockSpec((3, S, SLAB), lambda p: (0, 0, p)), pl.BlockSpec((None, 8, SLAB), lambda p: (p, 0, 0))],
        out_shape=[jax.ShapeDtypeStruct((3, S, D_MODEL), F32), jax.ShapeDtypeStruct((N_SLABS, 8, SLAB), F32)],
        scratch_shapes=[pltpu.VMEM((S, SLAB), F32), pltpu.VMEM((BLOCK, SLAB), F32)],
        compiler_params=_cparams(("arbitrary",)),
    )(slopes, sinks, qkv, dout, o, lse)


def _place():
    x, y, c = lax.axis_index("x"), lax.axis_index("y"), lax.axis_index("c")
    return x, y, c, 2 * x + y


def _other_chips(x, y):
    return [(1 - x, y), (x, 1 - y), (1 - x, 1 - y)]


HBM_SPEC = pl.BlockSpec(memory_space=pl.ANY)


def _slot(q):
    return 2 * (q % 2) + q // 2


BIG = ("ffn1_w_in", "ffn1_w_out", "ffn2_w_in", "ffn2_w_out", "a_w_qkv", "a_w_o", "kv_w", "b_w_q", "b_w_o")
QKV_SHARD = 3 * D_MODEL // N_CHIPS
ROW_SHARD = D_MODEL // N_CHIPS


LAYER0_ITEMS = (("ffn1_w_in", 0), ("ffn1_w_out", 0), ("a_w_qkv", None), ("a_w_o", None), ("ffn2_w_in", 0),
                ("ffn2_w_out", 0), ("kv_w", None))
LAYER1_ITEMS = (("ffn1_w_in", 1), ("ffn1_w_out", 1), ("b_w_q", None), ("b_w_o", None), ("ffn2_w_in", 1),
                ("ffn2_w_out", 1))
OUT_SHARD = D_FF // N_CHIPS


def _full_shape(name):
    if name.endswith("w_in"):
        return (D_MODEL, 2 * D_FF)
    if name.endswith("w_out"):
        return (D_FF, D_MODEL)
    if name == "a_w_qkv":
        return (D_MODEL, 3 * D_MODEL)
    if name == "kv_w":
        return (N_CHIPS, 2, ROW_SHARD // 2, 2 * N_KV_B * HEAD_DIM)
    return (N_CHIPS, 2, ROW_SHARD // 2, D_MODEL)


def _gather_src(item, ref, c):
    name, _ = item
    if name.endswith("w_in"):
        return ref.at[pl.ds(c * (D_MODEL // 2), D_MODEL // 2)]
    if name.endswith("w_out"):
        return ref.at[pl.ds(c * (OUT_SHARD // 2), OUT_SHARD // 2)]
    if name == "a_w_qkv":
        return ref.at[0, pl.ds(c * (D_MODEL // 2), D_MODEL // 2)]
    if name == "kv_w":
        return ref.at[pl.ds(c * (ROW_SHARD // 2), ROW_SHARD // 2)]
    return ref.at[0, pl.ds(c * (ROW_SHARD // 2), ROW_SHARD // 2)]


def _gather_dst(item, ref, q, c):
    name, _ = item
    if name.endswith("w_in"):
        return ref.at[pl.ds(c * (D_MODEL // 2), D_MODEL // 2), pl.ds(_slot(q) * HALF_FF, HALF_FF)]
    if name.endswith("w_out"):
        return ref.at[pl.ds(q * OUT_SHARD + c * (OUT_SHARD // 2), OUT_SHARD // 2)]
    if name == "a_w_qkv":
        return ref.at[pl.ds(c * (D_MODEL // 2), D_MODEL // 2), pl.ds(q * QKV_SHARD, QKV_SHARD)]
    return ref.at[q, c]


def _all_gather(items, shards, small):
    n = len(items)
    r = small.shape[0]
    per = 8

    def body(*refs):
        srcs, small_ref = refs[:n], refs[n]
        dsts, s_ref = refs[n + 1:2 * n + 1], refs[2 * n + 1]
        send_sems, recv_sems = refs[2 * n + 2:]
        x, y, c, myq = _place()
        sibling = (x, y, 1 - c)
        chips = _other_chips(x, y)

        def big(t, k, src, q, h, to):
            return pltpu.make_async_remote_copy(src_ref=src, dst_ref=_gather_dst(items[t], dsts[t], q, h),
                                                send_sem=send_sems.at[per * t + k], recv_sem=recv_sems.at[per * t + k],
                                                device_id=to, device_id_type=MESH)

        def tiny(k, q, to):
            return pltpu.make_async_remote_copy(src_ref=small_ref, dst_ref=s_ref.at[q], send_sem=send_sems.at[per * n + k],
                                                recv_sem=recv_sems.at[per * n + k], device_id=to, device_id_type=MESH)

        first = []
        for j, chip in enumerate(chips):
            first += [big(t, j, _gather_src(items[t], srcs[t], c), myq, c, (*chip, c)) for t in range(n)]
            first.append(tiny(j, myq, (*chip, c)))
        own = [big(t, 6 + h, _gather_src(items[t], srcs[t], h), myq, h, sibling) for t in range(n) for h in (0, 1)]
        own.append(tiny(3, myq, sibling))
        for cp in first + own:
            cp.start()
        passed = []
        for j, (cx, cy) in enumerate(chips):
            q = 2 * cx + cy
            for t in range(n):
                src = _gather_src(items[t], srcs[t], c)
                big(t, j, src, q, c, sibling).wait_recv()
                fwd = big(t, 3 + j, _gather_dst(items[t], dsts[t], q, c), q, c, sibling)
                fwd.start()
                passed.append(fwd)
        for j, (cx, cy) in enumerate(chips):
            q = 2 * cx + cy
            for t in range(n):
                big(t, 3 + j, _gather_src(items[t], srcs[t], c), q, 1 - c, sibling).wait_recv()
            tiny(j, q, sibling).wait_recv()
        for cp in own:
            cp.wait_recv()
        for cp in first + passed + own:
            cp.wait_send()

    outs = pl.pallas_call(
        body, name="all_gather_layer0",
        in_specs=[HBM_SPEC] * (n + 1), out_specs=[HBM_SPEC] * (n + 1),
        out_shape=[jax.ShapeDtypeStruct(_full_shape(name), BF16) for name, _ in items]
        + [jax.ShapeDtypeStruct((N_CHIPS, r, 128), F32)],
        scratch_shapes=[pltpu.SemaphoreType.DMA((per * n + 4,)), pltpu.SemaphoreType.DMA((per * n + 4,))],
    )(*[shards[item] for item in items], small)
    return list(outs[:n]), outs[n]


SEM_SPEC = pl.BlockSpec(memory_space=pltpu.SEMAPHORE)
DATAFLOW = pltpu.SideEffectType.DATAFLOW_SIDE_EFFECTING
PER_ITEM = 8


def _split_start(name, copies, n_sems, sources, land_shapes, after):
    n, m = len(sources), len(land_shapes)

    def body(*refs):
        srcs, lands = refs[:n], refs[n:n + m]
        send_sems, recv_sems = refs[n + m + 1], refs[n + m + 2]
        token = refs[-1]
        for src, dst_there, _, s, peer in copies(srcs, lands):
            pltpu.make_async_remote_copy(src_ref=src, dst_ref=dst_there, send_sem=send_sems.at[s], recv_sem=recv_sems.at[s],
                                         device_id=peer, device_id_type=MESH).start()
        token[...] = jnp.zeros_like(token)

    src_arrays = [pltpu.with_memory_space_constraint(a, pltpu.HBM) for a in sources]
    land_arrays = [pltpu.with_memory_space_constraint(lax.empty(s.shape, s.dtype), pltpu.HBM) for s in land_shapes]
    hbm = pl.BlockSpec(memory_space=pltpu.HBM)
    outs = pl.pallas_call(
        body, name=name,
        in_specs=[hbm] * (n + m) + [HBM_SPEC],
        out_specs=[SEM_SPEC, SEM_SPEC] + [hbm] * (n + m) + [pl.BlockSpec(memory_space=pltpu.VMEM)],
        out_shape=[pltpu.SemaphoreType.DMA((n_sems,)), pltpu.SemaphoreType.DMA((n_sems,))]
        + [pltpu.HBM(a.shape, a.dtype) for a in src_arrays + land_arrays] + [jax.ShapeDtypeStruct((8, 128), F32)],
        input_output_aliases={i: 2 + i for i in range(n + m)},
        compiler_params=pltpu.CompilerParams(has_side_effects=DATAFLOW),
    )(*src_arrays, *land_arrays, after)
    return (outs[0], outs[1], list(outs[2:2 + n]), list(outs[2 + n:2 + n + m])), outs[-1]


def _split_wait(name, copies, state, after):
    send_sems, recv_sems, srcs_thru, lands_thru = state
    n, m = len(srcs_thru), len(lands_thru)

    def body(*refs):
        srcs, lands = refs[:n], refs[n:n + m]
        send_sems, recv_sems = refs[n + m], refs[n + m + 1]
        for src, _, dst_here, s, peer in copies(srcs, lands):
            cp = pltpu.make_async_remote_copy(src_ref=src, dst_ref=dst_here, send_sem=send_sems.at[s], recv_sem=recv_sems.at[s],
                                              device_id=peer, device_id_type=MESH)
            cp.wait_send()
            cp.wait_recv()

    hbm = pl.BlockSpec(memory_space=pltpu.HBM)
    outs = pl.pallas_call(
        body, name=name,
        in_specs=[hbm] * (n + m) + [SEM_SPEC, SEM_SPEC, HBM_SPEC],
        out_specs=[hbm] * (n + m),
        out_shape=[pltpu.HBM(a.shape, a.dtype) for a in srcs_thru + lands_thru],
        input_output_aliases={i: i for i in range(n + m)},
        compiler_params=pltpu.CompilerParams(has_side_effects=DATAFLOW),
    )(*srcs_thru, *lands_thru, send_sems, recv_sems, after)
    return list(outs[:n]), list(outs[n:])


def _gather_copies(items):
    def copies(srcs, lands):
        x, y, c, myq = _place()
        out = []
        for t, item in enumerate(items):
            for h in (0, 1):
                src = _gather_src(item, srcs[t], h)
                for j, (cx, cy) in enumerate(_other_chips(x, y)):
                    out.append((src, _gather_dst(item, lands[t], myq, h), _gather_dst(item, lands[t], 2 * cx + cy, h),
                                PER_ITEM * t + 2 * j + h, (cx, cy, c)))
                out.append((src, _gather_dst(item, lands[t], myq, h), _gather_dst(item, lands[t], myq, h),
                            PER_ITEM * t + 6 + h, (x, y, 1 - c)))
        return out
    return copies


def _gather_start(items, shards, after):
    lands = [jax.ShapeDtypeStruct(_full_shape(name), BF16) for name, _ in items]
    return _split_start("gather_layer1_start", _gather_copies(items), PER_ITEM * len(items),
                        [shards[item] for item in items], lands, after)


def _gather_wait(items, state, after):
    return _split_wait("gather_layer1_wait", _gather_copies(items), state, after)[1]


def _small_all_reduce(v):
    r = v.shape[0]

    def body(v_ref, o_ref, buf_ref, send_sems, recv_sems):
        x, y, c, _ = _place()
        me = 4 * x + 2 * y + c
        buf_ref[me] = v_ref[...]
        copies = []
        for k in range(1, 8):
            fx, fy, fc = (k >> 2) & 1, (k >> 1) & 1, k & 1
            to = (x ^ fx, y ^ fy, c ^ fc)
            cp = pltpu.make_async_remote_copy(src_ref=v_ref, dst_ref=buf_ref.at[me], send_sem=send_sems.at[k - 1],
                                              recv_sem=recv_sems.at[k - 1], device_id=to, device_id_type=MESH)
            cp.start()
            copies.append(cp)
        for k in range(1, 8):
            fx, fy, fc = (k >> 2) & 1, (k >> 1) & 1, k & 1
            src_dev = 4 * (x ^ fx) + 2 * (y ^ fy) + (c ^ fc)
            pltpu.make_async_remote_copy(src_ref=v_ref, dst_ref=buf_ref.at[src_dev], send_sem=send_sems.at[k - 1],
                                         recv_sem=recv_sems.at[k - 1], device_id=(x, y, c), device_id_type=MESH).wait_recv()
        for cp in copies:
            cp.wait_send()
        tot = buf_ref[0]
        for i in range(1, 8):
            tot = tot + buf_ref[i]
        o_ref[...] = tot

    vm = pl.BlockSpec(memory_space=pltpu.VMEM)
    return pl.pallas_call(
        body, name="small_all_reduce", in_specs=[vm], out_specs=vm,
        out_shape=jax.ShapeDtypeStruct((r, 128), F32),
        scratch_shapes=[pltpu.VMEM((8, r, 128), F32), pltpu.SemaphoreType.DMA((7,)), pltpu.SemaphoreType.DMA((7,))],
    )(v)


def _grad_view(kind, g):
    if kind == "col":
        return g.reshape(2, g.shape[0] // 2, g.shape[1])
    return g.reshape(N_CHIPS, 2, g.shape[0] // (2 * N_CHIPS), g.shape[1])


def _half_of(kind, ref, h):
    return ref.at[h] if kind == "col" else ref.at[:, h]


def _half_shape(kind, view_shape):
    return view_shape[1:] if kind == "col" else (view_shape[0],) + view_shape[2:]


def _piece_of(kind, width, colblock, ref, q):
    if kind == "col":
        return ref.at[:, pl.ds(colblock(q) * width, width)]
    return ref.at[q]


def _piece_shape(kind, width, half_shape):
    return (half_shape[0], width) if kind == "col" else half_shape[1:]


def _pair_exchange(views, kinds, name):
    n = len(views)

    def body(*refs):
        ins, outs = refs[:n], refs[n:2 * n]
        send_sems, recv_sems = refs[2 * n:]
        x, y, c, _ = _place()
        cps = []
        for t in range(n):
            cp = pltpu.make_async_remote_copy(src_ref=_half_of(kinds[t], ins[t], 1 - c), dst_ref=outs[t],
                                              send_sem=send_sems.at[t], recv_sem=recv_sems.at[t],
                                              device_id=(x, y, 1 - c), device_id_type=MESH)
            cp.start()
            cps.append(cp)
        for cp in cps:
            cp.wait()

    return pl.pallas_call(
        body, name=name, in_specs=[HBM_SPEC] * n, out_specs=[HBM_SPEC] * n,
        out_shape=[jax.ShapeDtypeStruct(_half_shape(k, v.shape), v.dtype) for k, v in zip(kinds, views)],
        scratch_shapes=[pltpu.SemaphoreType.DMA((n,)), pltpu.SemaphoreType.DMA((n,))],
    )(*views)


def _pair_sum(kind, view, recv, c, name):
    hs = recv.shape
    N = hs[-1]
    rows = hs[-2]
    tr = _pick(rows, (512, 352, 128))
    tn = _pick(N, (1408, 1024, 512))

    def body(c_ref, p_ref, r_ref, s_ref):
        s_ref[...] = (p_ref[...] + r_ref[...]).astype(BF16)

    if kind == "col":
        grid = (rows // tr, N // tn)
        mine = pl.BlockSpec((None, tr, tn), lambda i, j, c_ref: (c_ref[0], i, j))
        blk = pl.BlockSpec((tr, tn), lambda i, j, c_ref: (i, j))
        sem = ("parallel", "parallel")
    else:
        grid = (N_CHIPS, rows // tr, N // tn)
        mine = pl.BlockSpec((None, None, tr, tn), lambda q, i, j, c_ref: (q, c_ref[0], i, j))
        blk = pl.BlockSpec((None, tr, tn), lambda q, i, j, c_ref: (q, i, j))
        sem = ("parallel", "parallel", "parallel")
    return pl.pallas_call(
        body, name=name,
        grid_spec=pltpu.PrefetchScalarGridSpec(num_scalar_prefetch=1, grid=grid, in_specs=[mine, blk], out_specs=blk),
        out_shape=jax.ShapeDtypeStruct(hs, BF16),
        compiler_params=_cparams(sem),
    )(c.reshape(1).astype(jnp.int32), view, recv)


def _chip_copies(kinds, widths, colblocks):
    def copies(srcs, lands):
        x, y, c, _ = _place()
        out = []
        for j, (cx, cy) in enumerate(_other_chips(x, y)):
            for t in range(len(kinds)):
                out.append((_piece_of(kinds[t], widths[t], colblocks[t], srcs[t], 2 * cx + cy), lands[t].at[j],
                            lands[t].at[j], 3 * t + j, (cx, cy, c)))
        return out
    return copies


def _chip_land_shapes(sums, kinds, widths):
    return [jax.ShapeDtypeStruct((3,) + _piece_shape(k, w, s.shape), BF16) for k, w, s in zip(kinds, widths, sums)]


def _chip_exchange(sums, kinds, widths, colblocks, name):
    n = len(sums)
    copies = _chip_copies(kinds, widths, colblocks)

    def body(*refs):
        send_sems, recv_sems = refs[2 * n:]
        cps = [pltpu.make_async_remote_copy(src_ref=src, dst_ref=dst, send_sem=send_sems.at[s], recv_sem=recv_sems.at[s],
                                            device_id=peer, device_id_type=MESH)
               for src, dst, _, s, peer in copies(refs[:n], refs[n:2 * n])]
        for cp in cps:
            cp.start()
        for cp in cps:
            cp.wait()

    return pl.pallas_call(
        body, name=name, in_specs=[HBM_SPEC] * n, out_specs=[HBM_SPEC] * n,
        out_shape=_chip_land_shapes(sums, kinds, widths),
        scratch_shapes=[pltpu.SemaphoreType.DMA((3 * n,)), pltpu.SemaphoreType.DMA((3 * n,))],
    )(*sums)


N_DIRECT = 7


def _direct_piece(kind, width, colblock, view_ref, q, h):
    if kind == "col":
        return view_ref.at[h, :, pl.ds(colblock(q) * width, width)]
    return view_ref.at[q, h]


def _direct_copies(kinds, widths, colblocks):
    def copies(srcs, lands):
        x, y, c, myq = _place()
        out = []
        for t in range(len(kinds)):
            def piece(q, h, t=t):
                return _direct_piece(kinds[t], widths[t], colblocks[t], srcs[t], q, h)
            for j, (cx, cy) in enumerate(_other_chips(x, y)):
                for h in (0, 1):
                    out.append((piece(2 * cx + cy, h), lands[t].at[2 * j + c], lands[t].at[2 * j + h],
                                10 * t + 3 * j + c + h, (cx, cy, h)))
            out.append((piece(myq, 1 - c), lands[t].at[6], lands[t].at[6], 10 * t + 9, (x, y, 1 - c)))
        return out
    return copies


def _chip_sum(kind, own_src, recv, block_idx, c, shard_shape, layer, into, name, direct=False):
    n_recv, rows, N = recv.shape
    tr = _pick(rows, (512, 352, 128))
    tn = _pick(N, (1408, 1024, 768, 512))
    ni, nj = rows // tr, N // tn

    def body(q_ref, s_ref, r_ref, *rest):
        o_ref = rest[-1]
        tot = s_ref[...].astype(F32)
        for k in range(n_recv):
            tot = tot + r_ref[k].astype(F32)
        o_ref[...] = tot

    if direct and kind == "col":
        own = pl.BlockSpec((None, tr, tn), lambda i, j, q_ref: (q_ref[1], i, q_ref[0] * nj + j))
    elif direct:
        own = pl.BlockSpec((None, None, tr, tn), lambda i, j, q_ref: (q_ref[0], q_ref[1], i, j))
    elif kind == "col":
        own = pl.BlockSpec((tr, tn), lambda i, j, q_ref: (i, q_ref[0] * nj + j))
    else:
        own = pl.BlockSpec((None, tr, tn), lambda i, j, q_ref: (q_ref[0], i, j))
    if len(shard_shape) == 3:
        lead = 0 if layer is None else layer
        out_spec = pl.BlockSpec((None, tr, tn), lambda i, j, q_ref: (lead, q_ref[1] * ni + i, j))
    else:
        out_spec = pl.BlockSpec((tr, tn), lambda i, j, q_ref: (q_ref[1] * ni + i, j))
    in_specs = [own, pl.BlockSpec((n_recv, tr, tn), lambda i, j, q_ref: (0, i, j))]
    s = own_src
    args = [jnp.stack([block_idx, c]).astype(jnp.int32), s, recv]
    aliases = {}
    if into is not None:
        in_specs.append(HBM_SPEC)
        args.append(into)
        aliases = {3: 0}
    return pl.pallas_call(
        body, name=name,
        grid_spec=pltpu.PrefetchScalarGridSpec(num_scalar_prefetch=1, grid=(ni, nj), in_specs=in_specs, out_specs=out_spec),
        out_shape=jax.ShapeDtypeStruct(shard_shape, F32), input_output_aliases=aliases,
        compiler_params=_cparams(("parallel", "parallel")),
    )(*args)


def _half_window(ref, h):
    rows = ref.shape[-2] // 2
    if ref.ndim == 3:
        return ref.at[:, pl.ds(h * rows, rows)]
    return ref.at[pl.ds(h * rows, rows)]


def _share_halves(grads):
    n = len(grads)

    def body(*refs):
        outs = refs[n:2 * n]
        send_sems, recv_sems = refs[2 * n:]
        x, y, c, _ = _place()
        cps = []
        for t in range(n):
            cp = pltpu.make_async_remote_copy(src_ref=_half_window(outs[t], c), dst_ref=_half_window(outs[t], c),
                                              send_sem=send_sems.at[t], recv_sem=recv_sems.at[t],
                                              device_id=(x, y, 1 - c), device_id_type=MESH)
            cp.start()
            cps.append(cp)
        for t in range(n):
            cps[t].wait_send()
            pltpu.make_async_remote_copy(src_ref=_half_window(outs[t], c), dst_ref=_half_window(outs[t], 1 - c),
                                         send_sem=send_sems.at[t], recv_sem=recv_sems.at[t],
                                         device_id=(x, y, 1 - c), device_id_type=MESH).wait_recv()

    return pl.pallas_call(
        body, name="grad_share_halves", in_specs=[HBM_SPEC] * n, out_specs=[HBM_SPEC] * n,
        out_shape=[jax.ShapeDtypeStruct(g.shape, F32) for g in grads],
        input_output_aliases={t: t for t in range(n)},
        scratch_shapes=[pltpu.SemaphoreType.DMA((n,)), pltpu.SemaphoreType.DMA((n,))],
    )(*grads)


def _adamw(w, g, m, v, name):
    R, W = w.shape
    tr = _pick(R, (512, 352, 256, 32))

    def body(w_ref, g_ref, m_ref, v_ref, d_ref, nm_ref, nv_ref):
        gv = g_ref[...]
        nm = ADAM_B1 * m_ref[...] + (1.0 - ADAM_B1) * gv
        nv = ADAM_B2 * v_ref[...] + (1.0 - ADAM_B2) * (gv * gv)
        m_hat = nm / (1.0 - ADAM_B1 ** ADAM_STEP)
        v_hat = nv / (1.0 - ADAM_B2 ** ADAM_STEP)
        d_ref[...] = -ADAM_LR * (m_hat / (jnp.sqrt(v_hat) + ADAM_EPS) + ADAM_WD * w_ref[...])
        nm_ref[...] = nm
        nv_ref[...] = nv

    blk = pl.BlockSpec((tr, W), lambda i: (i, 0))
    shp = jax.ShapeDtypeStruct((R, W), F32)
    return pl.pallas_call(
        body, name=name, grid=(R // tr,), in_specs=[blk] * 4, out_specs=[blk] * 3, out_shape=[shp] * 3,
        compiler_params=_cparams(("parallel",)),
    )(w, g, m, v)


SMALL_ROWS = 32


def _pack_small(ln_g, ln_b, sinks):
    rows = jnp.concatenate([ln_g.reshape(-1, 128), ln_b.reshape(-1, 128),
                            jnp.pad(sinks.reshape(1, -1), ((0, 0), (0, 128 - sinks.size)))], axis=0)
    return jnp.pad(rows, ((0, SMALL_ROWS - rows.shape[0]), (0, 0)))


def _unpack_small(s, ln_shape, sink_shape):
    n = ln_shape[0] * ln_shape[1] * ln_shape[2] // 128
    return s[:n].reshape(ln_shape), s[n:2 * n].reshape(ln_shape), s[2 * n, :sink_shape[1]].reshape(sink_shape)


def _ffn_fwd(xin, w_in, w_out, gain, bias, tag):
    u, h = _ffn_in(xin, w_in, "ffn_in_" + tag)
    y, yb, z = _mm_ln(h, w_out, xin, gain, bias, 0.5, "ffn_out_ln_" + tag)
    return y, yb, dict(u=u, h=h, z=z, xin=xin)


def _ffn_bwd(dy, saved, w_in, w_out, gain, xin_b, tag, dw_dtype=F32):
    dz, dzc, gg, gb = _ln_bwd(saved["z"], dy, gain, 0.5, "ln_bwd_" + tag)
    du = _ffn_bwd_h(dzc, w_out, saved["u"], "ffn_bwd_h_" + tag)
    d_w_out = _mm_tn(saved["h"], dzc, "ffn_dwout_" + tag, out_dtype=dw_dtype)
    d_w_in = _mm_tn(xin_b, du, "ffn_dwin_" + tag, out_dtype=dw_dtype)
    dx = _mm_nt(du, w_in, "ffn_dx_" + tag, add=dz, add_scale=ALPHA)
    return dx, d_w_in, d_w_out, gg, gb


def kernel(x, ffn1_w_in, ffn1_w_out, ffn2_w_in, ffn2_w_out, ln_g, ln_b, a_w_qkv, a_w_o, kv_w, b_w_q, b_sinks, b_w_o, loss_target, m_ffn1_w_in, m_ffn1_w_out, m_ffn2_w_in, m_ffn2_w_out, m_ln_g, m_ln_b, m_a_w_qkv, m_a_w_o, m_kv_w, m_b_w_q, m_b_sinks, m_b_w_o, v_ffn1_w_in, v_ffn1_w_out, v_ffn2_w_in, v_ffn2_w_out, v_ln_g, v_ln_b, v_a_w_qkv, v_a_w_o, v_kv_w, v_b_w_q, v_b_sinks, v_b_w_o):
    ws = dict(ffn1_w_in=ffn1_w_in, ffn1_w_out=ffn1_w_out, ffn2_w_in=ffn2_w_in, ffn2_w_out=ffn2_w_out, a_w_qkv=a_w_qkv,
              a_w_o=a_w_o, kv_w=kv_w, b_w_q=b_w_q, b_w_o=b_w_o)
    ms = dict(ffn1_w_in=m_ffn1_w_in, ffn1_w_out=m_ffn1_w_out, ffn2_w_in=m_ffn2_w_in, ffn2_w_out=m_ffn2_w_out,
              a_w_qkv=m_a_w_qkv, a_w_o=m_a_w_o, kv_w=m_kv_w, b_w_q=m_b_w_q, b_w_o=m_b_w_o)
    vs = dict(ffn1_w_in=v_ffn1_w_in, ffn1_w_out=v_ffn1_w_out, ffn2_w_in=v_ffn2_w_in, ffn2_w_out=v_ffn2_w_out,
              a_w_qkv=v_a_w_qkv, a_w_o=v_a_w_o, kv_w=v_kv_w, b_w_q=v_b_w_q, b_w_o=v_b_w_o)
    _, _, c_idx, myq = _place()
    xs = x[0]
    target = loss_target[0]

    shards = {(n, l): (ws[n] if l is None else ws[n][l]).astype(BF16) for n, l in LAYER0_ITEMS + LAYER1_ITEMS}

    def as_weights(items, arrays):
        return {n: (a.reshape(D_MODEL, a.shape[-1]) if a.ndim == 4 else a) for (n, _), a in zip(items, arrays)}

    full0, small = _all_gather(LAYER0_ITEMS, shards, _pack_small(ln_g, ln_b, b_sinks))
    gather_state, token = _gather_start(LAYER1_ITEMS, shards, small)

    def layer1_weights(after):
        return as_weights(LAYER1_ITEMS, _gather_wait(LAYER1_ITEMS, gather_state, after))

    n_ln = ln_g.size // 128
    lg = jnp.concatenate([small[q, :n_ln].reshape(DEPTH, 3, 1, -1) for q in range(N_CHIPS)], axis=-1)
    lb = jnp.concatenate([small[q, n_ln:2 * n_ln].reshape(DEPTH, 3, 1, -1) for q in range(N_CHIPS)], axis=-1)
    lg = lg + token[0, 0]
    reducer = _GradReducer(c_idx, myq, {n: ws[n].shape for n in BIG})
    sq, grad_x, _, gg, gb, dsink_part = _local_step(xs, target, as_weights(LAYER0_ITEMS, full0), layer1_weights,
                                                    lg, lb, b_sinks.reshape(N_HEADS), reducer.begin)

    loss_row = jnp.pad(jnp.sum(sq).reshape(1, 1), ((0, 0), (0, 127)))
    dsinks = jnp.pad(dsink_part[:, 0, :].reshape(N_SLABS, 2, HEAD_DIM)[:, :, 0].reshape(1, N_HEADS), ((0, 0), (0, 128 - N_HEADS)))
    gg_full = jnp.stack([jnp.stack([jnp.sum(gg[i][j], axis=0) for j in range(3)]) for i in range(DEPTH)])
    gb_full = jnp.stack([jnp.stack([jnp.sum(gb[i][j], axis=0) for j in range(3)]) for i in range(DEPTH)])
    small_in = jnp.concatenate([loss_row, dsinks, gg_full.reshape(-1, 128), gb_full.reshape(-1, 128)], axis=0)
    small_in = jnp.pad(small_in, ((0, (-small_in.shape[0]) % 8), (0, 0)))
    small_sum = _small_all_reduce(small_in)
    loss = small_sum[0, 0] * (0.5 / D_MODEL)
    grad_sinks = small_sum[1, :N_HEADS].reshape(b_sinks.shape)
    n_full = DEPTH * 3 * D_MODEL // 128
    cols = D_MODEL // N_CHIPS
    grad_ln_g = lax.dynamic_slice_in_dim(small_sum[2:2 + n_full].reshape(DEPTH, 3, D_MODEL), myq * cols, cols, axis=2)
    grad_ln_b = lax.dynamic_slice_in_dim(small_sum[2 + n_full:2 + 2 * n_full].reshape(DEPTH, 3, D_MODEL), myq * cols, cols, axis=2)
    return _update(reducer, grad_x, loss, grad_ln_g, grad_ln_b, grad_sinks, ws, ms, vs,
                   (ln_g, ln_b, b_sinks), (m_ln_g, m_ln_b, m_b_sinks), (v_ln_g, v_ln_b, v_b_sinks))


def _local_step(xs, target, W, layer1_weights, lg, lb, sinks, grads_ready=None):
    if grads_ready is None:
        grads_ready = lambda tag, grads, overlap: 0.0
    S = xs.shape[0]
    slopes = jnp.asarray(_alibi_slopes(N_HEADS))
    in1, out1, in2, out2 = [W["ffn1_w_in"]], [W["ffn1_w_out"]], [W["ffn2_w_in"]], [W["ffn2_w_out"]]

    y1, y1b, s1 = _ffn_fwd(xs, in1[0], out1[0], lg[0, 0], lb[0, 0], "a1")
    qkv_a = _mm_nn(y1b, W["a_w_qkv"], F32, "qkv_a", split=True)
    mix_a, o_a, lse_a = _attn_fwd(qkv_a, slopes, None, PATTERNS_A, "attn_a_fwd")
    y2, y2b, z2 = _mm_ln(mix_a, W["a_w_o"], y1, lg[0, 1], lb[0, 1], 1.0, "attn_a_out_ln")
    y3, y3b, s3 = _ffn_fwd(y2, in2[0], out2[0], lg[0, 2], lb[0, 2], "a2")
    kv = _mm_nn(y3b, W["kv_w"], F32, "kv_proj")
    W = dict(W, **layer1_weights(kv))
    in1, out1, in2, out2 = (in1 + [W["ffn1_w_in"]], out1 + [W["ffn1_w_out"]], in2 + [W["ffn2_w_in"]],
                            out2 + [W["ffn2_w_out"]])
    y4, y4b, s4 = _ffn_fwd(y3, in1[1], out1[1], lg[1, 0], lb[1, 0], "b1")
    q_b = _mm_nn(y4b, W["b_w_q"], F32, "q_b")
    k_sh = kv[:, :N_KV_B * HEAD_DIM].reshape(S, N_KV_B, 1, HEAD_DIM)
    v_sh = kv[:, N_KV_B * HEAD_DIM:].reshape(S, N_KV_B, 1, HEAD_DIM)
    k_exp = jnp.broadcast_to(k_sh, (S, N_KV_B, GROUP_B, HEAD_DIM)).reshape(S, D_MODEL)
    v_exp = jnp.broadcast_to(v_sh, (S, N_KV_B, GROUP_B, HEAD_DIM)).reshape(S, D_MODEL)
    qkv_b = jnp.stack([q_b, k_exp, v_exp])
    mix_b, o_b, lse_b = _attn_fwd(qkv_b, slopes, sinks, PATTERNS_B, "attn_b_fwd")
    y5, y5b, z5 = _mm_ln(mix_b, W["b_w_o"], y4, lg[1, 1], lb[1, 1], 1.0, "attn_b_out_ln")
    y6, _, s6 = _ffn_fwd(y5, in2[1], out2[1], lg[1, 2], lb[1, 2], "b2")

    dy6, sq = _loss_grad(y6, target, "loss_grad")
    gr = {n: None for n in BIG}
    gg = [[None] * 3 for _ in range(DEPTH)]
    gb = [[None] * 3 for _ in range(DEPTH)]

    dy5, d_in2_b, d_out2_b, gg[1][2], gb[1][2] = _ffn_bwd(dy6, s6, in2[1], out2[1], lg[1, 2], y5b, "b2", BF16)
    dz5, dz5b, gg[1][1], gb[1][1] = _ln_bwd(z5, dy5, lg[1, 1], 1.0, "ln_bwd_attn_b")
    gr["b_w_o"] = _mm_tn(mix_b, dz5b, "d_b_w_o", out_dtype=BF16)
    dmix_b = _mm_nt(dz5b, W["b_w_o"], "d_mix_b")
    dqkv_b, dsink_part = _attn_bwd(qkv_b, dmix_b, o_b, lse_b, slopes, sinks, PATTERNS_B, "attn_b_bwd")
    dq_b, dk_exp, dv_exp = (dqkv_b, 0), dqkv_b[1], dqkv_b[2]
    dkv = jnp.concatenate([dk_exp.reshape(S, N_KV_B, GROUP_B, HEAD_DIM).sum(axis=2).reshape(S, -1),
                           dv_exp.reshape(S, N_KV_B, GROUP_B, HEAD_DIM).sum(axis=2).reshape(S, -1)], axis=1)
    gr["b_w_q"] = _mm_tn(y4b, dq_b, "d_b_w_q", out_dtype=BF16)
    dy4 = _mm_nt(dq_b, W["b_w_q"], "d_y4", add=dz5, add_scale=ALPHA)
    dy3, d_in1_b, d_out1_b, gg[1][0], gb[1][0] = _ffn_bwd(dy4, s4, in1[1], out1[1], lg[1, 0], y3b, "b1", BF16)
    gr["kv_w"] = _mm_tn(y3b, dkv, "d_kv_w", out_dtype=BF16)
    dy3 = _mm_nt(dkv, W["kv_w"], "d_y3_kv", add=dy3, add_scale=1.0)
    tok = grads_ready("l1", {("ffn2_w_in", 1): d_in2_b, ("ffn2_w_out", 1): d_out2_b, ("b_w_o", None): gr["b_w_o"],
                             ("b_w_q", None): gr["b_w_q"], ("ffn1_w_in", 1): d_in1_b, ("ffn1_w_out", 1): d_out1_b,
                             ("kv_w", None): gr["kv_w"]}, True)
    lg0 = lg[0] + tok

    dy2, d_in2_a, d_out2_a, gg[0][2], gb[0][2] = _ffn_bwd(dy3, s3, in2[0], out2[0], lg0[2], y2b, "a2", BF16)
    tok = grads_ready("a2", {("ffn2_w_in", 0): d_in2_a, ("ffn2_w_out", 0): d_out2_a}, True)
    lg0 = lg0 + tok
    dz2, dz2b, gg[0][1], gb[0][1] = _ln_bwd(z2, dy2, lg0[1], 1.0, "ln_bwd_attn_a")
    gr["a_w_o"] = _mm_tn(mix_a, dz2b, "d_a_w_o", out_dtype=BF16)
    dmix_a = _mm_nt(dz2b, W["a_w_o"], "d_mix_a")
    dqkv_a, _ = _attn_bwd(qkv_a, dmix_a, o_a, lse_a, slopes, None, PATTERNS_A, "attn_a_bwd")
    gr["a_w_qkv"] = _mm_tn(y1b, dqkv_a, "d_a_w_qkv", split=True, out_dtype=BF16)
    tok = grads_ready("mix", {("a_w_o", None): gr["a_w_o"], ("a_w_qkv", None): gr["a_w_qkv"]}, True)
    lg0 = lg0 + tok
    dy1 = _mm_nt(dqkv_a, W["a_w_qkv"], "d_y1", add=dz2, add_scale=ALPHA, split=True)
    grad_x, d_in1_a, d_out1_a, gg[0][0], gb[0][0] = _ffn_bwd(dy1, s1, in1[0], out1[0], lg0[0], xs, "a1")
    grads_ready("a1", {("ffn1_w_in", 0): d_in1_a, ("ffn1_w_out", 0): d_out1_a}, False)
    gr["ffn1_w_in"] = [d_in1_a, d_in1_b]
    gr["ffn1_w_out"] = [d_out1_a, d_out1_b]
    gr["ffn2_w_in"] = [d_in2_a, d_in2_b]
    gr["ffn2_w_out"] = [d_out2_a, d_out2_b]
    return sq, grad_x, gr, gg, gb, dsink_part


def _grad_item(name, layer, g):
    if name.endswith("w_in"):
        return (g, "col", HALF_FF, _slot, name, layer)
    if name.endswith("w_out"):
        return (g, "row", D_MODEL, None, name, layer)
    if name == "a_w_qkv":
        return (g, "col", QKV_SHARD, lambda q: q, name, None)
    return (g, "row", g.shape[1], None, name, None)


class _GradReducer:
    def __init__(self, c_idx, myq, shard_shapes):
        self.c_idx, self.myq, self.shard_shapes = c_idx, myq, shard_shapes
        self.groups = []

    def begin(self, tag, grads, overlap):
        items = [_grad_item(n, l, g) for (n, l), g in grads.items()]
        kinds, widths, colblocks = [it[1] for it in items], [it[2] for it in items], [it[3] for it in items]
        views = [_grad_view(k, it[0]) for k, it in zip(kinds, items)]
        if overlap:
            lands = [jax.ShapeDtypeStruct((N_DIRECT,) + _piece_shape(k, w, _half_shape(k, v.shape)), BF16)
                     for k, w, v in zip(kinds, widths, views)]
            state, token = _split_start("grad_direct_start_" + tag, _direct_copies(kinds, widths, colblocks), 10 * len(items),
                                        views, lands, views[-1])
            self.groups.append((tag, items, None, state))
            return token[0, 0]
        from_sibling = _pair_exchange(views, kinds, "grad_pair_exchange_" + tag)
        sums = [_pair_sum(k, v, r, self.c_idx, "pair_sum_%s_%d" % (tag, t))
                for t, (k, v, r) in enumerate(zip(kinds, views, from_sibling))]
        self.groups.append((tag, items, sums, None))
        return 0.0

    def finish(self, after):
        half_done = {}
        for tag, items, sums, state in self.groups:
            kinds, widths, colblocks = [it[1] for it in items], [it[2] for it in items], [it[3] for it in items]
            direct = state is not None
            if direct:
                sums, received = _split_wait("grad_direct_wait_" + tag, _direct_copies(kinds, widths, colblocks), state, after)
            else:
                received = _chip_exchange(sums, kinds, widths, colblocks, "grad_chip_exchange_" + tag)
            for t, (it, s, r) in enumerate(zip(items, sums, received)):
                _, k, _, cb, name, layer = it
                own = cb(self.myq) if k == "col" else self.myq
                half_done[name] = _chip_sum(k, s, r, own, self.c_idx, self.shard_shapes[name], layer, half_done.get(name),
                                            "chip_sum_%s_%d" % (tag, t), direct=direct)
        return dict(zip(BIG, _share_halves([half_done[name] for name in BIG])))


def _update(reducer, grad_x, loss, grad_ln_g, grad_ln_b, grad_sinks, ws, ms, vs, small_w, small_m, small_v):
    ln_g, ln_b, b_sinks = small_w
    m_ln_g, m_ln_b, m_b_sinks = small_m
    v_ln_g, v_ln_b, v_b_sinks = small_v

    grads = reducer.finish(grad_x)

    deltas, new_m, new_v = {}, {}, {}
    for name in BIG:
        shp = ws[name].shape
        flat = lambda a: a.reshape(-1, shp[-1])
        d, nm, nv = _adamw(flat(ws[name]), flat(grads[name]), flat(ms[name]), flat(vs[name]), "adamw_" + name)
        deltas[name], new_m[name], new_v[name] = d.reshape(shp), nm.reshape(shp), nv.reshape(shp)
    delta_s, nm_s, nv_s = _adamw(_pack_small(ln_g, ln_b, b_sinks), _pack_small(grad_ln_g, grad_ln_b, grad_sinks),
                                 _pack_small(m_ln_g, m_ln_b, m_b_sinks), _pack_small(v_ln_g, v_ln_b, v_b_sinks), "adamw_small")
    for d, blob in ((grads, None), (deltas, delta_s), (new_m, nm_s), (new_v, nv_s)):
        if blob is None:
            d["ln_g"], d["ln_b"], d["b_sinks"] = grad_ln_g, grad_ln_b, grad_sinks
        else:
            d["ln_g"], d["ln_b"], d["b_sinks"] = _unpack_small(blob, ln_g.shape, b_sinks.shape)

    order = ("ffn1_w_in", "ffn1_w_out", "ffn2_w_in", "ffn2_w_out", "ln_g", "ln_b", "a_w_qkv", "a_w_o", "kv_w", "b_w_q",
             "b_sinks", "b_w_o")
    outs = [loss, grad_x[None]]
    for d in (grads, deltas, new_m, new_v):
        outs += [d[n] for n in order]
    return tuple(outs)
```

```python
import numpy as np
import jax
import jax.numpy as jnp
from jax import lax
from jax.experimental import pallas as pl
from jax.experimental.pallas import tpu as pltpu

F32 = jnp.float32
BF16 = jnp.bfloat16

D_MODEL = 1024
D_FF = 2816
HALF_FF = D_FF // 2
HEAD_DIM = 64
N_HEADS = 16
N_KV_B = 4
GROUP_B = N_HEADS // N_KV_B
DEPTH = 2
ALPHA = (2.0 * DEPTH) ** 0.25
LN_EPS = 1e-5
BLOCK = 128
SLAB = 128
N_SLABS = D_MODEL // SLAB
PATTERNS_A = ((1, 128, 1.0), (4, 128, 4.0), (16, 128, 16.0))
PATTERNS_B = ((1, 127, 1.0),)
NEG = -1e30

ADAM_LR = 0.001
ADAM_B1 = 0.9
ADAM_B2 = 0.999
ADAM_EPS = 1e-08
ADAM_WD = 0.01
ADAM_STEP = 10

N_CHIPS = 4
VMEM_LIMIT = 56 * 1024 * 1024
MESH = pl.DeviceIdType.MESH


def _alibi_slopes(n):
    return np.array([2.0 ** (-8.0 * (h + 1) / n) for h in range(n)], dtype=np.float32)


def _cparams(sem=None, vmem=VMEM_LIMIT):
    return pltpu.CompilerParams(dimension_semantics=sem, vmem_limit_bytes=vmem)


_DIMS = {"nn": ((1,), (0,)), "nt": ((1,), (1,)), "tn": ((0,), (0,))}


def _unlead(x):
    if isinstance(x, tuple):
        return x[0], x[1], x[0].shape[1:]
    return x, None, x.shape


def _bspec(block, imap, lead=None):
    if lead is None:
        return pl.BlockSpec(block, imap)
    return pl.BlockSpec((None,) + tuple(block), lambda *g: (lead,) + tuple(imap(*g)))


def _matmul(a, b, mode, out_dtype, tm, tn, tk, name, add=None, add_scale=1.0, split=False, into=None):
    out_spec = pl.BlockSpec((tm, tn), lambda i, j, k: (i, j))
    base, count = (0, 3) if split is True else (split or (0, 0))
    if mode == "nn":
        a, al, (M, K) = _unlead(a)
        b, bl, (K2, N) = _unlead(b)
        a_spec = _bspec((tm, tk), lambda i, j, k: (i, k), al)
        b_spec = _bspec((tk, tn), lambda i, j, k: (k, j), bl)
        out_struct = jax.ShapeDtypeStruct((M, N), out_dtype)
        if split:
            assert tn == D_MODEL and N == count * tn
            out_spec = pl.BlockSpec((None, tm, tn), lambda i, j, k: (j + base, i, 0))
            out_struct = jax.ShapeDtypeStruct((3, M, tn), out_dtype)
    elif mode == "nt":
        b, bl, (N, K2) = _unlead(b)
        if split:
            assert tk == D_MODEL
            M, K = a.shape[1], count * a.shape[2]
            a_spec = pl.BlockSpec((None, tm, tk), lambda i, j, k: (k + base, i, 0))
        else:
            a, al, (M, K) = _unlead(a)
            a_spec = _bspec((tm, tk), lambda i, j, k: (i, k), al)
        b_spec = _bspec((tn, tk), lambda i, j, k: (j, k), bl)
        out_struct = jax.ShapeDtypeStruct((M, N), out_dtype)
    else:
        a, al, (K, M) = _unlead(a)
        if split:
            assert tn == D_MODEL
            K2, N = b.shape[1], count * b.shape[2]
            b_spec = pl.BlockSpec((None, tk, tn), lambda i, j, k: (j + base, k, 0))
        else:
            b, bl, (K2, N) = _unlead(b)
            b_spec = _bspec((tk, tn), lambda i, j, k: (k, j), bl)
        a_spec = _bspec((tk, tm), lambda i, j, k: (k, i), al)
        out_struct = jax.ShapeDtypeStruct((M, N), out_dtype)
    assert K == K2 and M % tm == 0 and N % tn == 0 and K % tk == 0, (a.shape, b.shape, mode, tm, tn, tk)
    nk = K // tk
    dims = (_DIMS[mode], ((), ()))
    has_add = add is not None

    narrow = out_dtype != F32
    assert not (narrow and has_add)

    def body(*refs):
        if into is not None:
            refs = refs[:2] + refs[3:]
        if has_add:
            a_ref, b_ref, add_ref, o_ref = refs
            acc_ref = o_ref
        elif narrow:
            a_ref, b_ref, o_ref, acc_ref = refs
        else:
            a_ref, b_ref, o_ref = refs
            acc_ref = o_ref
        k = pl.program_id(2)
        part = lax.dot_general(a_ref[...].astype(BF16), b_ref[...].astype(BF16), dims, preferred_element_type=F32)
        if has_add:
            @pl.when(k == 0)
            def _():
                acc_ref[...] = part + add_scale * add_ref[...]
        else:
            @pl.when(k == 0)
            def _():
                acc_ref[...] = part

        @pl.when(k > 0)
        def _():
            acc_ref[...] += part

        if narrow:
            @pl.when(k == nk - 1)
            def _():
                o_ref[...] = acc_ref[...].astype(out_dtype)

    in_specs = [a_spec, b_spec]
    args = [a, b]
    aliases = {}
    if into is not None:
        assert mode == "nn" and split and not has_add
        in_specs.append(pl.BlockSpec(memory_space=pl.ANY))
        args.append(into)
        aliases = {2: 0}
    if has_add:
        in_specs.append(pl.BlockSpec((tm, tn), lambda i, j, k: (i, j)))
        args.append(add)
    return pl.pallas_call(
        body, name=name, grid=(M // tm, N // tn, nk),
        in_specs=in_specs, out_specs=out_spec, out_shape=out_struct, input_output_aliases=aliases,
        scratch_shapes=[pltpu.VMEM((tm, tn), F32)] if narrow else [],
        compiler_params=_cparams(("parallel", "parallel", "arbitrary")),
    )(*args)


def _pick(n, cands):
    for c in cands:
        if n % c == 0:
            return c
    raise ValueError((n, cands))


def _mm_nn(a, b, out_dtype, name, split=False, into=None):
    M, K = _unlead(a)[2]
    N = _unlead(b)[2][1]
    return _matmul(a, b, "nn", out_dtype, _pick(M, (1024, 512, 256)), _pick(N, (1024, 512)), _pick(K, (1024, 512)), name,
                   split=split, into=into)


def _mm_nt(a, b, name, add=None, add_scale=1.0, split=False):
    M, K = (a.shape[1], D_MODEL) if split else _unlead(a)[2]
    N = _unlead(b)[2][0]
    return _matmul(a, b, "nt", F32, _pick(M, (1024, 512, 256)), _pick(N, (1024, 512)),
                   _pick(K, (2816, 1024, 512)), name, add=add, add_scale=add_scale, split=split)


def _mm_tn(a, b, name, split=False, out_dtype=F32):
    K, M = _unlead(a)[2]
    N = D_MODEL if split else _unlead(b)[2][1]
    return _matmul(a, b, "tn", out_dtype, _pick(M, (1024, 1408, 512)), _pick(N, (1408, 1024, 512)),
                   _pick(K, (2048, 1024, 512, 256)), name, split=split)


def _ffn_in(x, w, name):
    S = x.shape[0]
    tm = _pick(S, (512, 256))
    w, wl, _ = _unlead(w)

    def body(x_ref, w_ref, t_ref, h_ref):
        acc = jnp.dot(x_ref[...].astype(BF16), w_ref[...], preferred_element_type=F32)
        g = acc[:, :HALF_FF]
        up = acc[:, HALF_FF:]
        sg = jax.nn.sigmoid(g)
        silu = g * sg
        t_ref[:, :HALF_FF] = (up * (sg * (1.0 + g * (1.0 - sg)))).astype(BF16)
        t_ref[:, HALF_FF:] = silu.astype(BF16)
        h_ref[...] = (silu * up).astype(BF16)

    return pl.pallas_call(
        body, name=name, grid=(2, S // tm),
        in_specs=[pl.BlockSpec((tm, D_MODEL), lambda j, i: (i, 0)),
                  _bspec((D_MODEL, D_FF), lambda j, i: (0, j), wl)],
        out_specs=[pl.BlockSpec((tm, D_FF), lambda j, i: (i, j)),
                   pl.BlockSpec((tm, HALF_FF), lambda j, i: (i, j))],
        out_shape=[jax.ShapeDtypeStruct((S, 2 * D_FF), BF16), jax.ShapeDtypeStruct((S, D_FF), BF16)],
        compiler_params=_cparams(("parallel", "parallel")),
    )(x, w)


def _ffn_bwd_h(dzc, w_out, u, name):
    S = dzc.shape[0]
    tm = _pick(S, (512, 256))
    w_out, wl, _ = _unlead(w_out)

    def body(dz_ref, w_ref, t_ref, du_ref):
        dh = lax.dot_general(dz_ref[...], w_ref[...], (((1,), (1,)), ((), ())), preferred_element_type=F32)
        du_ref[:, :HALF_FF] = (dh * t_ref[:, :HALF_FF].astype(F32)).astype(BF16)
        du_ref[:, HALF_FF:] = (dh * t_ref[:, HALF_FF:].astype(F32)).astype(BF16)

    return pl.pallas_call(
        body, name=name, grid=(2, S // tm),
        in_specs=[pl.BlockSpec((tm, D_MODEL), lambda j, i: (i, 0)),
                  _bspec((HALF_FF, D_MODEL), lambda j, i: (j, 0), wl),
                  pl.BlockSpec((tm, D_FF), lambda j, i: (i, j))],
        out_specs=pl.BlockSpec((tm, D_FF), lambda j, i: (i, j)),
        out_shape=jax.ShapeDtypeStruct((S, 2 * D_FF), BF16),
        compiler_params=_cparams(("parallel", "parallel")),
    )(dzc, w_out, u)


def _mm_ln(a, w, resid, gain, bias, c, name):
    S, K = a.shape
    tm = _pick(S, (512, 256))
    w, wl, _ = _unlead(w)

    def body(a_ref, w_ref, r_ref, g_ref, b_ref, y_ref, yb_ref, z_ref):
        z = ALPHA * r_ref[...] + c * jnp.dot(a_ref[...], w_ref[...], preferred_element_type=F32)
        mu = jnp.mean(z, axis=-1, keepdims=True)
        zc = z - mu
        var = jnp.mean(zc * zc, axis=-1, keepdims=True)
        y = zc * lax.rsqrt(var + LN_EPS) * g_ref[...] + b_ref[...]
        z_ref[...] = z
        y_ref[...] = y
        yb_ref[...] = y.astype(BF16)

    row = pl.BlockSpec((tm, D_MODEL), lambda i: (i, 0))
    vec = pl.BlockSpec((1, D_MODEL), lambda i: (0, 0))
    return pl.pallas_call(
        body, name=name, grid=(S // tm,),
        in_specs=[pl.BlockSpec((tm, K), lambda i: (i, 0)), _bspec((K, D_MODEL), lambda i: (0, 0), wl), row, vec, vec],
        out_specs=[row, row, row],
        out_shape=[jax.ShapeDtypeStruct((S, D_MODEL), F32), jax.ShapeDtypeStruct((S, D_MODEL), BF16),
                   jax.ShapeDtypeStruct((S, D_MODEL), F32)],
        compiler_params=_cparams(("parallel",)),
    )(a, w, resid, gain, bias)


def _ln_bwd(z, dy, gain, c, name):
    S = z.shape[0]
    tm = _pick(S, (512, 256))

    def body(z_ref, dy_ref, g_ref, dz_ref, dzc_ref, gg_ref, gb_ref):
        i = pl.program_id(0)
        zv = z_ref[...]
        dyv = dy_ref[...]
        mu = jnp.mean(zv, axis=-1, keepdims=True)
        zc = zv - mu
        var = jnp.mean(zc * zc, axis=-1, keepdims=True)
        rstd = lax.rsqrt(var + LN_EPS)
        xhat = zc * rstd
        dyg = dyv * g_ref[...]
        m1 = jnp.mean(dyg, axis=-1, keepdims=True)
        m2 = jnp.mean(dyg * xhat, axis=-1, keepdims=True)
        dz = rstd * (dyg - m1 - xhat * m2)
        dz_ref[...] = dz
        dzc_ref[...] = (c * dz).astype(BF16)
        pg = jnp.sum((dyv * xhat).reshape(tm // 8, 8, D_MODEL), axis=0)
        pb = jnp.sum(dyv.reshape(tm // 8, 8, D_MODEL), axis=0)

        @pl.when(i == 0)
        def _():
            gg_ref[...] = pg
            gb_ref[...] = pb

        @pl.when(i > 0)
        def _():
            gg_ref[...] += pg
            gb_ref[...] += pb

    row = pl.BlockSpec((tm, D_MODEL), lambda i: (i, 0))
    part = pl.BlockSpec((8, D_MODEL), lambda i: (0, 0))
    return pl.pallas_call(
        body, name=name, grid=(S // tm,),
        in_specs=[row, row, pl.BlockSpec((1, D_MODEL), lambda i: (0, 0))],
        out_specs=[row, row, part, part],
        out_shape=[jax.ShapeDtypeStruct((S, D_MODEL), F32), jax.ShapeDtypeStruct((S, D_MODEL), BF16),
                   jax.ShapeDtypeStruct((8, D_MODEL), F32), jax.ShapeDtypeStruct((8, D_MODEL), F32)],
        compiler_params=_cparams(("arbitrary",)),
    )(z, dy, gain)


def _loss_grad(y, t, name):
    S = y.shape[0]
    tm = _pick(S, (512, 256))

    def body(y_ref, t_ref, dy_ref, sq_ref):
        i = pl.program_id(0)
        e = y_ref[...] - t_ref[...]
        dy_ref[...] = e * (1.0 / D_MODEL)
        ps = jnp.sum((e * e).reshape(tm // 8, 8, D_MODEL), axis=0)

        @pl.when(i == 0)
        def _():
            sq_ref[...] = ps

        @pl.when(i > 0)
        def _():
            sq_ref[...] += ps

    row = pl.BlockSpec((tm, D_MODEL), lambda i: (i, 0))
    return pl.pallas_call(
        body, name=name, grid=(S // tm,),
        in_specs=[row, row], out_specs=[row, pl.BlockSpec((8, D_MODEL), lambda i: (0, 0))],
        out_shape=[jax.ShapeDtypeStruct((S, D_MODEL), F32), jax.ShapeDtypeStruct((8, D_MODEL), F32)],
        compiler_params=_cparams(("arbitrary",)),
    )(y, t)


def _rows(start, d):
    if d == 1:
        return pl.ds(pl.multiple_of(start, BLOCK), BLOCK)
    return pl.ds(start, BLOCK, stride=d)


def _ld(ref, start, d):
    return ref[_rows(start, d), :]


def _ld3(ref, lead, start, d):
    return ref[lead, _rows(start, d), :]


def _st3(ref, lead, start, d, val):
    ref[lead, _rows(start, d), :] = val


def _acc3(ref, lead, start, d, val):
    ref[lead, _rows(start, d), :] = ref[lead, _rows(start, d), :] + val


def _band_consts(slope0, slope1, maxd, scale):
    row = lax.broadcasted_iota(jnp.int32, (2 * BLOCK, 2 * BLOCK), 0)
    kj = lax.broadcasted_iota(jnp.int32, (2 * BLOCK, 2 * BLOCK), 1)
    top = row < BLOCK
    dist = BLOCK + jnp.where(top, row, row - BLOCK) - kj
    slope = jnp.where(top, slope0, slope1)
    base = jnp.where((dist >= 0) & (dist <= maxd), -(slope * (dist.astype(F32) * scale)), NEG)
    return base, kj < BLOCK


def _stack_heads(x, lo):
    return jnp.concatenate([jnp.where(lo, x, 0.0), jnp.where(lo, 0.0, x)], axis=0)


def _unstack_heads(x2, lo):
    return jnp.where(lo, x2[:BLOCK], x2[BLOCK:])


def _scores(q2, k2, base, prev_keys, first):
    s = lax.dot_general(q2, k2, (((1,), (1,)), ((), ())), preferred_element_type=F32) * (HEAD_DIM ** -0.5) + base
    return jnp.where(jnp.logical_and(prev_keys, first), NEG, s)


def _softmax_weights(ls):
    mx = ls[0]
    for l in ls[1:]:
        mx = jnp.maximum(mx, l)
    es = [jnp.exp(l - mx) for l in ls]
    tot = es[0]
    for e in es[1:]:
        tot = tot + e
    inv = 1.0 / tot
    return [e * inv for e in es]


def _attn_fwd(qkv, slopes, sinks, patterns, name):
    S = qkv.shape[1]
    npat = len(patterns)
    has_sink = sinks is not None
    if not has_sink:
        sinks = jnp.zeros((N_HEADS,), F32)
    rows_c = 256

    def body(slopes_ref, sinks_ref, x_ref, mix_ref, o_ref, lse_ref):
        p = pl.program_id(0)
        lo = lax.broadcasted_iota(jnp.int32, (BLOCK, SLAB), 1) < HEAD_DIM
        top1 = lax.broadcasted_iota(jnp.int32, (2 * BLOCK, 1), 0) < BLOCK
        sk2 = jnp.where(top1, sinks_ref[2 * p], sinks_ref[2 * p + 1])
        for pi, (d, maxd, scale) in enumerate(patterns):
            nb = S // d // BLOCK
            base, prev_keys = _band_consts(slopes_ref[2 * p], slopes_ref[2 * p + 1], maxd, scale)

            def blk(t, carry, pi=pi, d=d, nb=nb, base=base, prev_keys=prev_keys):
                r = t // nb
                n = t - r * nb
                start = r + (d * BLOCK) * n
                prev = jnp.where(n > 0, start - d * BLOCK, start)
                q2 = _stack_heads(_ld3(x_ref, 0, start, d), lo).astype(BF16)
                k2 = jnp.concatenate([_ld3(x_ref, 1, prev, d), _ld3(x_ref, 1, start, d)], axis=0).astype(BF16)
                v2 = jnp.concatenate([_ld3(x_ref, 2, prev, d), _ld3(x_ref, 2, start, d)], axis=0).astype(BF16)
                s = _scores(q2, k2, base, prev_keys, n == 0)
                m = jnp.max(s, axis=-1, keepdims=True)
                if has_sink:
                    m = jnp.maximum(m, sk2)
                e = jnp.exp(s - m)
                den = jnp.sum(e, axis=-1, keepdims=True)
                if has_sink:
                    den = den + jnp.exp(sk2 - m)
                o2 = jnp.dot((e / den).astype(BF16), v2, preferred_element_type=F32)
                _st3(o_ref, pi, start, d, _unstack_heads(o2, lo))
                _st3(lse_ref, pi, start, d, _unstack_heads(m + jnp.log(den), lo))
                return carry

            lax.fori_loop(0, d * nb, blk, 0, unroll=8)

        def comb(ci, carry):
            rows = pl.ds(pl.multiple_of(ci * rows_c, rows_c), rows_c)
            if npat == 1:
                mix_ref[rows, :] = o_ref[0, rows, :].astype(BF16)
            else:
                ws = _softmax_weights([lse_ref[i, rows, :] for i in range(npat)])
                acc = ws[0] * o_ref[0, rows, :]
                for i in range(1, npat):
                    acc = acc + ws[i] * o_ref[i, rows, :]
                mix_ref[rows, :] = acc.astype(BF16)
            return carry

        lax.fori_loop(0, S // rows_c, comb, 0)

    smem = pl.BlockSpec(memory_space=pltpu.SMEM)
    slab3 = pl.BlockSpec((npat, S, SLAB), lambda p: (0, 0, p))
    return pl.pallas_call(
        body, name=name, grid=(N_SLABS,),
        in_specs=[smem, smem, pl.BlockSpec((3, S, SLAB), lambda p: (0, 0, p))],
        out_specs=[pl.BlockSpec((S, SLAB), lambda p: (0, p)), slab3, slab3],
        out_shape=[jax.ShapeDtypeStruct((S, D_MODEL), BF16), jax.ShapeDtypeStruct((npat, S, D_MODEL), F32),
                   jax.ShapeDtypeStruct((npat, S, D_MODEL), F32)],
        compiler_params=_cparams(("arbitrary",)),
    )(slopes, sinks, qkv)


def _attn_bwd(qkv, dout, o, lse, slopes, sinks, patterns, name):
    S = qkv.shape[1]
    npat = len(patterns)
    has_sink = sinks is not None
    if not has_sink:
        sinks = jnp.zeros((N_HEADS,), F32)
    rows_c = 256

    def headsum(x, lo):
        s0 = jnp.sum(jnp.where(lo, x, 0.0), axis=-1, keepdims=True)
        s1 = jnp.sum(jnp.where(lo, 0.0, x), axis=-1, keepdims=True)
        return jnp.where(lo, s0, s1)

    def body(slopes_ref, sinks_ref, x_ref, do_ref, o_ref, lse_ref, dx_ref, dsink_ref, dbar_ref, sacc_ref):
        p = pl.program_id(0)
        lo = lax.broadcasted_iota(jnp.int32, (BLOCK, SLAB), 1) < HEAD_DIM
        lo_c = lax.broadcasted_iota(jnp.int32, (rows_c, SLAB), 1) < HEAD_DIM
        top1 = lax.broadcasted_iota(jnp.int32, (2 * BLOCK, 1), 0) < BLOCK
        sk2 = jnp.where(top1, sinks_ref[2 * p], sinks_ref[2 * p + 1])

        def prep(ci, carry):
            rows = pl.ds(pl.multiple_of(ci * rows_c, rows_c), rows_c)
            dov = do_ref[rows, :]
            dx_ref[:, rows, :] = jnp.zeros((3, rows_c, SLAB), F32)
            if npat == 1:
                dbar_ref[rows, :] = headsum(dov * o_ref[0, rows, :], lo_c)
            else:
                ws = _softmax_weights([lse_ref[i, rows, :] for i in range(npat)])
                acc = ws[0] * headsum(dov * o_ref[0, rows, :], lo_c)
                for i in range(1, npat):
                    acc = acc + ws[i] * headsum(dov * o_ref[i, rows, :], lo_c)
                dbar_ref[rows, :] = acc
            return carry

        lax.fori_loop(0, S // rows_c, prep, 0)
        sacc_ref[...] = jnp.zeros((BLOCK, SLAB), F32)

        for pi, (d, maxd, scale) in enumerate(patterns):
            nb = S // d // BLOCK
            base, prev_keys = _band_consts(slopes_ref[2 * p], slopes_ref[2 * p + 1], maxd, scale)

            def blk(t, carry, pi=pi, d=d, nb=nb, base=base, prev_keys=prev_keys):
                r = t // nb
                n = t - r * nb
                start = r + (d * BLOCK) * n
                prev = jnp.where(n > 0, start - d * BLOCK, start)
                q2 = _stack_heads(_ld3(x_ref, 0, start, d), lo).astype(BF16)
                k2 = jnp.concatenate([_ld3(x_ref, 1, prev, d), _ld3(x_ref, 1, start, d)], axis=0).astype(BF16)
                v2 = jnp.concatenate([_ld3(x_ref, 2, prev, d), _ld3(x_ref, 2, start, d)], axis=0).astype(BF16)
                ls = [_ld3(lse_ref, i, start, d) for i in range(npat)]
                w = _softmax_weights(ls)[pi] if npat > 1 else 1.0
                do2 = _stack_heads(w * _ld(do_ref, start, d), lo).astype(BF16)
                dl = w * _ld(dbar_ref, start, d)
                lse2 = jnp.concatenate([ls[pi][:, :1], ls[pi][:, HEAD_DIM:HEAD_DIM + 1]], axis=0)
                dl2 = jnp.concatenate([dl[:, :1], dl[:, HEAD_DIM:HEAD_DIM + 1]], axis=0)
                s = _scores(q2, k2, base, prev_keys, n == 0)
                pr = jnp.exp(s - lse2)
                dp = lax.dot_general(do2, v2, (((1,), (1,)), ((), ())), preferred_element_type=F32)
                ds = (pr * (dp - dl2) * (HEAD_DIM ** -0.5)).astype(BF16)
                dq2 = jnp.dot(ds, k2, preferred_element_type=F32)
                dk2 = lax.dot_general(ds, q2, (((0,), (0,)), ((), ())), preferred_element_type=F32)
                dv2 = lax.dot_general(pr.astype(BF16), do2, (((0,), (0,)), ((), ())), preferred_element_type=F32)
                _acc3(dx_ref, 0, start, d, _unstack_heads(dq2, lo))
                _acc3(dx_ref, 1, prev, d, dk2[:BLOCK])
                _acc3(dx_ref, 1, start, d, dk2[BLOCK:])
                _acc3(dx_ref, 2, prev, d, dv2[:BLOCK])
                _acc3(dx_ref, 2, start, d, dv2[BLOCK:])
                if has_sink:
                    sacc_ref[...] += _unstack_heads(-jnp.exp(sk2 - lse2) * dl2, lo)
                return carry

            lax.fori_loop(0, d * nb, blk, 0, unroll=4)

        dsink_ref[...] = jnp.broadcast_to(jnp.sum(sacc_ref[...], axis=0, keepdims=True), (8, SLAB))

    smem = pl.BlockSpec(memory_space=pltpu.SMEM)
    one = pl.Buffered(1)
    slab3 = pl.BlockSpec((npat, S, SLAB), lambda p: (0, 0, p), pipeline_mode=one)
    return pl.pallas_call(
        body, name=name, grid=(N_SLABS,),
        in_specs=[smem, smem, pl.BlockSpec((3, S, SLAB), lambda p: (0, 0, p), pipeline_mode=one),
                  pl.BlockSpec((S, SLAB), lambda p: (0, p), pipeline_mode=one), slab3, slab3],
        out_specs=[pl.BlockSpec((3, S, SLAB), lambda p: (0, 0, p)), pl.BlockSpec((None, 8, SLAB), lambda p: (p, 0, 0))],
        out_shape=[jax.ShapeDtypeStruct((3, S, D_MODEL), F32), jax.ShapeDtypeStruct((N_SLABS, 8, SLAB), F32)],
        scratch_shapes=[pltpu.VMEM((S, SLAB), F32), pltpu.VMEM((BLOCK, SLAB), F32)],
        compiler_params=_cparams(("arbitrary",)),
    )(slopes, sinks, qkv, dout, o, lse)


def _place():
    x, y, c = lax.axis_index("x"), lax.axis_index("y"), lax.axis_index("c")
    return x, y, c, 2 * x + y


def _other_chips(x, y):
    return [(1 - x, y), (x, 1 - y), (1 - x, 1 - y)]


HBM_SPEC = pl.BlockSpec(memory_space=pl.ANY)


def _slot(q):
    return 2 * (q % 2) + q // 2


BIG = ("ffn1_w_in", "ffn1_w_out", "ffn2_w_in", "ffn2_w_out", "a_w_qkv", "a_w_o", "kv_w", "b_w_q", "b_w_o")
QKV_SHARD = 3 * D_MODEL // N_CHIPS
ROW_SHARD = D_MODEL // N_CHIPS


LAYER0_ITEMS = (("ffn1_w_in", 0), ("ffn1_w_out", 0), ("a_w_qkv", None), ("a_w_o", None), ("ffn2_w_in", 0),
                ("ffn2_w_out", 0), ("kv_w", None))
LAYER1_ITEMS = (("ffn1_w_in", 1), ("ffn1_w_out", 1), ("b_w_q", None), ("b_w_o", None), ("ffn2_w_in", 1),
                ("ffn2_w_out", 1))
OUT_SHARD = D_FF // N_CHIPS


def _full_shape(name):
    if name.endswith("w_in"):
        return (D_MODEL, 2 * D_FF)
    if name.endswith("w_out"):
        return (D_FF, D_MODEL)
    if name == "a_w_qkv":
        return (D_MODEL, 3 * D_MODEL)
    if name == "kv_w":
        return (N_CHIPS, 2, ROW_SHARD // 2, 2 * N_KV_B * HEAD_DIM)
    return (N_CHIPS, 2, ROW_SHARD // 2, D_MODEL)


def _gather_src(item, ref, c):
    name, _ = item
    if name.endswith("w_in"):
        return ref.at[pl.ds(c * (D_MODEL // 2), D_MODEL // 2)]
    if name.endswith("w_out"):
        return ref.at[pl.ds(c * (OUT_SHARD // 2), OUT_SHARD // 2)]
    if name == "a_w_qkv":
        return ref.at[0, pl.ds(c * (D_MODEL // 2), D_MODEL // 2)]
    if name == "kv_w":
        return ref.at[pl.ds(c * (ROW_SHARD // 2), ROW_SHARD // 2)]
    return ref.at[0, pl.ds(c * (ROW_SHARD // 2), ROW_SHARD // 2)]


def _gather_dst(item, ref, q, c):
    name, _ = item
    if name.endswith("w_in"):
        return ref.at[pl.ds(c * (D_MODEL // 2), D_MODEL // 2), pl.ds(_slot(q) * HALF_FF, HALF_FF)]
    if name.endswith("w_out"):
        return ref.at[pl.ds(q * OUT_SHARD + c * (OUT_SHARD // 2), OUT_SHARD // 2)]
    if name == "a_w_qkv":
        return ref.at[pl.ds(c * (D_MODEL // 2), D_MODEL // 2), pl.ds(q * QKV_SHARD, QKV_SHARD)]
    return ref.at[q, c]


def _all_gather(items, shards, small):
    n = len(items)
    r = small.shape[0]
    per = 8

    def body(*refs):
        srcs, small_ref = refs[:n], refs[n]
        dsts, s_ref = refs[n + 1:2 * n + 1], refs[2 * n + 1]
        send_sems, recv_sems = refs[2 * n + 2:]
        x, y, c, myq = _place()
        sibling = (x, y, 1 - c)
        chips = _other_chips(x, y)

        def big(t, k, src, q, h, to):
            return pltpu.make_async_remote_copy(src_ref=src, dst_ref=_gather_dst(items[t], dsts[t], q, h),
                                                send_sem=send_sems.at[per * t + k], recv_sem=recv_sems.at[per * t + k],
                                                device_id=to, device_id_type=MESH)

        def tiny(k, q, to):
            return pltpu.make_async_remote_copy(src_ref=small_ref, dst_ref=s_ref.at[q], send_sem=send_sems.at[per * n + k],
                                                recv_sem=recv_sems.at[per * n + k], device_id=to, device_id_type=MESH)

        first = []
        for j, chip in enumerate(chips):
            first += [big(t, j, _gather_src(items[t], srcs[t], c), myq, c, (*chip, c)) for t in range(n)]
            first.append(tiny(j, myq, (*chip, c)))
        own = [big(t, 6 + h, _gather_src(items[t], srcs[t], h), myq, h, sibling) for t in range(n) for h in (0, 1)]
        own.append(tiny(3, myq, sibling))
        for cp in first + own:
            cp.start()
        passed = []
        for j, (cx, cy) in enumerate(chips):
            q = 2 * cx + cy
            for t in range(n):
                src = _gather_src(items[t], srcs[t], c)
                big(t, j, src, q, c, sibling).wait_recv()
                fwd = big(t, 3 + j, _gather_dst(items[t], dsts[t], q, c), q, c, sibling)
                fwd.start()
                passed.append(fwd)
        for j, (cx, cy) in enumerate(chips):
            q = 2 * cx + cy
            for t in range(n):
                big(t, 3 + j, _gather_src(items[t], srcs[t], c), q, 1 - c, sibling).wait_recv()
            tiny(j, q, sibling).wait_recv()
        for cp in own:
            cp.wait_recv()
        for cp in first + passed + own:
            cp.wait_send()

    outs = pl.pallas_call(
        body, name="all_gather_layer0",
        in_specs=[HBM_SPEC] * (n + 1), out_specs=[HBM_SPEC] * (n + 1),
        out_shape=[jax.ShapeDtypeStruct(_full_shape(name), BF16) for name, _ in items]
        + [jax.ShapeDtypeStruct((N_CHIPS, r, 128), F32)],
        scratch_shapes=[pltpu.SemaphoreType.DMA((per * n + 4,)), pltpu.SemaphoreType.DMA((per * n + 4,))],
    )(*[shards[item] for item in items], small)
    return list(outs[:n]), outs[n]


SEM_SPEC = pl.BlockSpec(memory_space=pltpu.SEMAPHORE)
DATAFLOW = pltpu.SideEffectType.DATAFLOW_SIDE_EFFECTING
PER_ITEM = 8


def _split_start(name, copies, n_sems, sources, land_shapes, after):
    n, m = len(sources), len(land_shapes)

    def body(*refs):
        srcs, lands = refs[:n], refs[n:n + m]
        send_sems, recv_sems = refs[n + m + 1], refs[n + m + 2]
        token = refs[-1]
        for src, dst_there, _, s, peer in copies(srcs, lands):
            pltpu.make_async_remote_copy(src_ref=src, dst_ref=dst_there, send_sem=send_sems.at[s], recv_sem=recv_sems.at[s],
                                         device_id=peer, device_id_type=MESH).start()
        token[...] = jnp.zeros_like(token)

    src_arrays = [pltpu.with_memory_space_constraint(a, pltpu.HBM) for a in sources]
    land_arrays = [pltpu.with_memory_space_constraint(lax.empty(s.shape, s.dtype), pltpu.HBM) for s in land_shapes]
    hbm = pl.BlockSpec(memory_space=pltpu.HBM)
    outs = pl.pallas_call(
        body, name=name,
        in_specs=[hbm] * (n + m) + [HBM_SPEC],
        out_specs=[SEM_SPEC, SEM_SPEC] + [hbm] * (n + m) + [pl.BlockSpec(memory_space=pltpu.VMEM)],
        out_shape=[pltpu.SemaphoreType.DMA((n_sems,)), pltpu.SemaphoreType.DMA((n_sems,))]
        + [pltpu.HBM(a.shape, a.dtype) for a in src_arrays + land_arrays] + [jax.ShapeDtypeStruct((8, 128), F32)],
        input_output_aliases={i: 2 + i for i in range(n + m)},
        compiler_params=pltpu.CompilerParams(has_side_effects=DATAFLOW),
    )(*src_arrays, *land_arrays, after)
    return (outs[0], outs[1], list(outs[2:2 + n]), list(outs[2 + n:2 + n + m])), outs[-1]


def _split_wait(name, copies, state, after):
    send_sems, recv_sems, srcs_thru, lands_thru = state
    n, m = len(srcs_thru), len(lands_thru)

    def body(*refs):
        srcs, lands = refs[:n], refs[n:n + m]
        send_sems, recv_sems = refs[n + m], refs[n + m + 1]
        for src, _, dst_here, s, peer in copies(srcs, lands):
            cp = pltpu.make_async_remote_copy(src_ref=src, dst_ref=dst_here, send_sem=send_sems.at[s], recv_sem=recv_sems.at[s],
                                              device_id=peer, device_id_type=MESH)
            cp.wait_send()
            cp.wait_recv()

    hbm = pl.BlockSpec(memory_space=pltpu.HBM)
    outs = pl.pallas_call(
        body, name=name,
        in_specs=[hbm] * (n + m) + [SEM_SPEC, SEM_SPEC, HBM_SPEC],
        out_specs=[hbm] * (n + m),
        out_shape=[pltpu.HBM(a.shape, a.dtype) for a in srcs_thru + lands_thru],
        input_output_aliases={i: i for i in range(n + m)},
        compiler_params=pltpu.CompilerParams(has_side_effects=DATAFLOW),
    )(*srcs_thru, *lands_thru, send_sems, recv_sems, after)
    return list(outs[:n]), list(outs[n:])


def _gather_copies(items):
    def copies(srcs, lands):
        x, y, c, myq = _place()
        out = []
        for t, item in enumerate(items):
            for h in (0, 1):
                src = _gather_src(item, srcs[t], h)
                for j, (cx, cy) in enumerate(_other_chips(x, y)):
                    out.append((src, _gather_dst(item, lands[t], myq, h), _gather_dst(item, lands[t], 2 * cx + cy, h),
                                PER_ITEM * t + 2 * j + h, (cx, cy, c)))
                out.append((src, _gather_dst(item, lands[t], myq, h), _gather_dst(item, lands[t], myq, h),
                            PER_ITEM * t + 6 + h, (x, y, 1 - c)))
        return out
    return copies


def _gather_start(items, shards, after):
    lands = [jax.ShapeDtypeStruct(_full_shape(name), BF16) for name, _ in items]
    return _split_start("gather_layer1_start", _gather_copies(items), PER_ITEM * len(items),
                        [shards[item] for item in items], lands, after)


def _gather_wait(items, state, after):
    return _split_wait("gather_layer1_wait", _gather_copies(items), state, after)[1]


def _small_all_reduce(v):
    r = v.shape[0]

    def body(v_ref, o_ref, buf_ref, send_sems, recv_sems):
        x, y, c, _ = _place()
        me = 4 * x + 2 * y + c
        buf_ref[me] = v_ref[...]
        copies = []
        for k in range(1, 8):
            fx, fy, fc = (k >> 2) & 1, (k >> 1) & 1, k & 1
            to = (x ^ fx, y ^ fy, c ^ fc)
            cp = pltpu.make_async_remote_copy(src_ref=v_ref, dst_ref=buf_ref.at[me], send_sem=send_sems.at[k - 1],
                                              recv_sem=recv_sems.at[k - 1], device_id=to, device_id_type=MESH)
            cp.start()
            copies.append(cp)
        for k in range(1, 8):
            fx, fy, fc = (k >> 2) & 1, (k >> 1) & 1, k & 1
            src_dev = 4 * (x ^ fx) + 2 * (y ^ fy) + (c ^ fc)
            pltpu.make_async_remote_copy(src_ref=v_ref, dst_ref=buf_ref.at[src_dev], send_sem=send_sems.at[k - 1],
                                         recv_sem=recv_sems.at[k - 1], device_id=(x, y, c), device_id_type=MESH).wait_recv()
        for cp in copies:
            cp.wait_send()
        tot = buf_ref[0]
        for i in range(1, 8):
            tot = tot + buf_ref[i]
        o_ref[...] = tot

    vm = pl.BlockSpec(memory_space=pltpu.VMEM)
    return pl.pallas_call(
        body, name="small_all_reduce", in_specs=[vm], out_specs=vm,
        out_shape=jax.ShapeDtypeStruct((r, 128), F32),
        scratch_shapes=[pltpu.VMEM((8, r, 128), F32), pltpu.SemaphoreType.DMA((7,)), pltpu.SemaphoreType.DMA((7,))],
    )(v)


def _grad_view(kind, g):
    if kind == "col":
        return g.reshape(2, g.shape[0] // 2, g.shape[1])
    return g.reshape(N_CHIPS, 2, g.shape[0] // (2 * N_CHIPS), g.shape[1])


def _half_of(kind, ref, h):
    return ref.at[h] if kind == "col" else ref.at[:, h]


def _half_shape(kind, view_shape):
    return view_shape[1:] if kind == "col" else (view_shape[0],) + view_shape[2:]


def _piece_of(kind, width, colblock, ref, q):
    if kind == "col":
        return ref.at[:, pl.ds(colblock(q) * width, width)]
    return ref.at[q]


def _piece_shape(kind, width, half_shape):
    return (half_shape[0], width) if kind == "col" else half_shape[1:]


def _pair_exchange(views, kinds, name):
    n = len(views)

    def body(*refs):
        ins, outs = refs[:n], refs[n:2 * n]
        send_sems, recv_sems = refs[2 * n:]
        x, y, c, _ = _place()
        cps = []
        for t in range(n):
            cp = pltpu.make_async_remote_copy(src_ref=_half_of(kinds[t], ins[t], 1 - c), dst_ref=outs[t],
                                              send_sem=send_sems.at[t], recv_sem=recv_sems.at[t],
                                              device_id=(x, y, 1 - c), device_id_type=MESH)
            cp.start()
            cps.append(cp)
        for cp in cps:
            cp.wait()

    return pl.pallas_call(
        body, name=name, in_specs=[HBM_SPEC] * n, out_specs=[HBM_SPEC] * n,
        out_shape=[jax.ShapeDtypeStruct(_half_shape(k, v.shape), v.dtype) for k, v in zip(kinds, views)],
        scratch_shapes=[pltpu.SemaphoreType.DMA((n,)), pltpu.SemaphoreType.DMA((n,))],
    )(*views)


def _pair_sum(kind, view, recv, c, name):
    hs = recv.shape
    N = hs[-1]
    rows = hs[-2]
    tr = _pick(rows, (512, 352, 128))
    tn = _pick(N, (1408, 1024, 512))

    def body(c_ref, p_ref, r_ref, s_ref):
        s_ref[...] = (p_ref[...] + r_ref[...]).astype(BF16)

    if kind == "col":
        grid = (rows // tr, N // tn)
        mine = pl.BlockSpec((None, tr, tn), lambda i, j, c_ref: (c_ref[0], i, j))
        blk = pl.BlockSpec((tr, tn), lambda i, j, c_ref: (i, j))
        sem = ("parallel", "parallel")
    else:
        grid = (N_CHIPS, rows // tr, N // tn)
        mine = pl.BlockSpec((None, None, tr, tn), lambda q, i, j, c_ref: (q, c_ref[0], i, j))
        blk = pl.BlockSpec((None, tr, tn), lambda q, i, j, c_ref: (q, i, j))
        sem = ("parallel", "parallel", "parallel")
    return pl.pallas_call(
        body, name=name,
        grid_spec=pltpu.PrefetchScalarGridSpec(num_scalar_prefetch=1, grid=grid, in_specs=[mine, blk], out_specs=blk),
        out_shape=jax.ShapeDtypeStruct(hs, BF16),
        compiler_params=_cparams(sem),
    )(c.reshape(1).astype(jnp.int32), view, recv)


def _chip_copies(kinds, widths, colblocks):
    def copies(srcs, lands):
        x, y, c, _ = _place()
        out = []
        for j, (cx, cy) in enumerate(_other_chips(x, y)):
            for t in range(len(kinds)):
                out.append((_piece_of(kinds[t], widths[t], colblocks[t], srcs[t], 2 * cx + cy), lands[t].at[j],
                            lands[t].at[j], 3 * t + j, (cx, cy, c)))
        return out
    return copies


def _chip_land_shapes(sums, kinds, widths):
    return [jax.ShapeDtypeStruct((3,) + _piece_shape(k, w, s.shape), BF16) for k, w, s in zip(kinds, widths, sums)]


def _chip_exchange(sums, kinds, widths, colblocks, name):
    n = len(sums)
    copies = _chip_copies(kinds, widths, colblocks)

    def body(*refs):
        send_sems, recv_sems = refs[2 * n:]
        cps = [pltpu.make_async_remote_copy(src_ref=src, dst_ref=dst, send_sem=send_sems.at[s], recv_sem=recv_sems.at[s],
                                            device_id=peer, device_id_type=MESH)
               for src, dst, _, s, peer in copies(refs[:n], refs[n:2 * n])]
        for cp in cps:
            cp.start()
        for cp in cps:
            cp.wait()

    return pl.pallas_call(
        body, name=name, in_specs=[HBM_SPEC] * n, out_specs=[HBM_SPEC] * n,
        out_shape=_chip_land_shapes(sums, kinds, widths),
        scratch_shapes=[pltpu.SemaphoreType.DMA((3 * n,)), pltpu.SemaphoreType.DMA((3 * n,))],
    )(*sums)


N_DIRECT = 7


def _direct_piece(kind, width, colblock, view_ref, q, h):
    if kind == "col":
        return view_ref.at[h, :, pl.ds(colblock(q) * width, width)]
    return view_ref.at[q, h]


def _direct_copies(kinds, widths, colblocks):
    def copies(srcs, lands):
        x, y, c, myq = _place()
        out = []
        for t in range(len(kinds)):
            def piece(q, h, t=t):
                return _direct_piece(kinds[t], widths[t], colblocks[t], srcs[t], q, h)
            for j, (cx, cy) in enumerate(_other_chips(x, y)):
                for h in (0, 1):
                    out.append((piece(2 * cx + cy, h), lands[t].at[2 * j + c], lands[t].at[2 * j + h],
                                10 * t + 3 * j + c + h, (cx, cy, h)))
            out.append((piece(myq, 1 - c), lands[t].at[6], lands[t].at[6], 10 * t + 9, (x, y, 1 - c)))
        return out
    return copies


def _chip_sum(kind, own_src, recv, block_idx, c, shard_shape, layer, into, name, direct=False):
    n_recv, rows, N = recv.shape
    tr = _pick(rows, (512, 352, 128))
    tn = _pick(N, (1408, 1024, 768, 512))
    ni, nj = rows // tr, N // tn

    def body(q_ref, s_ref, r_ref, *rest):
        o_ref = rest[-1]
        tot = s_ref[...].astype(F32)
        for k in range(n_recv):
            tot = tot + r_ref[k].astype(F32)
        o_ref[...] = tot

    if direct and kind == "col":
        own = pl.BlockSpec((None, tr, tn), lambda i, j, q_ref: (q_ref[1], i, q_ref[0] * nj + j))
    elif direct:
        own = pl.BlockSpec((None, None, tr, tn), lambda i, j, q_ref: (q_ref[0], q_ref[1], i, j))
    elif kind == "col":
        own = pl.BlockSpec((tr, tn), lambda i, j, q_ref: (i, q_ref[0] * nj + j))
    else:
        own = pl.BlockSpec((None, tr, tn), lambda i, j, q_ref: (q_ref[0], i, j))
    if len(shard_shape) == 3:
        lead = 0 if layer is None else layer
        out_spec = pl.BlockSpec((None, tr, tn), lambda i, j, q_ref: (lead, q_ref[1] * ni + i, j))
    else:
        out_spec = pl.BlockSpec((tr, tn), lambda i, j, q_ref: (q_ref[1] * ni + i, j))
    in_specs = [own, pl.BlockSpec((n_recv, tr, tn), lambda i, j, q_ref: (0, i, j))]
    s = own_src
    args = [jnp.stack([block_idx, c]).astype(jnp.int32), s, recv]
    aliases = {}
    if into is not None:
        in_specs.append(HBM_SPEC)
        args.append(into)
        aliases = {3: 0}
    return pl.pallas_call(
        body, name=name,
        grid_spec=pltpu.PrefetchScalarGridSpec(num_scalar_prefetch=1, grid=(ni, nj), in_specs=in_specs, out_specs=out_spec),
        out_shape=jax.ShapeDtypeStruct(shard_shape, F32), input_output_aliases=aliases,
        compiler_params=_cparams(("parallel", "parallel")),
    )(*args)


def _half_window(ref, h):
    rows = ref.shape[-2] // 2
    if ref.ndim == 3:
        return ref.at[:, pl.ds(h * rows, rows)]
    return ref.at[pl.ds(h * rows, rows)]


def _share_halves(grads):
    n = len(grads)

    def body(*refs):
        outs = refs[n:2 * n]
        send_sems, recv_sems = refs[2 * n:]
        x, y, c, _ = _place()
        cps = []
        for t in range(n):
            cp = pltpu.make_async_remote_copy(src_ref=_half_window(outs[t], c), dst_ref=_half_window(outs[t], c),
                                              send_sem=send_sems.at[t], recv_sem=recv_sems.at[t],
                                              device_id=(x, y, 1 - c), device_id_type=MESH)
            cp.start()
            cps.append(cp)
        for t in range(n):
            cps[t].wait_send()
            pltpu.make_async_remote_copy(src_ref=_half_window(outs[t], c), dst_ref=_half_window(outs[t], 1 - c),
                                         send_sem=send_sems.at[t], recv_sem=recv_sems.at[t],
                                         device_id=(x, y, 1 - c), device_id_type=MESH).wait_recv()

    return pl.pallas_call(
        body, name="grad_share_halves", in_specs=[HBM_SPEC] * n, out_specs=[HBM_SPEC] * n,
        out_shape=[jax.ShapeDtypeStruct(g.shape, F32) for g in grads],
        input_output_aliases={t: t for t in range(n)},
        scratch_shapes=[pltpu.SemaphoreType.DMA((n,)), pltpu.SemaphoreType.DMA((n,))],
    )(*grads)


def _adamw(w, g, m, v, name):
    R, W = w.shape
    tr = _pick(R, (512, 352, 256, 32))

    def body(w_ref, g_ref, m_ref, v_ref, d_ref, nm_ref, nv_ref):
        gv = g_ref[...]
        nm = ADAM_B1 * m_ref[...] + (1.0 - ADAM_B1) * gv
        nv = ADAM_B2 * v_ref[...] + (1.0 - ADAM_B2) * (gv * gv)
        m_hat = nm / (1.0 - ADAM_B1 ** ADAM_STEP)
        v_hat = nv / (1.0 - ADAM_B2 ** ADAM_STEP)
        d_ref[...] = -ADAM_LR * (m_hat / (jnp.sqrt(v_hat) + ADAM_EPS) + ADAM_WD * w_ref[...])
        nm_ref[...] = nm
        nv_ref[...] = nv

    blk = pl.BlockSpec((tr, W), lambda i: (i, 0))
    shp = jax.ShapeDtypeStruct((R, W), F32)
    return pl.pallas_call(
        body, name=name, grid=(R // tr,), in_specs=[blk] * 4, out_specs=[blk] * 3, out_shape=[shp] * 3,
        compiler_params=_cparams(("parallel",)),
    )(w, g, m, v)


SMALL_ROWS = 32


def _pack_small(ln_g, ln_b, sinks):
    rows = jnp.concatenate([ln_g.reshape(-1, 128), ln_b.reshape(-1, 128),
                            jnp.pad(sinks.reshape(1, -1), ((0, 0), (0, 128 - sinks.size)))], axis=0)
    return jnp.pad(rows, ((0, SMALL_ROWS - rows.shape[0]), (0, 0)))


def _unpack_small(s, ln_shape, sink_shape):
    n = ln_shape[0] * ln_shape[1] * ln_shape[2] // 128
    return s[:n].reshape(ln_shape), s[n:2 * n].reshape(ln_shape), s[2 * n, :sink_shape[1]].reshape(sink_shape)


def _ffn_fwd(xin, w_in, w_out, gain, bias, tag):
    u, h = _ffn_in(xin, w_in, "ffn_in_" + tag)
    y, yb, z = _mm_ln(h, w_out, xin, gain, bias, 0.5, "ffn_out_ln_" + tag)
    return y, yb, dict(u=u, h=h, z=z, xin=xin)


def _ffn_bwd(dy, saved, w_in, w_out, gain, xin_b, tag, dw_dtype=F32):
    dz, dzc, gg, gb = _ln_bwd(saved["z"], dy, gain, 0.5, "ln_bwd_" + tag)
    du = _ffn_bwd_h(dzc, w_out, saved["u"], "ffn_bwd_h_" + tag)
    d_w_out = _mm_tn(saved["h"], dzc, "ffn_dwout_" + tag, out_dtype=dw_dtype)
    d_w_in = _mm_tn(xin_b, du, "ffn_dwin_" + tag, out_dtype=dw_dtype)
    dx = _mm_nt(du, w_in, "ffn_dx_" + tag, add=dz, add_scale=ALPHA)
    return dx, d_w_in, d_w_out, gg, gb


def kernel(x, ffn1_w_in, ffn1_w_out, ffn2_w_in, ffn2_w_out, ln_g, ln_b, a_w_qkv, a_w_o, kv_w, b_w_q, b_sinks, b_w_o, loss_target, m_ffn1_w_in, m_ffn1_w_out, m_ffn2_w_in, m_ffn2_w_out, m_ln_g, m_ln_b, m_a_w_qkv, m_a_w_o, m_kv_w, m_b_w_q, m_b_sinks, m_b_w_o, v_ffn1_w_in, v_ffn1_w_out, v_ffn2_w_in, v_ffn2_w_out, v_ln_g, v_ln_b, v_a_w_qkv, v_a_w_o, v_kv_w, v_b_w_q, v_b_sinks, v_b_w_o):
    ws = dict(ffn1_w_in=ffn1_w_in, ffn1_w_out=ffn1_w_out, ffn2_w_in=ffn2_w_in, ffn2_w_out=ffn2_w_out, a_w_qkv=a_w_qkv,
              a_w_o=a_w_o, kv_w=kv_w, b_w_q=b_w_q, b_w_o=b_w_o)
    ms = dict(ffn1_w_in=m_ffn1_w_in, ffn1_w_out=m_ffn1_w_out, ffn2_w_in=m_ffn2_w_in, ffn2_w_out=m_ffn2_w_out,
              a_w_qkv=m_a_w_qkv, a_w_o=m_a_w_o, kv_w=m_kv_w, b_w_q=m_b_w_q, b_w_o=m_b_w_o)
    vs = dict(ffn1_w_in=v_ffn1_w_in, ffn1_w_out=v_ffn1_w_out, ffn2_w_in=v_ffn2_w_in, ffn2_w_out=v_ffn2_w_out,
              a_w_qkv=v_a_w_qkv, a_w_o=v_a_w_o, kv_w=v_kv_w, b_w_q=v_b_w_q, b_w_o=v_b_w_o)
    _, _, c_idx, myq = _place()
    xs = x[0]
    target = loss_target[0]

    shards = {(n, l): (ws[n] if l is None else ws[n][l]).astype(BF16) for n, l in LAYER0_ITEMS + LAYER1_ITEMS}

    def as_weights(items, arrays):
        return {n: (a.reshape(D_MODEL, a.shape[-1]) if a.ndim == 4 else a) for (n, _), a in zip(items, arrays)}

    full0, small = _all_gather(LAYER0_ITEMS, shards, _pack_small(ln_g, ln_b, b_sinks))
    gather_state, token = _gather_start(LAYER1_ITEMS, shards, small)

    def layer1_weights(after):
        return as_weights(LAYER1_ITEMS, _gather_wait(LAYER1_ITEMS, gather_state, after))

    n_ln = ln_g.size // 128
    lg = jnp.concatenate([small[q, :n_ln].reshape(DEPTH, 3, 1, -1) for q in range(N_CHIPS)], axis=-1)
    lb = jnp.concatenate([small[q, n_ln:2 * n_ln].reshape(DEPTH, 3, 1, -1) for q in range(N_CHIPS)], axis=-1)
    lg = lg + token[0, 0]
    reducer = _GradReducer(c_idx, myq, {n: ws[n].shape for n in BIG})
    sq, grad_x, _, gg, gb, dsink_part = _local_step(xs, target, as_weights(LAYER0_ITEMS, full0), layer1_weights,
                                                    lg, lb, b_sinks.reshape(N_HEADS), reducer.begin)

    loss_row = jnp.pad(jnp.sum(sq).reshape(1, 1), ((0, 0), (0, 127)))
    dsinks = jnp.pad(dsink_part[:, 0, :].reshape(N_SLABS, 2, HEAD_DIM)[:, :, 0].reshape(1, N_HEADS), ((0, 0), (0, 128 - N_HEADS)))
    gg_full = jnp.stack([jnp.stack([jnp.sum(gg[i][j], axis=0) for j in range(3)]) for i in range(DEPTH)])
    gb_full = jnp.stack([jnp.stack([jnp.sum(gb[i][j], axis=0) for j in range(3)]) for i in range(DEPTH)])
    small_in = jnp.concatenate([loss_row, dsinks, gg_full.reshape(-1, 128), gb_full.reshape(-1, 128)], axis=0)
    small_in = jnp.pad(small_in, ((0, (-small_in.shape[0]) % 8), (0, 0)))
    small_sum = _small_all_reduce(small_in)
    loss = small_sum[0, 0] * (0.5 / D_MODEL)
    grad_sinks = small_sum[1, :N_HEADS].reshape(b_sinks.shape)
    n_full = DEPTH * 3 * D_MODEL // 128
    cols = D_MODEL // N_CHIPS
    grad_ln_g = lax.dynamic_slice_in_dim(small_sum[2:2 + n_full].reshape(DEPTH, 3, D_MODEL), myq * cols, cols, axis=2)
    grad_ln_b = lax.dynamic_slice_in_dim(small_sum[2 + n_full:2 + 2 * n_full].reshape(DEPTH, 3, D_MODEL), myq * cols, cols, axis=2)
    return _update(reducer, grad_x, loss, grad_ln_g, grad_ln_b, grad_sinks, ws, ms, vs,
                   (ln_g, ln_b, b_sinks), (m_ln_g, m_ln_b, m_b_sinks), (v_ln_g, v_ln_b, v_b_sinks))


def _local_step(xs, target, W, layer1_weights, lg, lb, sinks, grads_ready=None):
    if grads_ready is None:
        grads_ready = lambda tag, grads, overlap: 0.0
    S = xs.shape[0]
    slopes = jnp.asarray(_alibi_slopes(N_HEADS))
    in1, out1, in2, out2 = [W["ffn1_w_in"]], [W["ffn1_w_out"]], [W["ffn2_w_in"]], [W["ffn2_w_out"]]

    y1, y1b, s1 = _ffn_fwd(xs, in1[0], out1[0], lg[0, 0], lb[0, 0], "a1")
    qkv_a = _mm_nn(y1b, W["a_w_qkv"], F32, "qkv_a", split=True)
    mix_a, o_a, lse_a = _attn_fwd(qkv_a, slopes, None, PATTERNS_A, "attn_a_fwd")
    y2, y2b, z2 = _mm_ln(mix_a, W["a_w_o"], y1, lg[0, 1], lb[0, 1], 1.0, "attn_a_out_ln")
    y3, y3b, s3 = _ffn_fwd(y2, in2[0], out2[0], lg[0, 2], lb[0, 2], "a2")
    kv_w_rep = jnp.broadcast_to(W["kv_w"].reshape(D_MODEL, 2, N_KV_B, 1, HEAD_DIM),
                                (D_MODEL, 2, N_KV_B, GROUP_B, HEAD_DIM)).reshape(D_MODEL, 2 * D_MODEL)
    kv_rep = _mm_nn(y3b, kv_w_rep, F32, "kv_proj", split=(1, 2))
    W = dict(W, **layer1_weights(kv_rep))
    in1, out1, in2, out2 = (in1 + [W["ffn1_w_in"]], out1 + [W["ffn1_w_out"]], in2 + [W["ffn2_w_in"]],
                            out2 + [W["ffn2_w_out"]])
    y4, y4b, s4 = _ffn_fwd(y3, in1[1], out1[1], lg[1, 0], lb[1, 0], "b1")
    qkv_b = _mm_nn(y4b, W["b_w_q"], F32, "q_b", split=(0, 1), into=kv_rep)
    mix_b, o_b, lse_b = _attn_fwd(qkv_b, slopes, sinks, PATTERNS_B, "attn_b_fwd")
    y5, y5b, z5 = _mm_ln(mix_b, W["b_w_o"], y4, lg[1, 1], lb[1, 1], 1.0, "attn_b_out_ln")
    y6, _, s6 = _ffn_fwd(y5, in2[1], out2[1], lg[1, 2], lb[1, 2], "b2")

    dy6, sq = _loss_grad(y6, target, "loss_grad")
    gr = {n: None for n in BIG}
    gg = [[None] * 3 for _ in range(DEPTH)]
    gb = [[None] * 3 for _ in range(DEPTH)]

    dy5, d_in2_b, d_out2_b, gg[1][2], gb[1][2] = _ffn_bwd(dy6, s6, in2[1], out2[1], lg[1, 2], y5b, "b2", BF16)
    dz5, dz5b, gg[1][1], gb[1][1] = _ln_bwd(z5, dy5, lg[1, 1], 1.0, "ln_bwd_attn_b")
    gr["b_w_o"] = _mm_tn(mix_b, dz5b, "d_b_w_o", out_dtype=BF16)
    dmix_b = _mm_nt(dz5b, W["b_w_o"], "d_mix_b")
    dqkv_b, dsink_part = _attn_bwd(qkv_b, dmix_b, o_b, lse_b, slopes, sinks, PATTERNS_B, "attn_b_bwd")
    dq_b = (dqkv_b, 0)
    gr["b_w_q"] = _mm_tn(y4b, dq_b, "d_b_w_q", out_dtype=BF16)
    dy4 = _mm_nt(dq_b, W["b_w_q"], "d_y4", add=dz5, add_scale=ALPHA)
    dy3, d_in1_b, d_out1_b, gg[1][0], gb[1][0] = _ffn_bwd(dy4, s4, in1[1], out1[1], lg[1, 0], y3b, "b1", BF16)
    d_kv_w_rep = _mm_tn(y3b, dqkv_b, "d_kv_w", split=(1, 2))
    gr["kv_w"] = d_kv_w_rep.reshape(D_MODEL, 2, N_KV_B, GROUP_B, HEAD_DIM).sum(axis=3).reshape(D_MODEL, -1).astype(BF16)
    dy3 = _mm_nt(dqkv_b, kv_w_rep, "d_y3_kv", add=dy3, add_scale=1.0, split=(1, 2))
    tok = grads_ready("l1", {("ffn2_w_in", 1): d_in2_b, ("ffn2_w_out", 1): d_out2_b, ("b_w_o", None): gr["b_w_o"],
                             ("b_w_q", None): gr["b_w_q"], ("ffn1_w_in", 1): d_in1_b, ("ffn1_w_out", 1): d_out1_b,
                             ("kv_w", None): gr["kv_w"]}, True)
    lg0 = lg[0] + tok

    dy2, d_in2_a, d_out2_a, gg[0][2], gb[0][2] = _ffn_bwd(dy3, s3, in2[0], out2[0], lg0[2], y2b, "a2", BF16)
    tok = grads_ready("a2", {("ffn2_w_in", 0): d_in2_a, ("ffn2_w_out", 0): d_out2_a}, True)
    lg0 = lg0 + tok
    dz2, dz2b, gg[0][1], gb[0][1] = _ln_bwd(z2, dy2, lg0[1], 1.0, "ln_bwd_attn_a")
    gr["a_w_o"] = _mm_tn(mix_a, dz2b, "d_a_w_o", out_dtype=BF16)
    dmix_a = _mm_nt(dz2b, W["a_w_o"], "d_mix_a")
    dqkv_a, _ = _attn_bwd(qkv_a, dmix_a, o_a, lse_a, slopes, None, PATTERNS_A, "attn_a_bwd")
    gr["a_w_qkv"] = _mm_tn(y1b, dqkv_a, "d_a_w_qkv", split=True, out_dtype=BF16)
    tok = grads_ready("mix", {("a_w_o", None): gr["a_w_o"], ("a_w_qkv", None): gr["a_w_qkv"]}, True)
    lg0 = lg0 + tok
    dy1 = _mm_nt(dqkv_a, W["a_w_qkv"], "d_y1", add=dz2, add_scale=ALPHA, split=True)
    grad_x, d_in1_a, d_out1_a, gg[0][0], gb[0][0] = _ffn_bwd(dy1, s1, in1[0], out1[0], lg0[0], xs, "a1")
    grads_ready("a1", {("ffn1_w_in", 0): d_in1_a, ("ffn1_w_out", 0): d_out1_a}, False)
    gr["ffn1_w_in"] = [d_in1_a, d_in1_b]
    gr["ffn1_w_out"] = [d_out1_a, d_out1_b]
    gr["ffn2_w_in"] = [d_in2_a, d_in2_b]
    gr["ffn2_w_out"] = [d_out2_a, d_out2_b]
    return sq, grad_x, gr, gg, gb, dsink_part


def _grad_item(name, layer, g):
    if name.endswith("w_in"):
        return (g, "col", HALF_FF, _slot, name, layer)
    if name.endswith("w_out"):
        return (g, "row", D_MODEL, None, name, layer)
    if name == "a_w_qkv":
        return (g, "col", QKV_SHARD, lambda q: q, name, None)
    return (g, "row", g.shape[1], None, name, None)


class _GradReducer:
    def __init__(self, c_idx, myq, shard_shapes):
        self.c_idx, self.myq, self.shard_shapes = c_idx, myq, shard_shapes
        self.groups = []

    def begin(self, tag, grads, overlap):
        items = [_grad_item(n, l, g) for (n, l), g in grads.items()]
        kinds, widths, colblocks = [it[1] for it in items], [it[2] for it in items], [it[3] for it in items]
        views = [_grad_view(k, it[0]) for k, it in zip(kinds, items)]
        if overlap:
            lands = [jax.ShapeDtypeStruct((N_DIRECT,) + _piece_shape(k, w, _half_shape(k, v.shape)), BF16)
                     for k, w, v in zip(kinds, widths, views)]
            state, token = _split_start("grad_direct_start_" + tag, _direct_copies(kinds, widths, colblocks), 10 * len(items),
                                        views, lands, views[-1])
            self.groups.append((tag, items, None, state))
            return token[0, 0]
        from_sibling = _pair_exchange(views, kinds, "grad_pair_exchange_" + tag)
        sums = [_pair_sum(k, v, r, self.c_idx, "pair_sum_%s_%d" % (tag, t))
                for t, (k, v, r) in enumerate(zip(kinds, views, from_sibling))]
        self.groups.append((tag, items, sums, None))
        return 0.0

    def finish(self, after):
        half_done = {}
        for tag, items, sums, state in self.groups:
            kinds, widths, colblocks = [it[1] for it in items], [it[2] for it in items], [it[3] for it in items]
            direct = state is not None
            if direct:
                sums, received = _split_wait("grad_direct_wait_" + tag, _direct_copies(kinds, widths, colblocks), state, after)
            else:
                received = _chip_exchange(sums, kinds, widths, colblocks, "grad_chip_exchange_" + tag)
            for t, (it, s, r) in enumerate(zip(items, sums, received)):
                _, k, _, cb, name, layer = it
                own = cb(self.myq) if k == "col" else self.myq
                half_done[name] = _chip_sum(k, s, r, own, self.c_idx, self.shard_shapes[name], layer, half_done.get(name),
                                            "chip_sum_%s_%d" % (tag, t), direct=direct)
        return dict(zip(BIG, _share_halves([half_done[name] for name in BIG])))


def _update(reducer, grad_x, loss, grad_ln_g, grad_ln_b, grad_sinks, ws, ms, vs, small_w, small_m, small_v):
    ln_g, ln_b, b_sinks = small_w
    m_ln_g, m_ln_b, m_b_sinks = small_m
    v_ln_g, v_ln_b, v_b_sinks = small_v

    grads = reducer.finish(grad_x)

    deltas, new_m, new_v = {}, {}, {}
    for name in BIG:
        shp = ws[name].shape
        flat = lambda a: a.reshape(-1, shp[-1])
        d, nm, nv = _adamw(flat(ws[name]), flat(grads[name]), flat(ms[name]), flat(vs[name]), "adamw_" + name)
        deltas[name], new_m[name], new_v[name] = d.reshape(shp), nm.reshape(shp), nv.reshape(shp)
    delta_s, nm_s, nv_s = _adamw(_pack_small(ln_g, ln_b, b_sinks), _pack_small(grad_ln_g, grad_ln_b, grad_sinks),
                                 _pack_small(m_ln_g, m_ln_b, m_b_sinks), _pack_small(v_ln_g, v_ln_b, v_b_sinks), "adamw_small")
    for d, blob in ((grads, None), (deltas, delta_s), (new_m, nm_s), (new_v, nv_s)):
        if blob is None:
            d["ln_g"], d["ln_b"], d["b_sinks"] = grad_ln_g, grad_ln_b, grad_sinks
        else:
            d["ln_g"], d["ln_b"], d["b_sinks"] = _unpack_small(blob, ln_g.shape, b_sinks.shape)

    order = ("ffn1_w_in", "ffn1_w_out", "ffn2_w_in", "ffn2_w_out", "ln_g", "ln_b", "a_w_qkv", "a_w_o", "kv_w", "b_w_q",
             "b_sinks", "b_w_o")
    outs = [loss, grad_x[None]]
    for d in (grads, deltas, new_m, new_v):
        outs += [d[n] for n in order]
    return tuple(outs)
```

```python
import numpy as np
import jax
import jax.numpy as jnp
from jax import lax
from jax.experimental import pallas as pl
from jax.experimental.pallas import tpu as pltpu

F32 = jnp.float32
BF16 = jnp.bfloat16

D_MODEL = 1024
D_FF = 2816
HALF_FF = D_FF // 2
HEAD_DIM = 64
N_HEADS = 16
N_KV_B = 4
GROUP_B = N_HEADS // N_KV_B
DEPTH = 2
ALPHA = (2.0 * DEPTH) ** 0.25
LN_EPS = 1e-5
BLOCK = 128
SLAB = 128
N_SLABS = D_MODEL // SLAB
PATTERNS_A = ((1, 128, 1.0), (4, 128, 4.0), (16, 128, 16.0))
PATTERNS_B = ((1, 127, 1.0),)
NEG = -1e30

ADAM_LR = 0.001
ADAM_B1 = 0.9
ADAM_B2 = 0.999
ADAM_EPS = 1e-08
ADAM_WD = 0.01
ADAM_STEP = 10

N_CHIPS = 4
VMEM_LIMIT = 56 * 1024 * 1024
MESH = pl.DeviceIdType.MESH


def _alibi_slopes(n):
    return np.array([2.0 ** (-8.0 * (h + 1) / n) for h in range(n)], dtype=np.float32)


def _cparams(sem=None, vmem=VMEM_LIMIT):
    return pltpu.CompilerParams(dimension_semantics=sem, vmem_limit_bytes=vmem)


_DIMS = {"nn": ((1,), (0,)), "nt": ((1,), (1,)), "tn": ((0,), (0,))}


def _unlead(x):
    if isinstance(x, tuple):
        return x[0], x[1], x[0].shape[1:]
    return x, None, x.shape


def _bspec(block, imap, lead=None):
    if lead is None:
        return pl.BlockSpec(block, imap)
    return pl.BlockSpec((None,) + tuple(block), lambda *g: (lead,) + tuple(imap(*g)))


def _matmul(a, b, mode, out_dtype, tm, tn, tk, name, add=None, add_scale=1.0, split=False, into=None):
    out_spec = pl.BlockSpec((tm, tn), lambda i, j, k: (i, j))
    base, count = (0, 3) if split is True else (split or (0, 0))
    if mode == "nn":
        a, al, (M, K) = _unlead(a)
        b, bl, (K2, N) = _unlead(b)
        a_spec = _bspec((tm, tk), lambda i, j, k: (i, k), al)
        b_spec = _bspec((tk, tn), lambda i, j, k: (k, j), bl)
        out_struct = jax.ShapeDtypeStruct((M, N), out_dtype)
        if split:
            assert tn == D_MODEL and N == count * tn
            out_spec = pl.BlockSpec((None, tm, tn), lambda i, j, k: (j + base, i, 0))
            out_struct = jax.ShapeDtypeStruct((3, M, tn), out_dtype)
    elif mode == "nt":
        b, bl, (N, K2) = _unlead(b)
        if split:
            assert tk == D_MODEL
            M, K = a.shape[1], count * a.shape[2]
            a_spec = pl.BlockSpec((None, tm, tk), lambda i, j, k: (k + base, i, 0))
        else:
            a, al, (M, K) = _unlead(a)
            a_spec = _bspec((tm, tk), lambda i, j, k: (i, k), al)
        b_spec = _bspec((tn, tk), lambda i, j, k: (j, k), bl)
        out_struct = jax.ShapeDtypeStruct((M, N), out_dtype)
    else:
        a, al, (K, M) = _unlead(a)
        if split:
            assert tn == D_MODEL
            K2, N = b.shape[1], count * b.shape[2]
            b_spec = pl.BlockSpec((None, tk, tn), lambda i, j, k: (j + base, k, 0))
        else:
            b, bl, (K2, N) = _unlead(b)
            b_spec = _bspec((tk, tn), lambda i, j, k: (k, j), bl)
        a_spec = _bspec((tk, tm), lambda i, j, k: (k, i), al)
        out_struct = jax.ShapeDtypeStruct((M, N), out_dtype)
    assert K == K2 and M % tm == 0 and N % tn == 0 and K % tk == 0, (a.shape, b.shape, mode, tm, tn, tk)
    nk = K // tk
    dims = (_DIMS[mode], ((), ()))
    has_add = add is not None

    narrow = out_dtype != F32
    assert not (narrow and has_add)

    def body(*refs):
        if into is not None:
            refs = refs[:2] + refs[3:]
        if has_add:
            a_ref, b_ref, add_ref, o_ref = refs
            acc_ref = o_ref
        elif narrow:
            a_ref, b_ref, o_ref, acc_ref = refs
        else:
            a_ref, b_ref, o_ref = refs
            acc_ref = o_ref
        k = pl.program_id(2)
        part = lax.dot_general(a_ref[...].astype(BF16), b_ref[...].astype(BF16), dims, preferred_element_type=F32)
        if has_add:
            @pl.when(k == 0)
            def _():
                acc_ref[...] = part + add_scale * add_ref[...]
        else:
            @pl.when(k == 0)
            def _():
                acc_ref[...] = part

        @pl.when(k > 0)
        def _():
            acc_ref[...] += part

        if narrow:
            @pl.when(k == nk - 1)
            def _():
                o_ref[...] = acc_ref[...].astype(out_dtype)

    in_specs = [a_spec, b_spec]
    args = [a, b]
    aliases = {}
    if into is not None:
        assert mode == "nn" and split and not has_add
        in_specs.append(pl.BlockSpec(memory_space=pl.ANY))
        args.append(into)
        aliases = {2: 0}
    if has_add:
        in_specs.append(pl.BlockSpec((tm, tn), lambda i, j, k: (i, j)))
        args.append(add)
    return pl.pallas_call(
        body, name=name, grid=(M // tm, N // tn, nk),
        in_specs=in_specs, out_specs=out_spec, out_shape=out_struct, input_output_aliases=aliases,
        scratch_shapes=[pltpu.VMEM((tm, tn), F32)] if narrow else [],
        compiler_params=_cparams(("parallel", "parallel", "arbitrary")),
    )(*args)


def _pick(n, cands):
    for c in cands:
        if n % c == 0:
            return c
    raise ValueError((n, cands))


def _mm_nn(a, b, out_dtype, name, split=False, into=None):
    M, K = _unlead(a)[2]
    N = _unlead(b)[2][1]
    return _matmul(a, b, "nn", out_dtype, _pick(M, (1024, 512, 256)), _pick(N, (1024, 512)), _pick(K, (1024, 512)), name,
                   split=split, into=into)


def _mm_nt(a, b, name, add=None, add_scale=1.0, split=False):
    M, K = (a.shape[1], D_MODEL) if split else _unlead(a)[2]
    N = _unlead(b)[2][0]
    return _matmul(a, b, "nt", F32, _pick(M, (1024, 512, 256)), _pick(N, (1024, 512)),
                   _pick(K, (2816, 1024, 512)), name, add=add, add_scale=add_scale, split=split)


def _mm_tn(a, b, name, split=False, out_dtype=F32):
    K, M = _unlead(a)[2]
    N = D_MODEL if split else _unlead(b)[2][1]
    return _matmul(a, b, "tn", out_dtype, _pick(M, (1024, 1408, 512)), _pick(N, (1408, 1024, 512)),
                   _pick(K, (2048, 1024, 512, 256)), name, split=split)


def _ffn_in(x, w, name):
    S = x.shape[0]
    tm = _pick(S, (512, 256))
    w, wl, _ = _unlead(w)

    def body(x_ref, w_ref, t_ref, h_ref):
        acc = jnp.dot(x_ref[...].astype(BF16), w_ref[...], preferred_element_type=F32)
        g = acc[:, :HALF_FF]
        up = acc[:, HALF_FF:]
        sg = jax.nn.sigmoid(g)
        silu = g * sg
        t_ref[:, :HALF_FF] = (up * (sg * (1.0 + g * (1.0 - sg)))).astype(BF16)
        t_ref[:, HALF_FF:] = silu.astype(BF16)
        h_ref[...] = (silu * up).astype(BF16)

    return pl.pallas_call(
        body, name=name, grid=(2, S // tm),
        in_specs=[pl.BlockSpec((tm, D_MODEL), lambda j, i: (i, 0)),
                  _bspec((D_MODEL, D_FF), lambda j, i: (0, j), wl)],
        out_specs=[pl.BlockSpec((tm, D_FF), lambda j, i: (i, j)),
                   pl.BlockSpec((tm, HALF_FF), lambda j, i: (i, j))],
        out_shape=[jax.ShapeDtypeStruct((S, 2 * D_FF), BF16), jax.ShapeDtypeStruct((S, D_FF), BF16)],
        compiler_params=_cparams(("parallel", "parallel")),
    )(x, w)


def _ffn_bwd_h(dzc, w_out, u, name):
    S = dzc.shape[0]
    tm = _pick(S, (512, 256))
    w_out, wl, _ = _unlead(w_out)

    def body(dz_ref, w_ref, t_ref, du_ref):
        dh = lax.dot_general(dz_ref[...], w_ref[...], (((1,), (1,)), ((), ())), preferred_element_type=F32)
        du_ref[:, :HALF_FF] = (dh * t_ref[:, :HALF_FF].astype(F32)).astype(BF16)
        du_ref[:, HALF_FF:] = (dh * t_ref[:, HALF_FF:].astype(F32)).astype(BF16)

    return pl.pallas_call(
        body, name=name, grid=(2, S // tm),
        in_specs=[pl.BlockSpec((tm, D_MODEL), lambda j, i: (i, 0)),
                  _bspec((HALF_FF, D_MODEL), lambda j, i: (j, 0), wl),
                  pl.BlockSpec((tm, D_FF), lambda j, i: (i, j))],
        out_specs=pl.BlockSpec((tm, D_FF), lambda j, i: (i, j)),
        out_shape=jax.ShapeDtypeStruct((S, 2 * D_FF), BF16),
        compiler_params=_cparams(("parallel", "parallel")),
    )(dzc, w_out, u)


def _mm_ln(a, w, resid, gain, bias, c, name):
    S, K = a.shape
    tm = _pick(S, (512, 256))
    w, wl, _ = _unlead(w)

    def body(a_ref, w_ref, r_ref, g_ref, b_ref, y_ref, yb_ref, z_ref):
        z = ALPHA * r_ref[...] + c * jnp.dot(a_ref[...], w_ref[...], preferred_element_type=F32)
        mu = jnp.mean(z, axis=-1, keepdims=True)
        zc = z - mu
        var = jnp.mean(zc * zc, axis=-1, keepdims=True)
        y = zc * lax.rsqrt(var + LN_EPS) * g_ref[...] + b_ref[...]
        z_ref[...] = z
        y_ref[...] = y
        yb_ref[...] = y.astype(BF16)

    row = pl.BlockSpec((tm, D_MODEL), lambda i: (i, 0))
    vec = pl.BlockSpec((1, D_MODEL), lambda i: (0, 0))
    return pl.pallas_call(
        body, name=name, grid=(S // tm,),
        in_specs=[pl.BlockSpec((tm, K), lambda i: (i, 0)), _bspec((K, D_MODEL), lambda i: (0, 0), wl), row, vec, vec],
        out_specs=[row, row, row],
        out_shape=[jax.ShapeDtypeStruct((S, D_MODEL), F32), jax.ShapeDtypeStruct((S, D_MODEL), BF16),
                   jax.ShapeDtypeStruct((S, D_MODEL), F32)],
        compiler_params=_cparams(("parallel",)),
    )(a, w, resid, gain, bias)


def _ln_bwd(z, dy, gain, c, name):
    S = z.shape[0]
    tm = _pick(S, (512, 256))

    def body(z_ref, dy_ref, g_ref, dz_ref, dzc_ref, gg_ref, gb_ref):
        i = pl.program_id(0)
        zv = z_ref[...]
        dyv = dy_ref[...]
        mu = jnp.mean(zv, axis=-1, keepdims=True)
        zc = zv - mu
        var = jnp.mean(zc * zc, axis=-1, keepdims=True)
        rstd = lax.rsqrt(var + LN_EPS)
        xhat = zc * rstd
        dyg = dyv * g_ref[...]
        m1 = jnp.mean(dyg, axis=-1, keepdims=True)
        m2 = jnp.mean(dyg * xhat, axis=-1, keepdims=True)
        dz = rstd * (dyg - m1 - xhat * m2)
        dz_ref[...] = dz
        dzc_ref[...] = (c * dz).astype(BF16)
        pg = jnp.sum((dyv * xhat).reshape(tm // 8, 8, D_MODEL), axis=0)
        pb = jnp.sum(dyv.reshape(tm // 8, 8, D_MODEL), axis=0)

        @pl.when(i == 0)
        def _():
            gg_ref[...] = pg
            gb_ref[...] = pb

        @pl.when(i > 0)
        def _():
            gg_ref[...] += pg
            gb_ref[...] += pb

    row = pl.BlockSpec((tm, D_MODEL), lambda i: (i, 0))
    part = pl.BlockSpec((8, D_MODEL), lambda i: (0, 0))
    return pl.pallas_call(
        body, name=name, grid=(S // tm,),
        in_specs=[row, row, pl.BlockSpec((1, D_MODEL), lambda i: (0, 0))],
        out_specs=[row, row, part, part],
        out_shape=[jax.ShapeDtypeStruct((S, D_MODEL), F32), jax.ShapeDtypeStruct((S, D_MODEL), BF16),
                   jax.ShapeDtypeStruct((8, D_MODEL), F32), jax.ShapeDtypeStruct((8, D_MODEL), F32)],
        compiler_params=_cparams(("arbitrary",)),
    )(z, dy, gain)


def _loss_grad(y, t, name):
    S = y.shape[0]
    tm = _pick(S, (512, 256))

    def body(y_ref, t_ref, dy_ref, sq_ref):
        i = pl.program_id(0)
        e = y_ref[...] - t_ref[...]
        dy_ref[...] = e * (1.0 / D_MODEL)
        ps = jnp.sum((e * e).reshape(tm // 8, 8, D_MODEL), axis=0)

        @pl.when(i == 0)
        def _():
            sq_ref[...] = ps

        @pl.when(i > 0)
        def _():
            sq_ref[...] += ps

    row = pl.BlockSpec((tm, D_MODEL), lambda i: (i, 0))
    return pl.pallas_call(
        body, name=name, grid=(S // tm,),
        in_specs=[row, row], out_specs=[row, pl.BlockSpec((8, D_MODEL), lambda i: (0, 0))],
        out_shape=[jax.ShapeDtypeStruct((S, D_MODEL), F32), jax.ShapeDtypeStruct((8, D_MODEL), F32)],
        compiler_params=_cparams(("arbitrary",)),
    )(y, t)


def _rows(start, d):
    if d == 1:
        return pl.ds(pl.multiple_of(start, BLOCK), BLOCK)
    return pl.ds(start, BLOCK, stride=d)


def _ld(ref, start, d):
    return ref[_rows(start, d), :]


def _ld3(ref, lead, start, d):
    return ref[lead, _rows(start, d), :]


def _st3(ref, lead, start, d, val):
    ref[lead, _rows(start, d), :] = val


def _acc3(ref, lead, start, d, val):
    ref[lead, _rows(start, d), :] = ref[lead, _rows(start, d), :] + val


def _band_consts(slope0, slope1, maxd, scale):
    row = lax.broadcasted_iota(jnp.int32, (2 * BLOCK, 2 * BLOCK), 0)
    kj = lax.broadcasted_iota(jnp.int32, (2 * BLOCK, 2 * BLOCK), 1)
    top = row < BLOCK
    dist = BLOCK + jnp.where(top, row, row - BLOCK) - kj
    slope = jnp.where(top, slope0, slope1)
    base = jnp.where((dist >= 0) & (dist <= maxd), -(slope * (dist.astype(F32) * scale)), NEG)
    return base, kj < BLOCK


def _stack_heads(x, lo):
    return jnp.concatenate([jnp.where(lo, x, 0.0), jnp.where(lo, 0.0, x)], axis=0)


def _unstack_heads(x2, lo):
    return jnp.where(lo, x2[:BLOCK], x2[BLOCK:])


def _scores(q2, k2, base, prev_keys, first):
    s = lax.dot_general(q2, k2, (((1,), (1,)), ((), ())), preferred_element_type=F32) * (HEAD_DIM ** -0.5) + base
    return jnp.where(jnp.logical_and(prev_keys, first), NEG, s)


def _softmax_weights(ls):
    mx = ls[0]
    for l in ls[1:]:
        mx = jnp.maximum(mx, l)
    es = [jnp.exp(l - mx) for l in ls]
    tot = es[0]
    for e in es[1:]:
        tot = tot + e
    inv = 1.0 / tot
    return [e * inv for e in es]


def _attn_fwd(qkv, slopes, sinks, patterns, name):
    S = qkv.shape[1]
    npat = len(patterns)
    has_sink = sinks is not None
    if not has_sink:
        sinks = jnp.zeros((N_HEADS,), F32)
    rows_c = 256

    def body(slopes_ref, sinks_ref, x_ref, mix_ref, o_ref, lse_ref, o_scr, lse_scr):
        p = pl.program_id(0)
        lo = lax.broadcasted_iota(jnp.int32, (BLOCK, SLAB), 1) < HEAD_DIM
        top1 = lax.broadcasted_iota(jnp.int32, (2 * BLOCK, 1), 0) < BLOCK
        sk2 = jnp.where(top1, sinks_ref[2 * p], sinks_ref[2 * p + 1])
        for pi, (d, maxd, scale) in enumerate(patterns):
            nb = S // d // BLOCK
            base, prev_keys = _band_consts(slopes_ref[2 * p], slopes_ref[2 * p + 1], maxd, scale)

            def blk(t, carry, pi=pi, d=d, nb=nb, base=base, prev_keys=prev_keys):
                r = t // nb
                n = t - r * nb
                start = r + (d * BLOCK) * n
                prev = jnp.where(n > 0, start - d * BLOCK, start)
                q2 = _stack_heads(_ld3(x_ref, 0, start, d), lo).astype(BF16)
                k2 = jnp.concatenate([_ld3(x_ref, 1, prev, d), _ld3(x_ref, 1, start, d)], axis=0).astype(BF16)
                v2 = jnp.concatenate([_ld3(x_ref, 2, prev, d), _ld3(x_ref, 2, start, d)], axis=0).astype(BF16)
                s = _scores(q2, k2, base, prev_keys, n == 0)
                m = jnp.max(s, axis=-1, keepdims=True)
                if has_sink:
                    m = jnp.maximum(m, sk2)
                e = jnp.exp(s - m)
                den = jnp.sum(e, axis=-1, keepdims=True)
                if has_sink:
                    den = den + jnp.exp(sk2 - m)
                o2 = jnp.dot((e / den).astype(BF16), v2, preferred_element_type=F32)
                _st3(o_scr, pi, start, d, _unstack_heads(o2, lo))
                _st3(lse_scr, pi, start, d, _unstack_heads(m + jnp.log(den), lo))
                return carry

            lax.fori_loop(0, d * nb, blk, 0, unroll=8)

        lane_c = lax.broadcasted_iota(jnp.int32, (rows_c, SLAB), 1)

        def comb(ci, carry):
            rows = pl.ds(pl.multiple_of(ci * rows_c, rows_c), rows_c)
            ls = [lse_scr[i, rows, :] for i in range(npat)]
            packed = jnp.zeros((rows_c, SLAB), F32)
            for i in range(npat):
                o_ref[i, rows, :] = o_scr[i, rows, :].astype(BF16)
                packed = jnp.where(lane_c == 2 * i, ls[i][:, :1], packed)
                packed = jnp.where(lane_c == 2 * i + 1, ls[i][:, HEAD_DIM:HEAD_DIM + 1], packed)
            lse_ref[rows, :] = packed
            if npat == 1:
                mix_ref[rows, :] = o_scr[0, rows, :].astype(BF16)
            else:
                ws = _softmax_weights(ls)
                acc = ws[0] * o_scr[0, rows, :]
                for i in range(1, npat):
                    acc = acc + ws[i] * o_scr[i, rows, :]
                mix_ref[rows, :] = acc.astype(BF16)
            return carry

        lax.fori_loop(0, S // rows_c, comb, 0)

    smem = pl.BlockSpec(memory_space=pltpu.SMEM)
    return pl.pallas_call(
        body, name=name, grid=(N_SLABS,),
        in_specs=[smem, smem, pl.BlockSpec((3, S, SLAB), lambda p: (0, 0, p))],
        out_specs=[pl.BlockSpec((S, SLAB), lambda p: (0, p)), pl.BlockSpec((npat, S, SLAB), lambda p: (0, 0, p)),
                   pl.BlockSpec((None, S, SLAB), lambda p: (p, 0, 0))],
        out_shape=[jax.ShapeDtypeStruct((S, D_MODEL), BF16), jax.ShapeDtypeStruct((npat, S, D_MODEL), BF16),
                   jax.ShapeDtypeStruct((N_SLABS, S, SLAB), F32)],
        scratch_shapes=[pltpu.VMEM((npat, S, SLAB), F32), pltpu.VMEM((npat, S, SLAB), F32)],
        compiler_params=_cparams(("arbitrary",)),
    )(slopes, sinks, qkv)


def _attn_bwd(qkv, dout, o, lse, slopes, sinks, patterns, name):
    S = qkv.shape[1]
    npat = len(patterns)
    has_sink = sinks is not None
    if not has_sink:
        sinks = jnp.zeros((N_HEADS,), F32)
    rows_c = 256

    def headsum(x, lo):
        s0 = jnp.sum(jnp.where(lo, x, 0.0), axis=-1, keepdims=True)
        s1 = jnp.sum(jnp.where(lo, 0.0, x), axis=-1, keepdims=True)
        return jnp.where(lo, s0, s1)

    def body(slopes_ref, sinks_ref, x_ref, do_ref, o_ref, lsep_ref, dxo_ref, dsink_ref, dbar_ref, sacc_ref, lse_ref, dx_ref):
        p = pl.program_id(0)
        lo = lax.broadcasted_iota(jnp.int32, (BLOCK, SLAB), 1) < HEAD_DIM
        lo_c = lax.broadcasted_iota(jnp.int32, (rows_c, SLAB), 1) < HEAD_DIM
        top1 = lax.broadcasted_iota(jnp.int32, (2 * BLOCK, 1), 0) < BLOCK
        sk2 = jnp.where(top1, sinks_ref[2 * p], sinks_ref[2 * p + 1])

        def prep(ci, carry):
            rows = pl.ds(pl.multiple_of(ci * rows_c, rows_c), rows_c)
            dov = do_ref[rows, :]
            dx_ref[:, rows, :] = jnp.zeros((3, rows_c, SLAB), F32)
            packed = lsep_ref[rows, :]
            ls = [jnp.where(lo_c, packed[:, 2 * i:2 * i + 1], packed[:, 2 * i + 1:2 * i + 2]) for i in range(npat)]
            for i in range(npat):
                lse_ref[i, rows, :] = ls[i]
            if npat == 1:
                dbar_ref[rows, :] = headsum(dov * o_ref[0, rows, :].astype(F32), lo_c)
            else:
                ws = _softmax_weights(ls)
                acc = ws[0] * headsum(dov * o_ref[0, rows, :].astype(F32), lo_c)
                for i in range(1, npat):
                    acc = acc + ws[i] * headsum(dov * o_ref[i, rows, :].astype(F32), lo_c)
                dbar_ref[rows, :] = acc
            return carry

        lax.fori_loop(0, S // rows_c, prep, 0)
        sacc_ref[...] = jnp.zeros((BLOCK, SLAB), F32)

        for pi, (d, maxd, scale) in enumerate(patterns):
            nb = S // d // BLOCK
            base, prev_keys = _band_consts(slopes_ref[2 * p], slopes_ref[2 * p + 1], maxd, scale)

            def blk(t, carry, pi=pi, d=d, nb=nb, base=base, prev_keys=prev_keys):
                r = t // nb
                n = t - r * nb
                start = r + (d * BLOCK) * n
                prev = jnp.where(n > 0, start - d * BLOCK, start)
                q2 = _stack_heads(_ld3(x_ref, 0, start, d), lo).astype(BF16)
                k2 = jnp.concatenate([_ld3(x_ref, 1, prev, d), _ld3(x_ref, 1, start, d)], axis=0).astype(BF16)
                v2 = jnp.concatenate([_ld3(x_ref, 2, prev, d), _ld3(x_ref, 2, start, d)], axis=0).astype(BF16)
                ls = [_ld3(lse_ref, i, start, d) for i in range(npat)]
                w = _softmax_weights(ls)[pi] if npat > 1 else 1.0
                do2 = _stack_heads(w * _ld(do_ref, start, d), lo).astype(BF16)
                dl = w * _ld(dbar_ref, start, d)
                lse2 = jnp.concatenate([ls[pi][:, :1], ls[pi][:, HEAD_DIM:HEAD_DIM + 1]], axis=0)
                dl2 = jnp.concatenate([dl[:, :1], dl[:, HEAD_DIM:HEAD_DIM + 1]], axis=0)
                s = _scores(q2, k2, base, prev_keys, n == 0)
                pr = jnp.exp(s - lse2)
                dp = lax.dot_general(do2, v2, (((1,), (1,)), ((), ())), preferred_element_type=F32)
                ds = (pr * (dp - dl2) * (HEAD_DIM ** -0.5)).astype(BF16)
                dq2 = jnp.dot(ds, k2, preferred_element_type=F32)
                dk2 = lax.dot_general(ds, q2, (((0,), (0,)), ((), ())), preferred_element_type=F32)
                dv2 = lax.dot_general(pr.astype(BF16), do2, (((0,), (0,)), ((), ())), preferred_element_type=F32)
                _acc3(dx_ref, 0, start, d, _unstack_heads(dq2, lo))
                _acc3(dx_ref, 1, prev, d, dk2[:BLOCK])
                _acc3(dx_ref, 1, start, d, dk2[BLOCK:])
                _acc3(dx_ref, 2, prev, d, dv2[:BLOCK])
                _acc3(dx_ref, 2, start, d, dv2[BLOCK:])
                if has_sink:
                    sacc_ref[...] += _unstack_heads(-jnp.exp(sk2 - lse2) * dl2, lo)
                return carry

            lax.fori_loop(0, d * nb, blk, 0, unroll=4)

        dsink_ref[...] = jnp.broadcast_to(jnp.sum(sacc_ref[...], axis=0, keepdims=True), (8, SLAB))

        def emit(ci, carry):
            rows = pl.ds(pl.multiple_of(ci * rows_c, rows_c), rows_c)
            dxo_ref[:, rows, :] = dx_ref[:, rows, :].astype(BF16)
            return carry

        lax.fori_loop(0, S // rows_c, emit, 0)

    smem = pl.BlockSpec(memory_space=pltpu.SMEM)
    return pl.pallas_call(
        body, name=name, grid=(N_SLABS,),
        in_specs=[smem, smem, pl.BlockSpec((3, S, SLAB), lambda p: (0, 0, p)), pl.BlockSpec((S, SLAB), lambda p: (0, p)),
                  pl.BlockSpec((npat, S, SLAB), lambda p: (0, 0, p)), pl.BlockSpec((None, S, SLAB), lambda p: (p, 0, 0))],
        out_specs=[pl.BlockSpec((3, S, SLAB), lambda p: (0, 0, p)), pl.BlockSpec((None, 8, SLAB), lambda p: (p, 0, 0))],
        out_shape=[jax.ShapeDtypeStruct((3, S, D_MODEL), BF16), jax.ShapeDtypeStruct((N_SLABS, 8, SLAB), F32)],
        scratch_shapes=[pltpu.VMEM((S, SLAB), F32), pltpu.VMEM((BLOCK, SLAB), F32), pltpu.VMEM((npat, S, SLAB), F32),
                        pltpu.VMEM((3, S, SLAB), F32)],
        compiler_params=_cparams(("arbitrary",)),
    )(slopes, sinks, qkv, dout, o, lse)


def _place():
    x, y, c = lax.axis_index("x"), lax.axis_index("y"), lax.axis_index("c")
    return x, y, c, 2 * x + y


def _other_chips(x, y):
    return [(1 - x, y), (x, 1 - y), (1 - x, 1 - y)]


HBM_SPEC = pl.BlockSpec(memory_space=pl.ANY)


def _slot(q):
    return 2 * (q % 2) + q // 2


BIG = ("ffn1_w_in", "ffn1_w_out", "ffn2_w_in", "ffn2_w_out", "a_w_qkv", "a_w_o", "kv_w", "b_w_q", "b_w_o")
QKV_SHARD = 3 * D_MODEL // N_CHIPS
ROW_SHARD = D_MODEL // N_CHIPS


LAYER0_ITEMS = (("ffn1_w_in", 0), ("ffn1_w_out", 0), ("a_w_qkv", None), ("a_w_o", None), ("ffn2_w_in", 0),
                ("ffn2_w_out", 0), ("kv_w", None))
LAYER1_ITEMS = (("ffn1_w_in", 1), ("ffn1_w_out", 1), ("b_w_q", None), ("b_w_o", None), ("ffn2_w_in", 1),
                ("ffn2_w_out", 1))
OUT_SHARD = D_FF // N_CHIPS


def _full_shape(name):
    if name.endswith("w_in"):
        return (D_MODEL, 2 * D_FF)
    if name.endswith("w_out"):
        return (D_FF, D_MODEL)
    if name == "a_w_qkv":
        return (D_MODEL, 3 * D_MODEL)
    if name == "kv_w":
        return (N_CHIPS, 2, ROW_SHARD // 2, 2 * N_KV_B * HEAD_DIM)
    return (N_CHIPS, 2, ROW_SHARD // 2, D_MODEL)


def _gather_src(item, ref, c):
    name, _ = item
    if name.endswith("w_in"):
        return ref.at[pl.ds(c * (D_MODEL // 2), D_MODEL // 2)]
    if name.endswith("w_out"):
        return ref.at[pl.ds(c * (OUT_SHARD // 2), OUT_SHARD // 2)]
    if name == "a_w_qkv":
        return ref.at[0, pl.ds(c * (D_MODEL // 2), D_MODEL // 2)]
    if name == "kv_w":
        return ref.at[pl.ds(c * (ROW_SHARD // 2), ROW_SHARD // 2)]
    return ref.at[0, pl.ds(c * (ROW_SHARD // 2), ROW_SHARD // 2)]


def _gather_dst(item, ref, q, c):
    name, _ = item
    if name.endswith("w_in"):
        return ref.at[pl.ds(c * (D_MODEL // 2), D_MODEL // 2), pl.ds(_slot(q) * HALF_FF, HALF_FF)]
    if name.endswith("w_out"):
        return ref.at[pl.ds(q * OUT_SHARD + c * (OUT_SHARD // 2), OUT_SHARD // 2)]
    if name == "a_w_qkv":
        return ref.at[pl.ds(c * (D_MODEL // 2), D_MODEL // 2), pl.ds(q * QKV_SHARD, QKV_SHARD)]
    return ref.at[q, c]


def _all_gather(items, shards, small):
    n = len(items)
    r = small.shape[0]
    per = 8

    def body(*refs):
        srcs, small_ref = refs[:n], refs[n]
        dsts, s_ref = refs[n + 1:2 * n + 1], refs[2 * n + 1]
        send_sems, recv_sems = refs[2 * n + 2:]
        x, y, c, myq = _place()
        sibling = (x, y, 1 - c)
        chips = _other_chips(x, y)

        def big(t, k, src, q, h, to):
            return pltpu.make_async_remote_copy(src_ref=src, dst_ref=_gather_dst(items[t], dsts[t], q, h),
                                                send_sem=send_sems.at[per * t + k], recv_sem=recv_sems.at[per * t + k],
                                                device_id=to, device_id_type=MESH)

        def tiny(k, q, to):
            return pltpu.make_async_remote_copy(src_ref=small_ref, dst_ref=s_ref.at[q], send_sem=send_sems.at[per * n + k],
                                                recv_sem=recv_sems.at[per * n + k], device_id=to, device_id_type=MESH)

        first = []
        for j, chip in enumerate(chips):
            first += [big(t, j, _gather_src(items[t], srcs[t], c), myq, c, (*chip, c)) for t in range(n)]
            first.append(tiny(j, myq, (*chip, c)))
        own = [big(t, 6 + h, _gather_src(items[t], srcs[t], h), myq, h, sibling) for t in range(n) for h in (0, 1)]
        own.append(tiny(3, myq, sibling))
        for cp in first + own:
            cp.start()
        passed = []
        for j, (cx, cy) in enumerate(chips):
            q = 2 * cx + cy
            for t in range(n):
                src = _gather_src(items[t], srcs[t], c)
                big(t, j, src, q, c, sibling).wait_recv()
                fwd = big(t, 3 + j, _gather_dst(items[t], dsts[t], q, c), q, c, sibling)
                fwd.start()
                passed.append(fwd)
        for j, (cx, cy) in enumerate(chips):
            q = 2 * cx + cy
            for t in range(n):
                big(t, 3 + j, _gather_src(items[t], srcs[t], c), q, 1 - c, sibling).wait_recv()
            tiny(j, q, sibling).wait_recv()
        for cp in own:
            cp.wait_recv()
        for cp in first + passed + own:
            cp.wait_send()

    outs = pl.pallas_call(
        body, name="all_gather_layer0",
        in_specs=[HBM_SPEC] * (n + 1), out_specs=[HBM_SPEC] * (n + 1),
        out_shape=[jax.ShapeDtypeStruct(_full_shape(name), BF16) for name, _ in items]
        + [jax.ShapeDtypeStruct((N_CHIPS, r, 128), F32)],
        scratch_shapes=[pltpu.SemaphoreType.DMA((per * n + 4,)), pltpu.SemaphoreType.DMA((per * n + 4,))],
    )(*[shards[item] for item in items], small)
    return list(outs[:n]), outs[n]


SEM_SPEC = pl.BlockSpec(memory_space=pltpu.SEMAPHORE)
DATAFLOW = pltpu.SideEffectType.DATAFLOW_SIDE_EFFECTING
PER_ITEM = 8


def _split_start(name, copies, n_sems, sources, land_shapes, after):
    n, m = len(sources), len(land_shapes)

    def body(*refs):
        srcs, lands = refs[:n], refs[n:n + m]
        send_sems, recv_sems = refs[n + m + 1], refs[n + m + 2]
        token = refs[-1]
        for src, dst_there, _, s, peer in copies(srcs, lands):
            pltpu.make_async_remote_copy(src_ref=src, dst_ref=dst_there, send_sem=send_sems.at[s], recv_sem=recv_sems.at[s],
                                         device_id=peer, device_id_type=MESH).start()
        token[...] = jnp.zeros_like(token)

    src_arrays = [pltpu.with_memory_space_constraint(a, pltpu.HBM) for a in sources]
    land_arrays = [pltpu.with_memory_space_constraint(lax.empty(s.shape, s.dtype), pltpu.HBM) for s in land_shapes]
    hbm = pl.BlockSpec(memory_space=pltpu.HBM)
    outs = pl.pallas_call(
        body, name=name,
        in_specs=[hbm] * (n + m) + [HBM_SPEC],
        out_specs=[SEM_SPEC, SEM_SPEC] + [hbm] * (n + m) + [pl.BlockSpec(memory_space=pltpu.VMEM)],
        out_shape=[pltpu.SemaphoreType.DMA((n_sems,)), pltpu.SemaphoreType.DMA((n_sems,))]
        + [pltpu.HBM(a.shape, a.dtype) for a in src_arrays + land_arrays] + [jax.ShapeDtypeStruct((8, 128), F32)],
        input_output_aliases={i: 2 + i for i in range(n + m)},
        compiler_params=pltpu.CompilerParams(has_side_effects=DATAFLOW),
    )(*src_arrays, *land_arrays, after)
    return (outs[0], outs[1], list(outs[2:2 + n]), list(outs[2 + n:2 + n + m])), outs[-1]


def _split_wait(name, copies, state, after):
    send_sems, recv_sems, srcs_thru, lands_thru = state
    n, m = len(srcs_thru), len(lands_thru)

    def body(*refs):
        srcs, lands = refs[:n], refs[n:n + m]
        send_sems, recv_sems = refs[n + m], refs[n + m + 1]
        for src, _, dst_here, s, peer in copies(srcs, lands):
            cp = pltpu.make_async_remote_copy(src_ref=src, dst_ref=dst_here, send_sem=send_sems.at[s], recv_sem=recv_sems.at[s],
                                              device_id=peer, device_id_type=MESH)
            cp.wait_send()
            cp.wait_recv()

    hbm = pl.BlockSpec(memory_space=pltpu.HBM)
    outs = pl.pallas_call(
        body, name=name,
        in_specs=[hbm] * (n + m) + [SEM_SPEC, SEM_SPEC, HBM_SPEC],
        out_specs=[hbm] * (n + m),
        out_shape=[pltpu.HBM(a.shape, a.dtype) for a in srcs_thru + lands_thru],
        input_output_aliases={i: i for i in range(n + m)},
        compiler_params=pltpu.CompilerParams(has_side_effects=DATAFLOW),
    )(*srcs_thru, *lands_thru, send_sems, recv_sems, after)
    return list(outs[:n]), list(outs[n:])


def _gather_copies(items):
    def copies(srcs, lands):
        x, y, c, myq = _place()
        out = []
        for t, item in enumerate(items):
            for h in (0, 1):
                src = _gather_src(item, srcs[t], h)
                for j, (cx, cy) in enumerate(_other_chips(x, y)):
                    out.append((src, _gather_dst(item, lands[t], myq, h), _gather_dst(item, lands[t], 2 * cx + cy, h),
                                PER_ITEM * t + 2 * j + h, (cx, cy, c)))
                out.append((src, _gather_dst(item, lands[t], myq, h), _gather_dst(item, lands[t], myq, h),
                            PER_ITEM * t + 6 + h, (x, y, 1 - c)))
        return out
    return copies


def _gather_start(items, shards, after):
    lands = [jax.ShapeDtypeStruct(_full_shape(name), BF16) for name, _ in items]
    return _split_start("gather_layer1_start", _gather_copies(items), PER_ITEM * len(items),
                        [shards[item] for item in items], lands, after)


def _gather_wait(items, state, after):
    return _split_wait("gather_layer1_wait", _gather_copies(items), state, after)[1]


def _small_all_reduce(v):
    r = v.shape[0]

    def body(v_ref, o_ref, buf_ref, send_sems, recv_sems):
        x, y, c, _ = _place()
        me = 4 * x + 2 * y + c
        buf_ref[me] = v_ref[...]
        copies = []
        for k in range(1, 8):
            fx, fy, fc = (k >> 2) & 1, (k >> 1) & 1, k & 1
            to = (x ^ fx, y ^ fy, c ^ fc)
            cp = pltpu.make_async_remote_copy(src_ref=v_ref, dst_ref=buf_ref.at[me], send_sem=send_sems.at[k - 1],
                                              recv_sem=recv_sems.at[k - 1], device_id=to, device_id_type=MESH)
            cp.start()
            copies.append(cp)
        for k in range(1, 8):
            fx, fy, fc = (k >> 2) & 1, (k >> 1) & 1, k & 1
            src_dev = 4 * (x ^ fx) + 2 * (y ^ fy) + (c ^ fc)
            pltpu.make_async_remote_copy(src_ref=v_ref, dst_ref=buf_ref.at[src_dev], send_sem=send_sems.at[k - 1],
                                         recv_sem=recv_sems.at[k - 1], device_id=(x, y, c), device_id_type=MESH).wait_recv()
        for cp in copies:
            cp.wait_send()
        tot = buf_ref[0]
        for i in range(1, 8):
            tot = tot + buf_ref[i]
        o_ref[...] = tot

    vm = pl.BlockSpec(memory_space=pltpu.VMEM)
    return pl.pallas_call(
        body, name="small_all_reduce", in_specs=[vm], out_specs=vm,
        out_shape=jax.ShapeDtypeStruct((r, 128), F32),
        scratch_shapes=[pltpu.VMEM((8, r, 128), F32), pltpu.SemaphoreType.DMA((7,)), pltpu.SemaphoreType.DMA((7,))],
    )(v)


def _grad_view(kind, g):
    if kind == "col":
        return g.reshape(2, g.shape[0] // 2, g.shape[1])
    return g.reshape(N_CHIPS, 2, g.shape[0] // (2 * N_CHIPS), g.shape[1])


def _half_of(kind, ref, h):
    return ref.at[h] if kind == "col" else ref.at[:, h]


def _half_shape(kind, view_shape):
    return view_shape[1:] if kind == "col" else (view_shape[0],) + view_shape[2:]


def _piece_of(kind, width, colblock, ref, q):
    if kind == "col":
        return ref.at[:, pl.ds(colblock(q) * width, width)]
    return ref.at[q]


def _piece_shape(kind, width, half_shape):
    return (half_shape[0], width) if kind == "col" else half_shape[1:]


def _pair_exchange(views, kinds, name):
    n = len(views)

    def body(*refs):
        ins, outs = refs[:n], refs[n:2 * n]
        send_sems, recv_sems = refs[2 * n:]
        x, y, c, _ = _place()
        cps = []
        for t in range(n):
            cp = pltpu.make_async_remote_copy(src_ref=_half_of(kinds[t], ins[t], 1 - c), dst_ref=outs[t],
                                              send_sem=send_sems.at[t], recv_sem=recv_sems.at[t],
                                              device_id=(x, y, 1 - c), device_id_type=MESH)
            cp.start()
            cps.append(cp)
        for cp in cps:
            cp.wait()

    return pl.pallas_call(
        body, name=name, in_specs=[HBM_SPEC] * n, out_specs=[HBM_SPEC] * n,
        out_shape=[jax.ShapeDtypeStruct(_half_shape(k, v.shape), v.dtype) for k, v in zip(kinds, views)],
        scratch_shapes=[pltpu.SemaphoreType.DMA((n,)), pltpu.SemaphoreType.DMA((n,))],
    )(*views)


def _pair_sum(kind, view, recv, c, name):
    hs = recv.shape
    N = hs[-1]
    rows = hs[-2]
    tr = _pick(rows, (512, 352, 128))
    tn = _pick(N, (1408, 1024, 512))

    def body(c_ref, p_ref, r_ref, s_ref):
        s_ref[...] = (p_ref[...] + r_ref[...]).astype(BF16)

    if kind == "col":
        grid = (rows // tr, N // tn)
        mine = pl.BlockSpec((None, tr, tn), lambda i, j, c_ref: (c_ref[0], i, j))
        blk = pl.BlockSpec((tr, tn), lambda i, j, c_ref: (i, j))
        sem = ("parallel", "parallel")
    else:
        grid = (N_CHIPS, rows // tr, N // tn)
        mine = pl.BlockSpec((None, None, tr, tn), lambda q, i, j, c_ref: (q, c_ref[0], i, j))
        blk = pl.BlockSpec((None, tr, tn), lambda q, i, j, c_ref: (q, i, j))
        sem = ("parallel", "parallel", "parallel")
    return pl.pallas_call(
        body, name=name,
        grid_spec=pltpu.PrefetchScalarGridSpec(num_scalar_prefetch=1, grid=grid, in_specs=[mine, blk], out_specs=blk),
        out_shape=jax.ShapeDtypeStruct(hs, BF16),
        compiler_params=_cparams(sem),
    )(c.reshape(1).astype(jnp.int32), view, recv)


def _chip_copies(kinds, widths, colblocks):
    def copies(srcs, lands):
        x, y, c, _ = _place()
        out = []
        for j, (cx, cy) in enumerate(_other_chips(x, y)):
            for t in range(len(kinds)):
                out.append((_piece_of(kinds[t], widths[t], colblocks[t], srcs[t], 2 * cx + cy), lands[t].at[j],
                            lands[t].at[j], 3 * t + j, (cx, cy, c)))
        return out
    return copies


def _chip_land_shapes(sums, kinds, widths):
    return [jax.ShapeDtypeStruct((3,) + _piece_shape(k, w, s.shape), BF16) for k, w, s in zip(kinds, widths, sums)]


def _chip_exchange(sums, kinds, widths, colblocks, name):
    n = len(sums)
    copies = _chip_copies(kinds, widths, colblocks)

    def body(*refs):
        send_sems, recv_sems = refs[2 * n:]
        cps = [pltpu.make_async_remote_copy(src_ref=src, dst_ref=dst, send_sem=send_sems.at[s], recv_sem=recv_sems.at[s],
                                            device_id=peer, device_id_type=MESH)
               for src, dst, _, s, peer in copies(refs[:n], refs[n:2 * n])]
        for cp in cps:
            cp.start()
        for cp in cps:
            cp.wait()

    return pl.pallas_call(
        body, name=name, in_specs=[HBM_SPEC] * n, out_specs=[HBM_SPEC] * n,
        out_shape=_chip_land_shapes(sums, kinds, widths),
        scratch_shapes=[pltpu.SemaphoreType.DMA((3 * n,)), pltpu.SemaphoreType.DMA((3 * n,))],
    )(*sums)


N_DIRECT = 7


def _direct_piece(kind, width, colblock, view_ref, q, h):
    if kind == "col":
        return view_ref.at[h, :, pl.ds(colblock(q) * width, width)]
    return view_ref.at[q, h]


def _direct_copies(kinds, widths, colblocks):
    def copies(srcs, lands):
        x, y, c, myq = _place()
        out = []
        for t in range(len(kinds)):
            def piece(q, h, t=t):
                return _direct_piece(kinds[t], widths[t], colblocks[t], srcs[t], q, h)
            for j, (cx, cy) in enumerate(_other_chips(x, y)):
                for h in (0, 1):
                    out.append((piece(2 * cx + cy, h), lands[t].at[2 * j + c], lands[t].at[2 * j + h],
                                10 * t + 3 * j + c + h, (cx, cy, h)))
            out.append((piece(myq, 1 - c), lands[t].at[6], lands[t].at[6], 10 * t + 9, (x, y, 1 - c)))
        return out
    return copies


def _chip_sum(kind, own_src, recv, block_idx, c, shard_shape, layer, into, name, direct=False):
    n_recv, rows, N = recv.shape
    tr = _pick(rows, (512, 352, 128))
    tn = _pick(N, (1408, 1024, 768, 512))
    ni, nj = rows // tr, N // tn

    def body(q_ref, s_ref, r_ref, *rest):
        o_ref = rest[-1]
        tot = s_ref[...].astype(F32)
        for k in range(n_recv):
            tot = tot + r_ref[k].astype(F32)
        o_ref[...] = tot

    if direct and kind == "col":
        own = pl.BlockSpec((None, tr, tn), lambda i, j, q_ref: (q_ref[1], i, q_ref[0] * nj + j))
    elif direct:
        own = pl.BlockSpec((None, None, tr, tn), lambda i, j, q_ref: (q_ref[0], q_ref[1], i, j))
    elif kind == "col":
        own = pl.BlockSpec((tr, tn), lambda i, j, q_ref: (i, q_ref[0] * nj + j))
    else:
        own = pl.BlockSpec((None, tr, tn), lambda i, j, q_ref: (q_ref[0], i, j))
    if len(shard_shape) == 3:
        lead = 0 if layer is None else layer
        out_spec = pl.BlockSpec((None, tr, tn), lambda i, j, q_ref: (lead, q_ref[1] * ni + i, j))
    else:
        out_spec = pl.BlockSpec((tr, tn), lambda i, j, q_ref: (q_ref[1] * ni + i, j))
    in_specs = [own, pl.BlockSpec((n_recv, tr, tn), lambda i, j, q_ref: (0, i, j))]
    s = own_src
    args = [jnp.stack([block_idx, c]).astype(jnp.int32), s, recv]
    aliases = {}
    if into is not None:
        in_specs.append(HBM_SPEC)
        args.append(into)
        aliases = {3: 0}
    return pl.pallas_call(
        body, name=name,
        grid_spec=pltpu.PrefetchScalarGridSpec(num_scalar_prefetch=1, grid=(ni, nj), in_specs=in_specs, out_specs=out_spec),
        out_shape=jax.ShapeDtypeStruct(shard_shape, F32), input_output_aliases=aliases,
        compiler_params=_cparams(("parallel", "parallel")),
    )(*args)


def _half_window(ref, h):
    rows = ref.shape[-2] // 2
    if ref.ndim == 3:
        return ref.at[:, pl.ds(h * rows, rows)]
    return ref.at[pl.ds(h * rows, rows)]


def _share_halves(grads):
    n = len(grads)

    def body(*refs):
        outs = refs[n:2 * n]
        send_sems, recv_sems = refs[2 * n:]
        x, y, c, _ = _place()
        cps = []
        for t in range(n):
            cp = pltpu.make_async_remote_copy(src_ref=_half_window(outs[t], c), dst_ref=_half_window(outs[t], c),
                                              send_sem=send_sems.at[t], recv_sem=recv_sems.at[t],
                                              device_id=(x, y, 1 - c), device_id_type=MESH)
            cp.start()
            cps.append(cp)
        for t in range(n):
            cps[t].wait_send()
            pltpu.make_async_remote_copy(src_ref=_half_window(outs[t], c), dst_ref=_half_window(outs[t], 1 - c),
                                         send_sem=send_sems.at[t], recv_sem=recv_sems.at[t],
                                         device_id=(x, y, 1 - c), device_id_type=MESH).wait_recv()

    return pl.pallas_call(
        body, name="grad_share_halves", in_specs=[HBM_SPEC] * n, out_specs=[HBM_SPEC] * n,
        out_shape=[jax.ShapeDtypeStruct(g.shape, F32) for g in grads],
        input_output_aliases={t: t for t in range(n)},
        scratch_shapes=[pltpu.SemaphoreType.DMA((n,)), pltpu.SemaphoreType.DMA((n,))],
    )(*grads)


def _adamw(w, g, m, v, name):
    R, W = w.shape
    tr = _pick(R, (512, 352, 256, 32))

    def body(w_ref, g_ref, m_ref, v_ref, d_ref, nm_ref, nv_ref):
        gv = g_ref[...]
        nm = ADAM_B1 * m_ref[...] + (1.0 - ADAM_B1) * gv
        nv = ADAM_B2 * v_ref[...] + (1.0 - ADAM_B2) * (gv * gv)
        m_hat = nm / (1.0 - ADAM_B1 ** ADAM_STEP)
        v_hat = nv / (1.0 - ADAM_B2 ** ADAM_STEP)
        d_ref[...] = -ADAM_LR * (m_hat / (jnp.sqrt(v_hat) + ADAM_EPS) + ADAM_WD * w_ref[...])
        nm_ref[...] = nm
        nv_ref[...] = nv

    blk = pl.BlockSpec((tr, W), lambda i: (i, 0))
    shp = jax.ShapeDtypeStruct((R, W), F32)
    return pl.pallas_call(
        body, name=name, grid=(R // tr,), in_specs=[blk] * 4, out_specs=[blk] * 3, out_shape=[shp] * 3,
        compiler_params=_cparams(("parallel",)),
    )(w, g, m, v)


SMALL_ROWS = 32


def _pack_small(ln_g, ln_b, sinks):
    rows = jnp.concatenate([ln_g.reshape(-1, 128), ln_b.reshape(-1, 128),
                            jnp.pad(sinks.reshape(1, -1), ((0, 0), (0, 128 - sinks.size)))], axis=0)
    return jnp.pad(rows, ((0, SMALL_ROWS - rows.shape[0]), (0, 0)))


def _unpack_small(s, ln_shape, sink_shape):
    n = ln_shape[0] * ln_shape[1] * ln_shape[2] // 128
    return s[:n].reshape(ln_shape), s[n:2 * n].reshape(ln_shape), s[2 * n, :sink_shape[1]].reshape(sink_shape)


def _ffn_fwd(xin, w_in, w_out, gain, bias, tag):
    u, h = _ffn_in(xin, w_in, "ffn_in_" + tag)
    y, yb, z = _mm_ln(h, w_out, xin, gain, bias, 0.5, "ffn_out_ln_" + tag)
    return y, yb, dict(u=u, h=h, z=z, xin=xin)


def _ffn_bwd(dy, saved, w_in, w_out, gain, xin_b, tag, dw_dtype=F32):
    dz, dzc, gg, gb = _ln_bwd(saved["z"], dy, gain, 0.5, "ln_bwd_" + tag)
    du = _ffn_bwd_h(dzc, w_out, saved["u"], "ffn_bwd_h_" + tag)
    d_w_out = _mm_tn(saved["h"], dzc, "ffn_dwout_" + tag, out_dtype=dw_dtype)
    d_w_in = _mm_tn(xin_b, du, "ffn_dwin_" + tag, out_dtype=dw_dtype)
    dx = _mm_nt(du, w_in, "ffn_dx_" + tag, add=dz, add_scale=ALPHA)
    return dx, d_w_in, d_w_out, gg, gb


def kernel(x, ffn1_w_in, ffn1_w_out, ffn2_w_in, ffn2_w_out, ln_g, ln_b, a_w_qkv, a_w_o, kv_w, b_w_q, b_sinks, b_w_o, loss_target, m_ffn1_w_in, m_ffn1_w_out, m_ffn2_w_in, m_ffn2_w_out, m_ln_g, m_ln_b, m_a_w_qkv, m_a_w_o, m_kv_w, m_b_w_q, m_b_sinks, m_b_w_o, v_ffn1_w_in, v_ffn1_w_out, v_ffn2_w_in, v_ffn2_w_out, v_ln_g, v_ln_b, v_a_w_qkv, v_a_w_o, v_kv_w, v_b_w_q, v_b_sinks, v_b_w_o):
    ws = dict(ffn1_w_in=ffn1_w_in, ffn1_w_out=ffn1_w_out, ffn2_w_in=ffn2_w_in, ffn2_w_out=ffn2_w_out, a_w_qkv=a_w_qkv,
              a_w_o=a_w_o, kv_w=kv_w, b_w_q=b_w_q, b_w_o=b_w_o)
    ms = dict(ffn1_w_in=m_ffn1_w_in, ffn1_w_out=m_ffn1_w_out, ffn2_w_in=m_ffn2_w_in, ffn2_w_out=m_ffn2_w_out,
              a_w_qkv=m_a_w_qkv, a_w_o=m_a_w_o, kv_w=m_kv_w, b_w_q=m_b_w_q, b_w_o=m_b_w_o)
    vs = dict(ffn1_w_in=v_ffn1_w_in, ffn1_w_out=v_ffn1_w_out, ffn2_w_in=v_ffn2_w_in, ffn2_w_out=v_ffn2_w_out,
              a_w_qkv=v_a_w_qkv, a_w_o=v_a_w_o, kv_w=v_kv_w, b_w_q=v_b_w_q, b_w_o=v_b_w_o)
    _, _, c_idx, myq = _place()
    xs = x[0]
    target = loss_target[0]

    shards = {(n, l): (ws[n] if l is None else ws[n][l]).astype(BF16) for n, l in LAYER0_ITEMS + LAYER1_ITEMS}

    def as_weights(items, arrays):
        return {n: (a.reshape(D_MODEL, a.shape[-1]) if a.ndim == 4 else a) for (n, _), a in zip(items, arrays)}

    full0, small = _all_gather(LAYER0_ITEMS, shards, _pack_small(ln_g, ln_b, b_sinks))
    gather_state, token = _gather_start(LAYER1_ITEMS, shards, small)

    def layer1_weights(after):
        return as_weights(LAYER1_ITEMS, _gather_wait(LAYER1_ITEMS, gather_state, after))

    n_ln = ln_g.size // 128
    lg = jnp.concatenate([small[q, :n_ln].reshape(DEPTH, 3, 1, -1) for q in range(N_CHIPS)], axis=-1)
    lb = jnp.concatenate([small[q, n_ln:2 * n_ln].reshape(DEPTH, 3, 1, -1) for q in range(N_CHIPS)], axis=-1)
    lg = lg + token[0, 0]
    reducer = _GradReducer(c_idx, myq, {n: ws[n].shape for n in BIG})
    sq, grad_x, _, gg, gb, dsink_part = _local_step(xs, target, as_weights(LAYER0_ITEMS, full0), layer1_weights,
                                                    lg, lb, b_sinks.reshape(N_HEADS), reducer.begin)

    loss_row = jnp.pad(jnp.sum(sq).reshape(1, 1), ((0, 0), (0, 127)))
    dsinks = jnp.pad(dsink_part[:, 0, :].reshape(N_SLABS, 2, HEAD_DIM)[:, :, 0].reshape(1, N_HEADS), ((0, 0), (0, 128 - N_HEADS)))
    gg_full = jnp.stack([jnp.stack([jnp.sum(gg[i][j], axis=0) for j in range(3)]) for i in range(DEPTH)])
    gb_full = jnp.stack([jnp.stack([jnp.sum(gb[i][j], axis=0) for j in range(3)]) for i in range(DEPTH)])
    small_in = jnp.concatenate([loss_row, dsinks, gg_full.reshape(-1, 128), gb_full.reshape(-1, 128)], axis=0)
    small_in = jnp.pad(small_in, ((0, (-small_in.shape[0]) % 8), (0, 0)))
    small_sum = _small_all_reduce(small_in)
    loss = small_sum[0, 0] * (0.5 / D_MODEL)
    grad_sinks = small_sum[1, :N_HEADS].reshape(b_sinks.shape)
    n_full = DEPTH * 3 * D_MODEL // 128
    cols = D_MODEL // N_CHIPS
    grad_ln_g = lax.dynamic_slice_in_dim(small_sum[2:2 + n_full].reshape(DEPTH, 3, D_MODEL), myq * cols, cols, axis=2)
    grad_ln_b = lax.dynamic_slice_in_dim(small_sum[2 + n_full:2 + 2 * n_full].reshape(DEPTH, 3, D_MODEL), myq * cols, cols, axis=2)
    return _update(reducer, grad_x, loss, grad_ln_g, grad_ln_b, grad_sinks, ws, ms, vs,
                   (ln_g, ln_b, b_sinks), (m_ln_g, m_ln_b, m_b_sinks), (v_ln_g, v_ln_b, v_b_sinks))


def _local_step(xs, target, W, layer1_weights, lg, lb, sinks, grads_ready=None):
    if grads_ready is None:
        grads_ready = lambda tag, grads, overlap: 0.0
    S = xs.shape[0]
    slopes = jnp.asarray(_alibi_slopes(N_HEADS))
    in1, out1, in2, out2 = [W["ffn1_w_in"]], [W["ffn1_w_out"]], [W["ffn2_w_in"]], [W["ffn2_w_out"]]

    y1, y1b, s1 = _ffn_fwd(xs, in1[0], out1[0], lg[0, 0], lb[0, 0], "a1")
    qkv_a = _mm_nn(y1b, W["a_w_qkv"], F32, "qkv_a", split=True)
    mix_a, o_a, lse_a = _attn_fwd(qkv_a, slopes, None, PATTERNS_A, "attn_a_fwd")
    y2, y2b, z2 = _mm_ln(mix_a, W["a_w_o"], y1, lg[0, 1], lb[0, 1], 1.0, "attn_a_out_ln")
    y3, y3b, s3 = _ffn_fwd(y2, in2[0], out2[0], lg[0, 2], lb[0, 2], "a2")
    kv_w_rep = jnp.broadcast_to(W["kv_w"].reshape(D_MODEL, 2, N_KV_B, 1, HEAD_DIM),
                                (D_MODEL, 2, N_KV_B, GROUP_B, HEAD_DIM)).reshape(D_MODEL, 2 * D_MODEL)
    kv_rep = _mm_nn(y3b, kv_w_rep, F32, "kv_proj", split=(1, 2))
    W = dict(W, **layer1_weights(kv_rep))
    in1, out1, in2, out2 = (in1 + [W["ffn1_w_in"]], out1 + [W["ffn1_w_out"]], in2 + [W["ffn2_w_in"]],
                            out2 + [W["ffn2_w_out"]])
    y4, y4b, s4 = _ffn_fwd(y3, in1[1], out1[1], lg[1, 0], lb[1, 0], "b1")
    qkv_b = _mm_nn(y4b, W["b_w_q"], F32, "q_b", split=(0, 1), into=kv_rep)
    mix_b, o_b, lse_b = _attn_fwd(qkv_b, slopes, sinks, PATTERNS_B, "attn_b_fwd")
    y5, y5b, z5 = _mm_ln(mix_b, W["b_w_o"], y4, lg[1, 1], lb[1, 1], 1.0, "attn_b_out_ln")
    y6, _, s6 = _ffn_fwd(y5, in2[1], out2[1], lg[1, 2], lb[1, 2], "b2")

    dy6, sq = _loss_grad(y6, target, "loss_grad")
    gr = {n: None for n in BIG}
    gg = [[None] * 3 for _ in range(DEPTH)]
    gb = [[None] * 3 for _ in range(DEPTH)]

    dy5, d_in2_b, d_out2_b, gg[1][2], gb[1][2] = _ffn_bwd(dy6, s6, in2[1], out2[1], lg[1, 2], y5b, "b2", BF16)
    dz5, dz5b, gg[1][1], gb[1][1] = _ln_bwd(z5, dy5, lg[1, 1], 1.0, "ln_bwd_attn_b")
    gr["b_w_o"] = _mm_tn(mix_b, dz5b, "d_b_w_o", out_dtype=BF16)
    dmix_b = _mm_nt(dz5b, W["b_w_o"], "d_mix_b")
    dqkv_b, dsink_part = _attn_bwd(qkv_b, dmix_b, o_b, lse_b, slopes, sinks, PATTERNS_B, "attn_b_bwd")
    dq_b = (dqkv_b, 0)
    gr["b_w_q"] = _mm_tn(y4b, dq_b, "d_b_w_q", out_dtype=BF16)
    dy4 = _mm_nt(dq_b, W["b_w_q"], "d_y4", add=dz5, add_scale=ALPHA)
    dy3, d_in1_b, d_out1_b, gg[1][0], gb[1][0] = _ffn_bwd(dy4, s4, in1[1], out1[1], lg[1, 0], y3b, "b1", BF16)
    d_kv_w_rep = _mm_tn(y3b, dqkv_b, "d_kv_w", split=(1, 2))
    gr["kv_w"] = d_kv_w_rep.reshape(D_MODEL, 2, N_KV_B, GROUP_B, HEAD_DIM).sum(axis=3).reshape(D_MODEL, -1).astype(BF16)
    dy3 = _mm_nt(dqkv_b, kv_w_rep, "d_y3_kv", add=dy3, add_scale=1.0, split=(1, 2))
    tok = grads_ready("l1", {("ffn2_w_in", 1): d_in2_b, ("ffn2_w_out", 1): d_out2_b, ("b_w_o", None): gr["b_w_o"],
                             ("b_w_q", None): gr["b_w_q"], ("ffn1_w_in", 1): d_in1_b, ("ffn1_w_out", 1): d_out1_b,
                             ("kv_w", None): gr["kv_w"]}, True)
    lg0 = lg[0] + tok

    dy2, d_in2_a, d_out2_a, gg[0][2], gb[0][2] = _ffn_bwd(dy3, s3, in2[0], out2[0], lg0[2], y2b, "a2", BF16)
    tok = grads_ready("a2", {("ffn2_w_in", 0): d_in2_a, ("ffn2_w_out", 0): d_out2_a}, True)
    lg0 = lg0 + tok
    dz2, dz2b, gg[0][1], gb[0][1] = _ln_bwd(z2, dy2, lg0[1], 1.0, "ln_bwd_attn_a")
    gr["a_w_o"] = _mm_tn(mix_a, dz2b, "d_a_w_o", out_dtype=BF16)
    dmix_a = _mm_nt(dz2b, W["a_w_o"], "d_mix_a")
    dqkv_a, _ = _attn_bwd(qkv_a, dmix_a, o_a, lse_a, slopes, None, PATTERNS_A, "attn_a_bwd")
    gr["a_w_qkv"] = _mm_tn(y1b, dqkv_a, "d_a_w_qkv", split=True, out_dtype=BF16)
    tok = grads_ready("mix", {("a_w_o", None): gr["a_w_o"], ("a_w_qkv", None): gr["a_w_qkv"]}, True)
    lg0 = lg0 + tok
    dy1 = _mm_nt(dqkv_a, W["a_w_qkv"], "d_y1", add=dz2, add_scale=ALPHA, split=True)
    grad_x, d_in1_a, d_out1_a, gg[0][0], gb[0][0] = _ffn_bwd(dy1, s1, in1[0], out1[0], lg0[0], xs, "a1")
    grads_ready("a1", {("ffn1_w_in", 0): d_in1_a, ("ffn1_w_out", 0): d_out1_a}, False)
    gr["ffn1_w_in"] = [d_in1_a, d_in1_b]
    gr["ffn1_w_out"] = [d_out1_a, d_out1_b]
    gr["ffn2_w_in"] = [d_in2_a, d_in2_b]
    gr["ffn2_w_out"] = [d_out2_a, d_out2_b]
    return sq, grad_x, gr, gg, gb, dsink_part


def _grad_item(name, layer, g):
    if name.endswith("w_in"):
        return (g, "col", HALF_FF, _slot, name, layer)
    if name.endswith("w_out"):
        return (g, "row", D_MODEL, None, name, layer)
    if name == "a_w_qkv":
        return (g, "col", QKV_SHARD, lambda q: q, name, None)
    return (g, "row", g.shape[1], None, name, None)


class _GradReducer:
    def __init__(self, c_idx, myq, shard_shapes):
        self.c_idx, self.myq, self.shard_shapes = c_idx, myq, shard_shapes
        self.groups = []

    def begin(self, tag, grads, overlap):
        items = [_grad_item(n, l, g) for (n, l), g in grads.items()]
        kinds, widths, colblocks = [it[1] for it in items], [it[2] for it in items], [it[3] for it in items]
        views = [_grad_view(k, it[0]) for k, it in zip(kinds, items)]
        if overlap:
            lands = [jax.ShapeDtypeStruct((N_DIRECT,) + _piece_shape(k, w, _half_shape(k, v.shape)), BF16)
                     for k, w, v in zip(kinds, widths, views)]
            state, token = _split_start("grad_direct_start_" + tag, _direct_copies(kinds, widths, colblocks), 10 * len(items),
                                        views, lands, views[-1])
            self.groups.append((tag, items, None, state))
            return token[0, 0]
        from_sibling = _pair_exchange(views, kinds, "grad_pair_exchange_" + tag)
        sums = [_pair_sum(k, v, r, self.c_idx, "pair_sum_%s_%d" % (tag, t))
                for t, (k, v, r) in enumerate(zip(kinds, views, from_sibling))]
        self.groups.append((tag, items, sums, None))
        return 0.0

    def finish(self, after):
        half_done = {}
        for tag, items, sums, state in self.groups:
            kinds, widths, colblocks = [it[1] for it in items], [it[2] for it in items], [it[3] for it in items]
            direct = state is not None
            if direct:
                sums, received = _split_wait("grad_direct_wait_" + tag, _direct_copies(kinds, widths, colblocks), state, after)
            else:
                received = _chip_exchange(sums, kinds, widths, colblocks, "grad_chip_exchange_" + tag)
            for t, (it, s, r) in enumerate(zip(items, sums, received)):
                _, k, _, cb, name, layer = it
                own = cb(self.myq) if k == "col" else self.myq
                half_done[name] = _chip_sum(k, s, r, own, self.c_idx, self.shard_shapes[name], layer, half_done.get(name),
                                            "chip_sum_%s_%d" % (tag, t), direct=direct)
        return dict(zip(BIG, _share_halves([half_done[name] for name in BIG])))


def _update(reducer, grad_x, loss, grad_ln_g, grad_ln_b, grad_sinks, ws, ms, vs, small_w, small_m, small_v):
    ln_g, ln_b, b_sinks = small_w
    m_ln_g, m_ln_b, m_b_sinks = small_m
    v_ln_g, v_ln_b, v_b_sinks = small_v

    grads = reducer.finish(grad_x)

    deltas, new_m, new_v = {}, {}, {}
    for name in BIG:
        shp = ws[name].shape
        flat = lambda a: a.reshape(-1, shp[-1])
        d, nm, nv = _adamw(flat(ws[name]), flat(grads[name]), flat(ms[name]), flat(vs[name]), "adamw_" + name)
        deltas[name], new_m[name], new_v[name] = d.reshape(shp), nm.reshape(shp), nv.reshape(shp)
    delta_s, nm_s, nv_s = _adamw(_pack_small(ln_g, ln_b, b_sinks), _pack_small(grad_ln_g, grad_ln_b, grad_sinks),
                                 _pack_small(m_ln_g, m_ln_b, m_b_sinks), _pack_small(v_ln_g, v_ln_b, v_b_sinks), "adamw_small")
    for d, blob in ((grads, None), (deltas, delta_s), (new_m, nm_s), (new_v, nv_s)):
        if blob is None:
            d["ln_g"], d["ln_b"], d["b_sinks"] = grad_ln_g, grad_ln_b, grad_sinks
        else:
            d["ln_g"], d["ln_b"], d["b_sinks"] = _unpack_small(blob, ln_g.shape, b_sinks.shape)

    order = ("ffn1_w_in", "ffn1_w_out", "ffn2_w_in", "ffn2_w_out", "ln_g", "ln_b", "a_w_qkv", "a_w_o", "kv_w", "b_w_q",
             "b_sinks", "b_w_o")
    outs = [loss, grad_x[None]]
    for d in (grads, deltas, new_m, new_v):
        outs += [d[n] for n in order]
    return tuple(outs)
```

```python
import numpy as np
import jax
import jax.numpy as jnp
from jax import lax
from jax.experimental import pallas as pl
from jax.experimental.pallas import tpu as pltpu

F32 = jnp.float32
BF16 = jnp.bfloat16

D_MODEL = 1024
D_FF = 2816
HALF_FF = D_FF // 2
HEAD_DIM = 64
N_HEADS = 16
N_KV_B = 4
GROUP_B = N_HEADS // N_KV_B
DEPTH = 2
ALPHA = (2.0 * DEPTH) ** 0.25
LN_EPS = 1e-5
BLOCK = 128
SLAB = 128
N_SLABS = D_MODEL // SLAB
PATTERNS_A = ((1, 128, 1.0), (4, 128, 4.0), (16, 128, 16.0))
PATTERNS_B = ((1, 127, 1.0),)
NEG = -1e30

ADAM_LR = 0.001
ADAM_B1 = 0.9
ADAM_B2 = 0.999
ADAM_EPS = 1e-08
ADAM_WD = 0.01
ADAM_STEP = 10

N_CHIPS = 4
VMEM_LIMIT = 56 * 1024 * 1024
MESH = pl.DeviceIdType.MESH


def _alibi_slopes(n):
    return np.array([2.0 ** (-8.0 * (h + 1) / n) for h in range(n)], dtype=np.float32)


def _cparams(sem=None, vmem=VMEM_LIMIT):
    return pltpu.CompilerParams(dimension_semantics=sem, vmem_limit_bytes=vmem)


_DIMS = {"nn": ((1,), (0,)), "nt": ((1,), (1,)), "tn": ((0,), (0,))}


def _unlead(x):
    if isinstance(x, tuple):
        return x[0], x[1], x[0].shape[1:]
    return x, None, x.shape


def _bspec(block, imap, lead=None):
    if lead is None:
        return pl.BlockSpec(block, imap)
    return pl.BlockSpec((None,) + tuple(block), lambda *g: (lead,) + tuple(imap(*g)))


def _matmul(a, b, mode, out_dtype, tm, tn, tk, name, add=None, add_scale=1.0, split=False, into=None):
    out_spec = pl.BlockSpec((tm, tn), lambda i, j, k: (i, j))
    base, count = (0, 3) if split is True else (split or (0, 0))
    if mode == "nn":
        a, al, (M, K) = _unlead(a)
        b, bl, (K2, N) = _unlead(b)
        a_spec = _bspec((tm, tk), lambda i, j, k: (i, k), al)
        b_spec = _bspec((tk, tn), lambda i, j, k: (k, j), bl)
        out_struct = jax.ShapeDtypeStruct((M, N), out_dtype)
        if split:
            assert tn == D_MODEL and N == count * tn
            out_spec = pl.BlockSpec((None, tm, tn), lambda i, j, k: (j + base, i, 0))
            out_struct = jax.ShapeDtypeStruct((3, M, tn), out_dtype)
    elif mode == "nt":
        b, bl, (N, K2) = _unlead(b)
        if split:
            assert tk == D_MODEL
            M, K = a.shape[1], count * a.shape[2]
            a_spec = pl.BlockSpec((None, tm, tk), lambda i, j, k: (k + base, i, 0))
        else:
            a, al, (M, K) = _unlead(a)
            a_spec = _bspec((tm, tk), lambda i, j, k: (i, k), al)
        b_spec = _bspec((tn, tk), lambda i, j, k: (j, k), bl)
        out_struct = jax.ShapeDtypeStruct((M, N), out_dtype)
    else:
        a, al, (K, M) = _unlead(a)
        if split:
            assert tn == D_MODEL
            K2, N = b.shape[1], count * b.shape[2]
            b_spec = pl.BlockSpec((None, tk, tn), lambda i, j, k: (j + base, k, 0))
        else:
            b, bl, (K2, N) = _unlead(b)
            b_spec = _bspec((tk, tn), lambda i, j, k: (k, j), bl)
        a_spec = _bspec((tk, tm), lambda i, j, k: (k, i), al)
        out_struct = jax.ShapeDtypeStruct((M, N), out_dtype)
    assert K == K2 and M % tm == 0 and N % tn == 0 and K % tk == 0, (a.shape, b.shape, mode, tm, tn, tk)
    nk = K // tk
    dims = (_DIMS[mode], ((), ()))
    has_add = add is not None

    narrow = out_dtype != F32
    assert not (narrow and has_add)

    def body(*refs):
        if into is not None:
            refs = refs[:2] + refs[3:]
        if has_add:
            a_ref, b_ref, add_ref, o_ref = refs
            acc_ref = o_ref
        elif narrow:
            a_ref, b_ref, o_ref, acc_ref = refs
        else:
            a_ref, b_ref, o_ref = refs
            acc_ref = o_ref
        k = pl.program_id(2)
        part = lax.dot_general(a_ref[...].astype(BF16), b_ref[...].astype(BF16), dims, preferred_element_type=F32)
        if has_add:
            @pl.when(k == 0)
            def _():
                acc_ref[...] = part + add_scale * add_ref[...]
        else:
            @pl.when(k == 0)
            def _():
                acc_ref[...] = part

        @pl.when(k > 0)
        def _():
            acc_ref[...] += part

        if narrow:
            @pl.when(k == nk - 1)
            def _():
                o_ref[...] = acc_ref[...].astype(out_dtype)

    in_specs = [a_spec, b_spec]
    args = [a, b]
    aliases = {}
    if into is not None:
        assert mode == "nn" and split and not has_add
        in_specs.append(pl.BlockSpec(memory_space=pl.ANY))
        args.append(into)
        aliases = {2: 0}
    if has_add:
        in_specs.append(pl.BlockSpec((tm, tn), lambda i, j, k: (i, j)))
        args.append(add)
    return pl.pallas_call(
        body, name=name, grid=(M // tm, N // tn, nk),
        in_specs=in_specs, out_specs=out_spec, out_shape=out_struct, input_output_aliases=aliases,
        scratch_shapes=[pltpu.VMEM((tm, tn), F32)] if narrow else [],
        compiler_params=_cparams(("parallel", "parallel", "arbitrary")),
    )(*args)


def _pick(n, cands):
    for c in cands:
        if n % c == 0:
            return c
    raise ValueError((n, cands))


def _mm_nn(a, b, out_dtype, name, split=False, into=None):
    M, K = _unlead(a)[2]
    N = _unlead(b)[2][1]
    return _matmul(a, b, "nn", out_dtype, _pick(M, (1024, 512, 256)), _pick(N, (1024, 512)), _pick(K, (1024, 512)), name,
                   split=split, into=into)


def _mm_nt(a, b, name, add=None, add_scale=1.0, split=False):
    M, K = (a.shape[1], D_MODEL) if split else _unlead(a)[2]
    N = _unlead(b)[2][0]
    return _matmul(a, b, "nt", F32, _pick(M, (1024, 512, 256)), _pick(N, (1024, 512)),
                   _pick(K, (2816, 1024, 512)), name, add=add, add_scale=add_scale, split=split)


def _mm_tn(a, b, name, split=False, out_dtype=F32):
    K, M = _unlead(a)[2]
    N = D_MODEL if split else _unlead(b)[2][1]
    return _matmul(a, b, "tn", out_dtype, _pick(M, (1024, 1408, 512)), _pick(N, (1408, 1024, 512)),
                   _pick(K, (2048, 1024, 512, 256)), name, split=split)


def _ffn_in(x, w, name):
    S = x.shape[0]
    tm = _pick(S, (512, 256))
    w, wl, _ = _unlead(w)

    def body(x_ref, w_ref, t_ref, h_ref):
        acc = jnp.dot(x_ref[...].astype(BF16), w_ref[...], preferred_element_type=F32)
        g = acc[:, :HALF_FF]
        up = acc[:, HALF_FF:]
        sg = jax.nn.sigmoid(g)
        silu = g * sg
        t_ref[:, :HALF_FF] = (up * (sg * (1.0 + g * (1.0 - sg)))).astype(BF16)
        t_ref[:, HALF_FF:] = silu.astype(BF16)
        h_ref[...] = (silu * up).astype(BF16)

    return pl.pallas_call(
        body, name=name, grid=(2, S // tm),
        in_specs=[pl.BlockSpec((tm, D_MODEL), lambda j, i: (i, 0)),
                  _bspec((D_MODEL, D_FF), lambda j, i: (0, j), wl)],
        out_specs=[pl.BlockSpec((tm, D_FF), lambda j, i: (i, j)),
                   pl.BlockSpec((tm, HALF_FF), lambda j, i: (i, j))],
        out_shape=[jax.ShapeDtypeStruct((S, 2 * D_FF), BF16), jax.ShapeDtypeStruct((S, D_FF), BF16)],
        compiler_params=_cparams(("parallel", "parallel")),
    )(x, w)


def _ffn_bwd_h(dzc, w_out, u, name):
    S = dzc.shape[0]
    tm = _pick(S, (512, 256))
    w_out, wl, _ = _unlead(w_out)

    def body(dz_ref, w_ref, t_ref, du_ref):
        dh = lax.dot_general(dz_ref[...], w_ref[...], (((1,), (1,)), ((), ())), preferred_element_type=F32)
        du_ref[:, :HALF_FF] = (dh * t_ref[:, :HALF_FF].astype(F32)).astype(BF16)
        du_ref[:, HALF_FF:] = (dh * t_ref[:, HALF_FF:].astype(F32)).astype(BF16)

    return pl.pallas_call(
        body, name=name, grid=(2, S // tm),
        in_specs=[pl.BlockSpec((tm, D_MODEL), lambda j, i: (i, 0)),
                  _bspec((HALF_FF, D_MODEL), lambda j, i: (j, 0), wl),
                  pl.BlockSpec((tm, D_FF), lambda j, i: (i, j))],
        out_specs=pl.BlockSpec((tm, D_FF), lambda j, i: (i, j)),
        out_shape=jax.ShapeDtypeStruct((S, 2 * D_FF), BF16),
        compiler_params=_cparams(("parallel", "parallel")),
    )(dzc, w_out, u)


def _mm_ln(a, w, resid, gain, bias, c, name):
    S, K = a.shape
    tm = _pick(S, (512, 256))
    w, wl, _ = _unlead(w)

    def body(a_ref, w_ref, r_ref, g_ref, b_ref, y_ref, yb_ref, z_ref):
        z = ALPHA * r_ref[...] + c * jnp.dot(a_ref[...], w_ref[...], preferred_element_type=F32)
        mu = jnp.mean(z, axis=-1, keepdims=True)
        zc = z - mu
        var = jnp.mean(zc * zc, axis=-1, keepdims=True)
        y = zc * lax.rsqrt(var + LN_EPS) * g_ref[...] + b_ref[...]
        z_ref[...] = z
        y_ref[...] = y
        yb_ref[...] = y.astype(BF16)

    row = pl.BlockSpec((tm, D_MODEL), lambda i: (i, 0))
    vec = pl.BlockSpec((1, D_MODEL), lambda i: (0, 0))
    return pl.pallas_call(
        body, name=name, grid=(S // tm,),
        in_specs=[pl.BlockSpec((tm, K), lambda i: (i, 0)), _bspec((K, D_MODEL), lambda i: (0, 0), wl), row, vec, vec],
        out_specs=[row, row, row],
        out_shape=[jax.ShapeDtypeStruct((S, D_MODEL), F32), jax.ShapeDtypeStruct((S, D_MODEL), BF16),
                   jax.ShapeDtypeStruct((S, D_MODEL), F32)],
        compiler_params=_cparams(("parallel",)),
    )(a, w, resid, gain, bias)


def _ln_bwd(z, dy, gain, c, name):
    S = z.shape[0]
    tm = _pick(S, (512, 256))

    def body(z_ref, dy_ref, g_ref, dz_ref, dzc_ref, gg_ref, gb_ref):
        i = pl.program_id(0)
        zv = z_ref[...]
        dyv = dy_ref[...]
        mu = jnp.mean(zv, axis=-1, keepdims=True)
        zc = zv - mu
        var = jnp.mean(zc * zc, axis=-1, keepdims=True)
        rstd = lax.rsqrt(var + LN_EPS)
        xhat = zc * rstd
        dyg = dyv * g_ref[...]
        m1 = jnp.mean(dyg, axis=-1, keepdims=True)
        m2 = jnp.mean(dyg * xhat, axis=-1, keepdims=True)
        dz = rstd * (dyg - m1 - xhat * m2)
        dz_ref[...] = dz
        dzc_ref[...] = (c * dz).astype(BF16)
        pg = jnp.sum((dyv * xhat).reshape(tm // 8, 8, D_MODEL), axis=0)
        pb = jnp.sum(dyv.reshape(tm // 8, 8, D_MODEL), axis=0)

        @pl.when(i == 0)
        def _():
            gg_ref[...] = pg
            gb_ref[...] = pb

        @pl.when(i > 0)
        def _():
            gg_ref[...] += pg
            gb_ref[...] += pb

    row = pl.BlockSpec((tm, D_MODEL), lambda i: (i, 0))
    part = pl.BlockSpec((8, D_MODEL), lambda i: (0, 0))
    return pl.pallas_call(
        body, name=name, grid=(S // tm,),
        in_specs=[row, row, pl.BlockSpec((1, D_MODEL), lambda i: (0, 0))],
        out_specs=[row, row, part, part],
        out_shape=[jax.ShapeDtypeStruct((S, D_MODEL), F32), jax.ShapeDtypeStruct((S, D_MODEL), BF16),
                   jax.ShapeDtypeStruct((8, D_MODEL), F32), jax.ShapeDtypeStruct((8, D_MODEL), F32)],
        compiler_params=_cparams(("arbitrary",)),
    )(z, dy, gain)


def _loss_grad(y, t, name):
    S = y.shape[0]
    tm = _pick(S, (512, 256))

    def body(y_ref, t_ref, dy_ref, sq_ref):
        i = pl.program_id(0)
        e = y_ref[...] - t_ref[...]
        dy_ref[...] = e * (1.0 / D_MODEL)
        ps = jnp.sum((e * e).reshape(tm // 8, 8, D_MODEL), axis=0)

        @pl.when(i == 0)
        def _():
            sq_ref[...] = ps

        @pl.when(i > 0)
        def _():
            sq_ref[...] += ps

    row = pl.BlockSpec((tm, D_MODEL), lambda i: (i, 0))
    return pl.pallas_call(
        body, name=name, grid=(S // tm,),
        in_specs=[row, row], out_specs=[row, pl.BlockSpec((8, D_MODEL), lambda i: (0, 0))],
        out_shape=[jax.ShapeDtypeStruct((S, D_MODEL), F32), jax.ShapeDtypeStruct((8, D_MODEL), F32)],
        compiler_params=_cparams(("arbitrary",)),
    )(y, t)


def _rows(start, d):
    if d == 1:
        return pl.ds(pl.multiple_of(start, BLOCK), BLOCK)
    return pl.ds(start, BLOCK, stride=d)


def _ld(ref, start, d):
    return ref[_rows(start, d), :]


def _ld3(ref, lead, start, d):
    return ref[lead, _rows(start, d), :]


def _st3(ref, lead, start, d, val):
    ref[lead, _rows(start, d), :] = val


def _acc3(ref, lead, start, d, val):
    ref[lead, _rows(start, d), :] = ref[lead, _rows(start, d), :] + val


def _band_consts(slope0, slope1, maxd, scale):
    row = lax.broadcasted_iota(jnp.int32, (2 * BLOCK, 2 * BLOCK), 0)
    kj = lax.broadcasted_iota(jnp.int32, (2 * BLOCK, 2 * BLOCK), 1)
    top = row < BLOCK
    dist = BLOCK + jnp.where(top, row, row - BLOCK) - kj
    slope = jnp.where(top, slope0, slope1)
    base = jnp.where((dist >= 0) & (dist <= maxd), -(slope * (dist.astype(F32) * scale)), NEG)
    return base, kj < BLOCK


def _stack_heads(x, lo):
    return jnp.concatenate([jnp.where(lo, x, 0.0), jnp.where(lo, 0.0, x)], axis=0)


def _unstack_heads(x2, lo):
    return jnp.where(lo, x2[:BLOCK], x2[BLOCK:])


def _scores(q2, k2, base, prev_keys, first):
    s = lax.dot_general(q2, k2, (((1,), (1,)), ((), ())), preferred_element_type=F32) * (HEAD_DIM ** -0.5) + base
    return jnp.where(jnp.logical_and(prev_keys, first), NEG, s)


def _softmax_weights(ls):
    mx = ls[0]
    for l in ls[1:]:
        mx = jnp.maximum(mx, l)
    es = [jnp.exp(l - mx) for l in ls]
    tot = es[0]
    for e in es[1:]:
        tot = tot + e
    inv = 1.0 / tot
    return [e * inv for e in es]


def _attn_fwd(qkv, slopes, sinks, patterns, name):
    S = qkv.shape[1]
    npat = len(patterns)
    has_sink = sinks is not None
    if not has_sink:
        sinks = jnp.zeros((N_HEADS,), F32)
    rows_c = 256

    def body(slopes_ref, sinks_ref, x_ref, mix_ref, o_ref, lse_ref, o_scr, lse_scr):
        p = pl.program_id(0)
        lo = lax.broadcasted_iota(jnp.int32, (BLOCK, SLAB), 1) < HEAD_DIM
        top1 = lax.broadcasted_iota(jnp.int32, (2 * BLOCK, 1), 0) < BLOCK
        sk2 = jnp.where(top1, sinks_ref[2 * p], sinks_ref[2 * p + 1])
        for pi, (d, maxd, scale) in enumerate(patterns):
            nb = S // d // BLOCK
            base, prev_keys = _band_consts(slopes_ref[2 * p], slopes_ref[2 * p + 1], maxd, scale)

            def blk(t, carry, pi=pi, d=d, nb=nb, base=base, prev_keys=prev_keys):
                r = t // nb
                n = t - r * nb
                start = r + (d * BLOCK) * n
                prev = jnp.where(n > 0, start - d * BLOCK, start)
                q2 = _stack_heads(_ld3(x_ref, 0, start, d), lo).astype(BF16)
                k2 = jnp.concatenate([_ld3(x_ref, 1, prev, d), _ld3(x_ref, 1, start, d)], axis=0).astype(BF16)
                v2 = jnp.concatenate([_ld3(x_ref, 2, prev, d), _ld3(x_ref, 2, start, d)], axis=0).astype(BF16)
                s = _scores(q2, k2, base, prev_keys, n == 0)
                m = jnp.max(s, axis=-1, keepdims=True)
                if has_sink:
                    m = jnp.maximum(m, sk2)
                e = jnp.exp(s - m)
                den = jnp.sum(e, axis=-1, keepdims=True)
                if has_sink:
                    den = den + jnp.exp(sk2 - m)
                o2 = jnp.dot((e / den).astype(BF16), v2, preferred_element_type=F32)
                _st3(o_scr, pi, start, d, _unstack_heads(o2, lo))
                _st3(lse_scr, pi, start, d, _unstack_heads(m + jnp.log(den), lo))
                return carry

            lax.fori_loop(0, d * nb, blk, 0, unroll=8)

        lane_c = lax.broadcasted_iota(jnp.int32, (rows_c, SLAB), 1)

        def comb(ci, carry):
            rows = pl.ds(pl.multiple_of(ci * rows_c, rows_c), rows_c)
            ls = [lse_scr[i, rows, :] for i in range(npat)]
            packed = jnp.zeros((rows_c, SLAB), F32)
            for i in range(npat):
                o_ref[i, rows, :] = o_scr[i, rows, :].astype(BF16)
                packed = jnp.where(lane_c == 2 * i, ls[i][:, :1], packed)
                packed = jnp.where(lane_c == 2 * i + 1, ls[i][:, HEAD_DIM:HEAD_DIM + 1], packed)
            lse_ref[rows, :] = packed
            if npat == 1:
                mix_ref[rows, :] = o_scr[0, rows, :].astype(BF16)
            else:
                ws = _softmax_weights(ls)
                acc = ws[0] * o_scr[0, rows, :]
                for i in range(1, npat):
                    acc = acc + ws[i] * o_scr[i, rows, :]
                mix_ref[rows, :] = acc.astype(BF16)
            return carry

        lax.fori_loop(0, S // rows_c, comb, 0)

    smem = pl.BlockSpec(memory_space=pltpu.SMEM)
    return pl.pallas_call(
        body, name=name, grid=(N_SLABS,),
        in_specs=[smem, smem, pl.BlockSpec((3, S, SLAB), lambda p: (0, 0, p))],
        out_specs=[pl.BlockSpec((S, SLAB), lambda p: (0, p)), pl.BlockSpec((npat, S, SLAB), lambda p: (0, 0, p)),
                   pl.BlockSpec((None, S, SLAB), lambda p: (p, 0, 0))],
        out_shape=[jax.ShapeDtypeStruct((S, D_MODEL), BF16), jax.ShapeDtypeStruct((npat, S, D_MODEL), BF16),
                   jax.ShapeDtypeStruct((N_SLABS, S, SLAB), F32)],
        scratch_shapes=[pltpu.VMEM((npat, S, SLAB), F32), pltpu.VMEM((npat, S, SLAB), F32)],
        compiler_params=_cparams(("arbitrary",)),
    )(slopes, sinks, qkv)


def _attn_bwd(qkv, dout, o, lse, slopes, sinks, patterns, name):
    S = qkv.shape[1]
    npat = len(patterns)
    has_sink = sinks is not None
    if not has_sink:
        sinks = jnp.zeros((N_HEADS,), F32)
    rows_c = 256

    def headsum(x, lo):
        s0 = jnp.sum(jnp.where(lo, x, 0.0), axis=-1, keepdims=True)
        s1 = jnp.sum(jnp.where(lo, 0.0, x), axis=-1, keepdims=True)
        return jnp.where(lo, s0, s1)

    def body(slopes_ref, sinks_ref, x_ref, do_ref, o_ref, lsep_ref, dxo_ref, dsink_ref, dbar_ref, sacc_ref, lse_ref, dx_ref):
        p = pl.program_id(0)
        lo = lax.broadcasted_iota(jnp.int32, (BLOCK, SLAB), 1) < HEAD_DIM
        lo_c = lax.broadcasted_iota(jnp.int32, (rows_c, SLAB), 1) < HEAD_DIM
        top1 = lax.broadcasted_iota(jnp.int32, (2 * BLOCK, 1), 0) < BLOCK
        sk2 = jnp.where(top1, sinks_ref[2 * p], sinks_ref[2 * p + 1])

        def prep(ci, carry):
            rows = pl.ds(pl.multiple_of(ci * rows_c, rows_c), rows_c)
            dov = do_ref[rows, :]
            dx_ref[:, rows, :] = jnp.zeros((3, rows_c, SLAB), F32)
            packed = lsep_ref[rows, :]
            ls = [jnp.where(lo_c, packed[:, 2 * i:2 * i + 1], packed[:, 2 * i + 1:2 * i + 2]) for i in range(npat)]
            for i in range(npat):
                lse_ref[i, rows, :] = ls[i]
            if npat == 1:
                dbar_ref[rows, :] = headsum(dov * o_ref[0, rows, :].astype(F32), lo_c)
            else:
                ws = _softmax_weights(ls)
                acc = ws[0] * headsum(dov * o_ref[0, rows, :].astype(F32), lo_c)
                for i in range(1, npat):
                    acc = acc + ws[i] * headsum(dov * o_ref[i, rows, :].astype(F32), lo_c)
                dbar_ref[rows, :] = acc
            return carry

        lax.fori_loop(0, S // rows_c, prep, 0)
        sacc_ref[...] = jnp.zeros((BLOCK, SLAB), F32)

        for pi, (d, maxd, scale) in enumerate(patterns):
            nb = S // d // BLOCK
            base, prev_keys = _band_consts(slopes_ref[2 * p], slopes_ref[2 * p + 1], maxd, scale)

            def blk(t, carry, pi=pi, d=d, nb=nb, base=base, prev_keys=prev_keys):
                r = t // nb
                n = t - r * nb
                start = r + (d * BLOCK) * n
                prev = jnp.where(n > 0, start - d * BLOCK, start)
                q2 = _stack_heads(_ld3(x_ref, 0, start, d), lo).astype(BF16)
                k2 = jnp.concatenate([_ld3(x_ref, 1, prev, d), _ld3(x_ref, 1, start, d)], axis=0).astype(BF16)
                v2 = jnp.concatenate([_ld3(x_ref, 2, prev, d), _ld3(x_ref, 2, start, d)], axis=0).astype(BF16)
                ls = [_ld3(lse_ref, i, start, d) for i in range(npat)]
                w = _softmax_weights(ls)[pi] if npat > 1 else 1.0
                do2 = _stack_heads(w * _ld(do_ref, start, d), lo).astype(BF16)
                dl = w * _ld(dbar_ref, start, d)
                lse2 = jnp.concatenate([ls[pi][:, :1], ls[pi][:, HEAD_DIM:HEAD_DIM + 1]], axis=0)
                dl2 = jnp.concatenate([dl[:, :1], dl[:, HEAD_DIM:HEAD_DIM + 1]], axis=0)
                s = _scores(q2, k2, base, prev_keys, n == 0)
                pr = jnp.exp(s - lse2)
                dp = lax.dot_general(do2, v2, (((1,), (1,)), ((), ())), preferred_element_type=F32)
                ds = (pr * (dp - dl2) * (HEAD_DIM ** -0.5)).astype(BF16)
                dq2 = jnp.dot(ds, k2, preferred_element_type=F32)
                dk2 = lax.dot_general(ds, q2, (((0,), (0,)), ((), ())), preferred_element_type=F32)
                dv2 = lax.dot_general(pr.astype(BF16), do2, (((0,), (0,)), ((), ())), preferred_element_type=F32)
                _acc3(dx_ref, 0, start, d, _unstack_heads(dq2, lo))
                _acc3(dx_ref, 1, prev, d, dk2[:BLOCK])
                _acc3(dx_ref, 1, start, d, dk2[BLOCK:])
                _acc3(dx_ref, 2, prev, d, dv2[:BLOCK])
                _acc3(dx_ref, 2, start, d, dv2[BLOCK:])
                if has_sink:
                    sacc_ref[...] += _unstack_heads(-jnp.exp(sk2 - lse2) * dl2, lo)
                return carry

            lax.fori_loop(0, d * nb, blk, 0, unroll=4)

        dsink_ref[...] = jnp.broadcast_to(jnp.sum(sacc_ref[...], axis=0, keepdims=True), (8, SLAB))

        def emit(ci, carry):
            rows = pl.ds(pl.multiple_of(ci * rows_c, rows_c), rows_c)
            dxo_ref[:, rows, :] = dx_ref[:, rows, :].astype(BF16)
            return carry

        lax.fori_loop(0, S // rows_c, emit, 0)

    smem = pl.BlockSpec(memory_space=pltpu.SMEM)
    return pl.pallas_call(
        body, name=name, grid=(N_SLABS,),
        in_specs=[smem, smem, pl.BlockSpec((3, S, SLAB), lambda p: (0, 0, p)), pl.BlockSpec((S, SLAB), lambda p: (0, p)),
                  pl.BlockSpec((npat, S, SLAB), lambda p: (0, 0, p)), pl.BlockSpec((None, S, SLAB), lambda p: (p, 0, 0))],
        out_specs=[pl.BlockSpec((3, S, SLAB), lambda p: (0, 0, p)), pl.BlockSpec((None, 8, SLAB), lambda p: (p, 0, 0))],
        out_shape=[jax.ShapeDtypeStruct((3, S, D_MODEL), BF16), jax.ShapeDtypeStruct((N_SLABS, 8, SLAB), F32)],
        scratch_shapes=[pltpu.VMEM((S, SLAB), F32), pltpu.VMEM((BLOCK, SLAB), F32), pltpu.VMEM((npat, S, SLAB), F32),
                        pltpu.VMEM((3, S, SLAB), F32)],
        compiler_params=_cparams(("arbitrary",)),
    )(slopes, sinks, qkv, dout, o, lse)


def _place():
    x, y, c = lax.axis_index("x"), lax.axis_index("y"), lax.axis_index("c")
    return x, y, c, 2 * x + y


def _other_chips(x, y):
    return [(1 - x, y), (x, 1 - y), (1 - x, 1 - y)]


HBM_SPEC = pl.BlockSpec(memory_space=pl.ANY)


def _slot(q):
    return 2 * (q % 2) + q // 2


BIG = ("ffn1_w_in", "ffn1_w_out", "ffn2_w_in", "ffn2_w_out", "a_w_qkv", "a_w_o", "kv_w", "b_w_q", "b_w_o")
QKV_SHARD = 3 * D_MODEL // N_CHIPS
ROW_SHARD = D_MODEL // N_CHIPS


LAYER0_ITEMS = (("ffn1_w_in", 0), ("ffn1_w_out", 0), ("a_w_qkv", None), ("a_w_o", None), ("ffn2_w_in", 0),
                ("ffn2_w_out", 0), ("kv_w", None))
LAYER1_ITEMS = (("ffn1_w_in", 1), ("ffn1_w_out", 1), ("b_w_q", None), ("b_w_o", None), ("ffn2_w_in", 1),
                ("ffn2_w_out", 1))
OUT_SHARD = D_FF // N_CHIPS


def _full_shape(name):
    if name.endswith("w_in"):
        return (D_MODEL, 2 * D_FF)
    if name.endswith("w_out"):
        return (D_FF, D_MODEL)
    if name == "a_w_qkv":
        return (D_MODEL, 3 * D_MODEL)
    if name == "kv_w":
        return (N_CHIPS, 2, ROW_SHARD // 2, 2 * N_KV_B * HEAD_DIM)
    return (N_CHIPS, 2, ROW_SHARD // 2, D_MODEL)


def _gather_src(item, ref, c):
    name, _ = item
    if name.endswith("w_in"):
        return ref.at[pl.ds(c * (D_MODEL // 2), D_MODEL // 2)]
    if name.endswith("w_out"):
        return ref.at[pl.ds(c * (OUT_SHARD // 2), OUT_SHARD // 2)]
    if name == "a_w_qkv":
        return ref.at[0, pl.ds(c * (D_MODEL // 2), D_MODEL // 2)]
    if name == "kv_w":
        return ref.at[pl.ds(c * (ROW_SHARD // 2), ROW_SHARD // 2)]
    return ref.at[0, pl.ds(c * (ROW_SHARD // 2), ROW_SHARD // 2)]


def _gather_dst(item, ref, q, c):
    name, _ = item
    if name.endswith("w_in"):
        return ref.at[pl.ds(c * (D_MODEL // 2), D_MODEL // 2), pl.ds(_slot(q) * HALF_FF, HALF_FF)]
    if name.endswith("w_out"):
        return ref.at[pl.ds(q * OUT_SHARD + c * (OUT_SHARD // 2), OUT_SHARD // 2)]
    if name == "a_w_qkv":
        return ref.at[pl.ds(c * (D_MODEL // 2), D_MODEL // 2), pl.ds(q * QKV_SHARD, QKV_SHARD)]
    return ref.at[q, c]


def _all_gather(items, shards, small):
    n = len(items)
    r = small.shape[0]
    per = 8

    def body(*refs):
        srcs, small_ref = refs[:n], refs[n]
        dsts, s_ref = refs[n + 1:2 * n + 1], refs[2 * n + 1]
        send_sems, recv_sems = refs[2 * n + 2:]
        x, y, c, myq = _place()
        sibling = (x, y, 1 - c)
        chips = _other_chips(x, y)

        def big(t, k, src, q, h, to):
            return pltpu.make_async_remote_copy(src_ref=src, dst_ref=_gather_dst(items[t], dsts[t], q, h),
                                                send_sem=send_sems.at[per * t + k], recv_sem=recv_sems.at[per * t + k],
                                                device_id=to, device_id_type=MESH)

        def tiny(k, q, to):
            return pltpu.make_async_remote_copy(src_ref=small_ref, dst_ref=s_ref.at[q], send_sem=send_sems.at[per * n + k],
                                                recv_sem=recv_sems.at[per * n + k], device_id=to, device_id_type=MESH)

        first = []
        for j, chip in enumerate(chips):
            first += [big(t, j, _gather_src(items[t], srcs[t], c), myq, c, (*chip, c)) for t in range(n)]
            first.append(tiny(j, myq, (*chip, c)))
        own = [big(t, 6 + h, _gather_src(items[t], srcs[t], h), myq, h, sibling) for t in range(n) for h in (0, 1)]
        own.append(tiny(3, myq, sibling))
        for cp in first + own:
            cp.start()
        passed = []
        for j, (cx, cy) in enumerate(chips):
            q = 2 * cx + cy
            for t in range(n):
                src = _gather_src(items[t], srcs[t], c)
                big(t, j, src, q, c, sibling).wait_recv()
                fwd = big(t, 3 + j, _gather_dst(items[t], dsts[t], q, c), q, c, sibling)
                fwd.start()
                passed.append(fwd)
        for j, (cx, cy) in enumerate(chips):
            q = 2 * cx + cy
            for t in range(n):
                big(t, 3 + j, _gather_src(items[t], srcs[t], c), q, 1 - c, sibling).wait_recv()
            tiny(j, q, sibling).wait_recv()
        for cp in own:
            cp.wait_recv()
        for cp in first + passed + own:
            cp.wait_send()

    outs = pl.pallas_call(
        body, name="all_gather_layer0",
        in_specs=[HBM_SPEC] * (n + 1), out_specs=[HBM_SPEC] * (n + 1),
        out_shape=[jax.ShapeDtypeStruct(_full_shape(name), BF16) for name, _ in items]
        + [jax.ShapeDtypeStruct((N_CHIPS, r, 128), F32)],
        scratch_shapes=[pltpu.SemaphoreType.DMA((per * n + 4,)), pltpu.SemaphoreType.DMA((per * n + 4,))],
    )(*[shards[item] for item in items], small)
    return list(outs[:n]), outs[n]


SEM_SPEC = pl.BlockSpec(memory_space=pltpu.SEMAPHORE)
DATAFLOW = pltpu.SideEffectType.DATAFLOW_SIDE_EFFECTING
PER_ITEM = 8


def _split_start(name, copies, n_sems, sources, land_shapes, after):
    n, m = len(sources), len(land_shapes)

    def body(*refs):
        srcs, lands = refs[:n], refs[n:n + m]
        send_sems, recv_sems = refs[n + m + 1], refs[n + m + 2]
        token = refs[-1]
        for src, dst_there, _, s, peer in copies(srcs, lands):
            pltpu.make_async_remote_copy(src_ref=src, dst_ref=dst_there, send_sem=send_sems.at[s], recv_sem=recv_sems.at[s],
                                         device_id=peer, device_id_type=MESH).start()
        token[...] = jnp.zeros_like(token)

    src_arrays = [pltpu.with_memory_space_constraint(a, pltpu.HBM) for a in sources]
    land_arrays = [pltpu.with_memory_space_constraint(lax.empty(s.shape, s.dtype), pltpu.HBM) for s in land_shapes]
    hbm = pl.BlockSpec(memory_space=pltpu.HBM)
    outs = pl.pallas_call(
        body, name=name,
        in_specs=[hbm] * (n + m) + [HBM_SPEC],
        out_specs=[SEM_SPEC, SEM_SPEC] + [hbm] * (n + m) + [pl.BlockSpec(memory_space=pltpu.VMEM)],
        out_shape=[pltpu.SemaphoreType.DMA((n_sems,)), pltpu.SemaphoreType.DMA((n_sems,))]
        + [pltpu.HBM(a.shape, a.dtype) for a in src_arrays + land_arrays] + [jax.ShapeDtypeStruct((8, 128), F32)],
        input_output_aliases={i: 2 + i for i in range(n + m)},
        compiler_params=pltpu.CompilerParams(has_side_effects=DATAFLOW),
    )(*src_arrays, *land_arrays, after)
    return (outs[0], outs[1], list(outs[2:2 + n]), list(outs[2 + n:2 + n + m])), outs[-1]


def _split_wait(name, copies, state, after):
    send_sems, recv_sems, srcs_thru, lands_thru = state
    n, m = len(srcs_thru), len(lands_thru)

    def body(*refs):
        srcs, lands = refs[:n], refs[n:n + m]
        send_sems, recv_sems = refs[n + m], refs[n + m + 1]
        for src, _, dst_here, s, peer in copies(srcs, lands):
            cp = pltpu.make_async_remote_copy(src_ref=src, dst_ref=dst_here, send_sem=send_sems.at[s], recv_sem=recv_sems.at[s],
                                              device_id=peer, device_id_type=MESH)
            cp.wait_send()
            cp.wait_recv()

    hbm = pl.BlockSpec(memory_space=pltpu.HBM)
    outs = pl.pallas_call(
        body, name=name,
        in_specs=[hbm] * (n + m) + [SEM_SPEC, SEM_SPEC, HBM_SPEC],
        out_specs=[hbm] * (n + m),
        out_shape=[pltpu.HBM(a.shape, a.dtype) for a in srcs_thru + lands_thru],
        input_output_aliases={i: i for i in range(n + m)},
        compiler_params=pltpu.CompilerParams(has_side_effects=DATAFLOW),
    )(*srcs_thru, *lands_thru, send_sems, recv_sems, after)
    return list(outs[:n]), list(outs[n:])


def _gather_copies(items):
    def copies(srcs, lands):
        x, y, c, myq = _place()
        out = []
        for t, item in enumerate(items):
            for h in (0, 1):
                src = _gather_src(item, srcs[t], h)
                for j, (cx, cy) in enumerate(_other_chips(x, y)):
                    out.append((src, _gather_dst(item, lands[t], myq, h), _gather_dst(item, lands[t], 2 * cx + cy, h),
                                PER_ITEM * t + 2 * j + h, (cx, cy, c)))
                out.append((src, _gather_dst(item, lands[t], myq, h), _gather_dst(item, lands[t], myq, h),
                            PER_ITEM * t + 6 + h, (x, y, 1 - c)))
        return out
    return copies


def _gather_start(items, shards, after):
    lands = [jax.ShapeDtypeStruct(_full_shape(name), BF16) for name, _ in items]
    return _split_start("gather_layer1_start", _gather_copies(items), PER_ITEM * len(items),
                        [shards[item] for item in items], lands, after)


def _gather_wait(items, state, after):
    return _split_wait("gather_layer1_wait", _gather_copies(items), state, after)[1]


def _small_all_reduce(v):
    r = v.shape[0]

    def body(v_ref, o_ref, buf_ref, send_sems, recv_sems):
        x, y, c, _ = _place()
        me = 4 * x + 2 * y + c
        buf_ref[me] = v_ref[...]
        copies = []
        for k in range(1, 8):
            fx, fy, fc = (k >> 2) & 1, (k >> 1) & 1, k & 1
            to = (x ^ fx, y ^ fy, c ^ fc)
            cp = pltpu.make_async_remote_copy(src_ref=v_ref, dst_ref=buf_ref.at[me], send_sem=send_sems.at[k - 1],
                                              recv_sem=recv_sems.at[k - 1], device_id=to, device_id_type=MESH)
            cp.start()
            copies.append(cp)
        for k in range(1, 8):
            fx, fy, fc = (k >> 2) & 1, (k >> 1) & 1, k & 1
            src_dev = 4 * (x ^ fx) + 2 * (y ^ fy) + (c ^ fc)
            pltpu.make_async_remote_copy(src_ref=v_ref, dst_ref=buf_ref.at[src_dev], send_sem=send_sems.at[k - 1],
                                         recv_sem=recv_sems.at[k - 1], device_id=(x, y, c), device_id_type=MESH).wait_recv()
        for cp in copies:
            cp.wait_send()
        tot = buf_ref[0]
        for i in range(1, 8):
            tot = tot + buf_ref[i]
        o_ref[...] = tot

    vm = pl.BlockSpec(memory_space=pltpu.VMEM)
    return pl.pallas_call(
        body, name="small_all_reduce", in_specs=[vm], out_specs=vm,
        out_shape=jax.ShapeDtypeStruct((r, 128), F32),
        scratch_shapes=[pltpu.VMEM((8, r, 128), F32), pltpu.SemaphoreType.DMA((7,)), pltpu.SemaphoreType.DMA((7,))],
    )(v)


def _grad_view(kind, g):
    if kind == "col":
        return g.reshape(2, g.shape[0] // 2, g.shape[1])
    return g.reshape(N_CHIPS, 2, g.shape[0] // (2 * N_CHIPS), g.shape[1])


def _half_of(kind, ref, h):
    return ref.at[h] if kind == "col" else ref.at[:, h]


def _half_shape(kind, view_shape):
    return view_shape[1:] if kind == "col" else (view_shape[0],) + view_shape[2:]


def _piece_of(kind, width, colblock, ref, q):
    if kind == "col":
        return ref.at[:, pl.ds(colblock(q) * width, width)]
    return ref.at[q]


def _piece_shape(kind, width, half_shape):
    return (half_shape[0], width) if kind == "col" else half_shape[1:]


def _pair_exchange(views, kinds, name):
    n = len(views)

    def body(*refs):
        ins, outs = refs[:n], refs[n:2 * n]
        send_sems, recv_sems = refs[2 * n:]
        x, y, c, _ = _place()
        cps = []
        for t in range(n):
            cp = pltpu.make_async_remote_copy(src_ref=_half_of(kinds[t], ins[t], 1 - c), dst_ref=outs[t],
                                              send_sem=send_sems.at[t], recv_sem=recv_sems.at[t],
                                              device_id=(x, y, 1 - c), device_id_type=MESH)
            cp.start()
            cps.append(cp)
        for cp in cps:
            cp.wait()

    return pl.pallas_call(
        body, name=name, in_specs=[HBM_SPEC] * n, out_specs=[HBM_SPEC] * n,
        out_shape=[jax.ShapeDtypeStruct(_half_shape(k, v.shape), v.dtype) for k, v in zip(kinds, views)],
        scratch_shapes=[pltpu.SemaphoreType.DMA((n,)), pltpu.SemaphoreType.DMA((n,))],
    )(*views)


def _pair_sum(kind, view, recv, c, name):
    hs = recv.shape
    N = hs[-1]
    rows = hs[-2]
    tr = _pick(rows, (512, 352, 128))
    tn = _pick(N, (1408, 1024, 512))

    def body(c_ref, p_ref, r_ref, s_ref):
        s_ref[...] = (p_ref[...] + r_ref[...]).astype(BF16)

    if kind == "col":
        grid = (rows // tr, N // tn)
        mine = pl.BlockSpec((None, tr, tn), lambda i, j, c_ref: (c_ref[0], i, j))
        blk = pl.BlockSpec((tr, tn), lambda i, j, c_ref: (i, j))
        sem = ("parallel", "parallel")
    else:
        grid = (N_CHIPS, rows // tr, N // tn)
        mine = pl.BlockSpec((None, None, tr, tn), lambda q, i, j, c_ref: (q, c_ref[0], i, j))
        blk = pl.BlockSpec((None, tr, tn), lambda q, i, j, c_ref: (q, i, j))
        sem = ("parallel", "parallel", "parallel")
    return pl.pallas_call(
        body, name=name,
        grid_spec=pltpu.PrefetchScalarGridSpec(num_scalar_prefetch=1, grid=grid, in_specs=[mine, blk], out_specs=blk),
        out_shape=jax.ShapeDtypeStruct(hs, BF16),
        compiler_params=_cparams(sem),
    )(c.reshape(1).astype(jnp.int32), view, recv)


def _chip_copies(kinds, widths, colblocks):
    def copies(srcs, lands):
        x, y, c, _ = _place()
        out = []
        for j, (cx, cy) in enumerate(_other_chips(x, y)):
            for t in range(len(kinds)):
                out.append((_piece_of(kinds[t], widths[t], colblocks[t], srcs[t], 2 * cx + cy), lands[t].at[j],
                            lands[t].at[j], 3 * t + j, (cx, cy, c)))
        return out
    return copies


def _chip_land_shapes(sums, kinds, widths):
    return [jax.ShapeDtypeStruct((3,) + _piece_shape(k, w, s.shape), BF16) for k, w, s in zip(kinds, widths, sums)]


def _chip_exchange(sums, kinds, widths, colblocks, name):
    n = len(sums)
    copies = _chip_copies(kinds, widths, colblocks)

    def body(*refs):
        send_sems, recv_sems = refs[2 * n:]
        cps = [pltpu.make_async_remote_copy(src_ref=src, dst_ref=dst, send_sem=send_sems.at[s], recv_sem=recv_sems.at[s],
                                            device_id=peer, device_id_type=MESH)
               for src, dst, _, s, peer in copies(refs[:n], refs[n:2 * n])]
        for cp in cps:
            cp.start()
        for cp in cps:
            cp.wait()

    return pl.pallas_call(
        body, name=name, in_specs=[HBM_SPEC] * n, out_specs=[HBM_SPEC] * n,
        out_shape=_chip_land_shapes(sums, kinds, widths),
        scratch_shapes=[pltpu.SemaphoreType.DMA((3 * n,)), pltpu.SemaphoreType.DMA((3 * n,))],
    )(*sums)


N_DIRECT = 7


def _direct_piece(kind, width, colblock, view_ref, q, h):
    if kind == "col":
        return view_ref.at[h, :, pl.ds(colblock(q) * width, width)]
    return view_ref.at[q, h]


def _direct_copies(kinds, widths, colblocks):
    def copies(srcs, lands):
        x, y, c, myq = _place()
        out = []
        for t in range(len(kinds)):
            def piece(q, h, t=t):
                return _direct_piece(kinds[t], widths[t], colblocks[t], srcs[t], q, h)
            for j, (cx, cy) in enumerate(_other_chips(x, y)):
                for h in (0, 1):
                    out.append((piece(2 * cx + cy, h), lands[t].at[2 * j + c], lands[t].at[2 * j + h],
                                10 * t + 3 * j + c + h, (cx, cy, h)))
            out.append((piece(myq, 1 - c), lands[t].at[6], lands[t].at[6], 10 * t + 9, (x, y, 1 - c)))
        return out
    return copies


def _chip_sum(kind, own_src, recv, block_idx, c, shard_shape, layer, into, name, direct=False):
    n_recv, rows, N = recv.shape
    tr = _pick(rows, (512, 352, 128))
    tn = _pick(N, (1408, 1024, 768, 512))
    ni, nj = rows // tr, N // tn

    def body(q_ref, s_ref, r_ref, *rest):
        o_ref = rest[-1]
        tot = s_ref[...].astype(F32)
        for k in range(n_recv):
            tot = tot + r_ref[k].astype(F32)
        o_ref[...] = tot

    if direct and kind == "col":
        own = pl.BlockSpec((None, tr, tn), lambda i, j, q_ref: (q_ref[1], i, q_ref[0] * nj + j))
    elif direct:
        own = pl.BlockSpec((None, None, tr, tn), lambda i, j, q_ref: (q_ref[0], q_ref[1], i, j))
    elif kind == "col":
        own = pl.BlockSpec((tr, tn), lambda i, j, q_ref: (i, q_ref[0] * nj + j))
    else:
        own = pl.BlockSpec((None, tr, tn), lambda i, j, q_ref: (q_ref[0], i, j))
    if len(shard_shape) == 3:
        lead = 0 if layer is None else layer
        out_spec = pl.BlockSpec((None, tr, tn), lambda i, j, q_ref: (lead, q_ref[1] * ni + i, j))
    else:
        out_spec = pl.BlockSpec((tr, tn), lambda i, j, q_ref: (q_ref[1] * ni + i, j))
    in_specs = [own, pl.BlockSpec((n_recv, tr, tn), lambda i, j, q_ref: (0, i, j))]
    s = own_src
    args = [jnp.stack([block_idx, c]).astype(jnp.int32), s, recv]
    aliases = {}
    if into is not None:
        in_specs.append(HBM_SPEC)
        args.append(into)
        aliases = {3: 0}
    return pl.pallas_call(
        body, name=name,
        grid_spec=pltpu.PrefetchScalarGridSpec(num_scalar_prefetch=1, grid=(ni, nj), in_specs=in_specs, out_specs=out_spec),
        out_shape=jax.ShapeDtypeStruct(shard_shape, F32), input_output_aliases=aliases,
        compiler_params=_cparams(("parallel", "parallel")),
    )(*args)


def _half_window(ref, h):
    rows = ref.shape[-2] // 2
    if ref.ndim == 3:
        return ref.at[:, pl.ds(h * rows, rows)]
    return ref.at[pl.ds(h * rows, rows)]


def _share_halves(grads, name):
    n = len(grads)

    def body(*refs):
        outs = refs[n:2 * n]
        send_sems, recv_sems = refs[2 * n:]
        x, y, c, _ = _place()
        cps = []
        for t in range(n):
            cp = pltpu.make_async_remote_copy(src_ref=_half_window(outs[t], c), dst_ref=_half_window(outs[t], c),
                                              send_sem=send_sems.at[t], recv_sem=recv_sems.at[t],
                                              device_id=(x, y, 1 - c), device_id_type=MESH)
            cp.start()
            cps.append(cp)
        for t in range(n):
            cps[t].wait_send()
            pltpu.make_async_remote_copy(src_ref=_half_window(outs[t], c), dst_ref=_half_window(outs[t], 1 - c),
                                         send_sem=send_sems.at[t], recv_sem=recv_sems.at[t],
                                         device_id=(x, y, 1 - c), device_id_type=MESH).wait_recv()

    return pl.pallas_call(
        body, name=name, in_specs=[HBM_SPEC] * n, out_specs=[HBM_SPEC] * n,
        out_shape=[jax.ShapeDtypeStruct(g.shape, F32) for g in grads],
        input_output_aliases={t: t for t in range(n)},
        scratch_shapes=[pltpu.SemaphoreType.DMA((n,)), pltpu.SemaphoreType.DMA((n,))],
    )(*grads)


def _adamw(w, g, m, v, name):
    R, W = w.shape
    tr = _pick(R, (512, 352, 256, 32))

    def body(w_ref, g_ref, m_ref, v_ref, d_ref, nm_ref, nv_ref):
        gv = g_ref[...]
        nm = ADAM_B1 * m_ref[...] + (1.0 - ADAM_B1) * gv
        nv = ADAM_B2 * v_ref[...] + (1.0 - ADAM_B2) * (gv * gv)
        m_hat = nm / (1.0 - ADAM_B1 ** ADAM_STEP)
        v_hat = nv / (1.0 - ADAM_B2 ** ADAM_STEP)
        d_ref[...] = -ADAM_LR * (m_hat / (jnp.sqrt(v_hat) + ADAM_EPS) + ADAM_WD * w_ref[...])
        nm_ref[...] = nm
        nv_ref[...] = nv

    blk = pl.BlockSpec((tr, W), lambda i: (i, 0))
    shp = jax.ShapeDtypeStruct((R, W), F32)
    return pl.pallas_call(
        body, name=name, grid=(R // tr,), in_specs=[blk] * 4, out_specs=[blk] * 3, out_shape=[shp] * 3,
        compiler_params=_cparams(("parallel",)),
    )(w, g, m, v)


SMALL_ROWS = 32


def _pack_small(ln_g, ln_b, sinks):
    rows = jnp.concatenate([ln_g.reshape(-1, 128), ln_b.reshape(-1, 128),
                            jnp.pad(sinks.reshape(1, -1), ((0, 0), (0, 128 - sinks.size)))], axis=0)
    return jnp.pad(rows, ((0, SMALL_ROWS - rows.shape[0]), (0, 0)))


def _unpack_small(s, ln_shape, sink_shape):
    n = ln_shape[0] * ln_shape[1] * ln_shape[2] // 128
    return s[:n].reshape(ln_shape), s[n:2 * n].reshape(ln_shape), s[2 * n, :sink_shape[1]].reshape(sink_shape)


def _ffn_fwd(xin, w_in, w_out, gain, bias, tag):
    u, h = _ffn_in(xin, w_in, "ffn_in_" + tag)
    y, yb, z = _mm_ln(h, w_out, xin, gain, bias, 0.5, "ffn_out_ln_" + tag)
    return y, yb, dict(u=u, h=h, z=z, xin=xin)


def _ffn_bwd(dy, saved, w_in, w_out, gain, xin_b, tag, dw_dtype=F32):
    dz, dzc, gg, gb = _ln_bwd(saved["z"], dy, gain, 0.5, "ln_bwd_" + tag)
    du = _ffn_bwd_h(dzc, w_out, saved["u"], "ffn_bwd_h_" + tag)
    d_w_out = _mm_tn(saved["h"], dzc, "ffn_dwout_" + tag, out_dtype=dw_dtype)
    d_w_in = _mm_tn(xin_b, du, "ffn_dwin_" + tag, out_dtype=dw_dtype)
    dx = _mm_nt(du, w_in, "ffn_dx_" + tag, add=dz, add_scale=ALPHA)
    return dx, d_w_in, d_w_out, gg, gb


def kernel(x, ffn1_w_in, ffn1_w_out, ffn2_w_in, ffn2_w_out, ln_g, ln_b, a_w_qkv, a_w_o, kv_w, b_w_q, b_sinks, b_w_o, loss_target, m_ffn1_w_in, m_ffn1_w_out, m_ffn2_w_in, m_ffn2_w_out, m_ln_g, m_ln_b, m_a_w_qkv, m_a_w_o, m_kv_w, m_b_w_q, m_b_sinks, m_b_w_o, v_ffn1_w_in, v_ffn1_w_out, v_ffn2_w_in, v_ffn2_w_out, v_ln_g, v_ln_b, v_a_w_qkv, v_a_w_o, v_kv_w, v_b_w_q, v_b_sinks, v_b_w_o):
    ws = dict(ffn1_w_in=ffn1_w_in, ffn1_w_out=ffn1_w_out, ffn2_w_in=ffn2_w_in, ffn2_w_out=ffn2_w_out, a_w_qkv=a_w_qkv,
              a_w_o=a_w_o, kv_w=kv_w, b_w_q=b_w_q, b_w_o=b_w_o)
    ms = dict(ffn1_w_in=m_ffn1_w_in, ffn1_w_out=m_ffn1_w_out, ffn2_w_in=m_ffn2_w_in, ffn2_w_out=m_ffn2_w_out,
              a_w_qkv=m_a_w_qkv, a_w_o=m_a_w_o, kv_w=m_kv_w, b_w_q=m_b_w_q, b_w_o=m_b_w_o)
    vs = dict(ffn1_w_in=v_ffn1_w_in, ffn1_w_out=v_ffn1_w_out, ffn2_w_in=v_ffn2_w_in, ffn2_w_out=v_ffn2_w_out,
              a_w_qkv=v_a_w_qkv, a_w_o=v_a_w_o, kv_w=v_kv_w, b_w_q=v_b_w_q, b_w_o=v_b_w_o)
    _, _, c_idx, myq = _place()
    xs = x[0]
    target = loss_target[0]

    shards = {(n, l): (ws[n] if l is None else ws[n][l]).astype(BF16) for n, l in LAYER0_ITEMS + LAYER1_ITEMS}

    def as_weights(items, arrays):
        return {n: (a.reshape(D_MODEL, a.shape[-1]) if a.ndim == 4 else a) for (n, _), a in zip(items, arrays)}

    full0, small = _all_gather(LAYER0_ITEMS, shards, _pack_small(ln_g, ln_b, b_sinks))
    gather_state, token = _gather_start(LAYER1_ITEMS, shards, small)

    def layer1_weights(after):
        return as_weights(LAYER1_ITEMS, _gather_wait(LAYER1_ITEMS, gather_state, after))

    n_ln = ln_g.size // 128
    lg = jnp.concatenate([small[q, :n_ln].reshape(DEPTH, 3, 1, -1) for q in range(N_CHIPS)], axis=-1)
    lb = jnp.concatenate([small[q, n_ln:2 * n_ln].reshape(DEPTH, 3, 1, -1) for q in range(N_CHIPS)], axis=-1)
    lg = lg + token[0, 0]
    reducer = _GradReducer(c_idx, myq, {n: ws[n].shape for n in BIG})
    sq, grad_x, _, gg, gb, dsink_part = _local_step(xs, target, as_weights(LAYER0_ITEMS, full0), layer1_weights,
                                                    lg, lb, b_sinks.reshape(N_HEADS), reducer.begin)

    loss_row = jnp.pad(jnp.sum(sq).reshape(1, 1), ((0, 0), (0, 127)))
    dsinks = jnp.pad(dsink_part[:, 0, :].reshape(N_SLABS, 2, HEAD_DIM)[:, :, 0].reshape(1, N_HEADS), ((0, 0), (0, 128 - N_HEADS)))
    gg_full = jnp.stack([jnp.stack([jnp.sum(gg[i][j], axis=0) for j in range(3)]) for i in range(DEPTH)])
    gb_full = jnp.stack([jnp.stack([jnp.sum(gb[i][j], axis=0) for j in range(3)]) for i in range(DEPTH)])
    small_in = jnp.concatenate([loss_row, dsinks, gg_full.reshape(-1, 128), gb_full.reshape(-1, 128)], axis=0)
    small_in = jnp.pad(small_in, ((0, (-small_in.shape[0]) % 8), (0, 0)))
    small_sum = _small_all_reduce(small_in)
    loss = small_sum[0, 0] * (0.5 / D_MODEL)
    grad_sinks = small_sum[1, :N_HEADS].reshape(b_sinks.shape)
    n_full = DEPTH * 3 * D_MODEL // 128
    cols = D_MODEL // N_CHIPS
    grad_ln_g = lax.dynamic_slice_in_dim(small_sum[2:2 + n_full].reshape(DEPTH, 3, D_MODEL), myq * cols, cols, axis=2)
    grad_ln_b = lax.dynamic_slice_in_dim(small_sum[2 + n_full:2 + 2 * n_full].reshape(DEPTH, 3, D_MODEL), myq * cols, cols, axis=2)
    return _update(reducer, grad_x, loss, grad_ln_g, grad_ln_b, grad_sinks, ws, ms, vs,
                   (ln_g, ln_b, b_sinks), (m_ln_g, m_ln_b, m_b_sinks), (v_ln_g, v_ln_b, v_b_sinks))


def _local_step(xs, target, W, layer1_weights, lg, lb, sinks, grads_ready=None):
    if grads_ready is None:
        grads_ready = lambda tag, grads, overlap: 0.0
    S = xs.shape[0]
    slopes = jnp.asarray(_alibi_slopes(N_HEADS))
    in1, out1, in2, out2 = [W["ffn1_w_in"]], [W["ffn1_w_out"]], [W["ffn2_w_in"]], [W["ffn2_w_out"]]

    y1, y1b, s1 = _ffn_fwd(xs, in1[0], out1[0], lg[0, 0], lb[0, 0], "a1")
    qkv_a = _mm_nn(y1b, W["a_w_qkv"], F32, "qkv_a", split=True)
    mix_a, o_a, lse_a = _attn_fwd(qkv_a, slopes, None, PATTERNS_A, "attn_a_fwd")
    y2, y2b, z2 = _mm_ln(mix_a, W["a_w_o"], y1, lg[0, 1], lb[0, 1], 1.0, "attn_a_out_ln")
    y3, y3b, s3 = _ffn_fwd(y2, in2[0], out2[0], lg[0, 2], lb[0, 2], "a2")
    kv_w_rep = jnp.broadcast_to(W["kv_w"].reshape(D_MODEL, 2, N_KV_B, 1, HEAD_DIM),
                                (D_MODEL, 2, N_KV_B, GROUP_B, HEAD_DIM)).reshape(D_MODEL, 2 * D_MODEL)
    kv_rep = _mm_nn(y3b, kv_w_rep, F32, "kv_proj", split=(1, 2))
    W = dict(W, **layer1_weights(kv_rep))
    in1, out1, in2, out2 = (in1 + [W["ffn1_w_in"]], out1 + [W["ffn1_w_out"]], in2 + [W["ffn2_w_in"]],
                            out2 + [W["ffn2_w_out"]])
    y4, y4b, s4 = _ffn_fwd(y3, in1[1], out1[1], lg[1, 0], lb[1, 0], "b1")
    qkv_b = _mm_nn(y4b, W["b_w_q"], F32, "q_b", split=(0, 1), into=kv_rep)
    mix_b, o_b, lse_b = _attn_fwd(qkv_b, slopes, sinks, PATTERNS_B, "attn_b_fwd")
    y5, y5b, z5 = _mm_ln(mix_b, W["b_w_o"], y4, lg[1, 1], lb[1, 1], 1.0, "attn_b_out_ln")
    y6, _, s6 = _ffn_fwd(y5, in2[1], out2[1], lg[1, 2], lb[1, 2], "b2")

    dy6, sq = _loss_grad(y6, target, "loss_grad")
    gr = {n: None for n in BIG}
    gg = [[None] * 3 for _ in range(DEPTH)]
    gb = [[None] * 3 for _ in range(DEPTH)]

    dy5, d_in2_b, d_out2_b, gg[1][2], gb[1][2] = _ffn_bwd(dy6, s6, in2[1], out2[1], lg[1, 2], y5b, "b2", BF16)
    dz5, dz5b, gg[1][1], gb[1][1] = _ln_bwd(z5, dy5, lg[1, 1], 1.0, "ln_bwd_attn_b")
    gr["b_w_o"] = _mm_tn(mix_b, dz5b, "d_b_w_o", out_dtype=BF16)
    dmix_b = _mm_nt(dz5b, W["b_w_o"], "d_mix_b")
    dqkv_b, dsink_part = _attn_bwd(qkv_b, dmix_b, o_b, lse_b, slopes, sinks, PATTERNS_B, "attn_b_bwd")
    dq_b = (dqkv_b, 0)
    gr["b_w_q"] = _mm_tn(y4b, dq_b, "d_b_w_q", out_dtype=BF16)
    dy4 = _mm_nt(dq_b, W["b_w_q"], "d_y4", add=dz5, add_scale=ALPHA)
    dy3, d_in1_b, d_out1_b, gg[1][0], gb[1][0] = _ffn_bwd(dy4, s4, in1[1], out1[1], lg[1, 0], y3b, "b1", BF16)
    d_kv_w_rep = _mm_tn(y3b, dqkv_b, "d_kv_w", split=(1, 2))
    gr["kv_w"] = d_kv_w_rep.reshape(D_MODEL, 2, N_KV_B, GROUP_B, HEAD_DIM).sum(axis=3).reshape(D_MODEL, -1).astype(BF16)
    dy3 = _mm_nt(dqkv_b, kv_w_rep, "d_y3_kv", add=dy3, add_scale=1.0, split=(1, 2))
    tok = grads_ready("l1", {("ffn2_w_in", 1): d_in2_b, ("ffn2_w_out", 1): d_out2_b, ("b_w_o", None): gr["b_w_o"],
                             ("b_w_q", None): gr["b_w_q"], ("ffn1_w_in", 1): d_in1_b, ("ffn1_w_out", 1): d_out1_b,
                             ("kv_w", None): gr["kv_w"]}, True)
    lg0 = lg[0] + tok

    dy2, d_in2_a, d_out2_a, gg[0][2], gb[0][2] = _ffn_bwd(dy3, s3, in2[0], out2[0], lg0[2], y2b, "a2", BF16)
    tok = grads_ready("a2", {("ffn2_w_in", 0): d_in2_a, ("ffn2_w_out", 0): d_out2_a}, True)
    lg0 = lg0 + tok
    dz2, dz2b, gg[0][1], gb[0][1] = _ln_bwd(z2, dy2, lg0[1], 1.0, "ln_bwd_attn_a")
    gr["a_w_o"] = _mm_tn(mix_a, dz2b, "d_a_w_o", out_dtype=BF16)
    dmix_a = _mm_nt(dz2b, W["a_w_o"], "d_mix_a")
    dqkv_a, _ = _attn_bwd(qkv_a, dmix_a, o_a, lse_a, slopes, None, PATTERNS_A, "attn_a_bwd")
    gr["a_w_qkv"] = _mm_tn(y1b, dqkv_a, "d_a_w_qkv", split=True, out_dtype=BF16)
    tok = grads_ready("mix", {("a_w_o", None): gr["a_w_o"], ("a_w_qkv", None): gr["a_w_qkv"]}, True)
    lg0 = lg0 + tok
    dy1 = _mm_nt(dqkv_a, W["a_w_qkv"], "d_y1", add=dz2, add_scale=ALPHA, split=True)
    grad_x, d_in1_a, d_out1_a, gg[0][0], gb[0][0] = _ffn_bwd(dy1, s1, in1[0], out1[0], lg0[0], xs, "a1")
    grads_ready("a1", {("ffn1_w_in", 0): d_in1_a, ("ffn1_w_out", 0): d_out1_a}, False)
    gr["ffn1_w_in"] = [d_in1_a, d_in1_b]
    gr["ffn1_w_out"] = [d_out1_a, d_out1_b]
    gr["ffn2_w_in"] = [d_in2_a, d_in2_b]
    gr["ffn2_w_out"] = [d_out2_a, d_out2_b]
    return sq, grad_x, gr, gg, gb, dsink_part


def _grad_item(name, layer, g):
    if name.endswith("w_in"):
        return (g, "col", HALF_FF, _slot, name, layer)
    if name.endswith("w_out"):
        return (g, "row", D_MODEL, None, name, layer)
    if name == "a_w_qkv":
        return (g, "col", QKV_SHARD, lambda q: q, name, None)
    return (g, "row", g.shape[1], None, name, None)


class _GradReducer:
    def __init__(self, c_idx, myq, shard_shapes):
        self.c_idx, self.myq, self.shard_shapes = c_idx, myq, shard_shapes
        self.groups = []

    def begin(self, tag, grads, overlap):
        items = [_grad_item(n, l, g) for (n, l), g in grads.items()]
        kinds, widths, colblocks = [it[1] for it in items], [it[2] for it in items], [it[3] for it in items]
        views = [_grad_view(k, it[0]) for k, it in zip(kinds, items)]
        if overlap:
            lands = [jax.ShapeDtypeStruct((N_DIRECT,) + _piece_shape(k, w, _half_shape(k, v.shape)), BF16)
                     for k, w, v in zip(kinds, widths, views)]
            state, token = _split_start("grad_direct_start_" + tag, _direct_copies(kinds, widths, colblocks), 10 * len(items),
                                        views, lands, views[-1])
            self.groups.append((tag, items, None, state))
            return token[0, 0]
        from_sibling = _pair_exchange(views, kinds, "grad_pair_exchange_" + tag)
        sums = [_pair_sum(k, v, r, self.c_idx, "pair_sum_%s_%d" % (tag, t))
                for t, (k, v, r) in enumerate(zip(kinds, views, from_sibling))]
        self.groups.append((tag, items, sums, None))
        return 0.0

    def _sum_group(self, tag, items, sums, received, direct):
        for t, (it, s, r) in enumerate(zip(items, sums, received)):
            _, k, _, cb, name, layer = it
            own = cb(self.myq) if k == "col" else self.myq
            self.half_done[name] = _chip_sum(k, s, r, own, self.c_idx, self.shard_shapes[name], layer,
                                             self.half_done.get(name), "chip_sum_%s_%d" % (tag, t), direct=direct)

    def finish_first(self, after):
        self.half_done, self.late = {}, []
        for tag, items, sums, state in self.groups:
            if state is None:
                kinds, widths, colblocks = [it[1] for it in items], [it[2] for it in items], [it[3] for it in items]
                copies = _chip_copies(kinds, widths, colblocks)
                st, _ = _split_start("grad_chip_start_" + tag, copies, 3 * len(items), sums,
                                     _chip_land_shapes(sums, kinds, widths), sums[-1])
                self.late.append((tag, items, copies, st))
        for tag, items, sums, state in self.groups:
            if state is not None:
                kinds, widths, colblocks = [it[1] for it in items], [it[2] for it in items], [it[3] for it in items]
                views, received = _split_wait("grad_direct_wait_" + tag, _direct_copies(kinds, widths, colblocks), state, after)
                self._sum_group(tag, items, views, received, True)
        late_names = {it[4] for _, items, _, _ in self.late for it in items}
        names = [n for n in BIG if n not in late_names]
        return dict(zip(names, _share_halves([self.half_done[n] for n in names], "grad_share_halves_first")))

    def finish_rest(self, after):
        names = []
        for tag, items, copies, st in self.late:
            sums, received = _split_wait("grad_chip_wait_" + tag, copies, st, after)
            self._sum_group(tag, items, sums, received, False)
            names += [it[4] for it in items if it[4] not in names]
        return dict(zip(names, _share_halves([self.half_done[n] for n in names], "grad_share_halves_rest")))


def _update(reducer, grad_x, loss, grad_ln_g, grad_ln_b, grad_sinks, ws, ms, vs, small_w, small_m, small_v):
    ln_g, ln_b, b_sinks = small_w
    m_ln_g, m_ln_b, m_b_sinks = small_m
    v_ln_g, v_ln_b, v_b_sinks = small_v

    deltas, new_m, new_v = {}, {}, {}

    def update(some):
        for name in some:
            shp = ws[name].shape
            flat = lambda a: a.reshape(-1, shp[-1])
            d, nm, nv = _adamw(flat(ws[name]), flat(some[name]), flat(ms[name]), flat(vs[name]), "adamw_" + name)
            deltas[name], new_m[name], new_v[name] = d.reshape(shp), nm.reshape(shp), nv.reshape(shp)
        return d

    grads = reducer.finish_first(grad_x)
    last = update(grads)
    rest = reducer.finish_rest(last)
    update(rest)
    grads.update(rest)
    delta_s, nm_s, nv_s = _adamw(_pack_small(ln_g, ln_b, b_sinks), _pack_small(grad_ln_g, grad_ln_b, grad_sinks),
                                 _pack_small(m_ln_g, m_ln_b, m_b_sinks), _pack_small(v_ln_g, v_ln_b, v_b_sinks), "adamw_small")
    for d, blob in ((grads, None), (deltas, delta_s), (new_m, nm_s), (new_v, nv_s)):
        if blob is None:
            d["ln_g"], d["ln_b"], d["b_sinks"] = grad_ln_g, grad_ln_b, grad_sinks
        else:
            d["ln_g"], d["ln_b"], d["b_sinks"] = _unpack_small(blob, ln_g.shape, b_sinks.shape)

    order = ("ffn1_w_in", "ffn1_w_out", "ffn2_w_in", "ffn2_w_out", "ln_g", "ln_b", "a_w_qkv", "a_w_o", "kv_w", "b_w_q",
             "b_sinks", "b_w_o")
    outs = [loss, grad_x[None]]
    for d in (grads, deltas, new_m, new_v):
        outs += [d[n] for n in order]
    return tuple(outs)
```

```python
import numpy as np
import jax
import jax.numpy as jnp
from jax import lax
from jax.experimental import pallas as pl
from jax.experimental.pallas import tpu as pltpu

F32 = jnp.float32
BF16 = jnp.bfloat16

D_MODEL = 1024
D_FF = 2816
HALF_FF = D_FF // 2
HEAD_DIM = 64
N_HEADS = 16
N_KV_B = 4
GROUP_B = N_HEADS // N_KV_B
DEPTH = 2
ALPHA = (2.0 * DEPTH) ** 0.25
LN_EPS = 1e-5
BLOCK = 128
SLAB = 128
N_SLABS = D_MODEL // SLAB
PATTERNS_A = ((1, 128, 1.0), (4, 128, 4.0), (16, 128, 16.0))
PATTERNS_B = ((1, 127, 1.0),)
NEG = -1e30

ADAM_LR = 0.001
ADAM_B1 = 0.9
ADAM_B2 = 0.999
ADAM_EPS = 1e-08
ADAM_WD = 0.01
ADAM_STEP = 10

N_CHIPS = 4
VMEM_LIMIT = 56 * 1024 * 1024
MESH = pl.DeviceIdType.MESH


def _alibi_slopes(n):
    return np.array([2.0 ** (-8.0 * (h + 1) / n) for h in range(n)], dtype=np.float32)


def _cparams(sem=None, vmem=VMEM_LIMIT):
    return pltpu.CompilerParams(dimension_semantics=sem, vmem_limit_bytes=vmem)


_DIMS = {"nn": ((1,), (0,)), "nt": ((1,), (1,)), "tn": ((0,), (0,))}


def _unlead(x):
    if isinstance(x, tuple):
        return x[0], x[1], x[0].shape[1:]
    return x, None, x.shape


def _bspec(block, imap, lead=None):
    if lead is None:
        return pl.BlockSpec(block, imap)
    return pl.BlockSpec((None,) + tuple(block), lambda *g: (lead,) + tuple(imap(*g)))


def _matmul(a, b, mode, out_dtype, tm, tn, tk, name, add=None, add_scale=1.0, split=False, into=None):
    out_spec = pl.BlockSpec((tm, tn), lambda i, j, k: (i, j))
    base, count = (0, 3) if split is True else (split or (0, 0))
    if mode == "nn":
        a, al, (M, K) = _unlead(a)
        b, bl, (K2, N) = _unlead(b)
        a_spec = _bspec((tm, tk), lambda i, j, k: (i, k), al)
        b_spec = _bspec((tk, tn), lambda i, j, k: (k, j), bl)
        out_struct = jax.ShapeDtypeStruct((M, N), out_dtype)
        if split:
            assert tn == D_MODEL and N == count * tn
            out_spec = pl.BlockSpec((None, tm, tn), lambda i, j, k: (j + base, i, 0))
            out_struct = jax.ShapeDtypeStruct((3, M, tn), out_dtype)
    elif mode == "nt":
        b, bl, (N, K2) = _unlead(b)
        if split:
            assert tk == D_MODEL
            M, K = a.shape[1], count * a.shape[2]
            a_spec = pl.BlockSpec((None, tm, tk), lambda i, j, k: (k + base, i, 0))
        else:
            a, al, (M, K) = _unlead(a)
            a_spec = _bspec((tm, tk), lambda i, j, k: (i, k), al)
        b_spec = _bspec((tn, tk), lambda i, j, k: (j, k), bl)
        out_struct = jax.ShapeDtypeStruct((M, N), out_dtype)
    else:
        a, al, (K, M) = _unlead(a)
        if split:
            assert tn == D_MODEL
            K2, N = b.shape[1], count * b.shape[2]
            b_spec = pl.BlockSpec((None, tk, tn), lambda i, j, k: (j + base, k, 0))
        else:
            b, bl, (K2, N) = _unlead(b)
            b_spec = _bspec((tk, tn), lambda i, j, k: (k, j), bl)
        a_spec = _bspec((tk, tm), lambda i, j, k: (k, i), al)
        out_struct = jax.ShapeDtypeStruct((M, N), out_dtype)
    assert K == K2 and M % tm == 0 and N % tn == 0 and K % tk == 0, (a.shape, b.shape, mode, tm, tn, tk)
    nk = K // tk
    dims = (_DIMS[mode], ((), ()))
    has_add = add is not None

    narrow = out_dtype != F32
    assert not (narrow and has_add)

    def body(*refs):
        if into is not None:
            refs = refs[:2] + refs[3:]
        if has_add:
            a_ref, b_ref, add_ref, o_ref = refs
            acc_ref = o_ref
        elif narrow:
            a_ref, b_ref, o_ref, acc_ref = refs
        else:
            a_ref, b_ref, o_ref = refs
            acc_ref = o_ref
        k = pl.program_id(2)
        part = lax.dot_general(a_ref[...].astype(BF16), b_ref[...].astype(BF16), dims, preferred_element_type=F32)
        if has_add:
            @pl.when(k == 0)
            def _():
                acc_ref[...] = part + add_scale * add_ref[...]
        else:
            @pl.when(k == 0)
            def _():
                acc_ref[...] = part

        @pl.when(k > 0)
        def _():
            acc_ref[...] += part

        if narrow:
            @pl.when(k == nk - 1)
            def _():
                o_ref[...] = acc_ref[...].astype(out_dtype)

    in_specs = [a_spec, b_spec]
    args = [a, b]
    aliases = {}
    if into is not None:
        assert mode == "nn" and split and not has_add
        in_specs.append(pl.BlockSpec(memory_space=pl.ANY))
        args.append(into)
        aliases = {2: 0}
    if has_add:
        in_specs.append(pl.BlockSpec((tm, tn), lambda i, j, k: (i, j)))
        args.append(add)
    return pl.pallas_call(
        body, name=name, grid=(M // tm, N // tn, nk),
        in_specs=in_specs, out_specs=out_spec, out_shape=out_struct, input_output_aliases=aliases,
        scratch_shapes=[pltpu.VMEM((tm, tn), F32)] if narrow else [],
        compiler_params=_cparams(("parallel", "parallel", "arbitrary")),
    )(*args)


def _pick(n, cands):
    for c in cands:
        if n % c == 0:
            return c
    raise ValueError((n, cands))


def _mm_nn(a, b, out_dtype, name, split=False, into=None):
    M, K = _unlead(a)[2]
    N = _unlead(b)[2][1]
    return _matmul(a, b, "nn", out_dtype, _pick(M, (1024, 512, 256)), _pick(N, (1024, 512)), _pick(K, (1024, 512)), name,
                   split=split, into=into)


def _mm_nt(a, b, name, add=None, add_scale=1.0, split=False):
    M, K = (a.shape[1], D_MODEL) if split else _unlead(a)[2]
    N = _unlead(b)[2][0]
    return _matmul(a, b, "nt", F32, _pick(M, (1024, 512, 256)), _pick(N, (1024, 512)),
                   _pick(K, (2816, 1024, 512)), name, add=add, add_scale=add_scale, split=split)


def _mm_tn(a, b, name, split=False, out_dtype=F32):
    K, M = _unlead(a)[2]
    N = D_MODEL if split else _unlead(b)[2][1]
    return _matmul(a, b, "tn", out_dtype, _pick(M, (1024, 1408, 512)), _pick(N, (1408, 1024, 512)),
                   _pick(K, (2048, 1024, 512, 256)), name, split=split)


def _ffn_in(x, w, name):
    S = x.shape[0]
    tm = _pick(S, (512, 256))
    w, wl, _ = _unlead(w)

    def body(x_ref, w_ref, t_ref, h_ref):
        acc = jnp.dot(x_ref[...].astype(BF16), w_ref[...], preferred_element_type=F32)
        g = acc[:, :HALF_FF]
        up = acc[:, HALF_FF:]
        sg = jax.nn.sigmoid(g)
        silu = g * sg
        t_ref[:, :HALF_FF] = (up * (sg * (1.0 + g * (1.0 - sg)))).astype(BF16)
        t_ref[:, HALF_FF:] = silu.astype(BF16)
        h_ref[...] = (silu * up).astype(BF16)

    return pl.pallas_call(
        body, name=name, grid=(2, S // tm),
        in_specs=[pl.BlockSpec((tm, D_MODEL), lambda j, i: (i, 0)),
                  _bspec((D_MODEL, D_FF), lambda j, i: (0, j), wl)],
        out_specs=[pl.BlockSpec((tm, D_FF), lambda j, i: (i, j)),
                   pl.BlockSpec((tm, HALF_FF), lambda j, i: (i, j))],
        out_shape=[jax.ShapeDtypeStruct((S, 2 * D_FF), BF16), jax.ShapeDtypeStruct((S, D_FF), BF16)],
        compiler_params=_cparams(("parallel", "parallel")),
    )(x, w)


def _ffn_bwd_h(dzc, w_out, u, name):
    S = dzc.shape[0]
    tm = _pick(S, (512, 256))
    w_out, wl, _ = _unlead(w_out)

    def body(dz_ref, w_ref, t_ref, du_ref):
        dh = lax.dot_general(dz_ref[...], w_ref[...], (((1,), (1,)), ((), ())), preferred_element_type=F32)
        du_ref[:, :HALF_FF] = (dh * t_ref[:, :HALF_FF].astype(F32)).astype(BF16)
        du_ref[:, HALF_FF:] = (dh * t_ref[:, HALF_FF:].astype(F32)).astype(BF16)

    return pl.pallas_call(
        body, name=name, grid=(2, S // tm),
        in_specs=[pl.BlockSpec((tm, D_MODEL), lambda j, i: (i, 0)),
                  _bspec((HALF_FF, D_MODEL), lambda j, i: (j, 0), wl),
                  pl.BlockSpec((tm, D_FF), lambda j, i: (i, j))],
        out_specs=pl.BlockSpec((tm, D_FF), lambda j, i: (i, j)),
        out_shape=jax.ShapeDtypeStruct((S, 2 * D_FF), BF16),
        compiler_params=_cparams(("parallel", "parallel")),
    )(dzc, w_out, u)


def _mm_ln(a, w, resid, gain, bias, c, name):
    S, K = a.shape
    tm = _pick(S, (512, 256))
    w, wl, _ = _unlead(w)

    def body(a_ref, w_ref, r_ref, g_ref, b_ref, y_ref, yb_ref, z_ref):
        z = ALPHA * r_ref[...] + c * jnp.dot(a_ref[...], w_ref[...], preferred_element_type=F32)
        mu = jnp.mean(z, axis=-1, keepdims=True)
        zc = z - mu
        var = jnp.mean(zc * zc, axis=-1, keepdims=True)
        y = zc * lax.rsqrt(var + LN_EPS) * g_ref[...] + b_ref[...]
        z_ref[...] = z
        y_ref[...] = y
        yb_ref[...] = y.astype(BF16)

    row = pl.BlockSpec((tm, D_MODEL), lambda i: (i, 0))
    vec = pl.BlockSpec((1, D_MODEL), lambda i: (0, 0))
    return pl.pallas_call(
        body, name=name, grid=(S // tm,),
        in_specs=[pl.BlockSpec((tm, K), lambda i: (i, 0)), _bspec((K, D_MODEL), lambda i: (0, 0), wl), row, vec, vec],
        out_specs=[row, row, row],
        out_shape=[jax.ShapeDtypeStruct((S, D_MODEL), F32), jax.ShapeDtypeStruct((S, D_MODEL), BF16),
                   jax.ShapeDtypeStruct((S, D_MODEL), F32)],
        compiler_params=_cparams(("parallel",)),
    )(a, w, resid, gain, bias)


def _ln_bwd(z, dy, gain, c, name):
    S = z.shape[0]
    tm = _pick(S, (512, 256))

    def body(z_ref, dy_ref, g_ref, dz_ref, dzc_ref, gg_ref, gb_ref):
        i = pl.program_id(0)
        zv = z_ref[...]
        dyv = dy_ref[...]
        mu = jnp.mean(zv, axis=-1, keepdims=True)
        zc = zv - mu
        var = jnp.mean(zc * zc, axis=-1, keepdims=True)
        rstd = lax.rsqrt(var + LN_EPS)
        xhat = zc * rstd
        dyg = dyv * g_ref[...]
        m1 = jnp.mean(dyg, axis=-1, keepdims=True)
        m2 = jnp.mean(dyg * xhat, axis=-1, keepdims=True)
        dz = rstd * (dyg - m1 - xhat * m2)
        dz_ref[...] = dz
        dzc_ref[...] = (c * dz).astype(BF16)
        pg = jnp.sum((dyv * xhat).reshape(tm // 8, 8, D_MODEL), axis=0)
        pb = jnp.sum(dyv.reshape(tm // 8, 8, D_MODEL), axis=0)

        @pl.when(i == 0)
        def _():
            gg_ref[...] = pg
            gb_ref[...] = pb

        @pl.when(i > 0)
        def _():
            gg_ref[...] += pg
            gb_ref[...] += pb

    row = pl.BlockSpec((tm, D_MODEL), lambda i: (i, 0))
    part = pl.BlockSpec((8, D_MODEL), lambda i: (0, 0))
    return pl.pallas_call(
        body, name=name, grid=(S // tm,),
        in_specs=[row, row, pl.BlockSpec((1, D_MODEL), lambda i: (0, 0))],
        out_specs=[row, row, part, part],
        out_shape=[jax.ShapeDtypeStruct((S, D_MODEL), F32), jax.ShapeDtypeStruct((S, D_MODEL), BF16),
                   jax.ShapeDtypeStruct((8, D_MODEL), F32), jax.ShapeDtypeStruct((8, D_MODEL), F32)],
        compiler_params=_cparams(("arbitrary",)),
    )(z, dy, gain)


def _loss_grad(y, t, name):
    S = y.shape[0]
    tm = _pick(S, (512, 256))

    def body(y_ref, t_ref, dy_ref, sq_ref):
        i = pl.program_id(0)
        e = y_ref[...] - t_ref[...]
        dy_ref[...] = e * (1.0 / D_MODEL)
        ps = jnp.sum((e * e).reshape(tm // 8, 8, D_MODEL), axis=0)

        @pl.when(i == 0)
        def _():
            sq_ref[...] = ps

        @pl.when(i > 0)
        def _():
            sq_ref[...] += ps

    row = pl.BlockSpec((tm, D_MODEL), lambda i: (i, 0))
    return pl.pallas_call(
        body, name=name, grid=(S // tm,),
        in_specs=[row, row], out_specs=[row, pl.BlockSpec((8, D_MODEL), lambda i: (0, 0))],
        out_shape=[jax.ShapeDtypeStruct((S, D_MODEL), F32), jax.ShapeDtypeStruct((8, D_MODEL), F32)],
        compiler_params=_cparams(("arbitrary",)),
    )(y, t)


def _rows(start, d):
    if d == 1:
        return pl.ds(pl.multiple_of(start, BLOCK), BLOCK)
    return pl.ds(start, BLOCK, stride=d)


def _ld(ref, start, d):
    return ref[_rows(start, d), :]


def _ld3(ref, lead, start, d):
    return ref[lead, _rows(start, d), :]


def _st3(ref, lead, start, d, val):
    ref[lead, _rows(start, d), :] = val


def _acc3(ref, lead, start, d, val):
    ref[lead, _rows(start, d), :] = ref[lead, _rows(start, d), :] + val


def _band_consts(slope0, slope1, maxd, scale):
    row = lax.broadcasted_iota(jnp.int32, (2 * BLOCK, 2 * BLOCK), 0)
    kj = lax.broadcasted_iota(jnp.int32, (2 * BLOCK, 2 * BLOCK), 1)
    top = row < BLOCK
    dist = BLOCK + jnp.where(top, row, row - BLOCK) - kj
    slope = jnp.where(top, slope0, slope1)
    base = jnp.where((dist >= 0) & (dist <= maxd), -(slope * (dist.astype(F32) * scale)), NEG)
    return base, kj < BLOCK


def _stack_heads(x, lo):
    return jnp.concatenate([jnp.where(lo, x, 0.0), jnp.where(lo, 0.0, x)], axis=0)


def _unstack_heads(x2, lo):
    return jnp.where(lo, x2[:BLOCK], x2[BLOCK:])


def _scores(q2, k2, base, prev_keys, first):
    s = lax.dot_general(q2, k2, (((1,), (1,)), ((), ())), preferred_element_type=F32) * (HEAD_DIM ** -0.5) + base
    return jnp.where(jnp.logical_and(prev_keys, first), NEG, s)


def _softmax_weights(ls):
    mx = ls[0]
    for l in ls[1:]:
        mx = jnp.maximum(mx, l)
    es = [jnp.exp(l - mx) for l in ls]
    tot = es[0]
    for e in es[1:]:
        tot = tot + e
    inv = 1.0 / tot
    return [e * inv for e in es]


def _attn_fwd(qkv, slopes, sinks, patterns, name):
    S = qkv.shape[1]
    npat = len(patterns)
    has_sink = sinks is not None
    if not has_sink:
        sinks = jnp.zeros((N_HEADS,), F32)
    rows_c = 256

    def body(slopes_ref, sinks_ref, x_ref, mix_ref, o_ref, lse_ref, o_scr, lse_scr):
        p = pl.program_id(0)
        lo = lax.broadcasted_iota(jnp.int32, (BLOCK, SLAB), 1) < HEAD_DIM
        top1 = lax.broadcasted_iota(jnp.int32, (2 * BLOCK, 1), 0) < BLOCK
        sk2 = jnp.where(top1, sinks_ref[2 * p], sinks_ref[2 * p + 1])
        for pi, (d, maxd, scale) in enumerate(patterns):
            nb = S // d // BLOCK
            base, prev_keys = _band_consts(slopes_ref[2 * p], slopes_ref[2 * p + 1], maxd, scale)

            def blk(t, carry, pi=pi, d=d, nb=nb, base=base, prev_keys=prev_keys):
                r = t // nb
                n = t - r * nb
                start = r + (d * BLOCK) * n
                prev = jnp.where(n > 0, start - d * BLOCK, start)
                q2 = _stack_heads(_ld3(x_ref, 0, start, d), lo).astype(BF16)
                k2 = jnp.concatenate([_ld3(x_ref, 1, prev, d), _ld3(x_ref, 1, start, d)], axis=0).astype(BF16)
                v2 = jnp.concatenate([_ld3(x_ref, 2, prev, d), _ld3(x_ref, 2, start, d)], axis=0).astype(BF16)
                s = _scores(q2, k2, base, prev_keys, n == 0)
                m = jnp.max(s, axis=-1, keepdims=True)
                if has_sink:
                    m = jnp.maximum(m, sk2)
                e = jnp.exp(s - m)
                den = jnp.sum(e, axis=-1, keepdims=True)
                if has_sink:
                    den = den + jnp.exp(sk2 - m)
                o2 = jnp.dot((e / den).astype(BF16), v2, preferred_element_type=F32)
                _st3(o_scr, pi, start, d, _unstack_heads(o2, lo))
                _st3(lse_scr, pi, start, d, _unstack_heads(m + jnp.log(den), lo))
                return carry

            lax.fori_loop(0, d * nb, blk, 0, unroll=8)

        lane_c = lax.broadcasted_iota(jnp.int32, (rows_c, SLAB), 1)

        def comb(ci, carry):
            rows = pl.ds(pl.multiple_of(ci * rows_c, rows_c), rows_c)
            ls = [lse_scr[i, rows, :] for i in range(npat)]
            packed = jnp.zeros((rows_c, SLAB), F32)
            for i in range(npat):
                o_ref[i, rows, :] = o_scr[i, rows, :].astype(BF16)
                packed = jnp.where(lane_c == 2 * i, ls[i][:, :1], packed)
                packed = jnp.where(lane_c == 2 * i + 1, ls[i][:, HEAD_DIM:HEAD_DIM + 1], packed)
            lse_ref[rows, :] = packed
            if npat == 1:
                mix_ref[rows, :] = o_scr[0, rows, :].astype(BF16)
            else:
                ws = _softmax_weights(ls)
                acc = ws[0] * o_scr[0, rows, :]
                for i in range(1, npat):
                    acc = acc + ws[i] * o_scr[i, rows, :]
                mix_ref[rows, :] = acc.astype(BF16)
            return carry

        lax.fori_loop(0, S // rows_c, comb, 0)

    smem = pl.BlockSpec(memory_space=pltpu.SMEM)
    return pl.pallas_call(
        body, name=name, grid=(N_SLABS,),
        in_specs=[smem, smem, pl.BlockSpec((3, S, SLAB), lambda p: (0, 0, p))],
        out_specs=[pl.BlockSpec((S, SLAB), lambda p: (0, p)), pl.BlockSpec((npat, S, SLAB), lambda p: (0, 0, p)),
                   pl.BlockSpec((None, S, SLAB), lambda p: (p, 0, 0))],
        out_shape=[jax.ShapeDtypeStruct((S, D_MODEL), BF16), jax.ShapeDtypeStruct((npat, S, D_MODEL), BF16),
                   jax.ShapeDtypeStruct((N_SLABS, S, SLAB), F32)],
        scratch_shapes=[pltpu.VMEM((npat, S, SLAB), F32), pltpu.VMEM((npat, S, SLAB), F32)],
        compiler_params=_cparams(("arbitrary",)),
    )(slopes, sinks, qkv)


def _attn_bwd(qkv, dout, o, lse, slopes, sinks, patterns, name):
    S = qkv.shape[1]
    npat = len(patterns)
    has_sink = sinks is not None
    if not has_sink:
        sinks = jnp.zeros((N_HEADS,), F32)
    rows_c = 256

    def headsum(x, lo):
        s0 = jnp.sum(jnp.where(lo, x, 0.0), axis=-1, keepdims=True)
        s1 = jnp.sum(jnp.where(lo, 0.0, x), axis=-1, keepdims=True)
        return jnp.where(lo, s0, s1)

    def body(slopes_ref, sinks_ref, x_ref, do_ref, o_ref, lsep_ref, dxo_ref, dsink_ref, dbar_ref, sacc_ref, lse_ref, dx_ref):
        p = pl.program_id(0)
        lo = lax.broadcasted_iota(jnp.int32, (BLOCK, SLAB), 1) < HEAD_DIM
        lo_c = lax.broadcasted_iota(jnp.int32, (rows_c, SLAB), 1) < HEAD_DIM
        top1 = lax.broadcasted_iota(jnp.int32, (2 * BLOCK, 1), 0) < BLOCK
        sk2 = jnp.where(top1, sinks_ref[2 * p], sinks_ref[2 * p + 1])

        def prep(ci, carry):
            rows = pl.ds(pl.multiple_of(ci * rows_c, rows_c), rows_c)
            dov = do_ref[rows, :]
            dx_ref[:, rows, :] = jnp.zeros((3, rows_c, SLAB), F32)
            packed = lsep_ref[rows, :]
            ls = [jnp.where(lo_c, packed[:, 2 * i:2 * i + 1], packed[:, 2 * i + 1:2 * i + 2]) for i in range(npat)]
            for i in range(npat):
                lse_ref[i, rows, :] = ls[i]
            if npat == 1:
                dbar_ref[rows, :] = headsum(dov * o_ref[0, rows, :].astype(F32), lo_c)
            else:
                ws = _softmax_weights(ls)
                acc = ws[0] * headsum(dov * o_ref[0, rows, :].astype(F32), lo_c)
                for i in range(1, npat):
                    acc = acc + ws[i] * headsum(dov * o_ref[i, rows, :].astype(F32), lo_c)
                dbar_ref[rows, :] = acc
            return carry

        lax.fori_loop(0, S // rows_c, prep, 0)
        sacc_ref[...] = jnp.zeros((BLOCK, SLAB), F32)

        for pi, (d, maxd, scale) in enumerate(patterns):
            nb = S // d // BLOCK
            base, prev_keys = _band_consts(slopes_ref[2 * p], slopes_ref[2 * p + 1], maxd, scale)

            def blk(t, carry, pi=pi, d=d, nb=nb, base=base, prev_keys=prev_keys):
                r = t // nb
                n = t - r * nb
                start = r + (d * BLOCK) * n
                prev = jnp.where(n > 0, start - d * BLOCK, start)
                q2 = _stack_heads(_ld3(x_ref, 0, start, d), lo).astype(BF16)
                k2 = jnp.concatenate([_ld3(x_ref, 1, prev, d), _ld3(x_ref, 1, start, d)], axis=0).astype(BF16)
                v2 = jnp.concatenate([_ld3(x_ref, 2, prev, d), _ld3(x_ref, 2, start, d)], axis=0).astype(BF16)
                ls = [_ld3(lse_ref, i, start, d) for i in range(npat)]
                w = _softmax_weights(ls)[pi] if npat > 1 else 1.0
                do2 = _stack_heads(w * _ld(do_ref, start, d), lo).astype(BF16)
                dl = w * _ld(dbar_ref, start, d)
                lse2 = jnp.concatenate([ls[pi][:, :1], ls[pi][:, HEAD_DIM:HEAD_DIM + 1]], axis=0)
                dl2 = jnp.concatenate([dl[:, :1], dl[:, HEAD_DIM:HEAD_DIM + 1]], axis=0)
                s = _scores(q2, k2, base, prev_keys, n == 0)
                pr = jnp.exp(s - lse2)
                dp = lax.dot_general(do2, v2, (((1,), (1,)), ((), ())), preferred_element_type=F32)
                ds = (pr * (dp - dl2) * (HEAD_DIM ** -0.5)).astype(BF16)
                dq2 = jnp.dot(ds, k2, preferred_element_type=F32)
                dk2 = lax.dot_general(ds, q2, (((0,), (0,)), ((), ())), preferred_element_type=F32)
                dv2 = lax.dot_general(pr.astype(BF16), do2, (((0,), (0,)), ((), ())), preferred_element_type=F32)
                _acc3(dx_ref, 0, start, d, _unstack_heads(dq2, lo))
                _acc3(dx_ref, 1, prev, d, dk2[:BLOCK])
                _acc3(dx_ref, 1, start, d, dk2[BLOCK:])
                _acc3(dx_ref, 2, prev, d, dv2[:BLOCK])
                _acc3(dx_ref, 2, start, d, dv2[BLOCK:])
                if has_sink:
                    sacc_ref[...] += _unstack_heads(-jnp.exp(sk2 - lse2) * dl2, lo)
                return carry

            lax.fori_loop(0, d * nb, blk, 0, unroll=4)

        dsink_ref[...] = jnp.broadcast_to(jnp.sum(sacc_ref[...], axis=0, keepdims=True), (8, SLAB))

        def emit(ci, carry):
            rows = pl.ds(pl.multiple_of(ci * rows_c, rows_c), rows_c)
            dxo_ref[:, rows, :] = dx_ref[:, rows, :].astype(BF16)
            return carry

        lax.fori_loop(0, S // rows_c, emit, 0)

    smem = pl.BlockSpec(memory_space=pltpu.SMEM)
    return pl.pallas_call(
        body, name=name, grid=(N_SLABS,),
        in_specs=[smem, smem, pl.BlockSpec((3, S, SLAB), lambda p: (0, 0, p)), pl.BlockSpec((S, SLAB), lambda p: (0, p)),
                  pl.BlockSpec((npat, S, SLAB), lambda p: (0, 0, p)), pl.BlockSpec((None, S, SLAB), lambda p: (p, 0, 0))],
        out_specs=[pl.BlockSpec((3, S, SLAB), lambda p: (0, 0, p)), pl.BlockSpec((None, 8, SLAB), lambda p: (p, 0, 0))],
        out_shape=[jax.ShapeDtypeStruct((3, S, D_MODEL), BF16), jax.ShapeDtypeStruct((N_SLABS, 8, SLAB), F32)],
        scratch_shapes=[pltpu.VMEM((S, SLAB), F32), pltpu.VMEM((BLOCK, SLAB), F32), pltpu.VMEM((npat, S, SLAB), F32),
                        pltpu.VMEM((3, S, SLAB), F32)],
        compiler_params=_cparams(("arbitrary",)),
    )(slopes, sinks, qkv, dout, o, lse)


def _place():
    x, y, c = lax.axis_index("x"), lax.axis_index("y"), lax.axis_index("c")
    return x, y, c, 2 * x + y


def _other_chips(x, y):
    return [(1 - x, y), (x, 1 - y), (1 - x, 1 - y)]


HBM_SPEC = pl.BlockSpec(memory_space=pl.ANY)


def _slot(q):
    return 2 * (q % 2) + q // 2


BIG = ("ffn1_w_in", "ffn1_w_out", "ffn2_w_in", "ffn2_w_out", "a_w_qkv", "a_w_o", "kv_w", "b_w_q", "b_w_o")
QKV_SHARD = 3 * D_MODEL // N_CHIPS
ROW_SHARD = D_MODEL // N_CHIPS


LAYER0_ITEMS = (("ffn1_w_in", 0), ("ffn1_w_out", 0), ("a_w_qkv", None), ("a_w_o", None), ("ffn2_w_in", 0),
                ("ffn2_w_out", 0), ("kv_w", None))
LAYER1_ITEMS = (("ffn1_w_in", 1), ("ffn1_w_out", 1), ("b_w_q", None), ("b_w_o", None), ("ffn2_w_in", 1),
                ("ffn2_w_out", 1))
OUT_SHARD = D_FF // N_CHIPS


def _full_shape(name):
    if name.endswith("w_in"):
        return (D_MODEL, 2 * D_FF)
    if name.endswith("w_out"):
        return (D_FF, D_MODEL)
    if name == "a_w_qkv":
        return (D_MODEL, 3 * D_MODEL)
    if name == "kv_w":
        return (N_CHIPS, 2, ROW_SHARD // 2, 2 * N_KV_B * HEAD_DIM)
    return (N_CHIPS, 2, ROW_SHARD // 2, D_MODEL)


def _gather_src(item, ref, c):
    name, _ = item
    if name.endswith("w_in"):
        return ref.at[pl.ds(c * (D_MODEL // 2), D_MODEL // 2)]
    if name.endswith("w_out"):
        return ref.at[pl.ds(c * (OUT_SHARD // 2), OUT_SHARD // 2)]
    if name == "a_w_qkv":
        return ref.at[0, pl.ds(c * (D_MODEL // 2), D_MODEL // 2)]
    if name == "kv_w":
        return ref.at[pl.ds(c * (ROW_SHARD // 2), ROW_SHARD // 2)]
    return ref.at[0, pl.ds(c * (ROW_SHARD // 2), ROW_SHARD // 2)]


def _gather_dst(item, ref, q, c):
    name, _ = item
    if name.endswith("w_in"):
        return ref.at[pl.ds(c * (D_MODEL // 2), D_MODEL // 2), pl.ds(_slot(q) * HALF_FF, HALF_FF)]
    if name.endswith("w_out"):
        return ref.at[pl.ds(q * OUT_SHARD + c * (OUT_SHARD // 2), OUT_SHARD // 2)]
    if name == "a_w_qkv":
        return ref.at[pl.ds(c * (D_MODEL // 2), D_MODEL // 2), pl.ds(q * QKV_SHARD, QKV_SHARD)]
    return ref.at[q, c]


def _all_gather(items, shards, small):
    n = len(items)
    r = small.shape[0]
    per = 8

    def body(*refs):
        srcs, small_ref = refs[:n], refs[n]
        dsts, s_ref = refs[n + 1:2 * n + 1], refs[2 * n + 1]
        send_sems, recv_sems = refs[2 * n + 2:]
        x, y, c, myq = _place()
        sibling = (x, y, 1 - c)
        chips = _other_chips(x, y)

        def big(t, k, src, q, h, to):
            return pltpu.make_async_remote_copy(src_ref=src, dst_ref=_gather_dst(items[t], dsts[t], q, h),
                                                send_sem=send_sems.at[per * t + k], recv_sem=recv_sems.at[per * t + k],
                                                device_id=to, device_id_type=MESH)

        def tiny(k, q, to):
            return pltpu.make_async_remote_copy(src_ref=small_ref, dst_ref=s_ref.at[q], send_sem=send_sems.at[per * n + k],
                                                recv_sem=recv_sems.at[per * n + k], device_id=to, device_id_type=MESH)

        first = []
        for j, chip in enumerate(chips):
            first += [big(t, j, _gather_src(items[t], srcs[t], c), myq, c, (*chip, c)) for t in range(n)]
            first.append(tiny(j, myq, (*chip, c)))
        own = [big(t, 6 + h, _gather_src(items[t], srcs[t], h), myq, h, sibling) for t in range(n) for h in (0, 1)]
        own.append(tiny(3, myq, sibling))
        for cp in first + own:
            cp.start()
        passed = []
        for j, (cx, cy) in enumerate(chips):
            q = 2 * cx + cy
            for t in range(n):
                src = _gather_src(items[t], srcs[t], c)
                big(t, j, src, q, c, sibling).wait_recv()
                fwd = big(t, 3 + j, _gather_dst(items[t], dsts[t], q, c), q, c, sibling)
                fwd.start()
                passed.append(fwd)
        for j, (cx, cy) in enumerate(chips):
            q = 2 * cx + cy
            for t in range(n):
                big(t, 3 + j, _gather_src(items[t], srcs[t], c), q, 1 - c, sibling).wait_recv()
            tiny(j, q, sibling).wait_recv()
        for cp in own:
            cp.wait_recv()
        for cp in first + passed + own:
            cp.wait_send()

    outs = pl.pallas_call(
        body, name="all_gather_layer0",
        in_specs=[HBM_SPEC] * (n + 1), out_specs=[HBM_SPEC] * (n + 1),
        out_shape=[jax.ShapeDtypeStruct(_full_shape(name), BF16) for name, _ in items]
        + [jax.ShapeDtypeStruct((N_CHIPS, r, 128), F32)],
        scratch_shapes=[pltpu.SemaphoreType.DMA((per * n + 4,)), pltpu.SemaphoreType.DMA((per * n + 4,))],
    )(*[shards[item] for item in items], small)
    return list(outs[:n]), outs[n]


SEM_SPEC = pl.BlockSpec(memory_space=pltpu.SEMAPHORE)
DATAFLOW = pltpu.SideEffectType.DATAFLOW_SIDE_EFFECTING
PER_ITEM = 8


def _split_start(name, copies, n_sems, sources, land_shapes, after):
    n, m = len(sources), len(land_shapes)

    def body(*refs):
        srcs, lands = refs[:n], refs[n:n + m]
        send_sems, recv_sems = refs[n + m + 1], refs[n + m + 2]
        token = refs[-1]
        for src, dst_there, _, s, peer in copies(srcs, lands):
            pltpu.make_async_remote_copy(src_ref=src, dst_ref=dst_there, send_sem=send_sems.at[s], recv_sem=recv_sems.at[s],
                                         device_id=peer, device_id_type=MESH).start()
        token[...] = jnp.zeros_like(token)

    src_arrays = [pltpu.with_memory_space_constraint(a, pltpu.HBM) for a in sources]
    land_arrays = [pltpu.with_memory_space_constraint(lax.empty(s.shape, s.dtype), pltpu.HBM) for s in land_shapes]
    hbm = pl.BlockSpec(memory_space=pltpu.HBM)
    outs = pl.pallas_call(
        body, name=name,
        in_specs=[hbm] * (n + m) + [HBM_SPEC],
        out_specs=[SEM_SPEC, SEM_SPEC] + [hbm] * (n + m) + [pl.BlockSpec(memory_space=pltpu.VMEM)],
        out_shape=[pltpu.SemaphoreType.DMA((n_sems,)), pltpu.SemaphoreType.DMA((n_sems,))]
        + [pltpu.HBM(a.shape, a.dtype) for a in src_arrays + land_arrays] + [jax.ShapeDtypeStruct((8, 128), F32)],
        input_output_aliases={i: 2 + i for i in range(n + m)},
        compiler_params=pltpu.CompilerParams(has_side_effects=DATAFLOW),
    )(*src_arrays, *land_arrays, after)
    return (outs[0], outs[1], list(outs[2:2 + n]), list(outs[2 + n:2 + n + m])), outs[-1]


def _split_wait(name, copies, state, after):
    send_sems, recv_sems, srcs_thru, lands_thru = state
    n, m = len(srcs_thru), len(lands_thru)
    after = list(after) if isinstance(after, (list, tuple)) else [after]

    def body(*refs):
        srcs, lands = refs[:n], refs[n:n + m]
        send_sems, recv_sems = refs[n + m], refs[n + m + 1]
        for src, _, dst_here, s, peer in copies(srcs, lands):
            cp = pltpu.make_async_remote_copy(src_ref=src, dst_ref=dst_here, send_sem=send_sems.at[s], recv_sem=recv_sems.at[s],
                                              device_id=peer, device_id_type=MESH)
            cp.wait_send()
            cp.wait_recv()

    hbm = pl.BlockSpec(memory_space=pltpu.HBM)
    outs = pl.pallas_call(
        body, name=name,
        in_specs=[hbm] * (n + m) + [SEM_SPEC, SEM_SPEC] + [HBM_SPEC] * len(after),
        out_specs=[hbm] * (n + m),
        out_shape=[pltpu.HBM(a.shape, a.dtype) for a in srcs_thru + lands_thru],
        input_output_aliases={i: i for i in range(n + m)},
        compiler_params=pltpu.CompilerParams(has_side_effects=DATAFLOW),
    )(*srcs_thru, *lands_thru, send_sems, recv_sems, *after)
    return list(outs[:n]), list(outs[n:])


def _gather_copies(items):
    def copies(srcs, lands):
        x, y, c, myq = _place()
        out = []
        for t, item in enumerate(items):
            for h in (0, 1):
                src = _gather_src(item, srcs[t], h)
                for j, (cx, cy) in enumerate(_other_chips(x, y)):
                    out.append((src, _gather_dst(item, lands[t], myq, h), _gather_dst(item, lands[t], 2 * cx + cy, h),
                                PER_ITEM * t + 2 * j + h, (cx, cy, c)))
                out.append((src, _gather_dst(item, lands[t], myq, h), _gather_dst(item, lands[t], myq, h),
                            PER_ITEM * t + 6 + h, (x, y, 1 - c)))
        return out
    return copies


def _gather_start(items, shards, after):
    lands = [jax.ShapeDtypeStruct(_full_shape(name), BF16) for name, _ in items]
    return _split_start("gather_layer1_start", _gather_copies(items), PER_ITEM * len(items),
                        [shards[item] for item in items], lands, after)


def _gather_wait(items, state, after):
    return _split_wait("gather_layer1_wait", _gather_copies(items), state, after)[1]


def _small_all_reduce(v):
    r = v.shape[0]

    def body(v_ref, o_ref, buf_ref, send_sems, recv_sems):
        x, y, c, _ = _place()
        me = 4 * x + 2 * y + c
        buf_ref[me] = v_ref[...]
        copies = []
        for k in range(1, 8):
            fx, fy, fc = (k >> 2) & 1, (k >> 1) & 1, k & 1
            to = (x ^ fx, y ^ fy, c ^ fc)
            cp = pltpu.make_async_remote_copy(src_ref=v_ref, dst_ref=buf_ref.at[me], send_sem=send_sems.at[k - 1],
                                              recv_sem=recv_sems.at[k - 1], device_id=to, device_id_type=MESH)
            cp.start()
            copies.append(cp)
        for k in range(1, 8):
            fx, fy, fc = (k >> 2) & 1, (k >> 1) & 1, k & 1
            src_dev = 4 * (x ^ fx) + 2 * (y ^ fy) + (c ^ fc)
            pltpu.make_async_remote_copy(src_ref=v_ref, dst_ref=buf_ref.at[src_dev], send_sem=send_sems.at[k - 1],
                                         recv_sem=recv_sems.at[k - 1], device_id=(x, y, c), device_id_type=MESH).wait_recv()
        for cp in copies:
            cp.wait_send()
        tot = buf_ref[0]
        for i in range(1, 8):
            tot = tot + buf_ref[i]
        o_ref[...] = tot

    vm = pl.BlockSpec(memory_space=pltpu.VMEM)
    return pl.pallas_call(
        body, name="small_all_reduce", in_specs=[vm], out_specs=vm,
        out_shape=jax.ShapeDtypeStruct((r, 128), F32),
        scratch_shapes=[pltpu.VMEM((8, r, 128), F32), pltpu.SemaphoreType.DMA((7,)), pltpu.SemaphoreType.DMA((7,))],
    )(v)


def _grad_view(kind, g):
    if kind == "col":
        return g.reshape(2, g.shape[0] // 2, g.shape[1])
    return g.reshape(N_CHIPS, 2, g.shape[0] // (2 * N_CHIPS), g.shape[1])


def _half_of(kind, ref, h):
    return ref.at[h] if kind == "col" else ref.at[:, h]


def _half_shape(kind, view_shape):
    return view_shape[1:] if kind == "col" else (view_shape[0],) + view_shape[2:]


def _piece_of(kind, width, colblock, ref, q):
    if kind == "col":
        return ref.at[:, pl.ds(colblock(q) * width, width)]
    return ref.at[q]


def _piece_shape(kind, width, half_shape):
    return (half_shape[0], width) if kind == "col" else half_shape[1:]


def _pair_exchange(views, kinds, name):
    n = len(views)

    def body(*refs):
        ins, outs = refs[:n], refs[n:2 * n]
        send_sems, recv_sems = refs[2 * n:]
        x, y, c, _ = _place()
        cps = []
        for t in range(n):
            cp = pltpu.make_async_remote_copy(src_ref=_half_of(kinds[t], ins[t], 1 - c), dst_ref=outs[t],
                                              send_sem=send_sems.at[t], recv_sem=recv_sems.at[t],
                                              device_id=(x, y, 1 - c), device_id_type=MESH)
            cp.start()
            cps.append(cp)
        for cp in cps:
            cp.wait()

    return pl.pallas_call(
        body, name=name, in_specs=[HBM_SPEC] * n, out_specs=[HBM_SPEC] * n,
        out_shape=[jax.ShapeDtypeStruct(_half_shape(k, v.shape), v.dtype) for k, v in zip(kinds, views)],
        scratch_shapes=[pltpu.SemaphoreType.DMA((n,)), pltpu.SemaphoreType.DMA((n,))],
    )(*views)


def _pair_sum(kind, view, recv, c, name):
    hs = recv.shape
    N = hs[-1]
    rows = hs[-2]
    tr = _pick(rows, (512, 352, 128))
    tn = _pick(N, (1408, 1024, 512))

    def body(c_ref, p_ref, r_ref, s_ref):
        s_ref[...] = (p_ref[...] + r_ref[...]).astype(BF16)

    if kind == "col":
        grid = (rows // tr, N // tn)
        mine = pl.BlockSpec((None, tr, tn), lambda i, j, c_ref: (c_ref[0], i, j))
        blk = pl.BlockSpec((tr, tn), lambda i, j, c_ref: (i, j))
        sem = ("parallel", "parallel")
    else:
        grid = (N_CHIPS, rows // tr, N // tn)
        mine = pl.BlockSpec((None, None, tr, tn), lambda q, i, j, c_ref: (q, c_ref[0], i, j))
        blk = pl.BlockSpec((None, tr, tn), lambda q, i, j, c_ref: (q, i, j))
        sem = ("parallel", "parallel", "parallel")
    return pl.pallas_call(
        body, name=name,
        grid_spec=pltpu.PrefetchScalarGridSpec(num_scalar_prefetch=1, grid=grid, in_specs=[mine, blk], out_specs=blk),
        out_shape=jax.ShapeDtypeStruct(hs, BF16),
        compiler_params=_cparams(sem),
    )(c.reshape(1).astype(jnp.int32), view, recv)


def _chip_copies(kinds, widths, colblocks):
    def copies(srcs, lands):
        x, y, c, _ = _place()
        out = []
        for j, (cx, cy) in enumerate(_other_chips(x, y)):
            for t in range(len(kinds)):
                out.append((_piece_of(kinds[t], widths[t], colblocks[t], srcs[t], 2 * cx + cy), lands[t].at[j],
                            lands[t].at[j], 3 * t + j, (cx, cy, c)))
        return out
    return copies


def _chip_land_shapes(sums, kinds, widths):
    return [jax.ShapeDtypeStruct((3,) + _piece_shape(k, w, s.shape), BF16) for k, w, s in zip(kinds, widths, sums)]


def _chip_exchange(sums, kinds, widths, colblocks, name):
    n = len(sums)
    copies = _chip_copies(kinds, widths, colblocks)

    def body(*refs):
        send_sems, recv_sems = refs[2 * n:]
        cps = [pltpu.make_async_remote_copy(src_ref=src, dst_ref=dst, send_sem=send_sems.at[s], recv_sem=recv_sems.at[s],
                                            device_id=peer, device_id_type=MESH)
               for src, dst, _, s, peer in copies(refs[:n], refs[n:2 * n])]
        for cp in cps:
            cp.start()
        for cp in cps:
            cp.wait()

    return pl.pallas_call(
        body, name=name, in_specs=[HBM_SPEC] * n, out_specs=[HBM_SPEC] * n,
        out_shape=_chip_land_shapes(sums, kinds, widths),
        scratch_shapes=[pltpu.SemaphoreType.DMA((3 * n,)), pltpu.SemaphoreType.DMA((3 * n,))],
    )(*sums)


N_DIRECT = 7


def _direct_piece(kind, width, colblock, view_ref, q, h):
    if kind == "col":
        return view_ref.at[h, :, pl.ds(colblock(q) * width, width)]
    return view_ref.at[q, h]


def _direct_copies(kinds, widths, colblocks):
    def copies(srcs, lands):
        x, y, c, myq = _place()
        out = []
        for t in range(len(kinds)):
            def piece(q, h, t=t):
                return _direct_piece(kinds[t], widths[t], colblocks[t], srcs[t], q, h)
            for j, (cx, cy) in enumerate(_other_chips(x, y)):
                for h in (0, 1):
                    out.append((piece(2 * cx + cy, h), lands[t].at[2 * j + c], lands[t].at[2 * j + h],
                                10 * t + 3 * j + c + h, (cx, cy, h)))
            out.append((piece(myq, 1 - c), lands[t].at[6], lands[t].at[6], 10 * t + 9, (x, y, 1 - c)))
        return out
    return copies


def _chip_sum(kind, own_src, recv, block_idx, c, shard_shape, layer, into, name, direct=False):
    n_recv, rows, N = recv.shape
    tr = _pick(rows, (512, 352, 128))
    tn = _pick(N, (1408, 1024, 768, 512))
    ni, nj = rows // tr, N // tn

    def body(q_ref, s_ref, r_ref, *rest):
        o_ref = rest[-1]
        tot = s_ref[...].astype(F32)
        for k in range(n_recv):
            tot = tot + r_ref[k].astype(F32)
        o_ref[...] = tot

    if direct and kind == "col":
        own = pl.BlockSpec((None, tr, tn), lambda i, j, q_ref: (q_ref[1], i, q_ref[0] * nj + j))
    elif direct:
        own = pl.BlockSpec((None, None, tr, tn), lambda i, j, q_ref: (q_ref[0], q_ref[1], i, j))
    elif kind == "col":
        own = pl.BlockSpec((tr, tn), lambda i, j, q_ref: (i, q_ref[0] * nj + j))
    else:
        own = pl.BlockSpec((None, tr, tn), lambda i, j, q_ref: (q_ref[0], i, j))
    if len(shard_shape) == 3:
        lead = 0 if layer is None else layer
        out_spec = pl.BlockSpec((None, tr, tn), lambda i, j, q_ref: (lead, q_ref[1] * ni + i, j))
    else:
        out_spec = pl.BlockSpec((tr, tn), lambda i, j, q_ref: (q_ref[1] * ni + i, j))
    in_specs = [own, pl.BlockSpec((n_recv, tr, tn), lambda i, j, q_ref: (0, i, j))]
    s = own_src
    args = [jnp.stack([block_idx, c]).astype(jnp.int32), s, recv]
    aliases = {}
    if into is not None:
        in_specs.append(HBM_SPEC)
        args.append(into)
        aliases = {3: 0}
    return pl.pallas_call(
        body, name=name,
        grid_spec=pltpu.PrefetchScalarGridSpec(num_scalar_prefetch=1, grid=(ni, nj), in_specs=in_specs, out_specs=out_spec),
        out_shape=jax.ShapeDtypeStruct(shard_shape, F32), input_output_aliases=aliases,
        compiler_params=_cparams(("parallel", "parallel")),
    )(*args)


def _half_window(ref, h):
    rows = ref.shape[-2] // 2
    if ref.ndim == 3:
        return ref.at[:, pl.ds(h * rows, rows)]
    return ref.at[pl.ds(h * rows, rows)]


def _share_halves(grads, name):
    n = len(grads)

    def body(*refs):
        outs = refs[n:2 * n]
        send_sems, recv_sems = refs[2 * n:]
        x, y, c, _ = _place()
        cps = []
        for t in range(n):
            cp = pltpu.make_async_remote_copy(src_ref=_half_window(outs[t], c), dst_ref=_half_window(outs[t], c),
                                              send_sem=send_sems.at[t], recv_sem=recv_sems.at[t],
                                              device_id=(x, y, 1 - c), device_id_type=MESH)
            cp.start()
            cps.append(cp)
        for t in range(n):
            cps[t].wait_send()
            pltpu.make_async_remote_copy(src_ref=_half_window(outs[t], c), dst_ref=_half_window(outs[t], 1 - c),
                                         send_sem=send_sems.at[t], recv_sem=recv_sems.at[t],
                                         device_id=(x, y, 1 - c), device_id_type=MESH).wait_recv()

    return pl.pallas_call(
        body, name=name, in_specs=[HBM_SPEC] * n, out_specs=[HBM_SPEC] * n,
        out_shape=[jax.ShapeDtypeStruct(g.shape, F32) for g in grads],
        input_output_aliases={t: t for t in range(n)},
        scratch_shapes=[pltpu.SemaphoreType.DMA((n,)), pltpu.SemaphoreType.DMA((n,))],
    )(*grads)


def _adamw(w, g, m, v, name):
    R, W = w.shape
    tr = _pick(R, (512, 352, 256, 32))

    def body(w_ref, g_ref, m_ref, v_ref, d_ref, nm_ref, nv_ref):
        gv = g_ref[...]
        nm = ADAM_B1 * m_ref[...] + (1.0 - ADAM_B1) * gv
        nv = ADAM_B2 * v_ref[...] + (1.0 - ADAM_B2) * (gv * gv)
        m_hat = nm / (1.0 - ADAM_B1 ** ADAM_STEP)
        v_hat = nv / (1.0 - ADAM_B2 ** ADAM_STEP)
        d_ref[...] = -ADAM_LR * (m_hat / (jnp.sqrt(v_hat) + ADAM_EPS) + ADAM_WD * w_ref[...])
        nm_ref[...] = nm
        nv_ref[...] = nv

    blk = pl.BlockSpec((tr, W), lambda i: (i, 0))
    shp = jax.ShapeDtypeStruct((R, W), F32)
    return pl.pallas_call(
        body, name=name, grid=(R // tr,), in_specs=[blk] * 4, out_specs=[blk] * 3, out_shape=[shp] * 3,
        compiler_params=_cparams(("parallel",)),
    )(w, g, m, v)


SMALL_ROWS = 32


def _pack_small(ln_g, ln_b, sinks):
    rows = jnp.concatenate([ln_g.reshape(-1, 128), ln_b.reshape(-1, 128),
                            jnp.pad(sinks.reshape(1, -1), ((0, 0), (0, 128 - sinks.size)))], axis=0)
    return jnp.pad(rows, ((0, SMALL_ROWS - rows.shape[0]), (0, 0)))


def _unpack_small(s, ln_shape, sink_shape):
    n = ln_shape[0] * ln_shape[1] * ln_shape[2] // 128
    return s[:n].reshape(ln_shape), s[n:2 * n].reshape(ln_shape), s[2 * n, :sink_shape[1]].reshape(sink_shape)


def _ffn_fwd(xin, w_in, w_out, gain, bias, tag):
    u, h = _ffn_in(xin, w_in, "ffn_in_" + tag)
    y, yb, z = _mm_ln(h, w_out, xin, gain, bias, 0.5, "ffn_out_ln_" + tag)
    return y, yb, dict(u=u, h=h, z=z, xin=xin)


def _ffn_bwd(dy, saved, w_in, w_out, gain, xin_b, tag, dw_dtype=F32):
    dz, dzc, gg, gb = _ln_bwd(saved["z"], dy, gain, 0.5, "ln_bwd_" + tag)
    du = _ffn_bwd_h(dzc, w_out, saved["u"], "ffn_bwd_h_" + tag)
    d_w_out = _mm_tn(saved["h"], dzc, "ffn_dwout_" + tag, out_dtype=dw_dtype)
    d_w_in = _mm_tn(xin_b, du, "ffn_dwin_" + tag, out_dtype=dw_dtype)
    dx = _mm_nt(du, w_in, "ffn_dx_" + tag, add=dz, add_scale=ALPHA)
    return dx, d_w_in, d_w_out, gg, gb


def kernel(x, ffn1_w_in, ffn1_w_out, ffn2_w_in, ffn2_w_out, ln_g, ln_b, a_w_qkv, a_w_o, kv_w, b_w_q, b_sinks, b_w_o, loss_target, m_ffn1_w_in, m_ffn1_w_out, m_ffn2_w_in, m_ffn2_w_out, m_ln_g, m_ln_b, m_a_w_qkv, m_a_w_o, m_kv_w, m_b_w_q, m_b_sinks, m_b_w_o, v_ffn1_w_in, v_ffn1_w_out, v_ffn2_w_in, v_ffn2_w_out, v_ln_g, v_ln_b, v_a_w_qkv, v_a_w_o, v_kv_w, v_b_w_q, v_b_sinks, v_b_w_o):
    ws = dict(ffn1_w_in=ffn1_w_in, ffn1_w_out=ffn1_w_out, ffn2_w_in=ffn2_w_in, ffn2_w_out=ffn2_w_out, a_w_qkv=a_w_qkv,
              a_w_o=a_w_o, kv_w=kv_w, b_w_q=b_w_q, b_w_o=b_w_o)
    ms = dict(ffn1_w_in=m_ffn1_w_in, ffn1_w_out=m_ffn1_w_out, ffn2_w_in=m_ffn2_w_in, ffn2_w_out=m_ffn2_w_out,
              a_w_qkv=m_a_w_qkv, a_w_o=m_a_w_o, kv_w=m_kv_w, b_w_q=m_b_w_q, b_w_o=m_b_w_o)
    vs = dict(ffn1_w_in=v_ffn1_w_in, ffn1_w_out=v_ffn1_w_out, ffn2_w_in=v_ffn2_w_in, ffn2_w_out=v_ffn2_w_out,
              a_w_qkv=v_a_w_qkv, a_w_o=v_a_w_o, kv_w=v_kv_w, b_w_q=v_b_w_q, b_w_o=v_b_w_o)
    _, _, c_idx, myq = _place()
    xs = x[0]
    target = loss_target[0]

    shards = {(n, l): (ws[n] if l is None else ws[n][l]).astype(BF16) for n, l in LAYER0_ITEMS + LAYER1_ITEMS}

    def as_weights(items, arrays):
        return {n: (a.reshape(D_MODEL, a.shape[-1]) if a.ndim == 4 else a) for (n, _), a in zip(items, arrays)}

    full0, small = _all_gather(LAYER0_ITEMS, shards, _pack_small(ln_g, ln_b, b_sinks))
    gather_state, token = _gather_start(LAYER1_ITEMS, shards, small)

    def layer1_weights(after):
        return as_weights(LAYER1_ITEMS, _gather_wait(LAYER1_ITEMS, gather_state, after))

    n_ln = ln_g.size // 128
    lg = jnp.concatenate([small[q, :n_ln].reshape(DEPTH, 3, 1, -1) for q in range(N_CHIPS)], axis=-1)
    lb = jnp.concatenate([small[q, n_ln:2 * n_ln].reshape(DEPTH, 3, 1, -1) for q in range(N_CHIPS)], axis=-1)
    lg = lg + token[0, 0]
    reducer = _GradReducer(c_idx, myq, {n: ws[n].shape for n in BIG})
    sq, grad_x, _, gg, gb, dsink_part = _local_step(xs, target, as_weights(LAYER0_ITEMS, full0), layer1_weights,
                                                    lg, lb, b_sinks.reshape(N_HEADS), reducer.begin)

    loss_row = jnp.pad(jnp.sum(sq).reshape(1, 1), ((0, 0), (0, 127)))
    dsinks = jnp.pad(dsink_part[:, 0, :].reshape(N_SLABS, 2, HEAD_DIM)[:, :, 0].reshape(1, N_HEADS), ((0, 0), (0, 128 - N_HEADS)))
    gg_full = jnp.stack([jnp.stack([jnp.sum(gg[i][j], axis=0) for j in range(3)]) for i in range(DEPTH)])
    gb_full = jnp.stack([jnp.stack([jnp.sum(gb[i][j], axis=0) for j in range(3)]) for i in range(DEPTH)])
    small_in = jnp.concatenate([loss_row, dsinks, gg_full.reshape(-1, 128), gb_full.reshape(-1, 128)], axis=0)
    small_in = jnp.pad(small_in, ((0, (-small_in.shape[0]) % 8), (0, 0)))
    small_sum = _small_all_reduce(small_in)
    loss = small_sum[0, 0] * (0.5 / D_MODEL)
    grad_sinks = small_sum[1, :N_HEADS].reshape(b_sinks.shape)
    n_full = DEPTH * 3 * D_MODEL // 128
    cols = D_MODEL // N_CHIPS
    grad_ln_g = lax.dynamic_slice_in_dim(small_sum[2:2 + n_full].reshape(DEPTH, 3, D_MODEL), myq * cols, cols, axis=2)
    grad_ln_b = lax.dynamic_slice_in_dim(small_sum[2 + n_full:2 + 2 * n_full].reshape(DEPTH, 3, D_MODEL), myq * cols, cols, axis=2)
    return _update(reducer, grad_x, loss, grad_ln_g, grad_ln_b, grad_sinks, ws, ms, vs,
                   (ln_g, ln_b, b_sinks), (m_ln_g, m_ln_b, m_b_sinks), (v_ln_g, v_ln_b, v_b_sinks))


def _local_step(xs, target, W, layer1_weights, lg, lb, sinks, grads_ready=None):
    if grads_ready is None:
        grads_ready = lambda tag, grads, overlap: 0.0
    S = xs.shape[0]
    slopes = jnp.asarray(_alibi_slopes(N_HEADS))
    in1, out1, in2, out2 = [W["ffn1_w_in"]], [W["ffn1_w_out"]], [W["ffn2_w_in"]], [W["ffn2_w_out"]]

    y1, y1b, s1 = _ffn_fwd(xs, in1[0], out1[0], lg[0, 0], lb[0, 0], "a1")
    qkv_a = _mm_nn(y1b, W["a_w_qkv"], F32, "qkv_a", split=True)
    mix_a, o_a, lse_a = _attn_fwd(qkv_a, slopes, None, PATTERNS_A, "attn_a_fwd")
    y2, y2b, z2 = _mm_ln(mix_a, W["a_w_o"], y1, lg[0, 1], lb[0, 1], 1.0, "attn_a_out_ln")
    y3, y3b, s3 = _ffn_fwd(y2, in2[0], out2[0], lg[0, 2], lb[0, 2], "a2")
    kv_w_rep = jnp.broadcast_to(W["kv_w"].reshape(D_MODEL, 2, N_KV_B, 1, HEAD_DIM),
                                (D_MODEL, 2, N_KV_B, GROUP_B, HEAD_DIM)).reshape(D_MODEL, 2 * D_MODEL)
    kv_rep = _mm_nn(y3b, kv_w_rep, F32, "kv_proj", split=(1, 2))
    W = dict(W, **layer1_weights(kv_rep))
    in1, out1, in2, out2 = (in1 + [W["ffn1_w_in"]], out1 + [W["ffn1_w_out"]], in2 + [W["ffn2_w_in"]],
                            out2 + [W["ffn2_w_out"]])
    y4, y4b, s4 = _ffn_fwd(y3, in1[1], out1[1], lg[1, 0], lb[1, 0], "b1")
    qkv_b = _mm_nn(y4b, W["b_w_q"], F32, "q_b", split=(0, 1), into=kv_rep)
    mix_b, o_b, lse_b = _attn_fwd(qkv_b, slopes, sinks, PATTERNS_B, "attn_b_fwd")
    y5, y5b, z5 = _mm_ln(mix_b, W["b_w_o"], y4, lg[1, 1], lb[1, 1], 1.0, "attn_b_out_ln")
    y6, _, s6 = _ffn_fwd(y5, in2[1], out2[1], lg[1, 2], lb[1, 2], "b2")

    dy6, sq = _loss_grad(y6, target, "loss_grad")
    gr = {n: None for n in BIG}
    gg = [[None] * 3 for _ in range(DEPTH)]
    gb = [[None] * 3 for _ in range(DEPTH)]

    dy5, d_in2_b, d_out2_b, gg[1][2], gb[1][2] = _ffn_bwd(dy6, s6, in2[1], out2[1], lg[1, 2], y5b, "b2", BF16)
    dz5, dz5b, gg[1][1], gb[1][1] = _ln_bwd(z5, dy5, lg[1, 1], 1.0, "ln_bwd_attn_b")
    gr["b_w_o"] = _mm_tn(mix_b, dz5b, "d_b_w_o", out_dtype=BF16)
    dmix_b = _mm_nt(dz5b, W["b_w_o"], "d_mix_b")
    dqkv_b, dsink_part = _attn_bwd(qkv_b, dmix_b, o_b, lse_b, slopes, sinks, PATTERNS_B, "attn_b_bwd")
    dq_b = (dqkv_b, 0)
    gr["b_w_q"] = _mm_tn(y4b, dq_b, "d_b_w_q", out_dtype=BF16)
    dy4 = _mm_nt(dq_b, W["b_w_q"], "d_y4", add=dz5, add_scale=ALPHA)
    dy3, d_in1_b, d_out1_b, gg[1][0], gb[1][0] = _ffn_bwd(dy4, s4, in1[1], out1[1], lg[1, 0], y3b, "b1", BF16)
    d_kv_w_rep = _mm_tn(y3b, dqkv_b, "d_kv_w", split=(1, 2))
    gr["kv_w"] = d_kv_w_rep.reshape(D_MODEL, 2, N_KV_B, GROUP_B, HEAD_DIM).sum(axis=3).reshape(D_MODEL, -1).astype(BF16)
    dy3 = _mm_nt(dqkv_b, kv_w_rep, "d_y3_kv", add=dy3, add_scale=1.0, split=(1, 2))
    tok = grads_ready("l1", {("ffn2_w_in", 1): d_in2_b, ("ffn2_w_out", 1): d_out2_b, ("b_w_o", None): gr["b_w_o"],
                             ("b_w_q", None): gr["b_w_q"], ("ffn1_w_in", 1): d_in1_b, ("ffn1_w_out", 1): d_out1_b,
                             ("kv_w", None): gr["kv_w"]}, True)
    lg0 = lg[0] + tok

    dy2, d_in2_a, d_out2_a, gg[0][2], gb[0][2] = _ffn_bwd(dy3, s3, in2[0], out2[0], lg0[2], y2b, "a2", BF16)
    tok = grads_ready("a2", {("ffn2_w_in", 0): d_in2_a, ("ffn2_w_out", 0): d_out2_a}, True)
    lg0 = lg0 + tok
    dz2, dz2b, gg[0][1], gb[0][1] = _ln_bwd(z2, dy2, lg0[1], 1.0, "ln_bwd_attn_a")
    gr["a_w_o"] = _mm_tn(mix_a, dz2b, "d_a_w_o", out_dtype=BF16)
    dmix_a = _mm_nt(dz2b, W["a_w_o"], "d_mix_a")
    dqkv_a, _ = _attn_bwd(qkv_a, dmix_a, o_a, lse_a, slopes, None, PATTERNS_A, "attn_a_bwd")
    gr["a_w_qkv"] = _mm_tn(y1b, dqkv_a, "d_a_w_qkv", split=True, out_dtype=BF16)
    tok = grads_ready("mix", {("a_w_o", None): gr["a_w_o"], ("a_w_qkv", None): gr["a_w_qkv"]}, True)
    lg0 = lg0 + tok
    dy1 = _mm_nt(dqkv_a, W["a_w_qkv"], "d_y1", add=dz2, add_scale=ALPHA, split=True)
    grad_x, d_in1_a, d_out1_a, gg[0][0], gb[0][0] = _ffn_bwd(dy1, s1, in1[0], out1[0], lg0[0], xs, "a1")
    grads_ready("a1", {("ffn1_w_in", 0): d_in1_a, ("ffn1_w_out", 0): d_out1_a}, False)
    gr["ffn1_w_in"] = [d_in1_a, d_in1_b]
    gr["ffn1_w_out"] = [d_out1_a, d_out1_b]
    gr["ffn2_w_in"] = [d_in2_a, d_in2_b]
    gr["ffn2_w_out"] = [d_out2_a, d_out2_b]
    return sq, grad_x, gr, gg, gb, dsink_part


def _grad_item(name, layer, g):
    if name.endswith("w_in"):
        return (g, "col", HALF_FF, _slot, name, layer)
    if name.endswith("w_out"):
        return (g, "row", D_MODEL, None, name, layer)
    if name == "a_w_qkv":
        return (g, "col", QKV_SHARD, lambda q: q, name, None)
    return (g, "row", g.shape[1], None, name, None)


class _GradReducer:
    def __init__(self, c_idx, myq, shard_shapes):
        self.c_idx, self.myq, self.shard_shapes = c_idx, myq, shard_shapes
        self.groups = []

    def begin(self, tag, grads, overlap):
        items = [_grad_item(n, l, g) for (n, l), g in grads.items()]
        kinds, widths, colblocks = [it[1] for it in items], [it[2] for it in items], [it[3] for it in items]
        views = [_grad_view(k, it[0]) for k, it in zip(kinds, items)]
        if overlap:
            lands = [jax.ShapeDtypeStruct((N_DIRECT,) + _piece_shape(k, w, _half_shape(k, v.shape)), BF16)
                     for k, w, v in zip(kinds, widths, views)]
            state, token = _split_start("grad_direct_start_" + tag, _direct_copies(kinds, widths, colblocks), 10 * len(items),
                                        views, lands, views[-1])
            self.groups.append((tag, items, None, state))
            return token[0, 0]
        from_sibling = _pair_exchange(views, kinds, "grad_pair_exchange_" + tag)
        sums = [_pair_sum(k, v, r, self.c_idx, "pair_sum_%s_%d" % (tag, t))
                for t, (k, v, r) in enumerate(zip(kinds, views, from_sibling))]
        self.groups.append((tag, items, sums, None))
        return 0.0

    def _sum_group(self, tag, items, sums, received, direct):
        for t, (it, s, r) in enumerate(zip(items, sums, received)):
            _, k, _, cb, name, layer = it
            own = cb(self.myq) if k == "col" else self.myq
            self.half_done[name] = _chip_sum(k, s, r, own, self.c_idx, self.shard_shapes[name], layer,
                                             self.half_done.get(name), "chip_sum_%s_%d" % (tag, t), direct=direct)

    def finish_first(self, after):
        self.half_done, self.late = {}, []
        for tag, items, sums, state in self.groups:
            if state is None:
                kinds, widths, colblocks = [it[1] for it in items], [it[2] for it in items], [it[3] for it in items]
                copies = _chip_copies(kinds, widths, colblocks)
                st, _ = _split_start("grad_chip_start_" + tag, copies, 3 * len(items), sums,
                                     _chip_land_shapes(sums, kinds, widths), sums[-1])
                self.late.append((tag, items, copies, st))
        for tag, items, sums, state in self.groups:
            if state is not None:
                kinds, widths, colblocks = [it[1] for it in items], [it[2] for it in items], [it[3] for it in items]
                views, received = _split_wait("grad_direct_wait_" + tag, _direct_copies(kinds, widths, colblocks), state, after)
                self._sum_group(tag, items, views, received, True)
        late_names = {it[4] for _, items, _, _ in self.late for it in items}
        names = [n for n in BIG if n not in late_names]
        return dict(zip(names, _share_halves([self.half_done[n] for n in names], "grad_share_halves_first")))

    def finish_rest(self, after):
        names = []
        for tag, items, copies, st in self.late:
            sums, received = _split_wait("grad_chip_wait_" + tag, copies, st, after)
            self._sum_group(tag, items, sums, received, False)
            names += [it[4] for it in items if it[4] not in names]
        return dict(zip(names, _share_halves([self.half_done[n] for n in names], "grad_share_halves_rest")))


def _update(reducer, grad_x, loss, grad_ln_g, grad_ln_b, grad_sinks, ws, ms, vs, small_w, small_m, small_v):
    ln_g, ln_b, b_sinks = small_w
    m_ln_g, m_ln_b, m_b_sinks = small_m
    v_ln_g, v_ln_b, v_b_sinks = small_v

    deltas, new_m, new_v = {}, {}, {}

    def update(some):
        done = []
        for name in some:
            shp = ws[name].shape
            flat = lambda a: a.reshape(-1, shp[-1])
            d, nm, nv = _adamw(flat(ws[name]), flat(some[name]), flat(ms[name]), flat(vs[name]), "adamw_" + name)
            deltas[name], new_m[name], new_v[name] = d.reshape(shp), nm.reshape(shp), nv.reshape(shp)
            done.append(d)
        return done

    grads = reducer.finish_first(grad_x)
    rest = reducer.finish_rest(update(grads))
    update(rest)
    grads.update(rest)
    delta_s, nm_s, nv_s = _adamw(_pack_small(ln_g, ln_b, b_sinks), _pack_small(grad_ln_g, grad_ln_b, grad_sinks),
                                 _pack_small(m_ln_g, m_ln_b, m_b_sinks), _pack_small(v_ln_g, v_ln_b, v_b_sinks), "adamw_small")
    for d, blob in ((grads, None), (deltas, delta_s), (new_m, nm_s), (new_v, nv_s)):
        if blob is None:
            d["ln_g"], d["ln_b"], d["b_sinks"] = grad_ln_g, grad_ln_b, grad_sinks
        else:
            d["ln_g"], d["ln_b"], d["b_sinks"] = _unpack_small(blob, ln_g.shape, b_sinks.shape)

    order = ("ffn1_w_in", "ffn1_w_out", "ffn2_w_in", "ffn2_w_out", "ln_g", "ln_b", "a_w_qkv", "a_w_o", "kv_w", "b_w_q",
             "b_sinks", "b_w_o")
    outs = [loss, grad_x[None]]
    for d in (grads, deltas, new_m, new_v):
        outs += [d[n] for n in order]
    return tuple(outs)
```

```python
import numpy as np
import jax
import jax.numpy as jnp
from jax import lax
from jax.experimental import pallas as pl
from jax.experimental.pallas import tpu as pltpu

F32 = jnp.float32
BF16 = jnp.bfloat16

D_MODEL = 1024
D_FF = 2816
HALF_FF = D_FF // 2
HEAD_DIM = 64
N_HEADS = 16
N_KV_B = 4
GROUP_B = N_HEADS // N_KV_B
DEPTH = 2
ALPHA = (2.0 * DEPTH) ** 0.25
LN_EPS = 1e-5
BLOCK = 128
SLAB = 128
N_SLABS = D_MODEL // SLAB
PATTERNS_A = ((1, 128, 1.0), (4, 128, 4.0), (16, 128, 16.0))
PATTERNS_B = ((1, 127, 1.0),)
NEG = -1e30

ADAM_LR = 0.001
ADAM_B1 = 0.9
ADAM_B2 = 0.999
ADAM_EPS = 1e-08
ADAM_WD = 0.01
ADAM_STEP = 10

N_CHIPS = 4
VMEM_LIMIT = 56 * 1024 * 1024
MESH = pl.DeviceIdType.MESH


def _alibi_slopes(n):
    return np.array([2.0 ** (-8.0 * (h + 1) / n) for h in range(n)], dtype=np.float32)


def _cparams(sem=None, vmem=VMEM_LIMIT):
    return pltpu.CompilerParams(dimension_semantics=sem, vmem_limit_bytes=vmem)


_DIMS = {"nn": ((1,), (0,)), "nt": ((1,), (1,)), "tn": ((0,), (0,))}


def _unlead(x):
    if isinstance(x, tuple):
        return x[0], x[1], x[0].shape[1:]
    return x, None, x.shape


def _bspec(block, imap, lead=None):
    if lead is None:
        return pl.BlockSpec(block, imap)
    return pl.BlockSpec((None,) + tuple(block), lambda *g: (lead,) + tuple(imap(*g)))


def _matmul(a, b, mode, out_dtype, tm, tn, tk, name, add=None, add_scale=1.0, split=False, into=None):
    out_spec = pl.BlockSpec((tm, tn), lambda i, j, k: (i, j))
    base, count = (0, 3) if split is True else (split or (0, 0))
    if mode == "nn":
        a, al, (M, K) = _unlead(a)
        b, bl, (K2, N) = _unlead(b)
        a_spec = _bspec((tm, tk), lambda i, j, k: (i, k), al)
        b_spec = _bspec((tk, tn), lambda i, j, k: (k, j), bl)
        out_struct = jax.ShapeDtypeStruct((M, N), out_dtype)
        if split:
            assert tn == D_MODEL and N == count * tn
            out_spec = pl.BlockSpec((None, tm, tn), lambda i, j, k: (j + base, i, 0))
            out_struct = jax.ShapeDtypeStruct((3, M, tn), out_dtype)
    elif mode == "nt":
        b, bl, (N, K2) = _unlead(b)
        if split:
            assert tk == D_MODEL
            M, K = a.shape[1], count * a.shape[2]
            a_spec = pl.BlockSpec((None, tm, tk), lambda i, j, k: (k + base, i, 0))
        else:
            a, al, (M, K) = _unlead(a)
            a_spec = _bspec((tm, tk), lambda i, j, k: (i, k), al)
        b_spec = _bspec((tn, tk), lambda i, j, k: (j, k), bl)
        out_struct = jax.ShapeDtypeStruct((M, N), out_dtype)
    else:
        a, al, (K, M) = _unlead(a)
        if split:
            assert tn == D_MODEL
            K2, N = b.shape[1], count * b.shape[2]
            b_spec = pl.BlockSpec((None, tk, tn), lambda i, j, k: (j + base, k, 0))
        else:
            b, bl, (K2, N) = _unlead(b)
            b_spec = _bspec((tk, tn), lambda i, j, k: (k, j), bl)
        a_spec = _bspec((tk, tm), lambda i, j, k: (k, i), al)
        out_struct = jax.ShapeDtypeStruct((M, N), out_dtype)
    assert K == K2 and M % tm == 0 and N % tn == 0 and K % tk == 0, (a.shape, b.shape, mode, tm, tn, tk)
    nk = K // tk
    dims = (_DIMS[mode], ((), ()))
    has_add = add is not None

    narrow = out_dtype != F32
    assert not (narrow and has_add)

    def body(*refs):
        if into is not None:
            refs = refs[:2] + refs[3:]
        if has_add:
            a_ref, b_ref, add_ref, o_ref = refs
            acc_ref = o_ref
        elif narrow:
            a_ref, b_ref, o_ref, acc_ref = refs
        else:
            a_ref, b_ref, o_ref = refs
            acc_ref = o_ref
        k = pl.program_id(2)
        part = lax.dot_general(a_ref[...].astype(BF16), b_ref[...].astype(BF16), dims, preferred_element_type=F32)
        if has_add:
            @pl.when(k == 0)
            def _():
                acc_ref[...] = part + add_scale * add_ref[...]
        else:
            @pl.when(k == 0)
            def _():
                acc_ref[...] = part

        @pl.when(k > 0)
        def _():
            acc_ref[...] += part

        if narrow:
            @pl.when(k == nk - 1)
            def _():
                o_ref[...] = acc_ref[...].astype(out_dtype)

    in_specs = [a_spec, b_spec]
    args = [a, b]
    aliases = {}
    if into is not None:
        assert mode == "nn" and split and not has_add
        in_specs.append(pl.BlockSpec(memory_space=pl.ANY))
        args.append(into)
        aliases = {2: 0}
    if has_add:
        in_specs.append(pl.BlockSpec((tm, tn), lambda i, j, k: (i, j)))
        args.append(add)
    return pl.pallas_call(
        body, name=name, grid=(M // tm, N // tn, nk),
        in_specs=in_specs, out_specs=out_spec, out_shape=out_struct, input_output_aliases=aliases,
        scratch_shapes=[pltpu.VMEM((tm, tn), F32)] if narrow else [],
        compiler_params=_cparams(("parallel", "parallel", "arbitrary")),
    )(*args)


def _pick(n, cands):
    for c in cands:
        if n % c == 0:
            return c
    raise ValueError((n, cands))


def _mm_nn(a, b, out_dtype, name, split=False, into=None):
    M, K = _unlead(a)[2]
    N = _unlead(b)[2][1]
    return _matmul(a, b, "nn", out_dtype, _pick(M, (1024, 512, 256)), _pick(N, (1024, 512)), _pick(K, (1024, 512)), name,
                   split=split, into=into)


def _mm_nt(a, b, name, add=None, add_scale=1.0, split=False):
    M, K = (a.shape[1], D_MODEL) if split else _unlead(a)[2]
    N = _unlead(b)[2][0]
    return _matmul(a, b, "nt", F32, _pick(M, (1024, 512, 256)), _pick(N, (1024, 512)),
                   _pick(K, (2816, 1024, 512)), name, add=add, add_scale=add_scale, split=split)


def _mm_tn(a, b, name, split=False, out_dtype=F32):
    K, M = _unlead(a)[2]
    N = D_MODEL if split else _unlead(b)[2][1]
    return _matmul(a, b, "tn", out_dtype, _pick(M, (1024, 1408, 512)), _pick(N, (1408, 1024, 512)),
                   _pick(K, (2048, 1024, 512, 256)), name, split=split)


def _ffn_in(x, w, name):
    S = x.shape[0]
    tm = _pick(S, (512, 256))
    w, wl, _ = _unlead(w)

    def body(x_ref, w_ref, t_ref, h_ref):
        acc = jnp.dot(x_ref[...].astype(BF16), w_ref[...], preferred_element_type=F32)
        g = acc[:, :HALF_FF]
        up = acc[:, HALF_FF:]
        sg = jax.nn.sigmoid(g)
        silu = g * sg
        t_ref[:, :HALF_FF] = (up * (sg * (1.0 + g * (1.0 - sg)))).astype(BF16)
        t_ref[:, HALF_FF:] = silu.astype(BF16)
        h_ref[...] = (silu * up).astype(BF16)

    return pl.pallas_call(
        body, name=name, grid=(2, S // tm),
        in_specs=[pl.BlockSpec((tm, D_MODEL), lambda j, i: (i, 0)),
                  _bspec((D_MODEL, D_FF), lambda j, i: (0, j), wl)],
        out_specs=[pl.BlockSpec((tm, D_FF), lambda j, i: (i, j)),
                   pl.BlockSpec((tm, HALF_FF), lambda j, i: (i, j))],
        out_shape=[jax.ShapeDtypeStruct((S, 2 * D_FF), BF16), jax.ShapeDtypeStruct((S, D_FF), BF16)],
        compiler_params=_cparams(("parallel", "parallel")),
    )(x, w)


def _ffn_bwd_h(dzc, w_out, u, name):
    S = dzc.shape[0]
    tm = _pick(S, (512, 256))
    w_out, wl, _ = _unlead(w_out)

    def body(dz_ref, w_ref, t_ref, du_ref):
        dh = lax.dot_general(dz_ref[...], w_ref[...], (((1,), (1,)), ((), ())), preferred_element_type=F32)
        du_ref[:, :HALF_FF] = (dh * t_ref[:, :HALF_FF].astype(F32)).astype(BF16)
        du_ref[:, HALF_FF:] = (dh * t_ref[:, HALF_FF:].astype(F32)).astype(BF16)

    return pl.pallas_call(
        body, name=name, grid=(2, S // tm),
        in_specs=[pl.BlockSpec((tm, D_MODEL), lambda j, i: (i, 0)),
                  _bspec((HALF_FF, D_MODEL), lambda j, i: (j, 0), wl),
                  pl.BlockSpec((tm, D_FF), lambda j, i: (i, j))],
        out_specs=pl.BlockSpec((tm, D_FF), lambda j, i: (i, j)),
        out_shape=jax.ShapeDtypeStruct((S, 2 * D_FF), BF16),
        compiler_params=_cparams(("parallel", "parallel")),
    )(dzc, w_out, u)


def _mm_ln(a, w, resid, gain, bias, c, name):
    S, K = a.shape
    tm = _pick(S, (512, 256))
    w, wl, _ = _unlead(w)

    def body(a_ref, w_ref, r_ref, g_ref, b_ref, y_ref, yb_ref, z_ref):
        z = ALPHA * r_ref[...] + c * jnp.dot(a_ref[...], w_ref[...], preferred_element_type=F32)
        mu = jnp.mean(z, axis=-1, keepdims=True)
        zc = z - mu
        var = jnp.mean(zc * zc, axis=-1, keepdims=True)
        y = zc * lax.rsqrt(var + LN_EPS) * g_ref[...] + b_ref[...]
        z_ref[...] = z
        y_ref[...] = y
        yb_ref[...] = y.astype(BF16)

    row = pl.BlockSpec((tm, D_MODEL), lambda i: (i, 0))
    vec = pl.BlockSpec((1, D_MODEL), lambda i: (0, 0))
    return pl.pallas_call(
        body, name=name, grid=(S // tm,),
        in_specs=[pl.BlockSpec((tm, K), lambda i: (i, 0)), _bspec((K, D_MODEL), lambda i: (0, 0), wl), row, vec, vec],
        out_specs=[row, row, row],
        out_shape=[jax.ShapeDtypeStruct((S, D_MODEL), F32), jax.ShapeDtypeStruct((S, D_MODEL), BF16),
                   jax.ShapeDtypeStruct((S, D_MODEL), F32)],
        compiler_params=_cparams(("parallel",)),
    )(a, w, resid, gain, bias)


def _ln_bwd(z, dy, gain, c, name):
    S = z.shape[0]
    tm = _pick(S, (512, 256))

    def body(z_ref, dy_ref, g_ref, dz_ref, dzc_ref, gg_ref, gb_ref):
        i = pl.program_id(0)
        zv = z_ref[...]
        dyv = dy_ref[...]
        mu = jnp.mean(zv, axis=-1, keepdims=True)
        zc = zv - mu
        var = jnp.mean(zc * zc, axis=-1, keepdims=True)
        rstd = lax.rsqrt(var + LN_EPS)
        xhat = zc * rstd
        dyg = dyv * g_ref[...]
        m1 = jnp.mean(dyg, axis=-1, keepdims=True)
        m2 = jnp.mean(dyg * xhat, axis=-1, keepdims=True)
        dz = rstd * (dyg - m1 - xhat * m2)
        dz_ref[...] = dz
        dzc_ref[...] = (c * dz).astype(BF16)
        pg = jnp.sum((dyv * xhat).reshape(tm // 8, 8, D_MODEL), axis=0)
        pb = jnp.sum(dyv.reshape(tm // 8, 8, D_MODEL), axis=0)

        @pl.when(i == 0)
        def _():
            gg_ref[...] = pg
            gb_ref[...] = pb

        @pl.when(i > 0)
        def _():
            gg_ref[...] += pg
            gb_ref[...] += pb

    row = pl.BlockSpec((tm, D_MODEL), lambda i: (i, 0))
    part = pl.BlockSpec((8, D_MODEL), lambda i: (0, 0))
    return pl.pallas_call(
        body, name=name, grid=(S // tm,),
        in_specs=[row, row, pl.BlockSpec((1, D_MODEL), lambda i: (0, 0))],
        out_specs=[row, row, part, part],
        out_shape=[jax.ShapeDtypeStruct((S, D_MODEL), F32), jax.ShapeDtypeStruct((S, D_MODEL), BF16),
                   jax.ShapeDtypeStruct((8, D_MODEL), F32), jax.ShapeDtypeStruct((8, D_MODEL), F32)],
        compiler_params=_cparams(("arbitrary",)),
    )(z, dy, gain)


def _loss_grad(y, t, name):
    S = y.shape[0]
    tm = _pick(S, (512, 256))

    def body(y_ref, t_ref, dy_ref, sq_ref):
        i = pl.program_id(0)
        e = y_ref[...] - t_ref[...]
        dy_ref[...] = e * (1.0 / D_MODEL)
        ps = jnp.sum((e * e).reshape(tm // 8, 8, D_MODEL), axis=0)

        @pl.when(i == 0)
        def _():
            sq_ref[...] = ps

        @pl.when(i > 0)
        def _():
            sq_ref[...] += ps

    row = pl.BlockSpec((tm, D_MODEL), lambda i: (i, 0))
    return pl.pallas_call(
        body, name=name, grid=(S // tm,),
        in_specs=[row, row], out_specs=[row, pl.BlockSpec((8, D_MODEL), lambda i: (0, 0))],
        out_shape=[jax.ShapeDtypeStruct((S, D_MODEL), F32), jax.ShapeDtypeStruct((8, D_MODEL), F32)],
        compiler_params=_cparams(("arbitrary",)),
    )(y, t)


def _rows(start, d):
    if d == 1:
        return pl.ds(pl.multiple_of(start, BLOCK), BLOCK)
    return pl.ds(start, BLOCK, stride=d)


def _ld(ref, start, d):
    return ref[_rows(start, d), :]


def _ld3(ref, lead, start, d):
    return ref[lead, _rows(start, d), :]


def _st3(ref, lead, start, d, val):
    ref[lead, _rows(start, d), :] = val


def _acc3(ref, lead, start, d, val):
    ref[lead, _rows(start, d), :] = ref[lead, _rows(start, d), :] + val


def _band_consts(slope0, slope1, maxd, scale):
    row = lax.broadcasted_iota(jnp.int32, (2 * BLOCK, 2 * BLOCK), 0)
    kj = lax.broadcasted_iota(jnp.int32, (2 * BLOCK, 2 * BLOCK), 1)
    top = row < BLOCK
    dist = BLOCK + jnp.where(top, row, row - BLOCK) - kj
    slope = jnp.where(top, slope0, slope1)
    base = jnp.where((dist >= 0) & (dist <= maxd), -(slope * (dist.astype(F32) * scale)), NEG)
    return base, kj < BLOCK


def _stack_heads(x, lo):
    return jnp.concatenate([jnp.where(lo, x, 0.0), jnp.where(lo, 0.0, x)], axis=0)


def _unstack_heads(x2, lo):
    return jnp.where(lo, x2[:BLOCK], x2[BLOCK:])


def _scores(q2, k2, base, prev_keys, first):
    s = lax.dot_general(q2, k2, (((1,), (1,)), ((), ())), preferred_element_type=F32) * (HEAD_DIM ** -0.5) + base
    return jnp.where(jnp.logical_and(prev_keys, first), NEG, s)


def _softmax_weights(ls):
    mx = ls[0]
    for l in ls[1:]:
        mx = jnp.maximum(mx, l)
    es = [jnp.exp(l - mx) for l in ls]
    tot = es[0]
    for e in es[1:]:
        tot = tot + e
    inv = 1.0 / tot
    return [e * inv for e in es]


def _attn_fwd(qkv, slopes, sinks, patterns, name):
    S = qkv.shape[1]
    npat = len(patterns)
    has_sink = sinks is not None
    if not has_sink:
        sinks = jnp.zeros((N_HEADS,), F32)
    rows_c = 256

    def body(slopes_ref, sinks_ref, x_ref, mix_ref, o_ref, lse_ref, o_scr, lse_scr):
        p = pl.program_id(0)
        lo = lax.broadcasted_iota(jnp.int32, (BLOCK, SLAB), 1) < HEAD_DIM
        top1 = lax.broadcasted_iota(jnp.int32, (2 * BLOCK, 1), 0) < BLOCK
        sk2 = jnp.where(top1, sinks_ref[2 * p], sinks_ref[2 * p + 1])
        for pi, (d, maxd, scale) in enumerate(patterns):
            nb = S // d // BLOCK
            base, prev_keys = _band_consts(slopes_ref[2 * p], slopes_ref[2 * p + 1], maxd, scale)

            def blk(t, carry, pi=pi, d=d, nb=nb, base=base, prev_keys=prev_keys):
                r = t // nb
                n = t - r * nb
                start = r + (d * BLOCK) * n
                prev = jnp.where(n > 0, start - d * BLOCK, start)
                q2 = _stack_heads(_ld3(x_ref, 0, start, d), lo).astype(BF16)
                k2 = jnp.concatenate([_ld3(x_ref, 1, prev, d), _ld3(x_ref, 1, start, d)], axis=0).astype(BF16)
                v2 = jnp.concatenate([_ld3(x_ref, 2, prev, d), _ld3(x_ref, 2, start, d)], axis=0).astype(BF16)
                s = _scores(q2, k2, base, prev_keys, n == 0)
                m = jnp.max(s, axis=-1, keepdims=True)
                if has_sink:
                    m = jnp.maximum(m, sk2)
                e = jnp.exp(s - m)
                den = jnp.sum(e, axis=-1, keepdims=True)
                if has_sink:
                    den = den + jnp.exp(sk2 - m)
                o2 = jnp.dot((e / den).astype(BF16), v2, preferred_element_type=F32)
                _st3(o_scr, pi, start, d, _unstack_heads(o2, lo))
                _st3(lse_scr, pi, start, d, _unstack_heads(m + jnp.log(den), lo))
                return carry

            lax.fori_loop(0, d * nb, blk, 0, unroll=8)

        lane_c = lax.broadcasted_iota(jnp.int32, (rows_c, SLAB), 1)

        def comb(ci, carry):
            rows = pl.ds(pl.multiple_of(ci * rows_c, rows_c), rows_c)
            ls = [lse_scr[i, rows, :] for i in range(npat)]
            packed = jnp.zeros((rows_c, SLAB), F32)
            for i in range(npat):
                o_ref[i, rows, :] = o_scr[i, rows, :].astype(BF16)
                packed = jnp.where(lane_c == 2 * i, ls[i][:, :1], packed)
                packed = jnp.where(lane_c == 2 * i + 1, ls[i][:, HEAD_DIM:HEAD_DIM + 1], packed)
            lse_ref[rows, :] = packed
            if npat == 1:
                mix_ref[rows, :] = o_scr[0, rows, :].astype(BF16)
            else:
                ws = _softmax_weights(ls)
                acc = ws[0] * o_scr[0, rows, :]
                for i in range(1, npat):
                    acc = acc + ws[i] * o_scr[i, rows, :]
                mix_ref[rows, :] = acc.astype(BF16)
            return carry

        lax.fori_loop(0, S // rows_c, comb, 0)

    smem = pl.BlockSpec(memory_space=pltpu.SMEM)
    return pl.pallas_call(
        body, name=name, grid=(N_SLABS,),
        in_specs=[smem, smem, pl.BlockSpec((3, S, SLAB), lambda p: (0, 0, p))],
        out_specs=[pl.BlockSpec((S, SLAB), lambda p: (0, p)), pl.BlockSpec((npat, S, SLAB), lambda p: (0, 0, p)),
                   pl.BlockSpec((None, S, SLAB), lambda p: (p, 0, 0))],
        out_shape=[jax.ShapeDtypeStruct((S, D_MODEL), BF16), jax.ShapeDtypeStruct((npat, S, D_MODEL), BF16),
                   jax.ShapeDtypeStruct((N_SLABS, S, SLAB), F32)],
        scratch_shapes=[pltpu.VMEM((npat, S, SLAB), F32), pltpu.VMEM((npat, S, SLAB), F32)],
        compiler_params=_cparams(("arbitrary",)),
    )(slopes, sinks, qkv)


def _attn_bwd(qkv, dout, o, lse, slopes, sinks, patterns, name):
    S = qkv.shape[1]
    npat = len(patterns)
    has_sink = sinks is not None
    if not has_sink:
        sinks = jnp.zeros((N_HEADS,), F32)
    rows_c = 256

    def headsum(x, lo):
        s0 = jnp.sum(jnp.where(lo, x, 0.0), axis=-1, keepdims=True)
        s1 = jnp.sum(jnp.where(lo, 0.0, x), axis=-1, keepdims=True)
        return jnp.where(lo, s0, s1)

    def body(slopes_ref, sinks_ref, x_ref, do_ref, o_ref, lsep_ref, dxo_ref, dsink_ref, dbar_ref, sacc_ref, lse_ref, dx_ref):
        p = pl.program_id(0)
        lo = lax.broadcasted_iota(jnp.int32, (BLOCK, SLAB), 1) < HEAD_DIM
        lo_c = lax.broadcasted_iota(jnp.int32, (rows_c, SLAB), 1) < HEAD_DIM
        top1 = lax.broadcasted_iota(jnp.int32, (2 * BLOCK, 1), 0) < BLOCK
        sk2 = jnp.where(top1, sinks_ref[2 * p], sinks_ref[2 * p + 1])

        def prep(ci, carry):
            rows = pl.ds(pl.multiple_of(ci * rows_c, rows_c), rows_c)
            dov = do_ref[rows, :]
            dx_ref[:, rows, :] = jnp.zeros((3, rows_c, SLAB), F32)
            packed = lsep_ref[rows, :]
            ls = [jnp.where(lo_c, packed[:, 2 * i:2 * i + 1], packed[:, 2 * i + 1:2 * i + 2]) for i in range(npat)]
            for i in range(npat):
                lse_ref[i, rows, :] = ls[i]
            if npat == 1:
                dbar_ref[rows, :] = headsum(dov * o_ref[0, rows, :].astype(F32), lo_c)
            else:
                ws = _softmax_weights(ls)
                acc = ws[0] * headsum(dov * o_ref[0, rows, :].astype(F32), lo_c)
                for i in range(1, npat):
                    acc = acc + ws[i] * headsum(dov * o_ref[i, rows, :].astype(F32), lo_c)
                dbar_ref[rows, :] = acc
            return carry

        lax.fori_loop(0, S // rows_c, prep, 0)
        sacc_ref[...] = jnp.zeros((BLOCK, SLAB), F32)

        for pi, (d, maxd, scale) in enumerate(patterns):
            nb = S // d // BLOCK
            base, prev_keys = _band_consts(slopes_ref[2 * p], slopes_ref[2 * p + 1], maxd, scale)

            def blk(t, carry, pi=pi, d=d, nb=nb, base=base, prev_keys=prev_keys):
                r = t // nb
                n = t - r * nb
                start = r + (d * BLOCK) * n
                prev = jnp.where(n > 0, start - d * BLOCK, start)
                q2 = _stack_heads(_ld3(x_ref, 0, start, d), lo).astype(BF16)
                k2 = jnp.concatenate([_ld3(x_ref, 1, prev, d), _ld3(x_ref, 1, start, d)], axis=0).astype(BF16)
                v2 = jnp.concatenate([_ld3(x_ref, 2, prev, d), _ld3(x_ref, 2, start, d)], axis=0).astype(BF16)
                ls = [_ld3(lse_ref, i, start, d) for i in range(npat)]
                w = _softmax_weights(ls)[pi] if npat > 1 else 1.0
                do2 = _stack_heads(w * _ld(do_ref, start, d), lo).astype(BF16)
                dl = w * _ld(dbar_ref, start, d)
                lse2 = jnp.concatenate([ls[pi][:, :1], ls[pi][:, HEAD_DIM:HEAD_DIM + 1]], axis=0)
                dl2 = jnp.concatenate([dl[:, :1], dl[:, HEAD_DIM:HEAD_DIM + 1]], axis=0)
                s = _scores(q2, k2, base, prev_keys, n == 0)
                pr = jnp.exp(s - lse2)
                dp = lax.dot_general(do2, v2, (((1,), (1,)), ((), ())), preferred_element_type=F32)
                ds = (pr * (dp - dl2) * (HEAD_DIM ** -0.5)).astype(BF16)
                dq2 = jnp.dot(ds, k2, preferred_element_type=F32)
                dk2 = lax.dot_general(ds, q2, (((0,), (0,)), ((), ())), preferred_element_type=F32)
                dv2 = lax.dot_general(pr.astype(BF16), do2, (((0,), (0,)), ((), ())), preferred_element_type=F32)
                _acc3(dx_ref, 0, start, d, _unstack_heads(dq2, lo))
                _acc3(dx_ref, 1, prev, d, dk2[:BLOCK])
                _acc3(dx_ref, 1, start, d, dk2[BLOCK:])
                _acc3(dx_ref, 2, prev, d, dv2[:BLOCK])
                _acc3(dx_ref, 2, start, d, dv2[BLOCK:])
                if has_sink:
                    sacc_ref[...] += _unstack_heads(-jnp.exp(sk2 - lse2) * dl2, lo)
                return carry

            lax.fori_loop(0, d * nb, blk, 0, unroll=4)

        dsink_ref[...] = jnp.broadcast_to(jnp.sum(sacc_ref[...], axis=0, keepdims=True), (8, SLAB))

        def emit(ci, carry):
            rows = pl.ds(pl.multiple_of(ci * rows_c, rows_c), rows_c)
            dxo_ref[:, rows, :] = dx_ref[:, rows, :].astype(BF16)
            return carry

        lax.fori_loop(0, S // rows_c, emit, 0)

    smem = pl.BlockSpec(memory_space=pltpu.SMEM)
    return pl.pallas_call(
        body, name=name, grid=(N_SLABS,),
        in_specs=[smem, smem, pl.BlockSpec((3, S, SLAB), lambda p: (0, 0, p)), pl.BlockSpec((S, SLAB), lambda p: (0, p)),
                  pl.BlockSpec((npat, S, SLAB), lambda p: (0, 0, p)), pl.BlockSpec((None, S, SLAB), lambda p: (p, 0, 0))],
        out_specs=[pl.BlockSpec((3, S, SLAB), lambda p: (0, 0, p)), pl.BlockSpec((None, 8, SLAB), lambda p: (p, 0, 0))],
        out_shape=[jax.ShapeDtypeStruct((3, S, D_MODEL), BF16), jax.ShapeDtypeStruct((N_SLABS, 8, SLAB), F32)],
        scratch_shapes=[pltpu.VMEM((S, SLAB), F32), pltpu.VMEM((BLOCK, SLAB), F32), pltpu.VMEM((npat, S, SLAB), F32),
                        pltpu.VMEM((3, S, SLAB), F32)],
        compiler_params=_cparams(("arbitrary",)),
    )(slopes, sinks, qkv, dout, o, lse)


def _place():
    x, y, c = lax.axis_index("x"), lax.axis_index("y"), lax.axis_index("c")
    return x, y, c, 2 * x + y


def _other_chips(x, y):
    return [(1 - x, y), (x, 1 - y), (1 - x, 1 - y)]


HBM_SPEC = pl.BlockSpec(memory_space=pl.ANY)


def _slot(q):
    return 2 * (q % 2) + q // 2


BIG = ("ffn1_w_in", "ffn1_w_out", "ffn2_w_in", "ffn2_w_out", "a_w_qkv", "a_w_o", "kv_w", "b_w_q", "b_w_o")
QKV_SHARD = 3 * D_MODEL // N_CHIPS
ROW_SHARD = D_MODEL // N_CHIPS


LAYER0_ITEMS = (("ffn1_w_in", 0), ("ffn1_w_out", 0), ("a_w_qkv", None), ("a_w_o", None), ("ffn2_w_in", 0),
                ("ffn2_w_out", 0), ("kv_w", None))
LAYER1_ITEMS = (("ffn1_w_in", 1), ("ffn1_w_out", 1), ("b_w_q", None), ("b_w_o", None), ("ffn2_w_in", 1),
                ("ffn2_w_out", 1))
OUT_SHARD = D_FF // N_CHIPS


def _full_shape(name):
    if name.endswith("w_in"):
        return (D_MODEL, 2 * D_FF)
    if name.endswith("w_out"):
        return (D_FF, D_MODEL)
    if name == "a_w_qkv":
        return (D_MODEL, 3 * D_MODEL)
    if name == "kv_w":
        return (N_CHIPS, 2, ROW_SHARD // 2, 2 * N_KV_B * HEAD_DIM)
    return (N_CHIPS, 2, ROW_SHARD // 2, D_MODEL)


def _gather_src(item, ref, c):
    name, _ = item
    if name.endswith("w_in"):
        return ref.at[pl.ds(c * (D_MODEL // 2), D_MODEL // 2)]
    if name.endswith("w_out"):
        return ref.at[pl.ds(c * (OUT_SHARD // 2), OUT_SHARD // 2)]
    if name == "a_w_qkv":
        return ref.at[0, pl.ds(c * (D_MODEL // 2), D_MODEL // 2)]
    if name == "kv_w":
        return ref.at[pl.ds(c * (ROW_SHARD // 2), ROW_SHARD // 2)]
    return ref.at[0, pl.ds(c * (ROW_SHARD // 2), ROW_SHARD // 2)]


def _gather_dst(item, ref, q, c):
    name, _ = item
    if name.endswith("w_in"):
        return ref.at[pl.ds(c * (D_MODEL // 2), D_MODEL // 2), pl.ds(_slot(q) * HALF_FF, HALF_FF)]
    if name.endswith("w_out"):
        return ref.at[pl.ds(q * OUT_SHARD + c * (OUT_SHARD // 2), OUT_SHARD // 2)]
    if name == "a_w_qkv":
        return ref.at[pl.ds(c * (D_MODEL // 2), D_MODEL // 2), pl.ds(q * QKV_SHARD, QKV_SHARD)]
    return ref.at[q, c]


def _all_gather(items, shards, small):
    n = len(items)
    r = small.shape[0]
    per = 8

    def body(*refs):
        srcs, small_ref = refs[:n], refs[n]
        dsts, s_ref = refs[n + 1:2 * n + 1], refs[2 * n + 1]
        send_sems, recv_sems = refs[2 * n + 2:]
        x, y, c, myq = _place()
        sibling = (x, y, 1 - c)
        chips = _other_chips(x, y)

        def big(t, k, src, q, h, to):
            return pltpu.make_async_remote_copy(src_ref=src, dst_ref=_gather_dst(items[t], dsts[t], q, h),
                                                send_sem=send_sems.at[per * t + k], recv_sem=recv_sems.at[per * t + k],
                                                device_id=to, device_id_type=MESH)

        def tiny(k, q, to):
            return pltpu.make_async_remote_copy(src_ref=small_ref, dst_ref=s_ref.at[q], send_sem=send_sems.at[per * n + k],
                                                recv_sem=recv_sems.at[per * n + k], device_id=to, device_id_type=MESH)

        first = []
        for j, chip in enumerate(chips):
            first += [big(t, j, _gather_src(items[t], srcs[t], c), myq, c, (*chip, c)) for t in range(n)]
            first.append(tiny(j, myq, (*chip, c)))
        own = [big(t, 6 + h, _gather_src(items[t], srcs[t], h), myq, h, sibling) for t in range(n) for h in (0, 1)]
        own.append(tiny(3, myq, sibling))
        for cp in first + own:
            cp.start()
        passed = []
        for j, (cx, cy) in enumerate(chips):
            q = 2 * cx + cy
            for t in range(n):
                src = _gather_src(items[t], srcs[t], c)
                big(t, j, src, q, c, sibling).wait_recv()
                fwd = big(t, 3 + j, _gather_dst(items[t], dsts[t], q, c), q, c, sibling)
                fwd.start()
                passed.append(fwd)
        for j, (cx, cy) in enumerate(chips):
            q = 2 * cx + cy
            for t in range(n):
                big(t, 3 + j, _gather_src(items[t], srcs[t], c), q, 1 - c, sibling).wait_recv()
            tiny(j, q, sibling).wait_recv()
        for cp in own:
            cp.wait_recv()
        for cp in first + passed + own:
            cp.wait_send()

    outs = pl.pallas_call(
        body, name="all_gather_layer0",
        in_specs=[HBM_SPEC] * (n + 1), out_specs=[HBM_SPEC] * (n + 1),
        out_shape=[jax.ShapeDtypeStruct(_full_shape(name), BF16) for name, _ in items]
        + [jax.ShapeDtypeStruct((N_CHIPS, r, 128), F32)],
        scratch_shapes=[pltpu.SemaphoreType.DMA((per * n + 4,)), pltpu.SemaphoreType.DMA((per * n + 4,))],
    )(*[shards[item] for item in items], small)
    return list(outs[:n]), outs[n]


SEM_SPEC = pl.BlockSpec(memory_space=pltpu.SEMAPHORE)
DATAFLOW = pltpu.SideEffectType.DATAFLOW_SIDE_EFFECTING
PER_ITEM = 8


def _split_start(name, copies, n_sems, sources, land_shapes, after):
    n, m = len(sources), len(land_shapes)

    def body(*refs):
        srcs, lands = refs[:n], refs[n:n + m]
        send_sems, recv_sems = refs[n + m + 1], refs[n + m + 2]
        token = refs[-1]
        for src, dst_there, _, s, peer in copies(srcs, lands):
            pltpu.make_async_remote_copy(src_ref=src, dst_ref=dst_there, send_sem=send_sems.at[s], recv_sem=recv_sems.at[s],
                                         device_id=peer, device_id_type=MESH).start()
        token[...] = jnp.zeros_like(token)

    src_arrays = [pltpu.with_memory_space_constraint(a, pltpu.HBM) for a in sources]
    land_arrays = [pltpu.with_memory_space_constraint(lax.empty(s.shape, s.dtype), pltpu.HBM) for s in land_shapes]
    hbm = pl.BlockSpec(memory_space=pltpu.HBM)
    outs = pl.pallas_call(
        body, name=name,
        in_specs=[hbm] * (n + m) + [HBM_SPEC],
        out_specs=[SEM_SPEC, SEM_SPEC] + [hbm] * (n + m) + [pl.BlockSpec(memory_space=pltpu.VMEM)],
        out_shape=[pltpu.SemaphoreType.DMA((n_sems,)), pltpu.SemaphoreType.DMA((n_sems,))]
        + [pltpu.HBM(a.shape, a.dtype) for a in src_arrays + land_arrays] + [jax.ShapeDtypeStruct((8, 128), F32)],
        input_output_aliases={i: 2 + i for i in range(n + m)},
        compiler_params=pltpu.CompilerParams(has_side_effects=DATAFLOW),
    )(*src_arrays, *land_arrays, after)
    return (outs[0], outs[1], list(outs[2:2 + n]), list(outs[2 + n:2 + n + m])), outs[-1]


def _split_wait(name, copies, state, after):
    send_sems, recv_sems, srcs_thru, lands_thru = state
    n, m = len(srcs_thru), len(lands_thru)
    after = list(after) if isinstance(after, (list, tuple)) else [after]

    def body(*refs):
        srcs, lands = refs[:n], refs[n:n + m]
        send_sems, recv_sems = refs[n + m], refs[n + m + 1]
        for src, _, dst_here, s, peer in copies(srcs, lands):
            cp = pltpu.make_async_remote_copy(src_ref=src, dst_ref=dst_here, send_sem=send_sems.at[s], recv_sem=recv_sems.at[s],
                                              device_id=peer, device_id_type=MESH)
            cp.wait_send()
            cp.wait_recv()

    hbm = pl.BlockSpec(memory_space=pltpu.HBM)
    outs = pl.pallas_call(
        body, name=name,
        in_specs=[hbm] * (n + m) + [SEM_SPEC, SEM_SPEC] + [HBM_SPEC] * len(after),
        out_specs=[hbm] * (n + m),
        out_shape=[pltpu.HBM(a.shape, a.dtype) for a in srcs_thru + lands_thru],
        input_output_aliases={i: i for i in range(n + m)},
        compiler_params=pltpu.CompilerParams(has_side_effects=DATAFLOW),
    )(*srcs_thru, *lands_thru, send_sems, recv_sems, *after)
    return list(outs[:n]), list(outs[n:])


def _gather_copies(items):
    def copies(srcs, lands):
        x, y, c, myq = _place()
        out = []
        for t, item in enumerate(items):
            for h in (0, 1):
                src = _gather_src(item, srcs[t], h)
                for j, (cx, cy) in enumerate(_other_chips(x, y)):
                    out.append((src, _gather_dst(item, lands[t], myq, h), _gather_dst(item, lands[t], 2 * cx + cy, h),
                                PER_ITEM * t + 2 * j + h, (cx, cy, c)))
                out.append((src, _gather_dst(item, lands[t], myq, h), _gather_dst(item, lands[t], myq, h),
                            PER_ITEM * t + 6 + h, (x, y, 1 - c)))
        return out
    return copies


def _gather_start(items, shards, after):
    lands = [jax.ShapeDtypeStruct(_full_shape(name), BF16) for name, _ in items]
    return _split_start("gather_layer1_start", _gather_copies(items), PER_ITEM * len(items),
                        [shards[item] for item in items], lands, after)


def _gather_wait(items, state, after):
    return _split_wait("gather_layer1_wait", _gather_copies(items), state, after)[1]


def _small_all_reduce(v):
    r = v.shape[0]

    def body(v_ref, o_ref, buf_ref, send_sems, recv_sems):
        x, y, c, _ = _place()
        me = 4 * x + 2 * y + c
        buf_ref[me] = v_ref[...]
        copies = []
        for k in range(1, 8):
            fx, fy, fc = (k >> 2) & 1, (k >> 1) & 1, k & 1
            to = (x ^ fx, y ^ fy, c ^ fc)
            cp = pltpu.make_async_remote_copy(src_ref=v_ref, dst_ref=buf_ref.at[me], send_sem=send_sems.at[k - 1],
                                              recv_sem=recv_sems.at[k - 1], device_id=to, device_id_type=MESH)
            cp.start()
            copies.append(cp)
        for k in range(1, 8):
            fx, fy, fc = (k >> 2) & 1, (k >> 1) & 1, k & 1
            src_dev = 4 * (x ^ fx) + 2 * (y ^ fy) + (c ^ fc)
            pltpu.make_async_remote_copy(src_ref=v_ref, dst_ref=buf_ref.at[src_dev], send_sem=send_sems.at[k - 1],
                                         recv_sem=recv_sems.at[k - 1], device_id=(x, y, c), device_id_type=MESH).wait_recv()
        for cp in copies:
            cp.wait_send()
        tot = buf_ref[0]
        for i in range(1, 8):
            tot = tot + buf_ref[i]
        o_ref[...] = tot

    vm = pl.BlockSpec(memory_space=pltpu.VMEM)
    return pl.pallas_call(
        body, name="small_all_reduce", in_specs=[vm], out_specs=vm,
        out_shape=jax.ShapeDtypeStruct((r, 128), F32),
        scratch_shapes=[pltpu.VMEM((8, r, 128), F32), pltpu.SemaphoreType.DMA((7,)), pltpu.SemaphoreType.DMA((7,))],
    )(v)


def _grad_view(kind, g):
    if kind == "col":
        return g.reshape(2, g.shape[0] // 2, g.shape[1])
    return g.reshape(N_CHIPS, 2, g.shape[0] // (2 * N_CHIPS), g.shape[1])


def _half_of(kind, ref, h):
    return ref.at[h] if kind == "col" else ref.at[:, h]


def _half_shape(kind, view_shape):
    return view_shape[1:] if kind == "col" else (view_shape[0],) + view_shape[2:]


def _piece_of(kind, width, colblock, ref, q):
    if kind == "col":
        return ref.at[:, pl.ds(colblock(q) * width, width)]
    return ref.at[q]


def _piece_shape(kind, width, half_shape):
    return (half_shape[0], width) if kind == "col" else half_shape[1:]


def _pair_exchange(views, kinds, name):
    n = len(views)

    def body(*refs):
        ins, outs = refs[:n], refs[n:2 * n]
        send_sems, recv_sems = refs[2 * n:]
        x, y, c, _ = _place()
        cps = []
        for t in range(n):
            cp = pltpu.make_async_remote_copy(src_ref=_half_of(kinds[t], ins[t], 1 - c), dst_ref=outs[t],
                                              send_sem=send_sems.at[t], recv_sem=recv_sems.at[t],
                                              device_id=(x, y, 1 - c), device_id_type=MESH)
            cp.start()
            cps.append(cp)
        for cp in cps:
            cp.wait()

    return pl.pallas_call(
        body, name=name, in_specs=[HBM_SPEC] * n, out_specs=[HBM_SPEC] * n,
        out_shape=[jax.ShapeDtypeStruct(_half_shape(k, v.shape), v.dtype) for k, v in zip(kinds, views)],
        scratch_shapes=[pltpu.SemaphoreType.DMA((n,)), pltpu.SemaphoreType.DMA((n,))],
    )(*views)


def _pair_sum(kind, view, recv, c, name):
    hs = recv.shape
    N = hs[-1]
    rows = hs[-2]
    tr = _pick(rows, (512, 352, 128))
    tn = _pick(N, (1408, 1024, 512))

    def body(c_ref, p_ref, r_ref, s_ref):
        s_ref[...] = (p_ref[...] + r_ref[...]).astype(BF16)

    if kind == "col":
        grid = (rows // tr, N // tn)
        mine = pl.BlockSpec((None, tr, tn), lambda i, j, c_ref: (c_ref[0], i, j))
        blk = pl.BlockSpec((tr, tn), lambda i, j, c_ref: (i, j))
        sem = ("parallel", "parallel")
    else:
        grid = (N_CHIPS, rows // tr, N // tn)
        mine = pl.BlockSpec((None, None, tr, tn), lambda q, i, j, c_ref: (q, c_ref[0], i, j))
        blk = pl.BlockSpec((None, tr, tn), lambda q, i, j, c_ref: (q, i, j))
        sem = ("parallel", "parallel", "parallel")
    return pl.pallas_call(
        body, name=name,
        grid_spec=pltpu.PrefetchScalarGridSpec(num_scalar_prefetch=1, grid=grid, in_specs=[mine, blk], out_specs=blk),
        out_shape=jax.ShapeDtypeStruct(hs, BF16),
        compiler_params=_cparams(sem),
    )(c.reshape(1).astype(jnp.int32), view, recv)


def _chip_copies(kinds, widths, colblocks):
    def copies(srcs, lands):
        x, y, c, _ = _place()
        out = []
        for j, (cx, cy) in enumerate(_other_chips(x, y)):
            for t in range(len(kinds)):
                out.append((_piece_of(kinds[t], widths[t], colblocks[t], srcs[t], 2 * cx + cy), lands[t].at[j],
                            lands[t].at[j], 3 * t + j, (cx, cy, c)))
        return out
    return copies


def _chip_land_shapes(sums, kinds, widths):
    return [jax.ShapeDtypeStruct((3,) + _piece_shape(k, w, s.shape), BF16) for k, w, s in zip(kinds, widths, sums)]


def _chip_exchange(sums, kinds, widths, colblocks, name):
    n = len(sums)
    copies = _chip_copies(kinds, widths, colblocks)

    def body(*refs):
        send_sems, recv_sems = refs[2 * n:]
        cps = [pltpu.make_async_remote_copy(src_ref=src, dst_ref=dst, send_sem=send_sems.at[s], recv_sem=recv_sems.at[s],
                                            device_id=peer, device_id_type=MESH)
               for src, dst, _, s, peer in copies(refs[:n], refs[n:2 * n])]
        for cp in cps:
            cp.start()
        for cp in cps:
            cp.wait()

    return pl.pallas_call(
        body, name=name, in_specs=[HBM_SPEC] * n, out_specs=[HBM_SPEC] * n,
        out_shape=_chip_land_shapes(sums, kinds, widths),
        scratch_shapes=[pltpu.SemaphoreType.DMA((3 * n,)), pltpu.SemaphoreType.DMA((3 * n,))],
    )(*sums)


N_DIRECT = 7


def _direct_piece(kind, width, colblock, view_ref, q, h):
    if kind == "col":
        return view_ref.at[h, :, pl.ds(colblock(q) * width, width)]
    return view_ref.at[q, h]


def _direct_copies(kinds, widths, colblocks):
    def copies(srcs, lands):
        x, y, c, myq = _place()
        out = []
        for t in range(len(kinds)):
            def piece(q, h, t=t):
                return _direct_piece(kinds[t], widths[t], colblocks[t], srcs[t], q, h)
            for j, (cx, cy) in enumerate(_other_chips(x, y)):
                for h in (0, 1):
                    out.append((piece(2 * cx + cy, h), lands[t].at[2 * j + c], lands[t].at[2 * j + h],
                                10 * t + 3 * j + c + h, (cx, cy, h)))
            out.append((piece(myq, 1 - c), lands[t].at[6], lands[t].at[6], 10 * t + 9, (x, y, 1 - c)))
        return out
    return copies


def _chip_sum(kind, own_src, recv, block_idx, c, shard_shape, layer, into, name, direct=False):
    n_recv, rows, N = recv.shape
    tr = _pick(rows, (512, 352, 128))
    tn = _pick(N, (1408, 1024, 768, 512))
    ni, nj = rows // tr, N // tn

    def body(q_ref, s_ref, r_ref, *rest):
        o_ref = rest[-1]
        tot = s_ref[...].astype(F32)
        for k in range(n_recv):
            tot = tot + r_ref[k].astype(F32)
        o_ref[...] = tot

    if direct and kind == "col":
        own = pl.BlockSpec((None, tr, tn), lambda i, j, q_ref: (q_ref[1], i, q_ref[0] * nj + j))
    elif direct:
        own = pl.BlockSpec((None, None, tr, tn), lambda i, j, q_ref: (q_ref[0], q_ref[1], i, j))
    elif kind == "col":
        own = pl.BlockSpec((tr, tn), lambda i, j, q_ref: (i, q_ref[0] * nj + j))
    else:
        own = pl.BlockSpec((None, tr, tn), lambda i, j, q_ref: (q_ref[0], i, j))
    if len(shard_shape) == 3:
        lead = 0 if layer is None else layer
        out_spec = pl.BlockSpec((None, tr, tn), lambda i, j, q_ref: (lead, q_ref[1] * ni + i, j))
    else:
        out_spec = pl.BlockSpec((tr, tn), lambda i, j, q_ref: (q_ref[1] * ni + i, j))
    in_specs = [own, pl.BlockSpec((n_recv, tr, tn), lambda i, j, q_ref: (0, i, j))]
    s = own_src
    args = [jnp.stack([block_idx, c]).astype(jnp.int32), s, recv]
    aliases = {}
    if into is not None:
        in_specs.append(HBM_SPEC)
        args.append(into)
        aliases = {3: 0}
    return pl.pallas_call(
        body, name=name,
        grid_spec=pltpu.PrefetchScalarGridSpec(num_scalar_prefetch=1, grid=(ni, nj), in_specs=in_specs, out_specs=out_spec),
        out_shape=jax.ShapeDtypeStruct(shard_shape, F32), input_output_aliases=aliases,
        compiler_params=_cparams(("parallel", "parallel")),
    )(*args)


def _half_window(ref, h):
    rows = ref.shape[-2] // 2
    if ref.ndim == 3:
        return ref.at[:, pl.ds(h * rows, rows)]
    return ref.at[pl.ds(h * rows, rows)]


def _share_halves(grads, name):
    n = len(grads)

    def body(*refs):
        outs = refs[n:2 * n]
        send_sems, recv_sems = refs[2 * n:]
        x, y, c, _ = _place()
        cps = []
        for t in range(n):
            cp = pltpu.make_async_remote_copy(src_ref=_half_window(outs[t], c), dst_ref=_half_window(outs[t], c),
                                              send_sem=send_sems.at[t], recv_sem=recv_sems.at[t],
                                              device_id=(x, y, 1 - c), device_id_type=MESH)
            cp.start()
            cps.append(cp)
        for t in range(n):
            cps[t].wait_send()
            pltpu.make_async_remote_copy(src_ref=_half_window(outs[t], c), dst_ref=_half_window(outs[t], 1 - c),
                                         send_sem=send_sems.at[t], recv_sem=recv_sems.at[t],
                                         device_id=(x, y, 1 - c), device_id_type=MESH).wait_recv()

    return pl.pallas_call(
        body, name=name, in_specs=[HBM_SPEC] * n, out_specs=[HBM_SPEC] * n,
        out_shape=[jax.ShapeDtypeStruct(g.shape, F32) for g in grads],
        input_output_aliases={t: t for t in range(n)},
        scratch_shapes=[pltpu.SemaphoreType.DMA((n,)), pltpu.SemaphoreType.DMA((n,))],
    )(*grads)


def _adamw(w, g, m, v, name):
    R, W = w.shape
    tr = _pick(R, (512, 352, 256, 32))

    def body(w_ref, g_ref, m_ref, v_ref, d_ref, nm_ref, nv_ref):
        gv = g_ref[...]
        nm = ADAM_B1 * m_ref[...] + (1.0 - ADAM_B1) * gv
        nv = ADAM_B2 * v_ref[...] + (1.0 - ADAM_B2) * (gv * gv)
        m_hat = nm / (1.0 - ADAM_B1 ** ADAM_STEP)
        v_hat = nv / (1.0 - ADAM_B2 ** ADAM_STEP)
        d_ref[...] = -ADAM_LR * (m_hat / (jnp.sqrt(v_hat) + ADAM_EPS) + ADAM_WD * w_ref[...])
        nm_ref[...] = nm
        nv_ref[...] = nv

    blk = pl.BlockSpec((tr, W), lambda i: (i, 0))
    shp = jax.ShapeDtypeStruct((R, W), F32)
    return pl.pallas_call(
        body, name=name, grid=(R // tr,), in_specs=[blk] * 4, out_specs=[blk] * 3, out_shape=[shp] * 3,
        compiler_params=_cparams(("parallel",)),
    )(w, g, m, v)


SMALL_ROWS = 32


def _pack_small(ln_g, ln_b, sinks):
    rows = jnp.concatenate([ln_g.reshape(-1, 128), ln_b.reshape(-1, 128),
                            jnp.pad(sinks.reshape(1, -1), ((0, 0), (0, 128 - sinks.size)))], axis=0)
    return jnp.pad(rows, ((0, SMALL_ROWS - rows.shape[0]), (0, 0)))


def _unpack_small(s, ln_shape, sink_shape):
    n = ln_shape[0] * ln_shape[1] * ln_shape[2] // 128
    return s[:n].reshape(ln_shape), s[n:2 * n].reshape(ln_shape), s[2 * n, :sink_shape[1]].reshape(sink_shape)


def _ffn_fwd(xin, w_in, w_out, gain, bias, tag):
    u, h = _ffn_in(xin, w_in, "ffn_in_" + tag)
    y, yb, z = _mm_ln(h, w_out, xin, gain, bias, 0.5, "ffn_out_ln_" + tag)
    return y, yb, dict(u=u, h=h, z=z, xin=xin)


def _ffn_bwd(dy, saved, w_in, w_out, gain, xin_b, tag, dw_dtype=F32):
    dz, dzc, gg, gb = _ln_bwd(saved["z"], dy, gain, 0.5, "ln_bwd_" + tag)
    du = _ffn_bwd_h(dzc, w_out, saved["u"], "ffn_bwd_h_" + tag)
    d_w_out = _mm_tn(saved["h"], dzc, "ffn_dwout_" + tag, out_dtype=dw_dtype)
    d_w_in = _mm_tn(xin_b, du, "ffn_dwin_" + tag, out_dtype=dw_dtype)
    dx = _mm_nt(du, w_in, "ffn_dx_" + tag, add=dz, add_scale=ALPHA)
    return dx, d_w_in, d_w_out, gg, gb


def kernel(x, ffn1_w_in, ffn1_w_out, ffn2_w_in, ffn2_w_out, ln_g, ln_b, a_w_qkv, a_w_o, kv_w, b_w_q, b_sinks, b_w_o, loss_target, m_ffn1_w_in, m_ffn1_w_out, m_ffn2_w_in, m_ffn2_w_out, m_ln_g, m_ln_b, m_a_w_qkv, m_a_w_o, m_kv_w, m_b_w_q, m_b_sinks, m_b_w_o, v_ffn1_w_in, v_ffn1_w_out, v_ffn2_w_in, v_ffn2_w_out, v_ln_g, v_ln_b, v_a_w_qkv, v_a_w_o, v_kv_w, v_b_w_q, v_b_sinks, v_b_w_o):
    ws = dict(ffn1_w_in=ffn1_w_in, ffn1_w_out=ffn1_w_out, ffn2_w_in=ffn2_w_in, ffn2_w_out=ffn2_w_out, a_w_qkv=a_w_qkv,
              a_w_o=a_w_o, kv_w=kv_w, b_w_q=b_w_q, b_w_o=b_w_o)
    ms = dict(ffn1_w_in=m_ffn1_w_in, ffn1_w_out=m_ffn1_w_out, ffn2_w_in=m_ffn2_w_in, ffn2_w_out=m_ffn2_w_out,
              a_w_qkv=m_a_w_qkv, a_w_o=m_a_w_o, kv_w=m_kv_w, b_w_q=m_b_w_q, b_w_o=m_b_w_o)
    vs = dict(ffn1_w_in=v_ffn1_w_in, ffn1_w_out=v_ffn1_w_out, ffn2_w_in=v_ffn2_w_in, ffn2_w_out=v_ffn2_w_out,
              a_w_qkv=v_a_w_qkv, a_w_o=v_a_w_o, kv_w=v_kv_w, b_w_q=v_b_w_q, b_w_o=v_b_w_o)
    _, _, c_idx, myq = _place()
    xs = x[0]
    target = loss_target[0]

    shards = {(n, l): (ws[n] if l is None else ws[n][l]).astype(BF16) for n, l in LAYER0_ITEMS + LAYER1_ITEMS}

    def as_weights(items, arrays):
        return {n: (a.reshape(D_MODEL, a.shape[-1]) if a.ndim == 4 else a) for (n, _), a in zip(items, arrays)}

    full0, small = _all_gather(LAYER0_ITEMS, shards, _pack_small(ln_g, ln_b, b_sinks))
    gather_state, token = _gather_start(LAYER1_ITEMS, shards, small)

    def layer1_weights(after):
        return as_weights(LAYER1_ITEMS, _gather_wait(LAYER1_ITEMS, gather_state, after))

    n_ln = ln_g.size // 128
    lg = jnp.concatenate([small[q, :n_ln].reshape(DEPTH, 3, 1, -1) for q in range(N_CHIPS)], axis=-1)
    lb = jnp.concatenate([small[q, n_ln:2 * n_ln].reshape(DEPTH, 3, 1, -1) for q in range(N_CHIPS)], axis=-1)
    lg = lg + token[0, 0]
    reducer = _GradReducer(c_idx, myq, {n: ws[n].shape for n in BIG})
    sq, grad_x, _, gg, gb, dsink_part = _local_step(xs, target, as_weights(LAYER0_ITEMS, full0), layer1_weights,
                                                    lg, lb, b_sinks.reshape(N_HEADS), reducer.begin)

    loss_row = jnp.pad(jnp.sum(sq).reshape(1, 1), ((0, 0), (0, 127)))
    dsinks = jnp.pad(dsink_part[:, 0, :].reshape(N_SLABS, 2, HEAD_DIM)[:, :, 0].reshape(1, N_HEADS), ((0, 0), (0, 128 - N_HEADS)))
    gg_full = jnp.stack([jnp.stack([jnp.sum(gg[i][j], axis=0) for j in range(3)]) for i in range(DEPTH)])
    gb_full = jnp.stack([jnp.stack([jnp.sum(gb[i][j], axis=0) for j in range(3)]) for i in range(DEPTH)])
    small_in = jnp.concatenate([loss_row, dsinks, gg_full.reshape(-1, 128), gb_full.reshape(-1, 128)], axis=0)
    small_in = jnp.pad(small_in, ((0, (-small_in.shape[0]) % 8), (0, 0)))
    small_sum = _small_all_reduce(small_in)
    loss = small_sum[0, 0] * (0.5 / D_MODEL)
    grad_sinks = small_sum[1, :N_HEADS].reshape(b_sinks.shape)
    n_full = DEPTH * 3 * D_MODEL // 128
    cols = D_MODEL // N_CHIPS
    grad_ln_g = lax.dynamic_slice_in_dim(small_sum[2:2 + n_full].reshape(DEPTH, 3, D_MODEL), myq * cols, cols, axis=2)
    grad_ln_b = lax.dynamic_slice_in_dim(small_sum[2 + n_full:2 + 2 * n_full].reshape(DEPTH, 3, D_MODEL), myq * cols, cols, axis=2)
    return _update(reducer, grad_x, loss, grad_ln_g, grad_ln_b, grad_sinks, ws, ms, vs,
                   (ln_g, ln_b, b_sinks), (m_ln_g, m_ln_b, m_b_sinks), (v_ln_g, v_ln_b, v_b_sinks))


def _local_step(xs, target, W, layer1_weights, lg, lb, sinks, grads_ready=None):
    if grads_ready is None:
        grads_ready = lambda tag, grads, overlap: 0.0
    S = xs.shape[0]
    slopes = jnp.asarray(_alibi_slopes(N_HEADS))
    in1, out1, in2, out2 = [W["ffn1_w_in"]], [W["ffn1_w_out"]], [W["ffn2_w_in"]], [W["ffn2_w_out"]]

    y1, y1b, s1 = _ffn_fwd(xs, in1[0], out1[0], lg[0, 0], lb[0, 0], "a1")
    qkv_a = _mm_nn(y1b, W["a_w_qkv"], F32, "qkv_a", split=True)
    mix_a, o_a, lse_a = _attn_fwd(qkv_a, slopes, None, PATTERNS_A, "attn_a_fwd")
    y2, y2b, z2 = _mm_ln(mix_a, W["a_w_o"], y1, lg[0, 1], lb[0, 1], 1.0, "attn_a_out_ln")
    y3, y3b, s3 = _ffn_fwd(y2, in2[0], out2[0], lg[0, 2], lb[0, 2], "a2")
    kv_w_rep = jnp.broadcast_to(W["kv_w"].reshape(D_MODEL, 2, N_KV_B, 1, HEAD_DIM),
                                (D_MODEL, 2, N_KV_B, GROUP_B, HEAD_DIM)).reshape(D_MODEL, 2 * D_MODEL)
    kv_rep = _mm_nn(y3b, kv_w_rep, F32, "kv_proj", split=(1, 2))
    W = dict(W, **layer1_weights(kv_rep))
    in1, out1, in2, out2 = (in1 + [W["ffn1_w_in"]], out1 + [W["ffn1_w_out"]], in2 + [W["ffn2_w_in"]],
                            out2 + [W["ffn2_w_out"]])
    y4, y4b, s4 = _ffn_fwd(y3, in1[1], out1[1], lg[1, 0], lb[1, 0], "b1")
    qkv_b = _mm_nn(y4b, W["b_w_q"], F32, "q_b", split=(0, 1), into=kv_rep)
    mix_b, o_b, lse_b = _attn_fwd(qkv_b, slopes, sinks, PATTERNS_B, "attn_b_fwd")
    y5, y5b, z5 = _mm_ln(mix_b, W["b_w_o"], y4, lg[1, 1], lb[1, 1], 1.0, "attn_b_out_ln")
    y6, _, s6 = _ffn_fwd(y5, in2[1], out2[1], lg[1, 2], lb[1, 2], "b2")

    dy6, sq = _loss_grad(y6, target, "loss_grad")
    gr = {n: None for n in BIG}
    gg = [[None] * 3 for _ in range(DEPTH)]
    gb = [[None] * 3 for _ in range(DEPTH)]

    dy5, d_in2_b, d_out2_b, gg[1][2], gb[1][2] = _ffn_bwd(dy6, s6, in2[1], out2[1], lg[1, 2], y5b, "b2", BF16)
    dz5, dz5b, gg[1][1], gb[1][1] = _ln_bwd(z5, dy5, lg[1, 1], 1.0, "ln_bwd_attn_b")
    gr["b_w_o"] = _mm_tn(mix_b, dz5b, "d_b_w_o", out_dtype=BF16)
    dmix_b = _mm_nt(dz5b, W["b_w_o"], "d_mix_b")
    dqkv_b, dsink_part = _attn_bwd(qkv_b, dmix_b, o_b, lse_b, slopes, sinks, PATTERNS_B, "attn_b_bwd")
    dq_b = (dqkv_b, 0)
    gr["b_w_q"] = _mm_tn(y4b, dq_b, "d_b_w_q", out_dtype=BF16)
    dy4 = _mm_nt(dq_b, W["b_w_q"], "d_y4", add=dz5, add_scale=ALPHA)
    dy3, d_in1_b, d_out1_b, gg[1][0], gb[1][0] = _ffn_bwd(dy4, s4, in1[1], out1[1], lg[1, 0], y3b, "b1", BF16)
    d_kv_w_rep = _mm_tn(y3b, dqkv_b, "d_kv_w", split=(1, 2))
    gr["kv_w"] = d_kv_w_rep.reshape(D_MODEL, 2, N_KV_B, GROUP_B, HEAD_DIM).sum(axis=3).reshape(D_MODEL, -1).astype(BF16)
    dy3 = _mm_nt(dqkv_b, kv_w_rep, "d_y3_kv", add=dy3, add_scale=1.0, split=(1, 2))
    tok = grads_ready("l1", {("ffn2_w_in", 1): d_in2_b, ("ffn2_w_out", 1): d_out2_b, ("b_w_o", None): gr["b_w_o"],
                             ("b_w_q", None): gr["b_w_q"], ("ffn1_w_in", 1): d_in1_b, ("ffn1_w_out", 1): d_out1_b,
                             ("kv_w", None): gr["kv_w"]}, True)
    lg0 = lg[0] + tok

    dy2, d_in2_a, d_out2_a, gg[0][2], gb[0][2] = _ffn_bwd(dy3, s3, in2[0], out2[0], lg0[2], y2b, "a2", BF16)
    tok = grads_ready("a2", {("ffn2_w_in", 0): d_in2_a, ("ffn2_w_out", 0): d_out2_a}, True)
    lg0 = lg0 + tok
    dz2, dz2b, gg[0][1], gb[0][1] = _ln_bwd(z2, dy2, lg0[1], 1.0, "ln_bwd_attn_a")
    gr["a_w_o"] = _mm_tn(mix_a, dz2b, "d_a_w_o", out_dtype=BF16)
    dmix_a = _mm_nt(dz2b, W["a_w_o"], "d_mix_a")
    dqkv_a, _ = _attn_bwd(qkv_a, dmix_a, o_a, lse_a, slopes, None, PATTERNS_A, "attn_a_bwd")
    gr["a_w_qkv"] = _mm_tn(y1b, dqkv_a, "d_a_w_qkv", split=True, out_dtype=BF16)
    tok = grads_ready("mix", {("a_w_o", None): gr["a_w_o"], ("a_w_qkv", None): gr["a_w_qkv"]}, True)
    lg0 = lg0 + tok
    dy1 = _mm_nt(dqkv_a, W["a_w_qkv"], "d_y1", add=dz2, add_scale=ALPHA, split=True)
    grad_x, d_in1_a, d_out1_a, gg[0][0], gb[0][0] = _ffn_bwd(dy1, s1, in1[0], out1[0], lg0[0], xs, "a1")
    grads_ready("a1", {("ffn1_w_in", 0): d_in1_a, ("ffn1_w_out", 0): d_out1_a}, False)
    gr["ffn1_w_in"] = [d_in1_a, d_in1_b]
    gr["ffn1_w_out"] = [d_out1_a, d_out1_b]
    gr["ffn2_w_in"] = [d_in2_a, d_in2_b]
    gr["ffn2_w_out"] = [d_out2_a, d_out2_b]
    return sq, grad_x, gr, gg, gb, dsink_part


def _grad_item(name, layer, g):
    if name.endswith("w_in"):
        return (g, "col", HALF_FF, _slot, name, layer)
    if name.endswith("w_out"):
        return (g, "row", D_MODEL, None, name, layer)
    if name == "a_w_qkv":
        return (g, "col", QKV_SHARD, lambda q: q, name, None)
    return (g, "row", g.shape[1], None, name, None)


class _GradReducer:
    def __init__(self, c_idx, myq, shard_shapes):
        self.c_idx, self.myq, self.shard_shapes = c_idx, myq, shard_shapes
        self.groups = []

    def begin(self, tag, grads, overlap):
        items = [_grad_item(n, l, g) for (n, l), g in grads.items()]
        kinds, widths, colblocks = [it[1] for it in items], [it[2] for it in items], [it[3] for it in items]
        views = [_grad_view(k, it[0]) for k, it in zip(kinds, items)]
        if overlap:
            lands = [jax.ShapeDtypeStruct((N_DIRECT,) + _piece_shape(k, w, _half_shape(k, v.shape)), BF16)
                     for k, w, v in zip(kinds, widths, views)]
            state, token = _split_start("grad_direct_start_" + tag, _direct_copies(kinds, widths, colblocks), 10 * len(items),
                                        views, lands, views[-1])
            self.groups.append((tag, items, None, state))
            return token[0, 0]
        from_sibling = _pair_exchange(views, kinds, "grad_pair_exchange_" + tag)
        sums = [_pair_sum(k, v, r, self.c_idx, "pair_sum_%s_%d" % (tag, t))
                for t, (k, v, r) in enumerate(zip(kinds, views, from_sibling))]
        self.groups.append((tag, items, sums, None))
        return 0.0

    def _sum_group(self, tag, items, sums, received, direct):
        for t, (it, s, r) in enumerate(zip(items, sums, received)):
            _, k, _, cb, name, layer = it
            own = cb(self.myq) if k == "col" else self.myq
            self.half_done[name] = _chip_sum(k, s, r, own, self.c_idx, self.shard_shapes[name], layer,
                                             self.half_done.get(name), "chip_sum_%s_%d" % (tag, t), direct=direct)

    def finish_first(self, after):
        self.half_done, self.late = {}, []
        started = [after]
        for tag, items, sums, state in self.groups:
            if state is None:
                kinds, widths, colblocks = [it[1] for it in items], [it[2] for it in items], [it[3] for it in items]
                copies = _chip_copies(kinds, widths, colblocks)
                st, token = _split_start("grad_chip_start_" + tag, copies, 3 * len(items), sums,
                                         _chip_land_shapes(sums, kinds, widths), sums[-1])
                self.late.append((tag, items, copies, st))
                started.append(token)
        for tag, items, sums, state in self.groups:
            if state is not None:
                kinds, widths, colblocks = [it[1] for it in items], [it[2] for it in items], [it[3] for it in items]
                views, received = _split_wait("grad_direct_wait_" + tag, _direct_copies(kinds, widths, colblocks), state,
                                              started)
                self._sum_group(tag, items, views, received, True)
        late_names = {it[4] for _, items, _, _ in self.late for it in items}
        names = [n for n in BIG if n not in late_names]
        return dict(zip(names, _share_halves([self.half_done[n] for n in names], "grad_share_halves_first")))

    def finish_rest(self, after):
        names = []
        for tag, items, copies, st in self.late:
            sums, received = _split_wait("grad_chip_wait_" + tag, copies, st, after)
            self._sum_group(tag, items, sums, received, False)
            names += [it[4] for it in items if it[4] not in names]
        return dict(zip(names, _share_halves([self.half_done[n] for n in names], "grad_share_halves_rest")))


def _update(reducer, grad_x, loss, grad_ln_g, grad_ln_b, grad_sinks, ws, ms, vs, small_w, small_m, small_v):
    ln_g, ln_b, b_sinks = small_w
    m_ln_g, m_ln_b, m_b_sinks = small_m
    v_ln_g, v_ln_b, v_b_sinks = small_v

    deltas, new_m, new_v = {}, {}, {}

    def update(some):
        done = []
        for name in some:
            shp = ws[name].shape
            flat = lambda a: a.reshape(-1, shp[-1])
            d, nm, nv = _adamw(flat(ws[name]), flat(some[name]), flat(ms[name]), flat(vs[name]), "adamw_" + name)
            deltas[name], new_m[name], new_v[name] = d.reshape(shp), nm.reshape(shp), nv.reshape(shp)
            done.append(d)
        return done

    grads = reducer.finish_first(grad_x)
    rest = reducer.finish_rest(update(grads))
    update(rest)
    grads.update(rest)
    delta_s, nm_s, nv_s = _adamw(_pack_small(ln_g, ln_b, b_sinks), _pack_small(grad_ln_g, grad_ln_b, grad_sinks),
                                 _pack_small(m_ln_g, m_ln_b, m_b_sinks), _pack_small(v_ln_g, v_ln_b, v_b_sinks), "adamw_small")
    for d, blob in ((grads, None), (deltas, delta_s), (new_m, nm_s), (new_v, nv_s)):
        if blob is None:
            d["ln_g"], d["ln_b"], d["b_sinks"] = grad_ln_g, grad_ln_b, grad_sinks
        else:
            d["ln_g"], d["ln_b"], d["b_sinks"] = _unpack_small(blob, ln_g.shape, b_sinks.shape)

    order = ("ffn1_w_in", "ffn1_w_out", "ffn2_w_in", "ffn2_w_out", "ln_g", "ln_b", "a_w_qkv", "a_w_o", "kv_w", "b_w_q",
             "b_sinks", "b_w_o")
    outs = [loss, grad_x[None]]
    for d in (grads, deltas, new_m, new_v):
        outs += [d[n] for n in order]
    return tuple(outs)
```

```python
import numpy as np
import jax
import jax.numpy as jnp
from jax import lax
from jax.experimental import pallas as pl
from jax.experimental.pallas import tpu as pltpu

F32 = jnp.float32
BF16 = jnp.bfloat16

D_MODEL = 1024
D_FF = 2816
HALF_FF = D_FF // 2
HEAD_DIM = 64
N_HEADS = 16
N_KV_B = 4
GROUP_B = N_HEADS // N_KV_B
DEPTH = 2
ALPHA = (2.0 * DEPTH) ** 0.25
LN_EPS = 1e-5
BLOCK = 128
SLAB = 128
N_SLABS = D_MODEL // SLAB
PATTERNS_A = ((1, 128, 1.0), (4, 128, 4.0), (16, 128, 16.0))
PATTERNS_B = ((1, 127, 1.0),)
NEG = -1e30

ADAM_LR = 0.001
ADAM_B1 = 0.9
ADAM_B2 = 0.999
ADAM_EPS = 1e-08
ADAM_WD = 0.01
ADAM_STEP = 10

N_CHIPS = 4
VMEM_LIMIT = 56 * 1024 * 1024
MESH = pl.DeviceIdType.MESH


def _alibi_slopes(n):
    return np.array([2.0 ** (-8.0 * (h + 1) / n) for h in range(n)], dtype=np.float32)


def _cparams(sem=None, vmem=VMEM_LIMIT):
    return pltpu.CompilerParams(dimension_semantics=sem, vmem_limit_bytes=vmem)


_DIMS = {"nn": ((1,), (0,)), "nt": ((1,), (1,)), "tn": ((0,), (0,))}


def _unlead(x):
    if isinstance(x, tuple):
        return x[0], x[1], x[0].shape[1:]
    return x, None, x.shape


def _bspec(block, imap, lead=None):
    if lead is None:
        return pl.BlockSpec(block, imap)
    return pl.BlockSpec((None,) + tuple(block), lambda *g: (lead,) + tuple(imap(*g)))


def _matmul(a, b, mode, out_dtype, tm, tn, tk, name, add=None, add_scale=1.0, split=False, into=None):
    out_spec = pl.BlockSpec((tm, tn), lambda i, j, k: (i, j))
    base, count = (0, 3) if split is True else (split or (0, 0))
    if mode == "nn":
        a, al, (M, K) = _unlead(a)
        b, bl, (K2, N) = _unlead(b)
        a_spec = _bspec((tm, tk), lambda i, j, k: (i, k), al)
        b_spec = _bspec((tk, tn), lambda i, j, k: (k, j), bl)
        out_struct = jax.ShapeDtypeStruct((M, N), out_dtype)
        if split:
            assert tn == D_MODEL and N == count * tn
            out_spec = pl.BlockSpec((None, tm, tn), lambda i, j, k: (j + base, i, 0))
            out_struct = jax.ShapeDtypeStruct((3, M, tn), out_dtype)
    elif mode == "nt":
        b, bl, (N, K2) = _unlead(b)
        if split:
            assert tk == D_MODEL
            M, K = a.shape[1], count * a.shape[2]
            a_spec = pl.BlockSpec((None, tm, tk), lambda i, j, k: (k + base, i, 0))
        else:
            a, al, (M, K) = _unlead(a)
            a_spec = _bspec((tm, tk), lambda i, j, k: (i, k), al)
        b_spec = _bspec((tn, tk), lambda i, j, k: (j, k), bl)
        out_struct = jax.ShapeDtypeStruct((M, N), out_dtype)
    else:
        a, al, (K, M) = _unlead(a)
        if split:
            assert tn == D_MODEL
            K2, N = b.shape[1], count * b.shape[2]
            b_spec = pl.BlockSpec((None, tk, tn), lambda i, j, k: (j + base, k, 0))
        else:
            b, bl, (K2, N) = _unlead(b)
            b_spec = _bspec((tk, tn), lambda i, j, k: (k, j), bl)
        a_spec = _bspec((tk, tm), lambda i, j, k: (k, i), al)
        out_struct = jax.ShapeDtypeStruct((M, N), out_dtype)
    assert K == K2 and M % tm == 0 and N % tn == 0 and K % tk == 0, (a.shape, b.shape, mode, tm, tn, tk)
    nk = K // tk
    dims = (_DIMS[mode], ((), ()))
    has_add = add is not None

    narrow = out_dtype != F32
    assert not (narrow and has_add)

    def body(*refs):
        if into is not None:
            refs = refs[:2] + refs[3:]
        if has_add:
            a_ref, b_ref, add_ref, o_ref = refs
            acc_ref = o_ref
        elif narrow:
            a_ref, b_ref, o_ref, acc_ref = refs
        else:
            a_ref, b_ref, o_ref = refs
            acc_ref = o_ref
        k = pl.program_id(2)
        part = lax.dot_general(a_ref[...].astype(BF16), b_ref[...].astype(BF16), dims, preferred_element_type=F32)
        if has_add:
            @pl.when(k == 0)
            def _():
                acc_ref[...] = part + add_scale * add_ref[...]
        else:
            @pl.when(k == 0)
            def _():
                acc_ref[...] = part

        @pl.when(k > 0)
        def _():
            acc_ref[...] += part

        if narrow:
            @pl.when(k == nk - 1)
            def _():
                o_ref[...] = acc_ref[...].astype(out_dtype)

    in_specs = [a_spec, b_spec]
    args = [a, b]
    aliases = {}
    if into is not None:
        assert mode == "nn" and split and not has_add
        in_specs.append(pl.BlockSpec(memory_space=pl.ANY))
        args.append(into)
        aliases = {2: 0}
    if has_add:
        in_specs.append(pl.BlockSpec((tm, tn), lambda i, j, k: (i, j)))
        args.append(add)
    return pl.pallas_call(
        body, name=name, grid=(M // tm, N // tn, nk),
        in_specs=in_specs, out_specs=out_spec, out_shape=out_struct, input_output_aliases=aliases,
        scratch_shapes=[pltpu.VMEM((tm, tn), F32)] if narrow else [],
        compiler_params=_cparams(("parallel", "parallel", "arbitrary")),
    )(*args)


def _pick(n, cands):
    for c in cands:
        if n % c == 0:
            return c
    raise ValueError((n, cands))


def _mm_nn(a, b, out_dtype, name, split=False, into=None):
    M, K = _unlead(a)[2]
    N = _unlead(b)[2][1]
    return _matmul(a, b, "nn", out_dtype, _pick(M, (1024, 512, 256)), _pick(N, (1024, 512)), _pick(K, (1024, 512)), name,
                   split=split, into=into)


def _mm_nt(a, b, name, add=None, add_scale=1.0, split=False):
    M, K = (a.shape[1], D_MODEL) if split else _unlead(a)[2]
    N = _unlead(b)[2][0]
    return _matmul(a, b, "nt", F32, _pick(M, (1024, 512, 256)), _pick(N, (1024, 512)),
                   _pick(K, (2816, 1024, 512)), name, add=add, add_scale=add_scale, split=split)


def _mm_tn(a, b, name, split=False, out_dtype=F32):
    K, M = _unlead(a)[2]
    N = D_MODEL if split else _unlead(b)[2][1]
    return _matmul(a, b, "tn", out_dtype, _pick(M, (1024, 1408, 512)), _pick(N, (1408, 1024, 512)),
                   _pick(K, (2048, 1024, 512, 256)), name, split=split)


def _ffn_in(x, w, name, block=None, into=None):
    S = x.shape[0]
    tm = _pick(S, (512, 256))
    w, wl, _ = _unlead(w)

    def body(x_ref, w_ref, t_ref, h_ref):
        acc = jnp.dot(x_ref[...].astype(BF16), w_ref[...], preferred_element_type=F32)
        g = acc[:, :HALF_FF]
        up = acc[:, HALF_FF:]
        sg = jax.nn.sigmoid(g)
        silu = g * sg
        t_ref[:, :HALF_FF] = (up * (sg * (1.0 + g * (1.0 - sg)))).astype(BF16)
        t_ref[:, HALF_FF:] = silu.astype(BF16)
        h_ref[...] = (silu * up).astype(BF16)

    out_shape = [jax.ShapeDtypeStruct((S, 2 * D_FF), BF16), jax.ShapeDtypeStruct((S, D_FF), BF16)]
    if block is None:
        return pl.pallas_call(
            body, name=name, grid=(2, S // tm),
            in_specs=[pl.BlockSpec((tm, D_MODEL), lambda j, i: (i, 0)),
                      _bspec((D_MODEL, D_FF), lambda j, i: (0, j), wl)],
            out_specs=[pl.BlockSpec((tm, D_FF), lambda j, i: (i, j)),
                       pl.BlockSpec((tm, HALF_FF), lambda j, i: (i, j))],
            out_shape=out_shape,
            compiler_params=_cparams(("parallel", "parallel")),
        )(x, w)

    def body_block(blk_ref, x_ref, w_ref, *rest):
        body(x_ref, w_ref, rest[-2], rest[-1])

    in_specs = [pl.BlockSpec((tm, D_MODEL), lambda i, blk: (i, 0)), pl.BlockSpec((D_MODEL, D_FF), lambda i, blk: (0, 0))]
    args = [block.reshape(1).astype(jnp.int32), x, w]
    aliases = {}
    if into is not None:
        in_specs += [pl.BlockSpec(memory_space=pl.ANY)] * 2
        args += list(into)
        aliases = {3: 0, 4: 1}
    return pl.pallas_call(
        body_block, name=name,
        grid_spec=pltpu.PrefetchScalarGridSpec(
            num_scalar_prefetch=1, grid=(S // tm,), in_specs=in_specs,
            out_specs=[pl.BlockSpec((tm, D_FF), lambda i, blk: (i, blk[0])),
                       pl.BlockSpec((tm, HALF_FF), lambda i, blk: (i, blk[0]))]),
        out_shape=out_shape, input_output_aliases=aliases,
        compiler_params=_cparams(("parallel",)),
    )(*args)


def _ffn_bwd_h(dzc, w_out, u, name):
    S = dzc.shape[0]
    tm = _pick(S, (512, 256))
    w_out, wl, _ = _unlead(w_out)

    def body(dz_ref, w_ref, t_ref, du_ref):
        dh = lax.dot_general(dz_ref[...], w_ref[...], (((1,), (1,)), ((), ())), preferred_element_type=F32)
        du_ref[:, :HALF_FF] = (dh * t_ref[:, :HALF_FF].astype(F32)).astype(BF16)
        du_ref[:, HALF_FF:] = (dh * t_ref[:, HALF_FF:].astype(F32)).astype(BF16)

    return pl.pallas_call(
        body, name=name, grid=(2, S // tm),
        in_specs=[pl.BlockSpec((tm, D_MODEL), lambda j, i: (i, 0)),
                  _bspec((HALF_FF, D_MODEL), lambda j, i: (j, 0), wl),
                  pl.BlockSpec((tm, D_FF), lambda j, i: (i, j))],
        out_specs=pl.BlockSpec((tm, D_FF), lambda j, i: (i, j)),
        out_shape=jax.ShapeDtypeStruct((S, 2 * D_FF), BF16),
        compiler_params=_cparams(("parallel", "parallel")),
    )(dzc, w_out, u)


def _mm_ln(a, w, resid, gain, bias, c, name):
    S, K = a.shape
    tm = _pick(S, (512, 256))
    w, wl, _ = _unlead(w)

    def body(a_ref, w_ref, r_ref, g_ref, b_ref, y_ref, yb_ref, z_ref):
        z = ALPHA * r_ref[...] + c * jnp.dot(a_ref[...], w_ref[...], preferred_element_type=F32)
        mu = jnp.mean(z, axis=-1, keepdims=True)
        zc = z - mu
        var = jnp.mean(zc * zc, axis=-1, keepdims=True)
        y = zc * lax.rsqrt(var + LN_EPS) * g_ref[...] + b_ref[...]
        z_ref[...] = z
        y_ref[...] = y
        yb_ref[...] = y.astype(BF16)

    row = pl.BlockSpec((tm, D_MODEL), lambda i: (i, 0))
    vec = pl.BlockSpec((1, D_MODEL), lambda i: (0, 0))
    return pl.pallas_call(
        body, name=name, grid=(S // tm,),
        in_specs=[pl.BlockSpec((tm, K), lambda i: (i, 0)), _bspec((K, D_MODEL), lambda i: (0, 0), wl), row, vec, vec],
        out_specs=[row, row, row],
        out_shape=[jax.ShapeDtypeStruct((S, D_MODEL), F32), jax.ShapeDtypeStruct((S, D_MODEL), BF16),
                   jax.ShapeDtypeStruct((S, D_MODEL), F32)],
        compiler_params=_cparams(("parallel",)),
    )(a, w, resid, gain, bias)


def _ln_bwd(z, dy, gain, c, name):
    S = z.shape[0]
    tm = _pick(S, (512, 256))

    def body(z_ref, dy_ref, g_ref, dz_ref, dzc_ref, gg_ref, gb_ref):
        i = pl.program_id(0)
        zv = z_ref[...]
        dyv = dy_ref[...]
        mu = jnp.mean(zv, axis=-1, keepdims=True)
        zc = zv - mu
        var = jnp.mean(zc * zc, axis=-1, keepdims=True)
        rstd = lax.rsqrt(var + LN_EPS)
        xhat = zc * rstd
        dyg = dyv * g_ref[...]
        m1 = jnp.mean(dyg, axis=-1, keepdims=True)
        m2 = jnp.mean(dyg * xhat, axis=-1, keepdims=True)
        dz = rstd * (dyg - m1 - xhat * m2)
        dz_ref[...] = dz
        dzc_ref[...] = (c * dz).astype(BF16)
        pg = jnp.sum((dyv * xhat).reshape(tm // 8, 8, D_MODEL), axis=0)
        pb = jnp.sum(dyv.reshape(tm // 8, 8, D_MODEL), axis=0)

        @pl.when(i == 0)
        def _():
            gg_ref[...] = pg
            gb_ref[...] = pb

        @pl.when(i > 0)
        def _():
            gg_ref[...] += pg
            gb_ref[...] += pb

    row = pl.BlockSpec((tm, D_MODEL), lambda i: (i, 0))
    part = pl.BlockSpec((8, D_MODEL), lambda i: (0, 0))
    return pl.pallas_call(
        body, name=name, grid=(S // tm,),
        in_specs=[row, row, pl.BlockSpec((1, D_MODEL), lambda i: (0, 0))],
        out_specs=[row, row, part, part],
        out_shape=[jax.ShapeDtypeStruct((S, D_MODEL), F32), jax.ShapeDtypeStruct((S, D_MODEL), BF16),
                   jax.ShapeDtypeStruct((8, D_MODEL), F32), jax.ShapeDtypeStruct((8, D_MODEL), F32)],
        compiler_params=_cparams(("arbitrary",)),
    )(z, dy, gain)


def _loss_grad(y, t, name):
    S = y.shape[0]
    tm = _pick(S, (512, 256))

    def body(y_ref, t_ref, dy_ref, sq_ref):
        i = pl.program_id(0)
        e = y_ref[...] - t_ref[...]
        dy_ref[...] = e * (1.0 / D_MODEL)
        ps = jnp.sum((e * e).reshape(tm // 8, 8, D_MODEL), axis=0)

        @pl.when(i == 0)
        def _():
            sq_ref[...] = ps

        @pl.when(i > 0)
        def _():
            sq_ref[...] += ps

    row = pl.BlockSpec((tm, D_MODEL), lambda i: (i, 0))
    return pl.pallas_call(
        body, name=name, grid=(S // tm,),
        in_specs=[row, row], out_specs=[row, pl.BlockSpec((8, D_MODEL), lambda i: (0, 0))],
        out_shape=[jax.ShapeDtypeStruct((S, D_MODEL), F32), jax.ShapeDtypeStruct((8, D_MODEL), F32)],
        compiler_params=_cparams(("arbitrary",)),
    )(y, t)


def _rows(start, d):
    if d == 1:
        return pl.ds(pl.multiple_of(start, BLOCK), BLOCK)
    return pl.ds(start, BLOCK, stride=d)


def _ld(ref, start, d):
    return ref[_rows(start, d), :]


def _ld3(ref, lead, start, d):
    return ref[lead, _rows(start, d), :]


def _st3(ref, lead, start, d, val):
    ref[lead, _rows(start, d), :] = val


def _acc3(ref, lead, start, d, val):
    ref[lead, _rows(start, d), :] = ref[lead, _rows(start, d), :] + val


def _band_consts(slope0, slope1, maxd, scale):
    row = lax.broadcasted_iota(jnp.int32, (2 * BLOCK, 2 * BLOCK), 0)
    kj = lax.broadcasted_iota(jnp.int32, (2 * BLOCK, 2 * BLOCK), 1)
    top = row < BLOCK
    dist = BLOCK + jnp.where(top, row, row - BLOCK) - kj
    slope = jnp.where(top, slope0, slope1)
    base = jnp.where((dist >= 0) & (dist <= maxd), -(slope * (dist.astype(F32) * scale)), NEG)
    return base, kj < BLOCK


def _stack_heads(x, lo):
    return jnp.concatenate([jnp.where(lo, x, 0.0), jnp.where(lo, 0.0, x)], axis=0)


def _unstack_heads(x2, lo):
    return jnp.where(lo, x2[:BLOCK], x2[BLOCK:])


def _scores(q2, k2, base, prev_keys, first):
    s = lax.dot_general(q2, k2, (((1,), (1,)), ((), ())), preferred_element_type=F32) * (HEAD_DIM ** -0.5) + base
    return jnp.where(jnp.logical_and(prev_keys, first), NEG, s)


def _softmax_weights(ls):
    mx = ls[0]
    for l in ls[1:]:
        mx = jnp.maximum(mx, l)
    es = [jnp.exp(l - mx) for l in ls]
    tot = es[0]
    for e in es[1:]:
        tot = tot + e
    inv = 1.0 / tot
    return [e * inv for e in es]


def _attn_fwd(qkv, slopes, sinks, patterns, name):
    S = qkv.shape[1]
    npat = len(patterns)
    has_sink = sinks is not None
    if not has_sink:
        sinks = jnp.zeros((N_HEADS,), F32)
    rows_c = 256

    def body(slopes_ref, sinks_ref, x_ref, mix_ref, o_ref, lse_ref, o_scr, lse_scr):
        p = pl.program_id(0)
        lo = lax.broadcasted_iota(jnp.int32, (BLOCK, SLAB), 1) < HEAD_DIM
        top1 = lax.broadcasted_iota(jnp.int32, (2 * BLOCK, 1), 0) < BLOCK
        sk2 = jnp.where(top1, sinks_ref[2 * p], sinks_ref[2 * p + 1])
        for pi, (d, maxd, scale) in enumerate(patterns):
            nb = S // d // BLOCK
            base, prev_keys = _band_consts(slopes_ref[2 * p], slopes_ref[2 * p + 1], maxd, scale)

            def blk(t, carry, pi=pi, d=d, nb=nb, base=base, prev_keys=prev_keys):
                r = t // nb
                n = t - r * nb
                start = r + (d * BLOCK) * n
                prev = jnp.where(n > 0, start - d * BLOCK, start)
                q2 = _stack_heads(_ld3(x_ref, 0, start, d), lo).astype(BF16)
                k2 = jnp.concatenate([_ld3(x_ref, 1, prev, d), _ld3(x_ref, 1, start, d)], axis=0).astype(BF16)
                v2 = jnp.concatenate([_ld3(x_ref, 2, prev, d), _ld3(x_ref, 2, start, d)], axis=0).astype(BF16)
                s = _scores(q2, k2, base, prev_keys, n == 0)
                m = jnp.max(s, axis=-1, keepdims=True)
                if has_sink:
                    m = jnp.maximum(m, sk2)
                e = jnp.exp(s - m)
                den = jnp.sum(e, axis=-1, keepdims=True)
                if has_sink:
                    den = den + jnp.exp(sk2 - m)
                o2 = jnp.dot((e / den).astype(BF16), v2, preferred_element_type=F32)
                _st3(o_scr, pi, start, d, _unstack_heads(o2, lo))
                _st3(lse_scr, pi, start, d, _unstack_heads(m + jnp.log(den), lo))
                return carry

            lax.fori_loop(0, d * nb, blk, 0, unroll=8)

        lane_c = lax.broadcasted_iota(jnp.int32, (rows_c, SLAB), 1)

        def comb(ci, carry):
            rows = pl.ds(pl.multiple_of(ci * rows_c, rows_c), rows_c)
            ls = [lse_scr[i, rows, :] for i in range(npat)]
            packed = jnp.zeros((rows_c, SLAB), F32)
            for i in range(npat):
                o_ref[i, rows, :] = o_scr[i, rows, :].astype(BF16)
                packed = jnp.where(lane_c == 2 * i, ls[i][:, :1], packed)
                packed = jnp.where(lane_c == 2 * i + 1, ls[i][:, HEAD_DIM:HEAD_DIM + 1], packed)
            lse_ref[rows, :] = packed
            if npat == 1:
                mix_ref[rows, :] = o_scr[0, rows, :].astype(BF16)
            else:
                ws = _softmax_weights(ls)
                acc = ws[0] * o_scr[0, rows, :]
                for i in range(1, npat):
                    acc = acc + ws[i] * o_scr[i, rows, :]
                mix_ref[rows, :] = acc.astype(BF16)
            return carry

        lax.fori_loop(0, S // rows_c, comb, 0)

    smem = pl.BlockSpec(memory_space=pltpu.SMEM)
    return pl.pallas_call(
        body, name=name, grid=(N_SLABS,),
        in_specs=[smem, smem, pl.BlockSpec((3, S, SLAB), lambda p: (0, 0, p))],
        out_specs=[pl.BlockSpec((S, SLAB), lambda p: (0, p)), pl.BlockSpec((npat, S, SLAB), lambda p: (0, 0, p)),
                   pl.BlockSpec((None, S, SLAB), lambda p: (p, 0, 0))],
        out_shape=[jax.ShapeDtypeStruct((S, D_MODEL), BF16), jax.ShapeDtypeStruct((npat, S, D_MODEL), BF16),
                   jax.ShapeDtypeStruct((N_SLABS, S, SLAB), F32)],
        scratch_shapes=[pltpu.VMEM((npat, S, SLAB), F32), pltpu.VMEM((npat, S, SLAB), F32)],
        compiler_params=_cparams(("arbitrary",)),
    )(slopes, sinks, qkv)


def _attn_bwd(qkv, dout, o, lse, slopes, sinks, patterns, name):
    S = qkv.shape[1]
    npat = len(patterns)
    has_sink = sinks is not None
    if not has_sink:
        sinks = jnp.zeros((N_HEADS,), F32)
    rows_c = 256

    def headsum(x, lo):
        s0 = jnp.sum(jnp.where(lo, x, 0.0), axis=-1, keepdims=True)
        s1 = jnp.sum(jnp.where(lo, 0.0, x), axis=-1, keepdims=True)
        return jnp.where(lo, s0, s1)

    def body(slopes_ref, sinks_ref, x_ref, do_ref, o_ref, lsep_ref, dxo_ref, dsink_ref, dbar_ref, sacc_ref, lse_ref, dx_ref):
        p = pl.program_id(0)
        lo = lax.broadcasted_iota(jnp.int32, (BLOCK, SLAB), 1) < HEAD_DIM
        lo_c = lax.broadcasted_iota(jnp.int32, (rows_c, SLAB), 1) < HEAD_DIM
        top1 = lax.broadcasted_iota(jnp.int32, (2 * BLOCK, 1), 0) < BLOCK
        sk2 = jnp.where(top1, sinks_ref[2 * p], sinks_ref[2 * p + 1])

        def prep(ci, carry):
            rows = pl.ds(pl.multiple_of(ci * rows_c, rows_c), rows_c)
            dov = do_ref[rows, :]
            dx_ref[:, rows, :] = jnp.zeros((3, rows_c, SLAB), F32)
            packed = lsep_ref[rows, :]
            ls = [jnp.where(lo_c, packed[:, 2 * i:2 * i + 1], packed[:, 2 * i + 1:2 * i + 2]) for i in range(npat)]
            for i in range(npat):
                lse_ref[i, rows, :] = ls[i]
            if npat == 1:
                dbar_ref[rows, :] = headsum(dov * o_ref[0, rows, :].astype(F32), lo_c)
            else:
                ws = _softmax_weights(ls)
                acc = ws[0] * headsum(dov * o_ref[0, rows, :].astype(F32), lo_c)
                for i in range(1, npat):
                    acc = acc + ws[i] * headsum(dov * o_ref[i, rows, :].astype(F32), lo_c)
                dbar_ref[rows, :] = acc
            return carry

        lax.fori_loop(0, S // rows_c, prep, 0)
        sacc_ref[...] = jnp.zeros((BLOCK, SLAB), F32)

        for pi, (d, maxd, scale) in enumerate(patterns):
            nb = S // d // BLOCK
            base, prev_keys = _band_consts(slopes_ref[2 * p], slopes_ref[2 * p + 1], maxd, scale)

            def blk(t, carry, pi=pi, d=d, nb=nb, base=base, prev_keys=prev_keys):
                r = t // nb
                n = t - r * nb
                start = r + (d * BLOCK) * n
                prev = jnp.where(n > 0, start - d * BLOCK, start)
                q2 = _stack_heads(_ld3(x_ref, 0, start, d), lo).astype(BF16)
                k2 = jnp.concatenate([_ld3(x_ref, 1, prev, d), _ld3(x_ref, 1, start, d)], axis=0).astype(BF16)
                v2 = jnp.concatenate([_ld3(x_ref, 2, prev, d), _ld3(x_ref, 2, start, d)], axis=0).astype(BF16)
                ls = [_ld3(lse_ref, i, start, d) for i in range(npat)]
                w = _softmax_weights(ls)[pi] if npat > 1 else 1.0
                do2 = _stack_heads(w * _ld(do_ref, start, d), lo).astype(BF16)
                dl = w * _ld(dbar_ref, start, d)
                lse2 = jnp.concatenate([ls[pi][:, :1], ls[pi][:, HEAD_DIM:HEAD_DIM + 1]], axis=0)
                dl2 = jnp.concatenate([dl[:, :1], dl[:, HEAD_DIM:HEAD_DIM + 1]], axis=0)
                s = _scores(q2, k2, base, prev_keys, n == 0)
                pr = jnp.exp(s - lse2)
                dp = lax.dot_general(do2, v2, (((1,), (1,)), ((), ())), preferred_element_type=F32)
                ds = (pr * (dp - dl2) * (HEAD_DIM ** -0.5)).astype(BF16)
                dq2 = jnp.dot(ds, k2, preferred_element_type=F32)
                dk2 = lax.dot_general(ds, q2, (((0,), (0,)), ((), ())), preferred_element_type=F32)
                dv2 = lax.dot_general(pr.astype(BF16), do2, (((0,), (0,)), ((), ())), preferred_element_type=F32)
                _acc3(dx_ref, 0, start, d, _unstack_heads(dq2, lo))
                _acc3(dx_ref, 1, prev, d, dk2[:BLOCK])
                _acc3(dx_ref, 1, start, d, dk2[BLOCK:])
                _acc3(dx_ref, 2, prev, d, dv2[:BLOCK])
                _acc3(dx_ref, 2, start, d, dv2[BLOCK:])
                if has_sink:
                    sacc_ref[...] += _unstack_heads(-jnp.exp(sk2 - lse2) * dl2, lo)
                return carry

            lax.fori_loop(0, d * nb, blk, 0, unroll=4)

        dsink_ref[...] = jnp.broadcast_to(jnp.sum(sacc_ref[...], axis=0, keepdims=True), (8, SLAB))

        def emit(ci, carry):
            rows = pl.ds(pl.multiple_of(ci * rows_c, rows_c), rows_c)
            dxo_ref[:, rows, :] = dx_ref[:, rows, :].astype(BF16)
            return carry

        lax.fori_loop(0, S // rows_c, emit, 0)

    smem = pl.BlockSpec(memory_space=pltpu.SMEM)
    return pl.pallas_call(
        body, name=name, grid=(N_SLABS,),
        in_specs=[smem, smem, pl.BlockSpec((3, S, SLAB), lambda p: (0, 0, p)), pl.BlockSpec((S, SLAB), lambda p: (0, p)),
                  pl.BlockSpec((npat, S, SLAB), lambda p: (0, 0, p)), pl.BlockSpec((None, S, SLAB), lambda p: (p, 0, 0))],
        out_specs=[pl.BlockSpec((3, S, SLAB), lambda p: (0, 0, p)), pl.BlockSpec((None, 8, SLAB), lambda p: (p, 0, 0))],
        out_shape=[jax.ShapeDtypeStruct((3, S, D_MODEL), BF16), jax.ShapeDtypeStruct((N_SLABS, 8, SLAB), F32)],
        scratch_shapes=[pltpu.VMEM((S, SLAB), F32), pltpu.VMEM((BLOCK, SLAB), F32), pltpu.VMEM((npat, S, SLAB), F32),
                        pltpu.VMEM((3, S, SLAB), F32)],
        compiler_params=_cparams(("arbitrary",)),
    )(slopes, sinks, qkv, dout, o, lse)


def _place():
    x, y, c = lax.axis_index("x"), lax.axis_index("y"), lax.axis_index("c")
    return x, y, c, 2 * x + y


def _other_chips(x, y):
    return [(1 - x, y), (x, 1 - y), (1 - x, 1 - y)]


HBM_SPEC = pl.BlockSpec(memory_space=pl.ANY)


def _slot(q):
    return 2 * (q % 2) + q // 2


BIG = ("ffn1_w_in", "ffn1_w_out", "ffn2_w_in", "ffn2_w_out", "a_w_qkv", "a_w_o", "kv_w", "b_w_q", "b_w_o")
QKV_SHARD = 3 * D_MODEL // N_CHIPS
ROW_SHARD = D_MODEL // N_CHIPS


LAYER0_ITEMS = (("ffn1_w_in", 0), ("ffn1_w_out", 0), ("a_w_qkv", None), ("a_w_o", None), ("ffn2_w_in", 0),
                ("ffn2_w_out", 0), ("kv_w", None))
LAYER1_ITEMS = (("ffn1_w_in", 1), ("ffn1_w_out", 1), ("b_w_q", None), ("b_w_o", None), ("ffn2_w_in", 1),
                ("ffn2_w_out", 1))
OUT_SHARD = D_FF // N_CHIPS


FIRST_OWN = ("ffn1_w_in_own", 0)
FIRST_OTHER = ("ffn1_w_in_other", 0)
GATHER_GROUPS = (("other", (FIRST_OTHER,)),
                 ("out0", (("ffn1_w_out", 0),)),
                 ("mixer_a", (("a_w_qkv", None), ("a_w_o", None))),
                 ("ffn2_0", (("ffn2_w_in", 0), ("ffn2_w_out", 0), ("kv_w", None))),
                 ("layer1", LAYER1_ITEMS))


def _full_shape(name):
    if name.startswith("ffn1_w_in_"):
        return (D_MODEL, D_FF)
    if name.endswith("w_in"):
        return (D_MODEL, 2 * D_FF)
    if name.endswith("w_out"):
        return (D_FF, D_MODEL)
    if name == "a_w_qkv":
        return (D_MODEL, 3 * D_MODEL)
    if name == "kv_w":
        return (N_CHIPS, 2, ROW_SHARD // 2, 2 * N_KV_B * HEAD_DIM)
    return (N_CHIPS, 2, ROW_SHARD // 2, D_MODEL)


def _gather_src(item, ref, c):
    name, _ = item
    if name.endswith("w_in") or name.startswith("ffn1_w_in_"):
        return ref.at[pl.ds(c * (D_MODEL // 2), D_MODEL // 2)]
    if name.endswith("w_out"):
        return ref.at[pl.ds(c * (OUT_SHARD // 2), OUT_SHARD // 2)]
    if name == "a_w_qkv":
        return ref.at[0, pl.ds(c * (D_MODEL // 2), D_MODEL // 2)]
    if name == "kv_w":
        return ref.at[pl.ds(c * (ROW_SHARD // 2), ROW_SHARD // 2)]
    return ref.at[0, pl.ds(c * (ROW_SHARD // 2), ROW_SHARD // 2)]


def _gather_dst(item, ref, q, c):
    name, _ = item
    if name.startswith("ffn1_w_in_"):
        return ref.at[pl.ds(c * (D_MODEL // 2), D_MODEL // 2), pl.ds((q // 2) * HALF_FF, HALF_FF)]
    if name.endswith("w_in"):
        return ref.at[pl.ds(c * (D_MODEL // 2), D_MODEL // 2), pl.ds(_slot(q) * HALF_FF, HALF_FF)]
    if name.endswith("w_out"):
        return ref.at[pl.ds(q * OUT_SHARD + c * (OUT_SHARD // 2), OUT_SHARD // 2)]
    if name == "a_w_qkv":
        return ref.at[pl.ds(c * (D_MODEL // 2), D_MODEL // 2), pl.ds(q * QKV_SHARD, QKV_SHARD)]
    return ref.at[q, c]


def _all_gather(items, shards, small, from_chips=(0, 1, 2)):
    n = len(items)
    r = small.shape[0]
    per = 8

    def body(*refs):
        srcs, small_ref = refs[:n], refs[n]
        dsts, s_ref = refs[n + 1:2 * n + 1], refs[2 * n + 1]
        send_sems, recv_sems = refs[2 * n + 2:]
        x, y, c, myq = _place()
        sibling = (x, y, 1 - c)
        chips = _other_chips(x, y)

        def big(t, k, src, q, h, to):
            return pltpu.make_async_remote_copy(src_ref=src, dst_ref=_gather_dst(items[t], dsts[t], q, h),
                                                send_sem=send_sems.at[per * t + k], recv_sem=recv_sems.at[per * t + k],
                                                device_id=to, device_id_type=MESH)

        def tiny(k, q, to):
            return pltpu.make_async_remote_copy(src_ref=small_ref, dst_ref=s_ref.at[q], send_sem=send_sems.at[per * n + k],
                                                recv_sem=recv_sems.at[per * n + k], device_id=to, device_id_type=MESH)

        first = []
        for j, chip in enumerate(chips):
            if j in from_chips:
                first += [big(t, j, _gather_src(items[t], srcs[t], c), myq, c, (*chip, c)) for t in range(n)]
            first.append(tiny(j, myq, (*chip, c)))
        own = [big(t, 6 + h, _gather_src(items[t], srcs[t], h), myq, h, sibling) for t in range(n) for h in (0, 1)]
        own.append(tiny(3, myq, sibling))
        for cp in first + own:
            cp.start()
        passed = []
        for j, (cx, cy) in enumerate(chips):
            q = 2 * cx + cy
            for t in range(n):
                if j in from_chips:
                    src = _gather_src(items[t], srcs[t], c)
                    big(t, j, src, q, c, sibling).wait_recv()
                    fwd = big(t, 3 + j, _gather_dst(items[t], dsts[t], q, c), q, c, sibling)
                    fwd.start()
                    passed.append(fwd)
        for j, (cx, cy) in enumerate(chips):
            q = 2 * cx + cy
            for t in range(n):
                if j in from_chips:
                    big(t, 3 + j, _gather_src(items[t], srcs[t], c), q, 1 - c, sibling).wait_recv()
            tiny(j, q, sibling).wait_recv()
        for cp in own:
            cp.wait_recv()
        for cp in first + passed + own:
            cp.wait_send()

    outs = pl.pallas_call(
        body, name="gather_first_block",
        in_specs=[HBM_SPEC] * (n + 1), out_specs=[HBM_SPEC] * (n + 1),
        out_shape=[jax.ShapeDtypeStruct(_full_shape(name), BF16) for name, _ in items]
        + [jax.ShapeDtypeStruct((N_CHIPS, r, 128), F32)],
        scratch_shapes=[pltpu.SemaphoreType.DMA((per * n + 4,)), pltpu.SemaphoreType.DMA((per * n + 4,))],
    )(*[shards[item] for item in items], small)
    return list(outs[:n]), outs[n]


SEM_SPEC = pl.BlockSpec(memory_space=pltpu.SEMAPHORE)
DATAFLOW = pltpu.SideEffectType.DATAFLOW_SIDE_EFFECTING
PER_ITEM = 8


def _split_start_groups(name, groups, after):
    sizes = [(len(g[2]), len(g[3])) for g in groups]
    n_arr = sum(n + m for n, m in sizes)
    n_grp = len(groups)

    def body(*refs):
        arrs = refs[:n_arr]
        sems = refs[n_arr + 1:n_arr + 1 + 2 * n_grp]
        token = refs[-1]
        at = 0
        for g, (copies, _, _, _) in enumerate(groups):
            n, m = sizes[g]
            srcs, lands = arrs[at:at + n], arrs[at + n:at + n + m]
            at += n + m
            for src, dst_there, _, s, peer in copies(srcs, lands):
                pltpu.make_async_remote_copy(src_ref=src, dst_ref=dst_there, send_sem=sems[2 * g].at[s],
                                             recv_sem=sems[2 * g + 1].at[s], device_id=peer, device_id_type=MESH).start()
        token[...] = jnp.zeros_like(token)

    arrays, sem_shapes = [], []
    for _, n_sems, sources, land_shapes in groups:
        arrays += [pltpu.with_memory_space_constraint(a, pltpu.HBM) for a in sources]
        arrays += [pltpu.with_memory_space_constraint(lax.empty(s.shape, s.dtype), pltpu.HBM) for s in land_shapes]
        sem_shapes += [pltpu.SemaphoreType.DMA((n_sems,)), pltpu.SemaphoreType.DMA((n_sems,))]
    hbm = pl.BlockSpec(memory_space=pltpu.HBM)
    outs = pl.pallas_call(
        body, name=name,
        in_specs=[hbm] * n_arr + [HBM_SPEC],
        out_specs=[SEM_SPEC] * (2 * n_grp) + [hbm] * n_arr + [pl.BlockSpec(memory_space=pltpu.VMEM)],
        out_shape=sem_shapes + [pltpu.HBM(a.shape, a.dtype) for a in arrays] + [jax.ShapeDtypeStruct((8, 128), F32)],
        input_output_aliases={i: 2 * n_grp + i for i in range(n_arr)},
        compiler_params=pltpu.CompilerParams(has_side_effects=DATAFLOW),
    )(*arrays, after)
    states, at = [], 2 * n_grp
    for g, (n, m) in enumerate(sizes):
        states.append((outs[2 * g], outs[2 * g + 1], list(outs[at:at + n]), list(outs[at + n:at + n + m])))
        at += n + m
    return states, outs[-1]


def _split_start(name, copies, n_sems, sources, land_shapes, after):
    states, token = _split_start_groups(name, [(copies, n_sems, sources, land_shapes)], after)
    return states[0], token


def _split_wait(name, copies, state, after):
    send_sems, recv_sems, srcs_thru, lands_thru = state
    n, m = len(srcs_thru), len(lands_thru)
    after = list(after) if isinstance(after, (list, tuple)) else [after]

    def body(*refs):
        srcs, lands = refs[:n], refs[n:n + m]
        send_sems, recv_sems = refs[n + m], refs[n + m + 1]
        for src, _, dst_here, s, peer in copies(srcs, lands):
            cp = pltpu.make_async_remote_copy(src_ref=src, dst_ref=dst_here, send_sem=send_sems.at[s], recv_sem=recv_sems.at[s],
                                              device_id=peer, device_id_type=MESH)
            cp.wait_send()
            cp.wait_recv()

    hbm = pl.BlockSpec(memory_space=pltpu.HBM)
    outs = pl.pallas_call(
        body, name=name,
        in_specs=[hbm] * (n + m) + [SEM_SPEC, SEM_SPEC] + [HBM_SPEC] * len(after),
        out_specs=[hbm] * (n + m),
        out_shape=[pltpu.HBM(a.shape, a.dtype) for a in srcs_thru + lands_thru],
        input_output_aliases={i: i for i in range(n + m)},
        compiler_params=pltpu.CompilerParams(has_side_effects=DATAFLOW),
    )(*srcs_thru, *lands_thru, send_sems, recv_sems, *after)
    return list(outs[:n]), list(outs[n:])


def _gather_copies(items):
    def copies(srcs, lands):
        x, y, c, myq = _place()
        out = []
        for t, item in enumerate(items):
            for h in (0, 1):
                src = _gather_src(item, srcs[t], h)
                for j, (cx, cy) in enumerate(_other_chips(x, y)):
                    if item == FIRST_OTHER and j == 0:
                        continue
                    out.append((src, _gather_dst(item, lands[t], myq, h), _gather_dst(item, lands[t], 2 * cx + cy, h),
                                PER_ITEM * t + 2 * j + h, (cx, cy, c)))
                if item != FIRST_OTHER:
                    out.append((src, _gather_dst(item, lands[t], myq, h), _gather_dst(item, lands[t], myq, h),
                                PER_ITEM * t + 6 + h, (x, y, 1 - c)))
        return out
    return copies


def _gather_start(items, shards, after):
    lands = [jax.ShapeDtypeStruct(_full_shape(name), BF16) for name, _ in items]
    return _split_start("gather_layer1_start", _gather_copies(items), PER_ITEM * len(items),
                        [shards[item] for item in items], lands, after)


def _gather_wait(items, state, after):
    return _split_wait("gather_layer1_wait", _gather_copies(items), state, after)[1]


def _small_all_reduce(v):
    r = v.shape[0]

    def body(v_ref, o_ref, buf_ref, send_sems, recv_sems):
        x, y, c, _ = _place()
        me = 4 * x + 2 * y + c
        buf_ref[me] = v_ref[...]
        copies = []
        for k in range(1, 8):
            fx, fy, fc = (k >> 2) & 1, (k >> 1) & 1, k & 1
            to = (x ^ fx, y ^ fy, c ^ fc)
            cp = pltpu.make_async_remote_copy(src_ref=v_ref, dst_ref=buf_ref.at[me], send_sem=send_sems.at[k - 1],
                                              recv_sem=recv_sems.at[k - 1], device_id=to, device_id_type=MESH)
            cp.start()
            copies.append(cp)
        for k in range(1, 8):
            fx, fy, fc = (k >> 2) & 1, (k >> 1) & 1, k & 1
            src_dev = 4 * (x ^ fx) + 2 * (y ^ fy) + (c ^ fc)
            pltpu.make_async_remote_copy(src_ref=v_ref, dst_ref=buf_ref.at[src_dev], send_sem=send_sems.at[k - 1],
                                         recv_sem=recv_sems.at[k - 1], device_id=(x, y, c), device_id_type=MESH).wait_recv()
        for cp in copies:
            cp.wait_send()
        tot = buf_ref[0]
        for i in range(1, 8):
            tot = tot + buf_ref[i]
        o_ref[...] = tot

    vm = pl.BlockSpec(memory_space=pltpu.VMEM)
    return pl.pallas_call(
        body, name="small_all_reduce", in_specs=[vm], out_specs=vm,
        out_shape=jax.ShapeDtypeStruct((r, 128), F32),
        scratch_shapes=[pltpu.VMEM((8, r, 128), F32), pltpu.SemaphoreType.DMA((7,)), pltpu.SemaphoreType.DMA((7,))],
    )(v)


def _grad_view(kind, g):
    if kind == "col":
        return g.reshape(2, g.shape[0] // 2, g.shape[1])
    return g.reshape(N_CHIPS, 2, g.shape[0] // (2 * N_CHIPS), g.shape[1])


def _half_of(kind, ref, h):
    return ref.at[h] if kind == "col" else ref.at[:, h]


def _half_shape(kind, view_shape):
    return view_shape[1:] if kind == "col" else (view_shape[0],) + view_shape[2:]


def _piece_of(kind, width, colblock, ref, q):
    if kind == "col":
        return ref.at[:, pl.ds(colblock(q) * width, width)]
    return ref.at[q]


def _piece_shape(kind, width, half_shape):
    return (half_shape[0], width) if kind == "col" else half_shape[1:]


def _pair_exchange(views, kinds, name):
    n = len(views)

    def body(*refs):
        ins, outs = refs[:n], refs[n:2 * n]
        send_sems, recv_sems = refs[2 * n:]
        x, y, c, _ = _place()
        cps = []
        for t in range(n):
            cp = pltpu.make_async_remote_copy(src_ref=_half_of(kinds[t], ins[t], 1 - c), dst_ref=outs[t],
                                              send_sem=send_sems.at[t], recv_sem=recv_sems.at[t],
                                              device_id=(x, y, 1 - c), device_id_type=MESH)
            cp.start()
            cps.append(cp)
        for cp in cps:
            cp.wait()

    return pl.pallas_call(
        body, name=name, in_specs=[HBM_SPEC] * n, out_specs=[HBM_SPEC] * n,
        out_shape=[jax.ShapeDtypeStruct(_half_shape(k, v.shape), v.dtype) for k, v in zip(kinds, views)],
        scratch_shapes=[pltpu.SemaphoreType.DMA((n,)), pltpu.SemaphoreType.DMA((n,))],
    )(*views)


def _pair_sum(kind, view, recv, c, name):
    hs = recv.shape
    N = hs[-1]
    rows = hs[-2]
    tr = _pick(rows, (512, 352, 128))
    tn = _pick(N, (1408, 1024, 512))

    def body(c_ref, p_ref, r_ref, s_ref):
        s_ref[...] = (p_ref[...] + r_ref[...]).astype(BF16)

    if kind == "col":
        grid = (rows // tr, N // tn)
        mine = pl.BlockSpec((None, tr, tn), lambda i, j, c_ref: (c_ref[0], i, j))
        blk = pl.BlockSpec((tr, tn), lambda i, j, c_ref: (i, j))
        sem = ("parallel", "parallel")
    else:
        grid = (N_CHIPS, rows // tr, N // tn)
        mine = pl.BlockSpec((None, None, tr, tn), lambda q, i, j, c_ref: (q, c_ref[0], i, j))
        blk = pl.BlockSpec((None, tr, tn), lambda q, i, j, c_ref: (q, i, j))
        sem = ("parallel", "parallel", "parallel")
    return pl.pallas_call(
        body, name=name,
        grid_spec=pltpu.PrefetchScalarGridSpec(num_scalar_prefetch=1, grid=grid, in_specs=[mine, blk], out_specs=blk),
        out_shape=jax.ShapeDtypeStruct(hs, BF16),
        compiler_params=_cparams(sem),
    )(c.reshape(1).astype(jnp.int32), view, recv)


def _chip_copies(kinds, widths, colblocks):
    def copies(srcs, lands):
        x, y, c, _ = _place()
        out = []
        for j, (cx, cy) in enumerate(_other_chips(x, y)):
            for t in range(len(kinds)):
                out.append((_piece_of(kinds[t], widths[t], colblocks[t], srcs[t], 2 * cx + cy), lands[t].at[j],
                            lands[t].at[j], 3 * t + j, (cx, cy, c)))
        return out
    return copies


def _chip_land_shapes(sums, kinds, widths):
    return [jax.ShapeDtypeStruct((3,) + _piece_shape(k, w, s.shape), BF16) for k, w, s in zip(kinds, widths, sums)]


def _chip_exchange(sums, kinds, widths, colblocks, name):
    n = len(sums)
    copies = _chip_copies(kinds, widths, colblocks)

    def body(*refs):
        send_sems, recv_sems = refs[2 * n:]
        cps = [pltpu.make_async_remote_copy(src_ref=src, dst_ref=dst, send_sem=send_sems.at[s], recv_sem=recv_sems.at[s],
                                            device_id=peer, device_id_type=MESH)
               for src, dst, _, s, peer in copies(refs[:n], refs[n:2 * n])]
        for cp in cps:
            cp.start()
        for cp in cps:
            cp.wait()

    return pl.pallas_call(
        body, name=name, in_specs=[HBM_SPEC] * n, out_specs=[HBM_SPEC] * n,
        out_shape=_chip_land_shapes(sums, kinds, widths),
        scratch_shapes=[pltpu.SemaphoreType.DMA((3 * n,)), pltpu.SemaphoreType.DMA((3 * n,))],
    )(*sums)


N_DIRECT = 7


def _direct_piece(kind, width, colblock, view_ref, q, h):
    if kind == "col":
        return view_ref.at[h, :, pl.ds(colblock(q) * width, width)]
    return view_ref.at[q, h]


def _direct_copies(kinds, widths, colblocks):
    def copies(srcs, lands):
        x, y, c, myq = _place()
        out = []
        for t in range(len(kinds)):
            def piece(q, h, t=t):
                return _direct_piece(kinds[t], widths[t], colblocks[t], srcs[t], q, h)
            for j, (cx, cy) in enumerate(_other_chips(x, y)):
                for h in (0, 1):
                    out.append((piece(2 * cx + cy, h), lands[t].at[2 * j + c], lands[t].at[2 * j + h],
                                10 * t + 3 * j + c + h, (cx, cy, h)))
            out.append((piece(myq, 1 - c), lands[t].at[6], lands[t].at[6], 10 * t + 9, (x, y, 1 - c)))
        return out
    return copies


def _chip_sum(kind, own_src, recv, block_idx, c, shard_shape, layer, into, name, direct=False):
    n_recv, rows, N = recv.shape
    tr = _pick(rows, (512, 352, 128))
    tn = _pick(N, (1408, 1024, 768, 512))
    ni, nj = rows // tr, N // tn

    def body(q_ref, s_ref, r_ref, *rest):
        o_ref = rest[-1]
        tot = s_ref[...].astype(F32)
        for k in range(n_recv):
            tot = tot + r_ref[k].astype(F32)
        o_ref[...] = tot

    if direct and kind == "col":
        own = pl.BlockSpec((None, tr, tn), lambda i, j, q_ref: (q_ref[1], i, q_ref[0] * nj + j))
    elif direct:
        own = pl.BlockSpec((None, None, tr, tn), lambda i, j, q_ref: (q_ref[0], q_ref[1], i, j))
    elif kind == "col":
        own = pl.BlockSpec((tr, tn), lambda i, j, q_ref: (i, q_ref[0] * nj + j))
    else:
        own = pl.BlockSpec((None, tr, tn), lambda i, j, q_ref: (q_ref[0], i, j))
    if len(shard_shape) == 3:
        lead = 0 if layer is None else layer
        out_spec = pl.BlockSpec((None, tr, tn), lambda i, j, q_ref: (lead, q_ref[1] * ni + i, j))
    else:
        out_spec = pl.BlockSpec((tr, tn), lambda i, j, q_ref: (q_ref[1] * ni + i, j))
    in_specs = [own, pl.BlockSpec((n_recv, tr, tn), lambda i, j, q_ref: (0, i, j))]
    s = own_src
    args = [jnp.stack([block_idx, c]).astype(jnp.int32), s, recv]
    aliases = {}
    if into is not None:
        in_specs.append(HBM_SPEC)
        args.append(into)
        aliases = {3: 0}
    return pl.pallas_call(
        body, name=name,
        grid_spec=pltpu.PrefetchScalarGridSpec(num_scalar_prefetch=1, grid=(ni, nj), in_specs=in_specs, out_specs=out_spec),
        out_shape=jax.ShapeDtypeStruct(shard_shape, F32), input_output_aliases=aliases,
        compiler_params=_cparams(("parallel", "parallel")),
    )(*args)


def _half_window(ref, h):
    rows = ref.shape[-2] // 2
    if ref.ndim == 3:
        return ref.at[:, pl.ds(h * rows, rows)]
    return ref.at[pl.ds(h * rows, rows)]


def _share_halves(grads, name):
    n = len(grads)

    def body(*refs):
        outs = refs[n:2 * n]
        send_sems, recv_sems = refs[2 * n:]
        x, y, c, _ = _place()
        cps = []
        for t in range(n):
            cp = pltpu.make_async_remote_copy(src_ref=_half_window(outs[t], c), dst_ref=_half_window(outs[t], c),
                                              send_sem=send_sems.at[t], recv_sem=recv_sems.at[t],
                                              device_id=(x, y, 1 - c), device_id_type=MESH)
            cp.start()
            cps.append(cp)
        for t in range(n):
            cps[t].wait_send()
            pltpu.make_async_remote_copy(src_ref=_half_window(outs[t], c), dst_ref=_half_window(outs[t], 1 - c),
                                         send_sem=send_sems.at[t], recv_sem=recv_sems.at[t],
                                         device_id=(x, y, 1 - c), device_id_type=MESH).wait_recv()

    return pl.pallas_call(
        body, name=name, in_specs=[HBM_SPEC] * n, out_specs=[HBM_SPEC] * n,
        out_shape=[jax.ShapeDtypeStruct(g.shape, F32) for g in grads],
        input_output_aliases={t: t for t in range(n)},
        scratch_shapes=[pltpu.SemaphoreType.DMA((n,)), pltpu.SemaphoreType.DMA((n,))],
    )(*grads)


def _adamw(w, g, m, v, name):
    R, W = w.shape
    tr = _pick(R, (512, 352, 256, 32))

    def body(w_ref, g_ref, m_ref, v_ref, d_ref, nm_ref, nv_ref):
        gv = g_ref[...]
        nm = ADAM_B1 * m_ref[...] + (1.0 - ADAM_B1) * gv
        nv = ADAM_B2 * v_ref[...] + (1.0 - ADAM_B2) * (gv * gv)
        m_hat = nm / (1.0 - ADAM_B1 ** ADAM_STEP)
        v_hat = nv / (1.0 - ADAM_B2 ** ADAM_STEP)
        d_ref[...] = -ADAM_LR * (m_hat / (jnp.sqrt(v_hat) + ADAM_EPS) + ADAM_WD * w_ref[...])
        nm_ref[...] = nm
        nv_ref[...] = nv

    blk = pl.BlockSpec((tr, W), lambda i: (i, 0))
    shp = jax.ShapeDtypeStruct((R, W), F32)
    return pl.pallas_call(
        body, name=name, grid=(R // tr,), in_specs=[blk] * 4, out_specs=[blk] * 3, out_shape=[shp] * 3,
        compiler_params=_cparams(("parallel",)),
    )(w, g, m, v)


SMALL_ROWS = 32


def _pack_small(ln_g, ln_b, sinks):
    rows = jnp.concatenate([ln_g.reshape(-1, 128), ln_b.reshape(-1, 128),
                            jnp.pad(sinks.reshape(1, -1), ((0, 0), (0, 128 - sinks.size)))], axis=0)
    return jnp.pad(rows, ((0, SMALL_ROWS - rows.shape[0]), (0, 0)))


def _unpack_small(s, ln_shape, sink_shape):
    n = ln_shape[0] * ln_shape[1] * ln_shape[2] // 128
    return s[:n].reshape(ln_shape), s[n:2 * n].reshape(ln_shape), s[2 * n, :sink_shape[1]].reshape(sink_shape)


def _ffn_fwd(xin, w_in, w_out, gain, bias, tag):
    u, h = _ffn_in(xin, w_in, "ffn_in_" + tag)
    y, yb, z = _mm_ln(h, w_out, xin, gain, bias, 0.5, "ffn_out_ln_" + tag)
    return y, yb, dict(u=u, h=h, z=z, xin=xin)


def _ffn_bwd(dy, saved, w_in, w_out, gain, xin_b, tag, dw_dtype=F32):
    dz, dzc, gg, gb = _ln_bwd(saved["z"], dy, gain, 0.5, "ln_bwd_" + tag)
    du = _ffn_bwd_h(dzc, w_out, saved["u"], "ffn_bwd_h_" + tag)
    d_w_out = _mm_tn(saved["h"], dzc, "ffn_dwout_" + tag, out_dtype=dw_dtype)
    d_w_in = _mm_tn(xin_b, du, "ffn_dwin_" + tag, out_dtype=dw_dtype)
    dx = _mm_nt(du, w_in, "ffn_dx_" + tag, add=dz, add_scale=ALPHA)
    return dx, d_w_in, d_w_out, gg, gb


def kernel(x, ffn1_w_in, ffn1_w_out, ffn2_w_in, ffn2_w_out, ln_g, ln_b, a_w_qkv, a_w_o, kv_w, b_w_q, b_sinks, b_w_o, loss_target, m_ffn1_w_in, m_ffn1_w_out, m_ffn2_w_in, m_ffn2_w_out, m_ln_g, m_ln_b, m_a_w_qkv, m_a_w_o, m_kv_w, m_b_w_q, m_b_sinks, m_b_w_o, v_ffn1_w_in, v_ffn1_w_out, v_ffn2_w_in, v_ffn2_w_out, v_ln_g, v_ln_b, v_a_w_qkv, v_a_w_o, v_kv_w, v_b_w_q, v_b_sinks, v_b_w_o):
    ws = dict(ffn1_w_in=ffn1_w_in, ffn1_w_out=ffn1_w_out, ffn2_w_in=ffn2_w_in, ffn2_w_out=ffn2_w_out, a_w_qkv=a_w_qkv,
              a_w_o=a_w_o, kv_w=kv_w, b_w_q=b_w_q, b_w_o=b_w_o)
    ms = dict(ffn1_w_in=m_ffn1_w_in, ffn1_w_out=m_ffn1_w_out, ffn2_w_in=m_ffn2_w_in, ffn2_w_out=m_ffn2_w_out,
              a_w_qkv=m_a_w_qkv, a_w_o=m_a_w_o, kv_w=m_kv_w, b_w_q=m_b_w_q, b_w_o=m_b_w_o)
    vs = dict(ffn1_w_in=v_ffn1_w_in, ffn1_w_out=v_ffn1_w_out, ffn2_w_in=v_ffn2_w_in, ffn2_w_out=v_ffn2_w_out,
              a_w_qkv=v_a_w_qkv, a_w_o=v_a_w_o, kv_w=v_kv_w, b_w_q=v_b_w_q, b_w_o=v_b_w_o)
    _, _, c_idx, myq = _place()
    xs = x[0]
    target = loss_target[0]

    shards = {(n, l): (ws[n] if l is None else ws[n][l]).astype(BF16) for n, l in LAYER0_ITEMS + LAYER1_ITEMS}
    shards[FIRST_OWN] = shards[FIRST_OTHER] = shards[("ffn1_w_in", 0)]
    (w_own,), small = _all_gather((FIRST_OWN,), shards, _pack_small(ln_g, ln_b, b_sinks), from_chips=(0,))
    groups = dict(GATHER_GROUPS)
    states, token = _split_start_groups(
        "gather_rest_start",
        [(_gather_copies(items), PER_ITEM * len(items), [shards[item] for item in items],
          [jax.ShapeDtypeStruct(_full_shape(name), BF16) for name, _ in items]) for _, items in GATHER_GROUPS], small)
    states = dict(zip(groups, states))

    def get_weights(tag, after):
        if tag == "first":
            return {"ffn1_w_in_own": w_own, "block": myq % 2 + token[0, 0].astype(jnp.int32)}
        arrays = _split_wait("gather_wait_" + tag, _gather_copies(groups[tag]), states[tag], after)[1]
        return {n: (a.reshape(D_MODEL, a.shape[-1]) if a.ndim == 4 else a) for (n, _), a in zip(groups[tag], arrays)}

    n_ln = ln_g.size // 128
    lg = jnp.concatenate([small[q, :n_ln].reshape(DEPTH, 3, 1, -1) for q in range(N_CHIPS)], axis=-1)
    lb = jnp.concatenate([small[q, n_ln:2 * n_ln].reshape(DEPTH, 3, 1, -1) for q in range(N_CHIPS)], axis=-1)
    reducer = _GradReducer(c_idx, myq, {n: ws[n].shape for n in BIG})
    sq, grad_x, _, gg, gb, dsink_part = _local_step(xs, target, get_weights, lg, lb, b_sinks.reshape(N_HEADS), reducer.begin)

    loss_row = jnp.pad(jnp.sum(sq).reshape(1, 1), ((0, 0), (0, 127)))
    dsinks = jnp.pad(dsink_part[:, 0, :].reshape(N_SLABS, 2, HEAD_DIM)[:, :, 0].reshape(1, N_HEADS), ((0, 0), (0, 128 - N_HEADS)))
    gg_full = jnp.stack([jnp.stack([jnp.sum(gg[i][j], axis=0) for j in range(3)]) for i in range(DEPTH)])
    gb_full = jnp.stack([jnp.stack([jnp.sum(gb[i][j], axis=0) for j in range(3)]) for i in range(DEPTH)])
    small_in = jnp.concatenate([loss_row, dsinks, gg_full.reshape(-1, 128), gb_full.reshape(-1, 128)], axis=0)
    small_in = jnp.pad(small_in, ((0, (-small_in.shape[0]) % 8), (0, 0)))
    small_sum = _small_all_reduce(small_in)
    loss = small_sum[0, 0] * (0.5 / D_MODEL)
    grad_sinks = small_sum[1, :N_HEADS].reshape(b_sinks.shape)
    n_full = DEPTH * 3 * D_MODEL // 128
    cols = D_MODEL // N_CHIPS
    grad_ln_g = lax.dynamic_slice_in_dim(small_sum[2:2 + n_full].reshape(DEPTH, 3, D_MODEL), myq * cols, cols, axis=2)
    grad_ln_b = lax.dynamic_slice_in_dim(small_sum[2 + n_full:2 + 2 * n_full].reshape(DEPTH, 3, D_MODEL), myq * cols, cols, axis=2)
    return _update(reducer, grad_x, loss, grad_ln_g, grad_ln_b, grad_sinks, ws, ms, vs,
                   (ln_g, ln_b, b_sinks), (m_ln_g, m_ln_b, m_b_sinks), (v_ln_g, v_ln_b, v_b_sinks))


def _local_step(xs, target, get_weights, lg, lb, sinks, grads_ready=None):
    if grads_ready is None:
        grads_ready = lambda tag, grads, overlap: 0.0
    slopes = jnp.asarray(_alibi_slopes(N_HEADS))

    W = get_weights("first", xs)
    blk = W["block"]
    t1, h1 = _ffn_in(xs, W["ffn1_w_in_own"], "ffn_in_a1_own", block=blk)
    W.update(get_weights("other", t1))
    t1, h1 = _ffn_in(xs, W["ffn1_w_in_other"], "ffn_in_a1_other", block=1 - blk, into=(t1, h1))
    w_in_a1 = lax.dynamic_update_slice(jnp.zeros((D_MODEL, 2 * D_FF), BF16), W["ffn1_w_in_own"], (0, blk * D_FF))
    w_in_a1 = lax.dynamic_update_slice(w_in_a1, W["ffn1_w_in_other"], (0, (1 - blk) * D_FF))
    W.update(get_weights("out0", h1))
    y1, y1b, z1 = _mm_ln(h1, W["ffn1_w_out"], xs, lg[0, 0], lb[0, 0], 0.5, "ffn_out_ln_a1")
    s1 = dict(u=t1, h=h1, z=z1, xin=xs)
    in1, out1 = [w_in_a1], [W["ffn1_w_out"]]
    W.update(get_weights("mixer_a", y1b))
    qkv_a = _mm_nn(y1b, W["a_w_qkv"], F32, "qkv_a", split=True)
    mix_a, o_a, lse_a = _attn_fwd(qkv_a, slopes, None, PATTERNS_A, "attn_a_fwd")
    y2, y2b, z2 = _mm_ln(mix_a, W["a_w_o"], y1, lg[0, 1], lb[0, 1], 1.0, "attn_a_out_ln")
    W.update(get_weights("ffn2_0", y2b))
    in2, out2 = [W["ffn2_w_in"]], [W["ffn2_w_out"]]
    y3, y3b, s3 = _ffn_fwd(y2, in2[0], out2[0], lg[0, 2], lb[0, 2], "a2")
    kv_w_rep = jnp.broadcast_to(W["kv_w"].reshape(D_MODEL, 2, N_KV_B, 1, HEAD_DIM),
                                (D_MODEL, 2, N_KV_B, GROUP_B, HEAD_DIM)).reshape(D_MODEL, 2 * D_MODEL)
    kv_rep = _mm_nn(y3b, kv_w_rep, F32, "kv_proj", split=(1, 2))
    W = dict(W, **get_weights("layer1", kv_rep))
    in1, out1, in2, out2 = (in1 + [W["ffn1_w_in"]], out1 + [W["ffn1_w_out"]], in2 + [W["ffn2_w_in"]],
                            out2 + [W["ffn2_w_out"]])
    y4, y4b, s4 = _ffn_fwd(y3, in1[1], out1[1], lg[1, 0], lb[1, 0], "b1")
    qkv_b = _mm_nn(y4b, W["b_w_q"], F32, "q_b", split=(0, 1), into=kv_rep)
    mix_b, o_b, lse_b = _attn_fwd(qkv_b, slopes, sinks, PATTERNS_B, "attn_b_fwd")
    y5, y5b, z5 = _mm_ln(mix_b, W["b_w_o"], y4, lg[1, 1], lb[1, 1], 1.0, "attn_b_out_ln")
    y6, _, s6 = _ffn_fwd(y5, in2[1], out2[1], lg[1, 2], lb[1, 2], "b2")

    dy6, sq = _loss_grad(y6, target, "loss_grad")
    gr = {n: None for n in BIG}
    gg = [[None] * 3 for _ in range(DEPTH)]
    gb = [[None] * 3 for _ in range(DEPTH)]

    dy5, d_in2_b, d_out2_b, gg[1][2], gb[1][2] = _ffn_bwd(dy6, s6, in2[1], out2[1], lg[1, 2], y5b, "b2", BF16)
    dz5, dz5b, gg[1][1], gb[1][1] = _ln_bwd(z5, dy5, lg[1, 1], 1.0, "ln_bwd_attn_b")
    gr["b_w_o"] = _mm_tn(mix_b, dz5b, "d_b_w_o", out_dtype=BF16)
    dmix_b = _mm_nt(dz5b, W["b_w_o"], "d_mix_b")
    dqkv_b, dsink_part = _attn_bwd(qkv_b, dmix_b, o_b, lse_b, slopes, sinks, PATTERNS_B, "attn_b_bwd")
    dq_b = (dqkv_b, 0)
    gr["b_w_q"] = _mm_tn(y4b, dq_b, "d_b_w_q", out_dtype=BF16)
    dy4 = _mm_nt(dq_b, W["b_w_q"], "d_y4", add=dz5, add_scale=ALPHA)
    dy3, d_in1_b, d_out1_b, gg[1][0], gb[1][0] = _ffn_bwd(dy4, s4, in1[1], out1[1], lg[1, 0], y3b, "b1", BF16)
    d_kv_w_rep = _mm_tn(y3b, dqkv_b, "d_kv_w", split=(1, 2))
    gr["kv_w"] = d_kv_w_rep.reshape(D_MODEL, 2, N_KV_B, GROUP_B, HEAD_DIM).sum(axis=3).reshape(D_MODEL, -1).astype(BF16)
    dy3 = _mm_nt(dqkv_b, kv_w_rep, "d_y3_kv", add=dy3, add_scale=1.0, split=(1, 2))
    tok = grads_ready("l1", {("ffn2_w_in", 1): d_in2_b, ("ffn2_w_out", 1): d_out2_b, ("b_w_o", None): gr["b_w_o"],
                             ("b_w_q", None): gr["b_w_q"], ("ffn1_w_in", 1): d_in1_b, ("ffn1_w_out", 1): d_out1_b,
                             ("kv_w", None): gr["kv_w"]}, True)
    lg0 = lg[0] + tok

    dy2, d_in2_a, d_out2_a, gg[0][2], gb[0][2] = _ffn_bwd(dy3, s3, in2[0], out2[0], lg0[2], y2b, "a2", BF16)
    tok = grads_ready("a2", {("ffn2_w_in", 0): d_in2_a, ("ffn2_w_out", 0): d_out2_a}, True)
    lg0 = lg0 + tok
    dz2, dz2b, gg[0][1], gb[0][1] = _ln_bwd(z2, dy2, lg0[1], 1.0, "ln_bwd_attn_a")
    gr["a_w_o"] = _mm_tn(mix_a, dz2b, "d_a_w_o", out_dtype=BF16)
    dmix_a = _mm_nt(dz2b, W["a_w_o"], "d_mix_a")
    dqkv_a, _ = _attn_bwd(qkv_a, dmix_a, o_a, lse_a, slopes, None, PATTERNS_A, "attn_a_bwd")
    gr["a_w_qkv"] = _mm_tn(y1b, dqkv_a, "d_a_w_qkv", split=True, out_dtype=BF16)
    tok = grads_ready("mix", {("a_w_o", None): gr["a_w_o"], ("a_w_qkv", None): gr["a_w_qkv"]}, True)
    lg0 = lg0 + tok
    dy1 = _mm_nt(dqkv_a, W["a_w_qkv"], "d_y1", add=dz2, add_scale=ALPHA, split=True)
    grad_x, d_in1_a, d_out1_a, gg[0][0], gb[0][0] = _ffn_bwd(dy1, s1, in1[0], out1[0], lg0[0], xs, "a1")
    grads_ready("a1", {("ffn1_w_in", 0): d_in1_a, ("ffn1_w_out", 0): d_out1_a}, False)
    gr["ffn1_w_in"] = [d_in1_a, d_in1_b]
    gr["ffn1_w_out"] = [d_out1_a, d_out1_b]
    gr["ffn2_w_in"] = [d_in2_a, d_in2_b]
    gr["ffn2_w_out"] = [d_out2_a, d_out2_b]
    return sq, grad_x, gr, gg, gb, dsink_part


def _grad_item(name, layer, g):
    if name.endswith("w_in"):
        return (g, "col", HALF_FF, _slot, name, layer)
    if name.endswith("w_out"):
        return (g, "row", D_MODEL, None, name, layer)
    if name == "a_w_qkv":
        return (g, "col", QKV_SHARD, lambda q: q, name, None)
    return (g, "row", g.shape[1], None, name, None)


class _GradReducer:
    def __init__(self, c_idx, myq, shard_shapes):
        self.c_idx, self.myq, self.shard_shapes = c_idx, myq, shard_shapes
        self.groups = []

    def begin(self, tag, grads, overlap):
        items = [_grad_item(n, l, g) for (n, l), g in grads.items()]
        kinds, widths, colblocks = [it[1] for it in items], [it[2] for it in items], [it[3] for it in items]
        views = [_grad_view(k, it[0]) for k, it in zip(kinds, items)]
        if overlap:
            lands = [jax.ShapeDtypeStruct((N_DIRECT,) + _piece_shape(k, w, _half_shape(k, v.shape)), BF16)
                     for k, w, v in zip(kinds, widths, views)]
            state, token = _split_start("grad_direct_start_" + tag, _direct_copies(kinds, widths, colblocks), 10 * len(items),
                                        views, lands, views[-1])
            self.groups.append((tag, items, None, state))
            return token[0, 0]
        from_sibling = _pair_exchange(views, kinds, "grad_pair_exchange_" + tag)
        sums = [_pair_sum(k, v, r, self.c_idx, "pair_sum_%s_%d" % (tag, t))
                for t, (k, v, r) in enumerate(zip(kinds, views, from_sibling))]
        self.groups.append((tag, items, sums, None))
        return 0.0

    def _sum_group(self, tag, items, sums, received, direct):
        for t, (it, s, r) in enumerate(zip(items, sums, received)):
            _, k, _, cb, name, layer = it
            own = cb(self.myq) if k == "col" else self.myq
            self.half_done[name] = _chip_sum(k, s, r, own, self.c_idx, self.shard_shapes[name], layer,
                                             self.half_done.get(name), "chip_sum_%s_%d" % (tag, t), direct=direct)

    def finish_first(self, after):
        self.half_done, self.late = {}, []
        started = [after]
        for tag, items, sums, state in self.groups:
            if state is None:
                kinds, widths, colblocks = [it[1] for it in items], [it[2] for it in items], [it[3] for it in items]
                copies = _chip_copies(kinds, widths, colblocks)
                st, token = _split_start("grad_chip_start_" + tag, copies, 3 * len(items), sums,
                                         _chip_land_shapes(sums, kinds, widths), sums[-1])
                self.late.append((tag, items, copies, st))
                started.append(token)
        for tag, items, sums, state in self.groups:
            if state is not None:
                kinds, widths, colblocks = [it[1] for it in items], [it[2] for it in items], [it[3] for it in items]
                views, received = _split_wait("grad_direct_wait_" + tag, _direct_copies(kinds, widths, colblocks), state,
                                              started)
                self._sum_group(tag, items, views, received, True)
        late_names = {it[4] for _, items, _, _ in self.late for it in items}
        names = [n for n in BIG if n not in late_names]
        return dict(zip(names, _share_halves([self.half_done[n] for n in names], "grad_share_halves_first")))

    def finish_rest(self, after):
        names = []
        for tag, items, copies, st in self.late:
            sums, received = _split_wait("grad_chip_wait_" + tag, copies, st, after)
            self._sum_group(tag, items, sums, received, False)
            names += [it[4] for it in items if it[4] not in names]
        return dict(zip(names, _share_halves([self.half_done[n] for n in names], "grad_share_halves_rest")))


def _update(reducer, grad_x, loss, grad_ln_g, grad_ln_b, grad_sinks, ws, ms, vs, small_w, small_m, small_v):
    ln_g, ln_b, b_sinks = small_w
    m_ln_g, m_ln_b, m_b_sinks = small_m
    v_ln_g, v_ln_b, v_b_sinks = small_v

    deltas, new_m, new_v = {}, {}, {}

    def update(some):
        done = []
        for name in some:
            shp = ws[name].shape
            flat = lambda a: a.reshape(-1, shp[-1])
            d, nm, nv = _adamw(flat(ws[name]), flat(some[name]), flat(ms[name]), flat(vs[name]), "adamw_" + name)
            deltas[name], new_m[name], new_v[name] = d.reshape(shp), nm.reshape(shp), nv.reshape(shp)
            done.append(d)
        return done

    grads = reducer.finish_first(grad_x)
    rest = reducer.finish_rest(update(grads))
    update(rest)
    grads.update(rest)
    delta_s, nm_s, nv_s = _adamw(_pack_small(ln_g, ln_b, b_sinks), _pack_small(grad_ln_g, grad_ln_b, grad_sinks),
                                 _pack_small(m_ln_g, m_ln_b, m_b_sinks), _pack_small(v_ln_g, v_ln_b, v_b_sinks), "adamw_small")
    for d, blob in ((grads, None), (deltas, delta_s), (new_m, nm_s), (new_v, nv_s)):
        if blob is None:
            d["ln_g"], d["ln_b"], d["b_sinks"] = grad_ln_g, grad_ln_b, grad_sinks
        else:
            d["ln_g"], d["ln_b"], d["b_sinks"] = _unpack_small(blob, ln_g.shape, b_sinks.shape)

    order = ("ffn1_w_in", "ffn1_w_out", "ffn2_w_in", "ffn2_w_out", "ln_g", "ln_b", "a_w_qkv", "a_w_o", "kv_w", "b_w_q",
             "b_sinks", "b_w_o")
    outs = [loss, grad_x[None]]
    for d in (grads, deltas, new_m, new_v):
        outs += [d[n] for n in order]
    return tuple(outs)
```

```python
import numpy as np
import jax
import jax.numpy as jnp
from jax import lax
from jax.experimental import pallas as pl
from jax.experimental.pallas import tpu as pltpu

F32 = jnp.float32
BF16 = jnp.bfloat16

D_MODEL = 1024
D_FF = 2816
HALF_FF = D_FF // 2
HEAD_DIM = 64
N_HEADS = 16
N_KV_B = 4
GROUP_B = N_HEADS // N_KV_B
DEPTH = 2
ALPHA = (2.0 * DEPTH) ** 0.25
LN_EPS = 1e-5
BLOCK = 128
SLAB = 128
N_SLABS = D_MODEL // SLAB
PATTERNS_A = ((1, 128, 1.0), (4, 128, 4.0), (16, 128, 16.0))
PATTERNS_B = ((1, 127, 1.0),)
NEG = -1e30

ADAM_LR = 0.001
ADAM_B1 = 0.9
ADAM_B2 = 0.999
ADAM_EPS = 1e-08
ADAM_WD = 0.01
ADAM_STEP = 10

N_CHIPS = 4
VMEM_LIMIT = 56 * 1024 * 1024
MESH = pl.DeviceIdType.MESH


def _alibi_slopes(n):
    return np.array([2.0 ** (-8.0 * (h + 1) / n) for h in range(n)], dtype=np.float32)


def _cparams(sem=None, vmem=VMEM_LIMIT):
    return pltpu.CompilerParams(dimension_semantics=sem, vmem_limit_bytes=vmem)


_DIMS = {"nn": ((1,), (0,)), "nt": ((1,), (1,)), "tn": ((0,), (0,))}


def _unlead(x):
    if isinstance(x, tuple):
        return x[0], x[1], x[0].shape[1:]
    return x, None, x.shape


def _bspec(block, imap, lead=None):
    if lead is None:
        return pl.BlockSpec(block, imap)
    return pl.BlockSpec((None,) + tuple(block), lambda *g: (lead,) + tuple(imap(*g)))


def _matmul(a, b, mode, out_dtype, tm, tn, tk, name, add=None, add_scale=1.0, split=False, into=None):
    out_spec = pl.BlockSpec((tm, tn), lambda i, j, k: (i, j))
    base, count = (0, 3) if split is True else (split or (0, 0))
    if mode == "nn":
        a, al, (M, K) = _unlead(a)
        b, bl, (K2, N) = _unlead(b)
        a_spec = _bspec((tm, tk), lambda i, j, k: (i, k), al)
        b_spec = _bspec((tk, tn), lambda i, j, k: (k, j), bl)
        out_struct = jax.ShapeDtypeStruct((M, N), out_dtype)
        if split:
            assert tn == D_MODEL and N == count * tn
            out_spec = pl.BlockSpec((None, tm, tn), lambda i, j, k: (j + base, i, 0))
            out_struct = jax.ShapeDtypeStruct((3, M, tn), out_dtype)
    elif mode == "nt":
        b, bl, (N, K2) = _unlead(b)
        if split:
            assert tk == D_MODEL
            M, K = a.shape[1], count * a.shape[2]
            a_spec = pl.BlockSpec((None, tm, tk), lambda i, j, k: (k + base, i, 0))
        else:
            a, al, (M, K) = _unlead(a)
            a_spec = _bspec((tm, tk), lambda i, j, k: (i, k), al)
        b_spec = _bspec((tn, tk), lambda i, j, k: (j, k), bl)
        out_struct = jax.ShapeDtypeStruct((M, N), out_dtype)
    else:
        a, al, (K, M) = _unlead(a)
        if split:
            assert tn == D_MODEL
            K2, N = b.shape[1], count * b.shape[2]
            b_spec = pl.BlockSpec((None, tk, tn), lambda i, j, k: (j + base, k, 0))
        else:
            b, bl, (K2, N) = _unlead(b)
            b_spec = _bspec((tk, tn), lambda i, j, k: (k, j), bl)
        a_spec = _bspec((tk, tm), lambda i, j, k: (k, i), al)
        out_struct = jax.ShapeDtypeStruct((M, N), out_dtype)
    assert K == K2 and M % tm == 0 and N % tn == 0 and K % tk == 0, (a.shape, b.shape, mode, tm, tn, tk)
    nk = K // tk
    dims = (_DIMS[mode], ((), ()))
    has_add = add is not None

    narrow = out_dtype != F32
    assert not (narrow and has_add)

    def body(*refs):
        if into is not None:
            refs = refs[:2] + refs[3:]
        if has_add:
            a_ref, b_ref, add_ref, o_ref = refs
            acc_ref = o_ref
        elif narrow:
            a_ref, b_ref, o_ref, acc_ref = refs
        else:
            a_ref, b_ref, o_ref = refs
            acc_ref = o_ref
        k = pl.program_id(2)
        part = lax.dot_general(a_ref[...].astype(BF16), b_ref[...].astype(BF16), dims, preferred_element_type=F32)
        if has_add:
            @pl.when(k == 0)
            def _():
                acc_ref[...] = part + add_scale * add_ref[...]
        else:
            @pl.when(k == 0)
            def _():
                acc_ref[...] = part

        @pl.when(k > 0)
        def _():
            acc_ref[...] += part

        if narrow:
            @pl.when(k == nk - 1)
            def _():
                o_ref[...] = acc_ref[...].astype(out_dtype)

    in_specs = [a_spec, b_spec]
    args = [a, b]
    aliases = {}
    if into is not None:
        assert mode == "nn" and split and not has_add
        in_specs.append(pl.BlockSpec(memory_space=pl.ANY))
        args.append(into)
        aliases = {2: 0}
    if has_add:
        in_specs.append(pl.BlockSpec((tm, tn), lambda i, j, k: (i, j)))
        args.append(add)
    return pl.pallas_call(
        body, name=name, grid=(M // tm, N // tn, nk),
        in_specs=in_specs, out_specs=out_spec, out_shape=out_struct, input_output_aliases=aliases,
        scratch_shapes=[pltpu.VMEM((tm, tn), F32)] if narrow else [],
        compiler_params=_cparams(("parallel", "parallel", "arbitrary")),
    )(*args)


def _pick(n, cands):
    for c in cands:
        if n % c == 0:
            return c
    raise ValueError((n, cands))


def _mm_nn(a, b, out_dtype, name, split=False, into=None):
    M, K = _unlead(a)[2]
    N = _unlead(b)[2][1]
    return _matmul(a, b, "nn", out_dtype, _pick(M, (1024, 512, 256)), _pick(N, (1024, 512)), _pick(K, (1024, 512)), name,
                   split=split, into=into)


def _mm_nt(a, b, name, add=None, add_scale=1.0, split=False):
    M, K = (a.shape[1], D_MODEL) if split else _unlead(a)[2]
    N = _unlead(b)[2][0]
    return _matmul(a, b, "nt", F32, _pick(M, (1024, 512, 256)), _pick(N, (1024, 512)),
                   _pick(K, (2816, 1024, 512)), name, add=add, add_scale=add_scale, split=split)


def _mm_tn(a, b, name, split=False, out_dtype=F32):
    K, M = _unlead(a)[2]
    N = D_MODEL if split else _unlead(b)[2][1]
    return _matmul(a, b, "tn", out_dtype, _pick(M, (1024, 1408, 512)), _pick(N, (1408, 1024, 512)),
                   _pick(K, (2048, 1024, 512, 256)), name, split=split)


def _ffn_in(x, w, name):
    S = x.shape[0]
    tm = _pick(S, (512, 256))
    w, wl, _ = _unlead(w)

    def body(x_ref, w_ref, t_ref, h_ref):
        acc = jnp.dot(x_ref[...].astype(BF16), w_ref[...], preferred_element_type=F32)
        g = acc[:, :HALF_FF]
        up = acc[:, HALF_FF:]
        sg = jax.nn.sigmoid(g)
        silu = g * sg
        t_ref[:, :HALF_FF] = (up * (sg * (1.0 + g * (1.0 - sg)))).astype(BF16)
        t_ref[:, HALF_FF:] = silu.astype(BF16)
        h_ref[...] = (silu * up).astype(BF16)

    return pl.pallas_call(
        body, name=name, grid=(2, S // tm),
        in_specs=[pl.BlockSpec((tm, D_MODEL), lambda j, i: (i, 0)),
                  _bspec((D_MODEL, D_FF), lambda j, i: (0, j), wl)],
        out_specs=[pl.BlockSpec((tm, D_FF), lambda j, i: (i, j)),
                   pl.BlockSpec((tm, HALF_FF), lambda j, i: (i, j))],
        out_shape=[jax.ShapeDtypeStruct((S, 2 * D_FF), BF16), jax.ShapeDtypeStruct((S, D_FF), BF16)],
        compiler_params=_cparams(("parallel", "parallel")),
    )(x, w)


def _ffn_bwd_h(dzc, w_out, u, name):
    S = dzc.shape[0]
    tm = _pick(S, (512, 256))
    w_out, wl, _ = _unlead(w_out)

    def body(dz_ref, w_ref, t_ref, du_ref):
        dh = lax.dot_general(dz_ref[...], w_ref[...], (((1,), (1,)), ((), ())), preferred_element_type=F32)
        du_ref[:, :HALF_FF] = (dh * t_ref[:, :HALF_FF].astype(F32)).astype(BF16)
        du_ref[:, HALF_FF:] = (dh * t_ref[:, HALF_FF:].astype(F32)).astype(BF16)

    return pl.pallas_call(
        body, name=name, grid=(2, S // tm),
        in_specs=[pl.BlockSpec((tm, D_MODEL), lambda j, i: (i, 0)),
                  _bspec((HALF_FF, D_MODEL), lambda j, i: (j, 0), wl),
                  pl.BlockSpec((tm, D_FF), lambda j, i: (i, j))],
        out_specs=pl.BlockSpec((tm, D_FF), lambda j, i: (i, j)),
        out_shape=jax.ShapeDtypeStruct((S, 2 * D_FF), BF16),
        compiler_params=_cparams(("parallel", "parallel")),
    )(dzc, w_out, u)


def _mm_ln(a, w, resid, gain, bias, c, name):
    S, K = a.shape
    tm = _pick(S, (512, 256))
    w, wl, _ = _unlead(w)

    def body(a_ref, w_ref, r_ref, g_ref, b_ref, y_ref, yb_ref, z_ref):
        z = ALPHA * r_ref[...] + c * jnp.dot(a_ref[...], w_ref[...], preferred_element_type=F32)
        mu = jnp.mean(z, axis=-1, keepdims=True)
        zc = z - mu
        var = jnp.mean(zc * zc, axis=-1, keepdims=True)
        y = zc * lax.rsqrt(var + LN_EPS) * g_ref[...] + b_ref[...]
        z_ref[...] = z
        y_ref[...] = y
        yb_ref[...] = y.astype(BF16)

    row = pl.BlockSpec((tm, D_MODEL), lambda i: (i, 0))
    vec = pl.BlockSpec((1, D_MODEL), lambda i: (0, 0))
    return pl.pallas_call(
        body, name=name, grid=(S // tm,),
        in_specs=[pl.BlockSpec((tm, K), lambda i: (i, 0)), _bspec((K, D_MODEL), lambda i: (0, 0), wl), row, vec, vec],
        out_specs=[row, row, row],
        out_shape=[jax.ShapeDtypeStruct((S, D_MODEL), F32), jax.ShapeDtypeStruct((S, D_MODEL), BF16),
                   jax.ShapeDtypeStruct((S, D_MODEL), F32)],
        compiler_params=_cparams(("parallel",)),
    )(a, w, resid, gain, bias)


def _ln_bwd(z, dy, gain, c, name):
    S = z.shape[0]
    tm = _pick(S, (512, 256))

    def body(z_ref, dy_ref, g_ref, dz_ref, dzc_ref, gg_ref, gb_ref):
        i = pl.program_id(0)
        zv = z_ref[...]
        dyv = dy_ref[...]
        mu = jnp.mean(zv, axis=-1, keepdims=True)
        zc = zv - mu
        var = jnp.mean(zc * zc, axis=-1, keepdims=True)
        rstd = lax.rsqrt(var + LN_EPS)
        xhat = zc * rstd
        dyg = dyv * g_ref[...]
        m1 = jnp.mean(dyg, axis=-1, keepdims=True)
        m2 = jnp.mean(dyg * xhat, axis=-1, keepdims=True)
        dz = rstd * (dyg - m1 - xhat * m2)
        dz_ref[...] = dz
        dzc_ref[...] = (c * dz).astype(BF16)
        pg = jnp.sum((dyv * xhat).reshape(tm // 8, 8, D_MODEL), axis=0)
        pb = jnp.sum(dyv.reshape(tm // 8, 8, D_MODEL), axis=0)

        @pl.when(i == 0)
        def _():
            gg_ref[...] = pg
            gb_ref[...] = pb

        @pl.when(i > 0)
        def _():
            gg_ref[...] += pg
            gb_ref[...] += pb

    row = pl.BlockSpec((tm, D_MODEL), lambda i: (i, 0))
    part = pl.BlockSpec((8, D_MODEL), lambda i: (0, 0))
    return pl.pallas_call(
        body, name=name, grid=(S // tm,),
        in_specs=[row, row, pl.BlockSpec((1, D_MODEL), lambda i: (0, 0))],
        out_specs=[row, row, part, part],
        out_shape=[jax.ShapeDtypeStruct((S, D_MODEL), F32), jax.ShapeDtypeStruct((S, D_MODEL), BF16),
                   jax.ShapeDtypeStruct((8, D_MODEL), F32), jax.ShapeDtypeStruct((8, D_MODEL), F32)],
        compiler_params=_cparams(("arbitrary",)),
    )(z, dy, gain)


def _loss_grad(y, t, name):
    S = y.shape[0]
    tm = _pick(S, (512, 256))

    def body(y_ref, t_ref, dy_ref, sq_ref):
        i = pl.program_id(0)
        e = y_ref[...] - t_ref[...]
        dy_ref[...] = e * (1.0 / D_MODEL)
        ps = jnp.sum((e * e).reshape(tm // 8, 8, D_MODEL), axis=0)

        @pl.when(i == 0)
        def _():
            sq_ref[...] = ps

        @pl.when(i > 0)
        def _():
            sq_ref[...] += ps

    row = pl.BlockSpec((tm, D_MODEL), lambda i: (i, 0))
    return pl.pallas_call(
        body, name=name, grid=(S // tm,),
        in_specs=[row, row], out_specs=[row, pl.BlockSpec((8, D_MODEL), lambda i: (0, 0))],
        out_shape=[jax.ShapeDtypeStruct((S, D_MODEL), F32), jax.ShapeDtypeStruct((8, D_MODEL), F32)],
        compiler_params=_cparams(("arbitrary",)),
    )(y, t)


def _rows(start, d):
    if d == 1:
        return pl.ds(pl.multiple_of(start, BLOCK), BLOCK)
    return pl.ds(start, BLOCK, stride=d)


def _ld(ref, start, d):
    return ref[_rows(start, d), :]


def _ld3(ref, lead, start, d):
    return ref[lead, _rows(start, d), :]


def _st3(ref, lead, start, d, val):
    ref[lead, _rows(start, d), :] = val


def _acc3(ref, lead, start, d, val):
    ref[lead, _rows(start, d), :] = ref[lead, _rows(start, d), :] + val


def _band_consts(slope0, slope1, maxd, scale):
    row = lax.broadcasted_iota(jnp.int32, (2 * BLOCK, 2 * BLOCK), 0)
    kj = lax.broadcasted_iota(jnp.int32, (2 * BLOCK, 2 * BLOCK), 1)
    top = row < BLOCK
    dist = BLOCK + jnp.where(top, row, row - BLOCK) - kj
    slope = jnp.where(top, slope0, slope1)
    base = jnp.where((dist >= 0) & (dist <= maxd), -(slope * (dist.astype(F32) * scale)), NEG)
    return base, kj < BLOCK


def _stack_heads(x, lo):
    return jnp.concatenate([jnp.where(lo, x, 0.0), jnp.where(lo, 0.0, x)], axis=0)


def _unstack_heads(x2, lo):
    return jnp.where(lo, x2[:BLOCK], x2[BLOCK:])


def _scores(q2, k2, base, prev_keys, first):
    s = lax.dot_general(q2, k2, (((1,), (1,)), ((), ())), preferred_element_type=F32) * (HEAD_DIM ** -0.5) + base
    return jnp.where(jnp.logical_and(prev_keys, first), NEG, s)


def _softmax_weights(ls):
    mx = ls[0]
    for l in ls[1:]:
        mx = jnp.maximum(mx, l)
    es = [jnp.exp(l - mx) for l in ls]
    tot = es[0]
    for e in es[1:]:
        tot = tot + e
    inv = 1.0 / tot
    return [e * inv for e in es]


def _attn_fwd(qkv, slopes, sinks, patterns, name):
    S = qkv.shape[1]
    npat = len(patterns)
    has_sink = sinks is not None
    if not has_sink:
        sinks = jnp.zeros((N_HEADS,), F32)
    rows_c = 256

    def body(slopes_ref, sinks_ref, x_ref, mix_ref, o_ref, lse_ref, o_scr, lse_scr):
        p = pl.program_id(0)
        lo = lax.broadcasted_iota(jnp.int32, (BLOCK, SLAB), 1) < HEAD_DIM
        top1 = lax.broadcasted_iota(jnp.int32, (2 * BLOCK, 1), 0) < BLOCK
        sk2 = jnp.where(top1, sinks_ref[2 * p], sinks_ref[2 * p + 1])
        for pi, (d, maxd, scale) in enumerate(patterns):
            nb = S // d // BLOCK
            base, prev_keys = _band_consts(slopes_ref[2 * p], slopes_ref[2 * p + 1], maxd, scale)

            def blk(t, carry, pi=pi, d=d, nb=nb, base=base, prev_keys=prev_keys):
                r = t // nb
                n = t - r * nb
                start = r + (d * BLOCK) * n
                prev = jnp.where(n > 0, start - d * BLOCK, start)
                q2 = _stack_heads(_ld3(x_ref, 0, start, d), lo).astype(BF16)
                k2 = jnp.concatenate([_ld3(x_ref, 1, prev, d), _ld3(x_ref, 1, start, d)], axis=0).astype(BF16)
                v2 = jnp.concatenate([_ld3(x_ref, 2, prev, d), _ld3(x_ref, 2, start, d)], axis=0).astype(BF16)
                s = _scores(q2, k2, base, prev_keys, n == 0)
                m = jnp.max(s, axis=-1, keepdims=True)
                if has_sink:
                    m = jnp.maximum(m, sk2)
                e = jnp.exp(s - m)
                den = jnp.sum(e, axis=-1, keepdims=True)
                if has_sink:
                    den = den + jnp.exp(sk2 - m)
                o2 = jnp.dot((e / den).astype(BF16), v2, preferred_element_type=F32)
                _st3(o_scr, pi, start, d, _unstack_heads(o2, lo))
                _st3(lse_scr, pi, start, d, _unstack_heads(m + jnp.log(den), lo))
                return carry

            lax.fori_loop(0, d * nb, blk, 0, unroll=8)

        lane_c = lax.broadcasted_iota(jnp.int32, (rows_c, SLAB), 1)

        def comb(ci, carry):
            rows = pl.ds(pl.multiple_of(ci * rows_c, rows_c), rows_c)
            ls = [lse_scr[i, rows, :] for i in range(npat)]
            packed = jnp.zeros((rows_c, SLAB), F32)
            for i in range(npat):
                o_ref[i, rows, :] = o_scr[i, rows, :].astype(BF16)
                packed = jnp.where(lane_c % HEAD_DIM == i, ls[i], packed)
            lse_ref[rows, :] = packed
            if npat == 1:
                mix_ref[rows, :] = o_scr[0, rows, :].astype(BF16)
            else:
                ws = _softmax_weights(ls)
                acc = ws[0] * o_scr[0, rows, :]
                for i in range(1, npat):
                    acc = acc + ws[i] * o_scr[i, rows, :]
                mix_ref[rows, :] = acc.astype(BF16)
            return carry

        lax.fori_loop(0, S // rows_c, comb, 0, unroll=2)

    smem = pl.BlockSpec(memory_space=pltpu.SMEM)
    return pl.pallas_call(
        body, name=name, grid=(N_SLABS,),
        in_specs=[smem, smem, pl.BlockSpec((3, S, SLAB), lambda p: (0, 0, p))],
        out_specs=[pl.BlockSpec((S, SLAB), lambda p: (0, p)), pl.BlockSpec((npat, S, SLAB), lambda p: (0, 0, p)),
                   pl.BlockSpec((None, S, SLAB), lambda p: (p, 0, 0))],
        out_shape=[jax.ShapeDtypeStruct((S, D_MODEL), BF16), jax.ShapeDtypeStruct((npat, S, D_MODEL), BF16),
                   jax.ShapeDtypeStruct((N_SLABS, S, SLAB), F32)],
        scratch_shapes=[pltpu.VMEM((npat, S, SLAB), F32), pltpu.VMEM((npat, S, SLAB), F32)],
        compiler_params=_cparams(("arbitrary",)),
    )(slopes, sinks, qkv)


def _attn_bwd(qkv, dout, o, lse, slopes, sinks, patterns, name):
    S = qkv.shape[1]
    npat = len(patterns)
    has_sink = sinks is not None
    if not has_sink:
        sinks = jnp.zeros((N_HEADS,), F32)
    rows_c = 256

    def headsum(x, lo):
        same = (lax.broadcasted_iota(jnp.int32, (SLAB, SLAB), 0) < HEAD_DIM) == (lax.broadcasted_iota(jnp.int32, (SLAB, SLAB), 1) < HEAD_DIM)
        return jnp.dot(x, same.astype(F32), precision=lax.Precision.HIGH, preferred_element_type=F32)

    def body(slopes_ref, sinks_ref, x_ref, do_ref, o_ref, lsep_ref, dxo_ref, dsink_ref, dbar_ref, sacc_ref, lse_ref, dx_ref):
        p = pl.program_id(0)
        lo = lax.broadcasted_iota(jnp.int32, (BLOCK, SLAB), 1) < HEAD_DIM
        lo_c = lax.broadcasted_iota(jnp.int32, (rows_c, SLAB), 1) < HEAD_DIM
        top1 = lax.broadcasted_iota(jnp.int32, (2 * BLOCK, 1), 0) < BLOCK
        sk2 = jnp.where(top1, sinks_ref[2 * p], sinks_ref[2 * p + 1])

        def prep(ci, carry):
            rows = pl.ds(pl.multiple_of(ci * rows_c, rows_c), rows_c)
            dov = do_ref[rows, :]
            dx_ref[:, rows, :] = jnp.zeros((3, rows_c, SLAB), F32)
            packed = lsep_ref[rows, :]
            ls = [jnp.where(lo_c, packed[:, i:i + 1], packed[:, HEAD_DIM + i:HEAD_DIM + i + 1]) for i in range(npat)]
            for i in range(npat):
                lse_ref[i, rows, :] = ls[i]
            if npat == 1:
                dbar_ref[rows, :] = headsum(dov * o_ref[0, rows, :].astype(F32), lo_c)
            else:
                ws = _softmax_weights(ls)
                acc = ws[0] * headsum(dov * o_ref[0, rows, :].astype(F32), lo_c)
                for i in range(1, npat):
                    acc = acc + ws[i] * headsum(dov * o_ref[i, rows, :].astype(F32), lo_c)
                dbar_ref[rows, :] = acc
            return carry

        lax.fori_loop(0, S // rows_c, prep, 0, unroll=2)
        sacc_ref[...] = jnp.zeros((BLOCK, SLAB), F32)

        for pi, (d, maxd, scale) in enumerate(patterns):
            nb = S // d // BLOCK
            base, prev_keys = _band_consts(slopes_ref[2 * p], slopes_ref[2 * p + 1], maxd, scale)

            def blk(t, carry, pi=pi, d=d, nb=nb, base=base, prev_keys=prev_keys):
                r = t // nb
                n = t - r * nb
                start = r + (d * BLOCK) * n
                prev = jnp.where(n > 0, start - d * BLOCK, start)
                q2 = _stack_heads(_ld3(x_ref, 0, start, d), lo).astype(BF16)
                k2 = jnp.concatenate([_ld3(x_ref, 1, prev, d), _ld3(x_ref, 1, start, d)], axis=0).astype(BF16)
                v2 = jnp.concatenate([_ld3(x_ref, 2, prev, d), _ld3(x_ref, 2, start, d)], axis=0).astype(BF16)
                ls = [_ld3(lse_ref, i, start, d) for i in range(npat)]
                w = _softmax_weights(ls)[pi] if npat > 1 else 1.0
                do2 = _stack_heads(w * _ld(do_ref, start, d), lo).astype(BF16)
                dl = w * _ld(dbar_ref, start, d)
                lse2 = jnp.concatenate([ls[pi][:, :1], ls[pi][:, HEAD_DIM:HEAD_DIM + 1]], axis=0)
                dl2 = jnp.concatenate([dl[:, :1], dl[:, HEAD_DIM:HEAD_DIM + 1]], axis=0)
                s = _scores(q2, k2, base, prev_keys, n == 0)
                pr = jnp.exp(s - lse2)
                dp = lax.dot_general(do2, v2, (((1,), (1,)), ((), ())), preferred_element_type=F32)
                ds = (pr * (dp - dl2) * (HEAD_DIM ** -0.5)).astype(BF16)
                dq2 = jnp.dot(ds, k2, preferred_element_type=F32)
                dk2 = lax.dot_general(ds, q2, (((0,), (0,)), ((), ())), preferred_element_type=F32)
                dv2 = lax.dot_general(pr.astype(BF16), do2, (((0,), (0,)), ((), ())), preferred_element_type=F32)
                _acc3(dx_ref, 0, start, d, _unstack_heads(dq2, lo))
                _acc3(dx_ref, 1, prev, d, dk2[:BLOCK])
                _acc3(dx_ref, 1, start, d, dk2[BLOCK:])
                _acc3(dx_ref, 2, prev, d, dv2[:BLOCK])
                _acc3(dx_ref, 2, start, d, dv2[BLOCK:])
                if has_sink:
                    sacc_ref[...] += _unstack_heads(-jnp.exp(sk2 - lse2) * dl2, lo)
                return carry

            lax.fori_loop(0, d * nb, blk, 0, unroll=4)

        dsink_ref[...] = jnp.broadcast_to(jnp.sum(sacc_ref[...], axis=0, keepdims=True), (8, SLAB))

        def emit(ci, carry):
            rows = pl.ds(pl.multiple_of(ci * rows_c, rows_c), rows_c)
            dxo_ref[:, rows, :] = dx_ref[:, rows, :].astype(BF16)
            return carry

        lax.fori_loop(0, S // rows_c, emit, 0, unroll=2)

    smem = pl.BlockSpec(memory_space=pltpu.SMEM)
    return pl.pallas_call(
        body, name=name, grid=(N_SLABS,),
        in_specs=[smem, smem, pl.BlockSpec((3, S, SLAB), lambda p: (0, 0, p)), pl.BlockSpec((S, SLAB), lambda p: (0, p)),
                  pl.BlockSpec((npat, S, SLAB), lambda p: (0, 0, p)), pl.BlockSpec((None, S, SLAB), lambda p: (p, 0, 0))],
        out_specs=[pl.BlockSpec((3, S, SLAB), lambda p: (0, 0, p)), pl.BlockSpec((None, 8, SLAB), lambda p: (p, 0, 0))],
        out_shape=[jax.ShapeDtypeStruct((3, S, D_MODEL), BF16), jax.ShapeDtypeStruct((N_SLABS, 8, SLAB), F32)],
        scratch_shapes=[pltpu.VMEM((S, SLAB), F32), pltpu.VMEM((BLOCK, SLAB), F32), pltpu.VMEM((npat, S, SLAB), F32),
                        pltpu.VMEM((3, S, SLAB), F32)],
        compiler_params=_cparams(("arbitrary",)),
    )(slopes, sinks, qkv, dout, o, lse)


def _place():
    x, y, c = lax.axis_index("x"), lax.axis_index("y"), lax.axis_index("c")
    return x, y, c, 2 * x + y


def _other_chips(x, y):
    return [(1 - x, y), (x, 1 - y), (1 - x, 1 - y)]


HBM_SPEC = pl.BlockSpec(memory_space=pl.ANY)


def _slot(q):
    return 2 * (q % 2) + q // 2


BIG = ("ffn1_w_in", "ffn1_w_out", "ffn2_w_in", "ffn2_w_out", "a_w_qkv", "a_w_o", "kv_w", "b_w_q", "b_w_o")
QKV_SHARD = 3 * D_MODEL // N_CHIPS
ROW_SHARD = D_MODEL // N_CHIPS


LAYER0_ITEMS = (("ffn1_w_in", 0), ("ffn1_w_out", 0), ("a_w_qkv", None), ("a_w_o", None), ("ffn2_w_in", 0),
                ("ffn2_w_out", 0), ("kv_w", None))
LAYER1_ITEMS = (("ffn1_w_in", 1), ("ffn1_w_out", 1), ("b_w_q", None), ("b_w_o", None), ("ffn2_w_in", 1),
                ("ffn2_w_out", 1))
OUT_SHARD = D_FF // N_CHIPS


def _full_shape(name):
    if name.endswith("w_in"):
        return (D_MODEL, 2 * D_FF)
    if name.endswith("w_out"):
        return (D_FF, D_MODEL)
    if name == "a_w_qkv":
        return (D_MODEL, 3 * D_MODEL)
    if name == "kv_w":
        return (N_CHIPS, 2, ROW_SHARD // 2, 2 * N_KV_B * HEAD_DIM)
    return (N_CHIPS, 2, ROW_SHARD // 2, D_MODEL)


def _gather_src(item, ref, c):
    name, _ = item
    if name.endswith("w_in"):
        return ref.at[pl.ds(c * (D_MODEL // 2), D_MODEL // 2)]
    if name.endswith("w_out"):
        return ref.at[pl.ds(c * (OUT_SHARD // 2), OUT_SHARD // 2)]
    if name == "a_w_qkv":
        return ref.at[0, pl.ds(c * (D_MODEL // 2), D_MODEL // 2)]
    if name == "kv_w":
        return ref.at[pl.ds(c * (ROW_SHARD // 2), ROW_SHARD // 2)]
    return ref.at[0, pl.ds(c * (ROW_SHARD // 2), ROW_SHARD // 2)]


def _gather_dst(item, ref, q, c):
    name, _ = item
    if name.endswith("w_in"):
        return ref.at[pl.ds(c * (D_MODEL // 2), D_MODEL // 2), pl.ds(_slot(q) * HALF_FF, HALF_FF)]
    if name.endswith("w_out"):
        return ref.at[pl.ds(q * OUT_SHARD + c * (OUT_SHARD // 2), OUT_SHARD // 2)]
    if name == "a_w_qkv":
        return ref.at[pl.ds(c * (D_MODEL // 2), D_MODEL // 2), pl.ds(q * QKV_SHARD, QKV_SHARD)]
    return ref.at[q, c]


def _all_gather(items, shards, small):
    n = len(items)
    r = small.shape[0]
    per = 8

    def body(*refs):
        srcs, small_ref = refs[:n], refs[n]
        dsts, s_ref = refs[n + 1:2 * n + 1], refs[2 * n + 1]
        send_sems, recv_sems = refs[2 * n + 2:]
        x, y, c, myq = _place()
        sibling = (x, y, 1 - c)
        chips = _other_chips(x, y)

        def big(t, k, src, q, h, to):
            return pltpu.make_async_remote_copy(src_ref=src, dst_ref=_gather_dst(items[t], dsts[t], q, h),
                                                send_sem=send_sems.at[per * t + k], recv_sem=recv_sems.at[per * t + k],
                                                device_id=to, device_id_type=MESH)

        def tiny(k, q, to):
            return pltpu.make_async_remote_copy(src_ref=small_ref, dst_ref=s_ref.at[q], send_sem=send_sems.at[per * n + k],
                                                recv_sem=recv_sems.at[per * n + k], device_id=to, device_id_type=MESH)

        first = []
        for j, chip in enumerate(chips):
            first += [big(t, j, _gather_src(items[t], srcs[t], c), myq, c, (*chip, c)) for t in range(n)]
            first.append(tiny(j, myq, (*chip, c)))
        own = [big(t, 6 + h, _gather_src(items[t], srcs[t], h), myq, h, sibling) for t in range(n) for h in (0, 1)]
        own.append(tiny(3, myq, sibling))
        for cp in first + own:
            cp.start()
        passed = []
        for j, (cx, cy) in enumerate(chips):
            q = 2 * cx + cy
            for t in range(n):
                src = _gather_src(items[t], srcs[t], c)
                big(t, j, src, q, c, sibling).wait_recv()
                fwd = big(t, 3 + j, _gather_dst(items[t], dsts[t], q, c), q, c, sibling)
                fwd.start()
                passed.append(fwd)
        for j, (cx, cy) in enumerate(chips):
            q = 2 * cx + cy
            for t in range(n):
                big(t, 3 + j, _gather_src(items[t], srcs[t], c), q, 1 - c, sibling).wait_recv()
            tiny(j, q, sibling).wait_recv()
        for cp in own:
            cp.wait_recv()
        for cp in first + passed + own:
            cp.wait_send()

    outs = pl.pallas_call(
        body, name="all_gather_layer0",
        in_specs=[HBM_SPEC] * (n + 1), out_specs=[HBM_SPEC] * (n + 1),
        out_shape=[jax.ShapeDtypeStruct(_full_shape(name), BF16) for name, _ in items]
        + [jax.ShapeDtypeStruct((N_CHIPS, r, 128), F32)],
        scratch_shapes=[pltpu.SemaphoreType.DMA((per * n + 4,)), pltpu.SemaphoreType.DMA((per * n + 4,))],
    )(*[shards[item] for item in items], small)
    return list(outs[:n]), outs[n]


SEM_SPEC = pl.BlockSpec(memory_space=pltpu.SEMAPHORE)
DATAFLOW = pltpu.SideEffectType.DATAFLOW_SIDE_EFFECTING
PER_ITEM = 8


def _split_start(name, copies, n_sems, sources, land_shapes, after):
    n, m = len(sources), len(land_shapes)

    def body(*refs):
        srcs, lands = refs[:n], refs[n:n + m]
        send_sems, recv_sems = refs[n + m + 1], refs[n + m + 2]
        token = refs[-1]
        for src, dst_there, _, s, peer in copies(srcs, lands):
            pltpu.make_async_remote_copy(src_ref=src, dst_ref=dst_there, send_sem=send_sems.at[s], recv_sem=recv_sems.at[s],
                                         device_id=peer, device_id_type=MESH).start()
        token[...] = jnp.zeros_like(token)

    src_arrays = [pltpu.with_memory_space_constraint(a, pltpu.HBM) for a in sources]
    land_arrays = [pltpu.with_memory_space_constraint(lax.empty(s.shape, s.dtype), pltpu.HBM) for s in land_shapes]
    hbm = pl.BlockSpec(memory_space=pltpu.HBM)
    outs = pl.pallas_call(
        body, name=name,
        in_specs=[hbm] * (n + m) + [HBM_SPEC],
        out_specs=[SEM_SPEC, SEM_SPEC] + [hbm] * (n + m) + [pl.BlockSpec(memory_space=pltpu.VMEM)],
        out_shape=[pltpu.SemaphoreType.DMA((n_sems,)), pltpu.SemaphoreType.DMA((n_sems,))]
        + [pltpu.HBM(a.shape, a.dtype) for a in src_arrays + land_arrays] + [jax.ShapeDtypeStruct((8, 128), F32)],
        input_output_aliases={i: 2 + i for i in range(n + m)},
        compiler_params=pltpu.CompilerParams(has_side_effects=DATAFLOW),
    )(*src_arrays, *land_arrays, after)
    return (outs[0], outs[1], list(outs[2:2 + n]), list(outs[2 + n:2 + n + m])), outs[-1]


def _split_wait(name, copies, state, after):
    send_sems, recv_sems, srcs_thru, lands_thru = state
    n, m = len(srcs_thru), len(lands_thru)
    after = list(after) if isinstance(after, (list, tuple)) else [after]

    def body(*refs):
        srcs, lands = refs[:n], refs[n:n + m]
        send_sems, recv_sems = refs[n + m], refs[n + m + 1]
        for src, _, dst_here, s, peer in copies(srcs, lands):
            cp = pltpu.make_async_remote_copy(src_ref=src, dst_ref=dst_here, send_sem=send_sems.at[s], recv_sem=recv_sems.at[s],
                                              device_id=peer, device_id_type=MESH)
            cp.wait_send()
            cp.wait_recv()

    hbm = pl.BlockSpec(memory_space=pltpu.HBM)
    outs = pl.pallas_call(
        body, name=name,
        in_specs=[hbm] * (n + m) + [SEM_SPEC, SEM_SPEC] + [HBM_SPEC] * len(after),
        out_specs=[hbm] * (n + m),
        out_shape=[pltpu.HBM(a.shape, a.dtype) for a in srcs_thru + lands_thru],
        input_output_aliases={i: i for i in range(n + m)},
        compiler_params=pltpu.CompilerParams(has_side_effects=DATAFLOW),
    )(*srcs_thru, *lands_thru, send_sems, recv_sems, *after)
    return list(outs[:n]), list(outs[n:])


def _gather_copies(items):
    def copies(srcs, lands):
        x, y, c, myq = _place()
        out = []
        for t, item in enumerate(items):
            for h in (0, 1):
                src = _gather_src(item, srcs[t], h)
                for j, (cx, cy) in enumerate(_other_chips(x, y)):
                    out.append((src, _gather_dst(item, lands[t], myq, h), _gather_dst(item, lands[t], 2 * cx + cy, h),
                                PER_ITEM * t + 2 * j + h, (cx, cy, c)))
                out.append((src, _gather_dst(item, lands[t], myq, h), _gather_dst(item, lands[t], myq, h),
                            PER_ITEM * t + 6 + h, (x, y, 1 - c)))
        return out
    return copies


def _gather_start(items, shards, after):
    lands = [jax.ShapeDtypeStruct(_full_shape(name), BF16) for name, _ in items]
    return _split_start("gather_layer1_start", _gather_copies(items), PER_ITEM * len(items),
                        [shards[item] for item in items], lands, after)


def _gather_wait(items, state, after):
    return _split_wait("gather_layer1_wait", _gather_copies(items), state, after)[1]


def _small_all_reduce(v):
    r = v.shape[0]

    def body(v_ref, o_ref, buf_ref, send_sems, recv_sems):
        x, y, c, _ = _place()
        me = 4 * x + 2 * y + c
        buf_ref[me] = v_ref[...]
        copies = []
        for k in range(1, 8):
            fx, fy, fc = (k >> 2) & 1, (k >> 1) & 1, k & 1
            to = (x ^ fx, y ^ fy, c ^ fc)
            cp = pltpu.make_async_remote_copy(src_ref=v_ref, dst_ref=buf_ref.at[me], send_sem=send_sems.at[k - 1],
                                              recv_sem=recv_sems.at[k - 1], device_id=to, device_id_type=MESH)
            cp.start()
            copies.append(cp)
        for k in range(1, 8):
            fx, fy, fc = (k >> 2) & 1, (k >> 1) & 1, k & 1
            src_dev = 4 * (x ^ fx) + 2 * (y ^ fy) + (c ^ fc)
            pltpu.make_async_remote_copy(src_ref=v_ref, dst_ref=buf_ref.at[src_dev], send_sem=send_sems.at[k - 1],
                                         recv_sem=recv_sems.at[k - 1], device_id=(x, y, c), device_id_type=MESH).wait_recv()
        for cp in copies:
            cp.wait_send()
        tot = buf_ref[0]
        for i in range(1, 8):
            tot = tot + buf_ref[i]
        o_ref[...] = tot

    vm = pl.BlockSpec(memory_space=pltpu.VMEM)
    return pl.pallas_call(
        body, name="small_all_reduce", in_specs=[vm], out_specs=vm,
        out_shape=jax.ShapeDtypeStruct((r, 128), F32),
        scratch_shapes=[pltpu.VMEM((8, r, 128), F32), pltpu.SemaphoreType.DMA((7,)), pltpu.SemaphoreType.DMA((7,))],
    )(v)


def _grad_view(kind, g):
    if kind == "col":
        return g.reshape(2, g.shape[0] // 2, g.shape[1])
    return g.reshape(N_CHIPS, 2, g.shape[0] // (2 * N_CHIPS), g.shape[1])


def _half_of(kind, ref, h):
    return ref.at[h] if kind == "col" else ref.at[:, h]


def _half_shape(kind, view_shape):
    return view_shape[1:] if kind == "col" else (view_shape[0],) + view_shape[2:]


def _piece_of(kind, width, colblock, ref, q):
    if kind == "col":
        return ref.at[:, pl.ds(colblock(q) * width, width)]
    return ref.at[q]


def _piece_shape(kind, width, half_shape):
    return (half_shape[0], width) if kind == "col" else half_shape[1:]


def _pair_exchange(views, kinds, name):
    n = len(views)

    def body(*refs):
        ins, outs = refs[:n], refs[n:2 * n]
        send_sems, recv_sems = refs[2 * n:]
        x, y, c, _ = _place()
        cps = []
        for t in range(n):
            cp = pltpu.make_async_remote_copy(src_ref=_half_of(kinds[t], ins[t], 1 - c), dst_ref=outs[t],
                                              send_sem=send_sems.at[t], recv_sem=recv_sems.at[t],
                                              device_id=(x, y, 1 - c), device_id_type=MESH)
            cp.start()
            cps.append(cp)
        for cp in cps:
            cp.wait()

    return pl.pallas_call(
        body, name=name, in_specs=[HBM_SPEC] * n, out_specs=[HBM_SPEC] * n,
        out_shape=[jax.ShapeDtypeStruct(_half_shape(k, v.shape), v.dtype) for k, v in zip(kinds, views)],
        scratch_shapes=[pltpu.SemaphoreType.DMA((n,)), pltpu.SemaphoreType.DMA((n,))],
    )(*views)


def _pair_sum(kind, view, recv, c, name):
    hs = recv.shape
    N = hs[-1]
    rows = hs[-2]
    tr = _pick(rows, (512, 352, 128))
    tn = _pick(N, (1408, 1024, 512))

    def body(c_ref, p_ref, r_ref, s_ref):
        s_ref[...] = (p_ref[...] + r_ref[...]).astype(BF16)

    if kind == "col":
        grid = (rows // tr, N // tn)
        mine = pl.BlockSpec((None, tr, tn), lambda i, j, c_ref: (c_ref[0], i, j))
        blk = pl.BlockSpec((tr, tn), lambda i, j, c_ref: (i, j))
        sem = ("parallel", "parallel")
    else:
        grid = (N_CHIPS, rows // tr, N // tn)
        mine = pl.BlockSpec((None, None, tr, tn), lambda q, i, j, c_ref: (q, c_ref[0], i, j))
        blk = pl.BlockSpec((None, tr, tn), lambda q, i, j, c_ref: (q, i, j))
        sem = ("parallel", "parallel", "parallel")
    return pl.pallas_call(
        body, name=name,
        grid_spec=pltpu.PrefetchScalarGridSpec(num_scalar_prefetch=1, grid=grid, in_specs=[mine, blk], out_specs=blk),
        out_shape=jax.ShapeDtypeStruct(hs, BF16),
        compiler_params=_cparams(sem),
    )(c.reshape(1).astype(jnp.int32), view, recv)


def _chip_copies(kinds, widths, colblocks):
    def copies(srcs, lands):
        x, y, c, _ = _place()
        out = []
        for j, (cx, cy) in enumerate(_other_chips(x, y)):
            for t in range(len(kinds)):
                out.append((_piece_of(kinds[t], widths[t], colblocks[t], srcs[t], 2 * cx + cy), lands[t].at[j],
                            lands[t].at[j], 3 * t + j, (cx, cy, c)))
        return out
    return copies


def _chip_land_shapes(sums, kinds, widths):
    return [jax.ShapeDtypeStruct((3,) + _piece_shape(k, w, s.shape), BF16) for k, w, s in zip(kinds, widths, sums)]


def _chip_exchange(sums, kinds, widths, colblocks, name):
    n = len(sums)
    copies = _chip_copies(kinds, widths, colblocks)

    def body(*refs):
        send_sems, recv_sems = refs[2 * n:]
        cps = [pltpu.make_async_remote_copy(src_ref=src, dst_ref=dst, send_sem=send_sems.at[s], recv_sem=recv_sems.at[s],
                                            device_id=peer, device_id_type=MESH)
               for src, dst, _, s, peer in copies(refs[:n], refs[n:2 * n])]
        for cp in cps:
            cp.start()
        for cp in cps:
            cp.wait()

    return pl.pallas_call(
        body, name=name, in_specs=[HBM_SPEC] * n, out_specs=[HBM_SPEC] * n,
        out_shape=_chip_land_shapes(sums, kinds, widths),
        scratch_shapes=[pltpu.SemaphoreType.DMA((3 * n,)), pltpu.SemaphoreType.DMA((3 * n,))],
    )(*sums)


N_DIRECT = 7


def _direct_piece(kind, width, colblock, view_ref, q, h):
    if kind == "col":
        return view_ref.at[h, :, pl.ds(colblock(q) * width, width)]
    return view_ref.at[q, h]


def _direct_copies(kinds, widths, colblocks):
    def copies(srcs, lands):
        x, y, c, myq = _place()
        out = []
        for t in range(len(kinds)):
            def piece(q, h, t=t):
                return _direct_piece(kinds[t], widths[t], colblocks[t], srcs[t], q, h)
            for j, (cx, cy) in enumerate(_other_chips(x, y)):
                for h in (0, 1):
                    out.append((piece(2 * cx + cy, h), lands[t].at[2 * j + c], lands[t].at[2 * j + h],
                                10 * t + 3 * j + c + h, (cx, cy, h)))
            out.append((piece(myq, 1 - c), lands[t].at[6], lands[t].at[6], 10 * t + 9, (x, y, 1 - c)))
        return out
    return copies


def _chip_sum(kind, own_src, recv, block_idx, c, shard_shape, layer, into, name, direct=False):
    n_recv, rows, N = recv.shape
    tr = _pick(rows, (512, 352, 128))
    tn = _pick(N, (1408, 1024, 768, 512))
    ni, nj = rows // tr, N // tn

    def body(q_ref, s_ref, r_ref, *rest):
        o_ref = rest[-1]
        tot = s_ref[...].astype(F32)
        for k in range(n_recv):
            tot = tot + r_ref[k].astype(F32)
        o_ref[...] = tot

    if direct and kind == "col":
        own = pl.BlockSpec((None, tr, tn), lambda i, j, q_ref: (q_ref[1], i, q_ref[0] * nj + j))
    elif direct:
        own = pl.BlockSpec((None, None, tr, tn), lambda i, j, q_ref: (q_ref[0], q_ref[1], i, j))
    elif kind == "col":
        own = pl.BlockSpec((tr, tn), lambda i, j, q_ref: (i, q_ref[0] * nj + j))
    else:
        own = pl.BlockSpec((None, tr, tn), lambda i, j, q_ref: (q_ref[0], i, j))
    if len(shard_shape) == 3:
        lead = 0 if layer is None else layer
        out_spec = pl.BlockSpec((None, tr, tn), lambda i, j, q_ref: (lead, q_ref[1] * ni + i, j))
    else:
        out_spec = pl.BlockSpec((tr, tn), lambda i, j, q_ref: (q_ref[1] * ni + i, j))
    in_specs = [own, pl.BlockSpec((n_recv, tr, tn), lambda i, j, q_ref: (0, i, j))]
    s = own_src
    args = [jnp.stack([block_idx, c]).astype(jnp.int32), s, recv]
    aliases = {}
    if into is not None:
        in_specs.append(HBM_SPEC)
        args.append(into)
        aliases = {3: 0}
    return pl.pallas_call(
        body, name=name,
        grid_spec=pltpu.PrefetchScalarGridSpec(num_scalar_prefetch=1, grid=(ni, nj), in_specs=in_specs, out_specs=out_spec),
        out_shape=jax.ShapeDtypeStruct(shard_shape, F32), input_output_aliases=aliases,
        compiler_params=_cparams(("parallel", "parallel")),
    )(*args)


def _half_window(ref, h):
    rows = ref.shape[-2] // 2
    if ref.ndim == 3:
        return ref.at[:, pl.ds(h * rows, rows)]
    return ref.at[pl.ds(h * rows, rows)]


def _share_halves(grads, name):
    n = len(grads)

    def body(*refs):
        outs = refs[n:2 * n]
        send_sems, recv_sems = refs[2 * n:]
        x, y, c, _ = _place()
        cps = []
        for t in range(n):
            cp = pltpu.make_async_remote_copy(src_ref=_half_window(outs[t], c), dst_ref=_half_window(outs[t], c),
                                              send_sem=send_sems.at[t], recv_sem=recv_sems.at[t],
                                              device_id=(x, y, 1 - c), device_id_type=MESH)
            cp.start()
            cps.append(cp)
        for t in range(n):
            cps[t].wait_send()
            pltpu.make_async_remote_copy(src_ref=_half_window(outs[t], c), dst_ref=_half_window(outs[t], 1 - c),
                                         send_sem=send_sems.at[t], recv_sem=recv_sems.at[t],
                                         device_id=(x, y, 1 - c), device_id_type=MESH).wait_recv()

    return pl.pallas_call(
        body, name=name, in_specs=[HBM_SPEC] * n, out_specs=[HBM_SPEC] * n,
        out_shape=[jax.ShapeDtypeStruct(g.shape, F32) for g in grads],
        input_output_aliases={t: t for t in range(n)},
        scratch_shapes=[pltpu.SemaphoreType.DMA((n,)), pltpu.SemaphoreType.DMA((n,))],
    )(*grads)


def _adamw(w, g, m, v, name):
    R, W = w.shape
    tr = _pick(R, (512, 352, 256, 32))

    def body(w_ref, g_ref, m_ref, v_ref, d_ref, nm_ref, nv_ref):
        gv = g_ref[...]
        nm = ADAM_B1 * m_ref[...] + (1.0 - ADAM_B1) * gv
        nv = ADAM_B2 * v_ref[...] + (1.0 - ADAM_B2) * (gv * gv)
        m_hat = nm / (1.0 - ADAM_B1 ** ADAM_STEP)
        v_hat = nv / (1.0 - ADAM_B2 ** ADAM_STEP)
        d_ref[...] = -ADAM_LR * (m_hat / (jnp.sqrt(v_hat) + ADAM_EPS) + ADAM_WD * w_ref[...])
        nm_ref[...] = nm
        nv_ref[...] = nv

    blk = pl.BlockSpec((tr, W), lambda i: (i, 0))
    shp = jax.ShapeDtypeStruct((R, W), F32)
    return pl.pallas_call(
        body, name=name, grid=(R // tr,), in_specs=[blk] * 4, out_specs=[blk] * 3, out_shape=[shp] * 3,
        compiler_params=_cparams(("parallel",)),
    )(w, g, m, v)


SMALL_ROWS = 32


def _pack_small(ln_g, ln_b, sinks):
    rows = jnp.concatenate([ln_g.reshape(-1, 128), ln_b.reshape(-1, 128),
                            jnp.pad(sinks.reshape(1, -1), ((0, 0), (0, 128 - sinks.size)))], axis=0)
    return jnp.pad(rows, ((0, SMALL_ROWS - rows.shape[0]), (0, 0)))


def _unpack_small(s, ln_shape, sink_shape):
    n = ln_shape[0] * ln_shape[1] * ln_shape[2] // 128
    return s[:n].reshape(ln_shape), s[n:2 * n].reshape(ln_shape), s[2 * n, :sink_shape[1]].reshape(sink_shape)


def _ffn_fwd(xin, w_in, w_out, gain, bias, tag):
    u, h = _ffn_in(xin, w_in, "ffn_in_" + tag)
    y, yb, z = _mm_ln(h, w_out, xin, gain, bias, 0.5, "ffn_out_ln_" + tag)
    return y, yb, dict(u=u, h=h, z=z, xin=xin)


def _ffn_bwd(dy, saved, w_in, w_out, gain, xin_b, tag, dw_dtype=F32):
    dz, dzc, gg, gb = _ln_bwd(saved["z"], dy, gain, 0.5, "ln_bwd_" + tag)
    du = _ffn_bwd_h(dzc, w_out, saved["u"], "ffn_bwd_h_" + tag)
    d_w_out = _mm_tn(saved["h"], dzc, "ffn_dwout_" + tag, out_dtype=dw_dtype)
    d_w_in = _mm_tn(xin_b, du, "ffn_dwin_" + tag, out_dtype=dw_dtype)
    dx = _mm_nt(du, w_in, "ffn_dx_" + tag, add=dz, add_scale=ALPHA)
    return dx, d_w_in, d_w_out, gg, gb


def kernel(x, ffn1_w_in, ffn1_w_out, ffn2_w_in, ffn2_w_out, ln_g, ln_b, a_w_qkv, a_w_o, kv_w, b_w_q, b_sinks, b_w_o, loss_target, m_ffn1_w_in, m_ffn1_w_out, m_ffn2_w_in, m_ffn2_w_out, m_ln_g, m_ln_b, m_a_w_qkv, m_a_w_o, m_kv_w, m_b_w_q, m_b_sinks, m_b_w_o, v_ffn1_w_in, v_ffn1_w_out, v_ffn2_w_in, v_ffn2_w_out, v_ln_g, v_ln_b, v_a_w_qkv, v_a_w_o, v_kv_w, v_b_w_q, v_b_sinks, v_b_w_o):
    ws = dict(ffn1_w_in=ffn1_w_in, ffn1_w_out=ffn1_w_out, ffn2_w_in=ffn2_w_in, ffn2_w_out=ffn2_w_out, a_w_qkv=a_w_qkv,
              a_w_o=a_w_o, kv_w=kv_w, b_w_q=b_w_q, b_w_o=b_w_o)
    ms = dict(ffn1_w_in=m_ffn1_w_in, ffn1_w_out=m_ffn1_w_out, ffn2_w_in=m_ffn2_w_in, ffn2_w_out=m_ffn2_w_out,
              a_w_qkv=m_a_w_qkv, a_w_o=m_a_w_o, kv_w=m_kv_w, b_w_q=m_b_w_q, b_w_o=m_b_w_o)
    vs = dict(ffn1_w_in=v_ffn1_w_in, ffn1_w_out=v_ffn1_w_out, ffn2_w_in=v_ffn2_w_in, ffn2_w_out=v_ffn2_w_out,
              a_w_qkv=v_a_w_qkv, a_w_o=v_a_w_o, kv_w=v_kv_w, b_w_q=v_b_w_q, b_w_o=v_b_w_o)
    _, _, c_idx, myq = _place()
    xs = x[0]
    target = loss_target[0]

    shards = {(n, l): (ws[n] if l is None else ws[n][l]).astype(BF16) for n, l in LAYER0_ITEMS + LAYER1_ITEMS}

    def as_weights(items, arrays):
        return {n: (a.reshape(D_MODEL, a.shape[-1]) if a.ndim == 4 else a) for (n, _), a in zip(items, arrays)}

    full0, small = _all_gather(LAYER0_ITEMS, shards, _pack_small(ln_g, ln_b, b_sinks))
    gather_state, token = _gather_start(LAYER1_ITEMS, shards, small)

    def layer1_weights(after):
        return as_weights(LAYER1_ITEMS, _gather_wait(LAYER1_ITEMS, gather_state, after))

    n_ln = ln_g.size // 128
    lg = jnp.concatenate([small[q, :n_ln].reshape(DEPTH, 3, 1, -1) for q in range(N_CHIPS)], axis=-1)
    lb = jnp.concatenate([small[q, n_ln:2 * n_ln].reshape(DEPTH, 3, 1, -1) for q in range(N_CHIPS)], axis=-1)
    lg = lg + token[0, 0]
    reducer = _GradReducer(c_idx, myq, {n: ws[n].shape for n in BIG})
    sq, grad_x, _, gg, gb, dsink_part = _local_step(xs, target, as_weights(LAYER0_ITEMS, full0), layer1_weights,
                                                    lg, lb, b_sinks.reshape(N_HEADS), reducer.begin)

    loss_row = jnp.pad(jnp.sum(sq).reshape(1, 1), ((0, 0), (0, 127)))
    dsinks = jnp.pad(dsink_part[:, 0, :].reshape(N_SLABS, 2, HEAD_DIM)[:, :, 0].reshape(1, N_HEADS), ((0, 0), (0, 128 - N_HEADS)))
    gg_full = jnp.stack([jnp.stack([jnp.sum(gg[i][j], axis=0) for j in range(3)]) for i in range(DEPTH)])
    gb_full = jnp.stack([jnp.stack([jnp.sum(gb[i][j], axis=0) for j in range(3)]) for i in range(DEPTH)])
    small_in = jnp.concatenate([loss_row, dsinks, gg_full.reshape(-1, 128), gb_full.reshape(-1, 128)], axis=0)
    small_in = jnp.pad(small_in, ((0, (-small_in.shape[0]) % 8), (0, 0)))
    small_sum = _small_all_reduce(small_in)
    loss = small_sum[0, 0] * (0.5 / D_MODEL)
    grad_sinks = small_sum[1, :N_HEADS].reshape(b_sinks.shape)
    n_full = DEPTH * 3 * D_MODEL // 128
    cols = D_MODEL // N_CHIPS
    grad_ln_g = lax.dynamic_slice_in_dim(small_sum[2:2 + n_full].reshape(DEPTH, 3, D_MODEL), myq * cols, cols, axis=2)
    grad_ln_b = lax.dynamic_slice_in_dim(small_sum[2 + n_full:2 + 2 * n_full].reshape(DEPTH, 3, D_MODEL), myq * cols, cols, axis=2)
    return _update(reducer, grad_x, loss, grad_ln_g, grad_ln_b, grad_sinks, ws, ms, vs,
                   (ln_g, ln_b, b_sinks), (m_ln_g, m_ln_b, m_b_sinks), (v_ln_g, v_ln_b, v_b_sinks))


def _local_step(xs, target, W, layer1_weights, lg, lb, sinks, grads_ready=None):
    if grads_ready is None:
        grads_ready = lambda tag, grads, overlap: 0.0
    S = xs.shape[0]
    slopes = jnp.asarray(_alibi_slopes(N_HEADS))
    in1, out1, in2, out2 = [W["ffn1_w_in"]], [W["ffn1_w_out"]], [W["ffn2_w_in"]], [W["ffn2_w_out"]]

    y1, y1b, s1 = _ffn_fwd(xs, in1[0], out1[0], lg[0, 0], lb[0, 0], "a1")
    qkv_a = _mm_nn(y1b, W["a_w_qkv"], F32, "qkv_a", split=True)
    mix_a, o_a, lse_a = _attn_fwd(qkv_a, slopes, None, PATTERNS_A, "attn_a_fwd")
    y2, y2b, z2 = _mm_ln(mix_a, W["a_w_o"], y1, lg[0, 1], lb[0, 1], 1.0, "attn_a_out_ln")
    y3, y3b, s3 = _ffn_fwd(y2, in2[0], out2[0], lg[0, 2], lb[0, 2], "a2")
    kv_w_rep = jnp.broadcast_to(W["kv_w"].reshape(D_MODEL, 2, N_KV_B, 1, HEAD_DIM),
                                (D_MODEL, 2, N_KV_B, GROUP_B, HEAD_DIM)).reshape(D_MODEL, 2 * D_MODEL)
    kv_rep = _mm_nn(y3b, kv_w_rep, F32, "kv_proj", split=(1, 2))
    W = dict(W, **layer1_weights(kv_rep))
    in1, out1, in2, out2 = (in1 + [W["ffn1_w_in"]], out1 + [W["ffn1_w_out"]], in2 + [W["ffn2_w_in"]],
                            out2 + [W["ffn2_w_out"]])
    y4, y4b, s4 = _ffn_fwd(y3, in1[1], out1[1], lg[1, 0], lb[1, 0], "b1")
    qkv_b = _mm_nn(y4b, W["b_w_q"], F32, "q_b", split=(0, 1), into=kv_rep)
    mix_b, o_b, lse_b = _attn_fwd(qkv_b, slopes, sinks, PATTERNS_B, "attn_b_fwd")
    y5, y5b, z5 = _mm_ln(mix_b, W["b_w_o"], y4, lg[1, 1], lb[1, 1], 1.0, "attn_b_out_ln")
    y6, _, s6 = _ffn_fwd(y5, in2[1], out2[1], lg[1, 2], lb[1, 2], "b2")

    dy6, sq = _loss_grad(y6, target, "loss_grad")
    gr = {n: None for n in BIG}
    gg = [[None] * 3 for _ in range(DEPTH)]
    gb = [[None] * 3 for _ in range(DEPTH)]

    dy5, d_in2_b, d_out2_b, gg[1][2], gb[1][2] = _ffn_bwd(dy6, s6, in2[1], out2[1], lg[1, 2], y5b, "b2", BF16)
    dz5, dz5b, gg[1][1], gb[1][1] = _ln_bwd(z5, dy5, lg[1, 1], 1.0, "ln_bwd_attn_b")
    gr["b_w_o"] = _mm_tn(mix_b, dz5b, "d_b_w_o", out_dtype=BF16)
    dmix_b = _mm_nt(dz5b, W["b_w_o"], "d_mix_b")
    dqkv_b, dsink_part = _attn_bwd(qkv_b, dmix_b, o_b, lse_b, slopes, sinks, PATTERNS_B, "attn_b_bwd")
    dq_b = (dqkv_b, 0)
    gr["b_w_q"] = _mm_tn(y4b, dq_b, "d_b_w_q", out_dtype=BF16)
    dy4 = _mm_nt(dq_b, W["b_w_q"], "d_y4", add=dz5, add_scale=ALPHA)
    dy3, d_in1_b, d_out1_b, gg[1][0], gb[1][0] = _ffn_bwd(dy4, s4, in1[1], out1[1], lg[1, 0], y3b, "b1", BF16)
    d_kv_w_rep = _mm_tn(y3b, dqkv_b, "d_kv_w", split=(1, 2))
    gr["kv_w"] = d_kv_w_rep.reshape(D_MODEL, 2, N_KV_B, GROUP_B, HEAD_DIM).sum(axis=3).reshape(D_MODEL, -1).astype(BF16)
    dy3 = _mm_nt(dqkv_b, kv_w_rep, "d_y3_kv", add=dy3, add_scale=1.0, split=(1, 2))
    tok = grads_ready("l1", {("ffn2_w_in", 1): d_in2_b, ("ffn2_w_out", 1): d_out2_b, ("b_w_o", None): gr["b_w_o"],
                             ("b_w_q", None): gr["b_w_q"], ("ffn1_w_in", 1): d_in1_b, ("ffn1_w_out", 1): d_out1_b,
                             ("kv_w", None): gr["kv_w"]}, True)
    lg0 = lg[0] + tok

    dy2, d_in2_a, d_out2_a, gg[0][2], gb[0][2] = _ffn_bwd(dy3, s3, in2[0], out2[0], lg0[2], y2b, "a2", BF16)
    tok = grads_ready("a2", {("ffn2_w_in", 0): d_in2_a, ("ffn2_w_out", 0): d_out2_a}, True)
    lg0 = lg0 + tok
    dz2, dz2b, gg[0][1], gb[0][1] = _ln_bwd(z2, dy2, lg0[1], 1.0, "ln_bwd_attn_a")
    gr["a_w_o"] = _mm_tn(mix_a, dz2b, "d_a_w_o", out_dtype=BF16)
    dmix_a = _mm_nt(dz2b, W["a_w_o"], "d_mix_a")
    dqkv_a, _ = _attn_bwd(qkv_a, dmix_a, o_a, lse_a, slopes, None, PATTERNS_A, "attn_a_bwd")
    gr["a_w_qkv"] = _mm_tn(y1b, dqkv_a, "d_a_w_qkv", split=True, out_dtype=BF16)
    tok = grads_ready("mix", {("a_w_o", None): gr["a_w_o"], ("a_w_qkv", None): gr["a_w_qkv"]}, True)
    lg0 = lg0 + tok
    dy1 = _mm_nt(dqkv_a, W["a_w_qkv"], "d_y1", add=dz2, add_scale=ALPHA, split=True)
    grad_x, d_in1_a, d_out1_a, gg[0][0], gb[0][0] = _ffn_bwd(dy1, s1, in1[0], out1[0], lg0[0], xs, "a1")
    grads_ready("a1", {("ffn1_w_in", 0): d_in1_a, ("ffn1_w_out", 0): d_out1_a}, False)
    gr["ffn1_w_in"] = [d_in1_a, d_in1_b]
    gr["ffn1_w_out"] = [d_out1_a, d_out1_b]
    gr["ffn2_w_in"] = [d_in2_a, d_in2_b]
    gr["ffn2_w_out"] = [d_out2_a, d_out2_b]
    return sq, grad_x, gr, gg, gb, dsink_part


def _grad_item(name, layer, g):
    if name.endswith("w_in"):
        return (g, "col", HALF_FF, _slot, name, layer)
    if name.endswith("w_out"):
        return (g, "row", D_MODEL, None, name, layer)
    if name == "a_w_qkv":
        return (g, "col", QKV_SHARD, lambda q: q, name, None)
    return (g, "row", g.shape[1], None, name, None)


class _GradReducer:
    def __init__(self, c_idx, myq, shard_shapes):
        self.c_idx, self.myq, self.shard_shapes = c_idx, myq, shard_shapes
        self.groups = []

    def begin(self, tag, grads, overlap):
        items = [_grad_item(n, l, g) for (n, l), g in grads.items()]
        kinds, widths, colblocks = [it[1] for it in items], [it[2] for it in items], [it[3] for it in items]
        views = [_grad_view(k, it[0]) for k, it in zip(kinds, items)]
        if overlap:
            lands = [jax.ShapeDtypeStruct((N_DIRECT,) + _piece_shape(k, w, _half_shape(k, v.shape)), BF16)
                     for k, w, v in zip(kinds, widths, views)]
            state, token = _split_start("grad_direct_start_" + tag, _direct_copies(kinds, widths, colblocks), 10 * len(items),
                                        views, lands, views[-1])
            self.groups.append((tag, items, None, state))
            return token[0, 0]
        from_sibling = _pair_exchange(views, kinds, "grad_pair_exchange_" + tag)
        sums = [_pair_sum(k, v, r, self.c_idx, "pair_sum_%s_%d" % (tag, t))
                for t, (k, v, r) in enumerate(zip(kinds, views, from_sibling))]
        self.groups.append((tag, items, sums, None))
        return 0.0

    def _sum_group(self, tag, items, sums, received, direct):
        for t, (it, s, r) in enumerate(zip(items, sums, received)):
            _, k, _, cb, name, layer = it
            own = cb(self.myq) if k == "col" else self.myq
            self.half_done[name] = _chip_sum(k, s, r, own, self.c_idx, self.shard_shapes[name], layer,
                                             self.half_done.get(name), "chip_sum_%s_%d" % (tag, t), direct=direct)

    def finish_first(self, after):
        self.half_done, self.late = {}, []
        started = [after]
        for tag, items, sums, state in self.groups:
            if state is None:
                kinds, widths, colblocks = [it[1] for it in items], [it[2] for it in items], [it[3] for it in items]
                copies = _chip_copies(kinds, widths, colblocks)
                st, token = _split_start("grad_chip_start_" + tag, copies, 3 * len(items), sums,
                                         _chip_land_shapes(sums, kinds, widths), sums[-1])
                self.late.append((tag, items, copies, st))
                started.append(token)
        for tag, items, sums, state in self.groups:
            if state is not None:
                kinds, widths, colblocks = [it[1] for it in items], [it[2] for it in items], [it[3] for it in items]
                views, received = _split_wait("grad_direct_wait_" + tag, _direct_copies(kinds, widths, colblocks), state,
                                              started)
                self._sum_group(tag, items, views, received, True)
        late_names = {it[4] for _, items, _, _ in self.late for it in items}
        names = [n for n in BIG if n not in late_names]
        return dict(zip(names, _share_halves([self.half_done[n] for n in names], "grad_share_halves_first")))

    def finish_rest(self, after):
        names = []
        for tag, items, copies, st in self.late:
            sums, received = _split_wait("grad_chip_wait_" + tag, copies, st, after)
            self._sum_group(tag, items, sums, received, False)
            names += [it[4] for it in items if it[4] not in names]
        return dict(zip(names, _share_halves([self.half_done[n] for n in names], "grad_share_halves_rest")))


def _update(reducer, grad_x, loss, grad_ln_g, grad_ln_b, grad_sinks, ws, ms, vs, small_w, small_m, small_v):
    ln_g, ln_b, b_sinks = small_w
    m_ln_g, m_ln_b, m_b_sinks = small_m
    v_ln_g, v_ln_b, v_b_sinks = small_v

    deltas, new_m, new_v = {}, {}, {}

    def update(some):
        done = []
        for name in some:
            shp = ws[name].shape
            flat = lambda a: a.reshape(-1, shp[-1])
            d, nm, nv = _adamw(flat(ws[name]), flat(some[name]), flat(ms[name]), flat(vs[name]), "adamw_" + name)
            deltas[name], new_m[name], new_v[name] = d.reshape(shp), nm.reshape(shp), nv.reshape(shp)
            done.append(d)
        return done

    grads = reducer.finish_first(grad_x)
    rest = reducer.finish_rest(update(grads))
    update(rest)
    grads.update(rest)
    delta_s, nm_s, nv_s = _adamw(_pack_small(ln_g, ln_b, b_sinks), _pack_small(grad_ln_g, grad_ln_b, grad_sinks),
                                 _pack_small(m_ln_g, m_ln_b, m_b_sinks), _pack_small(v_ln_g, v_ln_b, v_b_sinks), "adamw_small")
    for d, blob in ((grads, None), (deltas, delta_s), (new_m, nm_s), (new_v, nv_s)):
        if blob is None:
            d["ln_g"], d["ln_b"], d["b_sinks"] = grad_ln_g, grad_ln_b, grad_sinks
        else:
            d["ln_g"], d["ln_b"], d["b_sinks"] = _unpack_small(blob, ln_g.shape, b_sinks.shape)

    order = ("ffn1_w_in", "ffn1_w_out", "ffn2_w_in", "ffn2_w_out", "ln_g", "ln_b", "a_w_qkv", "a_w_o", "kv_w", "b_w_q",
             "b_sinks", "b_w_o")
    outs = [loss, grad_x[None]]
    for d in (grads, deltas, new_m, new_v):
        outs += [d[n] for n in order]
    return tuple(outs)
```

```python
import numpy as np
import jax
import jax.numpy as jnp
from jax import lax
from jax.experimental import pallas as pl
from jax.experimental.pallas import tpu as pltpu

F32 = jnp.float32
BF16 = jnp.bfloat16

D_MODEL = 1024
D_FF = 2816
HALF_FF = D_FF // 2
HEAD_DIM = 64
N_HEADS = 16
N_KV_B = 4
GROUP_B = N_HEADS // N_KV_B
DEPTH = 2
ALPHA = (2.0 * DEPTH) ** 0.25
LN_EPS = 1e-5
BLOCK = 128
SLAB = 128
N_SLABS = D_MODEL // SLAB
PATTERNS_A = ((1, 128, 1.0), (4, 128, 4.0), (16, 128, 16.0))
PATTERNS_B = ((1, 127, 1.0),)
NEG = -1e30

ADAM_LR = 0.001
ADAM_B1 = 0.9
ADAM_B2 = 0.999
ADAM_EPS = 1e-08
ADAM_WD = 0.01
ADAM_STEP = 10

N_CHIPS = 4
VMEM_LIMIT = 56 * 1024 * 1024
MESH = pl.DeviceIdType.MESH


def _alibi_slopes(n):
    return np.array([2.0 ** (-8.0 * (h + 1) / n) for h in range(n)], dtype=np.float32)


def _cparams(sem=None, vmem=VMEM_LIMIT):
    return pltpu.CompilerParams(dimension_semantics=sem, vmem_limit_bytes=vmem)


_DIMS = {"nn": ((1,), (0,)), "nt": ((1,), (1,)), "tn": ((0,), (0,))}


def _unlead(x):
    if isinstance(x, tuple):
        return x[0], x[1], x[0].shape[1:]
    return x, None, x.shape


def _bspec(block, imap, lead=None):
    if lead is None:
        return pl.BlockSpec(block, imap)
    return pl.BlockSpec((None,) + tuple(block), lambda *g: (lead,) + tuple(imap(*g)))


def _matmul(a, b, mode, out_dtype, tm, tn, tk, name, add=None, add_scale=1.0, split=False, into=None):
    out_spec = pl.BlockSpec((tm, tn), lambda i, j, k: (i, j))
    base, count = (0, 3) if split is True else (split or (0, 0))
    if mode == "nn":
        a, al, (M, K) = _unlead(a)
        b, bl, (K2, N) = _unlead(b)
        a_spec = _bspec((tm, tk), lambda i, j, k: (i, k), al)
        b_spec = _bspec((tk, tn), lambda i, j, k: (k, j), bl)
        out_struct = jax.ShapeDtypeStruct((M, N), out_dtype)
        if split:
            assert tn == D_MODEL and N == count * tn
            out_spec = pl.BlockSpec((None, tm, tn), lambda i, j, k: (j + base, i, 0))
            out_struct = jax.ShapeDtypeStruct((3, M, tn), out_dtype)
    elif mode == "nt":
        b, bl, (N, K2) = _unlead(b)
        if split:
            assert tk == D_MODEL
            M, K = a.shape[1], count * a.shape[2]
            a_spec = pl.BlockSpec((None, tm, tk), lambda i, j, k: (k + base, i, 0))
        else:
            a, al, (M, K) = _unlead(a)
            a_spec = _bspec((tm, tk), lambda i, j, k: (i, k), al)
        b_spec = _bspec((tn, tk), lambda i, j, k: (j, k), bl)
        out_struct = jax.ShapeDtypeStruct((M, N), out_dtype)
    else:
        a, al, (K, M) = _unlead(a)
        if split:
            assert tn == D_MODEL
            K2, N = b.shape[1], count * b.shape[2]
            b_spec = pl.BlockSpec((None, tk, tn), lambda i, j, k: (j + base, k, 0))
        else:
            b, bl, (K2, N) = _unlead(b)
            b_spec = _bspec((tk, tn), lambda i, j, k: (k, j), bl)
        a_spec = _bspec((tk, tm), lambda i, j, k: (k, i), al)
        out_struct = jax.ShapeDtypeStruct((M, N), out_dtype)
    assert K == K2 and M % tm == 0 and N % tn == 0 and K % tk == 0, (a.shape, b.shape, mode, tm, tn, tk)
    nk = K // tk
    dims = (_DIMS[mode], ((), ()))
    has_add = add is not None

    narrow = out_dtype != F32
    assert not (narrow and has_add)

    def body(*refs):
        if into is not None:
            refs = refs[:2] + refs[3:]
        if has_add:
            a_ref, b_ref, add_ref, o_ref = refs
            acc_ref = o_ref
        elif narrow:
            a_ref, b_ref, o_ref, acc_ref = refs
        else:
            a_ref, b_ref, o_ref = refs
            acc_ref = o_ref
        k = pl.program_id(2)
        part = lax.dot_general(a_ref[...].astype(BF16), b_ref[...].astype(BF16), dims, preferred_element_type=F32)
        if has_add:
            @pl.when(k == 0)
            def _():
                acc_ref[...] = part + add_scale * add_ref[...]
        else:
            @pl.when(k == 0)
            def _():
                acc_ref[...] = part

        @pl.when(k > 0)
        def _():
            acc_ref[...] += part

        if narrow:
            @pl.when(k == nk - 1)
            def _():
                o_ref[...] = acc_ref[...].astype(out_dtype)

    in_specs = [a_spec, b_spec]
    args = [a, b]
    aliases = {}
    if into is not None:
        assert mode == "nn" and split and not has_add
        in_specs.append(pl.BlockSpec(memory_space=pl.ANY))
        args.append(into)
        aliases = {2: 0}
    if has_add:
        in_specs.append(pl.BlockSpec((tm, tn), lambda i, j, k: (i, j)))
        args.append(add)
    return pl.pallas_call(
        body, name=name, grid=(M // tm, N // tn, nk),
        in_specs=in_specs, out_specs=out_spec, out_shape=out_struct, input_output_aliases=aliases,
        scratch_shapes=[pltpu.VMEM((tm, tn), F32)] if narrow else [],
        compiler_params=_cparams(("parallel", "parallel", "arbitrary")),
    )(*args)


def _pick(n, cands):
    for c in cands:
        if n % c == 0:
            return c
    raise ValueError((n, cands))


def _mm_nn(a, b, out_dtype, name, split=False, into=None):
    M, K = _unlead(a)[2]
    N = _unlead(b)[2][1]
    return _matmul(a, b, "nn", out_dtype, _pick(M, (1024, 512, 256)), _pick(N, (1024, 512)), _pick(K, (1024, 512)), name,
                   split=split, into=into)


def _mm_nt(a, b, name, add=None, add_scale=1.0, split=False):
    M, K = (a.shape[1], D_MODEL) if split else _unlead(a)[2]
    N = _unlead(b)[2][0]
    return _matmul(a, b, "nt", F32, _pick(M, (1024, 512, 256)), _pick(N, (1024, 512)),
                   _pick(K, (2816, 1024, 512)), name, add=add, add_scale=add_scale, split=split)


def _mm_tn(a, b, name, split=False, out_dtype=F32):
    K, M = _unlead(a)[2]
    N = D_MODEL if split else _unlead(b)[2][1]
    return _matmul(a, b, "tn", out_dtype, _pick(M, (1024, 1408, 512)), _pick(N, (1408, 1024, 512)),
                   _pick(K, (2048, 1024, 512, 256)), name, split=split)


def _ffn_in(x, w, name):
    S = x.shape[0]
    tm = _pick(S, (512, 256))
    w, wl, _ = _unlead(w)

    def body(x_ref, w_ref, t_ref, h_ref):
        acc = jnp.dot(x_ref[...].astype(BF16), w_ref[...], preferred_element_type=F32)
        g = acc[:, :HALF_FF]
        up = acc[:, HALF_FF:]
        sg = jax.nn.sigmoid(g)
        silu = g * sg
        t_ref[:, :HALF_FF] = (up * (sg * (1.0 + g * (1.0 - sg)))).astype(BF16)
        t_ref[:, HALF_FF:] = silu.astype(BF16)
        h_ref[...] = (silu * up).astype(BF16)

    return pl.pallas_call(
        body, name=name, grid=(2, S // tm),
        in_specs=[pl.BlockSpec((tm, D_MODEL), lambda j, i: (i, 0)),
                  _bspec((D_MODEL, D_FF), lambda j, i: (0, j), wl)],
        out_specs=[pl.BlockSpec((tm, D_FF), lambda j, i: (i, j)),
                   pl.BlockSpec((tm, HALF_FF), lambda j, i: (i, j))],
        out_shape=[jax.ShapeDtypeStruct((S, 2 * D_FF), BF16), jax.ShapeDtypeStruct((S, D_FF), BF16)],
        compiler_params=_cparams(("parallel", "parallel")),
    )(x, w)


def _ffn_bwd_h(dzc, w_out, u, name):
    S = dzc.shape[0]
    tm = _pick(S, (512, 256))
    w_out, wl, _ = _unlead(w_out)

    def body(dz_ref, w_ref, t_ref, du_ref):
        dh = lax.dot_general(dz_ref[...], w_ref[...], (((1,), (1,)), ((), ())), preferred_element_type=F32)
        du_ref[:, :HALF_FF] = (dh * t_ref[:, :HALF_FF].astype(F32)).astype(BF16)
        du_ref[:, HALF_FF:] = (dh * t_ref[:, HALF_FF:].astype(F32)).astype(BF16)

    return pl.pallas_call(
        body, name=name, grid=(2, S // tm),
        in_specs=[pl.BlockSpec((tm, D_MODEL), lambda j, i: (i, 0)),
                  _bspec((HALF_FF, D_MODEL), lambda j, i: (j, 0), wl),
                  pl.BlockSpec((tm, D_FF), lambda j, i: (i, j))],
        out_specs=pl.BlockSpec((tm, D_FF), lambda j, i: (i, j)),
        out_shape=jax.ShapeDtypeStruct((S, 2 * D_FF), BF16),
        compiler_params=_cparams(("parallel", "parallel")),
    )(dzc, w_out, u)


def _mm_ln(a, w, resid, gain, bias, c, name):
    S, K = a.shape
    tm = _pick(S, (512, 256))
    w, wl, _ = _unlead(w)

    def body(a_ref, w_ref, r_ref, g_ref, b_ref, y_ref, yb_ref, z_ref):
        z = ALPHA * r_ref[...] + c * jnp.dot(a_ref[...], w_ref[...], preferred_element_type=F32)
        mu = jnp.mean(z, axis=-1, keepdims=True)
        zc = z - mu
        var = jnp.mean(zc * zc, axis=-1, keepdims=True)
        y = zc * lax.rsqrt(var + LN_EPS) * g_ref[...] + b_ref[...]
        z_ref[...] = z
        y_ref[...] = y
        yb_ref[...] = y.astype(BF16)

    row = pl.BlockSpec((tm, D_MODEL), lambda i: (i, 0))
    vec = pl.BlockSpec((1, D_MODEL), lambda i: (0, 0))
    return pl.pallas_call(
        body, name=name, grid=(S // tm,),
        in_specs=[pl.BlockSpec((tm, K), lambda i: (i, 0)), _bspec((K, D_MODEL), lambda i: (0, 0), wl), row, vec, vec],
        out_specs=[row, row, row],
        out_shape=[jax.ShapeDtypeStruct((S, D_MODEL), F32), jax.ShapeDtypeStruct((S, D_MODEL), BF16),
                   jax.ShapeDtypeStruct((S, D_MODEL), F32)],
        compiler_params=_cparams(("parallel",)),
    )(a, w, resid, gain, bias)


def _ln_bwd(z, dy, gain, c, name):
    S = z.shape[0]
    tm = _pick(S, (512, 256))

    def body(z_ref, dy_ref, g_ref, dz_ref, dzc_ref, gg_ref, gb_ref):
        i = pl.program_id(0)
        zv = z_ref[...]
        dyv = dy_ref[...]
        mu = jnp.mean(zv, axis=-1, keepdims=True)
        zc = zv - mu
        var = jnp.mean(zc * zc, axis=-1, keepdims=True)
        rstd = lax.rsqrt(var + LN_EPS)
        xhat = zc * rstd
        dyg = dyv * g_ref[...]
        m1 = jnp.mean(dyg, axis=-1, keepdims=True)
        m2 = jnp.mean(dyg * xhat, axis=-1, keepdims=True)
        dz = rstd * (dyg - m1 - xhat * m2)
        dz_ref[...] = dz
        dzc_ref[...] = (c * dz).astype(BF16)
        pg = jnp.sum((dyv * xhat).reshape(tm // 8, 8, D_MODEL), axis=0)
        pb = jnp.sum(dyv.reshape(tm // 8, 8, D_MODEL), axis=0)

        @pl.when(i == 0)
        def _():
            gg_ref[...] = pg
            gb_ref[...] = pb

        @pl.when(i > 0)
        def _():
            gg_ref[...] += pg
            gb_ref[...] += pb

    row = pl.BlockSpec((tm, D_MODEL), lambda i: (i, 0))
    part = pl.BlockSpec((8, D_MODEL), lambda i: (0, 0))
    return pl.pallas_call(
        body, name=name, grid=(S // tm,),
        in_specs=[row, row, pl.BlockSpec((1, D_MODEL), lambda i: (0, 0))],
        out_specs=[row, row, part, part],
        out_shape=[jax.ShapeDtypeStruct((S, D_MODEL), F32), jax.ShapeDtypeStruct((S, D_MODEL), BF16),
                   jax.ShapeDtypeStruct((8, D_MODEL), F32), jax.ShapeDtypeStruct((8, D_MODEL), F32)],
        compiler_params=_cparams(("arbitrary",)),
    )(z, dy, gain)


def _loss_grad(y, t, name):
    S = y.shape[0]
    tm = _pick(S, (512, 256))

    def body(y_ref, t_ref, dy_ref, sq_ref):
        i = pl.program_id(0)
        e = y_ref[...] - t_ref[...]
        dy_ref[...] = e * (1.0 / D_MODEL)
        ps = jnp.sum((e * e).reshape(tm // 8, 8, D_MODEL), axis=0)

        @pl.when(i == 0)
        def _():
            sq_ref[...] = ps

        @pl.when(i > 0)
        def _():
            sq_ref[...] += ps

    row = pl.BlockSpec((tm, D_MODEL), lambda i: (i, 0))
    return pl.pallas_call(
        body, name=name, grid=(S // tm,),
        in_specs=[row, row], out_specs=[row, pl.BlockSpec((8, D_MODEL), lambda i: (0, 0))],
        out_shape=[jax.ShapeDtypeStruct((S, D_MODEL), F32), jax.ShapeDtypeStruct((8, D_MODEL), F32)],
        compiler_params=_cparams(("arbitrary",)),
    )(y, t)


def _rows(start, d):
    if d == 1:
        return pl.ds(pl.multiple_of(start, BLOCK), BLOCK)
    return pl.ds(start, BLOCK, stride=d)


def _ld(ref, start, d):
    return ref[_rows(start, d), :]


def _ld3(ref, lead, start, d):
    return ref[lead, _rows(start, d), :]


def _st3(ref, lead, start, d, val):
    ref[lead, _rows(start, d), :] = val


def _acc3(ref, lead, start, d, val):
    ref[lead, _rows(start, d), :] = ref[lead, _rows(start, d), :] + val


def _band_consts(slope0, slope1, maxd, scale):
    row = lax.broadcasted_iota(jnp.int32, (2 * BLOCK, 2 * BLOCK), 0)
    kj = lax.broadcasted_iota(jnp.int32, (2 * BLOCK, 2 * BLOCK), 1)
    top = row < BLOCK
    dist = BLOCK + jnp.where(top, row, row - BLOCK) - kj
    slope = jnp.where(top, slope0, slope1)
    base = jnp.where((dist >= 0) & (dist <= maxd), -(slope * (dist.astype(F32) * scale)), NEG)
    return base, kj < BLOCK


def _stack_heads(x, lo):
    return jnp.concatenate([jnp.where(lo, x, 0.0), jnp.where(lo, 0.0, x)], axis=0)


def _unstack_heads(x2, lo):
    return jnp.where(lo, x2[:BLOCK], x2[BLOCK:])


def _scores(q2, k2, base, prev_keys, first):
    s = lax.dot_general(q2, k2, (((1,), (1,)), ((), ())), preferred_element_type=F32) * (HEAD_DIM ** -0.5) + base
    return jnp.where(jnp.logical_and(prev_keys, first), NEG, s)


def _softmax_weights(ls):
    mx = ls[0]
    for l in ls[1:]:
        mx = jnp.maximum(mx, l)
    es = [jnp.exp(l - mx) for l in ls]
    tot = es[0]
    for e in es[1:]:
        tot = tot + e
    inv = 1.0 / tot
    return [e * inv for e in es]


def _attn_fwd(qkv, slopes, sinks, patterns, name):
    S = qkv.shape[1]
    npat = len(patterns)
    has_sink = sinks is not None
    if not has_sink:
        sinks = jnp.zeros((N_HEADS,), F32)
    rows_c = 256

    def body(slopes_ref, sinks_ref, x_ref, mix_ref, o_ref, lse_ref, o_scr, lse_scr):
        p = pl.program_id(0)
        lo = lax.broadcasted_iota(jnp.int32, (BLOCK, SLAB), 1) < HEAD_DIM
        top1 = lax.broadcasted_iota(jnp.int32, (2 * BLOCK, 1), 0) < BLOCK
        sk2 = jnp.where(top1, sinks_ref[2 * p], sinks_ref[2 * p + 1])
        for pi, (d, maxd, scale) in enumerate(patterns):
            nb = S // d // BLOCK
            base, prev_keys = _band_consts(slopes_ref[2 * p], slopes_ref[2 * p + 1], maxd, scale)

            def blk(t, carry, pi=pi, d=d, nb=nb, base=base, prev_keys=prev_keys):
                r = t // nb
                n = t - r * nb
                start = r + (d * BLOCK) * n
                prev = jnp.where(n > 0, start - d * BLOCK, start)
                q2 = _stack_heads(_ld3(x_ref, 0, start, d), lo).astype(BF16)
                k2 = jnp.concatenate([_ld3(x_ref, 1, prev, d), _ld3(x_ref, 1, start, d)], axis=0).astype(BF16)
                v2 = jnp.concatenate([_ld3(x_ref, 2, prev, d), _ld3(x_ref, 2, start, d)], axis=0).astype(BF16)
                s = _scores(q2, k2, base, prev_keys, n == 0)
                m = jnp.max(s, axis=-1, keepdims=True)
                if has_sink:
                    m = jnp.maximum(m, sk2)
                e = jnp.exp(s - m)
                den = jnp.sum(e, axis=-1, keepdims=True)
                if has_sink:
                    den = den + jnp.exp(sk2 - m)
                o2 = jnp.dot((e / den).astype(BF16), v2, preferred_element_type=F32)
                _st3(o_scr, pi, start, d, _unstack_heads(o2, lo))
                _st3(lse_scr, pi, start, d, _unstack_heads(m + jnp.log(den), lo))
                return carry

            lax.fori_loop(0, d * nb, blk, 0, unroll=8)

        lane_c = lax.broadcasted_iota(jnp.int32, (rows_c, SLAB), 1)

        def comb(ci, carry):
            rows = pl.ds(pl.multiple_of(ci * rows_c, rows_c), rows_c)
            ls = [lse_scr[i, rows, :] for i in range(npat)]
            packed = jnp.zeros((rows_c, SLAB), F32)
            for i in range(npat):
                o_ref[i, rows, :] = o_scr[i, rows, :].astype(BF16)
                packed = jnp.where(lane_c % HEAD_DIM == i, ls[i], packed)
            lse_ref[rows, :] = packed
            if npat == 1:
                mix_ref[rows, :] = o_scr[0, rows, :].astype(BF16)
            else:
                ws = _softmax_weights(ls)
                acc = ws[0] * o_scr[0, rows, :]
                for i in range(1, npat):
                    acc = acc + ws[i] * o_scr[i, rows, :]
                mix_ref[rows, :] = acc.astype(BF16)
            return carry

        lax.fori_loop(0, S // rows_c, comb, 0, unroll=2)

    smem = pl.BlockSpec(memory_space=pltpu.SMEM)
    return pl.pallas_call(
        body, name=name, grid=(N_SLABS,),
        in_specs=[smem, smem, pl.BlockSpec((3, S, SLAB), lambda p: (0, 0, p))],
        out_specs=[pl.BlockSpec((S, SLAB), lambda p: (0, p)), pl.BlockSpec((npat, S, SLAB), lambda p: (0, 0, p)),
                   pl.BlockSpec((None, S, SLAB), lambda p: (p, 0, 0))],
        out_shape=[jax.ShapeDtypeStruct((S, D_MODEL), BF16), jax.ShapeDtypeStruct((npat, S, D_MODEL), BF16),
                   jax.ShapeDtypeStruct((N_SLABS, S, SLAB), F32)],
        scratch_shapes=[pltpu.VMEM((npat, S, SLAB), F32), pltpu.VMEM((npat, S, SLAB), F32)],
        compiler_params=_cparams(("arbitrary",)),
    )(slopes, sinks, qkv)


def _attn_bwd(qkv, dout, o, lse, slopes, sinks, patterns, name):
    S = qkv.shape[1]
    npat = len(patterns)
    has_sink = sinks is not None
    if not has_sink:
        sinks = jnp.zeros((N_HEADS,), F32)
    rows_c = 256

    def headsum(x, lo):
        same = (lax.broadcasted_iota(jnp.int32, (SLAB, SLAB), 0) < HEAD_DIM) == (lax.broadcasted_iota(jnp.int32, (SLAB, SLAB), 1) < HEAD_DIM)
        return jnp.dot(x, same.astype(F32), precision=lax.Precision.HIGH, preferred_element_type=F32)

    def body(slopes_ref, sinks_ref, x_ref, do_ref, o_ref, lsep_ref, dxo_ref, dsink_ref, dbar_ref, sacc_ref, lse_ref, dx_ref):
        p = pl.program_id(0)
        lo = lax.broadcasted_iota(jnp.int32, (BLOCK, SLAB), 1) < HEAD_DIM
        lo_c = lax.broadcasted_iota(jnp.int32, (rows_c, SLAB), 1) < HEAD_DIM
        top1 = lax.broadcasted_iota(jnp.int32, (2 * BLOCK, 1), 0) < BLOCK
        sk2 = jnp.where(top1, sinks_ref[2 * p], sinks_ref[2 * p + 1])

        def prep(ci, carry):
            rows = pl.ds(pl.multiple_of(ci * rows_c, rows_c), rows_c)
            dov = do_ref[rows, :]
            dx_ref[:, rows, :] = jnp.zeros((3, rows_c, SLAB), F32)
            packed = lsep_ref[rows, :]
            ls = [jnp.where(lo_c, packed[:, i:i + 1], packed[:, HEAD_DIM + i:HEAD_DIM + i + 1]) for i in range(npat)]
            for i in range(npat):
                lse_ref[i, rows, :] = ls[i]
            if npat == 1:
                dbar_ref[rows, :] = headsum(dov * o_ref[0, rows, :].astype(F32), lo_c)
            else:
                ws = _softmax_weights(ls)
                acc = ws[0] * headsum(dov * o_ref[0, rows, :].astype(F32), lo_c)
                for i in range(1, npat):
                    acc = acc + ws[i] * headsum(dov * o_ref[i, rows, :].astype(F32), lo_c)
                dbar_ref[rows, :] = acc
            return carry

        lax.fori_loop(0, S // rows_c, prep, 0, unroll=2)
        sacc_ref[...] = jnp.zeros((BLOCK, SLAB), F32)

        for pi, (d, maxd, scale) in enumerate(patterns):
            nb = S // d // BLOCK
            base, prev_keys = _band_consts(slopes_ref[2 * p], slopes_ref[2 * p + 1], maxd, scale)

            def blk(t, carry, pi=pi, d=d, nb=nb, base=base, prev_keys=prev_keys):
                r = t // nb
                n = t - r * nb
                start = r + (d * BLOCK) * n
                prev = jnp.where(n > 0, start - d * BLOCK, start)
                q2 = _stack_heads(_ld3(x_ref, 0, start, d), lo).astype(BF16)
                k2 = jnp.concatenate([_ld3(x_ref, 1, prev, d), _ld3(x_ref, 1, start, d)], axis=0).astype(BF16)
                v2 = jnp.concatenate([_ld3(x_ref, 2, prev, d), _ld3(x_ref, 2, start, d)], axis=0).astype(BF16)
                ls = [_ld3(lse_ref, i, start, d) for i in range(npat)]
                w = _softmax_weights(ls)[pi] if npat > 1 else 1.0
                do2 = _stack_heads(w * _ld(do_ref, start, d), lo).astype(BF16)
                dl = w * _ld(dbar_ref, start, d)
                lse2 = jnp.concatenate([ls[pi][:, :1], ls[pi][:, HEAD_DIM:HEAD_DIM + 1]], axis=0)
                dl2 = jnp.concatenate([dl[:, :1], dl[:, HEAD_DIM:HEAD_DIM + 1]], axis=0)
                s = _scores(q2, k2, base, prev_keys, n == 0)
                pr = jnp.exp(s - lse2)
                dp = lax.dot_general(do2, v2, (((1,), (1,)), ((), ())), preferred_element_type=F32)
                ds = (pr * (dp - dl2) * (HEAD_DIM ** -0.5)).astype(BF16)
                dq2 = jnp.dot(ds, k2, preferred_element_type=F32)
                dk2 = lax.dot_general(ds, q2, (((0,), (0,)), ((), ())), preferred_element_type=F32)
                dv2 = lax.dot_general(pr.astype(BF16), do2, (((0,), (0,)), ((), ())), preferred_element_type=F32)
                _acc3(dx_ref, 0, start, d, _unstack_heads(dq2, lo))
                _acc3(dx_ref, 1, prev, d, dk2[:BLOCK])
                _acc3(dx_ref, 1, start, d, dk2[BLOCK:])
                _acc3(dx_ref, 2, prev, d, dv2[:BLOCK])
                _acc3(dx_ref, 2, start, d, dv2[BLOCK:])
                if has_sink:
                    sacc_ref[...] += _unstack_heads(-jnp.exp(sk2 - lse2) * dl2, lo)
                return carry

            lax.fori_loop(0, d * nb, blk, 0, unroll=8)

        dsink_ref[...] = jnp.broadcast_to(jnp.sum(sacc_ref[...], axis=0, keepdims=True), (8, SLAB))

        def emit(ci, carry):
            rows = pl.ds(pl.multiple_of(ci * rows_c, rows_c), rows_c)
            dxo_ref[:, rows, :] = dx_ref[:, rows, :].astype(BF16)
            return carry

        lax.fori_loop(0, S // rows_c, emit, 0, unroll=2)

    smem = pl.BlockSpec(memory_space=pltpu.SMEM)
    return pl.pallas_call(
        body, name=name, grid=(N_SLABS,),
        in_specs=[smem, smem, pl.BlockSpec((3, S, SLAB), lambda p: (0, 0, p)), pl.BlockSpec((S, SLAB), lambda p: (0, p)),
                  pl.BlockSpec((npat, S, SLAB), lambda p: (0, 0, p)), pl.BlockSpec((None, S, SLAB), lambda p: (p, 0, 0))],
        out_specs=[pl.BlockSpec((3, S, SLAB), lambda p: (0, 0, p)), pl.BlockSpec((None, 8, SLAB), lambda p: (p, 0, 0))],
        out_shape=[jax.ShapeDtypeStruct((3, S, D_MODEL), BF16), jax.ShapeDtypeStruct((N_SLABS, 8, SLAB), F32)],
        scratch_shapes=[pltpu.VMEM((S, SLAB), F32), pltpu.VMEM((BLOCK, SLAB), F32), pltpu.VMEM((npat, S, SLAB), F32),
                        pltpu.VMEM((3, S, SLAB), F32)],
        compiler_params=_cparams(("arbitrary",)),
    )(slopes, sinks, qkv, dout, o, lse)


def _place():
    x, y, c = lax.axis_index("x"), lax.axis_index("y"), lax.axis_index("c")
    return x, y, c, 2 * x + y


def _other_chips(x, y):
    return [(1 - x, y), (x, 1 - y), (1 - x, 1 - y)]


HBM_SPEC = pl.BlockSpec(memory_space=pl.ANY)


def _slot(q):
    return 2 * (q % 2) + q // 2


BIG = ("ffn1_w_in", "ffn1_w_out", "ffn2_w_in", "ffn2_w_out", "a_w_qkv", "a_w_o", "kv_w", "b_w_q", "b_w_o")
QKV_SHARD = 3 * D_MODEL // N_CHIPS
ROW_SHARD = D_MODEL // N_CHIPS


LAYER0_ITEMS = (("ffn1_w_in", 0), ("ffn1_w_out", 0), ("a_w_qkv", None), ("a_w_o", None), ("ffn2_w_in", 0),
                ("ffn2_w_out", 0), ("kv_w", None))
LAYER1_ITEMS = (("ffn1_w_in", 1), ("ffn1_w_out", 1), ("b_w_q", None), ("b_w_o", None), ("ffn2_w_in", 1),
                ("ffn2_w_out", 1))
OUT_SHARD = D_FF // N_CHIPS


def _full_shape(name):
    if name.endswith("w_in"):
        return (D_MODEL, 2 * D_FF)
    if name.endswith("w_out"):
        return (D_FF, D_MODEL)
    if name == "a_w_qkv":
        return (D_MODEL, 3 * D_MODEL)
    if name == "kv_w":
        return (N_CHIPS, 2, ROW_SHARD // 2, 2 * N_KV_B * HEAD_DIM)
    return (N_CHIPS, 2, ROW_SHARD // 2, D_MODEL)


def _gather_src(item, ref, c):
    name, _ = item
    if name.endswith("w_in"):
        return ref.at[pl.ds(c * (D_MODEL // 2), D_MODEL // 2)]
    if name.endswith("w_out"):
        return ref.at[pl.ds(c * (OUT_SHARD // 2), OUT_SHARD // 2)]
    if name == "a_w_qkv":
        return ref.at[0, pl.ds(c * (D_MODEL // 2), D_MODEL // 2)]
    if name == "kv_w":
        return ref.at[pl.ds(c * (ROW_SHARD // 2), ROW_SHARD // 2)]
    return ref.at[0, pl.ds(c * (ROW_SHARD // 2), ROW_SHARD // 2)]


def _gather_dst(item, ref, q, c):
    name, _ = item
    if name.endswith("w_in"):
        return ref.at[pl.ds(c * (D_MODEL // 2), D_MODEL // 2), pl.ds(_slot(q) * HALF_FF, HALF_FF)]
    if name.endswith("w_out"):
        return ref.at[pl.ds(q * OUT_SHARD + c * (OUT_SHARD // 2), OUT_SHARD // 2)]
    if name == "a_w_qkv":
        return ref.at[pl.ds(c * (D_MODEL // 2), D_MODEL // 2), pl.ds(q * QKV_SHARD, QKV_SHARD)]
    return ref.at[q, c]


def _all_gather(items, shards, small):
    n = len(items)
    r = small.shape[0]
    per = 8

    def body(*refs):
        srcs, small_ref = refs[:n], refs[n]
        dsts, s_ref = refs[n + 1:2 * n + 1], refs[2 * n + 1]
        send_sems, recv_sems = refs[2 * n + 2:]
        x, y, c, myq = _place()
        sibling = (x, y, 1 - c)
        chips = _other_chips(x, y)

        def big(t, k, src, q, h, to):
            return pltpu.make_async_remote_copy(src_ref=src, dst_ref=_gather_dst(items[t], dsts[t], q, h),
                                                send_sem=send_sems.at[per * t + k], recv_sem=recv_sems.at[per * t + k],
                                                device_id=to, device_id_type=MESH)

        def tiny(k, q, to):
            return pltpu.make_async_remote_copy(src_ref=small_ref, dst_ref=s_ref.at[q], send_sem=send_sems.at[per * n + k],
                                                recv_sem=recv_sems.at[per * n + k], device_id=to, device_id_type=MESH)

        first = []
        for j, chip in enumerate(chips):
            first += [big(t, j, _gather_src(items[t], srcs[t], c), myq, c, (*chip, c)) for t in range(n)]
            first.append(tiny(j, myq, (*chip, c)))
        own = [big(t, 6 + h, _gather_src(items[t], srcs[t], h), myq, h, sibling) for t in range(n) for h in (0, 1)]
        own.append(tiny(3, myq, sibling))
        for cp in first + own:
            cp.start()
        passed = []
        for j, (cx, cy) in enumerate(chips):
            q = 2 * cx + cy
            for t in range(n):
                src = _gather_src(items[t], srcs[t], c)
                big(t, j, src, q, c, sibling).wait_recv()
                fwd = big(t, 3 + j, _gather_dst(items[t], dsts[t], q, c), q, c, sibling)
                fwd.start()
                passed.append(fwd)
        for j, (cx, cy) in enumerate(chips):
            q = 2 * cx + cy
            for t in range(n):
                big(t, 3 + j, _gather_src(items[t], srcs[t], c), q, 1 - c, sibling).wait_recv()
            tiny(j, q, sibling).wait_recv()
        for cp in own:
            cp.wait_recv()
        for cp in first + passed + own:
            cp.wait_send()

    outs = pl.pallas_call(
        body, name="all_gather_layer0",
        in_specs=[HBM_SPEC] * (n + 1), out_specs=[HBM_SPEC] * (n + 1),
        out_shape=[jax.ShapeDtypeStruct(_full_shape(name), BF16) for name, _ in items]
        + [jax.ShapeDtypeStruct((N_CHIPS, r, 128), F32)],
        scratch_shapes=[pltpu.SemaphoreType.DMA((per * n + 4,)), pltpu.SemaphoreType.DMA((per * n + 4,))],
    )(*[shards[item] for item in items], small)
    return list(outs[:n]), outs[n]


SEM_SPEC = pl.BlockSpec(memory_space=pltpu.SEMAPHORE)
DATAFLOW = pltpu.SideEffectType.DATAFLOW_SIDE_EFFECTING
PER_ITEM = 8


def _split_start(name, copies, n_sems, sources, land_shapes, after):
    n, m = len(sources), len(land_shapes)

    def body(*refs):
        srcs, lands = refs[:n], refs[n:n + m]
        send_sems, recv_sems = refs[n + m + 1], refs[n + m + 2]
        token = refs[-1]
        for src, dst_there, _, s, peer in copies(srcs, lands):
            pltpu.make_async_remote_copy(src_ref=src, dst_ref=dst_there, send_sem=send_sems.at[s], recv_sem=recv_sems.at[s],
                                         device_id=peer, device_id_type=MESH).start()
        token[...] = jnp.zeros_like(token)

    src_arrays = [pltpu.with_memory_space_constraint(a, pltpu.HBM) for a in sources]
    land_arrays = [pltpu.with_memory_space_constraint(lax.empty(s.shape, s.dtype), pltpu.HBM) for s in land_shapes]
    hbm = pl.BlockSpec(memory_space=pltpu.HBM)
    outs = pl.pallas_call(
        body, name=name,
        in_specs=[hbm] * (n + m) + [HBM_SPEC],
        out_specs=[SEM_SPEC, SEM_SPEC] + [hbm] * (n + m) + [pl.BlockSpec(memory_space=pltpu.VMEM)],
        out_shape=[pltpu.SemaphoreType.DMA((n_sems,)), pltpu.SemaphoreType.DMA((n_sems,))]
        + [pltpu.HBM(a.shape, a.dtype) for a in src_arrays + land_arrays] + [jax.ShapeDtypeStruct((8, 128), F32)],
        input_output_aliases={i: 2 + i for i in range(n + m)},
        compiler_params=pltpu.CompilerParams(has_side_effects=DATAFLOW),
    )(*src_arrays, *land_arrays, after)
    return (outs[0], outs[1], list(outs[2:2 + n]), list(outs[2 + n:2 + n + m])), outs[-1]


def _split_wait(name, copies, state, after):
    send_sems, recv_sems, srcs_thru, lands_thru = state
    n, m = len(srcs_thru), len(lands_thru)
    after = list(after) if isinstance(after, (list, tuple)) else [after]

    def body(*refs):
        srcs, lands = refs[:n], refs[n:n + m]
        send_sems, recv_sems = refs[n + m], refs[n + m + 1]
        for src, _, dst_here, s, peer in copies(srcs, lands):
            cp = pltpu.make_async_remote_copy(src_ref=src, dst_ref=dst_here, send_sem=send_sems.at[s], recv_sem=recv_sems.at[s],
                                              device_id=peer, device_id_type=MESH)
            cp.wait_send()
            cp.wait_recv()

    hbm = pl.BlockSpec(memory_space=pltpu.HBM)
    outs = pl.pallas_call(
        body, name=name,
        in_specs=[hbm] * (n + m) + [SEM_SPEC, SEM_SPEC] + [HBM_SPEC] * len(after),
        out_specs=[hbm] * (n + m),
        out_shape=[pltpu.HBM(a.shape, a.dtype) for a in srcs_thru + lands_thru],
        input_output_aliases={i: i for i in range(n + m)},
        compiler_params=pltpu.CompilerParams(has_side_effects=DATAFLOW),
    )(*srcs_thru, *lands_thru, send_sems, recv_sems, *after)
    return list(outs[:n]), list(outs[n:])


def _gather_copies(items):
    def copies(srcs, lands):
        x, y, c, myq = _place()
        out = []
        for t, item in enumerate(items):
            for h in (0, 1):
                src = _gather_src(item, srcs[t], h)
                for j, (cx, cy) in enumerate(_other_chips(x, y)):
                    out.append((src, _gather_dst(item, lands[t], myq, h), _gather_dst(item, lands[t], 2 * cx + cy, h),
                                PER_ITEM * t + 2 * j + h, (cx, cy, c)))
                out.append((src, _gather_dst(item, lands[t], myq, h), _gather_dst(item, lands[t], myq, h),
                            PER_ITEM * t + 6 + h, (x, y, 1 - c)))
        return out
    return copies


def _gather_start(items, shards, after):
    lands = [jax.ShapeDtypeStruct(_full_shape(name), BF16) for name, _ in items]
    return _split_start("gather_layer1_start", _gather_copies(items), PER_ITEM * len(items),
                        [shards[item] for item in items], lands, after)


def _gather_wait(items, state, after):
    return _split_wait("gather_layer1_wait", _gather_copies(items), state, after)[1]


def _small_all_reduce(v):
    r = v.shape[0]

    def body(v_ref, o_ref, buf_ref, send_sems, recv_sems):
        x, y, c, _ = _place()
        me = 4 * x + 2 * y + c
        buf_ref[me] = v_ref[...]
        copies = []
        for k in range(1, 8):
            fx, fy, fc = (k >> 2) & 1, (k >> 1) & 1, k & 1
            to = (x ^ fx, y ^ fy, c ^ fc)
            cp = pltpu.make_async_remote_copy(src_ref=v_ref, dst_ref=buf_ref.at[me], send_sem=send_sems.at[k - 1],
                                              recv_sem=recv_sems.at[k - 1], device_id=to, device_id_type=MESH)
            cp.start()
            copies.append(cp)
        for k in range(1, 8):
            fx, fy, fc = (k >> 2) & 1, (k >> 1) & 1, k & 1
            src_dev = 4 * (x ^ fx) + 2 * (y ^ fy) + (c ^ fc)
            pltpu.make_async_remote_copy(src_ref=v_ref, dst_ref=buf_ref.at[src_dev], send_sem=send_sems.at[k - 1],
                                         recv_sem=recv_sems.at[k - 1], device_id=(x, y, c), device_id_type=MESH).wait_recv()
        for cp in copies:
            cp.wait_send()
        tot = buf_ref[0]
        for i in range(1, 8):
            tot = tot + buf_ref[i]
        o_ref[...] = tot

    vm = pl.BlockSpec(memory_space=pltpu.VMEM)
    return pl.pallas_call(
        body, name="small_all_reduce", in_specs=[vm], out_specs=vm,
        out_shape=jax.ShapeDtypeStruct((r, 128), F32),
        scratch_shapes=[pltpu.VMEM((8, r, 128), F32), pltpu.SemaphoreType.DMA((7,)), pltpu.SemaphoreType.DMA((7,))],
    )(v)


def _grad_view(kind, g):
    if kind == "col":
        return g.reshape(2, g.shape[0] // 2, g.shape[1])
    return g.reshape(N_CHIPS, 2, g.shape[0] // (2 * N_CHIPS), g.shape[1])


def _half_of(kind, ref, h):
    return ref.at[h] if kind == "col" else ref.at[:, h]


def _half_shape(kind, view_shape):
    return view_shape[1:] if kind == "col" else (view_shape[0],) + view_shape[2:]


def _piece_of(kind, width, colblock, ref, q):
    if kind == "col":
        return ref.at[:, pl.ds(colblock(q) * width, width)]
    return ref.at[q]


def _piece_shape(kind, width, half_shape):
    return (half_shape[0], width) if kind == "col" else half_shape[1:]


def _pair_exchange(views, kinds, name):
    n = len(views)

    def body(*refs):
        ins, outs = refs[:n], refs[n:2 * n]
        send_sems, recv_sems = refs[2 * n:]
        x, y, c, _ = _place()
        cps = []
        for t in range(n):
            cp = pltpu.make_async_remote_copy(src_ref=_half_of(kinds[t], ins[t], 1 - c), dst_ref=outs[t],
                                              send_sem=send_sems.at[t], recv_sem=recv_sems.at[t],
                                              device_id=(x, y, 1 - c), device_id_type=MESH)
            cp.start()
            cps.append(cp)
        for cp in cps:
            cp.wait()

    return pl.pallas_call(
        body, name=name, in_specs=[HBM_SPEC] * n, out_specs=[HBM_SPEC] * n,
        out_shape=[jax.ShapeDtypeStruct(_half_shape(k, v.shape), v.dtype) for k, v in zip(kinds, views)],
        scratch_shapes=[pltpu.SemaphoreType.DMA((n,)), pltpu.SemaphoreType.DMA((n,))],
    )(*views)


def _pair_sum(kind, view, recv, c, name):
    hs = recv.shape
    N = hs[-1]
    rows = hs[-2]
    tr = _pick(rows, (512, 352, 128))
    tn = _pick(N, (1408, 1024, 512))

    def body(c_ref, p_ref, r_ref, s_ref):
        s_ref[...] = (p_ref[...] + r_ref[...]).astype(BF16)

    if kind == "col":
        grid = (rows // tr, N // tn)
        mine = pl.BlockSpec((None, tr, tn), lambda i, j, c_ref: (c_ref[0], i, j))
        blk = pl.BlockSpec((tr, tn), lambda i, j, c_ref: (i, j))
        sem = ("parallel", "parallel")
    else:
        grid = (N_CHIPS, rows // tr, N // tn)
        mine = pl.BlockSpec((None, None, tr, tn), lambda q, i, j, c_ref: (q, c_ref[0], i, j))
        blk = pl.BlockSpec((None, tr, tn), lambda q, i, j, c_ref: (q, i, j))
        sem = ("parallel", "parallel", "parallel")
    return pl.pallas_call(
        body, name=name,
        grid_spec=pltpu.PrefetchScalarGridSpec(num_scalar_prefetch=1, grid=grid, in_specs=[mine, blk], out_specs=blk),
        out_shape=jax.ShapeDtypeStruct(hs, BF16),
        compiler_params=_cparams(sem),
    )(c.reshape(1).astype(jnp.int32), view, recv)


def _chip_copies(kinds, widths, colblocks):
    def copies(srcs, lands):
        x, y, c, _ = _place()
        out = []
        for j, (cx, cy) in enumerate(_other_chips(x, y)):
            for t in range(len(kinds)):
                out.append((_piece_of(kinds[t], widths[t], colblocks[t], srcs[t], 2 * cx + cy), lands[t].at[j],
                            lands[t].at[j], 3 * t + j, (cx, cy, c)))
        return out
    return copies


def _chip_land_shapes(sums, kinds, widths):
    return [jax.ShapeDtypeStruct((3,) + _piece_shape(k, w, s.shape), BF16) for k, w, s in zip(kinds, widths, sums)]


def _chip_exchange(sums, kinds, widths, colblocks, name):
    n = len(sums)
    copies = _chip_copies(kinds, widths, colblocks)

    def body(*refs):
        send_sems, recv_sems = refs[2 * n:]
        cps = [pltpu.make_async_remote_copy(src_ref=src, dst_ref=dst, send_sem=send_sems.at[s], recv_sem=recv_sems.at[s],
                                            device_id=peer, device_id_type=MESH)
               for src, dst, _, s, peer in copies(refs[:n], refs[n:2 * n])]
        for cp in cps:
            cp.start()
        for cp in cps:
            cp.wait()

    return pl.pallas_call(
        body, name=name, in_specs=[HBM_SPEC] * n, out_specs=[HBM_SPEC] * n,
        out_shape=_chip_land_shapes(sums, kinds, widths),
        scratch_shapes=[pltpu.SemaphoreType.DMA((3 * n,)), pltpu.SemaphoreType.DMA((3 * n,))],
    )(*sums)


N_DIRECT = 7


def _direct_piece(kind, width, colblock, view_ref, q, h):
    if kind == "col":
        return view_ref.at[h, :, pl.ds(colblock(q) * width, width)]
    return view_ref.at[q, h]


def _direct_copies(kinds, widths, colblocks):
    def copies(srcs, lands):
        x, y, c, myq = _place()
        out = []
        for t in range(len(kinds)):
            def piece(q, h, t=t):
                return _direct_piece(kinds[t], widths[t], colblocks[t], srcs[t], q, h)
            for j, (cx, cy) in enumerate(_other_chips(x, y)):
                for h in (0, 1):
                    out.append((piece(2 * cx + cy, h), lands[t].at[2 * j + c], lands[t].at[2 * j + h],
                                10 * t + 3 * j + c + h, (cx, cy, h)))
            out.append((piece(myq, 1 - c), lands[t].at[6], lands[t].at[6], 10 * t + 9, (x, y, 1 - c)))
        return out
    return copies


def _chip_sum(kind, own_src, recv, block_idx, c, shard_shape, layer, into, name, direct=False):
    n_recv, rows, N = recv.shape
    tr = _pick(rows, (512, 352, 128))
    tn = _pick(N, (1408, 1024, 768, 512))
    ni, nj = rows // tr, N // tn

    def body(q_ref, s_ref, r_ref, *rest):
        o_ref = rest[-1]
        tot = s_ref[...].astype(F32)
        for k in range(n_recv):
            tot = tot + r_ref[k].astype(F32)
        o_ref[...] = tot

    if direct and kind == "col":
        own = pl.BlockSpec((None, tr, tn), lambda i, j, q_ref: (q_ref[1], i, q_ref[0] * nj + j))
    elif direct:
        own = pl.BlockSpec((None, None, tr, tn), lambda i, j, q_ref: (q_ref[0], q_ref[1], i, j))
    elif kind == "col":
        own = pl.BlockSpec((tr, tn), lambda i, j, q_ref: (i, q_ref[0] * nj + j))
    else:
        own = pl.BlockSpec((None, tr, tn), lambda i, j, q_ref: (q_ref[0], i, j))
    if len(shard_shape) == 3:
        lead = 0 if layer is None else layer
        out_spec = pl.BlockSpec((None, tr, tn), lambda i, j, q_ref: (lead, q_ref[1] * ni + i, j))
    else:
        out_spec = pl.BlockSpec((tr, tn), lambda i, j, q_ref: (q_ref[1] * ni + i, j))
    in_specs = [own, pl.BlockSpec((n_recv, tr, tn), lambda i, j, q_ref: (0, i, j))]
    s = own_src
    args = [jnp.stack([block_idx, c]).astype(jnp.int32), s, recv]
    aliases = {}
    if into is not None:
        in_specs.append(HBM_SPEC)
        args.append(into)
        aliases = {3: 0}
    return pl.pallas_call(
        body, name=name,
        grid_spec=pltpu.PrefetchScalarGridSpec(num_scalar_prefetch=1, grid=(ni, nj), in_specs=in_specs, out_specs=out_spec),
        out_shape=jax.ShapeDtypeStruct(shard_shape, F32), input_output_aliases=aliases,
        compiler_params=_cparams(("parallel", "parallel")),
    )(*args)


def _half_window(ref, h):
    rows = ref.shape[-2] // 2
    if ref.ndim == 3:
        return ref.at[:, pl.ds(h * rows, rows)]
    return ref.at[pl.ds(h * rows, rows)]


def _share_halves(grads, name):
    n = len(grads)

    def body(*refs):
        outs = refs[n:2 * n]
        send_sems, recv_sems = refs[2 * n:]
        x, y, c, _ = _place()
        cps = []
        for t in range(n):
            cp = pltpu.make_async_remote_copy(src_ref=_half_window(outs[t], c), dst_ref=_half_window(outs[t], c),
                                              send_sem=send_sems.at[t], recv_sem=recv_sems.at[t],
                                              device_id=(x, y, 1 - c), device_id_type=MESH)
            cp.start()
            cps.append(cp)
        for t in range(n):
            cps[t].wait_send()
            pltpu.make_async_remote_copy(src_ref=_half_window(outs[t], c), dst_ref=_half_window(outs[t], 1 - c),
                                         send_sem=send_sems.at[t], recv_sem=recv_sems.at[t],
                                         device_id=(x, y, 1 - c), device_id_type=MESH).wait_recv()

    return pl.pallas_call(
        body, name=name, in_specs=[HBM_SPEC] * n, out_specs=[HBM_SPEC] * n,
        out_shape=[jax.ShapeDtypeStruct(g.shape, F32) for g in grads],
        input_output_aliases={t: t for t in range(n)},
        scratch_shapes=[pltpu.SemaphoreType.DMA((n,)), pltpu.SemaphoreType.DMA((n,))],
    )(*grads)


def _adamw(w, g, m, v, name):
    R, W = w.shape
    tr = _pick(R, (512, 352, 256, 32))

    def body(w_ref, g_ref, m_ref, v_ref, d_ref, nm_ref, nv_ref):
        gv = g_ref[...]
        nm = ADAM_B1 * m_ref[...] + (1.0 - ADAM_B1) * gv
        nv = ADAM_B2 * v_ref[...] + (1.0 - ADAM_B2) * (gv * gv)
        m_hat = nm / (1.0 - ADAM_B1 ** ADAM_STEP)
        v_hat = nv / (1.0 - ADAM_B2 ** ADAM_STEP)
        d_ref[...] = -ADAM_LR * (m_hat / (jnp.sqrt(v_hat) + ADAM_EPS) + ADAM_WD * w_ref[...])
        nm_ref[...] = nm
        nv_ref[...] = nv

    blk = pl.BlockSpec((tr, W), lambda i: (i, 0))
    shp = jax.ShapeDtypeStruct((R, W), F32)
    return pl.pallas_call(
        body, name=name, grid=(R // tr,), in_specs=[blk] * 4, out_specs=[blk] * 3, out_shape=[shp] * 3,
        compiler_params=_cparams(("parallel",)),
    )(w, g, m, v)


SMALL_ROWS = 32


def _pack_small(ln_g, ln_b, sinks):
    rows = jnp.concatenate([ln_g.reshape(-1, 128), ln_b.reshape(-1, 128),
                            jnp.pad(sinks.reshape(1, -1), ((0, 0), (0, 128 - sinks.size)))], axis=0)
    return jnp.pad(rows, ((0, SMALL_ROWS - rows.shape[0]), (0, 0)))


def _unpack_small(s, ln_shape, sink_shape):
    n = ln_shape[0] * ln_shape[1] * ln_shape[2] // 128
    return s[:n].reshape(ln_shape), s[n:2 * n].reshape(ln_shape), s[2 * n, :sink_shape[1]].reshape(sink_shape)


def _ffn_fwd(xin, w_in, w_out, gain, bias, tag):
    u, h = _ffn_in(xin, w_in, "ffn_in_" + tag)
    y, yb, z = _mm_ln(h, w_out, xin, gain, bias, 0.5, "ffn_out_ln_" + tag)
    return y, yb, dict(u=u, h=h, z=z, xin=xin)


def _ffn_bwd(dy, saved, w_in, w_out, gain, xin_b, tag, dw_dtype=F32):
    dz, dzc, gg, gb = _ln_bwd(saved["z"], dy, gain, 0.5, "ln_bwd_" + tag)
    du = _ffn_bwd_h(dzc, w_out, saved["u"], "ffn_bwd_h_" + tag)
    d_w_out = _mm_tn(saved["h"], dzc, "ffn_dwout_" + tag, out_dtype=dw_dtype)
    d_w_in = _mm_tn(xin_b, du, "ffn_dwin_" + tag, out_dtype=dw_dtype)
    dx = _mm_nt(du, w_in, "ffn_dx_" + tag, add=dz, add_scale=ALPHA)
    return dx, d_w_in, d_w_out, gg, gb


def kernel(x, ffn1_w_in, ffn1_w_out, ffn2_w_in, ffn2_w_out, ln_g, ln_b, a_w_qkv, a_w_o, kv_w, b_w_q, b_sinks, b_w_o, loss_target, m_ffn1_w_in, m_ffn1_w_out, m_ffn2_w_in, m_ffn2_w_out, m_ln_g, m_ln_b, m_a_w_qkv, m_a_w_o, m_kv_w, m_b_w_q, m_b_sinks, m_b_w_o, v_ffn1_w_in, v_ffn1_w_out, v_ffn2_w_in, v_ffn2_w_out, v_ln_g, v_ln_b, v_a_w_qkv, v_a_w_o, v_kv_w, v_b_w_q, v_b_sinks, v_b_w_o):
    ws = dict(ffn1_w_in=ffn1_w_in, ffn1_w_out=ffn1_w_out, ffn2_w_in=ffn2_w_in, ffn2_w_out=ffn2_w_out, a_w_qkv=a_w_qkv,
              a_w_o=a_w_o, kv_w=kv_w, b_w_q=b_w_q, b_w_o=b_w_o)
    ms = dict(ffn1_w_in=m_ffn1_w_in, ffn1_w_out=m_ffn1_w_out, ffn2_w_in=m_ffn2_w_in, ffn2_w_out=m_ffn2_w_out,
              a_w_qkv=m_a_w_qkv, a_w_o=m_a_w_o, kv_w=m_kv_w, b_w_q=m_b_w_q, b_w_o=m_b_w_o)
    vs = dict(ffn1_w_in=v_ffn1_w_in, ffn1_w_out=v_ffn1_w_out, ffn2_w_in=v_ffn2_w_in, ffn2_w_out=v_ffn2_w_out,
              a_w_qkv=v_a_w_qkv, a_w_o=v_a_w_o, kv_w=v_kv_w, b_w_q=v_b_w_q, b_w_o=v_b_w_o)
    _, _, c_idx, myq = _place()
    xs = x[0]
    target = loss_target[0]

    shards = {(n, l): (ws[n] if l is None else ws[n][l]).astype(BF16) for n, l in LAYER0_ITEMS + LAYER1_ITEMS}

    def as_weights(items, arrays):
        return {n: (a.reshape(D_MODEL, a.shape[-1]) if a.ndim == 4 else a) for (n, _), a in zip(items, arrays)}

    full0, small = _all_gather(LAYER0_ITEMS, shards, _pack_small(ln_g, ln_b, b_sinks))
    gather_state, token = _gather_start(LAYER1_ITEMS, shards, small)

    def layer1_weights(after):
        return as_weights(LAYER1_ITEMS, _gather_wait(LAYER1_ITEMS, gather_state, after))

    n_ln = ln_g.size // 128
    lg = jnp.concatenate([small[q, :n_ln].reshape(DEPTH, 3, 1, -1) for q in range(N_CHIPS)], axis=-1)
    lb = jnp.concatenate([small[q, n_ln:2 * n_ln].reshape(DEPTH, 3, 1, -1) for q in range(N_CHIPS)], axis=-1)
    lg = lg + token[0, 0]
    reducer = _GradReducer(c_idx, myq, {n: ws[n].shape for n in BIG})
    sq, grad_x, _, gg, gb, dsink_part = _local_step(xs, target, as_weights(LAYER0_ITEMS, full0), layer1_weights,
                                                    lg, lb, b_sinks.reshape(N_HEADS), reducer.begin)

    loss_row = jnp.pad(jnp.sum(sq).reshape(1, 1), ((0, 0), (0, 127)))
    dsinks = jnp.pad(dsink_part[:, 0, :].reshape(N_SLABS, 2, HEAD_DIM)[:, :, 0].reshape(1, N_HEADS), ((0, 0), (0, 128 - N_HEADS)))
    gg_full = jnp.stack([jnp.stack([jnp.sum(gg[i][j], axis=0) for j in range(3)]) for i in range(DEPTH)])
    gb_full = jnp.stack([jnp.stack([jnp.sum(gb[i][j], axis=0) for j in range(3)]) for i in range(DEPTH)])
    small_in = jnp.concatenate([loss_row, dsinks, gg_full.reshape(-1, 128), gb_full.reshape(-1, 128)], axis=0)
    small_in = jnp.pad(small_in, ((0, (-small_in.shape[0]) % 8), (0, 0)))
    small_sum = _small_all_reduce(small_in)
    loss = small_sum[0, 0] * (0.5 / D_MODEL)
    grad_sinks = small_sum[1, :N_HEADS].reshape(b_sinks.shape)
    n_full = DEPTH * 3 * D_MODEL // 128
    cols = D_MODEL // N_CHIPS
    grad_ln_g = lax.dynamic_slice_in_dim(small_sum[2:2 + n_full].reshape(DEPTH, 3, D_MODEL), myq * cols, cols, axis=2)
    grad_ln_b = lax.dynamic_slice_in_dim(small_sum[2 + n_full:2 + 2 * n_full].reshape(DEPTH, 3, D_MODEL), myq * cols, cols, axis=2)
    return _update(reducer, grad_x, loss, grad_ln_g, grad_ln_b, grad_sinks, ws, ms, vs,
                   (ln_g, ln_b, b_sinks), (m_ln_g, m_ln_b, m_b_sinks), (v_ln_g, v_ln_b, v_b_sinks))


def _local_step(xs, target, W, layer1_weights, lg, lb, sinks, grads_ready=None):
    if grads_ready is None:
        grads_ready = lambda tag, grads, overlap: 0.0
    S = xs.shape[0]
    slopes = jnp.asarray(_alibi_slopes(N_HEADS))
    in1, out1, in2, out2 = [W["ffn1_w_in"]], [W["ffn1_w_out"]], [W["ffn2_w_in"]], [W["ffn2_w_out"]]

    y1, y1b, s1 = _ffn_fwd(xs, in1[0], out1[0], lg[0, 0], lb[0, 0], "a1")
    qkv_a = _mm_nn(y1b, W["a_w_qkv"], F32, "qkv_a", split=True)
    mix_a, o_a, lse_a = _attn_fwd(qkv_a, slopes, None, PATTERNS_A, "attn_a_fwd")
    y2, y2b, z2 = _mm_ln(mix_a, W["a_w_o"], y1, lg[0, 1], lb[0, 1], 1.0, "attn_a_out_ln")
    y3, y3b, s3 = _ffn_fwd(y2, in2[0], out2[0], lg[0, 2], lb[0, 2], "a2")
    kv_w_rep = jnp.broadcast_to(W["kv_w"].reshape(D_MODEL, 2, N_KV_B, 1, HEAD_DIM),
                                (D_MODEL, 2, N_KV_B, GROUP_B, HEAD_DIM)).reshape(D_MODEL, 2 * D_MODEL)
    kv_rep = _mm_nn(y3b, kv_w_rep, F32, "kv_proj", split=(1, 2))
    W = dict(W, **layer1_weights(kv_rep))
    in1, out1, in2, out2 = (in1 + [W["ffn1_w_in"]], out1 + [W["ffn1_w_out"]], in2 + [W["ffn2_w_in"]],
                            out2 + [W["ffn2_w_out"]])
    y4, y4b, s4 = _ffn_fwd(y3, in1[1], out1[1], lg[1, 0], lb[1, 0], "b1")
    qkv_b = _mm_nn(y4b, W["b_w_q"], F32, "q_b", split=(0, 1), into=kv_rep)
    mix_b, o_b, lse_b = _attn_fwd(qkv_b, slopes, sinks, PATTERNS_B, "attn_b_fwd")
    y5, y5b, z5 = _mm_ln(mix_b, W["b_w_o"], y4, lg[1, 1], lb[1, 1], 1.0, "attn_b_out_ln")
    y6, _, s6 = _ffn_fwd(y5, in2[1], out2[1], lg[1, 2], lb[1, 2], "b2")

    dy6, sq = _loss_grad(y6, target, "loss_grad")
    gr = {n: None for n in BIG}
    gg = [[None] * 3 for _ in range(DEPTH)]
    gb = [[None] * 3 for _ in range(DEPTH)]

    dy5, d_in2_b, d_out2_b, gg[1][2], gb[1][2] = _ffn_bwd(dy6, s6, in2[1], out2[1], lg[1, 2], y5b, "b2", BF16)
    dz5, dz5b, gg[1][1], gb[1][1] = _ln_bwd(z5, dy5, lg[1, 1], 1.0, "ln_bwd_attn_b")
    gr["b_w_o"] = _mm_tn(mix_b, dz5b, "d_b_w_o", out_dtype=BF16)
    dmix_b = _mm_nt(dz5b, W["b_w_o"], "d_mix_b")
    dqkv_b, dsink_part = _attn_bwd(qkv_b, dmix_b, o_b, lse_b, slopes, sinks, PATTERNS_B, "attn_b_bwd")
    dq_b = (dqkv_b, 0)
    gr["b_w_q"] = _mm_tn(y4b, dq_b, "d_b_w_q", out_dtype=BF16)
    dy4 = _mm_nt(dq_b, W["b_w_q"], "d_y4", add=dz5, add_scale=ALPHA)
    dy3, d_in1_b, d_out1_b, gg[1][0], gb[1][0] = _ffn_bwd(dy4, s4, in1[1], out1[1], lg[1, 0], y3b, "b1", BF16)
    d_kv_w_rep = _mm_tn(y3b, dqkv_b, "d_kv_w", split=(1, 2))
    gr["kv_w"] = d_kv_w_rep.reshape(D_MODEL, 2, N_KV_B, GROUP_B, HEAD_DIM).sum(axis=3).reshape(D_MODEL, -1).astype(BF16)
    dy3 = _mm_nt(dqkv_b, kv_w_rep, "d_y3_kv", add=dy3, add_scale=1.0, split=(1, 2))
    tok = grads_ready("l1", {("ffn2_w_in", 1): d_in2_b, ("ffn2_w_out", 1): d_out2_b, ("b_w_o", None): gr["b_w_o"],
                             ("b_w_q", None): gr["b_w_q"], ("ffn1_w_in", 1): d_in1_b, ("ffn1_w_out", 1): d_out1_b,
                             ("kv_w", None): gr["kv_w"]}, True)
    lg0 = lg[0] + tok

    dy2, d_in2_a, d_out2_a, gg[0][2], gb[0][2] = _ffn_bwd(dy3, s3, in2[0], out2[0], lg0[2], y2b, "a2", BF16)
    tok = grads_ready("a2", {("ffn2_w_in", 0): d_in2_a, ("ffn2_w_out", 0): d_out2_a}, True)
    lg0 = lg0 + tok
    dz2, dz2b, gg[0][1], gb[0][1] = _ln_bwd(z2, dy2, lg0[1], 1.0, "ln_bwd_attn_a")
    gr["a_w_o"] = _mm_tn(mix_a, dz2b, "d_a_w_o", out_dtype=BF16)
    dmix_a = _mm_nt(dz2b, W["a_w_o"], "d_mix_a")
    dqkv_a, _ = _attn_bwd(qkv_a, dmix_a, o_a, lse_a, slopes, None, PATTERNS_A, "attn_a_bwd")
    gr["a_w_qkv"] = _mm_tn(y1b, dqkv_a, "d_a_w_qkv", split=True, out_dtype=BF16)
    tok = grads_ready("mix", {("a_w_o", None): gr["a_w_o"], ("a_w_qkv", None): gr["a_w_qkv"]}, True)
    lg0 = lg0 + tok
    dy1 = _mm_nt(dqkv_a, W["a_w_qkv"], "d_y1", add=dz2, add_scale=ALPHA, split=True)
    grad_x, d_in1_a, d_out1_a, gg[0][0], gb[0][0] = _ffn_bwd(dy1, s1, in1[0], out1[0], lg0[0], xs, "a1", BF16)
    grads_ready("a1", {("ffn1_w_in", 0): d_in1_a, ("ffn1_w_out", 0): d_out1_a}, True)
    gr["ffn1_w_in"] = [d_in1_a, d_in1_b]
    gr["ffn1_w_out"] = [d_out1_a, d_out1_b]
    gr["ffn2_w_in"] = [d_in2_a, d_in2_b]
    gr["ffn2_w_out"] = [d_out2_a, d_out2_b]
    return sq, grad_x, gr, gg, gb, dsink_part


def _grad_item(name, layer, g):
    if name.endswith("w_in"):
        return (g, "col", HALF_FF, _slot, name, layer)
    if name.endswith("w_out"):
        return (g, "row", D_MODEL, None, name, layer)
    if name == "a_w_qkv":
        return (g, "col", QKV_SHARD, lambda q: q, name, None)
    return (g, "row", g.shape[1], None, name, None)


class _GradReducer:
    def __init__(self, c_idx, myq, shard_shapes):
        self.c_idx, self.myq, self.shard_shapes = c_idx, myq, shard_shapes
        self.groups = []

    def begin(self, tag, grads, overlap):
        items = [_grad_item(n, l, g) for (n, l), g in grads.items()]
        kinds, widths, colblocks = [it[1] for it in items], [it[2] for it in items], [it[3] for it in items]
        views = [_grad_view(k, it[0]) for k, it in zip(kinds, items)]
        if overlap:
            lands = [jax.ShapeDtypeStruct((N_DIRECT,) + _piece_shape(k, w, _half_shape(k, v.shape)), BF16)
                     for k, w, v in zip(kinds, widths, views)]
            state, token = _split_start("grad_direct_start_" + tag, _direct_copies(kinds, widths, colblocks), 10 * len(items),
                                        views, lands, views[-1])
            self.groups.append((tag, items, None, state, token))
            return token[0, 0]
        from_sibling = _pair_exchange(views, kinds, "grad_pair_exchange_" + tag)
        sums = [_pair_sum(k, v, r, self.c_idx, "pair_sum_%s_%d" % (tag, t))
                for t, (k, v, r) in enumerate(zip(kinds, views, from_sibling))]
        self.groups.append((tag, items, sums, None, None))
        return 0.0

    def _sum_group(self, tag, items, sums, received, direct):
        for t, (it, s, r) in enumerate(zip(items, sums, received)):
            _, k, _, cb, name, layer = it
            own = cb(self.myq) if k == "col" else self.myq
            self.half_done[name] = _chip_sum(k, s, r, own, self.c_idx, self.shard_shapes[name], layer,
                                             self.half_done.get(name), "chip_sum_%s_%d" % (tag, t), direct=direct)

    def finish_first(self, after):
        self.half_done, self.late, early = {}, [], []
        started = [after]
        for g, (tag, items, sums, state, token) in enumerate(self.groups):
            kinds, widths, colblocks = [it[1] for it in items], [it[2] for it in items], [it[3] for it in items]
            if state is None:
                copies = _chip_copies(kinds, widths, colblocks)
                state, token = _split_start("grad_chip_start_" + tag, copies, 3 * len(items), sums,
                                            _chip_land_shapes(sums, kinds, widths), sums[-1])
                self.late.append((tag, items, copies, state, False))
                started.append(token)
            elif g == len(self.groups) - 1:
                self.late.append((tag, items, _direct_copies(kinds, widths, colblocks), state, True))
                started.append(token)
            else:
                early.append((tag, items, _direct_copies(kinds, widths, colblocks), state))
        for tag, items, copies, state in early:
            views, received = _split_wait("grad_direct_wait_" + tag, copies, state, started)
            self._sum_group(tag, items, views, received, True)
        late_names = {it[4] for _, items, _, _, _ in self.late for it in items}
        names = [n for n in BIG if n not in late_names]
        return dict(zip(names, _share_halves([self.half_done[n] for n in names], "grad_share_halves_first")))

    def finish_rest(self, after):
        names = []
        for tag, items, copies, state, direct in self.late:
            sums, received = _split_wait("grad_late_wait_" + tag, copies, state, after)
            self._sum_group(tag, items, sums, received, direct)
            names += [it[4] for it in items if it[4] not in names]
        return dict(zip(names, _share_halves([self.half_done[n] for n in names], "grad_share_halves_rest")))


def _update(reducer, grad_x, loss, grad_ln_g, grad_ln_b, grad_sinks, ws, ms, vs, small_w, small_m, small_v):
    ln_g, ln_b, b_sinks = small_w
    m_ln_g, m_ln_b, m_b_sinks = small_m
    v_ln_g, v_ln_b, v_b_sinks = small_v

    deltas, new_m, new_v = {}, {}, {}

    def update(some):
        done = []
        for name in some:
            shp = ws[name].shape
            flat = lambda a: a.reshape(-1, shp[-1])
            d, nm, nv = _adamw(flat(ws[name]), flat(some[name]), flat(ms[name]), flat(vs[name]), "adamw_" + name)
            deltas[name], new_m[name], new_v[name] = d.reshape(shp), nm.reshape(shp), nv.reshape(shp)
            done.append(d)
        return done

    grads = reducer.finish_first(grad_x)
    rest = reducer.finish_rest(update(grads))
    update(rest)
    grads.update(rest)
    delta_s, nm_s, nv_s = _adamw(_pack_small(ln_g, ln_b, b_sinks), _pack_small(grad_ln_g, grad_ln_b, grad_sinks),
                                 _pack_small(m_ln_g, m_ln_b, m_b_sinks), _pack_small(v_ln_g, v_ln_b, v_b_sinks), "adamw_small")
    for d, blob in ((grads, None), (deltas, delta_s), (new_m, nm_s), (new_v, nv_s)):
        if blob is None:
            d["ln_g"], d["ln_b"], d["b_sinks"] = grad_ln_g, grad_ln_b, grad_sinks
        else:
            d["ln_g"], d["ln_b"], d["b_sinks"] = _unpack_small(blob, ln_g.shape, b_sinks.shape)

    order = ("ffn1_w_in", "ffn1_w_out", "ffn2_w_in", "ffn2_w_out", "ln_g", "ln_b", "a_w_qkv", "a_w_o", "kv_w", "b_w_q",
             "b_sinks", "b_w_o")
    outs = [loss, grad_x[None]]
    for d in (grads, deltas, new_m, new_v):
        outs += [d[n] for n in order]
    return tuple(outs)
```

```python
import numpy as np
import jax
import jax.numpy as jnp
from jax import lax
from jax.experimental import pallas as pl
from jax.experimental.pallas import tpu as pltpu

F32 = jnp.float32
BF16 = jnp.bfloat16

D_MODEL = 1024
D_FF = 2816
HALF_FF = D_FF // 2
HEAD_DIM = 64
N_HEADS = 16
N_KV_B = 4
GROUP_B = N_HEADS // N_KV_B
DEPTH = 2
ALPHA = (2.0 * DEPTH) ** 0.25
LN_EPS = 1e-5
BLOCK = 128
SLAB = 128
N_SLABS = D_MODEL // SLAB
PATTERNS_A = ((1, 128, 1.0), (4, 128, 4.0), (16, 128, 16.0))
PATTERNS_B = ((1, 127, 1.0),)
NEG = -1e30

ADAM_LR = 0.001
ADAM_B1 = 0.9
ADAM_B2 = 0.999
ADAM_EPS = 1e-08
ADAM_WD = 0.01
ADAM_STEP = 10

N_CHIPS = 4
VMEM_LIMIT = 56 * 1024 * 1024
MESH = pl.DeviceIdType.MESH


def _alibi_slopes(n):
    return np.array([2.0 ** (-8.0 * (h + 1) / n) for h in range(n)], dtype=np.float32)


def _cparams(sem=None, vmem=VMEM_LIMIT):
    return pltpu.CompilerParams(dimension_semantics=sem, vmem_limit_bytes=vmem)


_DIMS = {"nn": ((1,), (0,)), "nt": ((1,), (1,)), "tn": ((0,), (0,))}


def _unlead(x):
    if isinstance(x, tuple):
        return x[0], x[1], x[0].shape[1:]
    return x, None, x.shape


def _bspec(block, imap, lead=None):
    if lead is None:
        return pl.BlockSpec(block, imap)
    return pl.BlockSpec((None,) + tuple(block), lambda *g: (lead,) + tuple(imap(*g)))


def _matmul(a, b, mode, out_dtype, tm, tn, tk, name, add=None, add_scale=1.0, split=False, into=None):
    out_spec = pl.BlockSpec((tm, tn), lambda i, j, k: (i, j))
    base, count = (0, 3) if split is True else (split or (0, 0))
    if mode == "nn":
        a, al, (M, K) = _unlead(a)
        b, bl, (K2, N) = _unlead(b)
        a_spec = _bspec((tm, tk), lambda i, j, k: (i, k), al)
        b_spec = _bspec((tk, tn), lambda i, j, k: (k, j), bl)
        out_struct = jax.ShapeDtypeStruct((M, N), out_dtype)
        if split:
            assert tn == D_MODEL and N == count * tn
            out_spec = pl.BlockSpec((None, tm, tn), lambda i, j, k: (j + base, i, 0))
            out_struct = jax.ShapeDtypeStruct((3, M, tn), out_dtype)
    elif mode == "nt":
        b, bl, (N, K2) = _unlead(b)
        if split:
            assert tk == D_MODEL
            M, K = a.shape[1], count * a.shape[2]
            a_spec = pl.BlockSpec((None, tm, tk), lambda i, j, k: (k + base, i, 0))
        else:
            a, al, (M, K) = _unlead(a)
            a_spec = _bspec((tm, tk), lambda i, j, k: (i, k), al)
        b_spec = _bspec((tn, tk), lambda i, j, k: (j, k), bl)
        out_struct = jax.ShapeDtypeStruct((M, N), out_dtype)
    else:
        a, al, (K, M) = _unlead(a)
        if split:
            assert tn == D_MODEL
            K2, N = b.shape[1], count * b.shape[2]
            b_spec = pl.BlockSpec((None, tk, tn), lambda i, j, k: (j + base, k, 0))
        else:
            b, bl, (K2, N) = _unlead(b)
            b_spec = _bspec((tk, tn), lambda i, j, k: (k, j), bl)
        a_spec = _bspec((tk, tm), lambda i, j, k: (k, i), al)
        out_struct = jax.ShapeDtypeStruct((M, N), out_dtype)
    assert K == K2 and M % tm == 0 and N % tn == 0 and K % tk == 0, (a.shape, b.shape, mode, tm, tn, tk)
    nk = K // tk
    dims = (_DIMS[mode], ((), ()))
    has_add = add is not None

    narrow = out_dtype != F32
    assert not (narrow and has_add)

    def body(*refs):
        if into is not None:
            refs = refs[:2] + refs[3:]
        if has_add:
            a_ref, b_ref, add_ref, o_ref = refs
            acc_ref = o_ref
        elif narrow:
            a_ref, b_ref, o_ref, acc_ref = refs
        else:
            a_ref, b_ref, o_ref = refs
            acc_ref = o_ref
        k = pl.program_id(2)
        part = lax.dot_general(a_ref[...].astype(BF16), b_ref[...].astype(BF16), dims, preferred_element_type=F32)
        if has_add:
            @pl.when(k == 0)
            def _():
                acc_ref[...] = part + add_scale * add_ref[...]
        else:
            @pl.when(k == 0)
            def _():
                acc_ref[...] = part

        @pl.when(k > 0)
        def _():
            acc_ref[...] += part

        if narrow:
            @pl.when(k == nk - 1)
            def _():
                o_ref[...] = acc_ref[...].astype(out_dtype)

    in_specs = [a_spec, b_spec]
    args = [a, b]
    aliases = {}
    if into is not None:
        assert mode == "nn" and split and not has_add
        in_specs.append(pl.BlockSpec(memory_space=pl.ANY))
        args.append(into)
        aliases = {2: 0}
    if has_add:
        in_specs.append(pl.BlockSpec((tm, tn), lambda i, j, k: (i, j)))
        args.append(add)
    return pl.pallas_call(
        body, name=name, grid=(M // tm, N // tn, nk),
        in_specs=in_specs, out_specs=out_spec, out_shape=out_struct, input_output_aliases=aliases,
        scratch_shapes=[pltpu.VMEM((tm, tn), F32)] if narrow else [],
        compiler_params=_cparams(("parallel", "parallel", "arbitrary")),
    )(*args)


def _pick(n, cands):
    for c in cands:
        if n % c == 0:
            return c
    raise ValueError((n, cands))


def _mm_nn(a, b, out_dtype, name, split=False, into=None):
    M, K = _unlead(a)[2]
    N = _unlead(b)[2][1]
    return _matmul(a, b, "nn", out_dtype, _pick(M, (1024, 512, 256)), _pick(N, (1024, 512)), _pick(K, (1024, 512)), name,
                   split=split, into=into)


def _mm_nt(a, b, name, add=None, add_scale=1.0, split=False):
    M, K = (a.shape[1], D_MODEL) if split else _unlead(a)[2]
    N = _unlead(b)[2][0]
    return _matmul(a, b, "nt", F32, _pick(M, (1024, 512, 256)), _pick(N, (1024, 512)),
                   _pick(K, (2816, 1024, 512)), name, add=add, add_scale=add_scale, split=split)


def _mm_tn(a, b, name, split=False, out_dtype=F32):
    K, M = _unlead(a)[2]
    N = D_MODEL if split else _unlead(b)[2][1]
    return _matmul(a, b, "tn", out_dtype, _pick(M, (1024, 1408, 512)), _pick(N, (1408, 1024, 512)),
                   _pick(K, (2048, 1024, 512, 256)), name, split=split)


def _ffn_in(x, w, name):
    S = x.shape[0]
    tm = _pick(S, (512, 256))
    w, wl, _ = _unlead(w)

    def body(x_ref, w_ref, t_ref, h_ref):
        acc = jnp.dot(x_ref[...].astype(BF16), w_ref[...], preferred_element_type=F32)
        g = acc[:, :HALF_FF]
        up = acc[:, HALF_FF:]
        sg = jax.nn.sigmoid(g)
        silu = g * sg
        t_ref[:, :HALF_FF] = (up * (sg * (1.0 + g * (1.0 - sg)))).astype(BF16)
        t_ref[:, HALF_FF:] = silu.astype(BF16)
        h_ref[...] = (silu * up).astype(BF16)

    return pl.pallas_call(
        body, name=name, grid=(2, S // tm),
        in_specs=[pl.BlockSpec((tm, D_MODEL), lambda j, i: (i, 0)),
                  _bspec((D_MODEL, D_FF), lambda j, i: (0, j), wl)],
        out_specs=[pl.BlockSpec((tm, D_FF), lambda j, i: (i, j)),
                   pl.BlockSpec((tm, HALF_FF), lambda j, i: (i, j))],
        out_shape=[jax.ShapeDtypeStruct((S, 2 * D_FF), BF16), jax.ShapeDtypeStruct((S, D_FF), BF16)],
        compiler_params=_cparams(("parallel", "parallel")),
    )(x, w)


def _ffn_bwd_h(dzc, w_out, u, name):
    S = dzc.shape[0]
    tm = _pick(S, (512, 256))
    w_out, wl, _ = _unlead(w_out)

    def body(dz_ref, w_ref, t_ref, du_ref):
        dh = lax.dot_general(dz_ref[...], w_ref[...], (((1,), (1,)), ((), ())), preferred_element_type=F32)
        du_ref[:, :HALF_FF] = (dh * t_ref[:, :HALF_FF].astype(F32)).astype(BF16)
        du_ref[:, HALF_FF:] = (dh * t_ref[:, HALF_FF:].astype(F32)).astype(BF16)

    return pl.pallas_call(
        body, name=name, grid=(2, S // tm),
        in_specs=[pl.BlockSpec((tm, D_MODEL), lambda j, i: (i, 0)),
                  _bspec((HALF_FF, D_MODEL), lambda j, i: (j, 0), wl),
                  pl.BlockSpec((tm, D_FF), lambda j, i: (i, j))],
        out_specs=pl.BlockSpec((tm, D_FF), lambda j, i: (i, j)),
        out_shape=jax.ShapeDtypeStruct((S, 2 * D_FF), BF16),
        compiler_params=_cparams(("parallel", "parallel")),
    )(dzc, w_out, u)


def _mm_ln(a, w, resid, gain, bias, c, name):
    S, K = a.shape
    tm = _pick(S, (512, 256))
    w, wl, _ = _unlead(w)

    def body(a_ref, w_ref, r_ref, g_ref, b_ref, y_ref, yb_ref, z_ref):
        z = ALPHA * r_ref[...] + c * jnp.dot(a_ref[...], w_ref[...], preferred_element_type=F32)
        mu = jnp.mean(z, axis=-1, keepdims=True)
        zc = z - mu
        var = jnp.mean(zc * zc, axis=-1, keepdims=True)
        y = zc * lax.rsqrt(var + LN_EPS) * g_ref[...] + b_ref[...]
        z_ref[...] = z
        y_ref[...] = y
        yb_ref[...] = y.astype(BF16)

    row = pl.BlockSpec((tm, D_MODEL), lambda i: (i, 0))
    vec = pl.BlockSpec((1, D_MODEL), lambda i: (0, 0))
    return pl.pallas_call(
        body, name=name, grid=(S // tm,),
        in_specs=[pl.BlockSpec((tm, K), lambda i: (i, 0)), _bspec((K, D_MODEL), lambda i: (0, 0), wl), row, vec, vec],
        out_specs=[row, row, row],
        out_shape=[jax.ShapeDtypeStruct((S, D_MODEL), F32), jax.ShapeDtypeStruct((S, D_MODEL), BF16),
                   jax.ShapeDtypeStruct((S, D_MODEL), F32)],
        compiler_params=_cparams(("parallel",)),
    )(a, w, resid, gain, bias)


def _ln_bwd(z, dy, gain, c, name):
    S = z.shape[0]
    tm = _pick(S, (512, 256))

    def body(z_ref, dy_ref, g_ref, dz_ref, dzc_ref, gg_ref, gb_ref):
        i = pl.program_id(0)
        zv = z_ref[...]
        dyv = dy_ref[...]
        mu = jnp.mean(zv, axis=-1, keepdims=True)
        zc = zv - mu
        var = jnp.mean(zc * zc, axis=-1, keepdims=True)
        rstd = lax.rsqrt(var + LN_EPS)
        xhat = zc * rstd
        dyg = dyv * g_ref[...]
        m1 = jnp.mean(dyg, axis=-1, keepdims=True)
        m2 = jnp.mean(dyg * xhat, axis=-1, keepdims=True)
        dz = rstd * (dyg - m1 - xhat * m2)
        dz_ref[...] = dz
        dzc_ref[...] = (c * dz).astype(BF16)
        pg = jnp.sum((dyv * xhat).reshape(tm // 8, 8, D_MODEL), axis=0)
        pb = jnp.sum(dyv.reshape(tm // 8, 8, D_MODEL), axis=0)

        @pl.when(i == 0)
        def _():
            gg_ref[...] = pg
            gb_ref[...] = pb

        @pl.when(i > 0)
        def _():
            gg_ref[...] += pg
            gb_ref[...] += pb

    row = pl.BlockSpec((tm, D_MODEL), lambda i: (i, 0))
    part = pl.BlockSpec((8, D_MODEL), lambda i: (0, 0))
    return pl.pallas_call(
        body, name=name, grid=(S // tm,),
        in_specs=[row, row, pl.BlockSpec((1, D_MODEL), lambda i: (0, 0))],
        out_specs=[row, row, part, part],
        out_shape=[jax.ShapeDtypeStruct((S, D_MODEL), F32), jax.ShapeDtypeStruct((S, D_MODEL), BF16),
                   jax.ShapeDtypeStruct((8, D_MODEL), F32), jax.ShapeDtypeStruct((8, D_MODEL), F32)],
        compiler_params=_cparams(("arbitrary",)),
    )(z, dy, gain)


def _loss_grad(y, t, name):
    S = y.shape[0]
    tm = _pick(S, (512, 256))

    def body(y_ref, t_ref, dy_ref, sq_ref):
        i = pl.program_id(0)
        e = y_ref[...] - t_ref[...]
        dy_ref[...] = e * (1.0 / D_MODEL)
        ps = jnp.sum((e * e).reshape(tm // 8, 8, D_MODEL), axis=0)

        @pl.when(i == 0)
        def _():
            sq_ref[...] = ps

        @pl.when(i > 0)
        def _():
            sq_ref[...] += ps

    row = pl.BlockSpec((tm, D_MODEL), lambda i: (i, 0))
    return pl.pallas_call(
        body, name=name, grid=(S // tm,),
        in_specs=[row, row], out_specs=[row, pl.BlockSpec((8, D_MODEL), lambda i: (0, 0))],
        out_shape=[jax.ShapeDtypeStruct((S, D_MODEL), F32), jax.ShapeDtypeStruct((8, D_MODEL), F32)],
        compiler_params=_cparams(("arbitrary",)),
    )(y, t)


def _rows(start, d):
    if d == 1:
        return pl.ds(pl.multiple_of(start, BLOCK), BLOCK)
    return pl.ds(start, BLOCK, stride=d)


def _ld(ref, start, d):
    return ref[_rows(start, d), :]


def _ld3(ref, lead, start, d):
    return ref[lead, _rows(start, d), :]


def _st3(ref, lead, start, d, val):
    ref[lead, _rows(start, d), :] = val


def _acc3(ref, lead, start, d, val):
    ref[lead, _rows(start, d), :] = ref[lead, _rows(start, d), :] + val


def _band_consts(slope0, slope1, maxd, scale):
    row = lax.broadcasted_iota(jnp.int32, (2 * BLOCK, 2 * BLOCK), 0)
    kj = lax.broadcasted_iota(jnp.int32, (2 * BLOCK, 2 * BLOCK), 1)
    top = row < BLOCK
    dist = BLOCK + jnp.where(top, row, row - BLOCK) - kj
    slope = jnp.where(top, slope0, slope1)
    base = jnp.where((dist >= 0) & (dist <= maxd), -(slope * (dist.astype(F32) * scale)), NEG)
    return base, kj < BLOCK


def _stack_heads(x, lo):
    return jnp.concatenate([jnp.where(lo, x, 0.0), jnp.where(lo, 0.0, x)], axis=0)


def _unstack_heads(x2, lo):
    return jnp.where(lo, x2[:BLOCK], x2[BLOCK:])


def _scores(q2, k2, base, prev_keys, first):
    s = lax.dot_general(q2, k2, (((1,), (1,)), ((), ())), preferred_element_type=F32) * (HEAD_DIM ** -0.5) + base
    return jnp.where(jnp.logical_and(prev_keys, first), NEG, s)


def _softmax_weights(ls):
    mx = ls[0]
    for l in ls[1:]:
        mx = jnp.maximum(mx, l)
    es = [jnp.exp(l - mx) for l in ls]
    tot = es[0]
    for e in es[1:]:
        tot = tot + e
    inv = 1.0 / tot
    return [e * inv for e in es]


def _attn_fwd(qkv, slopes, sinks, patterns, name):
    S = qkv.shape[1]
    npat = len(patterns)
    has_sink = sinks is not None
    if not has_sink:
        sinks = jnp.zeros((N_HEADS,), F32)
    rows_c = 256

    def body(slopes_ref, sinks_ref, x_ref, mix_ref, o_ref, lse_ref, o_scr, lse_scr):
        p = pl.program_id(0)
        lo = lax.broadcasted_iota(jnp.int32, (BLOCK, SLAB), 1) < HEAD_DIM
        top1 = lax.broadcasted_iota(jnp.int32, (2 * BLOCK, 1), 0) < BLOCK
        sk2 = jnp.where(top1, sinks_ref[2 * p], sinks_ref[2 * p + 1])
        for pi, (d, maxd, scale) in enumerate(patterns):
            nb = S // d // BLOCK
            base, prev_keys = _band_consts(slopes_ref[2 * p], slopes_ref[2 * p + 1], maxd, scale)

            def blk(t, carry, pi=pi, d=d, nb=nb, base=base, prev_keys=prev_keys):
                r = t // nb
                n = t - r * nb
                start = r + (d * BLOCK) * n
                prev = jnp.where(n > 0, start - d * BLOCK, start)
                q2 = _stack_heads(_ld3(x_ref, 0, start, d), lo).astype(BF16)
                k2 = jnp.concatenate([_ld3(x_ref, 1, prev, d), _ld3(x_ref, 1, start, d)], axis=0).astype(BF16)
                v2 = jnp.concatenate([_ld3(x_ref, 2, prev, d), _ld3(x_ref, 2, start, d)], axis=0).astype(BF16)
                s = _scores(q2, k2, base, prev_keys, n == 0)
                m = jnp.max(s, axis=-1, keepdims=True)
                if has_sink:
                    m = jnp.maximum(m, sk2)
                e = jnp.exp(s - m)
                den = jnp.sum(e, axis=-1, keepdims=True)
                if has_sink:
                    den = den + jnp.exp(sk2 - m)
                o2 = jnp.dot((e / den).astype(BF16), v2, preferred_element_type=F32)
                _st3(o_scr, pi, start, d, _unstack_heads(o2, lo))
                _st3(lse_scr, pi, start, d, _unstack_heads(m + jnp.log(den), lo))
                return carry

            lax.fori_loop(0, d * nb, blk, 0, unroll=8)

        lane_c = lax.broadcasted_iota(jnp.int32, (rows_c, SLAB), 1)

        def comb(ci, carry):
            rows = pl.ds(pl.multiple_of(ci * rows_c, rows_c), rows_c)
            ls = [lse_scr[i, rows, :] for i in range(npat)]
            packed = jnp.zeros((rows_c, SLAB), F32)
            for i in range(npat):
                o_ref[i, rows, :] = o_scr[i, rows, :].astype(BF16)
                packed = jnp.where(lane_c % HEAD_DIM == i, ls[i], packed)
            lse_ref[rows, :] = packed
            if npat == 1:
                mix_ref[rows, :] = o_scr[0, rows, :].astype(BF16)
            else:
                ws = _softmax_weights(ls)
                acc = ws[0] * o_scr[0, rows, :]
                for i in range(1, npat):
                    acc = acc + ws[i] * o_scr[i, rows, :]
                mix_ref[rows, :] = acc.astype(BF16)
            return carry

        lax.fori_loop(0, S // rows_c, comb, 0, unroll=2)

    smem = pl.BlockSpec(memory_space=pltpu.SMEM)
    return pl.pallas_call(
        body, name=name, grid=(N_SLABS,),
        in_specs=[smem, smem, pl.BlockSpec((3, S, SLAB), lambda p: (0, 0, p))],
        out_specs=[pl.BlockSpec((S, SLAB), lambda p: (0, p)), pl.BlockSpec((npat, S, SLAB), lambda p: (0, 0, p)),
                   pl.BlockSpec((None, S, SLAB), lambda p: (p, 0, 0))],
        out_shape=[jax.ShapeDtypeStruct((S, D_MODEL), BF16), jax.ShapeDtypeStruct((npat, S, D_MODEL), BF16),
                   jax.ShapeDtypeStruct((N_SLABS, S, SLAB), F32)],
        scratch_shapes=[pltpu.VMEM((npat, S, SLAB), F32), pltpu.VMEM((npat, S, SLAB), F32)],
        compiler_params=_cparams(("arbitrary",)),
    )(slopes, sinks, qkv)


def _attn_bwd(qkv, dout, o, lse, slopes, sinks, patterns, name):
    S = qkv.shape[1]
    npat = len(patterns)
    has_sink = sinks is not None
    if not has_sink:
        sinks = jnp.zeros((N_HEADS,), F32)
    rows_c = 256

    def headsum(x, lo):
        same = (lax.broadcasted_iota(jnp.int32, (SLAB, SLAB), 0) < HEAD_DIM) == (lax.broadcasted_iota(jnp.int32, (SLAB, SLAB), 1) < HEAD_DIM)
        return jnp.dot(x, same.astype(F32), precision=lax.Precision.HIGH, preferred_element_type=F32)

    def body(slopes_ref, sinks_ref, x_ref, do_ref, o_ref, lsep_ref, dxo_ref, dsink_ref, dbar_ref, sacc_ref, lse_ref, dx_ref):
        p = pl.program_id(0)
        lo = lax.broadcasted_iota(jnp.int32, (BLOCK, SLAB), 1) < HEAD_DIM
        lo_c = lax.broadcasted_iota(jnp.int32, (rows_c, SLAB), 1) < HEAD_DIM
        top1 = lax.broadcasted_iota(jnp.int32, (2 * BLOCK, 1), 0) < BLOCK
        sk2 = jnp.where(top1, sinks_ref[2 * p], sinks_ref[2 * p + 1])

        def prep(ci, carry):
            rows = pl.ds(pl.multiple_of(ci * rows_c, rows_c), rows_c)
            dov = do_ref[rows, :]
            dx_ref[:, rows, :] = jnp.zeros((3, rows_c, SLAB), F32)
            packed = lsep_ref[rows, :]
            ls = [jnp.where(lo_c, packed[:, i:i + 1], packed[:, HEAD_DIM + i:HEAD_DIM + i + 1]) for i in range(npat)]
            for i in range(npat):
                lse_ref[i, rows, :] = ls[i]
            if npat == 1:
                dbar_ref[rows, :] = headsum(dov * o_ref[0, rows, :].astype(F32), lo_c)
            else:
                ws = _softmax_weights(ls)
                acc = ws[0] * headsum(dov * o_ref[0, rows, :].astype(F32), lo_c)
                for i in range(1, npat):
                    acc = acc + ws[i] * headsum(dov * o_ref[i, rows, :].astype(F32), lo_c)
                dbar_ref[rows, :] = acc
            return carry

        lax.fori_loop(0, S // rows_c, prep, 0, unroll=2)
        sacc_ref[...] = jnp.zeros((BLOCK, SLAB), F32)

        for pi, (d, maxd, scale) in enumerate(patterns):
            nb = S // d // BLOCK
            base, prev_keys = _band_consts(slopes_ref[2 * p], slopes_ref[2 * p + 1], maxd, scale)

            def blk(t, carry, pi=pi, d=d, nb=nb, base=base, prev_keys=prev_keys):
                r = t // nb
                n = t - r * nb
                start = r + (d * BLOCK) * n
                prev = jnp.where(n > 0, start - d * BLOCK, start)
                q2 = _stack_heads(_ld3(x_ref, 0, start, d), lo).astype(BF16)
                k2 = jnp.concatenate([_ld3(x_ref, 1, prev, d), _ld3(x_ref, 1, start, d)], axis=0).astype(BF16)
                v2 = jnp.concatenate([_ld3(x_ref, 2, prev, d), _ld3(x_ref, 2, start, d)], axis=0).astype(BF16)
                ls = [_ld3(lse_ref, i, start, d) for i in range(npat)]
                w = _softmax_weights(ls)[pi] if npat > 1 else 1.0
                do2 = _stack_heads(w * _ld(do_ref, start, d), lo).astype(BF16)
                dl = w * _ld(dbar_ref, start, d)
                lse2 = jnp.concatenate([ls[pi][:, :1], ls[pi][:, HEAD_DIM:HEAD_DIM + 1]], axis=0)
                dl2 = jnp.concatenate([dl[:, :1], dl[:, HEAD_DIM:HEAD_DIM + 1]], axis=0)
                s = _scores(q2, k2, base, prev_keys, n == 0)
                pr = jnp.exp(s - lse2)
                dp = lax.dot_general(do2, v2, (((1,), (1,)), ((), ())), preferred_element_type=F32)
                ds = (pr * (dp - dl2) * (HEAD_DIM ** -0.5)).astype(BF16)
                dq2 = jnp.dot(ds, k2, preferred_element_type=F32)
                dk2 = lax.dot_general(ds, q2, (((0,), (0,)), ((), ())), preferred_element_type=F32)
                dv2 = lax.dot_general(pr.astype(BF16), do2, (((0,), (0,)), ((), ())), preferred_element_type=F32)
                _acc3(dx_ref, 0, start, d, _unstack_heads(dq2, lo))
                _acc3(dx_ref, 1, prev, d, dk2[:BLOCK])
                _acc3(dx_ref, 1, start, d, dk2[BLOCK:])
                _acc3(dx_ref, 2, prev, d, dv2[:BLOCK])
                _acc3(dx_ref, 2, start, d, dv2[BLOCK:])
                if has_sink:
                    sacc_ref[...] += _unstack_heads(-jnp.exp(sk2 - lse2) * dl2, lo)
                return carry

            lax.fori_loop(0, d * nb, blk, 0, unroll=8)

        dsink_ref[...] = jnp.broadcast_to(jnp.sum(sacc_ref[...], axis=0, keepdims=True), (8, SLAB))

        def emit(ci, carry):
            rows = pl.ds(pl.multiple_of(ci * rows_c, rows_c), rows_c)
            dxo_ref[:, rows, :] = dx_ref[:, rows, :].astype(BF16)
            return carry

        lax.fori_loop(0, S // rows_c, emit, 0, unroll=2)

    smem = pl.BlockSpec(memory_space=pltpu.SMEM)
    return pl.pallas_call(
        body, name=name, grid=(N_SLABS,),
        in_specs=[smem, smem, pl.BlockSpec((3, S, SLAB), lambda p: (0, 0, p)), pl.BlockSpec((S, SLAB), lambda p: (0, p)),
                  pl.BlockSpec((npat, S, SLAB), lambda p: (0, 0, p)), pl.BlockSpec((None, S, SLAB), lambda p: (p, 0, 0))],
        out_specs=[pl.BlockSpec((3, S, SLAB), lambda p: (0, 0, p)), pl.BlockSpec((None, 8, SLAB), lambda p: (p, 0, 0))],
        out_shape=[jax.ShapeDtypeStruct((3, S, D_MODEL), BF16), jax.ShapeDtypeStruct((N_SLABS, 8, SLAB), F32)],
        scratch_shapes=[pltpu.VMEM((S, SLAB), F32), pltpu.VMEM((BLOCK, SLAB), F32), pltpu.VMEM((npat, S, SLAB), F32),
                        pltpu.VMEM((3, S, SLAB), F32)],
        compiler_params=_cparams(("arbitrary",)),
    )(slopes, sinks, qkv, dout, o, lse)


def _place():
    x, y, c = lax.axis_index("x"), lax.axis_index("y"), lax.axis_index("c")
    return x, y, c, 2 * x + y


def _other_chips(x, y):
    return [(1 - x, y), (x, 1 - y), (1 - x, 1 - y)]


HBM_SPEC = pl.BlockSpec(memory_space=pl.ANY)


def _slot(q):
    return 2 * (q % 2) + q // 2


BIG = ("ffn1_w_in", "ffn1_w_out", "ffn2_w_in", "ffn2_w_out", "a_w_qkv", "a_w_o", "kv_w", "b_w_q", "b_w_o")
QKV_SHARD = 3 * D_MODEL // N_CHIPS
ROW_SHARD = D_MODEL // N_CHIPS


LAYER0_ITEMS = (("ffn1_w_in", 0), ("ffn1_w_out", 0), ("a_w_qkv", None), ("a_w_o", None), ("ffn2_w_in", 0),
                ("ffn2_w_out", 0), ("kv_w", None))
LAYER1_ITEMS = (("ffn1_w_in", 1), ("ffn1_w_out", 1), ("b_w_q", None), ("b_w_o", None), ("ffn2_w_in", 1),
                ("ffn2_w_out", 1))
OUT_SHARD = D_FF // N_CHIPS


def _full_shape(name):
    if name.endswith("w_in"):
        return (D_MODEL, 2 * D_FF)
    if name.endswith("w_out"):
        return (D_FF, D_MODEL)
    if name == "a_w_qkv":
        return (D_MODEL, 3 * D_MODEL)
    if name == "kv_w":
        return (N_CHIPS, 2, ROW_SHARD // 2, 2 * N_KV_B * HEAD_DIM)
    return (N_CHIPS, 2, ROW_SHARD // 2, D_MODEL)


def _gather_src(item, ref, c):
    name, _ = item
    if name.endswith("w_in"):
        return ref.at[pl.ds(c * (D_MODEL // 2), D_MODEL // 2)]
    if name.endswith("w_out"):
        return ref.at[pl.ds(c * (OUT_SHARD // 2), OUT_SHARD // 2)]
    if name == "a_w_qkv":
        return ref.at[0, pl.ds(c * (D_MODEL // 2), D_MODEL // 2)]
    if name == "kv_w":
        return ref.at[pl.ds(c * (ROW_SHARD // 2), ROW_SHARD // 2)]
    return ref.at[0, pl.ds(c * (ROW_SHARD // 2), ROW_SHARD // 2)]


def _gather_dst(item, ref, q, c):
    name, _ = item
    if name.endswith("w_in"):
        return ref.at[pl.ds(c * (D_MODEL // 2), D_MODEL // 2), pl.ds(_slot(q) * HALF_FF, HALF_FF)]
    if name.endswith("w_out"):
        return ref.at[pl.ds(q * OUT_SHARD + c * (OUT_SHARD // 2), OUT_SHARD // 2)]
    if name == "a_w_qkv":
        return ref.at[pl.ds(c * (D_MODEL // 2), D_MODEL // 2), pl.ds(q * QKV_SHARD, QKV_SHARD)]
    return ref.at[q, c]


def _all_gather(items, shards, small):
    n = len(items)
    r = small.shape[0]
    per = 8

    def body(*refs):
        srcs, small_ref = refs[:n], refs[n]
        dsts, s_ref = refs[n + 1:2 * n + 1], refs[2 * n + 1]
        send_sems, recv_sems = refs[2 * n + 2:]
        x, y, c, myq = _place()
        sibling = (x, y, 1 - c)
        chips = _other_chips(x, y)

        def big(t, k, src, q, h, to):
            return pltpu.make_async_remote_copy(src_ref=src, dst_ref=_gather_dst(items[t], dsts[t], q, h),
                                                send_sem=send_sems.at[per * t + k], recv_sem=recv_sems.at[per * t + k],
                                                device_id=to, device_id_type=MESH)

        def tiny(k, q, to):
            return pltpu.make_async_remote_copy(src_ref=small_ref, dst_ref=s_ref.at[q], send_sem=send_sems.at[per * n + k],
                                                recv_sem=recv_sems.at[per * n + k], device_id=to, device_id_type=MESH)

        first = []
        for j, chip in enumerate(chips):
            if j < 2:
                first += [big(t, j, _gather_src(items[t], srcs[t], c), myq, c, (*chip, c)) for t in range(n)]
            first.append(tiny(j, myq, (*chip, c)))
        own = [big(t, 6 + h, _gather_src(items[t], srcs[t], h), myq, h, sibling) for t in range(n) for h in (0, 1)]
        own.append(tiny(3, myq, sibling))
        for cp in first + own:
            cp.start()
        relay_from = ((x + 1 - c) % 2, (y + c) % 2)
        relay_to = ((x + c) % 2, (y + 1 - c) % 2, c)
        q_relay = 2 * relay_from[0] + relay_from[1]
        passed = []
        for t in range(n):
            src = _gather_src(items[t], srcs[t], c)
            for j, (cx, cy) in enumerate(chips[:2]):
                q = 2 * cx + cy
                big(t, j, src, q, c, sibling).wait_recv()
                fwd = big(t, 3 + j, _gather_dst(items[t], dsts[t], q, c), q, c, sibling)
                fwd.start()
                passed.append(fwd)
            relay = big(t, 2, _gather_dst(items[t], dsts[t], q_relay, c), q_relay, c, relay_to)
            relay.start()
            passed.append(relay)
        q_diag = 2 * chips[2][0] + chips[2][1]
        for t in range(n):
            big(t, 2, _gather_src(items[t], srcs[t], c), q_diag, c, sibling).wait_recv()
            fwd = big(t, 5, _gather_dst(items[t], dsts[t], q_diag, c), q_diag, c, sibling)
            fwd.start()
            passed.append(fwd)
        for j, (cx, cy) in enumerate(chips):
            q = 2 * cx + cy
            for t in range(n):
                big(t, 3 + j, _gather_src(items[t], srcs[t], c), q, 1 - c, sibling).wait_recv()
            tiny(j, q, sibling).wait_recv()
        for cp in own:
            cp.wait_recv()
        for cp in first + passed + own:
            cp.wait_send()

    outs = pl.pallas_call(
        body, name="all_gather_layer0",
        in_specs=[HBM_SPEC] * (n + 1), out_specs=[HBM_SPEC] * (n + 1),
        out_shape=[jax.ShapeDtypeStruct(_full_shape(name), BF16) for name, _ in items]
        + [jax.ShapeDtypeStruct((N_CHIPS, r, 128), F32)],
        scratch_shapes=[pltpu.SemaphoreType.DMA((per * n + 4,)), pltpu.SemaphoreType.DMA((per * n + 4,))],
    )(*[shards[item] for item in items], small)
    return list(outs[:n]), outs[n]


SEM_SPEC = pl.BlockSpec(memory_space=pltpu.SEMAPHORE)
DATAFLOW = pltpu.SideEffectType.DATAFLOW_SIDE_EFFECTING
PER_ITEM = 8


def _split_start(name, copies, n_sems, sources, land_shapes, after):
    n, m = len(sources), len(land_shapes)

    def body(*refs):
        srcs, lands = refs[:n], refs[n:n + m]
        send_sems, recv_sems = refs[n + m + 1], refs[n + m + 2]
        token = refs[-1]
        for src, dst_there, _, s, peer in copies(srcs, lands):
            pltpu.make_async_remote_copy(src_ref=src, dst_ref=dst_there, send_sem=send_sems.at[s], recv_sem=recv_sems.at[s],
                                         device_id=peer, device_id_type=MESH).start()
        token[...] = jnp.zeros_like(token)

    src_arrays = [pltpu.with_memory_space_constraint(a, pltpu.HBM) for a in sources]
    land_arrays = [pltpu.with_memory_space_constraint(lax.empty(s.shape, s.dtype), pltpu.HBM) for s in land_shapes]
    hbm = pl.BlockSpec(memory_space=pltpu.HBM)
    outs = pl.pallas_call(
        body, name=name,
        in_specs=[hbm] * (n + m) + [HBM_SPEC],
        out_specs=[SEM_SPEC, SEM_SPEC] + [hbm] * (n + m) + [pl.BlockSpec(memory_space=pltpu.VMEM)],
        out_shape=[pltpu.SemaphoreType.DMA((n_sems,)), pltpu.SemaphoreType.DMA((n_sems,))]
        + [pltpu.HBM(a.shape, a.dtype) for a in src_arrays + land_arrays] + [jax.ShapeDtypeStruct((8, 128), F32)],
        input_output_aliases={i: 2 + i for i in range(n + m)},
        compiler_params=pltpu.CompilerParams(has_side_effects=DATAFLOW),
    )(*src_arrays, *land_arrays, after)
    return (outs[0], outs[1], list(outs[2:2 + n]), list(outs[2 + n:2 + n + m])), outs[-1]


def _split_wait(name, copies, state, after):
    send_sems, recv_sems, srcs_thru, lands_thru = state
    n, m = len(srcs_thru), len(lands_thru)
    after = list(after) if isinstance(after, (list, tuple)) else [after]

    def body(*refs):
        srcs, lands = refs[:n], refs[n:n + m]
        send_sems, recv_sems = refs[n + m], refs[n + m + 1]
        for src, _, dst_here, s, peer in copies(srcs, lands):
            cp = pltpu.make_async_remote_copy(src_ref=src, dst_ref=dst_here, send_sem=send_sems.at[s], recv_sem=recv_sems.at[s],
                                              device_id=peer, device_id_type=MESH)
            cp.wait_send()
            cp.wait_recv()

    hbm = pl.BlockSpec(memory_space=pltpu.HBM)
    outs = pl.pallas_call(
        body, name=name,
        in_specs=[hbm] * (n + m) + [SEM_SPEC, SEM_SPEC] + [HBM_SPEC] * len(after),
        out_specs=[hbm] * (n + m),
        out_shape=[pltpu.HBM(a.shape, a.dtype) for a in srcs_thru + lands_thru],
        input_output_aliases={i: i for i in range(n + m)},
        compiler_params=pltpu.CompilerParams(has_side_effects=DATAFLOW),
    )(*srcs_thru, *lands_thru, send_sems, recv_sems, *after)
    return list(outs[:n]), list(outs[n:])


def _gather_copies(items):
    def copies(srcs, lands):
        x, y, c, myq = _place()
        out = []
        for t, item in enumerate(items):
            for h in (0, 1):
                src = _gather_src(item, srcs[t], h)
                for j, (cx, cy) in enumerate(_other_chips(x, y)):
                    out.append((src, _gather_dst(item, lands[t], myq, h), _gather_dst(item, lands[t], 2 * cx + cy, h),
                                PER_ITEM * t + 2 * j + h, (cx, cy, c)))
                out.append((src, _gather_dst(item, lands[t], myq, h), _gather_dst(item, lands[t], myq, h),
                            PER_ITEM * t + 6 + h, (x, y, 1 - c)))
        return out
    return copies


def _gather_start(items, shards, after):
    lands = [jax.ShapeDtypeStruct(_full_shape(name), BF16) for name, _ in items]
    return _split_start("gather_layer1_start", _gather_copies(items), PER_ITEM * len(items),
                        [shards[item] for item in items], lands, after)


def _gather_wait(items, state, after):
    return _split_wait("gather_layer1_wait", _gather_copies(items), state, after)[1]


def _small_all_reduce(v):
    r = v.shape[0]

    def body(v_ref, o_ref, buf_ref, send_sems, recv_sems):
        x, y, c, _ = _place()
        me = 4 * x + 2 * y + c
        buf_ref[me] = v_ref[...]
        copies = []
        for k in range(1, 8):
            fx, fy, fc = (k >> 2) & 1, (k >> 1) & 1, k & 1
            to = (x ^ fx, y ^ fy, c ^ fc)
            cp = pltpu.make_async_remote_copy(src_ref=v_ref, dst_ref=buf_ref.at[me], send_sem=send_sems.at[k - 1],
                                              recv_sem=recv_sems.at[k - 1], device_id=to, device_id_type=MESH)
            cp.start()
            copies.append(cp)
        for k in range(1, 8):
            fx, fy, fc = (k >> 2) & 1, (k >> 1) & 1, k & 1
            src_dev = 4 * (x ^ fx) + 2 * (y ^ fy) + (c ^ fc)
            pltpu.make_async_remote_copy(src_ref=v_ref, dst_ref=buf_ref.at[src_dev], send_sem=send_sems.at[k - 1],
                                         recv_sem=recv_sems.at[k - 1], device_id=(x, y, c), device_id_type=MESH).wait_recv()
        for cp in copies:
            cp.wait_send()
        tot = buf_ref[0]
        for i in range(1, 8):
            tot = tot + buf_ref[i]
        o_ref[...] = tot

    vm = pl.BlockSpec(memory_space=pltpu.VMEM)
    return pl.pallas_call(
        body, name="small_all_reduce", in_specs=[vm], out_specs=vm,
        out_shape=jax.ShapeDtypeStruct((r, 128), F32),
        scratch_shapes=[pltpu.VMEM((8, r, 128), F32), pltpu.SemaphoreType.DMA((7,)), pltpu.SemaphoreType.DMA((7,))],
    )(v)


def _grad_view(kind, g):
    if kind == "col":
        return g.reshape(2, g.shape[0] // 2, g.shape[1])
    return g.reshape(N_CHIPS, 2, g.shape[0] // (2 * N_CHIPS), g.shape[1])


def _half_of(kind, ref, h):
    return ref.at[h] if kind == "col" else ref.at[:, h]


def _half_shape(kind, view_shape):
    return view_shape[1:] if kind == "col" else (view_shape[0],) + view_shape[2:]


def _piece_of(kind, width, colblock, ref, q):
    if kind == "col":
        return ref.at[:, pl.ds(colblock(q) * width, width)]
    return ref.at[q]


def _piece_shape(kind, width, half_shape):
    return (half_shape[0], width) if kind == "col" else half_shape[1:]


def _pair_exchange(views, kinds, name):
    n = len(views)

    def body(*refs):
        ins, outs = refs[:n], refs[n:2 * n]
        send_sems, recv_sems = refs[2 * n:]
        x, y, c, _ = _place()
        cps = []
        for t in range(n):
            cp = pltpu.make_async_remote_copy(src_ref=_half_of(kinds[t], ins[t], 1 - c), dst_ref=outs[t],
                                              send_sem=send_sems.at[t], recv_sem=recv_sems.at[t],
                                              device_id=(x, y, 1 - c), device_id_type=MESH)
            cp.start()
            cps.append(cp)
        for cp in cps:
            cp.wait()

    return pl.pallas_call(
        body, name=name, in_specs=[HBM_SPEC] * n, out_specs=[HBM_SPEC] * n,
        out_shape=[jax.ShapeDtypeStruct(_half_shape(k, v.shape), v.dtype) for k, v in zip(kinds, views)],
        scratch_shapes=[pltpu.SemaphoreType.DMA((n,)), pltpu.SemaphoreType.DMA((n,))],
    )(*views)


def _pair_sum(kind, view, recv, c, name):
    hs = recv.shape
    N = hs[-1]
    rows = hs[-2]
    tr = _pick(rows, (512, 352, 128))
    tn = _pick(N, (1408, 1024, 512))

    def body(c_ref, p_ref, r_ref, s_ref):
        s_ref[...] = (p_ref[...] + r_ref[...]).astype(BF16)

    if kind == "col":
        grid = (rows // tr, N // tn)
        mine = pl.BlockSpec((None, tr, tn), lambda i, j, c_ref: (c_ref[0], i, j))
        blk = pl.BlockSpec((tr, tn), lambda i, j, c_ref: (i, j))
        sem = ("parallel", "parallel")
    else:
        grid = (N_CHIPS, rows // tr, N // tn)
        mine = pl.BlockSpec((None, None, tr, tn), lambda q, i, j, c_ref: (q, c_ref[0], i, j))
        blk = pl.BlockSpec((None, tr, tn), lambda q, i, j, c_ref: (q, i, j))
        sem = ("parallel", "parallel", "parallel")
    return pl.pallas_call(
        body, name=name,
        grid_spec=pltpu.PrefetchScalarGridSpec(num_scalar_prefetch=1, grid=grid, in_specs=[mine, blk], out_specs=blk),
        out_shape=jax.ShapeDtypeStruct(hs, BF16),
        compiler_params=_cparams(sem),
    )(c.reshape(1).astype(jnp.int32), view, recv)


def _chip_copies(kinds, widths, colblocks):
    def copies(srcs, lands):
        x, y, c, _ = _place()
        out = []
        for j, (cx, cy) in enumerate(_other_chips(x, y)):
            for t in range(len(kinds)):
                out.append((_piece_of(kinds[t], widths[t], colblocks[t], srcs[t], 2 * cx + cy), lands[t].at[j],
                            lands[t].at[j], 3 * t + j, (cx, cy, c)))
        return out
    return copies


def _chip_land_shapes(sums, kinds, widths):
    return [jax.ShapeDtypeStruct((3,) + _piece_shape(k, w, s.shape), BF16) for k, w, s in zip(kinds, widths, sums)]


def _chip_exchange(sums, kinds, widths, colblocks, name):
    n = len(sums)
    copies = _chip_copies(kinds, widths, colblocks)

    def body(*refs):
        send_sems, recv_sems = refs[2 * n:]
        cps = [pltpu.make_async_remote_copy(src_ref=src, dst_ref=dst, send_sem=send_sems.at[s], recv_sem=recv_sems.at[s],
                                            device_id=peer, device_id_type=MESH)
               for src, dst, _, s, peer in copies(refs[:n], refs[n:2 * n])]
        for cp in cps:
            cp.start()
        for cp in cps:
            cp.wait()

    return pl.pallas_call(
        body, name=name, in_specs=[HBM_SPEC] * n, out_specs=[HBM_SPEC] * n,
        out_shape=_chip_land_shapes(sums, kinds, widths),
        scratch_shapes=[pltpu.SemaphoreType.DMA((3 * n,)), pltpu.SemaphoreType.DMA((3 * n,))],
    )(*sums)


N_DIRECT = 7


def _direct_piece(kind, width, colblock, view_ref, q, h):
    if kind == "col":
        return view_ref.at[h, :, pl.ds(colblock(q) * width, width)]
    return view_ref.at[q, h]


def _direct_copies(kinds, widths, colblocks):
    def copies(srcs, lands):
        x, y, c, myq = _place()
        out = []
        for t in range(len(kinds)):
            def piece(q, h, t=t):
                return _direct_piece(kinds[t], widths[t], colblocks[t], srcs[t], q, h)
            for j, (cx, cy) in enumerate(_other_chips(x, y)):
                for h in (0, 1):
                    out.append((piece(2 * cx + cy, h), lands[t].at[2 * j + c], lands[t].at[2 * j + h],
                                10 * t + 3 * j + c + h, (cx, cy, h)))
            out.append((piece(myq, 1 - c), lands[t].at[6], lands[t].at[6], 10 * t + 9, (x, y, 1 - c)))
        return out
    return copies


def _chip_sum(kind, own_src, recv, block_idx, c, shard_shape, layer, into, name, direct=False):
    n_recv, rows, N = recv.shape
    tr = _pick(rows, (512, 352, 128))
    tn = _pick(N, (1408, 1024, 768, 512))
    ni, nj = rows // tr, N // tn

    def body(q_ref, s_ref, r_ref, *rest):
        o_ref = rest[-1]
        tot = s_ref[...].astype(F32)
        for k in range(n_recv):
            tot = tot + r_ref[k].astype(F32)
        o_ref[...] = tot

    if direct and kind == "col":
        own = pl.BlockSpec((None, tr, tn), lambda i, j, q_ref: (q_ref[1], i, q_ref[0] * nj + j))
    elif direct:
        own = pl.BlockSpec((None, None, tr, tn), lambda i, j, q_ref: (q_ref[0], q_ref[1], i, j))
    elif kind == "col":
        own = pl.BlockSpec((tr, tn), lambda i, j, q_ref: (i, q_ref[0] * nj + j))
    else:
        own = pl.BlockSpec((None, tr, tn), lambda i, j, q_ref: (q_ref[0], i, j))
    if len(shard_shape) == 3:
        lead = 0 if layer is None else layer
        out_spec = pl.BlockSpec((None, tr, tn), lambda i, j, q_ref: (lead, q_ref[1] * ni + i, j))
    else:
        out_spec = pl.BlockSpec((tr, tn), lambda i, j, q_ref: (q_ref[1] * ni + i, j))
    in_specs = [own, pl.BlockSpec((n_recv, tr, tn), lambda i, j, q_ref: (0, i, j))]
    s = own_src
    args = [jnp.stack([block_idx, c]).astype(jnp.int32), s, recv]
    aliases = {}
    if into is not None:
        in_specs.append(HBM_SPEC)
        args.append(into)
        aliases = {3: 0}
    return pl.pallas_call(
        body, name=name,
        grid_spec=pltpu.PrefetchScalarGridSpec(num_scalar_prefetch=1, grid=(ni, nj), in_specs=in_specs, out_specs=out_spec),
        out_shape=jax.ShapeDtypeStruct(shard_shape, F32), input_output_aliases=aliases,
        compiler_params=_cparams(("parallel", "parallel")),
    )(*args)


def _half_window(ref, h):
    rows = ref.shape[-2] // 2
    if ref.ndim == 3:
        return ref.at[:, pl.ds(h * rows, rows)]
    return ref.at[pl.ds(h * rows, rows)]


def _share_halves(grads, name):
    n = len(grads)

    def body(*refs):
        outs = refs[n:2 * n]
        send_sems, recv_sems = refs[2 * n:]
        x, y, c, _ = _place()
        cps = []
        for t in range(n):
            cp = pltpu.make_async_remote_copy(src_ref=_half_window(outs[t], c), dst_ref=_half_window(outs[t], c),
                                              send_sem=send_sems.at[t], recv_sem=recv_sems.at[t],
                                              device_id=(x, y, 1 - c), device_id_type=MESH)
            cp.start()
            cps.append(cp)
        for t in range(n):
            cps[t].wait_send()
            pltpu.make_async_remote_copy(src_ref=_half_window(outs[t], c), dst_ref=_half_window(outs[t], 1 - c),
                                         send_sem=send_sems.at[t], recv_sem=recv_sems.at[t],
                                         device_id=(x, y, 1 - c), device_id_type=MESH).wait_recv()

    return pl.pallas_call(
        body, name=name, in_specs=[HBM_SPEC] * n, out_specs=[HBM_SPEC] * n,
        out_shape=[jax.ShapeDtypeStruct(g.shape, F32) for g in grads],
        input_output_aliases={t: t for t in range(n)},
        scratch_shapes=[pltpu.SemaphoreType.DMA((n,)), pltpu.SemaphoreType.DMA((n,))],
    )(*grads)


def _adamw(w, g, m, v, name):
    R, W = w.shape
    tr = _pick(R, (512, 352, 256, 32))

    def body(w_ref, g_ref, m_ref, v_ref, d_ref, nm_ref, nv_ref):
        gv = g_ref[...]
        nm = ADAM_B1 * m_ref[...] + (1.0 - ADAM_B1) * gv
        nv = ADAM_B2 * v_ref[...] + (1.0 - ADAM_B2) * (gv * gv)
        m_hat = nm / (1.0 - ADAM_B1 ** ADAM_STEP)
        v_hat = nv / (1.0 - ADAM_B2 ** ADAM_STEP)
        d_ref[...] = -ADAM_LR * (m_hat / (jnp.sqrt(v_hat) + ADAM_EPS) + ADAM_WD * w_ref[...])
        nm_ref[...] = nm
        nv_ref[...] = nv

    blk = pl.BlockSpec((tr, W), lambda i: (i, 0))
    shp = jax.ShapeDtypeStruct((R, W), F32)
    return pl.pallas_call(
        body, name=name, grid=(R // tr,), in_specs=[blk] * 4, out_specs=[blk] * 3, out_shape=[shp] * 3,
        compiler_params=_cparams(("parallel",)),
    )(w, g, m, v)


SMALL_ROWS = 32


def _pack_small(ln_g, ln_b, sinks):
    rows = jnp.concatenate([ln_g.reshape(-1, 128), ln_b.reshape(-1, 128),
                            jnp.pad(sinks.reshape(1, -1), ((0, 0), (0, 128 - sinks.size)))], axis=0)
    return jnp.pad(rows, ((0, SMALL_ROWS - rows.shape[0]), (0, 0)))


def _unpack_small(s, ln_shape, sink_shape):
    n = ln_shape[0] * ln_shape[1] * ln_shape[2] // 128
    return s[:n].reshape(ln_shape), s[n:2 * n].reshape(ln_shape), s[2 * n, :sink_shape[1]].reshape(sink_shape)


def _ffn_fwd(xin, w_in, w_out, gain, bias, tag):
    u, h = _ffn_in(xin, w_in, "ffn_in_" + tag)
    y, yb, z = _mm_ln(h, w_out, xin, gain, bias, 0.5, "ffn_out_ln_" + tag)
    return y, yb, dict(u=u, h=h, z=z, xin=xin)


def _ffn_bwd(dy, saved, w_in, w_out, gain, xin_b, tag, dw_dtype=F32):
    dz, dzc, gg, gb = _ln_bwd(saved["z"], dy, gain, 0.5, "ln_bwd_" + tag)
    du = _ffn_bwd_h(dzc, w_out, saved["u"], "ffn_bwd_h_" + tag)
    d_w_out = _mm_tn(saved["h"], dzc, "ffn_dwout_" + tag, out_dtype=dw_dtype)
    d_w_in = _mm_tn(xin_b, du, "ffn_dwin_" + tag, out_dtype=dw_dtype)
    dx = _mm_nt(du, w_in, "ffn_dx_" + tag, add=dz, add_scale=ALPHA)
    return dx, d_w_in, d_w_out, gg, gb


def kernel(x, ffn1_w_in, ffn1_w_out, ffn2_w_in, ffn2_w_out, ln_g, ln_b, a_w_qkv, a_w_o, kv_w, b_w_q, b_sinks, b_w_o, loss_target, m_ffn1_w_in, m_ffn1_w_out, m_ffn2_w_in, m_ffn2_w_out, m_ln_g, m_ln_b, m_a_w_qkv, m_a_w_o, m_kv_w, m_b_w_q, m_b_sinks, m_b_w_o, v_ffn1_w_in, v_ffn1_w_out, v_ffn2_w_in, v_ffn2_w_out, v_ln_g, v_ln_b, v_a_w_qkv, v_a_w_o, v_kv_w, v_b_w_q, v_b_sinks, v_b_w_o):
    ws = dict(ffn1_w_in=ffn1_w_in, ffn1_w_out=ffn1_w_out, ffn2_w_in=ffn2_w_in, ffn2_w_out=ffn2_w_out, a_w_qkv=a_w_qkv,
              a_w_o=a_w_o, kv_w=kv_w, b_w_q=b_w_q, b_w_o=b_w_o)
    ms = dict(ffn1_w_in=m_ffn1_w_in, ffn1_w_out=m_ffn1_w_out, ffn2_w_in=m_ffn2_w_in, ffn2_w_out=m_ffn2_w_out,
              a_w_qkv=m_a_w_qkv, a_w_o=m_a_w_o, kv_w=m_kv_w, b_w_q=m_b_w_q, b_w_o=m_b_w_o)
    vs = dict(ffn1_w_in=v_ffn1_w_in, ffn1_w_out=v_ffn1_w_out, ffn2_w_in=v_ffn2_w_in, ffn2_w_out=v_ffn2_w_out,
              a_w_qkv=v_a_w_qkv, a_w_o=v_a_w_o, kv_w=v_kv_w, b_w_q=v_b_w_q, b_w_o=v_b_w_o)
    _, _, c_idx, myq = _place()
    xs = x[0]
    target = loss_target[0]

    shards = {(n, l): (ws[n] if l is None else ws[n][l]).astype(BF16) for n, l in LAYER0_ITEMS + LAYER1_ITEMS}

    def as_weights(items, arrays):
        return {n: (a.reshape(D_MODEL, a.shape[-1]) if a.ndim == 4 else a) for (n, _), a in zip(items, arrays)}

    full0, small = _all_gather(LAYER0_ITEMS, shards, _pack_small(ln_g, ln_b, b_sinks))
    gather_state, token = _gather_start(LAYER1_ITEMS, shards, small)

    def layer1_weights(after):
        return as_weights(LAYER1_ITEMS, _gather_wait(LAYER1_ITEMS, gather_state, after))

    n_ln = ln_g.size // 128
    lg = jnp.concatenate([small[q, :n_ln].reshape(DEPTH, 3, 1, -1) for q in range(N_CHIPS)], axis=-1)
    lb = jnp.concatenate([small[q, n_ln:2 * n_ln].reshape(DEPTH, 3, 1, -1) for q in range(N_CHIPS)], axis=-1)
    lg = lg + token[0, 0]
    reducer = _GradReducer(c_idx, myq, {n: ws[n].shape for n in BIG})
    sq, grad_x, _, gg, gb, dsink_part = _local_step(xs, target, as_weights(LAYER0_ITEMS, full0), layer1_weights,
                                                    lg, lb, b_sinks.reshape(N_HEADS), reducer.begin)

    loss_row = jnp.pad(jnp.sum(sq).reshape(1, 1), ((0, 0), (0, 127)))
    dsinks = jnp.pad(dsink_part[:, 0, :].reshape(N_SLABS, 2, HEAD_DIM)[:, :, 0].reshape(1, N_HEADS), ((0, 0), (0, 128 - N_HEADS)))
    gg_full = jnp.stack([jnp.stack([jnp.sum(gg[i][j], axis=0) for j in range(3)]) for i in range(DEPTH)])
    gb_full = jnp.stack([jnp.stack([jnp.sum(gb[i][j], axis=0) for j in range(3)]) for i in range(DEPTH)])
    small_in = jnp.concatenate([loss_row, dsinks, gg_full.reshape(-1, 128), gb_full.reshape(-1, 128)], axis=0)
    small_in = jnp.pad(small_in, ((0, (-small_in.shape[0]) % 8), (0, 0)))
    small_sum = _small_all_reduce(small_in)
    loss = small_sum[0, 0] * (0.5 / D_MODEL)
    grad_sinks = small_sum[1, :N_HEADS].reshape(b_sinks.shape)
    n_full = DEPTH * 3 * D_MODEL // 128
    cols = D_MODEL // N_CHIPS
    grad_ln_g = lax.dynamic_slice_in_dim(small_sum[2:2 + n_full].reshape(DEPTH, 3, D_MODEL), myq * cols, cols, axis=2)
    grad_ln_b = lax.dynamic_slice_in_dim(small_sum[2 + n_full:2 + 2 * n_full].reshape(DEPTH, 3, D_MODEL), myq * cols, cols, axis=2)
    return _update(reducer, grad_x, loss, grad_ln_g, grad_ln_b, grad_sinks, ws, ms, vs,
                   (ln_g, ln_b, b_sinks), (m_ln_g, m_ln_b, m_b_sinks), (v_ln_g, v_ln_b, v_b_sinks))


def _local_step(xs, target, W, layer1_weights, lg, lb, sinks, grads_ready=None):
    if grads_ready is None:
        grads_ready = lambda tag, grads, overlap: 0.0
    S = xs.shape[0]
    slopes = jnp.asarray(_alibi_slopes(N_HEADS))
    in1, out1, in2, out2 = [W["ffn1_w_in"]], [W["ffn1_w_out"]], [W["ffn2_w_in"]], [W["ffn2_w_out"]]

    y1, y1b, s1 = _ffn_fwd(xs, in1[0], out1[0], lg[0, 0], lb[0, 0], "a1")
    qkv_a = _mm_nn(y1b, W["a_w_qkv"], F32, "qkv_a", split=True)
    mix_a, o_a, lse_a = _attn_fwd(qkv_a, slopes, None, PATTERNS_A, "attn_a_fwd")
    y2, y2b, z2 = _mm_ln(mix_a, W["a_w_o"], y1, lg[0, 1], lb[0, 1], 1.0, "attn_a_out_ln")
    y3, y3b, s3 = _ffn_fwd(y2, in2[0], out2[0], lg[0, 2], lb[0, 2], "a2")
    kv_w_rep = jnp.broadcast_to(W["kv_w"].reshape(D_MODEL, 2, N_KV_B, 1, HEAD_DIM),
                                (D_MODEL, 2, N_KV_B, GROUP_B, HEAD_DIM)).reshape(D_MODEL, 2 * D_MODEL)
    kv_rep = _mm_nn(y3b, kv_w_rep, F32, "kv_proj", split=(1, 2))
    W = dict(W, **layer1_weights(kv_rep))
    in1, out1, in2, out2 = (in1 + [W["ffn1_w_in"]], out1 + [W["ffn1_w_out"]], in2 + [W["ffn2_w_in"]],
                            out2 + [W["ffn2_w_out"]])
    y4, y4b, s4 = _ffn_fwd(y3, in1[1], out1[1], lg[1, 0], lb[1, 0], "b1")
    qkv_b = _mm_nn(y4b, W["b_w_q"], F32, "q_b", split=(0, 1), into=kv_rep)
    mix_b, o_b, lse_b = _attn_fwd(qkv_b, slopes, sinks, PATTERNS_B, "attn_b_fwd")
    y5, y5b, z5 = _mm_ln(mix_b, W["b_w_o"], y4, lg[1, 1], lb[1, 1], 1.0, "attn_b_out_ln")
    y6, _, s6 = _ffn_fwd(y5, in2[1], out2[1], lg[1, 2], lb[1, 2], "b2")

    dy6, sq = _loss_grad(y6, target, "loss_grad")
    gr = {n: None for n in BIG}
    gg = [[None] * 3 for _ in range(DEPTH)]
    gb = [[None] * 3 for _ in range(DEPTH)]

    dy5, d_in2_b, d_out2_b, gg[1][2], gb[1][2] = _ffn_bwd(dy6, s6, in2[1], out2[1], lg[1, 2], y5b, "b2", BF16)
    dz5, dz5b, gg[1][1], gb[1][1] = _ln_bwd(z5, dy5, lg[1, 1], 1.0, "ln_bwd_attn_b")
    gr["b_w_o"] = _mm_tn(mix_b, dz5b, "d_b_w_o", out_dtype=BF16)
    dmix_b = _mm_nt(dz5b, W["b_w_o"], "d_mix_b")
    dqkv_b, dsink_part = _attn_bwd(qkv_b, dmix_b, o_b, lse_b, slopes, sinks, PATTERNS_B, "attn_b_bwd")
    dq_b = (dqkv_b, 0)
    gr["b_w_q"] = _mm_tn(y4b, dq_b, "d_b_w_q", out_dtype=BF16)
    dy4 = _mm_nt(dq_b, W["b_w_q"], "d_y4", add=dz5, add_scale=ALPHA)
    dy3, d_in1_b, d_out1_b, gg[1][0], gb[1][0] = _ffn_bwd(dy4, s4, in1[1], out1[1], lg[1, 0], y3b, "b1", BF16)
    d_kv_w_rep = _mm_tn(y3b, dqkv_b, "d_kv_w", split=(1, 2))
    gr["kv_w"] = d_kv_w_rep.reshape(D_MODEL, 2, N_KV_B, GROUP_B, HEAD_DIM).sum(axis=3).reshape(D_MODEL, -1).astype(BF16)
    dy3 = _mm_nt(dqkv_b, kv_w_rep, "d_y3_kv", add=dy3, add_scale=1.0, split=(1, 2))
    tok = grads_ready("l1", {("ffn2_w_in", 1): d_in2_b, ("ffn2_w_out", 1): d_out2_b, ("b_w_o", None): gr["b_w_o"],
                             ("b_w_q", None): gr["b_w_q"], ("ffn1_w_in", 1): d_in1_b, ("ffn1_w_out", 1): d_out1_b,
                             ("kv_w", None): gr["kv_w"]}, True)
    lg0 = lg[0] + tok

    dy2, d_in2_a, d_out2_a, gg[0][2], gb[0][2] = _ffn_bwd(dy3, s3, in2[0], out2[0], lg0[2], y2b, "a2", BF16)
    tok = grads_ready("a2", {("ffn2_w_in", 0): d_in2_a, ("ffn2_w_out", 0): d_out2_a}, True)
    lg0 = lg0 + tok
    dz2, dz2b, gg[0][1], gb[0][1] = _ln_bwd(z2, dy2, lg0[1], 1.0, "ln_bwd_attn_a")
    gr["a_w_o"] = _mm_tn(mix_a, dz2b, "d_a_w_o", out_dtype=BF16)
    dmix_a = _mm_nt(dz2b, W["a_w_o"], "d_mix_a")
    dqkv_a, _ = _attn_bwd(qkv_a, dmix_a, o_a, lse_a, slopes, None, PATTERNS_A, "attn_a_bwd")
    gr["a_w_qkv"] = _mm_tn(y1b, dqkv_a, "d_a_w_qkv", split=True, out_dtype=BF16)
    tok = grads_ready("mix", {("a_w_o", None): gr["a_w_o"], ("a_w_qkv", None): gr["a_w_qkv"]}, True)
    lg0 = lg0 + tok
    dy1 = _mm_nt(dqkv_a, W["a_w_qkv"], "d_y1", add=dz2, add_scale=ALPHA, split=True)
    grad_x, d_in1_a, d_out1_a, gg[0][0], gb[0][0] = _ffn_bwd(dy1, s1, in1[0], out1[0], lg0[0], xs, "a1", BF16)
    grads_ready("a1", {("ffn1_w_in", 0): d_in1_a, ("ffn1_w_out", 0): d_out1_a}, True)
    gr["ffn1_w_in"] = [d_in1_a, d_in1_b]
    gr["ffn1_w_out"] = [d_out1_a, d_out1_b]
    gr["ffn2_w_in"] = [d_in2_a, d_in2_b]
    gr["ffn2_w_out"] = [d_out2_a, d_out2_b]
    return sq, grad_x, gr, gg, gb, dsink_part


def _grad_item(name, layer, g):
    if name.endswith("w_in"):
        return (g, "col", HALF_FF, _slot, name, layer)
    if name.endswith("w_out"):
        return (g, "row", D_MODEL, None, name, layer)
    if name == "a_w_qkv":
        return (g, "col", QKV_SHARD, lambda q: q, name, None)
    return (g, "row", g.shape[1], None, name, None)


class _GradReducer:
    def __init__(self, c_idx, myq, shard_shapes):
        self.c_idx, self.myq, self.shard_shapes = c_idx, myq, shard_shapes
        self.groups = []

    def begin(self, tag, grads, overlap):
        items = [_grad_item(n, l, g) for (n, l), g in grads.items()]
        kinds, widths, colblocks = [it[1] for it in items], [it[2] for it in items], [it[3] for it in items]
        views = [_grad_view(k, it[0]) for k, it in zip(kinds, items)]
        if overlap:
            lands = [jax.ShapeDtypeStruct((N_DIRECT,) + _piece_shape(k, w, _half_shape(k, v.shape)), BF16)
                     for k, w, v in zip(kinds, widths, views)]
            state, token = _split_start("grad_direct_start_" + tag, _direct_copies(kinds, widths, colblocks), 10 * len(items),
                                        views, lands, views[-1])
            self.groups.append((tag, items, None, state, token))
            return token[0, 0]
        from_sibling = _pair_exchange(views, kinds, "grad_pair_exchange_" + tag)
        sums = [_pair_sum(k, v, r, self.c_idx, "pair_sum_%s_%d" % (tag, t))
                for t, (k, v, r) in enumerate(zip(kinds, views, from_sibling))]
        self.groups.append((tag, items, sums, None, None))
        return 0.0

    def _sum_group(self, tag, items, sums, received, direct):
        for t, (it, s, r) in enumerate(zip(items, sums, received)):
            _, k, _, cb, name, layer = it
            own = cb(self.myq) if k == "col" else self.myq
            self.half_done[name] = _chip_sum(k, s, r, own, self.c_idx, self.shard_shapes[name], layer,
                                             self.half_done.get(name), "chip_sum_%s_%d" % (tag, t), direct=direct)

    def finish_first(self, after):
        self.half_done, self.late, early = {}, [], []
        started = [after]
        for g, (tag, items, sums, state, token) in enumerate(self.groups):
            kinds, widths, colblocks = [it[1] for it in items], [it[2] for it in items], [it[3] for it in items]
            if state is None:
                copies = _chip_copies(kinds, widths, colblocks)
                state, token = _split_start("grad_chip_start_" + tag, copies, 3 * len(items), sums,
                                            _chip_land_shapes(sums, kinds, widths), sums[-1])
                self.late.append((tag, items, copies, state, False))
                started.append(token)
            elif g == len(self.groups) - 1:
                self.late.append((tag, items, _direct_copies(kinds, widths, colblocks), state, True))
                started.append(token)
            else:
                early.append((tag, items, _direct_copies(kinds, widths, colblocks), state))
        for tag, items, copies, state in early:
            views, received = _split_wait("grad_direct_wait_" + tag, copies, state, started)
            self._sum_group(tag, items, views, received, True)
        late_names = {it[4] for _, items, _, _, _ in self.late for it in items}
        names = [n for n in BIG if n not in late_names]
        return dict(zip(names, _share_halves([self.half_done[n] for n in names], "grad_share_halves_first")))

    def finish_rest(self, after):
        names = []
        for tag, items, copies, state, direct in self.late:
            sums, received = _split_wait("grad_late_wait_" + tag, copies, state, after)
            self._sum_group(tag, items, sums, received, direct)
            names += [it[4] for it in items if it[4] not in names]
        return dict(zip(names, _share_halves([self.half_done[n] for n in names], "grad_share_halves_rest")))


def _update(reducer, grad_x, loss, grad_ln_g, grad_ln_b, grad_sinks, ws, ms, vs, small_w, small_m, small_v):
    ln_g, ln_b, b_sinks = small_w
    m_ln_g, m_ln_b, m_b_sinks = small_m
    v_ln_g, v_ln_b, v_b_sinks = small_v

    deltas, new_m, new_v = {}, {}, {}

    def update(some):
        done = []
        for name in some:
            shp = ws[name].shape
            flat = lambda a: a.reshape(-1, shp[-1])
            d, nm, nv = _adamw(flat(ws[name]), flat(some[name]), flat(ms[name]), flat(vs[name]), "adamw_" + name)
            deltas[name], new_m[name], new_v[name] = d.reshape(shp), nm.reshape(shp), nv.reshape(shp)
            done.append(d)
        return done

    grads = reducer.finish_first(grad_x)
    rest = reducer.finish_rest(update(grads))
    update(rest)
    grads.update(rest)
    delta_s, nm_s, nv_s = _adamw(_pack_small(ln_g, ln_b, b_sinks), _pack_small(grad_ln_g, grad_ln_b, grad_sinks),
                                 _pack_small(m_ln_g, m_ln_b, m_b_sinks), _pack_small(v_ln_g, v_ln_b, v_b_sinks), "adamw_small")
    for d, blob in ((grads, None), (deltas, delta_s), (new_m, nm_s), (new_v, nv_s)):
        if blob is None:
            d["ln_g"], d["ln_b"], d["b_sinks"] = grad_ln_g, grad_ln_b, grad_sinks
        else:
            d["ln_g"], d["ln_b"], d["b_sinks"] = _unpack_small(blob, ln_g.shape, b_sinks.shape)

    order = ("ffn1_w_in", "ffn1_w_out", "ffn2_w_in", "ffn2_w_out", "ln_g", "ln_b", "a_w_qkv", "a_w_o", "kv_w", "b_w_q",
             "b_sinks", "b_w_o")
    outs = [loss, grad_x[None]]
    for d in (grads, deltas, new_m, new_v):
        outs += [d[n] for n in order]
    return tuple(outs)
```

```python
import numpy as np
import jax
import jax.numpy as jnp
from jax import lax
from jax.experimental import pallas as pl
from jax.experimental.pallas import tpu as pltpu

F32 = jnp.float32
BF16 = jnp.bfloat16

D_MODEL = 1024
D_FF = 2816
HALF_FF = D_FF // 2
HEAD_DIM = 64
N_HEADS = 16
N_KV_B = 4
GROUP_B = N_HEADS // N_KV_B
DEPTH = 2
ALPHA = (2.0 * DEPTH) ** 0.25
LN_EPS = 1e-5
BLOCK = 128
SLAB = 128
N_SLABS = D_MODEL // SLAB
PATTERNS_A = ((1, 128, 1.0), (4, 128, 4.0), (16, 128, 16.0))
PATTERNS_B = ((1, 127, 1.0),)
NEG = -1e30

ADAM_LR = 0.001
ADAM_B1 = 0.9
ADAM_B2 = 0.999
ADAM_EPS = 1e-08
ADAM_WD = 0.01
ADAM_STEP = 10

N_CHIPS = 4
VMEM_LIMIT = 56 * 1024 * 1024
MESH = pl.DeviceIdType.MESH


def _alibi_slopes(n):
    return np.array([2.0 ** (-8.0 * (h + 1) / n) for h in range(n)], dtype=np.float32)


def _cparams(sem=None, vmem=VMEM_LIMIT):
    return pltpu.CompilerParams(dimension_semantics=sem, vmem_limit_bytes=vmem)


_DIMS = {"nn": ((1,), (0,)), "nt": ((1,), (1,)), "tn": ((0,), (0,))}


def _unlead(x):
    if isinstance(x, tuple):
        return x[0], x[1], x[0].shape[1:]
    return x, None, x.shape


def _bspec(block, imap, lead=None):
    if lead is None:
        return pl.BlockSpec(block, imap)
    return pl.BlockSpec((None,) + tuple(block), lambda *g: (lead,) + tuple(imap(*g)))


def _matmul(a, b, mode, out_dtype, tm, tn, tk, name, add=None, add_scale=1.0, split=False, into=None):
    out_spec = pl.BlockSpec((tm, tn), lambda i, j, k: (i, j))
    base, count = (0, 3) if split is True else (split or (0, 0))
    if mode == "nn":
        a, al, (M, K) = _unlead(a)
        b, bl, (K2, N) = _unlead(b)
        a_spec = _bspec((tm, tk), lambda i, j, k: (i, k), al)
        b_spec = _bspec((tk, tn), lambda i, j, k: (k, j), bl)
        out_struct = jax.ShapeDtypeStruct((M, N), out_dtype)
        if split:
            assert tn == D_MODEL and N == count * tn
            out_spec = pl.BlockSpec((None, tm, tn), lambda i, j, k: (j + base, i, 0))
            out_struct = jax.ShapeDtypeStruct((3, M, tn), out_dtype)
    elif mode == "nt":
        b, bl, (N, K2) = _unlead(b)
        if split:
            assert tk == D_MODEL
            M, K = a.shape[1], count * a.shape[2]
            a_spec = pl.BlockSpec((None, tm, tk), lambda i, j, k: (k + base, i, 0))
        else:
            a, al, (M, K) = _unlead(a)
            a_spec = _bspec((tm, tk), lambda i, j, k: (i, k), al)
        b_spec = _bspec((tn, tk), lambda i, j, k: (j, k), bl)
        out_struct = jax.ShapeDtypeStruct((M, N), out_dtype)
    else:
        a, al, (K, M) = _unlead(a)
        if split:
            assert tn == D_MODEL
            K2, N = b.shape[1], count * b.shape[2]
            b_spec = pl.BlockSpec((None, tk, tn), lambda i, j, k: (j + base, k, 0))
        else:
            b, bl, (K2, N) = _unlead(b)
            b_spec = _bspec((tk, tn), lambda i, j, k: (k, j), bl)
        a_spec = _bspec((tk, tm), lambda i, j, k: (k, i), al)
        out_struct = jax.ShapeDtypeStruct((M, N), out_dtype)
    assert K == K2 and M % tm == 0 and N % tn == 0 and K % tk == 0, (a.shape, b.shape, mode, tm, tn, tk)
    nk = K // tk
    dims = (_DIMS[mode], ((), ()))
    has_add = add is not None

    narrow = out_dtype != F32
    assert not (narrow and has_add)

    def body(*refs):
        if into is not None:
            refs = refs[:2] + refs[3:]
        if has_add:
            a_ref, b_ref, add_ref, o_ref = refs
            acc_ref = o_ref
        elif narrow:
            a_ref, b_ref, o_ref, acc_ref = refs
        else:
            a_ref, b_ref, o_ref = refs
            acc_ref = o_ref
        k = pl.program_id(2)
        part = lax.dot_general(a_ref[...].astype(BF16), b_ref[...].astype(BF16), dims, preferred_element_type=F32)
        if has_add:
            @pl.when(k == 0)
            def _():
                acc_ref[...] = part + add_scale * add_ref[...]
        else:
            @pl.when(k == 0)
            def _():
                acc_ref[...] = part

        @pl.when(k > 0)
        def _():
            acc_ref[...] += part

        if narrow:
            @pl.when(k == nk - 1)
            def _():
                o_ref[...] = acc_ref[...].astype(out_dtype)

    in_specs = [a_spec, b_spec]
    args = [a, b]
    aliases = {}
    if into is not None:
        assert mode == "nn" and split and not has_add
        in_specs.append(pl.BlockSpec(memory_space=pl.ANY))
        args.append(into)
        aliases = {2: 0}
    if has_add:
        in_specs.append(pl.BlockSpec((tm, tn), lambda i, j, k: (i, j)))
        args.append(add)
    return pl.pallas_call(
        body, name=name, grid=(M // tm, N // tn, nk),
        in_specs=in_specs, out_specs=out_spec, out_shape=out_struct, input_output_aliases=aliases,
        scratch_shapes=[pltpu.VMEM((tm, tn), F32)] if narrow else [],
        compiler_params=_cparams(("parallel", "parallel", "arbitrary")),
    )(*args)


def _pick(n, cands):
    for c in cands:
        if n % c == 0:
            return c
    raise ValueError((n, cands))


def _mm_nn(a, b, out_dtype, name, split=False, into=None):
    M, K = _unlead(a)[2]
    N = _unlead(b)[2][1]
    return _matmul(a, b, "nn", out_dtype, _pick(M, (1024, 512, 256)), _pick(N, (1024, 512)), _pick(K, (1024, 512)), name,
                   split=split, into=into)


def _mm_nt(a, b, name, add=None, add_scale=1.0, split=False):
    M, K = (a.shape[1], D_MODEL) if split else _unlead(a)[2]
    N = _unlead(b)[2][0]
    return _matmul(a, b, "nt", F32, _pick(M, (1024, 512, 256)), _pick(N, (1024, 512)),
                   _pick(K, (2816, 1024, 512)), name, add=add, add_scale=add_scale, split=split)


def _mm_tn(a, b, name, split=False, out_dtype=F32):
    K, M = _unlead(a)[2]
    N = D_MODEL if split else _unlead(b)[2][1]
    return _matmul(a, b, "tn", out_dtype, _pick(M, (1024, 1408, 512)), _pick(N, (1408, 1024, 512)),
                   _pick(K, (2048, 1024, 512, 256)), name, split=split)


def _ffn_in(x, w, name):
    S = x.shape[0]
    tm = _pick(S, (512, 256))
    w, wl, _ = _unlead(w)

    def body(x_ref, w_ref, t_ref, h_ref):
        acc = jnp.dot(x_ref[...].astype(BF16), w_ref[...], preferred_element_type=F32)
        g = acc[:, :HALF_FF]
        up = acc[:, HALF_FF:]
        sg = jax.nn.sigmoid(g)
        silu = g * sg
        t_ref[:, :HALF_FF] = (up * (sg * (1.0 + g * (1.0 - sg)))).astype(BF16)
        t_ref[:, HALF_FF:] = silu.astype(BF16)
        h_ref[...] = (silu * up).astype(BF16)

    return pl.pallas_call(
        body, name=name, grid=(2, S // tm),
        in_specs=[pl.BlockSpec((tm, D_MODEL), lambda j, i: (i, 0)),
                  _bspec((D_MODEL, D_FF), lambda j, i: (0, j), wl)],
        out_specs=[pl.BlockSpec((tm, D_FF), lambda j, i: (i, j)),
                   pl.BlockSpec((tm, HALF_FF), lambda j, i: (i, j))],
        out_shape=[jax.ShapeDtypeStruct((S, 2 * D_FF), BF16), jax.ShapeDtypeStruct((S, D_FF), BF16)],
        compiler_params=_cparams(("parallel", "parallel")),
    )(x, w)


def _ffn_bwd_h(dzc, w_out, u, name):
    S = dzc.shape[0]
    tm = _pick(S, (512, 256))
    w_out, wl, _ = _unlead(w_out)

    def body(dz_ref, w_ref, t_ref, du_ref):
        dh = lax.dot_general(dz_ref[...], w_ref[...], (((1,), (1,)), ((), ())), preferred_element_type=F32)
        du_ref[:, :HALF_FF] = (dh * t_ref[:, :HALF_FF].astype(F32)).astype(BF16)
        du_ref[:, HALF_FF:] = (dh * t_ref[:, HALF_FF:].astype(F32)).astype(BF16)

    return pl.pallas_call(
        body, name=name, grid=(2, S // tm),
        in_specs=[pl.BlockSpec((tm, D_MODEL), lambda j, i: (i, 0)),
                  _bspec((HALF_FF, D_MODEL), lambda j, i: (j, 0), wl),
                  pl.BlockSpec((tm, D_FF), lambda j, i: (i, j))],
        out_specs=pl.BlockSpec((tm, D_FF), lambda j, i: (i, j)),
        out_shape=jax.ShapeDtypeStruct((S, 2 * D_FF), BF16),
        compiler_params=_cparams(("parallel", "parallel")),
    )(dzc, w_out, u)


def _mm_ln(a, w, resid, gain, bias, c, name):
    S, K = a.shape
    tm = _pick(S, (512, 256))
    w, wl, _ = _unlead(w)

    def body(a_ref, w_ref, r_ref, g_ref, b_ref, y_ref, yb_ref, z_ref):
        z = ALPHA * r_ref[...] + c * jnp.dot(a_ref[...], w_ref[...], preferred_element_type=F32)
        mu = jnp.mean(z, axis=-1, keepdims=True)
        zc = z - mu
        var = jnp.mean(zc * zc, axis=-1, keepdims=True)
        y = zc * lax.rsqrt(var + LN_EPS) * g_ref[...] + b_ref[...]
        z_ref[...] = z
        y_ref[...] = y
        yb_ref[...] = y.astype(BF16)

    row = pl.BlockSpec((tm, D_MODEL), lambda i: (i, 0))
    vec = pl.BlockSpec((1, D_MODEL), lambda i: (0, 0))
    return pl.pallas_call(
        body, name=name, grid=(S // tm,),
        in_specs=[pl.BlockSpec((tm, K), lambda i: (i, 0)), _bspec((K, D_MODEL), lambda i: (0, 0), wl), row, vec, vec],
        out_specs=[row, row, row],
        out_shape=[jax.ShapeDtypeStruct((S, D_MODEL), F32), jax.ShapeDtypeStruct((S, D_MODEL), BF16),
                   jax.ShapeDtypeStruct((S, D_MODEL), F32)],
        compiler_params=_cparams(("parallel",)),
    )(a, w, resid, gain, bias)


def _ln_bwd(z, dy, gain, c, name):
    S = z.shape[0]
    tm = _pick(S, (512, 256))

    def body(z_ref, dy_ref, g_ref, dz_ref, dzc_ref, gg_ref, gb_ref):
        i = pl.program_id(0)
        zv = z_ref[...]
        dyv = dy_ref[...]
        mu = jnp.mean(zv, axis=-1, keepdims=True)
        zc = zv - mu
        var = jnp.mean(zc * zc, axis=-1, keepdims=True)
        rstd = lax.rsqrt(var + LN_EPS)
        xhat = zc * rstd
        dyg = dyv * g_ref[...]
        m1 = jnp.mean(dyg, axis=-1, keepdims=True)
        m2 = jnp.mean(dyg * xhat, axis=-1, keepdims=True)
        dz = rstd * (dyg - m1 - xhat * m2)
        dz_ref[...] = dz
        dzc_ref[...] = (c * dz).astype(BF16)
        pg = jnp.sum((dyv * xhat).reshape(tm // 8, 8, D_MODEL), axis=0)
        pb = jnp.sum(dyv.reshape(tm // 8, 8, D_MODEL), axis=0)

        @pl.when(i == 0)
        def _():
            gg_ref[...] = pg
            gb_ref[...] = pb

        @pl.when(i > 0)
        def _():
            gg_ref[...] += pg
            gb_ref[...] += pb

    row = pl.BlockSpec((tm, D_MODEL), lambda i: (i, 0))
    part = pl.BlockSpec((8, D_MODEL), lambda i: (0, 0))
    return pl.pallas_call(
        body, name=name, grid=(S // tm,),
        in_specs=[row, row, pl.BlockSpec((1, D_MODEL), lambda i: (0, 0))],
        out_specs=[row, row, part, part],
        out_shape=[jax.ShapeDtypeStruct((S, D_MODEL), F32), jax.ShapeDtypeStruct((S, D_MODEL), BF16),
                   jax.ShapeDtypeStruct((8, D_MODEL), F32), jax.ShapeDtypeStruct((8, D_MODEL), F32)],
        compiler_params=_cparams(("arbitrary",)),
    )(z, dy, gain)


def _loss_grad(y, t, name):
    S = y.shape[0]
    tm = _pick(S, (512, 256))

    def body(y_ref, t_ref, dy_ref, sq_ref):
        i = pl.program_id(0)
        e = y_ref[...] - t_ref[...]
        dy_ref[...] = e * (1.0 / D_MODEL)
        ps = jnp.sum((e * e).reshape(tm // 8, 8, D_MODEL), axis=0)

        @pl.when(i == 0)
        def _():
            sq_ref[...] = ps

        @pl.when(i > 0)
        def _():
            sq_ref[...] += ps

    row = pl.BlockSpec((tm, D_MODEL), lambda i: (i, 0))
    return pl.pallas_call(
        body, name=name, grid=(S // tm,),
        in_specs=[row, row], out_specs=[row, pl.BlockSpec((8, D_MODEL), lambda i: (0, 0))],
        out_shape=[jax.ShapeDtypeStruct((S, D_MODEL), F32), jax.ShapeDtypeStruct((8, D_MODEL), F32)],
        compiler_params=_cparams(("arbitrary",)),
    )(y, t)


def _rows(start, d):
    if d == 1:
        return pl.ds(pl.multiple_of(start, BLOCK), BLOCK)
    return pl.ds(start, BLOCK, stride=d)


def _ld(ref, start, d):
    return ref[_rows(start, d), :]


def _ld3(ref, lead, start, d):
    return ref[lead, _rows(start, d), :]


def _st3(ref, lead, start, d, val):
    ref[lead, _rows(start, d), :] = val


def _acc3(ref, lead, start, d, val):
    ref[lead, _rows(start, d), :] = ref[lead, _rows(start, d), :] + val


def _band_consts(slope0, slope1, maxd, scale):
    row = lax.broadcasted_iota(jnp.int32, (2 * BLOCK, 2 * BLOCK), 0)
    kj = lax.broadcasted_iota(jnp.int32, (2 * BLOCK, 2 * BLOCK), 1)
    top = row < BLOCK
    dist = BLOCK + jnp.where(top, row, row - BLOCK) - kj
    slope = jnp.where(top, slope0, slope1)
    base = jnp.where((dist >= 0) & (dist <= maxd), -(slope * (dist.astype(F32) * scale)), NEG)
    return base, kj < BLOCK


def _stack_heads(x, lo):
    return jnp.concatenate([jnp.where(lo, x, 0.0), jnp.where(lo, 0.0, x)], axis=0)


def _unstack_heads(x2, lo):
    return jnp.where(lo, x2[:BLOCK], x2[BLOCK:])


def _scores(q2, k2, base, prev_keys, first):
    s = lax.dot_general(q2, k2, (((1,), (1,)), ((), ())), preferred_element_type=F32) * (HEAD_DIM ** -0.5) + base
    return jnp.where(jnp.logical_and(prev_keys, first), NEG, s)


def _softmax_weights(ls):
    mx = ls[0]
    for l in ls[1:]:
        mx = jnp.maximum(mx, l)
    es = [jnp.exp(l - mx) for l in ls]
    tot = es[0]
    for e in es[1:]:
        tot = tot + e
    inv = 1.0 / tot
    return [e * inv for e in es]


def _attn_fwd(qkv, slopes, sinks, patterns, name):
    S = qkv.shape[1]
    npat = len(patterns)
    has_sink = sinks is not None
    if not has_sink:
        sinks = jnp.zeros((N_HEADS,), F32)
    rows_c = 256

    def body(slopes_ref, sinks_ref, x_ref, mix_ref, o_ref, lse_ref, o_scr, lse_scr):
        p = pl.program_id(0)
        lo = lax.broadcasted_iota(jnp.int32, (BLOCK, SLAB), 1) < HEAD_DIM
        top1 = lax.broadcasted_iota(jnp.int32, (2 * BLOCK, 1), 0) < BLOCK
        sk2 = jnp.where(top1, sinks_ref[2 * p], sinks_ref[2 * p + 1])
        for pi, (d, maxd, scale) in enumerate(patterns):
            nb = S // d // BLOCK
            base, prev_keys = _band_consts(slopes_ref[2 * p], slopes_ref[2 * p + 1], maxd, scale)

            def blk(t, carry, pi=pi, d=d, nb=nb, base=base, prev_keys=prev_keys):
                r = t // nb
                n = t - r * nb
                start = r + (d * BLOCK) * n
                prev = jnp.where(n > 0, start - d * BLOCK, start)
                q2 = _stack_heads(_ld3(x_ref, 0, start, d), lo).astype(BF16)
                k2 = jnp.concatenate([_ld3(x_ref, 1, prev, d), _ld3(x_ref, 1, start, d)], axis=0).astype(BF16)
                v2 = jnp.concatenate([_ld3(x_ref, 2, prev, d), _ld3(x_ref, 2, start, d)], axis=0).astype(BF16)
                s = _scores(q2, k2, base, prev_keys, n == 0)
                m = jnp.max(s, axis=-1, keepdims=True)
                if has_sink:
                    m = jnp.maximum(m, sk2)
                e = jnp.exp(s - m)
                den = jnp.sum(e, axis=-1, keepdims=True)
                if has_sink:
                    den = den + jnp.exp(sk2 - m)
                o2 = jnp.dot((e / den).astype(BF16), v2, preferred_element_type=F32)
                _st3(o_scr, pi, start, d, _unstack_heads(o2, lo))
                _st3(lse_scr, pi, start, d, _unstack_heads(m + jnp.log(den), lo))
                return carry

            lax.fori_loop(0, d * nb, blk, 0, unroll=8)

        lane_c = lax.broadcasted_iota(jnp.int32, (rows_c, SLAB), 1)

        def comb(ci, carry):
            rows = pl.ds(pl.multiple_of(ci * rows_c, rows_c), rows_c)
            ls = [lse_scr[i, rows, :] for i in range(npat)]
            packed = jnp.zeros((rows_c, SLAB), F32)
            for i in range(npat):
                o_ref[i, rows, :] = o_scr[i, rows, :].astype(BF16)
                packed = jnp.where(lane_c % HEAD_DIM == i, ls[i], packed)
            lse_ref[rows, :] = packed
            if npat == 1:
                mix_ref[rows, :] = o_scr[0, rows, :].astype(BF16)
            else:
                ws = _softmax_weights(ls)
                acc = ws[0] * o_scr[0, rows, :]
                for i in range(1, npat):
                    acc = acc + ws[i] * o_scr[i, rows, :]
                mix_ref[rows, :] = acc.astype(BF16)
            return carry

        lax.fori_loop(0, S // rows_c, comb, 0, unroll=2)

    smem = pl.BlockSpec(memory_space=pltpu.SMEM)
    return pl.pallas_call(
        body, name=name, grid=(N_SLABS,),
        in_specs=[smem, smem, pl.BlockSpec((3, S, SLAB), lambda p: (0, 0, p))],
        out_specs=[pl.BlockSpec((S, SLAB), lambda p: (0, p)), pl.BlockSpec((npat, S, SLAB), lambda p: (0, 0, p)),
                   pl.BlockSpec((None, S, SLAB), lambda p: (p, 0, 0))],
        out_shape=[jax.ShapeDtypeStruct((S, D_MODEL), BF16), jax.ShapeDtypeStruct((npat, S, D_MODEL), BF16),
                   jax.ShapeDtypeStruct((N_SLABS, S, SLAB), F32)],
        scratch_shapes=[pltpu.VMEM((npat, S, SLAB), F32), pltpu.VMEM((npat, S, SLAB), F32)],
        compiler_params=_cparams(("arbitrary",)),
    )(slopes, sinks, qkv)


def _attn_bwd(qkv, dout, o, lse, slopes, sinks, patterns, name):
    S = qkv.shape[1]
    npat = len(patterns)
    has_sink = sinks is not None
    if not has_sink:
        sinks = jnp.zeros((N_HEADS,), F32)
    rows_c = 256

    def headsum(x, lo):
        same = (lax.broadcasted_iota(jnp.int32, (SLAB, SLAB), 0) < HEAD_DIM) == (lax.broadcasted_iota(jnp.int32, (SLAB, SLAB), 1) < HEAD_DIM)
        return jnp.dot(x, same.astype(F32), precision=lax.Precision.HIGH, preferred_element_type=F32)

    def body(slopes_ref, sinks_ref, x_ref, do_ref, o_ref, lsep_ref, dxo_ref, dsink_ref, dbar_ref, sacc_ref, lse_ref, dx_ref):
        p = pl.program_id(0)
        lo = lax.broadcasted_iota(jnp.int32, (BLOCK, SLAB), 1) < HEAD_DIM
        lo_c = lax.broadcasted_iota(jnp.int32, (rows_c, SLAB), 1) < HEAD_DIM
        top1 = lax.broadcasted_iota(jnp.int32, (2 * BLOCK, 1), 0) < BLOCK
        sk2 = jnp.where(top1, sinks_ref[2 * p], sinks_ref[2 * p + 1])

        def prep(ci, carry):
            rows = pl.ds(pl.multiple_of(ci * rows_c, rows_c), rows_c)
            dov = do_ref[rows, :]
            dx_ref[:, rows, :] = jnp.zeros((3, rows_c, SLAB), F32)
            packed = lsep_ref[rows, :]
            ls = [jnp.where(lo_c, packed[:, i:i + 1], packed[:, HEAD_DIM + i:HEAD_DIM + i + 1]) for i in range(npat)]
            for i in range(npat):
                lse_ref[i, rows, :] = ls[i]
            if npat == 1:
                dbar_ref[rows, :] = headsum(dov * o_ref[0, rows, :].astype(F32), lo_c)
            else:
                ws = _softmax_weights(ls)
                acc = ws[0] * headsum(dov * o_ref[0, rows, :].astype(F32), lo_c)
                for i in range(1, npat):
                    acc = acc + ws[i] * headsum(dov * o_ref[i, rows, :].astype(F32), lo_c)
                dbar_ref[rows, :] = acc
            return carry

        lax.fori_loop(0, S // rows_c, prep, 0, unroll=2)
        sacc_ref[...] = jnp.zeros((BLOCK, SLAB), F32)

        for pi, (d, maxd, scale) in enumerate(patterns):
            nb = S // d // BLOCK
            base, prev_keys = _band_consts(slopes_ref[2 * p], slopes_ref[2 * p + 1], maxd, scale)

            def blk(t, carry, pi=pi, d=d, nb=nb, base=base, prev_keys=prev_keys):
                r = t // nb
                n = t - r * nb
                start = r + (d * BLOCK) * n
                prev = jnp.where(n > 0, start - d * BLOCK, start)
                q2 = _stack_heads(_ld3(x_ref, 0, start, d), lo).astype(BF16)
                k2 = jnp.concatenate([_ld3(x_ref, 1, prev, d), _ld3(x_ref, 1, start, d)], axis=0).astype(BF16)
                v2 = jnp.concatenate([_ld3(x_ref, 2, prev, d), _ld3(x_ref, 2, start, d)], axis=0).astype(BF16)
                ls = [_ld3(lse_ref, i, start, d) for i in range(npat)]
                w = _softmax_weights(ls)[pi] if npat > 1 else 1.0
                do2 = _stack_heads(w * _ld(do_ref, start, d), lo).astype(BF16)
                dl = w * _ld(dbar_ref, start, d)
                lse2 = jnp.concatenate([ls[pi][:, :1], ls[pi][:, HEAD_DIM:HEAD_DIM + 1]], axis=0)
                dl2 = jnp.concatenate([dl[:, :1], dl[:, HEAD_DIM:HEAD_DIM + 1]], axis=0)
                s = _scores(q2, k2, base, prev_keys, n == 0)
                pr = jnp.exp(s - lse2)
                dp = lax.dot_general(do2, v2, (((1,), (1,)), ((), ())), preferred_element_type=F32)
                ds = (pr * (dp - dl2) * (HEAD_DIM ** -0.5)).astype(BF16)
                dq2 = jnp.dot(ds, k2, preferred_element_type=F32)
                dk2 = lax.dot_general(ds, q2, (((0,), (0,)), ((), ())), preferred_element_type=F32)
                dv2 = lax.dot_general(pr.astype(BF16), do2, (((0,), (0,)), ((), ())), preferred_element_type=F32)
                _acc3(dx_ref, 0, start, d, _unstack_heads(dq2, lo))
                _acc3(dx_ref, 1, prev, d, dk2[:BLOCK])
                _acc3(dx_ref, 1, start, d, dk2[BLOCK:])
                _acc3(dx_ref, 2, prev, d, dv2[:BLOCK])
                _acc3(dx_ref, 2, start, d, dv2[BLOCK:])
                if has_sink:
                    sacc_ref[...] += _unstack_heads(-jnp.exp(sk2 - lse2) * dl2, lo)
                return carry

            lax.fori_loop(0, d * nb, blk, 0, unroll=8)

        dsink_ref[...] = jnp.broadcast_to(jnp.sum(sacc_ref[...], axis=0, keepdims=True), (8, SLAB))

        def emit(ci, carry):
            rows = pl.ds(pl.multiple_of(ci * rows_c, rows_c), rows_c)
            dxo_ref[:, rows, :] = dx_ref[:, rows, :].astype(BF16)
            return carry

        lax.fori_loop(0, S // rows_c, emit, 0, unroll=2)

    smem = pl.BlockSpec(memory_space=pltpu.SMEM)
    return pl.pallas_call(
        body, name=name, grid=(N_SLABS,),
        in_specs=[smem, smem, pl.BlockSpec((3, S, SLAB), lambda p: (0, 0, p)), pl.BlockSpec((S, SLAB), lambda p: (0, p)),
                  pl.BlockSpec((npat, S, SLAB), lambda p: (0, 0, p)), pl.BlockSpec((None, S, SLAB), lambda p: (p, 0, 0))],
        out_specs=[pl.BlockSpec((3, S, SLAB), lambda p: (0, 0, p)), pl.BlockSpec((None, 8, SLAB), lambda p: (p, 0, 0))],
        out_shape=[jax.ShapeDtypeStruct((3, S, D_MODEL), BF16), jax.ShapeDtypeStruct((N_SLABS, 8, SLAB), F32)],
        scratch_shapes=[pltpu.VMEM((S, SLAB), F32), pltpu.VMEM((BLOCK, SLAB), F32), pltpu.VMEM((npat, S, SLAB), F32),
                        pltpu.VMEM((3, S, SLAB), F32)],
        compiler_params=_cparams(("arbitrary",)),
    )(slopes, sinks, qkv, dout, o, lse)


def _place():
    x, y, c = lax.axis_index("x"), lax.axis_index("y"), lax.axis_index("c")
    return x, y, c, 2 * x + y


def _other_chips(x, y):
    return [(1 - x, y), (x, 1 - y), (1 - x, 1 - y)]


HBM_SPEC = pl.BlockSpec(memory_space=pl.ANY)


def _slot(q):
    return 2 * (q % 2) + q // 2


BIG = ("ffn1_w_in", "ffn1_w_out", "ffn2_w_in", "ffn2_w_out", "a_w_qkv", "a_w_o", "kv_w", "b_w_q", "b_w_o")
QKV_SHARD = 3 * D_MODEL // N_CHIPS
ROW_SHARD = D_MODEL // N_CHIPS


FIRST_ITEMS = (("ffn1_w_in", 0), ("ffn1_w_out", 0), ("a_w_qkv", None), ("a_w_o", None))
FFN2_0_ITEMS = (("ffn2_w_in", 0), ("ffn2_w_out", 0), ("kv_w", None))
LAYER1_ITEMS = (("ffn1_w_in", 1), ("ffn1_w_out", 1), ("b_w_q", None), ("b_w_o", None), ("ffn2_w_in", 1),
                ("ffn2_w_out", 1))
OUT_SHARD = D_FF // N_CHIPS


def _full_shape(name):
    if name.endswith("w_in"):
        return (D_MODEL, 2 * D_FF)
    if name.endswith("w_out"):
        return (D_FF, D_MODEL)
    if name == "a_w_qkv":
        return (D_MODEL, 3 * D_MODEL)
    if name == "kv_w":
        return (N_CHIPS, 2, ROW_SHARD // 2, 2 * N_KV_B * HEAD_DIM)
    return (N_CHIPS, 2, ROW_SHARD // 2, D_MODEL)


def _gather_src(item, ref, c):
    name, _ = item
    if name.endswith("w_in"):
        return ref.at[pl.ds(c * (D_MODEL // 2), D_MODEL // 2)]
    if name.endswith("w_out"):
        return ref.at[pl.ds(c * (OUT_SHARD // 2), OUT_SHARD // 2)]
    if name == "a_w_qkv":
        return ref.at[0, pl.ds(c * (D_MODEL // 2), D_MODEL // 2)]
    if name == "kv_w":
        return ref.at[pl.ds(c * (ROW_SHARD // 2), ROW_SHARD // 2)]
    return ref.at[0, pl.ds(c * (ROW_SHARD // 2), ROW_SHARD // 2)]


def _gather_dst(item, ref, q, c):
    name, _ = item
    if name.endswith("w_in"):
        return ref.at[pl.ds(c * (D_MODEL // 2), D_MODEL // 2), pl.ds(_slot(q) * HALF_FF, HALF_FF)]
    if name.endswith("w_out"):
        return ref.at[pl.ds(q * OUT_SHARD + c * (OUT_SHARD // 2), OUT_SHARD // 2)]
    if name == "a_w_qkv":
        return ref.at[pl.ds(c * (D_MODEL // 2), D_MODEL // 2), pl.ds(q * QKV_SHARD, QKV_SHARD)]
    return ref.at[q, c]


def _all_gather(items, shards, small):
    n = len(items)
    r = small.shape[0]
    per = 8

    def body(*refs):
        srcs, small_ref = refs[:n], refs[n]
        dsts, s_ref = refs[n + 1:2 * n + 1], refs[2 * n + 1]
        send_sems, recv_sems = refs[2 * n + 2:]
        x, y, c, myq = _place()
        sibling = (x, y, 1 - c)
        chips = _other_chips(x, y)

        def big(t, k, src, q, h, to):
            return pltpu.make_async_remote_copy(src_ref=src, dst_ref=_gather_dst(items[t], dsts[t], q, h),
                                                send_sem=send_sems.at[per * t + k], recv_sem=recv_sems.at[per * t + k],
                                                device_id=to, device_id_type=MESH)

        def tiny(k, q, to):
            return pltpu.make_async_remote_copy(src_ref=small_ref, dst_ref=s_ref.at[q], send_sem=send_sems.at[per * n + k],
                                                recv_sem=recv_sems.at[per * n + k], device_id=to, device_id_type=MESH)

        first = []
        for j, chip in enumerate(chips):
            if j < 2:
                first += [big(t, j, _gather_src(items[t], srcs[t], c), myq, c, (*chip, c)) for t in range(n)]
            first.append(tiny(j, myq, (*chip, c)))
        own = [big(t, 6 + h, _gather_src(items[t], srcs[t], h), myq, h, sibling) for t in range(n) for h in (0, 1)]
        own.append(tiny(3, myq, sibling))
        for cp in first + own:
            cp.start()
        relay_from = ((x + 1 - c) % 2, (y + c) % 2)
        relay_to = ((x + c) % 2, (y + 1 - c) % 2, c)
        q_relay = 2 * relay_from[0] + relay_from[1]
        passed = []
        for t in range(n):
            src = _gather_src(items[t], srcs[t], c)
            for j, (cx, cy) in enumerate(chips[:2]):
                q = 2 * cx + cy
                big(t, j, src, q, c, sibling).wait_recv()
                fwd = big(t, 3 + j, _gather_dst(items[t], dsts[t], q, c), q, c, sibling)
                fwd.start()
                passed.append(fwd)
            relay = big(t, 2, _gather_dst(items[t], dsts[t], q_relay, c), q_relay, c, relay_to)
            relay.start()
            passed.append(relay)
        q_diag = 2 * chips[2][0] + chips[2][1]
        for t in range(n):
            big(t, 2, _gather_src(items[t], srcs[t], c), q_diag, c, sibling).wait_recv()
            fwd = big(t, 5, _gather_dst(items[t], dsts[t], q_diag, c), q_diag, c, sibling)
            fwd.start()
            passed.append(fwd)
        for j, (cx, cy) in enumerate(chips):
            q = 2 * cx + cy
            for t in range(n):
                big(t, 3 + j, _gather_src(items[t], srcs[t], c), q, 1 - c, sibling).wait_recv()
            tiny(j, q, sibling).wait_recv()
        for cp in own:
            cp.wait_recv()
        for cp in first + passed + own:
            cp.wait_send()

    outs = pl.pallas_call(
        body, name="all_gather_layer0",
        in_specs=[HBM_SPEC] * (n + 1), out_specs=[HBM_SPEC] * (n + 1),
        out_shape=[jax.ShapeDtypeStruct(_full_shape(name), BF16) for name, _ in items]
        + [jax.ShapeDtypeStruct((N_CHIPS, r, 128), F32)],
        scratch_shapes=[pltpu.SemaphoreType.DMA((per * n + 4,)), pltpu.SemaphoreType.DMA((per * n + 4,))],
    )(*[shards[item] for item in items], small)
    return list(outs[:n]), outs[n]


SEM_SPEC = pl.BlockSpec(memory_space=pltpu.SEMAPHORE)
DATAFLOW = pltpu.SideEffectType.DATAFLOW_SIDE_EFFECTING
PER_ITEM = 8


def _split_start(name, copies, n_sems, sources, land_shapes, after):
    n, m = len(sources), len(land_shapes)

    def body(*refs):
        srcs, lands = refs[:n], refs[n:n + m]
        send_sems, recv_sems = refs[n + m + 1], refs[n + m + 2]
        token = refs[-1]
        for src, dst_there, _, s, peer in copies(srcs, lands):
            pltpu.make_async_remote_copy(src_ref=src, dst_ref=dst_there, send_sem=send_sems.at[s], recv_sem=recv_sems.at[s],
                                         device_id=peer, device_id_type=MESH).start()
        token[...] = jnp.zeros_like(token)

    src_arrays = [pltpu.with_memory_space_constraint(a, pltpu.HBM) for a in sources]
    land_arrays = [pltpu.with_memory_space_constraint(lax.empty(s.shape, s.dtype), pltpu.HBM) for s in land_shapes]
    hbm = pl.BlockSpec(memory_space=pltpu.HBM)
    outs = pl.pallas_call(
        body, name=name,
        in_specs=[hbm] * (n + m) + [HBM_SPEC],
        out_specs=[SEM_SPEC, SEM_SPEC] + [hbm] * (n + m) + [pl.BlockSpec(memory_space=pltpu.VMEM)],
        out_shape=[pltpu.SemaphoreType.DMA((n_sems,)), pltpu.SemaphoreType.DMA((n_sems,))]
        + [pltpu.HBM(a.shape, a.dtype) for a in src_arrays + land_arrays] + [jax.ShapeDtypeStruct((8, 128), F32)],
        input_output_aliases={i: 2 + i for i in range(n + m)},
        compiler_params=pltpu.CompilerParams(has_side_effects=DATAFLOW),
    )(*src_arrays, *land_arrays, after)
    return (outs[0], outs[1], list(outs[2:2 + n]), list(outs[2 + n:2 + n + m])), outs[-1]


def _split_wait(name, copies, state, after):
    send_sems, recv_sems, srcs_thru, lands_thru = state
    n, m = len(srcs_thru), len(lands_thru)
    after = list(after) if isinstance(after, (list, tuple)) else [after]

    def body(*refs):
        srcs, lands = refs[:n], refs[n:n + m]
        send_sems, recv_sems = refs[n + m], refs[n + m + 1]
        for src, _, dst_here, s, peer in copies(srcs, lands):
            cp = pltpu.make_async_remote_copy(src_ref=src, dst_ref=dst_here, send_sem=send_sems.at[s], recv_sem=recv_sems.at[s],
                                              device_id=peer, device_id_type=MESH)
            cp.wait_send()
            cp.wait_recv()

    hbm = pl.BlockSpec(memory_space=pltpu.HBM)
    outs = pl.pallas_call(
        body, name=name,
        in_specs=[hbm] * (n + m) + [SEM_SPEC, SEM_SPEC] + [HBM_SPEC] * len(after),
        out_specs=[hbm] * (n + m),
        out_shape=[pltpu.HBM(a.shape, a.dtype) for a in srcs_thru + lands_thru],
        input_output_aliases={i: i for i in range(n + m)},
        compiler_params=pltpu.CompilerParams(has_side_effects=DATAFLOW),
    )(*srcs_thru, *lands_thru, send_sems, recv_sems, *after)
    return list(outs[:n]), list(outs[n:])


def _gather_copies(items):
    def copies(srcs, lands):
        x, y, c, myq = _place()
        out = []
        for t, item in enumerate(items):
            for h in (0, 1):
                src = _gather_src(item, srcs[t], h)
                for j, (cx, cy) in enumerate(_other_chips(x, y)):
                    out.append((src, _gather_dst(item, lands[t], myq, h), _gather_dst(item, lands[t], 2 * cx + cy, h),
                                PER_ITEM * t + 2 * j + h, (cx, cy, c)))
                out.append((src, _gather_dst(item, lands[t], myq, h), _gather_dst(item, lands[t], myq, h),
                            PER_ITEM * t + 6 + h, (x, y, 1 - c)))
        return out
    return copies


def _gather_start(tag, items, shards, after):
    lands = [jax.ShapeDtypeStruct(_full_shape(name), BF16) for name, _ in items]
    return _split_start("gather_%s_start" % tag, _gather_copies(items), PER_ITEM * len(items),
                        [shards[item] for item in items], lands, after)


def _gather_wait(tag, items, state, after):
    return _split_wait("gather_%s_wait" % tag, _gather_copies(items), state, after)[1]


def _small_all_reduce(v):
    r = v.shape[0]

    def body(v_ref, o_ref, buf_ref, send_sems, recv_sems):
        x, y, c, _ = _place()
        me = 4 * x + 2 * y + c
        buf_ref[me] = v_ref[...]
        copies = []
        for k in range(1, 8):
            fx, fy, fc = (k >> 2) & 1, (k >> 1) & 1, k & 1
            to = (x ^ fx, y ^ fy, c ^ fc)
            cp = pltpu.make_async_remote_copy(src_ref=v_ref, dst_ref=buf_ref.at[me], send_sem=send_sems.at[k - 1],
                                              recv_sem=recv_sems.at[k - 1], device_id=to, device_id_type=MESH)
            cp.start()
            copies.append(cp)
        for k in range(1, 8):
            fx, fy, fc = (k >> 2) & 1, (k >> 1) & 1, k & 1
            src_dev = 4 * (x ^ fx) + 2 * (y ^ fy) + (c ^ fc)
            pltpu.make_async_remote_copy(src_ref=v_ref, dst_ref=buf_ref.at[src_dev], send_sem=send_sems.at[k - 1],
                                         recv_sem=recv_sems.at[k - 1], device_id=(x, y, c), device_id_type=MESH).wait_recv()
        for cp in copies:
            cp.wait_send()
        tot = buf_ref[0]
        for i in range(1, 8):
            tot = tot + buf_ref[i]
        o_ref[...] = tot

    vm = pl.BlockSpec(memory_space=pltpu.VMEM)
    return pl.pallas_call(
        body, name="small_all_reduce", in_specs=[vm], out_specs=vm,
        out_shape=jax.ShapeDtypeStruct((r, 128), F32),
        scratch_shapes=[pltpu.VMEM((8, r, 128), F32), pltpu.SemaphoreType.DMA((7,)), pltpu.SemaphoreType.DMA((7,))],
    )(v)


def _grad_view(kind, g):
    if kind == "col":
        return g.reshape(2, g.shape[0] // 2, g.shape[1])
    return g.reshape(N_CHIPS, 2, g.shape[0] // (2 * N_CHIPS), g.shape[1])


def _half_of(kind, ref, h):
    return ref.at[h] if kind == "col" else ref.at[:, h]


def _half_shape(kind, view_shape):
    return view_shape[1:] if kind == "col" else (view_shape[0],) + view_shape[2:]


def _piece_of(kind, width, colblock, ref, q):
    if kind == "col":
        return ref.at[:, pl.ds(colblock(q) * width, width)]
    return ref.at[q]


def _piece_shape(kind, width, half_shape):
    return (half_shape[0], width) if kind == "col" else half_shape[1:]


def _pair_exchange(views, kinds, name):
    n = len(views)

    def body(*refs):
        ins, outs = refs[:n], refs[n:2 * n]
        send_sems, recv_sems = refs[2 * n:]
        x, y, c, _ = _place()
        cps = []
        for t in range(n):
            cp = pltpu.make_async_remote_copy(src_ref=_half_of(kinds[t], ins[t], 1 - c), dst_ref=outs[t],
                                              send_sem=send_sems.at[t], recv_sem=recv_sems.at[t],
                                              device_id=(x, y, 1 - c), device_id_type=MESH)
            cp.start()
            cps.append(cp)
        for cp in cps:
            cp.wait()

    return pl.pallas_call(
        body, name=name, in_specs=[HBM_SPEC] * n, out_specs=[HBM_SPEC] * n,
        out_shape=[jax.ShapeDtypeStruct(_half_shape(k, v.shape), v.dtype) for k, v in zip(kinds, views)],
        scratch_shapes=[pltpu.SemaphoreType.DMA((n,)), pltpu.SemaphoreType.DMA((n,))],
    )(*views)


def _pair_sum(kind, view, recv, c, name):
    hs = recv.shape
    N = hs[-1]
    rows = hs[-2]
    tr = _pick(rows, (512, 352, 128))
    tn = _pick(N, (1408, 1024, 512))

    def body(c_ref, p_ref, r_ref, s_ref):
        s_ref[...] = (p_ref[...] + r_ref[...]).astype(BF16)

    if kind == "col":
        grid = (rows // tr, N // tn)
        mine = pl.BlockSpec((None, tr, tn), lambda i, j, c_ref: (c_ref[0], i, j))
        blk = pl.BlockSpec((tr, tn), lambda i, j, c_ref: (i, j))
        sem = ("parallel", "parallel")
    else:
        grid = (N_CHIPS, rows // tr, N // tn)
        mine = pl.BlockSpec((None, None, tr, tn), lambda q, i, j, c_ref: (q, c_ref[0], i, j))
        blk = pl.BlockSpec((None, tr, tn), lambda q, i, j, c_ref: (q, i, j))
        sem = ("parallel", "parallel", "parallel")
    return pl.pallas_call(
        body, name=name,
        grid_spec=pltpu.PrefetchScalarGridSpec(num_scalar_prefetch=1, grid=grid, in_specs=[mine, blk], out_specs=blk),
        out_shape=jax.ShapeDtypeStruct(hs, BF16),
        compiler_params=_cparams(sem),
    )(c.reshape(1).astype(jnp.int32), view, recv)


def _chip_copies(kinds, widths, colblocks):
    def copies(srcs, lands):
        x, y, c, _ = _place()
        out = []
        for j, (cx, cy) in enumerate(_other_chips(x, y)):
            for t in range(len(kinds)):
                out.append((_piece_of(kinds[t], widths[t], colblocks[t], srcs[t], 2 * cx + cy), lands[t].at[j],
                            lands[t].at[j], 3 * t + j, (cx, cy, c)))
        return out
    return copies


def _chip_land_shapes(sums, kinds, widths):
    return [jax.ShapeDtypeStruct((3,) + _piece_shape(k, w, s.shape), BF16) for k, w, s in zip(kinds, widths, sums)]


def _chip_exchange(sums, kinds, widths, colblocks, name):
    n = len(sums)
    copies = _chip_copies(kinds, widths, colblocks)

    def body(*refs):
        send_sems, recv_sems = refs[2 * n:]
        cps = [pltpu.make_async_remote_copy(src_ref=src, dst_ref=dst, send_sem=send_sems.at[s], recv_sem=recv_sems.at[s],
                                            device_id=peer, device_id_type=MESH)
               for src, dst, _, s, peer in copies(refs[:n], refs[n:2 * n])]
        for cp in cps:
            cp.start()
        for cp in cps:
            cp.wait()

    return pl.pallas_call(
        body, name=name, in_specs=[HBM_SPEC] * n, out_specs=[HBM_SPEC] * n,
        out_shape=_chip_land_shapes(sums, kinds, widths),
        scratch_shapes=[pltpu.SemaphoreType.DMA((3 * n,)), pltpu.SemaphoreType.DMA((3 * n,))],
    )(*sums)


N_DIRECT = 7


def _direct_piece(kind, width, colblock, view_ref, q, h):
    if kind == "col":
        return view_ref.at[h, :, pl.ds(colblock(q) * width, width)]
    return view_ref.at[q, h]


def _direct_copies(kinds, widths, colblocks):
    def copies(srcs, lands):
        x, y, c, myq = _place()
        out = []
        for t in range(len(kinds)):
            def piece(q, h, t=t):
                return _direct_piece(kinds[t], widths[t], colblocks[t], srcs[t], q, h)
            for j, (cx, cy) in enumerate(_other_chips(x, y)):
                for h in (0, 1):
                    out.append((piece(2 * cx + cy, h), lands[t].at[2 * j + c], lands[t].at[2 * j + h],
                                10 * t + 3 * j + c + h, (cx, cy, h)))
            out.append((piece(myq, 1 - c), lands[t].at[6], lands[t].at[6], 10 * t + 9, (x, y, 1 - c)))
        return out
    return copies


def _chip_sum(kind, own_src, recv, block_idx, c, shard_shape, layer, into, name, direct=False):
    n_recv, rows, N = recv.shape
    tr = _pick(rows, (512, 352, 128))
    tn = _pick(N, (1408, 1024, 768, 512))
    ni, nj = rows // tr, N // tn

    def body(q_ref, s_ref, r_ref, *rest):
        o_ref = rest[-1]
        tot = s_ref[...].astype(F32)
        for k in range(n_recv):
            tot = tot + r_ref[k].astype(F32)
        o_ref[...] = tot

    if direct and kind == "col":
        own = pl.BlockSpec((None, tr, tn), lambda i, j, q_ref: (q_ref[1], i, q_ref[0] * nj + j))
    elif direct:
        own = pl.BlockSpec((None, None, tr, tn), lambda i, j, q_ref: (q_ref[0], q_ref[1], i, j))
    elif kind == "col":
        own = pl.BlockSpec((tr, tn), lambda i, j, q_ref: (i, q_ref[0] * nj + j))
    else:
        own = pl.BlockSpec((None, tr, tn), lambda i, j, q_ref: (q_ref[0], i, j))
    if len(shard_shape) == 3:
        lead = 0 if layer is None else layer
        out_spec = pl.BlockSpec((None, tr, tn), lambda i, j, q_ref: (lead, q_ref[1] * ni + i, j))
    else:
        out_spec = pl.BlockSpec((tr, tn), lambda i, j, q_ref: (q_ref[1] * ni + i, j))
    in_specs = [own, pl.BlockSpec((n_recv, tr, tn), lambda i, j, q_ref: (0, i, j))]
    s = own_src
    args = [jnp.stack([block_idx, c]).astype(jnp.int32), s, recv]
    aliases = {}
    if into is not None:
        in_specs.append(HBM_SPEC)
        args.append(into)
        aliases = {3: 0}
    return pl.pallas_call(
        body, name=name,
        grid_spec=pltpu.PrefetchScalarGridSpec(num_scalar_prefetch=1, grid=(ni, nj), in_specs=in_specs, out_specs=out_spec),
        out_shape=jax.ShapeDtypeStruct(shard_shape, F32), input_output_aliases=aliases,
        compiler_params=_cparams(("parallel", "parallel")),
    )(*args)


def _half_window(ref, h):
    rows = ref.shape[-2] // 2
    if ref.ndim == 3:
        return ref.at[:, pl.ds(h * rows, rows)]
    return ref.at[pl.ds(h * rows, rows)]


def _share_halves(grads, name):
    n = len(grads)

    def body(*refs):
        outs = refs[n:2 * n]
        send_sems, recv_sems = refs[2 * n:]
        x, y, c, _ = _place()
        cps = []
        for t in range(n):
            cp = pltpu.make_async_remote_copy(src_ref=_half_window(outs[t], c), dst_ref=_half_window(outs[t], c),
                                              send_sem=send_sems.at[t], recv_sem=recv_sems.at[t],
                                              device_id=(x, y, 1 - c), device_id_type=MESH)
            cp.start()
            cps.append(cp)
        for t in range(n):
            cps[t].wait_send()
            pltpu.make_async_remote_copy(src_ref=_half_window(outs[t], c), dst_ref=_half_window(outs[t], 1 - c),
                                         send_sem=send_sems.at[t], recv_sem=recv_sems.at[t],
                                         device_id=(x, y, 1 - c), device_id_type=MESH).wait_recv()

    return pl.pallas_call(
        body, name=name, in_specs=[HBM_SPEC] * n, out_specs=[HBM_SPEC] * n,
        out_shape=[jax.ShapeDtypeStruct(g.shape, F32) for g in grads],
        input_output_aliases={t: t for t in range(n)},
        scratch_shapes=[pltpu.SemaphoreType.DMA((n,)), pltpu.SemaphoreType.DMA((n,))],
    )(*grads)


def _adamw(w, g, m, v, name):
    R, W = w.shape
    tr = _pick(R, (512, 352, 256, 32))

    def body(w_ref, g_ref, m_ref, v_ref, d_ref, nm_ref, nv_ref):
        gv = g_ref[...]
        nm = ADAM_B1 * m_ref[...] + (1.0 - ADAM_B1) * gv
        nv = ADAM_B2 * v_ref[...] + (1.0 - ADAM_B2) * (gv * gv)
        m_hat = nm / (1.0 - ADAM_B1 ** ADAM_STEP)
        v_hat = nv / (1.0 - ADAM_B2 ** ADAM_STEP)
        d_ref[...] = -ADAM_LR * (m_hat / (jnp.sqrt(v_hat) + ADAM_EPS) + ADAM_WD * w_ref[...])
        nm_ref[...] = nm
        nv_ref[...] = nv

    blk = pl.BlockSpec((tr, W), lambda i: (i, 0))
    shp = jax.ShapeDtypeStruct((R, W), F32)
    return pl.pallas_call(
        body, name=name, grid=(R // tr,), in_specs=[blk] * 4, out_specs=[blk] * 3, out_shape=[shp] * 3,
        compiler_params=_cparams(("parallel",)),
    )(w, g, m, v)


SMALL_ROWS = 32


def _pack_small(ln_g, ln_b, sinks):
    rows = jnp.concatenate([ln_g.reshape(-1, 128), ln_b.reshape(-1, 128),
                            jnp.pad(sinks.reshape(1, -1), ((0, 0), (0, 128 - sinks.size)))], axis=0)
    return jnp.pad(rows, ((0, SMALL_ROWS - rows.shape[0]), (0, 0)))


def _unpack_small(s, ln_shape, sink_shape):
    n = ln_shape[0] * ln_shape[1] * ln_shape[2] // 128
    return s[:n].reshape(ln_shape), s[n:2 * n].reshape(ln_shape), s[2 * n, :sink_shape[1]].reshape(sink_shape)


def _ffn_fwd(xin, w_in, w_out, gain, bias, tag):
    u, h = _ffn_in(xin, w_in, "ffn_in_" + tag)
    y, yb, z = _mm_ln(h, w_out, xin, gain, bias, 0.5, "ffn_out_ln_" + tag)
    return y, yb, dict(u=u, h=h, z=z, xin=xin)


def _ffn_bwd(dy, saved, w_in, w_out, gain, xin_b, tag, dw_dtype=F32):
    dz, dzc, gg, gb = _ln_bwd(saved["z"], dy, gain, 0.5, "ln_bwd_" + tag)
    du = _ffn_bwd_h(dzc, w_out, saved["u"], "ffn_bwd_h_" + tag)
    d_w_out = _mm_tn(saved["h"], dzc, "ffn_dwout_" + tag, out_dtype=dw_dtype)
    d_w_in = _mm_tn(xin_b, du, "ffn_dwin_" + tag, out_dtype=dw_dtype)
    dx = _mm_nt(du, w_in, "ffn_dx_" + tag, add=dz, add_scale=ALPHA)
    return dx, d_w_in, d_w_out, gg, gb


def kernel(x, ffn1_w_in, ffn1_w_out, ffn2_w_in, ffn2_w_out, ln_g, ln_b, a_w_qkv, a_w_o, kv_w, b_w_q, b_sinks, b_w_o, loss_target, m_ffn1_w_in, m_ffn1_w_out, m_ffn2_w_in, m_ffn2_w_out, m_ln_g, m_ln_b, m_a_w_qkv, m_a_w_o, m_kv_w, m_b_w_q, m_b_sinks, m_b_w_o, v_ffn1_w_in, v_ffn1_w_out, v_ffn2_w_in, v_ffn2_w_out, v_ln_g, v_ln_b, v_a_w_qkv, v_a_w_o, v_kv_w, v_b_w_q, v_b_sinks, v_b_w_o):
    ws = dict(ffn1_w_in=ffn1_w_in, ffn1_w_out=ffn1_w_out, ffn2_w_in=ffn2_w_in, ffn2_w_out=ffn2_w_out, a_w_qkv=a_w_qkv,
              a_w_o=a_w_o, kv_w=kv_w, b_w_q=b_w_q, b_w_o=b_w_o)
    ms = dict(ffn1_w_in=m_ffn1_w_in, ffn1_w_out=m_ffn1_w_out, ffn2_w_in=m_ffn2_w_in, ffn2_w_out=m_ffn2_w_out,
              a_w_qkv=m_a_w_qkv, a_w_o=m_a_w_o, kv_w=m_kv_w, b_w_q=m_b_w_q, b_w_o=m_b_w_o)
    vs = dict(ffn1_w_in=v_ffn1_w_in, ffn1_w_out=v_ffn1_w_out, ffn2_w_in=v_ffn2_w_in, ffn2_w_out=v_ffn2_w_out,
              a_w_qkv=v_a_w_qkv, a_w_o=v_a_w_o, kv_w=v_kv_w, b_w_q=v_b_w_q, b_w_o=v_b_w_o)
    _, _, c_idx, myq = _place()
    xs = x[0]
    target = loss_target[0]

    shards = {(n, l): (ws[n] if l is None else ws[n][l]).astype(BF16)
              for n, l in FIRST_ITEMS + FFN2_0_ITEMS + LAYER1_ITEMS}

    def as_weights(items, arrays):
        return {n: (a.reshape(D_MODEL, a.shape[-1]) if a.ndim == 4 else a) for (n, _), a in zip(items, arrays)}

    first, small = _all_gather(FIRST_ITEMS, shards, _pack_small(ln_g, ln_b, b_sinks))
    states = {}
    states["ffn2_0"], token = _gather_start("ffn2_0", FFN2_0_ITEMS, shards, small)

    def later_weights(tag, after):
        if tag == "start_layer1":
            states["layer1"], tok = _gather_start("layer1", LAYER1_ITEMS, shards, after)
            return tok[0, 0]
        items = FFN2_0_ITEMS if tag == "ffn2_0" else LAYER1_ITEMS
        return as_weights(items, _gather_wait(tag, items, states[tag], after))

    n_ln = ln_g.size // 128
    lg = jnp.concatenate([small[q, :n_ln].reshape(DEPTH, 3, 1, -1) for q in range(N_CHIPS)], axis=-1)
    lb = jnp.concatenate([small[q, n_ln:2 * n_ln].reshape(DEPTH, 3, 1, -1) for q in range(N_CHIPS)], axis=-1)
    lg = lg + token[0, 0]
    reducer = _GradReducer(c_idx, myq, {n: ws[n].shape for n in BIG})
    sq, grad_x, _, gg, gb, dsink_part = _local_step(xs, target, as_weights(FIRST_ITEMS, first), later_weights,
                                                    lg, lb, b_sinks.reshape(N_HEADS), reducer.begin)

    loss_row = jnp.pad(jnp.sum(sq).reshape(1, 1), ((0, 0), (0, 127)))
    dsinks = jnp.pad(dsink_part[:, 0, :].reshape(N_SLABS, 2, HEAD_DIM)[:, :, 0].reshape(1, N_HEADS), ((0, 0), (0, 128 - N_HEADS)))
    gg_full = jnp.stack([jnp.stack([jnp.sum(gg[i][j], axis=0) for j in range(3)]) for i in range(DEPTH)])
    gb_full = jnp.stack([jnp.stack([jnp.sum(gb[i][j], axis=0) for j in range(3)]) for i in range(DEPTH)])
    small_in = jnp.concatenate([loss_row, dsinks, gg_full.reshape(-1, 128), gb_full.reshape(-1, 128)], axis=0)
    small_in = jnp.pad(small_in, ((0, (-small_in.shape[0]) % 8), (0, 0)))
    small_sum = _small_all_reduce(small_in)
    loss = small_sum[0, 0] * (0.5 / D_MODEL)
    grad_sinks = small_sum[1, :N_HEADS].reshape(b_sinks.shape)
    n_full = DEPTH * 3 * D_MODEL // 128
    cols = D_MODEL // N_CHIPS
    grad_ln_g = lax.dynamic_slice_in_dim(small_sum[2:2 + n_full].reshape(DEPTH, 3, D_MODEL), myq * cols, cols, axis=2)
    grad_ln_b = lax.dynamic_slice_in_dim(small_sum[2 + n_full:2 + 2 * n_full].reshape(DEPTH, 3, D_MODEL), myq * cols, cols, axis=2)
    return _update(reducer, grad_x, loss, grad_ln_g, grad_ln_b, grad_sinks, ws, ms, vs,
                   (ln_g, ln_b, b_sinks), (m_ln_g, m_ln_b, m_b_sinks), (v_ln_g, v_ln_b, v_b_sinks))


def _local_step(xs, target, W, later_weights, lg, lb, sinks, grads_ready=None):
    if grads_ready is None:
        grads_ready = lambda tag, grads, overlap: 0.0
    slopes = jnp.asarray(_alibi_slopes(N_HEADS))
    in1, out1 = [W["ffn1_w_in"]], [W["ffn1_w_out"]]

    y1, y1b, s1 = _ffn_fwd(xs, in1[0], out1[0], lg[0, 0], lb[0, 0], "a1")
    lg = lg + later_weights("start_layer1", y1b)
    qkv_a = _mm_nn(y1b, W["a_w_qkv"], F32, "qkv_a", split=True)
    mix_a, o_a, lse_a = _attn_fwd(qkv_a, slopes, None, PATTERNS_A, "attn_a_fwd")
    y2, y2b, z2 = _mm_ln(mix_a, W["a_w_o"], y1, lg[0, 1], lb[0, 1], 1.0, "attn_a_out_ln")
    W = dict(W, **later_weights("ffn2_0", y2b))
    in2, out2 = [W["ffn2_w_in"]], [W["ffn2_w_out"]]
    y3, y3b, s3 = _ffn_fwd(y2, in2[0], out2[0], lg[0, 2], lb[0, 2], "a2")
    kv_w_rep = jnp.broadcast_to(W["kv_w"].reshape(D_MODEL, 2, N_KV_B, 1, HEAD_DIM),
                                (D_MODEL, 2, N_KV_B, GROUP_B, HEAD_DIM)).reshape(D_MODEL, 2 * D_MODEL)
    kv_rep = _mm_nn(y3b, kv_w_rep, F32, "kv_proj", split=(1, 2))
    W = dict(W, **later_weights("layer1", kv_rep))
    in1, out1, in2, out2 = (in1 + [W["ffn1_w_in"]], out1 + [W["ffn1_w_out"]], in2 + [W["ffn2_w_in"]],
                            out2 + [W["ffn2_w_out"]])
    y4, y4b, s4 = _ffn_fwd(y3, in1[1], out1[1], lg[1, 0], lb[1, 0], "b1")
    qkv_b = _mm_nn(y4b, W["b_w_q"], F32, "q_b", split=(0, 1), into=kv_rep)
    mix_b, o_b, lse_b = _attn_fwd(qkv_b, slopes, sinks, PATTERNS_B, "attn_b_fwd")
    y5, y5b, z5 = _mm_ln(mix_b, W["b_w_o"], y4, lg[1, 1], lb[1, 1], 1.0, "attn_b_out_ln")
    y6, _, s6 = _ffn_fwd(y5, in2[1], out2[1], lg[1, 2], lb[1, 2], "b2")

    dy6, sq = _loss_grad(y6, target, "loss_grad")
    gr = {n: None for n in BIG}
    gg = [[None] * 3 for _ in range(DEPTH)]
    gb = [[None] * 3 for _ in range(DEPTH)]

    dy5, d_in2_b, d_out2_b, gg[1][2], gb[1][2] = _ffn_bwd(dy6, s6, in2[1], out2[1], lg[1, 2], y5b, "b2", BF16)
    dz5, dz5b, gg[1][1], gb[1][1] = _ln_bwd(z5, dy5, lg[1, 1], 1.0, "ln_bwd_attn_b")
    gr["b_w_o"] = _mm_tn(mix_b, dz5b, "d_b_w_o", out_dtype=BF16)
    dmix_b = _mm_nt(dz5b, W["b_w_o"], "d_mix_b")
    dqkv_b, dsink_part = _attn_bwd(qkv_b, dmix_b, o_b, lse_b, slopes, sinks, PATTERNS_B, "attn_b_bwd")
    dq_b = (dqkv_b, 0)
    gr["b_w_q"] = _mm_tn(y4b, dq_b, "d_b_w_q", out_dtype=BF16)
    dy4 = _mm_nt(dq_b, W["b_w_q"], "d_y4", add=dz5, add_scale=ALPHA)
    dy3, d_in1_b, d_out1_b, gg[1][0], gb[1][0] = _ffn_bwd(dy4, s4, in1[1], out1[1], lg[1, 0], y3b, "b1", BF16)
    d_kv_w_rep = _mm_tn(y3b, dqkv_b, "d_kv_w", split=(1, 2))
    gr["kv_w"] = d_kv_w_rep.reshape(D_MODEL, 2, N_KV_B, GROUP_B, HEAD_DIM).sum(axis=3).reshape(D_MODEL, -1).astype(BF16)
    dy3 = _mm_nt(dqkv_b, kv_w_rep, "d_y3_kv", add=dy3, add_scale=1.0, split=(1, 2))
    tok = grads_ready("l1", {("ffn2_w_in", 1): d_in2_b, ("ffn2_w_out", 1): d_out2_b, ("b_w_o", None): gr["b_w_o"],
                             ("b_w_q", None): gr["b_w_q"], ("ffn1_w_in", 1): d_in1_b, ("ffn1_w_out", 1): d_out1_b,
                             ("kv_w", None): gr["kv_w"]}, True)
    lg0 = lg[0] + tok

    dy2, d_in2_a, d_out2_a, gg[0][2], gb[0][2] = _ffn_bwd(dy3, s3, in2[0], out2[0], lg0[2], y2b, "a2", BF16)
    tok = grads_ready("a2", {("ffn2_w_in", 0): d_in2_a, ("ffn2_w_out", 0): d_out2_a}, True)
    lg0 = lg0 + tok
    dz2, dz2b, gg[0][1], gb[0][1] = _ln_bwd(z2, dy2, lg0[1], 1.0, "ln_bwd_attn_a")
    gr["a_w_o"] = _mm_tn(mix_a, dz2b, "d_a_w_o", out_dtype=BF16)
    dmix_a = _mm_nt(dz2b, W["a_w_o"], "d_mix_a")
    dqkv_a, _ = _attn_bwd(qkv_a, dmix_a, o_a, lse_a, slopes, None, PATTERNS_A, "attn_a_bwd")
    gr["a_w_qkv"] = _mm_tn(y1b, dqkv_a, "d_a_w_qkv", split=True, out_dtype=BF16)
    tok = grads_ready("mix", {("a_w_o", None): gr["a_w_o"], ("a_w_qkv", None): gr["a_w_qkv"]}, True)
    lg0 = lg0 + tok
    dy1 = _mm_nt(dqkv_a, W["a_w_qkv"], "d_y1", add=dz2, add_scale=ALPHA, split=True)
    grad_x, d_in1_a, d_out1_a, gg[0][0], gb[0][0] = _ffn_bwd(dy1, s1, in1[0], out1[0], lg0[0], xs, "a1", BF16)
    grads_ready("a1", {("ffn1_w_in", 0): d_in1_a, ("ffn1_w_out", 0): d_out1_a}, True)
    gr["ffn1_w_in"] = [d_in1_a, d_in1_b]
    gr["ffn1_w_out"] = [d_out1_a, d_out1_b]
    gr["ffn2_w_in"] = [d_in2_a, d_in2_b]
    gr["ffn2_w_out"] = [d_out2_a, d_out2_b]
    return sq, grad_x, gr, gg, gb, dsink_part


def _grad_item(name, layer, g):
    if name.endswith("w_in"):
        return (g, "col", HALF_FF, _slot, name, layer)
    if name.endswith("w_out"):
        return (g, "row", D_MODEL, None, name, layer)
    if name == "a_w_qkv":
        return (g, "col", QKV_SHARD, lambda q: q, name, None)
    return (g, "row", g.shape[1], None, name, None)


class _GradReducer:
    def __init__(self, c_idx, myq, shard_shapes):
        self.c_idx, self.myq, self.shard_shapes = c_idx, myq, shard_shapes
        self.groups = []

    def begin(self, tag, grads, overlap):
        items = [_grad_item(n, l, g) for (n, l), g in grads.items()]
        kinds, widths, colblocks = [it[1] for it in items], [it[2] for it in items], [it[3] for it in items]
        views = [_grad_view(k, it[0]) for k, it in zip(kinds, items)]
        if overlap:
            lands = [jax.ShapeDtypeStruct((N_DIRECT,) + _piece_shape(k, w, _half_shape(k, v.shape)), BF16)
                     for k, w, v in zip(kinds, widths, views)]
            state, token = _split_start("grad_direct_start_" + tag, _direct_copies(kinds, widths, colblocks), 10 * len(items),
                                        views, lands, views[-1])
            self.groups.append((tag, items, None, state, token))
            return token[0, 0]
        from_sibling = _pair_exchange(views, kinds, "grad_pair_exchange_" + tag)
        sums = [_pair_sum(k, v, r, self.c_idx, "pair_sum_%s_%d" % (tag, t))
                for t, (k, v, r) in enumerate(zip(kinds, views, from_sibling))]
        self.groups.append((tag, items, sums, None, None))
        return 0.0

    def _sum_group(self, tag, items, sums, received, direct):
        for t, (it, s, r) in enumerate(zip(items, sums, received)):
            _, k, _, cb, name, layer = it
            own = cb(self.myq) if k == "col" else self.myq
            self.half_done[name] = _chip_sum(k, s, r, own, self.c_idx, self.shard_shapes[name], layer,
                                             self.half_done.get(name), "chip_sum_%s_%d" % (tag, t), direct=direct)

    def finish_first(self, after):
        self.half_done, self.late, early = {}, [], []
        started = [after]
        for g, (tag, items, sums, state, token) in enumerate(self.groups):
            kinds, widths, colblocks = [it[1] for it in items], [it[2] for it in items], [it[3] for it in items]
            if state is None:
                copies = _chip_copies(kinds, widths, colblocks)
                state, token = _split_start("grad_chip_start_" + tag, copies, 3 * len(items), sums,
                                            _chip_land_shapes(sums, kinds, widths), sums[-1])
                self.late.append((tag, items, copies, state, False))
                started.append(token)
            elif g == len(self.groups) - 1:
                self.late.append((tag, items, _direct_copies(kinds, widths, colblocks), state, True))
                started.append(token)
            else:
                early.append((tag, items, _direct_copies(kinds, widths, colblocks), state))
        for tag, items, copies, state in early:
            views, received = _split_wait("grad_direct_wait_" + tag, copies, state, started)
            self._sum_group(tag, items, views, received, True)
        late_names = {it[4] for _, items, _, _, _ in self.late for it in items}
        names = [n for n in BIG if n not in late_names]
        return dict(zip(names, _share_halves([self.half_done[n] for n in names], "grad_share_halves_first")))

    def finish_rest(self, after):
        names = []
        for tag, items, copies, state, direct in self.late:
            sums, received = _split_wait("grad_late_wait_" + tag, copies, state, after)
            self._sum_group(tag, items, sums, received, direct)
            names += [it[4] for it in items if it[4] not in names]
        return dict(zip(names, _share_halves([self.half_done[n] for n in names], "grad_share_halves_rest")))


def _update(reducer, grad_x, loss, grad_ln_g, grad_ln_b, grad_sinks, ws, ms, vs, small_w, small_m, small_v):
    ln_g, ln_b, b_sinks = small_w
    m_ln_g, m_ln_b, m_b_sinks = small_m
    v_ln_g, v_ln_b, v_b_sinks = small_v

    deltas, new_m, new_v = {}, {}, {}

    def update(some):
        done = []
        for name in some:
            shp = ws[name].shape
            flat = lambda a: a.reshape(-1, shp[-1])
            d, nm, nv = _adamw(flat(ws[name]), flat(some[name]), flat(ms[name]), flat(vs[name]), "adamw_" + name)
            deltas[name], new_m[name], new_v[name] = d.reshape(shp), nm.reshape(shp), nv.reshape(shp)
            done.append(d)
        return done

    grads = reducer.finish_first(grad_x)
    rest = reducer.finish_rest(update(grads))
    update(rest)
    grads.update(rest)
    delta_s, nm_s, nv_s = _adamw(_pack_small(ln_g, ln_b, b_sinks), _pack_small(grad_ln_g, grad_ln_b, grad_sinks),
                                 _pack_small(m_ln_g, m_ln_b, m_b_sinks), _pack_small(v_ln_g, v_ln_b, v_b_sinks), "adamw_small")
    for d, blob in ((grads, None), (deltas, delta_s), (new_m, nm_s), (new_v, nv_s)):
        if blob is None:
            d["ln_g"], d["ln_b"], d["b_sinks"] = grad_ln_g, grad_ln_b, grad_sinks
        else:
            d["ln_g"], d["ln_b"], d["b_sinks"] = _unpack_small(blob, ln_g.shape, b_sinks.shape)

    order = ("ffn1_w_in", "ffn1_w_out", "ffn2_w_in", "ffn2_w_out", "ln_g", "ln_b", "a_w_qkv", "a_w_o", "kv_w", "b_w_q",
             "b_sinks", "b_w_o")
    outs = [loss, grad_x[None]]
    for d in (grads, deltas, new_m, new_v):
        outs += [d[n] for n in order]
    return tuple(outs)
```

```python
import numpy as np
import jax
import jax.numpy as jnp
from jax import lax
from jax.experimental import pallas as pl
from jax.experimental.pallas import tpu as pltpu

F32 = jnp.float32
BF16 = jnp.bfloat16

D_MODEL = 1024
D_FF = 2816
HALF_FF = D_FF // 2
HEAD_DIM = 64
N_HEADS = 16
N_KV_B = 4
GROUP_B = N_HEADS // N_KV_B
DEPTH = 2
ALPHA = (2.0 * DEPTH) ** 0.25
LN_EPS = 1e-5
BLOCK = 128
SLAB = 128
N_SLABS = D_MODEL // SLAB
PATTERNS_A = ((1, 128, 1.0), (4, 128, 4.0), (16, 128, 16.0))
PATTERNS_B = ((1, 127, 1.0),)
NEG = -1e30

ADAM_LR = 0.001
ADAM_B1 = 0.9
ADAM_B2 = 0.999
ADAM_EPS = 1e-08
ADAM_WD = 0.01
ADAM_STEP = 10

N_CHIPS = 4
VMEM_LIMIT = 56 * 1024 * 1024
MESH = pl.DeviceIdType.MESH


def _alibi_slopes(n):
    return np.array([2.0 ** (-8.0 * (h + 1) / n) for h in range(n)], dtype=np.float32)


def _cparams(sem=None, vmem=VMEM_LIMIT):
    return pltpu.CompilerParams(dimension_semantics=sem, vmem_limit_bytes=vmem)


_DIMS = {"nn": ((1,), (0,)), "nt": ((1,), (1,)), "tn": ((0,), (0,))}


def _unlead(x):
    if isinstance(x, tuple):
        return x[0], x[1], x[0].shape[1:]
    return x, None, x.shape


def _bspec(block, imap, lead=None):
    if lead is None:
        return pl.BlockSpec(block, imap)
    return pl.BlockSpec((None,) + tuple(block), lambda *g: (lead,) + tuple(imap(*g)))


def _ln_bwd_math(zv, dyv, gain):
    rows = zv.shape[0]
    mu = jnp.mean(zv, axis=-1, keepdims=True)
    zc = zv - mu
    var = jnp.mean(zc * zc, axis=-1, keepdims=True)
    rstd = lax.rsqrt(var + LN_EPS)
    xhat = zc * rstd
    dyg = dyv * gain
    m1 = jnp.mean(dyg, axis=-1, keepdims=True)
    m2 = jnp.mean(dyg * xhat, axis=-1, keepdims=True)
    dz = rstd * (dyg - m1 - xhat * m2)
    pg = jnp.sum((dyv * xhat).reshape(rows // 8, 8, D_MODEL), axis=0)
    pb = jnp.sum(dyv.reshape(rows // 8, 8, D_MODEL), axis=0)
    return dz, pg, pb


def _matmul(a, b, mode, out_dtype, tm, tn, tk, name, add=None, add_scale=1.0, split=False, into=None, ln=None):
    out_spec = pl.BlockSpec((tm, tn), lambda i, j, k: (i, j))
    base, count = (0, 3) if split is True else (split or (0, 0))
    if mode == "nn":
        a, al, (M, K) = _unlead(a)
        b, bl, (K2, N) = _unlead(b)
        a_spec = _bspec((tm, tk), lambda i, j, k: (i, k), al)
        b_spec = _bspec((tk, tn), lambda i, j, k: (k, j), bl)
        out_struct = jax.ShapeDtypeStruct((M, N), out_dtype)
        if split:
            assert tn == D_MODEL and N == count * tn
            out_spec = pl.BlockSpec((None, tm, tn), lambda i, j, k: (j + base, i, 0))
            out_struct = jax.ShapeDtypeStruct((3, M, tn), out_dtype)
    elif mode == "nt":
        b, bl, (N, K2) = _unlead(b)
        if split:
            assert tk == D_MODEL
            M, K = a.shape[1], count * a.shape[2]
            a_spec = pl.BlockSpec((None, tm, tk), lambda i, j, k: (k + base, i, 0))
        else:
            a, al, (M, K) = _unlead(a)
            a_spec = _bspec((tm, tk), lambda i, j, k: (i, k), al)
        b_spec = _bspec((tn, tk), lambda i, j, k: (j, k), bl)
        out_struct = jax.ShapeDtypeStruct((M, N), out_dtype)
    else:
        a, al, (K, M) = _unlead(a)
        if split:
            assert tn == D_MODEL
            K2, N = b.shape[1], count * b.shape[2]
            b_spec = pl.BlockSpec((None, tk, tn), lambda i, j, k: (j + base, k, 0))
        else:
            b, bl, (K2, N) = _unlead(b)
            b_spec = _bspec((tk, tn), lambda i, j, k: (k, j), bl)
        a_spec = _bspec((tk, tm), lambda i, j, k: (k, i), al)
        out_struct = jax.ShapeDtypeStruct((M, N), out_dtype)
    assert K == K2 and M % tm == 0 and N % tn == 0 and K % tk == 0, (a.shape, b.shape, mode, tm, tn, tk)
    nk = K // tk
    dims = (_DIMS[mode], ((), ()))
    has_add = add is not None

    narrow = out_dtype != F32
    assert not (narrow and has_add)
    if ln is not None:
        assert has_add and mode == "nt" and tn == N == D_MODEL

    def body(*refs):
        if into is not None:
            refs = refs[:2] + refs[3:]
        if ln is not None:
            a_ref, b_ref, add_ref, z_ref, g_ref, o_ref, dzc_ref, gg_ref, gb_ref = refs
            acc_ref = o_ref
        elif has_add:
            a_ref, b_ref, add_ref, o_ref = refs
            acc_ref = o_ref
        elif narrow:
            a_ref, b_ref, o_ref, acc_ref = refs
        else:
            a_ref, b_ref, o_ref = refs
            acc_ref = o_ref
        k = pl.program_id(2)
        part = lax.dot_general(a_ref[...].astype(BF16), b_ref[...].astype(BF16), dims, preferred_element_type=F32)
        if has_add:
            @pl.when(k == 0)
            def _():
                acc_ref[...] = part + add_scale * add_ref[...]
        else:
            @pl.when(k == 0)
            def _():
                acc_ref[...] = part

        @pl.when(k > 0)
        def _():
            acc_ref[...] += part

        if narrow:
            @pl.when(k == nk - 1)
            def _():
                o_ref[...] = acc_ref[...].astype(out_dtype)

        if ln is not None:
            @pl.when(k == nk - 1)
            def _():
                dz, pg, pb = _ln_bwd_math(z_ref[...], o_ref[...], g_ref[...])
                o_ref[...] = dz
                dzc_ref[...] = (ln[2] * dz).astype(BF16)
                first = pl.program_id(0) == 0

                @pl.when(first)
                def _():
                    gg_ref[...] = pg
                    gb_ref[...] = pb

                @pl.when(jnp.logical_not(first))
                def _():
                    gg_ref[...] += pg
                    gb_ref[...] += pb

    in_specs = [a_spec, b_spec]
    args = [a, b]
    aliases = {}
    if into is not None:
        assert mode == "nn" and split and not has_add
        in_specs.append(pl.BlockSpec(memory_space=pl.ANY))
        args.append(into)
        aliases = {2: 0}
    if has_add:
        in_specs.append(pl.BlockSpec((tm, tn), lambda i, j, k: (i, j)))
        args.append(add)
    sem = ("parallel", "parallel", "arbitrary")
    if ln is not None:
        part8 = pl.BlockSpec((8, N), lambda i, j, k: (0, 0))
        in_specs += [pl.BlockSpec((tm, tn), lambda i, j, k: (i, j)), pl.BlockSpec((1, N), lambda i, j, k: (0, 0))]
        args += [ln[0], ln[1]]
        out_spec = [out_spec, pl.BlockSpec((tm, tn), lambda i, j, k: (i, j)), part8, part8]
        out_struct = [out_struct, jax.ShapeDtypeStruct((M, N), BF16), jax.ShapeDtypeStruct((8, N), F32),
                      jax.ShapeDtypeStruct((8, N), F32)]
        sem = ("arbitrary", "arbitrary", "arbitrary")
    return pl.pallas_call(
        body, name=name, grid=(M // tm, N // tn, nk),
        in_specs=in_specs, out_specs=out_spec, out_shape=out_struct, input_output_aliases=aliases,
        scratch_shapes=[pltpu.VMEM((tm, tn), F32)] if narrow else [],
        compiler_params=_cparams(sem),
    )(*args)


def _pick(n, cands):
    for c in cands:
        if n % c == 0:
            return c
    raise ValueError((n, cands))


def _mm_nn(a, b, out_dtype, name, split=False, into=None):
    M, K = _unlead(a)[2]
    N = _unlead(b)[2][1]
    return _matmul(a, b, "nn", out_dtype, _pick(M, (1024, 512, 256)), _pick(N, (1024, 512)), _pick(K, (1024, 512)), name,
                   split=split, into=into)


def _mm_nt(a, b, name, add=None, add_scale=1.0, split=False, ln=None):
    M, K = (a.shape[1], D_MODEL) if split else _unlead(a)[2]
    N = _unlead(b)[2][0]
    tms = (512, 256) if ln is not None else (1024, 512, 256)
    return _matmul(a, b, "nt", F32, _pick(M, tms), _pick(N, (1024, 512)),
                   _pick(K, (2816, 1024, 512)), name, add=add, add_scale=add_scale, split=split, ln=ln)


def _mm_tn(a, b, name, split=False, out_dtype=F32):
    K, M = _unlead(a)[2]
    N = D_MODEL if split else _unlead(b)[2][1]
    return _matmul(a, b, "tn", out_dtype, _pick(M, (1024, 1408, 512)), _pick(N, (1408, 1024, 512)),
                   _pick(K, (2048, 1024, 512, 256)), name, split=split)


def _ffn_in(x, w, name):
    S = x.shape[0]
    tm = _pick(S, (512, 256))
    w, wl, _ = _unlead(w)

    def body(x_ref, w_ref, t_ref, h_ref):
        acc = jnp.dot(x_ref[...].astype(BF16), w_ref[...], preferred_element_type=F32)
        g = acc[:, :HALF_FF]
        up = acc[:, HALF_FF:]
        sg = jax.nn.sigmoid(g)
        silu = g * sg
        t_ref[:, :HALF_FF] = (up * (sg * (1.0 + g * (1.0 - sg)))).astype(BF16)
        t_ref[:, HALF_FF:] = silu.astype(BF16)
        h_ref[...] = (silu * up).astype(BF16)

    return pl.pallas_call(
        body, name=name, grid=(2, S // tm),
        in_specs=[pl.BlockSpec((tm, D_MODEL), lambda j, i: (i, 0)),
                  _bspec((D_MODEL, D_FF), lambda j, i: (0, j), wl)],
        out_specs=[pl.BlockSpec((tm, D_FF), lambda j, i: (i, j)),
                   pl.BlockSpec((tm, HALF_FF), lambda j, i: (i, j))],
        out_shape=[jax.ShapeDtypeStruct((S, 2 * D_FF), BF16), jax.ShapeDtypeStruct((S, D_FF), BF16)],
        compiler_params=_cparams(("parallel", "parallel")),
    )(x, w)


def _ffn_bwd_h(dzc, w_out, u, name):
    S = dzc.shape[0]
    tm = _pick(S, (512, 256))
    w_out, wl, _ = _unlead(w_out)

    def body(dz_ref, w_ref, t_ref, du_ref):
        dh = lax.dot_general(dz_ref[...], w_ref[...], (((1,), (1,)), ((), ())), preferred_element_type=F32)
        du_ref[:, :HALF_FF] = (dh * t_ref[:, :HALF_FF].astype(F32)).astype(BF16)
        du_ref[:, HALF_FF:] = (dh * t_ref[:, HALF_FF:].astype(F32)).astype(BF16)

    return pl.pallas_call(
        body, name=name, grid=(2, S // tm),
        in_specs=[pl.BlockSpec((tm, D_MODEL), lambda j, i: (i, 0)),
                  _bspec((HALF_FF, D_MODEL), lambda j, i: (j, 0), wl),
                  pl.BlockSpec((tm, D_FF), lambda j, i: (i, j))],
        out_specs=pl.BlockSpec((tm, D_FF), lambda j, i: (i, j)),
        out_shape=jax.ShapeDtypeStruct((S, 2 * D_FF), BF16),
        compiler_params=_cparams(("parallel", "parallel")),
    )(dzc, w_out, u)


def _mm_ln(a, w, resid, gain, bias, c, name):
    S, K = a.shape
    tm = _pick(S, (512, 256))
    w, wl, _ = _unlead(w)

    def body(a_ref, w_ref, r_ref, g_ref, b_ref, y_ref, yb_ref, z_ref):
        z = ALPHA * r_ref[...] + c * jnp.dot(a_ref[...], w_ref[...], preferred_element_type=F32)
        mu = jnp.mean(z, axis=-1, keepdims=True)
        zc = z - mu
        var = jnp.mean(zc * zc, axis=-1, keepdims=True)
        y = zc * lax.rsqrt(var + LN_EPS) * g_ref[...] + b_ref[...]
        z_ref[...] = z
        y_ref[...] = y
        yb_ref[...] = y.astype(BF16)

    row = pl.BlockSpec((tm, D_MODEL), lambda i: (i, 0))
    vec = pl.BlockSpec((1, D_MODEL), lambda i: (0, 0))
    return pl.pallas_call(
        body, name=name, grid=(S // tm,),
        in_specs=[pl.BlockSpec((tm, K), lambda i: (i, 0)), _bspec((K, D_MODEL), lambda i: (0, 0), wl), row, vec, vec],
        out_specs=[row, row, row],
        out_shape=[jax.ShapeDtypeStruct((S, D_MODEL), F32), jax.ShapeDtypeStruct((S, D_MODEL), BF16),
                   jax.ShapeDtypeStruct((S, D_MODEL), F32)],
        compiler_params=_cparams(("parallel",)),
    )(a, w, resid, gain, bias)


def _loss_ln_bwd(y, t, z, gain, c, name):
    S = y.shape[0]
    tm = _pick(S, (512, 256))

    def body(y_ref, t_ref, z_ref, g_ref, dz_ref, dzc_ref, gg_ref, gb_ref, sq_ref):
        i = pl.program_id(0)
        e = y_ref[...] - t_ref[...]
        dz, pg, pb = _ln_bwd_math(z_ref[...], e * (1.0 / D_MODEL), g_ref[...])
        dz_ref[...] = dz
        dzc_ref[...] = (c * dz).astype(BF16)
        ps = jnp.sum((e * e).reshape(tm // 8, 8, D_MODEL), axis=0)

        @pl.when(i == 0)
        def _():
            gg_ref[...] = pg
            gb_ref[...] = pb
            sq_ref[...] = ps

        @pl.when(i > 0)
        def _():
            gg_ref[...] += pg
            gb_ref[...] += pb
            sq_ref[...] += ps

    row = pl.BlockSpec((tm, D_MODEL), lambda i: (i, 0))
    part = pl.BlockSpec((8, D_MODEL), lambda i: (0, 0))
    part_shape = jax.ShapeDtypeStruct((8, D_MODEL), F32)
    return pl.pallas_call(
        body, name=name, grid=(S // tm,),
        in_specs=[row, row, row, pl.BlockSpec((1, D_MODEL), lambda i: (0, 0))],
        out_specs=[row, row, part, part, part],
        out_shape=[jax.ShapeDtypeStruct((S, D_MODEL), F32), jax.ShapeDtypeStruct((S, D_MODEL), BF16),
                   part_shape, part_shape, part_shape],
        compiler_params=_cparams(("arbitrary",)),
    )(y, t, z, gain)


def _rows(start, d):
    if d == 1:
        return pl.ds(pl.multiple_of(start, BLOCK), BLOCK)
    return pl.ds(start, BLOCK, stride=d)


def _ld(ref, start, d):
    return ref[_rows(start, d), :]


def _ld3(ref, lead, start, d):
    return ref[lead, _rows(start, d), :]


def _st3(ref, lead, start, d, val):
    ref[lead, _rows(start, d), :] = val


def _acc3(ref, lead, start, d, val):
    ref[lead, _rows(start, d), :] = ref[lead, _rows(start, d), :] + val


def _band_consts(slope0, slope1, maxd, scale):
    row = lax.broadcasted_iota(jnp.int32, (2 * BLOCK, 2 * BLOCK), 0)
    kj = lax.broadcasted_iota(jnp.int32, (2 * BLOCK, 2 * BLOCK), 1)
    top = row < BLOCK
    dist = BLOCK + jnp.where(top, row, row - BLOCK) - kj
    slope = jnp.where(top, slope0, slope1)
    base = jnp.where((dist >= 0) & (dist <= maxd), -(slope * (dist.astype(F32) * scale)), NEG)
    return base, kj < BLOCK


def _stack_heads(x, lo):
    return jnp.concatenate([jnp.where(lo, x, 0.0), jnp.where(lo, 0.0, x)], axis=0)


def _unstack_heads(x2, lo):
    return jnp.where(lo, x2[:BLOCK], x2[BLOCK:])


def _scores(q2, k2, base, prev_keys, first):
    s = lax.dot_general(q2, k2, (((1,), (1,)), ((), ())), preferred_element_type=F32) * (HEAD_DIM ** -0.5) + base
    return jnp.where(jnp.logical_and(prev_keys, first), NEG, s)


def _softmax_weights(ls):
    mx = ls[0]
    for l in ls[1:]:
        mx = jnp.maximum(mx, l)
    es = [jnp.exp(l - mx) for l in ls]
    tot = es[0]
    for e in es[1:]:
        tot = tot + e
    inv = 1.0 / tot
    return [e * inv for e in es]


def _attn_fwd(qkv, slopes, sinks, patterns, name):
    S = qkv.shape[1]
    npat = len(patterns)
    has_sink = sinks is not None
    if not has_sink:
        sinks = jnp.zeros((N_HEADS,), F32)
    rows_c = 256

    def body(slopes_ref, sinks_ref, x_ref, mix_ref, o_ref, lse_ref, o_scr, lse_scr):
        p = pl.program_id(0)
        lo = lax.broadcasted_iota(jnp.int32, (BLOCK, SLAB), 1) < HEAD_DIM
        top1 = lax.broadcasted_iota(jnp.int32, (2 * BLOCK, 1), 0) < BLOCK
        sk2 = jnp.where(top1, sinks_ref[2 * p], sinks_ref[2 * p + 1])
        for pi, (d, maxd, scale) in enumerate(patterns):
            nb = S // d // BLOCK
            base, prev_keys = _band_consts(slopes_ref[2 * p], slopes_ref[2 * p + 1], maxd, scale)

            def blk(t, carry, pi=pi, d=d, nb=nb, base=base, prev_keys=prev_keys):
                r = t // nb
                n = t - r * nb
                start = r + (d * BLOCK) * n
                prev = jnp.where(n > 0, start - d * BLOCK, start)
                q2 = _stack_heads(_ld3(x_ref, 0, start, d), lo).astype(BF16)
                k2 = jnp.concatenate([_ld3(x_ref, 1, prev, d), _ld3(x_ref, 1, start, d)], axis=0).astype(BF16)
                v2 = jnp.concatenate([_ld3(x_ref, 2, prev, d), _ld3(x_ref, 2, start, d)], axis=0).astype(BF16)
                s = _scores(q2, k2, base, prev_keys, n == 0)
                m = jnp.max(s, axis=-1, keepdims=True)
                if has_sink:
                    m = jnp.maximum(m, sk2)
                e = jnp.exp(s - m)
                den = jnp.sum(e, axis=-1, keepdims=True)
                if has_sink:
                    den = den + jnp.exp(sk2 - m)
                o2 = jnp.dot((e / den).astype(BF16), v2, preferred_element_type=F32)
                _st3(o_scr, pi, start, d, _unstack_heads(o2, lo))
                _st3(lse_scr, pi, start, d, _unstack_heads(m + jnp.log(den), lo))
                return carry

            lax.fori_loop(0, d * nb, blk, 0, unroll=8)

        lane_c = lax.broadcasted_iota(jnp.int32, (rows_c, SLAB), 1)

        def comb(ci, carry):
            rows = pl.ds(pl.multiple_of(ci * rows_c, rows_c), rows_c)
            ls = [lse_scr[i, rows, :] for i in range(npat)]
            packed = jnp.zeros((rows_c, SLAB), F32)
            for i in range(npat):
                o_ref[i, rows, :] = o_scr[i, rows, :].astype(BF16)
                packed = jnp.where(lane_c % HEAD_DIM == i, ls[i], packed)
            lse_ref[rows, :] = packed
            if npat == 1:
                mix_ref[rows, :] = o_scr[0, rows, :].astype(BF16)
            else:
                ws = _softmax_weights(ls)
                acc = ws[0] * o_scr[0, rows, :]
                for i in range(1, npat):
                    acc = acc + ws[i] * o_scr[i, rows, :]
                mix_ref[rows, :] = acc.astype(BF16)
            return carry

        lax.fori_loop(0, S // rows_c, comb, 0, unroll=2)

    smem = pl.BlockSpec(memory_space=pltpu.SMEM)
    return pl.pallas_call(
        body, name=name, grid=(N_SLABS,),
        in_specs=[smem, smem, pl.BlockSpec((3, S, SLAB), lambda p: (0, 0, p))],
        out_specs=[pl.BlockSpec((S, SLAB), lambda p: (0, p)), pl.BlockSpec((npat, S, SLAB), lambda p: (0, 0, p)),
                   pl.BlockSpec((None, S, SLAB), lambda p: (p, 0, 0))],
        out_shape=[jax.ShapeDtypeStruct((S, D_MODEL), BF16), jax.ShapeDtypeStruct((npat, S, D_MODEL), BF16),
                   jax.ShapeDtypeStruct((N_SLABS, S, SLAB), F32)],
        scratch_shapes=[pltpu.VMEM((npat, S, SLAB), F32), pltpu.VMEM((npat, S, SLAB), F32)],
        compiler_params=_cparams(("arbitrary",)),
    )(slopes, sinks, qkv)


def _attn_bwd(qkv, dout, o, lse, slopes, sinks, patterns, name):
    S = qkv.shape[1]
    npat = len(patterns)
    has_sink = sinks is not None
    if not has_sink:
        sinks = jnp.zeros((N_HEADS,), F32)
    rows_c = 256

    def headsum(x, lo):
        same = (lax.broadcasted_iota(jnp.int32, (SLAB, SLAB), 0) < HEAD_DIM) == (lax.broadcasted_iota(jnp.int32, (SLAB, SLAB), 1) < HEAD_DIM)
        return jnp.dot(x, same.astype(F32), precision=lax.Precision.HIGH, preferred_element_type=F32)

    def body(slopes_ref, sinks_ref, x_ref, do_ref, o_ref, lsep_ref, dxo_ref, dsink_ref, dbar_ref, sacc_ref, lse_ref, dx_ref):
        p = pl.program_id(0)
        lo = lax.broadcasted_iota(jnp.int32, (BLOCK, SLAB), 1) < HEAD_DIM
        lo_c = lax.broadcasted_iota(jnp.int32, (rows_c, SLAB), 1) < HEAD_DIM
        top1 = lax.broadcasted_iota(jnp.int32, (2 * BLOCK, 1), 0) < BLOCK
        sk2 = jnp.where(top1, sinks_ref[2 * p], sinks_ref[2 * p + 1])

        def prep(ci, carry):
            rows = pl.ds(pl.multiple_of(ci * rows_c, rows_c), rows_c)
            dov = do_ref[rows, :]
            dx_ref[:, rows, :] = jnp.zeros((3, rows_c, SLAB), F32)
            packed = lsep_ref[rows, :]
            ls = [jnp.where(lo_c, packed[:, i:i + 1], packed[:, HEAD_DIM + i:HEAD_DIM + i + 1]) for i in range(npat)]
            for i in range(npat):
                lse_ref[i, rows, :] = ls[i]
            if npat == 1:
                dbar_ref[rows, :] = headsum(dov * o_ref[0, rows, :].astype(F32), lo_c)
            else:
                ws = _softmax_weights(ls)
                acc = ws[0] * headsum(dov * o_ref[0, rows, :].astype(F32), lo_c)
                for i in range(1, npat):
                    acc = acc + ws[i] * headsum(dov * o_ref[i, rows, :].astype(F32), lo_c)
                dbar_ref[rows, :] = acc
            return carry

        lax.fori_loop(0, S // rows_c, prep, 0, unroll=2)
        sacc_ref[...] = jnp.zeros((BLOCK, SLAB), F32)

        for pi, (d, maxd, scale) in enumerate(patterns):
            nb = S // d // BLOCK
            base, prev_keys = _band_consts(slopes_ref[2 * p], slopes_ref[2 * p + 1], maxd, scale)

            def blk(t, carry, pi=pi, d=d, nb=nb, base=base, prev_keys=prev_keys):
                r = t // nb
                n = t - r * nb
                start = r + (d * BLOCK) * n
                prev = jnp.where(n > 0, start - d * BLOCK, start)
                q2 = _stack_heads(_ld3(x_ref, 0, start, d), lo).astype(BF16)
                k2 = jnp.concatenate([_ld3(x_ref, 1, prev, d), _ld3(x_ref, 1, start, d)], axis=0).astype(BF16)
                v2 = jnp.concatenate([_ld3(x_ref, 2, prev, d), _ld3(x_ref, 2, start, d)], axis=0).astype(BF16)
                ls = [_ld3(lse_ref, i, start, d) for i in range(npat)]
                w = _softmax_weights(ls)[pi] if npat > 1 else 1.0
                do2 = _stack_heads(w * _ld(do_ref, start, d), lo).astype(BF16)
                dl = w * _ld(dbar_ref, start, d)
                lse2 = jnp.concatenate([ls[pi][:, :1], ls[pi][:, HEAD_DIM:HEAD_DIM + 1]], axis=0)
                dl2 = jnp.concatenate([dl[:, :1], dl[:, HEAD_DIM:HEAD_DIM + 1]], axis=0)
                s = _scores(q2, k2, base, prev_keys, n == 0)
                pr = jnp.exp(s - lse2)
                dp = lax.dot_general(do2, v2, (((1,), (1,)), ((), ())), preferred_element_type=F32)
                ds = (pr * (dp - dl2) * (HEAD_DIM ** -0.5)).astype(BF16)
                dq2 = jnp.dot(ds, k2, preferred_element_type=F32)
                dk2 = lax.dot_general(ds, q2, (((0,), (0,)), ((), ())), preferred_element_type=F32)
                dv2 = lax.dot_general(pr.astype(BF16), do2, (((0,), (0,)), ((), ())), preferred_element_type=F32)
                _acc3(dx_ref, 0, start, d, _unstack_heads(dq2, lo))
                _acc3(dx_ref, 1, prev, d, dk2[:BLOCK])
                _acc3(dx_ref, 1, start, d, dk2[BLOCK:])
                _acc3(dx_ref, 2, prev, d, dv2[:BLOCK])
                _acc3(dx_ref, 2, start, d, dv2[BLOCK:])
                if has_sink:
                    sacc_ref[...] += _unstack_heads(-jnp.exp(sk2 - lse2) * dl2, lo)
                return carry

            lax.fori_loop(0, d * nb, blk, 0, unroll=8)

        dsink_ref[...] = jnp.broadcast_to(jnp.sum(sacc_ref[...], axis=0, keepdims=True), (8, SLAB))

        def emit(ci, carry):
            rows = pl.ds(pl.multiple_of(ci * rows_c, rows_c), rows_c)
            dxo_ref[:, rows, :] = dx_ref[:, rows, :].astype(BF16)
            return carry

        lax.fori_loop(0, S // rows_c, emit, 0, unroll=2)

    smem = pl.BlockSpec(memory_space=pltpu.SMEM)
    return pl.pallas_call(
        body, name=name, grid=(N_SLABS,),
        in_specs=[smem, smem, pl.BlockSpec((3, S, SLAB), lambda p: (0, 0, p)), pl.BlockSpec((S, SLAB), lambda p: (0, p)),
                  pl.BlockSpec((npat, S, SLAB), lambda p: (0, 0, p)), pl.BlockSpec((None, S, SLAB), lambda p: (p, 0, 0))],
        out_specs=[pl.BlockSpec((3, S, SLAB), lambda p: (0, 0, p)), pl.BlockSpec((None, 8, SLAB), lambda p: (p, 0, 0))],
        out_shape=[jax.ShapeDtypeStruct((3, S, D_MODEL), BF16), jax.ShapeDtypeStruct((N_SLABS, 8, SLAB), F32)],
        scratch_shapes=[pltpu.VMEM((S, SLAB), F32), pltpu.VMEM((BLOCK, SLAB), F32), pltpu.VMEM((npat, S, SLAB), F32),
                        pltpu.VMEM((3, S, SLAB), F32)],
        compiler_params=_cparams(("arbitrary",)),
    )(slopes, sinks, qkv, dout, o, lse)


def _place():
    x, y, c = lax.axis_index("x"), lax.axis_index("y"), lax.axis_index("c")
    return x, y, c, 2 * x + y


def _other_chips(x, y):
    return [(1 - x, y), (x, 1 - y), (1 - x, 1 - y)]


HBM_SPEC = pl.BlockSpec(memory_space=pl.ANY)


def _slot(q):
    return 2 * (q % 2) + q // 2


BIG = ("ffn1_w_in", "ffn1_w_out", "ffn2_w_in", "ffn2_w_out", "a_w_qkv", "a_w_o", "kv_w", "b_w_q", "b_w_o")
QKV_SHARD = 3 * D_MODEL // N_CHIPS
ROW_SHARD = D_MODEL // N_CHIPS


LAYER0_ITEMS = (("ffn1_w_in", 0), ("ffn1_w_out", 0), ("a_w_qkv", None), ("a_w_o", None), ("ffn2_w_in", 0),
                ("ffn2_w_out", 0), ("kv_w", None))
LAYER1_ITEMS = (("ffn1_w_in", 1), ("ffn1_w_out", 1), ("b_w_q", None), ("b_w_o", None), ("ffn2_w_in", 1),
                ("ffn2_w_out", 1))
OUT_SHARD = D_FF // N_CHIPS


def _full_shape(name):
    if name.endswith("w_in"):
        return (D_MODEL, 2 * D_FF)
    if name.endswith("w_out"):
        return (D_FF, D_MODEL)
    if name == "a_w_qkv":
        return (D_MODEL, 3 * D_MODEL)
    if name == "kv_w":
        return (N_CHIPS, 2, ROW_SHARD // 2, 2 * N_KV_B * HEAD_DIM)
    return (N_CHIPS, 2, ROW_SHARD // 2, D_MODEL)


def _gather_src(item, ref, c):
    name, _ = item
    if name.endswith("w_in"):
        return ref.at[pl.ds(c * (D_MODEL // 2), D_MODEL // 2)]
    if name.endswith("w_out"):
        return ref.at[pl.ds(c * (OUT_SHARD // 2), OUT_SHARD // 2)]
    if name == "a_w_qkv":
        return ref.at[0, pl.ds(c * (D_MODEL // 2), D_MODEL // 2)]
    if name == "kv_w":
        return ref.at[pl.ds(c * (ROW_SHARD // 2), ROW_SHARD // 2)]
    return ref.at[0, pl.ds(c * (ROW_SHARD // 2), ROW_SHARD // 2)]


def _gather_dst(item, ref, q, c):
    name, _ = item
    if name.endswith("w_in"):
        return ref.at[pl.ds(c * (D_MODEL // 2), D_MODEL // 2), pl.ds(_slot(q) * HALF_FF, HALF_FF)]
    if name.endswith("w_out"):
        return ref.at[pl.ds(q * OUT_SHARD + c * (OUT_SHARD // 2), OUT_SHARD // 2)]
    if name == "a_w_qkv":
        return ref.at[pl.ds(c * (D_MODEL // 2), D_MODEL // 2), pl.ds(q * QKV_SHARD, QKV_SHARD)]
    return ref.at[q, c]


def _all_gather(items, shards, small):
    n = len(items)
    r = small.shape[0]
    per = 8

    def body(*refs):
        srcs, small_ref = refs[:n], refs[n]
        dsts, s_ref = refs[n + 1:2 * n + 1], refs[2 * n + 1]
        send_sems, recv_sems = refs[2 * n + 2:]
        x, y, c, myq = _place()
        sibling = (x, y, 1 - c)
        chips = _other_chips(x, y)

        def big(t, k, src, q, h, to):
            return pltpu.make_async_remote_copy(src_ref=src, dst_ref=_gather_dst(items[t], dsts[t], q, h),
                                                send_sem=send_sems.at[per * t + k], recv_sem=recv_sems.at[per * t + k],
                                                device_id=to, device_id_type=MESH)

        def tiny(k, q, to):
            return pltpu.make_async_remote_copy(src_ref=small_ref, dst_ref=s_ref.at[q], send_sem=send_sems.at[per * n + k],
                                                recv_sem=recv_sems.at[per * n + k], device_id=to, device_id_type=MESH)

        first = []
        for j, chip in enumerate(chips):
            if j < 2:
                first += [big(t, j, _gather_src(items[t], srcs[t], c), myq, c, (*chip, c)) for t in range(n)]
            first.append(tiny(j, myq, (*chip, c)))
        own = [big(t, 6 + h, _gather_src(items[t], srcs[t], h), myq, h, sibling) for t in range(n) for h in (0, 1)]
        own.append(tiny(3, myq, sibling))
        for cp in first + own:
            cp.start()
        relay_from = ((x + 1 - c) % 2, (y + c) % 2)
        relay_to = ((x + c) % 2, (y + 1 - c) % 2, c)
        q_relay = 2 * relay_from[0] + relay_from[1]
        passed = []
        for t in range(n):
            src = _gather_src(items[t], srcs[t], c)
            for j, (cx, cy) in enumerate(chips[:2]):
                q = 2 * cx + cy
                big(t, j, src, q, c, sibling).wait_recv()
                fwd = big(t, 3 + j, _gather_dst(items[t], dsts[t], q, c), q, c, sibling)
                fwd.start()
                passed.append(fwd)
            relay = big(t, 2, _gather_dst(items[t], dsts[t], q_relay, c), q_relay, c, relay_to)
            relay.start()
            passed.append(relay)
        q_diag = 2 * chips[2][0] + chips[2][1]
        for t in range(n):
            big(t, 2, _gather_src(items[t], srcs[t], c), q_diag, c, sibling).wait_recv()
            fwd = big(t, 5, _gather_dst(items[t], dsts[t], q_diag, c), q_diag, c, sibling)
            fwd.start()
            passed.append(fwd)
        for j, (cx, cy) in enumerate(chips):
            q = 2 * cx + cy
            for t in range(n):
                big(t, 3 + j, _gather_src(items[t], srcs[t], c), q, 1 - c, sibling).wait_recv()
            tiny(j, q, sibling).wait_recv()
        for cp in own:
            cp.wait_recv()
        for cp in first + passed + own:
            cp.wait_send()

    outs = pl.pallas_call(
        body, name="all_gather_layer0",
        in_specs=[HBM_SPEC] * (n + 1), out_specs=[HBM_SPEC] * (n + 1),
        out_shape=[jax.ShapeDtypeStruct(_full_shape(name), BF16) for name, _ in items]
        + [jax.ShapeDtypeStruct((N_CHIPS, r, 128), F32)],
        scratch_shapes=[pltpu.SemaphoreType.DMA((per * n + 4,)), pltpu.SemaphoreType.DMA((per * n + 4,))],
    )(*[shards[item] for item in items], small)
    return list(outs[:n]), outs[n]


SEM_SPEC = pl.BlockSpec(memory_space=pltpu.SEMAPHORE)
DATAFLOW = pltpu.SideEffectType.DATAFLOW_SIDE_EFFECTING
PER_ITEM = 8


def _split_start(name, copies, n_sems, sources, land_shapes, after):
    n, m = len(sources), len(land_shapes)

    def body(*refs):
        srcs, lands = refs[:n], refs[n:n + m]
        send_sems, recv_sems = refs[n + m + 1], refs[n + m + 2]
        token = refs[-1]
        for src, dst_there, _, s, peer in copies(srcs, lands):
            pltpu.make_async_remote_copy(src_ref=src, dst_ref=dst_there, send_sem=send_sems.at[s], recv_sem=recv_sems.at[s],
                                         device_id=peer, device_id_type=MESH).start()
        token[...] = jnp.zeros_like(token)

    src_arrays = [pltpu.with_memory_space_constraint(a, pltpu.HBM) for a in sources]
    land_arrays = [pltpu.with_memory_space_constraint(lax.empty(s.shape, s.dtype), pltpu.HBM) for s in land_shapes]
    hbm = pl.BlockSpec(memory_space=pltpu.HBM)
    outs = pl.pallas_call(
        body, name=name,
        in_specs=[hbm] * (n + m) + [HBM_SPEC],
        out_specs=[SEM_SPEC, SEM_SPEC] + [hbm] * (n + m) + [pl.BlockSpec(memory_space=pltpu.VMEM)],
        out_shape=[pltpu.SemaphoreType.DMA((n_sems,)), pltpu.SemaphoreType.DMA((n_sems,))]
        + [pltpu.HBM(a.shape, a.dtype) for a in src_arrays + land_arrays] + [jax.ShapeDtypeStruct((8, 128), F32)],
        input_output_aliases={i: 2 + i for i in range(n + m)},
        compiler_params=pltpu.CompilerParams(has_side_effects=DATAFLOW),
    )(*src_arrays, *land_arrays, after)
    return (outs[0], outs[1], list(outs[2:2 + n]), list(outs[2 + n:2 + n + m])), outs[-1]


def _split_wait(name, copies, state, after):
    send_sems, recv_sems, srcs_thru, lands_thru = state
    n, m = len(srcs_thru), len(lands_thru)
    after = list(after) if isinstance(after, (list, tuple)) else [after]

    def body(*refs):
        srcs, lands = refs[:n], refs[n:n + m]
        send_sems, recv_sems = refs[n + m], refs[n + m + 1]
        for src, _, dst_here, s, peer in copies(srcs, lands):
            cp = pltpu.make_async_remote_copy(src_ref=src, dst_ref=dst_here, send_sem=send_sems.at[s], recv_sem=recv_sems.at[s],
                                              device_id=peer, device_id_type=MESH)
            cp.wait_send()
            cp.wait_recv()

    hbm = pl.BlockSpec(memory_space=pltpu.HBM)
    outs = pl.pallas_call(
        body, name=name,
        in_specs=[hbm] * (n + m) + [SEM_SPEC, SEM_SPEC] + [HBM_SPEC] * len(after),
        out_specs=[hbm] * (n + m),
        out_shape=[pltpu.HBM(a.shape, a.dtype) for a in srcs_thru + lands_thru],
        input_output_aliases={i: i for i in range(n + m)},
        compiler_params=pltpu.CompilerParams(has_side_effects=DATAFLOW),
    )(*srcs_thru, *lands_thru, send_sems, recv_sems, *after)
    return list(outs[:n]), list(outs[n:])


def _gather_copies(items):
    def copies(srcs, lands):
        x, y, c, myq = _place()
        out = []
        for t, item in enumerate(items):
            for h in (0, 1):
                src = _gather_src(item, srcs[t], h)
                for j, (cx, cy) in enumerate(_other_chips(x, y)):
                    out.append((src, _gather_dst(item, lands[t], myq, h), _gather_dst(item, lands[t], 2 * cx + cy, h),
                                PER_ITEM * t + 2 * j + h, (cx, cy, c)))
                out.append((src, _gather_dst(item, lands[t], myq, h), _gather_dst(item, lands[t], myq, h),
                            PER_ITEM * t + 6 + h, (x, y, 1 - c)))
        return out
    return copies


def _gather_start(items, shards, after):
    lands = [jax.ShapeDtypeStruct(_full_shape(name), BF16) for name, _ in items]
    return _split_start("gather_layer1_start", _gather_copies(items), PER_ITEM * len(items),
                        [shards[item] for item in items], lands, after)


def _gather_wait(items, state, after):
    return _split_wait("gather_layer1_wait", _gather_copies(items), state, after)[1]


def _small_all_reduce(v):
    r = v.shape[0]

    def body(v_ref, o_ref, buf_ref, send_sems, recv_sems):
        x, y, c, _ = _place()
        me = 4 * x + 2 * y + c
        buf_ref[me] = v_ref[...]
        copies = []
        for k in range(1, 8):
            fx, fy, fc = (k >> 2) & 1, (k >> 1) & 1, k & 1
            to = (x ^ fx, y ^ fy, c ^ fc)
            cp = pltpu.make_async_remote_copy(src_ref=v_ref, dst_ref=buf_ref.at[me], send_sem=send_sems.at[k - 1],
                                              recv_sem=recv_sems.at[k - 1], device_id=to, device_id_type=MESH)
            cp.start()
            copies.append(cp)
        for k in range(1, 8):
            fx, fy, fc = (k >> 2) & 1, (k >> 1) & 1, k & 1
            src_dev = 4 * (x ^ fx) + 2 * (y ^ fy) + (c ^ fc)
            pltpu.make_async_remote_copy(src_ref=v_ref, dst_ref=buf_ref.at[src_dev], send_sem=send_sems.at[k - 1],
                                         recv_sem=recv_sems.at[k - 1], device_id=(x, y, c), device_id_type=MESH).wait_recv()
        for cp in copies:
            cp.wait_send()
        tot = buf_ref[0]
        for i in range(1, 8):
            tot = tot + buf_ref[i]
        o_ref[...] = tot

    vm = pl.BlockSpec(memory_space=pltpu.VMEM)
    return pl.pallas_call(
        body, name="small_all_reduce", in_specs=[vm], out_specs=vm,
        out_shape=jax.ShapeDtypeStruct((r, 128), F32),
        scratch_shapes=[pltpu.VMEM((8, r, 128), F32), pltpu.SemaphoreType.DMA((7,)), pltpu.SemaphoreType.DMA((7,))],
    )(v)


def _grad_view(kind, g):
    if kind == "col":
        return g.reshape(2, g.shape[0] // 2, g.shape[1])
    return g.reshape(N_CHIPS, 2, g.shape[0] // (2 * N_CHIPS), g.shape[1])


def _half_of(kind, ref, h):
    return ref.at[h] if kind == "col" else ref.at[:, h]


def _half_shape(kind, view_shape):
    return view_shape[1:] if kind == "col" else (view_shape[0],) + view_shape[2:]


def _piece_of(kind, width, colblock, ref, q):
    if kind == "col":
        return ref.at[:, pl.ds(colblock(q) * width, width)]
    return ref.at[q]


def _piece_shape(kind, width, half_shape):
    return (half_shape[0], width) if kind == "col" else half_shape[1:]


def _pair_exchange(views, kinds, name):
    n = len(views)

    def body(*refs):
        ins, outs = refs[:n], refs[n:2 * n]
        send_sems, recv_sems = refs[2 * n:]
        x, y, c, _ = _place()
        cps = []
        for t in range(n):
            cp = pltpu.make_async_remote_copy(src_ref=_half_of(kinds[t], ins[t], 1 - c), dst_ref=outs[t],
                                              send_sem=send_sems.at[t], recv_sem=recv_sems.at[t],
                                              device_id=(x, y, 1 - c), device_id_type=MESH)
            cp.start()
            cps.append(cp)
        for cp in cps:
            cp.wait()

    return pl.pallas_call(
        body, name=name, in_specs=[HBM_SPEC] * n, out_specs=[HBM_SPEC] * n,
        out_shape=[jax.ShapeDtypeStruct(_half_shape(k, v.shape), v.dtype) for k, v in zip(kinds, views)],
        scratch_shapes=[pltpu.SemaphoreType.DMA((n,)), pltpu.SemaphoreType.DMA((n,))],
    )(*views)


def _pair_sum(kind, view, recv, c, name):
    hs = recv.shape
    N = hs[-1]
    rows = hs[-2]
    tr = _pick(rows, (512, 352, 128))
    tn = _pick(N, (1408, 1024, 512))

    def body(c_ref, p_ref, r_ref, s_ref):
        s_ref[...] = (p_ref[...] + r_ref[...]).astype(BF16)

    if kind == "col":
        grid = (rows // tr, N // tn)
        mine = pl.BlockSpec((None, tr, tn), lambda i, j, c_ref: (c_ref[0], i, j))
        blk = pl.BlockSpec((tr, tn), lambda i, j, c_ref: (i, j))
        sem = ("parallel", "parallel")
    else:
        grid = (N_CHIPS, rows // tr, N // tn)
        mine = pl.BlockSpec((None, None, tr, tn), lambda q, i, j, c_ref: (q, c_ref[0], i, j))
        blk = pl.BlockSpec((None, tr, tn), lambda q, i, j, c_ref: (q, i, j))
        sem = ("parallel", "parallel", "parallel")
    return pl.pallas_call(
        body, name=name,
        grid_spec=pltpu.PrefetchScalarGridSpec(num_scalar_prefetch=1, grid=grid, in_specs=[mine, blk], out_specs=blk),
        out_shape=jax.ShapeDtypeStruct(hs, BF16),
        compiler_params=_cparams(sem),
    )(c.reshape(1).astype(jnp.int32), view, recv)


def _chip_copies(kinds, widths, colblocks):
    def copies(srcs, lands):
        x, y, c, _ = _place()
        out = []
        for j, (cx, cy) in enumerate(_other_chips(x, y)):
            for t in range(len(kinds)):
                out.append((_piece_of(kinds[t], widths[t], colblocks[t], srcs[t], 2 * cx + cy), lands[t].at[j],
                            lands[t].at[j], 3 * t + j, (cx, cy, c)))
        return out
    return copies


def _chip_land_shapes(sums, kinds, widths):
    return [jax.ShapeDtypeStruct((3,) + _piece_shape(k, w, s.shape), BF16) for k, w, s in zip(kinds, widths, sums)]


def _chip_exchange(sums, kinds, widths, colblocks, name):
    n = len(sums)
    copies = _chip_copies(kinds, widths, colblocks)

    def body(*refs):
        send_sems, recv_sems = refs[2 * n:]
        cps = [pltpu.make_async_remote_copy(src_ref=src, dst_ref=dst, send_sem=send_sems.at[s], recv_sem=recv_sems.at[s],
                                            device_id=peer, device_id_type=MESH)
               for src, dst, _, s, peer in copies(refs[:n], refs[n:2 * n])]
        for cp in cps:
            cp.start()
        for cp in cps:
            cp.wait()

    return pl.pallas_call(
        body, name=name, in_specs=[HBM_SPEC] * n, out_specs=[HBM_SPEC] * n,
        out_shape=_chip_land_shapes(sums, kinds, widths),
        scratch_shapes=[pltpu.SemaphoreType.DMA((3 * n,)), pltpu.SemaphoreType.DMA((3 * n,))],
    )(*sums)


N_DIRECT = 7


def _direct_piece(kind, width, colblock, view_ref, q, h):
    if kind == "col":
        return view_ref.at[h, :, pl.ds(colblock(q) * width, width)]
    return view_ref.at[q, h]


def _direct_copies(kinds, widths, colblocks):
    def copies(srcs, lands):
        x, y, c, myq = _place()
        out = []
        for t in range(len(kinds)):
            def piece(q, h, t=t):
                return _direct_piece(kinds[t], widths[t], colblocks[t], srcs[t], q, h)
            for j, (cx, cy) in enumerate(_other_chips(x, y)):
                for h in (0, 1):
                    out.append((piece(2 * cx + cy, h), lands[t].at[2 * j + c], lands[t].at[2 * j + h],
                                10 * t + 3 * j + c + h, (cx, cy, h)))
            out.append((piece(myq, 1 - c), lands[t].at[6], lands[t].at[6], 10 * t + 9, (x, y, 1 - c)))
        return out
    return copies


def _chip_sum(kind, own_src, recv, block_idx, c, shard_shape, layer, into, name, direct=False):
    n_recv, rows, N = recv.shape
    tr = _pick(rows, (512, 352, 128))
    tn = _pick(N, (1408, 1024, 768, 512))
    ni, nj = rows // tr, N // tn

    def body(q_ref, s_ref, r_ref, *rest):
        o_ref = rest[-1]
        tot = s_ref[...].astype(F32)
        for k in range(n_recv):
            tot = tot + r_ref[k].astype(F32)
        o_ref[...] = tot

    if direct and kind == "col":
        own = pl.BlockSpec((None, tr, tn), lambda i, j, q_ref: (q_ref[1], i, q_ref[0] * nj + j))
    elif direct:
        own = pl.BlockSpec((None, None, tr, tn), lambda i, j, q_ref: (q_ref[0], q_ref[1], i, j))
    elif kind == "col":
        own = pl.BlockSpec((tr, tn), lambda i, j, q_ref: (i, q_ref[0] * nj + j))
    else:
        own = pl.BlockSpec((None, tr, tn), lambda i, j, q_ref: (q_ref[0], i, j))
    if len(shard_shape) == 3:
        lead = 0 if layer is None else layer
        out_spec = pl.BlockSpec((None, tr, tn), lambda i, j, q_ref: (lead, q_ref[1] * ni + i, j))
    else:
        out_spec = pl.BlockSpec((tr, tn), lambda i, j, q_ref: (q_ref[1] * ni + i, j))
    in_specs = [own, pl.BlockSpec((n_recv, tr, tn), lambda i, j, q_ref: (0, i, j))]
    s = own_src
    args = [jnp.stack([block_idx, c]).astype(jnp.int32), s, recv]
    aliases = {}
    if into is not None:
        in_specs.append(HBM_SPEC)
        args.append(into)
        aliases = {3: 0}
    return pl.pallas_call(
        body, name=name,
        grid_spec=pltpu.PrefetchScalarGridSpec(num_scalar_prefetch=1, grid=(ni, nj), in_specs=in_specs, out_specs=out_spec),
        out_shape=jax.ShapeDtypeStruct(shard_shape, F32), input_output_aliases=aliases,
        compiler_params=_cparams(("parallel", "parallel")),
    )(*args)


def _half_window(ref, h):
    rows = ref.shape[-2] // 2
    if ref.ndim == 3:
        return ref.at[:, pl.ds(h * rows, rows)]
    return ref.at[pl.ds(h * rows, rows)]


def _share_halves(grads, name):
    n = len(grads)

    def body(*refs):
        outs = refs[n:2 * n]
        send_sems, recv_sems = refs[2 * n:]
        x, y, c, _ = _place()
        cps = []
        for t in range(n):
            cp = pltpu.make_async_remote_copy(src_ref=_half_window(outs[t], c), dst_ref=_half_window(outs[t], c),
                                              send_sem=send_sems.at[t], recv_sem=recv_sems.at[t],
                                              device_id=(x, y, 1 - c), device_id_type=MESH)
            cp.start()
            cps.append(cp)
        for t in range(n):
            cps[t].wait_send()
            pltpu.make_async_remote_copy(src_ref=_half_window(outs[t], c), dst_ref=_half_window(outs[t], 1 - c),
                                         send_sem=send_sems.at[t], recv_sem=recv_sems.at[t],
                                         device_id=(x, y, 1 - c), device_id_type=MESH).wait_recv()

    return pl.pallas_call(
        body, name=name, in_specs=[HBM_SPEC] * n, out_specs=[HBM_SPEC] * n,
        out_shape=[jax.ShapeDtypeStruct(g.shape, F32) for g in grads],
        input_output_aliases={t: t for t in range(n)},
        scratch_shapes=[pltpu.SemaphoreType.DMA((n,)), pltpu.SemaphoreType.DMA((n,))],
    )(*grads)


def _adamw(w, g, m, v, name):
    R, W = w.shape
    tr = _pick(R, (512, 352, 256, 32))

    def body(w_ref, g_ref, m_ref, v_ref, d_ref, nm_ref, nv_ref):
        gv = g_ref[...]
        nm = ADAM_B1 * m_ref[...] + (1.0 - ADAM_B1) * gv
        nv = ADAM_B2 * v_ref[...] + (1.0 - ADAM_B2) * (gv * gv)
        m_hat = nm / (1.0 - ADAM_B1 ** ADAM_STEP)
        v_hat = nv / (1.0 - ADAM_B2 ** ADAM_STEP)
        d_ref[...] = -ADAM_LR * (m_hat / (jnp.sqrt(v_hat) + ADAM_EPS) + ADAM_WD * w_ref[...])
        nm_ref[...] = nm
        nv_ref[...] = nv

    blk = pl.BlockSpec((tr, W), lambda i: (i, 0))
    shp = jax.ShapeDtypeStruct((R, W), F32)
    return pl.pallas_call(
        body, name=name, grid=(R // tr,), in_specs=[blk] * 4, out_specs=[blk] * 3, out_shape=[shp] * 3,
        compiler_params=_cparams(("parallel",)),
    )(w, g, m, v)


SMALL_ROWS = 32


def _pack_small(ln_g, ln_b, sinks):
    rows = jnp.concatenate([ln_g.reshape(-1, 128), ln_b.reshape(-1, 128),
                            jnp.pad(sinks.reshape(1, -1), ((0, 0), (0, 128 - sinks.size)))], axis=0)
    return jnp.pad(rows, ((0, SMALL_ROWS - rows.shape[0]), (0, 0)))


def _unpack_small(s, ln_shape, sink_shape):
    n = ln_shape[0] * ln_shape[1] * ln_shape[2] // 128
    return s[:n].reshape(ln_shape), s[n:2 * n].reshape(ln_shape), s[2 * n, :sink_shape[1]].reshape(sink_shape)


def _ffn_fwd(xin, w_in, w_out, gain, bias, tag):
    u, h = _ffn_in(xin, w_in, "ffn_in_" + tag)
    y, yb, z = _mm_ln(h, w_out, xin, gain, bias, 0.5, "ffn_out_ln_" + tag)
    return y, yb, dict(u=u, h=h, z=z, xin=xin)


def _ffn_bwd(dz, dzc, saved, w_in, w_out, xin_b, tag, dw_dtype=F32, ln=None):
    du = _ffn_bwd_h(dzc, w_out, saved["u"], "ffn_bwd_h_" + tag)
    d_w_out = _mm_tn(saved["h"], dzc, "ffn_dwout_" + tag, out_dtype=dw_dtype)
    d_w_in = _mm_tn(xin_b, du, "ffn_dwin_" + tag, out_dtype=dw_dtype)
    dx = _mm_nt(du, w_in, "ffn_dx_" + tag, add=dz, add_scale=ALPHA, ln=ln)
    return dx, d_w_in, d_w_out


def kernel(x, ffn1_w_in, ffn1_w_out, ffn2_w_in, ffn2_w_out, ln_g, ln_b, a_w_qkv, a_w_o, kv_w, b_w_q, b_sinks, b_w_o, loss_target, m_ffn1_w_in, m_ffn1_w_out, m_ffn2_w_in, m_ffn2_w_out, m_ln_g, m_ln_b, m_a_w_qkv, m_a_w_o, m_kv_w, m_b_w_q, m_b_sinks, m_b_w_o, v_ffn1_w_in, v_ffn1_w_out, v_ffn2_w_in, v_ffn2_w_out, v_ln_g, v_ln_b, v_a_w_qkv, v_a_w_o, v_kv_w, v_b_w_q, v_b_sinks, v_b_w_o):
    ws = dict(ffn1_w_in=ffn1_w_in, ffn1_w_out=ffn1_w_out, ffn2_w_in=ffn2_w_in, ffn2_w_out=ffn2_w_out, a_w_qkv=a_w_qkv,
              a_w_o=a_w_o, kv_w=kv_w, b_w_q=b_w_q, b_w_o=b_w_o)
    ms = dict(ffn1_w_in=m_ffn1_w_in, ffn1_w_out=m_ffn1_w_out, ffn2_w_in=m_ffn2_w_in, ffn2_w_out=m_ffn2_w_out,
              a_w_qkv=m_a_w_qkv, a_w_o=m_a_w_o, kv_w=m_kv_w, b_w_q=m_b_w_q, b_w_o=m_b_w_o)
    vs = dict(ffn1_w_in=v_ffn1_w_in, ffn1_w_out=v_ffn1_w_out, ffn2_w_in=v_ffn2_w_in, ffn2_w_out=v_ffn2_w_out,
              a_w_qkv=v_a_w_qkv, a_w_o=v_a_w_o, kv_w=v_kv_w, b_w_q=v_b_w_q, b_w_o=v_b_w_o)
    _, _, c_idx, myq = _place()
    xs = x[0]
    target = loss_target[0]

    shards = {(n, l): (ws[n] if l is None else ws[n][l]).astype(BF16) for n, l in LAYER0_ITEMS + LAYER1_ITEMS}

    def as_weights(items, arrays):
        return {n: (a.reshape(D_MODEL, a.shape[-1]) if a.ndim == 4 else a) for (n, _), a in zip(items, arrays)}

    full0, small = _all_gather(LAYER0_ITEMS, shards, _pack_small(ln_g, ln_b, b_sinks))
    gather_state, token = _gather_start(LAYER1_ITEMS, shards, small)

    def layer1_weights(after):
        return as_weights(LAYER1_ITEMS, _gather_wait(LAYER1_ITEMS, gather_state, after))

    n_ln = ln_g.size // 128
    lg = jnp.concatenate([small[q, :n_ln].reshape(DEPTH, 3, 1, -1) for q in range(N_CHIPS)], axis=-1)
    lb = jnp.concatenate([small[q, n_ln:2 * n_ln].reshape(DEPTH, 3, 1, -1) for q in range(N_CHIPS)], axis=-1)
    lg = lg + token[0, 0]
    reducer = _GradReducer(c_idx, myq, {n: ws[n].shape for n in BIG})
    sq, grad_x, _, gg, gb, dsink_part = _local_step(xs, target, as_weights(LAYER0_ITEMS, full0), layer1_weights,
                                                    lg, lb, b_sinks.reshape(N_HEADS), reducer.begin)

    loss_row = jnp.pad(jnp.sum(sq).reshape(1, 1), ((0, 0), (0, 127)))
    dsinks = jnp.pad(dsink_part[:, 0, :].reshape(N_SLABS, 2, HEAD_DIM)[:, :, 0].reshape(1, N_HEADS), ((0, 0), (0, 128 - N_HEADS)))
    gg_full = jnp.stack([jnp.stack([jnp.sum(gg[i][j], axis=0) for j in range(3)]) for i in range(DEPTH)])
    gb_full = jnp.stack([jnp.stack([jnp.sum(gb[i][j], axis=0) for j in range(3)]) for i in range(DEPTH)])
    small_in = jnp.concatenate([loss_row, dsinks, gg_full.reshape(-1, 128), gb_full.reshape(-1, 128)], axis=0)
    small_in = jnp.pad(small_in, ((0, (-small_in.shape[0]) % 8), (0, 0)))
    small_sum = _small_all_reduce(small_in)
    loss = small_sum[0, 0] * (0.5 / D_MODEL)
    grad_sinks = small_sum[1, :N_HEADS].reshape(b_sinks.shape)
    n_full = DEPTH * 3 * D_MODEL // 128
    cols = D_MODEL // N_CHIPS
    grad_ln_g = lax.dynamic_slice_in_dim(small_sum[2:2 + n_full].reshape(DEPTH, 3, D_MODEL), myq * cols, cols, axis=2)
    grad_ln_b = lax.dynamic_slice_in_dim(small_sum[2 + n_full:2 + 2 * n_full].reshape(DEPTH, 3, D_MODEL), myq * cols, cols, axis=2)
    return _update(reducer, grad_x, loss, grad_ln_g, grad_ln_b, grad_sinks, ws, ms, vs,
                   (ln_g, ln_b, b_sinks), (m_ln_g, m_ln_b, m_b_sinks), (v_ln_g, v_ln_b, v_b_sinks))


def _local_step(xs, target, W, layer1_weights, lg, lb, sinks, grads_ready=None):
    if grads_ready is None:
        grads_ready = lambda tag, grads, overlap: 0.0
    S = xs.shape[0]
    slopes = jnp.asarray(_alibi_slopes(N_HEADS))
    in1, out1, in2, out2 = [W["ffn1_w_in"]], [W["ffn1_w_out"]], [W["ffn2_w_in"]], [W["ffn2_w_out"]]

    y1, y1b, s1 = _ffn_fwd(xs, in1[0], out1[0], lg[0, 0], lb[0, 0], "a1")
    qkv_a = _mm_nn(y1b, W["a_w_qkv"], F32, "qkv_a", split=True)
    mix_a, o_a, lse_a = _attn_fwd(qkv_a, slopes, None, PATTERNS_A, "attn_a_fwd")
    y2, y2b, z2 = _mm_ln(mix_a, W["a_w_o"], y1, lg[0, 1], lb[0, 1], 1.0, "attn_a_out_ln")
    y3, y3b, s3 = _ffn_fwd(y2, in2[0], out2[0], lg[0, 2], lb[0, 2], "a2")
    kv_w_rep = jnp.broadcast_to(W["kv_w"].reshape(D_MODEL, 2, N_KV_B, 1, HEAD_DIM),
                                (D_MODEL, 2, N_KV_B, GROUP_B, HEAD_DIM)).reshape(D_MODEL, 2 * D_MODEL)
    kv_rep = _mm_nn(y3b, kv_w_rep, F32, "kv_proj", split=(1, 2))
    W = dict(W, **layer1_weights(kv_rep))
    in1, out1, in2, out2 = (in1 + [W["ffn1_w_in"]], out1 + [W["ffn1_w_out"]], in2 + [W["ffn2_w_in"]],
                            out2 + [W["ffn2_w_out"]])
    y4, y4b, s4 = _ffn_fwd(y3, in1[1], out1[1], lg[1, 0], lb[1, 0], "b1")
    qkv_b = _mm_nn(y4b, W["b_w_q"], F32, "q_b", split=(0, 1), into=kv_rep)
    mix_b, o_b, lse_b = _attn_fwd(qkv_b, slopes, sinks, PATTERNS_B, "attn_b_fwd")
    y5, y5b, z5 = _mm_ln(mix_b, W["b_w_o"], y4, lg[1, 1], lb[1, 1], 1.0, "attn_b_out_ln")
    y6, _, s6 = _ffn_fwd(y5, in2[1], out2[1], lg[1, 2], lb[1, 2], "b2")

    gr = {n: None for n in BIG}
    gg = [[None] * 3 for _ in range(DEPTH)]
    gb = [[None] * 3 for _ in range(DEPTH)]
    dz6, dz6c, gg[1][2], gb[1][2], sq = _loss_ln_bwd(y6, target, s6["z"], lg[1, 2], 0.5, "loss_ln_bwd")

    (dz5, dz5b, gg[1][1], gb[1][1]), d_in2_b, d_out2_b = _ffn_bwd(dz6, dz6c, s6, in2[1], out2[1], y5b, "b2", BF16,
                                                                  ln=(z5, lg[1, 1], 1.0))
    gr["b_w_o"] = _mm_tn(mix_b, dz5b, "d_b_w_o", out_dtype=BF16)
    dmix_b = _mm_nt(dz5b, W["b_w_o"], "d_mix_b")
    dqkv_b, dsink_part = _attn_bwd(qkv_b, dmix_b, o_b, lse_b, slopes, sinks, PATTERNS_B, "attn_b_bwd")
    dq_b = (dqkv_b, 0)
    gr["b_w_q"] = _mm_tn(y4b, dq_b, "d_b_w_q", out_dtype=BF16)
    dz4, dz4c, gg[1][0], gb[1][0] = _mm_nt(dq_b, W["b_w_q"], "d_y4", add=dz5, add_scale=ALPHA, ln=(s4["z"], lg[1, 0], 0.5))
    dy3, d_in1_b, d_out1_b = _ffn_bwd(dz4, dz4c, s4, in1[1], out1[1], y3b, "b1", BF16)
    d_kv_w_rep = _mm_tn(y3b, dqkv_b, "d_kv_w", split=(1, 2))
    gr["kv_w"] = d_kv_w_rep.reshape(D_MODEL, 2, N_KV_B, GROUP_B, HEAD_DIM).sum(axis=3).reshape(D_MODEL, -1).astype(BF16)
    tok = grads_ready("l1", {("ffn2_w_in", 1): d_in2_b, ("ffn2_w_out", 1): d_out2_b, ("b_w_o", None): gr["b_w_o"],
                             ("b_w_q", None): gr["b_w_q"], ("ffn1_w_in", 1): d_in1_b, ("ffn1_w_out", 1): d_out1_b,
                             ("kv_w", None): gr["kv_w"]}, True)
    lg0 = lg[0] + tok
    dz3, dz3c, gg[0][2], gb[0][2] = _mm_nt(dqkv_b, kv_w_rep, "d_y3_kv", add=dy3, add_scale=1.0, split=(1, 2),
                                           ln=(s3["z"], lg0[2], 0.5))

    (dz2, dz2b, gg[0][1], gb[0][1]), d_in2_a, d_out2_a = _ffn_bwd(dz3, dz3c, s3, in2[0], out2[0], y2b, "a2", BF16,
                                                                  ln=(z2, lg0[1], 1.0))
    tok = grads_ready("a2", {("ffn2_w_in", 0): d_in2_a, ("ffn2_w_out", 0): d_out2_a}, True)
    lg0 = lg0 + tok
    gr["a_w_o"] = _mm_tn(mix_a, dz2b, "d_a_w_o", out_dtype=BF16)
    dmix_a = _mm_nt(dz2b, W["a_w_o"], "d_mix_a")
    dqkv_a, _ = _attn_bwd(qkv_a, dmix_a, o_a, lse_a, slopes, None, PATTERNS_A, "attn_a_bwd")
    gr["a_w_qkv"] = _mm_tn(y1b, dqkv_a, "d_a_w_qkv", split=True, out_dtype=BF16)
    tok = grads_ready("mix", {("a_w_o", None): gr["a_w_o"], ("a_w_qkv", None): gr["a_w_qkv"]}, True)
    lg0 = lg0 + tok
    dz1, dz1c, gg[0][0], gb[0][0] = _mm_nt(dqkv_a, W["a_w_qkv"], "d_y1", add=dz2, add_scale=ALPHA, split=True,
                                           ln=(s1["z"], lg0[0], 0.5))
    grad_x, d_in1_a, d_out1_a = _ffn_bwd(dz1, dz1c, s1, in1[0], out1[0], xs, "a1", BF16)
    grads_ready("a1", {("ffn1_w_in", 0): d_in1_a, ("ffn1_w_out", 0): d_out1_a}, True)
    gr["ffn1_w_in"] = [d_in1_a, d_in1_b]
    gr["ffn1_w_out"] = [d_out1_a, d_out1_b]
    gr["ffn2_w_in"] = [d_in2_a, d_in2_b]
    gr["ffn2_w_out"] = [d_out2_a, d_out2_b]
    return sq, grad_x, gr, gg, gb, dsink_part


def _grad_item(name, layer, g):
    if name.endswith("w_in"):
        return (g, "col", HALF_FF, _slot, name, layer)
    if name.endswith("w_out"):
        return (g, "row", D_MODEL, None, name, layer)
    if name == "a_w_qkv":
        return (g, "col", QKV_SHARD, lambda q: q, name, None)
    return (g, "row", g.shape[1], None, name, None)


class _GradReducer:
    def __init__(self, c_idx, myq, shard_shapes):
        self.c_idx, self.myq, self.shard_shapes = c_idx, myq, shard_shapes
        self.groups = []

    def begin(self, tag, grads, overlap):
        items = [_grad_item(n, l, g) for (n, l), g in grads.items()]
        kinds, widths, colblocks = [it[1] for it in items], [it[2] for it in items], [it[3] for it in items]
        views = [_grad_view(k, it[0]) for k, it in zip(kinds, items)]
        if overlap:
            lands = [jax.ShapeDtypeStruct((N_DIRECT,) + _piece_shape(k, w, _half_shape(k, v.shape)), BF16)
                     for k, w, v in zip(kinds, widths, views)]
            state, token = _split_start("grad_direct_start_" + tag, _direct_copies(kinds, widths, colblocks), 10 * len(items),
                                        views, lands, views[-1])
            self.groups.append((tag, items, None, state, token))
            return token[0, 0]
        from_sibling = _pair_exchange(views, kinds, "grad_pair_exchange_" + tag)
        sums = [_pair_sum(k, v, r, self.c_idx, "pair_sum_%s_%d" % (tag, t))
                for t, (k, v, r) in enumerate(zip(kinds, views, from_sibling))]
        self.groups.append((tag, items, sums, None, None))
        return 0.0

    def _sum_group(self, tag, items, sums, received, direct):
        for t, (it, s, r) in enumerate(zip(items, sums, received)):
            _, k, _, cb, name, layer = it
            own = cb(self.myq) if k == "col" else self.myq
            self.half_done[name] = _chip_sum(k, s, r, own, self.c_idx, self.shard_shapes[name], layer,
                                             self.half_done.get(name), "chip_sum_%s_%d" % (tag, t), direct=direct)

    def finish_first(self, after):
        self.half_done, self.late, early = {}, [], []
        started = [after]
        for g, (tag, items, sums, state, token) in enumerate(self.groups):
            kinds, widths, colblocks = [it[1] for it in items], [it[2] for it in items], [it[3] for it in items]
            if state is None:
                copies = _chip_copies(kinds, widths, colblocks)
                state, token = _split_start("grad_chip_start_" + tag, copies, 3 * len(items), sums,
                                            _chip_land_shapes(sums, kinds, widths), sums[-1])
                self.late.append((tag, items, copies, state, False))
                started.append(token)
            elif g == len(self.groups) - 1:
                self.late.append((tag, items, _direct_copies(kinds, widths, colblocks), state, True))
                started.append(token)
            else:
                early.append((tag, items, _direct_copies(kinds, widths, colblocks), state))
        for tag, items, copies, state in early:
            views, received = _split_wait("grad_direct_wait_" + tag, copies, state, started)
            self._sum_group(tag, items, views, received, True)
        late_names = {it[4] for _, items, _, _, _ in self.late for it in items}
        names = [n for n in BIG if n not in late_names]
        return dict(zip(names, _share_halves([self.half_done[n] for n in names], "grad_share_halves_first")))

    def finish_rest(self, after):
        names = []
        for tag, items, copies, state, direct in self.late:
            sums, received = _split_wait("grad_late_wait_" + tag, copies, state, after)
            self._sum_group(tag, items, sums, received, direct)
            names += [it[4] for it in items if it[4] not in names]
        return dict(zip(names, _share_halves([self.half_done[n] for n in names], "grad_share_halves_rest")))


def _update(reducer, grad_x, loss, grad_ln_g, grad_ln_b, grad_sinks, ws, ms, vs, small_w, small_m, small_v):
    ln_g, ln_b, b_sinks = small_w
    m_ln_g, m_ln_b, m_b_sinks = small_m
    v_ln_g, v_ln_b, v_b_sinks = small_v

    deltas, new_m, new_v = {}, {}, {}

    def update(some):
        done = []
        for name in some:
            shp = ws[name].shape
            flat = lambda a: a.reshape(-1, shp[-1])
            d, nm, nv = _adamw(flat(ws[name]), flat(some[name]), flat(ms[name]), flat(vs[name]), "adamw_" + name)
            deltas[name], new_m[name], new_v[name] = d.reshape(shp), nm.reshape(shp), nv.reshape(shp)
            done.append(d)
        return done

    grads = reducer.finish_first(grad_x)
    rest = reducer.finish_rest(update(grads))
    update(rest)
    grads.update(rest)
    delta_s, nm_s, nv_s = _adamw(_pack_small(ln_g, ln_b, b_sinks), _pack_small(grad_ln_g, grad_ln_b, grad_sinks),
                                 _pack_small(m_ln_g, m_ln_b, m_b_sinks), _pack_small(v_ln_g, v_ln_b, v_b_sinks), "adamw_small")
    for d, blob in ((grads, None), (deltas, delta_s), (new_m, nm_s), (new_v, nv_s)):
        if blob is None:
            d["ln_g"], d["ln_b"], d["b_sinks"] = grad_ln_g, grad_ln_b, grad_sinks
        else:
            d["ln_g"], d["ln_b"], d["b_sinks"] = _unpack_small(blob, ln_g.shape, b_sinks.shape)

    order = ("ffn1_w_in", "ffn1_w_out", "ffn2_w_in", "ffn2_w_out", "ln_g", "ln_b", "a_w_qkv", "a_w_o", "kv_w", "b_w_q",
             "b_sinks", "b_w_o")
    outs = [loss, grad_x[None]]
    for d in (grads, deltas, new_m, new_v):
        outs += [d[n] for n in order]
    return tuple(outs)
```

```python
import numpy as np
import jax
import jax.numpy as jnp
from jax import lax
from jax.experimental import pallas as pl
from jax.experimental.pallas import tpu as pltpu

F32 = jnp.float32
BF16 = jnp.bfloat16

D_MODEL = 1024
D_FF = 2816
HALF_FF = D_FF // 2
HEAD_DIM = 64
N_HEADS = 16
N_KV_B = 4
GROUP_B = N_HEADS // N_KV_B
DEPTH = 2
ALPHA = (2.0 * DEPTH) ** 0.25
LN_EPS = 1e-5
BLOCK = 128
SLAB = 128
N_SLABS = D_MODEL // SLAB
PATTERNS_A = ((1, 128, 1.0), (4, 128, 4.0), (16, 128, 16.0))
PATTERNS_B = ((1, 127, 1.0),)
NEG = -1e30

ADAM_LR = 0.001
ADAM_B1 = 0.9
ADAM_B2 = 0.999
ADAM_EPS = 1e-08
ADAM_WD = 0.01
ADAM_STEP = 10

N_CHIPS = 4
VMEM_LIMIT = 56 * 1024 * 1024
MESH = pl.DeviceIdType.MESH


def _alibi_slopes(n):
    return np.array([2.0 ** (-8.0 * (h + 1) / n) for h in range(n)], dtype=np.float32)


def _cparams(sem=None, vmem=VMEM_LIMIT):
    return pltpu.CompilerParams(dimension_semantics=sem, vmem_limit_bytes=vmem)


_DIMS = {"nn": ((1,), (0,)), "nt": ((1,), (1,)), "tn": ((0,), (0,))}


def _unlead(x):
    if isinstance(x, tuple):
        return x[0], x[1], x[0].shape[1:]
    return x, None, x.shape


def _bspec(block, imap, lead=None):
    if lead is None:
        return pl.BlockSpec(block, imap)
    return pl.BlockSpec((None,) + tuple(block), lambda *g: (lead,) + tuple(imap(*g)))


def _ln_bwd_math(zv, dyv, gain):
    rows = zv.shape[0]
    mu = jnp.mean(zv, axis=-1, keepdims=True)
    zc = zv - mu
    var = jnp.mean(zc * zc, axis=-1, keepdims=True)
    rstd = lax.rsqrt(var + LN_EPS)
    xhat = zc * rstd
    dyg = dyv * gain
    m1 = jnp.mean(dyg, axis=-1, keepdims=True)
    m2 = jnp.mean(dyg * xhat, axis=-1, keepdims=True)
    dz = rstd * (dyg - m1 - xhat * m2)
    pg = jnp.sum((dyv * xhat).reshape(rows // 8, 8, D_MODEL), axis=0)
    pb = jnp.sum(dyv.reshape(rows // 8, 8, D_MODEL), axis=0)
    return dz, pg, pb


def _matmul(a, b, mode, out_dtype, tm, tn, tk, name, add=None, add_scale=1.0, split=False, into=None, ln=None):
    out_spec = pl.BlockSpec((tm, tn), lambda i, j, k: (i, j))
    base, count = (0, 3) if split is True else (split or (0, 0))
    if mode == "nn":
        a, al, (M, K) = _unlead(a)
        b, bl, (K2, N) = _unlead(b)
        a_spec = _bspec((tm, tk), lambda i, j, k: (i, k), al)
        b_spec = _bspec((tk, tn), lambda i, j, k: (k, j), bl)
        out_struct = jax.ShapeDtypeStruct((M, N), out_dtype)
        if split:
            assert tn == D_MODEL and N == count * tn
            out_spec = pl.BlockSpec((None, tm, tn), lambda i, j, k: (j + base, i, 0))
            out_struct = jax.ShapeDtypeStruct((3, M, tn), out_dtype)
    elif mode == "nt":
        b, bl, (N, K2) = _unlead(b)
        if split:
            assert tk == D_MODEL
            M, K = a.shape[1], count * a.shape[2]
            a_spec = pl.BlockSpec((None, tm, tk), lambda i, j, k: (k + base, i, 0))
        else:
            a, al, (M, K) = _unlead(a)
            a_spec = _bspec((tm, tk), lambda i, j, k: (i, k), al)
        b_spec = _bspec((tn, tk), lambda i, j, k: (j, k), bl)
        out_struct = jax.ShapeDtypeStruct((M, N), out_dtype)
    else:
        a, al, (K, M) = _unlead(a)
        if split:
            assert tn == D_MODEL
            K2, N = b.shape[1], count * b.shape[2]
            b_spec = pl.BlockSpec((None, tk, tn), lambda i, j, k: (j + base, k, 0))
        else:
            b, bl, (K2, N) = _unlead(b)
            b_spec = _bspec((tk, tn), lambda i, j, k: (k, j), bl)
        a_spec = _bspec((tk, tm), lambda i, j, k: (k, i), al)
        out_struct = jax.ShapeDtypeStruct((M, N), out_dtype)
    assert K == K2 and M % tm == 0 and N % tn == 0 and K % tk == 0, (a.shape, b.shape, mode, tm, tn, tk)
    nk = K // tk
    dims = (_DIMS[mode], ((), ()))
    has_add = add is not None

    narrow = out_dtype != F32
    assert not (narrow and has_add)
    if ln is not None:
        assert has_add and mode == "nt" and tn == N == D_MODEL

    def body(*refs):
        if into is not None:
            refs = refs[:2] + refs[3:]
        if ln is not None:
            a_ref, b_ref, add_ref, z_ref, g_ref, o_ref, dzc_ref, gg_ref, gb_ref = refs
            acc_ref = o_ref
        elif has_add:
            a_ref, b_ref, add_ref, o_ref = refs
            acc_ref = o_ref
        elif narrow:
            a_ref, b_ref, o_ref, acc_ref = refs
        else:
            a_ref, b_ref, o_ref = refs
            acc_ref = o_ref
        k = pl.program_id(2)
        part = lax.dot_general(a_ref[...].astype(BF16), b_ref[...].astype(BF16), dims, preferred_element_type=F32)
        if has_add:
            @pl.when(k == 0)
            def _():
                acc_ref[...] = part + add_scale * add_ref[...]
        else:
            @pl.when(k == 0)
            def _():
                acc_ref[...] = part

        @pl.when(k > 0)
        def _():
            acc_ref[...] += part

        if narrow:
            @pl.when(k == nk - 1)
            def _():
                o_ref[...] = acc_ref[...].astype(out_dtype)

        if ln is not None:
            @pl.when(k == nk - 1)
            def _():
                dz, pg, pb = _ln_bwd_math(z_ref[...], o_ref[...], g_ref[...])
                o_ref[...] = dz
                dzc_ref[...] = (ln[2] * dz).astype(BF16)
                first = pl.program_id(0) == 0

                @pl.when(first)
                def _():
                    gg_ref[...] = pg
                    gb_ref[...] = pb

                @pl.when(jnp.logical_not(first))
                def _():
                    gg_ref[...] += pg
                    gb_ref[...] += pb

    in_specs = [a_spec, b_spec]
    args = [a, b]
    aliases = {}
    if into is not None:
        assert mode == "nn" and split and not has_add
        in_specs.append(pl.BlockSpec(memory_space=pl.ANY))
        args.append(into)
        aliases = {2: 0}
    if has_add:
        in_specs.append(pl.BlockSpec((tm, tn), lambda i, j, k: (i, j)))
        args.append(add)
    sem = ("parallel", "parallel", "arbitrary")
    if ln is not None:
        part8 = pl.BlockSpec((8, N), lambda i, j, k: (0, 0))
        in_specs += [pl.BlockSpec((tm, tn), lambda i, j, k: (i, j)), pl.BlockSpec((1, N), lambda i, j, k: (0, 0))]
        args += [ln[0], ln[1]]
        out_spec = [out_spec, pl.BlockSpec((tm, tn), lambda i, j, k: (i, j)), part8, part8]
        out_struct = [out_struct, jax.ShapeDtypeStruct((M, N), BF16), jax.ShapeDtypeStruct((8, N), F32),
                      jax.ShapeDtypeStruct((8, N), F32)]
        sem = ("arbitrary", "arbitrary", "arbitrary")
    return pl.pallas_call(
        body, name=name, grid=(M // tm, N // tn, nk),
        in_specs=in_specs, out_specs=out_spec, out_shape=out_struct, input_output_aliases=aliases,
        scratch_shapes=[pltpu.VMEM((tm, tn), F32)] if narrow else [],
        compiler_params=_cparams(sem),
    )(*args)


def _pick(n, cands):
    for c in cands:
        if n % c == 0:
            return c
    raise ValueError((n, cands))


def _mm_nn(a, b, out_dtype, name, split=False, into=None):
    M, K = _unlead(a)[2]
    N = _unlead(b)[2][1]
    return _matmul(a, b, "nn", out_dtype, _pick(M, (1024, 512, 256)), _pick(N, (1024, 512)), _pick(K, (1024, 512)), name,
                   split=split, into=into)


def _mm_nt(a, b, name, add=None, add_scale=1.0, split=False, ln=None):
    M, K = (a.shape[1], D_MODEL) if split else _unlead(a)[2]
    N = _unlead(b)[2][0]
    tms = (512, 256) if ln is not None else (1024, 512, 256)
    return _matmul(a, b, "nt", F32, _pick(M, tms), _pick(N, (1024, 512)),
                   _pick(K, (2816, 1024, 512)), name, add=add, add_scale=add_scale, split=split, ln=ln)


def _mm_tn(a, b, name, split=False, out_dtype=F32):
    K, M = _unlead(a)[2]
    N = D_MODEL if split else _unlead(b)[2][1]
    return _matmul(a, b, "tn", out_dtype, _pick(M, (1024, 1408, 512)), _pick(N, (1408, 1024, 512)),
                   _pick(K, (2048, 1024, 512, 256)), name, split=split)


def _d_kv_w(y, dqkv, name):
    S = y.shape[0]
    tk = _pick(S, (1024, 512))
    nk = S // tk
    width = N_KV_B * HEAD_DIM
    r, c = np.arange(D_MODEL)[:, None], np.arange(width)[None, :]
    fold = jnp.asarray((r // (GROUP_B * HEAD_DIM) == c // HEAD_DIM) & (r % HEAD_DIM == c % HEAD_DIM), BF16)

    def body(y_ref, dk_ref, dv_ref, f_ref, o_ref, acc_ref):
        k = pl.program_id(0)
        for j, ref in enumerate((dk_ref, dv_ref)):
            summed = jnp.dot(ref[...], f_ref[...], preferred_element_type=F32).astype(BF16)
            part = lax.dot_general(y_ref[...], summed, (_DIMS["tn"], ((), ())), preferred_element_type=F32)
            cols = slice(j * width, (j + 1) * width)

            @pl.when(k == 0)
            def _():
                acc_ref[:, cols] = part

            @pl.when(k > 0)
            def _():
                acc_ref[:, cols] += part

        @pl.when(k == nk - 1)
        def _():
            o_ref[...] = acc_ref[...].astype(BF16)

    return pl.pallas_call(
        body, name=name, grid=(nk,),
        in_specs=[pl.BlockSpec((tk, D_MODEL), lambda k: (k, 0)),
                  pl.BlockSpec((None, tk, D_MODEL), lambda k: (1, k, 0)),
                  pl.BlockSpec((None, tk, D_MODEL), lambda k: (2, k, 0)),
                  pl.BlockSpec((D_MODEL, width), lambda k: (0, 0))],
        out_specs=pl.BlockSpec((D_MODEL, 2 * width), lambda k: (0, 0)),
        out_shape=jax.ShapeDtypeStruct((D_MODEL, 2 * width), BF16),
        scratch_shapes=[pltpu.VMEM((D_MODEL, 2 * width), F32)],
        compiler_params=_cparams(("arbitrary",)),
    )(y, dqkv, dqkv, fold)


def _ffn_in(x, w, name):
    S = x.shape[0]
    tm = _pick(S, (512, 256))
    w, wl, _ = _unlead(w)

    def body(x_ref, w_ref, t_ref, h_ref):
        acc = jnp.dot(x_ref[...].astype(BF16), w_ref[...], preferred_element_type=F32)
        g = acc[:, :HALF_FF]
        up = acc[:, HALF_FF:]
        sg = jax.nn.sigmoid(g)
        silu = g * sg
        t_ref[:, :HALF_FF] = (up * (sg * (1.0 + g * (1.0 - sg)))).astype(BF16)
        t_ref[:, HALF_FF:] = silu.astype(BF16)
        h_ref[...] = (silu * up).astype(BF16)

    return pl.pallas_call(
        body, name=name, grid=(2, S // tm),
        in_specs=[pl.BlockSpec((tm, D_MODEL), lambda j, i: (i, 0)),
                  _bspec((D_MODEL, D_FF), lambda j, i: (0, j), wl)],
        out_specs=[pl.BlockSpec((tm, D_FF), lambda j, i: (i, j)),
                   pl.BlockSpec((tm, HALF_FF), lambda j, i: (i, j))],
        out_shape=[jax.ShapeDtypeStruct((S, 2 * D_FF), BF16), jax.ShapeDtypeStruct((S, D_FF), BF16)],
        compiler_params=_cparams(("parallel", "parallel")),
    )(x, w)


def _ffn_bwd_h(dzc, w_out, u, name):
    S = dzc.shape[0]
    tm = _pick(S, (512, 256))
    w_out, wl, _ = _unlead(w_out)

    def body(dz_ref, w_ref, t_ref, du_ref):
        dh = lax.dot_general(dz_ref[...], w_ref[...], (((1,), (1,)), ((), ())), preferred_element_type=F32)
        du_ref[:, :HALF_FF] = (dh * t_ref[:, :HALF_FF].astype(F32)).astype(BF16)
        du_ref[:, HALF_FF:] = (dh * t_ref[:, HALF_FF:].astype(F32)).astype(BF16)

    return pl.pallas_call(
        body, name=name, grid=(2, S // tm),
        in_specs=[pl.BlockSpec((tm, D_MODEL), lambda j, i: (i, 0)),
                  _bspec((HALF_FF, D_MODEL), lambda j, i: (j, 0), wl),
                  pl.BlockSpec((tm, D_FF), lambda j, i: (i, j))],
        out_specs=pl.BlockSpec((tm, D_FF), lambda j, i: (i, j)),
        out_shape=jax.ShapeDtypeStruct((S, 2 * D_FF), BF16),
        compiler_params=_cparams(("parallel", "parallel")),
    )(dzc, w_out, u)


def _mm_ln(a, w, resid, gain, bias, c, name):
    S, K = a.shape
    tm = _pick(S, (512, 256))
    w, wl, _ = _unlead(w)

    def body(a_ref, w_ref, r_ref, g_ref, b_ref, y_ref, yb_ref, z_ref):
        z = ALPHA * r_ref[...] + c * jnp.dot(a_ref[...], w_ref[...], preferred_element_type=F32)
        mu = jnp.mean(z, axis=-1, keepdims=True)
        zc = z - mu
        var = jnp.mean(zc * zc, axis=-1, keepdims=True)
        y = zc * lax.rsqrt(var + LN_EPS) * g_ref[...] + b_ref[...]
        z_ref[...] = z
        y_ref[...] = y
        yb_ref[...] = y.astype(BF16)

    row = pl.BlockSpec((tm, D_MODEL), lambda i: (i, 0))
    vec = pl.BlockSpec((1, D_MODEL), lambda i: (0, 0))
    return pl.pallas_call(
        body, name=name, grid=(S // tm,),
        in_specs=[pl.BlockSpec((tm, K), lambda i: (i, 0)), _bspec((K, D_MODEL), lambda i: (0, 0), wl), row, vec, vec],
        out_specs=[row, row, row],
        out_shape=[jax.ShapeDtypeStruct((S, D_MODEL), F32), jax.ShapeDtypeStruct((S, D_MODEL), BF16),
                   jax.ShapeDtypeStruct((S, D_MODEL), F32)],
        compiler_params=_cparams(("parallel",)),
    )(a, w, resid, gain, bias)


def _loss_ln_bwd(y, t, z, gain, c, name):
    S = y.shape[0]
    tm = _pick(S, (512, 256))

    def body(y_ref, t_ref, z_ref, g_ref, dz_ref, dzc_ref, gg_ref, gb_ref, sq_ref):
        i = pl.program_id(0)
        e = y_ref[...] - t_ref[...]
        dz, pg, pb = _ln_bwd_math(z_ref[...], e * (1.0 / D_MODEL), g_ref[...])
        dz_ref[...] = dz
        dzc_ref[...] = (c * dz).astype(BF16)
        ps = jnp.sum((e * e).reshape(tm // 8, 8, D_MODEL), axis=0)

        @pl.when(i == 0)
        def _():
            gg_ref[...] = pg
            gb_ref[...] = pb
            sq_ref[...] = ps

        @pl.when(i > 0)
        def _():
            gg_ref[...] += pg
            gb_ref[...] += pb
            sq_ref[...] += ps

    row = pl.BlockSpec((tm, D_MODEL), lambda i: (i, 0))
    part = pl.BlockSpec((8, D_MODEL), lambda i: (0, 0))
    part_shape = jax.ShapeDtypeStruct((8, D_MODEL), F32)
    return pl.pallas_call(
        body, name=name, grid=(S // tm,),
        in_specs=[row, row, row, pl.BlockSpec((1, D_MODEL), lambda i: (0, 0))],
        out_specs=[row, row, part, part, part],
        out_shape=[jax.ShapeDtypeStruct((S, D_MODEL), F32), jax.ShapeDtypeStruct((S, D_MODEL), BF16),
                   part_shape, part_shape, part_shape],
        compiler_params=_cparams(("arbitrary",)),
    )(y, t, z, gain)


def _rows(start, d):
    if d == 1:
        return pl.ds(pl.multiple_of(start, BLOCK), BLOCK)
    return pl.ds(start, BLOCK, stride=d)


def _ld(ref, start, d):
    return ref[_rows(start, d), :]


def _ld3(ref, lead, start, d):
    return ref[lead, _rows(start, d), :]


def _st3(ref, lead, start, d, val):
    ref[lead, _rows(start, d), :] = val


def _acc3(ref, lead, start, d, val):
    ref[lead, _rows(start, d), :] = ref[lead, _rows(start, d), :] + val


def _band_consts(slope0, slope1, maxd, scale):
    row = lax.broadcasted_iota(jnp.int32, (2 * BLOCK, 2 * BLOCK), 0)
    kj = lax.broadcasted_iota(jnp.int32, (2 * BLOCK, 2 * BLOCK), 1)
    top = row < BLOCK
    dist = BLOCK + jnp.where(top, row, row - BLOCK) - kj
    slope = jnp.where(top, slope0, slope1)
    base = jnp.where((dist >= 0) & (dist <= maxd), -(slope * (dist.astype(F32) * scale)), NEG)
    return base, kj < BLOCK


def _stack_heads(x, lo):
    return jnp.concatenate([jnp.where(lo, x, 0.0), jnp.where(lo, 0.0, x)], axis=0)


def _unstack_heads(x2, lo):
    return jnp.where(lo, x2[:BLOCK], x2[BLOCK:])


def _scores(q2, k2, base, prev_keys, first):
    s = lax.dot_general(q2, k2, (((1,), (1,)), ((), ())), preferred_element_type=F32) * (HEAD_DIM ** -0.5) + base
    return jnp.where(jnp.logical_and(prev_keys, first), NEG, s)


def _softmax_weights(ls):
    mx = ls[0]
    for l in ls[1:]:
        mx = jnp.maximum(mx, l)
    es = [jnp.exp(l - mx) for l in ls]
    tot = es[0]
    for e in es[1:]:
        tot = tot + e
    inv = 1.0 / tot
    return [e * inv for e in es]


def _attn_fwd(qkv, slopes, sinks, patterns, name):
    S = qkv.shape[1]
    npat = len(patterns)
    has_sink = sinks is not None
    if not has_sink:
        sinks = jnp.zeros((N_HEADS,), F32)
    rows_c = 256

    def body(slopes_ref, sinks_ref, x_ref, mix_ref, o_ref, lse_ref, o_scr, lse_scr):
        p = pl.program_id(0)
        lo = lax.broadcasted_iota(jnp.int32, (BLOCK, SLAB), 1) < HEAD_DIM
        top1 = lax.broadcasted_iota(jnp.int32, (2 * BLOCK, 1), 0) < BLOCK
        sk2 = jnp.where(top1, sinks_ref[2 * p], sinks_ref[2 * p + 1])
        for pi, (d, maxd, scale) in enumerate(patterns):
            nb = S // d // BLOCK
            base, prev_keys = _band_consts(slopes_ref[2 * p], slopes_ref[2 * p + 1], maxd, scale)

            def blk(t, carry, pi=pi, d=d, nb=nb, base=base, prev_keys=prev_keys):
                r = t // nb
                n = t - r * nb
                start = r + (d * BLOCK) * n
                prev = jnp.where(n > 0, start - d * BLOCK, start)
                q2 = _stack_heads(_ld3(x_ref, 0, start, d), lo).astype(BF16)
                k2 = jnp.concatenate([_ld3(x_ref, 1, prev, d), _ld3(x_ref, 1, start, d)], axis=0).astype(BF16)
                v2 = jnp.concatenate([_ld3(x_ref, 2, prev, d), _ld3(x_ref, 2, start, d)], axis=0).astype(BF16)
                s = _scores(q2, k2, base, prev_keys, n == 0)
                m = jnp.max(s, axis=-1, keepdims=True)
                if has_sink:
                    m = jnp.maximum(m, sk2)
                e = jnp.exp(s - m)
                den = jnp.sum(e, axis=-1, keepdims=True)
                if has_sink:
                    den = den + jnp.exp(sk2 - m)
                o2 = jnp.dot((e / den).astype(BF16), v2, preferred_element_type=F32)
                _st3(o_scr, pi, start, d, _unstack_heads(o2, lo))
                _st3(lse_scr, pi, start, d, _unstack_heads(m + jnp.log(den), lo))
                return carry

            lax.fori_loop(0, d * nb, blk, 0, unroll=8)

        lane_c = lax.broadcasted_iota(jnp.int32, (rows_c, SLAB), 1)

        def comb(ci, carry):
            rows = pl.ds(pl.multiple_of(ci * rows_c, rows_c), rows_c)
            ls = [lse_scr[i, rows, :] for i in range(npat)]
            packed = jnp.zeros((rows_c, SLAB), F32)
            for i in range(npat):
                o_ref[i, rows, :] = o_scr[i, rows, :].astype(BF16)
                packed = jnp.where(lane_c % HEAD_DIM == i, ls[i], packed)
            lse_ref[rows, :] = packed
            if npat == 1:
                mix_ref[rows, :] = o_scr[0, rows, :].astype(BF16)
            else:
                ws = _softmax_weights(ls)
                acc = ws[0] * o_scr[0, rows, :]
                for i in range(1, npat):
                    acc = acc + ws[i] * o_scr[i, rows, :]
                mix_ref[rows, :] = acc.astype(BF16)
            return carry

        lax.fori_loop(0, S // rows_c, comb, 0, unroll=2)

    smem = pl.BlockSpec(memory_space=pltpu.SMEM)
    return pl.pallas_call(
        body, name=name, grid=(N_SLABS,),
        in_specs=[smem, smem, pl.BlockSpec((3, S, SLAB), lambda p: (0, 0, p))],
        out_specs=[pl.BlockSpec((S, SLAB), lambda p: (0, p)), pl.BlockSpec((npat, S, SLAB), lambda p: (0, 0, p)),
                   pl.BlockSpec((None, S, SLAB), lambda p: (p, 0, 0))],
        out_shape=[jax.ShapeDtypeStruct((S, D_MODEL), BF16), jax.ShapeDtypeStruct((npat, S, D_MODEL), BF16),
                   jax.ShapeDtypeStruct((N_SLABS, S, SLAB), F32)],
        scratch_shapes=[pltpu.VMEM((npat, S, SLAB), F32), pltpu.VMEM((npat, S, SLAB), F32)],
        compiler_params=_cparams(("arbitrary",)),
    )(slopes, sinks, qkv)


def _attn_bwd(qkv, dout, o, lse, slopes, sinks, patterns, name):
    S = qkv.shape[1]
    npat = len(patterns)
    has_sink = sinks is not None
    if not has_sink:
        sinks = jnp.zeros((N_HEADS,), F32)
    rows_c = 256

    def headsum(x, lo):
        same = (lax.broadcasted_iota(jnp.int32, (SLAB, SLAB), 0) < HEAD_DIM) == (lax.broadcasted_iota(jnp.int32, (SLAB, SLAB), 1) < HEAD_DIM)
        return jnp.dot(x, same.astype(F32), precision=lax.Precision.HIGH, preferred_element_type=F32)

    def body(slopes_ref, sinks_ref, x_ref, do_ref, o_ref, lsep_ref, dxo_ref, dsink_ref, dbar_ref, sacc_ref, lse_ref, dx_ref):
        p = pl.program_id(0)
        lo = lax.broadcasted_iota(jnp.int32, (BLOCK, SLAB), 1) < HEAD_DIM
        lo_c = lax.broadcasted_iota(jnp.int32, (rows_c, SLAB), 1) < HEAD_DIM
        top1 = lax.broadcasted_iota(jnp.int32, (2 * BLOCK, 1), 0) < BLOCK
        sk2 = jnp.where(top1, sinks_ref[2 * p], sinks_ref[2 * p + 1])

        def prep(ci, carry):
            rows = pl.ds(pl.multiple_of(ci * rows_c, rows_c), rows_c)
            dov = do_ref[rows, :]
            dx_ref[:, rows, :] = jnp.zeros((3, rows_c, SLAB), F32)
            packed = lsep_ref[rows, :]
            ls = [jnp.where(lo_c, packed[:, i:i + 1], packed[:, HEAD_DIM + i:HEAD_DIM + i + 1]) for i in range(npat)]
            for i in range(npat):
                lse_ref[i, rows, :] = ls[i]
            if npat == 1:
                dbar_ref[rows, :] = headsum(dov * o_ref[0, rows, :].astype(F32), lo_c)
            else:
                ws = _softmax_weights(ls)
                acc = ws[0] * headsum(dov * o_ref[0, rows, :].astype(F32), lo_c)
                for i in range(1, npat):
                    acc = acc + ws[i] * headsum(dov * o_ref[i, rows, :].astype(F32), lo_c)
                dbar_ref[rows, :] = acc
            return carry

        lax.fori_loop(0, S // rows_c, prep, 0, unroll=2)
        sacc_ref[...] = jnp.zeros((BLOCK, SLAB), F32)

        for pi, (d, maxd, scale) in enumerate(patterns):
            nb = S // d // BLOCK
            base, prev_keys = _band_consts(slopes_ref[2 * p], slopes_ref[2 * p + 1], maxd, scale)

            def blk(t, carry, pi=pi, d=d, nb=nb, base=base, prev_keys=prev_keys):
                r = t // nb
                n = t - r * nb
                start = r + (d * BLOCK) * n
                prev = jnp.where(n > 0, start - d * BLOCK, start)
                q2 = _stack_heads(_ld3(x_ref, 0, start, d), lo).astype(BF16)
                k2 = jnp.concatenate([_ld3(x_ref, 1, prev, d), _ld3(x_ref, 1, start, d)], axis=0).astype(BF16)
                v2 = jnp.concatenate([_ld3(x_ref, 2, prev, d), _ld3(x_ref, 2, start, d)], axis=0).astype(BF16)
                ls = [_ld3(lse_ref, i, start, d) for i in range(npat)]
                w = _softmax_weights(ls)[pi] if npat > 1 else 1.0
                do2 = _stack_heads(w * _ld(do_ref, start, d), lo).astype(BF16)
                dl = w * _ld(dbar_ref, start, d)
                lse2 = jnp.concatenate([ls[pi][:, :1], ls[pi][:, HEAD_DIM:HEAD_DIM + 1]], axis=0)
                dl2 = jnp.concatenate([dl[:, :1], dl[:, HEAD_DIM:HEAD_DIM + 1]], axis=0)
                s = _scores(q2, k2, base, prev_keys, n == 0)
                pr = jnp.exp(s - lse2)
                dp = lax.dot_general(do2, v2, (((1,), (1,)), ((), ())), preferred_element_type=F32)
                ds = (pr * (dp - dl2) * (HEAD_DIM ** -0.5)).astype(BF16)
                dq2 = jnp.dot(ds, k2, preferred_element_type=F32)
                dk2 = lax.dot_general(ds, q2, (((0,), (0,)), ((), ())), preferred_element_type=F32)
                dv2 = lax.dot_general(pr.astype(BF16), do2, (((0,), (0,)), ((), ())), preferred_element_type=F32)
                _acc3(dx_ref, 0, start, d, _unstack_heads(dq2, lo))
                _acc3(dx_ref, 1, prev, d, dk2[:BLOCK])
                _acc3(dx_ref, 1, start, d, dk2[BLOCK:])
                _acc3(dx_ref, 2, prev, d, dv2[:BLOCK])
                _acc3(dx_ref, 2, start, d, dv2[BLOCK:])
                if has_sink:
                    sacc_ref[...] += _unstack_heads(-jnp.exp(sk2 - lse2) * dl2, lo)
                return carry

            lax.fori_loop(0, d * nb, blk, 0, unroll=8)

        dsink_ref[...] = jnp.broadcast_to(jnp.sum(sacc_ref[...], axis=0, keepdims=True), (8, SLAB))

        def emit(ci, carry):
            rows = pl.ds(pl.multiple_of(ci * rows_c, rows_c), rows_c)
            dxo_ref[:, rows, :] = dx_ref[:, rows, :].astype(BF16)
            return carry

        lax.fori_loop(0, S // rows_c, emit, 0, unroll=2)

    smem = pl.BlockSpec(memory_space=pltpu.SMEM)
    return pl.pallas_call(
        body, name=name, grid=(N_SLABS,),
        in_specs=[smem, smem, pl.BlockSpec((3, S, SLAB), lambda p: (0, 0, p)), pl.BlockSpec((S, SLAB), lambda p: (0, p)),
                  pl.BlockSpec((npat, S, SLAB), lambda p: (0, 0, p)), pl.BlockSpec((None, S, SLAB), lambda p: (p, 0, 0))],
        out_specs=[pl.BlockSpec((3, S, SLAB), lambda p: (0, 0, p)), pl.BlockSpec((None, 8, SLAB), lambda p: (p, 0, 0))],
        out_shape=[jax.ShapeDtypeStruct((3, S, D_MODEL), BF16), jax.ShapeDtypeStruct((N_SLABS, 8, SLAB), F32)],
        scratch_shapes=[pltpu.VMEM((S, SLAB), F32), pltpu.VMEM((BLOCK, SLAB), F32), pltpu.VMEM((npat, S, SLAB), F32),
                        pltpu.VMEM((3, S, SLAB), F32)],
        compiler_params=_cparams(("arbitrary",)),
    )(slopes, sinks, qkv, dout, o, lse)


def _place():
    x, y, c = lax.axis_index("x"), lax.axis_index("y"), lax.axis_index("c")
    return x, y, c, 2 * x + y


def _other_chips(x, y):
    return [(1 - x, y), (x, 1 - y), (1 - x, 1 - y)]


HBM_SPEC = pl.BlockSpec(memory_space=pl.ANY)


def _slot(q):
    return 2 * (q % 2) + q // 2


BIG = ("ffn1_w_in", "ffn1_w_out", "ffn2_w_in", "ffn2_w_out", "a_w_qkv", "a_w_o", "kv_w", "b_w_q", "b_w_o")
QKV_SHARD = 3 * D_MODEL // N_CHIPS
ROW_SHARD = D_MODEL // N_CHIPS


LAYER0_ITEMS = (("ffn1_w_in", 0), ("ffn1_w_out", 0), ("a_w_qkv", None), ("a_w_o", None), ("ffn2_w_in", 0),
                ("ffn2_w_out", 0), ("kv_w", None))
LAYER1_ITEMS = (("ffn1_w_in", 1), ("ffn1_w_out", 1), ("b_w_q", None), ("b_w_o", None), ("ffn2_w_in", 1),
                ("ffn2_w_out", 1))
OUT_SHARD = D_FF // N_CHIPS


def _full_shape(name):
    if name.endswith("w_in"):
        return (D_MODEL, 2 * D_FF)
    if name.endswith("w_out"):
        return (D_FF, D_MODEL)
    if name == "a_w_qkv":
        return (D_MODEL, 3 * D_MODEL)
    if name == "kv_w":
        return (N_CHIPS, 2, ROW_SHARD // 2, 2 * N_KV_B * HEAD_DIM)
    return (N_CHIPS, 2, ROW_SHARD // 2, D_MODEL)


def _gather_src(item, ref, c):
    name, layer = item
    if name.endswith("w_in"):
        return ref.at[layer, pl.ds(c * (D_MODEL // 2), D_MODEL // 2)]
    if name.endswith("w_out"):
        return ref.at[layer, pl.ds(c * (OUT_SHARD // 2), OUT_SHARD // 2)]
    if name == "a_w_qkv":
        return ref.at[0, pl.ds(c * (D_MODEL // 2), D_MODEL // 2)]
    if name == "kv_w":
        return ref.at[pl.ds(c * (ROW_SHARD // 2), ROW_SHARD // 2)]
    return ref.at[0, pl.ds(c * (ROW_SHARD // 2), ROW_SHARD // 2)]


def _gather_dst(item, ref, q, c):
    name, _ = item
    if name.endswith("w_in"):
        return ref.at[pl.ds(c * (D_MODEL // 2), D_MODEL // 2), pl.ds(_slot(q) * HALF_FF, HALF_FF)]
    if name.endswith("w_out"):
        return ref.at[pl.ds(q * OUT_SHARD + c * (OUT_SHARD // 2), OUT_SHARD // 2)]
    if name == "a_w_qkv":
        return ref.at[pl.ds(c * (D_MODEL // 2), D_MODEL // 2), pl.ds(q * QKV_SHARD, QKV_SHARD)]
    return ref.at[q, c]


def _all_gather(items, shards, small):
    n = len(items)
    r = small.shape[0]
    per = 8

    def body(*refs):
        srcs, small_ref = refs[:n], refs[n]
        dsts, s_ref = refs[n + 1:2 * n + 1], refs[2 * n + 1]
        send_sems, recv_sems = refs[2 * n + 2:]
        x, y, c, myq = _place()
        sibling = (x, y, 1 - c)
        chips = _other_chips(x, y)

        def big(t, k, src, q, h, to):
            return pltpu.make_async_remote_copy(src_ref=src, dst_ref=_gather_dst(items[t], dsts[t], q, h),
                                                send_sem=send_sems.at[per * t + k], recv_sem=recv_sems.at[per * t + k],
                                                device_id=to, device_id_type=MESH)

        def tiny(k, q, to):
            return pltpu.make_async_remote_copy(src_ref=small_ref, dst_ref=s_ref.at[q], send_sem=send_sems.at[per * n + k],
                                                recv_sem=recv_sems.at[per * n + k], device_id=to, device_id_type=MESH)

        first = []
        for j, chip in enumerate(chips):
            if j < 2:
                first += [big(t, j, _gather_src(items[t], srcs[t], c), myq, c, (*chip, c)) for t in range(n)]
            first.append(tiny(j, myq, (*chip, c)))
        own = [big(t, 6 + h, _gather_src(items[t], srcs[t], h), myq, h, sibling) for t in range(n) for h in (0, 1)]
        own.append(tiny(3, myq, sibling))
        for cp in first + own:
            cp.start()
        relay_from = ((x + 1 - c) % 2, (y + c) % 2)
        relay_to = ((x + c) % 2, (y + 1 - c) % 2, c)
        q_relay = 2 * relay_from[0] + relay_from[1]
        passed = []
        for t in range(n):
            src = _gather_src(items[t], srcs[t], c)
            for j, (cx, cy) in enumerate(chips[:2]):
                q = 2 * cx + cy
                big(t, j, src, q, c, sibling).wait_recv()
                fwd = big(t, 3 + j, _gather_dst(items[t], dsts[t], q, c), q, c, sibling)
                fwd.start()
                passed.append(fwd)
            relay = big(t, 2, _gather_dst(items[t], dsts[t], q_relay, c), q_relay, c, relay_to)
            relay.start()
            passed.append(relay)
        q_diag = 2 * chips[2][0] + chips[2][1]
        for t in range(n):
            big(t, 2, _gather_src(items[t], srcs[t], c), q_diag, c, sibling).wait_recv()
            fwd = big(t, 5, _gather_dst(items[t], dsts[t], q_diag, c), q_diag, c, sibling)
            fwd.start()
            passed.append(fwd)
        for j, (cx, cy) in enumerate(chips):
            q = 2 * cx + cy
            for t in range(n):
                big(t, 3 + j, _gather_src(items[t], srcs[t], c), q, 1 - c, sibling).wait_recv()
            tiny(j, q, sibling).wait_recv()
        for cp in own:
            cp.wait_recv()
        for cp in first + passed + own:
            cp.wait_send()

    outs = pl.pallas_call(
        body, name="all_gather_layer0",
        in_specs=[HBM_SPEC] * (n + 1), out_specs=[HBM_SPEC] * (n + 1),
        out_shape=[jax.ShapeDtypeStruct(_full_shape(name), BF16) for name, _ in items]
        + [jax.ShapeDtypeStruct((N_CHIPS, r, 128), F32)],
        scratch_shapes=[pltpu.SemaphoreType.DMA((per * n + 4,)), pltpu.SemaphoreType.DMA((per * n + 4,))],
    )(*[shards[item] for item in items], small)
    return list(outs[:n]), outs[n]


SEM_SPEC = pl.BlockSpec(memory_space=pltpu.SEMAPHORE)
DATAFLOW = pltpu.SideEffectType.DATAFLOW_SIDE_EFFECTING
PER_ITEM = 8


def _split_start(name, copies, n_sems, sources, land_shapes, after):
    n, m = len(sources), len(land_shapes)

    def body(*refs):
        srcs, lands = refs[:n], refs[n:n + m]
        send_sems, recv_sems = refs[n + m + 1], refs[n + m + 2]
        token = refs[-1]
        for src, dst_there, _, s, peer in copies(srcs, lands):
            pltpu.make_async_remote_copy(src_ref=src, dst_ref=dst_there, send_sem=send_sems.at[s], recv_sem=recv_sems.at[s],
                                         device_id=peer, device_id_type=MESH).start()
        token[...] = jnp.zeros_like(token)

    src_arrays = [pltpu.with_memory_space_constraint(a, pltpu.HBM) for a in sources]
    land_arrays = [pltpu.with_memory_space_constraint(lax.empty(s.shape, s.dtype), pltpu.HBM) for s in land_shapes]
    hbm = pl.BlockSpec(memory_space=pltpu.HBM)
    outs = pl.pallas_call(
        body, name=name,
        in_specs=[hbm] * (n + m) + [HBM_SPEC],
        out_specs=[SEM_SPEC, SEM_SPEC] + [hbm] * (n + m) + [pl.BlockSpec(memory_space=pltpu.VMEM)],
        out_shape=[pltpu.SemaphoreType.DMA((n_sems,)), pltpu.SemaphoreType.DMA((n_sems,))]
        + [pltpu.HBM(a.shape, a.dtype) for a in src_arrays + land_arrays] + [jax.ShapeDtypeStruct((8, 128), F32)],
        input_output_aliases={i: 2 + i for i in range(n + m)},
        compiler_params=pltpu.CompilerParams(has_side_effects=DATAFLOW),
    )(*src_arrays, *land_arrays, after)
    return (outs[0], outs[1], list(outs[2:2 + n]), list(outs[2 + n:2 + n + m])), outs[-1]


def _split_wait(name, copies, state, after):
    send_sems, recv_sems, srcs_thru, lands_thru = state
    n, m = len(srcs_thru), len(lands_thru)
    after = list(after) if isinstance(after, (list, tuple)) else [after]

    def body(*refs):
        srcs, lands = refs[:n], refs[n:n + m]
        send_sems, recv_sems = refs[n + m], refs[n + m + 1]
        for src, _, dst_here, s, peer in copies(srcs, lands):
            cp = pltpu.make_async_remote_copy(src_ref=src, dst_ref=dst_here, send_sem=send_sems.at[s], recv_sem=recv_sems.at[s],
                                              device_id=peer, device_id_type=MESH)
            cp.wait_send()
            cp.wait_recv()

    hbm = pl.BlockSpec(memory_space=pltpu.HBM)
    outs = pl.pallas_call(
        body, name=name,
        in_specs=[hbm] * (n + m) + [SEM_SPEC, SEM_SPEC] + [HBM_SPEC] * len(after),
        out_specs=[hbm] * (n + m),
        out_shape=[pltpu.HBM(a.shape, a.dtype) for a in srcs_thru + lands_thru],
        input_output_aliases={i: i for i in range(n + m)},
        compiler_params=pltpu.CompilerParams(has_side_effects=DATAFLOW),
    )(*srcs_thru, *lands_thru, send_sems, recv_sems, *after)
    return list(outs[:n]), list(outs[n:])


def _gather_copies(items):
    def copies(srcs, lands):
        x, y, c, myq = _place()
        out = []
        for t, item in enumerate(items):
            for h in (0, 1):
                src = _gather_src(item, srcs[t], h)
                for j, (cx, cy) in enumerate(_other_chips(x, y)):
                    out.append((src, _gather_dst(item, lands[t], myq, h), _gather_dst(item, lands[t], 2 * cx + cy, h),
                                PER_ITEM * t + 2 * j + h, (cx, cy, c)))
                out.append((src, _gather_dst(item, lands[t], myq, h), _gather_dst(item, lands[t], myq, h),
                            PER_ITEM * t + 6 + h, (x, y, 1 - c)))
        return out
    return copies


def _gather_start(items, shards, after):
    lands = [jax.ShapeDtypeStruct(_full_shape(name), BF16) for name, _ in items]
    return _split_start("gather_layer1_start", _gather_copies(items), PER_ITEM * len(items),
                        [shards[item] for item in items], lands, after)


def _gather_wait(items, state, after):
    return _split_wait("gather_layer1_wait", _gather_copies(items), state, after)[1]


def _small_all_reduce(v):
    r = v.shape[0]

    def body(v_ref, o_ref, buf_ref, send_sems, recv_sems):
        x, y, c, _ = _place()
        me = 4 * x + 2 * y + c
        buf_ref[me] = v_ref[...]
        copies = []
        for k in range(1, 8):
            fx, fy, fc = (k >> 2) & 1, (k >> 1) & 1, k & 1
            to = (x ^ fx, y ^ fy, c ^ fc)
            cp = pltpu.make_async_remote_copy(src_ref=v_ref, dst_ref=buf_ref.at[me], send_sem=send_sems.at[k - 1],
                                              recv_sem=recv_sems.at[k - 1], device_id=to, device_id_type=MESH)
            cp.start()
            copies.append(cp)
        for k in range(1, 8):
            fx, fy, fc = (k >> 2) & 1, (k >> 1) & 1, k & 1
            src_dev = 4 * (x ^ fx) + 2 * (y ^ fy) + (c ^ fc)
            pltpu.make_async_remote_copy(src_ref=v_ref, dst_ref=buf_ref.at[src_dev], send_sem=send_sems.at[k - 1],
                                         recv_sem=recv_sems.at[k - 1], device_id=(x, y, c), device_id_type=MESH).wait_recv()
        for cp in copies:
            cp.wait_send()
        tot = buf_ref[0]
        for i in range(1, 8):
            tot = tot + buf_ref[i]
        o_ref[...] = tot

    vm = pl.BlockSpec(memory_space=pltpu.VMEM)
    return pl.pallas_call(
        body, name="small_all_reduce", in_specs=[vm], out_specs=vm,
        out_shape=jax.ShapeDtypeStruct((r, 128), F32),
        scratch_shapes=[pltpu.VMEM((8, r, 128), F32), pltpu.SemaphoreType.DMA((7,)), pltpu.SemaphoreType.DMA((7,))],
    )(v)


def _grad_view(kind, g):
    if kind == "col":
        return g.reshape(2, g.shape[0] // 2, g.shape[1])
    return g.reshape(N_CHIPS, 2, g.shape[0] // (2 * N_CHIPS), g.shape[1])


def _half_of(kind, ref, h):
    return ref.at[h] if kind == "col" else ref.at[:, h]


def _half_shape(kind, view_shape):
    return view_shape[1:] if kind == "col" else (view_shape[0],) + view_shape[2:]


def _piece_of(kind, width, colblock, ref, q):
    if kind == "col":
        return ref.at[:, pl.ds(colblock(q) * width, width)]
    return ref.at[q]


def _piece_shape(kind, width, half_shape):
    return (half_shape[0], width) if kind == "col" else half_shape[1:]


def _pair_exchange(views, kinds, name):
    n = len(views)

    def body(*refs):
        ins, outs = refs[:n], refs[n:2 * n]
        send_sems, recv_sems = refs[2 * n:]
        x, y, c, _ = _place()
        cps = []
        for t in range(n):
            cp = pltpu.make_async_remote_copy(src_ref=_half_of(kinds[t], ins[t], 1 - c), dst_ref=outs[t],
                                              send_sem=send_sems.at[t], recv_sem=recv_sems.at[t],
                                              device_id=(x, y, 1 - c), device_id_type=MESH)
            cp.start()
            cps.append(cp)
        for cp in cps:
            cp.wait()

    return pl.pallas_call(
        body, name=name, in_specs=[HBM_SPEC] * n, out_specs=[HBM_SPEC] * n,
        out_shape=[jax.ShapeDtypeStruct(_half_shape(k, v.shape), v.dtype) for k, v in zip(kinds, views)],
        scratch_shapes=[pltpu.SemaphoreType.DMA((n,)), pltpu.SemaphoreType.DMA((n,))],
    )(*views)


def _pair_sum(kind, view, recv, c, name):
    hs = recv.shape
    N = hs[-1]
    rows = hs[-2]
    tr = _pick(rows, (512, 352, 128))
    tn = _pick(N, (1408, 1024, 512))

    def body(c_ref, p_ref, r_ref, s_ref):
        s_ref[...] = (p_ref[...] + r_ref[...]).astype(BF16)

    if kind == "col":
        grid = (rows // tr, N // tn)
        mine = pl.BlockSpec((None, tr, tn), lambda i, j, c_ref: (c_ref[0], i, j))
        blk = pl.BlockSpec((tr, tn), lambda i, j, c_ref: (i, j))
        sem = ("parallel", "parallel")
    else:
        grid = (N_CHIPS, rows // tr, N // tn)
        mine = pl.BlockSpec((None, None, tr, tn), lambda q, i, j, c_ref: (q, c_ref[0], i, j))
        blk = pl.BlockSpec((None, tr, tn), lambda q, i, j, c_ref: (q, i, j))
        sem = ("parallel", "parallel", "parallel")
    return pl.pallas_call(
        body, name=name,
        grid_spec=pltpu.PrefetchScalarGridSpec(num_scalar_prefetch=1, grid=grid, in_specs=[mine, blk], out_specs=blk),
        out_shape=jax.ShapeDtypeStruct(hs, BF16),
        compiler_params=_cparams(sem),
    )(c.reshape(1).astype(jnp.int32), view, recv)


def _chip_copies(kinds, widths, colblocks):
    def copies(srcs, lands):
        x, y, c, _ = _place()
        out = []
        for j, (cx, cy) in enumerate(_other_chips(x, y)):
            for t in range(len(kinds)):
                out.append((_piece_of(kinds[t], widths[t], colblocks[t], srcs[t], 2 * cx + cy), lands[t].at[j],
                            lands[t].at[j], 3 * t + j, (cx, cy, c)))
        return out
    return copies


def _chip_land_shapes(sums, kinds, widths):
    return [jax.ShapeDtypeStruct((3,) + _piece_shape(k, w, s.shape), BF16) for k, w, s in zip(kinds, widths, sums)]


def _chip_exchange(sums, kinds, widths, colblocks, name):
    n = len(sums)
    copies = _chip_copies(kinds, widths, colblocks)

    def body(*refs):
        send_sems, recv_sems = refs[2 * n:]
        cps = [pltpu.make_async_remote_copy(src_ref=src, dst_ref=dst, send_sem=send_sems.at[s], recv_sem=recv_sems.at[s],
                                            device_id=peer, device_id_type=MESH)
               for src, dst, _, s, peer in copies(refs[:n], refs[n:2 * n])]
        for cp in cps:
            cp.start()
        for cp in cps:
            cp.wait()

    return pl.pallas_call(
        body, name=name, in_specs=[HBM_SPEC] * n, out_specs=[HBM_SPEC] * n,
        out_shape=_chip_land_shapes(sums, kinds, widths),
        scratch_shapes=[pltpu.SemaphoreType.DMA((3 * n,)), pltpu.SemaphoreType.DMA((3 * n,))],
    )(*sums)


N_DIRECT = 7


def _direct_piece(kind, width, colblock, view_ref, q, h):
    if kind == "col":
        return view_ref.at[h, :, pl.ds(colblock(q) * width, width)]
    return view_ref.at[q, h]


def _direct_copies(kinds, widths, colblocks):
    def copies(srcs, lands):
        x, y, c, myq = _place()
        out = []
        for t in range(len(kinds)):
            def piece(q, h, t=t):
                return _direct_piece(kinds[t], widths[t], colblocks[t], srcs[t], q, h)
            for j, (cx, cy) in enumerate(_other_chips(x, y)):
                for h in (0, 1):
                    out.append((piece(2 * cx + cy, h), lands[t].at[2 * j + c], lands[t].at[2 * j + h],
                                10 * t + 3 * j + c + h, (cx, cy, h)))
            out.append((piece(myq, 1 - c), lands[t].at[6], lands[t].at[6], 10 * t + 9, (x, y, 1 - c)))
        return out
    return copies


def _chip_sum(kind, own_src, recv, block_idx, c, shard_shape, layer, into, name, direct=False):
    n_recv, rows, N = recv.shape
    tr = _pick(rows, (512, 352, 128))
    tn = _pick(N, (1408, 1024, 768, 512))
    ni, nj = rows // tr, N // tn

    def body(q_ref, s_ref, r_ref, *rest):
        o_ref = rest[-1]
        tot = s_ref[...].astype(F32)
        for k in range(n_recv):
            tot = tot + r_ref[k].astype(F32)
        o_ref[...] = tot

    if direct and kind == "col":
        own = pl.BlockSpec((None, tr, tn), lambda i, j, q_ref: (q_ref[1], i, q_ref[0] * nj + j))
    elif direct:
        own = pl.BlockSpec((None, None, tr, tn), lambda i, j, q_ref: (q_ref[0], q_ref[1], i, j))
    elif kind == "col":
        own = pl.BlockSpec((tr, tn), lambda i, j, q_ref: (i, q_ref[0] * nj + j))
    else:
        own = pl.BlockSpec((None, tr, tn), lambda i, j, q_ref: (q_ref[0], i, j))
    if len(shard_shape) == 3:
        lead = 0 if layer is None else layer
        out_spec = pl.BlockSpec((None, tr, tn), lambda i, j, q_ref: (lead, q_ref[1] * ni + i, j))
    else:
        out_spec = pl.BlockSpec((tr, tn), lambda i, j, q_ref: (q_ref[1] * ni + i, j))
    in_specs = [own, pl.BlockSpec((n_recv, tr, tn), lambda i, j, q_ref: (0, i, j))]
    s = own_src
    args = [jnp.stack([block_idx, c]).astype(jnp.int32), s, recv]
    aliases = {}
    if into is not None:
        in_specs.append(HBM_SPEC)
        args.append(into)
        aliases = {3: 0}
    return pl.pallas_call(
        body, name=name,
        grid_spec=pltpu.PrefetchScalarGridSpec(num_scalar_prefetch=1, grid=(ni, nj), in_specs=in_specs, out_specs=out_spec),
        out_shape=jax.ShapeDtypeStruct(shard_shape, F32), input_output_aliases=aliases,
        compiler_params=_cparams(("parallel", "parallel")),
    )(*args)


def _half_window(ref, h):
    rows = ref.shape[-2] // 2
    if ref.ndim == 3:
        return ref.at[:, pl.ds(h * rows, rows)]
    return ref.at[pl.ds(h * rows, rows)]


def _share_halves(grads, name):
    n = len(grads)

    def body(*refs):
        outs = refs[n:2 * n]
        send_sems, recv_sems = refs[2 * n:]
        x, y, c, _ = _place()
        cps = []
        for t in range(n):
            cp = pltpu.make_async_remote_copy(src_ref=_half_window(outs[t], c), dst_ref=_half_window(outs[t], c),
                                              send_sem=send_sems.at[t], recv_sem=recv_sems.at[t],
                                              device_id=(x, y, 1 - c), device_id_type=MESH)
            cp.start()
            cps.append(cp)
        for t in range(n):
            cps[t].wait_send()
            pltpu.make_async_remote_copy(src_ref=_half_window(outs[t], c), dst_ref=_half_window(outs[t], 1 - c),
                                         send_sem=send_sems.at[t], recv_sem=recv_sems.at[t],
                                         device_id=(x, y, 1 - c), device_id_type=MESH).wait_recv()

    return pl.pallas_call(
        body, name=name, in_specs=[HBM_SPEC] * n, out_specs=[HBM_SPEC] * n,
        out_shape=[jax.ShapeDtypeStruct(g.shape, F32) for g in grads],
        input_output_aliases={t: t for t in range(n)},
        scratch_shapes=[pltpu.SemaphoreType.DMA((n,)), pltpu.SemaphoreType.DMA((n,))],
    )(*grads)


def _adamw(w, g, m, v, name):
    R, W = w.shape
    tr = _pick(R, (512, 352, 256, 32))

    def body(w_ref, g_ref, m_ref, v_ref, d_ref, nm_ref, nv_ref):
        gv = g_ref[...]
        nm = ADAM_B1 * m_ref[...] + (1.0 - ADAM_B1) * gv
        nv = ADAM_B2 * v_ref[...] + (1.0 - ADAM_B2) * (gv * gv)
        m_hat = nm / (1.0 - ADAM_B1 ** ADAM_STEP)
        v_hat = nv / (1.0 - ADAM_B2 ** ADAM_STEP)
        d_ref[...] = -ADAM_LR * (m_hat / (jnp.sqrt(v_hat) + ADAM_EPS) + ADAM_WD * w_ref[...])
        nm_ref[...] = nm
        nv_ref[...] = nv

    blk = pl.BlockSpec((tr, W), lambda i: (i, 0))
    shp = jax.ShapeDtypeStruct((R, W), F32)
    return pl.pallas_call(
        body, name=name, grid=(R // tr,), in_specs=[blk] * 4, out_specs=[blk] * 3, out_shape=[shp] * 3,
        compiler_params=_cparams(("parallel",)),
    )(w, g, m, v)


SMALL_ROWS = 32


def _pack_small(ln_g, ln_b, sinks):
    rows = jnp.concatenate([ln_g.reshape(-1, 128), ln_b.reshape(-1, 128),
                            jnp.pad(sinks.reshape(1, -1), ((0, 0), (0, 128 - sinks.size)))], axis=0)
    return jnp.pad(rows, ((0, SMALL_ROWS - rows.shape[0]), (0, 0)))


def _unpack_small(s, ln_shape, sink_shape):
    n = ln_shape[0] * ln_shape[1] * ln_shape[2] // 128
    return s[:n].reshape(ln_shape), s[n:2 * n].reshape(ln_shape), s[2 * n, :sink_shape[1]].reshape(sink_shape)


def _ffn_fwd(xin, w_in, w_out, gain, bias, tag):
    u, h = _ffn_in(xin, w_in, "ffn_in_" + tag)
    y, yb, z = _mm_ln(h, w_out, xin, gain, bias, 0.5, "ffn_out_ln_" + tag)
    return y, yb, dict(u=u, h=h, z=z, xin=xin)


def _ffn_bwd(dz, dzc, saved, w_in, w_out, xin_b, tag, dw_dtype=F32, ln=None):
    du = _ffn_bwd_h(dzc, w_out, saved["u"], "ffn_bwd_h_" + tag)
    d_w_out = _mm_tn(saved["h"], dzc, "ffn_dwout_" + tag, out_dtype=dw_dtype)
    d_w_in = _mm_tn(xin_b, du, "ffn_dwin_" + tag, out_dtype=dw_dtype)
    dx = _mm_nt(du, w_in, "ffn_dx_" + tag, add=dz, add_scale=ALPHA, ln=ln)
    return dx, d_w_in, d_w_out


def kernel(x, ffn1_w_in, ffn1_w_out, ffn2_w_in, ffn2_w_out, ln_g, ln_b, a_w_qkv, a_w_o, kv_w, b_w_q, b_sinks, b_w_o, loss_target, m_ffn1_w_in, m_ffn1_w_out, m_ffn2_w_in, m_ffn2_w_out, m_ln_g, m_ln_b, m_a_w_qkv, m_a_w_o, m_kv_w, m_b_w_q, m_b_sinks, m_b_w_o, v_ffn1_w_in, v_ffn1_w_out, v_ffn2_w_in, v_ffn2_w_out, v_ln_g, v_ln_b, v_a_w_qkv, v_a_w_o, v_kv_w, v_b_w_q, v_b_sinks, v_b_w_o):
    ws = dict(ffn1_w_in=ffn1_w_in, ffn1_w_out=ffn1_w_out, ffn2_w_in=ffn2_w_in, ffn2_w_out=ffn2_w_out, a_w_qkv=a_w_qkv,
              a_w_o=a_w_o, kv_w=kv_w, b_w_q=b_w_q, b_w_o=b_w_o)
    ms = dict(ffn1_w_in=m_ffn1_w_in, ffn1_w_out=m_ffn1_w_out, ffn2_w_in=m_ffn2_w_in, ffn2_w_out=m_ffn2_w_out,
              a_w_qkv=m_a_w_qkv, a_w_o=m_a_w_o, kv_w=m_kv_w, b_w_q=m_b_w_q, b_w_o=m_b_w_o)
    vs = dict(ffn1_w_in=v_ffn1_w_in, ffn1_w_out=v_ffn1_w_out, ffn2_w_in=v_ffn2_w_in, ffn2_w_out=v_ffn2_w_out,
              a_w_qkv=v_a_w_qkv, a_w_o=v_a_w_o, kv_w=v_kv_w, b_w_q=v_b_w_q, b_w_o=v_b_w_o)
    _, _, c_idx, myq = _place()
    xs = x[0]
    target = loss_target[0]

    shards = {(n, l): ws[n].astype(BF16) for n, l in LAYER0_ITEMS + LAYER1_ITEMS}

    def as_weights(items, arrays):
        return {n: (a.reshape(D_MODEL, a.shape[-1]) if a.ndim == 4 else a) for (n, _), a in zip(items, arrays)}

    full0, small = _all_gather(LAYER0_ITEMS, shards, _pack_small(ln_g, ln_b, b_sinks))
    gather_state, token = _gather_start(LAYER1_ITEMS, shards, small)

    def layer1_weights(after):
        return as_weights(LAYER1_ITEMS, _gather_wait(LAYER1_ITEMS, gather_state, after))

    n_ln = ln_g.size // 128
    lg = jnp.concatenate([small[q, :n_ln].reshape(DEPTH, 3, 1, -1) for q in range(N_CHIPS)], axis=-1)
    lb = jnp.concatenate([small[q, n_ln:2 * n_ln].reshape(DEPTH, 3, 1, -1) for q in range(N_CHIPS)], axis=-1)
    lg = lg + token[0, 0]
    reducer = _GradReducer(c_idx, myq, {n: ws[n].shape for n in BIG})
    sq, grad_x, _, gg, gb, dsink_part = _local_step(xs, target, as_weights(LAYER0_ITEMS, full0), layer1_weights,
                                                    lg, lb, b_sinks.reshape(N_HEADS), reducer.begin)

    loss_row = jnp.pad(jnp.sum(sq).reshape(1, 1), ((0, 0), (0, 127)))
    dsinks = jnp.pad(dsink_part[:, 0, :].reshape(N_SLABS, 2, HEAD_DIM)[:, :, 0].reshape(1, N_HEADS), ((0, 0), (0, 128 - N_HEADS)))
    gg_full = jnp.stack([jnp.stack([jnp.sum(gg[i][j], axis=0) for j in range(3)]) for i in range(DEPTH)])
    gb_full = jnp.stack([jnp.stack([jnp.sum(gb[i][j], axis=0) for j in range(3)]) for i in range(DEPTH)])
    small_in = jnp.concatenate([loss_row, dsinks, gg_full.reshape(-1, 128), gb_full.reshape(-1, 128)], axis=0)
    small_in = jnp.pad(small_in, ((0, (-small_in.shape[0]) % 8), (0, 0)))
    small_sum = _small_all_reduce(small_in)
    loss = small_sum[0, 0] * (0.5 / D_MODEL)
    grad_sinks = small_sum[1, :N_HEADS].reshape(b_sinks.shape)
    n_full = DEPTH * 3 * D_MODEL // 128
    cols = D_MODEL // N_CHIPS
    grad_ln_g = lax.dynamic_slice_in_dim(small_sum[2:2 + n_full].reshape(DEPTH, 3, D_MODEL), myq * cols, cols, axis=2)
    grad_ln_b = lax.dynamic_slice_in_dim(small_sum[2 + n_full:2 + 2 * n_full].reshape(DEPTH, 3, D_MODEL), myq * cols, cols, axis=2)
    return _update(reducer, grad_x, loss, grad_ln_g, grad_ln_b, grad_sinks, ws, ms, vs,
                   (ln_g, ln_b, b_sinks), (m_ln_g, m_ln_b, m_b_sinks), (v_ln_g, v_ln_b, v_b_sinks))


def _local_step(xs, target, W, layer1_weights, lg, lb, sinks, grads_ready=None):
    if grads_ready is None:
        grads_ready = lambda tag, grads, overlap: 0.0
    S = xs.shape[0]
    slopes = jnp.asarray(_alibi_slopes(N_HEADS))
    in1, out1, in2, out2 = [W["ffn1_w_in"]], [W["ffn1_w_out"]], [W["ffn2_w_in"]], [W["ffn2_w_out"]]

    y1, y1b, s1 = _ffn_fwd(xs, in1[0], out1[0], lg[0, 0], lb[0, 0], "a1")
    qkv_a = _mm_nn(y1b, W["a_w_qkv"], F32, "qkv_a", split=True)
    mix_a, o_a, lse_a = _attn_fwd(qkv_a, slopes, None, PATTERNS_A, "attn_a_fwd")
    y2, y2b, z2 = _mm_ln(mix_a, W["a_w_o"], y1, lg[0, 1], lb[0, 1], 1.0, "attn_a_out_ln")
    y3, y3b, s3 = _ffn_fwd(y2, in2[0], out2[0], lg[0, 2], lb[0, 2], "a2")
    kv_w_rep = jnp.broadcast_to(W["kv_w"].reshape(D_MODEL, 2, N_KV_B, 1, HEAD_DIM),
                                (D_MODEL, 2, N_KV_B, GROUP_B, HEAD_DIM)).reshape(D_MODEL, 2 * D_MODEL)
    kv_rep = _mm_nn(y3b, kv_w_rep, F32, "kv_proj", split=(1, 2))
    W = dict(W, **layer1_weights(kv_rep))
    in1, out1, in2, out2 = (in1 + [W["ffn1_w_in"]], out1 + [W["ffn1_w_out"]], in2 + [W["ffn2_w_in"]],
                            out2 + [W["ffn2_w_out"]])
    y4, y4b, s4 = _ffn_fwd(y3, in1[1], out1[1], lg[1, 0], lb[1, 0], "b1")
    qkv_b = _mm_nn(y4b, W["b_w_q"], F32, "q_b", split=(0, 1), into=kv_rep)
    mix_b, o_b, lse_b = _attn_fwd(qkv_b, slopes, sinks, PATTERNS_B, "attn_b_fwd")
    y5, y5b, z5 = _mm_ln(mix_b, W["b_w_o"], y4, lg[1, 1], lb[1, 1], 1.0, "attn_b_out_ln")
    y6, _, s6 = _ffn_fwd(y5, in2[1], out2[1], lg[1, 2], lb[1, 2], "b2")

    gr = {n: None for n in BIG}
    gg = [[None] * 3 for _ in range(DEPTH)]
    gb = [[None] * 3 for _ in range(DEPTH)]
    dz6, dz6c, gg[1][2], gb[1][2], sq = _loss_ln_bwd(y6, target, s6["z"], lg[1, 2], 0.5, "loss_ln_bwd")

    (dz5, dz5b, gg[1][1], gb[1][1]), d_in2_b, d_out2_b = _ffn_bwd(dz6, dz6c, s6, in2[1], out2[1], y5b, "b2", BF16,
                                                                  ln=(z5, lg[1, 1], 1.0))
    gr["b_w_o"] = _mm_tn(mix_b, dz5b, "d_b_w_o", out_dtype=BF16)
    dmix_b = _mm_nt(dz5b, W["b_w_o"], "d_mix_b")
    dqkv_b, dsink_part = _attn_bwd(qkv_b, dmix_b, o_b, lse_b, slopes, sinks, PATTERNS_B, "attn_b_bwd")
    dq_b = (dqkv_b, 0)
    gr["b_w_q"] = _mm_tn(y4b, dq_b, "d_b_w_q", out_dtype=BF16)
    dz4, dz4c, gg[1][0], gb[1][0] = _mm_nt(dq_b, W["b_w_q"], "d_y4", add=dz5, add_scale=ALPHA, ln=(s4["z"], lg[1, 0], 0.5))
    dy3, d_in1_b, d_out1_b = _ffn_bwd(dz4, dz4c, s4, in1[1], out1[1], y3b, "b1", BF16)
    gr["kv_w"] = _d_kv_w(y3b, dqkv_b, "d_kv_w")
    tok = grads_ready("l1", {("ffn2_w_in", 1): d_in2_b, ("ffn2_w_out", 1): d_out2_b, ("b_w_o", None): gr["b_w_o"],
                             ("b_w_q", None): gr["b_w_q"], ("ffn1_w_in", 1): d_in1_b, ("ffn1_w_out", 1): d_out1_b,
                             ("kv_w", None): gr["kv_w"]}, True)
    lg0 = lg[0] + tok
    dz3, dz3c, gg[0][2], gb[0][2] = _mm_nt(dqkv_b, kv_w_rep, "d_y3_kv", add=dy3, add_scale=1.0, split=(1, 2),
                                           ln=(s3["z"], lg0[2], 0.5))

    (dz2, dz2b, gg[0][1], gb[0][1]), d_in2_a, d_out2_a = _ffn_bwd(dz3, dz3c, s3, in2[0], out2[0], y2b, "a2", BF16,
                                                                  ln=(z2, lg0[1], 1.0))
    tok = grads_ready("a2", {("ffn2_w_in", 0): d_in2_a, ("ffn2_w_out", 0): d_out2_a}, True)
    lg0 = lg0 + tok
    gr["a_w_o"] = _mm_tn(mix_a, dz2b, "d_a_w_o", out_dtype=BF16)
    dmix_a = _mm_nt(dz2b, W["a_w_o"], "d_mix_a")
    dqkv_a, _ = _attn_bwd(qkv_a, dmix_a, o_a, lse_a, slopes, None, PATTERNS_A, "attn_a_bwd")
    gr["a_w_qkv"] = _mm_tn(y1b, dqkv_a, "d_a_w_qkv", split=True, out_dtype=BF16)
    tok = grads_ready("mix", {("a_w_o", None): gr["a_w_o"], ("a_w_qkv", None): gr["a_w_qkv"]}, True)
    lg0 = lg0 + tok
    dz1, dz1c, gg[0][0], gb[0][0] = _mm_nt(dqkv_a, W["a_w_qkv"], "d_y1", add=dz2, add_scale=ALPHA, split=True,
                                           ln=(s1["z"], lg0[0], 0.5))
    grad_x, d_in1_a, d_out1_a = _ffn_bwd(dz1, dz1c, s1, in1[0], out1[0], xs, "a1", BF16)
    grads_ready("a1", {("ffn1_w_in", 0): d_in1_a, ("ffn1_w_out", 0): d_out1_a}, True)
    gr["ffn1_w_in"] = [d_in1_a, d_in1_b]
    gr["ffn1_w_out"] = [d_out1_a, d_out1_b]
    gr["ffn2_w_in"] = [d_in2_a, d_in2_b]
    gr["ffn2_w_out"] = [d_out2_a, d_out2_b]
    return sq, grad_x, gr, gg, gb, dsink_part


def _grad_item(name, layer, g):
    if name.endswith("w_in"):
        return (g, "col", HALF_FF, _slot, name, layer)
    if name.endswith("w_out"):
        return (g, "row", D_MODEL, None, name, layer)
    if name == "a_w_qkv":
        return (g, "col", QKV_SHARD, lambda q: q, name, None)
    return (g, "row", g.shape[1], None, name, None)


class _GradReducer:
    def __init__(self, c_idx, myq, shard_shapes):
        self.c_idx, self.myq, self.shard_shapes = c_idx, myq, shard_shapes
        self.groups = []

    def begin(self, tag, grads, overlap):
        items = [_grad_item(n, l, g) for (n, l), g in grads.items()]
        kinds, widths, colblocks = [it[1] for it in items], [it[2] for it in items], [it[3] for it in items]
        views = [_grad_view(k, it[0]) for k, it in zip(kinds, items)]
        if overlap:
            lands = [jax.ShapeDtypeStruct((N_DIRECT,) + _piece_shape(k, w, _half_shape(k, v.shape)), BF16)
                     for k, w, v in zip(kinds, widths, views)]
            state, token = _split_start("grad_direct_start_" + tag, _direct_copies(kinds, widths, colblocks), 10 * len(items),
                                        views, lands, views[-1])
            self.groups.append((tag, items, None, state, token))
            return token[0, 0]
        from_sibling = _pair_exchange(views, kinds, "grad_pair_exchange_" + tag)
        sums = [_pair_sum(k, v, r, self.c_idx, "pair_sum_%s_%d" % (tag, t))
                for t, (k, v, r) in enumerate(zip(kinds, views, from_sibling))]
        self.groups.append((tag, items, sums, None, None))
        return 0.0

    def _sum_group(self, tag, items, sums, received, direct):
        for t, (it, s, r) in enumerate(zip(items, sums, received)):
            _, k, _, cb, name, layer = it
            own = cb(self.myq) if k == "col" else self.myq
            self.half_done[name] = _chip_sum(k, s, r, own, self.c_idx, self.shard_shapes[name], layer,
                                             self.half_done.get(name), "chip_sum_%s_%d" % (tag, t), direct=direct)

    def finish_first(self, after):
        self.half_done, self.late, early = {}, [], []
        started = [after]
        for g, (tag, items, sums, state, token) in enumerate(self.groups):
            kinds, widths, colblocks = [it[1] for it in items], [it[2] for it in items], [it[3] for it in items]
            if state is None:
                copies = _chip_copies(kinds, widths, colblocks)
                state, token = _split_start("grad_chip_start_" + tag, copies, 3 * len(items), sums,
                                            _chip_land_shapes(sums, kinds, widths), sums[-1])
                self.late.append((tag, items, copies, state, False))
                started.append(token)
            elif g == len(self.groups) - 1:
                self.late.append((tag, items, _direct_copies(kinds, widths, colblocks), state, True))
                started.append(token)
            else:
                early.append((tag, items, _direct_copies(kinds, widths, colblocks), state))
        for tag, items, copies, state in early:
            views, received = _split_wait("grad_direct_wait_" + tag, copies, state, started)
            self._sum_group(tag, items, views, received, True)
        late_names = {it[4] for _, items, _, _, _ in self.late for it in items}
        names = [n for n in BIG if n not in late_names]
        return dict(zip(names, _share_halves([self.half_done[n] for n in names], "grad_share_halves_first")))

    def finish_rest(self, after):
        names = []
        for tag, items, copies, state, direct in self.late:
            sums, received = _split_wait("grad_late_wait_" + tag, copies, state, after)
            self._sum_group(tag, items, sums, received, direct)
            names += [it[4] for it in items if it[4] not in names]
        return dict(zip(names, _share_halves([self.half_done[n] for n in names], "grad_share_halves_rest")))


def _update(reducer, grad_x, loss, grad_ln_g, grad_ln_b, grad_sinks, ws, ms, vs, small_w, small_m, small_v):
    ln_g, ln_b, b_sinks = small_w
    m_ln_g, m_ln_b, m_b_sinks = small_m
    v_ln_g, v_ln_b, v_b_sinks = small_v

    deltas, new_m, new_v = {}, {}, {}

    def update(some):
        done = []
        for name in some:
            shp = ws[name].shape
            flat = lambda a: a.reshape(-1, shp[-1])
            d, nm, nv = _adamw(flat(ws[name]), flat(some[name]), flat(ms[name]), flat(vs[name]), "adamw_" + name)
            deltas[name], new_m[name], new_v[name] = d.reshape(shp), nm.reshape(shp), nv.reshape(shp)
            done.append(d)
        return done

    grads = reducer.finish_first(grad_x)
    rest = reducer.finish_rest(update(grads))
    update(rest)
    grads.update(rest)
    delta_s, nm_s, nv_s = _adamw(_pack_small(ln_g, ln_b, b_sinks), _pack_small(grad_ln_g, grad_ln_b, grad_sinks),
                                 _pack_small(m_ln_g, m_ln_b, m_b_sinks), _pack_small(v_ln_g, v_ln_b, v_b_sinks), "adamw_small")
    for d, blob in ((grads, None), (deltas, delta_s), (new_m, nm_s), (new_v, nv_s)):
        if blob is None:
            d["ln_g"], d["ln_b"], d["b_sinks"] = grad_ln_g, grad_ln_b, grad_sinks
        else:
            d["ln_g"], d["ln_b"], d["b_sinks"] = _unpack_small(blob, ln_g.shape, b_sinks.shape)

    order = ("ffn1_w_in", "ffn1_w_out", "ffn2_w_in", "ffn2_w_out", "ln_g", "ln_b", "a_w_qkv", "a_w_o", "kv_w", "b_w_q",
             "b_sinks", "b_w_o")
    outs = [loss, grad_x[None]]
    for d in (grads, deltas, new_m, new_v):
        outs += [d[n] for n in order]
    return tuple(outs)
```

```python
import numpy as np
import jax
import jax.numpy as jnp
from jax import lax
from jax.experimental import pallas as pl
from jax.experimental.pallas import tpu as pltpu

F32 = jnp.float32
BF16 = jnp.bfloat16

D_MODEL = 1024
D_FF = 2816
HALF_FF = D_FF // 2
HEAD_DIM = 64
N_HEADS = 16
N_KV_B = 4
GROUP_B = N_HEADS // N_KV_B
DEPTH = 2
ALPHA = (2.0 * DEPTH) ** 0.25
LN_EPS = 1e-5
BLOCK = 128
SLAB = 128
N_SLABS = D_MODEL // SLAB
PATTERNS_A = ((1, 128, 1.0), (4, 128, 4.0), (16, 128, 16.0))
PATTERNS_B = ((1, 127, 1.0),)
NEG = -1e30

ADAM_LR = 0.001
ADAM_B1 = 0.9
ADAM_B2 = 0.999
ADAM_EPS = 1e-08
ADAM_WD = 0.01
ADAM_STEP = 10

N_CHIPS = 4
VMEM_LIMIT = 56 * 1024 * 1024
MESH = pl.DeviceIdType.MESH


def _alibi_slopes(n):
    return np.array([2.0 ** (-8.0 * (h + 1) / n) for h in range(n)], dtype=np.float32)


def _cparams(sem=None, vmem=VMEM_LIMIT):
    return pltpu.CompilerParams(dimension_semantics=sem, vmem_limit_bytes=vmem)


_DIMS = {"nn": ((1,), (0,)), "nt": ((1,), (1,)), "tn": ((0,), (0,))}


def _unlead(x):
    if isinstance(x, tuple):
        return x[0], x[1], x[0].shape[1:]
    return x, None, x.shape


def _bspec(block, imap, lead=None):
    if lead is None:
        return pl.BlockSpec(block, imap)
    return pl.BlockSpec((None,) + tuple(block), lambda *g: (lead,) + tuple(imap(*g)))


def _ln_bwd_math(zv, dyv, gain):
    rows = zv.shape[0]
    mu = jnp.mean(zv, axis=-1, keepdims=True)
    zc = zv - mu
    var = jnp.mean(zc * zc, axis=-1, keepdims=True)
    rstd = lax.rsqrt(var + LN_EPS)
    xhat = zc * rstd
    dyg = dyv * gain
    m1 = jnp.mean(dyg, axis=-1, keepdims=True)
    m2 = jnp.mean(dyg * xhat, axis=-1, keepdims=True)
    dz = rstd * (dyg - m1 - xhat * m2)
    pg = jnp.sum((dyv * xhat).reshape(rows // 8, 8, D_MODEL), axis=0)
    pb = jnp.sum(dyv.reshape(rows // 8, 8, D_MODEL), axis=0)
    return dz, pg, pb


def _matmul(a, b, mode, out_dtype, tm, tn, tk, name, add=None, add_scale=1.0, split=False, into=None, ln=None):
    out_spec = pl.BlockSpec((tm, tn), lambda i, j, k: (i, j))
    base, count = (0, 3) if split is True else (split or (0, 0))
    if mode == "nn":
        a, al, (M, K) = _unlead(a)
        b, bl, (K2, N) = _unlead(b)
        a_spec = _bspec((tm, tk), lambda i, j, k: (i, k), al)
        b_spec = _bspec((tk, tn), lambda i, j, k: (k, j), bl)
        out_struct = jax.ShapeDtypeStruct((M, N), out_dtype)
        if split:
            assert tn == D_MODEL and N == count * tn
            out_spec = pl.BlockSpec((None, tm, tn), lambda i, j, k: (j + base, i, 0))
            out_struct = jax.ShapeDtypeStruct((3, M, tn), out_dtype)
    elif mode == "nt":
        b, bl, (N, K2) = _unlead(b)
        if split:
            assert tk == D_MODEL
            M, K = a.shape[1], count * a.shape[2]
            a_spec = pl.BlockSpec((None, tm, tk), lambda i, j, k: (k + base, i, 0))
        else:
            a, al, (M, K) = _unlead(a)
            a_spec = _bspec((tm, tk), lambda i, j, k: (i, k), al)
        b_spec = _bspec((tn, tk), lambda i, j, k: (j, k), bl)
        out_struct = jax.ShapeDtypeStruct((M, N), out_dtype)
    else:
        a, al, (K, M) = _unlead(a)
        if split:
            assert tn == D_MODEL
            K2, N = b.shape[1], count * b.shape[2]
            b_spec = pl.BlockSpec((None, tk, tn), lambda i, j, k: (j + base, k, 0))
        else:
            b, bl, (K2, N) = _unlead(b)
            b_spec = _bspec((tk, tn), lambda i, j, k: (k, j), bl)
        a_spec = _bspec((tk, tm), lambda i, j, k: (k, i), al)
        out_struct = jax.ShapeDtypeStruct((M, N), out_dtype)
    assert K == K2 and M % tm == 0 and N % tn == 0 and K % tk == 0, (a.shape, b.shape, mode, tm, tn, tk)
    nk = K // tk
    dims = (_DIMS[mode], ((), ()))
    has_add = add is not None

    narrow = out_dtype != F32
    assert not (narrow and has_add)
    if ln is not None:
        assert has_add and mode == "nt" and tn == N == D_MODEL

    def body(*refs):
        if into is not None:
            refs = refs[:2] + refs[3:]
        if ln is not None:
            a_ref, b_ref, add_ref, z_ref, g_ref, o_ref, dzc_ref, gg_ref, gb_ref = refs
            acc_ref = o_ref
        elif has_add:
            a_ref, b_ref, add_ref, o_ref = refs
            acc_ref = o_ref
        elif narrow:
            a_ref, b_ref, o_ref, acc_ref = refs
        else:
            a_ref, b_ref, o_ref = refs
            acc_ref = o_ref
        k = pl.program_id(2)
        part = lax.dot_general(a_ref[...].astype(BF16), b_ref[...].astype(BF16), dims, preferred_element_type=F32)
        if has_add:
            @pl.when(k == 0)
            def _():
                acc_ref[...] = part + add_scale * add_ref[...]
        else:
            @pl.when(k == 0)
            def _():
                acc_ref[...] = part

        @pl.when(k > 0)
        def _():
            acc_ref[...] += part

        if narrow:
            @pl.when(k == nk - 1)
            def _():
                o_ref[...] = acc_ref[...].astype(out_dtype)

        if ln is not None:
            @pl.when(k == nk - 1)
            def _():
                dz, pg, pb = _ln_bwd_math(z_ref[...], o_ref[...], g_ref[...])
                o_ref[...] = dz
                dzc_ref[...] = (ln[2] * dz).astype(BF16)
                first = pl.program_id(0) == 0

                @pl.when(first)
                def _():
                    gg_ref[...] = pg
                    gb_ref[...] = pb

                @pl.when(jnp.logical_not(first))
                def _():
                    gg_ref[...] += pg
                    gb_ref[...] += pb

    in_specs = [a_spec, b_spec]
    args = [a, b]
    aliases = {}
    if into is not None:
        assert mode == "nn" and split and not has_add
        in_specs.append(pl.BlockSpec(memory_space=pl.ANY))
        args.append(into)
        aliases = {2: 0}
    if has_add:
        in_specs.append(pl.BlockSpec((tm, tn), lambda i, j, k: (i, j)))
        args.append(add)
    sem = ("parallel", "parallel", "arbitrary")
    if ln is not None:
        part8 = pl.BlockSpec((8, N), lambda i, j, k: (0, 0))
        in_specs += [pl.BlockSpec((tm, tn), lambda i, j, k: (i, j)), pl.BlockSpec((1, N), lambda i, j, k: (0, 0))]
        args += [ln[0], ln[1]]
        out_spec = [out_spec, pl.BlockSpec((tm, tn), lambda i, j, k: (i, j)), part8, part8]
        out_struct = [out_struct, jax.ShapeDtypeStruct((M, N), BF16), jax.ShapeDtypeStruct((8, N), F32),
                      jax.ShapeDtypeStruct((8, N), F32)]
        sem = ("arbitrary", "arbitrary", "arbitrary")
    return pl.pallas_call(
        body, name=name, grid=(M // tm, N // tn, nk),
        in_specs=in_specs, out_specs=out_spec, out_shape=out_struct, input_output_aliases=aliases,
        scratch_shapes=[pltpu.VMEM((tm, tn), F32)] if narrow else [],
        compiler_params=_cparams(sem),
    )(*args)


def _pick(n, cands):
    for c in cands:
        if n % c == 0:
            return c
    raise ValueError((n, cands))


def _mm_nn(a, b, out_dtype, name, split=False, into=None):
    M, K = _unlead(a)[2]
    N = _unlead(b)[2][1]
    return _matmul(a, b, "nn", out_dtype, _pick(M, (1024, 512, 256)), _pick(N, (1024, 512)), _pick(K, (1024, 512)), name,
                   split=split, into=into)


def _mm_nt(a, b, name, add=None, add_scale=1.0, split=False, ln=None):
    M, K = (a.shape[1], D_MODEL) if split else _unlead(a)[2]
    N = _unlead(b)[2][0]
    tms = (512, 256) if ln is not None else (1024, 512, 256)
    return _matmul(a, b, "nt", F32, _pick(M, tms), _pick(N, (1024, 512)),
                   _pick(K, (2816, 1024, 512)), name, add=add, add_scale=add_scale, split=split, ln=ln)


def _mm_tn(a, b, name, split=False, out_dtype=F32):
    K, M = _unlead(a)[2]
    N = D_MODEL if split else _unlead(b)[2][1]
    return _matmul(a, b, "tn", out_dtype, _pick(M, (1024, 1408, 512)), _pick(N, (1408, 1024, 512)),
                   _pick(K, (2048, 1024, 512, 256)), name, split=split)


def _d_kv_w(y, dqkv, name):
    S = y.shape[0]
    tk = _pick(S, (1024, 512))
    nk = S // tk
    width = N_KV_B * HEAD_DIM
    r, c = np.arange(D_MODEL)[:, None], np.arange(width)[None, :]
    fold = jnp.asarray((r // (GROUP_B * HEAD_DIM) == c // HEAD_DIM) & (r % HEAD_DIM == c % HEAD_DIM), BF16)

    def body(y_ref, dk_ref, dv_ref, f_ref, o_ref, acc_ref):
        k = pl.program_id(0)
        summed = jnp.concatenate([jnp.dot(ref[...], f_ref[...], preferred_element_type=F32).astype(BF16)
                                  for ref in (dk_ref, dv_ref)], axis=1)
        part = lax.dot_general(summed, y_ref[...], (_DIMS["tn"], ((), ())), preferred_element_type=F32)

        @pl.when(k == 0)
        def _():
            acc_ref[...] = part

        @pl.when(k > 0)
        def _():
            acc_ref[...] += part

        @pl.when(k == nk - 1)
        def _():
            o_ref[...] = acc_ref[...].T.astype(BF16)

    return pl.pallas_call(
        body, name=name, grid=(nk,),
        in_specs=[pl.BlockSpec((tk, D_MODEL), lambda k: (k, 0)),
                  pl.BlockSpec((None, tk, D_MODEL), lambda k: (1, k, 0)),
                  pl.BlockSpec((None, tk, D_MODEL), lambda k: (2, k, 0)),
                  pl.BlockSpec((D_MODEL, width), lambda k: (0, 0))],
        out_specs=pl.BlockSpec((D_MODEL, 2 * width), lambda k: (0, 0)),
        out_shape=jax.ShapeDtypeStruct((D_MODEL, 2 * width), BF16),
        scratch_shapes=[pltpu.VMEM((2 * width, D_MODEL), F32)],
        compiler_params=_cparams(("arbitrary",)),
    )(y, dqkv, dqkv, fold)


def _ffn_in(x, w, name):
    S = x.shape[0]
    tm = _pick(S, (512, 256))
    w, wl, _ = _unlead(w)

    def body(x_ref, w_ref, t_ref, h_ref):
        acc = jnp.dot(x_ref[...].astype(BF16), w_ref[...], preferred_element_type=F32)
        g = acc[:, :HALF_FF]
        up = acc[:, HALF_FF:]
        sg = jax.nn.sigmoid(g)
        silu = g * sg
        t_ref[:, :HALF_FF] = (up * (sg * (1.0 + g * (1.0 - sg)))).astype(BF16)
        t_ref[:, HALF_FF:] = silu.astype(BF16)
        h_ref[...] = (silu * up).astype(BF16)

    return pl.pallas_call(
        body, name=name, grid=(2, S // tm),
        in_specs=[pl.BlockSpec((tm, D_MODEL), lambda j, i: (i, 0)),
                  _bspec((D_MODEL, D_FF), lambda j, i: (0, j), wl)],
        out_specs=[pl.BlockSpec((tm, D_FF), lambda j, i: (i, j)),
                   pl.BlockSpec((tm, HALF_FF), lambda j, i: (i, j))],
        out_shape=[jax.ShapeDtypeStruct((S, 2 * D_FF), BF16), jax.ShapeDtypeStruct((S, D_FF), BF16)],
        compiler_params=_cparams(("parallel", "parallel")),
    )(x, w)


def _ffn_bwd_h(dzc, w_out, u, name):
    S = dzc.shape[0]
    tm = _pick(S, (512, 256))
    w_out, wl, _ = _unlead(w_out)

    def body(dz_ref, w_ref, t_ref, du_ref):
        dh = lax.dot_general(dz_ref[...], w_ref[...], (((1,), (1,)), ((), ())), preferred_element_type=F32)
        du_ref[:, :HALF_FF] = (dh * t_ref[:, :HALF_FF].astype(F32)).astype(BF16)
        du_ref[:, HALF_FF:] = (dh * t_ref[:, HALF_FF:].astype(F32)).astype(BF16)

    return pl.pallas_call(
        body, name=name, grid=(2, S // tm),
        in_specs=[pl.BlockSpec((tm, D_MODEL), lambda j, i: (i, 0)),
                  _bspec((HALF_FF, D_MODEL), lambda j, i: (j, 0), wl),
                  pl.BlockSpec((tm, D_FF), lambda j, i: (i, j))],
        out_specs=pl.BlockSpec((tm, D_FF), lambda j, i: (i, j)),
        out_shape=jax.ShapeDtypeStruct((S, 2 * D_FF), BF16),
        compiler_params=_cparams(("parallel", "parallel")),
    )(dzc, w_out, u)


def _mm_ln(a, w, resid, gain, bias, c, name):
    S, K = a.shape
    tm = _pick(S, (512, 256))
    w, wl, _ = _unlead(w)

    def body(a_ref, w_ref, r_ref, g_ref, b_ref, y_ref, yb_ref, z_ref):
        z = ALPHA * r_ref[...] + c * jnp.dot(a_ref[...], w_ref[...], preferred_element_type=F32)
        mu = jnp.mean(z, axis=-1, keepdims=True)
        zc = z - mu
        var = jnp.mean(zc * zc, axis=-1, keepdims=True)
        y = zc * lax.rsqrt(var + LN_EPS) * g_ref[...] + b_ref[...]
        z_ref[...] = z
        y_ref[...] = y
        yb_ref[...] = y.astype(BF16)

    row = pl.BlockSpec((tm, D_MODEL), lambda i: (i, 0))
    vec = pl.BlockSpec((1, D_MODEL), lambda i: (0, 0))
    return pl.pallas_call(
        body, name=name, grid=(S // tm,),
        in_specs=[pl.BlockSpec((tm, K), lambda i: (i, 0)), _bspec((K, D_MODEL), lambda i: (0, 0), wl), row, vec, vec],
        out_specs=[row, row, row],
        out_shape=[jax.ShapeDtypeStruct((S, D_MODEL), F32), jax.ShapeDtypeStruct((S, D_MODEL), BF16),
                   jax.ShapeDtypeStruct((S, D_MODEL), F32)],
        compiler_params=_cparams(("parallel",)),
    )(a, w, resid, gain, bias)


def _loss_ln_bwd(y, t, z, gain, c, name):
    S = y.shape[0]
    tm = _pick(S, (512, 256))

    def body(y_ref, t_ref, z_ref, g_ref, dz_ref, dzc_ref, gg_ref, gb_ref, sq_ref):
        i = pl.program_id(0)
        e = y_ref[...] - t_ref[...]
        dz, pg, pb = _ln_bwd_math(z_ref[...], e * (1.0 / D_MODEL), g_ref[...])
        dz_ref[...] = dz
        dzc_ref[...] = (c * dz).astype(BF16)
        ps = jnp.sum((e * e).reshape(tm // 8, 8, D_MODEL), axis=0)

        @pl.when(i == 0)
        def _():
            gg_ref[...] = pg
            gb_ref[...] = pb
            sq_ref[...] = ps

        @pl.when(i > 0)
        def _():
            gg_ref[...] += pg
            gb_ref[...] += pb
            sq_ref[...] += ps

    row = pl.BlockSpec((tm, D_MODEL), lambda i: (i, 0))
    part = pl.BlockSpec((8, D_MODEL), lambda i: (0, 0))
    part_shape = jax.ShapeDtypeStruct((8, D_MODEL), F32)
    return pl.pallas_call(
        body, name=name, grid=(S // tm,),
        in_specs=[row, row, row, pl.BlockSpec((1, D_MODEL), lambda i: (0, 0))],
        out_specs=[row, row, part, part, part],
        out_shape=[jax.ShapeDtypeStruct((S, D_MODEL), F32), jax.ShapeDtypeStruct((S, D_MODEL), BF16),
                   part_shape, part_shape, part_shape],
        compiler_params=_cparams(("arbitrary",)),
    )(y, t, z, gain)


def _rows(start, d):
    if d == 1:
        return pl.ds(pl.multiple_of(start, BLOCK), BLOCK)
    return pl.ds(start, BLOCK, stride=d)


def _ld(ref, start, d):
    return ref[_rows(start, d), :]


def _ld3(ref, lead, start, d):
    return ref[lead, _rows(start, d), :]


def _st3(ref, lead, start, d, val):
    ref[lead, _rows(start, d), :] = val


def _acc3(ref, lead, start, d, val):
    ref[lead, _rows(start, d), :] = ref[lead, _rows(start, d), :] + val


def _band_consts(slope0, slope1, maxd, scale):
    row = lax.broadcasted_iota(jnp.int32, (2 * BLOCK, 2 * BLOCK), 0)
    kj = lax.broadcasted_iota(jnp.int32, (2 * BLOCK, 2 * BLOCK), 1)
    top = row < BLOCK
    dist = BLOCK + jnp.where(top, row, row - BLOCK) - kj
    slope = jnp.where(top, slope0, slope1)
    base = jnp.where((dist >= 0) & (dist <= maxd), -(slope * (dist.astype(F32) * scale)), NEG)
    return base, kj < BLOCK


def _stack_heads(x, lo):
    return jnp.concatenate([jnp.where(lo, x, 0.0), jnp.where(lo, 0.0, x)], axis=0)


def _unstack_heads(x2, lo):
    return jnp.where(lo, x2[:BLOCK], x2[BLOCK:])


def _scores(q2, k2, base, prev_keys, first):
    s = lax.dot_general(q2, k2, (((1,), (1,)), ((), ())), preferred_element_type=F32) * (HEAD_DIM ** -0.5) + base
    return jnp.where(jnp.logical_and(prev_keys, first), NEG, s)


def _softmax_weights(ls):
    mx = ls[0]
    for l in ls[1:]:
        mx = jnp.maximum(mx, l)
    es = [jnp.exp(l - mx) for l in ls]
    tot = es[0]
    for e in es[1:]:
        tot = tot + e
    inv = 1.0 / tot
    return [e * inv for e in es]


def _attn_fwd(qkv, slopes, sinks, patterns, name):
    S = qkv.shape[1]
    npat = len(patterns)
    has_sink = sinks is not None
    if not has_sink:
        sinks = jnp.zeros((N_HEADS,), F32)
    rows_c = 256

    def body(slopes_ref, sinks_ref, x_ref, mix_ref, o_ref, lse_ref, o_scr, lse_scr):
        p = pl.program_id(0)
        lo = lax.broadcasted_iota(jnp.int32, (BLOCK, SLAB), 1) < HEAD_DIM
        top1 = lax.broadcasted_iota(jnp.int32, (2 * BLOCK, 1), 0) < BLOCK
        sk2 = jnp.where(top1, sinks_ref[2 * p], sinks_ref[2 * p + 1])
        for pi, (d, maxd, scale) in enumerate(patterns):
            nb = S // d // BLOCK
            base, prev_keys = _band_consts(slopes_ref[2 * p], slopes_ref[2 * p + 1], maxd, scale)

            def blk(t, carry, pi=pi, d=d, nb=nb, base=base, prev_keys=prev_keys):
                r = t // nb
                n = t - r * nb
                start = r + (d * BLOCK) * n
                prev = jnp.where(n > 0, start - d * BLOCK, start)
                q2 = _stack_heads(_ld3(x_ref, 0, start, d), lo).astype(BF16)
                k2 = jnp.concatenate([_ld3(x_ref, 1, prev, d), _ld3(x_ref, 1, start, d)], axis=0).astype(BF16)
                v2 = jnp.concatenate([_ld3(x_ref, 2, prev, d), _ld3(x_ref, 2, start, d)], axis=0).astype(BF16)
                s = _scores(q2, k2, base, prev_keys, n == 0)
                m = jnp.max(s, axis=-1, keepdims=True)
                if has_sink:
                    m = jnp.maximum(m, sk2)
                e = jnp.exp(s - m)
                den = jnp.sum(e, axis=-1, keepdims=True)
                if has_sink:
                    den = den + jnp.exp(sk2 - m)
                o2 = jnp.dot((e / den).astype(BF16), v2, preferred_element_type=F32)
                _st3(o_scr, pi, start, d, _unstack_heads(o2, lo))
                _st3(lse_scr, pi, start, d, _unstack_heads(m + jnp.log(den), lo))
                return carry

            lax.fori_loop(0, d * nb, blk, 0, unroll=8)

        lane_c = lax.broadcasted_iota(jnp.int32, (rows_c, SLAB), 1)

        def comb(ci, carry):
            rows = pl.ds(pl.multiple_of(ci * rows_c, rows_c), rows_c)
            ls = [lse_scr[i, rows, :] for i in range(npat)]
            packed = jnp.zeros((rows_c, SLAB), F32)
            for i in range(npat):
                o_ref[i, rows, :] = o_scr[i, rows, :].astype(BF16)
                packed = jnp.where(lane_c % HEAD_DIM == i, ls[i], packed)
            lse_ref[rows, :] = packed
            if npat == 1:
                mix_ref[rows, :] = o_scr[0, rows, :].astype(BF16)
            else:
                ws = _softmax_weights(ls)
                acc = ws[0] * o_scr[0, rows, :]
                for i in range(1, npat):
                    acc = acc + ws[i] * o_scr[i, rows, :]
                mix_ref[rows, :] = acc.astype(BF16)
            return carry

        lax.fori_loop(0, S // rows_c, comb, 0, unroll=2)

    smem = pl.BlockSpec(memory_space=pltpu.SMEM)
    return pl.pallas_call(
        body, name=name, grid=(N_SLABS,),
        in_specs=[smem, smem, pl.BlockSpec((3, S, SLAB), lambda p: (0, 0, p))],
        out_specs=[pl.BlockSpec((S, SLAB), lambda p: (0, p)), pl.BlockSpec((npat, S, SLAB), lambda p: (0, 0, p)),
                   pl.BlockSpec((None, S, SLAB), lambda p: (p, 0, 0))],
        out_shape=[jax.ShapeDtypeStruct((S, D_MODEL), BF16), jax.ShapeDtypeStruct((npat, S, D_MODEL), BF16),
                   jax.ShapeDtypeStruct((N_SLABS, S, SLAB), F32)],
        scratch_shapes=[pltpu.VMEM((npat, S, SLAB), F32), pltpu.VMEM((npat, S, SLAB), F32)],
        compiler_params=_cparams(("arbitrary",)),
    )(slopes, sinks, qkv)


def _attn_bwd(qkv, dout, o, lse, slopes, sinks, patterns, name):
    S = qkv.shape[1]
    npat = len(patterns)
    has_sink = sinks is not None
    if not has_sink:
        sinks = jnp.zeros((N_HEADS,), F32)
    rows_c = 256

    def headsum(x, lo):
        same = (lax.broadcasted_iota(jnp.int32, (SLAB, SLAB), 0) < HEAD_DIM) == (lax.broadcasted_iota(jnp.int32, (SLAB, SLAB), 1) < HEAD_DIM)
        return jnp.dot(x, same.astype(F32), precision=lax.Precision.HIGH, preferred_element_type=F32)

    def body(slopes_ref, sinks_ref, x_ref, do_ref, o_ref, lsep_ref, dxo_ref, dsink_ref, dbar_ref, sacc_ref, lse_ref, dx_ref):
        p = pl.program_id(0)
        lo = lax.broadcasted_iota(jnp.int32, (BLOCK, SLAB), 1) < HEAD_DIM
        lo_c = lax.broadcasted_iota(jnp.int32, (rows_c, SLAB), 1) < HEAD_DIM
        top1 = lax.broadcasted_iota(jnp.int32, (2 * BLOCK, 1), 0) < BLOCK
        sk2 = jnp.where(top1, sinks_ref[2 * p], sinks_ref[2 * p + 1])

        def prep(ci, carry):
            rows = pl.ds(pl.multiple_of(ci * rows_c, rows_c), rows_c)
            dov = do_ref[rows, :]
            dx_ref[:, rows, :] = jnp.zeros((3, rows_c, SLAB), F32)
            packed = lsep_ref[rows, :]
            ls = [jnp.where(lo_c, packed[:, i:i + 1], packed[:, HEAD_DIM + i:HEAD_DIM + i + 1]) for i in range(npat)]
            for i in range(npat):
                lse_ref[i, rows, :] = ls[i]
            if npat == 1:
                dbar_ref[rows, :] = headsum(dov * o_ref[0, rows, :].astype(F32), lo_c)
            else:
                ws = _softmax_weights(ls)
                acc = ws[0] * headsum(dov * o_ref[0, rows, :].astype(F32), lo_c)
                for i in range(1, npat):
                    acc = acc + ws[i] * headsum(dov * o_ref[i, rows, :].astype(F32), lo_c)
                dbar_ref[rows, :] = acc
            return carry

        lax.fori_loop(0, S // rows_c, prep, 0, unroll=2)
        sacc_ref[...] = jnp.zeros((BLOCK, SLAB), F32)

        for pi, (d, maxd, scale) in enumerate(patterns):
            nb = S // d // BLOCK
            base, prev_keys = _band_consts(slopes_ref[2 * p], slopes_ref[2 * p + 1], maxd, scale)

            def blk(t, carry, pi=pi, d=d, nb=nb, base=base, prev_keys=prev_keys):
                r = t // nb
                n = t - r * nb
                start = r + (d * BLOCK) * n
                prev = jnp.where(n > 0, start - d * BLOCK, start)
                q2 = _stack_heads(_ld3(x_ref, 0, start, d), lo).astype(BF16)
                k2 = jnp.concatenate([_ld3(x_ref, 1, prev, d), _ld3(x_ref, 1, start, d)], axis=0).astype(BF16)
                v2 = jnp.concatenate([_ld3(x_ref, 2, prev, d), _ld3(x_ref, 2, start, d)], axis=0).astype(BF16)
                ls = [_ld3(lse_ref, i, start, d) for i in range(npat)]
                w = _softmax_weights(ls)[pi] if npat > 1 else 1.0
                do2 = _stack_heads(w * _ld(do_ref, start, d), lo).astype(BF16)
                dl = w * _ld(dbar_ref, start, d)
                lse2 = jnp.concatenate([ls[pi][:, :1], ls[pi][:, HEAD_DIM:HEAD_DIM + 1]], axis=0)
                dl2 = jnp.concatenate([dl[:, :1], dl[:, HEAD_DIM:HEAD_DIM + 1]], axis=0)
                s = _scores(q2, k2, base, prev_keys, n == 0)
                pr = jnp.exp(s - lse2)
                dp = lax.dot_general(do2, v2, (((1,), (1,)), ((), ())), preferred_element_type=F32)
                ds = (pr * (dp - dl2) * (HEAD_DIM ** -0.5)).astype(BF16)
                dq2 = jnp.dot(ds, k2, preferred_element_type=F32)
                dk2 = lax.dot_general(ds, q2, (((0,), (0,)), ((), ())), preferred_element_type=F32)
                dv2 = lax.dot_general(pr.astype(BF16), do2, (((0,), (0,)), ((), ())), preferred_element_type=F32)
                _acc3(dx_ref, 0, start, d, _unstack_heads(dq2, lo))
                _acc3(dx_ref, 1, prev, d, dk2[:BLOCK])
                _acc3(dx_ref, 1, start, d, dk2[BLOCK:])
                _acc3(dx_ref, 2, prev, d, dv2[:BLOCK])
                _acc3(dx_ref, 2, start, d, dv2[BLOCK:])
                if has_sink:
                    sacc_ref[...] += _unstack_heads(-jnp.exp(sk2 - lse2) * dl2, lo)
                return carry

            lax.fori_loop(0, d * nb, blk, 0, unroll=8)

        dsink_ref[...] = jnp.broadcast_to(jnp.sum(sacc_ref[...], axis=0, keepdims=True), (8, SLAB))

        def emit(ci, carry):
            rows = pl.ds(pl.multiple_of(ci * rows_c, rows_c), rows_c)
            dxo_ref[:, rows, :] = dx_ref[:, rows, :].astype(BF16)
            return carry

        lax.fori_loop(0, S // rows_c, emit, 0, unroll=2)

    smem = pl.BlockSpec(memory_space=pltpu.SMEM)
    return pl.pallas_call(
        body, name=name, grid=(N_SLABS,),
        in_specs=[smem, smem, pl.BlockSpec((3, S, SLAB), lambda p: (0, 0, p)), pl.BlockSpec((S, SLAB), lambda p: (0, p)),
                  pl.BlockSpec((npat, S, SLAB), lambda p: (0, 0, p)), pl.BlockSpec((None, S, SLAB), lambda p: (p, 0, 0))],
        out_specs=[pl.BlockSpec((3, S, SLAB), lambda p: (0, 0, p)), pl.BlockSpec((None, 8, SLAB), lambda p: (p, 0, 0))],
        out_shape=[jax.ShapeDtypeStruct((3, S, D_MODEL), BF16), jax.ShapeDtypeStruct((N_SLABS, 8, SLAB), F32)],
        scratch_shapes=[pltpu.VMEM((S, SLAB), F32), pltpu.VMEM((BLOCK, SLAB), F32), pltpu.VMEM((npat, S, SLAB), F32),
                        pltpu.VMEM((3, S, SLAB), F32)],
        compiler_params=_cparams(("arbitrary",)),
    )(slopes, sinks, qkv, dout, o, lse)


def _place():
    x, y, c = lax.axis_index("x"), lax.axis_index("y"), lax.axis_index("c")
    return x, y, c, 2 * x + y


def _other_chips(x, y):
    return [(1 - x, y), (x, 1 - y), (1 - x, 1 - y)]


HBM_SPEC = pl.BlockSpec(memory_space=pl.ANY)


def _slot(q):
    return 2 * (q % 2) + q // 2


BIG = ("ffn1_w_in", "ffn1_w_out", "ffn2_w_in", "ffn2_w_out", "a_w_qkv", "a_w_o", "kv_w", "b_w_q", "b_w_o")
QKV_SHARD = 3 * D_MODEL // N_CHIPS
ROW_SHARD = D_MODEL // N_CHIPS


LAYER0_ITEMS = (("ffn1_w_in", 0), ("ffn1_w_out", 0), ("a_w_qkv", None), ("a_w_o", None), ("ffn2_w_in", 0),
                ("ffn2_w_out", 0), ("kv_w", None))
LAYER1_ITEMS = (("ffn1_w_in", 1), ("ffn1_w_out", 1), ("b_w_q", None), ("b_w_o", None), ("ffn2_w_in", 1),
                ("ffn2_w_out", 1))
OUT_SHARD = D_FF // N_CHIPS


def _full_shape(name):
    if name.endswith("w_in"):
        return (D_MODEL, 2 * D_FF)
    if name.endswith("w_out"):
        return (D_FF, D_MODEL)
    if name == "a_w_qkv":
        return (D_MODEL, 3 * D_MODEL)
    if name == "kv_w":
        return (N_CHIPS, 2, ROW_SHARD // 2, 2 * N_KV_B * HEAD_DIM)
    return (N_CHIPS, 2, ROW_SHARD // 2, D_MODEL)


def _gather_src(item, ref, c):
    name, layer = item
    if name.endswith("w_in"):
        return ref.at[layer, pl.ds(c * (D_MODEL // 2), D_MODEL // 2)]
    if name.endswith("w_out"):
        return ref.at[layer, pl.ds(c * (OUT_SHARD // 2), OUT_SHARD // 2)]
    if name == "a_w_qkv":
        return ref.at[0, pl.ds(c * (D_MODEL // 2), D_MODEL // 2)]
    if name == "kv_w":
        return ref.at[pl.ds(c * (ROW_SHARD // 2), ROW_SHARD // 2)]
    return ref.at[0, pl.ds(c * (ROW_SHARD // 2), ROW_SHARD // 2)]


def _gather_dst(item, ref, q, c):
    name, _ = item
    if name.endswith("w_in"):
        return ref.at[pl.ds(c * (D_MODEL // 2), D_MODEL // 2), pl.ds(_slot(q) * HALF_FF, HALF_FF)]
    if name.endswith("w_out"):
        return ref.at[pl.ds(q * OUT_SHARD + c * (OUT_SHARD // 2), OUT_SHARD // 2)]
    if name == "a_w_qkv":
        return ref.at[pl.ds(c * (D_MODEL // 2), D_MODEL // 2), pl.ds(q * QKV_SHARD, QKV_SHARD)]
    return ref.at[q, c]


def _all_gather(items, shards, small):
    n = len(items)
    r = small.shape[0]
    per = 8

    def body(*refs):
        srcs, small_ref = refs[:n], refs[n]
        dsts, s_ref = refs[n + 1:2 * n + 1], refs[2 * n + 1]
        send_sems, recv_sems = refs[2 * n + 2:]
        x, y, c, myq = _place()
        sibling = (x, y, 1 - c)
        chips = _other_chips(x, y)

        def big(t, k, src, q, h, to):
            return pltpu.make_async_remote_copy(src_ref=src, dst_ref=_gather_dst(items[t], dsts[t], q, h),
                                                send_sem=send_sems.at[per * t + k], recv_sem=recv_sems.at[per * t + k],
                                                device_id=to, device_id_type=MESH)

        def tiny(k, q, to):
            return pltpu.make_async_remote_copy(src_ref=small_ref, dst_ref=s_ref.at[q], send_sem=send_sems.at[per * n + k],
                                                recv_sem=recv_sems.at[per * n + k], device_id=to, device_id_type=MESH)

        first = []
        for j, chip in enumerate(chips):
            if j < 2:
                first += [big(t, j, _gather_src(items[t], srcs[t], c), myq, c, (*chip, c)) for t in range(n)]
            first.append(tiny(j, myq, (*chip, c)))
        own = [big(t, 6 + h, _gather_src(items[t], srcs[t], h), myq, h, sibling) for t in range(n) for h in (0, 1)]
        own.append(tiny(3, myq, sibling))
        for cp in first + own:
            cp.start()
        relay_from = ((x + 1 - c) % 2, (y + c) % 2)
        relay_to = ((x + c) % 2, (y + 1 - c) % 2, c)
        q_relay = 2 * relay_from[0] + relay_from[1]
        passed = []
        for t in range(n):
            src = _gather_src(items[t], srcs[t], c)
            for j, (cx, cy) in enumerate(chips[:2]):
                q = 2 * cx + cy
                big(t, j, src, q, c, sibling).wait_recv()
                fwd = big(t, 3 + j, _gather_dst(items[t], dsts[t], q, c), q, c, sibling)
                fwd.start()
                passed.append(fwd)
            relay = big(t, 2, _gather_dst(items[t], dsts[t], q_relay, c), q_relay, c, relay_to)
            relay.start()
            passed.append(relay)
        q_diag = 2 * chips[2][0] + chips[2][1]
        for t in range(n):
            big(t, 2, _gather_src(items[t], srcs[t], c), q_diag, c, sibling).wait_recv()
            fwd = big(t, 5, _gather_dst(items[t], dsts[t], q_diag, c), q_diag, c, sibling)
            fwd.start()
            passed.append(fwd)
        for j, (cx, cy) in enumerate(chips):
            q = 2 * cx + cy
            for t in range(n):
                big(t, 3 + j, _gather_src(items[t], srcs[t], c), q, 1 - c, sibling).wait_recv()
            tiny(j, q, sibling).wait_recv()
        for cp in own:
            cp.wait_recv()
        for cp in first + passed + own:
            cp.wait_send()

    outs = pl.pallas_call(
        body, name="all_gather_layer0",
        in_specs=[HBM_SPEC] * (n + 1), out_specs=[HBM_SPEC] * (n + 1),
        out_shape=[jax.ShapeDtypeStruct(_full_shape(name), BF16) for name, _ in items]
        + [jax.ShapeDtypeStruct((N_CHIPS, r, 128), F32)],
        scratch_shapes=[pltpu.SemaphoreType.DMA((per * n + 4,)), pltpu.SemaphoreType.DMA((per * n + 4,))],
    )(*[shards[item] for item in items], small)
    return list(outs[:n]), outs[n]


SEM_SPEC = pl.BlockSpec(memory_space=pltpu.SEMAPHORE)
DATAFLOW = pltpu.SideEffectType.DATAFLOW_SIDE_EFFECTING
PER_ITEM = 8


def _split_start(name, copies, n_sems, sources, land_shapes, after):
    n, m = len(sources), len(land_shapes)

    def body(*refs):
        srcs, lands = refs[:n], refs[n:n + m]
        send_sems, recv_sems = refs[n + m + 1], refs[n + m + 2]
        token = refs[-1]
        for src, dst_there, _, s, peer in copies(srcs, lands):
            pltpu.make_async_remote_copy(src_ref=src, dst_ref=dst_there, send_sem=send_sems.at[s], recv_sem=recv_sems.at[s],
                                         device_id=peer, device_id_type=MESH).start()
        token[...] = jnp.zeros_like(token)

    src_arrays = [pltpu.with_memory_space_constraint(a, pltpu.HBM) for a in sources]
    land_arrays = [pltpu.with_memory_space_constraint(lax.empty(s.shape, s.dtype), pltpu.HBM) for s in land_shapes]
    hbm = pl.BlockSpec(memory_space=pltpu.HBM)
    outs = pl.pallas_call(
        body, name=name,
        in_specs=[hbm] * (n + m) + [HBM_SPEC],
        out_specs=[SEM_SPEC, SEM_SPEC] + [hbm] * (n + m) + [pl.BlockSpec(memory_space=pltpu.VMEM)],
        out_shape=[pltpu.SemaphoreType.DMA((n_sems,)), pltpu.SemaphoreType.DMA((n_sems,))]
        + [pltpu.HBM(a.shape, a.dtype) for a in src_arrays + land_arrays] + [jax.ShapeDtypeStruct((8, 128), F32)],
        input_output_aliases={i: 2 + i for i in range(n + m)},
        compiler_params=pltpu.CompilerParams(has_side_effects=DATAFLOW),
    )(*src_arrays, *land_arrays, after)
    return (outs[0], outs[1], list(outs[2:2 + n]), list(outs[2 + n:2 + n + m])), outs[-1]


def _split_wait(name, copies, state, after):
    send_sems, recv_sems, srcs_thru, lands_thru = state
    n, m = len(srcs_thru), len(lands_thru)
    after = list(after) if isinstance(after, (list, tuple)) else [after]

    def body(*refs):
        srcs, lands = refs[:n], refs[n:n + m]
        send_sems, recv_sems = refs[n + m], refs[n + m + 1]
        for src, _, dst_here, s, peer in copies(srcs, lands):
            cp = pltpu.make_async_remote_copy(src_ref=src, dst_ref=dst_here, send_sem=send_sems.at[s], recv_sem=recv_sems.at[s],
                                              device_id=peer, device_id_type=MESH)
            cp.wait_send()
            cp.wait_recv()

    hbm = pl.BlockSpec(memory_space=pltpu.HBM)
    outs = pl.pallas_call(
        body, name=name,
        in_specs=[hbm] * (n + m) + [SEM_SPEC, SEM_SPEC] + [HBM_SPEC] * len(after),
        out_specs=[hbm] * (n + m),
        out_shape=[pltpu.HBM(a.shape, a.dtype) for a in srcs_thru + lands_thru],
        input_output_aliases={i: i for i in range(n + m)},
        compiler_params=pltpu.CompilerParams(has_side_effects=DATAFLOW),
    )(*srcs_thru, *lands_thru, send_sems, recv_sems, *after)
    return list(outs[:n]), list(outs[n:])


def _gather_copies(items):
    def copies(srcs, lands):
        x, y, c, myq = _place()
        out = []
        for t, item in enumerate(items):
            for h in (0, 1):
                src = _gather_src(item, srcs[t], h)
                for j, (cx, cy) in enumerate(_other_chips(x, y)):
                    out.append((src, _gather_dst(item, lands[t], myq, h), _gather_dst(item, lands[t], 2 * cx + cy, h),
                                PER_ITEM * t + 2 * j + h, (cx, cy, c)))
                out.append((src, _gather_dst(item, lands[t], myq, h), _gather_dst(item, lands[t], myq, h),
                            PER_ITEM * t + 6 + h, (x, y, 1 - c)))
        return out
    return copies


def _gather_start(items, shards, after):
    lands = [jax.ShapeDtypeStruct(_full_shape(name), BF16) for name, _ in items]
    return _split_start("gather_layer1_start", _gather_copies(items), PER_ITEM * len(items),
                        [shards[item] for item in items], lands, after)


def _gather_wait(items, state, after):
    return _split_wait("gather_layer1_wait", _gather_copies(items), state, after)[1]


def _small_all_reduce(v):
    r = v.shape[0]

    def body(v_ref, o_ref, buf_ref, send_sems, recv_sems):
        x, y, c, _ = _place()
        me = 4 * x + 2 * y + c
        buf_ref[me] = v_ref[...]
        copies = []
        for k in range(1, 8):
            fx, fy, fc = (k >> 2) & 1, (k >> 1) & 1, k & 1
            to = (x ^ fx, y ^ fy, c ^ fc)
            cp = pltpu.make_async_remote_copy(src_ref=v_ref, dst_ref=buf_ref.at[me], send_sem=send_sems.at[k - 1],
                                              recv_sem=recv_sems.at[k - 1], device_id=to, device_id_type=MESH)
            cp.start()
            copies.append(cp)
        for k in range(1, 8):
            fx, fy, fc = (k >> 2) & 1, (k >> 1) & 1, k & 1
            src_dev = 4 * (x ^ fx) + 2 * (y ^ fy) + (c ^ fc)
            pltpu.make_async_remote_copy(src_ref=v_ref, dst_ref=buf_ref.at[src_dev], send_sem=send_sems.at[k - 1],
                                         recv_sem=recv_sems.at[k - 1], device_id=(x, y, c), device_id_type=MESH).wait_recv()
        for cp in copies:
            cp.wait_send()
        tot = buf_ref[0]
        for i in range(1, 8):
            tot = tot + buf_ref[i]
        o_ref[...] = tot

    vm = pl.BlockSpec(memory_space=pltpu.VMEM)
    return pl.pallas_call(
        body, name="small_all_reduce", in_specs=[vm], out_specs=vm,
        out_shape=jax.ShapeDtypeStruct((r, 128), F32),
        scratch_shapes=[pltpu.VMEM((8, r, 128), F32), pltpu.SemaphoreType.DMA((7,)), pltpu.SemaphoreType.DMA((7,))],
    )(v)


def _grad_view(kind, g):
    if kind == "col":
        return g.reshape(2, g.shape[0] // 2, g.shape[1])
    return g.reshape(N_CHIPS, 2, g.shape[0] // (2 * N_CHIPS), g.shape[1])


def _half_of(kind, ref, h):
    return ref.at[h] if kind == "col" else ref.at[:, h]


def _half_shape(kind, view_shape):
    return view_shape[1:] if kind == "col" else (view_shape[0],) + view_shape[2:]


def _piece_of(kind, width, colblock, ref, q):
    if kind == "col":
        return ref.at[:, pl.ds(colblock(q) * width, width)]
    return ref.at[q]


def _piece_shape(kind, width, half_shape):
    return (half_shape[0], width) if kind == "col" else half_shape[1:]


def _pair_exchange(views, kinds, name):
    n = len(views)

    def body(*refs):
        ins, outs = refs[:n], refs[n:2 * n]
        send_sems, recv_sems = refs[2 * n:]
        x, y, c, _ = _place()
        cps = []
        for t in range(n):
            cp = pltpu.make_async_remote_copy(src_ref=_half_of(kinds[t], ins[t], 1 - c), dst_ref=outs[t],
                                              send_sem=send_sems.at[t], recv_sem=recv_sems.at[t],
                                              device_id=(x, y, 1 - c), device_id_type=MESH)
            cp.start()
            cps.append(cp)
        for cp in cps:
            cp.wait()

    return pl.pallas_call(
        body, name=name, in_specs=[HBM_SPEC] * n, out_specs=[HBM_SPEC] * n,
        out_shape=[jax.ShapeDtypeStruct(_half_shape(k, v.shape), v.dtype) for k, v in zip(kinds, views)],
        scratch_shapes=[pltpu.SemaphoreType.DMA((n,)), pltpu.SemaphoreType.DMA((n,))],
    )(*views)


def _pair_sum(kind, view, recv, c, name):
    hs = recv.shape
    N = hs[-1]
    rows = hs[-2]
    tr = _pick(rows, (512, 352, 128))
    tn = _pick(N, (1408, 1024, 512))

    def body(c_ref, p_ref, r_ref, s_ref):
        s_ref[...] = (p_ref[...] + r_ref[...]).astype(BF16)

    if kind == "col":
        grid = (rows // tr, N // tn)
        mine = pl.BlockSpec((None, tr, tn), lambda i, j, c_ref: (c_ref[0], i, j))
        blk = pl.BlockSpec((tr, tn), lambda i, j, c_ref: (i, j))
        sem = ("parallel", "parallel")
    else:
        grid = (N_CHIPS, rows // tr, N // tn)
        mine = pl.BlockSpec((None, None, tr, tn), lambda q, i, j, c_ref: (q, c_ref[0], i, j))
        blk = pl.BlockSpec((None, tr, tn), lambda q, i, j, c_ref: (q, i, j))
        sem = ("parallel", "parallel", "parallel")
    return pl.pallas_call(
        body, name=name,
        grid_spec=pltpu.PrefetchScalarGridSpec(num_scalar_prefetch=1, grid=grid, in_specs=[mine, blk], out_specs=blk),
        out_shape=jax.ShapeDtypeStruct(hs, BF16),
        compiler_params=_cparams(sem),
    )(c.reshape(1).astype(jnp.int32), view, recv)


def _chip_copies(kinds, widths, colblocks):
    def copies(srcs, lands):
        x, y, c, _ = _place()
        out = []
        for j, (cx, cy) in enumerate(_other_chips(x, y)):
            for t in range(len(kinds)):
                out.append((_piece_of(kinds[t], widths[t], colblocks[t], srcs[t], 2 * cx + cy), lands[t].at[j],
                            lands[t].at[j], 3 * t + j, (cx, cy, c)))
        return out
    return copies


def _chip_land_shapes(sums, kinds, widths):
    return [jax.ShapeDtypeStruct((3,) + _piece_shape(k, w, s.shape), BF16) for k, w, s in zip(kinds, widths, sums)]


def _chip_exchange(sums, kinds, widths, colblocks, name):
    n = len(sums)
    copies = _chip_copies(kinds, widths, colblocks)

    def body(*refs):
        send_sems, recv_sems = refs[2 * n:]
        cps = [pltpu.make_async_remote_copy(src_ref=src, dst_ref=dst, send_sem=send_sems.at[s], recv_sem=recv_sems.at[s],
                                            device_id=peer, device_id_type=MESH)
               for src, dst, _, s, peer in copies(refs[:n], refs[n:2 * n])]
        for cp in cps:
            cp.start()
        for cp in cps:
            cp.wait()

    return pl.pallas_call(
        body, name=name, in_specs=[HBM_SPEC] * n, out_specs=[HBM_SPEC] * n,
        out_shape=_chip_land_shapes(sums, kinds, widths),
        scratch_shapes=[pltpu.SemaphoreType.DMA((3 * n,)), pltpu.SemaphoreType.DMA((3 * n,))],
    )(*sums)


N_DIRECT = 7


def _direct_piece(kind, width, colblock, view_ref, q, h):
    if kind == "col":
        return view_ref.at[h, :, pl.ds(colblock(q) * width, width)]
    return view_ref.at[q, h]


def _direct_copies(kinds, widths, colblocks):
    def copies(srcs, lands):
        x, y, c, myq = _place()
        out = []
        for t in range(len(kinds)):
            def piece(q, h, t=t):
                return _direct_piece(kinds[t], widths[t], colblocks[t], srcs[t], q, h)
            for j, (cx, cy) in enumerate(_other_chips(x, y)):
                for h in (0, 1):
                    out.append((piece(2 * cx + cy, h), lands[t].at[2 * j + c], lands[t].at[2 * j + h],
                                10 * t + 3 * j + c + h, (cx, cy, h)))
            out.append((piece(myq, 1 - c), lands[t].at[6], lands[t].at[6], 10 * t + 9, (x, y, 1 - c)))
        return out
    return copies


def _chip_sum(kind, own_src, recv, block_idx, c, shard_shape, layer, into, name, direct=False):
    n_recv, rows, N = recv.shape
    tr = _pick(rows, (512, 352, 128))
    tn = _pick(N, (1408, 1024, 768, 512))
    ni, nj = rows // tr, N // tn

    def body(q_ref, s_ref, r_ref, *rest):
        o_ref = rest[-1]
        tot = s_ref[...].astype(F32)
        for k in range(n_recv):
            tot = tot + r_ref[k].astype(F32)
        o_ref[...] = tot

    if direct and kind == "col":
        own = pl.BlockSpec((None, tr, tn), lambda i, j, q_ref: (q_ref[1], i, q_ref[0] * nj + j))
    elif direct:
        own = pl.BlockSpec((None, None, tr, tn), lambda i, j, q_ref: (q_ref[0], q_ref[1], i, j))
    elif kind == "col":
        own = pl.BlockSpec((tr, tn), lambda i, j, q_ref: (i, q_ref[0] * nj + j))
    else:
        own = pl.BlockSpec((None, tr, tn), lambda i, j, q_ref: (q_ref[0], i, j))
    if len(shard_shape) == 3:
        lead = 0 if layer is None else layer
        out_spec = pl.BlockSpec((None, tr, tn), lambda i, j, q_ref: (lead, q_ref[1] * ni + i, j))
    else:
        out_spec = pl.BlockSpec((tr, tn), lambda i, j, q_ref: (q_ref[1] * ni + i, j))
    in_specs = [own, pl.BlockSpec((n_recv, tr, tn), lambda i, j, q_ref: (0, i, j))]
    s = own_src
    args = [jnp.stack([block_idx, c]).astype(jnp.int32), s, recv]
    aliases = {}
    if into is not None:
        in_specs.append(HBM_SPEC)
        args.append(into)
        aliases = {3: 0}
    return pl.pallas_call(
        body, name=name,
        grid_spec=pltpu.PrefetchScalarGridSpec(num_scalar_prefetch=1, grid=(ni, nj), in_specs=in_specs, out_specs=out_spec),
        out_shape=jax.ShapeDtypeStruct(shard_shape, F32), input_output_aliases=aliases,
        compiler_params=_cparams(("parallel", "parallel")),
    )(*args)


def _half_window(ref, h):
    rows = ref.shape[-2] // 2
    if ref.ndim == 3:
        return ref.at[:, pl.ds(h * rows, rows)]
    return ref.at[pl.ds(h * rows, rows)]


def _share_halves(grads, name):
    n = len(grads)

    def body(*refs):
        outs = refs[n:2 * n]
        send_sems, recv_sems = refs[2 * n:]
        x, y, c, _ = _place()
        cps = []
        for t in range(n):
            cp = pltpu.make_async_remote_copy(src_ref=_half_window(outs[t], c), dst_ref=_half_window(outs[t], c),
                                              send_sem=send_sems.at[t], recv_sem=recv_sems.at[t],
                                              device_id=(x, y, 1 - c), device_id_type=MESH)
            cp.start()
            cps.append(cp)
        for t in range(n):
            cps[t].wait_send()
            pltpu.make_async_remote_copy(src_ref=_half_window(outs[t], c), dst_ref=_half_window(outs[t], 1 - c),
                                         send_sem=send_sems.at[t], recv_sem=recv_sems.at[t],
                                         device_id=(x, y, 1 - c), device_id_type=MESH).wait_recv()

    return pl.pallas_call(
        body, name=name, in_specs=[HBM_SPEC] * n, out_specs=[HBM_SPEC] * n,
        out_shape=[jax.ShapeDtypeStruct(g.shape, F32) for g in grads],
        input_output_aliases={t: t for t in range(n)},
        scratch_shapes=[pltpu.SemaphoreType.DMA((n,)), pltpu.SemaphoreType.DMA((n,))],
    )(*grads)


def _adamw(w, g, m, v, name):
    R, W = w.shape
    tr = _pick(R, (512, 352, 256, 32))

    def body(w_ref, g_ref, m_ref, v_ref, d_ref, nm_ref, nv_ref, go_ref):
        gv = g_ref[...]
        go_ref[...] = gv
        nm = ADAM_B1 * m_ref[...] + (1.0 - ADAM_B1) * gv
        nv = ADAM_B2 * v_ref[...] + (1.0 - ADAM_B2) * (gv * gv)
        m_hat = nm / (1.0 - ADAM_B1 ** ADAM_STEP)
        v_hat = nv / (1.0 - ADAM_B2 ** ADAM_STEP)
        d_ref[...] = -ADAM_LR * (m_hat / (jnp.sqrt(v_hat) + ADAM_EPS) + ADAM_WD * w_ref[...])
        nm_ref[...] = nm
        nv_ref[...] = nv

    blk = pl.BlockSpec((tr, W), lambda i: (i, 0))
    shp = jax.ShapeDtypeStruct((R, W), F32)
    return pl.pallas_call(
        body, name=name, grid=(R // tr,), in_specs=[blk] * 4, out_specs=[blk] * 4, out_shape=[shp] * 4,
        compiler_params=_cparams(("parallel",)),
    )(w, g, m, v)


SMALL_ROWS = 32


def _pack_small(ln_g, ln_b, sinks):
    rows = jnp.concatenate([ln_g.reshape(-1, 128), ln_b.reshape(-1, 128),
                            jnp.pad(sinks.reshape(1, -1), ((0, 0), (0, 128 - sinks.size)))], axis=0)
    return jnp.pad(rows, ((0, SMALL_ROWS - rows.shape[0]), (0, 0)))


def _unpack_small(s, ln_shape, sink_shape):
    n = ln_shape[0] * ln_shape[1] * ln_shape[2] // 128
    return s[:n].reshape(ln_shape), s[n:2 * n].reshape(ln_shape), s[2 * n, :sink_shape[1]].reshape(sink_shape)


def _ffn_fwd(xin, w_in, w_out, gain, bias, tag):
    u, h = _ffn_in(xin, w_in, "ffn_in_" + tag)
    y, yb, z = _mm_ln(h, w_out, xin, gain, bias, 0.5, "ffn_out_ln_" + tag)
    return y, yb, dict(u=u, h=h, z=z, xin=xin)


def _ffn_bwd(dz, dzc, saved, w_in, w_out, xin_b, tag, dw_dtype=F32, ln=None):
    du = _ffn_bwd_h(dzc, w_out, saved["u"], "ffn_bwd_h_" + tag)
    d_w_out = _mm_tn(saved["h"], dzc, "ffn_dwout_" + tag, out_dtype=dw_dtype)
    d_w_in = _mm_tn(xin_b, du, "ffn_dwin_" + tag, out_dtype=dw_dtype)
    dx = _mm_nt(du, w_in, "ffn_dx_" + tag, add=dz, add_scale=ALPHA, ln=ln)
    return dx, d_w_in, d_w_out


def kernel(x, ffn1_w_in, ffn1_w_out, ffn2_w_in, ffn2_w_out, ln_g, ln_b, a_w_qkv, a_w_o, kv_w, b_w_q, b_sinks, b_w_o, loss_target, m_ffn1_w_in, m_ffn1_w_out, m_ffn2_w_in, m_ffn2_w_out, m_ln_g, m_ln_b, m_a_w_qkv, m_a_w_o, m_kv_w, m_b_w_q, m_b_sinks, m_b_w_o, v_ffn1_w_in, v_ffn1_w_out, v_ffn2_w_in, v_ffn2_w_out, v_ln_g, v_ln_b, v_a_w_qkv, v_a_w_o, v_kv_w, v_b_w_q, v_b_sinks, v_b_w_o):
    ws = dict(ffn1_w_in=ffn1_w_in, ffn1_w_out=ffn1_w_out, ffn2_w_in=ffn2_w_in, ffn2_w_out=ffn2_w_out, a_w_qkv=a_w_qkv,
              a_w_o=a_w_o, kv_w=kv_w, b_w_q=b_w_q, b_w_o=b_w_o)
    ms = dict(ffn1_w_in=m_ffn1_w_in, ffn1_w_out=m_ffn1_w_out, ffn2_w_in=m_ffn2_w_in, ffn2_w_out=m_ffn2_w_out,
              a_w_qkv=m_a_w_qkv, a_w_o=m_a_w_o, kv_w=m_kv_w, b_w_q=m_b_w_q, b_w_o=m_b_w_o)
    vs = dict(ffn1_w_in=v_ffn1_w_in, ffn1_w_out=v_ffn1_w_out, ffn2_w_in=v_ffn2_w_in, ffn2_w_out=v_ffn2_w_out,
              a_w_qkv=v_a_w_qkv, a_w_o=v_a_w_o, kv_w=v_kv_w, b_w_q=v_b_w_q, b_w_o=v_b_w_o)
    _, _, c_idx, myq = _place()
    xs = x[0]
    target = loss_target[0]

    shards = {(n, l): ws[n].astype(BF16) for n, l in LAYER0_ITEMS + LAYER1_ITEMS}

    def as_weights(items, arrays):
        return {n: (a.reshape(D_MODEL, a.shape[-1]) if a.ndim == 4 else a) for (n, _), a in zip(items, arrays)}

    full0, small = _all_gather(LAYER0_ITEMS, shards, _pack_small(ln_g, ln_b, b_sinks))
    gather_state, token = _gather_start(LAYER1_ITEMS, shards, small)

    def layer1_weights(after):
        return as_weights(LAYER1_ITEMS, _gather_wait(LAYER1_ITEMS, gather_state, after))

    n_ln = ln_g.size // 128
    lg = jnp.concatenate([small[q, :n_ln].reshape(DEPTH, 3, 1, -1) for q in range(N_CHIPS)], axis=-1)
    lb = jnp.concatenate([small[q, n_ln:2 * n_ln].reshape(DEPTH, 3, 1, -1) for q in range(N_CHIPS)], axis=-1)
    lg = lg + token[0, 0]
    reducer = _GradReducer(c_idx, myq, {n: ws[n].shape for n in BIG})
    sq, grad_x, _, gg, gb, dsink_part = _local_step(xs, target, as_weights(LAYER0_ITEMS, full0), layer1_weights,
                                                    lg, lb, b_sinks.reshape(N_HEADS), reducer.begin)

    loss_row = jnp.pad(jnp.sum(sq).reshape(1, 1), ((0, 0), (0, 127)))
    dsinks = jnp.pad(dsink_part[:, 0, :].reshape(N_SLABS, 2, HEAD_DIM)[:, :, 0].reshape(1, N_HEADS), ((0, 0), (0, 128 - N_HEADS)))
    gg_full = jnp.stack([jnp.stack([jnp.sum(gg[i][j], axis=0) for j in range(3)]) for i in range(DEPTH)])
    gb_full = jnp.stack([jnp.stack([jnp.sum(gb[i][j], axis=0) for j in range(3)]) for i in range(DEPTH)])
    small_in = jnp.concatenate([loss_row, dsinks, gg_full.reshape(-1, 128), gb_full.reshape(-1, 128)], axis=0)
    small_in = jnp.pad(small_in, ((0, (-small_in.shape[0]) % 8), (0, 0)))
    small_sum = _small_all_reduce(small_in)
    loss = small_sum[0, 0] * (0.5 / D_MODEL)
    grad_sinks = small_sum[1, :N_HEADS].reshape(b_sinks.shape)
    n_full = DEPTH * 3 * D_MODEL // 128
    cols = D_MODEL // N_CHIPS
    grad_ln_g = lax.dynamic_slice_in_dim(small_sum[2:2 + n_full].reshape(DEPTH, 3, D_MODEL), myq * cols, cols, axis=2)
    grad_ln_b = lax.dynamic_slice_in_dim(small_sum[2 + n_full:2 + 2 * n_full].reshape(DEPTH, 3, D_MODEL), myq * cols, cols, axis=2)
    return _update(reducer, grad_x, loss, grad_ln_g, grad_ln_b, grad_sinks, ws, ms, vs,
                   (ln_g, ln_b, b_sinks), (m_ln_g, m_ln_b, m_b_sinks), (v_ln_g, v_ln_b, v_b_sinks))


def _local_step(xs, target, W, layer1_weights, lg, lb, sinks, grads_ready=None):
    if grads_ready is None:
        grads_ready = lambda tag, grads, overlap: 0.0
    S = xs.shape[0]
    slopes = jnp.asarray(_alibi_slopes(N_HEADS))
    in1, out1, in2, out2 = [W["ffn1_w_in"]], [W["ffn1_w_out"]], [W["ffn2_w_in"]], [W["ffn2_w_out"]]

    y1, y1b, s1 = _ffn_fwd(xs, in1[0], out1[0], lg[0, 0], lb[0, 0], "a1")
    qkv_a = _mm_nn(y1b, W["a_w_qkv"], F32, "qkv_a", split=True)
    mix_a, o_a, lse_a = _attn_fwd(qkv_a, slopes, None, PATTERNS_A, "attn_a_fwd")
    y2, y2b, z2 = _mm_ln(mix_a, W["a_w_o"], y1, lg[0, 1], lb[0, 1], 1.0, "attn_a_out_ln")
    y3, y3b, s3 = _ffn_fwd(y2, in2[0], out2[0], lg[0, 2], lb[0, 2], "a2")
    kv_w_rep = jnp.broadcast_to(W["kv_w"].reshape(D_MODEL, 2, N_KV_B, 1, HEAD_DIM),
                                (D_MODEL, 2, N_KV_B, GROUP_B, HEAD_DIM)).reshape(D_MODEL, 2 * D_MODEL)
    kv_rep = _mm_nn(y3b, kv_w_rep, F32, "kv_proj", split=(1, 2))
    W = dict(W, **layer1_weights(kv_rep))
    in1, out1, in2, out2 = (in1 + [W["ffn1_w_in"]], out1 + [W["ffn1_w_out"]], in2 + [W["ffn2_w_in"]],
                            out2 + [W["ffn2_w_out"]])
    y4, y4b, s4 = _ffn_fwd(y3, in1[1], out1[1], lg[1, 0], lb[1, 0], "b1")
    qkv_b = _mm_nn(y4b, W["b_w_q"], F32, "q_b", split=(0, 1), into=kv_rep)
    mix_b, o_b, lse_b = _attn_fwd(qkv_b, slopes, sinks, PATTERNS_B, "attn_b_fwd")
    y5, y5b, z5 = _mm_ln(mix_b, W["b_w_o"], y4, lg[1, 1], lb[1, 1], 1.0, "attn_b_out_ln")
    y6, _, s6 = _ffn_fwd(y5, in2[1], out2[1], lg[1, 2], lb[1, 2], "b2")

    gr = {n: None for n in BIG}
    gg = [[None] * 3 for _ in range(DEPTH)]
    gb = [[None] * 3 for _ in range(DEPTH)]
    dz6, dz6c, gg[1][2], gb[1][2], sq = _loss_ln_bwd(y6, target, s6["z"], lg[1, 2], 0.5, "loss_ln_bwd")

    (dz5, dz5b, gg[1][1], gb[1][1]), d_in2_b, d_out2_b = _ffn_bwd(dz6, dz6c, s6, in2[1], out2[1], y5b, "b2", BF16,
                                                                  ln=(z5, lg[1, 1], 1.0))
    gr["b_w_o"] = _mm_tn(mix_b, dz5b, "d_b_w_o", out_dtype=BF16)
    dmix_b = _mm_nt(dz5b, W["b_w_o"], "d_mix_b")
    dqkv_b, dsink_part = _attn_bwd(qkv_b, dmix_b, o_b, lse_b, slopes, sinks, PATTERNS_B, "attn_b_bwd")
    dq_b = (dqkv_b, 0)
    gr["b_w_q"] = _mm_tn(y4b, dq_b, "d_b_w_q", out_dtype=BF16)
    dz4, dz4c, gg[1][0], gb[1][0] = _mm_nt(dq_b, W["b_w_q"], "d_y4", add=dz5, add_scale=ALPHA, ln=(s4["z"], lg[1, 0], 0.5))
    dy3, d_in1_b, d_out1_b = _ffn_bwd(dz4, dz4c, s4, in1[1], out1[1], y3b, "b1", BF16)
    gr["kv_w"] = _d_kv_w(y3b, dqkv_b, "d_kv_w")
    tok = grads_ready("l1", {("ffn2_w_in", 1): d_in2_b, ("ffn2_w_out", 1): d_out2_b, ("b_w_o", None): gr["b_w_o"],
                             ("b_w_q", None): gr["b_w_q"], ("ffn1_w_in", 1): d_in1_b, ("ffn1_w_out", 1): d_out1_b,
                             ("kv_w", None): gr["kv_w"]}, True)
    lg0 = lg[0] + tok
    dz3, dz3c, gg[0][2], gb[0][2] = _mm_nt(dqkv_b, kv_w_rep, "d_y3_kv", add=dy3, add_scale=1.0, split=(1, 2),
                                           ln=(s3["z"], lg0[2], 0.5))

    (dz2, dz2b, gg[0][1], gb[0][1]), d_in2_a, d_out2_a = _ffn_bwd(dz3, dz3c, s3, in2[0], out2[0], y2b, "a2", BF16,
                                                                  ln=(z2, lg0[1], 1.0))
    tok = grads_ready("a2", {("ffn2_w_in", 0): d_in2_a, ("ffn2_w_out", 0): d_out2_a}, True)
    lg0 = lg0 + tok
    gr["a_w_o"] = _mm_tn(mix_a, dz2b, "d_a_w_o", out_dtype=BF16)
    dmix_a = _mm_nt(dz2b, W["a_w_o"], "d_mix_a")
    dqkv_a, _ = _attn_bwd(qkv_a, dmix_a, o_a, lse_a, slopes, None, PATTERNS_A, "attn_a_bwd")
    gr["a_w_qkv"] = _mm_tn(y1b, dqkv_a, "d_a_w_qkv", split=True, out_dtype=BF16)
    tok = grads_ready("mix", {("a_w_o", None): gr["a_w_o"], ("a_w_qkv", None): gr["a_w_qkv"]}, True)
    lg0 = lg0 + tok
    dz1, dz1c, gg[0][0], gb[0][0] = _mm_nt(dqkv_a, W["a_w_qkv"], "d_y1", add=dz2, add_scale=ALPHA, split=True,
                                           ln=(s1["z"], lg0[0], 0.5))
    grad_x, d_in1_a, d_out1_a = _ffn_bwd(dz1, dz1c, s1, in1[0], out1[0], xs, "a1", BF16)
    grads_ready("a1", {("ffn1_w_in", 0): d_in1_a, ("ffn1_w_out", 0): d_out1_a}, True)
    gr["ffn1_w_in"] = [d_in1_a, d_in1_b]
    gr["ffn1_w_out"] = [d_out1_a, d_out1_b]
    gr["ffn2_w_in"] = [d_in2_a, d_in2_b]
    gr["ffn2_w_out"] = [d_out2_a, d_out2_b]
    return sq, grad_x, gr, gg, gb, dsink_part


def _grad_item(name, layer, g):
    if name.endswith("w_in"):
        return (g, "col", HALF_FF, _slot, name, layer)
    if name.endswith("w_out"):
        return (g, "row", D_MODEL, None, name, layer)
    if name == "a_w_qkv":
        return (g, "col", QKV_SHARD, lambda q: q, name, None)
    return (g, "row", g.shape[1], None, name, None)


class _GradReducer:
    def __init__(self, c_idx, myq, shard_shapes):
        self.c_idx, self.myq, self.shard_shapes = c_idx, myq, shard_shapes
        self.groups = []

    def begin(self, tag, grads, overlap):
        items = [_grad_item(n, l, g) for (n, l), g in grads.items()]
        kinds, widths, colblocks = [it[1] for it in items], [it[2] for it in items], [it[3] for it in items]
        views = [_grad_view(k, it[0]) for k, it in zip(kinds, items)]
        if overlap:
            lands = [jax.ShapeDtypeStruct((N_DIRECT,) + _piece_shape(k, w, _half_shape(k, v.shape)), BF16)
                     for k, w, v in zip(kinds, widths, views)]
            state, token = _split_start("grad_direct_start_" + tag, _direct_copies(kinds, widths, colblocks), 10 * len(items),
                                        views, lands, jnp.zeros((8, 128), F32))
            self.groups.append((tag, items, None, state, token))
            return token[0, 0]
        from_sibling = _pair_exchange(views, kinds, "grad_pair_exchange_" + tag)
        sums = [_pair_sum(k, v, r, self.c_idx, "pair_sum_%s_%d" % (tag, t))
                for t, (k, v, r) in enumerate(zip(kinds, views, from_sibling))]
        self.groups.append((tag, items, sums, None, None))
        return 0.0

    def _sum_group(self, tag, items, sums, received, direct):
        for t, (it, s, r) in enumerate(zip(items, sums, received)):
            _, k, _, cb, name, layer = it
            own = cb(self.myq) if k == "col" else self.myq
            self.half_done[name] = _chip_sum(k, s, r, own, self.c_idx, self.shard_shapes[name], layer,
                                             self.half_done.get(name), "chip_sum_%s_%d" % (tag, t), direct=direct)

    def finish_first(self, after):
        self.half_done, self.late, early = {}, [], []
        started = [after]
        for g, (tag, items, sums, state, token) in enumerate(self.groups):
            kinds, widths, colblocks = [it[1] for it in items], [it[2] for it in items], [it[3] for it in items]
            if state is None:
                copies = _chip_copies(kinds, widths, colblocks)
                state, token = _split_start("grad_chip_start_" + tag, copies, 3 * len(items), sums,
                                            _chip_land_shapes(sums, kinds, widths), sums[-1])
                self.late.append((tag, items, copies, state, False))
                started.append(token)
            elif g == len(self.groups) - 1:
                self.late.append((tag, items, _direct_copies(kinds, widths, colblocks), state, True))
                started.append(token)
            else:
                early.append((tag, items, _direct_copies(kinds, widths, colblocks), state))
        for tag, items, copies, state in early:
            views, received = _split_wait("grad_direct_wait_" + tag, copies, state, started)
            self._sum_group(tag, items, views, received, True)
        late_names = {it[4] for _, items, _, _, _ in self.late for it in items}
        names = [n for n in BIG if n not in late_names]
        return dict(zip(names, _share_halves([self.half_done[n] for n in names], "grad_share_halves_first")))

    def finish_rest(self, after):
        names = []
        for tag, items, copies, state, direct in self.late:
            sums, received = _split_wait("grad_late_wait_" + tag, copies, state, after)
            self._sum_group(tag, items, sums, received, direct)
            names += [it[4] for it in items if it[4] not in names]
        return dict(zip(names, _share_halves([self.half_done[n] for n in names], "grad_share_halves_rest")))


def _update(reducer, grad_x, loss, grad_ln_g, grad_ln_b, grad_sinks, ws, ms, vs, small_w, small_m, small_v):
    ln_g, ln_b, b_sinks = small_w
    m_ln_g, m_ln_b, m_b_sinks = small_m
    v_ln_g, v_ln_b, v_b_sinks = small_v

    grads, deltas, new_m, new_v = {}, {}, {}, {}

    def update(some):
        done = []
        for name in some:
            shp = ws[name].shape
            flat = lambda a: a.reshape(-1, shp[-1])
            d, nm, nv, g = _adamw(flat(ws[name]), flat(some[name]), flat(ms[name]), flat(vs[name]), "adamw_" + name)
            grads[name], deltas[name], new_m[name], new_v[name] = g.reshape(shp), d.reshape(shp), nm.reshape(shp), nv.reshape(shp)
            done.append(d)
        return done

    rest = reducer.finish_rest(update(reducer.finish_first(grad_x)))
    update(rest)
    delta_s, nm_s, nv_s, _ = _adamw(_pack_small(ln_g, ln_b, b_sinks), _pack_small(grad_ln_g, grad_ln_b, grad_sinks),
                                    _pack_small(m_ln_g, m_ln_b, m_b_sinks), _pack_small(v_ln_g, v_ln_b, v_b_sinks), "adamw_small")
    for d, blob in ((grads, None), (deltas, delta_s), (new_m, nm_s), (new_v, nv_s)):
        if blob is None:
            d["ln_g"], d["ln_b"], d["b_sinks"] = grad_ln_g, grad_ln_b, grad_sinks
        else:
            d["ln_g"], d["ln_b"], d["b_sinks"] = _unpack_small(blob, ln_g.shape, b_sinks.shape)

    order = ("ffn1_w_in", "ffn1_w_out", "ffn2_w_in", "ffn2_w_out", "ln_g", "ln_b", "a_w_qkv", "a_w_o", "kv_w", "b_w_q",
             "b_sinks", "b_w_o")
    outs = [loss, grad_x[None]]
    for d in (grads, deltas, new_m, new_v):
        outs += [d[n] for n in order]
    return tuple(outs)
```

```python
import numpy as np
import jax
import jax.numpy as jnp
from jax import lax
from jax.experimental import pallas as pl
from jax.experimental.pallas import tpu as pltpu

F32 = jnp.float32
BF16 = jnp.bfloat16

D_MODEL = 1024
D_FF = 2816
HALF_FF = D_FF // 2
HEAD_DIM = 64
N_HEADS = 16
N_KV_B = 4
GROUP_B = N_HEADS // N_KV_B
DEPTH = 2
ALPHA = (2.0 * DEPTH) ** 0.25
LN_EPS = 1e-5
BLOCK = 128
SLAB = 128
N_SLABS = D_MODEL // SLAB
PATTERNS_A = ((1, 128, 1.0), (4, 128, 4.0), (16, 128, 16.0))
PATTERNS_B = ((1, 127, 1.0),)
NEG = -1e30

ADAM_LR = 0.001
ADAM_B1 = 0.9
ADAM_B2 = 0.999
ADAM_EPS = 1e-08
ADAM_WD = 0.01
ADAM_STEP = 10

N_CHIPS = 4
VMEM_LIMIT = 56 * 1024 * 1024
MESH = pl.DeviceIdType.MESH


def _alibi_slopes(n):
    return np.array([2.0 ** (-8.0 * (h + 1) / n) for h in range(n)], dtype=np.float32)


def _cparams(sem=None, vmem=VMEM_LIMIT):
    return pltpu.CompilerParams(dimension_semantics=sem, vmem_limit_bytes=vmem)


_DIMS = {"nn": ((1,), (0,)), "nt": ((1,), (1,)), "tn": ((0,), (0,))}


def _unlead(x):
    if isinstance(x, tuple):
        return x[0], x[1], x[0].shape[1:]
    return x, None, x.shape


def _bspec(block, imap, lead=None):
    if lead is None:
        return pl.BlockSpec(block, imap)
    return pl.BlockSpec((None,) + tuple(block), lambda *g: (lead,) + tuple(imap(*g)))


def _ln_bwd_math(zv, dyv, gain):
    rows = zv.shape[0]
    mu = jnp.mean(zv, axis=-1, keepdims=True)
    zc = zv - mu
    var = jnp.mean(zc * zc, axis=-1, keepdims=True)
    rstd = lax.rsqrt(var + LN_EPS)
    xhat = zc * rstd
    dyg = dyv * gain
    m1 = jnp.mean(dyg, axis=-1, keepdims=True)
    m2 = jnp.mean(dyg * xhat, axis=-1, keepdims=True)
    dz = rstd * (dyg - m1 - xhat * m2)
    pg = jnp.sum((dyv * xhat).reshape(rows // 8, 8, D_MODEL), axis=0)
    pb = jnp.sum(dyv.reshape(rows // 8, 8, D_MODEL), axis=0)
    return dz, pg, pb


LN_ROWS = 16


def _ln_rows(n_rows, step):
    def one(i, carry):
        step(pl.ds(pl.multiple_of(i * LN_ROWS, LN_ROWS), LN_ROWS))
        return carry

    lax.fori_loop(0, n_rows // LN_ROWS, one, 0, unroll=2)


def _ln_bwd_tile(z_ref, dy, g_ref, dz_ref, dzc_ref, part_refs, c, first, extra=None):
    @pl.when(first)
    def _():
        for ref in part_refs:
            ref[...] = jnp.zeros_like(ref)

    gain = g_ref[...]

    def step(rows):
        dz, pg, pb = _ln_bwd_math(z_ref[rows, :], dy(rows), gain)
        dz_ref[rows, :] = dz
        dzc_ref[rows, :] = (c * dz).astype(BF16)
        part_refs[0][...] += pg
        part_refs[1][...] += pb
        if extra is not None:
            part_refs[2][...] += extra(rows)

    _ln_rows(z_ref.shape[0], step)


def _matmul(a, b, mode, out_dtype, tm, tn, tk, name, add=None, add_scale=1.0, split=False, into=None, ln=None):
    out_spec = pl.BlockSpec((tm, tn), lambda i, j, k: (i, j))
    base, count = (0, 3) if split is True else (split or (0, 0))
    if mode == "nn":
        a, al, (M, K) = _unlead(a)
        b, bl, (K2, N) = _unlead(b)
        a_spec = _bspec((tm, tk), lambda i, j, k: (i, k), al)
        b_spec = _bspec((tk, tn), lambda i, j, k: (k, j), bl)
        out_struct = jax.ShapeDtypeStruct((M, N), out_dtype)
        if split:
            assert tn == D_MODEL and N == count * tn
            out_spec = pl.BlockSpec((None, tm, tn), lambda i, j, k: (j + base, i, 0))
            out_struct = jax.ShapeDtypeStruct((3, M, tn), out_dtype)
    elif mode == "nt":
        b, bl, (N, K2) = _unlead(b)
        if split:
            assert tk == D_MODEL
            M, K = a.shape[1], count * a.shape[2]
            a_spec = pl.BlockSpec((None, tm, tk), lambda i, j, k: (k + base, i, 0))
        else:
            a, al, (M, K) = _unlead(a)
            a_spec = _bspec((tm, tk), lambda i, j, k: (i, k), al)
        b_spec = _bspec((tn, tk), lambda i, j, k: (j, k), bl)
        out_struct = jax.ShapeDtypeStruct((M, N), out_dtype)
    else:
        a, al, (K, M) = _unlead(a)
        if split:
            assert tn == D_MODEL
            K2, N = b.shape[1], count * b.shape[2]
            b_spec = pl.BlockSpec((None, tk, tn), lambda i, j, k: (j + base, k, 0))
        else:
            b, bl, (K2, N) = _unlead(b)
            b_spec = _bspec((tk, tn), lambda i, j, k: (k, j), bl)
        a_spec = _bspec((tk, tm), lambda i, j, k: (k, i), al)
        out_struct = jax.ShapeDtypeStruct((M, N), out_dtype)
    assert K == K2 and M % tm == 0 and N % tn == 0 and K % tk == 0, (a.shape, b.shape, mode, tm, tn, tk)
    nk = K // tk
    dims = (_DIMS[mode], ((), ()))
    has_add = add is not None

    narrow = out_dtype != F32
    assert not (narrow and has_add)
    if ln is not None:
        assert has_add and mode == "nt" and tn == N == D_MODEL

    def body(*refs):
        if into is not None:
            refs = refs[:2] + refs[3:]
        if ln is not None:
            a_ref, b_ref, add_ref, z_ref, g_ref, o_ref, dzc_ref, gg_ref, gb_ref = refs
            acc_ref = o_ref
        elif has_add:
            a_ref, b_ref, add_ref, o_ref = refs
            acc_ref = o_ref
        elif narrow:
            a_ref, b_ref, o_ref, acc_ref = refs
        else:
            a_ref, b_ref, o_ref = refs
            acc_ref = o_ref
        k = pl.program_id(2)
        part = lax.dot_general(a_ref[...].astype(BF16), b_ref[...].astype(BF16), dims, preferred_element_type=F32)
        if has_add:
            @pl.when(k == 0)
            def _():
                acc_ref[...] = part + add_scale * add_ref[...]
        else:
            @pl.when(k == 0)
            def _():
                acc_ref[...] = part

        @pl.when(k > 0)
        def _():
            acc_ref[...] += part

        if narrow:
            @pl.when(k == nk - 1)
            def _():
                o_ref[...] = acc_ref[...].astype(out_dtype)

        if ln is not None:
            @pl.when(k == nk - 1)
            def _():
                _ln_bwd_tile(z_ref, lambda rows: o_ref[rows, :], g_ref, o_ref, dzc_ref, (gg_ref, gb_ref), ln[2],
                             pl.program_id(0) == 0)

    in_specs = [a_spec, b_spec]
    args = [a, b]
    aliases = {}
    if into is not None:
        assert mode == "nn" and split and not has_add
        in_specs.append(pl.BlockSpec(memory_space=pl.ANY))
        args.append(into)
        aliases = {2: 0}
    if has_add:
        in_specs.append(pl.BlockSpec((tm, tn), lambda i, j, k: (i, j)))
        args.append(add)
    sem = ("parallel", "parallel", "arbitrary")
    if ln is not None:
        part8 = pl.BlockSpec((8, N), lambda i, j, k: (0, 0))
        in_specs += [pl.BlockSpec((tm, tn), lambda i, j, k: (i, j)), pl.BlockSpec((1, N), lambda i, j, k: (0, 0))]
        args += [ln[0], ln[1]]
        out_spec = [out_spec, pl.BlockSpec((tm, tn), lambda i, j, k: (i, j)), part8, part8]
        out_struct = [out_struct, jax.ShapeDtypeStruct((M, N), BF16), jax.ShapeDtypeStruct((8, N), F32),
                      jax.ShapeDtypeStruct((8, N), F32)]
        sem = ("arbitrary", "arbitrary", "arbitrary")
    return pl.pallas_call(
        body, name=name, grid=(M // tm, N // tn, nk),
        in_specs=in_specs, out_specs=out_spec, out_shape=out_struct, input_output_aliases=aliases,
        scratch_shapes=[pltpu.VMEM((tm, tn), F32)] if narrow else [],
        compiler_params=_cparams(sem),
    )(*args)


def _pick(n, cands):
    for c in cands:
        if n % c == 0:
            return c
    raise ValueError((n, cands))


def _mm_nn(a, b, out_dtype, name, split=False, into=None):
    M, K = _unlead(a)[2]
    N = _unlead(b)[2][1]
    return _matmul(a, b, "nn", out_dtype, _pick(M, (1024, 512, 256)), _pick(N, (1024, 512)), _pick(K, (1024, 512)), name,
                   split=split, into=into)


def _mm_nt(a, b, name, add=None, add_scale=1.0, split=False, ln=None):
    M, K = (a.shape[1], D_MODEL) if split else _unlead(a)[2]
    N = _unlead(b)[2][0]
    tms = (512, 256) if ln is not None else (1024, 512, 256)
    return _matmul(a, b, "nt", F32, _pick(M, tms), _pick(N, (1024, 512)),
                   _pick(K, (2816, 1024, 512)), name, add=add, add_scale=add_scale, split=split, ln=ln)


def _mm_tn(a, b, name, split=False, out_dtype=F32):
    K, M = _unlead(a)[2]
    N = D_MODEL if split else _unlead(b)[2][1]
    return _matmul(a, b, "tn", out_dtype, _pick(M, (1024, 1408, 512)), _pick(N, (1408, 1024, 512)),
                   _pick(K, (2048, 1024, 512, 256)), name, split=split)


def _d_kv_w(y, dqkv, name):
    S = y.shape[0]
    tk = _pick(S, (1024, 512))
    nk = S // tk
    width = N_KV_B * HEAD_DIM
    r, c = np.arange(D_MODEL)[:, None], np.arange(width)[None, :]
    fold = jnp.asarray((r // (GROUP_B * HEAD_DIM) == c // HEAD_DIM) & (r % HEAD_DIM == c % HEAD_DIM), BF16)

    def body(y_ref, dk_ref, dv_ref, f_ref, o_ref, acc_ref):
        k = pl.program_id(0)
        summed = jnp.concatenate([jnp.dot(ref[...], f_ref[...], preferred_element_type=F32).astype(BF16)
                                  for ref in (dk_ref, dv_ref)], axis=1)
        part = lax.dot_general(summed, y_ref[...], (_DIMS["tn"], ((), ())), preferred_element_type=F32)

        @pl.when(k == 0)
        def _():
            acc_ref[...] = part

        @pl.when(k > 0)
        def _():
            acc_ref[...] += part

        @pl.when(k == nk - 1)
        def _():
            o_ref[...] = acc_ref[...].T.astype(BF16)

    return pl.pallas_call(
        body, name=name, grid=(nk,),
        in_specs=[pl.BlockSpec((tk, D_MODEL), lambda k: (k, 0)),
                  pl.BlockSpec((None, tk, D_MODEL), lambda k: (1, k, 0)),
                  pl.BlockSpec((None, tk, D_MODEL), lambda k: (2, k, 0)),
                  pl.BlockSpec((D_MODEL, width), lambda k: (0, 0))],
        out_specs=pl.BlockSpec((D_MODEL, 2 * width), lambda k: (0, 0)),
        out_shape=jax.ShapeDtypeStruct((D_MODEL, 2 * width), BF16),
        scratch_shapes=[pltpu.VMEM((2 * width, D_MODEL), F32)],
        compiler_params=_cparams(("arbitrary",)),
    )(y, dqkv, dqkv, fold)


def _ffn_in(x, w, name):
    S = x.shape[0]
    tm = _pick(S, (512, 256))
    w, wl, _ = _unlead(w)

    def body(x_ref, w_ref, t_ref, h_ref):
        acc = jnp.dot(x_ref[...].astype(BF16), w_ref[...], preferred_element_type=F32)
        g = acc[:, :HALF_FF]
        up = acc[:, HALF_FF:]
        sg = jax.nn.sigmoid(g)
        silu = g * sg
        t_ref[:, :HALF_FF] = (up * (sg * (1.0 + g * (1.0 - sg)))).astype(BF16)
        t_ref[:, HALF_FF:] = silu.astype(BF16)
        h_ref[...] = (silu * up).astype(BF16)

    return pl.pallas_call(
        body, name=name, grid=(2, S // tm),
        in_specs=[pl.BlockSpec((tm, D_MODEL), lambda j, i: (i, 0)),
                  _bspec((D_MODEL, D_FF), lambda j, i: (0, j), wl)],
        out_specs=[pl.BlockSpec((tm, D_FF), lambda j, i: (i, j)),
                   pl.BlockSpec((tm, HALF_FF), lambda j, i: (i, j))],
        out_shape=[jax.ShapeDtypeStruct((S, 2 * D_FF), BF16), jax.ShapeDtypeStruct((S, D_FF), BF16)],
        compiler_params=_cparams(("parallel", "parallel")),
    )(x, w)


def _ffn_bwd_h(dzc, w_out, u, name):
    S = dzc.shape[0]
    tm = _pick(S, (512, 256))
    w_out, wl, _ = _unlead(w_out)

    def body(dz_ref, w_ref, t_ref, du_ref):
        dh = lax.dot_general(dz_ref[...], w_ref[...], (((1,), (1,)), ((), ())), preferred_element_type=F32)
        du_ref[:, :HALF_FF] = (dh * t_ref[:, :HALF_FF].astype(F32)).astype(BF16)
        du_ref[:, HALF_FF:] = (dh * t_ref[:, HALF_FF:].astype(F32)).astype(BF16)

    return pl.pallas_call(
        body, name=name, grid=(2, S // tm),
        in_specs=[pl.BlockSpec((tm, D_MODEL), lambda j, i: (i, 0)),
                  _bspec((HALF_FF, D_MODEL), lambda j, i: (j, 0), wl),
                  pl.BlockSpec((tm, D_FF), lambda j, i: (i, j))],
        out_specs=pl.BlockSpec((tm, D_FF), lambda j, i: (i, j)),
        out_shape=jax.ShapeDtypeStruct((S, 2 * D_FF), BF16),
        compiler_params=_cparams(("parallel", "parallel")),
    )(dzc, w_out, u)


def _mm_ln(a, w, resid, gain, bias, c, name):
    S, K = a.shape
    tm = _pick(S, (512, 256))
    w, wl, _ = _unlead(w)

    def body(a_ref, w_ref, r_ref, g_ref, b_ref, y_ref, yb_ref, z_ref):
        z_ref[...] = jnp.dot(a_ref[...], w_ref[...], preferred_element_type=F32)
        gain, bias = g_ref[...], b_ref[...]

        def step(rows):
            z = ALPHA * r_ref[rows, :] + c * z_ref[rows, :]
            mu = jnp.mean(z, axis=-1, keepdims=True)
            zc = z - mu
            var = jnp.mean(zc * zc, axis=-1, keepdims=True)
            y = zc * lax.rsqrt(var + LN_EPS) * gain + bias
            z_ref[rows, :] = z
            y_ref[rows, :] = y
            yb_ref[rows, :] = y.astype(BF16)

        _ln_rows(tm, step)

    row = pl.BlockSpec((tm, D_MODEL), lambda i: (i, 0))
    vec = pl.BlockSpec((1, D_MODEL), lambda i: (0, 0))
    return pl.pallas_call(
        body, name=name, grid=(S // tm,),
        in_specs=[pl.BlockSpec((tm, K), lambda i: (i, 0)), _bspec((K, D_MODEL), lambda i: (0, 0), wl), row, vec, vec],
        out_specs=[row, row, row],
        out_shape=[jax.ShapeDtypeStruct((S, D_MODEL), F32), jax.ShapeDtypeStruct((S, D_MODEL), BF16),
                   jax.ShapeDtypeStruct((S, D_MODEL), F32)],
        compiler_params=_cparams(("parallel",)),
    )(a, w, resid, gain, bias)


def _loss_ln_bwd(y, t, z, gain, c, name):
    S = y.shape[0]
    tm = _pick(S, (512, 256))

    def body(y_ref, t_ref, z_ref, g_ref, dz_ref, dzc_ref, gg_ref, gb_ref, sq_ref):
        def err(rows):
            return y_ref[rows, :] - t_ref[rows, :]

        def sq(rows):
            e = err(rows)
            return jnp.sum((e * e).reshape(LN_ROWS // 8, 8, D_MODEL), axis=0)

        _ln_bwd_tile(z_ref, lambda rows: err(rows) * (1.0 / D_MODEL), g_ref, dz_ref, dzc_ref, (gg_ref, gb_ref, sq_ref), c,
                     pl.program_id(0) == 0, extra=sq)

    row = pl.BlockSpec((tm, D_MODEL), lambda i: (i, 0))
    part = pl.BlockSpec((8, D_MODEL), lambda i: (0, 0))
    part_shape = jax.ShapeDtypeStruct((8, D_MODEL), F32)
    return pl.pallas_call(
        body, name=name, grid=(S // tm,),
        in_specs=[row, row, row, pl.BlockSpec((1, D_MODEL), lambda i: (0, 0))],
        out_specs=[row, row, part, part, part],
        out_shape=[jax.ShapeDtypeStruct((S, D_MODEL), F32), jax.ShapeDtypeStruct((S, D_MODEL), BF16),
                   part_shape, part_shape, part_shape],
        compiler_params=_cparams(("arbitrary",)),
    )(y, t, z, gain)


def _rows(start, d):
    if d == 1:
        return pl.ds(pl.multiple_of(start, BLOCK), BLOCK)
    return pl.ds(start, BLOCK, stride=d)


def _ld(ref, start, d):
    return ref[_rows(start, d), :]


def _ld3(ref, lead, start, d):
    return ref[lead, _rows(start, d), :]


def _st3(ref, lead, start, d, val):
    ref[lead, _rows(start, d), :] = val


def _acc3(ref, lead, start, d, val):
    ref[lead, _rows(start, d), :] = ref[lead, _rows(start, d), :] + val


def _band_consts(slope0, slope1, maxd, scale):
    row = lax.broadcasted_iota(jnp.int32, (2 * BLOCK, 2 * BLOCK), 0)
    kj = lax.broadcasted_iota(jnp.int32, (2 * BLOCK, 2 * BLOCK), 1)
    top = row < BLOCK
    dist = BLOCK + jnp.where(top, row, row - BLOCK) - kj
    slope = jnp.where(top, slope0, slope1)
    base = jnp.where((dist >= 0) & (dist <= maxd), -(slope * (dist.astype(F32) * scale)), NEG)
    return base, kj < BLOCK


def _stack_heads(x, lo):
    return jnp.concatenate([jnp.where(lo, x, 0.0), jnp.where(lo, 0.0, x)], axis=0)


def _unstack_heads(x2, lo):
    return jnp.where(lo, x2[:BLOCK], x2[BLOCK:])


def _scores(q2, k2, base, prev_keys, first):
    s = lax.dot_general(q2, k2, (((1,), (1,)), ((), ())), preferred_element_type=F32) * (HEAD_DIM ** -0.5) + base
    return jnp.where(jnp.logical_and(prev_keys, first), NEG, s)


def _softmax_weights(ls):
    mx = ls[0]
    for l in ls[1:]:
        mx = jnp.maximum(mx, l)
    es = [jnp.exp(l - mx) for l in ls]
    tot = es[0]
    for e in es[1:]:
        tot = tot + e
    inv = 1.0 / tot
    return [e * inv for e in es]


def _attn_fwd(qkv, slopes, sinks, patterns, name):
    S = qkv.shape[1]
    npat = len(patterns)
    has_sink = sinks is not None
    if not has_sink:
        sinks = jnp.zeros((N_HEADS,), F32)
    rows_c = 256

    def body(slopes_ref, sinks_ref, x_ref, mix_ref, o_ref, lse_ref, o_scr, lse_scr):
        p = pl.program_id(0)
        lo = lax.broadcasted_iota(jnp.int32, (BLOCK, SLAB), 1) < HEAD_DIM
        top1 = lax.broadcasted_iota(jnp.int32, (2 * BLOCK, 1), 0) < BLOCK
        sk2 = jnp.where(top1, sinks_ref[2 * p], sinks_ref[2 * p + 1])
        for pi, (d, maxd, scale) in enumerate(patterns):
            nb = S // d // BLOCK
            base, prev_keys = _band_consts(slopes_ref[2 * p], slopes_ref[2 * p + 1], maxd, scale)

            def blk(t, carry, pi=pi, d=d, nb=nb, base=base, prev_keys=prev_keys):
                r = t // nb
                n = t - r * nb
                start = r + (d * BLOCK) * n
                prev = jnp.where(n > 0, start - d * BLOCK, start)
                q2 = _stack_heads(_ld3(x_ref, 0, start, d), lo).astype(BF16)
                k2 = jnp.concatenate([_ld3(x_ref, 1, prev, d), _ld3(x_ref, 1, start, d)], axis=0).astype(BF16)
                v2 = jnp.concatenate([_ld3(x_ref, 2, prev, d), _ld3(x_ref, 2, start, d)], axis=0).astype(BF16)
                s = _scores(q2, k2, base, prev_keys, n == 0)
                m = jnp.max(s, axis=-1, keepdims=True)
                if has_sink:
                    m = jnp.maximum(m, sk2)
                e = jnp.exp(s - m)
                den = jnp.sum(e, axis=-1, keepdims=True)
                if has_sink:
                    den = den + jnp.exp(sk2 - m)
                o2 = jnp.dot((e / den).astype(BF16), v2, preferred_element_type=F32)
                _st3(o_scr, pi, start, d, _unstack_heads(o2, lo))
                _st3(lse_scr, pi, start, d, _unstack_heads(m + jnp.log(den), lo))
                return carry

            lax.fori_loop(0, d * nb, blk, 0, unroll=8)

        lane_c = lax.broadcasted_iota(jnp.int32, (rows_c, SLAB), 1)

        def comb(ci, carry):
            rows = pl.ds(pl.multiple_of(ci * rows_c, rows_c), rows_c)
            ls = [lse_scr[i, rows, :] for i in range(npat)]
            packed = jnp.zeros((rows_c, SLAB), F32)
            for i in range(npat):
                o_ref[i, rows, :] = o_scr[i, rows, :].astype(BF16)
                packed = jnp.where(lane_c % HEAD_DIM == i, ls[i], packed)
            lse_ref[rows, :] = packed
            if npat == 1:
                mix_ref[rows, :] = o_scr[0, rows, :].astype(BF16)
            else:
                ws = _softmax_weights(ls)
                acc = ws[0] * o_scr[0, rows, :]
                for i in range(1, npat):
                    acc = acc + ws[i] * o_scr[i, rows, :]
                mix_ref[rows, :] = acc.astype(BF16)
            return carry

        lax.fori_loop(0, S // rows_c, comb, 0, unroll=2)

    smem = pl.BlockSpec(memory_space=pltpu.SMEM)
    return pl.pallas_call(
        body, name=name, grid=(N_SLABS,),
        in_specs=[smem, smem, pl.BlockSpec((3, S, SLAB), lambda p: (0, 0, p))],
        out_specs=[pl.BlockSpec((S, SLAB), lambda p: (0, p)), pl.BlockSpec((npat, S, SLAB), lambda p: (0, 0, p)),
                   pl.BlockSpec((None, S, SLAB), lambda p: (p, 0, 0))],
        out_shape=[jax.ShapeDtypeStruct((S, D_MODEL), BF16), jax.ShapeDtypeStruct((npat, S, D_MODEL), BF16),
                   jax.ShapeDtypeStruct((N_SLABS, S, SLAB), F32)],
        scratch_shapes=[pltpu.VMEM((npat, S, SLAB), F32), pltpu.VMEM((npat, S, SLAB), F32)],
        compiler_params=_cparams(("arbitrary",)),
    )(slopes, sinks, qkv)


def _attn_bwd(qkv, dout, o, lse, slopes, sinks, patterns, name):
    S = qkv.shape[1]
    npat = len(patterns)
    has_sink = sinks is not None
    if not has_sink:
        sinks = jnp.zeros((N_HEADS,), F32)
    rows_c = 256

    def headsum(x, lo):
        same = (lax.broadcasted_iota(jnp.int32, (SLAB, SLAB), 0) < HEAD_DIM) == (lax.broadcasted_iota(jnp.int32, (SLAB, SLAB), 1) < HEAD_DIM)
        return jnp.dot(x, same.astype(F32), precision=lax.Precision.HIGH, preferred_element_type=F32)

    def body(slopes_ref, sinks_ref, x_ref, do_ref, o_ref, lsep_ref, dxo_ref, dsink_ref, dbar_ref, sacc_ref, lse_ref, dx_ref):
        p = pl.program_id(0)
        lo = lax.broadcasted_iota(jnp.int32, (BLOCK, SLAB), 1) < HEAD_DIM
        lo_c = lax.broadcasted_iota(jnp.int32, (rows_c, SLAB), 1) < HEAD_DIM
        top1 = lax.broadcasted_iota(jnp.int32, (2 * BLOCK, 1), 0) < BLOCK
        sk2 = jnp.where(top1, sinks_ref[2 * p], sinks_ref[2 * p + 1])

        def prep(ci, carry):
            rows = pl.ds(pl.multiple_of(ci * rows_c, rows_c), rows_c)
            dov = do_ref[rows, :]
            dx_ref[:, rows, :] = jnp.zeros((3, rows_c, SLAB), F32)
            packed = lsep_ref[rows, :]
            ls = [jnp.where(lo_c, packed[:, i:i + 1], packed[:, HEAD_DIM + i:HEAD_DIM + i + 1]) for i in range(npat)]
            for i in range(npat):
                lse_ref[i, rows, :] = ls[i]
            if npat == 1:
                dbar_ref[rows, :] = headsum(dov * o_ref[0, rows, :].astype(F32), lo_c)
            else:
                ws = _softmax_weights(ls)
                acc = ws[0] * headsum(dov * o_ref[0, rows, :].astype(F32), lo_c)
                for i in range(1, npat):
                    acc = acc + ws[i] * headsum(dov * o_ref[i, rows, :].astype(F32), lo_c)
                dbar_ref[rows, :] = acc
            return carry

        lax.fori_loop(0, S // rows_c, prep, 0, unroll=2)
        sacc_ref[...] = jnp.zeros((BLOCK, SLAB), F32)

        for pi, (d, maxd, scale) in enumerate(patterns):
            nb = S // d // BLOCK
            base, prev_keys = _band_consts(slopes_ref[2 * p], slopes_ref[2 * p + 1], maxd, scale)

            def blk(t, carry, pi=pi, d=d, nb=nb, base=base, prev_keys=prev_keys):
                r = t // nb
                n = t - r * nb
                start = r + (d * BLOCK) * n
                prev = jnp.where(n > 0, start - d * BLOCK, start)
                q2 = _stack_heads(_ld3(x_ref, 0, start, d), lo).astype(BF16)
                k2 = jnp.concatenate([_ld3(x_ref, 1, prev, d), _ld3(x_ref, 1, start, d)], axis=0).astype(BF16)
                v2 = jnp.concatenate([_ld3(x_ref, 2, prev, d), _ld3(x_ref, 2, start, d)], axis=0).astype(BF16)
                ls = [_ld3(lse_ref, i, start, d) for i in range(npat)]
                w = _softmax_weights(ls)[pi] if npat > 1 else 1.0
                do2 = _stack_heads(w * _ld(do_ref, start, d), lo).astype(BF16)
                dl = w * _ld(dbar_ref, start, d)
                lse2 = jnp.concatenate([ls[pi][:, :1], ls[pi][:, HEAD_DIM:HEAD_DIM + 1]], axis=0)
                dl2 = jnp.concatenate([dl[:, :1], dl[:, HEAD_DIM:HEAD_DIM + 1]], axis=0)
                s = _scores(q2, k2, base, prev_keys, n == 0)
                pr = jnp.exp(s - lse2)
                dp = lax.dot_general(do2, v2, (((1,), (1,)), ((), ())), preferred_element_type=F32)
                ds = (pr * (dp - dl2) * (HEAD_DIM ** -0.5)).astype(BF16)
                dq2 = jnp.dot(ds, k2, preferred_element_type=F32)
                dk2 = lax.dot_general(ds, q2, (((0,), (0,)), ((), ())), preferred_element_type=F32)
                dv2 = lax.dot_general(pr.astype(BF16), do2, (((0,), (0,)), ((), ())), preferred_element_type=F32)
                _acc3(dx_ref, 0, start, d, _unstack_heads(dq2, lo))
                _acc3(dx_ref, 1, prev, d, dk2[:BLOCK])
                _acc3(dx_ref, 1, start, d, dk2[BLOCK:])
                _acc3(dx_ref, 2, prev, d, dv2[:BLOCK])
                _acc3(dx_ref, 2, start, d, dv2[BLOCK:])
                if has_sink:
                    sacc_ref[...] += _unstack_heads(-jnp.exp(sk2 - lse2) * dl2, lo)
                return carry

            lax.fori_loop(0, d * nb, blk, 0, unroll=8)

        dsink_ref[...] = jnp.broadcast_to(jnp.sum(sacc_ref[...], axis=0, keepdims=True), (8, SLAB))

        def emit(ci, carry):
            rows = pl.ds(pl.multiple_of(ci * rows_c, rows_c), rows_c)
            dxo_ref[:, rows, :] = dx_ref[:, rows, :].astype(BF16)
            return carry

        lax.fori_loop(0, S // rows_c, emit, 0, unroll=2)

    smem = pl.BlockSpec(memory_space=pltpu.SMEM)
    return pl.pallas_call(
        body, name=name, grid=(N_SLABS,),
        in_specs=[smem, smem, pl.BlockSpec((3, S, SLAB), lambda p: (0, 0, p)), pl.BlockSpec((S, SLAB), lambda p: (0, p)),
                  pl.BlockSpec((npat, S, SLAB), lambda p: (0, 0, p)), pl.BlockSpec((None, S, SLAB), lambda p: (p, 0, 0))],
        out_specs=[pl.BlockSpec((3, S, SLAB), lambda p: (0, 0, p)), pl.BlockSpec((None, 8, SLAB), lambda p: (p, 0, 0))],
        out_shape=[jax.ShapeDtypeStruct((3, S, D_MODEL), BF16), jax.ShapeDtypeStruct((N_SLABS, 8, SLAB), F32)],
        scratch_shapes=[pltpu.VMEM((S, SLAB), F32), pltpu.VMEM((BLOCK, SLAB), F32), pltpu.VMEM((npat, S, SLAB), F32),
                        pltpu.VMEM((3, S, SLAB), F32)],
        compiler_params=_cparams(("arbitrary",)),
    )(slopes, sinks, qkv, dout, o, lse)


def _place():
    x, y, c = lax.axis_index("x"), lax.axis_index("y"), lax.axis_index("c")
    return x, y, c, 2 * x + y


def _other_chips(x, y):
    return [(1 - x, y), (x, 1 - y), (1 - x, 1 - y)]


HBM_SPEC = pl.BlockSpec(memory_space=pl.ANY)


def _slot(q):
    return 2 * (q % 2) + q // 2


BIG = ("ffn1_w_in", "ffn1_w_out", "ffn2_w_in", "ffn2_w_out", "a_w_qkv", "a_w_o", "kv_w", "b_w_q", "b_w_o")
QKV_SHARD = 3 * D_MODEL // N_CHIPS
ROW_SHARD = D_MODEL // N_CHIPS


LAYER0_ITEMS = (("ffn1_w_in", 0), ("ffn1_w_out", 0), ("a_w_qkv", None), ("a_w_o", None), ("ffn2_w_in", 0),
                ("ffn2_w_out", 0), ("kv_w", None))
LAYER1_ITEMS = (("ffn1_w_in", 1), ("ffn1_w_out", 1), ("b_w_q", None), ("b_w_o", None), ("ffn2_w_in", 1),
                ("ffn2_w_out", 1))
OUT_SHARD = D_FF // N_CHIPS


def _full_shape(name):
    if name.endswith("w_in"):
        return (D_MODEL, 2 * D_FF)
    if name.endswith("w_out"):
        return (D_FF, D_MODEL)
    if name == "a_w_qkv":
        return (D_MODEL, 3 * D_MODEL)
    if name == "kv_w":
        return (N_CHIPS, 2, ROW_SHARD // 2, 2 * N_KV_B * HEAD_DIM)
    return (N_CHIPS, 2, ROW_SHARD // 2, D_MODEL)


def _gather_src(item, ref, c):
    name, layer = item
    if name.endswith("w_in"):
        return ref.at[layer, pl.ds(c * (D_MODEL // 2), D_MODEL // 2)]
    if name.endswith("w_out"):
        return ref.at[layer, pl.ds(c * (OUT_SHARD // 2), OUT_SHARD // 2)]
    if name == "a_w_qkv":
        return ref.at[0, pl.ds(c * (D_MODEL // 2), D_MODEL // 2)]
    if name == "kv_w":
        return ref.at[pl.ds(c * (ROW_SHARD // 2), ROW_SHARD // 2)]
    return ref.at[0, pl.ds(c * (ROW_SHARD // 2), ROW_SHARD // 2)]


def _gather_dst(item, ref, q, c):
    name, _ = item
    if name.endswith("w_in"):
        return ref.at[pl.ds(c * (D_MODEL // 2), D_MODEL // 2), pl.ds(_slot(q) * HALF_FF, HALF_FF)]
    if name.endswith("w_out"):
        return ref.at[pl.ds(q * OUT_SHARD + c * (OUT_SHARD // 2), OUT_SHARD // 2)]
    if name == "a_w_qkv":
        return ref.at[pl.ds(c * (D_MODEL // 2), D_MODEL // 2), pl.ds(q * QKV_SHARD, QKV_SHARD)]
    return ref.at[q, c]


def _all_gather(items, shards, small):
    n = len(items)
    r = small.shape[0]
    per = 8

    def body(*refs):
        srcs, small_ref = refs[:n], refs[n]
        dsts, s_ref = refs[n + 1:2 * n + 1], refs[2 * n + 1]
        send_sems, recv_sems = refs[2 * n + 2:]
        x, y, c, myq = _place()
        sibling = (x, y, 1 - c)
        chips = _other_chips(x, y)

        def big(t, k, src, q, h, to):
            return pltpu.make_async_remote_copy(src_ref=src, dst_ref=_gather_dst(items[t], dsts[t], q, h),
                                                send_sem=send_sems.at[per * t + k], recv_sem=recv_sems.at[per * t + k],
                                                device_id=to, device_id_type=MESH)

        def tiny(k, q, to):
            return pltpu.make_async_remote_copy(src_ref=small_ref, dst_ref=s_ref.at[q], send_sem=send_sems.at[per * n + k],
                                                recv_sem=recv_sems.at[per * n + k], device_id=to, device_id_type=MESH)

        first = []
        for j, chip in enumerate(chips):
            if j < 2:
                first += [big(t, j, _gather_src(items[t], srcs[t], c), myq, c, (*chip, c)) for t in range(n)]
            first.append(tiny(j, myq, (*chip, c)))
        own = [big(t, 6 + h, _gather_src(items[t], srcs[t], h), myq, h, sibling) for t in range(n) for h in (0, 1)]
        own.append(tiny(3, myq, sibling))
        for cp in first + own:
            cp.start()
        relay_from = ((x + 1 - c) % 2, (y + c) % 2)
        relay_to = ((x + c) % 2, (y + 1 - c) % 2, c)
        q_relay = 2 * relay_from[0] + relay_from[1]
        passed = []
        for t in range(n):
            src = _gather_src(items[t], srcs[t], c)
            for j, (cx, cy) in enumerate(chips[:2]):
                q = 2 * cx + cy
                big(t, j, src, q, c, sibling).wait_recv()
                fwd = big(t, 3 + j, _gather_dst(items[t], dsts[t], q, c), q, c, sibling)
                fwd.start()
                passed.append(fwd)
            relay = big(t, 2, _gather_dst(items[t], dsts[t], q_relay, c), q_relay, c, relay_to)
            relay.start()
            passed.append(relay)
        q_diag = 2 * chips[2][0] + chips[2][1]
        for t in range(n):
            big(t, 2, _gather_src(items[t], srcs[t], c), q_diag, c, sibling).wait_recv()
            fwd = big(t, 5, _gather_dst(items[t], dsts[t], q_diag, c), q_diag, c, sibling)
            fwd.start()
            passed.append(fwd)
        for j, (cx, cy) in enumerate(chips):
            q = 2 * cx + cy
            for t in range(n):
                big(t, 3 + j, _gather_src(items[t], srcs[t], c), q, 1 - c, sibling).wait_recv()
            tiny(j, q, sibling).wait_recv()
        for cp in own:
            cp.wait_recv()
        for cp in first + passed + own:
            cp.wait_send()

    outs = pl.pallas_call(
        body, name="all_gather_layer0",
        in_specs=[HBM_SPEC] * (n + 1), out_specs=[HBM_SPEC] * (n + 1),
        out_shape=[jax.ShapeDtypeStruct(_full_shape(name), BF16) for name, _ in items]
        + [jax.ShapeDtypeStruct((N_CHIPS, r, 128), F32)],
        scratch_shapes=[pltpu.SemaphoreType.DMA((per * n + 4,)), pltpu.SemaphoreType.DMA((per * n + 4,))],
    )(*[shards[item] for item in items], small)
    return list(outs[:n]), outs[n]


SEM_SPEC = pl.BlockSpec(memory_space=pltpu.SEMAPHORE)
DATAFLOW = pltpu.SideEffectType.DATAFLOW_SIDE_EFFECTING
PER_ITEM = 8


def _split_start(name, copies, n_sems, sources, land_shapes, after):
    n, m = len(sources), len(land_shapes)

    def body(*refs):
        srcs, lands = refs[:n], refs[n:n + m]
        send_sems, recv_sems = refs[n + m + 1], refs[n + m + 2]
        token = refs[-1]
        for src, dst_there, _, s, peer in copies(srcs, lands):
            pltpu.make_async_remote_copy(src_ref=src, dst_ref=dst_there, send_sem=send_sems.at[s], recv_sem=recv_sems.at[s],
                                         device_id=peer, device_id_type=MESH).start()
        token[...] = jnp.zeros_like(token)

    src_arrays = [pltpu.with_memory_space_constraint(a, pltpu.HBM) for a in sources]
    land_arrays = [pltpu.with_memory_space_constraint(lax.empty(s.shape, s.dtype), pltpu.HBM) for s in land_shapes]
    hbm = pl.BlockSpec(memory_space=pltpu.HBM)
    outs = pl.pallas_call(
        body, name=name,
        in_specs=[hbm] * (n + m) + [HBM_SPEC],
        out_specs=[SEM_SPEC, SEM_SPEC] + [hbm] * (n + m) + [pl.BlockSpec(memory_space=pltpu.VMEM)],
        out_shape=[pltpu.SemaphoreType.DMA((n_sems,)), pltpu.SemaphoreType.DMA((n_sems,))]
        + [pltpu.HBM(a.shape, a.dtype) for a in src_arrays + land_arrays] + [jax.ShapeDtypeStruct((8, 128), F32)],
        input_output_aliases={i: 2 + i for i in range(n + m)},
        compiler_params=pltpu.CompilerParams(has_side_effects=DATAFLOW),
    )(*src_arrays, *land_arrays, after)
    return (outs[0], outs[1], list(outs[2:2 + n]), list(outs[2 + n:2 + n + m])), outs[-1]


def _split_wait(name, copies, state, after):
    send_sems, recv_sems, srcs_thru, lands_thru = state
    n, m = len(srcs_thru), len(lands_thru)
    after = list(after) if isinstance(after, (list, tuple)) else [after]

    def body(*refs):
        srcs, lands = refs[:n], refs[n:n + m]
        send_sems, recv_sems = refs[n + m], refs[n + m + 1]
        for src, _, dst_here, s, peer in copies(srcs, lands):
            cp = pltpu.make_async_remote_copy(src_ref=src, dst_ref=dst_here, send_sem=send_sems.at[s], recv_sem=recv_sems.at[s],
                                              device_id=peer, device_id_type=MESH)
            cp.wait_send()
            cp.wait_recv()

    hbm = pl.BlockSpec(memory_space=pltpu.HBM)
    outs = pl.pallas_call(
        body, name=name,
        in_specs=[hbm] * (n + m) + [SEM_SPEC, SEM_SPEC] + [HBM_SPEC] * len(after),
        out_specs=[hbm] * (n + m),
        out_shape=[pltpu.HBM(a.shape, a.dtype) for a in srcs_thru + lands_thru],
        input_output_aliases={i: i for i in range(n + m)},
        compiler_params=pltpu.CompilerParams(has_side_effects=DATAFLOW),
    )(*srcs_thru, *lands_thru, send_sems, recv_sems, *after)
    return list(outs[:n]), list(outs[n:])


def _gather_copies(items):
    def copies(srcs, lands):
        x, y, c, myq = _place()
        out = []
        for t, item in enumerate(items):
            for h in (0, 1):
                src = _gather_src(item, srcs[t], h)
                for j, (cx, cy) in enumerate(_other_chips(x, y)):
                    out.append((src, _gather_dst(item, lands[t], myq, h), _gather_dst(item, lands[t], 2 * cx + cy, h),
                                PER_ITEM * t + 2 * j + h, (cx, cy, c)))
                out.append((src, _gather_dst(item, lands[t], myq, h), _gather_dst(item, lands[t], myq, h),
                            PER_ITEM * t + 6 + h, (x, y, 1 - c)))
        return out
    return copies


def _gather_start(items, shards, after):
    lands = [jax.ShapeDtypeStruct(_full_shape(name), BF16) for name, _ in items]
    return _split_start("gather_layer1_start", _gather_copies(items), PER_ITEM * len(items),
                        [shards[item] for item in items], lands, after)


def _gather_wait(items, state, after):
    return _split_wait("gather_layer1_wait", _gather_copies(items), state, after)[1]


def _small_all_reduce(v):
    r = v.shape[0]

    def body(v_ref, o_ref, buf_ref, send_sems, recv_sems):
        x, y, c, _ = _place()
        me = 4 * x + 2 * y + c
        buf_ref[me] = v_ref[...]
        copies = []
        for k in range(1, 8):
            fx, fy, fc = (k >> 2) & 1, (k >> 1) & 1, k & 1
            to = (x ^ fx, y ^ fy, c ^ fc)
            cp = pltpu.make_async_remote_copy(src_ref=v_ref, dst_ref=buf_ref.at[me], send_sem=send_sems.at[k - 1],
                                              recv_sem=recv_sems.at[k - 1], device_id=to, device_id_type=MESH)
            cp.start()
            copies.append(cp)
        for k in range(1, 8):
            fx, fy, fc = (k >> 2) & 1, (k >> 1) & 1, k & 1
            src_dev = 4 * (x ^ fx) + 2 * (y ^ fy) + (c ^ fc)
            pltpu.make_async_remote_copy(src_ref=v_ref, dst_ref=buf_ref.at[src_dev], send_sem=send_sems.at[k - 1],
                                         recv_sem=recv_sems.at[k - 1], device_id=(x, y, c), device_id_type=MESH).wait_recv()
        for cp in copies:
            cp.wait_send()
        tot = buf_ref[0]
        for i in range(1, 8):
            tot = tot + buf_ref[i]
        o_ref[...] = tot

    vm = pl.BlockSpec(memory_space=pltpu.VMEM)
    return pl.pallas_call(
        body, name="small_all_reduce", in_specs=[vm], out_specs=vm,
        out_shape=jax.ShapeDtypeStruct((r, 128), F32),
        scratch_shapes=[pltpu.VMEM((8, r, 128), F32), pltpu.SemaphoreType.DMA((7,)), pltpu.SemaphoreType.DMA((7,))],
    )(v)


def _grad_view(kind, g):
    if kind == "col":
        return g.reshape(2, g.shape[0] // 2, g.shape[1])
    return g.reshape(N_CHIPS, 2, g.shape[0] // (2 * N_CHIPS), g.shape[1])


def _half_of(kind, ref, h):
    return ref.at[h] if kind == "col" else ref.at[:, h]


def _half_shape(kind, view_shape):
    return view_shape[1:] if kind == "col" else (view_shape[0],) + view_shape[2:]


def _piece_of(kind, width, colblock, ref, q):
    if kind == "col":
        return ref.at[:, pl.ds(colblock(q) * width, width)]
    return ref.at[q]


def _piece_shape(kind, width, half_shape):
    return (half_shape[0], width) if kind == "col" else half_shape[1:]


def _pair_exchange(views, kinds, name):
    n = len(views)

    def body(*refs):
        ins, outs = refs[:n], refs[n:2 * n]
        send_sems, recv_sems = refs[2 * n:]
        x, y, c, _ = _place()
        cps = []
        for t in range(n):
            cp = pltpu.make_async_remote_copy(src_ref=_half_of(kinds[t], ins[t], 1 - c), dst_ref=outs[t],
                                              send_sem=send_sems.at[t], recv_sem=recv_sems.at[t],
                                              device_id=(x, y, 1 - c), device_id_type=MESH)
            cp.start()
            cps.append(cp)
        for cp in cps:
            cp.wait()

    return pl.pallas_call(
        body, name=name, in_specs=[HBM_SPEC] * n, out_specs=[HBM_SPEC] * n,
        out_shape=[jax.ShapeDtypeStruct(_half_shape(k, v.shape), v.dtype) for k, v in zip(kinds, views)],
        scratch_shapes=[pltpu.SemaphoreType.DMA((n,)), pltpu.SemaphoreType.DMA((n,))],
    )(*views)


def _pair_sum(kind, view, recv, c, name):
    hs = recv.shape
    N = hs[-1]
    rows = hs[-2]
    tr = _pick(rows, (512, 352, 128))
    tn = _pick(N, (1408, 1024, 512))

    def body(c_ref, p_ref, r_ref, s_ref):
        s_ref[...] = (p_ref[...] + r_ref[...]).astype(BF16)

    if kind == "col":
        grid = (rows // tr, N // tn)
        mine = pl.BlockSpec((None, tr, tn), lambda i, j, c_ref: (c_ref[0], i, j))
        blk = pl.BlockSpec((tr, tn), lambda i, j, c_ref: (i, j))
        sem = ("parallel", "parallel")
    else:
        grid = (N_CHIPS, rows // tr, N // tn)
        mine = pl.BlockSpec((None, None, tr, tn), lambda q, i, j, c_ref: (q, c_ref[0], i, j))
        blk = pl.BlockSpec((None, tr, tn), lambda q, i, j, c_ref: (q, i, j))
        sem = ("parallel", "parallel", "parallel")
    return pl.pallas_call(
        body, name=name,
        grid_spec=pltpu.PrefetchScalarGridSpec(num_scalar_prefetch=1, grid=grid, in_specs=[mine, blk], out_specs=blk),
        out_shape=jax.ShapeDtypeStruct(hs, BF16),
        compiler_params=_cparams(sem),
    )(c.reshape(1).astype(jnp.int32), view, recv)


def _chip_copies(kinds, widths, colblocks):
    def copies(srcs, lands):
        x, y, c, _ = _place()
        out = []
        for j, (cx, cy) in enumerate(_other_chips(x, y)):
            for t in range(len(kinds)):
                out.append((_piece_of(kinds[t], widths[t], colblocks[t], srcs[t], 2 * cx + cy), lands[t].at[j],
                            lands[t].at[j], 3 * t + j, (cx, cy, c)))
        return out
    return copies


def _chip_land_shapes(sums, kinds, widths):
    return [jax.ShapeDtypeStruct((3,) + _piece_shape(k, w, s.shape), BF16) for k, w, s in zip(kinds, widths, sums)]


def _chip_exchange(sums, kinds, widths, colblocks, name):
    n = len(sums)
    copies = _chip_copies(kinds, widths, colblocks)

    def body(*refs):
        send_sems, recv_sems = refs[2 * n:]
        cps = [pltpu.make_async_remote_copy(src_ref=src, dst_ref=dst, send_sem=send_sems.at[s], recv_sem=recv_sems.at[s],
                                            device_id=peer, device_id_type=MESH)
               for src, dst, _, s, peer in copies(refs[:n], refs[n:2 * n])]
        for cp in cps:
            cp.start()
        for cp in cps:
            cp.wait()

    return pl.pallas_call(
        body, name=name, in_specs=[HBM_SPEC] * n, out_specs=[HBM_SPEC] * n,
        out_shape=_chip_land_shapes(sums, kinds, widths),
        scratch_shapes=[pltpu.SemaphoreType.DMA((3 * n,)), pltpu.SemaphoreType.DMA((3 * n,))],
    )(*sums)


N_DIRECT = 7


def _direct_piece(kind, width, colblock, view_ref, q, h):
    if kind == "col":
        return view_ref.at[h, :, pl.ds(colblock(q) * width, width)]
    return view_ref.at[q, h]


def _direct_copies(kinds, widths, colblocks):
    def copies(srcs, lands):
        x, y, c, myq = _place()
        out = []
        for t in range(len(kinds)):
            def piece(q, h, t=t):
                return _direct_piece(kinds[t], widths[t], colblocks[t], srcs[t], q, h)
            for j, (cx, cy) in enumerate(_other_chips(x, y)):
                for h in (0, 1):
                    out.append((piece(2 * cx + cy, h), lands[t].at[2 * j + c], lands[t].at[2 * j + h],
                                10 * t + 3 * j + c + h, (cx, cy, h)))
            out.append((piece(myq, 1 - c), lands[t].at[6], lands[t].at[6], 10 * t + 9, (x, y, 1 - c)))
        return out
    return copies


def _chip_sum(kind, own_src, recv, block_idx, c, shard_shape, layer, into, name, direct=False):
    n_recv, rows, N = recv.shape
    tr = _pick(rows, (512, 352, 128))
    tn = _pick(N, (1408, 1024, 768, 512))
    ni, nj = rows // tr, N // tn

    def body(q_ref, s_ref, r_ref, *rest):
        o_ref = rest[-1]
        tot = s_ref[...].astype(F32)
        for k in range(n_recv):
            tot = tot + r_ref[k].astype(F32)
        o_ref[...] = tot

    if direct and kind == "col":
        own = pl.BlockSpec((None, tr, tn), lambda i, j, q_ref: (q_ref[1], i, q_ref[0] * nj + j))
    elif direct:
        own = pl.BlockSpec((None, None, tr, tn), lambda i, j, q_ref: (q_ref[0], q_ref[1], i, j))
    elif kind == "col":
        own = pl.BlockSpec((tr, tn), lambda i, j, q_ref: (i, q_ref[0] * nj + j))
    else:
        own = pl.BlockSpec((None, tr, tn), lambda i, j, q_ref: (q_ref[0], i, j))
    if len(shard_shape) == 3:
        lead = 0 if layer is None else layer
        out_spec = pl.BlockSpec((None, tr, tn), lambda i, j, q_ref: (lead, q_ref[1] * ni + i, j))
    else:
        out_spec = pl.BlockSpec((tr, tn), lambda i, j, q_ref: (q_ref[1] * ni + i, j))
    in_specs = [own, pl.BlockSpec((n_recv, tr, tn), lambda i, j, q_ref: (0, i, j))]
    s = own_src
    args = [jnp.stack([block_idx, c]).astype(jnp.int32), s, recv]
    aliases = {}
    if into is not None:
        in_specs.append(HBM_SPEC)
        args.append(into)
        aliases = {3: 0}
    return pl.pallas_call(
        body, name=name,
        grid_spec=pltpu.PrefetchScalarGridSpec(num_scalar_prefetch=1, grid=(ni, nj), in_specs=in_specs, out_specs=out_spec),
        out_shape=jax.ShapeDtypeStruct(shard_shape, F32), input_output_aliases=aliases,
        compiler_params=_cparams(("parallel", "parallel")),
    )(*args)


def _half_window(ref, h):
    rows = ref.shape[-2] // 2
    if ref.ndim == 3:
        return ref.at[:, pl.ds(h * rows, rows)]
    return ref.at[pl.ds(h * rows, rows)]


def _share_halves(grads, name):
    n = len(grads)

    def body(*refs):
        outs = refs[n:2 * n]
        send_sems, recv_sems = refs[2 * n:]
        x, y, c, _ = _place()
        cps = []
        for t in range(n):
            cp = pltpu.make_async_remote_copy(src_ref=_half_window(outs[t], c), dst_ref=_half_window(outs[t], c),
                                              send_sem=send_sems.at[t], recv_sem=recv_sems.at[t],
                                              device_id=(x, y, 1 - c), device_id_type=MESH)
            cp.start()
            cps.append(cp)
        for t in range(n):
            cps[t].wait_send()
            pltpu.make_async_remote_copy(src_ref=_half_window(outs[t], c), dst_ref=_half_window(outs[t], 1 - c),
                                         send_sem=send_sems.at[t], recv_sem=recv_sems.at[t],
                                         device_id=(x, y, 1 - c), device_id_type=MESH).wait_recv()

    return pl.pallas_call(
        body, name=name, in_specs=[HBM_SPEC] * n, out_specs=[HBM_SPEC] * n,
        out_shape=[jax.ShapeDtypeStruct(g.shape, F32) for g in grads],
        input_output_aliases={t: t for t in range(n)},
        scratch_shapes=[pltpu.SemaphoreType.DMA((n,)), pltpu.SemaphoreType.DMA((n,))],
    )(*grads)


def _adamw(w, g, m, v, name):
    R, W = w.shape
    tr = _pick(R, (512, 352, 256, 32))

    def body(w_ref, g_ref, m_ref, v_ref, d_ref, nm_ref, nv_ref, go_ref):
        gv = g_ref[...]
        go_ref[...] = gv
        nm = ADAM_B1 * m_ref[...] + (1.0 - ADAM_B1) * gv
        nv = ADAM_B2 * v_ref[...] + (1.0 - ADAM_B2) * (gv * gv)
        m_hat = nm / (1.0 - ADAM_B1 ** ADAM_STEP)
        v_hat = nv / (1.0 - ADAM_B2 ** ADAM_STEP)
        d_ref[...] = -ADAM_LR * (m_hat / (jnp.sqrt(v_hat) + ADAM_EPS) + ADAM_WD * w_ref[...])
        nm_ref[...] = nm
        nv_ref[...] = nv

    blk = pl.BlockSpec((tr, W), lambda i: (i, 0))
    shp = jax.ShapeDtypeStruct((R, W), F32)
    return pl.pallas_call(
        body, name=name, grid=(R // tr,), in_specs=[blk] * 4, out_specs=[blk] * 4, out_shape=[shp] * 4,
        compiler_params=_cparams(("parallel",)),
    )(w, g, m, v)


SMALL_ROWS = 32


def _pack_small(ln_g, ln_b, sinks):
    rows = jnp.concatenate([ln_g.reshape(-1, 128), ln_b.reshape(-1, 128),
                            jnp.pad(sinks.reshape(1, -1), ((0, 0), (0, 128 - sinks.size)))], axis=0)
    return jnp.pad(rows, ((0, SMALL_ROWS - rows.shape[0]), (0, 0)))


def _unpack_small(s, ln_shape, sink_shape):
    n = ln_shape[0] * ln_shape[1] * ln_shape[2] // 128
    return s[:n].reshape(ln_shape), s[n:2 * n].reshape(ln_shape), s[2 * n, :sink_shape[1]].reshape(sink_shape)


def _ffn_fwd(xin, w_in, w_out, gain, bias, tag):
    u, h = _ffn_in(xin, w_in, "ffn_in_" + tag)
    y, yb, z = _mm_ln(h, w_out, xin, gain, bias, 0.5, "ffn_out_ln_" + tag)
    return y, yb, dict(u=u, h=h, z=z, xin=xin)


def _ffn_bwd(dz, dzc, saved, w_in, w_out, xin_b, tag, dw_dtype=F32, ln=None):
    du = _ffn_bwd_h(dzc, w_out, saved["u"], "ffn_bwd_h_" + tag)
    d_w_out = _mm_tn(saved["h"], dzc, "ffn_dwout_" + tag, out_dtype=dw_dtype)
    d_w_in = _mm_tn(xin_b, du, "ffn_dwin_" + tag, out_dtype=dw_dtype)
    dx = _mm_nt(du, w_in, "ffn_dx_" + tag, add=dz, add_scale=ALPHA, ln=ln)
    return dx, d_w_in, d_w_out


def kernel(x, ffn1_w_in, ffn1_w_out, ffn2_w_in, ffn2_w_out, ln_g, ln_b, a_w_qkv, a_w_o, kv_w, b_w_q, b_sinks, b_w_o, loss_target, m_ffn1_w_in, m_ffn1_w_out, m_ffn2_w_in, m_ffn2_w_out, m_ln_g, m_ln_b, m_a_w_qkv, m_a_w_o, m_kv_w, m_b_w_q, m_b_sinks, m_b_w_o, v_ffn1_w_in, v_ffn1_w_out, v_ffn2_w_in, v_ffn2_w_out, v_ln_g, v_ln_b, v_a_w_qkv, v_a_w_o, v_kv_w, v_b_w_q, v_b_sinks, v_b_w_o):
    ws = dict(ffn1_w_in=ffn1_w_in, ffn1_w_out=ffn1_w_out, ffn2_w_in=ffn2_w_in, ffn2_w_out=ffn2_w_out, a_w_qkv=a_w_qkv,
              a_w_o=a_w_o, kv_w=kv_w, b_w_q=b_w_q, b_w_o=b_w_o)
    ms = dict(ffn1_w_in=m_ffn1_w_in, ffn1_w_out=m_ffn1_w_out, ffn2_w_in=m_ffn2_w_in, ffn2_w_out=m_ffn2_w_out,
              a_w_qkv=m_a_w_qkv, a_w_o=m_a_w_o, kv_w=m_kv_w, b_w_q=m_b_w_q, b_w_o=m_b_w_o)
    vs = dict(ffn1_w_in=v_ffn1_w_in, ffn1_w_out=v_ffn1_w_out, ffn2_w_in=v_ffn2_w_in, ffn2_w_out=v_ffn2_w_out,
              a_w_qkv=v_a_w_qkv, a_w_o=v_a_w_o, kv_w=v_kv_w, b_w_q=v_b_w_q, b_w_o=v_b_w_o)
    _, _, c_idx, myq = _place()
    xs = x[0]
    target = loss_target[0]

    shards = {(n, l): ws[n].astype(BF16) for n, l in LAYER0_ITEMS + LAYER1_ITEMS}

    def as_weights(items, arrays):
        return {n: (a.reshape(D_MODEL, a.shape[-1]) if a.ndim == 4 else a) for (n, _), a in zip(items, arrays)}

    full0, small = _all_gather(LAYER0_ITEMS, shards, _pack_small(ln_g, ln_b, b_sinks))
    gather_state, token = _gather_start(LAYER1_ITEMS, shards, small)

    def layer1_weights(after):
        return as_weights(LAYER1_ITEMS, _gather_wait(LAYER1_ITEMS, gather_state, after))

    n_ln = ln_g.size // 128
    lg = jnp.concatenate([small[q, :n_ln].reshape(DEPTH, 3, 1, -1) for q in range(N_CHIPS)], axis=-1)
    lb = jnp.concatenate([small[q, n_ln:2 * n_ln].reshape(DEPTH, 3, 1, -1) for q in range(N_CHIPS)], axis=-1)
    lg = lg + token[0, 0]
    reducer = _GradReducer(c_idx, myq, {n: ws[n].shape for n in BIG})
    sq, grad_x, _, gg, gb, dsink_part = _local_step(xs, target, as_weights(LAYER0_ITEMS, full0), layer1_weights,
                                                    lg, lb, b_sinks.reshape(N_HEADS), reducer.begin)

    loss_row = jnp.pad(jnp.sum(sq).reshape(1, 1), ((0, 0), (0, 127)))
    dsinks = jnp.pad(dsink_part[:, 0, :].reshape(N_SLABS, 2, HEAD_DIM)[:, :, 0].reshape(1, N_HEADS), ((0, 0), (0, 128 - N_HEADS)))
    gg_full = jnp.stack([jnp.stack([jnp.sum(gg[i][j], axis=0) for j in range(3)]) for i in range(DEPTH)])
    gb_full = jnp.stack([jnp.stack([jnp.sum(gb[i][j], axis=0) for j in range(3)]) for i in range(DEPTH)])
    small_in = jnp.concatenate([loss_row, dsinks, gg_full.reshape(-1, 128), gb_full.reshape(-1, 128)], axis=0)
    small_in = jnp.pad(small_in, ((0, (-small_in.shape[0]) % 8), (0, 0)))
    small_sum = _small_all_reduce(small_in)
    loss = small_sum[0, 0] * (0.5 / D_MODEL)
    grad_sinks = small_sum[1, :N_HEADS].reshape(b_sinks.shape)
    n_full = DEPTH * 3 * D_MODEL // 128
    cols = D_MODEL // N_CHIPS
    grad_ln_g = lax.dynamic_slice_in_dim(small_sum[2:2 + n_full].reshape(DEPTH, 3, D_MODEL), myq * cols, cols, axis=2)
    grad_ln_b = lax.dynamic_slice_in_dim(small_sum[2 + n_full:2 + 2 * n_full].reshape(DEPTH, 3, D_MODEL), myq * cols, cols, axis=2)
    return _update(reducer, grad_x, loss, grad_ln_g, grad_ln_b, grad_sinks, ws, ms, vs,
                   (ln_g, ln_b, b_sinks), (m_ln_g, m_ln_b, m_b_sinks), (v_ln_g, v_ln_b, v_b_sinks))


def _local_step(xs, target, W, layer1_weights, lg, lb, sinks, grads_ready=None):
    if grads_ready is None:
        grads_ready = lambda tag, grads, overlap: 0.0
    S = xs.shape[0]
    slopes = jnp.asarray(_alibi_slopes(N_HEADS))
    in1, out1, in2, out2 = [W["ffn1_w_in"]], [W["ffn1_w_out"]], [W["ffn2_w_in"]], [W["ffn2_w_out"]]

    y1, y1b, s1 = _ffn_fwd(xs, in1[0], out1[0], lg[0, 0], lb[0, 0], "a1")
    qkv_a = _mm_nn(y1b, W["a_w_qkv"], F32, "qkv_a", split=True)
    mix_a, o_a, lse_a = _attn_fwd(qkv_a, slopes, None, PATTERNS_A, "attn_a_fwd")
    y2, y2b, z2 = _mm_ln(mix_a, W["a_w_o"], y1, lg[0, 1], lb[0, 1], 1.0, "attn_a_out_ln")
    y3, y3b, s3 = _ffn_fwd(y2, in2[0], out2[0], lg[0, 2], lb[0, 2], "a2")
    kv_w_rep = jnp.broadcast_to(W["kv_w"].reshape(D_MODEL, 2, N_KV_B, 1, HEAD_DIM),
                                (D_MODEL, 2, N_KV_B, GROUP_B, HEAD_DIM)).reshape(D_MODEL, 2 * D_MODEL)
    kv_rep = _mm_nn(y3b, kv_w_rep, F32, "kv_proj", split=(1, 2))
    W = dict(W, **layer1_weights(kv_rep))
    in1, out1, in2, out2 = (in1 + [W["ffn1_w_in"]], out1 + [W["ffn1_w_out"]], in2 + [W["ffn2_w_in"]],
                            out2 + [W["ffn2_w_out"]])
    y4, y4b, s4 = _ffn_fwd(y3, in1[1], out1[1], lg[1, 0], lb[1, 0], "b1")
    qkv_b = _mm_nn(y4b, W["b_w_q"], F32, "q_b", split=(0, 1), into=kv_rep)
    mix_b, o_b, lse_b = _attn_fwd(qkv_b, slopes, sinks, PATTERNS_B, "attn_b_fwd")
    y5, y5b, z5 = _mm_ln(mix_b, W["b_w_o"], y4, lg[1, 1], lb[1, 1], 1.0, "attn_b_out_ln")
    y6, _, s6 = _ffn_fwd(y5, in2[1], out2[1], lg[1, 2], lb[1, 2], "b2")

    gr = {n: None for n in BIG}
    gg = [[None] * 3 for _ in range(DEPTH)]
    gb = [[None] * 3 for _ in range(DEPTH)]
    dz6, dz6c, gg[1][2], gb[1][2], sq = _loss_ln_bwd(y6, target, s6["z"], lg[1, 2], 0.5, "loss_ln_bwd")

    (dz5, dz5b, gg[1][1], gb[1][1]), d_in2_b, d_out2_b = _ffn_bwd(dz6, dz6c, s6, in2[1], out2[1], y5b, "b2", BF16,
                                                                  ln=(z5, lg[1, 1], 1.0))
    gr["b_w_o"] = _mm_tn(mix_b, dz5b, "d_b_w_o", out_dtype=BF16)
    dmix_b = _mm_nt(dz5b, W["b_w_o"], "d_mix_b")
    dqkv_b, dsink_part = _attn_bwd(qkv_b, dmix_b, o_b, lse_b, slopes, sinks, PATTERNS_B, "attn_b_bwd")
    dq_b = (dqkv_b, 0)
    gr["b_w_q"] = _mm_tn(y4b, dq_b, "d_b_w_q", out_dtype=BF16)
    dz4, dz4c, gg[1][0], gb[1][0] = _mm_nt(dq_b, W["b_w_q"], "d_y4", add=dz5, add_scale=ALPHA, ln=(s4["z"], lg[1, 0], 0.5))
    dy3, d_in1_b, d_out1_b = _ffn_bwd(dz4, dz4c, s4, in1[1], out1[1], y3b, "b1", BF16)
    gr["kv_w"] = _d_kv_w(y3b, dqkv_b, "d_kv_w")
    tok = grads_ready("l1", {("ffn2_w_in", 1): d_in2_b, ("ffn2_w_out", 1): d_out2_b, ("b_w_o", None): gr["b_w_o"],
                             ("b_w_q", None): gr["b_w_q"], ("ffn1_w_in", 1): d_in1_b, ("ffn1_w_out", 1): d_out1_b,
                             ("kv_w", None): gr["kv_w"]}, True)
    lg0 = lg[0] + tok
    dz3, dz3c, gg[0][2], gb[0][2] = _mm_nt(dqkv_b, kv_w_rep, "d_y3_kv", add=dy3, add_scale=1.0, split=(1, 2),
                                           ln=(s3["z"], lg0[2], 0.5))

    (dz2, dz2b, gg[0][1], gb[0][1]), d_in2_a, d_out2_a = _ffn_bwd(dz3, dz3c, s3, in2[0], out2[0], y2b, "a2", BF16,
                                                                  ln=(z2, lg0[1], 1.0))
    tok = grads_ready("a2", {("ffn2_w_in", 0): d_in2_a, ("ffn2_w_out", 0): d_out2_a}, True)
    lg0 = lg0 + tok
    gr["a_w_o"] = _mm_tn(mix_a, dz2b, "d_a_w_o", out_dtype=BF16)
    dmix_a = _mm_nt(dz2b, W["a_w_o"], "d_mix_a")
    dqkv_a, _ = _attn_bwd(qkv_a, dmix_a, o_a, lse_a, slopes, None, PATTERNS_A, "attn_a_bwd")
    gr["a_w_qkv"] = _mm_tn(y1b, dqkv_a, "d_a_w_qkv", split=True, out_dtype=BF16)
    tok = grads_ready("mix", {("a_w_o", None): gr["a_w_o"], ("a_w_qkv", None): gr["a_w_qkv"]}, True)
    lg0 = lg0 + tok
    dz1, dz1c, gg[0][0], gb[0][0] = _mm_nt(dqkv_a, W["a_w_qkv"], "d_y1", add=dz2, add_scale=ALPHA, split=True,
                                           ln=(s1["z"], lg0[0], 0.5))
    grad_x, d_in1_a, d_out1_a = _ffn_bwd(dz1, dz1c, s1, in1[0], out1[0], xs, "a1", BF16)
    grads_ready("a1", {("ffn1_w_in", 0): d_in1_a, ("ffn1_w_out", 0): d_out1_a}, True)
    gr["ffn1_w_in"] = [d_in1_a, d_in1_b]
    gr["ffn1_w_out"] = [d_out1_a, d_out1_b]
    gr["ffn2_w_in"] = [d_in2_a, d_in2_b]
    gr["ffn2_w_out"] = [d_out2_a, d_out2_b]
    return sq, grad_x, gr, gg, gb, dsink_part


def _grad_item(name, layer, g):
    if name.endswith("w_in"):
        return (g, "col", HALF_FF, _slot, name, layer)
    if name.endswith("w_out"):
        return (g, "row", D_MODEL, None, name, layer)
    if name == "a_w_qkv":
        return (g, "col", QKV_SHARD, lambda q: q, name, None)
    return (g, "row", g.shape[1], None, name, None)


class _GradReducer:
    def __init__(self, c_idx, myq, shard_shapes):
        self.c_idx, self.myq, self.shard_shapes = c_idx, myq, shard_shapes
        self.groups = []

    def begin(self, tag, grads, overlap):
        items = [_grad_item(n, l, g) for (n, l), g in grads.items()]
        kinds, widths, colblocks = [it[1] for it in items], [it[2] for it in items], [it[3] for it in items]
        views = [_grad_view(k, it[0]) for k, it in zip(kinds, items)]
        if overlap:
            lands = [jax.ShapeDtypeStruct((N_DIRECT,) + _piece_shape(k, w, _half_shape(k, v.shape)), BF16)
                     for k, w, v in zip(kinds, widths, views)]
            state, token = _split_start("grad_direct_start_" + tag, _direct_copies(kinds, widths, colblocks), 10 * len(items),
                                        views, lands, jnp.zeros((8, 128), F32))
            self.groups.append((tag, items, None, state, token))
            return token[0, 0]
        from_sibling = _pair_exchange(views, kinds, "grad_pair_exchange_" + tag)
        sums = [_pair_sum(k, v, r, self.c_idx, "pair_sum_%s_%d" % (tag, t))
                for t, (k, v, r) in enumerate(zip(kinds, views, from_sibling))]
        self.groups.append((tag, items, sums, None, None))
        return 0.0

    def _sum_group(self, tag, items, sums, received, direct):
        for t, (it, s, r) in enumerate(zip(items, sums, received)):
            _, k, _, cb, name, layer = it
            own = cb(self.myq) if k == "col" else self.myq
            self.half_done[name] = _chip_sum(k, s, r, own, self.c_idx, self.shard_shapes[name], layer,
                                             self.half_done.get(name), "chip_sum_%s_%d" % (tag, t), direct=direct)

    def finish_first(self, after):
        self.half_done, self.late, early = {}, [], []
        started = [after]
        for g, (tag, items, sums, state, token) in enumerate(self.groups):
            kinds, widths, colblocks = [it[1] for it in items], [it[2] for it in items], [it[3] for it in items]
            if state is None:
                copies = _chip_copies(kinds, widths, colblocks)
                state, token = _split_start("grad_chip_start_" + tag, copies, 3 * len(items), sums,
                                            _chip_land_shapes(sums, kinds, widths), sums[-1])
                self.late.append((tag, items, copies, state, False))
                started.append(token)
            elif g == len(self.groups) - 1:
                self.late.append((tag, items, _direct_copies(kinds, widths, colblocks), state, True))
                started.append(token)
            else:
                early.append((tag, items, _direct_copies(kinds, widths, colblocks), state))
        for tag, items, copies, state in early:
            views, received = _split_wait("grad_direct_wait_" + tag, copies, state, started)
            self._sum_group(tag, items, views, received, True)
        late_names = {it[4] for _, items, _, _, _ in self.late for it in items}
        names = [n for n in BIG if n not in late_names]
        return dict(zip(names, _share_halves([self.half_done[n] for n in names], "grad_share_halves_first")))

    def finish_rest(self, after):
        names = []
        for tag, items, copies, state, direct in self.late:
            sums, received = _split_wait("grad_late_wait_" + tag, copies, state, after)
            self._sum_group(tag, items, sums, received, direct)
            names += [it[4] for it in items if it[4] not in names]
        return dict(zip(names, _share_halves([self.half_done[n] for n in names], "grad_share_halves_rest")))


def _update(reducer, grad_x, loss, grad_ln_g, grad_ln_b, grad_sinks, ws, ms, vs, small_w, small_m, small_v):
    ln_g, ln_b, b_sinks = small_w
    m_ln_g, m_ln_b, m_b_sinks = small_m
    v_ln_g, v_ln_b, v_b_sinks = small_v

    grads, deltas, new_m, new_v = {}, {}, {}, {}

    def update(some):
        done = []
        for name in some:
            shp = ws[name].shape
            flat = lambda a: a.reshape(-1, shp[-1])
            d, nm, nv, g = _adamw(flat(ws[name]), flat(some[name]), flat(ms[name]), flat(vs[name]), "adamw_" + name)
            grads[name], deltas[name], new_m[name], new_v[name] = g.reshape(shp), d.reshape(shp), nm.reshape(shp), nv.reshape(shp)
            done.append(d)
        return done

    rest = reducer.finish_rest(update(reducer.finish_first(grad_x)))
    update(rest)
    delta_s, nm_s, nv_s, _ = _adamw(_pack_small(ln_g, ln_b, b_sinks), _pack_small(grad_ln_g, grad_ln_b, grad_sinks),
                                    _pack_small(m_ln_g, m_ln_b, m_b_sinks), _pack_small(v_ln_g, v_ln_b, v_b_sinks), "adamw_small")
    for d, blob in ((grads, None), (deltas, delta_s), (new_m, nm_s), (new_v, nv_s)):
        if blob is None:
            d["ln_g"], d["ln_b"], d["b_sinks"] = grad_ln_g, grad_ln_b, grad_sinks
        else:
            d["ln_g"], d["ln_b"], d["b_sinks"] = _unpack_small(blob, ln_g.shape, b_sinks.shape)

    order = ("ffn1_w_in", "ffn1_w_out", "ffn2_w_in", "ffn2_w_out", "ln_g", "ln_b", "a_w_qkv", "a_w_o", "kv_w", "b_w_q",
             "b_sinks", "b_w_o")
    outs = [loss, grad_x[None]]
    for d in (grads, deltas, new_m, new_v):
        outs += [d[n] for n in order]
    return tuple(outs)
```

```python
import numpy as np
import jax
import jax.numpy as jnp
from jax import lax
from jax.experimental import pallas as pl
from jax.experimental.pallas import tpu as pltpu

F32 = jnp.float32
BF16 = jnp.bfloat16

D_MODEL = 1024
D_FF = 2816
HALF_FF = D_FF // 2
HEAD_DIM = 64
N_HEADS = 16
N_KV_B = 4
GROUP_B = N_HEADS // N_KV_B
DEPTH = 2
ALPHA = (2.0 * DEPTH) ** 0.25
LN_EPS = 1e-5
BLOCK = 128
SLAB = 128
N_SLABS = D_MODEL // SLAB
PATTERNS_A = ((1, 128, 1.0), (4, 128, 4.0), (16, 128, 16.0))
PATTERNS_B = ((1, 127, 1.0),)
NEG = -1e30

ADAM_LR = 0.001
ADAM_B1 = 0.9
ADAM_B2 = 0.999
ADAM_EPS = 1e-08
ADAM_WD = 0.01
ADAM_STEP = 10

N_CHIPS = 4
VMEM_LIMIT = 56 * 1024 * 1024
WHOLE_WEIGHT_BYTES = 12 * 1024 * 1024
MESH = pl.DeviceIdType.MESH


def _alibi_slopes(n):
    return np.array([2.0 ** (-8.0 * (h + 1) / n) for h in range(n)], dtype=np.float32)


def _cparams(sem=None, vmem=VMEM_LIMIT):
    return pltpu.CompilerParams(dimension_semantics=sem, vmem_limit_bytes=vmem)


_DIMS = {"nn": ((1,), (0,)), "nt": ((1,), (1,)), "tn": ((0,), (0,))}


def _unlead(x):
    if isinstance(x, tuple):
        return x[0], x[1], x[0].shape[1:]
    return x, None, x.shape


def _bspec(block, imap, lead=None, **kw):
    if lead is None:
        return pl.BlockSpec(block, imap, **kw)
    return pl.BlockSpec((None,) + tuple(block), lambda *g: (lead,) + tuple(imap(*g)), **kw)


def _ln_bwd_math(zv, dyv, gain):
    rows = zv.shape[0]
    mu = jnp.mean(zv, axis=-1, keepdims=True)
    zc = zv - mu
    var = jnp.mean(zc * zc, axis=-1, keepdims=True)
    rstd = lax.rsqrt(var + LN_EPS)
    xhat = zc * rstd
    dyg = dyv * gain
    m1 = jnp.mean(dyg, axis=-1, keepdims=True)
    m2 = jnp.mean(dyg * xhat, axis=-1, keepdims=True)
    dz = rstd * (dyg - m1 - xhat * m2)
    pg = jnp.sum((dyv * xhat).reshape(rows // 8, 8, D_MODEL), axis=0)
    pb = jnp.sum(dyv.reshape(rows // 8, 8, D_MODEL), axis=0)
    return dz, pg, pb


def _matmul(a, b, mode, out_dtype, tm, tn, tk, name, add=None, add_scale=1.0, split=False, into=None, ln=None):
    out_spec = pl.BlockSpec((tm, tn), lambda i, j, k: (i, j))
    base, count = (0, 3) if split is True else (split or (0, 0))
    if mode == "nn":
        a, al, (M, K) = _unlead(a)
        b, bl, (K2, N) = _unlead(b)
        a_spec = _bspec((tm, tk), lambda i, j, k: (i, k), al)
        b_spec = _bspec((tk, tn), lambda i, j, k: (k, j), bl)
        out_struct = jax.ShapeDtypeStruct((M, N), out_dtype)
        if split:
            assert tn == D_MODEL and N == count * tn
            out_spec = pl.BlockSpec((None, tm, tn), lambda i, j, k: (j + base, i, 0))
            out_struct = jax.ShapeDtypeStruct((3, M, tn), out_dtype)
    elif mode == "nt":
        b, bl, (N, K2) = _unlead(b)
        if split:
            assert tk == D_MODEL
            M, K = a.shape[1], count * a.shape[2]
            a_spec = pl.BlockSpec((None, tm, tk), lambda i, j, k: (k + base, i, 0))
        else:
            a, al, (M, K) = _unlead(a)
            a_spec = _bspec((tm, tk), lambda i, j, k: (i, k), al)
        whole_b = {"pipeline_mode": pl.Buffered(1)} if (tn, tk) == (N, K2) else {}
        b_spec = _bspec((tn, tk), lambda i, j, k: (j, k), bl, **whole_b)
        out_struct = jax.ShapeDtypeStruct((M, N), out_dtype)
    else:
        a, al, (K, M) = _unlead(a)
        if split:
            assert tn == D_MODEL
            K2, N = b.shape[1], count * b.shape[2]
            b_spec = pl.BlockSpec((None, tk, tn), lambda i, j, k: (j + base, k, 0))
        else:
            b, bl, (K2, N) = _unlead(b)
            b_spec = _bspec((tk, tn), lambda i, j, k: (k, j), bl)
        a_spec = _bspec((tk, tm), lambda i, j, k: (k, i), al)
        out_struct = jax.ShapeDtypeStruct((M, N), out_dtype)
    assert K == K2 and M % tm == 0 and N % tn == 0 and K % tk == 0, (a.shape, b.shape, mode, tm, tn, tk)
    nk = K // tk
    dims = (_DIMS[mode], ((), ()))
    has_add = add is not None

    narrow = out_dtype != F32
    assert not (narrow and has_add)
    if ln is not None:
        assert has_add and mode == "nt" and tn == N == D_MODEL

    def body(*refs):
        if into is not None:
            refs = refs[:2] + refs[3:]
        if ln is not None:
            a_ref, b_ref, add_ref, z_ref, g_ref, o_ref, dzc_ref, gg_ref, gb_ref = refs
            acc_ref = o_ref
        elif has_add:
            a_ref, b_ref, add_ref, o_ref = refs
            acc_ref = o_ref
        elif narrow:
            a_ref, b_ref, o_ref, acc_ref = refs
        else:
            a_ref, b_ref, o_ref = refs
            acc_ref = o_ref
        k = pl.program_id(2)
        part = lax.dot_general(a_ref[...].astype(BF16), b_ref[...].astype(BF16), dims, preferred_element_type=F32)
        if has_add:
            @pl.when(k == 0)
            def _():
                acc_ref[...] = part + add_scale * add_ref[...]
        else:
            @pl.when(k == 0)
            def _():
                acc_ref[...] = part

        @pl.when(k > 0)
        def _():
            acc_ref[...] += part

        if narrow:
            @pl.when(k == nk - 1)
            def _():
                o_ref[...] = acc_ref[...].astype(out_dtype)

        if ln is not None:
            @pl.when(k == nk - 1)
            def _():
                dz, pg, pb = _ln_bwd_math(z_ref[...], o_ref[...], g_ref[...])
                o_ref[...] = dz
                dzc_ref[...] = (ln[2] * dz).astype(BF16)
                first = pl.program_id(0) == 0

                @pl.when(first)
                def _():
                    gg_ref[...] = pg
                    gb_ref[...] = pb

                @pl.when(jnp.logical_not(first))
                def _():
                    gg_ref[...] += pg
                    gb_ref[...] += pb

    in_specs = [a_spec, b_spec]
    args = [a, b]
    aliases = {}
    if into is not None:
        assert mode == "nn" and split and not has_add
        in_specs.append(pl.BlockSpec(memory_space=pl.ANY))
        args.append(into)
        aliases = {2: 0}
    if has_add:
        in_specs.append(pl.BlockSpec((tm, tn), lambda i, j, k: (i, j)))
        args.append(add)
    sem = ("parallel", "parallel", "arbitrary")
    if ln is not None:
        part8 = pl.BlockSpec((8, N), lambda i, j, k: (0, 0))
        in_specs += [pl.BlockSpec((tm, tn), lambda i, j, k: (i, j)), pl.BlockSpec((1, N), lambda i, j, k: (0, 0))]
        args += [ln[0], ln[1]]
        out_spec = [out_spec, pl.BlockSpec((tm, tn), lambda i, j, k: (i, j)), part8, part8]
        out_struct = [out_struct, jax.ShapeDtypeStruct((M, N), BF16), jax.ShapeDtypeStruct((8, N), F32),
                      jax.ShapeDtypeStruct((8, N), F32)]
        sem = ("arbitrary", "arbitrary", "arbitrary")
    return pl.pallas_call(
        body, name=name, grid=(M // tm, N // tn, nk),
        in_specs=in_specs, out_specs=out_spec, out_shape=out_struct, input_output_aliases=aliases,
        scratch_shapes=[pltpu.VMEM((tm, tn), F32)] if narrow else [],
        compiler_params=_cparams(sem),
    )(*args)


def _pick(n, cands):
    for c in cands:
        if n % c == 0:
            return c
    raise ValueError((n, cands))


def _mm_nn(a, b, out_dtype, name, split=False, into=None):
    M, K = _unlead(a)[2]
    N = _unlead(b)[2][1]
    return _matmul(a, b, "nn", out_dtype, _pick(M, (1024, 512, 256)), _pick(N, (1024, 512)), _pick(K, (1024, 512)), name,
                   split=split, into=into)


def _mm_nt(a, b, name, add=None, add_scale=1.0, split=False, ln=None):
    M, K = (a.shape[1], D_MODEL) if split else _unlead(a)[2]
    N = _unlead(b)[2][0]
    tn = _pick(N, (1024, 512))
    if not split and tn == N and N * K * 2 <= WHOLE_WEIGHT_BYTES:
        tm, tk = _pick(M, (512, 256)), K
    else:
        tms = (512, 256) if ln is not None else (1024, 512, 256)
        tm, tk = _pick(M, tms), _pick(K, (2816, 1024, 512))
    return _matmul(a, b, "nt", F32, tm, tn, tk, name, add=add, add_scale=add_scale, split=split, ln=ln)


def _mm_tn(a, b, name, split=False, out_dtype=F32):
    K, M = _unlead(a)[2]
    N = D_MODEL if split else _unlead(b)[2][1]
    return _matmul(a, b, "tn", out_dtype, _pick(M, (1024, 1408, 512)), _pick(N, (1408, 1024, 512)),
                   _pick(K, (2048, 1024, 512, 256)), name, split=split)


def _d_kv_w(y, dqkv, name):
    S = y.shape[0]
    tk = _pick(S, (1024, 512))
    nk = S // tk
    width = N_KV_B * HEAD_DIM
    r, c = np.arange(D_MODEL)[:, None], np.arange(width)[None, :]
    fold = jnp.asarray((r // (GROUP_B * HEAD_DIM) == c // HEAD_DIM) & (r % HEAD_DIM == c % HEAD_DIM), BF16)

    def body(y_ref, dk_ref, dv_ref, f_ref, o_ref, acc_ref):
        k = pl.program_id(0)
        summed = jnp.concatenate([jnp.dot(ref[...], f_ref[...], preferred_element_type=F32).astype(BF16)
                                  for ref in (dk_ref, dv_ref)], axis=1)
        part = lax.dot_general(summed, y_ref[...], (_DIMS["tn"], ((), ())), preferred_element_type=F32)

        @pl.when(k == 0)
        def _():
            acc_ref[...] = part

        @pl.when(k > 0)
        def _():
            acc_ref[...] += part

        @pl.when(k == nk - 1)
        def _():
            o_ref[...] = acc_ref[...].T.astype(BF16)

    return pl.pallas_call(
        body, name=name, grid=(nk,),
        in_specs=[pl.BlockSpec((tk, D_MODEL), lambda k: (k, 0)),
                  pl.BlockSpec((None, tk, D_MODEL), lambda k: (1, k, 0)),
                  pl.BlockSpec((None, tk, D_MODEL), lambda k: (2, k, 0)),
                  pl.BlockSpec((D_MODEL, width), lambda k: (0, 0))],
        out_specs=pl.BlockSpec((D_MODEL, 2 * width), lambda k: (0, 0)),
        out_shape=jax.ShapeDtypeStruct((D_MODEL, 2 * width), BF16),
        scratch_shapes=[pltpu.VMEM((2 * width, D_MODEL), F32)],
        compiler_params=_cparams(("arbitrary",)),
    )(y, dqkv, dqkv, fold)


def _ffn_in(x, w, name):
    S = x.shape[0]
    tm = _pick(S, (512, 256))
    w, wl, _ = _unlead(w)

    def body(x_ref, w_ref, t_ref, h_ref):
        acc = jnp.dot(x_ref[...].astype(BF16), w_ref[...], preferred_element_type=F32)
        g = acc[:, :HALF_FF]
        up = acc[:, HALF_FF:]
        sg = jax.nn.sigmoid(g)
        silu = g * sg
        t_ref[:, :HALF_FF] = (up * (sg * (1.0 + g * (1.0 - sg)))).astype(BF16)
        t_ref[:, HALF_FF:] = silu.astype(BF16)
        h_ref[...] = (silu * up).astype(BF16)

    return pl.pallas_call(
        body, name=name, grid=(2, S // tm),
        in_specs=[pl.BlockSpec((tm, D_MODEL), lambda j, i: (i, 0)),
                  _bspec((D_MODEL, D_FF), lambda j, i: (0, j), wl)],
        out_specs=[pl.BlockSpec((tm, D_FF), lambda j, i: (i, j)),
                   pl.BlockSpec((tm, HALF_FF), lambda j, i: (i, j))],
        out_shape=[jax.ShapeDtypeStruct((S, 2 * D_FF), BF16), jax.ShapeDtypeStruct((S, D_FF), BF16)],
        compiler_params=_cparams(("parallel", "parallel")),
    )(x, w)


def _ffn_bwd_h(dzc, w_out, u, name):
    S = dzc.shape[0]
    tm = _pick(S, (512, 256))
    w_out, wl, _ = _unlead(w_out)

    def body(dz_ref, w_ref, t_ref, du_ref):
        dh = lax.dot_general(dz_ref[...], w_ref[...], (((1,), (1,)), ((), ())), preferred_element_type=F32)
        du_ref[:, :HALF_FF] = (dh * t_ref[:, :HALF_FF].astype(F32)).astype(BF16)
        du_ref[:, HALF_FF:] = (dh * t_ref[:, HALF_FF:].astype(F32)).astype(BF16)

    return pl.pallas_call(
        body, name=name, grid=(2, S // tm),
        in_specs=[pl.BlockSpec((tm, D_MODEL), lambda j, i: (i, 0)),
                  _bspec((HALF_FF, D_MODEL), lambda j, i: (j, 0), wl),
                  pl.BlockSpec((tm, D_FF), lambda j, i: (i, j))],
        out_specs=pl.BlockSpec((tm, D_FF), lambda j, i: (i, j)),
        out_shape=jax.ShapeDtypeStruct((S, 2 * D_FF), BF16),
        compiler_params=_cparams(("parallel", "parallel")),
    )(dzc, w_out, u)


def _mm_ln(a, w, resid, gain, bias, c, name):
    S, K = a.shape
    tm = _pick(S, (512, 256))
    w, wl, _ = _unlead(w)

    def body(a_ref, w_ref, r_ref, g_ref, b_ref, y_ref, yb_ref, z_ref):
        z = ALPHA * r_ref[...] + c * jnp.dot(a_ref[...], w_ref[...], preferred_element_type=F32)
        mu = jnp.mean(z, axis=-1, keepdims=True)
        zc = z - mu
        var = jnp.mean(zc * zc, axis=-1, keepdims=True)
        y = zc * lax.rsqrt(var + LN_EPS) * g_ref[...] + b_ref[...]
        z_ref[...] = z
        y_ref[...] = y
        yb_ref[...] = y.astype(BF16)

    row = pl.BlockSpec((tm, D_MODEL), lambda i: (i, 0))
    vec = pl.BlockSpec((1, D_MODEL), lambda i: (0, 0))
    return pl.pallas_call(
        body, name=name, grid=(S // tm,),
        in_specs=[pl.BlockSpec((tm, K), lambda i: (i, 0)), _bspec((K, D_MODEL), lambda i: (0, 0), wl), row, vec, vec],
        out_specs=[row, row, row],
        out_shape=[jax.ShapeDtypeStruct((S, D_MODEL), F32), jax.ShapeDtypeStruct((S, D_MODEL), BF16),
                   jax.ShapeDtypeStruct((S, D_MODEL), F32)],
        compiler_params=_cparams(("parallel",)),
    )(a, w, resid, gain, bias)


def _loss_ln_bwd(y, t, z, gain, c, name):
    S = y.shape[0]
    tm = _pick(S, (512, 256))

    def body(y_ref, t_ref, z_ref, g_ref, dz_ref, dzc_ref, gg_ref, gb_ref, sq_ref):
        i = pl.program_id(0)
        e = y_ref[...] - t_ref[...]
        dz, pg, pb = _ln_bwd_math(z_ref[...], e * (1.0 / D_MODEL), g_ref[...])
        dz_ref[...] = dz
        dzc_ref[...] = (c * dz).astype(BF16)
        ps = jnp.sum((e * e).reshape(tm // 8, 8, D_MODEL), axis=0)

        @pl.when(i == 0)
        def _():
            gg_ref[...] = pg
            gb_ref[...] = pb
            sq_ref[...] = ps

        @pl.when(i > 0)
        def _():
            gg_ref[...] += pg
            gb_ref[...] += pb
            sq_ref[...] += ps

    row = pl.BlockSpec((tm, D_MODEL), lambda i: (i, 0))
    part = pl.BlockSpec((8, D_MODEL), lambda i: (0, 0))
    part_shape = jax.ShapeDtypeStruct((8, D_MODEL), F32)
    return pl.pallas_call(
        body, name=name, grid=(S // tm,),
        in_specs=[row, row, row, pl.BlockSpec((1, D_MODEL), lambda i: (0, 0))],
        out_specs=[row, row, part, part, part],
        out_shape=[jax.ShapeDtypeStruct((S, D_MODEL), F32), jax.ShapeDtypeStruct((S, D_MODEL), BF16),
                   part_shape, part_shape, part_shape],
        compiler_params=_cparams(("arbitrary",)),
    )(y, t, z, gain)


def _rows(start, d):
    if d == 1:
        return pl.ds(pl.multiple_of(start, BLOCK), BLOCK)
    return pl.ds(start, BLOCK, stride=d)


def _ld(ref, start, d):
    return ref[_rows(start, d), :]


def _ld3(ref, lead, start, d):
    return ref[lead, _rows(start, d), :]


def _st3(ref, lead, start, d, val):
    ref[lead, _rows(start, d), :] = val


def _acc3(ref, lead, start, d, val):
    ref[lead, _rows(start, d), :] = ref[lead, _rows(start, d), :] + val


def _band_consts(slope0, slope1, maxd, scale):
    row = lax.broadcasted_iota(jnp.int32, (2 * BLOCK, 2 * BLOCK), 0)
    kj = lax.broadcasted_iota(jnp.int32, (2 * BLOCK, 2 * BLOCK), 1)
    top = row < BLOCK
    dist = BLOCK + jnp.where(top, row, row - BLOCK) - kj
    slope = jnp.where(top, slope0, slope1)
    base = jnp.where((dist >= 0) & (dist <= maxd), -(slope * (dist.astype(F32) * scale)), NEG)
    return base, kj < BLOCK


def _stack_heads(x, lo):
    return jnp.concatenate([jnp.where(lo, x, 0.0), jnp.where(lo, 0.0, x)], axis=0)


def _unstack_heads(x2, lo):
    return jnp.where(lo, x2[:BLOCK], x2[BLOCK:])


def _scores(q2, k2, base, prev_keys, first):
    s = lax.dot_general(q2, k2, (((1,), (1,)), ((), ())), preferred_element_type=F32) * (HEAD_DIM ** -0.5) + base
    return jnp.where(jnp.logical_and(prev_keys, first), NEG, s)


def _softmax_weights(ls):
    mx = ls[0]
    for l in ls[1:]:
        mx = jnp.maximum(mx, l)
    es = [jnp.exp(l - mx) for l in ls]
    tot = es[0]
    for e in es[1:]:
        tot = tot + e
    inv = 1.0 / tot
    return [e * inv for e in es]


def _attn_fwd(qkv, slopes, sinks, patterns, name):
    S = qkv.shape[1]
    npat = len(patterns)
    has_sink = sinks is not None
    if not has_sink:
        sinks = jnp.zeros((N_HEADS,), F32)
    rows_c = 256

    def body(slopes_ref, sinks_ref, x_ref, mix_ref, o_ref, lse_ref, o_scr, lse_scr):
        p = pl.program_id(0)
        lo = lax.broadcasted_iota(jnp.int32, (BLOCK, SLAB), 1) < HEAD_DIM
        top1 = lax.broadcasted_iota(jnp.int32, (2 * BLOCK, 1), 0) < BLOCK
        sk2 = jnp.where(top1, sinks_ref[2 * p], sinks_ref[2 * p + 1])
        for pi, (d, maxd, scale) in enumerate(patterns):
            nb = S // d // BLOCK
            base, prev_keys = _band_consts(slopes_ref[2 * p], slopes_ref[2 * p + 1], maxd, scale)

            def blk(t, carry, pi=pi, d=d, nb=nb, base=base, prev_keys=prev_keys):
                r = t // nb
                n = t - r * nb
                start = r + (d * BLOCK) * n
                prev = jnp.where(n > 0, start - d * BLOCK, start)
                q2 = _stack_heads(_ld3(x_ref, 0, start, d), lo).astype(BF16)
                k2 = jnp.concatenate([_ld3(x_ref, 1, prev, d), _ld3(x_ref, 1, start, d)], axis=0).astype(BF16)
                v2 = jnp.concatenate([_ld3(x_ref, 2, prev, d), _ld3(x_ref, 2, start, d)], axis=0).astype(BF16)
                s = _scores(q2, k2, base, prev_keys, n == 0)
                m = jnp.max(s, axis=-1, keepdims=True)
                if has_sink:
                    m = jnp.maximum(m, sk2)
                e = jnp.exp(s - m)
                den = jnp.sum(e, axis=-1, keepdims=True)
                if has_sink:
                    den = den + jnp.exp(sk2 - m)
                o2 = jnp.dot((e / den).astype(BF16), v2, preferred_element_type=F32)
                _st3(o_scr, pi, start, d, _unstack_heads(o2, lo))
                _st3(lse_scr, pi, start, d, _unstack_heads(m + jnp.log(den), lo))
                return carry

            lax.fori_loop(0, d * nb, blk, 0, unroll=8)

        lane_c = lax.broadcasted_iota(jnp.int32, (rows_c, SLAB), 1)

        def comb(ci, carry):
            rows = pl.ds(pl.multiple_of(ci * rows_c, rows_c), rows_c)
            ls = [lse_scr[i, rows, :] for i in range(npat)]
            packed = jnp.zeros((rows_c, SLAB), F32)
            for i in range(npat):
                o_ref[i, rows, :] = o_scr[i, rows, :].astype(BF16)
                packed = jnp.where(lane_c % HEAD_DIM == i, ls[i], packed)
            lse_ref[rows, :] = packed
            if npat == 1:
                mix_ref[rows, :] = o_scr[0, rows, :].astype(BF16)
            else:
                ws = _softmax_weights(ls)
                acc = ws[0] * o_scr[0, rows, :]
                for i in range(1, npat):
                    acc = acc + ws[i] * o_scr[i, rows, :]
                mix_ref[rows, :] = acc.astype(BF16)
            return carry

        lax.fori_loop(0, S // rows_c, comb, 0, unroll=2)

    smem = pl.BlockSpec(memory_space=pltpu.SMEM)
    return pl.pallas_call(
        body, name=name, grid=(N_SLABS,),
        in_specs=[smem, smem, pl.BlockSpec((3, S, SLAB), lambda p: (0, 0, p))],
        out_specs=[pl.BlockSpec((S, SLAB), lambda p: (0, p)), pl.BlockSpec((npat, S, SLAB), lambda p: (0, 0, p)),
                   pl.BlockSpec((None, S, SLAB), lambda p: (p, 0, 0))],
        out_shape=[jax.ShapeDtypeStruct((S, D_MODEL), BF16), jax.ShapeDtypeStruct((npat, S, D_MODEL), BF16),
                   jax.ShapeDtypeStruct((N_SLABS, S, SLAB), F32)],
        scratch_shapes=[pltpu.VMEM((npat, S, SLAB), F32), pltpu.VMEM((npat, S, SLAB), F32)],
        compiler_params=_cparams(("arbitrary",)),
    )(slopes, sinks, qkv)


def _attn_bwd(qkv, dout, o, lse, slopes, sinks, patterns, name):
    S = qkv.shape[1]
    npat = len(patterns)
    has_sink = sinks is not None
    if not has_sink:
        sinks = jnp.zeros((N_HEADS,), F32)
    rows_c = 256

    def headsum(x, lo):
        same = (lax.broadcasted_iota(jnp.int32, (SLAB, SLAB), 0) < HEAD_DIM) == (lax.broadcasted_iota(jnp.int32, (SLAB, SLAB), 1) < HEAD_DIM)
        return jnp.dot(x, same.astype(F32), precision=lax.Precision.HIGH, preferred_element_type=F32)

    def body(slopes_ref, sinks_ref, x_ref, do_ref, o_ref, lsep_ref, dxo_ref, dsink_ref, dbar_ref, sacc_ref, lse_ref, dx_ref):
        p = pl.program_id(0)
        lo = lax.broadcasted_iota(jnp.int32, (BLOCK, SLAB), 1) < HEAD_DIM
        lo_c = lax.broadcasted_iota(jnp.int32, (rows_c, SLAB), 1) < HEAD_DIM
        top1 = lax.broadcasted_iota(jnp.int32, (2 * BLOCK, 1), 0) < BLOCK
        sk2 = jnp.where(top1, sinks_ref[2 * p], sinks_ref[2 * p + 1])

        def prep(ci, carry):
            rows = pl.ds(pl.multiple_of(ci * rows_c, rows_c), rows_c)
            dov = do_ref[rows, :]
            dx_ref[:, rows, :] = jnp.zeros((3, rows_c, SLAB), F32)
            packed = lsep_ref[rows, :]
            ls = [jnp.where(lo_c, packed[:, i:i + 1], packed[:, HEAD_DIM + i:HEAD_DIM + i + 1]) for i in range(npat)]
            for i in range(npat):
                lse_ref[i, rows, :] = ls[i]
            if npat == 1:
                dbar_ref[rows, :] = headsum(dov * o_ref[0, rows, :].astype(F32), lo_c)
            else:
                ws = _softmax_weights(ls)
                acc = ws[0] * headsum(dov * o_ref[0, rows, :].astype(F32), lo_c)
                for i in range(1, npat):
                    acc = acc + ws[i] * headsum(dov * o_ref[i, rows, :].astype(F32), lo_c)
                dbar_ref[rows, :] = acc
            return carry

        lax.fori_loop(0, S // rows_c, prep, 0, unroll=2)
        sacc_ref[...] = jnp.zeros((BLOCK, SLAB), F32)

        for pi, (d, maxd, scale) in enumerate(patterns):
            nb = S // d // BLOCK
            base, prev_keys = _band_consts(slopes_ref[2 * p], slopes_ref[2 * p + 1], maxd, scale)

            def blk(t, carry, pi=pi, d=d, nb=nb, base=base, prev_keys=prev_keys):
                r = t // nb
                n = t - r * nb
                start = r + (d * BLOCK) * n
                prev = jnp.where(n > 0, start - d * BLOCK, start)
                q2 = _stack_heads(_ld3(x_ref, 0, start, d), lo).astype(BF16)
                k2 = jnp.concatenate([_ld3(x_ref, 1, prev, d), _ld3(x_ref, 1, start, d)], axis=0).astype(BF16)
                v2 = jnp.concatenate([_ld3(x_ref, 2, prev, d), _ld3(x_ref, 2, start, d)], axis=0).astype(BF16)
                ls = [_ld3(lse_ref, i, start, d) for i in range(npat)]
                w = _softmax_weights(ls)[pi] if npat > 1 else 1.0
                do2 = _stack_heads(w * _ld(do_ref, start, d), lo).astype(BF16)
                dl = w * _ld(dbar_ref, start, d)
                lse2 = jnp.concatenate([ls[pi][:, :1], ls[pi][:, HEAD_DIM:HEAD_DIM + 1]], axis=0)
                dl2 = jnp.concatenate([dl[:, :1], dl[:, HEAD_DIM:HEAD_DIM + 1]], axis=0)
                s = _scores(q2, k2, base, prev_keys, n == 0)
                pr = jnp.exp(s - lse2)
                dp = lax.dot_general(do2, v2, (((1,), (1,)), ((), ())), preferred_element_type=F32)
                ds = (pr * (dp - dl2) * (HEAD_DIM ** -0.5)).astype(BF16)
                dq2 = jnp.dot(ds, k2, preferred_element_type=F32)
                dk2 = lax.dot_general(ds, q2, (((0,), (0,)), ((), ())), preferred_element_type=F32)
                dv2 = lax.dot_general(pr.astype(BF16), do2, (((0,), (0,)), ((), ())), preferred_element_type=F32)
                _acc3(dx_ref, 0, start, d, _unstack_heads(dq2, lo))
                _acc3(dx_ref, 1, prev, d, dk2[:BLOCK])
                _acc3(dx_ref, 1, start, d, dk2[BLOCK:])
                _acc3(dx_ref, 2, prev, d, dv2[:BLOCK])
                _acc3(dx_ref, 2, start, d, dv2[BLOCK:])
                if has_sink:
                    sacc_ref[...] += _unstack_heads(-jnp.exp(sk2 - lse2) * dl2, lo)
                return carry

            lax.fori_loop(0, d * nb, blk, 0, unroll=8)

        dsink_ref[...] = jnp.broadcast_to(jnp.sum(sacc_ref[...], axis=0, keepdims=True), (8, SLAB))

        def emit(ci, carry):
            rows = pl.ds(pl.multiple_of(ci * rows_c, rows_c), rows_c)
            dxo_ref[:, rows, :] = dx_ref[:, rows, :].astype(BF16)
            return carry

        lax.fori_loop(0, S // rows_c, emit, 0, unroll=2)

    smem = pl.BlockSpec(memory_space=pltpu.SMEM)
    return pl.pallas_call(
        body, name=name, grid=(N_SLABS,),
        in_specs=[smem, smem, pl.BlockSpec((3, S, SLAB), lambda p: (0, 0, p)), pl.BlockSpec((S, SLAB), lambda p: (0, p)),
                  pl.BlockSpec((npat, S, SLAB), lambda p: (0, 0, p)), pl.BlockSpec((None, S, SLAB), lambda p: (p, 0, 0))],
        out_specs=[pl.BlockSpec((3, S, SLAB), lambda p: (0, 0, p)), pl.BlockSpec((None, 8, SLAB), lambda p: (p, 0, 0))],
        out_shape=[jax.ShapeDtypeStruct((3, S, D_MODEL), BF16), jax.ShapeDtypeStruct((N_SLABS, 8, SLAB), F32)],
        scratch_shapes=[pltpu.VMEM((S, SLAB), F32), pltpu.VMEM((BLOCK, SLAB), F32), pltpu.VMEM((npat, S, SLAB), F32),
                        pltpu.VMEM((3, S, SLAB), F32)],
        compiler_params=_cparams(("arbitrary",)),
    )(slopes, sinks, qkv, dout, o, lse)


def _place():
    x, y, c = lax.axis_index("x"), lax.axis_index("y"), lax.axis_index("c")
    return x, y, c, 2 * x + y


def _other_chips(x, y):
    return [(1 - x, y), (x, 1 - y), (1 - x, 1 - y)]


HBM_SPEC = pl.BlockSpec(memory_space=pl.ANY)


def _slot(q):
    return 2 * (q % 2) + q // 2


BIG = ("ffn1_w_in", "ffn1_w_out", "ffn2_w_in", "ffn2_w_out", "a_w_qkv", "a_w_o", "kv_w", "b_w_q", "b_w_o")
QKV_SHARD = 3 * D_MODEL // N_CHIPS
ROW_SHARD = D_MODEL // N_CHIPS


LAYER0_ITEMS = (("ffn1_w_in", 0), ("ffn1_w_out", 0), ("a_w_qkv", None), ("a_w_o", None), ("ffn2_w_in", 0),
                ("ffn2_w_out", 0), ("kv_w", None))
LAYER1_ITEMS = (("ffn1_w_in", 1), ("ffn1_w_out", 1), ("b_w_q", None), ("b_w_o", None), ("ffn2_w_in", 1),
                ("ffn2_w_out", 1))
OUT_SHARD = D_FF // N_CHIPS


def _full_shape(name):
    if name.endswith("w_in"):
        return (D_MODEL, 2 * D_FF)
    if name.endswith("w_out"):
        return (D_FF, D_MODEL)
    if name == "a_w_qkv":
        return (D_MODEL, 3 * D_MODEL)
    if name == "kv_w":
        return (N_CHIPS, 2, ROW_SHARD // 2, 2 * N_KV_B * HEAD_DIM)
    return (N_CHIPS, 2, ROW_SHARD // 2, D_MODEL)


def _gather_src(item, ref, c):
    name, layer = item
    if name.endswith("w_in"):
        return ref.at[layer, pl.ds(c * (D_MODEL // 2), D_MODEL // 2)]
    if name.endswith("w_out"):
        return ref.at[layer, pl.ds(c * (OUT_SHARD // 2), OUT_SHARD // 2)]
    if name == "a_w_qkv":
        return ref.at[0, pl.ds(c * (D_MODEL // 2), D_MODEL // 2)]
    if name == "kv_w":
        return ref.at[pl.ds(c * (ROW_SHARD // 2), ROW_SHARD // 2)]
    return ref.at[0, pl.ds(c * (ROW_SHARD // 2), ROW_SHARD // 2)]


def _gather_dst(item, ref, q, c):
    name, _ = item
    if name.endswith("w_in"):
        return ref.at[pl.ds(c * (D_MODEL // 2), D_MODEL // 2), pl.ds(_slot(q) * HALF_FF, HALF_FF)]
    if name.endswith("w_out"):
        return ref.at[pl.ds(q * OUT_SHARD + c * (OUT_SHARD // 2), OUT_SHARD // 2)]
    if name == "a_w_qkv":
        return ref.at[pl.ds(c * (D_MODEL // 2), D_MODEL // 2), pl.ds(q * QKV_SHARD, QKV_SHARD)]
    return ref.at[q, c]


def _all_gather(items, shards, small):
    n = len(items)
    r = small.shape[0]
    per = 8

    def body(*refs):
        srcs, small_ref = refs[:n], refs[n]
        dsts, s_ref = refs[n + 1:2 * n + 1], refs[2 * n + 1]
        send_sems, recv_sems = refs[2 * n + 2:]
        x, y, c, myq = _place()
        sibling = (x, y, 1 - c)
        chips = _other_chips(x, y)

        def big(t, k, src, q, h, to):
            return pltpu.make_async_remote_copy(src_ref=src, dst_ref=_gather_dst(items[t], dsts[t], q, h),
                                                send_sem=send_sems.at[per * t + k], recv_sem=recv_sems.at[per * t + k],
                                                device_id=to, device_id_type=MESH)

        def tiny(k, q, to):
            return pltpu.make_async_remote_copy(src_ref=small_ref, dst_ref=s_ref.at[q], send_sem=send_sems.at[per * n + k],
                                                recv_sem=recv_sems.at[per * n + k], device_id=to, device_id_type=MESH)

        first = []
        for j, chip in enumerate(chips):
            if j < 2:
                first += [big(t, j, _gather_src(items[t], srcs[t], c), myq, c, (*chip, c)) for t in range(n)]
            first.append(tiny(j, myq, (*chip, c)))
        own = [big(t, 6 + h, _gather_src(items[t], srcs[t], h), myq, h, sibling) for t in range(n) for h in (0, 1)]
        own.append(tiny(3, myq, sibling))
        for cp in first + own:
            cp.start()
        relay_from = ((x + 1 - c) % 2, (y + c) % 2)
        relay_to = ((x + c) % 2, (y + 1 - c) % 2, c)
        q_relay = 2 * relay_from[0] + relay_from[1]
        passed = []
        for t in range(n):
            src = _gather_src(items[t], srcs[t], c)
            for j, (cx, cy) in enumerate(chips[:2]):
                q = 2 * cx + cy
                big(t, j, src, q, c, sibling).wait_recv()
                fwd = big(t, 3 + j, _gather_dst(items[t], dsts[t], q, c), q, c, sibling)
                fwd.start()
                passed.append(fwd)
            relay = big(t, 2, _gather_dst(items[t], dsts[t], q_relay, c), q_relay, c, relay_to)
            relay.start()
            passed.append(relay)
        q_diag = 2 * chips[2][0] + chips[2][1]
        for t in range(n):
            big(t, 2, _gather_src(items[t], srcs[t], c), q_diag, c, sibling).wait_recv()
            fwd = big(t, 5, _gather_dst(items[t], dsts[t], q_diag, c), q_diag, c, sibling)
            fwd.start()
            passed.append(fwd)
        for j, (cx, cy) in enumerate(chips):
            q = 2 * cx + cy
            for t in range(n):
                big(t, 3 + j, _gather_src(items[t], srcs[t], c), q, 1 - c, sibling).wait_recv()
            tiny(j, q, sibling).wait_recv()
        for cp in own:
            cp.wait_recv()
        for cp in first + passed + own:
            cp.wait_send()

    outs = pl.pallas_call(
        body, name="all_gather_layer0",
        in_specs=[HBM_SPEC] * (n + 1), out_specs=[HBM_SPEC] * (n + 1),
        out_shape=[jax.ShapeDtypeStruct(_full_shape(name), BF16) for name, _ in items]
        + [jax.ShapeDtypeStruct((N_CHIPS, r, 128), F32)],
        scratch_shapes=[pltpu.SemaphoreType.DMA((per * n + 4,)), pltpu.SemaphoreType.DMA((per * n + 4,))],
    )(*[shards[item] for item in items], small)
    return list(outs[:n]), outs[n]


SEM_SPEC = pl.BlockSpec(memory_space=pltpu.SEMAPHORE)
DATAFLOW = pltpu.SideEffectType.DATAFLOW_SIDE_EFFECTING
PER_ITEM = 8


def _split_start(name, copies, n_sems, sources, land_shapes, after):
    n, m = len(sources), len(land_shapes)

    def body(*refs):
        srcs, lands = refs[:n], refs[n:n + m]
        send_sems, recv_sems = refs[n + m + 1], refs[n + m + 2]
        token = refs[-1]
        for src, dst_there, _, s, peer in copies(srcs, lands):
            pltpu.make_async_remote_copy(src_ref=src, dst_ref=dst_there, send_sem=send_sems.at[s], recv_sem=recv_sems.at[s],
                                         device_id=peer, device_id_type=MESH).start()
        token[...] = jnp.zeros_like(token)

    src_arrays = [pltpu.with_memory_space_constraint(a, pltpu.HBM) for a in sources]
    land_arrays = [pltpu.with_memory_space_constraint(lax.empty(s.shape, s.dtype), pltpu.HBM) for s in land_shapes]
    hbm = pl.BlockSpec(memory_space=pltpu.HBM)
    outs = pl.pallas_call(
        body, name=name,
        in_specs=[hbm] * (n + m) + [HBM_SPEC],
        out_specs=[SEM_SPEC, SEM_SPEC] + [hbm] * (n + m) + [pl.BlockSpec(memory_space=pltpu.VMEM)],
        out_shape=[pltpu.SemaphoreType.DMA((n_sems,)), pltpu.SemaphoreType.DMA((n_sems,))]
        + [pltpu.HBM(a.shape, a.dtype) for a in src_arrays + land_arrays] + [jax.ShapeDtypeStruct((8, 128), F32)],
        input_output_aliases={i: 2 + i for i in range(n + m)},
        compiler_params=pltpu.CompilerParams(has_side_effects=DATAFLOW),
    )(*src_arrays, *land_arrays, after)
    return (outs[0], outs[1], list(outs[2:2 + n]), list(outs[2 + n:2 + n + m])), outs[-1]


def _split_wait(name, copies, state, after):
    send_sems, recv_sems, srcs_thru, lands_thru = state
    n, m = len(srcs_thru), len(lands_thru)
    after = list(after) if isinstance(after, (list, tuple)) else [after]

    def body(*refs):
        srcs, lands = refs[:n], refs[n:n + m]
        send_sems, recv_sems = refs[n + m], refs[n + m + 1]
        for src, _, dst_here, s, peer in copies(srcs, lands):
            cp = pltpu.make_async_remote_copy(src_ref=src, dst_ref=dst_here, send_sem=send_sems.at[s], recv_sem=recv_sems.at[s],
                                              device_id=peer, device_id_type=MESH)
            cp.wait_send()
            cp.wait_recv()

    hbm = pl.BlockSpec(memory_space=pltpu.HBM)
    outs = pl.pallas_call(
        body, name=name,
        in_specs=[hbm] * (n + m) + [SEM_SPEC, SEM_SPEC] + [HBM_SPEC] * len(after),
        out_specs=[hbm] * (n + m),
        out_shape=[pltpu.HBM(a.shape, a.dtype) for a in srcs_thru + lands_thru],
        input_output_aliases={i: i for i in range(n + m)},
        compiler_params=pltpu.CompilerParams(has_side_effects=DATAFLOW),
    )(*srcs_thru, *lands_thru, send_sems, recv_sems, *after)
    return list(outs[:n]), list(outs[n:])


def _gather_copies(items):
    def copies(srcs, lands):
        x, y, c, myq = _place()
        out = []
        for t, item in enumerate(items):
            for h in (0, 1):
                src = _gather_src(item, srcs[t], h)
                for j, (cx, cy) in enumerate(_other_chips(x, y)):
                    out.append((src, _gather_dst(item, lands[t], myq, h), _gather_dst(item, lands[t], 2 * cx + cy, h),
                                PER_ITEM * t + 2 * j + h, (cx, cy, c)))
                out.append((src, _gather_dst(item, lands[t], myq, h), _gather_dst(item, lands[t], myq, h),
                            PER_ITEM * t + 6 + h, (x, y, 1 - c)))
        return out
    return copies


def _gather_start(items, shards, after):
    lands = [jax.ShapeDtypeStruct(_full_shape(name), BF16) for name, _ in items]
    return _split_start("gather_layer1_start", _gather_copies(items), PER_ITEM * len(items),
                        [shards[item] for item in items], lands, after)


def _gather_wait(items, state, after):
    return _split_wait("gather_layer1_wait", _gather_copies(items), state, after)[1]


def _small_all_reduce(v):
    r = v.shape[0]

    def body(v_ref, o_ref, buf_ref, send_sems, recv_sems):
        x, y, c, _ = _place()
        me = 4 * x + 2 * y + c
        buf_ref[me] = v_ref[...]
        copies = []
        for k in range(1, 8):
            fx, fy, fc = (k >> 2) & 1, (k >> 1) & 1, k & 1
            to = (x ^ fx, y ^ fy, c ^ fc)
            cp = pltpu.make_async_remote_copy(src_ref=v_ref, dst_ref=buf_ref.at[me], send_sem=send_sems.at[k - 1],
                                              recv_sem=recv_sems.at[k - 1], device_id=to, device_id_type=MESH)
            cp.start()
            copies.append(cp)
        for k in range(1, 8):
            fx, fy, fc = (k >> 2) & 1, (k >> 1) & 1, k & 1
            src_dev = 4 * (x ^ fx) + 2 * (y ^ fy) + (c ^ fc)
            pltpu.make_async_remote_copy(src_ref=v_ref, dst_ref=buf_ref.at[src_dev], send_sem=send_sems.at[k - 1],
                                         recv_sem=recv_sems.at[k - 1], device_id=(x, y, c), device_id_type=MESH).wait_recv()
        for cp in copies:
            cp.wait_send()
        tot = buf_ref[0]
        for i in range(1, 8):
            tot = tot + buf_ref[i]
        o_ref[...] = tot

    vm = pl.BlockSpec(memory_space=pltpu.VMEM)
    return pl.pallas_call(
        body, name="small_all_reduce", in_specs=[vm], out_specs=vm,
        out_shape=jax.ShapeDtypeStruct((r, 128), F32),
        scratch_shapes=[pltpu.VMEM((8, r, 128), F32), pltpu.SemaphoreType.DMA((7,)), pltpu.SemaphoreType.DMA((7,))],
    )(v)


def _grad_view(kind, g):
    if kind == "col":
        return g.reshape(2, g.shape[0] // 2, g.shape[1])
    return g.reshape(N_CHIPS, 2, g.shape[0] // (2 * N_CHIPS), g.shape[1])


def _half_of(kind, ref, h):
    return ref.at[h] if kind == "col" else ref.at[:, h]


def _half_shape(kind, view_shape):
    return view_shape[1:] if kind == "col" else (view_shape[0],) + view_shape[2:]


def _piece_of(kind, width, colblock, ref, q):
    if kind == "col":
        return ref.at[:, pl.ds(colblock(q) * width, width)]
    return ref.at[q]


def _piece_shape(kind, width, half_shape):
    return (half_shape[0], width) if kind == "col" else half_shape[1:]


def _pair_exchange(views, kinds, name):
    n = len(views)

    def body(*refs):
        ins, outs = refs[:n], refs[n:2 * n]
        send_sems, recv_sems = refs[2 * n:]
        x, y, c, _ = _place()
        cps = []
        for t in range(n):
            cp = pltpu.make_async_remote_copy(src_ref=_half_of(kinds[t], ins[t], 1 - c), dst_ref=outs[t],
                                              send_sem=send_sems.at[t], recv_sem=recv_sems.at[t],
                                              device_id=(x, y, 1 - c), device_id_type=MESH)
            cp.start()
            cps.append(cp)
        for cp in cps:
            cp.wait()

    return pl.pallas_call(
        body, name=name, in_specs=[HBM_SPEC] * n, out_specs=[HBM_SPEC] * n,
        out_shape=[jax.ShapeDtypeStruct(_half_shape(k, v.shape), v.dtype) for k, v in zip(kinds, views)],
        scratch_shapes=[pltpu.SemaphoreType.DMA((n,)), pltpu.SemaphoreType.DMA((n,))],
    )(*views)


def _pair_sum(kind, view, recv, c, name):
    hs = recv.shape
    N = hs[-1]
    rows = hs[-2]
    tr = _pick(rows, (512, 352, 128))
    tn = _pick(N, (1408, 1024, 512))

    def body(c_ref, p_ref, r_ref, s_ref):
        s_ref[...] = (p_ref[...] + r_ref[...]).astype(BF16)

    if kind == "col":
        grid = (rows // tr, N // tn)
        mine = pl.BlockSpec((None, tr, tn), lambda i, j, c_ref: (c_ref[0], i, j))
        blk = pl.BlockSpec((tr, tn), lambda i, j, c_ref: (i, j))
        sem = ("parallel", "parallel")
    else:
        grid = (N_CHIPS, rows // tr, N // tn)
        mine = pl.BlockSpec((None, None, tr, tn), lambda q, i, j, c_ref: (q, c_ref[0], i, j))
        blk = pl.BlockSpec((None, tr, tn), lambda q, i, j, c_ref: (q, i, j))
        sem = ("parallel", "parallel", "parallel")
    return pl.pallas_call(
        body, name=name,
        grid_spec=pltpu.PrefetchScalarGridSpec(num_scalar_prefetch=1, grid=grid, in_specs=[mine, blk], out_specs=blk),
        out_shape=jax.ShapeDtypeStruct(hs, BF16),
        compiler_params=_cparams(sem),
    )(c.reshape(1).astype(jnp.int32), view, recv)


def _chip_copies(kinds, widths, colblocks):
    def copies(srcs, lands):
        x, y, c, _ = _place()
        out = []
        for j, (cx, cy) in enumerate(_other_chips(x, y)):
            for t in range(len(kinds)):
                out.append((_piece_of(kinds[t], widths[t], colblocks[t], srcs[t], 2 * cx + cy), lands[t].at[j],
                            lands[t].at[j], 3 * t + j, (cx, cy, c)))
        return out
    return copies


def _chip_land_shapes(sums, kinds, widths):
    return [jax.ShapeDtypeStruct((3,) + _piece_shape(k, w, s.shape), BF16) for k, w, s in zip(kinds, widths, sums)]


def _chip_exchange(sums, kinds, widths, colblocks, name):
    n = len(sums)
    copies = _chip_copies(kinds, widths, colblocks)

    def body(*refs):
        send_sems, recv_sems = refs[2 * n:]
        cps = [pltpu.make_async_remote_copy(src_ref=src, dst_ref=dst, send_sem=send_sems.at[s], recv_sem=recv_sems.at[s],
                                            device_id=peer, device_id_type=MESH)
               for src, dst, _, s, peer in copies(refs[:n], refs[n:2 * n])]
        for cp in cps:
            cp.start()
        for cp in cps:
            cp.wait()

    return pl.pallas_call(
        body, name=name, in_specs=[HBM_SPEC] * n, out_specs=[HBM_SPEC] * n,
        out_shape=_chip_land_shapes(sums, kinds, widths),
        scratch_shapes=[pltpu.SemaphoreType.DMA((3 * n,)), pltpu.SemaphoreType.DMA((3 * n,))],
    )(*sums)


N_DIRECT = 7


def _direct_piece(kind, width, colblock, view_ref, q, h):
    if kind == "col":
        return view_ref.at[h, :, pl.ds(colblock(q) * width, width)]
    return view_ref.at[q, h]


def _direct_copies(kinds, widths, colblocks):
    def copies(srcs, lands):
        x, y, c, myq = _place()
        out = []
        for t in range(len(kinds)):
            def piece(q, h, t=t):
                return _direct_piece(kinds[t], widths[t], colblocks[t], srcs[t], q, h)
            for j, (cx, cy) in enumerate(_other_chips(x, y)):
                for h in (0, 1):
                    out.append((piece(2 * cx + cy, h), lands[t].at[2 * j + c], lands[t].at[2 * j + h],
                                10 * t + 3 * j + c + h, (cx, cy, h)))
            out.append((piece(myq, 1 - c), lands[t].at[6], lands[t].at[6], 10 * t + 9, (x, y, 1 - c)))
        return out
    return copies


def _chip_sum(kind, own_src, recv, block_idx, c, shard_shape, layer, into, name, direct=False):
    n_recv, rows, N = recv.shape
    tr = _pick(rows, (512, 352, 128))
    tn = _pick(N, (1408, 1024, 768, 512))
    ni, nj = rows // tr, N // tn

    def body(q_ref, s_ref, r_ref, *rest):
        o_ref = rest[-1]
        tot = s_ref[...].astype(F32)
        for k in range(n_recv):
            tot = tot + r_ref[k].astype(F32)
        o_ref[...] = tot

    if direct and kind == "col":
        own = pl.BlockSpec((None, tr, tn), lambda i, j, q_ref: (q_ref[1], i, q_ref[0] * nj + j))
    elif direct:
        own = pl.BlockSpec((None, None, tr, tn), lambda i, j, q_ref: (q_ref[0], q_ref[1], i, j))
    elif kind == "col":
        own = pl.BlockSpec((tr, tn), lambda i, j, q_ref: (i, q_ref[0] * nj + j))
    else:
        own = pl.BlockSpec((None, tr, tn), lambda i, j, q_ref: (q_ref[0], i, j))
    if len(shard_shape) == 3:
        lead = 0 if layer is None else layer
        out_spec = pl.BlockSpec((None, tr, tn), lambda i, j, q_ref: (lead, q_ref[1] * ni + i, j))
    else:
        out_spec = pl.BlockSpec((tr, tn), lambda i, j, q_ref: (q_ref[1] * ni + i, j))
    in_specs = [own, pl.BlockSpec((n_recv, tr, tn), lambda i, j, q_ref: (0, i, j))]
    s = own_src
    args = [jnp.stack([block_idx, c]).astype(jnp.int32), s, recv]
    aliases = {}
    if into is not None:
        in_specs.append(HBM_SPEC)
        args.append(into)
        aliases = {3: 0}
    return pl.pallas_call(
        body, name=name,
        grid_spec=pltpu.PrefetchScalarGridSpec(num_scalar_prefetch=1, grid=(ni, nj), in_specs=in_specs, out_specs=out_spec),
        out_shape=jax.ShapeDtypeStruct(shard_shape, F32), input_output_aliases=aliases,
        compiler_params=_cparams(("parallel", "parallel")),
    )(*args)


def _half_window(ref, h):
    rows = ref.shape[-2] // 2
    if ref.ndim == 3:
        return ref.at[:, pl.ds(h * rows, rows)]
    return ref.at[pl.ds(h * rows, rows)]


def _share_halves(grads, name):
    n = len(grads)

    def body(*refs):
        outs = refs[n:2 * n]
        send_sems, recv_sems = refs[2 * n:]
        x, y, c, _ = _place()
        cps = []
        for t in range(n):
            cp = pltpu.make_async_remote_copy(src_ref=_half_window(outs[t], c), dst_ref=_half_window(outs[t], c),
                                              send_sem=send_sems.at[t], recv_sem=recv_sems.at[t],
                                              device_id=(x, y, 1 - c), device_id_type=MESH)
            cp.start()
            cps.append(cp)
        for t in range(n):
            cps[t].wait_send()
            pltpu.make_async_remote_copy(src_ref=_half_window(outs[t], c), dst_ref=_half_window(outs[t], 1 - c),
                                         send_sem=send_sems.at[t], recv_sem=recv_sems.at[t],
                                         device_id=(x, y, 1 - c), device_id_type=MESH).wait_recv()

    return pl.pallas_call(
        body, name=name, in_specs=[HBM_SPEC] * n, out_specs=[HBM_SPEC] * n,
        out_shape=[jax.ShapeDtypeStruct(g.shape, F32) for g in grads],
        input_output_aliases={t: t for t in range(n)},
        scratch_shapes=[pltpu.SemaphoreType.DMA((n,)), pltpu.SemaphoreType.DMA((n,))],
    )(*grads)


def _adamw(w, g, m, v, name):
    R, W = w.shape
    tr = _pick(R, (512, 352, 256, 32))

    def body(w_ref, g_ref, m_ref, v_ref, d_ref, nm_ref, nv_ref, go_ref):
        gv = g_ref[...]
        go_ref[...] = gv
        nm = ADAM_B1 * m_ref[...] + (1.0 - ADAM_B1) * gv
        nv = ADAM_B2 * v_ref[...] + (1.0 - ADAM_B2) * (gv * gv)
        m_hat = nm / (1.0 - ADAM_B1 ** ADAM_STEP)
        v_hat = nv / (1.0 - ADAM_B2 ** ADAM_STEP)
        d_ref[...] = -ADAM_LR * (m_hat / (jnp.sqrt(v_hat) + ADAM_EPS) + ADAM_WD * w_ref[...])
        nm_ref[...] = nm
        nv_ref[...] = nv

    blk = pl.BlockSpec((tr, W), lambda i: (i, 0))
    shp = jax.ShapeDtypeStruct((R, W), F32)
    return pl.pallas_call(
        body, name=name, grid=(R // tr,), in_specs=[blk] * 4, out_specs=[blk] * 4, out_shape=[shp] * 4,
        compiler_params=_cparams(("parallel",)),
    )(w, g, m, v)


SMALL_ROWS = 32


def _pack_small(ln_g, ln_b, sinks):
    rows = jnp.concatenate([ln_g.reshape(-1, 128), ln_b.reshape(-1, 128),
                            jnp.pad(sinks.reshape(1, -1), ((0, 0), (0, 128 - sinks.size)))], axis=0)
    return jnp.pad(rows, ((0, SMALL_ROWS - rows.shape[0]), (0, 0)))


def _unpack_small(s, ln_shape, sink_shape):
    n = ln_shape[0] * ln_shape[1] * ln_shape[2] // 128
    return s[:n].reshape(ln_shape), s[n:2 * n].reshape(ln_shape), s[2 * n, :sink_shape[1]].reshape(sink_shape)


def _ffn_fwd(xin, w_in, w_out, gain, bias, tag):
    u, h = _ffn_in(xin, w_in, "ffn_in_" + tag)
    y, yb, z = _mm_ln(h, w_out, xin, gain, bias, 0.5, "ffn_out_ln_" + tag)
    return y, yb, dict(u=u, h=h, z=z, xin=xin)


def _ffn_bwd(dz, dzc, saved, w_in, w_out, xin_b, tag, dw_dtype=F32, ln=None):
    du = _ffn_bwd_h(dzc, w_out, saved["u"], "ffn_bwd_h_" + tag)
    d_w_out = _mm_tn(saved["h"], dzc, "ffn_dwout_" + tag, out_dtype=dw_dtype)
    d_w_in = _mm_tn(xin_b, du, "ffn_dwin_" + tag, out_dtype=dw_dtype)
    dx = _mm_nt(du, w_in, "ffn_dx_" + tag, add=dz, add_scale=ALPHA, ln=ln)
    return dx, d_w_in, d_w_out


def kernel(x, ffn1_w_in, ffn1_w_out, ffn2_w_in, ffn2_w_out, ln_g, ln_b, a_w_qkv, a_w_o, kv_w, b_w_q, b_sinks, b_w_o, loss_target, m_ffn1_w_in, m_ffn1_w_out, m_ffn2_w_in, m_ffn2_w_out, m_ln_g, m_ln_b, m_a_w_qkv, m_a_w_o, m_kv_w, m_b_w_q, m_b_sinks, m_b_w_o, v_ffn1_w_in, v_ffn1_w_out, v_ffn2_w_in, v_ffn2_w_out, v_ln_g, v_ln_b, v_a_w_qkv, v_a_w_o, v_kv_w, v_b_w_q, v_b_sinks, v_b_w_o):
    ws = dict(ffn1_w_in=ffn1_w_in, ffn1_w_out=ffn1_w_out, ffn2_w_in=ffn2_w_in, ffn2_w_out=ffn2_w_out, a_w_qkv=a_w_qkv,
              a_w_o=a_w_o, kv_w=kv_w, b_w_q=b_w_q, b_w_o=b_w_o)
    ms = dict(ffn1_w_in=m_ffn1_w_in, ffn1_w_out=m_ffn1_w_out, ffn2_w_in=m_ffn2_w_in, ffn2_w_out=m_ffn2_w_out,
              a_w_qkv=m_a_w_qkv, a_w_o=m_a_w_o, kv_w=m_kv_w, b_w_q=m_b_w_q, b_w_o=m_b_w_o)
    vs = dict(ffn1_w_in=v_ffn1_w_in, ffn1_w_out=v_ffn1_w_out, ffn2_w_in=v_ffn2_w_in, ffn2_w_out=v_ffn2_w_out,
              a_w_qkv=v_a_w_qkv, a_w_o=v_a_w_o, kv_w=v_kv_w, b_w_q=v_b_w_q, b_w_o=v_b_w_o)
    _, _, c_idx, myq = _place()
    xs = x[0]
    target = loss_target[0]

    shards = {(n, l): ws[n].astype(BF16) for n, l in LAYER0_ITEMS + LAYER1_ITEMS}

    def as_weights(items, arrays):
        return {n: (a.reshape(D_MODEL, a.shape[-1]) if a.ndim == 4 else a) for (n, _), a in zip(items, arrays)}

    full0, small = _all_gather(LAYER0_ITEMS, shards, _pack_small(ln_g, ln_b, b_sinks))
    gather_state, token = _gather_start(LAYER1_ITEMS, shards, small)

    def layer1_weights(after):
        return as_weights(LAYER1_ITEMS, _gather_wait(LAYER1_ITEMS, gather_state, after))

    n_ln = ln_g.size // 128
    lg = jnp.concatenate([small[q, :n_ln].reshape(DEPTH, 3, 1, -1) for q in range(N_CHIPS)], axis=-1)
    lb = jnp.concatenate([small[q, n_ln:2 * n_ln].reshape(DEPTH, 3, 1, -1) for q in range(N_CHIPS)], axis=-1)
    lg = lg + token[0, 0]
    reducer = _GradReducer(c_idx, myq, {n: ws[n].shape for n in BIG})
    sq, grad_x, _, gg, gb, dsink_part = _local_step(xs, target, as_weights(LAYER0_ITEMS, full0), layer1_weights,
                                                    lg, lb, b_sinks.reshape(N_HEADS), reducer.begin)

    loss_row = jnp.pad(jnp.sum(sq).reshape(1, 1), ((0, 0), (0, 127)))
    dsinks = jnp.pad(dsink_part[:, 0, :].reshape(N_SLABS, 2, HEAD_DIM)[:, :, 0].reshape(1, N_HEADS), ((0, 0), (0, 128 - N_HEADS)))
    gg_full = jnp.stack([jnp.stack([jnp.sum(gg[i][j], axis=0) for j in range(3)]) for i in range(DEPTH)])
    gb_full = jnp.stack([jnp.stack([jnp.sum(gb[i][j], axis=0) for j in range(3)]) for i in range(DEPTH)])
    small_in = jnp.concatenate([loss_row, dsinks, gg_full.reshape(-1, 128), gb_full.reshape(-1, 128)], axis=0)
    small_in = jnp.pad(small_in, ((0, (-small_in.shape[0]) % 8), (0, 0)))
    small_sum = _small_all_reduce(small_in)
    loss = small_sum[0, 0] * (0.5 / D_MODEL)
    grad_sinks = small_sum[1, :N_HEADS].reshape(b_sinks.shape)
    n_full = DEPTH * 3 * D_MODEL // 128
    cols = D_MODEL // N_CHIPS
    grad_ln_g = lax.dynamic_slice_in_dim(small_sum[2:2 + n_full].reshape(DEPTH, 3, D_MODEL), myq * cols, cols, axis=2)
    grad_ln_b = lax.dynamic_slice_in_dim(small_sum[2 + n_full:2 + 2 * n_full].reshape(DEPTH, 3, D_MODEL), myq * cols, cols, axis=2)
    return _update(reducer, grad_x, loss, grad_ln_g, grad_ln_b, grad_sinks, ws, ms, vs,
                   (ln_g, ln_b, b_sinks), (m_ln_g, m_ln_b, m_b_sinks), (v_ln_g, v_ln_b, v_b_sinks))


def _local_step(xs, target, W, layer1_weights, lg, lb, sinks, grads_ready=None):
    if grads_ready is None:
        grads_ready = lambda tag, grads, overlap: 0.0
    S = xs.shape[0]
    slopes = jnp.asarray(_alibi_slopes(N_HEADS))
    in1, out1, in2, out2 = [W["ffn1_w_in"]], [W["ffn1_w_out"]], [W["ffn2_w_in"]], [W["ffn2_w_out"]]

    y1, y1b, s1 = _ffn_fwd(xs, in1[0], out1[0], lg[0, 0], lb[0, 0], "a1")
    qkv_a = _mm_nn(y1b, W["a_w_qkv"], F32, "qkv_a", split=True)
    mix_a, o_a, lse_a = _attn_fwd(qkv_a, slopes, None, PATTERNS_A, "attn_a_fwd")
    y2, y2b, z2 = _mm_ln(mix_a, W["a_w_o"], y1, lg[0, 1], lb[0, 1], 1.0, "attn_a_out_ln")
    y3, y3b, s3 = _ffn_fwd(y2, in2[0], out2[0], lg[0, 2], lb[0, 2], "a2")
    kv_w_rep = jnp.broadcast_to(W["kv_w"].reshape(D_MODEL, 2, N_KV_B, 1, HEAD_DIM),
                                (D_MODEL, 2, N_KV_B, GROUP_B, HEAD_DIM)).reshape(D_MODEL, 2 * D_MODEL)
    kv_rep = _mm_nn(y3b, kv_w_rep, F32, "kv_proj", split=(1, 2))
    W = dict(W, **layer1_weights(kv_rep))
    in1, out1, in2, out2 = (in1 + [W["ffn1_w_in"]], out1 + [W["ffn1_w_out"]], in2 + [W["ffn2_w_in"]],
                            out2 + [W["ffn2_w_out"]])
    y4, y4b, s4 = _ffn_fwd(y3, in1[1], out1[1], lg[1, 0], lb[1, 0], "b1")
    qkv_b = _mm_nn(y4b, W["b_w_q"], F32, "q_b", split=(0, 1), into=kv_rep)
    mix_b, o_b, lse_b = _attn_fwd(qkv_b, slopes, sinks, PATTERNS_B, "attn_b_fwd")
    y5, y5b, z5 = _mm_ln(mix_b, W["b_w_o"], y4, lg[1, 1], lb[1, 1], 1.0, "attn_b_out_ln")
    y6, _, s6 = _ffn_fwd(y5, in2[1], out2[1], lg[1, 2], lb[1, 2], "b2")

    gr = {n: None for n in BIG}
    gg = [[None] * 3 for _ in range(DEPTH)]
    gb = [[None] * 3 for _ in range(DEPTH)]
    dz6, dz6c, gg[1][2], gb[1][2], sq = _loss_ln_bwd(y6, target, s6["z"], lg[1, 2], 0.5, "loss_ln_bwd")

    (dz5, dz5b, gg[1][1], gb[1][1]), d_in2_b, d_out2_b = _ffn_bwd(dz6, dz6c, s6, in2[1], out2[1], y5b, "b2", BF16,
                                                                  ln=(z5, lg[1, 1], 1.0))
    gr["b_w_o"] = _mm_tn(mix_b, dz5b, "d_b_w_o", out_dtype=BF16)
    dmix_b = _mm_nt(dz5b, W["b_w_o"], "d_mix_b")
    dqkv_b, dsink_part = _attn_bwd(qkv_b, dmix_b, o_b, lse_b, slopes, sinks, PATTERNS_B, "attn_b_bwd")
    dq_b = (dqkv_b, 0)
    gr["b_w_q"] = _mm_tn(y4b, dq_b, "d_b_w_q", out_dtype=BF16)
    dz4, dz4c, gg[1][0], gb[1][0] = _mm_nt(dq_b, W["b_w_q"], "d_y4", add=dz5, add_scale=ALPHA, ln=(s4["z"], lg[1, 0], 0.5))
    dy3, d_in1_b, d_out1_b = _ffn_bwd(dz4, dz4c, s4, in1[1], out1[1], y3b, "b1", BF16)
    gr["kv_w"] = _d_kv_w(y3b, dqkv_b, "d_kv_w")
    tok = grads_ready("l1", {("ffn2_w_in", 1): d_in2_b, ("ffn2_w_out", 1): d_out2_b, ("b_w_o", None): gr["b_w_o"],
                             ("b_w_q", None): gr["b_w_q"], ("ffn1_w_in", 1): d_in1_b, ("ffn1_w_out", 1): d_out1_b,
                             ("kv_w", None): gr["kv_w"]}, True)
    lg0 = lg[0] + tok
    dz3, dz3c, gg[0][2], gb[0][2] = _mm_nt(dqkv_b, kv_w_rep, "d_y3_kv", add=dy3, add_scale=1.0, split=(1, 2),
                                           ln=(s3["z"], lg0[2], 0.5))

    (dz2, dz2b, gg[0][1], gb[0][1]), d_in2_a, d_out2_a = _ffn_bwd(dz3, dz3c, s3, in2[0], out2[0], y2b, "a2", BF16,
                                                                  ln=(z2, lg0[1], 1.0))
    tok = grads_ready("a2", {("ffn2_w_in", 0): d_in2_a, ("ffn2_w_out", 0): d_out2_a}, True)
    lg0 = lg0 + tok
    gr["a_w_o"] = _mm_tn(mix_a, dz2b, "d_a_w_o", out_dtype=BF16)
    dmix_a = _mm_nt(dz2b, W["a_w_o"], "d_mix_a")
    dqkv_a, _ = _attn_bwd(qkv_a, dmix_a, o_a, lse_a, slopes, None, PATTERNS_A, "attn_a_bwd")
    gr["a_w_qkv"] = _mm_tn(y1b, dqkv_a, "d_a_w_qkv", split=True, out_dtype=BF16)
    tok = grads_ready("mix", {("a_w_o", None): gr["a_w_o"], ("a_w_qkv", None): gr["a_w_qkv"]}, True)
    lg0 = lg0 + tok
    dz1, dz1c, gg[0][0], gb[0][0] = _mm_nt(dqkv_a, W["a_w_qkv"], "d_y1", add=dz2, add_scale=ALPHA, split=True,
                                           ln=(s1["z"], lg0[0], 0.5))
    grad_x, d_in1_a, d_out1_a = _ffn_bwd(dz1, dz1c, s1, in1[0], out1[0], xs, "a1", BF16)
    grads_ready("a1", {("ffn1_w_in", 0): d_in1_a, ("ffn1_w_out", 0): d_out1_a}, True)
    gr["ffn1_w_in"] = [d_in1_a, d_in1_b]
    gr["ffn1_w_out"] = [d_out1_a, d_out1_b]
    gr["ffn2_w_in"] = [d_in2_a, d_in2_b]
    gr["ffn2_w_out"] = [d_out2_a, d_out2_b]
    return sq, grad_x, gr, gg, gb, dsink_part


def _grad_item(name, layer, g):
    if name.endswith("w_in"):
        return (g, "col", HALF_FF, _slot, name, layer)
    if name.endswith("w_out"):
        return (g, "row", D_MODEL, None, name, layer)
    if name == "a_w_qkv":
        return (g, "col", QKV_SHARD, lambda q: q, name, None)
    return (g, "row", g.shape[1], None, name, None)


class _GradReducer:
    def __init__(self, c_idx, myq, shard_shapes):
        self.c_idx, self.myq, self.shard_shapes = c_idx, myq, shard_shapes
        self.groups = []

    def begin(self, tag, grads, overlap):
        items = [_grad_item(n, l, g) for (n, l), g in grads.items()]
        kinds, widths, colblocks = [it[1] for it in items], [it[2] for it in items], [it[3] for it in items]
        views = [_grad_view(k, it[0]) for k, it in zip(kinds, items)]
        if overlap:
            lands = [jax.ShapeDtypeStruct((N_DIRECT,) + _piece_shape(k, w, _half_shape(k, v.shape)), BF16)
                     for k, w, v in zip(kinds, widths, views)]
            state, token = _split_start("grad_direct_start_" + tag, _direct_copies(kinds, widths, colblocks), 10 * len(items),
                                        views, lands, jnp.zeros((8, 128), F32))
            self.groups.append((tag, items, None, state, token))
            return token[0, 0]
        from_sibling = _pair_exchange(views, kinds, "grad_pair_exchange_" + tag)
        sums = [_pair_sum(k, v, r, self.c_idx, "pair_sum_%s_%d" % (tag, t))
                for t, (k, v, r) in enumerate(zip(kinds, views, from_sibling))]
        self.groups.append((tag, items, sums, None, None))
        return 0.0

    def _sum_group(self, tag, items, sums, received, direct):
        for t, (it, s, r) in enumerate(zip(items, sums, received)):
            _, k, _, cb, name, layer = it
            own = cb(self.myq) if k == "col" else self.myq
            self.half_done[name] = _chip_sum(k, s, r, own, self.c_idx, self.shard_shapes[name], layer,
                                             self.half_done.get(name), "chip_sum_%s_%d" % (tag, t), direct=direct)

    def finish_first(self, after):
        self.half_done, self.late, early = {}, [], []
        started = [after]
        for g, (tag, items, sums, state, token) in enumerate(self.groups):
            kinds, widths, colblocks = [it[1] for it in items], [it[2] for it in items], [it[3] for it in items]
            if state is None:
                copies = _chip_copies(kinds, widths, colblocks)
                state, token = _split_start("grad_chip_start_" + tag, copies, 3 * len(items), sums,
                                            _chip_land_shapes(sums, kinds, widths), sums[-1])
                self.late.append((tag, items, copies, state, False))
                started.append(token)
            elif g == len(self.groups) - 1:
                self.late.append((tag, items, _direct_copies(kinds, widths, colblocks), state, True))
                started.append(token)
            else:
                early.append((tag, items, _direct_copies(kinds, widths, colblocks), state))
        for tag, items, copies, state in early:
            views, received = _split_wait("grad_direct_wait_" + tag, copies, state, started)
            self._sum_group(tag, items, views, received, True)
        late_names = {it[4] for _, items, _, _, _ in self.late for it in items}
        names = [n for n in BIG if n not in late_names]
        return dict(zip(names, _share_halves([self.half_done[n] for n in names], "grad_share_halves_first")))

    def finish_rest(self, after):
        names = []
        for tag, items, copies, state, direct in self.late:
            sums, received = _split_wait("grad_late_wait_" + tag, copies, state, after)
            self._sum_group(tag, items, sums, received, direct)
            names += [it[4] for it in items if it[4] not in names]
        return dict(zip(names, _share_halves([self.half_done[n] for n in names], "grad_share_halves_rest")))


def _update(reducer, grad_x, loss, grad_ln_g, grad_ln_b, grad_sinks, ws, ms, vs, small_w, small_m, small_v):
    ln_g, ln_b, b_sinks = small_w
    m_ln_g, m_ln_b, m_b_sinks = small_m
    v_ln_g, v_ln_b, v_b_sinks = small_v

    grads, deltas, new_m, new_v = {}, {}, {}, {}

    def update(some):
        done = []
        for name in some:
            shp = ws[name].shape
            flat = lambda a: a.reshape(-1, shp[-1])
            d, nm, nv, g = _adamw(flat(ws[name]), flat(some[name]), flat(ms[name]), flat(vs[name]), "adamw_" + name)
            grads[name], deltas[name], new_m[name], new_v[name] = g.reshape(shp), d.reshape(shp), nm.reshape(shp), nv.reshape(shp)
            done.append(d)
        return done

    rest = reducer.finish_rest(update(reducer.finish_first(grad_x)))
    update(rest)
    delta_s, nm_s, nv_s, _ = _adamw(_pack_small(ln_g, ln_b, b_sinks), _pack_small(grad_ln_g, grad_ln_b, grad_sinks),
                                    _pack_small(m_ln_g, m_ln_b, m_b_sinks), _pack_small(v_ln_g, v_ln_b, v_b_sinks), "adamw_small")
    for d, blob in ((grads, None), (deltas, delta_s), (new_m, nm_s), (new_v, nv_s)):
        if blob is None:
            d["ln_g"], d["ln_b"], d["b_sinks"] = grad_ln_g, grad_ln_b, grad_sinks
        else:
            d["ln_g"], d["ln_b"], d["b_sinks"] = _unpack_small(blob, ln_g.shape, b_sinks.shape)

    order = ("ffn1_w_in", "ffn1_w_out", "ffn2_w_in", "ffn2_w_out", "ln_g", "ln_b", "a_w_qkv", "a_w_o", "kv_w", "b_w_q",
             "b_sinks", "b_w_o")
    outs = [loss, grad_x[None]]
    for d in (grads, deltas, new_m, new_v):
        outs += [d[n] for n in order]
    return tuple(outs)
```

```python
import numpy as np
import jax
import jax.numpy as jnp
from jax import lax
from jax.experimental import pallas as pl
from jax.experimental.pallas import tpu as pltpu

F32 = jnp.float32
BF16 = jnp.bfloat16

D_MODEL = 1024
D_FF = 2816
HALF_FF = D_FF // 2
HEAD_DIM = 64
N_HEADS = 16
N_KV_B = 4
GROUP_B = N_HEADS // N_KV_B
DEPTH = 2
ALPHA = (2.0 * DEPTH) ** 0.25
LN_EPS = 1e-5
BLOCK = 128
SLAB = 128
N_SLABS = D_MODEL // SLAB
PATTERNS_A = ((1, 128, 1.0), (4, 128, 4.0), (16, 128, 16.0))
PATTERNS_B = ((1, 127, 1.0),)
NEG = -1e30

ADAM_LR = 0.001
ADAM_B1 = 0.9
ADAM_B2 = 0.999
ADAM_EPS = 1e-08
ADAM_WD = 0.01
ADAM_STEP = 10

N_CHIPS = 4
VMEM_LIMIT = 56 * 1024 * 1024
WHOLE_WEIGHT_BYTES = 12 * 1024 * 1024
MESH = pl.DeviceIdType.MESH


def _alibi_slopes(n):
    return np.array([2.0 ** (-8.0 * (h + 1) / n) for h in range(n)], dtype=np.float32)


def _cparams(sem=None, vmem=VMEM_LIMIT):
    return pltpu.CompilerParams(dimension_semantics=sem, vmem_limit_bytes=vmem)


_DIMS = {"nn": ((1,), (0,)), "nt": ((1,), (1,)), "tn": ((0,), (0,))}


def _unlead(x):
    if isinstance(x, tuple):
        return x[0], x[1], x[0].shape[1:]
    return x, None, x.shape


def _bspec(block, imap, lead=None, **kw):
    if lead is None:
        return pl.BlockSpec(block, imap, **kw)
    return pl.BlockSpec((None,) + tuple(block), lambda *g: (lead,) + tuple(imap(*g)), **kw)


def _ln_bwd_math(zv, dyv, gain):
    rows = zv.shape[0]
    mu = jnp.mean(zv, axis=-1, keepdims=True)
    zc = zv - mu
    var = jnp.mean(zc * zc, axis=-1, keepdims=True)
    rstd = lax.rsqrt(var + LN_EPS)
    xhat = zc * rstd
    dyg = dyv * gain
    m1 = jnp.mean(dyg, axis=-1, keepdims=True)
    m2 = jnp.mean(dyg * xhat, axis=-1, keepdims=True)
    dz = rstd * (dyg - m1 - xhat * m2)
    pg = jnp.sum((dyv * xhat).reshape(rows // 8, 8, D_MODEL), axis=0)
    pb = jnp.sum(dyv.reshape(rows // 8, 8, D_MODEL), axis=0)
    return dz, pg, pb


def _matmul(a, b, mode, out_dtype, tm, tn, tk, name, add=None, add_scale=1.0, split=False, into=None, ln=None):
    out_spec = pl.BlockSpec((tm, tn), lambda i, j, k: (i, j))
    base, count = (0, 3) if split is True else (split or (0, 0))
    if mode == "nn":
        a, al, (M, K) = _unlead(a)
        b, bl, (K2, N) = _unlead(b)
        a_spec = _bspec((tm, tk), lambda i, j, k: (i, k), al)
        b_spec = _bspec((tk, tn), lambda i, j, k: (k, j), bl)
        out_struct = jax.ShapeDtypeStruct((M, N), out_dtype)
        if split:
            assert tn == D_MODEL and N == count * tn
            out_spec = pl.BlockSpec((None, tm, tn), lambda i, j, k: (j + base, i, 0))
            out_struct = jax.ShapeDtypeStruct((3, M, tn), out_dtype)
    elif mode == "nt":
        b, bl, (N, K2) = _unlead(b)
        if split:
            M, K = a.shape[1], count * a.shape[2]
            if tk == K:
                a_spec = [pl.BlockSpec((None, tm, D_MODEL), lambda i, j, k, s=s: (s + base, i, 0)) for s in range(count)]
            else:
                assert tk == D_MODEL
                a_spec = pl.BlockSpec((None, tm, tk), lambda i, j, k: (k + base, i, 0))
        else:
            a, al, (M, K) = _unlead(a)
            a_spec = _bspec((tm, tk), lambda i, j, k: (i, k), al)
        whole_b = {"pipeline_mode": pl.Buffered(1)} if (tn, tk) == (N, K2) else {}
        b_spec = _bspec((tn, tk), lambda i, j, k: (j, k), bl, **whole_b)
        out_struct = jax.ShapeDtypeStruct((M, N), out_dtype)
    else:
        a, al, (K, M) = _unlead(a)
        if split:
            assert tn == D_MODEL
            K2, N = b.shape[1], count * b.shape[2]
            b_spec = pl.BlockSpec((None, tk, tn), lambda i, j, k: (j + base, k, 0))
        else:
            b, bl, (K2, N) = _unlead(b)
            b_spec = _bspec((tk, tn), lambda i, j, k: (k, j), bl)
        a_spec = _bspec((tk, tm), lambda i, j, k: (k, i), al)
        out_struct = jax.ShapeDtypeStruct((M, N), out_dtype)
    assert K == K2 and M % tm == 0 and N % tn == 0 and K % tk == 0, (a.shape, b.shape, mode, tm, tn, tk)
    nk = K // tk
    dims = (_DIMS[mode], ((), ()))
    has_add = add is not None

    narrow = out_dtype != F32
    assert not (narrow and has_add)
    if ln is not None:
        assert has_add and mode == "nt" and tn == N == D_MODEL

    a_specs = a_spec if isinstance(a_spec, list) else [a_spec]
    n_a = len(a_specs)

    def body(*refs):
        a_refs, refs = refs[:n_a], refs[n_a - 1:]
        if into is not None:
            refs = refs[:2] + refs[3:]
        if ln is not None:
            a_ref, b_ref, add_ref, z_ref, g_ref, o_ref, dzc_ref, gg_ref, gb_ref = refs
            acc_ref = o_ref
        elif has_add:
            a_ref, b_ref, add_ref, o_ref = refs
            acc_ref = o_ref
        elif narrow:
            a_ref, b_ref, o_ref, acc_ref = refs
        else:
            a_ref, b_ref, o_ref = refs
            acc_ref = o_ref
        k = pl.program_id(2)
        if n_a == 1:
            part = lax.dot_general(a_ref[...].astype(BF16), b_ref[...].astype(BF16), dims, preferred_element_type=F32)
        else:
            part = sum(lax.dot_general(r[...], b_ref[:, s * D_MODEL:(s + 1) * D_MODEL], dims, preferred_element_type=F32)
                       for s, r in enumerate(a_refs))
        if has_add:
            @pl.when(k == 0)
            def _():
                acc_ref[...] = part + add_scale * add_ref[...]
        else:
            @pl.when(k == 0)
            def _():
                acc_ref[...] = part

        @pl.when(k > 0)
        def _():
            acc_ref[...] += part

        if narrow:
            @pl.when(k == nk - 1)
            def _():
                o_ref[...] = acc_ref[...].astype(out_dtype)

        if ln is not None:
            @pl.when(k == nk - 1)
            def _():
                dz, pg, pb = _ln_bwd_math(z_ref[...], o_ref[...], g_ref[...])
                o_ref[...] = dz
                dzc_ref[...] = (ln[2] * dz).astype(BF16)
                first = pl.program_id(0) == 0

                @pl.when(first)
                def _():
                    gg_ref[...] = pg
                    gb_ref[...] = pb

                @pl.when(jnp.logical_not(first))
                def _():
                    gg_ref[...] += pg
                    gb_ref[...] += pb

    in_specs = [*a_specs, b_spec]
    args = [a] * n_a + [b]
    aliases = {}
    if into is not None:
        assert mode == "nn" and split and not has_add and n_a == 1
        in_specs.append(pl.BlockSpec(memory_space=pl.ANY))
        args.append(into)
        aliases = {2: 0}
    if has_add:
        in_specs.append(pl.BlockSpec((tm, tn), lambda i, j, k: (i, j)))
        args.append(add)
    sem = ("parallel", "parallel", "arbitrary")
    if ln is not None:
        part8 = pl.BlockSpec((8, N), lambda i, j, k: (0, 0))
        in_specs += [pl.BlockSpec((tm, tn), lambda i, j, k: (i, j)), pl.BlockSpec((1, N), lambda i, j, k: (0, 0))]
        args += [ln[0], ln[1]]
        out_spec = [out_spec, pl.BlockSpec((tm, tn), lambda i, j, k: (i, j)), part8, part8]
        out_struct = [out_struct, jax.ShapeDtypeStruct((M, N), BF16), jax.ShapeDtypeStruct((8, N), F32),
                      jax.ShapeDtypeStruct((8, N), F32)]
        sem = ("arbitrary", "arbitrary", "arbitrary")
    return pl.pallas_call(
        body, name=name, grid=(M // tm, N // tn, nk),
        in_specs=in_specs, out_specs=out_spec, out_shape=out_struct, input_output_aliases=aliases,
        scratch_shapes=[pltpu.VMEM((tm, tn), F32)] if narrow else [],
        compiler_params=_cparams(sem),
    )(*args)


def _pick(n, cands):
    for c in cands:
        if n % c == 0:
            return c
    raise ValueError((n, cands))


def _mm_nn(a, b, out_dtype, name, split=False, into=None):
    M, K = _unlead(a)[2]
    N = _unlead(b)[2][1]
    return _matmul(a, b, "nn", out_dtype, _pick(M, (1024, 512, 256)), _pick(N, (1024, 512)), _pick(K, (1024, 512)), name,
                   split=split, into=into)


def _mm_nt(a, b, name, add=None, add_scale=1.0, split=False, ln=None):
    M = a.shape[1] if split else _unlead(a)[2][0]
    N, K = _unlead(b)[2]
    tn = _pick(N, (1024, 512))
    if tn == N and N * K * 2 <= WHOLE_WEIGHT_BYTES:
        tm, tk = _pick(M, (512, 256)), K
    else:
        tms = (512, 256) if ln is not None else (1024, 512, 256)
        tm, tk = _pick(M, tms), _pick(D_MODEL if split else K, (2816, 1024, 512))
    return _matmul(a, b, "nt", F32, tm, tn, tk, name, add=add, add_scale=add_scale, split=split, ln=ln)


def _mm_tn(a, b, name, split=False, out_dtype=F32):
    K, M = _unlead(a)[2]
    N = D_MODEL if split else _unlead(b)[2][1]
    return _matmul(a, b, "tn", out_dtype, _pick(M, (1024, 1408, 512)), _pick(N, (1408, 1024, 512)),
                   _pick(K, (2048, 1024, 512, 256)), name, split=split)


def _d_kv_w(y, dqkv, name):
    S = y.shape[0]
    tk = _pick(S, (1024, 512))
    nk = S // tk
    width = N_KV_B * HEAD_DIM
    r, c = np.arange(D_MODEL)[:, None], np.arange(width)[None, :]
    fold = jnp.asarray((r // (GROUP_B * HEAD_DIM) == c // HEAD_DIM) & (r % HEAD_DIM == c % HEAD_DIM), BF16)

    def body(y_ref, dk_ref, dv_ref, f_ref, o_ref, acc_ref):
        k = pl.program_id(0)
        summed = jnp.concatenate([jnp.dot(ref[...], f_ref[...], preferred_element_type=F32).astype(BF16)
                                  for ref in (dk_ref, dv_ref)], axis=1)
        part = lax.dot_general(summed, y_ref[...], (_DIMS["tn"], ((), ())), preferred_element_type=F32)

        @pl.when(k == 0)
        def _():
            acc_ref[...] = part

        @pl.when(k > 0)
        def _():
            acc_ref[...] += part

        @pl.when(k == nk - 1)
        def _():
            o_ref[...] = acc_ref[...].T.astype(BF16)

    return pl.pallas_call(
        body, name=name, grid=(nk,),
        in_specs=[pl.BlockSpec((tk, D_MODEL), lambda k: (k, 0)),
                  pl.BlockSpec((None, tk, D_MODEL), lambda k: (1, k, 0)),
                  pl.BlockSpec((None, tk, D_MODEL), lambda k: (2, k, 0)),
                  pl.BlockSpec((D_MODEL, width), lambda k: (0, 0))],
        out_specs=pl.BlockSpec((D_MODEL, 2 * width), lambda k: (0, 0)),
        out_shape=jax.ShapeDtypeStruct((D_MODEL, 2 * width), BF16),
        scratch_shapes=[pltpu.VMEM((2 * width, D_MODEL), F32)],
        compiler_params=_cparams(("arbitrary",)),
    )(y, dqkv, dqkv, fold)


def _ffn_in(x, w, name):
    S = x.shape[0]
    tm = _pick(S, (512, 256))
    w, wl, _ = _unlead(w)

    def body(x_ref, w_ref, t_ref, h_ref):
        acc = jnp.dot(x_ref[...].astype(BF16), w_ref[...], preferred_element_type=F32)
        g = acc[:, :HALF_FF]
        up = acc[:, HALF_FF:]
        sg = jax.nn.sigmoid(g)
        silu = g * sg
        t_ref[:, :HALF_FF] = (up * (sg * (1.0 + g * (1.0 - sg)))).astype(BF16)
        t_ref[:, HALF_FF:] = silu.astype(BF16)
        h_ref[...] = (silu * up).astype(BF16)

    return pl.pallas_call(
        body, name=name, grid=(2, S // tm),
        in_specs=[pl.BlockSpec((tm, D_MODEL), lambda j, i: (i, 0)),
                  _bspec((D_MODEL, D_FF), lambda j, i: (0, j), wl)],
        out_specs=[pl.BlockSpec((tm, D_FF), lambda j, i: (i, j)),
                   pl.BlockSpec((tm, HALF_FF), lambda j, i: (i, j))],
        out_shape=[jax.ShapeDtypeStruct((S, 2 * D_FF), BF16), jax.ShapeDtypeStruct((S, D_FF), BF16)],
        compiler_params=_cparams(("parallel", "parallel")),
    )(x, w)


def _ffn_bwd_h(dzc, w_out, u, name):
    S = dzc.shape[0]
    tm = _pick(S, (512, 256))
    w_out, wl, _ = _unlead(w_out)

    def body(dz_ref, w_ref, t_ref, du_ref):
        dh = lax.dot_general(dz_ref[...], w_ref[...], (((1,), (1,)), ((), ())), preferred_element_type=F32)
        du_ref[:, :HALF_FF] = (dh * t_ref[:, :HALF_FF].astype(F32)).astype(BF16)
        du_ref[:, HALF_FF:] = (dh * t_ref[:, HALF_FF:].astype(F32)).astype(BF16)

    return pl.pallas_call(
        body, name=name, grid=(2, S // tm),
        in_specs=[pl.BlockSpec((tm, D_MODEL), lambda j, i: (i, 0)),
                  _bspec((HALF_FF, D_MODEL), lambda j, i: (j, 0), wl),
                  pl.BlockSpec((tm, D_FF), lambda j, i: (i, j))],
        out_specs=pl.BlockSpec((tm, D_FF), lambda j, i: (i, j)),
        out_shape=jax.ShapeDtypeStruct((S, 2 * D_FF), BF16),
        compiler_params=_cparams(("parallel", "parallel")),
    )(dzc, w_out, u)


def _mm_ln(a, w, resid, gain, bias, c, name):
    S, K = a.shape
    tm = _pick(S, (512, 256))
    w, wl, _ = _unlead(w)

    def body(a_ref, w_ref, r_ref, g_ref, b_ref, y_ref, yb_ref, z_ref):
        z = ALPHA * r_ref[...] + c * jnp.dot(a_ref[...], w_ref[...], preferred_element_type=F32)
        mu = jnp.mean(z, axis=-1, keepdims=True)
        zc = z - mu
        var = jnp.mean(zc * zc, axis=-1, keepdims=True)
        y = zc * lax.rsqrt(var + LN_EPS) * g_ref[...] + b_ref[...]
        z_ref[...] = z
        y_ref[...] = y
        yb_ref[...] = y.astype(BF16)

    row = pl.BlockSpec((tm, D_MODEL), lambda i: (i, 0))
    vec = pl.BlockSpec((1, D_MODEL), lambda i: (0, 0))
    return pl.pallas_call(
        body, name=name, grid=(S // tm,),
        in_specs=[pl.BlockSpec((tm, K), lambda i: (i, 0)), _bspec((K, D_MODEL), lambda i: (0, 0), wl), row, vec, vec],
        out_specs=[row, row, row],
        out_shape=[jax.ShapeDtypeStruct((S, D_MODEL), F32), jax.ShapeDtypeStruct((S, D_MODEL), BF16),
                   jax.ShapeDtypeStruct((S, D_MODEL), F32)],
        compiler_params=_cparams(("parallel",)),
    )(a, w, resid, gain, bias)


def _loss_ln_bwd(y, t, z, gain, c, name):
    S = y.shape[0]
    tm = _pick(S, (512, 256))

    def body(y_ref, t_ref, z_ref, g_ref, dz_ref, dzc_ref, gg_ref, gb_ref, sq_ref):
        i = pl.program_id(0)
        e = y_ref[...] - t_ref[...]
        dz, pg, pb = _ln_bwd_math(z_ref[...], e * (1.0 / D_MODEL), g_ref[...])
        dz_ref[...] = dz
        dzc_ref[...] = (c * dz).astype(BF16)
        ps = jnp.sum((e * e).reshape(tm // 8, 8, D_MODEL), axis=0)

        @pl.when(i == 0)
        def _():
            gg_ref[...] = pg
            gb_ref[...] = pb
            sq_ref[...] = ps

        @pl.when(i > 0)
        def _():
            gg_ref[...] += pg
            gb_ref[...] += pb
            sq_ref[...] += ps

    row = pl.BlockSpec((tm, D_MODEL), lambda i: (i, 0))
    part = pl.BlockSpec((8, D_MODEL), lambda i: (0, 0))
    part_shape = jax.ShapeDtypeStruct((8, D_MODEL), F32)
    return pl.pallas_call(
        body, name=name, grid=(S // tm,),
        in_specs=[row, row, row, pl.BlockSpec((1, D_MODEL), lambda i: (0, 0))],
        out_specs=[row, row, part, part, part],
        out_shape=[jax.ShapeDtypeStruct((S, D_MODEL), F32), jax.ShapeDtypeStruct((S, D_MODEL), BF16),
                   part_shape, part_shape, part_shape],
        compiler_params=_cparams(("arbitrary",)),
    )(y, t, z, gain)


def _rows(start, d):
    if d == 1:
        return pl.ds(pl.multiple_of(start, BLOCK), BLOCK)
    return pl.ds(start, BLOCK, stride=d)


def _ld(ref, start, d):
    return ref[_rows(start, d), :]


def _ld3(ref, lead, start, d):
    return ref[lead, _rows(start, d), :]


def _st3(ref, lead, start, d, val):
    ref[lead, _rows(start, d), :] = val


def _acc3(ref, lead, start, d, val):
    ref[lead, _rows(start, d), :] = ref[lead, _rows(start, d), :] + val


def _band_consts(slope0, slope1, maxd, scale):
    row = lax.broadcasted_iota(jnp.int32, (2 * BLOCK, 2 * BLOCK), 0)
    kj = lax.broadcasted_iota(jnp.int32, (2 * BLOCK, 2 * BLOCK), 1)
    top = row < BLOCK
    dist = BLOCK + jnp.where(top, row, row - BLOCK) - kj
    slope = jnp.where(top, slope0, slope1)
    base = jnp.where((dist >= 0) & (dist <= maxd), -(slope * (dist.astype(F32) * scale)), NEG)
    return base, kj < BLOCK


def _stack_heads(x, lo):
    return jnp.concatenate([jnp.where(lo, x, 0.0), jnp.where(lo, 0.0, x)], axis=0)


def _unstack_heads(x2, lo):
    return jnp.where(lo, x2[:BLOCK], x2[BLOCK:])


def _scores(q2, k2, base, prev_keys, first):
    s = lax.dot_general(q2, k2, (((1,), (1,)), ((), ())), preferred_element_type=F32) * (HEAD_DIM ** -0.5) + base
    return jnp.where(jnp.logical_and(prev_keys, first), NEG, s)


def _softmax_weights(ls):
    mx = ls[0]
    for l in ls[1:]:
        mx = jnp.maximum(mx, l)
    es = [jnp.exp(l - mx) for l in ls]
    tot = es[0]
    for e in es[1:]:
        tot = tot + e
    inv = 1.0 / tot
    return [e * inv for e in es]


def _attn_fwd(qkv, slopes, sinks, patterns, name):
    S = qkv.shape[1]
    npat = len(patterns)
    has_sink = sinks is not None
    if not has_sink:
        sinks = jnp.zeros((N_HEADS,), F32)
    rows_c = 256

    def body(slopes_ref, sinks_ref, x_ref, mix_ref, o_ref, lse_ref, o_scr, lse_scr):
        p = pl.program_id(0)
        lo = lax.broadcasted_iota(jnp.int32, (BLOCK, SLAB), 1) < HEAD_DIM
        top1 = lax.broadcasted_iota(jnp.int32, (2 * BLOCK, 1), 0) < BLOCK
        sk2 = jnp.where(top1, sinks_ref[2 * p], sinks_ref[2 * p + 1])
        for pi, (d, maxd, scale) in enumerate(patterns):
            nb = S // d // BLOCK
            base, prev_keys = _band_consts(slopes_ref[2 * p], slopes_ref[2 * p + 1], maxd, scale)

            def blk(t, carry, pi=pi, d=d, nb=nb, base=base, prev_keys=prev_keys):
                r = t // nb
                n = t - r * nb
                start = r + (d * BLOCK) * n
                prev = jnp.where(n > 0, start - d * BLOCK, start)
                q2 = _stack_heads(_ld3(x_ref, 0, start, d), lo).astype(BF16)
                k2 = jnp.concatenate([_ld3(x_ref, 1, prev, d), _ld3(x_ref, 1, start, d)], axis=0).astype(BF16)
                v2 = jnp.concatenate([_ld3(x_ref, 2, prev, d), _ld3(x_ref, 2, start, d)], axis=0).astype(BF16)
                s = _scores(q2, k2, base, prev_keys, n == 0)
                m = jnp.max(s, axis=-1, keepdims=True)
                if has_sink:
                    m = jnp.maximum(m, sk2)
                e = jnp.exp(s - m)
                den = jnp.sum(e, axis=-1, keepdims=True)
                if has_sink:
                    den = den + jnp.exp(sk2 - m)
                o2 = jnp.dot((e / den).astype(BF16), v2, preferred_element_type=F32)
                _st3(o_scr, pi, start, d, _unstack_heads(o2, lo))
                _st3(lse_scr, pi, start, d, _unstack_heads(m + jnp.log(den), lo))
                return carry

            lax.fori_loop(0, d * nb, blk, 0, unroll=8)

        lane_c = lax.broadcasted_iota(jnp.int32, (rows_c, SLAB), 1)

        def comb(ci, carry):
            rows = pl.ds(pl.multiple_of(ci * rows_c, rows_c), rows_c)
            ls = [lse_scr[i, rows, :] for i in range(npat)]
            packed = jnp.zeros((rows_c, SLAB), F32)
            for i in range(npat):
                o_ref[i, rows, :] = o_scr[i, rows, :].astype(BF16)
                packed = jnp.where(lane_c % HEAD_DIM == i, ls[i], packed)
            lse_ref[rows, :] = packed
            if npat == 1:
                mix_ref[rows, :] = o_scr[0, rows, :].astype(BF16)
            else:
                ws = _softmax_weights(ls)
                acc = ws[0] * o_scr[0, rows, :]
                for i in range(1, npat):
                    acc = acc + ws[i] * o_scr[i, rows, :]
                mix_ref[rows, :] = acc.astype(BF16)
            return carry

        lax.fori_loop(0, S // rows_c, comb, 0, unroll=2)

    smem = pl.BlockSpec(memory_space=pltpu.SMEM)
    return pl.pallas_call(
        body, name=name, grid=(N_SLABS,),
        in_specs=[smem, smem, pl.BlockSpec((3, S, SLAB), lambda p: (0, 0, p))],
        out_specs=[pl.BlockSpec((S, SLAB), lambda p: (0, p)), pl.BlockSpec((npat, S, SLAB), lambda p: (0, 0, p)),
                   pl.BlockSpec((None, S, SLAB), lambda p: (p, 0, 0))],
        out_shape=[jax.ShapeDtypeStruct((S, D_MODEL), BF16), jax.ShapeDtypeStruct((npat, S, D_MODEL), BF16),
                   jax.ShapeDtypeStruct((N_SLABS, S, SLAB), F32)],
        scratch_shapes=[pltpu.VMEM((npat, S, SLAB), F32), pltpu.VMEM((npat, S, SLAB), F32)],
        compiler_params=_cparams(("arbitrary",)),
    )(slopes, sinks, qkv)


def _attn_bwd(qkv, dout, o, lse, slopes, sinks, patterns, name):
    S = qkv.shape[1]
    npat = len(patterns)
    has_sink = sinks is not None
    if not has_sink:
        sinks = jnp.zeros((N_HEADS,), F32)
    rows_c = 256

    def headsum(x, lo):
        same = (lax.broadcasted_iota(jnp.int32, (SLAB, SLAB), 0) < HEAD_DIM) == (lax.broadcasted_iota(jnp.int32, (SLAB, SLAB), 1) < HEAD_DIM)
        return jnp.dot(x, same.astype(F32), precision=lax.Precision.HIGH, preferred_element_type=F32)

    def body(slopes_ref, sinks_ref, x_ref, do_ref, o_ref, lsep_ref, dxo_ref, dsink_ref, dbar_ref, sacc_ref, lse_ref, dx_ref):
        p = pl.program_id(0)
        lo = lax.broadcasted_iota(jnp.int32, (BLOCK, SLAB), 1) < HEAD_DIM
        lo_c = lax.broadcasted_iota(jnp.int32, (rows_c, SLAB), 1) < HEAD_DIM
        top1 = lax.broadcasted_iota(jnp.int32, (2 * BLOCK, 1), 0) < BLOCK
        sk2 = jnp.where(top1, sinks_ref[2 * p], sinks_ref[2 * p + 1])

        def prep(ci, carry):
            rows = pl.ds(pl.multiple_of(ci * rows_c, rows_c), rows_c)
            dov = do_ref[rows, :]
            dx_ref[:, rows, :] = jnp.zeros((3, rows_c, SLAB), F32)
            packed = lsep_ref[rows, :]
            ls = [jnp.where(lo_c, packed[:, i:i + 1], packed[:, HEAD_DIM + i:HEAD_DIM + i + 1]) for i in range(npat)]
            for i in range(npat):
                lse_ref[i, rows, :] = ls[i]
            if npat == 1:
                dbar_ref[rows, :] = headsum(dov * o_ref[0, rows, :].astype(F32), lo_c)
            else:
                ws = _softmax_weights(ls)
                acc = ws[0] * headsum(dov * o_ref[0, rows, :].astype(F32), lo_c)
                for i in range(1, npat):
                    acc = acc + ws[i] * headsum(dov * o_ref[i, rows, :].astype(F32), lo_c)
                dbar_ref[rows, :] = acc
            return carry

        lax.fori_loop(0, S // rows_c, prep, 0, unroll=2)
        sacc_ref[...] = jnp.zeros((BLOCK, SLAB), F32)

        for pi, (d, maxd, scale) in enumerate(patterns):
            nb = S // d // BLOCK
            base, prev_keys = _band_consts(slopes_ref[2 * p], slopes_ref[2 * p + 1], maxd, scale)

            def blk(t, carry, pi=pi, d=d, nb=nb, base=base, prev_keys=prev_keys):
                r = t // nb
                n = t - r * nb
                start = r + (d * BLOCK) * n
                prev = jnp.where(n > 0, start - d * BLOCK, start)
                q2 = _stack_heads(_ld3(x_ref, 0, start, d), lo).astype(BF16)
                k2 = jnp.concatenate([_ld3(x_ref, 1, prev, d), _ld3(x_ref, 1, start, d)], axis=0).astype(BF16)
                v2 = jnp.concatenate([_ld3(x_ref, 2, prev, d), _ld3(x_ref, 2, start, d)], axis=0).astype(BF16)
                ls = [_ld3(lse_ref, i, start, d) for i in range(npat)]
                w = _softmax_weights(ls)[pi] if npat > 1 else 1.0
                do2 = _stack_heads(w * _ld(do_ref, start, d), lo).astype(BF16)
                dl = w * _ld(dbar_ref, start, d)
                lse2 = jnp.concatenate([ls[pi][:, :1], ls[pi][:, HEAD_DIM:HEAD_DIM + 1]], axis=0)
                dl2 = jnp.concatenate([dl[:, :1], dl[:, HEAD_DIM:HEAD_DIM + 1]], axis=0)
                s = _scores(q2, k2, base, prev_keys, n == 0)
                pr = jnp.exp(s - lse2)
                dp = lax.dot_general(do2, v2, (((1,), (1,)), ((), ())), preferred_element_type=F32)
                ds = (pr * (dp - dl2) * (HEAD_DIM ** -0.5)).astype(BF16)
                dq2 = jnp.dot(ds, k2, preferred_element_type=F32)
                dk2 = lax.dot_general(ds, q2, (((0,), (0,)), ((), ())), preferred_element_type=F32)
                dv2 = lax.dot_general(pr.astype(BF16), do2, (((0,), (0,)), ((), ())), preferred_element_type=F32)
                _acc3(dx_ref, 0, start, d, _unstack_heads(dq2, lo))
                _acc3(dx_ref, 1, prev, d, dk2[:BLOCK])
                _acc3(dx_ref, 1, start, d, dk2[BLOCK:])
                _acc3(dx_ref, 2, prev, d, dv2[:BLOCK])
                _acc3(dx_ref, 2, start, d, dv2[BLOCK:])
                if has_sink:
                    sacc_ref[...] += _unstack_heads(-jnp.exp(sk2 - lse2) * dl2, lo)
                return carry

            lax.fori_loop(0, d * nb, blk, 0, unroll=8)

        dsink_ref[...] = jnp.broadcast_to(jnp.sum(sacc_ref[...], axis=0, keepdims=True), (8, SLAB))

        def emit(ci, carry):
            rows = pl.ds(pl.multiple_of(ci * rows_c, rows_c), rows_c)
            dxo_ref[:, rows, :] = dx_ref[:, rows, :].astype(BF16)
            return carry

        lax.fori_loop(0, S // rows_c, emit, 0, unroll=2)

    smem = pl.BlockSpec(memory_space=pltpu.SMEM)
    return pl.pallas_call(
        body, name=name, grid=(N_SLABS,),
        in_specs=[smem, smem, pl.BlockSpec((3, S, SLAB), lambda p: (0, 0, p)), pl.BlockSpec((S, SLAB), lambda p: (0, p)),
                  pl.BlockSpec((npat, S, SLAB), lambda p: (0, 0, p)), pl.BlockSpec((None, S, SLAB), lambda p: (p, 0, 0))],
        out_specs=[pl.BlockSpec((3, S, SLAB), lambda p: (0, 0, p)), pl.BlockSpec((None, 8, SLAB), lambda p: (p, 0, 0))],
        out_shape=[jax.ShapeDtypeStruct((3, S, D_MODEL), BF16), jax.ShapeDtypeStruct((N_SLABS, 8, SLAB), F32)],
        scratch_shapes=[pltpu.VMEM((S, SLAB), F32), pltpu.VMEM((BLOCK, SLAB), F32), pltpu.VMEM((npat, S, SLAB), F32),
                        pltpu.VMEM((3, S, SLAB), F32)],
        compiler_params=_cparams(("arbitrary",)),
    )(slopes, sinks, qkv, dout, o, lse)


def _place():
    x, y, c = lax.axis_index("x"), lax.axis_index("y"), lax.axis_index("c")
    return x, y, c, 2 * x + y


def _other_chips(x, y):
    return [(1 - x, y), (x, 1 - y), (1 - x, 1 - y)]


HBM_SPEC = pl.BlockSpec(memory_space=pl.ANY)


def _slot(q):
    return 2 * (q % 2) + q // 2


BIG = ("ffn1_w_in", "ffn1_w_out", "ffn2_w_in", "ffn2_w_out", "a_w_qkv", "a_w_o", "kv_w", "b_w_q", "b_w_o")
QKV_SHARD = 3 * D_MODEL // N_CHIPS
ROW_SHARD = D_MODEL // N_CHIPS


LAYER0_ITEMS = (("ffn1_w_in", 0), ("ffn1_w_out", 0), ("a_w_qkv", None), ("a_w_o", None), ("ffn2_w_in", 0),
                ("ffn2_w_out", 0), ("kv_w", None))
LAYER1_ITEMS = (("ffn1_w_in", 1), ("ffn1_w_out", 1), ("b_w_q", None), ("b_w_o", None), ("ffn2_w_in", 1),
                ("ffn2_w_out", 1))
OUT_SHARD = D_FF // N_CHIPS


def _full_shape(name):
    if name.endswith("w_in"):
        return (D_MODEL, 2 * D_FF)
    if name.endswith("w_out"):
        return (D_FF, D_MODEL)
    if name == "a_w_qkv":
        return (D_MODEL, 3 * D_MODEL)
    if name == "kv_w":
        return (N_CHIPS, 2, ROW_SHARD // 2, 2 * N_KV_B * HEAD_DIM)
    return (N_CHIPS, 2, ROW_SHARD // 2, D_MODEL)


def _gather_src(item, ref, c):
    name, layer = item
    if name.endswith("w_in"):
        return ref.at[layer, pl.ds(c * (D_MODEL // 2), D_MODEL // 2)]
    if name.endswith("w_out"):
        return ref.at[layer, pl.ds(c * (OUT_SHARD // 2), OUT_SHARD // 2)]
    if name == "a_w_qkv":
        return ref.at[0, pl.ds(c * (D_MODEL // 2), D_MODEL // 2)]
    if name == "kv_w":
        return ref.at[pl.ds(c * (ROW_SHARD // 2), ROW_SHARD // 2)]
    return ref.at[0, pl.ds(c * (ROW_SHARD // 2), ROW_SHARD // 2)]


def _gather_dst(item, ref, q, c):
    name, _ = item
    if name.endswith("w_in"):
        return ref.at[pl.ds(c * (D_MODEL // 2), D_MODEL // 2), pl.ds(_slot(q) * HALF_FF, HALF_FF)]
    if name.endswith("w_out"):
        return ref.at[pl.ds(q * OUT_SHARD + c * (OUT_SHARD // 2), OUT_SHARD // 2)]
    if name == "a_w_qkv":
        return ref.at[pl.ds(c * (D_MODEL // 2), D_MODEL // 2), pl.ds(q * QKV_SHARD, QKV_SHARD)]
    return ref.at[q, c]


def _all_gather(items, shards, small):
    n = len(items)
    r = small.shape[0]
    per = 8

    def body(*refs):
        srcs, small_ref = refs[:n], refs[n]
        dsts, s_ref = refs[n + 1:2 * n + 1], refs[2 * n + 1]
        send_sems, recv_sems = refs[2 * n + 2:]
        x, y, c, myq = _place()
        sibling = (x, y, 1 - c)
        chips = _other_chips(x, y)

        def big(t, k, src, q, h, to):
            return pltpu.make_async_remote_copy(src_ref=src, dst_ref=_gather_dst(items[t], dsts[t], q, h),
                                                send_sem=send_sems.at[per * t + k], recv_sem=recv_sems.at[per * t + k],
                                                device_id=to, device_id_type=MESH)

        def tiny(k, q, to):
            return pltpu.make_async_remote_copy(src_ref=small_ref, dst_ref=s_ref.at[q], send_sem=send_sems.at[per * n + k],
                                                recv_sem=recv_sems.at[per * n + k], device_id=to, device_id_type=MESH)

        first = []
        for j, chip in enumerate(chips):
            if j < 2:
                first += [big(t, j, _gather_src(items[t], srcs[t], c), myq, c, (*chip, c)) for t in range(n)]
            first.append(tiny(j, myq, (*chip, c)))
        own = [big(t, 6 + h, _gather_src(items[t], srcs[t], h), myq, h, sibling) for t in range(n) for h in (0, 1)]
        own.append(tiny(3, myq, sibling))
        for cp in first + own:
            cp.start()
        relay_from = ((x + 1 - c) % 2, (y + c) % 2)
        relay_to = ((x + c) % 2, (y + 1 - c) % 2, c)
        q_relay = 2 * relay_from[0] + relay_from[1]
        passed = []
        for t in range(n):
            src = _gather_src(items[t], srcs[t], c)
            for j, (cx, cy) in enumerate(chips[:2]):
                q = 2 * cx + cy
                big(t, j, src, q, c, sibling).wait_recv()
                fwd = big(t, 3 + j, _gather_dst(items[t], dsts[t], q, c), q, c, sibling)
                fwd.start()
                passed.append(fwd)
            relay = big(t, 2, _gather_dst(items[t], dsts[t], q_relay, c), q_relay, c, relay_to)
            relay.start()
            passed.append(relay)
        q_diag = 2 * chips[2][0] + chips[2][1]
        for t in range(n):
            big(t, 2, _gather_src(items[t], srcs[t], c), q_diag, c, sibling).wait_recv()
            fwd = big(t, 5, _gather_dst(items[t], dsts[t], q_diag, c), q_diag, c, sibling)
            fwd.start()
            passed.append(fwd)
        for j, (cx, cy) in enumerate(chips):
            q = 2 * cx + cy
            for t in range(n):
                big(t, 3 + j, _gather_src(items[t], srcs[t], c), q, 1 - c, sibling).wait_recv()
            tiny(j, q, sibling).wait_recv()
        for cp in own:
            cp.wait_recv()
        for cp in first + passed + own:
            cp.wait_send()

    outs = pl.pallas_call(
        body, name="all_gather_layer0",
        in_specs=[HBM_SPEC] * (n + 1), out_specs=[HBM_SPEC] * (n + 1),
        out_shape=[jax.ShapeDtypeStruct(_full_shape(name), BF16) for name, _ in items]
        + [jax.ShapeDtypeStruct((N_CHIPS, r, 128), F32)],
        scratch_shapes=[pltpu.SemaphoreType.DMA((per * n + 4,)), pltpu.SemaphoreType.DMA((per * n + 4,))],
    )(*[shards[item] for item in items], small)
    return list(outs[:n]), outs[n]


SEM_SPEC = pl.BlockSpec(memory_space=pltpu.SEMAPHORE)
DATAFLOW = pltpu.SideEffectType.DATAFLOW_SIDE_EFFECTING
PER_ITEM = 8


def _split_start(name, copies, n_sems, sources, land_shapes, after):
    n, m = len(sources), len(land_shapes)

    def body(*refs):
        srcs, lands = refs[:n], refs[n:n + m]
        send_sems, recv_sems = refs[n + m + 1], refs[n + m + 2]
        token = refs[-1]
        for src, dst_there, _, s, peer in copies(srcs, lands):
            pltpu.make_async_remote_copy(src_ref=src, dst_ref=dst_there, send_sem=send_sems.at[s], recv_sem=recv_sems.at[s],
                                         device_id=peer, device_id_type=MESH).start()
        token[...] = jnp.zeros_like(token)

    src_arrays = [pltpu.with_memory_space_constraint(a, pltpu.HBM) for a in sources]
    land_arrays = [pltpu.with_memory_space_constraint(lax.empty(s.shape, s.dtype), pltpu.HBM) for s in land_shapes]
    hbm = pl.BlockSpec(memory_space=pltpu.HBM)
    outs = pl.pallas_call(
        body, name=name,
        in_specs=[hbm] * (n + m) + [HBM_SPEC],
        out_specs=[SEM_SPEC, SEM_SPEC] + [hbm] * (n + m) + [pl.BlockSpec(memory_space=pltpu.VMEM)],
        out_shape=[pltpu.SemaphoreType.DMA((n_sems,)), pltpu.SemaphoreType.DMA((n_sems,))]
        + [pltpu.HBM(a.shape, a.dtype) for a in src_arrays + land_arrays] + [jax.ShapeDtypeStruct((8, 128), F32)],
        input_output_aliases={i: 2 + i for i in range(n + m)},
        compiler_params=pltpu.CompilerParams(has_side_effects=DATAFLOW),
    )(*src_arrays, *land_arrays, after)
    return (outs[0], outs[1], list(outs[2:2 + n]), list(outs[2 + n:2 + n + m])), outs[-1]


def _split_wait(name, copies, state, after):
    send_sems, recv_sems, srcs_thru, lands_thru = state
    n, m = len(srcs_thru), len(lands_thru)
    after = list(after) if isinstance(after, (list, tuple)) else [after]

    def body(*refs):
        srcs, lands = refs[:n], refs[n:n + m]
        send_sems, recv_sems = refs[n + m], refs[n + m + 1]
        for src, _, dst_here, s, peer in copies(srcs, lands):
            cp = pltpu.make_async_remote_copy(src_ref=src, dst_ref=dst_here, send_sem=send_sems.at[s], recv_sem=recv_sems.at[s],
                                              device_id=peer, device_id_type=MESH)
            cp.wait_send()
            cp.wait_recv()

    hbm = pl.BlockSpec(memory_space=pltpu.HBM)
    outs = pl.pallas_call(
        body, name=name,
        in_specs=[hbm] * (n + m) + [SEM_SPEC, SEM_SPEC] + [HBM_SPEC] * len(after),
        out_specs=[hbm] * (n + m),
        out_shape=[pltpu.HBM(a.shape, a.dtype) for a in srcs_thru + lands_thru],
        input_output_aliases={i: i for i in range(n + m)},
        compiler_params=pltpu.CompilerParams(has_side_effects=DATAFLOW),
    )(*srcs_thru, *lands_thru, send_sems, recv_sems, *after)
    return list(outs[:n]), list(outs[n:])


def _gather_copies(items):
    def copies(srcs, lands):
        x, y, c, myq = _place()
        out = []
        for t, item in enumerate(items):
            for h in (0, 1):
                src = _gather_src(item, srcs[t], h)
                for j, (cx, cy) in enumerate(_other_chips(x, y)):
                    out.append((src, _gather_dst(item, lands[t], myq, h), _gather_dst(item, lands[t], 2 * cx + cy, h),
                                PER_ITEM * t + 2 * j + h, (cx, cy, c)))
                out.append((src, _gather_dst(item, lands[t], myq, h), _gather_dst(item, lands[t], myq, h),
                            PER_ITEM * t + 6 + h, (x, y, 1 - c)))
        return out
    return copies


def _gather_start(items, shards, after):
    lands = [jax.ShapeDtypeStruct(_full_shape(name), BF16) for name, _ in items]
    return _split_start("gather_layer1_start", _gather_copies(items), PER_ITEM * len(items),
                        [shards[item] for item in items], lands, after)


def _gather_wait(items, state, after):
    return _split_wait("gather_layer1_wait", _gather_copies(items), state, after)[1]


def _small_all_reduce(v):
    r = v.shape[0]

    def body(v_ref, o_ref, buf_ref, send_sems, recv_sems):
        x, y, c, _ = _place()
        me = 4 * x + 2 * y + c
        buf_ref[me] = v_ref[...]
        copies = []
        for k in range(1, 8):
            fx, fy, fc = (k >> 2) & 1, (k >> 1) & 1, k & 1
            to = (x ^ fx, y ^ fy, c ^ fc)
            cp = pltpu.make_async_remote_copy(src_ref=v_ref, dst_ref=buf_ref.at[me], send_sem=send_sems.at[k - 1],
                                              recv_sem=recv_sems.at[k - 1], device_id=to, device_id_type=MESH)
            cp.start()
            copies.append(cp)
        for k in range(1, 8):
            fx, fy, fc = (k >> 2) & 1, (k >> 1) & 1, k & 1
            src_dev = 4 * (x ^ fx) + 2 * (y ^ fy) + (c ^ fc)
            pltpu.make_async_remote_copy(src_ref=v_ref, dst_ref=buf_ref.at[src_dev], send_sem=send_sems.at[k - 1],
                                         recv_sem=recv_sems.at[k - 1], device_id=(x, y, c), device_id_type=MESH).wait_recv()
        for cp in copies:
            cp.wait_send()
        tot = buf_ref[0]
        for i in range(1, 8):
            tot = tot + buf_ref[i]
        o_ref[...] = tot

    vm = pl.BlockSpec(memory_space=pltpu.VMEM)
    return pl.pallas_call(
        body, name="small_all_reduce", in_specs=[vm], out_specs=vm,
        out_shape=jax.ShapeDtypeStruct((r, 128), F32),
        scratch_shapes=[pltpu.VMEM((8, r, 128), F32), pltpu.SemaphoreType.DMA((7,)), pltpu.SemaphoreType.DMA((7,))],
    )(v)


def _grad_view(kind, g):
    if kind == "col":
        return g.reshape(2, g.shape[0] // 2, g.shape[1])
    return g.reshape(N_CHIPS, 2, g.shape[0] // (2 * N_CHIPS), g.shape[1])


def _half_of(kind, ref, h):
    return ref.at[h] if kind == "col" else ref.at[:, h]


def _half_shape(kind, view_shape):
    return view_shape[1:] if kind == "col" else (view_shape[0],) + view_shape[2:]


def _piece_of(kind, width, colblock, ref, q):
    if kind == "col":
        return ref.at[:, pl.ds(colblock(q) * width, width)]
    return ref.at[q]


def _piece_shape(kind, width, half_shape):
    return (half_shape[0], width) if kind == "col" else half_shape[1:]


def _pair_exchange(views, kinds, name):
    n = len(views)

    def body(*refs):
        ins, outs = refs[:n], refs[n:2 * n]
        send_sems, recv_sems = refs[2 * n:]
        x, y, c, _ = _place()
        cps = []
        for t in range(n):
            cp = pltpu.make_async_remote_copy(src_ref=_half_of(kinds[t], ins[t], 1 - c), dst_ref=outs[t],
                                              send_sem=send_sems.at[t], recv_sem=recv_sems.at[t],
                                              device_id=(x, y, 1 - c), device_id_type=MESH)
            cp.start()
            cps.append(cp)
        for cp in cps:
            cp.wait()

    return pl.pallas_call(
        body, name=name, in_specs=[HBM_SPEC] * n, out_specs=[HBM_SPEC] * n,
        out_shape=[jax.ShapeDtypeStruct(_half_shape(k, v.shape), v.dtype) for k, v in zip(kinds, views)],
        scratch_shapes=[pltpu.SemaphoreType.DMA((n,)), pltpu.SemaphoreType.DMA((n,))],
    )(*views)


def _pair_sum(kind, view, recv, c, name):
    hs = recv.shape
    N = hs[-1]
    rows = hs[-2]
    tr = _pick(rows, (512, 352, 128))
    tn = _pick(N, (1408, 1024, 512))

    def body(c_ref, p_ref, r_ref, s_ref):
        s_ref[...] = (p_ref[...] + r_ref[...]).astype(BF16)

    if kind == "col":
        grid = (rows // tr, N // tn)
        mine = pl.BlockSpec((None, tr, tn), lambda i, j, c_ref: (c_ref[0], i, j))
        blk = pl.BlockSpec((tr, tn), lambda i, j, c_ref: (i, j))
        sem = ("parallel", "parallel")
    else:
        grid = (N_CHIPS, rows // tr, N // tn)
        mine = pl.BlockSpec((None, None, tr, tn), lambda q, i, j, c_ref: (q, c_ref[0], i, j))
        blk = pl.BlockSpec((None, tr, tn), lambda q, i, j, c_ref: (q, i, j))
        sem = ("parallel", "parallel", "parallel")
    return pl.pallas_call(
        body, name=name,
        grid_spec=pltpu.PrefetchScalarGridSpec(num_scalar_prefetch=1, grid=grid, in_specs=[mine, blk], out_specs=blk),
        out_shape=jax.ShapeDtypeStruct(hs, BF16),
        compiler_params=_cparams(sem),
    )(c.reshape(1).astype(jnp.int32), view, recv)


def _chip_copies(kinds, widths, colblocks):
    def copies(srcs, lands):
        x, y, c, _ = _place()
        out = []
        for j, (cx, cy) in enumerate(_other_chips(x, y)):
            for t in range(len(kinds)):
                out.append((_piece_of(kinds[t], widths[t], colblocks[t], srcs[t], 2 * cx + cy), lands[t].at[j],
                            lands[t].at[j], 3 * t + j, (cx, cy, c)))
        return out
    return copies


def _chip_land_shapes(sums, kinds, widths):
    return [jax.ShapeDtypeStruct((3,) + _piece_shape(k, w, s.shape), BF16) for k, w, s in zip(kinds, widths, sums)]


def _chip_exchange(sums, kinds, widths, colblocks, name):
    n = len(sums)
    copies = _chip_copies(kinds, widths, colblocks)

    def body(*refs):
        send_sems, recv_sems = refs[2 * n:]
        cps = [pltpu.make_async_remote_copy(src_ref=src, dst_ref=dst, send_sem=send_sems.at[s], recv_sem=recv_sems.at[s],
                                            device_id=peer, device_id_type=MESH)
               for src, dst, _, s, peer in copies(refs[:n], refs[n:2 * n])]
        for cp in cps:
            cp.start()
        for cp in cps:
            cp.wait()

    return pl.pallas_call(
        body, name=name, in_specs=[HBM_SPEC] * n, out_specs=[HBM_SPEC] * n,
        out_shape=_chip_land_shapes(sums, kinds, widths),
        scratch_shapes=[pltpu.SemaphoreType.DMA((3 * n,)), pltpu.SemaphoreType.DMA((3 * n,))],
    )(*sums)


N_DIRECT = 7


def _direct_piece(kind, width, colblock, view_ref, q, h):
    if kind == "col":
        return view_ref.at[h, :, pl.ds(colblock(q) * width, width)]
    return view_ref.at[q, h]


def _direct_copies(kinds, widths, colblocks):
    def copies(srcs, lands):
        x, y, c, myq = _place()
        out = []
        for t in range(len(kinds)):
            def piece(q, h, t=t):
                return _direct_piece(kinds[t], widths[t], colblocks[t], srcs[t], q, h)
            for j, (cx, cy) in enumerate(_other_chips(x, y)):
                for h in (0, 1):
                    out.append((piece(2 * cx + cy, h), lands[t].at[2 * j + c], lands[t].at[2 * j + h],
                                10 * t + 3 * j + c + h, (cx, cy, h)))
            out.append((piece(myq, 1 - c), lands[t].at[6], lands[t].at[6], 10 * t + 9, (x, y, 1 - c)))
        return out
    return copies


def _chip_sum(kind, own_src, recv, block_idx, c, shard_shape, layer, into, name, direct=False):
    n_recv, rows, N = recv.shape
    tr = _pick(rows, (512, 352, 128))
    tn = _pick(N, (1408, 1024, 768, 512))
    ni, nj = rows // tr, N // tn

    def body(q_ref, s_ref, r_ref, *rest):
        o_ref = rest[-1]
        tot = s_ref[...].astype(F32)
        for k in range(n_recv):
            tot = tot + r_ref[k].astype(F32)
        o_ref[...] = tot

    if direct and kind == "col":
        own = pl.BlockSpec((None, tr, tn), lambda i, j, q_ref: (q_ref[1], i, q_ref[0] * nj + j))
    elif direct:
        own = pl.BlockSpec((None, None, tr, tn), lambda i, j, q_ref: (q_ref[0], q_ref[1], i, j))
    elif kind == "col":
        own = pl.BlockSpec((tr, tn), lambda i, j, q_ref: (i, q_ref[0] * nj + j))
    else:
        own = pl.BlockSpec((None, tr, tn), lambda i, j, q_ref: (q_ref[0], i, j))
    if len(shard_shape) == 3:
        lead = 0 if layer is None else layer
        out_spec = pl.BlockSpec((None, tr, tn), lambda i, j, q_ref: (lead, q_ref[1] * ni + i, j))
    else:
        out_spec = pl.BlockSpec((tr, tn), lambda i, j, q_ref: (q_ref[1] * ni + i, j))
    in_specs = [own, pl.BlockSpec((n_recv, tr, tn), lambda i, j, q_ref: (0, i, j))]
    s = own_src
    args = [jnp.stack([block_idx, c]).astype(jnp.int32), s, recv]
    aliases = {}
    if into is not None:
        in_specs.append(HBM_SPEC)
        args.append(into)
        aliases = {3: 0}
    return pl.pallas_call(
        body, name=name,
        grid_spec=pltpu.PrefetchScalarGridSpec(num_scalar_prefetch=1, grid=(ni, nj), in_specs=in_specs, out_specs=out_spec),
        out_shape=jax.ShapeDtypeStruct(shard_shape, F32), input_output_aliases=aliases,
        compiler_params=_cparams(("parallel", "parallel")),
    )(*args)


def _half_window(ref, h):
    rows = ref.shape[-2] // 2
    if ref.ndim == 3:
        return ref.at[:, pl.ds(h * rows, rows)]
    return ref.at[pl.ds(h * rows, rows)]


def _share_halves(grads, name):
    n = len(grads)

    def body(*refs):
        outs = refs[n:2 * n]
        send_sems, recv_sems = refs[2 * n:]
        x, y, c, _ = _place()
        cps = []
        for t in range(n):
            cp = pltpu.make_async_remote_copy(src_ref=_half_window(outs[t], c), dst_ref=_half_window(outs[t], c),
                                              send_sem=send_sems.at[t], recv_sem=recv_sems.at[t],
                                              device_id=(x, y, 1 - c), device_id_type=MESH)
            cp.start()
            cps.append(cp)
        for t in range(n):
            cps[t].wait_send()
            pltpu.make_async_remote_copy(src_ref=_half_window(outs[t], c), dst_ref=_half_window(outs[t], 1 - c),
                                         send_sem=send_sems.at[t], recv_sem=recv_sems.at[t],
                                         device_id=(x, y, 1 - c), device_id_type=MESH).wait_recv()

    return pl.pallas_call(
        body, name=name, in_specs=[HBM_SPEC] * n, out_specs=[HBM_SPEC] * n,
        out_shape=[jax.ShapeDtypeStruct(g.shape, F32) for g in grads],
        input_output_aliases={t: t for t in range(n)},
        scratch_shapes=[pltpu.SemaphoreType.DMA((n,)), pltpu.SemaphoreType.DMA((n,))],
    )(*grads)


def _adamw(w, g, m, v, name):
    R, W = w.shape
    tr = _pick(R, (512, 352, 256, 32))

    def body(w_ref, g_ref, m_ref, v_ref, d_ref, nm_ref, nv_ref, go_ref):
        gv = g_ref[...]
        go_ref[...] = gv
        nm = ADAM_B1 * m_ref[...] + (1.0 - ADAM_B1) * gv
        nv = ADAM_B2 * v_ref[...] + (1.0 - ADAM_B2) * (gv * gv)
        m_hat = nm / (1.0 - ADAM_B1 ** ADAM_STEP)
        v_hat = nv / (1.0 - ADAM_B2 ** ADAM_STEP)
        d_ref[...] = -ADAM_LR * (m_hat / (jnp.sqrt(v_hat) + ADAM_EPS) + ADAM_WD * w_ref[...])
        nm_ref[...] = nm
        nv_ref[...] = nv

    blk = pl.BlockSpec((tr, W), lambda i: (i, 0))
    shp = jax.ShapeDtypeStruct((R, W), F32)
    return pl.pallas_call(
        body, name=name, grid=(R // tr,), in_specs=[blk] * 4, out_specs=[blk] * 4, out_shape=[shp] * 4,
        compiler_params=_cparams(("parallel",)),
    )(w, g, m, v)


SMALL_ROWS = 32


def _pack_small(ln_g, ln_b, sinks):
    rows = jnp.concatenate([ln_g.reshape(-1, 128), ln_b.reshape(-1, 128),
                            jnp.pad(sinks.reshape(1, -1), ((0, 0), (0, 128 - sinks.size)))], axis=0)
    return jnp.pad(rows, ((0, SMALL_ROWS - rows.shape[0]), (0, 0)))


def _unpack_small(s, ln_shape, sink_shape):
    n = ln_shape[0] * ln_shape[1] * ln_shape[2] // 128
    return s[:n].reshape(ln_shape), s[n:2 * n].reshape(ln_shape), s[2 * n, :sink_shape[1]].reshape(sink_shape)


def _ffn_fwd(xin, w_in, w_out, gain, bias, tag):
    u, h = _ffn_in(xin, w_in, "ffn_in_" + tag)
    y, yb, z = _mm_ln(h, w_out, xin, gain, bias, 0.5, "ffn_out_ln_" + tag)
    return y, yb, dict(u=u, h=h, z=z, xin=xin)


def _ffn_bwd(dz, dzc, saved, w_in, w_out, xin_b, tag, dw_dtype=F32, ln=None):
    du = _ffn_bwd_h(dzc, w_out, saved["u"], "ffn_bwd_h_" + tag)
    d_w_out = _mm_tn(saved["h"], dzc, "ffn_dwout_" + tag, out_dtype=dw_dtype)
    d_w_in = _mm_tn(xin_b, du, "ffn_dwin_" + tag, out_dtype=dw_dtype)
    dx = _mm_nt(du, w_in, "ffn_dx_" + tag, add=dz, add_scale=ALPHA, ln=ln)
    return dx, d_w_in, d_w_out


def kernel(x, ffn1_w_in, ffn1_w_out, ffn2_w_in, ffn2_w_out, ln_g, ln_b, a_w_qkv, a_w_o, kv_w, b_w_q, b_sinks, b_w_o, loss_target, m_ffn1_w_in, m_ffn1_w_out, m_ffn2_w_in, m_ffn2_w_out, m_ln_g, m_ln_b, m_a_w_qkv, m_a_w_o, m_kv_w, m_b_w_q, m_b_sinks, m_b_w_o, v_ffn1_w_in, v_ffn1_w_out, v_ffn2_w_in, v_ffn2_w_out, v_ln_g, v_ln_b, v_a_w_qkv, v_a_w_o, v_kv_w, v_b_w_q, v_b_sinks, v_b_w_o):
    ws = dict(ffn1_w_in=ffn1_w_in, ffn1_w_out=ffn1_w_out, ffn2_w_in=ffn2_w_in, ffn2_w_out=ffn2_w_out, a_w_qkv=a_w_qkv,
              a_w_o=a_w_o, kv_w=kv_w, b_w_q=b_w_q, b_w_o=b_w_o)
    ms = dict(ffn1_w_in=m_ffn1_w_in, ffn1_w_out=m_ffn1_w_out, ffn2_w_in=m_ffn2_w_in, ffn2_w_out=m_ffn2_w_out,
              a_w_qkv=m_a_w_qkv, a_w_o=m_a_w_o, kv_w=m_kv_w, b_w_q=m_b_w_q, b_w_o=m_b_w_o)
    vs = dict(ffn1_w_in=v_ffn1_w_in, ffn1_w_out=v_ffn1_w_out, ffn2_w_in=v_ffn2_w_in, ffn2_w_out=v_ffn2_w_out,
              a_w_qkv=v_a_w_qkv, a_w_o=v_a_w_o, kv_w=v_kv_w, b_w_q=v_b_w_q, b_w_o=v_b_w_o)
    _, _, c_idx, myq = _place()
    xs = x[0]
    target = loss_target[0]

    shards = {(n, l): ws[n].astype(BF16) for n, l in LAYER0_ITEMS + LAYER1_ITEMS}

    def as_weights(items, arrays):
        return {n: (a.reshape(D_MODEL, a.shape[-1]) if a.ndim == 4 else a) for (n, _), a in zip(items, arrays)}

    full0, small = _all_gather(LAYER0_ITEMS, shards, _pack_small(ln_g, ln_b, b_sinks))
    gather_state, token = _gather_start(LAYER1_ITEMS, shards, small)

    def layer1_weights(after):
        return as_weights(LAYER1_ITEMS, _gather_wait(LAYER1_ITEMS, gather_state, after))

    n_ln = ln_g.size // 128
    lg = jnp.concatenate([small[q, :n_ln].reshape(DEPTH, 3, 1, -1) for q in range(N_CHIPS)], axis=-1)
    lb = jnp.concatenate([small[q, n_ln:2 * n_ln].reshape(DEPTH, 3, 1, -1) for q in range(N_CHIPS)], axis=-1)
    lg = lg + token[0, 0]
    reducer = _GradReducer(c_idx, myq, {n: ws[n].shape for n in BIG})
    sq, grad_x, _, gg, gb, dsink_part = _local_step(xs, target, as_weights(LAYER0_ITEMS, full0), layer1_weights,
                                                    lg, lb, b_sinks.reshape(N_HEADS), reducer.begin)

    loss_row = jnp.pad(jnp.sum(sq).reshape(1, 1), ((0, 0), (0, 127)))
    dsinks = jnp.pad(dsink_part[:, 0, :].reshape(N_SLABS, 2, HEAD_DIM)[:, :, 0].reshape(1, N_HEADS), ((0, 0), (0, 128 - N_HEADS)))
    gg_full = jnp.stack([jnp.stack([jnp.sum(gg[i][j], axis=0) for j in range(3)]) for i in range(DEPTH)])
    gb_full = jnp.stack([jnp.stack([jnp.sum(gb[i][j], axis=0) for j in range(3)]) for i in range(DEPTH)])
    small_in = jnp.concatenate([loss_row, dsinks, gg_full.reshape(-1, 128), gb_full.reshape(-1, 128)], axis=0)
    small_in = jnp.pad(small_in, ((0, (-small_in.shape[0]) % 8), (0, 0)))
    small_sum = _small_all_reduce(small_in)
    loss = small_sum[0, 0] * (0.5 / D_MODEL)
    grad_sinks = small_sum[1, :N_HEADS].reshape(b_sinks.shape)
    n_full = DEPTH * 3 * D_MODEL // 128
    cols = D_MODEL // N_CHIPS
    grad_ln_g = lax.dynamic_slice_in_dim(small_sum[2:2 + n_full].reshape(DEPTH, 3, D_MODEL), myq * cols, cols, axis=2)
    grad_ln_b = lax.dynamic_slice_in_dim(small_sum[2 + n_full:2 + 2 * n_full].reshape(DEPTH, 3, D_MODEL), myq * cols, cols, axis=2)
    return _update(reducer, grad_x, loss, grad_ln_g, grad_ln_b, grad_sinks, ws, ms, vs,
                   (ln_g, ln_b, b_sinks), (m_ln_g, m_ln_b, m_b_sinks), (v_ln_g, v_ln_b, v_b_sinks))


def _local_step(xs, target, W, layer1_weights, lg, lb, sinks, grads_ready=None):
    if grads_ready is None:
        grads_ready = lambda tag, grads, overlap: 0.0
    S = xs.shape[0]
    slopes = jnp.asarray(_alibi_slopes(N_HEADS))
    in1, out1, in2, out2 = [W["ffn1_w_in"]], [W["ffn1_w_out"]], [W["ffn2_w_in"]], [W["ffn2_w_out"]]

    y1, y1b, s1 = _ffn_fwd(xs, in1[0], out1[0], lg[0, 0], lb[0, 0], "a1")
    qkv_a = _mm_nn(y1b, W["a_w_qkv"], F32, "qkv_a", split=True)
    mix_a, o_a, lse_a = _attn_fwd(qkv_a, slopes, None, PATTERNS_A, "attn_a_fwd")
    y2, y2b, z2 = _mm_ln(mix_a, W["a_w_o"], y1, lg[0, 1], lb[0, 1], 1.0, "attn_a_out_ln")
    y3, y3b, s3 = _ffn_fwd(y2, in2[0], out2[0], lg[0, 2], lb[0, 2], "a2")
    kv_w_rep = jnp.broadcast_to(W["kv_w"].reshape(D_MODEL, 2, N_KV_B, 1, HEAD_DIM),
                                (D_MODEL, 2, N_KV_B, GROUP_B, HEAD_DIM)).reshape(D_MODEL, 2 * D_MODEL)
    kv_rep = _mm_nn(y3b, kv_w_rep, F32, "kv_proj", split=(1, 2))
    W = dict(W, **layer1_weights(kv_rep))
    in1, out1, in2, out2 = (in1 + [W["ffn1_w_in"]], out1 + [W["ffn1_w_out"]], in2 + [W["ffn2_w_in"]],
                            out2 + [W["ffn2_w_out"]])
    y4, y4b, s4 = _ffn_fwd(y3, in1[1], out1[1], lg[1, 0], lb[1, 0], "b1")
    qkv_b = _mm_nn(y4b, W["b_w_q"], F32, "q_b", split=(0, 1), into=kv_rep)
    mix_b, o_b, lse_b = _attn_fwd(qkv_b, slopes, sinks, PATTERNS_B, "attn_b_fwd")
    y5, y5b, z5 = _mm_ln(mix_b, W["b_w_o"], y4, lg[1, 1], lb[1, 1], 1.0, "attn_b_out_ln")
    y6, _, s6 = _ffn_fwd(y5, in2[1], out2[1], lg[1, 2], lb[1, 2], "b2")

    gr = {n: None for n in BIG}
    gg = [[None] * 3 for _ in range(DEPTH)]
    gb = [[None] * 3 for _ in range(DEPTH)]
    dz6, dz6c, gg[1][2], gb[1][2], sq = _loss_ln_bwd(y6, target, s6["z"], lg[1, 2], 0.5, "loss_ln_bwd")

    (dz5, dz5b, gg[1][1], gb[1][1]), d_in2_b, d_out2_b = _ffn_bwd(dz6, dz6c, s6, in2[1], out2[1], y5b, "b2", BF16,
                                                                  ln=(z5, lg[1, 1], 1.0))
    gr["b_w_o"] = _mm_tn(mix_b, dz5b, "d_b_w_o", out_dtype=BF16)
    dmix_b = _mm_nt(dz5b, W["b_w_o"], "d_mix_b")
    dqkv_b, dsink_part = _attn_bwd(qkv_b, dmix_b, o_b, lse_b, slopes, sinks, PATTERNS_B, "attn_b_bwd")
    dq_b = (dqkv_b, 0)
    gr["b_w_q"] = _mm_tn(y4b, dq_b, "d_b_w_q", out_dtype=BF16)
    dz4, dz4c, gg[1][0], gb[1][0] = _mm_nt(dq_b, W["b_w_q"], "d_y4", add=dz5, add_scale=ALPHA, ln=(s4["z"], lg[1, 0], 0.5))
    dy3, d_in1_b, d_out1_b = _ffn_bwd(dz4, dz4c, s4, in1[1], out1[1], y3b, "b1", BF16)
    gr["kv_w"] = _d_kv_w(y3b, dqkv_b, "d_kv_w")
    tok = grads_ready("l1", {("ffn2_w_in", 1): d_in2_b, ("ffn2_w_out", 1): d_out2_b, ("b_w_o", None): gr["b_w_o"],
                             ("b_w_q", None): gr["b_w_q"], ("ffn1_w_in", 1): d_in1_b, ("ffn1_w_out", 1): d_out1_b,
                             ("kv_w", None): gr["kv_w"]}, True)
    lg0 = lg[0] + tok
    dz3, dz3c, gg[0][2], gb[0][2] = _mm_nt(dqkv_b, kv_w_rep, "d_y3_kv", add=dy3, add_scale=1.0, split=(1, 2),
                                           ln=(s3["z"], lg0[2], 0.5))

    (dz2, dz2b, gg[0][1], gb[0][1]), d_in2_a, d_out2_a = _ffn_bwd(dz3, dz3c, s3, in2[0], out2[0], y2b, "a2", BF16,
                                                                  ln=(z2, lg0[1], 1.0))
    tok = grads_ready("a2", {("ffn2_w_in", 0): d_in2_a, ("ffn2_w_out", 0): d_out2_a}, True)
    lg0 = lg0 + tok
    gr["a_w_o"] = _mm_tn(mix_a, dz2b, "d_a_w_o", out_dtype=BF16)
    dmix_a = _mm_nt(dz2b, W["a_w_o"], "d_mix_a")
    dqkv_a, _ = _attn_bwd(qkv_a, dmix_a, o_a, lse_a, slopes, None, PATTERNS_A, "attn_a_bwd")
    gr["a_w_qkv"] = _mm_tn(y1b, dqkv_a, "d_a_w_qkv", split=True, out_dtype=BF16)
    tok = grads_ready("mix", {("a_w_o", None): gr["a_w_o"], ("a_w_qkv", None): gr["a_w_qkv"]}, True)
    lg0 = lg0 + tok
    dz1, dz1c, gg[0][0], gb[0][0] = _mm_nt(dqkv_a, W["a_w_qkv"], "d_y1", add=dz2, add_scale=ALPHA, split=True,
                                           ln=(s1["z"], lg0[0], 0.5))
    grad_x, d_in1_a, d_out1_a = _ffn_bwd(dz1, dz1c, s1, in1[0], out1[0], xs, "a1", BF16)
    grads_ready("a1", {("ffn1_w_in", 0): d_in1_a, ("ffn1_w_out", 0): d_out1_a}, True)
    gr["ffn1_w_in"] = [d_in1_a, d_in1_b]
    gr["ffn1_w_out"] = [d_out1_a, d_out1_b]
    gr["ffn2_w_in"] = [d_in2_a, d_in2_b]
    gr["ffn2_w_out"] = [d_out2_a, d_out2_b]
    return sq, grad_x, gr, gg, gb, dsink_part


def _grad_item(name, layer, g):
    if name.endswith("w_in"):
        return (g, "col", HALF_FF, _slot, name, layer)
    if name.endswith("w_out"):
        return (g, "row", D_MODEL, None, name, layer)
    if name == "a_w_qkv":
        return (g, "col", QKV_SHARD, lambda q: q, name, None)
    return (g, "row", g.shape[1], None, name, None)


class _GradReducer:
    def __init__(self, c_idx, myq, shard_shapes):
        self.c_idx, self.myq, self.shard_shapes = c_idx, myq, shard_shapes
        self.groups = []

    def begin(self, tag, grads, overlap):
        items = [_grad_item(n, l, g) for (n, l), g in grads.items()]
        kinds, widths, colblocks = [it[1] for it in items], [it[2] for it in items], [it[3] for it in items]
        views = [_grad_view(k, it[0]) for k, it in zip(kinds, items)]
        if overlap:
            lands = [jax.ShapeDtypeStruct((N_DIRECT,) + _piece_shape(k, w, _half_shape(k, v.shape)), BF16)
                     for k, w, v in zip(kinds, widths, views)]
            state, token = _split_start("grad_direct_start_" + tag, _direct_copies(kinds, widths, colblocks), 10 * len(items),
                                        views, lands, jnp.zeros((8, 128), F32))
            self.groups.append((tag, items, None, state, token))
            return token[0, 0]
        from_sibling = _pair_exchange(views, kinds, "grad_pair_exchange_" + tag)
        sums = [_pair_sum(k, v, r, self.c_idx, "pair_sum_%s_%d" % (tag, t))
                for t, (k, v, r) in enumerate(zip(kinds, views, from_sibling))]
        self.groups.append((tag, items, sums, None, None))
        return 0.0

    def _sum_group(self, tag, items, sums, received, direct):
        for t, (it, s, r) in enumerate(zip(items, sums, received)):
            _, k, _, cb, name, layer = it
            own = cb(self.myq) if k == "col" else self.myq
            self.half_done[name] = _chip_sum(k, s, r, own, self.c_idx, self.shard_shapes[name], layer,
                                             self.half_done.get(name), "chip_sum_%s_%d" % (tag, t), direct=direct)

    def finish_first(self, after):
        self.half_done, self.late, early = {}, [], []
        started = [after]
        for g, (tag, items, sums, state, token) in enumerate(self.groups):
            kinds, widths, colblocks = [it[1] for it in items], [it[2] for it in items], [it[3] for it in items]
            if state is None:
                copies = _chip_copies(kinds, widths, colblocks)
                state, token = _split_start("grad_chip_start_" + tag, copies, 3 * len(items), sums,
                                            _chip_land_shapes(sums, kinds, widths), sums[-1])
                self.late.append((tag, items, copies, state, False))
                started.append(token)
            elif g == len(self.groups) - 1:
                self.late.append((tag, items, _direct_copies(kinds, widths, colblocks), state, True))
                started.append(token)
            else:
                early.append((tag, items, _direct_copies(kinds, widths, colblocks), state))
        for tag, items, copies, state in early:
            views, received = _split_wait("grad_direct_wait_" + tag, copies, state, started)
            self._sum_group(tag, items, views, received, True)
        late_names = {it[4] for _, items, _, _, _ in self.late for it in items}
        names = [n for n in BIG if n not in late_names]
        return dict(zip(names, _share_halves([self.half_done[n] for n in names], "grad_share_halves_first")))

    def finish_rest(self, after):
        names = []
        for tag, items, copies, state, direct in self.late:
            sums, received = _split_wait("grad_late_wait_" + tag, copies, state, after)
            self._sum_group(tag, items, sums, received, direct)
            names += [it[4] for it in items if it[4] not in names]
        return dict(zip(names, _share_halves([self.half_done[n] for n in names], "grad_share_halves_rest")))


def _update(reducer, grad_x, loss, grad_ln_g, grad_ln_b, grad_sinks, ws, ms, vs, small_w, small_m, small_v):
    ln_g, ln_b, b_sinks = small_w
    m_ln_g, m_ln_b, m_b_sinks = small_m
    v_ln_g, v_ln_b, v_b_sinks = small_v

    grads, deltas, new_m, new_v = {}, {}, {}, {}

    def update(some):
        done = []
        for name in some:
            shp = ws[name].shape
            flat = lambda a: a.reshape(-1, shp[-1])
            d, nm, nv, g = _adamw(flat(ws[name]), flat(some[name]), flat(ms[name]), flat(vs[name]), "adamw_" + name)
            grads[name], deltas[name], new_m[name], new_v[name] = g.reshape(shp), d.reshape(shp), nm.reshape(shp), nv.reshape(shp)
            done.append(d)
        return done

    rest = reducer.finish_rest(update(reducer.finish_first(grad_x)))
    update(rest)
    delta_s, nm_s, nv_s, _ = _adamw(_pack_small(ln_g, ln_b, b_sinks), _pack_small(grad_ln_g, grad_ln_b, grad_sinks),
                                    _pack_small(m_ln_g, m_ln_b, m_b_sinks), _pack_small(v_ln_g, v_ln_b, v_b_sinks), "adamw_small")
    for d, blob in ((grads, None), (deltas, delta_s), (new_m, nm_s), (new_v, nv_s)):
        if blob is None:
            d["ln_g"], d["ln_b"], d["b_sinks"] = grad_ln_g, grad_ln_b, grad_sinks
        else:
            d["ln_g"], d["ln_b"], d["b_sinks"] = _unpack_small(blob, ln_g.shape, b_sinks.shape)

    order = ("ffn1_w_in", "ffn1_w_out", "ffn2_w_in", "ffn2_w_out", "ln_g", "ln_b", "a_w_qkv", "a_w_o", "kv_w", "b_w_q",
             "b_sinks", "b_w_o")
    outs = [loss, grad_x[None]]
    for d in (grads, deltas, new_m, new_v):
        outs += [d[n] for n in order]
    return tuple(outs)
```

```python
import numpy as np
import jax
import jax.numpy as jnp
from jax import lax
from jax.experimental import pallas as pl
from jax.experimental.pallas import tpu as pltpu

F32 = jnp.float32
BF16 = jnp.bfloat16

D_MODEL = 1024
D_FF = 2816
HALF_FF = D_FF // 2
HEAD_DIM = 64
N_HEADS = 16
N_KV_B = 4
GROUP_B = N_HEADS // N_KV_B
DEPTH = 2
ALPHA = (2.0 * DEPTH) ** 0.25
LN_EPS = 1e-5
BLOCK = 128
SLAB = 128
N_SLABS = D_MODEL // SLAB
PATTERNS_A = ((1, 128, 1.0), (4, 128, 4.0), (16, 128, 16.0))
PATTERNS_B = ((1, 127, 1.0),)
NEG = -1e30

ADAM_LR = 0.001
ADAM_B1 = 0.9
ADAM_B2 = 0.999
ADAM_EPS = 1e-08
ADAM_WD = 0.01
ADAM_STEP = 10

N_CHIPS = 4
VMEM_LIMIT = 56 * 1024 * 1024
WHOLE_WEIGHT_BYTES = 12 * 1024 * 1024
MESH = pl.DeviceIdType.MESH


def _alibi_slopes(n):
    return np.array([2.0 ** (-8.0 * (h + 1) / n) for h in range(n)], dtype=np.float32)


def _cparams(sem=None, vmem=VMEM_LIMIT):
    return pltpu.CompilerParams(dimension_semantics=sem, vmem_limit_bytes=vmem)


_DIMS = {"nn": ((1,), (0,)), "nt": ((1,), (1,)), "tn": ((0,), (0,))}


def _unlead(x):
    if isinstance(x, tuple):
        return x[0], x[1], x[0].shape[1:]
    return x, None, x.shape


def _bspec(block, imap, lead=None, **kw):
    if lead is None:
        return pl.BlockSpec(block, imap, **kw)
    return pl.BlockSpec((None,) + tuple(block), lambda *g: (lead,) + tuple(imap(*g)), **kw)


def _ln_bwd_math(zv, dyv, gain):
    rows = zv.shape[0]
    mu = jnp.mean(zv, axis=-1, keepdims=True)
    zc = zv - mu
    var = jnp.mean(zc * zc, axis=-1, keepdims=True)
    rstd = lax.rsqrt(var + LN_EPS)
    xhat = zc * rstd
    dyg = dyv * gain
    m1 = jnp.mean(dyg, axis=-1, keepdims=True)
    m2 = jnp.mean(dyg * xhat, axis=-1, keepdims=True)
    dz = rstd * (dyg - m1 - xhat * m2)
    pg = jnp.sum((dyv * xhat).reshape(rows // 8, 8, D_MODEL), axis=0)
    pb = jnp.sum(dyv.reshape(rows // 8, 8, D_MODEL), axis=0)
    return dz, pg, pb


def _matmul(a, b, mode, out_dtype, tm, tn, tk, name, add=None, add_scale=1.0, split=False, into=None, ln=None):
    out_spec = pl.BlockSpec((tm, tn), lambda i, j, k: (i, j))
    base, count = (0, 3) if split is True else (split or (0, 0))
    if mode == "nn":
        a, al, (M, K) = _unlead(a)
        b, bl, (K2, N) = _unlead(b)
        a_spec = _bspec((tm, tk), lambda i, j, k: (i, k), al)
        b_spec = _bspec((tk, tn), lambda i, j, k: (k, j), bl)
        out_struct = jax.ShapeDtypeStruct((M, N), out_dtype)
        if split:
            assert tn == D_MODEL and N == count * tn
            out_spec = pl.BlockSpec((None, tm, tn), lambda i, j, k: (j + base, i, 0))
            out_struct = jax.ShapeDtypeStruct((3, M, tn), out_dtype)
    elif mode == "nt":
        b, bl, (N, K2) = _unlead(b)
        if split:
            M, K = a.shape[1], count * a.shape[2]
            if tk == K:
                a_spec = [pl.BlockSpec((None, tm, D_MODEL), lambda i, j, k, s=s: (s + base, i, 0)) for s in range(count)]
            else:
                assert tk == D_MODEL
                a_spec = pl.BlockSpec((None, tm, tk), lambda i, j, k: (k + base, i, 0))
        else:
            a, al, (M, K) = _unlead(a)
            a_spec = _bspec((tm, tk), lambda i, j, k: (i, k), al)
        whole_b = {"pipeline_mode": pl.Buffered(1)} if (tn, tk) == (N, K2) else {}
        b_spec = _bspec((tn, tk), lambda i, j, k: (j, k), bl, **whole_b)
        out_struct = jax.ShapeDtypeStruct((M, N), out_dtype)
    else:
        a, al, (K, M) = _unlead(a)
        if split:
            assert tn == D_MODEL
            K2, N = b.shape[1], count * b.shape[2]
            b_spec = pl.BlockSpec((None, tk, tn), lambda i, j, k: (j + base, k, 0))
        else:
            b, bl, (K2, N) = _unlead(b)
            b_spec = _bspec((tk, tn), lambda i, j, k: (k, j), bl)
        a_spec = _bspec((tk, tm), lambda i, j, k: (k, i), al)
        out_struct = jax.ShapeDtypeStruct((M, N), out_dtype)
    assert K == K2 and M % tm == 0 and N % tn == 0 and K % tk == 0, (a.shape, b.shape, mode, tm, tn, tk)
    nk = K // tk
    dims = (_DIMS[mode], ((), ()))
    has_add = add is not None

    narrow = out_dtype != F32
    assert not (narrow and has_add)
    if ln is not None:
        assert has_add and mode == "nt" and tn == N == D_MODEL

    a_specs = a_spec if isinstance(a_spec, list) else [a_spec]
    n_a = len(a_specs)

    def body(*refs):
        a_refs, refs = refs[:n_a], refs[n_a - 1:]
        if into is not None:
            refs = refs[:2] + refs[3:]
        if ln is not None:
            a_ref, b_ref, add_ref, z_ref, g_ref, o_ref, dzc_ref, gg_ref, gb_ref = refs
            acc_ref = o_ref
        elif has_add:
            a_ref, b_ref, add_ref, o_ref = refs
            acc_ref = o_ref
        elif narrow:
            a_ref, b_ref, o_ref, acc_ref = refs
        else:
            a_ref, b_ref, o_ref = refs
            acc_ref = o_ref
        k = pl.program_id(2)
        if n_a == 1:
            part = lax.dot_general(a_ref[...].astype(BF16), b_ref[...].astype(BF16), dims, preferred_element_type=F32)
        else:
            part = sum(lax.dot_general(r[...], b_ref[:, s * D_MODEL:(s + 1) * D_MODEL], dims, preferred_element_type=F32)
                       for s, r in enumerate(a_refs))
        if has_add:
            @pl.when(k == 0)
            def _():
                acc_ref[...] = part + add_scale * add_ref[...]
        else:
            @pl.when(k == 0)
            def _():
                acc_ref[...] = part

        @pl.when(k > 0)
        def _():
            acc_ref[...] += part

        if narrow:
            @pl.when(k == nk - 1)
            def _():
                o_ref[...] = acc_ref[...].astype(out_dtype)

        if ln is not None:
            @pl.when(k == nk - 1)
            def _():
                dz, pg, pb = _ln_bwd_math(z_ref[...], o_ref[...], g_ref[...])
                o_ref[...] = dz
                dzc_ref[...] = (ln[2] * dz).astype(BF16)
                first = pl.program_id(0) == 0

                @pl.when(first)
                def _():
                    gg_ref[...] = pg
                    gb_ref[...] = pb

                @pl.when(jnp.logical_not(first))
                def _():
                    gg_ref[...] += pg
                    gb_ref[...] += pb

    in_specs = [*a_specs, b_spec]
    args = [a] * n_a + [b]
    aliases = {}
    if into is not None:
        assert mode == "nn" and split and not has_add and n_a == 1
        in_specs.append(pl.BlockSpec(memory_space=pl.ANY))
        args.append(into)
        aliases = {2: 0}
    if has_add:
        in_specs.append(pl.BlockSpec((tm, tn), lambda i, j, k: (i, j)))
        args.append(add)
    sem = ("parallel", "parallel", "arbitrary")
    if ln is not None:
        part8 = pl.BlockSpec((8, N), lambda i, j, k: (0, 0))
        in_specs += [pl.BlockSpec((tm, tn), lambda i, j, k: (i, j)), pl.BlockSpec((1, N), lambda i, j, k: (0, 0))]
        args += [ln[0], ln[1]]
        out_spec = [out_spec, pl.BlockSpec((tm, tn), lambda i, j, k: (i, j)), part8, part8]
        out_struct = [out_struct, jax.ShapeDtypeStruct((M, N), BF16), jax.ShapeDtypeStruct((8, N), F32),
                      jax.ShapeDtypeStruct((8, N), F32)]
        sem = ("arbitrary", "arbitrary", "arbitrary")
    return pl.pallas_call(
        body, name=name, grid=(M // tm, N // tn, nk),
        in_specs=in_specs, out_specs=out_spec, out_shape=out_struct, input_output_aliases=aliases,
        scratch_shapes=[pltpu.VMEM((tm, tn), F32)] if narrow else [],
        compiler_params=_cparams(sem),
    )(*args)


def _pick(n, cands):
    for c in cands:
        if n % c == 0:
            return c
    raise ValueError((n, cands))


def _mm_nn(a, b, out_dtype, name, split=False, into=None):
    M, K = _unlead(a)[2]
    N = _unlead(b)[2][1]
    return _matmul(a, b, "nn", out_dtype, _pick(M, (1024, 512, 256)), _pick(N, (1024, 512)), _pick(K, (1024, 512)), name,
                   split=split, into=into)


def _mm_nt(a, b, name, add=None, add_scale=1.0, split=False, ln=None):
    M = a.shape[1] if split else _unlead(a)[2][0]
    N, K = _unlead(b)[2]
    tn = _pick(N, (1024, 512))
    if tn == N and N * K * 2 <= WHOLE_WEIGHT_BYTES:
        tm, tk = _pick(M, (512, 256)), K
    else:
        tms = (512, 256) if ln is not None else (1024, 512, 256)
        tm, tk = _pick(M, tms), _pick(D_MODEL if split else K, (2816, 1024, 512))
    return _matmul(a, b, "nt", F32, tm, tn, tk, name, add=add, add_scale=add_scale, split=split, ln=ln)


def _mm_tn(a, b, name, split=False, out_dtype=F32):
    K, M = _unlead(a)[2]
    N = D_MODEL if split else _unlead(b)[2][1]
    return _matmul(a, b, "tn", out_dtype, _pick(M, (1024, 1408, 512)), _pick(N, (1408, 1024, 512)),
                   _pick(K, (2048, 1024, 512, 256)), name, split=split)


def _d_kv_w(y, dqkv, name):
    S = y.shape[0]
    tk = _pick(S, (1024, 512))
    nk = S // tk
    width = N_KV_B * HEAD_DIM
    r, c = np.arange(D_MODEL)[:, None], np.arange(width)[None, :]
    fold = jnp.asarray((r // (GROUP_B * HEAD_DIM) == c // HEAD_DIM) & (r % HEAD_DIM == c % HEAD_DIM), BF16)

    def body(y_ref, dk_ref, dv_ref, f_ref, o_ref, acc_ref):
        k = pl.program_id(0)
        summed = jnp.concatenate([jnp.dot(ref[...], f_ref[...], preferred_element_type=F32).astype(BF16)
                                  for ref in (dk_ref, dv_ref)], axis=1)
        part = lax.dot_general(summed, y_ref[...], (_DIMS["tn"], ((), ())), preferred_element_type=F32)

        @pl.when(k == 0)
        def _():
            acc_ref[...] = part

        @pl.when(k > 0)
        def _():
            acc_ref[...] += part

        @pl.when(k == nk - 1)
        def _():
            o_ref[...] = acc_ref[...].T.astype(BF16)

    return pl.pallas_call(
        body, name=name, grid=(nk,),
        in_specs=[pl.BlockSpec((tk, D_MODEL), lambda k: (k, 0)),
                  pl.BlockSpec((None, tk, D_MODEL), lambda k: (1, k, 0)),
                  pl.BlockSpec((None, tk, D_MODEL), lambda k: (2, k, 0)),
                  pl.BlockSpec((D_MODEL, width), lambda k: (0, 0))],
        out_specs=pl.BlockSpec((D_MODEL, 2 * width), lambda k: (0, 0)),
        out_shape=jax.ShapeDtypeStruct((D_MODEL, 2 * width), BF16),
        scratch_shapes=[pltpu.VMEM((2 * width, D_MODEL), F32)],
        compiler_params=_cparams(("arbitrary",)),
    )(y, dqkv, dqkv, fold)


def _ffn_in(x, w, name):
    S = x.shape[0]
    tm = _pick(S, (512, 256))
    w, wl, _ = _unlead(w)

    def body(x_ref, w_ref, t_ref, h_ref):
        acc = jnp.dot(x_ref[...].astype(BF16), w_ref[...], preferred_element_type=F32)
        g = acc[:, :HALF_FF]
        up = acc[:, HALF_FF:]
        sg = jax.nn.sigmoid(g)
        silu = g * sg
        t_ref[:, :HALF_FF] = (up * (sg * (1.0 + g * (1.0 - sg)))).astype(BF16)
        t_ref[:, HALF_FF:] = silu.astype(BF16)
        h_ref[...] = (silu * up).astype(BF16)

    return pl.pallas_call(
        body, name=name, grid=(2, S // tm),
        in_specs=[pl.BlockSpec((tm, D_MODEL), lambda j, i: (i, 0)),
                  _bspec((D_MODEL, D_FF), lambda j, i: (0, j), wl)],
        out_specs=[pl.BlockSpec((tm, D_FF), lambda j, i: (i, j)),
                   pl.BlockSpec((tm, HALF_FF), lambda j, i: (i, j))],
        out_shape=[jax.ShapeDtypeStruct((S, 2 * D_FF), BF16), jax.ShapeDtypeStruct((S, D_FF), BF16)],
        compiler_params=_cparams(("parallel", "parallel")),
    )(x, w)


def _ffn_bwd_h(dzc, w_out, u, name):
    S = dzc.shape[0]
    tm = _pick(S, (512, 256))
    w_out, wl, _ = _unlead(w_out)

    def body(dz_ref, w_ref, t_ref, du_ref):
        dh = lax.dot_general(dz_ref[...], w_ref[...], (((1,), (1,)), ((), ())), preferred_element_type=F32)
        du_ref[:, :HALF_FF] = (dh * t_ref[:, :HALF_FF].astype(F32)).astype(BF16)
        du_ref[:, HALF_FF:] = (dh * t_ref[:, HALF_FF:].astype(F32)).astype(BF16)

    return pl.pallas_call(
        body, name=name, grid=(2, S // tm),
        in_specs=[pl.BlockSpec((tm, D_MODEL), lambda j, i: (i, 0)),
                  _bspec((HALF_FF, D_MODEL), lambda j, i: (j, 0), wl),
                  pl.BlockSpec((tm, D_FF), lambda j, i: (i, j))],
        out_specs=pl.BlockSpec((tm, D_FF), lambda j, i: (i, j)),
        out_shape=jax.ShapeDtypeStruct((S, 2 * D_FF), BF16),
        compiler_params=_cparams(("parallel", "parallel")),
    )(dzc, w_out, u)


def _mm_ln(a, w, resid, gain, bias, c, name):
    S, K = a.shape
    tm = _pick(S, (512, 256))
    w, wl, _ = _unlead(w)

    def body(a_ref, w_ref, r_ref, g_ref, b_ref, y_ref, yb_ref, z_ref):
        z = ALPHA * r_ref[...] + c * jnp.dot(a_ref[...], w_ref[...], preferred_element_type=F32)
        mu = jnp.mean(z, axis=-1, keepdims=True)
        zc = z - mu
        var = jnp.mean(zc * zc, axis=-1, keepdims=True)
        y = zc * lax.rsqrt(var + LN_EPS) * g_ref[...] + b_ref[...]
        z_ref[...] = z
        y_ref[...] = y
        yb_ref[...] = y.astype(BF16)

    row = pl.BlockSpec((tm, D_MODEL), lambda i: (i, 0))
    vec = pl.BlockSpec((1, D_MODEL), lambda i: (0, 0))
    return pl.pallas_call(
        body, name=name, grid=(S // tm,),
        in_specs=[pl.BlockSpec((tm, K), lambda i: (i, 0)), _bspec((K, D_MODEL), lambda i: (0, 0), wl), row, vec, vec],
        out_specs=[row, row, row],
        out_shape=[jax.ShapeDtypeStruct((S, D_MODEL), F32), jax.ShapeDtypeStruct((S, D_MODEL), BF16),
                   jax.ShapeDtypeStruct((S, D_MODEL), F32)],
        compiler_params=_cparams(("parallel",)),
    )(a, w, resid, gain, bias)


def _loss_ln_bwd(y, t, z, gain, c, name):
    S = y.shape[0]
    tm = _pick(S, (512, 256))

    def body(y_ref, t_ref, z_ref, g_ref, dz_ref, dzc_ref, gg_ref, gb_ref, sq_ref):
        i = pl.program_id(0)
        e = y_ref[...] - t_ref[...]
        dz, pg, pb = _ln_bwd_math(z_ref[...], e * (1.0 / D_MODEL), g_ref[...])
        dz_ref[...] = dz
        dzc_ref[...] = (c * dz).astype(BF16)
        ps = jnp.sum((e * e).reshape(tm // 8, 8, D_MODEL), axis=0)

        @pl.when(i == 0)
        def _():
            gg_ref[...] = pg
            gb_ref[...] = pb
            sq_ref[...] = ps

        @pl.when(i > 0)
        def _():
            gg_ref[...] += pg
            gb_ref[...] += pb
            sq_ref[...] += ps

    row = pl.BlockSpec((tm, D_MODEL), lambda i: (i, 0))
    part = pl.BlockSpec((8, D_MODEL), lambda i: (0, 0))
    part_shape = jax.ShapeDtypeStruct((8, D_MODEL), F32)
    return pl.pallas_call(
        body, name=name, grid=(S // tm,),
        in_specs=[row, row, row, pl.BlockSpec((1, D_MODEL), lambda i: (0, 0))],
        out_specs=[row, row, part, part, part],
        out_shape=[jax.ShapeDtypeStruct((S, D_MODEL), F32), jax.ShapeDtypeStruct((S, D_MODEL), BF16),
                   part_shape, part_shape, part_shape],
        compiler_params=_cparams(("arbitrary",)),
    )(y, t, z, gain)


def _rows(start, d):
    if d == 1:
        return pl.ds(pl.multiple_of(start, BLOCK), BLOCK)
    return pl.ds(start, BLOCK, stride=d)


def _ld(ref, start, d):
    return ref[_rows(start, d), :]


def _ld3(ref, lead, start, d):
    return ref[lead, _rows(start, d), :]


def _st3(ref, lead, start, d, val):
    ref[lead, _rows(start, d), :] = val


def _acc3(ref, lead, start, d, val):
    ref[lead, _rows(start, d), :] = ref[lead, _rows(start, d), :] + val


def _band_consts(slope0, slope1, maxd, scale):
    row = lax.broadcasted_iota(jnp.int32, (2 * BLOCK, 2 * BLOCK), 0)
    kj = lax.broadcasted_iota(jnp.int32, (2 * BLOCK, 2 * BLOCK), 1)
    top = row < BLOCK
    dist = BLOCK + jnp.where(top, row, row - BLOCK) - kj
    slope = jnp.where(top, slope0, slope1)
    base = jnp.where((dist >= 0) & (dist <= maxd), -(slope * (dist.astype(F32) * scale)), NEG)
    return base, kj < BLOCK


def _stack_heads(x, lo):
    return jnp.concatenate([jnp.where(lo, x, 0.0), jnp.where(lo, 0.0, x)], axis=0)


def _unstack_heads(x2, lo):
    return jnp.where(lo, x2[:BLOCK], x2[BLOCK:])


def _scores(q2, k2, base, prev_keys, first):
    s = lax.dot_general(q2, k2, (((1,), (1,)), ((), ())), preferred_element_type=F32) * (HEAD_DIM ** -0.5) + base
    return jnp.where(jnp.logical_and(prev_keys, first), NEG, s)


def _softmax_weights(ls):
    mx = ls[0]
    for l in ls[1:]:
        mx = jnp.maximum(mx, l)
    es = [jnp.exp(l - mx) for l in ls]
    tot = es[0]
    for e in es[1:]:
        tot = tot + e
    inv = 1.0 / tot
    return [e * inv for e in es]


def _attn_fwd(qkv, slopes, sinks, patterns, name):
    S = qkv.shape[1]
    npat = len(patterns)
    has_sink = sinks is not None
    if not has_sink:
        sinks = jnp.zeros((N_HEADS,), F32)
    rows_c = 256

    def body(slopes_ref, sinks_ref, x_ref, mix_ref, o_ref, lse_ref, o_scr, lse_scr):
        p = pl.program_id(0)
        lo = lax.broadcasted_iota(jnp.int32, (BLOCK, SLAB), 1) < HEAD_DIM
        top1 = lax.broadcasted_iota(jnp.int32, (2 * BLOCK, 1), 0) < BLOCK
        sk2 = jnp.where(top1, sinks_ref[2 * p], sinks_ref[2 * p + 1])
        for pi, (d, maxd, scale) in enumerate(patterns):
            nb = S // d // BLOCK
            base, prev_keys = _band_consts(slopes_ref[2 * p], slopes_ref[2 * p + 1], maxd, scale)

            def blk(t, carry, pi=pi, d=d, nb=nb, base=base, prev_keys=prev_keys):
                r = t // nb
                n = t - r * nb
                start = r + (d * BLOCK) * n
                prev = jnp.where(n > 0, start - d * BLOCK, start)
                q2 = _stack_heads(_ld3(x_ref, 0, start, d), lo).astype(BF16)
                k2 = jnp.concatenate([_ld3(x_ref, 1, prev, d), _ld3(x_ref, 1, start, d)], axis=0).astype(BF16)
                v2 = jnp.concatenate([_ld3(x_ref, 2, prev, d), _ld3(x_ref, 2, start, d)], axis=0).astype(BF16)
                s = _scores(q2, k2, base, prev_keys, n == 0)
                m = jnp.max(s, axis=-1, keepdims=True)
                if has_sink:
                    m = jnp.maximum(m, sk2)
                e = jnp.exp(s - m)
                den = jnp.sum(e, axis=-1, keepdims=True)
                if has_sink:
                    den = den + jnp.exp(sk2 - m)
                o2 = jnp.dot((e / den).astype(BF16), v2, preferred_element_type=F32)
                _st3(o_scr, pi, start, d, _unstack_heads(o2, lo))
                _st3(lse_scr, pi, start, d, _unstack_heads(m + jnp.log(den), lo))
                return carry

            lax.fori_loop(0, d * nb, blk, 0, unroll=8)

        lane_c = lax.broadcasted_iota(jnp.int32, (rows_c, SLAB), 1)

        def comb(ci, carry):
            rows = pl.ds(pl.multiple_of(ci * rows_c, rows_c), rows_c)
            ls = [lse_scr[i, rows, :] for i in range(npat)]
            packed = jnp.zeros((rows_c, SLAB), F32)
            for i in range(npat):
                o_ref[i, rows, :] = o_scr[i, rows, :].astype(BF16)
                packed = jnp.where(lane_c % HEAD_DIM == i, ls[i], packed)
            lse_ref[rows, :] = packed
            if npat == 1:
                mix_ref[rows, :] = o_scr[0, rows, :].astype(BF16)
            else:
                ws = _softmax_weights(ls)
                acc = ws[0] * o_scr[0, rows, :]
                for i in range(1, npat):
                    acc = acc + ws[i] * o_scr[i, rows, :]
                mix_ref[rows, :] = acc.astype(BF16)
            return carry

        lax.fori_loop(0, S // rows_c, comb, 0, unroll=2)

    smem = pl.BlockSpec(memory_space=pltpu.SMEM)
    return pl.pallas_call(
        body, name=name, grid=(N_SLABS,),
        in_specs=[smem, smem, pl.BlockSpec((3, S, SLAB), lambda p: (0, 0, p))],
        out_specs=[pl.BlockSpec((S, SLAB), lambda p: (0, p)), pl.BlockSpec((npat, S, SLAB), lambda p: (0, 0, p)),
                   pl.BlockSpec((None, S, SLAB), lambda p: (p, 0, 0))],
        out_shape=[jax.ShapeDtypeStruct((S, D_MODEL), BF16), jax.ShapeDtypeStruct((npat, S, D_MODEL), BF16),
                   jax.ShapeDtypeStruct((N_SLABS, S, SLAB), F32)],
        scratch_shapes=[pltpu.VMEM((npat, S, SLAB), F32), pltpu.VMEM((npat, S, SLAB), F32)],
        compiler_params=_cparams(("arbitrary",)),
    )(slopes, sinks, qkv)


def _attn_bwd(qkv, dout, o, lse, slopes, sinks, patterns, name):
    S = qkv.shape[1]
    npat = len(patterns)
    has_sink = sinks is not None
    if not has_sink:
        sinks = jnp.zeros((N_HEADS,), F32)
    rows_c = 256

    def headsum(x, lo):
        same = (lax.broadcasted_iota(jnp.int32, (SLAB, SLAB), 0) < HEAD_DIM) == (lax.broadcasted_iota(jnp.int32, (SLAB, SLAB), 1) < HEAD_DIM)
        return jnp.dot(x, same.astype(F32), precision=lax.Precision.HIGH, preferred_element_type=F32)

    def body(slopes_ref, sinks_ref, x_ref, do_ref, o_ref, lsep_ref, dxo_ref, dsink_ref, dbar_ref, sacc_ref, lse_ref, dx_ref):
        p = pl.program_id(0)
        lo = lax.broadcasted_iota(jnp.int32, (BLOCK, SLAB), 1) < HEAD_DIM
        lo_c = lax.broadcasted_iota(jnp.int32, (rows_c, SLAB), 1) < HEAD_DIM
        top1 = lax.broadcasted_iota(jnp.int32, (2 * BLOCK, 1), 0) < BLOCK
        sk2 = jnp.where(top1, sinks_ref[2 * p], sinks_ref[2 * p + 1])

        def prep(ci, carry):
            rows = pl.ds(pl.multiple_of(ci * rows_c, rows_c), rows_c)
            dov = do_ref[rows, :]
            dx_ref[:, rows, :] = jnp.zeros((3, rows_c, SLAB), F32)
            packed = lsep_ref[rows, :]
            ls = [jnp.where(lo_c, packed[:, i:i + 1], packed[:, HEAD_DIM + i:HEAD_DIM + i + 1]) for i in range(npat)]
            for i in range(npat):
                lse_ref[i, rows, :] = ls[i]
            if npat == 1:
                dbar_ref[rows, :] = headsum(dov * o_ref[0, rows, :].astype(F32), lo_c)
            else:
                ws = _softmax_weights(ls)
                acc = ws[0] * headsum(dov * o_ref[0, rows, :].astype(F32), lo_c)
                for i in range(1, npat):
                    acc = acc + ws[i] * headsum(dov * o_ref[i, rows, :].astype(F32), lo_c)
                dbar_ref[rows, :] = acc
            return carry

        lax.fori_loop(0, S // rows_c, prep, 0, unroll=2)
        sacc_ref[...] = jnp.zeros((BLOCK, SLAB), F32)

        for pi, (d, maxd, scale) in enumerate(patterns):
            nb = S // d // BLOCK
            base, prev_keys = _band_consts(slopes_ref[2 * p], slopes_ref[2 * p + 1], maxd, scale)

            def blk(t, carry, pi=pi, d=d, nb=nb, base=base, prev_keys=prev_keys):
                r = t // nb
                n = t - r * nb
                start = r + (d * BLOCK) * n
                prev = jnp.where(n > 0, start - d * BLOCK, start)
                q2 = _stack_heads(_ld3(x_ref, 0, start, d), lo).astype(BF16)
                k2 = jnp.concatenate([_ld3(x_ref, 1, prev, d), _ld3(x_ref, 1, start, d)], axis=0).astype(BF16)
                v2 = jnp.concatenate([_ld3(x_ref, 2, prev, d), _ld3(x_ref, 2, start, d)], axis=0).astype(BF16)
                ls = [_ld3(lse_ref, i, start, d) for i in range(npat)]
                w = _softmax_weights(ls)[pi] if npat > 1 else 1.0
                do2 = _stack_heads(w * _ld(do_ref, start, d), lo).astype(BF16)
                dl = w * _ld(dbar_ref, start, d)
                lse2 = jnp.concatenate([ls[pi][:, :1], ls[pi][:, HEAD_DIM:HEAD_DIM + 1]], axis=0)
                dl2 = jnp.concatenate([dl[:, :1], dl[:, HEAD_DIM:HEAD_DIM + 1]], axis=0)
                s = _scores(q2, k2, base, prev_keys, n == 0)
                pr = jnp.exp(s - lse2)
                dp = lax.dot_general(do2, v2, (((1,), (1,)), ((), ())), preferred_element_type=F32)
                ds = (pr * (dp - dl2) * (HEAD_DIM ** -0.5)).astype(BF16)
                dq2 = jnp.dot(ds, k2, preferred_element_type=F32)
                dk2 = lax.dot_general(ds, q2, (((0,), (0,)), ((), ())), preferred_element_type=F32)
                dv2 = lax.dot_general(pr.astype(BF16), do2, (((0,), (0,)), ((), ())), preferred_element_type=F32)
                _acc3(dx_ref, 0, start, d, _unstack_heads(dq2, lo))
                _acc3(dx_ref, 1, prev, d, dk2[:BLOCK])
                _acc3(dx_ref, 1, start, d, dk2[BLOCK:])
                _acc3(dx_ref, 2, prev, d, dv2[:BLOCK])
                _acc3(dx_ref, 2, start, d, dv2[BLOCK:])
                if has_sink:
                    sacc_ref[...] += _unstack_heads(-jnp.exp(sk2 - lse2) * dl2, lo)
                return carry

            lax.fori_loop(0, d * nb, blk, 0, unroll=8)

        dsink_ref[...] = jnp.broadcast_to(jnp.sum(sacc_ref[...], axis=0, keepdims=True), (8, SLAB))

        def emit(ci, carry):
            rows = pl.ds(pl.multiple_of(ci * rows_c, rows_c), rows_c)
            dxo_ref[:, rows, :] = dx_ref[:, rows, :].astype(BF16)
            return carry

        lax.fori_loop(0, S // rows_c, emit, 0, unroll=2)

    smem = pl.BlockSpec(memory_space=pltpu.SMEM)
    return pl.pallas_call(
        body, name=name, grid=(N_SLABS,),
        in_specs=[smem, smem, pl.BlockSpec((3, S, SLAB), lambda p: (0, 0, p)), pl.BlockSpec((S, SLAB), lambda p: (0, p)),
                  pl.BlockSpec((npat, S, SLAB), lambda p: (0, 0, p)), pl.BlockSpec((None, S, SLAB), lambda p: (p, 0, 0))],
        out_specs=[pl.BlockSpec((3, S, SLAB), lambda p: (0, 0, p)), pl.BlockSpec((None, 8, SLAB), lambda p: (p, 0, 0))],
        out_shape=[jax.ShapeDtypeStruct((3, S, D_MODEL), BF16), jax.ShapeDtypeStruct((N_SLABS, 8, SLAB), F32)],
        scratch_shapes=[pltpu.VMEM((S, SLAB), F32), pltpu.VMEM((BLOCK, SLAB), F32), pltpu.VMEM((npat, S, SLAB), F32),
                        pltpu.VMEM((3, S, SLAB), F32)],
        compiler_params=_cparams(("arbitrary",)),
    )(slopes, sinks, qkv, dout, o, lse)


def _place():
    x, y, c = lax.axis_index("x"), lax.axis_index("y"), lax.axis_index("c")
    return x, y, c, 2 * x + y


def _other_chips(x, y):
    return [(1 - x, y), (x, 1 - y), (1 - x, 1 - y)]


HBM_SPEC = pl.BlockSpec(memory_space=pl.ANY)


def _slot(q):
    return 2 * (q % 2) + q // 2


BIG = ("ffn1_w_in", "ffn1_w_out", "ffn2_w_in", "ffn2_w_out", "a_w_qkv", "a_w_o", "kv_w", "b_w_q", "b_w_o")
QKV_SHARD = 3 * D_MODEL // N_CHIPS
ROW_SHARD = D_MODEL // N_CHIPS


LAYER0_ITEMS = (("ffn1_w_in", 0), ("ffn1_w_out", 0), ("a_w_qkv", None), ("a_w_o", None), ("ffn2_w_in", 0),
                ("ffn2_w_out", 0), ("kv_w", None))
LAYER1_ITEMS = (("ffn1_w_in", 1), ("ffn1_w_out", 1), ("b_w_q", None), ("b_w_o", None), ("ffn2_w_in", 1),
                ("ffn2_w_out", 1))
OUT_SHARD = D_FF // N_CHIPS


def _full_shape(name):
    if name.endswith("w_in"):
        return (D_MODEL, 2 * D_FF)
    if name.endswith("w_out"):
        return (D_FF, D_MODEL)
    if name == "a_w_qkv":
        return (D_MODEL, 3 * D_MODEL)
    if name == "kv_w":
        return (N_CHIPS, 2, ROW_SHARD // 2, 2 * N_KV_B * HEAD_DIM)
    return (N_CHIPS, 2, ROW_SHARD // 2, D_MODEL)


def _gather_src(item, ref, c):
    name, layer = item
    if name.endswith("w_in"):
        return ref.at[layer, pl.ds(c * (D_MODEL // 2), D_MODEL // 2)]
    if name.endswith("w_out"):
        return ref.at[layer, pl.ds(c * (OUT_SHARD // 2), OUT_SHARD // 2)]
    if name == "a_w_qkv":
        return ref.at[0, pl.ds(c * (D_MODEL // 2), D_MODEL // 2)]
    if name == "kv_w":
        return ref.at[pl.ds(c * (ROW_SHARD // 2), ROW_SHARD // 2)]
    return ref.at[0, pl.ds(c * (ROW_SHARD // 2), ROW_SHARD // 2)]


def _gather_dst(item, ref, q, c):
    name, _ = item
    if name.endswith("w_in"):
        return ref.at[pl.ds(c * (D_MODEL // 2), D_MODEL // 2), pl.ds(_slot(q) * HALF_FF, HALF_FF)]
    if name.endswith("w_out"):
        return ref.at[pl.ds(q * OUT_SHARD + c * (OUT_SHARD // 2), OUT_SHARD // 2)]
    if name == "a_w_qkv":
        return ref.at[pl.ds(c * (D_MODEL // 2), D_MODEL // 2), pl.ds(q * QKV_SHARD, QKV_SHARD)]
    return ref.at[q, c]


def _all_gather(items, shards, small):
    n = len(items)
    r = small.shape[0]
    per = 8

    def body(*refs):
        srcs, small_ref = refs[:n], refs[n]
        dsts, s_ref = refs[n + 1:2 * n + 1], refs[2 * n + 1]
        send_sems, recv_sems = refs[2 * n + 2:]
        x, y, c, myq = _place()
        sibling = (x, y, 1 - c)
        chips = _other_chips(x, y)

        def big(t, k, src, q, h, to):
            return pltpu.make_async_remote_copy(src_ref=src, dst_ref=_gather_dst(items[t], dsts[t], q, h),
                                                send_sem=send_sems.at[per * t + k], recv_sem=recv_sems.at[per * t + k],
                                                device_id=to, device_id_type=MESH)

        def tiny(k, q, to):
            return pltpu.make_async_remote_copy(src_ref=small_ref, dst_ref=s_ref.at[q], send_sem=send_sems.at[per * n + k],
                                                recv_sem=recv_sems.at[per * n + k], device_id=to, device_id_type=MESH)

        first = []
        for j, chip in enumerate(chips):
            if j < 2:
                first += [big(t, j, _gather_src(items[t], srcs[t], c), myq, c, (*chip, c)) for t in range(n)]
            first.append(tiny(j, myq, (*chip, c)))
        own = [big(t, 6 + h, _gather_src(items[t], srcs[t], h), myq, h, sibling) for t in range(n) for h in (0, 1)]
        own.append(tiny(3, myq, sibling))
        for cp in first + own:
            cp.start()
        relay_from = ((x + 1 - c) % 2, (y + c) % 2)
        relay_to = ((x + c) % 2, (y + 1 - c) % 2, c)
        q_relay = 2 * relay_from[0] + relay_from[1]
        passed = []
        for t in range(n):
            src = _gather_src(items[t], srcs[t], c)
            for j, (cx, cy) in enumerate(chips[:2]):
                q = 2 * cx + cy
                big(t, j, src, q, c, sibling).wait_recv()
                fwd = big(t, 3 + j, _gather_dst(items[t], dsts[t], q, c), q, c, sibling)
                fwd.start()
                passed.append(fwd)
            relay = big(t, 2, _gather_dst(items[t], dsts[t], q_relay, c), q_relay, c, relay_to)
            relay.start()
            passed.append(relay)
        q_diag = 2 * chips[2][0] + chips[2][1]
        for t in range(n):
            big(t, 2, _gather_src(items[t], srcs[t], c), q_diag, c, sibling).wait_recv()
            fwd = big(t, 5, _gather_dst(items[t], dsts[t], q_diag, c), q_diag, c, sibling)
            fwd.start()
            passed.append(fwd)
        for j, (cx, cy) in enumerate(chips):
            q = 2 * cx + cy
            for t in range(n):
                big(t, 3 + j, _gather_src(items[t], srcs[t], c), q, 1 - c, sibling).wait_recv()
            tiny(j, q, sibling).wait_recv()
        for cp in own:
            cp.wait_recv()
        for cp in first + passed + own:
            cp.wait_send()

    outs = pl.pallas_call(
        body, name="all_gather_layer0",
        in_specs=[HBM_SPEC] * (n + 1), out_specs=[HBM_SPEC] * (n + 1),
        out_shape=[jax.ShapeDtypeStruct(_full_shape(name), BF16) for name, _ in items]
        + [jax.ShapeDtypeStruct((N_CHIPS, r, 128), F32)],
        scratch_shapes=[pltpu.SemaphoreType.DMA((per * n + 4,)), pltpu.SemaphoreType.DMA((per * n + 4,))],
    )(*[shards[item] for item in items], small)
    return list(outs[:n]), outs[n]


SEM_SPEC = pl.BlockSpec(memory_space=pltpu.SEMAPHORE)
DATAFLOW = pltpu.SideEffectType.DATAFLOW_SIDE_EFFECTING
PER_ITEM = 8


def _split_start(name, copies, n_sems, sources, land_shapes, after):
    n, m = len(sources), len(land_shapes)

    def body(*refs):
        srcs, lands = refs[:n], refs[n:n + m]
        send_sems, recv_sems = refs[n + m + 1], refs[n + m + 2]
        token = refs[-1]
        for src, dst_there, _, s, peer in copies(srcs, lands):
            pltpu.make_async_remote_copy(src_ref=src, dst_ref=dst_there, send_sem=send_sems.at[s], recv_sem=recv_sems.at[s],
                                         device_id=peer, device_id_type=MESH).start()
        token[...] = jnp.zeros_like(token)

    src_arrays = [pltpu.with_memory_space_constraint(a, pltpu.HBM) for a in sources]
    land_arrays = [pltpu.with_memory_space_constraint(lax.empty(s.shape, s.dtype), pltpu.HBM) for s in land_shapes]
    hbm = pl.BlockSpec(memory_space=pltpu.HBM)
    outs = pl.pallas_call(
        body, name=name,
        in_specs=[hbm] * (n + m) + [HBM_SPEC],
        out_specs=[SEM_SPEC, SEM_SPEC] + [hbm] * (n + m) + [pl.BlockSpec(memory_space=pltpu.VMEM)],
        out_shape=[pltpu.SemaphoreType.DMA((n_sems,)), pltpu.SemaphoreType.DMA((n_sems,))]
        + [pltpu.HBM(a.shape, a.dtype) for a in src_arrays + land_arrays] + [jax.ShapeDtypeStruct((8, 128), F32)],
        input_output_aliases={i: 2 + i for i in range(n + m)},
        compiler_params=pltpu.CompilerParams(has_side_effects=DATAFLOW),
    )(*src_arrays, *land_arrays, after)
    return (outs[0], outs[1], list(outs[2:2 + n]), list(outs[2 + n:2 + n + m])), outs[-1]


def _split_wait(name, copies, state, after):
    send_sems, recv_sems, srcs_thru, lands_thru = state
    n, m = len(srcs_thru), len(lands_thru)
    after = list(after) if isinstance(after, (list, tuple)) else [after]

    def body(*refs):
        srcs, lands = refs[:n], refs[n:n + m]
        send_sems, recv_sems = refs[n + m], refs[n + m + 1]
        for src, _, dst_here, s, peer in copies(srcs, lands):
            cp = pltpu.make_async_remote_copy(src_ref=src, dst_ref=dst_here, send_sem=send_sems.at[s], recv_sem=recv_sems.at[s],
                                              device_id=peer, device_id_type=MESH)
            cp.wait_send()
            cp.wait_recv()

    hbm = pl.BlockSpec(memory_space=pltpu.HBM)
    outs = pl.pallas_call(
        body, name=name,
        in_specs=[hbm] * (n + m) + [SEM_SPEC, SEM_SPEC] + [HBM_SPEC] * len(after),
        out_specs=[hbm] * (n + m),
        out_shape=[pltpu.HBM(a.shape, a.dtype) for a in srcs_thru + lands_thru],
        input_output_aliases={i: i for i in range(n + m)},
        compiler_params=pltpu.CompilerParams(has_side_effects=DATAFLOW),
    )(*srcs_thru, *lands_thru, send_sems, recv_sems, *after)
    return list(outs[:n]), list(outs[n:])


def _gather_copies(items):
    def copies(srcs, lands):
        x, y, c, myq = _place()
        out = []
        for t, item in enumerate(items):
            for h in (0, 1):
                src = _gather_src(item, srcs[t], h)
                for j, (cx, cy) in enumerate(_other_chips(x, y)):
                    out.append((src, _gather_dst(item, lands[t], myq, h), _gather_dst(item, lands[t], 2 * cx + cy, h),
                                PER_ITEM * t + 2 * j + h, (cx, cy, c)))
                out.append((src, _gather_dst(item, lands[t], myq, h), _gather_dst(item, lands[t], myq, h),
                            PER_ITEM * t + 6 + h, (x, y, 1 - c)))
        return out
    return copies


def _gather_start(items, shards, after):
    lands = [jax.ShapeDtypeStruct(_full_shape(name), BF16) for name, _ in items]
    return _split_start("gather_layer1_start", _gather_copies(items), PER_ITEM * len(items),
                        [shards[item] for item in items], lands, after)


def _gather_wait(items, state, after):
    return _split_wait("gather_layer1_wait", _gather_copies(items), state, after)[1]


def _small_all_reduce(v, after=()):
    r = v.shape[0]

    def body(v_ref, *rest):
        o_ref, buf_ref, send_sems, recv_sems = rest[len(after):]
        x, y, c, _ = _place()
        me = 4 * x + 2 * y + c
        buf_ref[me] = v_ref[...]
        copies = []
        for k in range(1, 8):
            fx, fy, fc = (k >> 2) & 1, (k >> 1) & 1, k & 1
            to = (x ^ fx, y ^ fy, c ^ fc)
            cp = pltpu.make_async_remote_copy(src_ref=v_ref, dst_ref=buf_ref.at[me], send_sem=send_sems.at[k - 1],
                                              recv_sem=recv_sems.at[k - 1], device_id=to, device_id_type=MESH)
            cp.start()
            copies.append(cp)
        for k in range(1, 8):
            fx, fy, fc = (k >> 2) & 1, (k >> 1) & 1, k & 1
            src_dev = 4 * (x ^ fx) + 2 * (y ^ fy) + (c ^ fc)
            pltpu.make_async_remote_copy(src_ref=v_ref, dst_ref=buf_ref.at[src_dev], send_sem=send_sems.at[k - 1],
                                         recv_sem=recv_sems.at[k - 1], device_id=(x, y, c), device_id_type=MESH).wait_recv()
        for cp in copies:
            cp.wait_send()
        tot = buf_ref[0]
        for i in range(1, 8):
            tot = tot + buf_ref[i]
        o_ref[...] = tot

    vm = pl.BlockSpec(memory_space=pltpu.VMEM)
    return pl.pallas_call(
        body, name="small_all_reduce", in_specs=[vm] + [HBM_SPEC] * len(after), out_specs=vm,
        out_shape=jax.ShapeDtypeStruct((r, 128), F32),
        scratch_shapes=[pltpu.VMEM((8, r, 128), F32), pltpu.SemaphoreType.DMA((7,)), pltpu.SemaphoreType.DMA((7,))],
    )(v, *after)


def _grad_view(kind, g):
    if kind == "col":
        return g.reshape(2, g.shape[0] // 2, g.shape[1])
    return g.reshape(N_CHIPS, 2, g.shape[0] // (2 * N_CHIPS), g.shape[1])


def _half_of(kind, ref, h):
    return ref.at[h] if kind == "col" else ref.at[:, h]


def _half_shape(kind, view_shape):
    return view_shape[1:] if kind == "col" else (view_shape[0],) + view_shape[2:]


def _piece_of(kind, width, colblock, ref, q):
    if kind == "col":
        return ref.at[:, pl.ds(colblock(q) * width, width)]
    return ref.at[q]


def _piece_shape(kind, width, half_shape):
    return (half_shape[0], width) if kind == "col" else half_shape[1:]


def _pair_exchange(views, kinds, name):
    n = len(views)

    def body(*refs):
        ins, outs = refs[:n], refs[n:2 * n]
        send_sems, recv_sems = refs[2 * n:]
        x, y, c, _ = _place()
        cps = []
        for t in range(n):
            cp = pltpu.make_async_remote_copy(src_ref=_half_of(kinds[t], ins[t], 1 - c), dst_ref=outs[t],
                                              send_sem=send_sems.at[t], recv_sem=recv_sems.at[t],
                                              device_id=(x, y, 1 - c), device_id_type=MESH)
            cp.start()
            cps.append(cp)
        for cp in cps:
            cp.wait()

    return pl.pallas_call(
        body, name=name, in_specs=[HBM_SPEC] * n, out_specs=[HBM_SPEC] * n,
        out_shape=[jax.ShapeDtypeStruct(_half_shape(k, v.shape), v.dtype) for k, v in zip(kinds, views)],
        scratch_shapes=[pltpu.SemaphoreType.DMA((n,)), pltpu.SemaphoreType.DMA((n,))],
    )(*views)


def _pair_sum(kind, view, recv, c, name):
    hs = recv.shape
    N = hs[-1]
    rows = hs[-2]
    tr = _pick(rows, (512, 352, 128))
    tn = _pick(N, (1408, 1024, 512))

    def body(c_ref, p_ref, r_ref, s_ref):
        s_ref[...] = (p_ref[...] + r_ref[...]).astype(BF16)

    if kind == "col":
        grid = (rows // tr, N // tn)
        mine = pl.BlockSpec((None, tr, tn), lambda i, j, c_ref: (c_ref[0], i, j))
        blk = pl.BlockSpec((tr, tn), lambda i, j, c_ref: (i, j))
        sem = ("parallel", "parallel")
    else:
        grid = (N_CHIPS, rows // tr, N // tn)
        mine = pl.BlockSpec((None, None, tr, tn), lambda q, i, j, c_ref: (q, c_ref[0], i, j))
        blk = pl.BlockSpec((None, tr, tn), lambda q, i, j, c_ref: (q, i, j))
        sem = ("parallel", "parallel", "parallel")
    return pl.pallas_call(
        body, name=name,
        grid_spec=pltpu.PrefetchScalarGridSpec(num_scalar_prefetch=1, grid=grid, in_specs=[mine, blk], out_specs=blk),
        out_shape=jax.ShapeDtypeStruct(hs, BF16),
        compiler_params=_cparams(sem),
    )(c.reshape(1).astype(jnp.int32), view, recv)


def _chip_copies(kinds, widths, colblocks):
    def copies(srcs, lands):
        x, y, c, _ = _place()
        out = []
        for j, (cx, cy) in enumerate(_other_chips(x, y)):
            for t in range(len(kinds)):
                out.append((_piece_of(kinds[t], widths[t], colblocks[t], srcs[t], 2 * cx + cy), lands[t].at[j],
                            lands[t].at[j], 3 * t + j, (cx, cy, c)))
        return out
    return copies


def _chip_land_shapes(sums, kinds, widths):
    return [jax.ShapeDtypeStruct((3,) + _piece_shape(k, w, s.shape), BF16) for k, w, s in zip(kinds, widths, sums)]


def _chip_exchange(sums, kinds, widths, colblocks, name):
    n = len(sums)
    copies = _chip_copies(kinds, widths, colblocks)

    def body(*refs):
        send_sems, recv_sems = refs[2 * n:]
        cps = [pltpu.make_async_remote_copy(src_ref=src, dst_ref=dst, send_sem=send_sems.at[s], recv_sem=recv_sems.at[s],
                                            device_id=peer, device_id_type=MESH)
               for src, dst, _, s, peer in copies(refs[:n], refs[n:2 * n])]
        for cp in cps:
            cp.start()
        for cp in cps:
            cp.wait()

    return pl.pallas_call(
        body, name=name, in_specs=[HBM_SPEC] * n, out_specs=[HBM_SPEC] * n,
        out_shape=_chip_land_shapes(sums, kinds, widths),
        scratch_shapes=[pltpu.SemaphoreType.DMA((3 * n,)), pltpu.SemaphoreType.DMA((3 * n,))],
    )(*sums)


N_DIRECT = 7


def _direct_piece(kind, width, colblock, view_ref, q, h):
    if kind == "col":
        return view_ref.at[h, :, pl.ds(colblock(q) * width, width)]
    return view_ref.at[q, h]


def _direct_copies(kinds, widths, colblocks):
    def copies(srcs, lands):
        x, y, c, myq = _place()
        out = []
        for t in range(len(kinds)):
            def piece(q, h, t=t):
                return _direct_piece(kinds[t], widths[t], colblocks[t], srcs[t], q, h)
            for j, (cx, cy) in enumerate(_other_chips(x, y)):
                for h in (0, 1):
                    out.append((piece(2 * cx + cy, h), lands[t].at[2 * j + c], lands[t].at[2 * j + h],
                                10 * t + 3 * j + c + h, (cx, cy, h)))
            out.append((piece(myq, 1 - c), lands[t].at[6], lands[t].at[6], 10 * t + 9, (x, y, 1 - c)))
        return out
    return copies


def _chip_sum(kind, own_src, recv, block_idx, c, shard_shape, layer, into, name, direct=False):
    n_recv, rows, N = recv.shape
    tr = _pick(rows, (512, 352, 128))
    tn = _pick(N, (1408, 1024, 768, 512))
    ni, nj = rows // tr, N // tn

    def body(q_ref, s_ref, r_ref, *rest):
        o_ref = rest[-1]
        tot = s_ref[...].astype(F32)
        for k in range(n_recv):
            tot = tot + r_ref[k].astype(F32)
        o_ref[...] = tot

    if direct and kind == "col":
        own = pl.BlockSpec((None, tr, tn), lambda i, j, q_ref: (q_ref[1], i, q_ref[0] * nj + j))
    elif direct:
        own = pl.BlockSpec((None, None, tr, tn), lambda i, j, q_ref: (q_ref[0], q_ref[1], i, j))
    elif kind == "col":
        own = pl.BlockSpec((tr, tn), lambda i, j, q_ref: (i, q_ref[0] * nj + j))
    else:
        own = pl.BlockSpec((None, tr, tn), lambda i, j, q_ref: (q_ref[0], i, j))
    if len(shard_shape) == 3:
        lead = 0 if layer is None else layer
        out_spec = pl.BlockSpec((None, tr, tn), lambda i, j, q_ref: (lead, q_ref[1] * ni + i, j))
    else:
        out_spec = pl.BlockSpec((tr, tn), lambda i, j, q_ref: (q_ref[1] * ni + i, j))
    in_specs = [own, pl.BlockSpec((n_recv, tr, tn), lambda i, j, q_ref: (0, i, j))]
    s = own_src
    args = [jnp.stack([block_idx, c]).astype(jnp.int32), s, recv]
    aliases = {}
    if into is not None:
        in_specs.append(HBM_SPEC)
        args.append(into)
        aliases = {3: 0}
    return pl.pallas_call(
        body, name=name,
        grid_spec=pltpu.PrefetchScalarGridSpec(num_scalar_prefetch=1, grid=(ni, nj), in_specs=in_specs, out_specs=out_spec),
        out_shape=jax.ShapeDtypeStruct(shard_shape, F32), input_output_aliases=aliases,
        compiler_params=_cparams(("parallel", "parallel")),
    )(*args)


def _half_window(ref, h):
    rows = ref.shape[-2] // 2
    if ref.ndim == 3:
        return ref.at[:, pl.ds(h * rows, rows)]
    return ref.at[pl.ds(h * rows, rows)]


def _share_halves(grads, name):
    n = len(grads)

    def body(*refs):
        outs = refs[n:2 * n]
        send_sems, recv_sems = refs[2 * n:]
        x, y, c, _ = _place()
        cps = []
        for t in range(n):
            cp = pltpu.make_async_remote_copy(src_ref=_half_window(outs[t], c), dst_ref=_half_window(outs[t], c),
                                              send_sem=send_sems.at[t], recv_sem=recv_sems.at[t],
                                              device_id=(x, y, 1 - c), device_id_type=MESH)
            cp.start()
            cps.append(cp)
        for t in range(n):
            cps[t].wait_send()
            pltpu.make_async_remote_copy(src_ref=_half_window(outs[t], c), dst_ref=_half_window(outs[t], 1 - c),
                                         send_sem=send_sems.at[t], recv_sem=recv_sems.at[t],
                                         device_id=(x, y, 1 - c), device_id_type=MESH).wait_recv()

    return pl.pallas_call(
        body, name=name, in_specs=[HBM_SPEC] * n, out_specs=[HBM_SPEC] * n,
        out_shape=[jax.ShapeDtypeStruct(g.shape, F32) for g in grads],
        input_output_aliases={t: t for t in range(n)},
        scratch_shapes=[pltpu.SemaphoreType.DMA((n,)), pltpu.SemaphoreType.DMA((n,))],
    )(*grads)


def _adamw(w, g, m, v, name):
    R, W = w.shape
    tr = _pick(R, (512, 352, 256, 32))

    def body(w_ref, g_ref, m_ref, v_ref, d_ref, nm_ref, nv_ref, go_ref):
        gv = g_ref[...]
        go_ref[...] = gv
        nm = ADAM_B1 * m_ref[...] + (1.0 - ADAM_B1) * gv
        nv = ADAM_B2 * v_ref[...] + (1.0 - ADAM_B2) * (gv * gv)
        m_hat = nm / (1.0 - ADAM_B1 ** ADAM_STEP)
        v_hat = nv / (1.0 - ADAM_B2 ** ADAM_STEP)
        d_ref[...] = -ADAM_LR * (m_hat / (jnp.sqrt(v_hat) + ADAM_EPS) + ADAM_WD * w_ref[...])
        nm_ref[...] = nm
        nv_ref[...] = nv

    blk = pl.BlockSpec((tr, W), lambda i: (i, 0))
    shp = jax.ShapeDtypeStruct((R, W), F32)
    return pl.pallas_call(
        body, name=name, grid=(R // tr,), in_specs=[blk] * 4, out_specs=[blk] * 4, out_shape=[shp] * 4,
        compiler_params=_cparams(("parallel",)),
    )(w, g, m, v)


SMALL_ROWS = 32


def _pack_small(ln_g, ln_b, sinks):
    rows = jnp.concatenate([ln_g.reshape(-1, 128), ln_b.reshape(-1, 128),
                            jnp.pad(sinks.reshape(1, -1), ((0, 0), (0, 128 - sinks.size)))], axis=0)
    return jnp.pad(rows, ((0, SMALL_ROWS - rows.shape[0]), (0, 0)))


def _unpack_small(s, ln_shape, sink_shape):
    n = ln_shape[0] * ln_shape[1] * ln_shape[2] // 128
    return s[:n].reshape(ln_shape), s[n:2 * n].reshape(ln_shape), s[2 * n, :sink_shape[1]].reshape(sink_shape)


def _ffn_fwd(xin, w_in, w_out, gain, bias, tag):
    u, h = _ffn_in(xin, w_in, "ffn_in_" + tag)
    y, yb, z = _mm_ln(h, w_out, xin, gain, bias, 0.5, "ffn_out_ln_" + tag)
    return y, yb, dict(u=u, h=h, z=z, xin=xin)


def _ffn_bwd(dz, dzc, saved, w_in, w_out, xin_b, tag, dw_dtype=F32, ln=None):
    du = _ffn_bwd_h(dzc, w_out, saved["u"], "ffn_bwd_h_" + tag)
    d_w_out = _mm_tn(saved["h"], dzc, "ffn_dwout_" + tag, out_dtype=dw_dtype)
    d_w_in = _mm_tn(xin_b, du, "ffn_dwin_" + tag, out_dtype=dw_dtype)
    dx = _mm_nt(du, w_in, "ffn_dx_" + tag, add=dz, add_scale=ALPHA, ln=ln)
    return dx, d_w_in, d_w_out


def kernel(x, ffn1_w_in, ffn1_w_out, ffn2_w_in, ffn2_w_out, ln_g, ln_b, a_w_qkv, a_w_o, kv_w, b_w_q, b_sinks, b_w_o, loss_target, m_ffn1_w_in, m_ffn1_w_out, m_ffn2_w_in, m_ffn2_w_out, m_ln_g, m_ln_b, m_a_w_qkv, m_a_w_o, m_kv_w, m_b_w_q, m_b_sinks, m_b_w_o, v_ffn1_w_in, v_ffn1_w_out, v_ffn2_w_in, v_ffn2_w_out, v_ln_g, v_ln_b, v_a_w_qkv, v_a_w_o, v_kv_w, v_b_w_q, v_b_sinks, v_b_w_o):
    ws = dict(ffn1_w_in=ffn1_w_in, ffn1_w_out=ffn1_w_out, ffn2_w_in=ffn2_w_in, ffn2_w_out=ffn2_w_out, a_w_qkv=a_w_qkv,
              a_w_o=a_w_o, kv_w=kv_w, b_w_q=b_w_q, b_w_o=b_w_o)
    ms = dict(ffn1_w_in=m_ffn1_w_in, ffn1_w_out=m_ffn1_w_out, ffn2_w_in=m_ffn2_w_in, ffn2_w_out=m_ffn2_w_out,
              a_w_qkv=m_a_w_qkv, a_w_o=m_a_w_o, kv_w=m_kv_w, b_w_q=m_b_w_q, b_w_o=m_b_w_o)
    vs = dict(ffn1_w_in=v_ffn1_w_in, ffn1_w_out=v_ffn1_w_out, ffn2_w_in=v_ffn2_w_in, ffn2_w_out=v_ffn2_w_out,
              a_w_qkv=v_a_w_qkv, a_w_o=v_a_w_o, kv_w=v_kv_w, b_w_q=v_b_w_q, b_w_o=v_b_w_o)
    _, _, c_idx, myq = _place()
    xs = x[0]
    target = loss_target[0]

    shards = {(n, l): ws[n].astype(BF16) for n, l in LAYER0_ITEMS + LAYER1_ITEMS}

    def as_weights(items, arrays):
        return {n: (a.reshape(D_MODEL, a.shape[-1]) if a.ndim == 4 else a) for (n, _), a in zip(items, arrays)}

    full0, small = _all_gather(LAYER0_ITEMS, shards, _pack_small(ln_g, ln_b, b_sinks))
    gather_state, token = _gather_start(LAYER1_ITEMS, shards, small)

    def layer1_weights(after):
        return as_weights(LAYER1_ITEMS, _gather_wait(LAYER1_ITEMS, gather_state, after))

    n_ln = ln_g.size // 128
    lg = jnp.concatenate([small[q, :n_ln].reshape(DEPTH, 3, 1, -1) for q in range(N_CHIPS)], axis=-1)
    lb = jnp.concatenate([small[q, n_ln:2 * n_ln].reshape(DEPTH, 3, 1, -1) for q in range(N_CHIPS)], axis=-1)
    lg = lg + token[0, 0]
    reducer = _GradReducer(c_idx, myq, {n: ws[n].shape for n in BIG})
    sq, grad_x, _, gg, gb, dsink_part = _local_step(xs, target, as_weights(LAYER0_ITEMS, full0), layer1_weights,
                                                    lg, lb, b_sinks.reshape(N_HEADS), reducer.begin)

    loss_row = jnp.pad(jnp.sum(sq).reshape(1, 1), ((0, 0), (0, 127)))
    dsinks = jnp.pad(dsink_part[:, 0, :].reshape(N_SLABS, 2, HEAD_DIM)[:, :, 0].reshape(1, N_HEADS), ((0, 0), (0, 128 - N_HEADS)))
    gg_full = jnp.stack([jnp.stack([jnp.sum(gg[i][j], axis=0) for j in range(3)]) for i in range(DEPTH)])
    gb_full = jnp.stack([jnp.stack([jnp.sum(gb[i][j], axis=0) for j in range(3)]) for i in range(DEPTH)])
    small_in = jnp.concatenate([loss_row, dsinks, gg_full.reshape(-1, 128), gb_full.reshape(-1, 128)], axis=0)
    small_in = jnp.pad(small_in, ((0, (-small_in.shape[0]) % 8), (0, 0)))
    def reduce_small(after):
        small_sum = _small_all_reduce(small_in, after)
        loss = small_sum[0, 0] * (0.5 / D_MODEL)
        grad_sinks = small_sum[1, :N_HEADS].reshape(b_sinks.shape)
        n_full = DEPTH * 3 * D_MODEL // 128
        cols = D_MODEL // N_CHIPS
        grad_ln_g = lax.dynamic_slice_in_dim(small_sum[2:2 + n_full].reshape(DEPTH, 3, D_MODEL), myq * cols, cols, axis=2)
        grad_ln_b = lax.dynamic_slice_in_dim(small_sum[2 + n_full:2 + 2 * n_full].reshape(DEPTH, 3, D_MODEL), myq * cols, cols, axis=2)
        return loss, grad_ln_g, grad_ln_b, grad_sinks

    return _update(reducer, grad_x, reduce_small, ws, ms, vs,
                   (ln_g, ln_b, b_sinks), (m_ln_g, m_ln_b, m_b_sinks), (v_ln_g, v_ln_b, v_b_sinks))


def _local_step(xs, target, W, layer1_weights, lg, lb, sinks, grads_ready=None):
    if grads_ready is None:
        grads_ready = lambda tag, grads, overlap: 0.0
    S = xs.shape[0]
    slopes = jnp.asarray(_alibi_slopes(N_HEADS))
    in1, out1, in2, out2 = [W["ffn1_w_in"]], [W["ffn1_w_out"]], [W["ffn2_w_in"]], [W["ffn2_w_out"]]

    y1, y1b, s1 = _ffn_fwd(xs, in1[0], out1[0], lg[0, 0], lb[0, 0], "a1")
    qkv_a = _mm_nn(y1b, W["a_w_qkv"], F32, "qkv_a", split=True)
    mix_a, o_a, lse_a = _attn_fwd(qkv_a, slopes, None, PATTERNS_A, "attn_a_fwd")
    y2, y2b, z2 = _mm_ln(mix_a, W["a_w_o"], y1, lg[0, 1], lb[0, 1], 1.0, "attn_a_out_ln")
    y3, y3b, s3 = _ffn_fwd(y2, in2[0], out2[0], lg[0, 2], lb[0, 2], "a2")
    kv_w_rep = jnp.broadcast_to(W["kv_w"].reshape(D_MODEL, 2, N_KV_B, 1, HEAD_DIM),
                                (D_MODEL, 2, N_KV_B, GROUP_B, HEAD_DIM)).reshape(D_MODEL, 2 * D_MODEL)
    kv_rep = _mm_nn(y3b, kv_w_rep, F32, "kv_proj", split=(1, 2))
    W = dict(W, **layer1_weights(kv_rep))
    in1, out1, in2, out2 = (in1 + [W["ffn1_w_in"]], out1 + [W["ffn1_w_out"]], in2 + [W["ffn2_w_in"]],
                            out2 + [W["ffn2_w_out"]])
    y4, y4b, s4 = _ffn_fwd(y3, in1[1], out1[1], lg[1, 0], lb[1, 0], "b1")
    qkv_b = _mm_nn(y4b, W["b_w_q"], F32, "q_b", split=(0, 1), into=kv_rep)
    mix_b, o_b, lse_b = _attn_fwd(qkv_b, slopes, sinks, PATTERNS_B, "attn_b_fwd")
    y5, y5b, z5 = _mm_ln(mix_b, W["b_w_o"], y4, lg[1, 1], lb[1, 1], 1.0, "attn_b_out_ln")
    y6, _, s6 = _ffn_fwd(y5, in2[1], out2[1], lg[1, 2], lb[1, 2], "b2")

    gr = {n: None for n in BIG}
    gg = [[None] * 3 for _ in range(DEPTH)]
    gb = [[None] * 3 for _ in range(DEPTH)]
    dz6, dz6c, gg[1][2], gb[1][2], sq = _loss_ln_bwd(y6, target, s6["z"], lg[1, 2], 0.5, "loss_ln_bwd")

    (dz5, dz5b, gg[1][1], gb[1][1]), d_in2_b, d_out2_b = _ffn_bwd(dz6, dz6c, s6, in2[1], out2[1], y5b, "b2", BF16,
                                                                  ln=(z5, lg[1, 1], 1.0))
    gr["b_w_o"] = _mm_tn(mix_b, dz5b, "d_b_w_o", out_dtype=BF16)
    dmix_b = _mm_nt(dz5b, W["b_w_o"], "d_mix_b")
    dqkv_b, dsink_part = _attn_bwd(qkv_b, dmix_b, o_b, lse_b, slopes, sinks, PATTERNS_B, "attn_b_bwd")
    dq_b = (dqkv_b, 0)
    gr["b_w_q"] = _mm_tn(y4b, dq_b, "d_b_w_q", out_dtype=BF16)
    dz4, dz4c, gg[1][0], gb[1][0] = _mm_nt(dq_b, W["b_w_q"], "d_y4", add=dz5, add_scale=ALPHA, ln=(s4["z"], lg[1, 0], 0.5))
    dy3, d_in1_b, d_out1_b = _ffn_bwd(dz4, dz4c, s4, in1[1], out1[1], y3b, "b1", BF16)
    gr["kv_w"] = _d_kv_w(y3b, dqkv_b, "d_kv_w")
    tok = grads_ready("l1", {("ffn2_w_in", 1): d_in2_b, ("ffn2_w_out", 1): d_out2_b, ("b_w_o", None): gr["b_w_o"],
                             ("b_w_q", None): gr["b_w_q"], ("ffn1_w_in", 1): d_in1_b, ("ffn1_w_out", 1): d_out1_b,
                             ("kv_w", None): gr["kv_w"]}, True)
    lg0 = lg[0] + tok
    dz3, dz3c, gg[0][2], gb[0][2] = _mm_nt(dqkv_b, kv_w_rep, "d_y3_kv", add=dy3, add_scale=1.0, split=(1, 2),
                                           ln=(s3["z"], lg0[2], 0.5))

    (dz2, dz2b, gg[0][1], gb[0][1]), d_in2_a, d_out2_a = _ffn_bwd(dz3, dz3c, s3, in2[0], out2[0], y2b, "a2", BF16,
                                                                  ln=(z2, lg0[1], 1.0))
    tok = grads_ready("a2", {("ffn2_w_in", 0): d_in2_a, ("ffn2_w_out", 0): d_out2_a}, True)
    lg0 = lg0 + tok
    gr["a_w_o"] = _mm_tn(mix_a, dz2b, "d_a_w_o", out_dtype=BF16)
    dmix_a = _mm_nt(dz2b, W["a_w_o"], "d_mix_a")
    dqkv_a, _ = _attn_bwd(qkv_a, dmix_a, o_a, lse_a, slopes, None, PATTERNS_A, "attn_a_bwd")
    gr["a_w_qkv"] = _mm_tn(y1b, dqkv_a, "d_a_w_qkv", split=True, out_dtype=BF16)
    tok = grads_ready("mix", {("a_w_o", None): gr["a_w_o"], ("a_w_qkv", None): gr["a_w_qkv"]}, True)
    lg0 = lg0 + tok
    dz1, dz1c, gg[0][0], gb[0][0] = _mm_nt(dqkv_a, W["a_w_qkv"], "d_y1", add=dz2, add_scale=ALPHA, split=True,
                                           ln=(s1["z"], lg0[0], 0.5))
    grad_x, d_in1_a, d_out1_a = _ffn_bwd(dz1, dz1c, s1, in1[0], out1[0], xs, "a1", BF16)
    grads_ready("a1", {("ffn1_w_in", 0): d_in1_a, ("ffn1_w_out", 0): d_out1_a}, True)
    gr["ffn1_w_in"] = [d_in1_a, d_in1_b]
    gr["ffn1_w_out"] = [d_out1_a, d_out1_b]
    gr["ffn2_w_in"] = [d_in2_a, d_in2_b]
    gr["ffn2_w_out"] = [d_out2_a, d_out2_b]
    return sq, grad_x, gr, gg, gb, dsink_part


def _grad_item(name, layer, g):
    if name.endswith("w_in"):
        return (g, "col", HALF_FF, _slot, name, layer)
    if name.endswith("w_out"):
        return (g, "row", D_MODEL, None, name, layer)
    if name == "a_w_qkv":
        return (g, "col", QKV_SHARD, lambda q: q, name, None)
    return (g, "row", g.shape[1], None, name, None)


class _GradReducer:
    def __init__(self, c_idx, myq, shard_shapes):
        self.c_idx, self.myq, self.shard_shapes = c_idx, myq, shard_shapes
        self.groups = []

    def begin(self, tag, grads, overlap):
        items = [_grad_item(n, l, g) for (n, l), g in grads.items()]
        kinds, widths, colblocks = [it[1] for it in items], [it[2] for it in items], [it[3] for it in items]
        views = [_grad_view(k, it[0]) for k, it in zip(kinds, items)]
        if overlap:
            lands = [jax.ShapeDtypeStruct((N_DIRECT,) + _piece_shape(k, w, _half_shape(k, v.shape)), BF16)
                     for k, w, v in zip(kinds, widths, views)]
            state, token = _split_start("grad_direct_start_" + tag, _direct_copies(kinds, widths, colblocks), 10 * len(items),
                                        views, lands, jnp.zeros((8, 128), F32))
            self.groups.append((tag, items, None, state, token))
            return token[0, 0]
        from_sibling = _pair_exchange(views, kinds, "grad_pair_exchange_" + tag)
        sums = [_pair_sum(k, v, r, self.c_idx, "pair_sum_%s_%d" % (tag, t))
                for t, (k, v, r) in enumerate(zip(kinds, views, from_sibling))]
        self.groups.append((tag, items, sums, None, None))
        return 0.0

    def _sum_group(self, tag, items, sums, received, direct):
        for t, (it, s, r) in enumerate(zip(items, sums, received)):
            _, k, _, cb, name, layer = it
            own = cb(self.myq) if k == "col" else self.myq
            self.half_done[name] = _chip_sum(k, s, r, own, self.c_idx, self.shard_shapes[name], layer,
                                             self.half_done.get(name), "chip_sum_%s_%d" % (tag, t), direct=direct)

    def finish_first(self, after):
        self.half_done, self.late, early = {}, [], []
        started = [after]
        for g, (tag, items, sums, state, token) in enumerate(self.groups):
            kinds, widths, colblocks = [it[1] for it in items], [it[2] for it in items], [it[3] for it in items]
            if state is None:
                copies = _chip_copies(kinds, widths, colblocks)
                state, token = _split_start("grad_chip_start_" + tag, copies, 3 * len(items), sums,
                                            _chip_land_shapes(sums, kinds, widths), sums[-1])
                self.late.append((tag, items, copies, state, False))
                started.append(token)
            elif g == len(self.groups) - 1:
                self.late.append((tag, items, _direct_copies(kinds, widths, colblocks), state, True))
                started.append(token)
            else:
                early.append((tag, items, _direct_copies(kinds, widths, colblocks), state))
        for tag, items, copies, state in early:
            views, received = _split_wait("grad_direct_wait_" + tag, copies, state, started)
            self._sum_group(tag, items, views, received, True)
        late_names = {it[4] for _, items, _, _, _ in self.late for it in items}
        names = [n for n in BIG if n not in late_names]
        return dict(zip(names, _share_halves([self.half_done[n] for n in names], "grad_share_halves_first")))

    def finish_rest(self, after):
        names = []
        for tag, items, copies, state, direct in self.late:
            sums, received = _split_wait("grad_late_wait_" + tag, copies, state, after)
            self._sum_group(tag, items, sums, received, direct)
            names += [it[4] for it in items if it[4] not in names]
        return dict(zip(names, _share_halves([self.half_done[n] for n in names], "grad_share_halves_rest")))


def _update(reducer, grad_x, reduce_small, ws, ms, vs, small_w, small_m, small_v):
    ln_g, ln_b, b_sinks = small_w
    m_ln_g, m_ln_b, m_b_sinks = small_m
    v_ln_g, v_ln_b, v_b_sinks = small_v

    grads, deltas, new_m, new_v = {}, {}, {}, {}

    def update(some):
        done = []
        for name in some:
            shp = ws[name].shape
            flat = lambda a: a.reshape(-1, shp[-1])
            d, nm, nv, g = _adamw(flat(ws[name]), flat(some[name]), flat(ms[name]), flat(vs[name]), "adamw_" + name)
            grads[name], deltas[name], new_m[name], new_v[name] = g.reshape(shp), d.reshape(shp), nm.reshape(shp), nv.reshape(shp)
            done.append(d)
        return done

    rest = reducer.finish_rest(update(reducer.finish_first(grad_x)))
    loss, grad_ln_g, grad_ln_b, grad_sinks = reduce_small(list(rest.values()))
    update(rest)
    delta_s, nm_s, nv_s, _ = _adamw(_pack_small(ln_g, ln_b, b_sinks), _pack_small(grad_ln_g, grad_ln_b, grad_sinks),
                                    _pack_small(m_ln_g, m_ln_b, m_b_sinks), _pack_small(v_ln_g, v_ln_b, v_b_sinks), "adamw_small")
    for d, blob in ((grads, None), (deltas, delta_s), (new_m, nm_s), (new_v, nv_s)):
        if blob is None:
            d["ln_g"], d["ln_b"], d["b_sinks"] = grad_ln_g, grad_ln_b, grad_sinks
        else:
            d["ln_g"], d["ln_b"], d["b_sinks"] = _unpack_small(blob, ln_g.shape, b_sinks.shape)

    order = ("ffn1_w_in", "ffn1_w_out", "ffn2_w_in", "ffn2_w_out", "ln_g", "ln_b", "a_w_qkv", "a_w_o", "kv_w", "b_w_q",
             "b_sinks", "b_w_o")
    outs = [loss, grad_x[None]]
    for d in (grads, deltas, new_m, new_v):
        outs += [d[n] for n in order]
    return tuple(outs)
```

```python
import numpy as np
import jax
import jax.numpy as jnp
from jax import lax
from jax.experimental import pallas as pl
from jax.experimental.pallas import tpu as pltpu

F32 = jnp.float32
BF16 = jnp.bfloat16

D_MODEL = 1024
D_FF = 2816
HALF_FF = D_FF // 2
HEAD_DIM = 64
N_HEADS = 16
N_KV_B = 4
GROUP_B = N_HEADS // N_KV_B
DEPTH = 2
ALPHA = (2.0 * DEPTH) ** 0.25
LN_EPS = 1e-5
BLOCK = 128
SLAB = 128
N_SLABS = D_MODEL // SLAB
PATTERNS_A = ((1, 128, 1.0), (4, 128, 4.0), (16, 128, 16.0))
PATTERNS_B = ((1, 127, 1.0),)
NEG = -1e30

ADAM_LR = 0.001
ADAM_B1 = 0.9
ADAM_B2 = 0.999
ADAM_EPS = 1e-08
ADAM_WD = 0.01
ADAM_STEP = 10

N_CHIPS = 4
VMEM_LIMIT = 56 * 1024 * 1024
WHOLE_WEIGHT_BYTES = 12 * 1024 * 1024
MESH = pl.DeviceIdType.MESH


def _alibi_slopes(n):
    return np.array([2.0 ** (-8.0 * (h + 1) / n) for h in range(n)], dtype=np.float32)


def _cparams(sem=None, vmem=VMEM_LIMIT):
    return pltpu.CompilerParams(dimension_semantics=sem, vmem_limit_bytes=vmem)


_DIMS = {"nn": ((1,), (0,)), "nt": ((1,), (1,)), "tn": ((0,), (0,))}


def _unlead(x):
    if isinstance(x, tuple):
        return x[0], x[1], x[0].shape[1:]
    return x, None, x.shape


def _bspec(block, imap, lead=None, **kw):
    if lead is None:
        return pl.BlockSpec(block, imap, **kw)
    return pl.BlockSpec((None,) + tuple(block), lambda *g: (lead,) + tuple(imap(*g)), **kw)


def _ln_bwd_math(zv, dyv, gain):
    rows = zv.shape[0]
    mu = jnp.mean(zv, axis=-1, keepdims=True)
    zc = zv - mu
    var = jnp.mean(zc * zc, axis=-1, keepdims=True)
    rstd = lax.rsqrt(var + LN_EPS)
    xhat = zc * rstd
    dyg = dyv * gain
    m1 = jnp.mean(dyg, axis=-1, keepdims=True)
    m2 = jnp.mean(dyg * xhat, axis=-1, keepdims=True)
    dz = rstd * (dyg - m1 - xhat * m2)
    pg = jnp.sum((dyv * xhat).reshape(rows // 8, 8, D_MODEL), axis=0)
    pb = jnp.sum(dyv.reshape(rows // 8, 8, D_MODEL), axis=0)
    return dz, pg, pb


def _matmul(a, b, mode, out_dtype, tm, tn, tk, name, add=None, add_scale=1.0, split=False, into=None, ln=None,
            after=None):
    out_spec = pl.BlockSpec((tm, tn), lambda i, j, k: (i, j))
    base, count = (0, 3) if split is True else (split or (0, 0))
    if mode == "nn":
        a, al, (M, K) = _unlead(a)
        b, bl, (K2, N) = _unlead(b)
        a_spec = _bspec((tm, tk), lambda i, j, k: (i, k), al)
        b_spec = _bspec((tk, tn), lambda i, j, k: (k, j), bl)
        out_struct = jax.ShapeDtypeStruct((M, N), out_dtype)
        if split:
            assert tn == D_MODEL and N == count * tn
            out_spec = pl.BlockSpec((None, tm, tn), lambda i, j, k: (j + base, i, 0))
            out_struct = jax.ShapeDtypeStruct((3, M, tn), out_dtype)
    elif mode == "nt":
        b, bl, (N, K2) = _unlead(b)
        if split:
            M, K = a.shape[1], count * a.shape[2]
            if tk == K:
                a_spec = [pl.BlockSpec((None, tm, D_MODEL), lambda i, j, k, s=s: (s + base, i, 0)) for s in range(count)]
            else:
                assert tk == D_MODEL
                a_spec = pl.BlockSpec((None, tm, tk), lambda i, j, k: (k + base, i, 0))
        else:
            a, al, (M, K) = _unlead(a)
            a_spec = _bspec((tm, tk), lambda i, j, k: (i, k), al)
        whole_b = {"pipeline_mode": pl.Buffered(1)} if (tn, tk) == (N, K2) else {}
        b_spec = _bspec((tn, tk), lambda i, j, k: (j, k), bl, **whole_b)
        out_struct = jax.ShapeDtypeStruct((M, N), out_dtype)
    else:
        a, al, (K, M) = _unlead(a)
        if split:
            assert tn == D_MODEL
            K2, N = b.shape[1], count * b.shape[2]
            b_spec = pl.BlockSpec((None, tk, tn), lambda i, j, k: (j + base, k, 0))
        else:
            b, bl, (K2, N) = _unlead(b)
            b_spec = _bspec((tk, tn), lambda i, j, k: (k, j), bl)
        a_spec = _bspec((tk, tm), lambda i, j, k: (k, i), al)
        out_struct = jax.ShapeDtypeStruct((M, N), out_dtype)
    assert K == K2 and M % tm == 0 and N % tn == 0 and K % tk == 0, (a.shape, b.shape, mode, tm, tn, tk)
    nk = K // tk
    dims = (_DIMS[mode], ((), ()))
    has_add = add is not None

    narrow = out_dtype != F32
    assert not (narrow and has_add)
    if ln is not None:
        assert has_add and mode == "nt" and tn == N == D_MODEL

    a_specs = a_spec if isinstance(a_spec, list) else [a_spec]
    n_a = len(a_specs)

    def body(*refs):
        if after is not None:
            refs = refs[1:]
        a_refs, refs = refs[:n_a], refs[n_a - 1:]
        if into is not None:
            refs = refs[:2] + refs[3:]
        if ln is not None:
            a_ref, b_ref, add_ref, z_ref, g_ref, o_ref, dzc_ref, gg_ref, gb_ref = refs
            acc_ref = o_ref
        elif has_add:
            a_ref, b_ref, add_ref, o_ref = refs
            acc_ref = o_ref
        elif narrow:
            a_ref, b_ref, o_ref, acc_ref = refs
        else:
            a_ref, b_ref, o_ref = refs
            acc_ref = o_ref
        k = pl.program_id(2)
        if n_a == 1:
            part = lax.dot_general(a_ref[...].astype(BF16), b_ref[...].astype(BF16), dims, preferred_element_type=F32)
        else:
            part = sum(lax.dot_general(r[...], b_ref[:, s * D_MODEL:(s + 1) * D_MODEL], dims, preferred_element_type=F32)
                       for s, r in enumerate(a_refs))
        if has_add:
            @pl.when(k == 0)
            def _():
                acc_ref[...] = part + add_scale * add_ref[...]
        else:
            @pl.when(k == 0)
            def _():
                acc_ref[...] = part

        @pl.when(k > 0)
        def _():
            acc_ref[...] += part

        if narrow:
            @pl.when(k == nk - 1)
            def _():
                o_ref[...] = acc_ref[...].astype(out_dtype)

        if ln is not None:
            @pl.when(k == nk - 1)
            def _():
                dz, pg, pb = _ln_bwd_math(z_ref[...], o_ref[...], g_ref[...])
                o_ref[...] = dz
                dzc_ref[...] = (ln[2] * dz).astype(BF16)
                first = pl.program_id(0) == 0

                @pl.when(first)
                def _():
                    gg_ref[...] = pg
                    gb_ref[...] = pb

                @pl.when(jnp.logical_not(first))
                def _():
                    gg_ref[...] += pg
                    gb_ref[...] += pb

    in_specs = [*a_specs, b_spec]
    args = [a] * n_a + [b]
    aliases = {}
    if into is not None:
        assert mode == "nn" and split and not has_add and n_a == 1
        in_specs.append(pl.BlockSpec(memory_space=pl.ANY))
        args.append(into)
        aliases = {2: 0}
    if has_add:
        in_specs.append(pl.BlockSpec((tm, tn), lambda i, j, k: (i, j)))
        args.append(add)
    sem = ("parallel", "parallel", "arbitrary")
    if ln is not None:
        part8 = pl.BlockSpec((8, N), lambda i, j, k: (0, 0))
        in_specs += [pl.BlockSpec((tm, tn), lambda i, j, k: (i, j)), pl.BlockSpec((1, N), lambda i, j, k: (0, 0))]
        args += [ln[0], ln[1]]
        out_spec = [out_spec, pl.BlockSpec((tm, tn), lambda i, j, k: (i, j)), part8, part8]
        out_struct = [out_struct, jax.ShapeDtypeStruct((M, N), BF16), jax.ShapeDtypeStruct((8, N), F32),
                      jax.ShapeDtypeStruct((8, N), F32)]
        sem = ("arbitrary", "arbitrary", "arbitrary")
    if after is not None:
        assert into is None
        in_specs.insert(0, pl.BlockSpec(memory_space=pl.ANY))
        args.insert(0, after)
    return pl.pallas_call(
        body, name=name, grid=(M // tm, N // tn, nk),
        in_specs=in_specs, out_specs=out_spec, out_shape=out_struct, input_output_aliases=aliases,
        scratch_shapes=[pltpu.VMEM((tm, tn), F32)] if narrow else [],
        compiler_params=_cparams(sem),
    )(*args)


def _pick(n, cands):
    for c in cands:
        if n % c == 0:
            return c
    raise ValueError((n, cands))


def _mm_nn(a, b, out_dtype, name, split=False, into=None):
    M, K = _unlead(a)[2]
    N = _unlead(b)[2][1]
    return _matmul(a, b, "nn", out_dtype, _pick(M, (1024, 512, 256)), _pick(N, (1024, 512)), _pick(K, (1024, 512)), name,
                   split=split, into=into)


def _mm_nt(a, b, name, add=None, add_scale=1.0, split=False, ln=None, after=None):
    M = a.shape[1] if split else _unlead(a)[2][0]
    N, K = _unlead(b)[2]
    tn = _pick(N, (1024, 512))
    if tn == N and N * K * 2 <= WHOLE_WEIGHT_BYTES:
        tm, tk = _pick(M, (512, 256)), K
    else:
        tms = (512, 256) if ln is not None else (1024, 512, 256)
        tm, tk = _pick(M, tms), _pick(D_MODEL if split else K, (2816, 1024, 512))
    return _matmul(a, b, "nt", F32, tm, tn, tk, name, add=add, add_scale=add_scale, split=split, ln=ln, after=after)


def _mm_tn(a, b, name, split=False, out_dtype=F32):
    K, M = _unlead(a)[2]
    N = D_MODEL if split else _unlead(b)[2][1]
    return _matmul(a, b, "tn", out_dtype, _pick(M, (1024, 1408, 512)), _pick(N, (1408, 1024, 512)),
                   _pick(K, (2048, 1024, 512, 256)), name, split=split)


def _d_kv_w(y, dqkv, name):
    S = y.shape[0]
    tk = _pick(S, (1024, 512))
    nk = S // tk
    width = N_KV_B * HEAD_DIM
    r, c = np.arange(D_MODEL)[:, None], np.arange(width)[None, :]
    fold = jnp.asarray((r // (GROUP_B * HEAD_DIM) == c // HEAD_DIM) & (r % HEAD_DIM == c % HEAD_DIM), BF16)

    def body(y_ref, dk_ref, dv_ref, f_ref, o_ref, acc_ref):
        k = pl.program_id(0)
        summed = jnp.concatenate([jnp.dot(ref[...], f_ref[...], preferred_element_type=F32).astype(BF16)
                                  for ref in (dk_ref, dv_ref)], axis=1)
        part = lax.dot_general(summed, y_ref[...], (_DIMS["tn"], ((), ())), preferred_element_type=F32)

        @pl.when(k == 0)
        def _():
            acc_ref[...] = part

        @pl.when(k > 0)
        def _():
            acc_ref[...] += part

        @pl.when(k == nk - 1)
        def _():
            o_ref[...] = acc_ref[...].T.astype(BF16)

    return pl.pallas_call(
        body, name=name, grid=(nk,),
        in_specs=[pl.BlockSpec((tk, D_MODEL), lambda k: (k, 0)),
                  pl.BlockSpec((None, tk, D_MODEL), lambda k: (1, k, 0)),
                  pl.BlockSpec((None, tk, D_MODEL), lambda k: (2, k, 0)),
                  pl.BlockSpec((D_MODEL, width), lambda k: (0, 0))],
        out_specs=pl.BlockSpec((D_MODEL, 2 * width), lambda k: (0, 0)),
        out_shape=jax.ShapeDtypeStruct((D_MODEL, 2 * width), BF16),
        scratch_shapes=[pltpu.VMEM((2 * width, D_MODEL), F32)],
        compiler_params=_cparams(("arbitrary",)),
    )(y, dqkv, dqkv, fold)


def _ffn_in(x, w, name):
    S = x.shape[0]
    tm = _pick(S, (512, 256))
    w, wl, _ = _unlead(w)

    def body(x_ref, w_ref, t_ref, h_ref):
        acc = jnp.dot(x_ref[...].astype(BF16), w_ref[...], preferred_element_type=F32)
        g = acc[:, :HALF_FF]
        up = acc[:, HALF_FF:]
        sg = jax.nn.sigmoid(g)
        silu = g * sg
        t_ref[:, :HALF_FF] = (up * (sg * (1.0 + g * (1.0 - sg)))).astype(BF16)
        t_ref[:, HALF_FF:] = silu.astype(BF16)
        h_ref[...] = (silu * up).astype(BF16)

    return pl.pallas_call(
        body, name=name, grid=(2, S // tm),
        in_specs=[pl.BlockSpec((tm, D_MODEL), lambda j, i: (i, 0)),
                  _bspec((D_MODEL, D_FF), lambda j, i: (0, j), wl)],
        out_specs=[pl.BlockSpec((tm, D_FF), lambda j, i: (i, j)),
                   pl.BlockSpec((tm, HALF_FF), lambda j, i: (i, j))],
        out_shape=[jax.ShapeDtypeStruct((S, 2 * D_FF), BF16), jax.ShapeDtypeStruct((S, D_FF), BF16)],
        compiler_params=_cparams(("parallel", "parallel")),
    )(x, w)


def _ffn_bwd_h(dzc, w_out, u, name):
    S = dzc.shape[0]
    tm = _pick(S, (512, 256))
    w_out, wl, _ = _unlead(w_out)

    def body(dz_ref, w_ref, t_ref, du_ref):
        dh = lax.dot_general(dz_ref[...], w_ref[...], (((1,), (1,)), ((), ())), preferred_element_type=F32)
        du_ref[:, :HALF_FF] = (dh * t_ref[:, :HALF_FF].astype(F32)).astype(BF16)
        du_ref[:, HALF_FF:] = (dh * t_ref[:, HALF_FF:].astype(F32)).astype(BF16)

    return pl.pallas_call(
        body, name=name, grid=(2, S // tm),
        in_specs=[pl.BlockSpec((tm, D_MODEL), lambda j, i: (i, 0)),
                  _bspec((HALF_FF, D_MODEL), lambda j, i: (j, 0), wl),
                  pl.BlockSpec((tm, D_FF), lambda j, i: (i, j))],
        out_specs=pl.BlockSpec((tm, D_FF), lambda j, i: (i, j)),
        out_shape=jax.ShapeDtypeStruct((S, 2 * D_FF), BF16),
        compiler_params=_cparams(("parallel", "parallel")),
    )(dzc, w_out, u)


def _mm_ln(a, w, resid, gain, bias, c, name):
    S, K = a.shape
    tm = _pick(S, (512, 256))
    w, wl, _ = _unlead(w)

    def body(a_ref, w_ref, r_ref, g_ref, b_ref, y_ref, yb_ref, z_ref):
        z = ALPHA * r_ref[...] + c * jnp.dot(a_ref[...], w_ref[...], preferred_element_type=F32)
        mu = jnp.mean(z, axis=-1, keepdims=True)
        zc = z - mu
        var = jnp.mean(zc * zc, axis=-1, keepdims=True)
        y = zc * lax.rsqrt(var + LN_EPS) * g_ref[...] + b_ref[...]
        z_ref[...] = z
        y_ref[...] = y
        yb_ref[...] = y.astype(BF16)

    row = pl.BlockSpec((tm, D_MODEL), lambda i: (i, 0))
    vec = pl.BlockSpec((1, D_MODEL), lambda i: (0, 0))
    return pl.pallas_call(
        body, name=name, grid=(S // tm,),
        in_specs=[pl.BlockSpec((tm, K), lambda i: (i, 0)), _bspec((K, D_MODEL), lambda i: (0, 0), wl), row, vec, vec],
        out_specs=[row, row, row],
        out_shape=[jax.ShapeDtypeStruct((S, D_MODEL), F32), jax.ShapeDtypeStruct((S, D_MODEL), BF16),
                   jax.ShapeDtypeStruct((S, D_MODEL), F32)],
        compiler_params=_cparams(("parallel",)),
    )(a, w, resid, gain, bias)


def _loss_ln_bwd(y, t, z, gain, c, name):
    S = y.shape[0]
    tm = _pick(S, (512, 256))

    def body(y_ref, t_ref, z_ref, g_ref, dz_ref, dzc_ref, gg_ref, gb_ref, sq_ref):
        i = pl.program_id(0)
        e = y_ref[...] - t_ref[...]
        dz, pg, pb = _ln_bwd_math(z_ref[...], e * (1.0 / D_MODEL), g_ref[...])
        dz_ref[...] = dz
        dzc_ref[...] = (c * dz).astype(BF16)
        ps = jnp.sum((e * e).reshape(tm // 8, 8, D_MODEL), axis=0)

        @pl.when(i == 0)
        def _():
            gg_ref[...] = pg
            gb_ref[...] = pb
            sq_ref[...] = ps

        @pl.when(i > 0)
        def _():
            gg_ref[...] += pg
            gb_ref[...] += pb
            sq_ref[...] += ps

    row = pl.BlockSpec((tm, D_MODEL), lambda i: (i, 0))
    part = pl.BlockSpec((8, D_MODEL), lambda i: (0, 0))
    part_shape = jax.ShapeDtypeStruct((8, D_MODEL), F32)
    return pl.pallas_call(
        body, name=name, grid=(S // tm,),
        in_specs=[row, row, row, pl.BlockSpec((1, D_MODEL), lambda i: (0, 0))],
        out_specs=[row, row, part, part, part],
        out_shape=[jax.ShapeDtypeStruct((S, D_MODEL), F32), jax.ShapeDtypeStruct((S, D_MODEL), BF16),
                   part_shape, part_shape, part_shape],
        compiler_params=_cparams(("arbitrary",)),
    )(y, t, z, gain)


def _rows(start, d):
    if d == 1:
        return pl.ds(pl.multiple_of(start, BLOCK), BLOCK)
    return pl.ds(start, BLOCK, stride=d)


def _ld(ref, start, d):
    return ref[_rows(start, d), :]


def _ld3(ref, lead, start, d):
    return ref[lead, _rows(start, d), :]


def _st3(ref, lead, start, d, val):
    ref[lead, _rows(start, d), :] = val


def _acc3(ref, lead, start, d, val):
    ref[lead, _rows(start, d), :] = ref[lead, _rows(start, d), :] + val


def _band_consts(slope0, slope1, maxd, scale):
    row = lax.broadcasted_iota(jnp.int32, (2 * BLOCK, 2 * BLOCK), 0)
    kj = lax.broadcasted_iota(jnp.int32, (2 * BLOCK, 2 * BLOCK), 1)
    top = row < BLOCK
    dist = BLOCK + jnp.where(top, row, row - BLOCK) - kj
    slope = jnp.where(top, slope0, slope1)
    base = jnp.where((dist >= 0) & (dist <= maxd), -(slope * (dist.astype(F32) * scale)), NEG)
    return base, kj < BLOCK


def _stack_heads(x, lo):
    return jnp.concatenate([jnp.where(lo, x, 0.0), jnp.where(lo, 0.0, x)], axis=0)


def _unstack_heads(x2, lo):
    return jnp.where(lo, x2[:BLOCK], x2[BLOCK:])


def _scores(q2, k2, base, prev_keys, first):
    s = lax.dot_general(q2, k2, (((1,), (1,)), ((), ())), preferred_element_type=F32) * (HEAD_DIM ** -0.5) + base
    return jnp.where(jnp.logical_and(prev_keys, first), NEG, s)


def _softmax_weights(ls):
    mx = ls[0]
    for l in ls[1:]:
        mx = jnp.maximum(mx, l)
    es = [jnp.exp(l - mx) for l in ls]
    tot = es[0]
    for e in es[1:]:
        tot = tot + e
    inv = 1.0 / tot
    return [e * inv for e in es]


def _attn_fwd(qkv, slopes, sinks, patterns, name):
    S = qkv.shape[1]
    npat = len(patterns)
    has_sink = sinks is not None
    if not has_sink:
        sinks = jnp.zeros((N_HEADS,), F32)
    rows_c = 256

    def body(slopes_ref, sinks_ref, x_ref, mix_ref, o_ref, lse_ref, o_scr, lse_scr):
        p = pl.program_id(0)
        lo = lax.broadcasted_iota(jnp.int32, (BLOCK, SLAB), 1) < HEAD_DIM
        top1 = lax.broadcasted_iota(jnp.int32, (2 * BLOCK, 1), 0) < BLOCK
        sk2 = jnp.where(top1, sinks_ref[2 * p], sinks_ref[2 * p + 1])
        for pi, (d, maxd, scale) in enumerate(patterns):
            nb = S // d // BLOCK
            base, prev_keys = _band_consts(slopes_ref[2 * p], slopes_ref[2 * p + 1], maxd, scale)

            def blk(t, carry, pi=pi, d=d, nb=nb, base=base, prev_keys=prev_keys):
                r = t // nb
                n = t - r * nb
                start = r + (d * BLOCK) * n
                prev = jnp.where(n > 0, start - d * BLOCK, start)
                q2 = _stack_heads(_ld3(x_ref, 0, start, d), lo).astype(BF16)
                k2 = jnp.concatenate([_ld3(x_ref, 1, prev, d), _ld3(x_ref, 1, start, d)], axis=0).astype(BF16)
                v2 = jnp.concatenate([_ld3(x_ref, 2, prev, d), _ld3(x_ref, 2, start, d)], axis=0).astype(BF16)
                s = _scores(q2, k2, base, prev_keys, n == 0)
                m = jnp.max(s, axis=-1, keepdims=True)
                if has_sink:
                    m = jnp.maximum(m, sk2)
                e = jnp.exp(s - m)
                den = jnp.sum(e, axis=-1, keepdims=True)
                if has_sink:
                    den = den + jnp.exp(sk2 - m)
                o2 = jnp.dot((e / den).astype(BF16), v2, preferred_element_type=F32)
                _st3(o_scr, pi, start, d, _unstack_heads(o2, lo))
                _st3(lse_scr, pi, start, d, _unstack_heads(m + jnp.log(den), lo))
                return carry

            lax.fori_loop(0, d * nb, blk, 0, unroll=8)

        lane_c = lax.broadcasted_iota(jnp.int32, (rows_c, SLAB), 1)

        def comb(ci, carry):
            rows = pl.ds(pl.multiple_of(ci * rows_c, rows_c), rows_c)
            ls = [lse_scr[i, rows, :] for i in range(npat)]
            packed = jnp.zeros((rows_c, SLAB), F32)
            for i in range(npat):
                o_ref[i, rows, :] = o_scr[i, rows, :].astype(BF16)
                packed = jnp.where(lane_c % HEAD_DIM == i, ls[i], packed)
            lse_ref[rows, :] = packed
            if npat == 1:
                mix_ref[rows, :] = o_scr[0, rows, :].astype(BF16)
            else:
                ws = _softmax_weights(ls)
                acc = ws[0] * o_scr[0, rows, :]
                for i in range(1, npat):
                    acc = acc + ws[i] * o_scr[i, rows, :]
                mix_ref[rows, :] = acc.astype(BF16)
            return carry

        lax.fori_loop(0, S // rows_c, comb, 0, unroll=2)

    smem = pl.BlockSpec(memory_space=pltpu.SMEM)
    return pl.pallas_call(
        body, name=name, grid=(N_SLABS,),
        in_specs=[smem, smem, pl.BlockSpec((3, S, SLAB), lambda p: (0, 0, p))],
        out_specs=[pl.BlockSpec((S, SLAB), lambda p: (0, p)), pl.BlockSpec((npat, S, SLAB), lambda p: (0, 0, p)),
                   pl.BlockSpec((None, S, SLAB), lambda p: (p, 0, 0))],
        out_shape=[jax.ShapeDtypeStruct((S, D_MODEL), BF16), jax.ShapeDtypeStruct((npat, S, D_MODEL), BF16),
                   jax.ShapeDtypeStruct((N_SLABS, S, SLAB), F32)],
        scratch_shapes=[pltpu.VMEM((npat, S, SLAB), F32), pltpu.VMEM((npat, S, SLAB), F32)],
        compiler_params=_cparams(("arbitrary",)),
    )(slopes, sinks, qkv)


def _attn_bwd(qkv, dout, o, lse, slopes, sinks, patterns, name):
    S = qkv.shape[1]
    npat = len(patterns)
    has_sink = sinks is not None
    if not has_sink:
        sinks = jnp.zeros((N_HEADS,), F32)
    rows_c = 256

    def headsum(x, lo):
        same = (lax.broadcasted_iota(jnp.int32, (SLAB, SLAB), 0) < HEAD_DIM) == (lax.broadcasted_iota(jnp.int32, (SLAB, SLAB), 1) < HEAD_DIM)
        return jnp.dot(x, same.astype(F32), precision=lax.Precision.HIGH, preferred_element_type=F32)

    def body(slopes_ref, sinks_ref, x_ref, do_ref, o_ref, lsep_ref, dxo_ref, dsink_ref, dbar_ref, sacc_ref, lse_ref, dx_ref):
        p = pl.program_id(0)
        lo = lax.broadcasted_iota(jnp.int32, (BLOCK, SLAB), 1) < HEAD_DIM
        lo_c = lax.broadcasted_iota(jnp.int32, (rows_c, SLAB), 1) < HEAD_DIM
        top1 = lax.broadcasted_iota(jnp.int32, (2 * BLOCK, 1), 0) < BLOCK
        sk2 = jnp.where(top1, sinks_ref[2 * p], sinks_ref[2 * p + 1])

        def prep(ci, carry):
            rows = pl.ds(pl.multiple_of(ci * rows_c, rows_c), rows_c)
            dov = do_ref[rows, :]
            dx_ref[:, rows, :] = jnp.zeros((3, rows_c, SLAB), F32)
            packed = lsep_ref[rows, :]
            ls = [jnp.where(lo_c, packed[:, i:i + 1], packed[:, HEAD_DIM + i:HEAD_DIM + i + 1]) for i in range(npat)]
            for i in range(npat):
                lse_ref[i, rows, :] = ls[i]
            if npat == 1:
                dbar_ref[rows, :] = headsum(dov * o_ref[0, rows, :].astype(F32), lo_c)
            else:
                ws = _softmax_weights(ls)
                acc = ws[0] * headsum(dov * o_ref[0, rows, :].astype(F32), lo_c)
                for i in range(1, npat):
                    acc = acc + ws[i] * headsum(dov * o_ref[i, rows, :].astype(F32), lo_c)
                dbar_ref[rows, :] = acc
            return carry

        lax.fori_loop(0, S // rows_c, prep, 0, unroll=2)
        sacc_ref[...] = jnp.zeros((BLOCK, SLAB), F32)

        for pi, (d, maxd, scale) in enumerate(patterns):
            nb = S // d // BLOCK
            base, prev_keys = _band_consts(slopes_ref[2 * p], slopes_ref[2 * p + 1], maxd, scale)

            def blk(t, carry, pi=pi, d=d, nb=nb, base=base, prev_keys=prev_keys):
                r = t // nb
                n = t - r * nb
                start = r + (d * BLOCK) * n
                prev = jnp.where(n > 0, start - d * BLOCK, start)
                q2 = _stack_heads(_ld3(x_ref, 0, start, d), lo).astype(BF16)
                k2 = jnp.concatenate([_ld3(x_ref, 1, prev, d), _ld3(x_ref, 1, start, d)], axis=0).astype(BF16)
                v2 = jnp.concatenate([_ld3(x_ref, 2, prev, d), _ld3(x_ref, 2, start, d)], axis=0).astype(BF16)
                ls = [_ld3(lse_ref, i, start, d) for i in range(npat)]
                w = _softmax_weights(ls)[pi] if npat > 1 else 1.0
                do2 = _stack_heads(w * _ld(do_ref, start, d), lo).astype(BF16)
                dl = w * _ld(dbar_ref, start, d)
                lse2 = jnp.concatenate([ls[pi][:, :1], ls[pi][:, HEAD_DIM:HEAD_DIM + 1]], axis=0)
                dl2 = jnp.concatenate([dl[:, :1], dl[:, HEAD_DIM:HEAD_DIM + 1]], axis=0)
                s = _scores(q2, k2, base, prev_keys, n == 0)
                pr = jnp.exp(s - lse2)
                dp = lax.dot_general(do2, v2, (((1,), (1,)), ((), ())), preferred_element_type=F32)
                ds = (pr * (dp - dl2) * (HEAD_DIM ** -0.5)).astype(BF16)
                dq2 = jnp.dot(ds, k2, preferred_element_type=F32)
                dk2 = lax.dot_general(ds, q2, (((0,), (0,)), ((), ())), preferred_element_type=F32)
                dv2 = lax.dot_general(pr.astype(BF16), do2, (((0,), (0,)), ((), ())), preferred_element_type=F32)
                _acc3(dx_ref, 0, start, d, _unstack_heads(dq2, lo))
                _acc3(dx_ref, 1, prev, d, dk2[:BLOCK])
                _acc3(dx_ref, 1, start, d, dk2[BLOCK:])
                _acc3(dx_ref, 2, prev, d, dv2[:BLOCK])
                _acc3(dx_ref, 2, start, d, dv2[BLOCK:])
                if has_sink:
                    sacc_ref[...] += _unstack_heads(-jnp.exp(sk2 - lse2) * dl2, lo)
                return carry

            lax.fori_loop(0, d * nb, blk, 0, unroll=8)

        dsink_ref[...] = jnp.broadcast_to(jnp.sum(sacc_ref[...], axis=0, keepdims=True), (8, SLAB))

        def emit(ci, carry):
            rows = pl.ds(pl.multiple_of(ci * rows_c, rows_c), rows_c)
            dxo_ref[:, rows, :] = dx_ref[:, rows, :].astype(BF16)
            return carry

        lax.fori_loop(0, S // rows_c, emit, 0, unroll=2)

    smem = pl.BlockSpec(memory_space=pltpu.SMEM)
    return pl.pallas_call(
        body, name=name, grid=(N_SLABS,),
        in_specs=[smem, smem, pl.BlockSpec((3, S, SLAB), lambda p: (0, 0, p)), pl.BlockSpec((S, SLAB), lambda p: (0, p)),
                  pl.BlockSpec((npat, S, SLAB), lambda p: (0, 0, p)), pl.BlockSpec((None, S, SLAB), lambda p: (p, 0, 0))],
        out_specs=[pl.BlockSpec((3, S, SLAB), lambda p: (0, 0, p)), pl.BlockSpec((None, 8, SLAB), lambda p: (p, 0, 0))],
        out_shape=[jax.ShapeDtypeStruct((3, S, D_MODEL), BF16), jax.ShapeDtypeStruct((N_SLABS, 8, SLAB), F32)],
        scratch_shapes=[pltpu.VMEM((S, SLAB), F32), pltpu.VMEM((BLOCK, SLAB), F32), pltpu.VMEM((npat, S, SLAB), F32),
                        pltpu.VMEM((3, S, SLAB), F32)],
        compiler_params=_cparams(("arbitrary",)),
    )(slopes, sinks, qkv, dout, o, lse)


def _place():
    x, y, c = lax.axis_index("x"), lax.axis_index("y"), lax.axis_index("c")
    return x, y, c, 2 * x + y


def _other_chips(x, y):
    return [(1 - x, y), (x, 1 - y), (1 - x, 1 - y)]


HBM_SPEC = pl.BlockSpec(memory_space=pl.ANY)


def _slot(q):
    return 2 * (q % 2) + q // 2


BIG = ("ffn1_w_in", "ffn1_w_out", "ffn2_w_in", "ffn2_w_out", "a_w_qkv", "a_w_o", "kv_w", "b_w_q", "b_w_o")
QKV_SHARD = 3 * D_MODEL // N_CHIPS
ROW_SHARD = D_MODEL // N_CHIPS


LAYER0_ITEMS = (("ffn1_w_in", 0), ("ffn1_w_out", 0), ("a_w_qkv", None), ("a_w_o", None), ("ffn2_w_in", 0),
                ("ffn2_w_out", 0), ("kv_w", None))
LAYER1_ITEMS = (("ffn1_w_in", 1), ("ffn1_w_out", 1), ("b_w_q", None), ("b_w_o", None), ("ffn2_w_in", 1),
                ("ffn2_w_out", 1))
OUT_SHARD = D_FF // N_CHIPS


def _full_shape(name):
    if name.endswith("w_in"):
        return (D_MODEL, 2 * D_FF)
    if name.endswith("w_out"):
        return (D_FF, D_MODEL)
    if name == "a_w_qkv":
        return (D_MODEL, 3 * D_MODEL)
    if name == "kv_w":
        return (N_CHIPS, 2, ROW_SHARD // 2, 2 * N_KV_B * HEAD_DIM)
    return (N_CHIPS, 2, ROW_SHARD // 2, D_MODEL)


def _gather_src(item, ref, c):
    name, layer = item
    if name.endswith("w_in"):
        return ref.at[layer, pl.ds(c * (D_MODEL // 2), D_MODEL // 2)]
    if name.endswith("w_out"):
        return ref.at[layer, pl.ds(c * (OUT_SHARD // 2), OUT_SHARD // 2)]
    if name == "a_w_qkv":
        return ref.at[0, pl.ds(c * (D_MODEL // 2), D_MODEL // 2)]
    if name == "kv_w":
        return ref.at[pl.ds(c * (ROW_SHARD // 2), ROW_SHARD // 2)]
    return ref.at[0, pl.ds(c * (ROW_SHARD // 2), ROW_SHARD // 2)]


def _gather_dst(item, ref, q, c):
    name, _ = item
    if name.endswith("w_in"):
        return ref.at[pl.ds(c * (D_MODEL // 2), D_MODEL // 2), pl.ds(_slot(q) * HALF_FF, HALF_FF)]
    if name.endswith("w_out"):
        return ref.at[pl.ds(q * OUT_SHARD + c * (OUT_SHARD // 2), OUT_SHARD // 2)]
    if name == "a_w_qkv":
        return ref.at[pl.ds(c * (D_MODEL // 2), D_MODEL // 2), pl.ds(q * QKV_SHARD, QKV_SHARD)]
    return ref.at[q, c]


def _all_gather(items, shards, small):
    n = len(items)
    r = small.shape[0]
    per = 8

    def body(*refs):
        srcs, small_ref = refs[:n], refs[n]
        dsts, s_ref = refs[n + 1:2 * n + 1], refs[2 * n + 1]
        send_sems, recv_sems = refs[2 * n + 2:]
        x, y, c, myq = _place()
        sibling = (x, y, 1 - c)
        chips = _other_chips(x, y)

        def big(t, k, src, q, h, to):
            return pltpu.make_async_remote_copy(src_ref=src, dst_ref=_gather_dst(items[t], dsts[t], q, h),
                                                send_sem=send_sems.at[per * t + k], recv_sem=recv_sems.at[per * t + k],
                                                device_id=to, device_id_type=MESH)

        def tiny(k, q, to):
            return pltpu.make_async_remote_copy(src_ref=small_ref, dst_ref=s_ref.at[q], send_sem=send_sems.at[per * n + k],
                                                recv_sem=recv_sems.at[per * n + k], device_id=to, device_id_type=MESH)

        first = []
        for j, chip in enumerate(chips):
            if j < 2:
                first += [big(t, j, _gather_src(items[t], srcs[t], c), myq, c, (*chip, c)) for t in range(n)]
            first.append(tiny(j, myq, (*chip, c)))
        own = [big(t, 6 + h, _gather_src(items[t], srcs[t], h), myq, h, sibling) for t in range(n) for h in (0, 1)]
        own.append(tiny(3, myq, sibling))
        for cp in first + own:
            cp.start()
        relay_from = ((x + 1 - c) % 2, (y + c) % 2)
        relay_to = ((x + c) % 2, (y + 1 - c) % 2, c)
        q_relay = 2 * relay_from[0] + relay_from[1]
        passed = []
        for t in range(n):
            src = _gather_src(items[t], srcs[t], c)
            for j, (cx, cy) in enumerate(chips[:2]):
                q = 2 * cx + cy
                big(t, j, src, q, c, sibling).wait_recv()
                fwd = big(t, 3 + j, _gather_dst(items[t], dsts[t], q, c), q, c, sibling)
                fwd.start()
                passed.append(fwd)
            relay = big(t, 2, _gather_dst(items[t], dsts[t], q_relay, c), q_relay, c, relay_to)
            relay.start()
            passed.append(relay)
        q_diag = 2 * chips[2][0] + chips[2][1]
        for t in range(n):
            big(t, 2, _gather_src(items[t], srcs[t], c), q_diag, c, sibling).wait_recv()
            fwd = big(t, 5, _gather_dst(items[t], dsts[t], q_diag, c), q_diag, c, sibling)
            fwd.start()
            passed.append(fwd)
        for j, (cx, cy) in enumerate(chips):
            q = 2 * cx + cy
            for t in range(n):
                big(t, 3 + j, _gather_src(items[t], srcs[t], c), q, 1 - c, sibling).wait_recv()
            tiny(j, q, sibling).wait_recv()
        for cp in own:
            cp.wait_recv()
        for cp in first + passed + own:
            cp.wait_send()

    outs = pl.pallas_call(
        body, name="all_gather_layer0",
        in_specs=[HBM_SPEC] * (n + 1), out_specs=[HBM_SPEC] * (n + 1),
        out_shape=[jax.ShapeDtypeStruct(_full_shape(name), BF16) for name, _ in items]
        + [jax.ShapeDtypeStruct((N_CHIPS, r, 128), F32)],
        scratch_shapes=[pltpu.SemaphoreType.DMA((per * n + 4,)), pltpu.SemaphoreType.DMA((per * n + 4,))],
    )(*[shards[item] for item in items], small)
    return list(outs[:n]), outs[n]


SEM_SPEC = pl.BlockSpec(memory_space=pltpu.SEMAPHORE)
DATAFLOW = pltpu.SideEffectType.DATAFLOW_SIDE_EFFECTING
PER_ITEM = 8


def _split_start(name, copies, n_sems, sources, land_shapes, after):
    n, m = len(sources), len(land_shapes)

    def body(*refs):
        srcs, lands = refs[:n], refs[n:n + m]
        send_sems, recv_sems = refs[n + m + 1], refs[n + m + 2]
        token = refs[-1]
        for src, dst_there, _, s, peer in copies(srcs, lands):
            pltpu.make_async_remote_copy(src_ref=src, dst_ref=dst_there, send_sem=send_sems.at[s], recv_sem=recv_sems.at[s],
                                         device_id=peer, device_id_type=MESH).start()
        token[...] = jnp.zeros_like(token)

    src_arrays = [pltpu.with_memory_space_constraint(a, pltpu.HBM) for a in sources]
    land_arrays = [pltpu.with_memory_space_constraint(lax.empty(s.shape, s.dtype), pltpu.HBM) for s in land_shapes]
    hbm = pl.BlockSpec(memory_space=pltpu.HBM)
    outs = pl.pallas_call(
        body, name=name,
        in_specs=[hbm] * (n + m) + [HBM_SPEC],
        out_specs=[SEM_SPEC, SEM_SPEC] + [hbm] * (n + m) + [pl.BlockSpec(memory_space=pltpu.VMEM)],
        out_shape=[pltpu.SemaphoreType.DMA((n_sems,)), pltpu.SemaphoreType.DMA((n_sems,))]
        + [pltpu.HBM(a.shape, a.dtype) for a in src_arrays + land_arrays] + [jax.ShapeDtypeStruct((8, 128), F32)],
        input_output_aliases={i: 2 + i for i in range(n + m)},
        compiler_params=pltpu.CompilerParams(has_side_effects=DATAFLOW),
    )(*src_arrays, *land_arrays, after)
    return (outs[0], outs[1], list(outs[2:2 + n]), list(outs[2 + n:2 + n + m])), outs[-1]


def _split_wait(name, copies, state, after):
    send_sems, recv_sems, srcs_thru, lands_thru = state
    n, m = len(srcs_thru), len(lands_thru)
    after = list(after) if isinstance(after, (list, tuple)) else [after]

    def body(*refs):
        srcs, lands = refs[:n], refs[n:n + m]
        send_sems, recv_sems = refs[n + m], refs[n + m + 1]
        for src, _, dst_here, s, peer in copies(srcs, lands):
            cp = pltpu.make_async_remote_copy(src_ref=src, dst_ref=dst_here, send_sem=send_sems.at[s], recv_sem=recv_sems.at[s],
                                              device_id=peer, device_id_type=MESH)
            cp.wait_send()
            cp.wait_recv()

    hbm = pl.BlockSpec(memory_space=pltpu.HBM)
    outs = pl.pallas_call(
        body, name=name,
        in_specs=[hbm] * (n + m) + [SEM_SPEC, SEM_SPEC] + [HBM_SPEC] * len(after),
        out_specs=[hbm] * (n + m),
        out_shape=[pltpu.HBM(a.shape, a.dtype) for a in srcs_thru + lands_thru],
        input_output_aliases={i: i for i in range(n + m)},
        compiler_params=pltpu.CompilerParams(has_side_effects=DATAFLOW),
    )(*srcs_thru, *lands_thru, send_sems, recv_sems, *after)
    return list(outs[:n]), list(outs[n:])


def _gather_copies(items):
    def copies(srcs, lands):
        x, y, c, myq = _place()
        out = []
        for t, item in enumerate(items):
            for h in (0, 1):
                src = _gather_src(item, srcs[t], h)
                for j, (cx, cy) in enumerate(_other_chips(x, y)):
                    out.append((src, _gather_dst(item, lands[t], myq, h), _gather_dst(item, lands[t], 2 * cx + cy, h),
                                PER_ITEM * t + 2 * j + h, (cx, cy, c)))
                out.append((src, _gather_dst(item, lands[t], myq, h), _gather_dst(item, lands[t], myq, h),
                            PER_ITEM * t + 6 + h, (x, y, 1 - c)))
        return out
    return copies


def _gather_start(items, shards, after):
    lands = [jax.ShapeDtypeStruct(_full_shape(name), BF16) for name, _ in items]
    return _split_start("gather_layer1_start", _gather_copies(items), PER_ITEM * len(items),
                        [shards[item] for item in items], lands, after)


def _gather_wait(items, state, after):
    return _split_wait("gather_layer1_wait", _gather_copies(items), state, after)[1]


def _small_all_reduce(v, after=()):
    r = v.shape[0]

    def body(v_ref, *rest):
        o_ref, buf_ref, send_sems, recv_sems = rest[len(after):]
        x, y, c, _ = _place()
        me = 4 * x + 2 * y + c
        buf_ref[me] = v_ref[...]
        copies = []
        for k in range(1, 8):
            fx, fy, fc = (k >> 2) & 1, (k >> 1) & 1, k & 1
            to = (x ^ fx, y ^ fy, c ^ fc)
            cp = pltpu.make_async_remote_copy(src_ref=v_ref, dst_ref=buf_ref.at[me], send_sem=send_sems.at[k - 1],
                                              recv_sem=recv_sems.at[k - 1], device_id=to, device_id_type=MESH)
            cp.start()
            copies.append(cp)
        for k in range(1, 8):
            fx, fy, fc = (k >> 2) & 1, (k >> 1) & 1, k & 1
            src_dev = 4 * (x ^ fx) + 2 * (y ^ fy) + (c ^ fc)
            pltpu.make_async_remote_copy(src_ref=v_ref, dst_ref=buf_ref.at[src_dev], send_sem=send_sems.at[k - 1],
                                         recv_sem=recv_sems.at[k - 1], device_id=(x, y, c), device_id_type=MESH).wait_recv()
        for cp in copies:
            cp.wait_send()
        tot = buf_ref[0]
        for i in range(1, 8):
            tot = tot + buf_ref[i]
        o_ref[...] = tot

    vm = pl.BlockSpec(memory_space=pltpu.VMEM)
    return pl.pallas_call(
        body, name="small_all_reduce", in_specs=[vm] + [HBM_SPEC] * len(after), out_specs=vm,
        out_shape=jax.ShapeDtypeStruct((r, 128), F32),
        scratch_shapes=[pltpu.VMEM((8, r, 128), F32), pltpu.SemaphoreType.DMA((7,)), pltpu.SemaphoreType.DMA((7,))],
    )(v, *after)


def _grad_view(kind, g):
    if kind == "col":
        return g.reshape(2, g.shape[0] // 2, g.shape[1])
    return g.reshape(N_CHIPS, 2, g.shape[0] // (2 * N_CHIPS), g.shape[1])


def _half_of(kind, ref, h):
    return ref.at[h] if kind == "col" else ref.at[:, h]


def _half_shape(kind, view_shape):
    return view_shape[1:] if kind == "col" else (view_shape[0],) + view_shape[2:]


def _piece_of(kind, width, colblock, ref, q):
    if kind == "col":
        return ref.at[:, pl.ds(colblock(q) * width, width)]
    return ref.at[q]


def _piece_shape(kind, width, half_shape):
    return (half_shape[0], width) if kind == "col" else half_shape[1:]


def _pair_exchange(views, kinds, name):
    n = len(views)

    def body(*refs):
        ins, outs = refs[:n], refs[n:2 * n]
        send_sems, recv_sems = refs[2 * n:]
        x, y, c, _ = _place()
        cps = []
        for t in range(n):
            cp = pltpu.make_async_remote_copy(src_ref=_half_of(kinds[t], ins[t], 1 - c), dst_ref=outs[t],
                                              send_sem=send_sems.at[t], recv_sem=recv_sems.at[t],
                                              device_id=(x, y, 1 - c), device_id_type=MESH)
            cp.start()
            cps.append(cp)
        for cp in cps:
            cp.wait()

    return pl.pallas_call(
        body, name=name, in_specs=[HBM_SPEC] * n, out_specs=[HBM_SPEC] * n,
        out_shape=[jax.ShapeDtypeStruct(_half_shape(k, v.shape), v.dtype) for k, v in zip(kinds, views)],
        scratch_shapes=[pltpu.SemaphoreType.DMA((n,)), pltpu.SemaphoreType.DMA((n,))],
    )(*views)


def _pair_sum(kind, view, recv, c, name):
    hs = recv.shape
    N = hs[-1]
    rows = hs[-2]
    tr = _pick(rows, (512, 352, 128))
    tn = _pick(N, (1408, 1024, 512))

    def body(c_ref, p_ref, r_ref, s_ref):
        s_ref[...] = (p_ref[...] + r_ref[...]).astype(BF16)

    if kind == "col":
        grid = (rows // tr, N // tn)
        mine = pl.BlockSpec((None, tr, tn), lambda i, j, c_ref: (c_ref[0], i, j))
        blk = pl.BlockSpec((tr, tn), lambda i, j, c_ref: (i, j))
        sem = ("parallel", "parallel")
    else:
        grid = (N_CHIPS, rows // tr, N // tn)
        mine = pl.BlockSpec((None, None, tr, tn), lambda q, i, j, c_ref: (q, c_ref[0], i, j))
        blk = pl.BlockSpec((None, tr, tn), lambda q, i, j, c_ref: (q, i, j))
        sem = ("parallel", "parallel", "parallel")
    return pl.pallas_call(
        body, name=name,
        grid_spec=pltpu.PrefetchScalarGridSpec(num_scalar_prefetch=1, grid=grid, in_specs=[mine, blk], out_specs=blk),
        out_shape=jax.ShapeDtypeStruct(hs, BF16),
        compiler_params=_cparams(sem),
    )(c.reshape(1).astype(jnp.int32), view, recv)


def _chip_copies(kinds, widths, colblocks):
    def copies(srcs, lands):
        x, y, c, _ = _place()
        out = []
        for j, (cx, cy) in enumerate(_other_chips(x, y)):
            for t in range(len(kinds)):
                out.append((_piece_of(kinds[t], widths[t], colblocks[t], srcs[t], 2 * cx + cy), lands[t].at[j],
                            lands[t].at[j], 3 * t + j, (cx, cy, c)))
        return out
    return copies


def _chip_land_shapes(sums, kinds, widths):
    return [jax.ShapeDtypeStruct((3,) + _piece_shape(k, w, s.shape), BF16) for k, w, s in zip(kinds, widths, sums)]


def _chip_exchange(sums, kinds, widths, colblocks, name):
    n = len(sums)
    copies = _chip_copies(kinds, widths, colblocks)

    def body(*refs):
        send_sems, recv_sems = refs[2 * n:]
        cps = [pltpu.make_async_remote_copy(src_ref=src, dst_ref=dst, send_sem=send_sems.at[s], recv_sem=recv_sems.at[s],
                                            device_id=peer, device_id_type=MESH)
               for src, dst, _, s, peer in copies(refs[:n], refs[n:2 * n])]
        for cp in cps:
            cp.start()
        for cp in cps:
            cp.wait()

    return pl.pallas_call(
        body, name=name, in_specs=[HBM_SPEC] * n, out_specs=[HBM_SPEC] * n,
        out_shape=_chip_land_shapes(sums, kinds, widths),
        scratch_shapes=[pltpu.SemaphoreType.DMA((3 * n,)), pltpu.SemaphoreType.DMA((3 * n,))],
    )(*sums)


N_DIRECT = 7


def _direct_piece(kind, width, colblock, view_ref, q, h):
    if kind == "col":
        return view_ref.at[h, :, pl.ds(colblock(q) * width, width)]
    return view_ref.at[q, h]


def _direct_copies(kinds, widths, colblocks):
    def copies(srcs, lands):
        x, y, c, myq = _place()
        out = []
        for t in range(len(kinds)):
            def piece(q, h, t=t):
                return _direct_piece(kinds[t], widths[t], colblocks[t], srcs[t], q, h)
            for j, (cx, cy) in enumerate(_other_chips(x, y)):
                for h in (0, 1):
                    out.append((piece(2 * cx + cy, h), lands[t].at[2 * j + c], lands[t].at[2 * j + h],
                                10 * t + 3 * j + c + h, (cx, cy, h)))
            out.append((piece(myq, 1 - c), lands[t].at[6], lands[t].at[6], 10 * t + 9, (x, y, 1 - c)))
        return out
    return copies


def _chip_sum(kind, own_src, recv, block_idx, c, shard_shape, layer, into, name, direct=False):
    n_recv, rows, N = recv.shape
    tr = _pick(rows, (512, 352, 128))
    tn = _pick(N, (1408, 1024, 768, 512))
    ni, nj = rows // tr, N // tn

    def body(q_ref, s_ref, r_ref, *rest):
        o_ref = rest[-1]
        tot = s_ref[...].astype(F32)
        for k in range(n_recv):
            tot = tot + r_ref[k].astype(F32)
        o_ref[...] = tot

    if direct and kind == "col":
        own = pl.BlockSpec((None, tr, tn), lambda i, j, q_ref: (q_ref[1], i, q_ref[0] * nj + j))
    elif direct:
        own = pl.BlockSpec((None, None, tr, tn), lambda i, j, q_ref: (q_ref[0], q_ref[1], i, j))
    elif kind == "col":
        own = pl.BlockSpec((tr, tn), lambda i, j, q_ref: (i, q_ref[0] * nj + j))
    else:
        own = pl.BlockSpec((None, tr, tn), lambda i, j, q_ref: (q_ref[0], i, j))
    if len(shard_shape) == 3:
        lead = 0 if layer is None else layer
        out_spec = pl.BlockSpec((None, tr, tn), lambda i, j, q_ref: (lead, q_ref[1] * ni + i, j))
    else:
        out_spec = pl.BlockSpec((tr, tn), lambda i, j, q_ref: (q_ref[1] * ni + i, j))
    in_specs = [own, pl.BlockSpec((n_recv, tr, tn), lambda i, j, q_ref: (0, i, j))]
    s = own_src
    args = [jnp.stack([block_idx, c]).astype(jnp.int32), s, recv]
    aliases = {}
    if into is not None:
        in_specs.append(HBM_SPEC)
        args.append(into)
        aliases = {3: 0}
    return pl.pallas_call(
        body, name=name,
        grid_spec=pltpu.PrefetchScalarGridSpec(num_scalar_prefetch=1, grid=(ni, nj), in_specs=in_specs, out_specs=out_spec),
        out_shape=jax.ShapeDtypeStruct(shard_shape, F32), input_output_aliases=aliases,
        compiler_params=_cparams(("parallel", "parallel")),
    )(*args)


def _half_window(ref, h):
    rows = ref.shape[-2] // 2
    if ref.ndim == 3:
        return ref.at[:, pl.ds(h * rows, rows)]
    return ref.at[pl.ds(h * rows, rows)]


def _share_halves(grads, name):
    n = len(grads)

    def body(*refs):
        outs = refs[n:2 * n]
        send_sems, recv_sems = refs[2 * n:]
        x, y, c, _ = _place()
        cps = []
        for t in range(n):
            cp = pltpu.make_async_remote_copy(src_ref=_half_window(outs[t], c), dst_ref=_half_window(outs[t], c),
                                              send_sem=send_sems.at[t], recv_sem=recv_sems.at[t],
                                              device_id=(x, y, 1 - c), device_id_type=MESH)
            cp.start()
            cps.append(cp)
        for t in range(n):
            cps[t].wait_send()
            pltpu.make_async_remote_copy(src_ref=_half_window(outs[t], c), dst_ref=_half_window(outs[t], 1 - c),
                                         send_sem=send_sems.at[t], recv_sem=recv_sems.at[t],
                                         device_id=(x, y, 1 - c), device_id_type=MESH).wait_recv()

    return pl.pallas_call(
        body, name=name, in_specs=[HBM_SPEC] * n, out_specs=[HBM_SPEC] * n,
        out_shape=[jax.ShapeDtypeStruct(g.shape, F32) for g in grads],
        input_output_aliases={t: t for t in range(n)},
        scratch_shapes=[pltpu.SemaphoreType.DMA((n,)), pltpu.SemaphoreType.DMA((n,))],
    )(*grads)


def _adamw(w, g, m, v, name):
    R, W = w.shape
    tr = _pick(R, (512, 352, 256, 32))

    def body(w_ref, g_ref, m_ref, v_ref, d_ref, nm_ref, nv_ref, go_ref):
        gv = g_ref[...]
        go_ref[...] = gv
        nm = ADAM_B1 * m_ref[...] + (1.0 - ADAM_B1) * gv
        nv = ADAM_B2 * v_ref[...] + (1.0 - ADAM_B2) * (gv * gv)
        m_hat = nm / (1.0 - ADAM_B1 ** ADAM_STEP)
        v_hat = nv / (1.0 - ADAM_B2 ** ADAM_STEP)
        d_ref[...] = -ADAM_LR * (m_hat / (jnp.sqrt(v_hat) + ADAM_EPS) + ADAM_WD * w_ref[...])
        nm_ref[...] = nm
        nv_ref[...] = nv

    blk = pl.BlockSpec((tr, W), lambda i: (i, 0))
    shp = jax.ShapeDtypeStruct((R, W), F32)
    return pl.pallas_call(
        body, name=name, grid=(R // tr,), in_specs=[blk] * 4, out_specs=[blk] * 4, out_shape=[shp] * 4,
        compiler_params=_cparams(("parallel",)),
    )(w, g, m, v)


SMALL_ROWS = 32


def _pack_small(ln_g, ln_b, sinks):
    rows = jnp.concatenate([ln_g.reshape(-1, 128), ln_b.reshape(-1, 128),
                            jnp.pad(sinks.reshape(1, -1), ((0, 0), (0, 128 - sinks.size)))], axis=0)
    return jnp.pad(rows, ((0, SMALL_ROWS - rows.shape[0]), (0, 0)))


def _unpack_small(s, ln_shape, sink_shape):
    n = ln_shape[0] * ln_shape[1] * ln_shape[2] // 128
    return s[:n].reshape(ln_shape), s[n:2 * n].reshape(ln_shape), s[2 * n, :sink_shape[1]].reshape(sink_shape)


def _ffn_fwd(xin, w_in, w_out, gain, bias, tag):
    u, h = _ffn_in(xin, w_in, "ffn_in_" + tag)
    y, yb, z = _mm_ln(h, w_out, xin, gain, bias, 0.5, "ffn_out_ln_" + tag)
    return y, yb, dict(u=u, h=h, z=z, xin=xin)


def _ffn_bwd(dz, dzc, saved, w_in, w_out, xin_b, tag, dw_dtype=F32, ln=None, ready=None):
    du = _ffn_bwd_h(dzc, w_out, saved["u"], "ffn_bwd_h_" + tag)
    d_w_out = _mm_tn(saved["h"], dzc, "ffn_dwout_" + tag, out_dtype=dw_dtype)
    d_w_in = _mm_tn(xin_b, du, "ffn_dwin_" + tag, out_dtype=dw_dtype)
    after = None if ready is None else ready(d_w_in, d_w_out)
    dx = _mm_nt(du, w_in, "ffn_dx_" + tag, add=dz, add_scale=ALPHA, ln=ln, after=after)
    return dx, d_w_in, d_w_out


def kernel(x, ffn1_w_in, ffn1_w_out, ffn2_w_in, ffn2_w_out, ln_g, ln_b, a_w_qkv, a_w_o, kv_w, b_w_q, b_sinks, b_w_o, loss_target, m_ffn1_w_in, m_ffn1_w_out, m_ffn2_w_in, m_ffn2_w_out, m_ln_g, m_ln_b, m_a_w_qkv, m_a_w_o, m_kv_w, m_b_w_q, m_b_sinks, m_b_w_o, v_ffn1_w_in, v_ffn1_w_out, v_ffn2_w_in, v_ffn2_w_out, v_ln_g, v_ln_b, v_a_w_qkv, v_a_w_o, v_kv_w, v_b_w_q, v_b_sinks, v_b_w_o):
    ws = dict(ffn1_w_in=ffn1_w_in, ffn1_w_out=ffn1_w_out, ffn2_w_in=ffn2_w_in, ffn2_w_out=ffn2_w_out, a_w_qkv=a_w_qkv,
              a_w_o=a_w_o, kv_w=kv_w, b_w_q=b_w_q, b_w_o=b_w_o)
    ms = dict(ffn1_w_in=m_ffn1_w_in, ffn1_w_out=m_ffn1_w_out, ffn2_w_in=m_ffn2_w_in, ffn2_w_out=m_ffn2_w_out,
              a_w_qkv=m_a_w_qkv, a_w_o=m_a_w_o, kv_w=m_kv_w, b_w_q=m_b_w_q, b_w_o=m_b_w_o)
    vs = dict(ffn1_w_in=v_ffn1_w_in, ffn1_w_out=v_ffn1_w_out, ffn2_w_in=v_ffn2_w_in, ffn2_w_out=v_ffn2_w_out,
              a_w_qkv=v_a_w_qkv, a_w_o=v_a_w_o, kv_w=v_kv_w, b_w_q=v_b_w_q, b_w_o=v_b_w_o)
    _, _, c_idx, myq = _place()
    xs = x[0]
    target = loss_target[0]

    shards = {(n, l): ws[n].astype(BF16) for n, l in LAYER0_ITEMS + LAYER1_ITEMS}

    def as_weights(items, arrays):
        return {n: (a.reshape(D_MODEL, a.shape[-1]) if a.ndim == 4 else a) for (n, _), a in zip(items, arrays)}

    full0, small = _all_gather(LAYER0_ITEMS, shards, _pack_small(ln_g, ln_b, b_sinks))
    gather_state, token = _gather_start(LAYER1_ITEMS, shards, small)

    def layer1_weights(after):
        return as_weights(LAYER1_ITEMS, _gather_wait(LAYER1_ITEMS, gather_state, after))

    n_ln = ln_g.size // 128
    lg = jnp.concatenate([small[q, :n_ln].reshape(DEPTH, 3, 1, -1) for q in range(N_CHIPS)], axis=-1)
    lb = jnp.concatenate([small[q, n_ln:2 * n_ln].reshape(DEPTH, 3, 1, -1) for q in range(N_CHIPS)], axis=-1)
    lg = lg + token[0, 0]
    reducer = _GradReducer(c_idx, myq, {n: ws[n].shape for n in BIG})
    sq, grad_x, _, gg, gb, dsink_part = _local_step(xs, target, as_weights(LAYER0_ITEMS, full0), layer1_weights,
                                                    lg, lb, b_sinks.reshape(N_HEADS), reducer.begin)

    loss_row = jnp.pad(jnp.sum(sq).reshape(1, 1), ((0, 0), (0, 127)))
    dsinks = jnp.pad(dsink_part[:, 0, :].reshape(N_SLABS, 2, HEAD_DIM)[:, :, 0].reshape(1, N_HEADS), ((0, 0), (0, 128 - N_HEADS)))
    gg_full = jnp.stack([jnp.stack([jnp.sum(gg[i][j], axis=0) for j in range(3)]) for i in range(DEPTH)])
    gb_full = jnp.stack([jnp.stack([jnp.sum(gb[i][j], axis=0) for j in range(3)]) for i in range(DEPTH)])
    small_in = jnp.concatenate([loss_row, dsinks, gg_full.reshape(-1, 128), gb_full.reshape(-1, 128)], axis=0)
    small_in = jnp.pad(small_in, ((0, (-small_in.shape[0]) % 8), (0, 0)))
    def reduce_small(after):
        small_sum = _small_all_reduce(small_in, after)
        loss = small_sum[0, 0] * (0.5 / D_MODEL)
        grad_sinks = small_sum[1, :N_HEADS].reshape(b_sinks.shape)
        n_full = DEPTH * 3 * D_MODEL // 128
        cols = D_MODEL // N_CHIPS
        grad_ln_g = lax.dynamic_slice_in_dim(small_sum[2:2 + n_full].reshape(DEPTH, 3, D_MODEL), myq * cols, cols, axis=2)
        grad_ln_b = lax.dynamic_slice_in_dim(small_sum[2 + n_full:2 + 2 * n_full].reshape(DEPTH, 3, D_MODEL), myq * cols, cols, axis=2)
        return loss, grad_ln_g, grad_ln_b, grad_sinks

    return _update(reducer, grad_x, reduce_small, ws, ms, vs,
                   (ln_g, ln_b, b_sinks), (m_ln_g, m_ln_b, m_b_sinks), (v_ln_g, v_ln_b, v_b_sinks))


def _local_step(xs, target, W, layer1_weights, lg, lb, sinks, grads_ready=None):
    if grads_ready is None:
        grads_ready = lambda tag, grads, overlap: 0.0
    S = xs.shape[0]
    slopes = jnp.asarray(_alibi_slopes(N_HEADS))
    in1, out1, in2, out2 = [W["ffn1_w_in"]], [W["ffn1_w_out"]], [W["ffn2_w_in"]], [W["ffn2_w_out"]]

    y1, y1b, s1 = _ffn_fwd(xs, in1[0], out1[0], lg[0, 0], lb[0, 0], "a1")
    qkv_a = _mm_nn(y1b, W["a_w_qkv"], F32, "qkv_a", split=True)
    mix_a, o_a, lse_a = _attn_fwd(qkv_a, slopes, None, PATTERNS_A, "attn_a_fwd")
    y2, y2b, z2 = _mm_ln(mix_a, W["a_w_o"], y1, lg[0, 1], lb[0, 1], 1.0, "attn_a_out_ln")
    y3, y3b, s3 = _ffn_fwd(y2, in2[0], out2[0], lg[0, 2], lb[0, 2], "a2")
    kv_w_rep = jnp.broadcast_to(W["kv_w"].reshape(D_MODEL, 2, N_KV_B, 1, HEAD_DIM),
                                (D_MODEL, 2, N_KV_B, GROUP_B, HEAD_DIM)).reshape(D_MODEL, 2 * D_MODEL)
    kv_rep = _mm_nn(y3b, kv_w_rep, F32, "kv_proj", split=(1, 2))
    W = dict(W, **layer1_weights(kv_rep))
    in1, out1, in2, out2 = (in1 + [W["ffn1_w_in"]], out1 + [W["ffn1_w_out"]], in2 + [W["ffn2_w_in"]],
                            out2 + [W["ffn2_w_out"]])
    y4, y4b, s4 = _ffn_fwd(y3, in1[1], out1[1], lg[1, 0], lb[1, 0], "b1")
    qkv_b = _mm_nn(y4b, W["b_w_q"], F32, "q_b", split=(0, 1), into=kv_rep)
    mix_b, o_b, lse_b = _attn_fwd(qkv_b, slopes, sinks, PATTERNS_B, "attn_b_fwd")
    y5, y5b, z5 = _mm_ln(mix_b, W["b_w_o"], y4, lg[1, 1], lb[1, 1], 1.0, "attn_b_out_ln")
    y6, _, s6 = _ffn_fwd(y5, in2[1], out2[1], lg[1, 2], lb[1, 2], "b2")

    gr = {n: None for n in BIG}
    gg = [[None] * 3 for _ in range(DEPTH)]
    gb = [[None] * 3 for _ in range(DEPTH)]
    dz6, dz6c, gg[1][2], gb[1][2], sq = _loss_ln_bwd(y6, target, s6["z"], lg[1, 2], 0.5, "loss_ln_bwd")

    (dz5, dz5b, gg[1][1], gb[1][1]), d_in2_b, d_out2_b = _ffn_bwd(dz6, dz6c, s6, in2[1], out2[1], y5b, "b2", BF16,
                                                                  ln=(z5, lg[1, 1], 1.0))
    gr["b_w_o"] = _mm_tn(mix_b, dz5b, "d_b_w_o", out_dtype=BF16)
    dmix_b = _mm_nt(dz5b, W["b_w_o"], "d_mix_b")
    dqkv_b, dsink_part = _attn_bwd(qkv_b, dmix_b, o_b, lse_b, slopes, sinks, PATTERNS_B, "attn_b_bwd")
    dq_b = (dqkv_b, 0)
    gr["b_w_q"] = _mm_tn(y4b, dq_b, "d_b_w_q", out_dtype=BF16)
    dz4, dz4c, gg[1][0], gb[1][0] = _mm_nt(dq_b, W["b_w_q"], "d_y4", add=dz5, add_scale=ALPHA, ln=(s4["z"], lg[1, 0], 0.5))
    dy3, d_in1_b, d_out1_b = _ffn_bwd(dz4, dz4c, s4, in1[1], out1[1], y3b, "b1", BF16)
    gr["kv_w"] = _d_kv_w(y3b, dqkv_b, "d_kv_w")
    tok = grads_ready("l1", {("ffn2_w_in", 1): d_in2_b, ("ffn2_w_out", 1): d_out2_b, ("b_w_o", None): gr["b_w_o"],
                             ("b_w_q", None): gr["b_w_q"], ("ffn1_w_in", 1): d_in1_b, ("ffn1_w_out", 1): d_out1_b,
                             ("kv_w", None): gr["kv_w"]}, True)
    lg0 = lg[0] + tok
    dz3, dz3c, gg[0][2], gb[0][2] = _mm_nt(dqkv_b, kv_w_rep, "d_y3_kv", add=dy3, add_scale=1.0, split=(1, 2),
                                           ln=(s3["z"], lg0[2], 0.5))

    (dz2, dz2b, gg[0][1], gb[0][1]), d_in2_a, d_out2_a = _ffn_bwd(dz3, dz3c, s3, in2[0], out2[0], y2b, "a2", BF16,
                                                                  ln=(z2, lg0[1], 1.0))
    tok = grads_ready("a2", {("ffn2_w_in", 0): d_in2_a, ("ffn2_w_out", 0): d_out2_a}, True)
    lg0 = lg0 + tok
    gr["a_w_o"] = _mm_tn(mix_a, dz2b, "d_a_w_o", out_dtype=BF16)
    dmix_a = _mm_nt(dz2b, W["a_w_o"], "d_mix_a")
    dqkv_a, _ = _attn_bwd(qkv_a, dmix_a, o_a, lse_a, slopes, None, PATTERNS_A, "attn_a_bwd")
    gr["a_w_qkv"] = _mm_tn(y1b, dqkv_a, "d_a_w_qkv", split=True, out_dtype=BF16)
    tok = grads_ready("mix", {("a_w_o", None): gr["a_w_o"], ("a_w_qkv", None): gr["a_w_qkv"]}, True)
    lg0 = lg0 + tok
    dz1, dz1c, gg[0][0], gb[0][0] = _mm_nt(dqkv_a, W["a_w_qkv"], "d_y1", add=dz2, add_scale=ALPHA, split=True,
                                           ln=(s1["z"], lg0[0], 0.5))
    def ready_a1(d_in, d_out):
        tok = grads_ready("a1", {("ffn1_w_in", 0): d_in, ("ffn1_w_out", 0): d_out}, True)
        return jnp.reshape(jnp.asarray(tok, F32), (1, 1))

    grad_x, d_in1_a, d_out1_a = _ffn_bwd(dz1, dz1c, s1, in1[0], out1[0], xs, "a1", BF16, ready=ready_a1)
    gr["ffn1_w_in"] = [d_in1_a, d_in1_b]
    gr["ffn1_w_out"] = [d_out1_a, d_out1_b]
    gr["ffn2_w_in"] = [d_in2_a, d_in2_b]
    gr["ffn2_w_out"] = [d_out2_a, d_out2_b]
    return sq, grad_x, gr, gg, gb, dsink_part


def _grad_item(name, layer, g):
    if name.endswith("w_in"):
        return (g, "col", HALF_FF, _slot, name, layer)
    if name.endswith("w_out"):
        return (g, "row", D_MODEL, None, name, layer)
    if name == "a_w_qkv":
        return (g, "col", QKV_SHARD, lambda q: q, name, None)
    return (g, "row", g.shape[1], None, name, None)


class _GradReducer:
    def __init__(self, c_idx, myq, shard_shapes):
        self.c_idx, self.myq, self.shard_shapes = c_idx, myq, shard_shapes
        self.groups = []

    def begin(self, tag, grads, overlap):
        items = [_grad_item(n, l, g) for (n, l), g in grads.items()]
        kinds, widths, colblocks = [it[1] for it in items], [it[2] for it in items], [it[3] for it in items]
        views = [_grad_view(k, it[0]) for k, it in zip(kinds, items)]
        if overlap:
            lands = [jax.ShapeDtypeStruct((N_DIRECT,) + _piece_shape(k, w, _half_shape(k, v.shape)), BF16)
                     for k, w, v in zip(kinds, widths, views)]
            state, token = _split_start("grad_direct_start_" + tag, _direct_copies(kinds, widths, colblocks), 10 * len(items),
                                        views, lands, jnp.zeros((8, 128), F32))
            self.groups.append((tag, items, None, state, token))
            return token[0, 0]
        from_sibling = _pair_exchange(views, kinds, "grad_pair_exchange_" + tag)
        sums = [_pair_sum(k, v, r, self.c_idx, "pair_sum_%s_%d" % (tag, t))
                for t, (k, v, r) in enumerate(zip(kinds, views, from_sibling))]
        self.groups.append((tag, items, sums, None, None))
        return 0.0

    def _sum_group(self, tag, items, sums, received, direct):
        for t, (it, s, r) in enumerate(zip(items, sums, received)):
            _, k, _, cb, name, layer = it
            own = cb(self.myq) if k == "col" else self.myq
            self.half_done[name] = _chip_sum(k, s, r, own, self.c_idx, self.shard_shapes[name], layer,
                                             self.half_done.get(name), "chip_sum_%s_%d" % (tag, t), direct=direct)

    def finish_first(self, after):
        self.half_done, self.late, early = {}, [], []
        started = [after]
        for g, (tag, items, sums, state, token) in enumerate(self.groups):
            kinds, widths, colblocks = [it[1] for it in items], [it[2] for it in items], [it[3] for it in items]
            if state is None:
                copies = _chip_copies(kinds, widths, colblocks)
                state, token = _split_start("grad_chip_start_" + tag, copies, 3 * len(items), sums,
                                            _chip_land_shapes(sums, kinds, widths), sums[-1])
                self.late.append((tag, items, copies, state, False))
                started.append(token)
            elif g == len(self.groups) - 1:
                self.late.append((tag, items, _direct_copies(kinds, widths, colblocks), state, True))
                started.append(token)
            else:
                early.append((tag, items, _direct_copies(kinds, widths, colblocks), state))
        for tag, items, copies, state in early:
            views, received = _split_wait("grad_direct_wait_" + tag, copies, state, started)
            self._sum_group(tag, items, views, received, True)
        late_names = {it[4] for _, items, _, _, _ in self.late for it in items}
        names = [n for n in BIG if n not in late_names]
        return dict(zip(names, _share_halves([self.half_done[n] for n in names], "grad_share_halves_first")))

    def finish_rest(self, after):
        names = []
        for tag, items, copies, state, direct in self.late:
            sums, received = _split_wait("grad_late_wait_" + tag, copies, state, after)
            self._sum_group(tag, items, sums, received, direct)
            names += [it[4] for it in items if it[4] not in names]
        return dict(zip(names, _share_halves([self.half_done[n] for n in names], "grad_share_halves_rest")))


def _update(reducer, grad_x, reduce_small, ws, ms, vs, small_w, small_m, small_v):
    ln_g, ln_b, b_sinks = small_w
    m_ln_g, m_ln_b, m_b_sinks = small_m
    v_ln_g, v_ln_b, v_b_sinks = small_v

    grads, deltas, new_m, new_v = {}, {}, {}, {}

    def update(some):
        done = []
        for name in some:
            shp = ws[name].shape
            flat = lambda a: a.reshape(-1, shp[-1])
            d, nm, nv, g = _adamw(flat(ws[name]), flat(some[name]), flat(ms[name]), flat(vs[name]), "adamw_" + name)
            grads[name], deltas[name], new_m[name], new_v[name] = g.reshape(shp), d.reshape(shp), nm.reshape(shp), nv.reshape(shp)
            done.append(d)
        return done

    rest = reducer.finish_rest(update(reducer.finish_first(grad_x)))
    loss, grad_ln_g, grad_ln_b, grad_sinks = reduce_small(list(rest.values()))
    update(rest)
    delta_s, nm_s, nv_s, _ = _adamw(_pack_small(ln_g, ln_b, b_sinks), _pack_small(grad_ln_g, grad_ln_b, grad_sinks),
                                    _pack_small(m_ln_g, m_ln_b, m_b_sinks), _pack_small(v_ln_g, v_ln_b, v_b_sinks), "adamw_small")
    for d, blob in ((grads, None), (deltas, delta_s), (new_m, nm_s), (new_v, nv_s)):
        if blob is None:
            d["ln_g"], d["ln_b"], d["b_sinks"] = grad_ln_g, grad_ln_b, grad_sinks
        else:
            d["ln_g"], d["ln_b"], d["b_sinks"] = _unpack_small(blob, ln_g.shape, b_sinks.shape)

    order = ("ffn1_w_in", "ffn1_w_out", "ffn2_w_in", "ffn2_w_out", "ln_g", "ln_b", "a_w_qkv", "a_w_o", "kv_w", "b_w_q",
             "b_sinks", "b_w_o")
    outs = [loss, grad_x[None]]
    for d in (grads, deltas, new_m, new_v):
        outs += [d[n] for n in order]
    return tuple(outs)
```

```python
import numpy as np
import jax
import jax.numpy as jnp
from jax import lax
from jax.experimental import pallas as pl
from jax.experimental.pallas import tpu as pltpu

F32 = jnp.float32
BF16 = jnp.bfloat16

D_MODEL = 1024
D_FF = 2816
HALF_FF = D_FF // 2
HEAD_DIM = 64
N_HEADS = 16
N_KV_B = 4
GROUP_B = N_HEADS // N_KV_B
DEPTH = 2
ALPHA = (2.0 * DEPTH) ** 0.25
LN_EPS = 1e-5
BLOCK = 128
SLAB = 128
N_SLABS = D_MODEL // SLAB
PATTERNS_A = ((1, 128, 1.0), (4, 128, 4.0), (16, 128, 16.0))
PATTERNS_B = ((1, 127, 1.0),)
NEG = -1e30

ADAM_LR = 0.001
ADAM_B1 = 0.9
ADAM_B2 = 0.999
ADAM_EPS = 1e-08
ADAM_WD = 0.01
ADAM_STEP = 10

N_CHIPS = 4
VMEM_LIMIT = 56 * 1024 * 1024
WHOLE_WEIGHT_BYTES = 12 * 1024 * 1024
MESH = pl.DeviceIdType.MESH


def _alibi_slopes(n):
    return np.array([2.0 ** (-8.0 * (h + 1) / n) for h in range(n)], dtype=np.float32)


def _cparams(sem=None, vmem=VMEM_LIMIT):
    return pltpu.CompilerParams(dimension_semantics=sem, vmem_limit_bytes=vmem)


_DIMS = {"nn": ((1,), (0,)), "nt": ((1,), (1,)), "tn": ((0,), (0,))}


def _unlead(x):
    if isinstance(x, tuple):
        return x[0], x[1], x[0].shape[1:]
    return x, None, x.shape


def _bspec(block, imap, lead=None, **kw):
    if lead is None:
        return pl.BlockSpec(block, imap, **kw)
    return pl.BlockSpec((None,) + tuple(block), lambda *g: (lead,) + tuple(imap(*g)), **kw)


def _ln_bwd_math(zv, dyv, gain):
    rows = zv.shape[0]
    mu = jnp.mean(zv, axis=-1, keepdims=True)
    zc = zv - mu
    var = jnp.mean(zc * zc, axis=-1, keepdims=True)
    rstd = lax.rsqrt(var + LN_EPS)
    xhat = zc * rstd
    dyg = dyv * gain
    m1 = jnp.mean(dyg, axis=-1, keepdims=True)
    m2 = jnp.mean(dyg * xhat, axis=-1, keepdims=True)
    dz = rstd * (dyg - m1 - xhat * m2)
    pg = jnp.sum((dyv * xhat).reshape(rows // 8, 8, D_MODEL), axis=0)
    pb = jnp.sum(dyv.reshape(rows // 8, 8, D_MODEL), axis=0)
    return dz, pg, pb


def _matmul(a, b, mode, out_dtype, tm, tn, tk, name, add=None, add_scale=1.0, split=False, into=None, ln=None,
            after=None):
    out_spec = pl.BlockSpec((tm, tn), lambda i, j, k: (i, j))
    base, count = (0, 3) if split is True else (split or (0, 0))
    if mode == "nn":
        a, al, (M, K) = _unlead(a)
        b, bl, (K2, N) = _unlead(b)
        a_spec = _bspec((tm, tk), lambda i, j, k: (i, k), al)
        b_spec = _bspec((tk, tn), lambda i, j, k: (k, j), bl)
        out_struct = jax.ShapeDtypeStruct((M, N), out_dtype)
        if split:
            assert tn == D_MODEL and N == count * tn
            out_spec = pl.BlockSpec((None, tm, tn), lambda i, j, k: (j + base, i, 0))
            out_struct = jax.ShapeDtypeStruct((3, M, tn), out_dtype)
    elif mode == "nt":
        b, bl, (N, K2) = _unlead(b)
        if split:
            M, K = a.shape[1], count * a.shape[2]
            if tk == K:
                a_spec = [pl.BlockSpec((None, tm, D_MODEL), lambda i, j, k, s=s: (s + base, i, 0)) for s in range(count)]
            else:
                assert tk == D_MODEL
                a_spec = pl.BlockSpec((None, tm, tk), lambda i, j, k: (k + base, i, 0))
        else:
            a, al, (M, K) = _unlead(a)
            a_spec = _bspec((tm, tk), lambda i, j, k: (i, k), al)
        whole_b = {"pipeline_mode": pl.Buffered(1)} if (tn, tk) == (N, K2) else {}
        b_spec = _bspec((tn, tk), lambda i, j, k: (j, k), bl, **whole_b)
        out_struct = jax.ShapeDtypeStruct((M, N), out_dtype)
    else:
        a, al, (K, M) = _unlead(a)
        if split:
            assert tn == D_MODEL
            K2, N = b.shape[1], count * b.shape[2]
            b_spec = pl.BlockSpec((None, tk, tn), lambda i, j, k: (j + base, k, 0))
        else:
            b, bl, (K2, N) = _unlead(b)
            b_spec = _bspec((tk, tn), lambda i, j, k: (k, j), bl)
        a_spec = _bspec((tk, tm), lambda i, j, k: (k, i), al)
        out_struct = jax.ShapeDtypeStruct((M, N), out_dtype)
    assert K == K2 and M % tm == 0 and N % tn == 0 and K % tk == 0, (a.shape, b.shape, mode, tm, tn, tk)
    nk = K // tk
    dims = (_DIMS[mode], ((), ()))
    has_add = add is not None

    narrow = out_dtype != F32
    assert not (narrow and has_add)
    if ln is not None:
        assert has_add and mode == "nt" and tn == N == D_MODEL

    a_specs = a_spec if isinstance(a_spec, list) else [a_spec]
    n_a = len(a_specs)

    def body(*refs):
        if after is not None:
            refs = refs[1:]
        a_refs, refs = refs[:n_a], refs[n_a - 1:]
        if into is not None:
            refs = refs[:2] + refs[3:]
        if ln is not None:
            a_ref, b_ref, add_ref, z_ref, g_ref, o_ref, dzc_ref, gg_ref, gb_ref = refs
            acc_ref = o_ref
        elif has_add:
            a_ref, b_ref, add_ref, o_ref = refs
            acc_ref = o_ref
        elif narrow:
            a_ref, b_ref, o_ref, acc_ref = refs
        else:
            a_ref, b_ref, o_ref = refs
            acc_ref = o_ref
        k = pl.program_id(2)
        if n_a == 1:
            part = lax.dot_general(a_ref[...].astype(BF16), b_ref[...].astype(BF16), dims, preferred_element_type=F32)
        else:
            part = sum(lax.dot_general(r[...], b_ref[:, s * D_MODEL:(s + 1) * D_MODEL], dims, preferred_element_type=F32)
                       for s, r in enumerate(a_refs))
        if has_add:
            @pl.when(k == 0)
            def _():
                acc_ref[...] = part + add_scale * add_ref[...]
        else:
            @pl.when(k == 0)
            def _():
                acc_ref[...] = part

        @pl.when(k > 0)
        def _():
            acc_ref[...] += part

        if narrow:
            @pl.when(k == nk - 1)
            def _():
                o_ref[...] = acc_ref[...].astype(out_dtype)

        if ln is not None:
            @pl.when(k == nk - 1)
            def _():
                dz, pg, pb = _ln_bwd_math(z_ref[...], o_ref[...], g_ref[...])
                o_ref[...] = dz
                dzc_ref[...] = (ln[2] * dz).astype(BF16)
                first = pl.program_id(0) == 0

                @pl.when(first)
                def _():
                    gg_ref[...] = pg
                    gb_ref[...] = pb

                @pl.when(jnp.logical_not(first))
                def _():
                    gg_ref[...] += pg
                    gb_ref[...] += pb

    in_specs = [*a_specs, b_spec]
    args = [a] * n_a + [b]
    aliases = {}
    if into is not None:
        assert mode == "nn" and split and not has_add and n_a == 1
        in_specs.append(pl.BlockSpec(memory_space=pl.ANY))
        args.append(into)
        aliases = {2: 0}
    if has_add:
        in_specs.append(pl.BlockSpec((tm, tn), lambda i, j, k: (i, j)))
        args.append(add)
    sem = ("parallel", "parallel", "arbitrary")
    if ln is not None:
        part8 = pl.BlockSpec((8, N), lambda i, j, k: (0, 0))
        in_specs += [pl.BlockSpec((tm, tn), lambda i, j, k: (i, j)), pl.BlockSpec((1, N), lambda i, j, k: (0, 0))]
        args += [ln[0], ln[1]]
        out_spec = [out_spec, pl.BlockSpec((tm, tn), lambda i, j, k: (i, j)), part8, part8]
        out_struct = [out_struct, jax.ShapeDtypeStruct((M, N), BF16), jax.ShapeDtypeStruct((8, N), F32),
                      jax.ShapeDtypeStruct((8, N), F32)]
        sem = ("arbitrary", "arbitrary", "arbitrary")
    if after is not None:
        assert into is None
        in_specs.insert(0, pl.BlockSpec(memory_space=pl.ANY))
        args.insert(0, after)
    return pl.pallas_call(
        body, name=name, grid=(M // tm, N // tn, nk),
        in_specs=in_specs, out_specs=out_spec, out_shape=out_struct, input_output_aliases=aliases,
        scratch_shapes=[pltpu.VMEM((tm, tn), F32)] if narrow else [],
        compiler_params=_cparams(sem),
    )(*args)


def _pick(n, cands):
    for c in cands:
        if n % c == 0:
            return c
    raise ValueError((n, cands))


def _mm_nn(a, b, out_dtype, name, split=False, into=None):
    M, K = _unlead(a)[2]
    N = _unlead(b)[2][1]
    return _matmul(a, b, "nn", out_dtype, _pick(M, (1024, 512, 256)), _pick(N, (1024, 512)), _pick(K, (1024, 512)), name,
                   split=split, into=into)


def _mm_nt(a, b, name, add=None, add_scale=1.0, split=False, ln=None, after=None):
    M = a.shape[1] if split else _unlead(a)[2][0]
    N, K = _unlead(b)[2]
    tn = _pick(N, (1024, 512))
    if tn == N and N * K * 2 <= WHOLE_WEIGHT_BYTES:
        tm, tk = _pick(M, (512, 256)), K
    else:
        tms = (512, 256) if ln is not None else (1024, 512, 256)
        tm, tk = _pick(M, tms), _pick(D_MODEL if split else K, (2816, 1024, 512))
    return _matmul(a, b, "nt", F32, tm, tn, tk, name, add=add, add_scale=add_scale, split=split, ln=ln, after=after)


def _mm_tn(a, b, name, split=False, out_dtype=F32):
    K, M = _unlead(a)[2]
    N = D_MODEL if split else _unlead(b)[2][1]
    return _matmul(a, b, "tn", out_dtype, _pick(M, (1024, 1408, 512)), _pick(N, (1408, 1024, 512)),
                   _pick(K, (2048, 1024, 512, 256)), name, split=split)


def _d_kv_w(y, dqkv, name):
    S = y.shape[0]
    tk = _pick(S, (1024, 512))
    nk = S // tk
    width = N_KV_B * HEAD_DIM
    r, c = np.arange(D_MODEL)[:, None], np.arange(width)[None, :]
    fold = jnp.asarray((r // (GROUP_B * HEAD_DIM) == c // HEAD_DIM) & (r % HEAD_DIM == c % HEAD_DIM), BF16)

    def body(y_ref, dk_ref, dv_ref, f_ref, o_ref, acc_ref):
        k = pl.program_id(0)
        summed = jnp.concatenate([jnp.dot(ref[...], f_ref[...], preferred_element_type=F32).astype(BF16)
                                  for ref in (dk_ref, dv_ref)], axis=1)
        part = lax.dot_general(summed, y_ref[...], (_DIMS["tn"], ((), ())), preferred_element_type=F32)

        @pl.when(k == 0)
        def _():
            acc_ref[...] = part

        @pl.when(k > 0)
        def _():
            acc_ref[...] += part

        @pl.when(k == nk - 1)
        def _():
            o_ref[...] = acc_ref[...].T.astype(BF16)

    return pl.pallas_call(
        body, name=name, grid=(nk,),
        in_specs=[pl.BlockSpec((tk, D_MODEL), lambda k: (k, 0)),
                  pl.BlockSpec((None, tk, D_MODEL), lambda k: (1, k, 0)),
                  pl.BlockSpec((None, tk, D_MODEL), lambda k: (2, k, 0)),
                  pl.BlockSpec((D_MODEL, width), lambda k: (0, 0))],
        out_specs=pl.BlockSpec((D_MODEL, 2 * width), lambda k: (0, 0)),
        out_shape=jax.ShapeDtypeStruct((D_MODEL, 2 * width), BF16),
        scratch_shapes=[pltpu.VMEM((2 * width, D_MODEL), F32)],
        compiler_params=_cparams(("arbitrary",)),
    )(y, dqkv, dqkv, fold)


def _ffn_in(x, w, name):
    S = x.shape[0]
    tm = _pick(S, (512, 256))
    w, wl, _ = _unlead(w)

    def body(x_ref, w_ref, t_ref, h_ref):
        acc = jnp.dot(x_ref[...].astype(BF16), w_ref[...], preferred_element_type=F32)
        g = acc[:, :HALF_FF]
        up = acc[:, HALF_FF:]
        sg = jax.nn.sigmoid(g)
        silu = g * sg
        t_ref[:, :HALF_FF] = (up * (sg * (1.0 + g * (1.0 - sg)))).astype(BF16)
        t_ref[:, HALF_FF:] = silu.astype(BF16)
        h_ref[...] = (silu * up).astype(BF16)

    return pl.pallas_call(
        body, name=name, grid=(2, S // tm),
        in_specs=[pl.BlockSpec((tm, D_MODEL), lambda j, i: (i, 0)),
                  _bspec((D_MODEL, D_FF), lambda j, i: (0, j), wl)],
        out_specs=[pl.BlockSpec((tm, D_FF), lambda j, i: (i, j)),
                   pl.BlockSpec((tm, HALF_FF), lambda j, i: (i, j))],
        out_shape=[jax.ShapeDtypeStruct((S, 2 * D_FF), BF16), jax.ShapeDtypeStruct((S, D_FF), BF16)],
        compiler_params=_cparams(("parallel", "parallel")),
    )(x, w)


def _ffn_bwd_h(dzc, w_out, u, name):
    S = dzc.shape[0]
    tm = _pick(S, (512, 256))
    w_out, wl, _ = _unlead(w_out)

    def body(dz_ref, w_ref, t_ref, du_ref):
        dh = lax.dot_general(dz_ref[...], w_ref[...], (((1,), (1,)), ((), ())), preferred_element_type=F32)
        du_ref[:, :HALF_FF] = (dh * t_ref[:, :HALF_FF].astype(F32)).astype(BF16)
        du_ref[:, HALF_FF:] = (dh * t_ref[:, HALF_FF:].astype(F32)).astype(BF16)

    return pl.pallas_call(
        body, name=name, grid=(2, S // tm),
        in_specs=[pl.BlockSpec((tm, D_MODEL), lambda j, i: (i, 0)),
                  _bspec((HALF_FF, D_MODEL), lambda j, i: (j, 0), wl),
                  pl.BlockSpec((tm, D_FF), lambda j, i: (i, j))],
        out_specs=pl.BlockSpec((tm, D_FF), lambda j, i: (i, j)),
        out_shape=jax.ShapeDtypeStruct((S, 2 * D_FF), BF16),
        compiler_params=_cparams(("parallel", "parallel")),
    )(dzc, w_out, u)


def _mm_ln(a, w, resid, gain, bias, c, name):
    S, K = a.shape
    tm = _pick(S, (512, 256))
    w, wl, _ = _unlead(w)

    def body(a_ref, w_ref, r_ref, g_ref, b_ref, y_ref, yb_ref, z_ref):
        z = ALPHA * r_ref[...] + c * jnp.dot(a_ref[...], w_ref[...], preferred_element_type=F32)
        mu = jnp.mean(z, axis=-1, keepdims=True)
        zc = z - mu
        var = jnp.mean(zc * zc, axis=-1, keepdims=True)
        y = zc * lax.rsqrt(var + LN_EPS) * g_ref[...] + b_ref[...]
        z_ref[...] = z
        y_ref[...] = y
        yb_ref[...] = y.astype(BF16)

    row = pl.BlockSpec((tm, D_MODEL), lambda i: (i, 0))
    vec = pl.BlockSpec((1, D_MODEL), lambda i: (0, 0))
    return pl.pallas_call(
        body, name=name, grid=(S // tm,),
        in_specs=[pl.BlockSpec((tm, K), lambda i: (i, 0)), _bspec((K, D_MODEL), lambda i: (0, 0), wl), row, vec, vec],
        out_specs=[row, row, row],
        out_shape=[jax.ShapeDtypeStruct((S, D_MODEL), F32), jax.ShapeDtypeStruct((S, D_MODEL), BF16),
                   jax.ShapeDtypeStruct((S, D_MODEL), F32)],
        compiler_params=_cparams(("parallel",)),
    )(a, w, resid, gain, bias)


def _loss_ln_bwd(y, t, z, gain, c, name):
    S = y.shape[0]
    tm = _pick(S, (512, 256))

    def body(y_ref, t_ref, z_ref, g_ref, dz_ref, dzc_ref, gg_ref, gb_ref, sq_ref):
        i = pl.program_id(0)
        e = y_ref[...] - t_ref[...]
        dz, pg, pb = _ln_bwd_math(z_ref[...], e * (1.0 / D_MODEL), g_ref[...])
        dz_ref[...] = dz
        dzc_ref[...] = (c * dz).astype(BF16)
        ps = jnp.sum((e * e).reshape(tm // 8, 8, D_MODEL), axis=0)

        @pl.when(i == 0)
        def _():
            gg_ref[...] = pg
            gb_ref[...] = pb
            sq_ref[...] = ps

        @pl.when(i > 0)
        def _():
            gg_ref[...] += pg
            gb_ref[...] += pb
            sq_ref[...] += ps

    row = pl.BlockSpec((tm, D_MODEL), lambda i: (i, 0))
    part = pl.BlockSpec((8, D_MODEL), lambda i: (0, 0))
    part_shape = jax.ShapeDtypeStruct((8, D_MODEL), F32)
    return pl.pallas_call(
        body, name=name, grid=(S // tm,),
        in_specs=[row, row, row, pl.BlockSpec((1, D_MODEL), lambda i: (0, 0))],
        out_specs=[row, row, part, part, part],
        out_shape=[jax.ShapeDtypeStruct((S, D_MODEL), F32), jax.ShapeDtypeStruct((S, D_MODEL), BF16),
                   part_shape, part_shape, part_shape],
        compiler_params=_cparams(("arbitrary",)),
    )(y, t, z, gain)


def _rows(start, d):
    if d == 1:
        return pl.ds(pl.multiple_of(start, BLOCK), BLOCK)
    return pl.ds(start, BLOCK, stride=d)


def _ld(ref, start, d):
    return ref[_rows(start, d), :]


def _ld3(ref, lead, start, d):
    return ref[lead, _rows(start, d), :]


def _st3(ref, lead, start, d, val):
    ref[lead, _rows(start, d), :] = val


def _acc3(ref, lead, start, d, val):
    ref[lead, _rows(start, d), :] = ref[lead, _rows(start, d), :] + val


def _band_consts(slope0, slope1, maxd, scale):
    row = lax.broadcasted_iota(jnp.int32, (2 * BLOCK, 2 * BLOCK), 0)
    kj = lax.broadcasted_iota(jnp.int32, (2 * BLOCK, 2 * BLOCK), 1)
    top = row < BLOCK
    dist = BLOCK + jnp.where(top, row, row - BLOCK) - kj
    slope = jnp.where(top, slope0, slope1)
    base = jnp.where((dist >= 0) & (dist <= maxd), -(slope * (dist.astype(F32) * scale)), NEG)
    return base, kj < BLOCK


def _stack_heads(x, lo):
    return jnp.concatenate([jnp.where(lo, x, 0.0), jnp.where(lo, 0.0, x)], axis=0)


def _unstack_heads(x2, lo):
    return jnp.where(lo, x2[:BLOCK], x2[BLOCK:])


def _scores(q2, k2, base, prev_keys, first):
    s = lax.dot_general(q2, k2, (((1,), (1,)), ((), ())), preferred_element_type=F32) * (HEAD_DIM ** -0.5) + base
    return jnp.where(jnp.logical_and(prev_keys, first), NEG, s)


def _softmax_weights(ls):
    mx = ls[0]
    for l in ls[1:]:
        mx = jnp.maximum(mx, l)
    es = [jnp.exp(l - mx) for l in ls]
    tot = es[0]
    for e in es[1:]:
        tot = tot + e
    inv = 1.0 / tot
    return [e * inv for e in es]


def _attn_fwd(qkv, slopes, sinks, patterns, name):
    S = qkv.shape[1]
    npat = len(patterns)
    has_sink = sinks is not None
    if not has_sink:
        sinks = jnp.zeros((N_HEADS,), F32)
    rows_c = 256

    def body(slopes_ref, sinks_ref, x_ref, mix_ref, o_ref, lse_ref, o_scr, lse_scr):
        p = pl.program_id(0)
        lo = lax.broadcasted_iota(jnp.int32, (BLOCK, SLAB), 1) < HEAD_DIM
        top1 = lax.broadcasted_iota(jnp.int32, (2 * BLOCK, 1), 0) < BLOCK
        sk2 = jnp.where(top1, sinks_ref[2 * p], sinks_ref[2 * p + 1])
        for pi, (d, maxd, scale) in enumerate(patterns):
            nb = S // d // BLOCK
            base, prev_keys = _band_consts(slopes_ref[2 * p], slopes_ref[2 * p + 1], maxd, scale)

            def blk(t, carry, pi=pi, d=d, nb=nb, base=base, prev_keys=prev_keys):
                r = t // nb
                n = t - r * nb
                start = r + (d * BLOCK) * n
                prev = jnp.where(n > 0, start - d * BLOCK, start)
                q2 = _stack_heads(_ld3(x_ref, 0, start, d), lo).astype(BF16)
                k2 = jnp.concatenate([_ld3(x_ref, 1, prev, d), _ld3(x_ref, 1, start, d)], axis=0).astype(BF16)
                v2 = jnp.concatenate([_ld3(x_ref, 2, prev, d), _ld3(x_ref, 2, start, d)], axis=0).astype(BF16)
                s = _scores(q2, k2, base, prev_keys, n == 0)
                m = jnp.max(s, axis=-1, keepdims=True)
                if has_sink:
                    m = jnp.maximum(m, sk2)
                e = jnp.exp(s - m)
                den = jnp.sum(e, axis=-1, keepdims=True)
                if has_sink:
                    den = den + jnp.exp(sk2 - m)
                o2 = jnp.dot((e / den).astype(BF16), v2, preferred_element_type=F32)
                _st3(o_scr, pi, start, d, _unstack_heads(o2, lo))
                _st3(lse_scr, pi, start, d, _unstack_heads(m + jnp.log(den), lo))
                return carry

            lax.fori_loop(0, d * nb, blk, 0, unroll=8)

        lane_c = lax.broadcasted_iota(jnp.int32, (rows_c, SLAB), 1)

        def comb(ci, carry):
            rows = pl.ds(pl.multiple_of(ci * rows_c, rows_c), rows_c)
            ls = [lse_scr[i, rows, :] for i in range(npat)]
            packed = jnp.zeros((rows_c, SLAB), F32)
            for i in range(npat):
                o_ref[i, rows, :] = o_scr[i, rows, :].astype(BF16)
                packed = jnp.where(lane_c % HEAD_DIM == i, ls[i], packed)
            lse_ref[rows, :] = packed
            if npat == 1:
                mix_ref[rows, :] = o_scr[0, rows, :].astype(BF16)
            else:
                ws = _softmax_weights(ls)
                acc = ws[0] * o_scr[0, rows, :]
                for i in range(1, npat):
                    acc = acc + ws[i] * o_scr[i, rows, :]
                mix_ref[rows, :] = acc.astype(BF16)
            return carry

        lax.fori_loop(0, S // rows_c, comb, 0, unroll=2)

    smem = pl.BlockSpec(memory_space=pltpu.SMEM)
    return pl.pallas_call(
        body, name=name, grid=(N_SLABS,),
        in_specs=[smem, smem, pl.BlockSpec((3, S, SLAB), lambda p: (0, 0, p))],
        out_specs=[pl.BlockSpec((S, SLAB), lambda p: (0, p)), pl.BlockSpec((npat, S, SLAB), lambda p: (0, 0, p)),
                   pl.BlockSpec((None, S, SLAB), lambda p: (p, 0, 0))],
        out_shape=[jax.ShapeDtypeStruct((S, D_MODEL), BF16), jax.ShapeDtypeStruct((npat, S, D_MODEL), BF16),
                   jax.ShapeDtypeStruct((N_SLABS, S, SLAB), F32)],
        scratch_shapes=[pltpu.VMEM((npat, S, SLAB), F32), pltpu.VMEM((npat, S, SLAB), F32)],
        compiler_params=_cparams(("arbitrary",)),
    )(slopes, sinks, qkv)


def _attn_bwd(qkv, dout, o, lse, slopes, sinks, patterns, name):
    S = qkv.shape[1]
    npat = len(patterns)
    has_sink = sinks is not None
    if not has_sink:
        sinks = jnp.zeros((N_HEADS,), F32)
    rows_c = 256

    def headsum(x, lo):
        same = (lax.broadcasted_iota(jnp.int32, (SLAB, SLAB), 0) < HEAD_DIM) == (lax.broadcasted_iota(jnp.int32, (SLAB, SLAB), 1) < HEAD_DIM)
        return jnp.dot(x, same.astype(F32), precision=lax.Precision.HIGH, preferred_element_type=F32)

    def body(slopes_ref, sinks_ref, x_ref, do_ref, o_ref, lsep_ref, dxo_ref, dsink_ref, dbar_ref, sacc_ref, lse_ref, dx_ref):
        p = pl.program_id(0)
        lo = lax.broadcasted_iota(jnp.int32, (BLOCK, SLAB), 1) < HEAD_DIM
        lo_c = lax.broadcasted_iota(jnp.int32, (rows_c, SLAB), 1) < HEAD_DIM
        top1 = lax.broadcasted_iota(jnp.int32, (2 * BLOCK, 1), 0) < BLOCK
        sk2 = jnp.where(top1, sinks_ref[2 * p], sinks_ref[2 * p + 1])

        def prep(ci, carry):
            rows = pl.ds(pl.multiple_of(ci * rows_c, rows_c), rows_c)
            dov = do_ref[rows, :]
            dx_ref[:, rows, :] = jnp.zeros((3, rows_c, SLAB), F32)
            packed = lsep_ref[rows, :]
            ls = [jnp.where(lo_c, packed[:, i:i + 1], packed[:, HEAD_DIM + i:HEAD_DIM + i + 1]) for i in range(npat)]
            for i in range(npat):
                lse_ref[i, rows, :] = ls[i]
            if npat == 1:
                dbar_ref[rows, :] = headsum(dov * o_ref[0, rows, :].astype(F32), lo_c)
            else:
                ws = _softmax_weights(ls)
                acc = ws[0] * headsum(dov * o_ref[0, rows, :].astype(F32), lo_c)
                for i in range(1, npat):
                    acc = acc + ws[i] * headsum(dov * o_ref[i, rows, :].astype(F32), lo_c)
                dbar_ref[rows, :] = acc
            return carry

        lax.fori_loop(0, S // rows_c, prep, 0, unroll=2)
        sacc_ref[...] = jnp.zeros((BLOCK, SLAB), F32)

        for pi, (d, maxd, scale) in enumerate(patterns):
            nb = S // d // BLOCK
            base, prev_keys = _band_consts(slopes_ref[2 * p], slopes_ref[2 * p + 1], maxd, scale)

            def blk(t, carry, pi=pi, d=d, nb=nb, base=base, prev_keys=prev_keys):
                r = t // nb
                n = t - r * nb
                start = r + (d * BLOCK) * n
                prev = jnp.where(n > 0, start - d * BLOCK, start)
                q2 = _stack_heads(_ld3(x_ref, 0, start, d), lo).astype(BF16)
                k2 = jnp.concatenate([_ld3(x_ref, 1, prev, d), _ld3(x_ref, 1, start, d)], axis=0).astype(BF16)
                v2 = jnp.concatenate([_ld3(x_ref, 2, prev, d), _ld3(x_ref, 2, start, d)], axis=0).astype(BF16)
                ls = [_ld3(lse_ref, i, start, d) for i in range(npat)]
                w = _softmax_weights(ls)[pi] if npat > 1 else 1.0
                do2 = _stack_heads(w * _ld(do_ref, start, d), lo).astype(BF16)
                dl = w * _ld(dbar_ref, start, d)
                lse2 = jnp.concatenate([ls[pi][:, :1], ls[pi][:, HEAD_DIM:HEAD_DIM + 1]], axis=0)
                dl2 = jnp.concatenate([dl[:, :1], dl[:, HEAD_DIM:HEAD_DIM + 1]], axis=0)
                s = _scores(q2, k2, base, prev_keys, n == 0)
                pr = jnp.exp(s - lse2)
                dp = lax.dot_general(do2, v2, (((1,), (1,)), ((), ())), preferred_element_type=F32)
                ds = (pr * (dp - dl2) * (HEAD_DIM ** -0.5)).astype(BF16)
                dq2 = jnp.dot(ds, k2, preferred_element_type=F32)
                dk2 = lax.dot_general(ds, q2, (((0,), (0,)), ((), ())), preferred_element_type=F32)
                dv2 = lax.dot_general(pr.astype(BF16), do2, (((0,), (0,)), ((), ())), preferred_element_type=F32)
                _acc3(dx_ref, 0, start, d, _unstack_heads(dq2, lo))
                _acc3(dx_ref, 1, prev, d, dk2[:BLOCK])
                _acc3(dx_ref, 1, start, d, dk2[BLOCK:])
                _acc3(dx_ref, 2, prev, d, dv2[:BLOCK])
                _acc3(dx_ref, 2, start, d, dv2[BLOCK:])
                if has_sink:
                    sacc_ref[...] += _unstack_heads(-jnp.exp(sk2 - lse2) * dl2, lo)
                return carry

            lax.fori_loop(0, d * nb, blk, 0, unroll=8)

        dsink_ref[...] = jnp.broadcast_to(jnp.sum(sacc_ref[...], axis=0, keepdims=True), (8, SLAB))

        def emit(ci, carry):
            rows = pl.ds(pl.multiple_of(ci * rows_c, rows_c), rows_c)
            dxo_ref[:, rows, :] = dx_ref[:, rows, :].astype(BF16)
            return carry

        lax.fori_loop(0, S // rows_c, emit, 0, unroll=2)

    smem = pl.BlockSpec(memory_space=pltpu.SMEM)
    return pl.pallas_call(
        body, name=name, grid=(N_SLABS,),
        in_specs=[smem, smem, pl.BlockSpec((3, S, SLAB), lambda p: (0, 0, p)), pl.BlockSpec((S, SLAB), lambda p: (0, p)),
                  pl.BlockSpec((npat, S, SLAB), lambda p: (0, 0, p)), pl.BlockSpec((None, S, SLAB), lambda p: (p, 0, 0))],
        out_specs=[pl.BlockSpec((3, S, SLAB), lambda p: (0, 0, p)), pl.BlockSpec((None, 8, SLAB), lambda p: (p, 0, 0))],
        out_shape=[jax.ShapeDtypeStruct((3, S, D_MODEL), BF16), jax.ShapeDtypeStruct((N_SLABS, 8, SLAB), F32)],
        scratch_shapes=[pltpu.VMEM((S, SLAB), F32), pltpu.VMEM((BLOCK, SLAB), F32), pltpu.VMEM((npat, S, SLAB), F32),
                        pltpu.VMEM((3, S, SLAB), F32)],
        compiler_params=_cparams(("arbitrary",)),
    )(slopes, sinks, qkv, dout, o, lse)


def _place():
    x, y, c = lax.axis_index("x"), lax.axis_index("y"), lax.axis_index("c")
    return x, y, c, 2 * x + y


def _other_chips(x, y):
    return [(1 - x, y), (x, 1 - y), (1 - x, 1 - y)]


HBM_SPEC = pl.BlockSpec(memory_space=pl.ANY)


def _slot(q):
    return 2 * (q % 2) + q // 2


BIG = ("ffn1_w_in", "ffn1_w_out", "ffn2_w_in", "ffn2_w_out", "a_w_qkv", "a_w_o", "kv_w", "b_w_q", "b_w_o")
QKV_SHARD = 3 * D_MODEL // N_CHIPS
ROW_SHARD = D_MODEL // N_CHIPS


FIRST_ITEMS = (("ffn1_w_in", 0), ("ffn1_w_out", 0), ("a_w_qkv", None), ("a_w_o", None))
LATER_ITEMS = {"ffn2_0": (("ffn2_w_in", 0), ("ffn2_w_out", 0), ("kv_w", None)),
               "layer1": (("ffn1_w_in", 1), ("ffn1_w_out", 1), ("b_w_q", None), ("b_w_o", None)),
               "ffn2_1": (("ffn2_w_in", 1), ("ffn2_w_out", 1))}
OUT_SHARD = D_FF // N_CHIPS


def _full_shape(name):
    if name.endswith("w_in"):
        return (D_MODEL, 2 * D_FF)
    if name.endswith("w_out"):
        return (D_FF, D_MODEL)
    if name == "a_w_qkv":
        return (D_MODEL, 3 * D_MODEL)
    if name == "kv_w":
        return (N_CHIPS, 2, ROW_SHARD // 2, 2 * N_KV_B * HEAD_DIM)
    return (N_CHIPS, 2, ROW_SHARD // 2, D_MODEL)


def _gather_src(item, ref, c):
    name, layer = item
    if name.endswith("w_in"):
        return ref.at[layer, pl.ds(c * (D_MODEL // 2), D_MODEL // 2)]
    if name.endswith("w_out"):
        return ref.at[layer, pl.ds(c * (OUT_SHARD // 2), OUT_SHARD // 2)]
    if name == "a_w_qkv":
        return ref.at[0, pl.ds(c * (D_MODEL // 2), D_MODEL // 2)]
    if name == "kv_w":
        return ref.at[pl.ds(c * (ROW_SHARD // 2), ROW_SHARD // 2)]
    return ref.at[0, pl.ds(c * (ROW_SHARD // 2), ROW_SHARD // 2)]


def _gather_dst(item, ref, q, c):
    name, _ = item
    if name.endswith("w_in"):
        return ref.at[pl.ds(c * (D_MODEL // 2), D_MODEL // 2), pl.ds(_slot(q) * HALF_FF, HALF_FF)]
    if name.endswith("w_out"):
        return ref.at[pl.ds(q * OUT_SHARD + c * (OUT_SHARD // 2), OUT_SHARD // 2)]
    if name == "a_w_qkv":
        return ref.at[pl.ds(c * (D_MODEL // 2), D_MODEL // 2), pl.ds(q * QKV_SHARD, QKV_SHARD)]
    return ref.at[q, c]


def _all_gather(items, shards, small):
    n = len(items)
    r = small.shape[0]
    per = 8

    def body(*refs):
        srcs, small_ref = refs[:n], refs[n]
        dsts, s_ref = refs[n + 1:2 * n + 1], refs[2 * n + 1]
        send_sems, recv_sems = refs[2 * n + 2:]
        x, y, c, myq = _place()
        sibling = (x, y, 1 - c)
        chips = _other_chips(x, y)

        def big(t, k, src, q, h, to):
            return pltpu.make_async_remote_copy(src_ref=src, dst_ref=_gather_dst(items[t], dsts[t], q, h),
                                                send_sem=send_sems.at[per * t + k], recv_sem=recv_sems.at[per * t + k],
                                                device_id=to, device_id_type=MESH)

        def tiny(k, q, to):
            return pltpu.make_async_remote_copy(src_ref=small_ref, dst_ref=s_ref.at[q], send_sem=send_sems.at[per * n + k],
                                                recv_sem=recv_sems.at[per * n + k], device_id=to, device_id_type=MESH)

        first = []
        for j, chip in enumerate(chips):
            if j < 2:
                first += [big(t, j, _gather_src(items[t], srcs[t], c), myq, c, (*chip, c)) for t in range(n)]
            first.append(tiny(j, myq, (*chip, c)))
        own = [big(t, 6 + h, _gather_src(items[t], srcs[t], h), myq, h, sibling) for t in range(n) for h in (0, 1)]
        own.append(tiny(3, myq, sibling))
        for cp in first + own:
            cp.start()
        relay_from = ((x + 1 - c) % 2, (y + c) % 2)
        relay_to = ((x + c) % 2, (y + 1 - c) % 2, c)
        q_relay = 2 * relay_from[0] + relay_from[1]
        passed = []
        for t in range(n):
            src = _gather_src(items[t], srcs[t], c)
            for j, (cx, cy) in enumerate(chips[:2]):
                q = 2 * cx + cy
                big(t, j, src, q, c, sibling).wait_recv()
                fwd = big(t, 3 + j, _gather_dst(items[t], dsts[t], q, c), q, c, sibling)
                fwd.start()
                passed.append(fwd)
            relay = big(t, 2, _gather_dst(items[t], dsts[t], q_relay, c), q_relay, c, relay_to)
            relay.start()
            passed.append(relay)
        q_diag = 2 * chips[2][0] + chips[2][1]
        for t in range(n):
            big(t, 2, _gather_src(items[t], srcs[t], c), q_diag, c, sibling).wait_recv()
            fwd = big(t, 5, _gather_dst(items[t], dsts[t], q_diag, c), q_diag, c, sibling)
            fwd.start()
            passed.append(fwd)
        for j, (cx, cy) in enumerate(chips):
            q = 2 * cx + cy
            for t in range(n):
                big(t, 3 + j, _gather_src(items[t], srcs[t], c), q, 1 - c, sibling).wait_recv()
            tiny(j, q, sibling).wait_recv()
        for cp in own:
            cp.wait_recv()
        for cp in first + passed + own:
            cp.wait_send()

    outs = pl.pallas_call(
        body, name="all_gather_layer0",
        in_specs=[HBM_SPEC] * (n + 1), out_specs=[HBM_SPEC] * (n + 1),
        out_shape=[jax.ShapeDtypeStruct(_full_shape(name), BF16) for name, _ in items]
        + [jax.ShapeDtypeStruct((N_CHIPS, r, 128), F32)],
        scratch_shapes=[pltpu.SemaphoreType.DMA((per * n + 4,)), pltpu.SemaphoreType.DMA((per * n + 4,))],
    )(*[shards[item] for item in items], small)
    return list(outs[:n]), outs[n]


SEM_SPEC = pl.BlockSpec(memory_space=pltpu.SEMAPHORE)
DATAFLOW = pltpu.SideEffectType.DATAFLOW_SIDE_EFFECTING
PER_ITEM = 8


def _split_start(name, copies, n_sems, sources, land_shapes, after):
    n, m = len(sources), len(land_shapes)

    def body(*refs):
        srcs, lands = refs[:n], refs[n:n + m]
        send_sems, recv_sems = refs[n + m + 1], refs[n + m + 2]
        token = refs[-1]
        for src, dst_there, _, s, peer in copies(srcs, lands):
            pltpu.make_async_remote_copy(src_ref=src, dst_ref=dst_there, send_sem=send_sems.at[s], recv_sem=recv_sems.at[s],
                                         device_id=peer, device_id_type=MESH).start()
        token[...] = jnp.zeros_like(token)

    src_arrays = [pltpu.with_memory_space_constraint(a, pltpu.HBM) for a in sources]
    land_arrays = [pltpu.with_memory_space_constraint(lax.empty(s.shape, s.dtype), pltpu.HBM) for s in land_shapes]
    hbm = pl.BlockSpec(memory_space=pltpu.HBM)
    outs = pl.pallas_call(
        body, name=name,
        in_specs=[hbm] * (n + m) + [HBM_SPEC],
        out_specs=[SEM_SPEC, SEM_SPEC] + [hbm] * (n + m) + [pl.BlockSpec(memory_space=pltpu.VMEM)],
        out_shape=[pltpu.SemaphoreType.DMA((n_sems,)), pltpu.SemaphoreType.DMA((n_sems,))]
        + [pltpu.HBM(a.shape, a.dtype) for a in src_arrays + land_arrays] + [jax.ShapeDtypeStruct((8, 128), F32)],
        input_output_aliases={i: 2 + i for i in range(n + m)},
        compiler_params=pltpu.CompilerParams(has_side_effects=DATAFLOW),
    )(*src_arrays, *land_arrays, after)
    return (outs[0], outs[1], list(outs[2:2 + n]), list(outs[2 + n:2 + n + m])), outs[-1]


def _split_wait(name, copies, state, after):
    send_sems, recv_sems, srcs_thru, lands_thru = state
    n, m = len(srcs_thru), len(lands_thru)
    after = list(after) if isinstance(after, (list, tuple)) else [after]

    def body(*refs):
        srcs, lands = refs[:n], refs[n:n + m]
        send_sems, recv_sems = refs[n + m], refs[n + m + 1]
        for src, _, dst_here, s, peer in copies(srcs, lands):
            cp = pltpu.make_async_remote_copy(src_ref=src, dst_ref=dst_here, send_sem=send_sems.at[s], recv_sem=recv_sems.at[s],
                                              device_id=peer, device_id_type=MESH)
            cp.wait_send()
            cp.wait_recv()

    hbm = pl.BlockSpec(memory_space=pltpu.HBM)
    outs = pl.pallas_call(
        body, name=name,
        in_specs=[hbm] * (n + m) + [SEM_SPEC, SEM_SPEC] + [HBM_SPEC] * len(after),
        out_specs=[hbm] * (n + m),
        out_shape=[pltpu.HBM(a.shape, a.dtype) for a in srcs_thru + lands_thru],
        input_output_aliases={i: i for i in range(n + m)},
        compiler_params=pltpu.CompilerParams(has_side_effects=DATAFLOW),
    )(*srcs_thru, *lands_thru, send_sems, recv_sems, *after)
    return list(outs[:n]), list(outs[n:])


def _gather_copies(items):
    def copies(srcs, lands):
        x, y, c, myq = _place()
        out = []
        for t, item in enumerate(items):
            for h in (0, 1):
                src = _gather_src(item, srcs[t], h)
                for j, (cx, cy) in enumerate(_other_chips(x, y)):
                    out.append((src, _gather_dst(item, lands[t], myq, h), _gather_dst(item, lands[t], 2 * cx + cy, h),
                                PER_ITEM * t + 2 * j + h, (cx, cy, c)))
                out.append((src, _gather_dst(item, lands[t], myq, h), _gather_dst(item, lands[t], myq, h),
                            PER_ITEM * t + 6 + h, (x, y, 1 - c)))
        return out
    return copies


def _gather_start(tag, shards, after):
    items = LATER_ITEMS[tag]
    lands = [jax.ShapeDtypeStruct(_full_shape(name), BF16) for name, _ in items]
    return _split_start("gather_%s_start" % tag, _gather_copies(items), PER_ITEM * len(items),
                        [shards[item] for item in items], lands, after)


def _gather_wait(tag, state, after):
    return _split_wait("gather_%s_wait" % tag, _gather_copies(LATER_ITEMS[tag]), state, after)


def _small_all_reduce(v, after=()):
    r = v.shape[0]

    def body(v_ref, *rest):
        o_ref, buf_ref, send_sems, recv_sems = rest[len(after):]
        x, y, c, _ = _place()
        me = 4 * x + 2 * y + c
        buf_ref[me] = v_ref[...]
        copies = []
        for k in range(1, 8):
            fx, fy, fc = (k >> 2) & 1, (k >> 1) & 1, k & 1
            to = (x ^ fx, y ^ fy, c ^ fc)
            cp = pltpu.make_async_remote_copy(src_ref=v_ref, dst_ref=buf_ref.at[me], send_sem=send_sems.at[k - 1],
                                              recv_sem=recv_sems.at[k - 1], device_id=to, device_id_type=MESH)
            cp.start()
            copies.append(cp)
        for k in range(1, 8):
            fx, fy, fc = (k >> 2) & 1, (k >> 1) & 1, k & 1
            src_dev = 4 * (x ^ fx) + 2 * (y ^ fy) + (c ^ fc)
            pltpu.make_async_remote_copy(src_ref=v_ref, dst_ref=buf_ref.at[src_dev], send_sem=send_sems.at[k - 1],
                                         recv_sem=recv_sems.at[k - 1], device_id=(x, y, c), device_id_type=MESH).wait_recv()
        for cp in copies:
            cp.wait_send()
        tot = buf_ref[0]
        for i in range(1, 8):
            tot = tot + buf_ref[i]
        o_ref[...] = tot

    vm = pl.BlockSpec(memory_space=pltpu.VMEM)
    return pl.pallas_call(
        body, name="small_all_reduce", in_specs=[vm] + [HBM_SPEC] * len(after), out_specs=vm,
        out_shape=jax.ShapeDtypeStruct((r, 128), F32),
        scratch_shapes=[pltpu.VMEM((8, r, 128), F32), pltpu.SemaphoreType.DMA((7,)), pltpu.SemaphoreType.DMA((7,))],
    )(v, *after)


def _grad_view(kind, g):
    if kind == "col":
        return g.reshape(2, g.shape[0] // 2, g.shape[1])
    return g.reshape(N_CHIPS, 2, g.shape[0] // (2 * N_CHIPS), g.shape[1])


def _half_of(kind, ref, h):
    return ref.at[h] if kind == "col" else ref.at[:, h]


def _half_shape(kind, view_shape):
    return view_shape[1:] if kind == "col" else (view_shape[0],) + view_shape[2:]


def _piece_of(kind, width, colblock, ref, q):
    if kind == "col":
        return ref.at[:, pl.ds(colblock(q) * width, width)]
    return ref.at[q]


def _piece_shape(kind, width, half_shape):
    return (half_shape[0], width) if kind == "col" else half_shape[1:]


def _pair_exchange(views, kinds, name):
    n = len(views)

    def body(*refs):
        ins, outs = refs[:n], refs[n:2 * n]
        send_sems, recv_sems = refs[2 * n:]
        x, y, c, _ = _place()
        cps = []
        for t in range(n):
            cp = pltpu.make_async_remote_copy(src_ref=_half_of(kinds[t], ins[t], 1 - c), dst_ref=outs[t],
                                              send_sem=send_sems.at[t], recv_sem=recv_sems.at[t],
                                              device_id=(x, y, 1 - c), device_id_type=MESH)
            cp.start()
            cps.append(cp)
        for cp in cps:
            cp.wait()

    return pl.pallas_call(
        body, name=name, in_specs=[HBM_SPEC] * n, out_specs=[HBM_SPEC] * n,
        out_shape=[jax.ShapeDtypeStruct(_half_shape(k, v.shape), v.dtype) for k, v in zip(kinds, views)],
        scratch_shapes=[pltpu.SemaphoreType.DMA((n,)), pltpu.SemaphoreType.DMA((n,))],
    )(*views)


def _pair_sum(kind, view, recv, c, name):
    hs = recv.shape
    N = hs[-1]
    rows = hs[-2]
    tr = _pick(rows, (512, 352, 128))
    tn = _pick(N, (1408, 1024, 512))

    def body(c_ref, p_ref, r_ref, s_ref):
        s_ref[...] = (p_ref[...] + r_ref[...]).astype(BF16)

    if kind == "col":
        grid = (rows // tr, N // tn)
        mine = pl.BlockSpec((None, tr, tn), lambda i, j, c_ref: (c_ref[0], i, j))
        blk = pl.BlockSpec((tr, tn), lambda i, j, c_ref: (i, j))
        sem = ("parallel", "parallel")
    else:
        grid = (N_CHIPS, rows // tr, N // tn)
        mine = pl.BlockSpec((None, None, tr, tn), lambda q, i, j, c_ref: (q, c_ref[0], i, j))
        blk = pl.BlockSpec((None, tr, tn), lambda q, i, j, c_ref: (q, i, j))
        sem = ("parallel", "parallel", "parallel")
    return pl.pallas_call(
        body, name=name,
        grid_spec=pltpu.PrefetchScalarGridSpec(num_scalar_prefetch=1, grid=grid, in_specs=[mine, blk], out_specs=blk),
        out_shape=jax.ShapeDtypeStruct(hs, BF16),
        compiler_params=_cparams(sem),
    )(c.reshape(1).astype(jnp.int32), view, recv)


def _chip_copies(kinds, widths, colblocks):
    def copies(srcs, lands):
        x, y, c, _ = _place()
        out = []
        for j, (cx, cy) in enumerate(_other_chips(x, y)):
            for t in range(len(kinds)):
                out.append((_piece_of(kinds[t], widths[t], colblocks[t], srcs[t], 2 * cx + cy), lands[t].at[j],
                            lands[t].at[j], 3 * t + j, (cx, cy, c)))
        return out
    return copies


def _chip_land_shapes(sums, kinds, widths):
    return [jax.ShapeDtypeStruct((3,) + _piece_shape(k, w, s.shape), BF16) for k, w, s in zip(kinds, widths, sums)]


def _chip_exchange(sums, kinds, widths, colblocks, name):
    n = len(sums)
    copies = _chip_copies(kinds, widths, colblocks)

    def body(*refs):
        send_sems, recv_sems = refs[2 * n:]
        cps = [pltpu.make_async_remote_copy(src_ref=src, dst_ref=dst, send_sem=send_sems.at[s], recv_sem=recv_sems.at[s],
                                            device_id=peer, device_id_type=MESH)
               for src, dst, _, s, peer in copies(refs[:n], refs[n:2 * n])]
        for cp in cps:
            cp.start()
        for cp in cps:
            cp.wait()

    return pl.pallas_call(
        body, name=name, in_specs=[HBM_SPEC] * n, out_specs=[HBM_SPEC] * n,
        out_shape=_chip_land_shapes(sums, kinds, widths),
        scratch_shapes=[pltpu.SemaphoreType.DMA((3 * n,)), pltpu.SemaphoreType.DMA((3 * n,))],
    )(*sums)


N_DIRECT = 7


def _direct_piece(kind, width, colblock, view_ref, q, h):
    if kind == "col":
        return view_ref.at[h, :, pl.ds(colblock(q) * width, width)]
    return view_ref.at[q, h]


def _direct_copies(kinds, widths, colblocks):
    def copies(srcs, lands):
        x, y, c, myq = _place()
        out = []
        for t in range(len(kinds)):
            def piece(q, h, t=t):
                return _direct_piece(kinds[t], widths[t], colblocks[t], srcs[t], q, h)
            for j, (cx, cy) in enumerate(_other_chips(x, y)):
                for h in (0, 1):
                    out.append((piece(2 * cx + cy, h), lands[t].at[2 * j + c], lands[t].at[2 * j + h],
                                10 * t + 3 * j + c + h, (cx, cy, h)))
            out.append((piece(myq, 1 - c), lands[t].at[6], lands[t].at[6], 10 * t + 9, (x, y, 1 - c)))
        return out
    return copies


def _chip_sum(kind, own_src, recv, block_idx, c, shard_shape, layer, into, name, direct=False):
    n_recv, rows, N = recv.shape
    tr = _pick(rows, (512, 352, 128))
    tn = _pick(N, (1408, 1024, 768, 512))
    ni, nj = rows // tr, N // tn

    def body(q_ref, s_ref, r_ref, *rest):
        o_ref = rest[-1]
        tot = s_ref[...].astype(F32)
        for k in range(n_recv):
            tot = tot + r_ref[k].astype(F32)
        o_ref[...] = tot

    if direct and kind == "col":
        own = pl.BlockSpec((None, tr, tn), lambda i, j, q_ref: (q_ref[1], i, q_ref[0] * nj + j))
    elif direct:
        own = pl.BlockSpec((None, None, tr, tn), lambda i, j, q_ref: (q_ref[0], q_ref[1], i, j))
    elif kind == "col":
        own = pl.BlockSpec((tr, tn), lambda i, j, q_ref: (i, q_ref[0] * nj + j))
    else:
        own = pl.BlockSpec((None, tr, tn), lambda i, j, q_ref: (q_ref[0], i, j))
    if len(shard_shape) == 3:
        lead = 0 if layer is None else layer
        out_spec = pl.BlockSpec((None, tr, tn), lambda i, j, q_ref: (lead, q_ref[1] * ni + i, j))
    else:
        out_spec = pl.BlockSpec((tr, tn), lambda i, j, q_ref: (q_ref[1] * ni + i, j))
    in_specs = [own, pl.BlockSpec((n_recv, tr, tn), lambda i, j, q_ref: (0, i, j))]
    s = own_src
    args = [jnp.stack([block_idx, c]).astype(jnp.int32), s, recv]
    aliases = {}
    if into is not None:
        in_specs.append(HBM_SPEC)
        args.append(into)
        aliases = {3: 0}
    return pl.pallas_call(
        body, name=name,
        grid_spec=pltpu.PrefetchScalarGridSpec(num_scalar_prefetch=1, grid=(ni, nj), in_specs=in_specs, out_specs=out_spec),
        out_shape=jax.ShapeDtypeStruct(shard_shape, F32), input_output_aliases=aliases,
        compiler_params=_cparams(("parallel", "parallel")),
    )(*args)


def _half_window(ref, h):
    rows = ref.shape[-2] // 2
    if ref.ndim == 3:
        return ref.at[:, pl.ds(h * rows, rows)]
    return ref.at[pl.ds(h * rows, rows)]


def _share_halves(grads, name):
    n = len(grads)

    def body(*refs):
        outs = refs[n:2 * n]
        send_sems, recv_sems = refs[2 * n:]
        x, y, c, _ = _place()
        cps = []
        for t in range(n):
            cp = pltpu.make_async_remote_copy(src_ref=_half_window(outs[t], c), dst_ref=_half_window(outs[t], c),
                                              send_sem=send_sems.at[t], recv_sem=recv_sems.at[t],
                                              device_id=(x, y, 1 - c), device_id_type=MESH)
            cp.start()
            cps.append(cp)
        for t in range(n):
            cps[t].wait_send()
            pltpu.make_async_remote_copy(src_ref=_half_window(outs[t], c), dst_ref=_half_window(outs[t], 1 - c),
                                         send_sem=send_sems.at[t], recv_sem=recv_sems.at[t],
                                         device_id=(x, y, 1 - c), device_id_type=MESH).wait_recv()

    return pl.pallas_call(
        body, name=name, in_specs=[HBM_SPEC] * n, out_specs=[HBM_SPEC] * n,
        out_shape=[jax.ShapeDtypeStruct(g.shape, F32) for g in grads],
        input_output_aliases={t: t for t in range(n)},
        scratch_shapes=[pltpu.SemaphoreType.DMA((n,)), pltpu.SemaphoreType.DMA((n,))],
    )(*grads)


def _adamw(w, g, m, v, name):
    R, W = w.shape
    tr = _pick(R, (512, 352, 256, 32))

    def body(w_ref, g_ref, m_ref, v_ref, d_ref, nm_ref, nv_ref, go_ref):
        gv = g_ref[...]
        go_ref[...] = gv
        nm = ADAM_B1 * m_ref[...] + (1.0 - ADAM_B1) * gv
        nv = ADAM_B2 * v_ref[...] + (1.0 - ADAM_B2) * (gv * gv)
        m_hat = nm / (1.0 - ADAM_B1 ** ADAM_STEP)
        v_hat = nv / (1.0 - ADAM_B2 ** ADAM_STEP)
        d_ref[...] = -ADAM_LR * (m_hat / (jnp.sqrt(v_hat) + ADAM_EPS) + ADAM_WD * w_ref[...])
        nm_ref[...] = nm
        nv_ref[...] = nv

    blk = pl.BlockSpec((tr, W), lambda i: (i, 0))
    shp = jax.ShapeDtypeStruct((R, W), F32)
    return pl.pallas_call(
        body, name=name, grid=(R // tr,), in_specs=[blk] * 4, out_specs=[blk] * 4, out_shape=[shp] * 4,
        compiler_params=_cparams(("parallel",)),
    )(w, g, m, v)


SMALL_ROWS = 32


def _pack_small(ln_g, ln_b, sinks):
    rows = jnp.concatenate([ln_g.reshape(-1, 128), ln_b.reshape(-1, 128),
                            jnp.pad(sinks.reshape(1, -1), ((0, 0), (0, 128 - sinks.size)))], axis=0)
    return jnp.pad(rows, ((0, SMALL_ROWS - rows.shape[0]), (0, 0)))


def _unpack_small(s, ln_shape, sink_shape):
    n = ln_shape[0] * ln_shape[1] * ln_shape[2] // 128
    return s[:n].reshape(ln_shape), s[n:2 * n].reshape(ln_shape), s[2 * n, :sink_shape[1]].reshape(sink_shape)


def _ffn_fwd(xin, w_in, w_out, gain, bias, tag):
    u, h = _ffn_in(xin, w_in, "ffn_in_" + tag)
    y, yb, z = _mm_ln(h, w_out, xin, gain, bias, 0.5, "ffn_out_ln_" + tag)
    return y, yb, dict(u=u, h=h, z=z, xin=xin)


def _ffn_bwd(dz, dzc, saved, w_in, w_out, xin_b, tag, dw_dtype=F32, ln=None, ready=None):
    du = _ffn_bwd_h(dzc, w_out, saved["u"], "ffn_bwd_h_" + tag)
    d_w_out = _mm_tn(saved["h"], dzc, "ffn_dwout_" + tag, out_dtype=dw_dtype)
    d_w_in = _mm_tn(xin_b, du, "ffn_dwin_" + tag, out_dtype=dw_dtype)
    after = None if ready is None else ready(d_w_in, d_w_out)
    dx = _mm_nt(du, w_in, "ffn_dx_" + tag, add=dz, add_scale=ALPHA, ln=ln, after=after)
    return dx, d_w_in, d_w_out


def kernel(x, ffn1_w_in, ffn1_w_out, ffn2_w_in, ffn2_w_out, ln_g, ln_b, a_w_qkv, a_w_o, kv_w, b_w_q, b_sinks, b_w_o, loss_target, m_ffn1_w_in, m_ffn1_w_out, m_ffn2_w_in, m_ffn2_w_out, m_ln_g, m_ln_b, m_a_w_qkv, m_a_w_o, m_kv_w, m_b_w_q, m_b_sinks, m_b_w_o, v_ffn1_w_in, v_ffn1_w_out, v_ffn2_w_in, v_ffn2_w_out, v_ln_g, v_ln_b, v_a_w_qkv, v_a_w_o, v_kv_w, v_b_w_q, v_b_sinks, v_b_w_o):
    ws = dict(ffn1_w_in=ffn1_w_in, ffn1_w_out=ffn1_w_out, ffn2_w_in=ffn2_w_in, ffn2_w_out=ffn2_w_out, a_w_qkv=a_w_qkv,
              a_w_o=a_w_o, kv_w=kv_w, b_w_q=b_w_q, b_w_o=b_w_o)
    ms = dict(ffn1_w_in=m_ffn1_w_in, ffn1_w_out=m_ffn1_w_out, ffn2_w_in=m_ffn2_w_in, ffn2_w_out=m_ffn2_w_out,
              a_w_qkv=m_a_w_qkv, a_w_o=m_a_w_o, kv_w=m_kv_w, b_w_q=m_b_w_q, b_w_o=m_b_w_o)
    vs = dict(ffn1_w_in=v_ffn1_w_in, ffn1_w_out=v_ffn1_w_out, ffn2_w_in=v_ffn2_w_in, ffn2_w_out=v_ffn2_w_out,
              a_w_qkv=v_a_w_qkv, a_w_o=v_a_w_o, kv_w=v_kv_w, b_w_q=v_b_w_q, b_w_o=v_b_w_o)
    _, _, c_idx, myq = _place()
    xs = x[0]
    target = loss_target[0]

    later = tuple(item for items in LATER_ITEMS.values() for item in items)
    shards = {(n, l): ws[n].astype(BF16) for n, l in FIRST_ITEMS + later}

    def as_weights(items, arrays):
        return {n: (a.reshape(D_MODEL, a.shape[-1]) if a.ndim == 4 else a) for (n, _), a in zip(items, arrays)}

    first, small = _all_gather(FIRST_ITEMS, shards, _pack_small(ln_g, ln_b, b_sinks))
    states = {}
    states["ffn2_0"], token = _gather_start("ffn2_0", shards, small)

    def later_weights(tag, after):
        if isinstance(tag, tuple):
            states[tag[1]], tok = _gather_start(tag[1], shards, after)
            return tok[0, 0]
        handed_on, full = _gather_wait(tag, states[tag], after)
        if tag == "ffn2_0":
            shards.update(zip(LATER_ITEMS["ffn2_1"], handed_on))
        return as_weights(LATER_ITEMS[tag], full)

    n_ln = ln_g.size // 128
    lg = jnp.concatenate([small[q, :n_ln].reshape(DEPTH, 3, 1, -1) for q in range(N_CHIPS)], axis=-1)
    lb = jnp.concatenate([small[q, n_ln:2 * n_ln].reshape(DEPTH, 3, 1, -1) for q in range(N_CHIPS)], axis=-1)
    lg = lg + token[0, 0]
    reducer = _GradReducer(c_idx, myq, {n: ws[n].shape for n in BIG})
    sq, grad_x, _, gg, gb, dsink_part = _local_step(xs, target, as_weights(FIRST_ITEMS, first), later_weights,
                                                    lg, lb, b_sinks.reshape(N_HEADS), reducer.begin)

    loss_row = jnp.pad(jnp.sum(sq).reshape(1, 1), ((0, 0), (0, 127)))
    dsinks = jnp.pad(dsink_part[:, 0, :].reshape(N_SLABS, 2, HEAD_DIM)[:, :, 0].reshape(1, N_HEADS), ((0, 0), (0, 128 - N_HEADS)))
    gg_full = jnp.stack([jnp.stack([jnp.sum(gg[i][j], axis=0) for j in range(3)]) for i in range(DEPTH)])
    gb_full = jnp.stack([jnp.stack([jnp.sum(gb[i][j], axis=0) for j in range(3)]) for i in range(DEPTH)])
    small_in = jnp.concatenate([loss_row, dsinks, gg_full.reshape(-1, 128), gb_full.reshape(-1, 128)], axis=0)
    small_in = jnp.pad(small_in, ((0, (-small_in.shape[0]) % 8), (0, 0)))
    def reduce_small(after):
        small_sum = _small_all_reduce(small_in, after)
        loss = small_sum[0, 0] * (0.5 / D_MODEL)
        grad_sinks = small_sum[1, :N_HEADS].reshape(b_sinks.shape)
        n_full = DEPTH * 3 * D_MODEL // 128
        cols = D_MODEL // N_CHIPS
        grad_ln_g = lax.dynamic_slice_in_dim(small_sum[2:2 + n_full].reshape(DEPTH, 3, D_MODEL), myq * cols, cols, axis=2)
        grad_ln_b = lax.dynamic_slice_in_dim(small_sum[2 + n_full:2 + 2 * n_full].reshape(DEPTH, 3, D_MODEL), myq * cols, cols, axis=2)
        return loss, grad_ln_g, grad_ln_b, grad_sinks

    return _update(reducer, grad_x, reduce_small, ws, ms, vs,
                   (ln_g, ln_b, b_sinks), (m_ln_g, m_ln_b, m_b_sinks), (v_ln_g, v_ln_b, v_b_sinks))


def _local_step(xs, target, W, later_weights, lg, lb, sinks, grads_ready=None):
    if grads_ready is None:
        grads_ready = lambda tag, grads, overlap: 0.0
    S = xs.shape[0]
    slopes = jnp.asarray(_alibi_slopes(N_HEADS))
    in1, out1 = [W["ffn1_w_in"]], [W["ffn1_w_out"]]

    y1, y1b, s1 = _ffn_fwd(xs, in1[0], out1[0], lg[0, 0], lb[0, 0], "a1")
    lg = lg + later_weights(("start", "layer1"), y1b)
    qkv_a = _mm_nn(y1b, W["a_w_qkv"], F32, "qkv_a", split=True)
    mix_a, o_a, lse_a = _attn_fwd(qkv_a, slopes, None, PATTERNS_A, "attn_a_fwd")
    y2, y2b, z2 = _mm_ln(mix_a, W["a_w_o"], y1, lg[0, 1], lb[0, 1], 1.0, "attn_a_out_ln")
    W = dict(W, **later_weights("ffn2_0", y2b))
    in2, out2 = [W["ffn2_w_in"]], [W["ffn2_w_out"]]
    lg = lg + later_weights(("start", "ffn2_1"), in2[0])
    y3, y3b, s3 = _ffn_fwd(y2, in2[0], out2[0], lg[0, 2], lb[0, 2], "a2")
    kv_w_rep = jnp.broadcast_to(W["kv_w"].reshape(D_MODEL, 2, N_KV_B, 1, HEAD_DIM),
                                (D_MODEL, 2, N_KV_B, GROUP_B, HEAD_DIM)).reshape(D_MODEL, 2 * D_MODEL)
    kv_rep = _mm_nn(y3b, kv_w_rep, F32, "kv_proj", split=(1, 2))
    W = dict(W, **later_weights("layer1", kv_rep))
    in1, out1 = in1 + [W["ffn1_w_in"]], out1 + [W["ffn1_w_out"]]
    y4, y4b, s4 = _ffn_fwd(y3, in1[1], out1[1], lg[1, 0], lb[1, 0], "b1")
    qkv_b = _mm_nn(y4b, W["b_w_q"], F32, "q_b", split=(0, 1), into=kv_rep)
    mix_b, o_b, lse_b = _attn_fwd(qkv_b, slopes, sinks, PATTERNS_B, "attn_b_fwd")
    y5, y5b, z5 = _mm_ln(mix_b, W["b_w_o"], y4, lg[1, 1], lb[1, 1], 1.0, "attn_b_out_ln")
    last = later_weights("ffn2_1", y5b)
    in2, out2 = in2 + [last["ffn2_w_in"]], out2 + [last["ffn2_w_out"]]
    y6, _, s6 = _ffn_fwd(y5, in2[1], out2[1], lg[1, 2], lb[1, 2], "b2")

    gr = {n: None for n in BIG}
    gg = [[None] * 3 for _ in range(DEPTH)]
    gb = [[None] * 3 for _ in range(DEPTH)]
    dz6, dz6c, gg[1][2], gb[1][2], sq = _loss_ln_bwd(y6, target, s6["z"], lg[1, 2], 0.5, "loss_ln_bwd")

    (dz5, dz5b, gg[1][1], gb[1][1]), d_in2_b, d_out2_b = _ffn_bwd(dz6, dz6c, s6, in2[1], out2[1], y5b, "b2", BF16,
                                                                  ln=(z5, lg[1, 1], 1.0))
    gr["b_w_o"] = _mm_tn(mix_b, dz5b, "d_b_w_o", out_dtype=BF16)
    dmix_b = _mm_nt(dz5b, W["b_w_o"], "d_mix_b")
    dqkv_b, dsink_part = _attn_bwd(qkv_b, dmix_b, o_b, lse_b, slopes, sinks, PATTERNS_B, "attn_b_bwd")
    dq_b = (dqkv_b, 0)
    gr["b_w_q"] = _mm_tn(y4b, dq_b, "d_b_w_q", out_dtype=BF16)
    dz4, dz4c, gg[1][0], gb[1][0] = _mm_nt(dq_b, W["b_w_q"], "d_y4", add=dz5, add_scale=ALPHA, ln=(s4["z"], lg[1, 0], 0.5))
    dy3, d_in1_b, d_out1_b = _ffn_bwd(dz4, dz4c, s4, in1[1], out1[1], y3b, "b1", BF16)
    gr["kv_w"] = _d_kv_w(y3b, dqkv_b, "d_kv_w")
    tok = grads_ready("l1", {("ffn2_w_in", 1): d_in2_b, ("ffn2_w_out", 1): d_out2_b, ("b_w_o", None): gr["b_w_o"],
                             ("b_w_q", None): gr["b_w_q"], ("ffn1_w_in", 1): d_in1_b, ("ffn1_w_out", 1): d_out1_b,
                             ("kv_w", None): gr["kv_w"]}, True)
    lg0 = lg[0] + tok
    dz3, dz3c, gg[0][2], gb[0][2] = _mm_nt(dqkv_b, kv_w_rep, "d_y3_kv", add=dy3, add_scale=1.0, split=(1, 2),
                                           ln=(s3["z"], lg0[2], 0.5))

    (dz2, dz2b, gg[0][1], gb[0][1]), d_in2_a, d_out2_a = _ffn_bwd(dz3, dz3c, s3, in2[0], out2[0], y2b, "a2", BF16,
                                                                  ln=(z2, lg0[1], 1.0))
    tok = grads_ready("a2", {("ffn2_w_in", 0): d_in2_a, ("ffn2_w_out", 0): d_out2_a}, True)
    lg0 = lg0 + tok
    gr["a_w_o"] = _mm_tn(mix_a, dz2b, "d_a_w_o", out_dtype=BF16)
    dmix_a = _mm_nt(dz2b, W["a_w_o"], "d_mix_a")
    dqkv_a, _ = _attn_bwd(qkv_a, dmix_a, o_a, lse_a, slopes, None, PATTERNS_A, "attn_a_bwd")
    gr["a_w_qkv"] = _mm_tn(y1b, dqkv_a, "d_a_w_qkv", split=True, out_dtype=BF16)
    tok = grads_ready("mix", {("a_w_o", None): gr["a_w_o"], ("a_w_qkv", None): gr["a_w_qkv"]}, True)
    lg0 = lg0 + tok
    dz1, dz1c, gg[0][0], gb[0][0] = _mm_nt(dqkv_a, W["a_w_qkv"], "d_y1", add=dz2, add_scale=ALPHA, split=True,
                                           ln=(s1["z"], lg0[0], 0.5))
    def ready_a1(d_in, d_out):
        tok = grads_ready("a1", {("ffn1_w_in", 0): d_in, ("ffn1_w_out", 0): d_out}, True)
        return jnp.reshape(jnp.asarray(tok, F32), (1, 1))

    grad_x, d_in1_a, d_out1_a = _ffn_bwd(dz1, dz1c, s1, in1[0], out1[0], xs, "a1", BF16, ready=ready_a1)
    gr["ffn1_w_in"] = [d_in1_a, d_in1_b]
    gr["ffn1_w_out"] = [d_out1_a, d_out1_b]
    gr["ffn2_w_in"] = [d_in2_a, d_in2_b]
    gr["ffn2_w_out"] = [d_out2_a, d_out2_b]
    return sq, grad_x, gr, gg, gb, dsink_part


def _grad_item(name, layer, g):
    if name.endswith("w_in"):
        return (g, "col", HALF_FF, _slot, name, layer)
    if name.endswith("w_out"):
        return (g, "row", D_MODEL, None, name, layer)
    if name == "a_w_qkv":
        return (g, "col", QKV_SHARD, lambda q: q, name, None)
    return (g, "row", g.shape[1], None, name, None)


class _GradReducer:
    def __init__(self, c_idx, myq, shard_shapes):
        self.c_idx, self.myq, self.shard_shapes = c_idx, myq, shard_shapes
        self.groups = []

    def begin(self, tag, grads, overlap):
        items = [_grad_item(n, l, g) for (n, l), g in grads.items()]
        kinds, widths, colblocks = [it[1] for it in items], [it[2] for it in items], [it[3] for it in items]
        views = [_grad_view(k, it[0]) for k, it in zip(kinds, items)]
        if overlap:
            lands = [jax.ShapeDtypeStruct((N_DIRECT,) + _piece_shape(k, w, _half_shape(k, v.shape)), BF16)
                     for k, w, v in zip(kinds, widths, views)]
            state, token = _split_start("grad_direct_start_" + tag, _direct_copies(kinds, widths, colblocks), 10 * len(items),
                                        views, lands, jnp.zeros((8, 128), F32))
            self.groups.append((tag, items, None, state, token))
            return token[0, 0]
        from_sibling = _pair_exchange(views, kinds, "grad_pair_exchange_" + tag)
        sums = [_pair_sum(k, v, r, self.c_idx, "pair_sum_%s_%d" % (tag, t))
                for t, (k, v, r) in enumerate(zip(kinds, views, from_sibling))]
        self.groups.append((tag, items, sums, None, None))
        return 0.0

    def _sum_group(self, tag, items, sums, received, direct):
        for t, (it, s, r) in enumerate(zip(items, sums, received)):
            _, k, _, cb, name, layer = it
            own = cb(self.myq) if k == "col" else self.myq
            self.half_done[name] = _chip_sum(k, s, r, own, self.c_idx, self.shard_shapes[name], layer,
                                             self.half_done.get(name), "chip_sum_%s_%d" % (tag, t), direct=direct)

    def finish_first(self, after):
        self.half_done, self.late, early = {}, [], []
        started = [after]
        for g, (tag, items, sums, state, token) in enumerate(self.groups):
            kinds, widths, colblocks = [it[1] for it in items], [it[2] for it in items], [it[3] for it in items]
            if state is None:
                copies = _chip_copies(kinds, widths, colblocks)
                state, token = _split_start("grad_chip_start_" + tag, copies, 3 * len(items), sums,
                                            _chip_land_shapes(sums, kinds, widths), sums[-1])
                self.late.append((tag, items, copies, state, False))
                started.append(token)
            elif g == len(self.groups) - 1:
                self.late.append((tag, items, _direct_copies(kinds, widths, colblocks), state, True))
                started.append(token)
            else:
                early.append((tag, items, _direct_copies(kinds, widths, colblocks), state))
        for tag, items, copies, state in early:
            views, received = _split_wait("grad_direct_wait_" + tag, copies, state, started)
            self._sum_group(tag, items, views, received, True)
        late_names = {it[4] for _, items, _, _, _ in self.late for it in items}
        names = [n for n in BIG if n not in late_names]
        return dict(zip(names, _share_halves([self.half_done[n] for n in names], "grad_share_halves_first")))

    def finish_rest(self, after):
        names = []
        for tag, items, copies, state, direct in self.late:
            sums, received = _split_wait("grad_late_wait_" + tag, copies, state, after)
            self._sum_group(tag, items, sums, received, direct)
            names += [it[4] for it in items if it[4] not in names]
        return dict(zip(names, _share_halves([self.half_done[n] for n in names], "grad_share_halves_rest")))


def _update(reducer, grad_x, reduce_small, ws, ms, vs, small_w, small_m, small_v):
    ln_g, ln_b, b_sinks = small_w
    m_ln_g, m_ln_b, m_b_sinks = small_m
    v_ln_g, v_ln_b, v_b_sinks = small_v

    grads, deltas, new_m, new_v = {}, {}, {}, {}

    def update(some):
        done = []
        for name in some:
            shp = ws[name].shape
            flat = lambda a: a.reshape(-1, shp[-1])
            d, nm, nv, g = _adamw(flat(ws[name]), flat(some[name]), flat(ms[name]), flat(vs[name]), "adamw_" + name)
            grads[name], deltas[name], new_m[name], new_v[name] = g.reshape(shp), d.reshape(shp), nm.reshape(shp), nv.reshape(shp)
            done.append(d)
        return done

    rest = reducer.finish_rest(update(reducer.finish_first(grad_x)))
    loss, grad_ln_g, grad_ln_b, grad_sinks = reduce_small(list(rest.values()))
    update(rest)
    delta_s, nm_s, nv_s, _ = _adamw(_pack_small(ln_g, ln_b, b_sinks), _pack_small(grad_ln_g, grad_ln_b, grad_sinks),
                                    _pack_small(m_ln_g, m_ln_b, m_b_sinks), _pack_small(v_ln_g, v_ln_b, v_b_sinks), "adamw_small")
    for d, blob in ((grads, None), (deltas, delta_s), (new_m, nm_s), (new_v, nv_s)):
        if blob is None:
            d["ln_g"], d["ln_b"], d["b_sinks"] = grad_ln_g, grad_ln_b, grad_sinks
        else:
            d["ln_g"], d["ln_b"], d["b_sinks"] = _unpack_small(blob, ln_g.shape, b_sinks.shape)

    order = ("ffn1_w_in", "ffn1_w_out", "ffn2_w_in", "ffn2_w_out", "ln_g", "ln_b", "a_w_qkv", "a_w_o", "kv_w", "b_w_q",
             "b_sinks", "b_w_o")
    outs = [loss, grad_x[None]]
    for d in (grads, deltas, new_m, new_v):
        outs += [d[n] for n in order]
    return tuple(outs)
```

```python
import numpy as np
import jax
import jax.numpy as jnp
from jax import lax
from jax.experimental import pallas as pl
from jax.experimental.pallas import tpu as pltpu

F32 = jnp.float32
BF16 = jnp.bfloat16

D_MODEL = 1024
D_FF = 2816
HALF_FF = D_FF // 2
HEAD_DIM = 64
N_HEADS = 16
N_KV_B = 4
GROUP_B = N_HEADS // N_KV_B
DEPTH = 2
ALPHA = (2.0 * DEPTH) ** 0.25
LN_EPS = 1e-5
BLOCK = 128
SLAB = 128
N_SLABS = D_MODEL // SLAB
PATTERNS_A = ((1, 128, 1.0), (4, 128, 4.0), (16, 128, 16.0))
PATTERNS_B = ((1, 127, 1.0),)
NEG = -1e30

ADAM_LR = 0.001
ADAM_B1 = 0.9
ADAM_B2 = 0.999
ADAM_EPS = 1e-08
ADAM_WD = 0.01
ADAM_STEP = 10

N_CHIPS = 4
VMEM_LIMIT = 56 * 1024 * 1024
WHOLE_WEIGHT_BYTES = 12 * 1024 * 1024
MESH = pl.DeviceIdType.MESH


def _alibi_slopes(n):
    return np.array([2.0 ** (-8.0 * (h + 1) / n) for h in range(n)], dtype=np.float32)


def _cparams(sem=None, vmem=VMEM_LIMIT):
    return pltpu.CompilerParams(dimension_semantics=sem, vmem_limit_bytes=vmem)


_DIMS = {"nn": ((1,), (0,)), "nt": ((1,), (1,)), "tn": ((0,), (0,))}


def _unlead(x):
    if isinstance(x, tuple):
        return x[0], x[1], x[0].shape[1:]
    return x, None, x.shape


def _bspec(block, imap, lead=None, **kw):
    if lead is None:
        return pl.BlockSpec(block, imap, **kw)
    return pl.BlockSpec((None,) + tuple(block), lambda *g: (lead,) + tuple(imap(*g)), **kw)


def _ln_bwd_math(zv, dyv, gain):
    rows = zv.shape[0]
    mu = jnp.mean(zv, axis=-1, keepdims=True)
    zc = zv - mu
    var = jnp.mean(zc * zc, axis=-1, keepdims=True)
    rstd = lax.rsqrt(var + LN_EPS)
    xhat = zc * rstd
    dyg = dyv * gain
    m1 = jnp.mean(dyg, axis=-1, keepdims=True)
    m2 = jnp.mean(dyg * xhat, axis=-1, keepdims=True)
    dz = rstd * (dyg - m1 - xhat * m2)
    pg = jnp.sum((dyv * xhat).reshape(rows // 8, 8, D_MODEL), axis=0)
    pb = jnp.sum(dyv.reshape(rows // 8, 8, D_MODEL), axis=0)
    return dz, pg, pb


def _matmul(a, b, mode, out_dtype, tm, tn, tk, name, add=None, add_scale=1.0, split=False, into=None, ln=None,
            after=None):
    out_spec = pl.BlockSpec((tm, tn), lambda i, j, k: (i, j))
    base, count = (0, 3) if split is True else (split or (0, 0))
    if mode == "nn":
        a, al, (M, K) = _unlead(a)
        b, bl, (K2, N) = _unlead(b)
        a_spec = _bspec((tm, tk), lambda i, j, k: (i, k), al)
        b_spec = _bspec((tk, tn), lambda i, j, k: (k, j), bl)
        out_struct = jax.ShapeDtypeStruct((M, N), out_dtype)
        if split:
            assert tn == D_MODEL and N == count * tn
            out_spec = pl.BlockSpec((None, tm, tn), lambda i, j, k: (j + base, i, 0))
            out_struct = jax.ShapeDtypeStruct((3, M, tn), out_dtype)
    elif mode == "nt":
        b, bl, (N, K2) = _unlead(b)
        if split:
            M, K = a.shape[1], count * a.shape[2]
            if tk == K:
                a_spec = [pl.BlockSpec((None, tm, D_MODEL), lambda i, j, k, s=s: (s + base, i, 0)) for s in range(count)]
            else:
                assert tk == D_MODEL
                a_spec = pl.BlockSpec((None, tm, tk), lambda i, j, k: (k + base, i, 0))
        else:
            a, al, (M, K) = _unlead(a)
            a_spec = _bspec((tm, tk), lambda i, j, k: (i, k), al)
        whole_b = {"pipeline_mode": pl.Buffered(1)} if (tn, tk) == (N, K2) else {}
        b_spec = _bspec((tn, tk), lambda i, j, k: (j, k), bl, **whole_b)
        out_struct = jax.ShapeDtypeStruct((M, N), out_dtype)
    else:
        a, al, (K, M) = _unlead(a)
        if split:
            assert tn == D_MODEL
            K2, N = b.shape[1], count * b.shape[2]
            b_spec = pl.BlockSpec((None, tk, tn), lambda i, j, k: (j + base, k, 0))
        else:
            b, bl, (K2, N) = _unlead(b)
            b_spec = _bspec((tk, tn), lambda i, j, k: (k, j), bl)
        a_spec = _bspec((tk, tm), lambda i, j, k: (k, i), al)
        out_struct = jax.ShapeDtypeStruct((M, N), out_dtype)
    assert K == K2 and M % tm == 0 and N % tn == 0 and K % tk == 0, (a.shape, b.shape, mode, tm, tn, tk)
    nk = K // tk
    dims = (_DIMS[mode], ((), ()))
    has_add = add is not None

    narrow = out_dtype != F32
    assert not (narrow and has_add)
    if ln is not None:
        assert has_add and mode == "nt" and tn == N == D_MODEL

    a_specs = a_spec if isinstance(a_spec, list) else [a_spec]
    n_a = len(a_specs)

    def body(*refs):
        if after is not None:
            refs = refs[1:]
        a_refs, refs = refs[:n_a], refs[n_a - 1:]
        if into is not None:
            refs = refs[:2] + refs[3:]
        if ln is not None:
            a_ref, b_ref, add_ref, z_ref, g_ref, o_ref, dzc_ref, gg_ref, gb_ref = refs
            acc_ref = o_ref
        elif has_add:
            a_ref, b_ref, add_ref, o_ref = refs
            acc_ref = o_ref
        elif narrow:
            a_ref, b_ref, o_ref, acc_ref = refs
        else:
            a_ref, b_ref, o_ref = refs
            acc_ref = o_ref
        k = pl.program_id(2)
        if n_a == 1:
            part = lax.dot_general(a_ref[...].astype(BF16), b_ref[...].astype(BF16), dims, preferred_element_type=F32)
        else:
            part = sum(lax.dot_general(r[...], b_ref[:, s * D_MODEL:(s + 1) * D_MODEL], dims, preferred_element_type=F32)
                       for s, r in enumerate(a_refs))
        if has_add:
            @pl.when(k == 0)
            def _():
                acc_ref[...] = part + add_scale * add_ref[...]
        else:
            @pl.when(k == 0)
            def _():
                acc_ref[...] = part

        @pl.when(k > 0)
        def _():
            acc_ref[...] += part

        if narrow:
            @pl.when(k == nk - 1)
            def _():
                o_ref[...] = acc_ref[...].astype(out_dtype)

        if ln is not None:
            @pl.when(k == nk - 1)
            def _():
                dz, pg, pb = _ln_bwd_math(z_ref[...], o_ref[...], g_ref[...])
                o_ref[...] = dz
                dzc_ref[...] = (ln[2] * dz).astype(BF16)
                first = pl.program_id(0) == 0

                @pl.when(first)
                def _():
                    gg_ref[...] = pg
                    gb_ref[...] = pb

                @pl.when(jnp.logical_not(first))
                def _():
                    gg_ref[...] += pg
                    gb_ref[...] += pb

    in_specs = [*a_specs, b_spec]
    args = [a] * n_a + [b]
    aliases = {}
    if into is not None:
        assert mode == "nn" and split and not has_add and n_a == 1
        in_specs.append(pl.BlockSpec(memory_space=pl.ANY))
        args.append(into)
        aliases = {2: 0}
    if has_add:
        in_specs.append(pl.BlockSpec((tm, tn), lambda i, j, k: (i, j)))
        args.append(add)
    sem = ("parallel", "parallel", "arbitrary")
    if ln is not None:
        part8 = pl.BlockSpec((8, N), lambda i, j, k: (0, 0))
        in_specs += [pl.BlockSpec((tm, tn), lambda i, j, k: (i, j)), pl.BlockSpec((1, N), lambda i, j, k: (0, 0))]
        args += [ln[0], ln[1]]
        out_spec = [out_spec, pl.BlockSpec((tm, tn), lambda i, j, k: (i, j)), part8, part8]
        out_struct = [out_struct, jax.ShapeDtypeStruct((M, N), BF16), jax.ShapeDtypeStruct((8, N), F32),
                      jax.ShapeDtypeStruct((8, N), F32)]
        sem = ("arbitrary", "arbitrary", "arbitrary")
    if after is not None:
        assert into is None
        in_specs.insert(0, pl.BlockSpec(memory_space=pl.ANY))
        args.insert(0, after)
    return pl.pallas_call(
        body, name=name, grid=(M // tm, N // tn, nk),
        in_specs=in_specs, out_specs=out_spec, out_shape=out_struct, input_output_aliases=aliases,
        scratch_shapes=[pltpu.VMEM((tm, tn), F32)] if narrow else [],
        compiler_params=_cparams(sem),
    )(*args)


def _pick(n, cands):
    for c in cands:
        if n % c == 0:
            return c
    raise ValueError((n, cands))


def _mm_nn(a, b, out_dtype, name, split=False, into=None):
    M, K = _unlead(a)[2]
    N = _unlead(b)[2][1]
    return _matmul(a, b, "nn", out_dtype, _pick(M, (1024, 512, 256)), _pick(N, (1024, 512)), _pick(K, (1024, 512)), name,
                   split=split, into=into)


def _mm_nt(a, b, name, add=None, add_scale=1.0, split=False, ln=None, after=None):
    M = a.shape[1] if split else _unlead(a)[2][0]
    N, K = _unlead(b)[2]
    tn = _pick(N, (1024, 512))
    if tn == N and N * K * 2 <= WHOLE_WEIGHT_BYTES:
        tm, tk = _pick(M, (512, 256)), K
    else:
        tms = (512, 256) if ln is not None else (1024, 512, 256)
        tm, tk = _pick(M, tms), _pick(D_MODEL if split else K, (2816, 1024, 512))
    return _matmul(a, b, "nt", F32, tm, tn, tk, name, add=add, add_scale=add_scale, split=split, ln=ln, after=after)


def _mm_tn(a, b, name, split=False, out_dtype=F32):
    K, M = _unlead(a)[2]
    N = D_MODEL if split else _unlead(b)[2][1]
    return _matmul(a, b, "tn", out_dtype, _pick(M, (1024, 1408, 512)), _pick(N, (1408, 1024, 512)),
                   _pick(K, (2048, 1024, 512, 256)), name, split=split)


def _d_kv_w(y, dqkv, name):
    S = y.shape[0]
    tk = _pick(S, (1024, 512))
    nk = S // tk
    width = N_KV_B * HEAD_DIM
    r, c = np.arange(D_MODEL)[:, None], np.arange(width)[None, :]
    fold = jnp.asarray((r // (GROUP_B * HEAD_DIM) == c // HEAD_DIM) & (r % HEAD_DIM == c % HEAD_DIM), BF16)

    def body(y_ref, dk_ref, dv_ref, f_ref, o_ref, acc_ref):
        k = pl.program_id(0)
        summed = jnp.concatenate([jnp.dot(ref[...], f_ref[...], preferred_element_type=F32).astype(BF16)
                                  for ref in (dk_ref, dv_ref)], axis=1)
        part = lax.dot_general(summed, y_ref[...], (_DIMS["tn"], ((), ())), preferred_element_type=F32)

        @pl.when(k == 0)
        def _():
            acc_ref[...] = part

        @pl.when(k > 0)
        def _():
            acc_ref[...] += part

        @pl.when(k == nk - 1)
        def _():
            o_ref[...] = acc_ref[...].T.astype(BF16)

    return pl.pallas_call(
        body, name=name, grid=(nk,),
        in_specs=[pl.BlockSpec((tk, D_MODEL), lambda k: (k, 0)),
                  pl.BlockSpec((None, tk, D_MODEL), lambda k: (1, k, 0)),
                  pl.BlockSpec((None, tk, D_MODEL), lambda k: (2, k, 0)),
                  pl.BlockSpec((D_MODEL, width), lambda k: (0, 0))],
        out_specs=pl.BlockSpec((D_MODEL, 2 * width), lambda k: (0, 0)),
        out_shape=jax.ShapeDtypeStruct((D_MODEL, 2 * width), BF16),
        scratch_shapes=[pltpu.VMEM((2 * width, D_MODEL), F32)],
        compiler_params=_cparams(("arbitrary",)),
    )(y, dqkv, dqkv, fold)


def _ffn_in(x, w, name):
    S = x.shape[0]
    tm = _pick(S, (512, 256))
    w, wl, _ = _unlead(w)

    def body(x_ref, w_ref, t_ref, h_ref):
        acc = jnp.dot(x_ref[...].astype(BF16), w_ref[...], preferred_element_type=F32)
        g = acc[:, :HALF_FF]
        up = acc[:, HALF_FF:]
        sg = jax.nn.sigmoid(g)
        silu = g * sg
        t_ref[:, :HALF_FF] = (up * (sg * (1.0 + g * (1.0 - sg)))).astype(BF16)
        t_ref[:, HALF_FF:] = silu.astype(BF16)
        h_ref[...] = (silu * up).astype(BF16)

    return pl.pallas_call(
        body, name=name, grid=(2, S // tm),
        in_specs=[pl.BlockSpec((tm, D_MODEL), lambda j, i: (i, 0)),
                  _bspec((D_MODEL, D_FF), lambda j, i: (0, j), wl)],
        out_specs=[pl.BlockSpec((tm, D_FF), lambda j, i: (i, j)),
                   pl.BlockSpec((tm, HALF_FF), lambda j, i: (i, j))],
        out_shape=[jax.ShapeDtypeStruct((S, 2 * D_FF), BF16), jax.ShapeDtypeStruct((S, D_FF), BF16)],
        compiler_params=_cparams(("parallel", "parallel")),
    )(x, w)


def _ffn_bwd_h(dzc, w_out, u, name):
    S = dzc.shape[0]
    tm = _pick(S, (512, 256))
    w_out, wl, _ = _unlead(w_out)

    def body(dz_ref, w_ref, t_ref, du_ref):
        dh = lax.dot_general(dz_ref[...], w_ref[...], (((1,), (1,)), ((), ())), preferred_element_type=F32)
        du_ref[:, :HALF_FF] = (dh * t_ref[:, :HALF_FF].astype(F32)).astype(BF16)
        du_ref[:, HALF_FF:] = (dh * t_ref[:, HALF_FF:].astype(F32)).astype(BF16)

    return pl.pallas_call(
        body, name=name, grid=(2, S // tm),
        in_specs=[pl.BlockSpec((tm, D_MODEL), lambda j, i: (i, 0)),
                  _bspec((HALF_FF, D_MODEL), lambda j, i: (j, 0), wl),
                  pl.BlockSpec((tm, D_FF), lambda j, i: (i, j))],
        out_specs=pl.BlockSpec((tm, D_FF), lambda j, i: (i, j)),
        out_shape=jax.ShapeDtypeStruct((S, 2 * D_FF), BF16),
        compiler_params=_cparams(("parallel", "parallel")),
    )(dzc, w_out, u)


def _mm_ln(a, w, resid, gain, bias, c, name):
    S, K = a.shape
    tm = _pick(S, (512, 256))
    w, wl, _ = _unlead(w)

    def body(a_ref, w_ref, r_ref, g_ref, b_ref, y_ref, yb_ref, z_ref):
        z = ALPHA * r_ref[...] + c * jnp.dot(a_ref[...], w_ref[...], preferred_element_type=F32)
        mu = jnp.mean(z, axis=-1, keepdims=True)
        zc = z - mu
        var = jnp.mean(zc * zc, axis=-1, keepdims=True)
        y = zc * lax.rsqrt(var + LN_EPS) * g_ref[...] + b_ref[...]
        z_ref[...] = z
        y_ref[...] = y
        yb_ref[...] = y.astype(BF16)

    row = pl.BlockSpec((tm, D_MODEL), lambda i: (i, 0))
    vec = pl.BlockSpec((1, D_MODEL), lambda i: (0, 0))
    return pl.pallas_call(
        body, name=name, grid=(S // tm,),
        in_specs=[pl.BlockSpec((tm, K), lambda i: (i, 0)), _bspec((K, D_MODEL), lambda i: (0, 0), wl), row, vec, vec],
        out_specs=[row, row, row],
        out_shape=[jax.ShapeDtypeStruct((S, D_MODEL), F32), jax.ShapeDtypeStruct((S, D_MODEL), BF16),
                   jax.ShapeDtypeStruct((S, D_MODEL), F32)],
        compiler_params=_cparams(("parallel",)),
    )(a, w, resid, gain, bias)


def _loss_ln_bwd(y, t, z, gain, c, name):
    S = y.shape[0]
    tm = _pick(S, (512, 256))

    def body(y_ref, t_ref, z_ref, g_ref, dz_ref, dzc_ref, gg_ref, gb_ref, sq_ref):
        i = pl.program_id(0)
        e = y_ref[...] - t_ref[...]
        dz, pg, pb = _ln_bwd_math(z_ref[...], e * (1.0 / D_MODEL), g_ref[...])
        dz_ref[...] = dz
        dzc_ref[...] = (c * dz).astype(BF16)
        ps = jnp.sum((e * e).reshape(tm // 8, 8, D_MODEL), axis=0)

        @pl.when(i == 0)
        def _():
            gg_ref[...] = pg
            gb_ref[...] = pb
            sq_ref[...] = ps

        @pl.when(i > 0)
        def _():
            gg_ref[...] += pg
            gb_ref[...] += pb
            sq_ref[...] += ps

    row = pl.BlockSpec((tm, D_MODEL), lambda i: (i, 0))
    part = pl.BlockSpec((8, D_MODEL), lambda i: (0, 0))
    part_shape = jax.ShapeDtypeStruct((8, D_MODEL), F32)
    return pl.pallas_call(
        body, name=name, grid=(S // tm,),
        in_specs=[row, row, row, pl.BlockSpec((1, D_MODEL), lambda i: (0, 0))],
        out_specs=[row, row, part, part, part],
        out_shape=[jax.ShapeDtypeStruct((S, D_MODEL), F32), jax.ShapeDtypeStruct((S, D_MODEL), BF16),
                   part_shape, part_shape, part_shape],
        compiler_params=_cparams(("arbitrary",)),
    )(y, t, z, gain)


def _rows(start, d):
    if d == 1:
        return pl.ds(pl.multiple_of(start, BLOCK), BLOCK)
    return pl.ds(start, BLOCK, stride=d)


def _ld(ref, start, d):
    return ref[_rows(start, d), :]


def _ld3(ref, lead, start, d):
    return ref[lead, _rows(start, d), :]


def _st3(ref, lead, start, d, val):
    ref[lead, _rows(start, d), :] = val


def _acc3(ref, lead, start, d, val):
    ref[lead, _rows(start, d), :] = ref[lead, _rows(start, d), :] + val


def _band_consts(slope0, slope1, maxd, scale):
    row = lax.broadcasted_iota(jnp.int32, (2 * BLOCK, 2 * BLOCK), 0)
    kj = lax.broadcasted_iota(jnp.int32, (2 * BLOCK, 2 * BLOCK), 1)
    top = row < BLOCK
    dist = BLOCK + jnp.where(top, row, row - BLOCK) - kj
    slope = jnp.where(top, slope0, slope1)
    base = jnp.where((dist >= 0) & (dist <= maxd), -(slope * (dist.astype(F32) * scale)), NEG)
    return base, kj < BLOCK


def _stack_heads(x, lo):
    return jnp.concatenate([jnp.where(lo, x, 0.0), jnp.where(lo, 0.0, x)], axis=0)


def _unstack_heads(x2, lo):
    return jnp.where(lo, x2[:BLOCK], x2[BLOCK:])


def _scores(q2, k2, base, prev_keys, first):
    s = lax.dot_general(q2, k2, (((1,), (1,)), ((), ())), preferred_element_type=F32) * (HEAD_DIM ** -0.5) + base
    return jnp.where(jnp.logical_and(prev_keys, first), NEG, s)


def _softmax_weights(ls):
    mx = ls[0]
    for l in ls[1:]:
        mx = jnp.maximum(mx, l)
    es = [jnp.exp(l - mx) for l in ls]
    tot = es[0]
    for e in es[1:]:
        tot = tot + e
    inv = 1.0 / tot
    return [e * inv for e in es]


def _attn_fwd(qkv, slopes, sinks, patterns, name):
    S = qkv.shape[1]
    npat = len(patterns)
    has_sink = sinks is not None
    if not has_sink:
        sinks = jnp.zeros((N_HEADS,), F32)
    rows_c = 256

    def body(slopes_ref, sinks_ref, x_ref, mix_ref, o_ref, lse_ref, o_scr, lse_scr):
        p = pl.program_id(0)
        lo = lax.broadcasted_iota(jnp.int32, (BLOCK, SLAB), 1) < HEAD_DIM
        top1 = lax.broadcasted_iota(jnp.int32, (2 * BLOCK, 1), 0) < BLOCK
        sk2 = jnp.where(top1, sinks_ref[2 * p], sinks_ref[2 * p + 1])
        for pi, (d, maxd, scale) in enumerate(patterns):
            nb = S // d // BLOCK
            base, prev_keys = _band_consts(slopes_ref[2 * p], slopes_ref[2 * p + 1], maxd, scale)

            def blk(t, carry, pi=pi, d=d, nb=nb, base=base, prev_keys=prev_keys):
                r = t // nb
                n = t - r * nb
                start = r + (d * BLOCK) * n
                prev = jnp.where(n > 0, start - d * BLOCK, start)
                q2 = _stack_heads(_ld3(x_ref, 0, start, d), lo).astype(BF16)
                k2 = jnp.concatenate([_ld3(x_ref, 1, prev, d), _ld3(x_ref, 1, start, d)], axis=0).astype(BF16)
                v2 = jnp.concatenate([_ld3(x_ref, 2, prev, d), _ld3(x_ref, 2, start, d)], axis=0).astype(BF16)
                s = _scores(q2, k2, base, prev_keys, n == 0)
                m = jnp.max(s, axis=-1, keepdims=True)
                if has_sink:
                    m = jnp.maximum(m, sk2)
                e = jnp.exp(s - m)
                den = jnp.sum(e, axis=-1, keepdims=True)
                if has_sink:
                    den = den + jnp.exp(sk2 - m)
                o2 = jnp.dot((e / den).astype(BF16), v2, preferred_element_type=F32)
                _st3(o_scr, pi, start, d, _unstack_heads(o2, lo))
                _st3(lse_scr, pi, start, d, _unstack_heads(m + jnp.log(den), lo))
                return carry

            lax.fori_loop(0, d * nb, blk, 0, unroll=8)

        lane_c = lax.broadcasted_iota(jnp.int32, (rows_c, SLAB), 1)

        def comb(ci, carry):
            rows = pl.ds(pl.multiple_of(ci * rows_c, rows_c), rows_c)
            ls = [lse_scr[i, rows, :] for i in range(npat)]
            packed = jnp.zeros((rows_c, SLAB), F32)
            for i in range(npat):
                o_ref[i, rows, :] = o_scr[i, rows, :].astype(BF16)
                packed = jnp.where(lane_c % HEAD_DIM == i, ls[i], packed)
            lse_ref[rows, :] = packed
            if npat == 1:
                mix_ref[rows, :] = o_scr[0, rows, :].astype(BF16)
            else:
                ws = _softmax_weights(ls)
                acc = ws[0] * o_scr[0, rows, :]
                for i in range(1, npat):
                    acc = acc + ws[i] * o_scr[i, rows, :]
                mix_ref[rows, :] = acc.astype(BF16)
            return carry

        lax.fori_loop(0, S // rows_c, comb, 0, unroll=2)

    smem = pl.BlockSpec(memory_space=pltpu.SMEM)
    return pl.pallas_call(
        body, name=name, grid=(N_SLABS,),
        in_specs=[smem, smem, pl.BlockSpec((3, S, SLAB), lambda p: (0, 0, p))],
        out_specs=[pl.BlockSpec((S, SLAB), lambda p: (0, p)), pl.BlockSpec((npat, S, SLAB), lambda p: (0, 0, p)),
                   pl.BlockSpec((None, S, SLAB), lambda p: (p, 0, 0))],
        out_shape=[jax.ShapeDtypeStruct((S, D_MODEL), BF16), jax.ShapeDtypeStruct((npat, S, D_MODEL), BF16),
                   jax.ShapeDtypeStruct((N_SLABS, S, SLAB), F32)],
        scratch_shapes=[pltpu.VMEM((npat, S, SLAB), F32), pltpu.VMEM((npat, S, SLAB), F32)],
        compiler_params=_cparams(("arbitrary",)),
    )(slopes, sinks, qkv)


def _attn_bwd(qkv, dout, o, lse, slopes, sinks, patterns, name):
    S = qkv.shape[1]
    npat = len(patterns)
    has_sink = sinks is not None
    if not has_sink:
        sinks = jnp.zeros((N_HEADS,), F32)
    rows_c = 256

    def headsum(x, lo):
        same = (lax.broadcasted_iota(jnp.int32, (SLAB, SLAB), 0) < HEAD_DIM) == (lax.broadcasted_iota(jnp.int32, (SLAB, SLAB), 1) < HEAD_DIM)
        return jnp.dot(x, same.astype(F32), precision=lax.Precision.HIGH, preferred_element_type=F32)

    def body(slopes_ref, sinks_ref, x_ref, do_ref, o_ref, lsep_ref, dxo_ref, dsink_ref, dbar_ref, sacc_ref, lse_ref, dx_ref):
        p = pl.program_id(0)
        lo = lax.broadcasted_iota(jnp.int32, (BLOCK, SLAB), 1) < HEAD_DIM
        lo_c = lax.broadcasted_iota(jnp.int32, (rows_c, SLAB), 1) < HEAD_DIM
        top1 = lax.broadcasted_iota(jnp.int32, (2 * BLOCK, 1), 0) < BLOCK
        sk2 = jnp.where(top1, sinks_ref[2 * p], sinks_ref[2 * p + 1])

        def prep(ci, carry):
            rows = pl.ds(pl.multiple_of(ci * rows_c, rows_c), rows_c)
            dov = do_ref[rows, :]
            dx_ref[:, rows, :] = jnp.zeros((3, rows_c, SLAB), F32)
            packed = lsep_ref[rows, :]
            ls = [jnp.where(lo_c, packed[:, i:i + 1], packed[:, HEAD_DIM + i:HEAD_DIM + i + 1]) for i in range(npat)]
            for i in range(npat):
                lse_ref[i, rows, :] = ls[i]
            if npat == 1:
                dbar_ref[rows, :] = headsum(dov * o_ref[0, rows, :].astype(F32), lo_c)
            else:
                ws = _softmax_weights(ls)
                acc = ws[0] * headsum(dov * o_ref[0, rows, :].astype(F32), lo_c)
                for i in range(1, npat):
                    acc = acc + ws[i] * headsum(dov * o_ref[i, rows, :].astype(F32), lo_c)
                dbar_ref[rows, :] = acc
            return carry

        lax.fori_loop(0, S // rows_c, prep, 0, unroll=2)
        sacc_ref[...] = jnp.zeros((BLOCK, SLAB), F32)

        for pi, (d, maxd, scale) in enumerate(patterns):
            nb = S // d // BLOCK
            base, prev_keys = _band_consts(slopes_ref[2 * p], slopes_ref[2 * p + 1], maxd, scale)

            def blk(t, carry, pi=pi, d=d, nb=nb, base=base, prev_keys=prev_keys):
                r = t // nb
                n = t - r * nb
                start = r + (d * BLOCK) * n
                prev = jnp.where(n > 0, start - d * BLOCK, start)
                q2 = _stack_heads(_ld3(x_ref, 0, start, d), lo).astype(BF16)
                k2 = jnp.concatenate([_ld3(x_ref, 1, prev, d), _ld3(x_ref, 1, start, d)], axis=0).astype(BF16)
                v2 = jnp.concatenate([_ld3(x_ref, 2, prev, d), _ld3(x_ref, 2, start, d)], axis=0).astype(BF16)
                ls = [_ld3(lse_ref, i, start, d) for i in range(npat)]
                w = _softmax_weights(ls)[pi] if npat > 1 else 1.0
                do2 = _stack_heads(w * _ld(do_ref, start, d), lo).astype(BF16)
                dl = w * _ld(dbar_ref, start, d)
                lse2 = jnp.concatenate([ls[pi][:, :1], ls[pi][:, HEAD_DIM:HEAD_DIM + 1]], axis=0)
                dl2 = jnp.concatenate([dl[:, :1], dl[:, HEAD_DIM:HEAD_DIM + 1]], axis=0)
                s = _scores(q2, k2, base, prev_keys, n == 0)
                pr = jnp.exp(s - lse2)
                dp = lax.dot_general(do2, v2, (((1,), (1,)), ((), ())), preferred_element_type=F32)
                ds = (pr * (dp - dl2) * (HEAD_DIM ** -0.5)).astype(BF16)
                dq2 = jnp.dot(ds, k2, preferred_element_type=F32)
                dk2 = lax.dot_general(ds, q2, (((0,), (0,)), ((), ())), preferred_element_type=F32)
                dv2 = lax.dot_general(pr.astype(BF16), do2, (((0,), (0,)), ((), ())), preferred_element_type=F32)
                _acc3(dx_ref, 0, start, d, _unstack_heads(dq2, lo))
                _acc3(dx_ref, 1, prev, d, dk2[:BLOCK])
                _acc3(dx_ref, 1, start, d, dk2[BLOCK:])
                _acc3(dx_ref, 2, prev, d, dv2[:BLOCK])
                _acc3(dx_ref, 2, start, d, dv2[BLOCK:])
                if has_sink:
                    sacc_ref[...] += _unstack_heads(-jnp.exp(sk2 - lse2) * dl2, lo)
                return carry

            lax.fori_loop(0, d * nb, blk, 0, unroll=8)

        dsink_ref[...] = jnp.broadcast_to(jnp.sum(sacc_ref[...], axis=0, keepdims=True), (8, SLAB))

        def emit(ci, carry):
            rows = pl.ds(pl.multiple_of(ci * rows_c, rows_c), rows_c)
            dxo_ref[:, rows, :] = dx_ref[:, rows, :].astype(BF16)
            return carry

        lax.fori_loop(0, S // rows_c, emit, 0, unroll=2)

    smem = pl.BlockSpec(memory_space=pltpu.SMEM)
    return pl.pallas_call(
        body, name=name, grid=(N_SLABS,),
        in_specs=[smem, smem, pl.BlockSpec((3, S, SLAB), lambda p: (0, 0, p)), pl.BlockSpec((S, SLAB), lambda p: (0, p)),
                  pl.BlockSpec((npat, S, SLAB), lambda p: (0, 0, p)), pl.BlockSpec((None, S, SLAB), lambda p: (p, 0, 0))],
        out_specs=[pl.BlockSpec((3, S, SLAB), lambda p: (0, 0, p)), pl.BlockSpec((None, 8, SLAB), lambda p: (p, 0, 0))],
        out_shape=[jax.ShapeDtypeStruct((3, S, D_MODEL), BF16), jax.ShapeDtypeStruct((N_SLABS, 8, SLAB), F32)],
        scratch_shapes=[pltpu.VMEM((S, SLAB), F32), pltpu.VMEM((BLOCK, SLAB), F32), pltpu.VMEM((npat, S, SLAB), F32),
                        pltpu.VMEM((3, S, SLAB), F32)],
        compiler_params=_cparams(("arbitrary",)),
    )(slopes, sinks, qkv, dout, o, lse)


def _place():
    x, y, c = lax.axis_index("x"), lax.axis_index("y"), lax.axis_index("c")
    return x, y, c, 2 * x + y


def _other_chips(x, y):
    return [(1 - x, y), (x, 1 - y), (1 - x, 1 - y)]


HBM_SPEC = pl.BlockSpec(memory_space=pl.ANY)


def _slot(q):
    return 2 * (q % 2) + q // 2


BIG = ("ffn1_w_in", "ffn1_w_out", "ffn2_w_in", "ffn2_w_out", "a_w_qkv", "a_w_o", "kv_w", "b_w_q", "b_w_o")
QKV_SHARD = 3 * D_MODEL // N_CHIPS
ROW_SHARD = D_MODEL // N_CHIPS


FIRST_ITEMS = (("ffn1_w_in", 0), ("ffn1_w_out", 0))
LATER_ITEMS = {"mixer_a": (("a_w_qkv", None), ("a_w_o", None)),
               "ffn2_0": (("ffn2_w_in", 0), ("ffn2_w_out", 0), ("kv_w", None)),
               "layer1": (("ffn1_w_in", 1), ("ffn1_w_out", 1), ("b_w_q", None), ("b_w_o", None)),
               "ffn2_1": (("ffn2_w_in", 1), ("ffn2_w_out", 1))}
OUT_SHARD = D_FF // N_CHIPS


def _full_shape(name):
    if name.endswith("w_in"):
        return (D_MODEL, 2 * D_FF)
    if name.endswith("w_out"):
        return (D_FF, D_MODEL)
    if name == "a_w_qkv":
        return (D_MODEL, 3 * D_MODEL)
    if name == "kv_w":
        return (N_CHIPS, 2, ROW_SHARD // 2, 2 * N_KV_B * HEAD_DIM)
    return (N_CHIPS, 2, ROW_SHARD // 2, D_MODEL)


def _gather_src(item, ref, c):
    name, layer = item
    if name.endswith("w_in"):
        return ref.at[layer, pl.ds(c * (D_MODEL // 2), D_MODEL // 2)]
    if name.endswith("w_out"):
        return ref.at[layer, pl.ds(c * (OUT_SHARD // 2), OUT_SHARD // 2)]
    if name == "a_w_qkv":
        return ref.at[0, pl.ds(c * (D_MODEL // 2), D_MODEL // 2)]
    if name == "kv_w":
        return ref.at[pl.ds(c * (ROW_SHARD // 2), ROW_SHARD // 2)]
    return ref.at[0, pl.ds(c * (ROW_SHARD // 2), ROW_SHARD // 2)]


def _gather_dst(item, ref, q, c):
    name, _ = item
    if name.endswith("w_in"):
        return ref.at[pl.ds(c * (D_MODEL // 2), D_MODEL // 2), pl.ds(_slot(q) * HALF_FF, HALF_FF)]
    if name.endswith("w_out"):
        return ref.at[pl.ds(q * OUT_SHARD + c * (OUT_SHARD // 2), OUT_SHARD // 2)]
    if name == "a_w_qkv":
        return ref.at[pl.ds(c * (D_MODEL // 2), D_MODEL // 2), pl.ds(q * QKV_SHARD, QKV_SHARD)]
    return ref.at[q, c]


def _all_gather(items, shards, small):
    n = len(items)
    r = small.shape[0]
    per = 8

    def body(*refs):
        srcs, small_ref = refs[:n], refs[n]
        dsts, s_ref = refs[n + 1:2 * n + 1], refs[2 * n + 1]
        send_sems, recv_sems = refs[2 * n + 2:]
        x, y, c, myq = _place()
        sibling = (x, y, 1 - c)
        chips = _other_chips(x, y)

        def big(t, k, src, q, h, to):
            return pltpu.make_async_remote_copy(src_ref=src, dst_ref=_gather_dst(items[t], dsts[t], q, h),
                                                send_sem=send_sems.at[per * t + k], recv_sem=recv_sems.at[per * t + k],
                                                device_id=to, device_id_type=MESH)

        def tiny(k, q, to):
            return pltpu.make_async_remote_copy(src_ref=small_ref, dst_ref=s_ref.at[q], send_sem=send_sems.at[per * n + k],
                                                recv_sem=recv_sems.at[per * n + k], device_id=to, device_id_type=MESH)

        first = []
        for j, chip in enumerate(chips):
            if j < 2:
                first += [big(t, j, _gather_src(items[t], srcs[t], c), myq, c, (*chip, c)) for t in range(n)]
            first.append(tiny(j, myq, (*chip, c)))
        own = [big(t, 6 + h, _gather_src(items[t], srcs[t], h), myq, h, sibling) for t in range(n) for h in (0, 1)]
        own.append(tiny(3, myq, sibling))
        for cp in first + own:
            cp.start()
        relay_from = ((x + 1 - c) % 2, (y + c) % 2)
        relay_to = ((x + c) % 2, (y + 1 - c) % 2, c)
        q_relay = 2 * relay_from[0] + relay_from[1]
        passed = []
        for t in range(n):
            src = _gather_src(items[t], srcs[t], c)
            for j, (cx, cy) in enumerate(chips[:2]):
                q = 2 * cx + cy
                big(t, j, src, q, c, sibling).wait_recv()
                fwd = big(t, 3 + j, _gather_dst(items[t], dsts[t], q, c), q, c, sibling)
                fwd.start()
                passed.append(fwd)
            relay = big(t, 2, _gather_dst(items[t], dsts[t], q_relay, c), q_relay, c, relay_to)
            relay.start()
            passed.append(relay)
        q_diag = 2 * chips[2][0] + chips[2][1]
        for t in range(n):
            big(t, 2, _gather_src(items[t], srcs[t], c), q_diag, c, sibling).wait_recv()
            fwd = big(t, 5, _gather_dst(items[t], dsts[t], q_diag, c), q_diag, c, sibling)
            fwd.start()
            passed.append(fwd)
        for j, (cx, cy) in enumerate(chips):
            q = 2 * cx + cy
            for t in range(n):
                big(t, 3 + j, _gather_src(items[t], srcs[t], c), q, 1 - c, sibling).wait_recv()
            tiny(j, q, sibling).wait_recv()
        for cp in own:
            cp.wait_recv()
        for cp in first + passed + own:
            cp.wait_send()

    outs = pl.pallas_call(
        body, name="all_gather_layer0",
        in_specs=[HBM_SPEC] * (n + 1), out_specs=[HBM_SPEC] * (n + 1),
        out_shape=[jax.ShapeDtypeStruct(_full_shape(name), BF16) for name, _ in items]
        + [jax.ShapeDtypeStruct((N_CHIPS, r, 128), F32)],
        scratch_shapes=[pltpu.SemaphoreType.DMA((per * n + 4,)), pltpu.SemaphoreType.DMA((per * n + 4,))],
    )(*[shards[item] for item in items], small)
    return list(outs[:n]), outs[n]


SEM_SPEC = pl.BlockSpec(memory_space=pltpu.SEMAPHORE)
DATAFLOW = pltpu.SideEffectType.DATAFLOW_SIDE_EFFECTING
PER_ITEM = 8


def _split_start(name, copies, n_sems, sources, land_shapes, after):
    n, m = len(sources), len(land_shapes)

    def body(*refs):
        srcs, lands = refs[:n], refs[n:n + m]
        send_sems, recv_sems = refs[n + m + 1], refs[n + m + 2]
        token = refs[-1]
        for src, dst_there, _, s, peer in copies(srcs, lands):
            pltpu.make_async_remote_copy(src_ref=src, dst_ref=dst_there, send_sem=send_sems.at[s], recv_sem=recv_sems.at[s],
                                         device_id=peer, device_id_type=MESH).start()
        token[...] = jnp.zeros_like(token)

    src_arrays = [pltpu.with_memory_space_constraint(a, pltpu.HBM) for a in sources]
    land_arrays = [pltpu.with_memory_space_constraint(lax.empty(s.shape, s.dtype), pltpu.HBM) for s in land_shapes]
    hbm = pl.BlockSpec(memory_space=pltpu.HBM)
    outs = pl.pallas_call(
        body, name=name,
        in_specs=[hbm] * (n + m) + [HBM_SPEC],
        out_specs=[SEM_SPEC, SEM_SPEC] + [hbm] * (n + m) + [pl.BlockSpec(memory_space=pltpu.VMEM)],
        out_shape=[pltpu.SemaphoreType.DMA((n_sems,)), pltpu.SemaphoreType.DMA((n_sems,))]
        + [pltpu.HBM(a.shape, a.dtype) for a in src_arrays + land_arrays] + [jax.ShapeDtypeStruct((8, 128), F32)],
        input_output_aliases={i: 2 + i for i in range(n + m)},
        compiler_params=pltpu.CompilerParams(has_side_effects=DATAFLOW),
    )(*src_arrays, *land_arrays, after)
    return (outs[0], outs[1], list(outs[2:2 + n]), list(outs[2 + n:2 + n + m])), outs[-1]


def _split_wait(name, copies, state, after):
    send_sems, recv_sems, srcs_thru, lands_thru = state
    n, m = len(srcs_thru), len(lands_thru)
    after = list(after) if isinstance(after, (list, tuple)) else [after]

    def body(*refs):
        srcs, lands = refs[:n], refs[n:n + m]
        send_sems, recv_sems = refs[n + m], refs[n + m + 1]
        for src, _, dst_here, s, peer in copies(srcs, lands):
            cp = pltpu.make_async_remote_copy(src_ref=src, dst_ref=dst_here, send_sem=send_sems.at[s], recv_sem=recv_sems.at[s],
                                              device_id=peer, device_id_type=MESH)
            cp.wait_send()
            cp.wait_recv()

    hbm = pl.BlockSpec(memory_space=pltpu.HBM)
    outs = pl.pallas_call(
        body, name=name,
        in_specs=[hbm] * (n + m) + [SEM_SPEC, SEM_SPEC] + [HBM_SPEC] * len(after),
        out_specs=[hbm] * (n + m),
        out_shape=[pltpu.HBM(a.shape, a.dtype) for a in srcs_thru + lands_thru],
        input_output_aliases={i: i for i in range(n + m)},
        compiler_params=pltpu.CompilerParams(has_side_effects=DATAFLOW),
    )(*srcs_thru, *lands_thru, send_sems, recv_sems, *after)
    return list(outs[:n]), list(outs[n:])


def _gather_copies(items):
    def copies(srcs, lands):
        x, y, c, myq = _place()
        out = []
        for t, item in enumerate(items):
            for h in (0, 1):
                src = _gather_src(item, srcs[t], h)
                for j, (cx, cy) in enumerate(_other_chips(x, y)):
                    out.append((src, _gather_dst(item, lands[t], myq, h), _gather_dst(item, lands[t], 2 * cx + cy, h),
                                PER_ITEM * t + 2 * j + h, (cx, cy, c)))
                out.append((src, _gather_dst(item, lands[t], myq, h), _gather_dst(item, lands[t], myq, h),
                            PER_ITEM * t + 6 + h, (x, y, 1 - c)))
        return out
    return copies


def _gather_start(tag, shards, after):
    items = LATER_ITEMS[tag]
    lands = [jax.ShapeDtypeStruct(_full_shape(name), BF16) for name, _ in items]
    return _split_start("gather_%s_start" % tag, _gather_copies(items), PER_ITEM * len(items),
                        [shards[item] for item in items], lands, after)


def _gather_wait(tag, state, after):
    return _split_wait("gather_%s_wait" % tag, _gather_copies(LATER_ITEMS[tag]), state, after)


def _small_all_reduce(v, after=()):
    r = v.shape[0]

    def body(v_ref, *rest):
        o_ref, buf_ref, send_sems, recv_sems = rest[len(after):]
        x, y, c, _ = _place()
        me = 4 * x + 2 * y + c
        buf_ref[me] = v_ref[...]
        copies = []
        for k in range(1, 8):
            fx, fy, fc = (k >> 2) & 1, (k >> 1) & 1, k & 1
            to = (x ^ fx, y ^ fy, c ^ fc)
            cp = pltpu.make_async_remote_copy(src_ref=v_ref, dst_ref=buf_ref.at[me], send_sem=send_sems.at[k - 1],
                                              recv_sem=recv_sems.at[k - 1], device_id=to, device_id_type=MESH)
            cp.start()
            copies.append(cp)
        for k in range(1, 8):
            fx, fy, fc = (k >> 2) & 1, (k >> 1) & 1, k & 1
            src_dev = 4 * (x ^ fx) + 2 * (y ^ fy) + (c ^ fc)
            pltpu.make_async_remote_copy(src_ref=v_ref, dst_ref=buf_ref.at[src_dev], send_sem=send_sems.at[k - 1],
                                         recv_sem=recv_sems.at[k - 1], device_id=(x, y, c), device_id_type=MESH).wait_recv()
        for cp in copies:
            cp.wait_send()
        tot = buf_ref[0]
        for i in range(1, 8):
            tot = tot + buf_ref[i]
        o_ref[...] = tot

    vm = pl.BlockSpec(memory_space=pltpu.VMEM)
    return pl.pallas_call(
        body, name="small_all_reduce", in_specs=[vm] + [HBM_SPEC] * len(after), out_specs=vm,
        out_shape=jax.ShapeDtypeStruct((r, 128), F32),
        scratch_shapes=[pltpu.VMEM((8, r, 128), F32), pltpu.SemaphoreType.DMA((7,)), pltpu.SemaphoreType.DMA((7,))],
    )(v, *after)


def _grad_view(kind, g):
    if kind == "col":
        return g.reshape(2, g.shape[0] // 2, g.shape[1])
    return g.reshape(N_CHIPS, 2, g.shape[0] // (2 * N_CHIPS), g.shape[1])


def _half_of(kind, ref, h):
    return ref.at[h] if kind == "col" else ref.at[:, h]


def _half_shape(kind, view_shape):
    return view_shape[1:] if kind == "col" else (view_shape[0],) + view_shape[2:]


def _piece_of(kind, width, colblock, ref, q):
    if kind == "col":
        return ref.at[:, pl.ds(colblock(q) * width, width)]
    return ref.at[q]


def _piece_shape(kind, width, half_shape):
    return (half_shape[0], width) if kind == "col" else half_shape[1:]


def _pair_exchange(views, kinds, name):
    n = len(views)

    def body(*refs):
        ins, outs = refs[:n], refs[n:2 * n]
        send_sems, recv_sems = refs[2 * n:]
        x, y, c, _ = _place()
        cps = []
        for t in range(n):
            cp = pltpu.make_async_remote_copy(src_ref=_half_of(kinds[t], ins[t], 1 - c), dst_ref=outs[t],
                                              send_sem=send_sems.at[t], recv_sem=recv_sems.at[t],
                                              device_id=(x, y, 1 - c), device_id_type=MESH)
            cp.start()
            cps.append(cp)
        for cp in cps:
            cp.wait()

    return pl.pallas_call(
        body, name=name, in_specs=[HBM_SPEC] * n, out_specs=[HBM_SPEC] * n,
        out_shape=[jax.ShapeDtypeStruct(_half_shape(k, v.shape), v.dtype) for k, v in zip(kinds, views)],
        scratch_shapes=[pltpu.SemaphoreType.DMA((n,)), pltpu.SemaphoreType.DMA((n,))],
    )(*views)


def _pair_sum(kind, view, recv, c, name):
    hs = recv.shape
    N = hs[-1]
    rows = hs[-2]
    tr = _pick(rows, (512, 352, 128))
    tn = _pick(N, (1408, 1024, 512))

    def body(c_ref, p_ref, r_ref, s_ref):
        s_ref[...] = (p_ref[...] + r_ref[...]).astype(BF16)

    if kind == "col":
        grid = (rows // tr, N // tn)
        mine = pl.BlockSpec((None, tr, tn), lambda i, j, c_ref: (c_ref[0], i, j))
        blk = pl.BlockSpec((tr, tn), lambda i, j, c_ref: (i, j))
        sem = ("parallel", "parallel")
    else:
        grid = (N_CHIPS, rows // tr, N // tn)
        mine = pl.BlockSpec((None, None, tr, tn), lambda q, i, j, c_ref: (q, c_ref[0], i, j))
        blk = pl.BlockSpec((None, tr, tn), lambda q, i, j, c_ref: (q, i, j))
        sem = ("parallel", "parallel", "parallel")
    return pl.pallas_call(
        body, name=name,
        grid_spec=pltpu.PrefetchScalarGridSpec(num_scalar_prefetch=1, grid=grid, in_specs=[mine, blk], out_specs=blk),
        out_shape=jax.ShapeDtypeStruct(hs, BF16),
        compiler_params=_cparams(sem),
    )(c.reshape(1).astype(jnp.int32), view, recv)


def _chip_copies(kinds, widths, colblocks):
    def copies(srcs, lands):
        x, y, c, _ = _place()
        out = []
        for j, (cx, cy) in enumerate(_other_chips(x, y)):
            for t in range(len(kinds)):
                out.append((_piece_of(kinds[t], widths[t], colblocks[t], srcs[t], 2 * cx + cy), lands[t].at[j],
                            lands[t].at[j], 3 * t + j, (cx, cy, c)))
        return out
    return copies


def _chip_land_shapes(sums, kinds, widths):
    return [jax.ShapeDtypeStruct((3,) + _piece_shape(k, w, s.shape), BF16) for k, w, s in zip(kinds, widths, sums)]


def _chip_exchange(sums, kinds, widths, colblocks, name):
    n = len(sums)
    copies = _chip_copies(kinds, widths, colblocks)

    def body(*refs):
        send_sems, recv_sems = refs[2 * n:]
        cps = [pltpu.make_async_remote_copy(src_ref=src, dst_ref=dst, send_sem=send_sems.at[s], recv_sem=recv_sems.at[s],
                                            device_id=peer, device_id_type=MESH)
               for src, dst, _, s, peer in copies(refs[:n], refs[n:2 * n])]
        for cp in cps:
            cp.start()
        for cp in cps:
            cp.wait()

    return pl.pallas_call(
        body, name=name, in_specs=[HBM_SPEC] * n, out_specs=[HBM_SPEC] * n,
        out_shape=_chip_land_shapes(sums, kinds, widths),
        scratch_shapes=[pltpu.SemaphoreType.DMA((3 * n,)), pltpu.SemaphoreType.DMA((3 * n,))],
    )(*sums)


N_DIRECT = 7


def _direct_piece(kind, width, colblock, view_ref, q, h):
    if kind == "col":
        return view_ref.at[h, :, pl.ds(colblock(q) * width, width)]
    return view_ref.at[q, h]


def _direct_copies(kinds, widths, colblocks):
    def copies(srcs, lands):
        x, y, c, myq = _place()
        out = []
        for t in range(len(kinds)):
            def piece(q, h, t=t):
                return _direct_piece(kinds[t], widths[t], colblocks[t], srcs[t], q, h)
            for j, (cx, cy) in enumerate(_other_chips(x, y)):
                for h in (0, 1):
                    out.append((piece(2 * cx + cy, h), lands[t].at[2 * j + c], lands[t].at[2 * j + h],
                                10 * t + 3 * j + c + h, (cx, cy, h)))
            out.append((piece(myq, 1 - c), lands[t].at[6], lands[t].at[6], 10 * t + 9, (x, y, 1 - c)))
        return out
    return copies


def _chip_sum(kind, own_src, recv, block_idx, c, shard_shape, layer, into, name, direct=False):
    n_recv, rows, N = recv.shape
    tr = _pick(rows, (512, 352, 128))
    tn = _pick(N, (1408, 1024, 768, 512))
    ni, nj = rows // tr, N // tn

    def body(q_ref, s_ref, r_ref, *rest):
        o_ref = rest[-1]
        tot = s_ref[...].astype(F32)
        for k in range(n_recv):
            tot = tot + r_ref[k].astype(F32)
        o_ref[...] = tot

    if direct and kind == "col":
        own = pl.BlockSpec((None, tr, tn), lambda i, j, q_ref: (q_ref[1], i, q_ref[0] * nj + j))
    elif direct:
        own = pl.BlockSpec((None, None, tr, tn), lambda i, j, q_ref: (q_ref[0], q_ref[1], i, j))
    elif kind == "col":
        own = pl.BlockSpec((tr, tn), lambda i, j, q_ref: (i, q_ref[0] * nj + j))
    else:
        own = pl.BlockSpec((None, tr, tn), lambda i, j, q_ref: (q_ref[0], i, j))
    if len(shard_shape) == 3:
        lead = 0 if layer is None else layer
        out_spec = pl.BlockSpec((None, tr, tn), lambda i, j, q_ref: (lead, q_ref[1] * ni + i, j))
    else:
        out_spec = pl.BlockSpec((tr, tn), lambda i, j, q_ref: (q_ref[1] * ni + i, j))
    in_specs = [own, pl.BlockSpec((n_recv, tr, tn), lambda i, j, q_ref: (0, i, j))]
    s = own_src
    args = [jnp.stack([block_idx, c]).astype(jnp.int32), s, recv]
    aliases = {}
    if into is not None:
        in_specs.append(HBM_SPEC)
        args.append(into)
        aliases = {3: 0}
    return pl.pallas_call(
        body, name=name,
        grid_spec=pltpu.PrefetchScalarGridSpec(num_scalar_prefetch=1, grid=(ni, nj), in_specs=in_specs, out_specs=out_spec),
        out_shape=jax.ShapeDtypeStruct(shard_shape, F32), input_output_aliases=aliases,
        compiler_params=_cparams(("parallel", "parallel")),
    )(*args)


def _half_window(ref, h):
    rows = ref.shape[-2] // 2
    if ref.ndim == 3:
        return ref.at[:, pl.ds(h * rows, rows)]
    return ref.at[pl.ds(h * rows, rows)]


def _share_halves(grads, name):
    n = len(grads)

    def body(*refs):
        outs = refs[n:2 * n]
        send_sems, recv_sems = refs[2 * n:]
        x, y, c, _ = _place()
        cps = []
        for t in range(n):
            cp = pltpu.make_async_remote_copy(src_ref=_half_window(outs[t], c), dst_ref=_half_window(outs[t], c),
                                              send_sem=send_sems.at[t], recv_sem=recv_sems.at[t],
                                              device_id=(x, y, 1 - c), device_id_type=MESH)
            cp.start()
            cps.append(cp)
        for t in range(n):
            cps[t].wait_send()
            pltpu.make_async_remote_copy(src_ref=_half_window(outs[t], c), dst_ref=_half_window(outs[t], 1 - c),
                                         send_sem=send_sems.at[t], recv_sem=recv_sems.at[t],
                                         device_id=(x, y, 1 - c), device_id_type=MESH).wait_recv()

    return pl.pallas_call(
        body, name=name, in_specs=[HBM_SPEC] * n, out_specs=[HBM_SPEC] * n,
        out_shape=[jax.ShapeDtypeStruct(g.shape, F32) for g in grads],
        input_output_aliases={t: t for t in range(n)},
        scratch_shapes=[pltpu.SemaphoreType.DMA((n,)), pltpu.SemaphoreType.DMA((n,))],
    )(*grads)


def _adamw(w, g, m, v, name):
    R, W = w.shape
    tr = _pick(R, (512, 352, 256, 32))

    def body(w_ref, g_ref, m_ref, v_ref, d_ref, nm_ref, nv_ref, go_ref):
        gv = g_ref[...]
        go_ref[...] = gv
        nm = ADAM_B1 * m_ref[...] + (1.0 - ADAM_B1) * gv
        nv = ADAM_B2 * v_ref[...] + (1.0 - ADAM_B2) * (gv * gv)
        m_hat = nm / (1.0 - ADAM_B1 ** ADAM_STEP)
        v_hat = nv / (1.0 - ADAM_B2 ** ADAM_STEP)
        d_ref[...] = -ADAM_LR * (m_hat / (jnp.sqrt(v_hat) + ADAM_EPS) + ADAM_WD * w_ref[...])
        nm_ref[...] = nm
        nv_ref[...] = nv

    blk = pl.BlockSpec((tr, W), lambda i: (i, 0))
    shp = jax.ShapeDtypeStruct((R, W), F32)
    return pl.pallas_call(
        body, name=name, grid=(R // tr,), in_specs=[blk] * 4, out_specs=[blk] * 4, out_shape=[shp] * 4,
        compiler_params=_cparams(("parallel",)),
    )(w, g, m, v)


SMALL_ROWS = 32


def _pack_small(ln_g, ln_b, sinks):
    rows = jnp.concatenate([ln_g.reshape(-1, 128), ln_b.reshape(-1, 128),
                            jnp.pad(sinks.reshape(1, -1), ((0, 0), (0, 128 - sinks.size)))], axis=0)
    return jnp.pad(rows, ((0, SMALL_ROWS - rows.shape[0]), (0, 0)))


def _unpack_small(s, ln_shape, sink_shape):
    n = ln_shape[0] * ln_shape[1] * ln_shape[2] // 128
    return s[:n].reshape(ln_shape), s[n:2 * n].reshape(ln_shape), s[2 * n, :sink_shape[1]].reshape(sink_shape)


def _ffn_fwd(xin, w_in, w_out, gain, bias, tag):
    u, h = _ffn_in(xin, w_in, "ffn_in_" + tag)
    y, yb, z = _mm_ln(h, w_out, xin, gain, bias, 0.5, "ffn_out_ln_" + tag)
    return y, yb, dict(u=u, h=h, z=z, xin=xin)


def _ffn_bwd(dz, dzc, saved, w_in, w_out, xin_b, tag, dw_dtype=F32, ln=None, ready=None):
    du = _ffn_bwd_h(dzc, w_out, saved["u"], "ffn_bwd_h_" + tag)
    d_w_out = _mm_tn(saved["h"], dzc, "ffn_dwout_" + tag, out_dtype=dw_dtype)
    d_w_in = _mm_tn(xin_b, du, "ffn_dwin_" + tag, out_dtype=dw_dtype)
    after = None if ready is None else ready(d_w_in, d_w_out)
    dx = _mm_nt(du, w_in, "ffn_dx_" + tag, add=dz, add_scale=ALPHA, ln=ln, after=after)
    return dx, d_w_in, d_w_out


def kernel(x, ffn1_w_in, ffn1_w_out, ffn2_w_in, ffn2_w_out, ln_g, ln_b, a_w_qkv, a_w_o, kv_w, b_w_q, b_sinks, b_w_o, loss_target, m_ffn1_w_in, m_ffn1_w_out, m_ffn2_w_in, m_ffn2_w_out, m_ln_g, m_ln_b, m_a_w_qkv, m_a_w_o, m_kv_w, m_b_w_q, m_b_sinks, m_b_w_o, v_ffn1_w_in, v_ffn1_w_out, v_ffn2_w_in, v_ffn2_w_out, v_ln_g, v_ln_b, v_a_w_qkv, v_a_w_o, v_kv_w, v_b_w_q, v_b_sinks, v_b_w_o):
    ws = dict(ffn1_w_in=ffn1_w_in, ffn1_w_out=ffn1_w_out, ffn2_w_in=ffn2_w_in, ffn2_w_out=ffn2_w_out, a_w_qkv=a_w_qkv,
              a_w_o=a_w_o, kv_w=kv_w, b_w_q=b_w_q, b_w_o=b_w_o)
    ms = dict(ffn1_w_in=m_ffn1_w_in, ffn1_w_out=m_ffn1_w_out, ffn2_w_in=m_ffn2_w_in, ffn2_w_out=m_ffn2_w_out,
              a_w_qkv=m_a_w_qkv, a_w_o=m_a_w_o, kv_w=m_kv_w, b_w_q=m_b_w_q, b_w_o=m_b_w_o)
    vs = dict(ffn1_w_in=v_ffn1_w_in, ffn1_w_out=v_ffn1_w_out, ffn2_w_in=v_ffn2_w_in, ffn2_w_out=v_ffn2_w_out,
              a_w_qkv=v_a_w_qkv, a_w_o=v_a_w_o, kv_w=v_kv_w, b_w_q=v_b_w_q, b_w_o=v_b_w_o)
    _, _, c_idx, myq = _place()
    xs = x[0]
    target = loss_target[0]

    later = tuple(item for items in LATER_ITEMS.values() for item in items)
    shards = {(n, l): ws[n].astype(BF16) for n, l in FIRST_ITEMS + later}

    def as_weights(items, arrays):
        return {n: (a.reshape(D_MODEL, a.shape[-1]) if a.ndim == 4 else a) for (n, _), a in zip(items, arrays)}

    first, small = _all_gather(FIRST_ITEMS, shards, _pack_small(ln_g, ln_b, b_sinks))
    states = {}
    states["mixer_a"], token = _gather_start("mixer_a", shards, small)
    states["ffn2_0"], token = _gather_start("ffn2_0", shards, token)

    def later_weights(tag, after):
        if isinstance(tag, tuple):
            states[tag[1]], tok = _gather_start(tag[1], shards, after)
            return tok[0, 0]
        handed_on, full = _gather_wait(tag, states[tag], after)
        if tag == "ffn2_0":
            shards.update(zip(LATER_ITEMS["ffn2_1"], handed_on))
        return as_weights(LATER_ITEMS[tag], full)

    n_ln = ln_g.size // 128
    lg = jnp.concatenate([small[q, :n_ln].reshape(DEPTH, 3, 1, -1) for q in range(N_CHIPS)], axis=-1)
    lb = jnp.concatenate([small[q, n_ln:2 * n_ln].reshape(DEPTH, 3, 1, -1) for q in range(N_CHIPS)], axis=-1)
    lg = lg + token[0, 0]
    reducer = _GradReducer(c_idx, myq, {n: ws[n].shape for n in BIG})
    sq, grad_x, _, gg, gb, dsink_part = _local_step(xs, target, as_weights(FIRST_ITEMS, first), later_weights,
                                                    lg, lb, b_sinks.reshape(N_HEADS), reducer.begin)

    loss_row = jnp.pad(jnp.sum(sq).reshape(1, 1), ((0, 0), (0, 127)))
    dsinks = jnp.pad(dsink_part[:, 0, :].reshape(N_SLABS, 2, HEAD_DIM)[:, :, 0].reshape(1, N_HEADS), ((0, 0), (0, 128 - N_HEADS)))
    gg_full = jnp.stack([jnp.stack([jnp.sum(gg[i][j], axis=0) for j in range(3)]) for i in range(DEPTH)])
    gb_full = jnp.stack([jnp.stack([jnp.sum(gb[i][j], axis=0) for j in range(3)]) for i in range(DEPTH)])
    small_in = jnp.concatenate([loss_row, dsinks, gg_full.reshape(-1, 128), gb_full.reshape(-1, 128)], axis=0)
    small_in = jnp.pad(small_in, ((0, (-small_in.shape[0]) % 8), (0, 0)))
    def reduce_small(after):
        small_sum = _small_all_reduce(small_in, after)
        loss = small_sum[0, 0] * (0.5 / D_MODEL)
        grad_sinks = small_sum[1, :N_HEADS].reshape(b_sinks.shape)
        n_full = DEPTH * 3 * D_MODEL // 128
        cols = D_MODEL // N_CHIPS
        grad_ln_g = lax.dynamic_slice_in_dim(small_sum[2:2 + n_full].reshape(DEPTH, 3, D_MODEL), myq * cols, cols, axis=2)
        grad_ln_b = lax.dynamic_slice_in_dim(small_sum[2 + n_full:2 + 2 * n_full].reshape(DEPTH, 3, D_MODEL), myq * cols, cols, axis=2)
        return loss, grad_ln_g, grad_ln_b, grad_sinks

    return _update(reducer, grad_x, reduce_small, ws, ms, vs,
                   (ln_g, ln_b, b_sinks), (m_ln_g, m_ln_b, m_b_sinks), (v_ln_g, v_ln_b, v_b_sinks))


def _local_step(xs, target, W, later_weights, lg, lb, sinks, grads_ready=None):
    if grads_ready is None:
        grads_ready = lambda tag, grads, overlap: 0.0
    S = xs.shape[0]
    slopes = jnp.asarray(_alibi_slopes(N_HEADS))
    in1, out1 = [W["ffn1_w_in"]], [W["ffn1_w_out"]]

    y1, y1b, s1 = _ffn_fwd(xs, in1[0], out1[0], lg[0, 0], lb[0, 0], "a1")
    W = dict(W, **later_weights("mixer_a", y1b))
    lg = lg + later_weights(("start", "layer1"), W["a_w_o"])
    qkv_a = _mm_nn(y1b, W["a_w_qkv"], F32, "qkv_a", split=True)
    mix_a, o_a, lse_a = _attn_fwd(qkv_a, slopes, None, PATTERNS_A, "attn_a_fwd")
    y2, y2b, z2 = _mm_ln(mix_a, W["a_w_o"], y1, lg[0, 1], lb[0, 1], 1.0, "attn_a_out_ln")
    W = dict(W, **later_weights("ffn2_0", y2b))
    in2, out2 = [W["ffn2_w_in"]], [W["ffn2_w_out"]]
    lg = lg + later_weights(("start", "ffn2_1"), in2[0])
    y3, y3b, s3 = _ffn_fwd(y2, in2[0], out2[0], lg[0, 2], lb[0, 2], "a2")
    kv_w_rep = jnp.broadcast_to(W["kv_w"].reshape(D_MODEL, 2, N_KV_B, 1, HEAD_DIM),
                                (D_MODEL, 2, N_KV_B, GROUP_B, HEAD_DIM)).reshape(D_MODEL, 2 * D_MODEL)
    kv_rep = _mm_nn(y3b, kv_w_rep, F32, "kv_proj", split=(1, 2))
    W = dict(W, **later_weights("layer1", kv_rep))
    in1, out1 = in1 + [W["ffn1_w_in"]], out1 + [W["ffn1_w_out"]]
    y4, y4b, s4 = _ffn_fwd(y3, in1[1], out1[1], lg[1, 0], lb[1, 0], "b1")
    qkv_b = _mm_nn(y4b, W["b_w_q"], F32, "q_b", split=(0, 1), into=kv_rep)
    mix_b, o_b, lse_b = _attn_fwd(qkv_b, slopes, sinks, PATTERNS_B, "attn_b_fwd")
    y5, y5b, z5 = _mm_ln(mix_b, W["b_w_o"], y4, lg[1, 1], lb[1, 1], 1.0, "attn_b_out_ln")
    last = later_weights("ffn2_1", y5b)
    in2, out2 = in2 + [last["ffn2_w_in"]], out2 + [last["ffn2_w_out"]]
    y6, _, s6 = _ffn_fwd(y5, in2[1], out2[1], lg[1, 2], lb[1, 2], "b2")

    gr = {n: None for n in BIG}
    gg = [[None] * 3 for _ in range(DEPTH)]
    gb = [[None] * 3 for _ in range(DEPTH)]
    dz6, dz6c, gg[1][2], gb[1][2], sq = _loss_ln_bwd(y6, target, s6["z"], lg[1, 2], 0.5, "loss_ln_bwd")

    (dz5, dz5b, gg[1][1], gb[1][1]), d_in2_b, d_out2_b = _ffn_bwd(dz6, dz6c, s6, in2[1], out2[1], y5b, "b2", BF16,
                                                                  ln=(z5, lg[1, 1], 1.0))
    gr["b_w_o"] = _mm_tn(mix_b, dz5b, "d_b_w_o", out_dtype=BF16)
    dmix_b = _mm_nt(dz5b, W["b_w_o"], "d_mix_b")
    dqkv_b, dsink_part = _attn_bwd(qkv_b, dmix_b, o_b, lse_b, slopes, sinks, PATTERNS_B, "attn_b_bwd")
    dq_b = (dqkv_b, 0)
    gr["b_w_q"] = _mm_tn(y4b, dq_b, "d_b_w_q", out_dtype=BF16)
    dz4, dz4c, gg[1][0], gb[1][0] = _mm_nt(dq_b, W["b_w_q"], "d_y4", add=dz5, add_scale=ALPHA, ln=(s4["z"], lg[1, 0], 0.5))
    dy3, d_in1_b, d_out1_b = _ffn_bwd(dz4, dz4c, s4, in1[1], out1[1], y3b, "b1", BF16)
    gr["kv_w"] = _d_kv_w(y3b, dqkv_b, "d_kv_w")
    tok = grads_ready("l1", {("ffn2_w_in", 1): d_in2_b, ("ffn2_w_out", 1): d_out2_b, ("b_w_o", None): gr["b_w_o"],
                             ("b_w_q", None): gr["b_w_q"], ("ffn1_w_in", 1): d_in1_b, ("ffn1_w_out", 1): d_out1_b,
                             ("kv_w", None): gr["kv_w"]}, True)
    lg0 = lg[0] + tok
    dz3, dz3c, gg[0][2], gb[0][2] = _mm_nt(dqkv_b, kv_w_rep, "d_y3_kv", add=dy3, add_scale=1.0, split=(1, 2),
                                           ln=(s3["z"], lg0[2], 0.5))

    (dz2, dz2b, gg[0][1], gb[0][1]), d_in2_a, d_out2_a = _ffn_bwd(dz3, dz3c, s3, in2[0], out2[0], y2b, "a2", BF16,
                                                                  ln=(z2, lg0[1], 1.0))
    tok = grads_ready("a2", {("ffn2_w_in", 0): d_in2_a, ("ffn2_w_out", 0): d_out2_a}, True)
    lg0 = lg0 + tok
    gr["a_w_o"] = _mm_tn(mix_a, dz2b, "d_a_w_o", out_dtype=BF16)
    dmix_a = _mm_nt(dz2b, W["a_w_o"], "d_mix_a")
    dqkv_a, _ = _attn_bwd(qkv_a, dmix_a, o_a, lse_a, slopes, None, PATTERNS_A, "attn_a_bwd")
    gr["a_w_qkv"] = _mm_tn(y1b, dqkv_a, "d_a_w_qkv", split=True, out_dtype=BF16)
    tok = grads_ready("mix", {("a_w_o", None): gr["a_w_o"], ("a_w_qkv", None): gr["a_w_qkv"]}, True)
    lg0 = lg0 + tok
    dz1, dz1c, gg[0][0], gb[0][0] = _mm_nt(dqkv_a, W["a_w_qkv"], "d_y1", add=dz2, add_scale=ALPHA, split=True,
                                           ln=(s1["z"], lg0[0], 0.5))
    def ready_a1(d_in, d_out):
        tok = grads_ready("a1", {("ffn1_w_in", 0): d_in, ("ffn1_w_out", 0): d_out}, True)
        return jnp.reshape(jnp.asarray(tok, F32), (1, 1))

    grad_x, d_in1_a, d_out1_a = _ffn_bwd(dz1, dz1c, s1, in1[0], out1[0], xs, "a1", BF16, ready=ready_a1)
    gr["ffn1_w_in"] = [d_in1_a, d_in1_b]
    gr["ffn1_w_out"] = [d_out1_a, d_out1_b]
    gr["ffn2_w_in"] = [d_in2_a, d_in2_b]
    gr["ffn2_w_out"] = [d_out2_a, d_out2_b]
    return sq, grad_x, gr, gg, gb, dsink_part


def _grad_item(name, layer, g):
    if name.endswith("w_in"):
        return (g, "col", HALF_FF, _slot, name, layer)
    if name.endswith("w_out"):
        return (g, "row", D_MODEL, None, name, layer)
    if name == "a_w_qkv":
        return (g, "col", QKV_SHARD, lambda q: q, name, None)
    return (g, "row", g.shape[1], None, name, None)


class _GradReducer:
    def __init__(self, c_idx, myq, shard_shapes):
        self.c_idx, self.myq, self.shard_shapes = c_idx, myq, shard_shapes
        self.groups = []

    def begin(self, tag, grads, overlap):
        items = [_grad_item(n, l, g) for (n, l), g in grads.items()]
        kinds, widths, colblocks = [it[1] for it in items], [it[2] for it in items], [it[3] for it in items]
        views = [_grad_view(k, it[0]) for k, it in zip(kinds, items)]
        if overlap:
            lands = [jax.ShapeDtypeStruct((N_DIRECT,) + _piece_shape(k, w, _half_shape(k, v.shape)), BF16)
                     for k, w, v in zip(kinds, widths, views)]
            state, token = _split_start("grad_direct_start_" + tag, _direct_copies(kinds, widths, colblocks), 10 * len(items),
                                        views, lands, jnp.zeros((8, 128), F32))
            self.groups.append((tag, items, None, state, token))
            return token[0, 0]
        from_sibling = _pair_exchange(views, kinds, "grad_pair_exchange_" + tag)
        sums = [_pair_sum(k, v, r, self.c_idx, "pair_sum_%s_%d" % (tag, t))
                for t, (k, v, r) in enumerate(zip(kinds, views, from_sibling))]
        self.groups.append((tag, items, sums, None, None))
        return 0.0

    def _sum_group(self, tag, items, sums, received, direct):
        for t, (it, s, r) in enumerate(zip(items, sums, received)):
            _, k, _, cb, name, layer = it
            own = cb(self.myq) if k == "col" else self.myq
            self.half_done[name] = _chip_sum(k, s, r, own, self.c_idx, self.shard_shapes[name], layer,
                                             self.half_done.get(name), "chip_sum_%s_%d" % (tag, t), direct=direct)

    def finish_first(self, after):
        self.half_done, self.late, early = {}, [], []
        started = [after]
        for g, (tag, items, sums, state, token) in enumerate(self.groups):
            kinds, widths, colblocks = [it[1] for it in items], [it[2] for it in items], [it[3] for it in items]
            if state is None:
                copies = _chip_copies(kinds, widths, colblocks)
                state, token = _split_start("grad_chip_start_" + tag, copies, 3 * len(items), sums,
                                            _chip_land_shapes(sums, kinds, widths), sums[-1])
                self.late.append((tag, items, copies, state, False))
                started.append(token)
            elif g == len(self.groups) - 1:
                self.late.append((tag, items, _direct_copies(kinds, widths, colblocks), state, True))
                started.append(token)
            else:
                early.append((tag, items, _direct_copies(kinds, widths, colblocks), state))
        for tag, items, copies, state in early:
            views, received = _split_wait("grad_direct_wait_" + tag, copies, state, started)
            self._sum_group(tag, items, views, received, True)
        late_names = {it[4] for _, items, _, _, _ in self.late for it in items}
        names = [n for n in BIG if n not in late_names]
        return dict(zip(names, _share_halves([self.half_done[n] for n in names], "grad_share_halves_first")))

    def finish_rest(self, after):
        names = []
        for tag, items, copies, state, direct in self.late:
            sums, received = _split_wait("grad_late_wait_" + tag, copies, state, after)
            self._sum_group(tag, items, sums, received, direct)
            names += [it[4] for it in items if it[4] not in names]
        return dict(zip(names, _share_halves([self.half_done[n] for n in names], "grad_share_halves_rest")))


def _update(reducer, grad_x, reduce_small, ws, ms, vs, small_w, small_m, small_v):
    ln_g, ln_b, b_sinks = small_w
    m_ln_g, m_ln_b, m_b_sinks = small_m
    v_ln_g, v_ln_b, v_b_sinks = small_v

    grads, deltas, new_m, new_v = {}, {}, {}, {}

    def update(some):
        done = []
        for name in some:
            shp = ws[name].shape
            flat = lambda a: a.reshape(-1, shp[-1])
            d, nm, nv, g = _adamw(flat(ws[name]), flat(some[name]), flat(ms[name]), flat(vs[name]), "adamw_" + name)
            grads[name], deltas[name], new_m[name], new_v[name] = g.reshape(shp), d.reshape(shp), nm.reshape(shp), nv.reshape(shp)
            done.append(d)
        return done

    rest = reducer.finish_rest(update(reducer.finish_first(grad_x)))
    loss, grad_ln_g, grad_ln_b, grad_sinks = reduce_small(list(rest.values()))
    update(rest)
    delta_s, nm_s, nv_s, _ = _adamw(_pack_small(ln_g, ln_b, b_sinks), _pack_small(grad_ln_g, grad_ln_b, grad_sinks),
                                    _pack_small(m_ln_g, m_ln_b, m_b_sinks), _pack_small(v_ln_g, v_ln_b, v_b_sinks), "adamw_small")
    for d, blob in ((grads, None), (deltas, delta_s), (new_m, nm_s), (new_v, nv_s)):
        if blob is None:
            d["ln_g"], d["ln_b"], d["b_sinks"] = grad_ln_g, grad_ln_b, grad_sinks
        else:
            d["ln_g"], d["ln_b"], d["b_sinks"] = _unpack_small(blob, ln_g.shape, b_sinks.shape)

    order = ("ffn1_w_in", "ffn1_w_out", "ffn2_w_in", "ffn2_w_out", "ln_g", "ln_b", "a_w_qkv", "a_w_o", "kv_w", "b_w_q",
             "b_sinks", "b_w_o")
    outs = [loss, grad_x[None]]
    for d in (grads, deltas, new_m, new_v):
        outs += [d[n] for n in order]
    return tuple(outs)
```
